```python
import jax, jax.numpy as jnp
from jax import lax
import numpy as np

D_MODEL = 1024
BATCH = 8
SEQ = 4096
DEPTH = 1

D_FF = 2816
D_A = D_MODEL
D_B = D_MODEL
GROUP = 128
CONV_A = 31
CONV_B = 3
EPS = 1e-6
SPLITS = (D_A, 2 * D_A, 2 * D_A + D_B, 2 * D_A + 2 * D_B, 2 * D_A + 3 * D_B, 2 * D_A + 3 * D_B + D_MODEL)
D_IN = 2 * D_A + 3 * D_B + 2 * D_MODEL

kernel_name = "macaron_gated_conformer_shortconv_hybrid"


def rmsnorm(x, g):
    xf = x.astype(jnp.float32)
    y = xf * lax.rsqrt(jnp.mean(xf * xf, axis=-1, keepdims=True) + EPS)
    return (y * g.astype(jnp.float32)).astype(x.dtype)


def layernorm(x, g, b):
    xf = x.astype(jnp.float32)
    mu = jnp.mean(xf, axis=-1, keepdims=True)
    var = jnp.mean(jnp.square(xf - mu), axis=-1, keepdims=True)
    y = (xf - mu) * lax.rsqrt(var + EPS)
    return (y * g.astype(jnp.float32) + b.astype(jnp.float32)).astype(x.dtype)


def swiglu(x, w_gate, w_up, w_down):
    return (jax.nn.silu(x @ w_gate) * (x @ w_up)) @ w_down


def causal_depthwise_conv(x, w):
    k, c = w.shape
    return lax.conv_general_dilated(
        x, w[:, None, :].astype(x.dtype), window_strides=(1,), padding=((k - 1, 0),),
        dimension_numbers=("NWC", "WIO", "NWC"), feature_group_count=c)


def _fwd_setup_inputs(seed: int = 0) -> dict:
    key = jax.random.key(seed)
    ks = jax.random.split(key, 24)
    f32 = jnp.float32

    def nrm(k, shape, fan_in):
        return jax.random.normal(k, shape, f32) * (fan_in ** -0.5)

    def gain(k, shape):
        return 1.0 + 0.01 * jax.random.normal(k, shape, f32)

    L = DEPTH
    return {
        "x": jax.random.normal(ks[0], (BATCH, SEQ, D_MODEL), f32),
        "ffn1_norm": gain(ks[1], (L, D_MODEL)),
        "ffn1_w_gate": nrm(ks[2], (L, D_MODEL, D_FF), D_MODEL),
        "ffn1_w_up": nrm(ks[3], (L, D_MODEL, D_FF), D_MODEL),
        "ffn1_w_down": nrm(ks[4], (L, D_FF, D_MODEL), D_FF),
        "mix_norm": gain(ks[5], (L, D_MODEL)),
        "w_in": nrm(ks[6], (L, D_MODEL, D_IN), D_MODEL),
        "a_dw_w": nrm(ks[7], (L, CONV_A, D_A), CONV_A),
        "a_dw_b": 0.01 * jax.random.normal(ks[8], (L, D_A), f32),
        "a_ln_g": gain(ks[9], (L, D_A)),
        "a_ln_b": 0.01 * jax.random.normal(ks[10], (L, D_A), f32),
        "a_w_out": nrm(ks[11], (L, D_A, D_MODEL), D_A),
        "b_conv_w": nrm(ks[12], (L, CONV_B, D_B), CONV_B),
        "b_w_out": nrm(ks[13], (L, D_B, D_MODEL), D_B),
        "w_o": nrm(ks[14], (L, D_MODEL, D_MODEL), D_MODEL),
        "ffn2_norm": gain(ks[15], (L, D_MODEL)),
        "ffn2_w_gate": nrm(ks[16], (L, D_MODEL, D_FF), D_MODEL),
        "ffn2_w_up": nrm(ks[17], (L, D_MODEL, D_FF), D_MODEL),
        "ffn2_w_down": nrm(ks[18], (L, D_FF, D_MODEL), D_FF),
        "final_norm": gain(ks[19], (D_MODEL,)),
    }


def _fwd_reference(x, ffn1_norm, ffn1_w_gate, ffn1_w_up, ffn1_w_down, mix_norm, w_in,
              a_dw_w, a_dw_b, a_ln_g, a_ln_b, a_w_out, b_conv_w, b_w_out, w_o,
              ffn2_norm, ffn2_w_gate, ffn2_w_up, ffn2_w_down, final_norm):
    h = x
    for l in range(DEPTH):
        h = h + 0.5 * swiglu(rmsnorm(h, ffn1_norm[l]), ffn1_w_gate[l], ffn1_w_up[l], ffn1_w_down[l])

        u = rmsnorm(h, mix_norm[l])
        z = u @ w_in[l]
        a_val, a_gate, b_B, b_C, b_x, g_a, g_b = jnp.split(z, SPLITS, axis=-1)

        a = a_val * jax.nn.sigmoid(a_gate)
        a = causal_depthwise_conv(a, a_dw_w[l]) + a_dw_b[l]
        a = jax.nn.silu(layernorm(a, a_ln_g[l], a_ln_b[l]))
        y_a = a @ a_w_out[l]

        v = causal_depthwise_conv(b_C * b_x, b_conv_w[l])
        y_b = (b_B * v) @ b_w_out[l]

        m = jax.nn.sigmoid(g_a) * y_a + jax.nn.sigmoid(g_b) * y_b
        h = h + m @ w_o[l]

        h = h + 0.5 * swiglu(rmsnorm(h, ffn2_norm[l]), ffn2_w_gate[l], ffn2_w_up[l], ffn2_w_down[l])
    return rmsnorm(h, final_norm)


import jax as _jax
import jax.numpy as _jnp

TWIN_FORMAT = 'train_step'
FWD_PARAMS = ['x', 'ffn1_norm', 'ffn1_w_gate', 'ffn1_w_up', 'ffn1_w_down', 'mix_norm', 'w_in', 'a_dw_w', 'a_dw_b', 'a_ln_g', 'a_ln_b', 'a_w_out', 'b_conv_w', 'b_w_out', 'w_o', 'ffn2_norm', 'ffn2_w_gate', 'ffn2_w_up', 'ffn2_w_down', 'final_norm']
TWIN_WEIGHTS = ['ffn1_norm', 'ffn1_w_gate', 'ffn1_w_up', 'ffn1_w_down', 'mix_norm', 'w_in', 'a_dw_w', 'a_dw_b', 'a_ln_g', 'a_ln_b', 'a_w_out', 'b_conv_w', 'b_w_out', 'w_o', 'ffn2_norm', 'ffn2_w_gate', 'ffn2_w_up', 'ffn2_w_down', 'final_norm']
TWIN_DIFF_INPUT = 'x'
TWIN_INPUTS = ['x', 'ffn1_norm', 'ffn1_w_gate', 'ffn1_w_up', 'ffn1_w_down', 'mix_norm', 'w_in', 'a_dw_w', 'a_dw_b', 'a_ln_g', 'a_ln_b', 'a_w_out', 'b_conv_w', 'b_w_out', 'w_o', 'ffn2_norm', 'ffn2_w_gate', 'ffn2_w_up', 'ffn2_w_down', 'final_norm', 'loss_target', 'm_ffn1_norm', 'm_ffn1_w_gate', 'm_ffn1_w_up', 'm_ffn1_w_down', 'm_mix_norm', 'm_w_in', 'm_a_dw_w', 'm_a_dw_b', 'm_a_ln_g', 'm_a_ln_b', 'm_a_w_out', 'm_b_conv_w', 'm_b_w_out', 'm_w_o', 'm_ffn2_norm', 'm_ffn2_w_gate', 'm_ffn2_w_up', 'm_ffn2_w_down', 'm_final_norm', 'v_ffn1_norm', 'v_ffn1_w_gate', 'v_ffn1_w_up', 'v_ffn1_w_down', 'v_mix_norm', 'v_w_in', 'v_a_dw_w', 'v_a_dw_b', 'v_a_ln_g', 'v_a_ln_b', 'v_a_w_out', 'v_b_conv_w', 'v_b_w_out', 'v_w_o', 'v_ffn2_norm', 'v_ffn2_w_gate', 'v_ffn2_w_up', 'v_ffn2_w_down', 'v_final_norm']
TWIN_OUTPUTS = ['loss', 'grad_x', 'grad_ffn1_norm', 'grad_ffn1_w_gate', 'grad_ffn1_w_up', 'grad_ffn1_w_down', 'grad_mix_norm', 'grad_w_in', 'grad_a_dw_w', 'grad_a_dw_b', 'grad_a_ln_g', 'grad_a_ln_b', 'grad_a_w_out', 'grad_b_conv_w', 'grad_b_w_out', 'grad_w_o', 'grad_ffn2_norm', 'grad_ffn2_w_gate', 'grad_ffn2_w_up', 'grad_ffn2_w_down', 'grad_final_norm', 'delta_ffn1_norm', 'delta_ffn1_w_gate', 'delta_ffn1_w_up', 'delta_ffn1_w_down', 'delta_mix_norm', 'delta_w_in', 'delta_a_dw_w', 'delta_a_dw_b', 'delta_a_ln_g', 'delta_a_ln_b', 'delta_a_w_out', 'delta_b_conv_w', 'delta_b_w_out', 'delta_w_o', 'delta_ffn2_norm', 'delta_ffn2_w_gate', 'delta_ffn2_w_up', 'delta_ffn2_w_down', 'delta_final_norm', 'new_m_ffn1_norm', 'new_m_ffn1_w_gate', 'new_m_ffn1_w_up', 'new_m_ffn1_w_down', 'new_m_mix_norm', 'new_m_w_in', 'new_m_a_dw_w', 'new_m_a_dw_b', 'new_m_a_ln_g', 'new_m_a_ln_b', 'new_m_a_w_out', 'new_m_b_conv_w', 'new_m_b_w_out', 'new_m_w_o', 'new_m_ffn2_norm', 'new_m_ffn2_w_gate', 'new_m_ffn2_w_up', 'new_m_ffn2_w_down', 'new_m_final_norm', 'new_v_ffn1_norm', 'new_v_ffn1_w_gate', 'new_v_ffn1_w_up', 'new_v_ffn1_w_down', 'new_v_mix_norm', 'new_v_w_in', 'new_v_a_dw_w', 'new_v_a_dw_b', 'new_v_a_ln_g', 'new_v_a_ln_b', 'new_v_a_w_out', 'new_v_b_conv_w', 'new_v_b_w_out', 'new_v_w_o', 'new_v_ffn2_norm', 'new_v_ffn2_w_gate', 'new_v_ffn2_w_up', 'new_v_ffn2_w_down', 'new_v_final_norm']
TWIN_LEAF_KINDS = {'loss': 'loss', 'grad_x': 'grad_x', 'grad_ffn1_norm': 'grad_w', 'grad_ffn1_w_gate': 'grad_w', 'grad_ffn1_w_up': 'grad_w', 'grad_ffn1_w_down': 'grad_w', 'grad_mix_norm': 'grad_w', 'grad_w_in': 'grad_w', 'grad_a_dw_w': 'grad_w', 'grad_a_dw_b': 'grad_w', 'grad_a_ln_g': 'grad_w', 'grad_a_ln_b': 'grad_w', 'grad_a_w_out': 'grad_w', 'grad_b_conv_w': 'grad_w', 'grad_b_w_out': 'grad_w', 'grad_w_o': 'grad_w', 'grad_ffn2_norm': 'grad_w', 'grad_ffn2_w_gate': 'grad_w', 'grad_ffn2_w_up': 'grad_w', 'grad_ffn2_w_down': 'grad_w', 'grad_final_norm': 'grad_w', 'delta_ffn1_norm': 'delta_w', 'delta_ffn1_w_gate': 'delta_w', 'delta_ffn1_w_up': 'delta_w', 'delta_ffn1_w_down': 'delta_w', 'delta_mix_norm': 'delta_w', 'delta_w_in': 'delta_w', 'delta_a_dw_w': 'delta_w', 'delta_a_dw_b': 'delta_w', 'delta_a_ln_g': 'delta_w', 'delta_a_ln_b': 'delta_w', 'delta_a_w_out': 'delta_w', 'delta_b_conv_w': 'delta_w', 'delta_b_w_out': 'delta_w', 'delta_w_o': 'delta_w', 'delta_ffn2_norm': 'delta_w', 'delta_ffn2_w_gate': 'delta_w', 'delta_ffn2_w_up': 'delta_w', 'delta_ffn2_w_down': 'delta_w', 'delta_final_norm': 'delta_w', 'new_m_ffn1_norm': 'new_m', 'new_m_ffn1_w_gate': 'new_m', 'new_m_ffn1_w_up': 'new_m', 'new_m_ffn1_w_down': 'new_m', 'new_m_mix_norm': 'new_m', 'new_m_w_in': 'new_m', 'new_m_a_dw_w': 'new_m', 'new_m_a_dw_b': 'new_m', 'new_m_a_ln_g': 'new_m', 'new_m_a_ln_b': 'new_m', 'new_m_a_w_out': 'new_m', 'new_m_b_conv_w': 'new_m', 'new_m_b_w_out': 'new_m', 'new_m_w_o': 'new_m', 'new_m_ffn2_norm': 'new_m', 'new_m_ffn2_w_gate': 'new_m', 'new_m_ffn2_w_up': 'new_m', 'new_m_ffn2_w_down': 'new_m', 'new_m_final_norm': 'new_m', 'new_v_ffn1_norm': 'new_v', 'new_v_ffn1_w_gate': 'new_v', 'new_v_ffn1_w_up': 'new_v', 'new_v_ffn1_w_down': 'new_v', 'new_v_mix_norm': 'new_v', 'new_v_w_in': 'new_v', 'new_v_a_dw_w': 'new_v', 'new_v_a_dw_b': 'new_v', 'new_v_a_ln_g': 'new_v', 'new_v_a_ln_b': 'new_v', 'new_v_a_w_out': 'new_v', 'new_v_b_conv_w': 'new_v', 'new_v_b_w_out': 'new_v', 'new_v_w_o': 'new_v', 'new_v_ffn2_norm': 'new_v', 'new_v_ffn2_w_gate': 'new_v', 'new_v_ffn2_w_up': 'new_v', 'new_v_ffn2_w_down': 'new_v', 'new_v_final_norm': 'new_v'}


def _forward(args):
    return _fwd_reference(*[args[k] for k in FWD_PARAMS])


def _output_shape():
    out = _jax.eval_shape(lambda: _forward(_fwd_setup_inputs(0)))
    return out.shape, out.dtype

N_MICROBATCH = 1
ADAM_LR = 0.001
ADAM_B1 = 0.9
ADAM_B2 = 0.999
ADAM_EPS = 1e-08
ADAM_WD = 0.01
ADAM_STEP = 10
PER_EXAMPLE_BATCH_AXIS = {'x': 0, 'loss_target': 0}
SHARED_INPUTS = []
_WEIGHT_DTYPES = {'ffn1_norm': _jnp.float32, 'ffn1_w_gate': _jnp.float32, 'ffn1_w_up': _jnp.float32, 'ffn1_w_down': _jnp.float32, 'mix_norm': _jnp.float32, 'w_in': _jnp.float32, 'a_dw_w': _jnp.float32, 'a_dw_b': _jnp.float32, 'a_ln_g': _jnp.float32, 'a_ln_b': _jnp.float32, 'a_w_out': _jnp.float32, 'b_conv_w': _jnp.float32, 'b_w_out': _jnp.float32, 'w_o': _jnp.float32, 'ffn2_norm': _jnp.float32, 'ffn2_w_gate': _jnp.float32, 'ffn2_w_up': _jnp.float32, 'ffn2_w_down': _jnp.float32, 'final_norm': _jnp.float32}
MOMENT_SCALE = {'ffn1_norm': 9.049586e-02, 'ffn1_w_gate': 3.901146e-02, 'ffn1_w_up': 3.765221e-02, 'ffn1_w_down': 6.242842e-02, 'mix_norm': 1.492734e-01, 'w_in': 5.761486e-02, 'a_dw_w': 5.068497e-02, 'a_dw_b': 1.076283e-01, 'a_ln_g': 6.442208e-02, 'a_ln_b': 5.138925e-02, 'a_w_out': 4.895678e-02, 'b_conv_w': 8.108803e-02, 'b_w_out': 8.016387e-02, 'w_o': 9.369169e-02, 'ffn2_norm': 6.240813e-02, 'ffn2_w_gate': 2.625792e-02, 'ffn2_w_up': 2.547723e-02, 'ffn2_w_down': 4.214851e-02, 'final_norm': 3.189867e+01}


def _to_microbatches(a, axis):
    t = _jnp.moveaxis(a, axis, 0)
    t = t.reshape((N_MICROBATCH, t.shape[0] // N_MICROBATCH) + t.shape[1:])
    return _jnp.moveaxis(t, 1, axis + 1)


def setup_inputs(seed: int = 0) -> dict:
    inp = _fwd_setup_inputs(seed)
    key = _jax.random.fold_in(_jax.random.key(seed), 7919)
    shape, _ = _output_shape()
    out = dict(inp)
    out["loss_target"] = _jax.random.normal(_jax.random.fold_in(key, 0), shape, _jnp.float32)
    for i, name in enumerate(TWIN_WEIGHTS):
        w = inp[name].astype(_jnp.float32)
        if MOMENT_SCALE is None:
            s = _jnp.sqrt(_jnp.mean(_jnp.square(w)) + 1e-30)
        else:
            s = MOMENT_SCALE[name]
        km, kv = _jax.random.split(_jax.random.fold_in(key, i + 1))
        out[name] = w
        out["m_" + name] = s * _jax.random.normal(km, w.shape, _jnp.float32)
        out["v_" + name] = (s * s) * _jax.random.uniform(kv, w.shape, _jnp.float32, 0.5, 1.5)
    if N_MICROBATCH > 1:
        for name, axis in PER_EXAMPLE_BATCH_AXIS.items():
            out[name] = _to_microbatches(out[name], axis)
    return {'x': out['x'], 'ffn1_norm': out['ffn1_norm'], 'ffn1_w_gate': out['ffn1_w_gate'], 'ffn1_w_up': out['ffn1_w_up'], 'ffn1_w_down': out['ffn1_w_down'], 'mix_norm': out['mix_norm'], 'w_in': out['w_in'], 'a_dw_w': out['a_dw_w'], 'a_dw_b': out['a_dw_b'], 'a_ln_g': out['a_ln_g'], 'a_ln_b': out['a_ln_b'], 'a_w_out': out['a_w_out'], 'b_conv_w': out['b_conv_w'], 'b_w_out': out['b_w_out'], 'w_o': out['w_o'], 'ffn2_norm': out['ffn2_norm'], 'ffn2_w_gate': out['ffn2_w_gate'], 'ffn2_w_up': out['ffn2_w_up'], 'ffn2_w_down': out['ffn2_w_down'], 'final_norm': out['final_norm'], 'loss_target': out['loss_target'], 'm_ffn1_norm': out['m_ffn1_norm'], 'm_ffn1_w_gate': out['m_ffn1_w_gate'], 'm_ffn1_w_up': out['m_ffn1_w_up'], 'm_ffn1_w_down': out['m_ffn1_w_down'], 'm_mix_norm': out['m_mix_norm'], 'm_w_in': out['m_w_in'], 'm_a_dw_w': out['m_a_dw_w'], 'm_a_dw_b': out['m_a_dw_b'], 'm_a_ln_g': out['m_a_ln_g'], 'm_a_ln_b': out['m_a_ln_b'], 'm_a_w_out': out['m_a_w_out'], 'm_b_conv_w': out['m_b_conv_w'], 'm_b_w_out': out['m_b_w_out'], 'm_w_o': out['m_w_o'], 'm_ffn2_norm': out['m_ffn2_norm'], 'm_ffn2_w_gate': out['m_ffn2_w_gate'], 'm_ffn2_w_up': out['m_ffn2_w_up'], 'm_ffn2_w_down': out['m_ffn2_w_down'], 'm_final_norm': out['m_final_norm'], 'v_ffn1_norm': out['v_ffn1_norm'], 'v_ffn1_w_gate': out['v_ffn1_w_gate'], 'v_ffn1_w_up': out['v_ffn1_w_up'], 'v_ffn1_w_down': out['v_ffn1_w_down'], 'v_mix_norm': out['v_mix_norm'], 'v_w_in': out['v_w_in'], 'v_a_dw_w': out['v_a_dw_w'], 'v_a_dw_b': out['v_a_dw_b'], 'v_a_ln_g': out['v_a_ln_g'], 'v_a_ln_b': out['v_a_ln_b'], 'v_a_w_out': out['v_a_w_out'], 'v_b_conv_w': out['v_b_conv_w'], 'v_b_w_out': out['v_b_w_out'], 'v_w_o': out['v_w_o'], 'v_ffn2_norm': out['v_ffn2_norm'], 'v_ffn2_w_gate': out['v_ffn2_w_gate'], 'v_ffn2_w_up': out['v_ffn2_w_up'], 'v_ffn2_w_down': out['v_ffn2_w_down'], 'v_final_norm': out['v_final_norm']}


def _loss(weights, diff, rest, loss_target):
    with _jax.named_scope("forward"):
        args = {**rest, TWIN_DIFF_INPUT: diff, **{k: w.astype(_WEIGHT_DTYPES[k]) for k, w in weights.items()}}
        y = _forward(args)
    with _jax.named_scope("loss_head"):
        err = _jnp.square(y.astype(_jnp.float32) - loss_target)
        return 0.5 * _jnp.sum(_jnp.mean(err, axis=-1)) if err.ndim else 0.5 * err


def _adamw(w, g, m, v):
    m = ADAM_B1 * m + (1.0 - ADAM_B1) * g
    v = ADAM_B2 * v + (1.0 - ADAM_B2) * _jnp.square(g)
    m_hat = m / (1.0 - ADAM_B1 ** ADAM_STEP)
    v_hat = v / (1.0 - ADAM_B2 ** ADAM_STEP)
    delta = -ADAM_LR * (m_hat / (_jnp.sqrt(v_hat) + ADAM_EPS) + ADAM_WD * w)
    return delta, m, v


def reference(x, ffn1_norm, ffn1_w_gate, ffn1_w_up, ffn1_w_down, mix_norm, w_in, a_dw_w, a_dw_b, a_ln_g, a_ln_b, a_w_out, b_conv_w, b_w_out, w_o, ffn2_norm, ffn2_w_gate, ffn2_w_up, ffn2_w_down, final_norm, loss_target, m_ffn1_norm, m_ffn1_w_gate, m_ffn1_w_up, m_ffn1_w_down, m_mix_norm, m_w_in, m_a_dw_w, m_a_dw_b, m_a_ln_g, m_a_ln_b, m_a_w_out, m_b_conv_w, m_b_w_out, m_w_o, m_ffn2_norm, m_ffn2_w_gate, m_ffn2_w_up, m_ffn2_w_down, m_final_norm, v_ffn1_norm, v_ffn1_w_gate, v_ffn1_w_up, v_ffn1_w_down, v_mix_norm, v_w_in, v_a_dw_w, v_a_dw_b, v_a_ln_g, v_a_ln_b, v_a_w_out, v_b_conv_w, v_b_w_out, v_w_o, v_ffn2_norm, v_ffn2_w_gate, v_ffn2_w_up, v_ffn2_w_down, v_final_norm):
    given = dict(x=x, ffn1_norm=ffn1_norm, ffn1_w_gate=ffn1_w_gate, ffn1_w_up=ffn1_w_up, ffn1_w_down=ffn1_w_down, mix_norm=mix_norm, w_in=w_in, a_dw_w=a_dw_w, a_dw_b=a_dw_b, a_ln_g=a_ln_g, a_ln_b=a_ln_b, a_w_out=a_w_out, b_conv_w=b_conv_w, b_w_out=b_w_out, w_o=w_o, ffn2_norm=ffn2_norm, ffn2_w_gate=ffn2_w_gate, ffn2_w_up=ffn2_w_up, ffn2_w_down=ffn2_w_down, final_norm=final_norm, loss_target=loss_target, m_ffn1_norm=m_ffn1_norm, m_ffn1_w_gate=m_ffn1_w_gate, m_ffn1_w_up=m_ffn1_w_up, m_ffn1_w_down=m_ffn1_w_down, m_mix_norm=m_mix_norm, m_w_in=m_w_in, m_a_dw_w=m_a_dw_w, m_a_dw_b=m_a_dw_b, m_a_ln_g=m_a_ln_g, m_a_ln_b=m_a_ln_b, m_a_w_out=m_a_w_out, m_b_conv_w=m_b_conv_w, m_b_w_out=m_b_w_out, m_w_o=m_w_o, m_ffn2_norm=m_ffn2_norm, m_ffn2_w_gate=m_ffn2_w_gate, m_ffn2_w_up=m_ffn2_w_up, m_ffn2_w_down=m_ffn2_w_down, m_final_norm=m_final_norm, v_ffn1_norm=v_ffn1_norm, v_ffn1_w_gate=v_ffn1_w_gate, v_ffn1_w_up=v_ffn1_w_up, v_ffn1_w_down=v_ffn1_w_down, v_mix_norm=v_mix_norm, v_w_in=v_w_in, v_a_dw_w=v_a_dw_w, v_a_dw_b=v_a_dw_b, v_a_ln_g=v_a_ln_g, v_a_ln_b=v_a_ln_b, v_a_w_out=v_a_w_out, v_b_conv_w=v_b_conv_w, v_b_w_out=v_b_w_out, v_w_o=v_w_o, v_ffn2_norm=v_ffn2_norm, v_ffn2_w_gate=v_ffn2_w_gate, v_ffn2_w_up=v_ffn2_w_up, v_ffn2_w_down=v_ffn2_w_down, v_final_norm=v_final_norm)
    weights = {n: given[n] for n in TWIN_WEIGHTS}
    shared = {n: given[n] for n in SHARED_INPUTS}
    per_example = {n: given[n] for n in ['x']}
    grad_fn = _jax.value_and_grad(_loss, argnums=(0, 1))

    def one_microbatch(ex, loss_target):
        ex = dict(ex)
        diff = ex.pop(TWIN_DIFF_INPUT)
        return grad_fn(weights, diff, {**shared, **ex}, loss_target)

    if N_MICROBATCH == 1:
        loss, (grad_w, grad_x) = one_microbatch(per_example, given["loss_target"])
    else:
        def body(carry, xs):
            loss_sum, grad_sum = carry
            l_k, (gw_k, gx_k) = one_microbatch(xs[0], xs[1])
            with _jax.named_scope("update"):
                return (loss_sum + l_k, _jax.tree.map(_jnp.add, grad_sum, gw_k)), gx_k

        init = (_jnp.zeros((), _jnp.float32), _jax.tree.map(_jnp.zeros_like, weights))
        (loss, grad_w), grad_x = _jax.lax.scan(body, init, (per_example, given["loss_target"]))
    with _jax.named_scope("update"):
        delta_w, new_m, new_v = {}, {}, {}
        for n in TWIN_WEIGHTS:
            delta_w[n], new_m[n], new_v[n] = _adamw(weights[n], grad_w[n], given["m_" + n], given["v_" + n])
    return (loss, grad_x, *[grad_w[n] for n in TWIN_WEIGHTS], *[delta_w[n] for n in TWIN_WEIGHTS],
            *[new_m[n] for n in TWIN_WEIGHTS], *[new_v[n] for n in TWIN_WEIGHTS])
```

```python
import functools

import jax
import jax.numpy as jnp
from jax import lax
from jax.experimental import pallas as pl
from jax.experimental.pallas import tpu as pltpu

T = 4096
D = 1024
F = 2816
NG = 7
NDEV = 8
KA, KB = 31, 3
EPS = 1e-6
ADAM_LR, ADAM_B1, ADAM_B2, ADAM_EPS, ADAM_WD, ADAM_STEP = 0.001, 0.9, 0.999, 1e-08, 0.01, 10

TM = 512
TMB = 256
TMI = 1024
FC = 256
TB = 1024
CH = 128
HALO = 32
LANE = 128
TK = 1024
VMEM_LIMIT = 56 * 1024 * 1024

BF = jnp.bfloat16
F32 = jnp.float32
MESH = pl.DeviceIdType.MESH
ANY = pl.BlockSpec(memory_space=pl.ANY)

ORDER = ("wg1", "wu1", "wd1", "wg2", "wu2", "wd2", "win", "wa", "wb", "wo")


class _Layout:
    def __init__(self):
        self.FS, self.DIS, self.DS, self.DI = F // NDEV, NG * D // NDEV, D // NDEV, NG * D
        fs, dis, ds = self.FS, self.DIS, self.DS
        self.rows = dict(wg1=fs, wu1=fs, wd1=fs, wg2=fs, wu2=fs, wd2=fs, win=dis, wa=ds, wb=ds, wo=ds)
        self.fl, off = {}, 0
        for n in ORDER:
            self.fl[n] = off
            off += self.rows[n]
        self.RT = off
        win_off = -(-6 * F // D) * D
        self.wc = dict(wg1=0, wu1=F, wd1=2 * F, wg2=3 * F, wu2=4 * F, wd2=5 * F, win=win_off,
                       wa=win_off + self.DI, wb=win_off + self.DI + D, wo=win_off + self.DI + 2 * D)
        self.WCR = win_off + self.DI + 3 * D


def _nt(a, b):
    return lax.dot_general(a, b, (((1,), (1,)), ((), ())), preferred_element_type=F32)


def _nn(a, b):
    return lax.dot_general(a, b, (((1,), (0,)), ((), ())), preferred_element_type=F32)


def _tn(a, b):
    return lax.dot_general(a, b, (((0,), (0,)), ((), ())), preferred_element_type=F32)


def _sig(x):
    return 1.0 / (1.0 + jnp.exp(-x))


def _params(sem):
    return pltpu.CompilerParams(dimension_semantics=sem, vmem_limit_bytes=VMEM_LIMIT)


def _position():
    return lax.axis_index("x"), lax.axis_index("y"), lax.axis_index("c")


def _peer(pos, j):
    x, y, c = pos
    return (1 - x if j & 4 else x, 1 - y if j & 2 else y, 1 - c if j & 1 else c)


def _lin(pos):
    return 4 * pos[0] + 2 * pos[1] + pos[2]


def _ag_weights(flat):
    lay = _Layout()

    def body(flat_ref, out_ref, send_sems, recv_sems, local_sem):
        me = _position()
        sib = _peer(me, 1)
        chips = [_peer(me, 4), _peer(me, 2), _peer(me, 6)]

        def region(name, dev):
            r = lay.rows[name]
            return out_ref.at[pl.ds(lay.wc[name] + _lin(dev) * r, r), :]

        def copies(k, dev, to, from_flat):
            out = []
            for name in ORDER:
                src = flat_ref.at[pl.ds(lay.fl[name], lay.rows[name]), :] if from_flat else region(name, dev)
                out.append(pltpu.make_async_remote_copy(
                    src_ref=src, dst_ref=region(name, dev), send_sem=send_sems.at[k], recv_sem=recv_sems.at[k],
                    device_id=to, device_id_type=MESH))
            return out

        def whole(k):
            return pltpu.make_async_remote_copy(
                src_ref=flat_ref, dst_ref=out_ref.at[pl.ds(0, lay.RT), :], send_sem=send_sems.at[k],
                recv_sem=recv_sems.at[k], device_id=me, device_id_type=MESH)

        mine = [pltpu.make_async_copy(flat_ref.at[pl.ds(lay.fl[n], lay.rows[n]), :], region(n, me), local_sem)
                for n in ORDER]
        for cp in mine:
            cp.start()
        for cp in copies(0, me, sib, True):
            cp.start()
        for j, chip in enumerate(chips):
            for cp in copies(1 + j, me, chip, True):
                cp.start()
        for j, chip in enumerate(chips):
            whole(1 + j).wait_recv()
            for cp in copies(4 + j, chip, sib, False):
                cp.start()
        whole(0).wait_recv()
        for j in range(3):
            whole(4 + j).wait_recv()
        for k in range(7):
            whole(k).wait_send()
        pltpu.make_async_copy(flat_ref, out_ref.at[pl.ds(0, lay.RT), :], local_sem).wait()

    return pl.pallas_call(
        body, name="ag_weights",
        out_shape=jax.ShapeDtypeStruct((lay.WCR, D), BF),
        in_specs=[ANY], out_specs=ANY,
        scratch_shapes=[pltpu.SemaphoreType.DMA((7,)), pltpu.SemaphoreType.DMA((7,)), pltpu.SemaphoreType.DMA],
    )(flat)


def _rs_grads(gu1, gd1, gu2, gd2, gin, ga, gb, go):
    lay = _Layout()
    src_of = dict(wg1=(0, 0), wu1=(0, F), wd1=(1, 0), wg2=(2, 0), wu2=(2, F), wd2=(3, 0),
                  win=(4, 0), wa=(5, 0), wb=(6, 0), wo=(7, 0))

    def body(*refs):
        arrs, recv = refs[:8], refs[8]
        send_sems, recv_sems, local_sem = refs[9:]
        me = _position()

        def src(name, dev):
            a, base = src_of[name]
            r = lay.rows[name]
            return arrs[a].at[pl.ds(base + _lin(dev) * r, r), :]

        def dst(name):
            return recv.at[_lin(me), pl.ds(lay.fl[name], lay.rows[name]), :]

        for n in ORDER:
            pltpu.make_async_copy(src(n, me), dst(n), local_sem).start()
        for j in range(1, NDEV):
            to = _peer(me, j)
            for n in ORDER:
                pltpu.make_async_remote_copy(
                    src_ref=src(n, to), dst_ref=dst(n), send_sem=send_sems.at[j - 1], recv_sem=recv_sems.at[j - 1],
                    device_id=to, device_id_type=MESH).start()
        for j in range(1, NDEV):
            slot = pltpu.make_async_remote_copy(
                src_ref=recv.at[0], dst_ref=recv.at[0], send_sem=send_sems.at[j - 1], recv_sem=recv_sems.at[j - 1],
                device_id=me, device_id_type=MESH)
            slot.wait_recv()
            slot.wait_send()
        pltpu.make_async_copy(recv.at[0], recv.at[0], local_sem).wait()

    return pl.pallas_call(
        body, name="rs_grads",
        out_shape=jax.ShapeDtypeStruct((NDEV, lay.RT, D), BF),
        in_specs=[ANY] * 8, out_specs=ANY,
        scratch_shapes=[pltpu.SemaphoreType.DMA((7,)), pltpu.SemaphoreType.DMA((7,)), pltpu.SemaphoreType.DMA],
    )(gu1, gd1, gu2, gd2, gin, ga, gb, go)


def _ag_small(x, name):
    def body(x_ref, out_ref, send_sems, recv_sems, local_sem):
        me = _position()
        mine = pltpu.make_async_copy(x_ref, out_ref.at[_lin(me)], local_sem)
        mine.start()
        cps = [pltpu.make_async_remote_copy(
            src_ref=x_ref, dst_ref=out_ref.at[_lin(me)], send_sem=send_sems.at[j - 1], recv_sem=recv_sems.at[j - 1],
            device_id=_peer(me, j), device_id_type=MESH) for j in range(1, NDEV)]
        for cp in cps:
            cp.start()
        for cp in cps:
            cp.wait_recv()
        for cp in cps:
            cp.wait_send()
        mine.wait()

    return pl.pallas_call(
        body, name=name,
        out_shape=jax.ShapeDtypeStruct((NDEV,) + x.shape, x.dtype),
        in_specs=[ANY], out_specs=ANY,
        scratch_shapes=[pltpu.SemaphoreType.DMA((7,)), pltpu.SemaphoreType.DMA((7,)), pltpu.SemaphoreType.DMA],
    )(x)


def _load_ffn_weights(wcat_ref, offs, scratch, sem):
    @pl.when(pl.program_id(0) == 0)
    def _():
        cps = [pltpu.make_async_copy(wcat_ref.at[pl.ds(off, F), :], dst, sem.at[i])
               for i, (off, dst) in enumerate(zip(offs, scratch))]
        for cp in cps:
            cp.start()
        for cp in cps:
            cp.wait()


def _ffn_fwd(x, g, wcat, offs, name):
    nf = F // FC

    def body(x_ref, g_ref, wcat_ref, h_ref, n_ref, gg_ref, uu_ref, wg_s, wu_s, wd_s, sem):
        _load_ffn_weights(wcat_ref, offs, (wg_s, wu_s, wd_s), sem)
        xf = x_ref[...]
        r = lax.rsqrt(jnp.mean(xf * xf, axis=-1, keepdims=True) + EPS)
        nb = (xf * r * g_ref[...]).astype(BF)
        n_ref[...] = nb
        acc = jnp.zeros((TM, D), F32)
        for c in range(nf):
            sl = slice(c * FC, (c + 1) * FC)
            gc = _nt(nb, wg_s[sl, :])
            uc = _nt(nb, wu_s[sl, :])
            gg_ref[:, sl] = gc.astype(BF)
            uu_ref[:, sl] = uc.astype(BF)
            a = (0.5 * gc * _sig(gc)) * uc
            acc = acc + _nn(a.astype(BF), wd_s[sl, :])
        h_ref[...] = xf + acc

    row = lambda i: (i, 0)
    return pl.pallas_call(
        body, name=name, grid=(T // TM,),
        out_shape=(jax.ShapeDtypeStruct((T, D), F32), jax.ShapeDtypeStruct((T, D), BF),
                   jax.ShapeDtypeStruct((T, F), BF), jax.ShapeDtypeStruct((T, F), BF)),
        in_specs=[pl.BlockSpec((TM, D), row), pl.BlockSpec((1, D), lambda i: (0, 0)), ANY],
        out_specs=(pl.BlockSpec((TM, D), row), pl.BlockSpec((TM, D), row),
                   pl.BlockSpec((TM, F), row), pl.BlockSpec((TM, F), row)),
        scratch_shapes=[pltpu.VMEM((F, D), BF)] * 3 + [pltpu.SemaphoreType.DMA((3,))],
        compiler_params=_params(("arbitrary",)),
    )(x, g, wcat)


def _mix_in(h1, gm, wcat):
    lay = _Layout()
    wblk = lay.wc["win"] // D

    def body(h_ref, g_ref, w_ref, u_ref, z_ref, u_s):
        @pl.when(pl.program_id(1) == 0)
        def _():
            xf = h_ref[...]
            r = lax.rsqrt(jnp.mean(xf * xf, axis=-1, keepdims=True) + EPS)
            ub = (xf * r * g_ref[...]).astype(BF)
            u_s[...] = ub
            u_ref[...] = ub
        z_ref[...] = _nt(u_s[...], w_ref[...])

    return pl.pallas_call(
        body, name="mix_in", grid=(T // TMI, NG),
        out_shape=(jax.ShapeDtypeStruct((T, D), BF), jax.ShapeDtypeStruct((NG, T, D), F32)),
        in_specs=[pl.BlockSpec((TMI, D), lambda i, j: (i, 0)), pl.BlockSpec((1, D), lambda i, j: (0, 0)),
                  pl.BlockSpec((D, D), lambda i, j: (wblk + j, 0))],
        out_specs=(pl.BlockSpec((TMI, D), lambda i, j: (i, 0)), pl.BlockSpec((None, TMI, D), lambda i, j: (j, i, 0))),
        scratch_shapes=[pltpu.VMEM((TMI, D), BF)],
        compiler_params=_params(("arbitrary", "arbitrary")),
    )(h1, gm, wcat)


def _shift_up(w, b):
    return w if b == 0 else pltpu.roll(w, w.shape[0] - b, 0)


def _fold8(p):
    red = p[0:8, :]
    for i in range(1, p.shape[0] // 8):
        red = red + p[8 * i:8 * i + 8, :]
    return red


def _conv_fwd(z, cw, bias):
    nt = T // TB
    hb = TB // HALO

    def body(z_ref, zh_ref, cw_ref, b_ref, a1_ref, q_ref, apad, ppad):
        first = pl.program_id(1) == 0
        apad[0:HALO, :] = jnp.where(first, 0.0, zh_ref[0] * _sig(zh_ref[1]))
        apad[HALO:, :] = z_ref[0] * _sig(z_ref[1])
        ppad[0:HALO, :] = jnp.where(first, 0.0, zh_ref[3] * zh_ref[4])
        ppad[HALO:, :] = z_ref[3] * z_ref[4]
        bias_row = b_ref[...]

        def chunk(r, carry):
            base = pl.multiple_of(r * CH, CH)
            w = apad[pl.ds(base, CH + HALO), :]
            acc = jnp.broadcast_to(bias_row, (CH, LANE))
            for b in range(8):
                wb = _shift_up(w, b)
                for a in range(5):
                    s = 8 * a + b
                    if 2 <= s <= HALO:
                        acc = acc + cw_ref[pl.ds(s - 2, 1), :] * wb[8 * a:8 * a + CH, :]
            a1_ref[pl.ds(base, CH), :] = acc
            pw = ppad[pl.ds(base, CH + HALO), :]
            v = (cw_ref[pl.ds(32, 1), :] * _shift_up(pw, 6)[24:24 + CH, :]
                 + cw_ref[pl.ds(33, 1), :] * _shift_up(pw, 7)[24:24 + CH, :]
                 + cw_ref[pl.ds(34, 1), :] * pw[32:32 + CH, :])
            q_ref[pl.ds(base, CH), :] = (z_ref[2, pl.ds(base, CH), :] * v).astype(BF)
            return carry

        lax.fori_loop(0, TB // CH, chunk, 0)

    return pl.pallas_call(
        body, name="conv_fwd", grid=(D // LANE, nt),
        out_shape=(jax.ShapeDtypeStruct((T, D), F32), jax.ShapeDtypeStruct((T, D), BF)),
        in_specs=[pl.BlockSpec((5, TB, LANE), lambda c, t: (0, t, c)),
                  pl.BlockSpec((5, HALO, LANE), lambda c, t: (0, jnp.maximum(t * hb - 1, 0), c)),
                  pl.BlockSpec((None, 40, LANE), lambda c, t: (c, 0, 0)),
                  pl.BlockSpec((1, LANE), lambda c, t: (0, c))],
        out_specs=(pl.BlockSpec((TB, LANE), lambda c, t: (t, c)), pl.BlockSpec((TB, LANE), lambda c, t: (t, c))),
        scratch_shapes=[pltpu.VMEM((TB + HALO, LANE), F32), pltpu.VMEM((TB + HALO, LANE), F32)],
        compiler_params=_params(("arbitrary", "arbitrary")),
    )(z, z, cw, bias)


def _layernorm_silu(a1, lng, lnb):
    mu = jnp.mean(a1, axis=-1, keepdims=True)
    xc = a1 - mu
    rs = lax.rsqrt(jnp.mean(xc * xc, axis=-1, keepdims=True) + EPS)
    xh = xc * rs
    a2 = xh * lng + lnb
    sg = _sig(a2)
    return xh, rs, a2, sg


def _mix_out(a1, q, z, h1, lng, lnb, wcat):
    lay = _Layout()
    ia, ib, io = lay.wc["wa"] // D, lay.wc["wb"] // D, lay.wc["wo"] // D

    def body(a1_ref, q_ref, ga_ref, gb_ref, h_ref, lng_ref, lnb_ref, wa_ref, wb_ref, wo_ref, h2_ref, ya_ref, yb_ref):
        _, _, a2, sg = _layernorm_silu(a1_ref[...], lng_ref[...], lnb_ref[...])
        ya = _nn((a2 * sg).astype(BF), wa_ref[...])
        yb = _nn(q_ref[...], wb_ref[...])
        ya_ref[...] = ya
        yb_ref[...] = yb
        m = _sig(ga_ref[...]) * ya + _sig(gb_ref[...]) * yb
        h2_ref[...] = h_ref[...] + _nn(m.astype(BF), wo_ref[...])

    row = lambda i: (i, 0)
    vec = pl.BlockSpec((1, D), lambda i: (0, 0))
    return pl.pallas_call(
        body, name="mix_out", grid=(T // TM,),
        out_shape=(jax.ShapeDtypeStruct((T, D), F32),) * 3,
        in_specs=[pl.BlockSpec((TM, D), row), pl.BlockSpec((TM, D), row),
                  pl.BlockSpec((None, TM, D), lambda i: (5, i, 0)), pl.BlockSpec((None, TM, D), lambda i: (6, i, 0)),
                  pl.BlockSpec((TM, D), row), vec, vec,
                  pl.BlockSpec((D, D), lambda i: (ia, 0)), pl.BlockSpec((D, D), lambda i: (ib, 0)),
                  pl.BlockSpec((D, D), lambda i: (io, 0))],
        out_specs=(pl.BlockSpec((TM, D), row),) * 3,
        compiler_params=_params(("arbitrary",)),
    )(a1, q, z, z, h1, lng, lnb, wcat, wcat, wcat)


def _final_loss(h3, gf, tgt):
    def body(h_ref, g_ref, t_ref, dh_ref, s_ref):
        @pl.when(pl.program_id(0) == 0)
        def _():
            s_ref[...] = jnp.zeros_like(s_ref)
        xf = h_ref[...]
        g = g_ref[...]
        r = lax.rsqrt(jnp.mean(xf * xf, axis=-1, keepdims=True) + EPS)
        xr = xf * r
        e = xr * g - t_ref[...]
        s_ref[1:2, :] += jnp.sum(e * e, axis=0, keepdims=True) * (0.5 / D)
        dy = e * (1.0 / D)
        s_ref[0:1, :] += jnp.sum(dy * xr, axis=0, keepdims=True)
        gdy = dy * g
        dh_ref[...] = r * gdy - xr * (r * jnp.mean(gdy * xr, axis=-1, keepdims=True))

    row = lambda i: (i, 0)
    return pl.pallas_call(
        body, name="final_loss", grid=(T // TM,),
        out_shape=(jax.ShapeDtypeStruct((T, D), F32), jax.ShapeDtypeStruct((8, D), F32)),
        in_specs=[pl.BlockSpec((TM, D), row), pl.BlockSpec((1, D), lambda i: (0, 0)), pl.BlockSpec((TM, D), row)],
        out_specs=(pl.BlockSpec((TM, D), row), pl.BlockSpec((8, D), lambda i: (0, 0))),
        compiler_params=_params(("arbitrary",)),
    )(h3, gf, tgt)


def _rmsnorm_bwd(xf, g, dn):
    r = lax.rsqrt(jnp.mean(xf * xf, axis=-1, keepdims=True) + EPS)
    xr = xf * r
    gdn = dn * g
    dx = r * gdn - xr * (r * jnp.mean(gdn * xr, axis=-1, keepdims=True))
    return dx, jnp.sum(dn * xr, axis=0, keepdims=True)


def _ffn_bwd(dh, x, g, gg, uu, wcat, offs, name):
    nf = F // FC

    def body(dh_ref, x_ref, g_ref, gg_ref, uu_ref, wcat_ref, dx_ref, dgu_ref, a_ref, s_ref, wg_s, wu_s, wd_s, sem):
        _load_ffn_weights(wcat_ref, offs, (wg_s, wu_s, wd_s), sem)

        @pl.when(pl.program_id(0) == 0)
        def _():
            s_ref[...] = jnp.zeros_like(s_ref)

        dhf = dh_ref[...]
        dhb = dhf.astype(BF)
        dn = jnp.zeros((TMB, D), F32)
        for c in range(nf):
            sl = slice(c * FC, (c + 1) * FC)
            da = 0.5 * _nt(dhb, wd_s[sl, :])
            gc = gg_ref[:, sl].astype(F32)
            uc = uu_ref[:, sl].astype(F32)
            sg = _sig(gc)
            silu = gc * sg
            dgc = (da * uc * (sg * (1.0 + gc * (1.0 - sg)))).astype(BF)
            duc = (da * silu).astype(BF)
            dgu_ref[0, :, sl] = dgc
            dgu_ref[1, :, sl] = duc
            a_ref[0, :, sl] = (0.5 * silu * uc).astype(BF)
            dn = dn + _nn(dgc, wg_s[sl, :]) + _nn(duc, wu_s[sl, :])
        dxn, dg = _rmsnorm_bwd(x_ref[...], g_ref[...], dn)
        dx_ref[...] = dhf + dxn
        s_ref[0:1, :] += dg

    row = lambda i: (i, 0)
    return pl.pallas_call(
        body, name=name, grid=(T // TMB,),
        out_shape=(jax.ShapeDtypeStruct((T, D), F32), jax.ShapeDtypeStruct((2, T, F), BF),
                   jax.ShapeDtypeStruct((1, T, F), BF), jax.ShapeDtypeStruct((8, D), F32)),
        in_specs=[pl.BlockSpec((TMB, D), row), pl.BlockSpec((TMB, D), row), pl.BlockSpec((1, D), lambda i: (0, 0)),
                  pl.BlockSpec((TMB, F), row), pl.BlockSpec((TMB, F), row), ANY],
        out_specs=(pl.BlockSpec((TMB, D), row), pl.BlockSpec((2, TMB, F), lambda i: (0, i, 0)),
                   pl.BlockSpec((1, TMB, F), lambda i: (0, i, 0)), pl.BlockSpec((8, D), lambda i: (0, 0))),
        scratch_shapes=[pltpu.VMEM((F, D), BF)] * 3 + [pltpu.SemaphoreType.DMA((3,))],
        compiler_params=_params(("arbitrary",)),
    )(dh, x, g, gg, uu, wcat)


def _tn_matmul(lhs, rhs, tr, name):
    ng, _, cdim = lhs.shape
    nc, nk = cdim // tr, T // TK

    def body(l_ref, r_ref, o_ref, acc):
        k = pl.program_id(2)

        @pl.when(k == 0)
        def _():
            acc[...] = jnp.zeros_like(acc)

        acc[...] += _tn(l_ref[...], r_ref[...].astype(BF))

        @pl.when(k == nk - 1)
        def _():
            o_ref[...] = acc[...].astype(BF)

    return pl.pallas_call(
        body, name=name, grid=(ng, nc, nk),
        out_shape=jax.ShapeDtypeStruct((ng * cdim, D), BF),
        in_specs=[pl.BlockSpec((None, TK, tr), lambda g, c, k: (g, k, c)),
                  pl.BlockSpec((TK, D), lambda g, c, k: (k, 0))],
        out_specs=pl.BlockSpec((tr, D), lambda g, c, k: (g * nc + c, 0)),
        scratch_shapes=[pltpu.VMEM((tr, D), F32)],
        compiler_params=_params(("arbitrary", "arbitrary", "arbitrary")),
    )(lhs, rhs)


def _mix_out_bwd(dh2, ya, yb, z, a1, lng, lnb, wcat):
    lay = _Layout()
    ia, ib, io = lay.wc["wa"] // D, lay.wc["wb"] // D, lay.wc["wo"] // D

    def body(dh_ref, ya_ref, yb_ref, ga_ref, gb_ref, a1_ref, lng_ref, lnb_ref, wa_ref, wb_ref, wo_ref,
             dzg_ref, da1_ref, dq_ref, m_ref, a3_ref, dya_ref, dyb_ref, s_ref):
        @pl.when(pl.program_id(0) == 0)
        def _():
            s_ref[...] = jnp.zeros_like(s_ref)

        dm = _nt(dh_ref[...].astype(BF), wo_ref[...])
        ya, yb = ya_ref[...], yb_ref[...]
        sa, sb = _sig(ga_ref[...]), _sig(gb_ref[...])
        m_ref[0] = (sa * ya + sb * yb).astype(BF)
        dzg_ref[0] = (dm * ya * (sa * (1.0 - sa))).astype(BF)
        dzg_ref[1] = (dm * yb * (sb * (1.0 - sb))).astype(BF)
        dya = (dm * sa).astype(BF)
        dyb = (dm * sb).astype(BF)
        dya_ref[...] = dya
        dyb_ref[...] = dyb
        dq_ref[...] = _nt(dyb, wb_ref[...])
        da3 = _nt(dya, wa_ref[...])
        lng = lng_ref[...]
        xh, rs, a2, sg = _layernorm_silu(a1_ref[...], lng, lnb_ref[...])
        a3_ref[0] = (a2 * sg).astype(BF)
        da2 = da3 * (sg * (1.0 + a2 * (1.0 - sg)))
        s_ref[0:1, :] += jnp.sum(da2 * xh, axis=0, keepdims=True)
        s_ref[1:2, :] += jnp.sum(da2, axis=0, keepdims=True)
        dxh = da2 * lng
        da1 = rs * (dxh - jnp.mean(dxh, axis=-1, keepdims=True) - xh * jnp.mean(dxh * xh, axis=-1, keepdims=True))
        da1_ref[...] = da1
        s_ref[2:3, :] += jnp.sum(da1, axis=0, keepdims=True)

    row = lambda i: (i, 0)
    row3 = lambda i: (0, i, 0)
    vec = pl.BlockSpec((1, D), lambda i: (0, 0))
    return pl.pallas_call(
        body, name="mix_out_bwd", grid=(T // TM,),
        out_shape=(jax.ShapeDtypeStruct((2, T, D), BF), jax.ShapeDtypeStruct((T, D), F32),
                   jax.ShapeDtypeStruct((T, D), F32), jax.ShapeDtypeStruct((1, T, D), BF),
                   jax.ShapeDtypeStruct((1, T, D), BF), jax.ShapeDtypeStruct((T, D), BF),
                   jax.ShapeDtypeStruct((T, D), BF), jax.ShapeDtypeStruct((8, D), F32)),
        in_specs=[pl.BlockSpec((TM, D), row), pl.BlockSpec((TM, D), row), pl.BlockSpec((TM, D), row),
                  pl.BlockSpec((None, TM, D), lambda i: (5, i, 0)), pl.BlockSpec((None, TM, D), lambda i: (6, i, 0)),
                  pl.BlockSpec((TM, D), row), vec, vec,
                  pl.BlockSpec((D, D), lambda i: (ia, 0)), pl.BlockSpec((D, D), lambda i: (ib, 0)),
                  pl.BlockSpec((D, D), lambda i: (io, 0))],
        out_specs=(pl.BlockSpec((2, TM, D), row3), pl.BlockSpec((TM, D), row), pl.BlockSpec((TM, D), row),
                   pl.BlockSpec((1, TM, D), row3), pl.BlockSpec((1, TM, D), row3), pl.BlockSpec((TM, D), row),
                   pl.BlockSpec((TM, D), row), pl.BlockSpec((8, D), lambda i: (0, 0))),
        compiler_params=_params(("arbitrary",)),
    )(dh2, ya, yb, z, z, a1, lng, lnb, wcat, wcat, wcat)


def _conv_bwd(z, da1, dq, dzg, cw):
    nt = T // TB
    hb = TB // HALO
    last_h = T // HALO - 1

    def body(z_ref, zp_ref, zn_ref, da1_ref, da1n_ref, dq_ref, dqn_ref, dzg_ref, cw_ref,
             dz_ref, dwa_ref, dwb_ref, apad, dypad, ppad, dvpad, acc_a, acc_b):
        t = pl.program_id(1)
        first, last = t == 0, t == nt - 1
        apad[0:HALO, :] = jnp.where(first, 0.0, zp_ref[0] * _sig(zp_ref[1]))
        apad[HALO:, :] = z_ref[0] * _sig(z_ref[1])
        ppad[0:HALO, :] = jnp.where(first, 0.0, zp_ref[3] * zp_ref[4])
        ppad[HALO:, :] = z_ref[3] * z_ref[4]
        dypad[0:TB, :] = da1_ref[...]
        dypad[TB:, :] = jnp.where(last, 0.0, da1n_ref[...])
        dvpad[0:TB, :] = dq_ref[...] * z_ref[2]
        dvpad[TB:, :] = jnp.where(last, 0.0, dqn_ref[...] * zn_ref[2])

        @pl.when(t == 0)
        def _():
            acc_a[...] = jnp.zeros_like(acc_a)
            acc_b[...] = jnp.zeros_like(acc_b)

        def chunk(r, carry):
            base = pl.multiple_of(r * CH, CH)
            rows = pl.ds(base, CH)
            dw_ = dypad[pl.ds(base, CH + HALO), :]
            da0 = jnp.zeros((CH, LANE), F32)
            for b in range(8):
                wb = _shift_up(dw_, b)
                for a in range(4):
                    o = 8 * a + b
                    if o <= KA - 1:
                        da0 = da0 + cw_ref[pl.ds(KA - 1 - o, 1), :] * wb[8 * a:8 * a + CH, :]
            z0, z1 = z_ref[0, rows, :], z_ref[1, rows, :]
            s1 = _sig(z1)
            dz_ref[0, rows, :] = (da0 * s1).astype(BF)
            dz_ref[1, rows, :] = (da0 * z0 * (s1 * (1.0 - s1))).astype(BF)
            dyc = dypad[rows, :]
            aw = apad[pl.ds(base, CH + HALO), :]
            for b in range(8):
                wb = _shift_up(aw, b)
                for a in range(5):
                    s = 8 * a + b
                    if 2 <= s <= HALO:
                        k8 = 8 * (s - 2)
                        acc_a[k8:k8 + 8, :] += _fold8(dyc * wb[8 * a:8 * a + CH, :])
            pw = ppad[pl.ds(base, CH + HALO), :]
            p6 = _shift_up(pw, 6)[24:24 + CH, :]
            p7 = _shift_up(pw, 7)[24:24 + CH, :]
            p8 = pw[32:32 + CH, :]
            wb0, wb1, wb2 = cw_ref[pl.ds(32, 1), :], cw_ref[pl.ds(33, 1), :], cw_ref[pl.ds(34, 1), :]
            v = wb0 * p6 + wb1 * p7 + wb2 * p8
            dz_ref[2, rows, :] = (dq_ref[rows, :] * v).astype(BF)
            dvw = dvpad[pl.ds(base, CH + HALO), :]
            dvc = dvw[0:CH, :]
            dp = wb2 * dvc + wb1 * _shift_up(dvw, 1)[0:CH, :] + wb0 * _shift_up(dvw, 2)[0:CH, :]
            dz_ref[3, rows, :] = (dp * z_ref[4, rows, :]).astype(BF)
            dz_ref[4, rows, :] = (dp * z_ref[3, rows, :]).astype(BF)
            acc_b[0:8, :] += _fold8(dvc * p6)
            acc_b[8:16, :] += _fold8(dvc * p7)
            acc_b[16:24, :] += _fold8(dvc * p8)
            dz_ref[5, rows, :] = dzg_ref[0, rows, :]
            dz_ref[6, rows, :] = dzg_ref[1, rows, :]
            return carry

        lax.fori_loop(0, TB // CH, chunk, 0)

        @pl.when(t == nt - 1)
        def _():
            for k in range(KA):
                dwa_ref[k:k + 1, :] = jnp.sum(acc_a[8 * k:8 * k + 8, :], axis=0, keepdims=True)
            dwa_ref[KA:32, :] = jnp.zeros((32 - KA, LANE), F32)
            for k in range(KB):
                dwb_ref[k:k + 1, :] = jnp.sum(acc_b[8 * k:8 * k + 8, :], axis=0, keepdims=True)
            dwb_ref[KB:8, :] = jnp.zeros((8 - KB, LANE), F32)

    blk = lambda c, t: (t, c)
    nxt = lambda c, t: (jnp.minimum((t + 1) * hb, last_h), c)
    return pl.pallas_call(
        body, name="conv_bwd", grid=(D // LANE, nt),
        out_shape=(jax.ShapeDtypeStruct((NG, T, D), BF), jax.ShapeDtypeStruct((32, D), F32),
                   jax.ShapeDtypeStruct((8, D), F32)),
        in_specs=[pl.BlockSpec((5, TB, LANE), lambda c, t: (0, t, c)),
                  pl.BlockSpec((5, HALO, LANE), lambda c, t: (0, jnp.maximum(t * hb - 1, 0), c)),
                  pl.BlockSpec((5, HALO, LANE), lambda c, t: (0, jnp.minimum((t + 1) * hb, last_h), c)),
                  pl.BlockSpec((TB, LANE), blk), pl.BlockSpec((HALO, LANE), nxt),
                  pl.BlockSpec((TB, LANE), blk), pl.BlockSpec((HALO, LANE), nxt),
                  pl.BlockSpec((2, TB, LANE), lambda c, t: (0, t, c)),
                  pl.BlockSpec((None, 40, LANE), lambda c, t: (c, 0, 0))],
        out_specs=(pl.BlockSpec((NG, TB, LANE), lambda c, t: (0, t, c)),
                   pl.BlockSpec((32, LANE), lambda c, t: (0, c)), pl.BlockSpec((8, LANE), lambda c, t: (0, c))),
        scratch_shapes=[pltpu.VMEM((TB + HALO, LANE), F32)] * 4
                       + [pltpu.VMEM((8 * 32, LANE), F32), pltpu.VMEM((24, LANE), F32)],
        compiler_params=_params(("arbitrary", "arbitrary")),
    )(z, z, z, da1, da1, dq, dq, dzg, cw)


def _mix_in_bwd(dz, dh2, h1, gm, wcat):
    lay = _Layout()
    wblk = lay.wc["win"] // D

    def body(dz_ref, w_ref, dh_ref, h_ref, g_ref, o_ref, s_ref, acc):
        i, j = pl.program_id(0), pl.program_id(1)

        @pl.when((i == 0) & (j == 0))
        def _():
            s_ref[...] = jnp.zeros_like(s_ref)

        @pl.when(j == 0)
        def _():
            acc[...] = jnp.zeros_like(acc)

        acc[...] += _nn(dz_ref[...], w_ref[...])

        @pl.when(j == NG - 1)
        def _():
            dx, dg = _rmsnorm_bwd(h_ref[...], g_ref[...], acc[...])
            o_ref[...] = dh_ref[...] + dx
            s_ref[0:1, :] += dg

    row = lambda i, j: (i, 0)
    return pl.pallas_call(
        body, name="mix_in_bwd", grid=(T // TMI, NG),
        out_shape=(jax.ShapeDtypeStruct((T, D), F32), jax.ShapeDtypeStruct((8, D), F32)),
        in_specs=[pl.BlockSpec((None, TMI, D), lambda i, j: (j, i, 0)),
                  pl.BlockSpec((D, D), lambda i, j: (wblk + j, 0)),
                  pl.BlockSpec((TMI, D), row), pl.BlockSpec((TMI, D), row), pl.BlockSpec((1, D), lambda i, j: (0, 0))],
        out_specs=(pl.BlockSpec((TMI, D), row), pl.BlockSpec((8, D), lambda i, j: (0, 0))),
        scratch_shapes=[pltpu.VMEM((TMI, D), F32)],
        compiler_params=_params(("arbitrary", "arbitrary")),
    )(dz, wcat, dh2, h1, gm)


def _row_tile(n, want, mult):
    for t in range(min(want, n), 0, -1):
        if n % t == 0 and t % mult == 0:
            return t
    return n


def _sum_slots(recv, name):
    _, rows, cols = recv.shape
    tr = _row_tile(rows, 1024, 16)

    def body(r_ref, o_ref):
        s = r_ref[0].astype(F32)
        for k in range(1, NDEV):
            s = s + r_ref[k].astype(F32)
        o_ref[...] = s

    return pl.pallas_call(
        body, name=name, grid=(rows // tr,),
        out_shape=jax.ShapeDtypeStruct((rows, cols), F32),
        in_specs=[pl.BlockSpec((NDEV, tr, cols), lambda i: (0, i, 0))],
        out_specs=pl.BlockSpec((tr, cols), lambda i: (i, 0)),
        compiler_params=_params(("arbitrary",)),
    )(recv)


def _sum_small(parts, loss_row):
    _, rows, cols = parts.shape

    def body(p_ref, o_ref, l_ref):
        s = p_ref[0]
        for k in range(1, NDEV):
            s = s + p_ref[k]
        o_ref[...] = s
        l_ref[...] = jnp.broadcast_to(jnp.sum(s[loss_row:loss_row + 1, :], axis=-1, keepdims=True), (8, LANE))

    return pl.pallas_call(
        body, name="sum_small",
        out_shape=(jax.ShapeDtypeStruct((rows, cols), F32), jax.ShapeDtypeStruct((8, LANE), F32)),
    )(parts)


def _adam(gs, ws, ms, vs, name):
    n = len(gs)
    rows, cols = ws[0].shape
    tr = _row_tile(rows, 256, 8)
    c1 = 1.0 - ADAM_B1 ** ADAM_STEP
    c2 = 1.0 - ADAM_B2 ** ADAM_STEP

    def body(*refs):
        for i in range(n):
            g, w, m, v = (refs[4 * i + k][...] for k in range(4))
            d_ref, m_ref, v_ref = refs[4 * n + 3 * i: 4 * n + 3 * i + 3]
            m2 = ADAM_B1 * m + (1.0 - ADAM_B1) * g
            v2 = ADAM_B2 * v + (1.0 - ADAM_B2) * (g * g)
            d_ref[...] = -ADAM_LR * ((m2 / c1) / (jnp.sqrt(v2 / c2) + ADAM_EPS) + ADAM_WD * w)
            m_ref[...] = m2
            v_ref[...] = v2

    spec = pl.BlockSpec((tr, cols), lambda i: (i, 0))
    args = []
    for i in range(n):
        args += [gs[i], ws[i], ms[i], vs[i]]
    outs = pl.pallas_call(
        body, name=name, grid=(rows // tr,),
        out_shape=(jax.ShapeDtypeStruct((rows, cols), F32),) * (3 * n),
        in_specs=[spec] * (4 * n), out_specs=(spec,) * (3 * n),
        compiler_params=_params(("arbitrary",)),
    )(*args)
    return [tuple(outs[3 * i: 3 * i + 3]) for i in range(n)]


def kernel(x, ffn1_norm, ffn1_w_gate, ffn1_w_up, ffn1_w_down, mix_norm, w_in, a_dw_w, a_dw_b, a_ln_g, a_ln_b, a_w_out, b_conv_w, b_w_out, w_o, ffn2_norm, ffn2_w_gate, ffn2_w_up, ffn2_w_down, final_norm, loss_target, m_ffn1_norm, m_ffn1_w_gate, m_ffn1_w_up, m_ffn1_w_down, m_mix_norm, m_w_in, m_a_dw_w, m_a_dw_b, m_a_ln_g, m_a_ln_b, m_a_w_out, m_b_conv_w, m_b_w_out, m_w_o, m_ffn2_norm, m_ffn2_w_gate, m_ffn2_w_up, m_ffn2_w_down, m_final_norm, v_ffn1_norm, v_ffn1_w_gate, v_ffn1_w_up, v_ffn1_w_down, v_mix_norm, v_w_in, v_a_dw_w, v_a_dw_b, v_a_ln_g, v_a_ln_b, v_a_w_out, v_b_conv_w, v_b_w_out, v_w_o, v_ffn2_norm, v_ffn2_w_gate, v_ffn2_w_up, v_ffn2_w_down, v_final_norm):
    lay = _Layout()
    names = ("ffn1_norm", "ffn1_w_gate", "ffn1_w_up", "ffn1_w_down", "mix_norm", "w_in", "a_dw_w", "a_dw_b",
             "a_ln_g", "a_ln_b", "a_w_out", "b_conv_w", "b_w_out", "w_o", "ffn2_norm", "ffn2_w_gate", "ffn2_w_up",
             "ffn2_w_down", "final_norm")
    w = dict(ffn1_norm=ffn1_norm, ffn1_w_gate=ffn1_w_gate, ffn1_w_up=ffn1_w_up, ffn1_w_down=ffn1_w_down,
             mix_norm=mix_norm, w_in=w_in, a_dw_w=a_dw_w, a_dw_b=a_dw_b, a_ln_g=a_ln_g, a_ln_b=a_ln_b,
             a_w_out=a_w_out, b_conv_w=b_conv_w, b_w_out=b_w_out, w_o=w_o, ffn2_norm=ffn2_norm,
             ffn2_w_gate=ffn2_w_gate, ffn2_w_up=ffn2_w_up, ffn2_w_down=ffn2_w_down, final_norm=final_norm)
    m = dict(ffn1_norm=m_ffn1_norm, ffn1_w_gate=m_ffn1_w_gate, ffn1_w_up=m_ffn1_w_up, ffn1_w_down=m_ffn1_w_down,
             mix_norm=m_mix_norm, w_in=m_w_in, a_dw_w=m_a_dw_w, a_dw_b=m_a_dw_b, a_ln_g=m_a_ln_g, a_ln_b=m_a_ln_b,
             a_w_out=m_a_w_out, b_conv_w=m_b_conv_w, b_w_out=m_b_w_out, w_o=m_w_o, ffn2_norm=m_ffn2_norm,
             ffn2_w_gate=m_ffn2_w_gate, ffn2_w_up=m_ffn2_w_up, ffn2_w_down=m_ffn2_w_down, final_norm=m_final_norm)
    v = dict(ffn1_norm=v_ffn1_norm, ffn1_w_gate=v_ffn1_w_gate, ffn1_w_up=v_ffn1_w_up, ffn1_w_down=v_ffn1_w_down,
             mix_norm=v_mix_norm, w_in=v_w_in, a_dw_w=v_a_dw_w, a_dw_b=v_a_dw_b, a_ln_g=v_a_ln_g, a_ln_b=v_a_ln_b,
             a_w_out=v_a_w_out, b_conv_w=v_b_conv_w, b_w_out=v_b_w_out, w_o=v_w_o, ffn2_norm=v_ffn2_norm,
             ffn2_w_gate=v_ffn2_w_gate, ffn2_w_up=v_ffn2_w_up, ffn2_w_down=v_ffn2_w_down, final_norm=v_final_norm)
    me = 4 * lax.axis_index("x") + 2 * lax.axis_index("y") + lax.axis_index("c")

    to_rows = dict(wg1=ffn1_w_gate[0].T, wu1=ffn1_w_up[0].T, wd1=ffn1_w_down[0], wg2=ffn2_w_gate[0].T,
                   wu2=ffn2_w_up[0].T, wd2=ffn2_w_down[0], win=w_in[0].T, wa=a_w_out[0], wb=b_w_out[0], wo=w_o[0])
    flat = jnp.concatenate([to_rows[n].astype(BF) for n in ORDER], axis=0)
    cw_shard = jnp.concatenate([a_dw_w[0], jnp.zeros((1, LANE), F32), b_conv_w[0], jnp.zeros((5, LANE), F32)], axis=0)

    wcat = _ag_weights(flat)
    cw = _ag_small(cw_shard, "ag_conv_w")

    x2, tgt = x[0], loss_target[0]
    offs1 = (lay.wc["wg1"], lay.wc["wu1"], lay.wc["wd1"])
    offs2 = (lay.wc["wg2"], lay.wc["wu2"], lay.wc["wd2"])

    h1, n1, gg1, uu1 = _ffn_fwd(x2, ffn1_norm, wcat, offs1, "ffn1_fwd")
    u, z = _mix_in(h1, mix_norm, wcat)
    a1, q = _conv_fwd(z, cw, a_dw_b)
    h2, ya, yb = _mix_out(a1, q, z, h1, a_ln_g, a_ln_b, wcat)
    h3, n2, gg2, uu2 = _ffn_fwd(h2, ffn2_norm, wcat, offs2, "ffn2_fwd")
    dh3, s_final = _final_loss(h3, final_norm.reshape(1, D), tgt)

    tr_f = F // 2 if (F // 2) % LANE == 0 else F
    dh2, dgu2, act2, s_ffn2 = _ffn_bwd(dh3, h2, ffn2_norm, gg2, uu2, wcat, offs2, "ffn2_bwd")
    gu2 = _tn_matmul(dgu2, n2, tr_f, "dw_gu2")
    gd2 = _tn_matmul(act2, dh3, tr_f, "dw_d2")
    dzg, da1, dq, mb, a3b, dya, dyb, s_mix = _mix_out_bwd(dh2, ya, yb, z, a1, a_ln_g, a_ln_b, wcat)
    go = _tn_matmul(mb, dh2, D, "dw_o")
    ga = _tn_matmul(a3b, dya, D, "dw_a")
    gb = _tn_matmul(q.reshape(1, T, D), dyb, D, "dw_b")
    dz, dwa, dwb = _conv_bwd(z, da1, dq, dzg, cw)
    dh1, s_in = _mix_in_bwd(dz, dh2, h1, mix_norm, wcat)
    gin = _tn_matmul(dz, u, D, "dw_in")
    dx, dgu1, act1, s_ffn1 = _ffn_bwd(dh1, x2, ffn1_norm, gg1, uu1, wcat, offs1, "ffn1_bwd")
    gu1 = _tn_matmul(dgu1, n1, tr_f, "dw_gu1")
    gd1 = _tn_matmul(act1, dh1, tr_f, "dw_d1")

    recv = _rs_grads(gu1, gd1, gu2, gd2, gin, ga, gb, go)
    gsum = _sum_slots(recv, "sum_grads")
    small = jnp.concatenate([s_ffn1, s_in, s_mix, s_ffn2, s_final, dwa, dwb], axis=0)
    small_sum, loss_blk = _sum_small(_ag_small(small, "ag_small"), 33)
    loss = loss_blk[0, 0]

    def seg(n):
        return gsum[lay.fl[n]:lay.fl[n] + lay.rows[n]]

    g = dict(ffn1_w_gate=seg("wg1").T, ffn1_w_up=seg("wu1").T, ffn1_w_down=seg("wd1"),
             ffn2_w_gate=seg("wg2").T, ffn2_w_up=seg("wu2").T, ffn2_w_down=seg("wd2"), w_in=seg("win").T,
             a_w_out=seg("wa"), b_w_out=seg("wb"), w_o=seg("wo"),
             ffn1_norm=small_sum[0:1], mix_norm=small_sum[8:9], a_ln_g=small_sum[16:17], a_ln_b=small_sum[17:18],
             a_dw_b=small_sum[18:19], ffn2_norm=small_sum[24:25], final_norm=small_sum[32:33],
             a_dw_w=lax.dynamic_slice_in_dim(small_sum[40:40 + KA], me * LANE, LANE, axis=1),
             b_conv_w=lax.dynamic_slice_in_dim(small_sum[72:72 + KB], me * LANE, LANE, axis=1))

    upd = {}

    def run(group, name, as2d=lambda a: a[0], back=lambda a, n: a.reshape(w[n].shape)):
        res = _adam([g[n] for n in group], [as2d(w[n]) for n in group], [as2d(m[n]) for n in group],
                    [as2d(v[n]) for n in group], name)
        for n, r in zip(group, res):
            upd[n] = tuple(back(a, n) for a in r)

    run(("ffn1_w_gate", "ffn1_w_up", "ffn2_w_gate", "ffn2_w_up"), "adam_gate_up")
    run(("ffn1_w_down", "ffn2_w_down"), "adam_down")
    run(("w_in",), "adam_in")
    run(("a_w_out", "b_w_out", "w_o"), "adam_square")
    run(("a_dw_w",), "adam_dw")
    run(("b_conv_w",), "adam_conv")
    vecs = ("ffn1_norm", "mix_norm", "a_dw_b", "a_ln_g", "a_ln_b", "ffn2_norm", "final_norm")
    run(vecs, "adam_vec", as2d=lambda a: a.reshape(1, D))

    grads = [g[n].reshape(w[n].shape) for n in names]
    return (loss, dx.reshape(x.shape), *grads, *[upd[n][0] for n in names], *[upd[n][1] for n in names],
            *[upd[n][2] for n in names])
```

```python
import jax
import jax.numpy as jnp
from jax import lax
from jax.experimental import pallas as pl
from jax.experimental.pallas import tpu as pltpu

T = 4096
D = 1024
F = 2816
NG = 7
NDEV = 8
NCHIP = 4
KA, KB = 31, 3
EPS = 1e-6
ADAM_LR, ADAM_B1, ADAM_B2, ADAM_EPS, ADAM_WD, ADAM_STEP = 0.001, 0.9, 0.999, 1e-08, 0.01, 10

TM = 512
TMB = 256
TMI = 1024
FC = 256
TB = 1024
CH = 128
HALO = 32
LANE = 128
TK = 1024
VMEM_LIMIT = 56 * 1024 * 1024

BF = jnp.bfloat16
F32 = jnp.float32
MESH = pl.DeviceIdType.MESH
ANY = pl.BlockSpec(memory_space=pl.ANY)

ORDER = ("wg1", "wu1", "wd1", "wg2", "wu2", "wd2", "win", "wa", "wb", "wo")


class _Layout:
    def __init__(self):
        fs, dis, ds = F // NDEV, NG * D // NDEV, D // NDEV
        self.rows = dict(wg1=fs, wu1=fs, wd1=fs, wg2=fs, wu2=fs, wd2=fs, win=dis, wa=ds, wb=ds, wo=ds)
        self.fl, off = {}, 0
        for n in ORDER:
            self.fl[n] = off
            off += self.rows[n]
        self.RT = off


class _Stage:
    def __init__(self, names):
        lay = _Layout()
        self.names, self.fl = names, lay.fl
        self.rows = {n: lay.rows[n] for n in names}
        self.off, self.wc, o, w = {}, {}, 0, 0
        for n in names:
            self.off[n], self.wc[n] = o, w
            o += self.rows[n]
            w += NDEV * self.rows[n]
        self.R, self.W = o, w


def _nt(a, b):
    return lax.dot_general(a, b, (((1,), (1,)), ((), ())), preferred_element_type=F32)


def _nn(a, b):
    return lax.dot_general(a, b, (((1,), (0,)), ((), ())), preferred_element_type=F32)


def _tn(a, b):
    return lax.dot_general(a, b, (((0,), (0,)), ((), ())), preferred_element_type=F32)


def _sig(x):
    return 1.0 / (1.0 + jnp.exp(-x))


def _position():
    return lax.axis_index("x"), lax.axis_index("y"), lax.axis_index("c")


def _peer(pos, j):
    x, y, c = pos
    return (1 - x if j & 4 else x, 1 - y if j & 2 else y, 1 - c if j & 1 else c)


def _lin(pos):
    return 4 * pos[0] + 2 * pos[1] + pos[2]


def _chip(pos):
    return 2 * pos[0] + pos[1]


class _Comm:
    def __init__(self, inputs, out_shapes, scratch, start, finish):
        self.inputs, self.out_shapes, self.scratch, self.start, self.finish = inputs, out_shapes, scratch, start, finish


def _call(body, *, name, grid, args, in_specs, out_shape, out_specs, scratch_shapes=(), comm=None,
          num_scalar_prefetch=0):
    in_specs, out_shape, out_specs, scratch_shapes = list(in_specs), list(out_shape), list(out_specs), list(scratch_shapes)
    n_in, n_out, n_scr = len(in_specs), len(out_shape), len(scratch_shapes)
    sp = num_scalar_prefetch
    if comm is None:
        kernel_fn = lambda *refs: body(*refs)
        c_in = c_out = c_scr = 0
    else:
        c_in, c_out, c_scr = len(comm.inputs), len(comm.out_shapes), len(comm.scratch)

        def kernel_fn(*refs):
            pre, refs = refs[:sp], refs[sp:]
            ins, cins = refs[:n_in], refs[n_in:n_in + c_in]
            o0 = n_in + c_in
            outs, couts = refs[o0:o0 + n_out], refs[o0 + n_out:o0 + n_out + c_out]
            s0 = o0 + n_out + c_out
            scr, cscr = refs[s0:s0 + n_scr], refs[s0 + n_scr:]
            first = pl.program_id(0) == 0
            last = pl.program_id(0) == grid[0] - 1
            for a in range(1, len(grid)):
                first = first & (pl.program_id(a) == 0)
                last = last & (pl.program_id(a) == grid[a] - 1)

            @pl.when(first)
            def _():
                comm.start(cins, couts, cscr)

            body(*pre, *ins, *outs, *scr)

            @pl.when(last)
            def _():
                comm.finish(cins, couts, cscr)

        args = list(args) + list(comm.inputs)
        in_specs += [ANY] * c_in
        out_shape += list(comm.out_shapes)
        out_specs += [ANY] * c_out
        scratch_shapes += list(comm.scratch)
    params = pltpu.CompilerParams(dimension_semantics=("arbitrary",) * len(grid), vmem_limit_bytes=VMEM_LIMIT)
    if sp:
        grid_spec = pltpu.PrefetchScalarGridSpec(num_scalar_prefetch=sp, grid=grid, in_specs=in_specs,
                                                 out_specs=out_specs, scratch_shapes=scratch_shapes)
        return pl.pallas_call(kernel_fn, name=name, grid_spec=grid_spec, out_shape=out_shape,
                              compiler_params=params)(*args)
    return pl.pallas_call(kernel_fn, name=name, grid=grid, in_specs=in_specs, out_shape=out_shape, out_specs=out_specs,
                          scratch_shapes=scratch_shapes, compiler_params=params)(*args)


def _run_comm(comm, name):
    def body(*refs):
        c_in, c_out = len(comm.inputs), len(comm.out_shapes)
        comm.start(refs[:c_in], refs[c_in:c_in + c_out], refs[c_in + c_out:])
        comm.finish(refs[:c_in], refs[c_in:c_in + c_out], refs[c_in + c_out:])

    return pl.pallas_call(
        body, name=name, out_shape=list(comm.out_shapes), in_specs=[ANY] * len(comm.inputs),
        out_specs=[ANY] * len(comm.out_shapes), scratch_shapes=list(comm.scratch))(*comm.inputs)


def _ag_comm(names, flat):
    st = _Stage(names)

    def parts(refs):
        (flat_ref,), (out_ref,), (send_sems, recv_sems, local_sem) = refs
        me = _position()

        def region(name, dev):
            r = st.rows[name]
            return out_ref.at[pl.ds(st.wc[name] + _lin(dev) * r, r), :]

        def own(name):
            return flat_ref.at[pl.ds(st.fl[name], st.rows[name]), :]

        def copies(k, dev, to, from_flat):
            return [pltpu.make_async_remote_copy(
                src_ref=own(n) if from_flat else region(n, dev), dst_ref=region(n, dev), send_sem=send_sems.at[k],
                recv_sem=recv_sems.at[k], device_id=to, device_id_type=MESH) for n in names]

        def whole(k):
            return pltpu.make_async_remote_copy(
                src_ref=flat_ref.at[pl.ds(0, st.R), :], dst_ref=out_ref.at[pl.ds(0, st.R), :],
                send_sem=send_sems.at[k], recv_sem=recv_sems.at[k], device_id=me, device_id_type=MESH)

        return me, region, own, copies, whole, flat_ref, out_ref, local_sem

    def start(*refs):
        me, region, own, copies, _, _, _, local_sem = parts(refs)
        for n in names:
            pltpu.make_async_copy(own(n), region(n, me), local_sem).start()
        for cp in copies(0, me, _peer(me, 1), True):
            cp.start()
        for j, bits in enumerate((4, 2, 6)):
            for cp in copies(1 + j, me, _peer(me, bits), True):
                cp.start()

    def finish(*refs):
        me, _, _, copies, whole, flat_ref, out_ref, local_sem = parts(refs)
        for j, bits in enumerate((4, 2, 6)):
            whole(1 + j).wait_recv()
            for cp in copies(4 + j, _peer(me, bits), _peer(me, 1), False):
                cp.start()
        whole(0).wait_recv()
        for j in range(3):
            whole(4 + j).wait_recv()
        for k in range(7):
            whole(k).wait_send()
        pltpu.make_async_copy(flat_ref.at[pl.ds(0, st.R), :], out_ref.at[pl.ds(0, st.R), :], local_sem).wait()

    return _Comm([flat], [jax.ShapeDtypeStruct((st.W, D), BF)],
                 [pltpu.SemaphoreType.DMA((7,)), pltpu.SemaphoreType.DMA((7,)), pltpu.SemaphoreType.DMA],
                 start, finish)


def _rs_pair_comm(names, src):
    st = _Stage(names)
    arrays = []
    for n in names:
        if not any(src[n][0] is a for a in arrays):
            arrays.append(src[n][0])
    idx = {n: [i for i, a in enumerate(arrays) if a is src[n][0]][0] for n in names}

    def slot_wait(refs):
        recv = refs[1][0]
        send_sem, recv_sem = refs[2]
        return pltpu.make_async_remote_copy(src_ref=recv, dst_ref=recv, send_sem=send_sem, recv_sem=recv_sem,
                                            device_id=_position(), device_id_type=MESH)

    def start(*refs):
        ins, (recv,), (send_sem, recv_sem) = refs
        me = _position()
        sib = _peer(me, 1)
        for q in range(NCHIP):
            dev = (q // 2, q % 2, sib[2])
            for n in names:
                r = st.rows[n]
                pltpu.make_async_remote_copy(
                    src_ref=ins[idx[n]].at[pl.ds(src[n][1] + _lin(dev) * r, r), :],
                    dst_ref=recv.at[q, pl.ds(st.off[n], r), :], send_sem=send_sem, recv_sem=recv_sem,
                    device_id=sib, device_id_type=MESH).start()

    def finish(*refs):
        w = slot_wait(refs)
        w.wait_recv()
        w.wait_send()

    return _Comm(arrays, [jax.ShapeDtypeStruct((NCHIP, st.R, D), BF)],
                 [pltpu.SemaphoreType.DMA, pltpu.SemaphoreType.DMA], start, finish)


def _pair_add(names, src, recv, name):
    st = _Stage(names)
    c_arr = jnp.reshape(lax.axis_index("c"), (1,)).astype(jnp.int32)

    def body(c_ref, *refs):
        r_ref, o_ref = refs[len(names)], refs[len(names) + 1]
        for a_ref, n in zip(refs, names):
            rows = slice(st.off[n], st.off[n] + st.rows[n])
            o_ref[rows, :] = (a_ref[...].astype(F32) + r_ref[rows, :].astype(F32)).astype(BF)

    def shard_spec(n):
        r, base = st.rows[n], src[n][1] // st.rows[n]
        return pl.BlockSpec((r, D), lambda q, c_ref: (base + 2 * q + c_ref[0], 0))

    slot = pl.BlockSpec((None, st.R, D), lambda q, c_ref: (q, 0, 0))
    return _call(body, name=name, grid=(NCHIP,), args=[c_arr] + [src[n][0] for n in names] + [recv],
                 in_specs=[shard_spec(n) for n in names] + [slot],
                 out_shape=[jax.ShapeDtypeStruct((NCHIP, st.R, D), BF)], out_specs=[slot], num_scalar_prefetch=1)[0]


def _rs_chip_comm(part):
    def copies(refs):
        (p_ref,), (recv,), (send_sems, recv_sems, local_sem) = refs
        me = _position()
        mine = pltpu.make_async_copy(p_ref.at[_chip(me)], recv.at[_chip(me)], local_sem)
        out = []
        for j, bits in enumerate((4, 2, 6)):
            to = _peer(me, bits)
            out.append(pltpu.make_async_remote_copy(
                src_ref=p_ref.at[_chip(to)], dst_ref=recv.at[_chip(me)], send_sem=send_sems.at[j],
                recv_sem=recv_sems.at[j], device_id=to, device_id_type=MESH))
        return mine, out

    def start(*refs):
        mine, out = copies(refs)
        mine.start()
        for cp in out:
            cp.start()

    def finish(*refs):
        mine, out = copies(refs)
        for cp in out:
            cp.wait_recv()
        for cp in out:
            cp.wait_send()
        mine.wait()

    return _Comm([part], [jax.ShapeDtypeStruct(part.shape, BF)],
                 [pltpu.SemaphoreType.DMA((3,)), pltpu.SemaphoreType.DMA((3,)), pltpu.SemaphoreType.DMA],
                 start, finish)


def _ag_small(x, name):
    def body(x_ref, out_ref, send_sems, recv_sems, local_sem):
        me = _position()
        mine = pltpu.make_async_copy(x_ref, out_ref.at[_lin(me)], local_sem)
        mine.start()
        cps = [pltpu.make_async_remote_copy(
            src_ref=x_ref, dst_ref=out_ref.at[_lin(me)], send_sem=send_sems.at[j - 1], recv_sem=recv_sems.at[j - 1],
            device_id=_peer(me, j), device_id_type=MESH) for j in range(1, NDEV)]
        for cp in cps:
            cp.start()
        for cp in cps:
            cp.wait_recv()
        for cp in cps:
            cp.wait_send()
        mine.wait()

    return pl.pallas_call(
        body, name=name,
        out_shape=jax.ShapeDtypeStruct((NDEV,) + x.shape, x.dtype),
        in_specs=[ANY], out_specs=ANY,
        scratch_shapes=[pltpu.SemaphoreType.DMA((7,)), pltpu.SemaphoreType.DMA((7,)), pltpu.SemaphoreType.DMA],
    )(x)


def _load_ffn_weights(srcs, offs, scratch, sem):
    @pl.when(pl.program_id(0) == 0)
    def _():
        cps = [pltpu.make_async_copy(s.at[pl.ds(off, F), :], dst, sem.at[i])
               for i, (s, off, dst) in enumerate(zip(srcs, offs, scratch))]
        for cp in cps:
            cp.start()
        for cp in cps:
            cp.wait()


def _ffn_fwd(x, g, wbufs, offs, name, comm=None):
    nf = F // FC

    def body(x_ref, g_ref, b0, b1, b2, h_ref, n_ref, gg_ref, uu_ref, wg_s, wu_s, wd_s, sem):
        _load_ffn_weights((b0, b1, b2), offs, (wg_s, wu_s, wd_s), sem)
        xf = x_ref[...]
        r = lax.rsqrt(jnp.mean(xf * xf, axis=-1, keepdims=True) + EPS)
        nb = (xf * r * g_ref[...]).astype(BF)
        n_ref[...] = nb
        acc = jnp.zeros((TM, D), F32)
        for c in range(nf):
            sl = slice(c * FC, (c + 1) * FC)
            gc = _nt(nb, wg_s[sl, :])
            uc = _nt(nb, wu_s[sl, :])
            gg_ref[:, sl] = gc.astype(BF)
            uu_ref[:, sl] = uc.astype(BF)
            a = (0.5 * gc * _sig(gc)) * uc
            acc = acc + _nn(a.astype(BF), wd_s[sl, :])
        h_ref[...] = xf + acc

    row = lambda i: (i, 0)
    return _call(
        body, name=name, grid=(T // TM,), args=[x, g, *wbufs], comm=comm,
        in_specs=[pl.BlockSpec((TM, D), row), pl.BlockSpec((1, D), lambda i: (0, 0)), ANY, ANY, ANY],
        out_shape=[jax.ShapeDtypeStruct((T, D), F32), jax.ShapeDtypeStruct((T, D), BF),
                   jax.ShapeDtypeStruct((T, F), BF), jax.ShapeDtypeStruct((T, F), BF)],
        out_specs=[pl.BlockSpec((TM, D), row), pl.BlockSpec((TM, D), row),
                   pl.BlockSpec((TM, F), row), pl.BlockSpec((TM, F), row)],
        scratch_shapes=[pltpu.VMEM((F, D), BF)] * 3 + [pltpu.SemaphoreType.DMA((3,))])


def _mix_in(h1, gm, win, comm=None):
    def body(h_ref, g_ref, w_ref, u_ref, z_ref, u_s):
        @pl.when(pl.program_id(1) == 0)
        def _():
            xf = h_ref[...]
            r = lax.rsqrt(jnp.mean(xf * xf, axis=-1, keepdims=True) + EPS)
            ub = (xf * r * g_ref[...]).astype(BF)
            u_s[...] = ub
            u_ref[...] = ub
        z_ref[...] = _nt(u_s[...], w_ref[...])

    return _call(
        body, name="mix_in", grid=(T // TMI, NG), args=[h1, gm, win], comm=comm,
        in_specs=[pl.BlockSpec((TMI, D), lambda i, j: (i, 0)), pl.BlockSpec((1, D), lambda i, j: (0, 0)),
                  pl.BlockSpec((D, D), lambda i, j: (j, 0))],
        out_shape=[jax.ShapeDtypeStruct((T, D), BF), jax.ShapeDtypeStruct((NG, T, D), F32)],
        out_specs=[pl.BlockSpec((TMI, D), lambda i, j: (i, 0)), pl.BlockSpec((None, TMI, D), lambda i, j: (j, i, 0))],
        scratch_shapes=[pltpu.VMEM((TMI, D), BF)])


def _shift_up(w, b):
    return w if b == 0 else pltpu.roll(w, w.shape[0] - b, 0)


def _fold8(p):
    red = p[0:8, :]
    for i in range(1, p.shape[0] // 8):
        red = red + p[8 * i:8 * i + 8, :]
    return red


def _conv_fwd(z, cw, bias, comm=None):
    nt = T // TB
    hb = TB // HALO

    def body(z_ref, zh_ref, cw_ref, b_ref, a1_ref, q_ref, apad, ppad):
        first = pl.program_id(1) == 0
        apad[0:HALO, :] = jnp.where(first, 0.0, zh_ref[0] * _sig(zh_ref[1]))
        apad[HALO:, :] = z_ref[0] * _sig(z_ref[1])
        ppad[0:HALO, :] = jnp.where(first, 0.0, zh_ref[3] * zh_ref[4])
        ppad[HALO:, :] = z_ref[3] * z_ref[4]
        bias_row = b_ref[...]

        def chunk(r, carry):
            base = pl.multiple_of(r * CH, CH)
            w = apad[pl.ds(base, CH + HALO), :]
            acc = jnp.broadcast_to(bias_row, (CH, LANE))
            for b in range(8):
                wb = _shift_up(w, b)
                for a in range(5):
                    s = 8 * a + b
                    if 2 <= s <= HALO:
                        acc = acc + cw_ref[pl.ds(s - 2, 1), :] * wb[8 * a:8 * a + CH, :]
            a1_ref[pl.ds(base, CH), :] = acc
            pw = ppad[pl.ds(base, CH + HALO), :]
            v = (cw_ref[pl.ds(32, 1), :] * _shift_up(pw, 6)[24:24 + CH, :]
                 + cw_ref[pl.ds(33, 1), :] * _shift_up(pw, 7)[24:24 + CH, :]
                 + cw_ref[pl.ds(34, 1), :] * pw[32:32 + CH, :])
            q_ref[pl.ds(base, CH), :] = (z_ref[2, pl.ds(base, CH), :] * v).astype(BF)
            return carry

        lax.fori_loop(0, TB // CH, chunk, 0)

    return _call(
        body, name="conv_fwd", grid=(D // LANE, nt), args=[z, z, cw, bias], comm=comm,
        in_specs=[pl.BlockSpec((5, TB, LANE), lambda c, t: (0, t, c)),
                  pl.BlockSpec((5, HALO, LANE), lambda c, t: (0, jnp.maximum(t * hb - 1, 0), c)),
                  pl.BlockSpec((None, 40, LANE), lambda c, t: (c, 0, 0)),
                  pl.BlockSpec((1, LANE), lambda c, t: (0, c))],
        out_shape=[jax.ShapeDtypeStruct((T, D), F32), jax.ShapeDtypeStruct((T, D), BF)],
        out_specs=[pl.BlockSpec((TB, LANE), lambda c, t: (t, c)), pl.BlockSpec((TB, LANE), lambda c, t: (t, c))],
        scratch_shapes=[pltpu.VMEM((TB + HALO, LANE), F32), pltpu.VMEM((TB + HALO, LANE), F32)])


def _layernorm_silu(a1, lng, lnb):
    mu = jnp.mean(a1, axis=-1, keepdims=True)
    xc = a1 - mu
    rs = lax.rsqrt(jnp.mean(xc * xc, axis=-1, keepdims=True) + EPS)
    xh = xc * rs
    a2 = xh * lng + lnb
    sg = _sig(a2)
    return xh, rs, a2, sg


def _square_specs(blocks):
    return [pl.BlockSpec((D, D), lambda i, b=b: (b, 0)) for b in blocks]


def _mix_out(a1, q, z, h1, lng, lnb, wsq, comm=None):
    def body(a1_ref, q_ref, ga_ref, gb_ref, h_ref, lng_ref, lnb_ref, wa_ref, wb_ref, wo_ref, h2_ref, ya_ref, yb_ref):
        _, _, a2, sg = _layernorm_silu(a1_ref[...], lng_ref[...], lnb_ref[...])
        ya = _nn((a2 * sg).astype(BF), wa_ref[...])
        yb = _nn(q_ref[...], wb_ref[...])
        ya_ref[...] = ya
        yb_ref[...] = yb
        m = _sig(ga_ref[...]) * ya + _sig(gb_ref[...]) * yb
        h2_ref[...] = h_ref[...] + _nn(m.astype(BF), wo_ref[...])

    row = lambda i: (i, 0)
    vec = pl.BlockSpec((1, D), lambda i: (0, 0))
    return _call(
        body, name="mix_out", grid=(T // TM,), args=[a1, q, z, z, h1, lng, lnb, wsq, wsq, wsq], comm=comm,
        in_specs=[pl.BlockSpec((TM, D), row), pl.BlockSpec((TM, D), row),
                  pl.BlockSpec((None, TM, D), lambda i: (5, i, 0)), pl.BlockSpec((None, TM, D), lambda i: (6, i, 0)),
                  pl.BlockSpec((TM, D), row), vec, vec] + _square_specs((0, 1, 2)),
        out_shape=[jax.ShapeDtypeStruct((T, D), F32)] * 3,
        out_specs=[pl.BlockSpec((TM, D), row)] * 3)


def _final_loss(h3, gf, tgt, comm=None):
    def body(h_ref, g_ref, t_ref, dh_ref, s_ref):
        @pl.when(pl.program_id(0) == 0)
        def _():
            s_ref[...] = jnp.zeros_like(s_ref)
        xf = h_ref[...]
        g = g_ref[...]
        r = lax.rsqrt(jnp.mean(xf * xf, axis=-1, keepdims=True) + EPS)
        xr = xf * r
        e = xr * g - t_ref[...]
        s_ref[1:2, :] += jnp.sum(e * e, axis=0, keepdims=True) * (0.5 / D)
        dy = e * (1.0 / D)
        s_ref[0:1, :] += jnp.sum(dy * xr, axis=0, keepdims=True)
        gdy = dy * g
        dh_ref[...] = r * gdy - xr * (r * jnp.mean(gdy * xr, axis=-1, keepdims=True))

    row = lambda i: (i, 0)
    return _call(
        body, name="final_loss", grid=(T // TM,), args=[h3, gf, tgt], comm=comm,
        in_specs=[pl.BlockSpec((TM, D), row), pl.BlockSpec((1, D), lambda i: (0, 0)), pl.BlockSpec((TM, D), row)],
        out_shape=[jax.ShapeDtypeStruct((T, D), F32), jax.ShapeDtypeStruct((8, D), F32)],
        out_specs=[pl.BlockSpec((TM, D), row), pl.BlockSpec((8, D), lambda i: (0, 0))])


def _rmsnorm_bwd(xf, g, dn):
    r = lax.rsqrt(jnp.mean(xf * xf, axis=-1, keepdims=True) + EPS)
    xr = xf * r
    gdn = dn * g
    dx = r * gdn - xr * (r * jnp.mean(gdn * xr, axis=-1, keepdims=True))
    return dx, jnp.sum(dn * xr, axis=0, keepdims=True)


def _ffn_bwd(dh, x, g, gg, uu, wbufs, offs, name, comm=None):
    nf = F // FC

    def body(dh_ref, x_ref, g_ref, gg_ref, uu_ref, b0, b1, b2, dx_ref, dgu_ref, a_ref, s_ref, wg_s, wu_s, wd_s, sem):
        _load_ffn_weights((b0, b1, b2), offs, (wg_s, wu_s, wd_s), sem)

        @pl.when(pl.program_id(0) == 0)
        def _():
            s_ref[...] = jnp.zeros_like(s_ref)

        dhf = dh_ref[...]
        dhb = dhf.astype(BF)
        dn = jnp.zeros((TMB, D), F32)
        for c in range(nf):
            sl = slice(c * FC, (c + 1) * FC)
            da = 0.5 * _nt(dhb, wd_s[sl, :])
            gc = gg_ref[:, sl].astype(F32)
            uc = uu_ref[:, sl].astype(F32)
            sg = _sig(gc)
            silu = gc * sg
            dgc = (da * uc * (sg * (1.0 + gc * (1.0 - sg)))).astype(BF)
            duc = (da * silu).astype(BF)
            dgu_ref[0, :, sl] = dgc
            dgu_ref[1, :, sl] = duc
            a_ref[0, :, sl] = (0.5 * silu * uc).astype(BF)
            dn = dn + _nn(dgc, wg_s[sl, :]) + _nn(duc, wu_s[sl, :])
        dxn, dg = _rmsnorm_bwd(x_ref[...], g_ref[...], dn)
        dx_ref[...] = dhf + dxn
        s_ref[0:1, :] += dg

    row = lambda i: (i, 0)
    return _call(
        body, name=name, grid=(T // TMB,), args=[dh, x, g, gg, uu, *wbufs], comm=comm,
        in_specs=[pl.BlockSpec((TMB, D), row), pl.BlockSpec((TMB, D), row), pl.BlockSpec((1, D), lambda i: (0, 0)),
                  pl.BlockSpec((TMB, F), row), pl.BlockSpec((TMB, F), row), ANY, ANY, ANY],
        out_shape=[jax.ShapeDtypeStruct((T, D), F32), jax.ShapeDtypeStruct((2, T, F), BF),
                   jax.ShapeDtypeStruct((1, T, F), BF), jax.ShapeDtypeStruct((8, D), F32)],
        out_specs=[pl.BlockSpec((TMB, D), row), pl.BlockSpec((2, TMB, F), lambda i: (0, i, 0)),
                   pl.BlockSpec((1, TMB, F), lambda i: (0, i, 0)), pl.BlockSpec((8, D), lambda i: (0, 0))],
        scratch_shapes=[pltpu.VMEM((F, D), BF)] * 3 + [pltpu.SemaphoreType.DMA((3,))])


def _tn_matmul(lhs, rhs, tr, name, comm=None):
    ng, _, cdim = lhs.shape
    nc, nk = cdim // tr, T // TK

    def body(l_ref, r_ref, o_ref, acc):
        k = pl.program_id(2)

        @pl.when(k == 0)
        def _():
            acc[...] = jnp.zeros_like(acc)

        acc[...] += _tn(l_ref[...], r_ref[...].astype(BF))

        @pl.when(k == nk - 1)
        def _():
            o_ref[...] = acc[...].astype(BF)

    return _call(
        body, name=name, grid=(ng, nc, nk), args=[lhs, rhs], comm=comm,
        in_specs=[pl.BlockSpec((None, TK, tr), lambda g, c, k: (g, k, c)),
                  pl.BlockSpec((TK, D), lambda g, c, k: (k, 0))],
        out_shape=[jax.ShapeDtypeStruct((ng * cdim, D), BF)],
        out_specs=[pl.BlockSpec((tr, D), lambda g, c, k: (g * nc + c, 0))],
        scratch_shapes=[pltpu.VMEM((tr, D), F32)])


def _mix_out_bwd(dh2, ya, yb, z, a1, lng, lnb, wsq, comm=None):
    def body(dh_ref, ya_ref, yb_ref, ga_ref, gb_ref, a1_ref, lng_ref, lnb_ref, wa_ref, wb_ref, wo_ref,
             dzg_ref, da1_ref, dq_ref, m_ref, a3_ref, dya_ref, dyb_ref, s_ref):
        @pl.when(pl.program_id(0) == 0)
        def _():
            s_ref[...] = jnp.zeros_like(s_ref)

        dm = _nt(dh_ref[...].astype(BF), wo_ref[...])
        ya, yb = ya_ref[...], yb_ref[...]
        sa, sb = _sig(ga_ref[...]), _sig(gb_ref[...])
        m_ref[0] = (sa * ya + sb * yb).astype(BF)
        dzg_ref[0] = (dm * ya * (sa * (1.0 - sa))).astype(BF)
        dzg_ref[1] = (dm * yb * (sb * (1.0 - sb))).astype(BF)
        dya = (dm * sa).astype(BF)
        dyb = (dm * sb).astype(BF)
        dya_ref[...] = dya
        dyb_ref[...] = dyb
        dq_ref[...] = _nt(dyb, wb_ref[...])
        da3 = _nt(dya, wa_ref[...])
        lng = lng_ref[...]
        xh, rs, a2, sg = _layernorm_silu(a1_ref[...], lng, lnb_ref[...])
        a3_ref[0] = (a2 * sg).astype(BF)
        da2 = da3 * (sg * (1.0 + a2 * (1.0 - sg)))
        s_ref[0:1, :] += jnp.sum(da2 * xh, axis=0, keepdims=True)
        s_ref[1:2, :] += jnp.sum(da2, axis=0, keepdims=True)
        dxh = da2 * lng
        da1 = rs * (dxh - jnp.mean(dxh, axis=-1, keepdims=True) - xh * jnp.mean(dxh * xh, axis=-1, keepdims=True))
        da1_ref[...] = da1
        s_ref[2:3, :] += jnp.sum(da1, axis=0, keepdims=True)

    row = lambda i: (i, 0)
    row3 = lambda i: (0, i, 0)
    vec = pl.BlockSpec((1, D), lambda i: (0, 0))
    return _call(
        body, name="mix_out_bwd", grid=(T // TM,), args=[dh2, ya, yb, z, z, a1, lng, lnb, wsq, wsq, wsq], comm=comm,
        in_specs=[pl.BlockSpec((TM, D), row), pl.BlockSpec((TM, D), row), pl.BlockSpec((TM, D), row),
                  pl.BlockSpec((None, TM, D), lambda i: (5, i, 0)), pl.BlockSpec((None, TM, D), lambda i: (6, i, 0)),
                  pl.BlockSpec((TM, D), row), vec, vec] + _square_specs((0, 1, 2)),
        out_shape=[jax.ShapeDtypeStruct((2, T, D), BF), jax.ShapeDtypeStruct((T, D), F32),
                   jax.ShapeDtypeStruct((T, D), F32), jax.ShapeDtypeStruct((1, T, D), BF),
                   jax.ShapeDtypeStruct((1, T, D), BF), jax.ShapeDtypeStruct((T, D), BF),
                   jax.ShapeDtypeStruct((T, D), BF), jax.ShapeDtypeStruct((8, D), F32)],
        out_specs=[pl.BlockSpec((2, TM, D), row3), pl.BlockSpec((TM, D), row), pl.BlockSpec((TM, D), row),
                   pl.BlockSpec((1, TM, D), row3), pl.BlockSpec((1, TM, D), row3), pl.BlockSpec((TM, D), row),
                   pl.BlockSpec((TM, D), row), pl.BlockSpec((8, D), lambda i: (0, 0))])


def _conv_bwd(z, da1, dq, dzg, cw, comm=None):
    nt = T // TB
    hb = TB // HALO
    last_h = T // HALO - 1

    def body(z_ref, zp_ref, zn_ref, da1_ref, da1n_ref, dq_ref, dqn_ref, dzg_ref, cw_ref,
             dz_ref, dwa_ref, dwb_ref, apad, dypad, ppad, dvpad, acc_a, acc_b):
        t = pl.program_id(1)
        first, last = t == 0, t == nt - 1
        apad[0:HALO, :] = jnp.where(first, 0.0, zp_ref[0] * _sig(zp_ref[1]))
        apad[HALO:, :] = z_ref[0] * _sig(z_ref[1])
        ppad[0:HALO, :] = jnp.where(first, 0.0, zp_ref[3] * zp_ref[4])
        ppad[HALO:, :] = z_ref[3] * z_ref[4]
        dypad[0:TB, :] = da1_ref[...]
        dypad[TB:, :] = jnp.where(last, 0.0, da1n_ref[...])
        dvpad[0:TB, :] = dq_ref[...] * z_ref[2]
        dvpad[TB:, :] = jnp.where(last, 0.0, dqn_ref[...] * zn_ref[2])

        @pl.when(t == 0)
        def _():
            acc_a[...] = jnp.zeros_like(acc_a)
            acc_b[...] = jnp.zeros_like(acc_b)

        def chunk(r, carry):
            base = pl.multiple_of(r * CH, CH)
            rows = pl.ds(base, CH)
            dw_ = dypad[pl.ds(base, CH + HALO), :]
            da0 = jnp.zeros((CH, LANE), F32)
            for b in range(8):
                wb = _shift_up(dw_, b)
                for a in range(4):
                    o = 8 * a + b
                    if o <= KA - 1:
                        da0 = da0 + cw_ref[pl.ds(KA - 1 - o, 1), :] * wb[8 * a:8 * a + CH, :]
            z0, z1 = z_ref[0, rows, :], z_ref[1, rows, :]
            s1 = _sig(z1)
            dz_ref[0, rows, :] = (da0 * s1).astype(BF)
            dz_ref[1, rows, :] = (da0 * z0 * (s1 * (1.0 - s1))).astype(BF)
            dyc = dypad[rows, :]
            aw = apad[pl.ds(base, CH + HALO), :]
            for b in range(8):
                wb = _shift_up(aw, b)
                for a in range(5):
                    s = 8 * a + b
                    if 2 <= s <= HALO:
                        k8 = 8 * (s - 2)
                        acc_a[k8:k8 + 8, :] += _fold8(dyc * wb[8 * a:8 * a + CH, :])
            pw = ppad[pl.ds(base, CH + HALO), :]
            p6 = _shift_up(pw, 6)[24:24 + CH, :]
            p7 = _shift_up(pw, 7)[24:24 + CH, :]
            p8 = pw[32:32 + CH, :]
            wb0, wb1, wb2 = cw_ref[pl.ds(32, 1), :], cw_ref[pl.ds(33, 1), :], cw_ref[pl.ds(34, 1), :]
            v = wb0 * p6 + wb1 * p7 + wb2 * p8
            dz_ref[2, rows, :] = (dq_ref[rows, :] * v).astype(BF)
            dvw = dvpad[pl.ds(base, CH + HALO), :]
            dvc = dvw[0:CH, :]
            dp = wb2 * dvc + wb1 * _shift_up(dvw, 1)[0:CH, :] + wb0 * _shift_up(dvw, 2)[0:CH, :]
            dz_ref[3, rows, :] = (dp * z_ref[4, rows, :]).astype(BF)
            dz_ref[4, rows, :] = (dp * z_ref[3, rows, :]).astype(BF)
            acc_b[0:8, :] += _fold8(dvc * p6)
            acc_b[8:16, :] += _fold8(dvc * p7)
            acc_b[16:24, :] += _fold8(dvc * p8)
            dz_ref[5, rows, :] = dzg_ref[0, rows, :]
            dz_ref[6, rows, :] = dzg_ref[1, rows, :]
            return carry

        lax.fori_loop(0, TB // CH, chunk, 0)

        @pl.when(t == nt - 1)
        def _():
            for k in range(KA):
                dwa_ref[k:k + 1, :] = jnp.sum(acc_a[8 * k:8 * k + 8, :], axis=0, keepdims=True)
            dwa_ref[KA:32, :] = jnp.zeros((32 - KA, LANE), F32)
            for k in range(KB):
                dwb_ref[k:k + 1, :] = jnp.sum(acc_b[8 * k:8 * k + 8, :], axis=0, keepdims=True)
            dwb_ref[KB:8, :] = jnp.zeros((8 - KB, LANE), F32)

    blk = lambda c, t: (t, c)
    nxt = lambda c, t: (jnp.minimum((t + 1) * hb, last_h), c)
    return _call(
        body, name="conv_bwd", grid=(D // LANE, nt), args=[z, z, z, da1, da1, dq, dq, dzg, cw], comm=comm,
        in_specs=[pl.BlockSpec((5, TB, LANE), lambda c, t: (0, t, c)),
                  pl.BlockSpec((5, HALO, LANE), lambda c, t: (0, jnp.maximum(t * hb - 1, 0), c)),
                  pl.BlockSpec((5, HALO, LANE), lambda c, t: (0, jnp.minimum((t + 1) * hb, last_h), c)),
                  pl.BlockSpec((TB, LANE), blk), pl.BlockSpec((HALO, LANE), nxt),
                  pl.BlockSpec((TB, LANE), blk), pl.BlockSpec((HALO, LANE), nxt),
                  pl.BlockSpec((2, TB, LANE), lambda c, t: (0, t, c)),
                  pl.BlockSpec((None, 40, LANE), lambda c, t: (c, 0, 0))],
        out_shape=[jax.ShapeDtypeStruct((NG, T, D), BF), jax.ShapeDtypeStruct((32, D), F32),
                   jax.ShapeDtypeStruct((8, D), F32)],
        out_specs=[pl.BlockSpec((NG, TB, LANE), lambda c, t: (0, t, c)),
                   pl.BlockSpec((32, LANE), lambda c, t: (0, c)), pl.BlockSpec((8, LANE), lambda c, t: (0, c))],
        scratch_shapes=[pltpu.VMEM((TB + HALO, LANE), F32)] * 4
                       + [pltpu.VMEM((8 * 32, LANE), F32), pltpu.VMEM((24, LANE), F32)])


def _mix_in_bwd(dz, dh2, h1, gm, win, comm=None):
    def body(dz_ref, w_ref, dh_ref, h_ref, g_ref, o_ref, s_ref, acc):
        i, j = pl.program_id(0), pl.program_id(1)

        @pl.when((i == 0) & (j == 0))
        def _():
            s_ref[...] = jnp.zeros_like(s_ref)

        @pl.when(j == 0)
        def _():
            acc[...] = jnp.zeros_like(acc)

        acc[...] += _nn(dz_ref[...], w_ref[...])

        @pl.when(j == NG - 1)
        def _():
            dx, dg = _rmsnorm_bwd(h_ref[...], g_ref[...], acc[...])
            o_ref[...] = dh_ref[...] + dx
            s_ref[0:1, :] += dg

    row = lambda i, j: (i, 0)
    return _call(
        body, name="mix_in_bwd", grid=(T // TMI, NG), args=[dz, win, dh2, h1, gm], comm=comm,
        in_specs=[pl.BlockSpec((None, TMI, D), lambda i, j: (j, i, 0)),
                  pl.BlockSpec((D, D), lambda i, j: (j, 0)),
                  pl.BlockSpec((TMI, D), row), pl.BlockSpec((TMI, D), row), pl.BlockSpec((1, D), lambda i, j: (0, 0))],
        out_shape=[jax.ShapeDtypeStruct((T, D), F32), jax.ShapeDtypeStruct((8, D), F32)],
        out_specs=[pl.BlockSpec((TMI, D), row), pl.BlockSpec((8, D), lambda i, j: (0, 0))],
        scratch_shapes=[pltpu.VMEM((TMI, D), F32)])


def _row_tile(n, want, mult):
    for t in range(min(want, n), 0, -1):
        if n % t == 0 and t % mult == 0:
            return t
    return n


def _sum_slots(recv, name):
    ns, rows, cols = recv.shape
    tr = _row_tile(rows, 1024, 16)

    def body(r_ref, o_ref):
        s = r_ref[0].astype(F32)
        for k in range(1, ns):
            s = s + r_ref[k].astype(F32)
        o_ref[...] = s

    return _call(
        body, name=name, grid=(rows // tr,), args=[recv],
        in_specs=[pl.BlockSpec((ns, tr, cols), lambda i: (0, i, 0))],
        out_shape=[jax.ShapeDtypeStruct((rows, cols), F32)],
        out_specs=[pl.BlockSpec((tr, cols), lambda i: (i, 0))])[0]


def _sum_small(parts, loss_row):
    _, rows, cols = parts.shape

    def body(p_ref, o_ref, l_ref):
        s = p_ref[0]
        for k in range(1, NDEV):
            s = s + p_ref[k]
        o_ref[...] = s
        l_ref[...] = jnp.broadcast_to(jnp.sum(s[loss_row:loss_row + 1, :], axis=-1, keepdims=True), (8, LANE))

    return pl.pallas_call(
        body, name="sum_small",
        out_shape=(jax.ShapeDtypeStruct((rows, cols), F32), jax.ShapeDtypeStruct((8, LANE), F32)),
    )(parts)


def _adam(gs, ws, ms, vs, name, comm=None):
    n = len(gs)
    rows, cols = ws[0].shape
    tr = _row_tile(rows, 256, 8)
    c1 = 1.0 - ADAM_B1 ** ADAM_STEP
    c2 = 1.0 - ADAM_B2 ** ADAM_STEP

    def body(*refs):
        for i in range(n):
            g, w, m, v = (refs[4 * i + k][...] for k in range(4))
            d_ref, m_ref, v_ref = refs[4 * n + 3 * i: 4 * n + 3 * i + 3]
            m2 = ADAM_B1 * m + (1.0 - ADAM_B1) * g
            v2 = ADAM_B2 * v + (1.0 - ADAM_B2) * (g * g)
            d_ref[...] = -ADAM_LR * ((m2 / c1) / (jnp.sqrt(v2 / c2) + ADAM_EPS) + ADAM_WD * w)
            m_ref[...] = m2
            v_ref[...] = v2

    spec = pl.BlockSpec((tr, cols), lambda i: (i, 0))
    args = []
    for i in range(n):
        args += [gs[i], ws[i], ms[i], vs[i]]
    outs = _call(body, name=name, grid=(rows // tr,), args=args, comm=comm, in_specs=[spec] * (4 * n),
                 out_shape=[jax.ShapeDtypeStruct((rows, cols), F32)] * (3 * n), out_specs=[spec] * (3 * n))
    return [tuple(outs[3 * i: 3 * i + 3]) for i in range(n)], outs[3 * n:]


def kernel(x, ffn1_norm, ffn1_w_gate, ffn1_w_up, ffn1_w_down, mix_norm, w_in, a_dw_w, a_dw_b, a_ln_g, a_ln_b, a_w_out, b_conv_w, b_w_out, w_o, ffn2_norm, ffn2_w_gate, ffn2_w_up, ffn2_w_down, final_norm, loss_target, m_ffn1_norm, m_ffn1_w_gate, m_ffn1_w_up, m_ffn1_w_down, m_mix_norm, m_w_in, m_a_dw_w, m_a_dw_b, m_a_ln_g, m_a_ln_b, m_a_w_out, m_b_conv_w, m_b_w_out, m_w_o, m_ffn2_norm, m_ffn2_w_gate, m_ffn2_w_up, m_ffn2_w_down, m_final_norm, v_ffn1_norm, v_ffn1_w_gate, v_ffn1_w_up, v_ffn1_w_down, v_mix_norm, v_w_in, v_a_dw_w, v_a_dw_b, v_a_ln_g, v_a_ln_b, v_a_w_out, v_b_conv_w, v_b_w_out, v_w_o, v_ffn2_norm, v_ffn2_w_gate, v_ffn2_w_up, v_ffn2_w_down, v_final_norm):
    names = ("ffn1_norm", "ffn1_w_gate", "ffn1_w_up", "ffn1_w_down", "mix_norm", "w_in", "a_dw_w", "a_dw_b",
             "a_ln_g", "a_ln_b", "a_w_out", "b_conv_w", "b_w_out", "w_o", "ffn2_norm", "ffn2_w_gate", "ffn2_w_up",
             "ffn2_w_down", "final_norm")
    w = dict(ffn1_norm=ffn1_norm, ffn1_w_gate=ffn1_w_gate, ffn1_w_up=ffn1_w_up, ffn1_w_down=ffn1_w_down,
             mix_norm=mix_norm, w_in=w_in, a_dw_w=a_dw_w, a_dw_b=a_dw_b, a_ln_g=a_ln_g, a_ln_b=a_ln_b,
             a_w_out=a_w_out, b_conv_w=b_conv_w, b_w_out=b_w_out, w_o=w_o, ffn2_norm=ffn2_norm,
             ffn2_w_gate=ffn2_w_gate, ffn2_w_up=ffn2_w_up, ffn2_w_down=ffn2_w_down, final_norm=final_norm)
    m = dict(ffn1_norm=m_ffn1_norm, ffn1_w_gate=m_ffn1_w_gate, ffn1_w_up=m_ffn1_w_up, ffn1_w_down=m_ffn1_w_down,
             mix_norm=m_mix_norm, w_in=m_w_in, a_dw_w=m_a_dw_w, a_dw_b=m_a_dw_b, a_ln_g=m_a_ln_g, a_ln_b=m_a_ln_b,
             a_w_out=m_a_w_out, b_conv_w=m_b_conv_w, b_w_out=m_b_w_out, w_o=m_w_o, ffn2_norm=m_ffn2_norm,
             ffn2_w_gate=m_ffn2_w_gate, ffn2_w_up=m_ffn2_w_up, ffn2_w_down=m_ffn2_w_down, final_norm=m_final_norm)
    v = dict(ffn1_norm=v_ffn1_norm, ffn1_w_gate=v_ffn1_w_gate, ffn1_w_up=v_ffn1_w_up, ffn1_w_down=v_ffn1_w_down,
             mix_norm=v_mix_norm, w_in=v_w_in, a_dw_w=v_a_dw_w, a_dw_b=v_a_dw_b, a_ln_g=v_a_ln_g, a_ln_b=v_a_ln_b,
             a_w_out=v_a_w_out, b_conv_w=v_b_conv_w, b_w_out=v_b_w_out, w_o=v_w_o, ffn2_norm=v_ffn2_norm,
             ffn2_w_gate=v_ffn2_w_gate, ffn2_w_up=v_ffn2_w_up, ffn2_w_down=v_ffn2_w_down, final_norm=v_final_norm)
    me = 4 * lax.axis_index("x") + 2 * lax.axis_index("y") + lax.axis_index("c")

    to_rows = dict(wg1=ffn1_w_gate[0].T, wu1=ffn1_w_up[0].T, wd1=ffn1_w_down[0], wg2=ffn2_w_gate[0].T,
                   wu2=ffn2_w_up[0].T, wd2=ffn2_w_down[0], win=w_in[0].T, wa=a_w_out[0], wb=b_w_out[0], wo=w_o[0])
    flat = jnp.concatenate([to_rows[n].astype(BF) for n in ORDER], axis=0)
    cw_shard = jnp.concatenate([a_dw_w[0], jnp.zeros((1, LANE), F32), b_conv_w[0], jnp.zeros((5, LANE), F32)], axis=0)

    x2, tgt = x[0], loss_target[0]
    st_a, st_b, st_c, st_d = ("wg1", "wu1", "wd1"), ("win",), ("wa", "wb", "wo", "wg2"), ("wu2", "wd2")

    (buf_a,) = _run_comm(_ag_comm(st_a, flat), "ag_ffn1")
    cw = _ag_small(cw_shard, "ag_conv_w")
    h1, n1, gg1, uu1, buf_b = _ffn_fwd(x2, ffn1_norm, (buf_a,) * 3, (0, F, 2 * F), "ffn1_fwd", _ag_comm(st_b, flat))
    u, z, buf_c = _mix_in(h1, mix_norm, buf_b, _ag_comm(st_c, flat))
    a1, q, buf_d = _conv_fwd(z, cw, a_dw_b, _ag_comm(st_d, flat))
    h2, ya, yb = _mix_out(a1, q, z, h1, a_ln_g, a_ln_b, buf_c)
    ffn2_bufs, ffn2_offs = (buf_c, buf_d, buf_d), (3 * D, 0, F)
    h3, n2, gg2, uu2 = _ffn_fwd(h2, ffn2_norm, ffn2_bufs, ffn2_offs, "ffn2_fwd")
    dh3, s_final = _final_loss(h3, final_norm.reshape(1, D), tgt)

    tr_f = F // 2 if (F // 2) % LANE == 0 else F
    dh2, dgu2, act2, s_ffn2 = _ffn_bwd(dh3, h2, ffn2_norm, gg2, uu2, ffn2_bufs, ffn2_offs, "ffn2_bwd")
    (gu2,) = _tn_matmul(dgu2, n2, tr_f, "dw_gu2")
    (gd2,) = _tn_matmul(act2, dh3, tr_f, "dw_d2")
    rs2 = ("wg2", "wu2", "wd2")
    src2 = dict(wg2=(gu2, 0), wu2=(gu2, F), wd2=(gd2, 0))
    dzg, da1, dq, mb, a3b, dya, dyb, s_mix, pair2 = _mix_out_bwd(dh2, ya, yb, z, a1, a_ln_g, a_ln_b, buf_c,
                                                                  _rs_pair_comm(rs2, src2))
    part2 = _pair_add(rs2, src2, pair2, "pair_add2")
    (go,) = _tn_matmul(mb, dh2, D, "dw_o")
    (ga,) = _tn_matmul(a3b, dya, D, "dw_a")
    (gb,) = _tn_matmul(q.reshape(1, T, D), dyb, D, "dw_b")
    dz, dwa, dwb, recv2 = _conv_bwd(z, da1, dq, dzg, cw, _rs_chip_comm(part2))
    (gin,) = _tn_matmul(dz, u, D, "dw_in")
    rsm = ("win", "wa", "wb", "wo")
    srcm = dict(win=(gin, 0), wa=(ga, 0), wb=(gb, 0), wo=(go, 0))
    dh1, s_in, pairm = _mix_in_bwd(dz, dh2, h1, mix_norm, buf_b, _rs_pair_comm(rsm, srcm))
    partm = _pair_add(rsm, srcm, pairm, "pair_addm")
    dx, dgu1, act1, s_ffn1, recvm = _ffn_bwd(dh1, x2, ffn1_norm, gg1, uu1, (buf_a,) * 3, (0, F, 2 * F), "ffn1_bwd",
                                              _rs_chip_comm(partm))
    (gu1,) = _tn_matmul(dgu1, n1, tr_f, "dw_gu1")
    (gd1,) = _tn_matmul(act1, dh1, tr_f, "dw_d1")
    rs1 = ("wg1", "wu1", "wd1")
    src1 = dict(wg1=(gu1, 0), wu1=(gu1, F), wd1=(gd1, 0))
    (pair1,) = _run_comm(_rs_pair_comm(rs1, src1), "rs_pair1")
    part1 = _pair_add(rs1, src1, pair1, "pair_add1")
    (recv1,) = _run_comm(_rs_chip_comm(part1), "rs_chip1")

    gsum = {}
    for stage, recv, nm in ((rs2, recv2, "sum2"), (rsm, recvm, "summ"), (rs1, recv1, "sum1")):
        st, total = _Stage(stage), _sum_slots(recv, nm)
        for n in stage:
            gsum[n] = total[st.off[n]:st.off[n] + st.rows[n]]

    small = jnp.concatenate([s_ffn1, s_in, s_mix, s_ffn2, s_final, dwa, dwb], axis=0)
    small_sum, loss_blk = _sum_small(_ag_small(small, "ag_small"), 33)
    loss = loss_blk[0, 0]

    g = dict(ffn1_w_gate=gsum["wg1"].T, ffn1_w_up=gsum["wu1"].T, ffn1_w_down=gsum["wd1"],
             ffn2_w_gate=gsum["wg2"].T, ffn2_w_up=gsum["wu2"].T, ffn2_w_down=gsum["wd2"], w_in=gsum["win"].T,
             a_w_out=gsum["wa"], b_w_out=gsum["wb"], w_o=gsum["wo"],
             ffn1_norm=small_sum[0:1], mix_norm=small_sum[8:9], a_ln_g=small_sum[16:17], a_ln_b=small_sum[17:18],
             a_dw_b=small_sum[18:19], ffn2_norm=small_sum[24:25], final_norm=small_sum[32:33],
             a_dw_w=lax.dynamic_slice_in_dim(small_sum[40:40 + KA], me * LANE, LANE, axis=1),
             b_conv_w=lax.dynamic_slice_in_dim(small_sum[72:72 + KB], me * LANE, LANE, axis=1))

    upd = {}

    def run(group, name, as2d=lambda a: a[0]):
        res, _ = _adam([g[n] for n in group], [as2d(w[n]) for n in group], [as2d(m[n]) for n in group],
                       [as2d(v[n]) for n in group], name)
        for n, r in zip(group, res):
            upd[n] = tuple(a.reshape(w[n].shape) for a in r)

    run(("ffn1_w_gate", "ffn1_w_up", "ffn2_w_gate", "ffn2_w_up"), "adam_gate_up")
    run(("ffn1_w_down", "ffn2_w_down"), "adam_down")
    run(("w_in",), "adam_in")
    run(("a_w_out", "b_w_out", "w_o"), "adam_square")
    run(("a_dw_w",), "adam_dw")
    run(("b_conv_w",), "adam_conv")
    vecs = ("ffn1_norm", "mix_norm", "a_dw_b", "a_ln_g", "a_ln_b", "ffn2_norm", "final_norm")
    run(vecs, "adam_vec", as2d=lambda a: a.reshape(1, D))

    grads = [g[n].reshape(w[n].shape) for n in names]
    return (loss, dx.reshape(x.shape), *grads, *[upd[n][0] for n in names], *[upd[n][1] for n in names],
            *[upd[n][2] for n in names])
```

```python
import jax
import jax.numpy as jnp
from jax import lax
from jax.experimental import pallas as pl
from jax.experimental.pallas import tpu as pltpu

T = 4096
D = 1024
F = 2816
NG = 7
NDEV = 8
NCHIP = 4
KA, KB = 31, 3
EPS = 1e-6
ADAM_LR, ADAM_B1, ADAM_B2, ADAM_EPS, ADAM_WD, ADAM_STEP = 0.001, 0.9, 0.999, 1e-08, 0.01, 10

TM = 512
TMI = 1024
FC = 256
TB = 1024
CH = 128
HALO = 32
LANE = 128
TK = 1024
VMEM_LIMIT = 56 * 1024 * 1024

BF = jnp.bfloat16
F32 = jnp.float32
MESH = pl.DeviceIdType.MESH
ANY = pl.BlockSpec(memory_space=pl.ANY)

ORDER = ("wg1", "wu1", "wd1", "wg2", "wu2", "wd2", "win", "wa", "wb", "wo")


class _Layout:
    def __init__(self):
        fs, dis, ds = F // NDEV, NG * D // NDEV, D // NDEV
        self.rows = dict(wg1=fs, wu1=fs, wd1=fs, wg2=fs, wu2=fs, wd2=fs, win=dis, wa=ds, wb=ds, wo=ds)
        self.fl, off = {}, 0
        for n in ORDER:
            self.fl[n] = off
            off += self.rows[n]
        self.RT = off


class _Stage:
    def __init__(self, names):
        lay = _Layout()
        self.names, self.fl = names, lay.fl
        self.rows = {n: lay.rows[n] for n in names}
        self.off, self.wc, o, w = {}, {}, 0, 0
        for n in names:
            self.off[n], self.wc[n] = o, w
            o += self.rows[n]
            w += NDEV * self.rows[n]
        self.R, self.W = o, w


def _nt(a, b):
    return lax.dot_general(a, b, (((1,), (1,)), ((), ())), preferred_element_type=F32)


def _nn(a, b):
    return lax.dot_general(a, b, (((1,), (0,)), ((), ())), preferred_element_type=F32)


def _tn(a, b):
    return lax.dot_general(a, b, (((0,), (0,)), ((), ())), preferred_element_type=F32)


def _sig(x):
    return 1.0 / (1.0 + jnp.exp(-x))


def _position():
    return lax.axis_index("x"), lax.axis_index("y"), lax.axis_index("c")


def _peer(pos, j):
    x, y, c = pos
    return (1 - x if j & 4 else x, 1 - y if j & 2 else y, 1 - c if j & 1 else c)


def _lin(pos):
    return 4 * pos[0] + 2 * pos[1] + pos[2]


def _chip(pos):
    return 2 * pos[0] + pos[1]


class _Comm:
    def __init__(self, inputs, out_shapes, scratch, start, finish):
        self.inputs, self.out_shapes, self.scratch, self.start, self.finish = inputs, out_shapes, scratch, start, finish


def _call(body, *, name, grid, args, in_specs, out_shape, out_specs, scratch_shapes=(), comm=None,
          num_scalar_prefetch=0):
    in_specs, out_shape, out_specs, scratch_shapes = list(in_specs), list(out_shape), list(out_specs), list(scratch_shapes)
    n_in, n_out, n_scr = len(in_specs), len(out_shape), len(scratch_shapes)
    sp = num_scalar_prefetch
    if comm is None:
        kernel_fn = lambda *refs: body(*refs)
        c_in = c_out = c_scr = 0
    else:
        c_in, c_out, c_scr = len(comm.inputs), len(comm.out_shapes), len(comm.scratch)

        def kernel_fn(*refs):
            pre, refs = refs[:sp], refs[sp:]
            ins, cins = refs[:n_in], refs[n_in:n_in + c_in]
            o0 = n_in + c_in
            outs, couts = refs[o0:o0 + n_out], refs[o0 + n_out:o0 + n_out + c_out]
            s0 = o0 + n_out + c_out
            scr, cscr = refs[s0:s0 + n_scr], refs[s0 + n_scr:]
            first = pl.program_id(0) == 0
            last = pl.program_id(0) == grid[0] - 1
            for a in range(1, len(grid)):
                first = first & (pl.program_id(a) == 0)
                last = last & (pl.program_id(a) == grid[a] - 1)

            @pl.when(first)
            def _():
                comm.start(cins, couts, cscr)

            body(*pre, *ins, *outs, *scr)

            @pl.when(last)
            def _():
                comm.finish(cins, couts, cscr)

        args = list(args) + list(comm.inputs)
        in_specs += [ANY] * c_in
        out_shape += list(comm.out_shapes)
        out_specs += [ANY] * c_out
        scratch_shapes += list(comm.scratch)
    params = pltpu.CompilerParams(dimension_semantics=("arbitrary",) * len(grid), vmem_limit_bytes=VMEM_LIMIT)
    if sp:
        grid_spec = pltpu.PrefetchScalarGridSpec(num_scalar_prefetch=sp, grid=grid, in_specs=in_specs,
                                                 out_specs=out_specs, scratch_shapes=scratch_shapes)
        return pl.pallas_call(kernel_fn, name=name, grid_spec=grid_spec, out_shape=out_shape,
                              compiler_params=params)(*args)
    return pl.pallas_call(kernel_fn, name=name, grid=grid, in_specs=in_specs, out_shape=out_shape, out_specs=out_specs,
                          scratch_shapes=scratch_shapes, compiler_params=params)(*args)


def _join(a, b):
    na = (len(a.inputs), len(a.out_shapes), len(a.scratch))

    def split(refs):
        return ([r[:n] for r, n in zip(refs, na)], [r[n:] for r, n in zip(refs, na)])

    def start(*refs):
        ra, rb = split(refs)
        a.start(*ra)
        b.start(*rb)

    def finish(*refs):
        ra, rb = split(refs)
        a.finish(*ra)
        b.finish(*rb)

    return _Comm(list(a.inputs) + list(b.inputs), list(a.out_shapes) + list(b.out_shapes),
                 list(a.scratch) + list(b.scratch), start, finish)


def _run_comm(comm, name):
    def body(*refs):
        c_in, c_out = len(comm.inputs), len(comm.out_shapes)
        comm.start(refs[:c_in], refs[c_in:c_in + c_out], refs[c_in + c_out:])
        comm.finish(refs[:c_in], refs[c_in:c_in + c_out], refs[c_in + c_out:])

    return pl.pallas_call(
        body, name=name, out_shape=list(comm.out_shapes), in_specs=[ANY] * len(comm.inputs),
        out_specs=[ANY] * len(comm.out_shapes), scratch_shapes=list(comm.scratch))(*comm.inputs)


def _ag_comm(names, flat):
    st = _Stage(names)

    def parts(refs):
        (flat_ref,), (out_ref,), (send_sems, recv_sems, local_sem) = refs
        me = _position()

        def region(name, dev):
            r = st.rows[name]
            return out_ref.at[pl.ds(st.wc[name] + _lin(dev) * r, r), :]

        def own(name):
            return flat_ref.at[pl.ds(st.fl[name], st.rows[name]), :]

        def copies(k, dev, to, from_flat):
            return [pltpu.make_async_remote_copy(
                src_ref=own(n) if from_flat else region(n, dev), dst_ref=region(n, dev), send_sem=send_sems.at[k],
                recv_sem=recv_sems.at[k], device_id=to, device_id_type=MESH) for n in names]

        def whole(k):
            return pltpu.make_async_remote_copy(
                src_ref=flat_ref.at[pl.ds(0, st.R), :], dst_ref=out_ref.at[pl.ds(0, st.R), :],
                send_sem=send_sems.at[k], recv_sem=recv_sems.at[k], device_id=me, device_id_type=MESH)

        return me, region, own, copies, whole, flat_ref, out_ref, local_sem

    def start(*refs):
        me, region, own, copies, _, _, _, local_sem = parts(refs)
        for n in names:
            pltpu.make_async_copy(own(n), region(n, me), local_sem).start()
        for cp in copies(0, me, _peer(me, 1), True):
            cp.start()
        for j, bits in enumerate((4, 2, 6)):
            for cp in copies(1 + j, me, _peer(me, bits), True):
                cp.start()

    def finish(*refs):
        me, _, _, copies, whole, flat_ref, out_ref, local_sem = parts(refs)
        for j, bits in enumerate((4, 2, 6)):
            whole(1 + j).wait_recv()
            for cp in copies(4 + j, _peer(me, bits), _peer(me, 1), False):
                cp.start()
        whole(0).wait_recv()
        for j in range(3):
            whole(4 + j).wait_recv()
        for k in range(7):
            whole(k).wait_send()
        pltpu.make_async_copy(flat_ref.at[pl.ds(0, st.R), :], out_ref.at[pl.ds(0, st.R), :], local_sem).wait()

    return _Comm([flat], [jax.ShapeDtypeStruct((st.W, D), BF)],
                 [pltpu.SemaphoreType.DMA((7,)), pltpu.SemaphoreType.DMA((7,)), pltpu.SemaphoreType.DMA],
                 start, finish)


def _rs_pair_comm(names, src):
    st = _Stage(names)
    arrays = []
    for n in names:
        if not any(src[n][0] is a for a in arrays):
            arrays.append(src[n][0])
    idx = {n: [i for i, a in enumerate(arrays) if a is src[n][0]][0] for n in names}

    def slot_wait(refs):
        recv = refs[1][0]
        send_sem, recv_sem = refs[2]
        return pltpu.make_async_remote_copy(src_ref=recv, dst_ref=recv, send_sem=send_sem, recv_sem=recv_sem,
                                            device_id=_position(), device_id_type=MESH)

    def start(*refs):
        ins, (recv,), (send_sem, recv_sem) = refs
        me = _position()
        sib = _peer(me, 1)
        for q in range(NCHIP):
            dev = (q // 2, q % 2, sib[2])
            for n in names:
                r = st.rows[n]
                pltpu.make_async_remote_copy(
                    src_ref=ins[idx[n]].at[pl.ds(src[n][1] + _lin(dev) * r, r), :],
                    dst_ref=recv.at[q, pl.ds(st.off[n], r), :], send_sem=send_sem, recv_sem=recv_sem,
                    device_id=sib, device_id_type=MESH).start()

    def finish(*refs):
        w = slot_wait(refs)
        w.wait_recv()
        w.wait_send()

    return _Comm(arrays, [jax.ShapeDtypeStruct((NCHIP, st.R, D), BF)],
                 [pltpu.SemaphoreType.DMA, pltpu.SemaphoreType.DMA], start, finish)


def _pair_add(names, src, recv, name):
    st = _Stage(names)
    c_arr = jnp.reshape(lax.axis_index("c"), (1,)).astype(jnp.int32)

    def body(c_ref, *refs):
        r_ref, o_ref = refs[len(names)], refs[len(names) + 1]
        for a_ref, n in zip(refs, names):
            rows = slice(st.off[n], st.off[n] + st.rows[n])
            o_ref[rows, :] = (a_ref[...].astype(F32) + r_ref[rows, :].astype(F32)).astype(BF)

    def shard_spec(n):
        r, base = st.rows[n], src[n][1] // st.rows[n]
        return pl.BlockSpec((r, D), lambda q, c_ref: (base + 2 * q + c_ref[0], 0))

    slot = pl.BlockSpec((None, st.R, D), lambda q, c_ref: (q, 0, 0))
    return _call(body, name=name, grid=(NCHIP,), args=[c_arr] + [src[n][0] for n in names] + [recv],
                 in_specs=[shard_spec(n) for n in names] + [slot],
                 out_shape=[jax.ShapeDtypeStruct((NCHIP, st.R, D), BF)], out_specs=[slot], num_scalar_prefetch=1)[0]


def _rs_chip_comm(part):
    def copies(refs):
        (p_ref,), (recv,), (send_sems, recv_sems, local_sem) = refs
        me = _position()
        mine = pltpu.make_async_copy(p_ref.at[_chip(me)], recv.at[_chip(me)], local_sem)
        out = []
        for j, bits in enumerate((4, 2, 6)):
            to = _peer(me, bits)
            out.append(pltpu.make_async_remote_copy(
                src_ref=p_ref.at[_chip(to)], dst_ref=recv.at[_chip(me)], send_sem=send_sems.at[j],
                recv_sem=recv_sems.at[j], device_id=to, device_id_type=MESH))
        return mine, out

    def start(*refs):
        mine, out = copies(refs)
        mine.start()
        for cp in out:
            cp.start()

    def finish(*refs):
        mine, out = copies(refs)
        for cp in out:
            cp.wait_recv()
        for cp in out:
            cp.wait_send()
        mine.wait()

    return _Comm([part], [jax.ShapeDtypeStruct(part.shape, BF)],
                 [pltpu.SemaphoreType.DMA((3,)), pltpu.SemaphoreType.DMA((3,)), pltpu.SemaphoreType.DMA],
                 start, finish)


def _direct_comm(x, scatter):
    def copies(refs):
        (x_ref,), (out_ref,), (send_sems, recv_sems, local_sem) = refs
        me = _position()

        def piece(dev):
            return x_ref.at[_lin(dev)] if scatter else x_ref

        mine = pltpu.make_async_copy(piece(me), out_ref.at[_lin(me)], local_sem)
        return mine, [pltpu.make_async_remote_copy(
            src_ref=piece(_peer(me, j)), dst_ref=out_ref.at[_lin(me)], send_sem=send_sems.at[j - 1],
            recv_sem=recv_sems.at[j - 1], device_id=_peer(me, j), device_id_type=MESH) for j in range(1, NDEV)]

    def start(*refs):
        mine, cps = copies(refs)
        mine.start()
        for cp in cps:
            cp.start()

    def finish(*refs):
        mine, cps = copies(refs)
        for cp in cps:
            cp.wait_recv()
        for cp in cps:
            cp.wait_send()
        mine.wait()

    shape = x.shape if scatter else (NDEV,) + x.shape
    return _Comm([x], [jax.ShapeDtypeStruct(shape, x.dtype)],
                 [pltpu.SemaphoreType.DMA((7,)), pltpu.SemaphoreType.DMA((7,)), pltpu.SemaphoreType.DMA],
                 start, finish)


def _load_ffn_weights(srcs, offs, scratch, sem):
    @pl.when(pl.program_id(0) == 0)
    def _():
        cps = [pltpu.make_async_copy(s.at[pl.ds(off, F), :], dst, sem.at[i])
               for i, (s, off, dst) in enumerate(zip(srcs, offs, scratch))]
        for cp in cps:
            cp.start()
        for cp in cps:
            cp.wait()


def _ffn_fwd(x, g, wbufs, offs, name, comm=None):
    nf = F // FC

    def body(x_ref, g_ref, b0, b1, b2, h_ref, n_ref, gg_ref, uu_ref, wg_s, wu_s, wd_s, sem):
        _load_ffn_weights((b0, b1, b2), offs, (wg_s, wu_s, wd_s), sem)
        xf = x_ref[...]
        r = lax.rsqrt(jnp.mean(xf * xf, axis=-1, keepdims=True) + EPS)
        nb = (xf * r * g_ref[...]).astype(BF)
        n_ref[...] = nb
        acc = jnp.zeros((TM, D), F32)
        for c in range(nf):
            sl = slice(c * FC, (c + 1) * FC)
            gc = _nt(nb, wg_s[sl, :])
            uc = _nt(nb, wu_s[sl, :])
            gg_ref[:, sl] = gc.astype(BF)
            uu_ref[:, sl] = uc.astype(BF)
            a = (0.5 * gc * _sig(gc)) * uc
            acc = acc + _nn(a.astype(BF), wd_s[sl, :])
        h_ref[...] = xf + acc

    row = lambda i: (i, 0)
    return _call(
        body, name=name, grid=(T // TM,), args=[x, g, *wbufs], comm=comm,
        in_specs=[pl.BlockSpec((TM, D), row), pl.BlockSpec((1, D), lambda i: (0, 0)), ANY, ANY, ANY],
        out_shape=[jax.ShapeDtypeStruct((T, D), F32), jax.ShapeDtypeStruct((T, D), BF),
                   jax.ShapeDtypeStruct((T, F), BF), jax.ShapeDtypeStruct((T, F), BF)],
        out_specs=[pl.BlockSpec((TM, D), row), pl.BlockSpec((TM, D), row),
                   pl.BlockSpec((TM, F), row), pl.BlockSpec((TM, F), row)],
        scratch_shapes=[pltpu.VMEM((F, D), BF)] * 3 + [pltpu.SemaphoreType.DMA((3,))])


def _mix_in(h1, gm, win, comm=None):
    def body(h_ref, g_ref, w_ref, u_ref, z_ref, u_s):
        @pl.when(pl.program_id(1) == 0)
        def _():
            xf = h_ref[...]
            r = lax.rsqrt(jnp.mean(xf * xf, axis=-1, keepdims=True) + EPS)
            ub = (xf * r * g_ref[...]).astype(BF)
            u_s[...] = ub
            u_ref[...] = ub
        z_ref[...] = _nt(u_s[...], w_ref[...])

    return _call(
        body, name="mix_in", grid=(T // TMI, NG), args=[h1, gm, win], comm=comm,
        in_specs=[pl.BlockSpec((TMI, D), lambda i, j: (i, 0)), pl.BlockSpec((1, D), lambda i, j: (0, 0)),
                  pl.BlockSpec((D, D), lambda i, j: (j, 0))],
        out_shape=[jax.ShapeDtypeStruct((T, D), BF), jax.ShapeDtypeStruct((NG, T, D), F32)],
        out_specs=[pl.BlockSpec((TMI, D), lambda i, j: (i, 0)), pl.BlockSpec((None, TMI, D), lambda i, j: (j, i, 0))],
        scratch_shapes=[pltpu.VMEM((TMI, D), BF)])


def _shift_up(w, b):
    return w if b == 0 else pltpu.roll(w, w.shape[0] - b, 0)


def _fold8(p):
    red = p[0:8, :]
    for i in range(1, p.shape[0] // 8):
        red = red + p[8 * i:8 * i + 8, :]
    return red


def _conv_fwd(z, cw, bias, comm=None):
    nt = T // TB
    hb = TB // HALO

    def body(z_ref, zh_ref, cw_ref, b_ref, a1_ref, q_ref, apad, ppad):
        first = pl.program_id(1) == 0
        apad[0:HALO, :] = jnp.where(first, 0.0, zh_ref[0] * _sig(zh_ref[1]))
        apad[HALO:, :] = z_ref[0] * _sig(z_ref[1])
        ppad[0:HALO, :] = jnp.where(first, 0.0, zh_ref[3] * zh_ref[4])
        ppad[HALO:, :] = z_ref[3] * z_ref[4]
        bias_row = b_ref[...]

        def chunk(r, carry):
            base = pl.multiple_of(r * CH, CH)
            w = apad[pl.ds(base, CH + HALO), :]
            acc = jnp.broadcast_to(bias_row, (CH, LANE))
            for b in range(8):
                wb = _shift_up(w, b)
                for a in range(5):
                    s = 8 * a + b
                    if 2 <= s <= HALO:
                        acc = acc + cw_ref[pl.ds(s - 2, 1), :] * wb[8 * a:8 * a + CH, :]
            a1_ref[pl.ds(base, CH), :] = acc
            pw = ppad[pl.ds(base, CH + HALO), :]
            v = (cw_ref[pl.ds(32, 1), :] * _shift_up(pw, 6)[24:24 + CH, :]
                 + cw_ref[pl.ds(33, 1), :] * _shift_up(pw, 7)[24:24 + CH, :]
                 + cw_ref[pl.ds(34, 1), :] * pw[32:32 + CH, :])
            q_ref[pl.ds(base, CH), :] = (z_ref[2, pl.ds(base, CH), :] * v).astype(BF)
            return carry

        lax.fori_loop(0, TB // CH, chunk, 0)

    return _call(
        body, name="conv_fwd", grid=(D // LANE, nt), args=[z, z, cw, bias], comm=comm,
        in_specs=[pl.BlockSpec((5, TB, LANE), lambda c, t: (0, t, c)),
                  pl.BlockSpec((5, HALO, LANE), lambda c, t: (0, jnp.maximum(t * hb - 1, 0), c)),
                  pl.BlockSpec((None, 40, LANE), lambda c, t: (c, 0, 0)),
                  pl.BlockSpec((1, LANE), lambda c, t: (0, c))],
        out_shape=[jax.ShapeDtypeStruct((T, D), F32), jax.ShapeDtypeStruct((T, D), BF)],
        out_specs=[pl.BlockSpec((TB, LANE), lambda c, t: (t, c)), pl.BlockSpec((TB, LANE), lambda c, t: (t, c))],
        scratch_shapes=[pltpu.VMEM((TB + HALO, LANE), F32), pltpu.VMEM((TB + HALO, LANE), F32)])


def _layernorm_silu(a1, lng, lnb):
    mu = jnp.mean(a1, axis=-1, keepdims=True)
    xc = a1 - mu
    rs = lax.rsqrt(jnp.mean(xc * xc, axis=-1, keepdims=True) + EPS)
    xh = xc * rs
    a2 = xh * lng + lnb
    sg = _sig(a2)
    return xh, rs, a2, sg


def _square_specs(blocks):
    return [pl.BlockSpec((D, D), lambda i, b=b: (b, 0)) for b in blocks]


def _mix_out(a1, q, z, h1, lng, lnb, wsq, comm=None):
    def body(a1_ref, q_ref, ga_ref, gb_ref, h_ref, lng_ref, lnb_ref, wa_ref, wb_ref, wo_ref, h2_ref, ya_ref, yb_ref):
        _, _, a2, sg = _layernorm_silu(a1_ref[...], lng_ref[...], lnb_ref[...])
        ya = _nn((a2 * sg).astype(BF), wa_ref[...])
        yb = _nn(q_ref[...], wb_ref[...])
        ya_ref[...] = ya
        yb_ref[...] = yb
        m = _sig(ga_ref[...]) * ya + _sig(gb_ref[...]) * yb
        h2_ref[...] = h_ref[...] + _nn(m.astype(BF), wo_ref[...])

    row = lambda i: (i, 0)
    vec = pl.BlockSpec((1, D), lambda i: (0, 0))
    return _call(
        body, name="mix_out", grid=(T // TM,), args=[a1, q, z, z, h1, lng, lnb, wsq, wsq, wsq], comm=comm,
        in_specs=[pl.BlockSpec((TM, D), row), pl.BlockSpec((TM, D), row),
                  pl.BlockSpec((None, TM, D), lambda i: (5, i, 0)), pl.BlockSpec((None, TM, D), lambda i: (6, i, 0)),
                  pl.BlockSpec((TM, D), row), vec, vec] + _square_specs((0, 1, 2)),
        out_shape=[jax.ShapeDtypeStruct((T, D), F32)] * 3,
        out_specs=[pl.BlockSpec((TM, D), row)] * 3)


def _final_loss(h3, gf, tgt, comm=None):
    def body(h_ref, g_ref, t_ref, dh_ref, s_ref):
        @pl.when(pl.program_id(0) == 0)
        def _():
            s_ref[...] = jnp.zeros_like(s_ref)
        xf = h_ref[...]
        g = g_ref[...]
        r = lax.rsqrt(jnp.mean(xf * xf, axis=-1, keepdims=True) + EPS)
        xr = xf * r
        e = xr * g - t_ref[...]
        s_ref[1:2, :] += jnp.sum(e * e, axis=0, keepdims=True) * (0.5 / D)
        dy = e * (1.0 / D)
        s_ref[0:1, :] += jnp.sum(dy * xr, axis=0, keepdims=True)
        gdy = dy * g
        dh_ref[...] = r * gdy - xr * (r * jnp.mean(gdy * xr, axis=-1, keepdims=True))

    row = lambda i: (i, 0)
    return _call(
        body, name="final_loss", grid=(T // TM,), args=[h3, gf, tgt], comm=comm,
        in_specs=[pl.BlockSpec((TM, D), row), pl.BlockSpec((1, D), lambda i: (0, 0)), pl.BlockSpec((TM, D), row)],
        out_shape=[jax.ShapeDtypeStruct((T, D), F32), jax.ShapeDtypeStruct((8, D), F32)],
        out_specs=[pl.BlockSpec((TM, D), row), pl.BlockSpec((8, D), lambda i: (0, 0))])


def _rmsnorm_bwd(xf, g, dn):
    r = lax.rsqrt(jnp.mean(xf * xf, axis=-1, keepdims=True) + EPS)
    xr = xf * r
    gdn = dn * g
    dx = r * gdn - xr * (r * jnp.mean(gdn * xr, axis=-1, keepdims=True))
    return dx, jnp.sum(dn * xr, axis=0, keepdims=True)


def _ffn_bwd_hidden(dh, gg, uu, wbuf, off, name, comm=None):
    nf = F // FC

    def body(dh_ref, gg_ref, uu_ref, b0, dgu_ref, a_ref, wd_s, sem):
        _load_ffn_weights((b0,), (off,), (wd_s,), sem)
        dhb = dh_ref[...].astype(BF)
        for c in range(nf):
            sl = slice(c * FC, (c + 1) * FC)
            da = 0.5 * _nt(dhb, wd_s[sl, :])
            gc = gg_ref[:, sl].astype(F32)
            uc = uu_ref[:, sl].astype(F32)
            sg = _sig(gc)
            silu = gc * sg
            dgu_ref[0, :, sl] = (da * uc * (sg * (1.0 + gc * (1.0 - sg)))).astype(BF)
            dgu_ref[1, :, sl] = (da * silu).astype(BF)
            a_ref[0, :, sl] = (0.5 * silu * uc).astype(BF)

    row = lambda i: (i, 0)
    return _call(
        body, name=name, grid=(T // TM,), args=[dh, gg, uu, wbuf], comm=comm,
        in_specs=[pl.BlockSpec((TM, D), row), pl.BlockSpec((TM, F), row), pl.BlockSpec((TM, F), row), ANY],
        out_shape=[jax.ShapeDtypeStruct((2, T, F), BF), jax.ShapeDtypeStruct((1, T, F), BF)],
        out_specs=[pl.BlockSpec((2, TM, F), lambda i: (0, i, 0)), pl.BlockSpec((1, TM, F), lambda i: (0, i, 0))],
        scratch_shapes=[pltpu.VMEM((F, D), BF), pltpu.SemaphoreType.DMA((1,))])


def _ffn_bwd_input(dgu, dh, x, g, wbufs, offs, name, comm=None):
    nf = F // FC

    def body(dgu_ref, dh_ref, x_ref, g_ref, b0, b1, dx_ref, s_ref, wg_s, wu_s, sem):
        _load_ffn_weights((b0, b1), offs, (wg_s, wu_s), sem)

        @pl.when(pl.program_id(0) == 0)
        def _():
            s_ref[...] = jnp.zeros_like(s_ref)

        dn = jnp.zeros((TM, D), F32)
        for c in range(nf):
            sl = slice(c * FC, (c + 1) * FC)
            dn = dn + _nn(dgu_ref[0, :, sl], wg_s[sl, :]) + _nn(dgu_ref[1, :, sl], wu_s[sl, :])
        dxn, dg = _rmsnorm_bwd(x_ref[...], g_ref[...], dn)
        dx_ref[...] = dh_ref[...] + dxn
        s_ref[0:1, :] += dg

    row = lambda i: (i, 0)
    return _call(
        body, name=name, grid=(T // TM,), args=[dgu, dh, x, g, *wbufs], comm=comm,
        in_specs=[pl.BlockSpec((2, TM, F), lambda i: (0, i, 0)), pl.BlockSpec((TM, D), row),
                  pl.BlockSpec((TM, D), row), pl.BlockSpec((1, D), lambda i: (0, 0)), ANY, ANY],
        out_shape=[jax.ShapeDtypeStruct((T, D), F32), jax.ShapeDtypeStruct((8, D), F32)],
        out_specs=[pl.BlockSpec((TM, D), row), pl.BlockSpec((8, D), lambda i: (0, 0))],
        scratch_shapes=[pltpu.VMEM((F, D), BF)] * 2 + [pltpu.SemaphoreType.DMA((2,))])


def _tn_matmul(lhs, rhs, tr, name, comm=None):
    ng, _, cdim = lhs.shape
    nc, nk = cdim // tr, T // TK

    def body(l_ref, r_ref, o_ref, acc):
        k = pl.program_id(2)

        @pl.when(k == 0)
        def _():
            acc[...] = jnp.zeros_like(acc)

        acc[...] += _tn(l_ref[...], r_ref[...].astype(BF))

        @pl.when(k == nk - 1)
        def _():
            o_ref[...] = acc[...].astype(BF)

    return _call(
        body, name=name, grid=(ng, nc, nk), args=[lhs, rhs], comm=comm,
        in_specs=[pl.BlockSpec((None, TK, tr), lambda g, c, k: (g, k, c)),
                  pl.BlockSpec((TK, D), lambda g, c, k: (k, 0))],
        out_shape=[jax.ShapeDtypeStruct((ng * cdim, D), BF)],
        out_specs=[pl.BlockSpec((tr, D), lambda g, c, k: (g * nc + c, 0))],
        scratch_shapes=[pltpu.VMEM((tr, D), F32)])


def _mix_out_bwd(dh2, ya, yb, z, a1, lng, lnb, wsq, comm=None):
    def body(dh_ref, ya_ref, yb_ref, ga_ref, gb_ref, a1_ref, lng_ref, lnb_ref, wa_ref, wb_ref, wo_ref,
             dzg_ref, da1_ref, dq_ref, m_ref, a3_ref, dya_ref, dyb_ref, s_ref):
        @pl.when(pl.program_id(0) == 0)
        def _():
            s_ref[...] = jnp.zeros_like(s_ref)

        dm = _nt(dh_ref[...].astype(BF), wo_ref[...])
        ya, yb = ya_ref[...], yb_ref[...]
        sa, sb = _sig(ga_ref[...]), _sig(gb_ref[...])
        m_ref[0] = (sa * ya + sb * yb).astype(BF)
        dzg_ref[0] = (dm * ya * (sa * (1.0 - sa))).astype(BF)
        dzg_ref[1] = (dm * yb * (sb * (1.0 - sb))).astype(BF)
        dya = (dm * sa).astype(BF)
        dyb = (dm * sb).astype(BF)
        dya_ref[...] = dya
        dyb_ref[...] = dyb
        dq_ref[...] = _nt(dyb, wb_ref[...])
        da3 = _nt(dya, wa_ref[...])
        lng = lng_ref[...]
        xh, rs, a2, sg = _layernorm_silu(a1_ref[...], lng, lnb_ref[...])
        a3_ref[0] = (a2 * sg).astype(BF)
        da2 = da3 * (sg * (1.0 + a2 * (1.0 - sg)))
        s_ref[0:1, :] += jnp.sum(da2 * xh, axis=0, keepdims=True)
        s_ref[1:2, :] += jnp.sum(da2, axis=0, keepdims=True)
        dxh = da2 * lng
        da1 = rs * (dxh - jnp.mean(dxh, axis=-1, keepdims=True) - xh * jnp.mean(dxh * xh, axis=-1, keepdims=True))
        da1_ref[...] = da1
        s_ref[2:3, :] += jnp.sum(da1, axis=0, keepdims=True)

    row = lambda i: (i, 0)
    row3 = lambda i: (0, i, 0)
    vec = pl.BlockSpec((1, D), lambda i: (0, 0))
    return _call(
        body, name="mix_out_bwd", grid=(T // TM,), args=[dh2, ya, yb, z, z, a1, lng, lnb, wsq, wsq, wsq], comm=comm,
        in_specs=[pl.BlockSpec((TM, D), row), pl.BlockSpec((TM, D), row), pl.BlockSpec((TM, D), row),
                  pl.BlockSpec((None, TM, D), lambda i: (5, i, 0)), pl.BlockSpec((None, TM, D), lambda i: (6, i, 0)),
                  pl.BlockSpec((TM, D), row), vec, vec] + _square_specs((0, 1, 2)),
        out_shape=[jax.ShapeDtypeStruct((2, T, D), BF), jax.ShapeDtypeStruct((T, D), F32),
                   jax.ShapeDtypeStruct((T, D), F32), jax.ShapeDtypeStruct((1, T, D), BF),
                   jax.ShapeDtypeStruct((1, T, D), BF), jax.ShapeDtypeStruct((T, D), BF),
                   jax.ShapeDtypeStruct((T, D), BF), jax.ShapeDtypeStruct((8, D), F32)],
        out_specs=[pl.BlockSpec((2, TM, D), row3), pl.BlockSpec((TM, D), row), pl.BlockSpec((TM, D), row),
                   pl.BlockSpec((1, TM, D), row3), pl.BlockSpec((1, TM, D), row3), pl.BlockSpec((TM, D), row),
                   pl.BlockSpec((TM, D), row), pl.BlockSpec((8, D), lambda i: (0, 0))])


def _conv_bwd(z, da1, dq, dzg, cw, comm=None):
    nt = T // TB
    hb = TB // HALO
    last_h = T // HALO - 1

    def body(z_ref, zp_ref, zn_ref, da1_ref, da1n_ref, dq_ref, dqn_ref, dzg_ref, cw_ref,
             dz_ref, dwa_ref, dwb_ref, apad, dypad, ppad, dvpad, acc_a, acc_b):
        t = pl.program_id(1)
        first, last = t == 0, t == nt - 1
        apad[0:HALO, :] = jnp.where(first, 0.0, zp_ref[0] * _sig(zp_ref[1]))
        apad[HALO:, :] = z_ref[0] * _sig(z_ref[1])
        ppad[0:HALO, :] = jnp.where(first, 0.0, zp_ref[3] * zp_ref[4])
        ppad[HALO:, :] = z_ref[3] * z_ref[4]
        dypad[0:TB, :] = da1_ref[...]
        dypad[TB:, :] = jnp.where(last, 0.0, da1n_ref[...])
        dvpad[0:TB, :] = dq_ref[...] * z_ref[2]
        dvpad[TB:, :] = jnp.where(last, 0.0, dqn_ref[...] * zn_ref[2])

        @pl.when(t == 0)
        def _():
            acc_a[...] = jnp.zeros_like(acc_a)
            acc_b[...] = jnp.zeros_like(acc_b)

        def chunk(r, carry):
            base = pl.multiple_of(r * CH, CH)
            rows = pl.ds(base, CH)
            dw_ = dypad[pl.ds(base, CH + HALO), :]
            da0 = jnp.zeros((CH, LANE), F32)
            for b in range(8):
                wb = _shift_up(dw_, b)
                for a in range(4):
                    o = 8 * a + b
                    if o <= KA - 1:
                        da0 = da0 + cw_ref[pl.ds(KA - 1 - o, 1), :] * wb[8 * a:8 * a + CH, :]
            z0, z1 = z_ref[0, rows, :], z_ref[1, rows, :]
            s1 = _sig(z1)
            dz_ref[0, rows, :] = (da0 * s1).astype(BF)
            dz_ref[1, rows, :] = (da0 * z0 * (s1 * (1.0 - s1))).astype(BF)
            dyc = dypad[rows, :]
            aw = apad[pl.ds(base, CH + HALO), :]
            for b in range(8):
                wb = _shift_up(aw, b)
                for a in range(5):
                    s = 8 * a + b
                    if 2 <= s <= HALO:
                        k8 = 8 * (s - 2)
                        acc_a[k8:k8 + 8, :] += _fold8(dyc * wb[8 * a:8 * a + CH, :])
            pw = ppad[pl.ds(base, CH + HALO), :]
            p6 = _shift_up(pw, 6)[24:24 + CH, :]
            p7 = _shift_up(pw, 7)[24:24 + CH, :]
            p8 = pw[32:32 + CH, :]
            wb0, wb1, wb2 = cw_ref[pl.ds(32, 1), :], cw_ref[pl.ds(33, 1), :], cw_ref[pl.ds(34, 1), :]
            v = wb0 * p6 + wb1 * p7 + wb2 * p8
            dz_ref[2, rows, :] = (dq_ref[rows, :] * v).astype(BF)
            dvw = dvpad[pl.ds(base, CH + HALO), :]
            dvc = dvw[0:CH, :]
            dp = wb2 * dvc + wb1 * _shift_up(dvw, 1)[0:CH, :] + wb0 * _shift_up(dvw, 2)[0:CH, :]
            dz_ref[3, rows, :] = (dp * z_ref[4, rows, :]).astype(BF)
            dz_ref[4, rows, :] = (dp * z_ref[3, rows, :]).astype(BF)
            acc_b[0:8, :] += _fold8(dvc * p6)
            acc_b[8:16, :] += _fold8(dvc * p7)
            acc_b[16:24, :] += _fold8(dvc * p8)
            dz_ref[5, rows, :] = dzg_ref[0, rows, :]
            dz_ref[6, rows, :] = dzg_ref[1, rows, :]
            return carry

        lax.fori_loop(0, TB // CH, chunk, 0)

        @pl.when(t == nt - 1)
        def _():
            for k in range(KA):
                dwa_ref[k:k + 1, :] = jnp.sum(acc_a[8 * k:8 * k + 8, :], axis=0, keepdims=True)
            dwa_ref[KA:32, :] = jnp.zeros((32 - KA, LANE), F32)
            for k in range(KB):
                dwb_ref[k:k + 1, :] = jnp.sum(acc_b[8 * k:8 * k + 8, :], axis=0, keepdims=True)
            dwb_ref[KB:8, :] = jnp.zeros((8 - KB, LANE), F32)

    blk = lambda c, t: (t, c)
    nxt = lambda c, t: (jnp.minimum((t + 1) * hb, last_h), c)
    return _call(
        body, name="conv_bwd", grid=(D // LANE, nt), args=[z, z, z, da1, da1, dq, dq, dzg, cw], comm=comm,
        in_specs=[pl.BlockSpec((5, TB, LANE), lambda c, t: (0, t, c)),
                  pl.BlockSpec((5, HALO, LANE), lambda c, t: (0, jnp.maximum(t * hb - 1, 0), c)),
                  pl.BlockSpec((5, HALO, LANE), lambda c, t: (0, jnp.minimum((t + 1) * hb, last_h), c)),
                  pl.BlockSpec((TB, LANE), blk), pl.BlockSpec((HALO, LANE), nxt),
                  pl.BlockSpec((TB, LANE), blk), pl.BlockSpec((HALO, LANE), nxt),
                  pl.BlockSpec((2, TB, LANE), lambda c, t: (0, t, c)),
                  pl.BlockSpec((None, 40, LANE), lambda c, t: (c, 0, 0))],
        out_shape=[jax.ShapeDtypeStruct((NG, T, D), BF), jax.ShapeDtypeStruct((32, D), F32),
                   jax.ShapeDtypeStruct((8, D), F32)],
        out_specs=[pl.BlockSpec((NG, TB, LANE), lambda c, t: (0, t, c)),
                   pl.BlockSpec((32, LANE), lambda c, t: (0, c)), pl.BlockSpec((8, LANE), lambda c, t: (0, c))],
        scratch_shapes=[pltpu.VMEM((TB + HALO, LANE), F32)] * 4
                       + [pltpu.VMEM((8 * 32, LANE), F32), pltpu.VMEM((24, LANE), F32)])


def _mix_in_bwd(dz, dh2, h1, gm, win, comm=None):
    def body(dz_ref, w_ref, dh_ref, h_ref, g_ref, o_ref, s_ref, acc):
        i, j = pl.program_id(0), pl.program_id(1)

        @pl.when((i == 0) & (j == 0))
        def _():
            s_ref[...] = jnp.zeros_like(s_ref)

        @pl.when(j == 0)
        def _():
            acc[...] = jnp.zeros_like(acc)

        acc[...] += _nn(dz_ref[...], w_ref[...])

        @pl.when(j == NG - 1)
        def _():
            dx, dg = _rmsnorm_bwd(h_ref[...], g_ref[...], acc[...])
            o_ref[...] = dh_ref[...] + dx
            s_ref[0:1, :] += dg

    row = lambda i, j: (i, 0)
    return _call(
        body, name="mix_in_bwd", grid=(T // TMI, NG), args=[dz, win, dh2, h1, gm], comm=comm,
        in_specs=[pl.BlockSpec((None, TMI, D), lambda i, j: (j, i, 0)),
                  pl.BlockSpec((D, D), lambda i, j: (j, 0)),
                  pl.BlockSpec((TMI, D), row), pl.BlockSpec((TMI, D), row), pl.BlockSpec((1, D), lambda i, j: (0, 0))],
        out_shape=[jax.ShapeDtypeStruct((T, D), F32), jax.ShapeDtypeStruct((8, D), F32)],
        out_specs=[pl.BlockSpec((TMI, D), row), pl.BlockSpec((8, D), lambda i, j: (0, 0))],
        scratch_shapes=[pltpu.VMEM((TMI, D), F32)])


def _row_tile(n, want, mult):
    for t in range(min(want, n), 0, -1):
        if n % t == 0 and t % mult == 0:
            return t
    return n


def _sum_slots(recv, name):
    ns, rows, cols = recv.shape
    tr = _row_tile(rows, 1024, 16)

    def body(r_ref, o_ref):
        s = r_ref[0].astype(F32)
        for k in range(1, ns):
            s = s + r_ref[k].astype(F32)
        o_ref[...] = s

    return _call(
        body, name=name, grid=(rows // tr,), args=[recv],
        in_specs=[pl.BlockSpec((ns, tr, cols), lambda i: (0, i, 0))],
        out_shape=[jax.ShapeDtypeStruct((rows, cols), F32)],
        out_specs=[pl.BlockSpec((tr, cols), lambda i: (i, 0))])[0]


def _pack_small(s_ffn1, s_in, s_mix, s_ffn2, s_final, dwa, dwb):
    def body(f1, mi, mo, f2, fl, wa_ref, wb_ref, v_ref, k_ref):
        for dst, (ref, row) in enumerate(((f1, 0), (mi, 0), (mo, 0), (mo, 1), (mo, 2), (f2, 0), (fl, 0), (fl, 1))):
            v_ref[dst:dst + 1, :] = ref[row:row + 1, :]
        for k in range(NDEV):
            k_ref[k, 0:32, :] = wa_ref[:, k * LANE:(k + 1) * LANE]
            k_ref[k, 32:40, :] = wb_ref[:, k * LANE:(k + 1) * LANE]

    return pl.pallas_call(
        body, name="pack_small",
        out_shape=(jax.ShapeDtypeStruct((8, D), F32), jax.ShapeDtypeStruct((NDEV, 40, LANE), F32)),
    )(s_ffn1, s_in, s_mix, s_ffn2, s_final, dwa, dwb)


def _sum_small(vecs, convs):
    def body(v_ref, k_ref, vs_ref, ks_ref, l_ref):
        s, c = v_ref[0], k_ref[0]
        for k in range(1, NDEV):
            s = s + v_ref[k]
            c = c + k_ref[k]
        vs_ref[...] = s
        ks_ref[...] = c
        l_ref[...] = jnp.broadcast_to(jnp.sum(s[7:8, :], axis=-1, keepdims=True), (8, LANE))

    return pl.pallas_call(
        body, name="sum_small",
        out_shape=(jax.ShapeDtypeStruct((8, D), F32), jax.ShapeDtypeStruct((40, LANE), F32),
                   jax.ShapeDtypeStruct((8, LANE), F32)),
    )(vecs, convs)


def _adam(gs, ws, ms, vs, name, comm=None):
    n = len(gs)
    rows, cols = ws[0].shape
    tr = _row_tile(rows, 256, 8)
    c1 = 1.0 - ADAM_B1 ** ADAM_STEP
    c2 = 1.0 - ADAM_B2 ** ADAM_STEP

    def body(*refs):
        for i in range(n):
            g, w, m, v = (refs[4 * i + k][...] for k in range(4))
            d_ref, m_ref, v_ref = refs[4 * n + 3 * i: 4 * n + 3 * i + 3]
            m2 = ADAM_B1 * m + (1.0 - ADAM_B1) * g
            v2 = ADAM_B2 * v + (1.0 - ADAM_B2) * (g * g)
            d_ref[...] = -ADAM_LR * ((m2 / c1) / (jnp.sqrt(v2 / c2) + ADAM_EPS) + ADAM_WD * w)
            m_ref[...] = m2
            v_ref[...] = v2

    spec = pl.BlockSpec((tr, cols), lambda i: (i, 0))
    args = []
    for i in range(n):
        args += [gs[i], ws[i], ms[i], vs[i]]
    outs = _call(body, name=name, grid=(rows // tr,), args=args, comm=comm, in_specs=[spec] * (4 * n),
                 out_shape=[jax.ShapeDtypeStruct((rows, cols), F32)] * (3 * n), out_specs=[spec] * (3 * n))
    return [tuple(outs[3 * i: 3 * i + 3]) for i in range(n)], outs[3 * n:]


def kernel(x, ffn1_norm, ffn1_w_gate, ffn1_w_up, ffn1_w_down, mix_norm, w_in, a_dw_w, a_dw_b, a_ln_g, a_ln_b, a_w_out, b_conv_w, b_w_out, w_o, ffn2_norm, ffn2_w_gate, ffn2_w_up, ffn2_w_down, final_norm, loss_target, m_ffn1_norm, m_ffn1_w_gate, m_ffn1_w_up, m_ffn1_w_down, m_mix_norm, m_w_in, m_a_dw_w, m_a_dw_b, m_a_ln_g, m_a_ln_b, m_a_w_out, m_b_conv_w, m_b_w_out, m_w_o, m_ffn2_norm, m_ffn2_w_gate, m_ffn2_w_up, m_ffn2_w_down, m_final_norm, v_ffn1_norm, v_ffn1_w_gate, v_ffn1_w_up, v_ffn1_w_down, v_mix_norm, v_w_in, v_a_dw_w, v_a_dw_b, v_a_ln_g, v_a_ln_b, v_a_w_out, v_b_conv_w, v_b_w_out, v_w_o, v_ffn2_norm, v_ffn2_w_gate, v_ffn2_w_up, v_ffn2_w_down, v_final_norm):
    names = ("ffn1_norm", "ffn1_w_gate", "ffn1_w_up", "ffn1_w_down", "mix_norm", "w_in", "a_dw_w", "a_dw_b",
             "a_ln_g", "a_ln_b", "a_w_out", "b_conv_w", "b_w_out", "w_o", "ffn2_norm", "ffn2_w_gate", "ffn2_w_up",
             "ffn2_w_down", "final_norm")
    w = dict(ffn1_norm=ffn1_norm, ffn1_w_gate=ffn1_w_gate, ffn1_w_up=ffn1_w_up, ffn1_w_down=ffn1_w_down,
             mix_norm=mix_norm, w_in=w_in, a_dw_w=a_dw_w, a_dw_b=a_dw_b, a_ln_g=a_ln_g, a_ln_b=a_ln_b,
             a_w_out=a_w_out, b_conv_w=b_conv_w, b_w_out=b_w_out, w_o=w_o, ffn2_norm=ffn2_norm,
             ffn2_w_gate=ffn2_w_gate, ffn2_w_up=ffn2_w_up, ffn2_w_down=ffn2_w_down, final_norm=final_norm)
    m = dict(ffn1_norm=m_ffn1_norm, ffn1_w_gate=m_ffn1_w_gate, ffn1_w_up=m_ffn1_w_up, ffn1_w_down=m_ffn1_w_down,
             mix_norm=m_mix_norm, w_in=m_w_in, a_dw_w=m_a_dw_w, a_dw_b=m_a_dw_b, a_ln_g=m_a_ln_g, a_ln_b=m_a_ln_b,
             a_w_out=m_a_w_out, b_conv_w=m_b_conv_w, b_w_out=m_b_w_out, w_o=m_w_o, ffn2_norm=m_ffn2_norm,
             ffn2_w_gate=m_ffn2_w_gate, ffn2_w_up=m_ffn2_w_up, ffn2_w_down=m_ffn2_w_down, final_norm=m_final_norm)
    v = dict(ffn1_norm=v_ffn1_norm, ffn1_w_gate=v_ffn1_w_gate, ffn1_w_up=v_ffn1_w_up, ffn1_w_down=v_ffn1_w_down,
             mix_norm=v_mix_norm, w_in=v_w_in, a_dw_w=v_a_dw_w, a_dw_b=v_a_dw_b, a_ln_g=v_a_ln_g, a_ln_b=v_a_ln_b,
             a_w_out=v_a_w_out, b_conv_w=v_b_conv_w, b_w_out=v_b_w_out, w_o=v_w_o, ffn2_norm=v_ffn2_norm,
             ffn2_w_gate=v_ffn2_w_gate, ffn2_w_up=v_ffn2_w_up, ffn2_w_down=v_ffn2_w_down, final_norm=v_final_norm)
    to_rows = dict(wg1=ffn1_w_gate[0].T, wu1=ffn1_w_up[0].T, wd1=ffn1_w_down[0], wg2=ffn2_w_gate[0].T,
                   wu2=ffn2_w_up[0].T, wd2=ffn2_w_down[0], win=w_in[0].T, wa=a_w_out[0], wb=b_w_out[0], wo=w_o[0])
    flat = jnp.concatenate([to_rows[n].astype(BF) for n in ORDER], axis=0)
    cw_shard = jnp.concatenate([a_dw_w[0], jnp.zeros((1, LANE), F32), b_conv_w[0], jnp.zeros((5, LANE), F32)], axis=0)

    x2, tgt = x[0], loss_target[0]
    st_a, st_b, st_c, st_d = ("wg1", "wu1", "wd1"), ("win",), ("wa", "wb", "wo", "wg2"), ("wu2", "wd2")

    buf_a, cw = _run_comm(_join(_ag_comm(st_a, flat), _direct_comm(cw_shard, False)), "ag_ffn1")
    h1, n1, gg1, uu1, buf_b = _ffn_fwd(x2, ffn1_norm, (buf_a,) * 3, (0, F, 2 * F), "ffn1_fwd", _ag_comm(st_b, flat))
    u, z, buf_c = _mix_in(h1, mix_norm, buf_b, _ag_comm(st_c, flat))
    a1, q, buf_d = _conv_fwd(z, cw, a_dw_b, _ag_comm(st_d, flat))
    h2, ya, yb = _mix_out(a1, q, z, h1, a_ln_g, a_ln_b, buf_c)
    ffn2_bufs, ffn2_offs = (buf_c, buf_d, buf_d), (3 * D, 0, F)
    h3, n2, gg2, uu2 = _ffn_fwd(h2, ffn2_norm, ffn2_bufs, ffn2_offs, "ffn2_fwd")
    dh3, s_final = _final_loss(h3, final_norm.reshape(1, D), tgt)

    tr_f = F // 2 if (F // 2) % LANE == 0 else F
    def pair(stage, src):
        return _rs_pair_comm(stage, src)

    def chip(stage, src, pair_buf, tag):
        return _rs_chip_comm(_pair_add(stage, src, pair_buf, "pair_add_" + tag))

    dgu2, act2 = _ffn_bwd_hidden(dh3, gg2, uu2, buf_d, F, "ffn2_bwd_h")
    (gu2,) = _tn_matmul(dgu2, n2, tr_f, "dw_gu2")
    s2a, src2a = ("wg2", "wu2"), dict(wg2=(gu2, 0), wu2=(gu2, F))
    gd2, pair2a = _tn_matmul(act2, dh3, tr_f, "dw_d2", pair(s2a, src2a))
    s2b, src2b = ("wd2",), dict(wd2=(gd2, 0))
    dh2, s_ffn2, recv2a, pair2b = _ffn_bwd_input(dgu2, dh3, h2, ffn2_norm, (buf_c, buf_d), (3 * D, 0), "ffn2_bwd_x",
                                                 _join(chip(s2a, src2a, pair2a, "2a"), pair(s2b, src2b)))
    dzg, da1, dq, mb, a3b, dya, dyb, s_mix, recv2b = _mix_out_bwd(dh2, ya, yb, z, a1, a_ln_g, a_ln_b, buf_c,
                                                                   chip(s2b, src2b, pair2b, "2b"))
    (go,) = _tn_matmul(mb, dh2, D, "dw_o")
    (ga,) = _tn_matmul(a3b, dya, D, "dw_a")
    (gb,) = _tn_matmul(q.reshape(1, T, D), dyb, D, "dw_b")
    ssq, srcsq = ("wa", "wb", "wo"), dict(wa=(ga, 0), wb=(gb, 0), wo=(go, 0))
    dz, dwa, dwb, pairsq = _conv_bwd(z, da1, dq, dzg, cw, pair(ssq, srcsq))
    gin, recvsq = _tn_matmul(dz, u, D, "dw_in", chip(ssq, srcsq, pairsq, "sq"))
    sin, srcin = ("win",), dict(win=(gin, 0))
    dh1, s_in, pairin = _mix_in_bwd(dz, dh2, h1, mix_norm, buf_b, pair(sin, srcin))
    dgu1, act1 = _ffn_bwd_hidden(dh1, gg1, uu1, buf_a, 2 * F, "ffn1_bwd_h")
    gu1, recvin = _tn_matmul(dgu1, n1, tr_f, "dw_gu1", chip(sin, srcin, pairin, "in"))
    s1a, src1a = ("wg1", "wu1"), dict(wg1=(gu1, 0), wu1=(gu1, F))
    gd1, pair1a = _tn_matmul(act1, dh1, tr_f, "dw_d1", pair(s1a, src1a))
    s1b, src1b = ("wd1",), dict(wd1=(gd1, 0))
    dx, s_ffn1, recv1a, pair1b = _ffn_bwd_input(dgu1, dh1, x2, ffn1_norm, (buf_a, buf_a), (0, F), "ffn1_bwd_x",
                                                _join(chip(s1a, src1a, pair1a, "1a"), pair(s1b, src1b)))
    (recv1b,) = _run_comm(chip(s1b, src1b, pair1b, "1b"), "rs_chip_1b")
    stages = ((s2a, recv2a, "2a"), (s2b, recv2b, "2b"), (ssq, recvsq, "sq"), (sin, recvin, "in"),
              (s1a, recv1a, "1a"), (s1b, recv1b, "1b"))

    gsum = {}
    for stage, recv, tag in stages:
        st, total = _Stage(stage), _sum_slots(recv, "sum_" + tag)
        for n in stage:
            gsum[n] = total[st.off[n]:st.off[n] + st.rows[n]]

    vec8, convk = _pack_small(s_ffn1, s_in, s_mix, s_ffn2, s_final, dwa, dwb)
    vec_all, conv_all = _run_comm(_join(_direct_comm(vec8, False), _direct_comm(convk, True)), "xchg_small")
    vec_sum, conv_sum, loss_blk = _sum_small(vec_all, conv_all)
    loss = loss_blk[0, 0]

    g = dict(ffn1_w_gate=gsum["wg1"].T, ffn1_w_up=gsum["wu1"].T, ffn1_w_down=gsum["wd1"],
             ffn2_w_gate=gsum["wg2"].T, ffn2_w_up=gsum["wu2"].T, ffn2_w_down=gsum["wd2"], w_in=gsum["win"].T,
             a_w_out=gsum["wa"], b_w_out=gsum["wb"], w_o=gsum["wo"],
             ffn1_norm=vec_sum[0:1], mix_norm=vec_sum[1:2], a_ln_g=vec_sum[2:3], a_ln_b=vec_sum[3:4],
             a_dw_b=vec_sum[4:5], ffn2_norm=vec_sum[5:6], final_norm=vec_sum[6:7],
             a_dw_w=conv_sum[0:KA], b_conv_w=conv_sum[32:32 + KB])

    upd = {}

    def run(group, name, as2d=lambda a: a[0]):
        res, _ = _adam([g[n] for n in group], [as2d(w[n]) for n in group], [as2d(m[n]) for n in group],
                       [as2d(v[n]) for n in group], name)
        for n, r in zip(group, res):
            upd[n] = tuple(a.reshape(w[n].shape) for a in r)

    run(("ffn1_w_gate", "ffn1_w_up", "ffn2_w_gate", "ffn2_w_up"), "adam_gate_up")
    run(("ffn1_w_down", "ffn2_w_down"), "adam_down")
    run(("w_in",), "adam_in")
    run(("a_w_out", "b_w_out", "w_o"), "adam_square")
    run(("a_dw_w",), "adam_dw")
    run(("b_conv_w",), "adam_conv")
    vecs = ("ffn1_norm", "mix_norm", "a_dw_b", "a_ln_g", "a_ln_b", "ffn2_norm", "final_norm")
    run(vecs, "adam_vec", as2d=lambda a: a.reshape(1, D))

    grads = [g[n].reshape(w[n].shape) for n in names]
    return (loss, dx.reshape(x.shape), *grads, *[upd[n][0] for n in names], *[upd[n][1] for n in names],
            *[upd[n][2] for n in names])
```

```python
import jax
import jax.numpy as jnp
from jax import lax
from jax.experimental import pallas as pl
from jax.experimental.pallas import tpu as pltpu

T = 4096
D = 1024
F = 2816
NG = 7
NDEV = 8
NCHIP = 4
KA, KB = 31, 3
EPS = 1e-6
ADAM_LR, ADAM_B1, ADAM_B2, ADAM_EPS, ADAM_WD, ADAM_STEP = 0.001, 0.9, 0.999, 1e-08, 0.01, 10

TM = 512
TMI = 1024
FC = 256
TB = 1024
CH = 128
HALO = 32
LANE = 128
TK = 1024
VMEM_LIMIT = 56 * 1024 * 1024

BF = jnp.bfloat16
F32 = jnp.float32
MESH = pl.DeviceIdType.MESH
ANY = pl.BlockSpec(memory_space=pl.ANY)

ORDER = ("wg1", "wu1", "wd1", "wg2", "wu2", "wd2", "win", "wa", "wb", "wo")


class _Layout:
    def __init__(self):
        fs, dis, ds = F // NDEV, NG * D // NDEV, D // NDEV
        self.rows = dict(wg1=fs, wu1=fs, wd1=fs, wg2=fs, wu2=fs, wd2=fs, win=dis, wa=ds, wb=ds, wo=ds)
        self.fl, off = {}, 0
        for n in ORDER:
            self.fl[n] = off
            off += self.rows[n]
        self.RT = off


class _Stage:
    def __init__(self, names):
        lay = _Layout()
        self.names, self.fl = names, lay.fl
        self.rows = {n: lay.rows[n] for n in names}
        self.off, self.wc, o, w = {}, {}, 0, 0
        for n in names:
            self.off[n], self.wc[n] = o, w
            o += self.rows[n]
            w += NDEV * self.rows[n]
        self.R, self.W = o, w


def _nt(a, b):
    return lax.dot_general(a, b, (((1,), (1,)), ((), ())), preferred_element_type=F32)


def _nn(a, b):
    return lax.dot_general(a, b, (((1,), (0,)), ((), ())), preferred_element_type=F32)


def _tn(a, b):
    return lax.dot_general(a, b, (((0,), (0,)), ((), ())), preferred_element_type=F32)


def _sig(x):
    return 1.0 / (1.0 + jnp.exp(-x))


def _position():
    return lax.axis_index("x"), lax.axis_index("y"), lax.axis_index("c")


def _peer(pos, j):
    x, y, c = pos
    return (1 - x if j & 4 else x, 1 - y if j & 2 else y, 1 - c if j & 1 else c)


def _lin(pos):
    return 4 * pos[0] + 2 * pos[1] + pos[2]


def _chip(pos):
    return 2 * pos[0] + pos[1]


class _Comm:
    def __init__(self, inputs, out_shapes, scratch, start, finish):
        self.inputs, self.out_shapes, self.scratch, self.start, self.finish = inputs, out_shapes, scratch, start, finish


def _call(body, *, name, grid, args, in_specs, out_shape, out_specs, scratch_shapes=(), comm=None,
          num_scalar_prefetch=0):
    in_specs, out_shape, out_specs, scratch_shapes = list(in_specs), list(out_shape), list(out_specs), list(scratch_shapes)
    n_in, n_out, n_scr = len(in_specs), len(out_shape), len(scratch_shapes)
    sp = num_scalar_prefetch
    if comm is None:
        kernel_fn = lambda *refs: body(*refs)
        c_in = c_out = c_scr = 0
    else:
        c_in, c_out, c_scr = len(comm.inputs), len(comm.out_shapes), len(comm.scratch)

        def kernel_fn(*refs):
            pre, refs = refs[:sp], refs[sp:]
            ins, cins = refs[:n_in], refs[n_in:n_in + c_in]
            o0 = n_in + c_in
            outs, couts = refs[o0:o0 + n_out], refs[o0 + n_out:o0 + n_out + c_out]
            s0 = o0 + n_out + c_out
            scr, cscr = refs[s0:s0 + n_scr], refs[s0 + n_scr:]
            first = pl.program_id(0) == 0
            last = pl.program_id(0) == grid[0] - 1
            for a in range(1, len(grid)):
                first = first & (pl.program_id(a) == 0)
                last = last & (pl.program_id(a) == grid[a] - 1)

            @pl.when(first)
            def _():
                comm.start(cins, couts, cscr)

            body(*pre, *ins, *outs, *scr)

            @pl.when(last)
            def _():
                comm.finish(cins, couts, cscr)

        args = list(args) + list(comm.inputs)
        in_specs += [ANY] * c_in
        out_shape += list(comm.out_shapes)
        out_specs += [ANY] * c_out
        scratch_shapes += list(comm.scratch)
    params = pltpu.CompilerParams(dimension_semantics=("arbitrary",) * len(grid), vmem_limit_bytes=VMEM_LIMIT)
    if sp:
        grid_spec = pltpu.PrefetchScalarGridSpec(num_scalar_prefetch=sp, grid=grid, in_specs=in_specs,
                                                 out_specs=out_specs, scratch_shapes=scratch_shapes)
        return pl.pallas_call(kernel_fn, name=name, grid_spec=grid_spec, out_shape=out_shape,
                              compiler_params=params)(*args)
    return pl.pallas_call(kernel_fn, name=name, grid=grid, in_specs=in_specs, out_shape=out_shape, out_specs=out_specs,
                          scratch_shapes=scratch_shapes, compiler_params=params)(*args)


def _join(a, b):
    na = (len(a.inputs), len(a.out_shapes), len(a.scratch))

    def split(refs):
        return ([r[:n] for r, n in zip(refs, na)], [r[n:] for r, n in zip(refs, na)])

    def start(*refs):
        ra, rb = split(refs)
        a.start(*ra)
        b.start(*rb)

    def finish(*refs):
        ra, rb = split(refs)
        a.finish(*ra)
        b.finish(*rb)

    return _Comm(list(a.inputs) + list(b.inputs), list(a.out_shapes) + list(b.out_shapes),
                 list(a.scratch) + list(b.scratch), start, finish)


def _run_comm(comm, name):
    def body(*refs):
        c_in, c_out = len(comm.inputs), len(comm.out_shapes)
        comm.start(refs[:c_in], refs[c_in:c_in + c_out], refs[c_in + c_out:])
        comm.finish(refs[:c_in], refs[c_in:c_in + c_out], refs[c_in + c_out:])

    return pl.pallas_call(
        body, name=name, out_shape=list(comm.out_shapes), in_specs=[ANY] * len(comm.inputs),
        out_specs=[ANY] * len(comm.out_shapes), scratch_shapes=list(comm.scratch))(*comm.inputs)


def _ag_comm(names, flat):
    st = _Stage(names)

    def parts(refs):
        (flat_ref,), (out_ref,), (send_sems, recv_sems, local_sem) = refs
        me = _position()

        def region(name, dev):
            r = st.rows[name]
            return out_ref.at[pl.ds(st.wc[name] + _lin(dev) * r, r), :]

        def own(name):
            return flat_ref.at[pl.ds(st.fl[name], st.rows[name]), :]

        def copies(k, dev, to, from_flat):
            return [pltpu.make_async_remote_copy(
                src_ref=own(n) if from_flat else region(n, dev), dst_ref=region(n, dev), send_sem=send_sems.at[k],
                recv_sem=recv_sems.at[k], device_id=to, device_id_type=MESH) for n in names]

        def whole(k):
            return pltpu.make_async_remote_copy(
                src_ref=flat_ref.at[pl.ds(0, st.R), :], dst_ref=out_ref.at[pl.ds(0, st.R), :],
                send_sem=send_sems.at[k], recv_sem=recv_sems.at[k], device_id=me, device_id_type=MESH)

        return me, region, own, copies, whole, flat_ref, out_ref, local_sem

    def start(*refs):
        me, region, own, copies, _, _, _, local_sem = parts(refs)
        for n in names:
            pltpu.make_async_copy(own(n), region(n, me), local_sem).start()
        for cp in copies(0, me, _peer(me, 1), True):
            cp.start()
        for j, bits in enumerate((4, 2, 6)):
            for cp in copies(1 + j, me, _peer(me, bits), True):
                cp.start()

    def finish(*refs):
        me, _, _, copies, whole, flat_ref, out_ref, local_sem = parts(refs)
        for j, bits in enumerate((4, 2, 6)):
            whole(1 + j).wait_recv()
            for cp in copies(4 + j, _peer(me, bits), _peer(me, 1), False):
                cp.start()
        whole(0).wait_recv()
        for j in range(3):
            whole(4 + j).wait_recv()
        for k in range(7):
            whole(k).wait_send()
        pltpu.make_async_copy(flat_ref.at[pl.ds(0, st.R), :], out_ref.at[pl.ds(0, st.R), :], local_sem).wait()

    return _Comm([flat], [jax.ShapeDtypeStruct((st.W, D), BF)],
                 [pltpu.SemaphoreType.DMA((7,)), pltpu.SemaphoreType.DMA((7,)), pltpu.SemaphoreType.DMA],
                 start, finish)


def _rs_pair_comm(names, src):
    st = _Stage(names)
    arrays = []
    for n in names:
        if not any(src[n][0] is a for a in arrays):
            arrays.append(src[n][0])
    idx = {n: [i for i, a in enumerate(arrays) if a is src[n][0]][0] for n in names}

    def slot_wait(refs):
        recv = refs[1][0]
        send_sem, recv_sem = refs[2]
        return pltpu.make_async_remote_copy(src_ref=recv, dst_ref=recv, send_sem=send_sem, recv_sem=recv_sem,
                                            device_id=_position(), device_id_type=MESH)

    def start(*refs):
        ins, (recv,), (send_sem, recv_sem) = refs
        me = _position()
        sib = _peer(me, 1)
        for q in range(NCHIP):
            dev = (q // 2, q % 2, sib[2])
            for n in names:
                r = st.rows[n]
                pltpu.make_async_remote_copy(
                    src_ref=ins[idx[n]].at[pl.ds(src[n][1] + _lin(dev) * r, r), :],
                    dst_ref=recv.at[q, pl.ds(st.off[n], r), :], send_sem=send_sem, recv_sem=recv_sem,
                    device_id=sib, device_id_type=MESH).start()

    def finish(*refs):
        w = slot_wait(refs)
        w.wait_recv()
        w.wait_send()

    return _Comm(arrays, [jax.ShapeDtypeStruct((NCHIP, st.R, D), BF)],
                 [pltpu.SemaphoreType.DMA, pltpu.SemaphoreType.DMA], start, finish)


def _pair_add(names, src, recv, name):
    st = _Stage(names)
    c_arr = jnp.reshape(lax.axis_index("c"), (1,)).astype(jnp.int32)

    def body(c_ref, *refs):
        r_ref, o_ref = refs[len(names)], refs[len(names) + 1]
        for a_ref, n in zip(refs, names):
            rows = slice(st.off[n], st.off[n] + st.rows[n])
            o_ref[rows, :] = (a_ref[...].astype(F32) + r_ref[rows, :].astype(F32)).astype(BF)

    def shard_spec(n):
        r, base = st.rows[n], src[n][1] // st.rows[n]
        return pl.BlockSpec((r, D), lambda q, c_ref: (base + 2 * q + c_ref[0], 0))

    slot = pl.BlockSpec((None, st.R, D), lambda q, c_ref: (q, 0, 0))
    return _call(body, name=name, grid=(NCHIP,), args=[c_arr] + [src[n][0] for n in names] + [recv],
                 in_specs=[shard_spec(n) for n in names] + [slot],
                 out_shape=[jax.ShapeDtypeStruct((NCHIP, st.R, D), BF)], out_specs=[slot], num_scalar_prefetch=1)[0]


def _rs_chip_comm(part):
    def copies(refs):
        (p_ref,), (recv,), (send_sems, recv_sems, local_sem) = refs
        me = _position()
        mine = pltpu.make_async_copy(p_ref.at[_chip(me)], recv.at[_chip(me)], local_sem)
        out = []
        for j, bits in enumerate((4, 2, 6)):
            to = _peer(me, bits)
            out.append(pltpu.make_async_remote_copy(
                src_ref=p_ref.at[_chip(to)], dst_ref=recv.at[_chip(me)], send_sem=send_sems.at[j],
                recv_sem=recv_sems.at[j], device_id=to, device_id_type=MESH))
        return mine, out

    def start(*refs):
        mine, out = copies(refs)
        mine.start()
        for cp in out:
            cp.start()

    def finish(*refs):
        mine, out = copies(refs)
        for cp in out:
            cp.wait_recv()
        for cp in out:
            cp.wait_send()
        mine.wait()

    return _Comm([part], [jax.ShapeDtypeStruct(part.shape, BF)],
                 [pltpu.SemaphoreType.DMA((3,)), pltpu.SemaphoreType.DMA((3,)), pltpu.SemaphoreType.DMA],
                 start, finish)


def _direct_comm(x, scatter):
    def copies(refs):
        (x_ref,), (out_ref,), (send_sems, recv_sems, local_sem) = refs
        me = _position()

        def piece(dev):
            return x_ref.at[_lin(dev)] if scatter else x_ref

        mine = pltpu.make_async_copy(piece(me), out_ref.at[_lin(me)], local_sem)
        return mine, [pltpu.make_async_remote_copy(
            src_ref=piece(_peer(me, j)), dst_ref=out_ref.at[_lin(me)], send_sem=send_sems.at[j - 1],
            recv_sem=recv_sems.at[j - 1], device_id=_peer(me, j), device_id_type=MESH) for j in range(1, NDEV)]

    def start(*refs):
        mine, cps = copies(refs)
        mine.start()
        for cp in cps:
            cp.start()

    def finish(*refs):
        mine, cps = copies(refs)
        for cp in cps:
            cp.wait_recv()
        for cp in cps:
            cp.wait_send()
        mine.wait()

    shape = x.shape if scatter else (NDEV,) + x.shape
    return _Comm([x], [jax.ShapeDtypeStruct(shape, x.dtype)],
                 [pltpu.SemaphoreType.DMA((7,)), pltpu.SemaphoreType.DMA((7,)), pltpu.SemaphoreType.DMA],
                 start, finish)


def _load_ffn_weights(srcs, offs, scratch, sem):
    @pl.when(pl.program_id(0) == 0)
    def _():
        cps = [pltpu.make_async_copy(s.at[pl.ds(off, F), :], dst, sem.at[i])
               for i, (s, off, dst) in enumerate(zip(srcs, offs, scratch))]
        for cp in cps:
            cp.start()
        for cp in cps:
            cp.wait()


def _ffn_fwd(x, g, wbufs, offs, name, comm=None):
    nf = F // FC

    def body(x_ref, g_ref, b0, b1, b2, h_ref, n_ref, gg_ref, uu_ref, wg_s, wu_s, wd_s, sem):
        _load_ffn_weights((b0, b1, b2), offs, (wg_s, wu_s, wd_s), sem)
        xf = x_ref[...]
        r = lax.rsqrt(jnp.mean(xf * xf, axis=-1, keepdims=True) + EPS)
        nb = (xf * r * g_ref[...]).astype(BF)
        n_ref[...] = nb
        acc = jnp.zeros((TM, D), F32)
        for c in range(nf):
            sl = slice(c * FC, (c + 1) * FC)
            gc = _nt(nb, wg_s[sl, :])
            uc = _nt(nb, wu_s[sl, :])
            gg_ref[:, sl] = gc.astype(BF)
            uu_ref[:, sl] = uc.astype(BF)
            a = (0.5 * gc * _sig(gc)) * uc
            acc = acc + _nn(a.astype(BF), wd_s[sl, :])
        h_ref[...] = xf + acc

    row = lambda i: (i, 0)
    return _call(
        body, name=name, grid=(T // TM,), args=[x, g, *wbufs], comm=comm,
        in_specs=[pl.BlockSpec((TM, D), row), pl.BlockSpec((1, D), lambda i: (0, 0)), ANY, ANY, ANY],
        out_shape=[jax.ShapeDtypeStruct((T, D), F32), jax.ShapeDtypeStruct((T, D), BF),
                   jax.ShapeDtypeStruct((T, F), BF), jax.ShapeDtypeStruct((T, F), BF)],
        out_specs=[pl.BlockSpec((TM, D), row), pl.BlockSpec((TM, D), row),
                   pl.BlockSpec((TM, F), row), pl.BlockSpec((TM, F), row)],
        scratch_shapes=[pltpu.VMEM((F, D), BF)] * 3 + [pltpu.SemaphoreType.DMA((3,))])


def _mix_in(h1, gm, win, comm=None):
    def body(h_ref, g_ref, w_ref, u_ref, z_ref, u_s):
        @pl.when(pl.program_id(1) == 0)
        def _():
            xf = h_ref[...]
            r = lax.rsqrt(jnp.mean(xf * xf, axis=-1, keepdims=True) + EPS)
            ub = (xf * r * g_ref[...]).astype(BF)
            u_s[...] = ub
            u_ref[...] = ub
        z_ref[...] = _nt(u_s[...], w_ref[...])

    return _call(
        body, name="mix_in", grid=(T // TMI, NG), args=[h1, gm, win], comm=comm,
        in_specs=[pl.BlockSpec((TMI, D), lambda i, j: (i, 0)), pl.BlockSpec((1, D), lambda i, j: (0, 0)),
                  pl.BlockSpec((D, D), lambda i, j: (j, 0))],
        out_shape=[jax.ShapeDtypeStruct((T, D), BF), jax.ShapeDtypeStruct((NG, T, D), F32)],
        out_specs=[pl.BlockSpec((TMI, D), lambda i, j: (i, 0)), pl.BlockSpec((None, TMI, D), lambda i, j: (j, i, 0))],
        scratch_shapes=[pltpu.VMEM((TMI, D), BF)])


def _shift_up(w, b):
    return w if b == 0 else pltpu.roll(w, w.shape[0] - b, 0)


def _fold8(p):
    red = p[0:8, :]
    for i in range(1, p.shape[0] // 8):
        red = red + p[8 * i:8 * i + 8, :]
    return red


def _conv_fwd(z, cw, bias, comm=None):
    nt = T // TB
    hb = TB // HALO

    def body(z_ref, zh_ref, cw_ref, b_ref, a1_ref, q_ref, apad, ppad):
        first = pl.program_id(1) == 0
        apad[0:HALO, :] = jnp.where(first, 0.0, zh_ref[0] * _sig(zh_ref[1]))
        apad[HALO:, :] = z_ref[0] * _sig(z_ref[1])
        ppad[0:HALO, :] = jnp.where(first, 0.0, zh_ref[3] * zh_ref[4])
        ppad[HALO:, :] = z_ref[3] * z_ref[4]
        bias_row = b_ref[...]

        def chunk(r, carry):
            base = pl.multiple_of(r * CH, CH)
            w = apad[pl.ds(base, CH + HALO), :]
            acc = jnp.broadcast_to(bias_row, (CH, LANE))
            for b in range(8):
                wb = _shift_up(w, b)
                for a in range(5):
                    s = 8 * a + b
                    if 2 <= s <= HALO:
                        acc = acc + cw_ref[pl.ds(s - 2, 1), :] * wb[8 * a:8 * a + CH, :]
            a1_ref[pl.ds(base, CH), :] = acc
            pw = ppad[pl.ds(base, CH + HALO), :]
            v = (cw_ref[pl.ds(32, 1), :] * _shift_up(pw, 6)[24:24 + CH, :]
                 + cw_ref[pl.ds(33, 1), :] * _shift_up(pw, 7)[24:24 + CH, :]
                 + cw_ref[pl.ds(34, 1), :] * pw[32:32 + CH, :])
            q_ref[pl.ds(base, CH), :] = (z_ref[2, pl.ds(base, CH), :] * v).astype(BF)
            return carry

        lax.fori_loop(0, TB // CH, chunk, 0)

    return _call(
        body, name="conv_fwd", grid=(D // LANE, nt), args=[z, z, cw, bias], comm=comm,
        in_specs=[pl.BlockSpec((5, TB, LANE), lambda c, t: (0, t, c)),
                  pl.BlockSpec((5, HALO, LANE), lambda c, t: (0, jnp.maximum(t * hb - 1, 0), c)),
                  pl.BlockSpec((None, 40, LANE), lambda c, t: (c, 0, 0)),
                  pl.BlockSpec((1, LANE), lambda c, t: (0, c))],
        out_shape=[jax.ShapeDtypeStruct((T, D), F32), jax.ShapeDtypeStruct((T, D), BF)],
        out_specs=[pl.BlockSpec((TB, LANE), lambda c, t: (t, c)), pl.BlockSpec((TB, LANE), lambda c, t: (t, c))],
        scratch_shapes=[pltpu.VMEM((TB + HALO, LANE), F32), pltpu.VMEM((TB + HALO, LANE), F32)])


def _layernorm_silu(a1, lng, lnb):
    mu = jnp.mean(a1, axis=-1, keepdims=True)
    xc = a1 - mu
    rs = lax.rsqrt(jnp.mean(xc * xc, axis=-1, keepdims=True) + EPS)
    xh = xc * rs
    a2 = xh * lng + lnb
    sg = _sig(a2)
    return xh, rs, a2, sg


def _square_specs(blocks):
    return [pl.BlockSpec((D, D), lambda i, b=b: (b, 0)) for b in blocks]


def _mix_out(a1, q, z, h1, lng, lnb, wsq, comm=None):
    def body(a1_ref, q_ref, ga_ref, gb_ref, h_ref, lng_ref, lnb_ref, wa_ref, wb_ref, wo_ref, h2_ref, ya_ref, yb_ref):
        _, _, a2, sg = _layernorm_silu(a1_ref[...], lng_ref[...], lnb_ref[...])
        ya = _nn((a2 * sg).astype(BF), wa_ref[...])
        yb = _nn(q_ref[...], wb_ref[...])
        ya_ref[...] = ya
        yb_ref[...] = yb
        m = _sig(ga_ref[...]) * ya + _sig(gb_ref[...]) * yb
        h2_ref[...] = h_ref[...] + _nn(m.astype(BF), wo_ref[...])

    row = lambda i: (i, 0)
    vec = pl.BlockSpec((1, D), lambda i: (0, 0))
    return _call(
        body, name="mix_out", grid=(T // TM,), args=[a1, q, z, z, h1, lng, lnb, wsq, wsq, wsq], comm=comm,
        in_specs=[pl.BlockSpec((TM, D), row), pl.BlockSpec((TM, D), row),
                  pl.BlockSpec((None, TM, D), lambda i: (5, i, 0)), pl.BlockSpec((None, TM, D), lambda i: (6, i, 0)),
                  pl.BlockSpec((TM, D), row), vec, vec] + _square_specs((0, 1, 2)),
        out_shape=[jax.ShapeDtypeStruct((T, D), F32)] * 3,
        out_specs=[pl.BlockSpec((TM, D), row)] * 3)


def _final_loss(h3, gf, tgt, comm=None):
    def body(h_ref, g_ref, t_ref, dh_ref, s_ref):
        @pl.when(pl.program_id(0) == 0)
        def _():
            s_ref[...] = jnp.zeros_like(s_ref)
        xf = h_ref[...]
        g = g_ref[...]
        r = lax.rsqrt(jnp.mean(xf * xf, axis=-1, keepdims=True) + EPS)
        xr = xf * r
        e = xr * g - t_ref[...]
        s_ref[1:2, :] += jnp.sum(e * e, axis=0, keepdims=True) * (0.5 / D)
        dy = e * (1.0 / D)
        s_ref[0:1, :] += jnp.sum(dy * xr, axis=0, keepdims=True)
        gdy = dy * g
        dh_ref[...] = r * gdy - xr * (r * jnp.mean(gdy * xr, axis=-1, keepdims=True))

    row = lambda i: (i, 0)
    return _call(
        body, name="final_loss", grid=(T // TM,), args=[h3, gf, tgt], comm=comm,
        in_specs=[pl.BlockSpec((TM, D), row), pl.BlockSpec((1, D), lambda i: (0, 0)), pl.BlockSpec((TM, D), row)],
        out_shape=[jax.ShapeDtypeStruct((T, D), F32), jax.ShapeDtypeStruct((8, D), F32)],
        out_specs=[pl.BlockSpec((TM, D), row), pl.BlockSpec((8, D), lambda i: (0, 0))])


def _rmsnorm_bwd(xf, g, dn):
    r = lax.rsqrt(jnp.mean(xf * xf, axis=-1, keepdims=True) + EPS)
    xr = xf * r
    gdn = dn * g
    dx = r * gdn - xr * (r * jnp.mean(gdn * xr, axis=-1, keepdims=True))
    return dx, jnp.sum(dn * xr, axis=0, keepdims=True)


def _ffn_bwd_hidden(dh, gg, uu, wbuf, off, name, comm=None):
    nf = F // FC

    def body(dh_ref, gg_ref, uu_ref, b0, dgu_ref, a_ref, wd_s, sem):
        _load_ffn_weights((b0,), (off,), (wd_s,), sem)
        dhb = (0.5 * dh_ref[...]).astype(BF)
        for c in range(nf):
            sl = slice(c * FC, (c + 1) * FC)
            da = _nt(dhb, wd_s[sl, :])
            gc = gg_ref[:, sl].astype(F32)
            uc = uu_ref[:, sl].astype(F32)
            sg = _sig(gc)
            silu = gc * sg
            dgu_ref[0, :, sl] = ((da * uc) * (sg * (1.0 + gc * (1.0 - sg)))).astype(BF)
            dgu_ref[1, :, sl] = (da * silu).astype(BF)
            a_ref[0, :, sl] = (silu * uc).astype(BF)

    row = lambda i: (i, 0)
    return _call(
        body, name=name, grid=(T // TM,), args=[dh, gg, uu, wbuf], comm=comm,
        in_specs=[pl.BlockSpec((TM, D), row), pl.BlockSpec((TM, F), row), pl.BlockSpec((TM, F), row), ANY],
        out_shape=[jax.ShapeDtypeStruct((2, T, F), BF), jax.ShapeDtypeStruct((1, T, F), BF)],
        out_specs=[pl.BlockSpec((2, TM, F), lambda i: (0, i, 0)), pl.BlockSpec((1, TM, F), lambda i: (0, i, 0))],
        scratch_shapes=[pltpu.VMEM((F, D), BF), pltpu.SemaphoreType.DMA((1,))])


def _ffn_bwd_input(dgu, dh, x, g, wbufs, offs, name, comm=None):
    nf = F // FC

    def body(dgu_ref, dh_ref, x_ref, g_ref, b0, b1, dx_ref, s_ref, wg_s, wu_s, sem):
        _load_ffn_weights((b0, b1), offs, (wg_s, wu_s), sem)

        @pl.when(pl.program_id(0) == 0)
        def _():
            s_ref[...] = jnp.zeros_like(s_ref)

        dn = jnp.zeros((TM, D), F32)
        for c in range(nf):
            sl = slice(c * FC, (c + 1) * FC)
            dn = dn + _nn(dgu_ref[0, :, sl], wg_s[sl, :]) + _nn(dgu_ref[1, :, sl], wu_s[sl, :])
        dxn, dg = _rmsnorm_bwd(x_ref[...], g_ref[...], dn)
        dx_ref[...] = dh_ref[...] + dxn
        s_ref[0:1, :] += dg

    row = lambda i: (i, 0)
    return _call(
        body, name=name, grid=(T // TM,), args=[dgu, dh, x, g, *wbufs], comm=comm,
        in_specs=[pl.BlockSpec((2, TM, F), lambda i: (0, i, 0)), pl.BlockSpec((TM, D), row),
                  pl.BlockSpec((TM, D), row), pl.BlockSpec((1, D), lambda i: (0, 0)), ANY, ANY],
        out_shape=[jax.ShapeDtypeStruct((T, D), F32), jax.ShapeDtypeStruct((8, D), F32)],
        out_specs=[pl.BlockSpec((TM, D), row), pl.BlockSpec((8, D), lambda i: (0, 0))],
        scratch_shapes=[pltpu.VMEM((F, D), BF)] * 2 + [pltpu.SemaphoreType.DMA((2,))])


def _tn_matmul(lhs, rhs, tr, name, comm=None, scale=None):
    ng, _, cdim = lhs.shape
    nc, nk = cdim // tr, T // TK

    def body(l_ref, r_ref, o_ref, acc):
        k = pl.program_id(2)

        @pl.when(k == 0)
        def _():
            acc[...] = jnp.zeros_like(acc)

        r = r_ref[...] if scale is None else scale * r_ref[...]
        acc[...] += _tn(l_ref[...], r.astype(BF))

        @pl.when(k == nk - 1)
        def _():
            o_ref[...] = acc[...].astype(BF)

    return _call(
        body, name=name, grid=(ng, nc, nk), args=[lhs, rhs], comm=comm,
        in_specs=[pl.BlockSpec((None, TK, tr), lambda g, c, k: (g, k, c)),
                  pl.BlockSpec((TK, D), lambda g, c, k: (k, 0))],
        out_shape=[jax.ShapeDtypeStruct((ng * cdim, D), BF)],
        out_specs=[pl.BlockSpec((tr, D), lambda g, c, k: (g * nc + c, 0))],
        scratch_shapes=[pltpu.VMEM((tr, D), F32)])


def _mix_out_bwd(dh2, ya, yb, z, a1, lng, lnb, wsq, comm=None):
    def body(dh_ref, ya_ref, yb_ref, ga_ref, gb_ref, a1_ref, lng_ref, lnb_ref, wa_ref, wb_ref, wo_ref,
             dzg_ref, da1_ref, dq_ref, m_ref, a3_ref, dya_ref, dyb_ref, s_ref):
        @pl.when(pl.program_id(0) == 0)
        def _():
            s_ref[...] = jnp.zeros_like(s_ref)

        dm = _nt(dh_ref[...].astype(BF), wo_ref[...])
        ya, yb = ya_ref[...], yb_ref[...]
        sa, sb = _sig(ga_ref[...]), _sig(gb_ref[...])
        m_ref[0] = (sa * ya + sb * yb).astype(BF)
        dzg_ref[0] = (dm * ya * (sa * (1.0 - sa))).astype(BF)
        dzg_ref[1] = (dm * yb * (sb * (1.0 - sb))).astype(BF)
        dya = (dm * sa).astype(BF)
        dyb = (dm * sb).astype(BF)
        dya_ref[...] = dya
        dyb_ref[...] = dyb
        dq_ref[...] = _nt(dyb, wb_ref[...])
        da3 = _nt(dya, wa_ref[...])
        lng = lng_ref[...]
        xh, rs, a2, sg = _layernorm_silu(a1_ref[...], lng, lnb_ref[...])
        a3_ref[0] = (a2 * sg).astype(BF)
        da2 = da3 * (sg * (1.0 + a2 * (1.0 - sg)))
        s_ref[0:1, :] += jnp.sum(da2 * xh, axis=0, keepdims=True)
        s_ref[1:2, :] += jnp.sum(da2, axis=0, keepdims=True)
        dxh = da2 * lng
        da1 = rs * (dxh - jnp.mean(dxh, axis=-1, keepdims=True) - xh * jnp.mean(dxh * xh, axis=-1, keepdims=True))
        da1_ref[...] = da1
        s_ref[2:3, :] += jnp.sum(da1, axis=0, keepdims=True)

    row = lambda i: (i, 0)
    row3 = lambda i: (0, i, 0)
    vec = pl.BlockSpec((1, D), lambda i: (0, 0))
    return _call(
        body, name="mix_out_bwd", grid=(T // TM,), args=[dh2, ya, yb, z, z, a1, lng, lnb, wsq, wsq, wsq], comm=comm,
        in_specs=[pl.BlockSpec((TM, D), row), pl.BlockSpec((TM, D), row), pl.BlockSpec((TM, D), row),
                  pl.BlockSpec((None, TM, D), lambda i: (5, i, 0)), pl.BlockSpec((None, TM, D), lambda i: (6, i, 0)),
                  pl.BlockSpec((TM, D), row), vec, vec] + _square_specs((0, 1, 2)),
        out_shape=[jax.ShapeDtypeStruct((2, T, D), BF), jax.ShapeDtypeStruct((T, D), F32),
                   jax.ShapeDtypeStruct((T, D), F32), jax.ShapeDtypeStruct((1, T, D), BF),
                   jax.ShapeDtypeStruct((1, T, D), BF), jax.ShapeDtypeStruct((T, D), BF),
                   jax.ShapeDtypeStruct((T, D), BF), jax.ShapeDtypeStruct((8, D), F32)],
        out_specs=[pl.BlockSpec((2, TM, D), row3), pl.BlockSpec((TM, D), row), pl.BlockSpec((TM, D), row),
                   pl.BlockSpec((1, TM, D), row3), pl.BlockSpec((1, TM, D), row3), pl.BlockSpec((TM, D), row),
                   pl.BlockSpec((TM, D), row), pl.BlockSpec((8, D), lambda i: (0, 0))])


def _conv_bwd(z, da1, dq, dzg, cw, comm=None):
    nt = T // TB
    hb = TB // HALO
    last_h = T // HALO - 1

    def body(z_ref, zp_ref, zn_ref, da1_ref, da1n_ref, dq_ref, dqn_ref, dzg_ref, cw_ref,
             dz_ref, dwa_ref, dwb_ref, apad, dypad, ppad, dvpad, acc_a, acc_b):
        t = pl.program_id(1)
        first, last = t == 0, t == nt - 1
        apad[0:HALO, :] = jnp.where(first, 0.0, zp_ref[0] * _sig(zp_ref[1]))
        apad[HALO:, :] = z_ref[0] * _sig(z_ref[1])
        ppad[0:HALO, :] = jnp.where(first, 0.0, zp_ref[3] * zp_ref[4])
        ppad[HALO:, :] = z_ref[3] * z_ref[4]
        dypad[0:TB, :] = da1_ref[...]
        dypad[TB:, :] = jnp.where(last, 0.0, da1n_ref[...])
        dvpad[0:TB, :] = dq_ref[...] * z_ref[2]
        dvpad[TB:, :] = jnp.where(last, 0.0, dqn_ref[...] * zn_ref[2])

        @pl.when(t == 0)
        def _():
            acc_a[...] = jnp.zeros_like(acc_a)
            acc_b[...] = jnp.zeros_like(acc_b)

        def chunk(r, carry):
            base = pl.multiple_of(r * CH, CH)
            rows = pl.ds(base, CH)
            dw_ = dypad[pl.ds(base, CH + HALO), :]
            da0 = jnp.zeros((CH, LANE), F32)
            for b in range(8):
                wb = _shift_up(dw_, b)
                for a in range(4):
                    o = 8 * a + b
                    if o <= KA - 1:
                        da0 = da0 + cw_ref[pl.ds(KA - 1 - o, 1), :] * wb[8 * a:8 * a + CH, :]
            z0, z1 = z_ref[0, rows, :], z_ref[1, rows, :]
            s1 = _sig(z1)
            dz_ref[0, rows, :] = (da0 * s1).astype(BF)
            dz_ref[1, rows, :] = (da0 * z0 * (s1 * (1.0 - s1))).astype(BF)
            dyc = dypad[rows, :]
            aw = apad[pl.ds(base, CH + HALO), :]
            for b in range(8):
                wb = _shift_up(aw, b)
                for a in range(5):
                    s = 8 * a + b
                    if 2 <= s <= HALO:
                        k8 = 8 * (s - 2)
                        acc_a[k8:k8 + 8, :] += _fold8(dyc * wb[8 * a:8 * a + CH, :])
            pw = ppad[pl.ds(base, CH + HALO), :]
            p6 = _shift_up(pw, 6)[24:24 + CH, :]
            p7 = _shift_up(pw, 7)[24:24 + CH, :]
            p8 = pw[32:32 + CH, :]
            wb0, wb1, wb2 = cw_ref[pl.ds(32, 1), :], cw_ref[pl.ds(33, 1), :], cw_ref[pl.ds(34, 1), :]
            v = wb0 * p6 + wb1 * p7 + wb2 * p8
            dz_ref[2, rows, :] = (dq_ref[rows, :] * v).astype(BF)
            dvw = dvpad[pl.ds(base, CH + HALO), :]
            dvc = dvw[0:CH, :]
            dp = wb2 * dvc + wb1 * _shift_up(dvw, 1)[0:CH, :] + wb0 * _shift_up(dvw, 2)[0:CH, :]
            dz_ref[3, rows, :] = (dp * z_ref[4, rows, :]).astype(BF)
            dz_ref[4, rows, :] = (dp * z_ref[3, rows, :]).astype(BF)
            acc_b[0:8, :] += _fold8(dvc * p6)
            acc_b[8:16, :] += _fold8(dvc * p7)
            acc_b[16:24, :] += _fold8(dvc * p8)
            dz_ref[5, rows, :] = dzg_ref[0, rows, :]
            dz_ref[6, rows, :] = dzg_ref[1, rows, :]
            return carry

        lax.fori_loop(0, TB // CH, chunk, 0)

        @pl.when(t == nt - 1)
        def _():
            for k in range(KA):
                dwa_ref[k:k + 1, :] = jnp.sum(acc_a[8 * k:8 * k + 8, :], axis=0, keepdims=True)
            dwa_ref[KA:32, :] = jnp.zeros((32 - KA, LANE), F32)
            for k in range(KB):
                dwb_ref[k:k + 1, :] = jnp.sum(acc_b[8 * k:8 * k + 8, :], axis=0, keepdims=True)
            dwb_ref[KB:8, :] = jnp.zeros((8 - KB, LANE), F32)

    blk = lambda c, t: (t, c)
    nxt = lambda c, t: (jnp.minimum((t + 1) * hb, last_h), c)
    return _call(
        body, name="conv_bwd", grid=(D // LANE, nt), args=[z, z, z, da1, da1, dq, dq, dzg, cw], comm=comm,
        in_specs=[pl.BlockSpec((5, TB, LANE), lambda c, t: (0, t, c)),
                  pl.BlockSpec((5, HALO, LANE), lambda c, t: (0, jnp.maximum(t * hb - 1, 0), c)),
                  pl.BlockSpec((5, HALO, LANE), lambda c, t: (0, jnp.minimum((t + 1) * hb, last_h), c)),
                  pl.BlockSpec((TB, LANE), blk), pl.BlockSpec((HALO, LANE), nxt),
                  pl.BlockSpec((TB, LANE), blk), pl.BlockSpec((HALO, LANE), nxt),
                  pl.BlockSpec((2, TB, LANE), lambda c, t: (0, t, c)),
                  pl.BlockSpec((None, 40, LANE), lambda c, t: (c, 0, 0))],
        out_shape=[jax.ShapeDtypeStruct((NG, T, D), BF), jax.ShapeDtypeStruct((32, D), F32),
                   jax.ShapeDtypeStruct((8, D), F32)],
        out_specs=[pl.BlockSpec((NG, TB, LANE), lambda c, t: (0, t, c)),
                   pl.BlockSpec((32, LANE), lambda c, t: (0, c)), pl.BlockSpec((8, LANE), lambda c, t: (0, c))],
        scratch_shapes=[pltpu.VMEM((TB + HALO, LANE), F32)] * 4
                       + [pltpu.VMEM((8 * 32, LANE), F32), pltpu.VMEM((24, LANE), F32)])


def _mix_in_bwd(dz, dh2, h1, gm, win, comm=None):
    def body(dz_ref, w_ref, dh_ref, h_ref, g_ref, o_ref, s_ref, acc):
        i, j = pl.program_id(0), pl.program_id(1)

        @pl.when((i == 0) & (j == 0))
        def _():
            s_ref[...] = jnp.zeros_like(s_ref)

        @pl.when(j == 0)
        def _():
            acc[...] = jnp.zeros_like(acc)

        acc[...] += _nn(dz_ref[...], w_ref[...])

        @pl.when(j == NG - 1)
        def _():
            dx, dg = _rmsnorm_bwd(h_ref[...], g_ref[...], acc[...])
            o_ref[...] = dh_ref[...] + dx
            s_ref[0:1, :] += dg

    row = lambda i, j: (i, 0)
    return _call(
        body, name="mix_in_bwd", grid=(T // TMI, NG), args=[dz, win, dh2, h1, gm], comm=comm,
        in_specs=[pl.BlockSpec((None, TMI, D), lambda i, j: (j, i, 0)),
                  pl.BlockSpec((D, D), lambda i, j: (j, 0)),
                  pl.BlockSpec((TMI, D), row), pl.BlockSpec((TMI, D), row), pl.BlockSpec((1, D), lambda i, j: (0, 0))],
        out_shape=[jax.ShapeDtypeStruct((T, D), F32), jax.ShapeDtypeStruct((8, D), F32)],
        out_specs=[pl.BlockSpec((TMI, D), row), pl.BlockSpec((8, D), lambda i, j: (0, 0))],
        scratch_shapes=[pltpu.VMEM((TMI, D), F32)])


def _row_tile(n, want, mult):
    for t in range(min(want, n), 0, -1):
        if n % t == 0 and t % mult == 0:
            return t
    return n


def _sum_slots(recv, name):
    ns, rows, cols = recv.shape
    tr = _row_tile(rows, 1024, 16)

    def body(r_ref, o_ref):
        s = r_ref[0].astype(F32)
        for k in range(1, ns):
            s = s + r_ref[k].astype(F32)
        o_ref[...] = s

    return _call(
        body, name=name, grid=(rows // tr,), args=[recv],
        in_specs=[pl.BlockSpec((ns, tr, cols), lambda i: (0, i, 0))],
        out_shape=[jax.ShapeDtypeStruct((rows, cols), F32)],
        out_specs=[pl.BlockSpec((tr, cols), lambda i: (i, 0))])[0]


def _pack_small(s_ffn1, s_in, s_mix, s_ffn2, s_final, dwa, dwb):
    def body(f1, mi, mo, f2, fl, wa_ref, wb_ref, v_ref, k_ref):
        for dst, (ref, row) in enumerate(((f1, 0), (mi, 0), (mo, 0), (mo, 1), (mo, 2), (f2, 0), (fl, 0), (fl, 1))):
            v_ref[dst:dst + 1, :] = ref[row:row + 1, :]
        for k in range(NDEV):
            k_ref[k, 0:32, :] = wa_ref[:, k * LANE:(k + 1) * LANE]
            k_ref[k, 32:40, :] = wb_ref[:, k * LANE:(k + 1) * LANE]

    return pl.pallas_call(
        body, name="pack_small",
        out_shape=(jax.ShapeDtypeStruct((8, D), F32), jax.ShapeDtypeStruct((NDEV, 40, LANE), F32)),
    )(s_ffn1, s_in, s_mix, s_ffn2, s_final, dwa, dwb)


def _sum_small(vecs, convs):
    def body(v_ref, k_ref, vs_ref, ks_ref, l_ref):
        s, c = v_ref[0], k_ref[0]
        for k in range(1, NDEV):
            s = s + v_ref[k]
            c = c + k_ref[k]
        vs_ref[...] = s
        ks_ref[...] = c
        l_ref[...] = jnp.broadcast_to(jnp.sum(s[7:8, :], axis=-1, keepdims=True), (8, LANE))

    return pl.pallas_call(
        body, name="sum_small",
        out_shape=(jax.ShapeDtypeStruct((8, D), F32), jax.ShapeDtypeStruct((40, LANE), F32),
                   jax.ShapeDtypeStruct((8, LANE), F32)),
    )(vecs, convs)


def _adam(gs, ws, ms, vs, name, comm=None):
    n = len(gs)
    rows, cols = ws[0].shape
    tr = _row_tile(rows, 256, 8)
    c1 = 1.0 - ADAM_B1 ** ADAM_STEP
    c2 = 1.0 - ADAM_B2 ** ADAM_STEP

    def body(*refs):
        for i in range(n):
            g, w, m, v = (refs[4 * i + k][...] for k in range(4))
            d_ref, m_ref, v_ref = refs[4 * n + 3 * i: 4 * n + 3 * i + 3]
            m2 = ADAM_B1 * m + (1.0 - ADAM_B1) * g
            v2 = ADAM_B2 * v + (1.0 - ADAM_B2) * (g * g)
            d_ref[...] = -ADAM_LR * ((m2 / c1) / (jnp.sqrt(v2 / c2) + ADAM_EPS) + ADAM_WD * w)
            m_ref[...] = m2
            v_ref[...] = v2

    spec = pl.BlockSpec((tr, cols), lambda i: (i, 0))
    args = []
    for i in range(n):
        args += [gs[i], ws[i], ms[i], vs[i]]
    outs = _call(body, name=name, grid=(rows // tr,), args=args, comm=comm, in_specs=[spec] * (4 * n),
                 out_shape=[jax.ShapeDtypeStruct((rows, cols), F32)] * (3 * n), out_specs=[spec] * (3 * n))
    return [tuple(outs[3 * i: 3 * i + 3]) for i in range(n)], outs[3 * n:]


def kernel(x, ffn1_norm, ffn1_w_gate, ffn1_w_up, ffn1_w_down, mix_norm, w_in, a_dw_w, a_dw_b, a_ln_g, a_ln_b, a_w_out, b_conv_w, b_w_out, w_o, ffn2_norm, ffn2_w_gate, ffn2_w_up, ffn2_w_down, final_norm, loss_target, m_ffn1_norm, m_ffn1_w_gate, m_ffn1_w_up, m_ffn1_w_down, m_mix_norm, m_w_in, m_a_dw_w, m_a_dw_b, m_a_ln_g, m_a_ln_b, m_a_w_out, m_b_conv_w, m_b_w_out, m_w_o, m_ffn2_norm, m_ffn2_w_gate, m_ffn2_w_up, m_ffn2_w_down, m_final_norm, v_ffn1_norm, v_ffn1_w_gate, v_ffn1_w_up, v_ffn1_w_down, v_mix_norm, v_w_in, v_a_dw_w, v_a_dw_b, v_a_ln_g, v_a_ln_b, v_a_w_out, v_b_conv_w, v_b_w_out, v_w_o, v_ffn2_norm, v_ffn2_w_gate, v_ffn2_w_up, v_ffn2_w_down, v_final_norm):
    names = ("ffn1_norm", "ffn1_w_gate", "ffn1_w_up", "ffn1_w_down", "mix_norm", "w_in", "a_dw_w", "a_dw_b",
             "a_ln_g", "a_ln_b", "a_w_out", "b_conv_w", "b_w_out", "w_o", "ffn2_norm", "ffn2_w_gate", "ffn2_w_up",
             "ffn2_w_down", "final_norm")
    w = dict(ffn1_norm=ffn1_norm, ffn1_w_gate=ffn1_w_gate, ffn1_w_up=ffn1_w_up, ffn1_w_down=ffn1_w_down,
             mix_norm=mix_norm, w_in=w_in, a_dw_w=a_dw_w, a_dw_b=a_dw_b, a_ln_g=a_ln_g, a_ln_b=a_ln_b,
             a_w_out=a_w_out, b_conv_w=b_conv_w, b_w_out=b_w_out, w_o=w_o, ffn2_norm=ffn2_norm,
             ffn2_w_gate=ffn2_w_gate, ffn2_w_up=ffn2_w_up, ffn2_w_down=ffn2_w_down, final_norm=final_norm)
    m = dict(ffn1_norm=m_ffn1_norm, ffn1_w_gate=m_ffn1_w_gate, ffn1_w_up=m_ffn1_w_up, ffn1_w_down=m_ffn1_w_down,
             mix_norm=m_mix_norm, w_in=m_w_in, a_dw_w=m_a_dw_w, a_dw_b=m_a_dw_b, a_ln_g=m_a_ln_g, a_ln_b=m_a_ln_b,
             a_w_out=m_a_w_out, b_conv_w=m_b_conv_w, b_w_out=m_b_w_out, w_o=m_w_o, ffn2_norm=m_ffn2_norm,
             ffn2_w_gate=m_ffn2_w_gate, ffn2_w_up=m_ffn2_w_up, ffn2_w_down=m_ffn2_w_down, final_norm=m_final_norm)
    v = dict(ffn1_norm=v_ffn1_norm, ffn1_w_gate=v_ffn1_w_gate, ffn1_w_up=v_ffn1_w_up, ffn1_w_down=v_ffn1_w_down,
             mix_norm=v_mix_norm, w_in=v_w_in, a_dw_w=v_a_dw_w, a_dw_b=v_a_dw_b, a_ln_g=v_a_ln_g, a_ln_b=v_a_ln_b,
             a_w_out=v_a_w_out, b_conv_w=v_b_conv_w, b_w_out=v_b_w_out, w_o=v_w_o, ffn2_norm=v_ffn2_norm,
             ffn2_w_gate=v_ffn2_w_gate, ffn2_w_up=v_ffn2_w_up, ffn2_w_down=v_ffn2_w_down, final_norm=v_final_norm)
    to_rows = dict(wg1=ffn1_w_gate[0].T, wu1=ffn1_w_up[0].T, wd1=ffn1_w_down[0], wg2=ffn2_w_gate[0].T,
                   wu2=ffn2_w_up[0].T, wd2=ffn2_w_down[0], win=w_in[0].T, wa=a_w_out[0], wb=b_w_out[0], wo=w_o[0])
    flat = jnp.concatenate([to_rows[n].astype(BF) for n in ORDER], axis=0)
    cw_shard = jnp.concatenate([a_dw_w[0], jnp.zeros((1, LANE), F32), b_conv_w[0], jnp.zeros((5, LANE), F32)], axis=0)

    x2, tgt = x[0], loss_target[0]
    st_a, st_b, st_c, st_d = ("wg1", "wu1", "wd1"), ("win",), ("wa", "wb", "wo", "wg2"), ("wu2", "wd2")

    buf_a, cw = _run_comm(_join(_ag_comm(st_a, flat), _direct_comm(cw_shard, False)), "ag_ffn1")
    h1, n1, gg1, uu1, buf_b = _ffn_fwd(x2, ffn1_norm, (buf_a,) * 3, (0, F, 2 * F), "ffn1_fwd", _ag_comm(st_b, flat))
    u, z, buf_c = _mix_in(h1, mix_norm, buf_b, _ag_comm(st_c, flat))
    a1, q, buf_d = _conv_fwd(z, cw, a_dw_b, _ag_comm(st_d, flat))
    h2, ya, yb = _mix_out(a1, q, z, h1, a_ln_g, a_ln_b, buf_c)
    ffn2_bufs, ffn2_offs = (buf_c, buf_d, buf_d), (3 * D, 0, F)
    h3, n2, gg2, uu2 = _ffn_fwd(h2, ffn2_norm, ffn2_bufs, ffn2_offs, "ffn2_fwd")
    dh3, s_final = _final_loss(h3, final_norm.reshape(1, D), tgt)

    tr_f = F // 2 if (F // 2) % LANE == 0 else F
    def pair(stage, src):
        return _rs_pair_comm(stage, src)

    def chip(stage, src, pair_buf, tag):
        return _rs_chip_comm(_pair_add(stage, src, pair_buf, "pair_add_" + tag))

    dgu2, act2 = _ffn_bwd_hidden(dh3, gg2, uu2, buf_d, F, "ffn2_bwd_h")
    (gu2,) = _tn_matmul(dgu2, n2, tr_f, "dw_gu2")
    s2a, src2a = ("wg2", "wu2"), dict(wg2=(gu2, 0), wu2=(gu2, F))
    gd2, pair2a = _tn_matmul(act2, dh3, tr_f, "dw_d2", pair(s2a, src2a), scale=0.5)
    s2b, src2b = ("wd2",), dict(wd2=(gd2, 0))
    dh2, s_ffn2, recv2a, pair2b = _ffn_bwd_input(dgu2, dh3, h2, ffn2_norm, (buf_c, buf_d), (3 * D, 0), "ffn2_bwd_x",
                                                 _join(chip(s2a, src2a, pair2a, "2a"), pair(s2b, src2b)))
    dzg, da1, dq, mb, a3b, dya, dyb, s_mix, recv2b = _mix_out_bwd(dh2, ya, yb, z, a1, a_ln_g, a_ln_b, buf_c,
                                                                   chip(s2b, src2b, pair2b, "2b"))
    (go,) = _tn_matmul(mb, dh2, D, "dw_o")
    (ga,) = _tn_matmul(a3b, dya, D, "dw_a")
    (gb,) = _tn_matmul(q.reshape(1, T, D), dyb, D, "dw_b")
    ssq, srcsq = ("wa", "wb", "wo"), dict(wa=(ga, 0), wb=(gb, 0), wo=(go, 0))
    dz, dwa, dwb, pairsq = _conv_bwd(z, da1, dq, dzg, cw, pair(ssq, srcsq))
    gin, recvsq = _tn_matmul(dz, u, D, "dw_in", chip(ssq, srcsq, pairsq, "sq"))
    sin, srcin = ("win",), dict(win=(gin, 0))
    dh1, s_in, pairin = _mix_in_bwd(dz, dh2, h1, mix_norm, buf_b, pair(sin, srcin))
    dgu1, act1 = _ffn_bwd_hidden(dh1, gg1, uu1, buf_a, 2 * F, "ffn1_bwd_h")
    gu1, recvin = _tn_matmul(dgu1, n1, tr_f, "dw_gu1", chip(sin, srcin, pairin, "in"))
    s1a, src1a = ("wg1", "wu1"), dict(wg1=(gu1, 0), wu1=(gu1, F))
    gd1, pair1a = _tn_matmul(act1, dh1, tr_f, "dw_d1", pair(s1a, src1a), scale=0.5)
    s1b, src1b = ("wd1",), dict(wd1=(gd1, 0))
    dx, s_ffn1, recv1a, pair1b = _ffn_bwd_input(dgu1, dh1, x2, ffn1_norm, (buf_a, buf_a), (0, F), "ffn1_bwd_x",
                                                _join(chip(s1a, src1a, pair1a, "1a"), pair(s1b, src1b)))
    (recv1b,) = _run_comm(chip(s1b, src1b, pair1b, "1b"), "rs_chip_1b")
    stages = ((s2a, recv2a, "2a"), (s2b, recv2b, "2b"), (ssq, recvsq, "sq"), (sin, recvin, "in"),
              (s1a, recv1a, "1a"), (s1b, recv1b, "1b"))

    gsum = {}
    for stage, recv, tag in stages:
        st, total = _Stage(stage), _sum_slots(recv, "sum_" + tag)
        for n in stage:
            gsum[n] = total[st.off[n]:st.off[n] + st.rows[n]]

    vec8, convk = _pack_small(s_ffn1, s_in, s_mix, s_ffn2, s_final, dwa, dwb)
    vec_all, conv_all = _run_comm(_join(_direct_comm(vec8, False), _direct_comm(convk, True)), "xchg_small")
    vec_sum, conv_sum, loss_blk = _sum_small(vec_all, conv_all)
    loss = loss_blk[0, 0]

    g = dict(ffn1_w_gate=gsum["wg1"], ffn1_w_up=gsum["wu1"], ffn1_w_down=gsum["wd1"],
             ffn2_w_gate=gsum["wg2"], ffn2_w_up=gsum["wu2"], ffn2_w_down=gsum["wd2"], w_in=gsum["win"].T,
             a_w_out=gsum["wa"], b_w_out=gsum["wb"], w_o=gsum["wo"],
             ffn1_norm=vec_sum[0:1], mix_norm=vec_sum[1:2], a_ln_g=vec_sum[2:3], a_ln_b=vec_sum[3:4],
             a_dw_b=vec_sum[4:5], ffn2_norm=vec_sum[5:6], final_norm=vec_sum[6:7],
             a_dw_w=conv_sum[0:KA], b_conv_w=conv_sum[32:32 + KB])
    gate_up = ("ffn1_w_gate", "ffn1_w_up", "ffn2_w_gate", "ffn2_w_up")

    upd = {}

    def run(group, name, as2d=lambda a: a[0], back=lambda a, n: a.reshape(w[n].shape)):
        res, _ = _adam([g[n] for n in group], [as2d(w[n]) for n in group], [as2d(m[n]) for n in group],
                       [as2d(v[n]) for n in group], name)
        for n, r in zip(group, res):
            upd[n] = tuple(back(a, n) for a in r)

    run(gate_up, "adam_gate_up", as2d=lambda a: a[0].T, back=lambda a, n: a.T[None])
    for n in gate_up:
        g[n] = g[n].T
    run(("ffn1_w_down", "ffn2_w_down"), "adam_down")
    run(("w_in",), "adam_in")
    run(("a_w_out", "b_w_out", "w_o"), "adam_square")
    run(("a_dw_w",), "adam_dw")
    run(("b_conv_w",), "adam_conv")
    vecs = ("ffn1_norm", "mix_norm", "a_dw_b", "a_ln_g", "a_ln_b", "ffn2_norm", "final_norm")
    run(vecs, "adam_vec", as2d=lambda a: a.reshape(1, D))

    grads = [g[n].reshape(w[n].shape) for n in names]
    return (loss, dx.reshape(x.shape), *grads, *[upd[n][0] for n in names], *[upd[n][1] for n in names],
            *[upd[n][2] for n in names])
```

```python
import jax
import jax.numpy as jnp
from jax import lax
from jax.experimental import pallas as pl
from jax.experimental.pallas import tpu as pltpu

T = 4096
D = 1024
F = 2816
NG = 7
NDEV = 8
NCHIP = 4
KA, KB = 31, 3
EPS = 1e-6
ADAM_LR, ADAM_B1, ADAM_B2, ADAM_EPS, ADAM_WD, ADAM_STEP = 0.001, 0.9, 0.999, 1e-08, 0.01, 10

TM = 512
TMI = 1024
FC = 256
TB = 1024
CH = 128
HALO = 32
LANE = 128
TK = 1024
VMEM_LIMIT = 56 * 1024 * 1024

BF = jnp.bfloat16
F32 = jnp.float32
MESH = pl.DeviceIdType.MESH
ANY = pl.BlockSpec(memory_space=pl.ANY)

ORDER = ("wg1", "wu1", "wd1", "wg2", "wu2", "wd2", "win", "wa", "wb", "wo")


class _Layout:
    def __init__(self):
        fs, dis, ds = F // NDEV, NG * D // NDEV, D // NDEV
        self.rows = dict(wg1=fs, wu1=fs, wd1=fs, wg2=fs, wu2=fs, wd2=fs, win=dis, wa=ds, wb=ds, wo=ds)
        self.fl, off = {}, 0
        for n in ORDER:
            self.fl[n] = off
            off += self.rows[n]
        self.RT = off


class _Stage:
    def __init__(self, names):
        lay = _Layout()
        self.names = names
        self.rows, self.full, self.sub, self.fl = {}, {}, {}, {}
        for n in names:
            base, i, k = (n.split("/") + ["0", "1"])[:3]
            self.full[n] = lay.rows[base]
            self.rows[n] = lay.rows[base] // int(k)
            self.sub[n] = int(i) * self.rows[n]
            self.fl[n] = lay.fl[base] + self.sub[n]
        self.off, self.wc, o, w = {}, {}, 0, 0
        for n in names:
            self.off[n], self.wc[n] = o, w
            o += self.rows[n]
            w += NDEV * self.rows[n]
        self.R, self.W = o, w

    def grad_row(self, n, first, dev_lin):
        return first + dev_lin * self.full[n] + self.sub[n]


def _nt(a, b):
    return lax.dot_general(a, b, (((1,), (1,)), ((), ())), preferred_element_type=F32)


def _nn(a, b):
    return lax.dot_general(a, b, (((1,), (0,)), ((), ())), preferred_element_type=F32)


def _tn(a, b):
    return lax.dot_general(a, b, (((0,), (0,)), ((), ())), preferred_element_type=F32)


def _sig(x):
    return 1.0 / (1.0 + jnp.exp(-x))


def _position():
    return lax.axis_index("x"), lax.axis_index("y"), lax.axis_index("c")


def _peer(pos, j):
    x, y, c = pos
    return (1 - x if j & 4 else x, 1 - y if j & 2 else y, 1 - c if j & 1 else c)


def _lin(pos):
    return 4 * pos[0] + 2 * pos[1] + pos[2]


def _chip(pos):
    return 2 * pos[0] + pos[1]


class _Comm:
    def __init__(self, inputs, out_shapes, scratch, start, finish):
        self.inputs, self.out_shapes, self.scratch, self.start, self.finish = inputs, out_shapes, scratch, start, finish


def _call(body, *, name, grid, args, in_specs, out_shape, out_specs, scratch_shapes=(), comm=None,
          num_scalar_prefetch=0):
    in_specs, out_shape, out_specs, scratch_shapes = list(in_specs), list(out_shape), list(out_specs), list(scratch_shapes)
    n_in, n_out, n_scr = len(in_specs), len(out_shape), len(scratch_shapes)
    sp = num_scalar_prefetch
    if comm is None:
        kernel_fn = lambda *refs: body(*refs)
        c_in = c_out = c_scr = 0
    else:
        c_in, c_out, c_scr = len(comm.inputs), len(comm.out_shapes), len(comm.scratch)

        def kernel_fn(*refs):
            pre, refs = refs[:sp], refs[sp:]
            ins, cins = refs[:n_in], refs[n_in:n_in + c_in]
            o0 = n_in + c_in
            outs, couts = refs[o0:o0 + n_out], refs[o0 + n_out:o0 + n_out + c_out]
            s0 = o0 + n_out + c_out
            scr, cscr = refs[s0:s0 + n_scr], refs[s0 + n_scr:]
            first = pl.program_id(0) == 0
            last = pl.program_id(0) == grid[0] - 1
            for a in range(1, len(grid)):
                first = first & (pl.program_id(a) == 0)
                last = last & (pl.program_id(a) == grid[a] - 1)

            @pl.when(first)
            def _():
                comm.start(cins, couts, cscr)

            body(*pre, *ins, *outs, *scr)

            @pl.when(last)
            def _():
                comm.finish(cins, couts, cscr)

        args = list(args) + list(comm.inputs)
        in_specs += [ANY] * c_in
        out_shape += list(comm.out_shapes)
        out_specs += [ANY] * c_out
        scratch_shapes += list(comm.scratch)
    params = pltpu.CompilerParams(dimension_semantics=("arbitrary",) * len(grid), vmem_limit_bytes=VMEM_LIMIT)
    if sp:
        grid_spec = pltpu.PrefetchScalarGridSpec(num_scalar_prefetch=sp, grid=grid, in_specs=in_specs,
                                                 out_specs=out_specs, scratch_shapes=scratch_shapes)
        return pl.pallas_call(kernel_fn, name=name, grid_spec=grid_spec, out_shape=out_shape,
                              compiler_params=params)(*args)
    return pl.pallas_call(kernel_fn, name=name, grid=grid, in_specs=in_specs, out_shape=out_shape, out_specs=out_specs,
                          scratch_shapes=scratch_shapes, compiler_params=params)(*args)


def _join(a, b):
    na = (len(a.inputs), len(a.out_shapes), len(a.scratch))

    def split(refs):
        return ([r[:n] for r, n in zip(refs, na)], [r[n:] for r, n in zip(refs, na)])

    def start(*refs):
        ra, rb = split(refs)
        a.start(*ra)
        b.start(*rb)

    def finish(*refs):
        ra, rb = split(refs)
        a.finish(*ra)
        b.finish(*rb)

    return _Comm(list(a.inputs) + list(b.inputs), list(a.out_shapes) + list(b.out_shapes),
                 list(a.scratch) + list(b.scratch), start, finish)


def _run_comm(comm, name):
    def body(*refs):
        c_in, c_out = len(comm.inputs), len(comm.out_shapes)
        comm.start(refs[:c_in], refs[c_in:c_in + c_out], refs[c_in + c_out:])
        comm.finish(refs[:c_in], refs[c_in:c_in + c_out], refs[c_in + c_out:])

    return pl.pallas_call(
        body, name=name, out_shape=list(comm.out_shapes), in_specs=[ANY] * len(comm.inputs),
        out_specs=[ANY] * len(comm.out_shapes), scratch_shapes=list(comm.scratch))(*comm.inputs)


def _ag_comm(names, flat):
    st = _Stage(names)

    def parts(refs):
        (flat_ref,), (out_ref,), (send_sems, recv_sems, local_sem) = refs
        me = _position()

        def region(name, dev):
            r = st.rows[name]
            return out_ref.at[pl.ds(st.wc[name] + _lin(dev) * r, r), :]

        def own(name):
            return flat_ref.at[pl.ds(st.fl[name], st.rows[name]), :]

        def copies(k, dev, to, from_flat):
            return [pltpu.make_async_remote_copy(
                src_ref=own(n) if from_flat else region(n, dev), dst_ref=region(n, dev), send_sem=send_sems.at[k],
                recv_sem=recv_sems.at[k], device_id=to, device_id_type=MESH) for n in names]

        def whole(k):
            return pltpu.make_async_remote_copy(
                src_ref=flat_ref.at[pl.ds(0, st.R), :], dst_ref=out_ref.at[pl.ds(0, st.R), :],
                send_sem=send_sems.at[k], recv_sem=recv_sems.at[k], device_id=me, device_id_type=MESH)

        return me, region, own, copies, whole, flat_ref, out_ref, local_sem

    def start(*refs):
        me, region, own, copies, _, _, _, local_sem = parts(refs)
        for n in names:
            pltpu.make_async_copy(own(n), region(n, me), local_sem).start()
        for cp in copies(0, me, _peer(me, 1), True):
            cp.start()
        for j, bits in enumerate((4, 2, 6)):
            for cp in copies(1 + j, me, _peer(me, bits), True):
                cp.start()

    def finish(*refs):
        me, _, _, copies, whole, flat_ref, out_ref, local_sem = parts(refs)
        for j, bits in enumerate((4, 2, 6)):
            whole(1 + j).wait_recv()
            for cp in copies(4 + j, _peer(me, bits), _peer(me, 1), False):
                cp.start()
        whole(0).wait_recv()
        for j in range(3):
            whole(4 + j).wait_recv()
        for k in range(7):
            whole(k).wait_send()
        pltpu.make_async_copy(flat_ref.at[pl.ds(0, st.R), :], out_ref.at[pl.ds(0, st.R), :], local_sem).wait()

    return _Comm([flat], [jax.ShapeDtypeStruct((st.W, D), BF)],
                 [pltpu.SemaphoreType.DMA((7,)), pltpu.SemaphoreType.DMA((7,)), pltpu.SemaphoreType.DMA],
                 start, finish)


def _rs_pair_comm(names, src):
    st = _Stage(names)
    arrays = []
    for n in names:
        if not any(src[n][0] is a for a in arrays):
            arrays.append(src[n][0])
    idx = {n: [i for i, a in enumerate(arrays) if a is src[n][0]][0] for n in names}

    def slot_wait(refs):
        recv = refs[1][0]
        send_sem, recv_sem = refs[2]
        return pltpu.make_async_remote_copy(src_ref=recv, dst_ref=recv, send_sem=send_sem, recv_sem=recv_sem,
                                            device_id=_position(), device_id_type=MESH)

    def start(*refs):
        ins, (recv,), (send_sem, recv_sem) = refs
        me = _position()
        sib = _peer(me, 1)
        for q in range(NCHIP):
            dev = (q // 2, q % 2, sib[2])
            for n in names:
                r = st.rows[n]
                pltpu.make_async_remote_copy(
                    src_ref=ins[idx[n]].at[pl.ds(st.grad_row(n, src[n][1], _lin(dev)), r), :],
                    dst_ref=recv.at[q, pl.ds(st.off[n], r), :], send_sem=send_sem, recv_sem=recv_sem,
                    device_id=sib, device_id_type=MESH).start()

    def finish(*refs):
        w = slot_wait(refs)
        w.wait_recv()
        w.wait_send()

    return _Comm(arrays, [jax.ShapeDtypeStruct((NCHIP, st.R, D), BF)],
                 [pltpu.SemaphoreType.DMA, pltpu.SemaphoreType.DMA], start, finish)


def _pair_add(names, src, recv, name):
    st = _Stage(names)
    c_arr = jnp.reshape(lax.axis_index("c"), (1,)).astype(jnp.int32)

    def body(c_ref, *refs):
        r_ref, o_ref = refs[len(names)], refs[len(names) + 1]
        for a_ref, n in zip(refs, names):
            rows = slice(st.off[n], st.off[n] + st.rows[n])
            o_ref[rows, :] = (a_ref[...].astype(F32) + r_ref[rows, :].astype(F32)).astype(BF)

    def shard_spec(n):
        r = st.rows[n]
        base, step = st.grad_row(n, src[n][1], 0) // r, st.full[n] // r
        return pl.BlockSpec((r, D), lambda q, c_ref: (base + step * (2 * q + c_ref[0]), 0))

    slot = pl.BlockSpec((None, st.R, D), lambda q, c_ref: (q, 0, 0))
    return _call(body, name=name, grid=(NCHIP,), args=[c_arr] + [src[n][0] for n in names] + [recv],
                 in_specs=[shard_spec(n) for n in names] + [slot],
                 out_shape=[jax.ShapeDtypeStruct((NCHIP, st.R, D), BF)], out_specs=[slot], num_scalar_prefetch=1)[0]


def _rs_chip_comm(part):
    def copies(refs):
        (p_ref,), (recv,), (send_sems, recv_sems, local_sem) = refs
        me = _position()
        mine = pltpu.make_async_copy(p_ref.at[_chip(me)], recv.at[_chip(me)], local_sem)
        out = []
        for j, bits in enumerate((4, 2, 6)):
            to = _peer(me, bits)
            out.append(pltpu.make_async_remote_copy(
                src_ref=p_ref.at[_chip(to)], dst_ref=recv.at[_chip(me)], send_sem=send_sems.at[j],
                recv_sem=recv_sems.at[j], device_id=to, device_id_type=MESH))
        return mine, out

    def start(*refs):
        mine, out = copies(refs)
        mine.start()
        for cp in out:
            cp.start()

    def finish(*refs):
        mine, out = copies(refs)
        for cp in out:
            cp.wait_recv()
        for cp in out:
            cp.wait_send()
        mine.wait()

    return _Comm([part], [jax.ShapeDtypeStruct(part.shape, BF)],
                 [pltpu.SemaphoreType.DMA((3,)), pltpu.SemaphoreType.DMA((3,)), pltpu.SemaphoreType.DMA],
                 start, finish)


def _direct_comm(x, scatter):
    def copies(refs):
        (x_ref,), (out_ref,), (send_sems, recv_sems, local_sem) = refs
        me = _position()

        def piece(dev):
            return x_ref.at[_lin(dev)] if scatter else x_ref

        mine = pltpu.make_async_copy(piece(me), out_ref.at[_lin(me)], local_sem)
        return mine, [pltpu.make_async_remote_copy(
            src_ref=piece(_peer(me, j)), dst_ref=out_ref.at[_lin(me)], send_sem=send_sems.at[j - 1],
            recv_sem=recv_sems.at[j - 1], device_id=_peer(me, j), device_id_type=MESH) for j in range(1, NDEV)]

    def start(*refs):
        mine, cps = copies(refs)
        mine.start()
        for cp in cps:
            cp.start()

    def finish(*refs):
        mine, cps = copies(refs)
        for cp in cps:
            cp.wait_recv()
        for cp in cps:
            cp.wait_send()
        mine.wait()

    shape = x.shape if scatter else (NDEV,) + x.shape
    return _Comm([x], [jax.ShapeDtypeStruct(shape, x.dtype)],
                 [pltpu.SemaphoreType.DMA((7,)), pltpu.SemaphoreType.DMA((7,)), pltpu.SemaphoreType.DMA],
                 start, finish)


def _pack_weights(shards):
    lay = _Layout()

    def body(*refs):
        o_ref = refs[-1]
        for ref, n in zip(refs, ORDER):
            x = ref[...].T if n == "win" else ref[...]
            o_ref[lay.fl[n]:lay.fl[n] + lay.rows[n], :] = x.astype(BF)

    return pl.pallas_call(
        body, name="pack_weights", out_shape=jax.ShapeDtypeStruct((lay.RT, D), BF),
        compiler_params=pltpu.CompilerParams(vmem_limit_bytes=VMEM_LIMIT))(*[shards[n] for n in ORDER])


def _load_ffn_weights(srcs, offs, scratch, sem):
    @pl.when(pl.program_id(0) == 0)
    def _():
        cps = [pltpu.make_async_copy(s.at[pl.ds(off, F), :], dst, sem.at[i])
               for i, (s, off, dst) in enumerate(zip(srcs, offs, scratch))]
        for cp in cps:
            cp.start()
        for cp in cps:
            cp.wait()


def _ffn_fwd(x, g, wbufs, offs, name, comm=None):
    nf = F // FC

    def body(x_ref, g_ref, b0, b1, b2, h_ref, n_ref, gg_ref, uu_ref, wg_s, wu_s, wd_s, sem):
        _load_ffn_weights((b0, b1, b2), offs, (wg_s, wu_s, wd_s), sem)
        xf = x_ref[...]
        r = lax.rsqrt(jnp.mean(xf * xf, axis=-1, keepdims=True) + EPS)
        nb = (xf * r * g_ref[...]).astype(BF)
        n_ref[...] = nb
        acc = jnp.zeros((TM, D), F32)
        for c in range(nf):
            sl = slice(c * FC, (c + 1) * FC)
            gc = _nt(nb, wg_s[sl, :])
            uc = _nt(nb, wu_s[sl, :])
            gg_ref[:, sl] = gc.astype(BF)
            uu_ref[:, sl] = uc.astype(BF)
            a = (0.5 * gc * _sig(gc)) * uc
            acc = acc + _nn(a.astype(BF), wd_s[sl, :])
        h_ref[...] = xf + acc

    row = lambda i: (i, 0)
    return _call(
        body, name=name, grid=(T // TM,), args=[x, g, *wbufs], comm=comm,
        in_specs=[pl.BlockSpec((TM, D), row), pl.BlockSpec((1, D), lambda i: (0, 0)), ANY, ANY, ANY],
        out_shape=[jax.ShapeDtypeStruct((T, D), F32), jax.ShapeDtypeStruct((T, D), BF),
                   jax.ShapeDtypeStruct((T, F), BF), jax.ShapeDtypeStruct((T, F), BF)],
        out_specs=[pl.BlockSpec((TM, D), row), pl.BlockSpec((TM, D), row),
                   pl.BlockSpec((TM, F), row), pl.BlockSpec((TM, F), row)],
        scratch_shapes=[pltpu.VMEM((F, D), BF)] * 3 + [pltpu.SemaphoreType.DMA((3,))])


def _mix_in(h1, gm, win, comm=None):
    def body(h_ref, g_ref, w_ref, u_ref, z_ref, u_s):
        @pl.when(pl.program_id(1) == 0)
        def _():
            xf = h_ref[...]
            r = lax.rsqrt(jnp.mean(xf * xf, axis=-1, keepdims=True) + EPS)
            ub = (xf * r * g_ref[...]).astype(BF)
            u_s[...] = ub
            u_ref[...] = ub
        z_ref[...] = _nt(u_s[...], w_ref[...])

    return _call(
        body, name="mix_in", grid=(T // TMI, NG), args=[h1, gm, win], comm=comm,
        in_specs=[pl.BlockSpec((TMI, D), lambda i, j: (i, 0)), pl.BlockSpec((1, D), lambda i, j: (0, 0)),
                  pl.BlockSpec((D, D), lambda i, j: (j, 0))],
        out_shape=[jax.ShapeDtypeStruct((T, D), BF), jax.ShapeDtypeStruct((NG, T, D), F32)],
        out_specs=[pl.BlockSpec((TMI, D), lambda i, j: (i, 0)), pl.BlockSpec((None, TMI, D), lambda i, j: (j, i, 0))],
        scratch_shapes=[pltpu.VMEM((TMI, D), BF)])


def _shift_up(w, b):
    return w if b == 0 else pltpu.roll(w, w.shape[0] - b, 0)


def _fold8(p):
    red = p[0:8, :]
    for i in range(1, p.shape[0] // 8):
        red = red + p[8 * i:8 * i + 8, :]
    return red


def _conv_fwd(z, cw, bias, comm=None):
    nt = T // TB
    hb = TB // HALO

    def body(z_ref, zh_ref, cw_ref, b_ref, a1_ref, q_ref, apad, ppad):
        first = pl.program_id(1) == 0
        apad[0:HALO, :] = jnp.where(first, 0.0, zh_ref[0] * _sig(zh_ref[1]))
        apad[HALO:, :] = z_ref[0] * _sig(z_ref[1])
        ppad[0:HALO, :] = jnp.where(first, 0.0, zh_ref[3] * zh_ref[4])
        ppad[HALO:, :] = z_ref[3] * z_ref[4]
        bias_row = b_ref[...]

        def chunk(r, carry):
            base = pl.multiple_of(r * CH, CH)
            w = apad[pl.ds(base, CH + HALO), :]
            acc = jnp.broadcast_to(bias_row, (CH, LANE))
            for b in range(8):
                wb = _shift_up(w, b)
                for a in range(5):
                    s = 8 * a + b
                    if 2 <= s <= HALO:
                        acc = acc + cw_ref[pl.ds(s - 2, 1), :] * wb[8 * a:8 * a + CH, :]
            a1_ref[pl.ds(base, CH), :] = acc
            pw = ppad[pl.ds(base, CH + HALO), :]
            v = (cw_ref[pl.ds(32, 1), :] * _shift_up(pw, 6)[24:24 + CH, :]
                 + cw_ref[pl.ds(33, 1), :] * _shift_up(pw, 7)[24:24 + CH, :]
                 + cw_ref[pl.ds(34, 1), :] * pw[32:32 + CH, :])
            q_ref[pl.ds(base, CH), :] = (z_ref[2, pl.ds(base, CH), :] * v).astype(BF)
            return carry

        lax.fori_loop(0, TB // CH, chunk, 0)

    return _call(
        body, name="conv_fwd", grid=(D // LANE, nt), args=[z, z, cw, bias], comm=comm,
        in_specs=[pl.BlockSpec((5, TB, LANE), lambda c, t: (0, t, c)),
                  pl.BlockSpec((5, HALO, LANE), lambda c, t: (0, jnp.maximum(t * hb - 1, 0), c)),
                  pl.BlockSpec((None, 40, LANE), lambda c, t: (c, 0, 0)),
                  pl.BlockSpec((1, LANE), lambda c, t: (0, c))],
        out_shape=[jax.ShapeDtypeStruct((T, D), F32), jax.ShapeDtypeStruct((T, D), BF)],
        out_specs=[pl.BlockSpec((TB, LANE), lambda c, t: (t, c)), pl.BlockSpec((TB, LANE), lambda c, t: (t, c))],
        scratch_shapes=[pltpu.VMEM((TB + HALO, LANE), F32), pltpu.VMEM((TB + HALO, LANE), F32)])


def _layernorm_silu(a1, lng, lnb):
    mu = jnp.mean(a1, axis=-1, keepdims=True)
    xc = a1 - mu
    rs = lax.rsqrt(jnp.mean(xc * xc, axis=-1, keepdims=True) + EPS)
    xh = xc * rs
    a2 = xh * lng + lnb
    sg = _sig(a2)
    return xh, rs, a2, sg


def _square_specs(blocks):
    return [pl.BlockSpec((D, D), lambda i, b=b: (b, 0)) for b in blocks]


def _mix_out(a1, q, z, h1, lng, lnb, wsq, comm=None):
    def body(a1_ref, q_ref, ga_ref, gb_ref, h_ref, lng_ref, lnb_ref, wa_ref, wb_ref, wo_ref, h2_ref, ya_ref, yb_ref):
        _, _, a2, sg = _layernorm_silu(a1_ref[...], lng_ref[...], lnb_ref[...])
        ya = _nn((a2 * sg).astype(BF), wa_ref[...])
        yb = _nn(q_ref[...], wb_ref[...])
        ya_ref[...] = ya
        yb_ref[...] = yb
        m = _sig(ga_ref[...]) * ya + _sig(gb_ref[...]) * yb
        h2_ref[...] = h_ref[...] + _nn(m.astype(BF), wo_ref[...])

    row = lambda i: (i, 0)
    vec = pl.BlockSpec((1, D), lambda i: (0, 0))
    return _call(
        body, name="mix_out", grid=(T // TM,), args=[a1, q, z, z, h1, lng, lnb, wsq, wsq, wsq], comm=comm,
        in_specs=[pl.BlockSpec((TM, D), row), pl.BlockSpec((TM, D), row),
                  pl.BlockSpec((None, TM, D), lambda i: (5, i, 0)), pl.BlockSpec((None, TM, D), lambda i: (6, i, 0)),
                  pl.BlockSpec((TM, D), row), vec, vec] + _square_specs((0, 1, 2)),
        out_shape=[jax.ShapeDtypeStruct((T, D), F32)] * 3,
        out_specs=[pl.BlockSpec((TM, D), row)] * 3)


def _final_loss(h3, gf, tgt, comm=None):
    def body(h_ref, g_ref, t_ref, dh_ref, s_ref):
        @pl.when(pl.program_id(0) == 0)
        def _():
            s_ref[...] = jnp.zeros_like(s_ref)
        xf = h_ref[...]
        g = g_ref[...]
        r = lax.rsqrt(jnp.mean(xf * xf, axis=-1, keepdims=True) + EPS)
        xr = xf * r
        e = xr * g - t_ref[...]
        s_ref[1:2, :] += jnp.sum(e * e, axis=0, keepdims=True) * (0.5 / D)
        dy = e * (1.0 / D)
        s_ref[0:1, :] += jnp.sum(dy * xr, axis=0, keepdims=True)
        gdy = dy * g
        dh_ref[...] = r * gdy - xr * (r * jnp.mean(gdy * xr, axis=-1, keepdims=True))

    row = lambda i: (i, 0)
    return _call(
        body, name="final_loss", grid=(T // TM,), args=[h3, gf, tgt], comm=comm,
        in_specs=[pl.BlockSpec((TM, D), row), pl.BlockSpec((1, D), lambda i: (0, 0)), pl.BlockSpec((TM, D), row)],
        out_shape=[jax.ShapeDtypeStruct((T, D), F32), jax.ShapeDtypeStruct((8, D), F32)],
        out_specs=[pl.BlockSpec((TM, D), row), pl.BlockSpec((8, D), lambda i: (0, 0))])


def _rmsnorm_bwd(xf, g, dn):
    r = lax.rsqrt(jnp.mean(xf * xf, axis=-1, keepdims=True) + EPS)
    xr = xf * r
    gdn = dn * g
    dx = r * gdn - xr * (r * jnp.mean(gdn * xr, axis=-1, keepdims=True))
    return dx, jnp.sum(dn * xr, axis=0, keepdims=True)


def _ffn_bwd_hidden(dh, gg, uu, wbuf, off, name, comm=None):
    nf = F // FC

    def body(dh_ref, gg_ref, uu_ref, b0, dgu_ref, a_ref, wd_s, sem):
        _load_ffn_weights((b0,), (off,), (wd_s,), sem)
        dhb = (0.5 * dh_ref[...]).astype(BF)
        for c in range(nf):
            sl = slice(c * FC, (c + 1) * FC)
            da = _nt(dhb, wd_s[sl, :])
            gc = gg_ref[:, sl].astype(F32)
            uc = uu_ref[:, sl].astype(F32)
            sg = _sig(gc)
            silu = gc * sg
            dgu_ref[0, :, sl] = ((da * uc) * (sg * (1.0 + gc * (1.0 - sg)))).astype(BF)
            dgu_ref[1, :, sl] = (da * silu).astype(BF)
            a_ref[0, :, sl] = (silu * uc).astype(BF)

    row = lambda i: (i, 0)
    return _call(
        body, name=name, grid=(T // TM,), args=[dh, gg, uu, wbuf], comm=comm,
        in_specs=[pl.BlockSpec((TM, D), row), pl.BlockSpec((TM, F), row), pl.BlockSpec((TM, F), row), ANY],
        out_shape=[jax.ShapeDtypeStruct((2, T, F), BF), jax.ShapeDtypeStruct((1, T, F), BF)],
        out_specs=[pl.BlockSpec((2, TM, F), lambda i: (0, i, 0)), pl.BlockSpec((1, TM, F), lambda i: (0, i, 0))],
        scratch_shapes=[pltpu.VMEM((F, D), BF), pltpu.SemaphoreType.DMA((1,))])


def _ffn_bwd_input(dgu, dh, x, g, wbufs, offs, name, comm=None):
    def body(dgu_ref, dh_ref, x_ref, g_ref, b0, b1, dx_ref, s_ref, wg_s, wu_s, sem):
        _load_ffn_weights((b0, b1), offs, (wg_s, wu_s), sem)

        @pl.when(pl.program_id(0) == 0)
        def _():
            s_ref[...] = jnp.zeros_like(s_ref)

        dn = _nn(dgu_ref[0], wg_s[...]) + _nn(dgu_ref[1], wu_s[...])
        dxn, dg = _rmsnorm_bwd(x_ref[...], g_ref[...], dn)
        dx_ref[...] = dh_ref[...] + dxn
        s_ref[0:1, :] += dg

    row = lambda i: (i, 0)
    return _call(
        body, name=name, grid=(T // TM,), args=[dgu, dh, x, g, *wbufs], comm=comm,
        in_specs=[pl.BlockSpec((2, TM, F), lambda i: (0, i, 0)), pl.BlockSpec((TM, D), row),
                  pl.BlockSpec((TM, D), row), pl.BlockSpec((1, D), lambda i: (0, 0)), ANY, ANY],
        out_shape=[jax.ShapeDtypeStruct((T, D), F32), jax.ShapeDtypeStruct((8, D), F32)],
        out_specs=[pl.BlockSpec((TM, D), row), pl.BlockSpec((8, D), lambda i: (0, 0))],
        scratch_shapes=[pltpu.VMEM((F, D), BF)] * 2 + [pltpu.SemaphoreType.DMA((2,))])


def _tn_matmul(lhs, rhs, tr, name, comm=None, scale=None):
    ng, _, cdim = lhs.shape
    nc, nk = cdim // tr, T // TK

    def body(l_ref, r_ref, o_ref, acc):
        k = pl.program_id(2)

        @pl.when(k == 0)
        def _():
            acc[...] = jnp.zeros_like(acc)

        r = r_ref[...] if scale is None else scale * r_ref[...]
        acc[...] += _tn(l_ref[...], r.astype(BF))

        @pl.when(k == nk - 1)
        def _():
            o_ref[...] = acc[...].astype(BF)

    return _call(
        body, name=name, grid=(ng, nc, nk), args=[lhs, rhs], comm=comm,
        in_specs=[pl.BlockSpec((None, TK, tr), lambda g, c, k: (g, k, c)),
                  pl.BlockSpec((TK, D), lambda g, c, k: (k, 0))],
        out_shape=[jax.ShapeDtypeStruct((ng * cdim, D), BF)],
        out_specs=[pl.BlockSpec((tr, D), lambda g, c, k: (g * nc + c, 0))],
        scratch_shapes=[pltpu.VMEM((tr, D), F32)])


def _mix_out_bwd(dh2, ya, yb, z, a1, lng, lnb, wsq, comm=None):
    def body(dh_ref, ya_ref, yb_ref, ga_ref, gb_ref, a1_ref, lng_ref, lnb_ref, wa_ref, wb_ref, wo_ref,
             dzg_ref, da1_ref, dq_ref, m_ref, a3_ref, dya_ref, dyb_ref, s_ref):
        @pl.when(pl.program_id(0) == 0)
        def _():
            s_ref[...] = jnp.zeros_like(s_ref)

        dm = _nt(dh_ref[...].astype(BF), wo_ref[...])
        ya, yb = ya_ref[...], yb_ref[...]
        sa, sb = _sig(ga_ref[...]), _sig(gb_ref[...])
        m_ref[0] = (sa * ya + sb * yb).astype(BF)
        dzg_ref[0] = (dm * ya * (sa * (1.0 - sa))).astype(BF)
        dzg_ref[1] = (dm * yb * (sb * (1.0 - sb))).astype(BF)
        dya = (dm * sa).astype(BF)
        dyb = (dm * sb).astype(BF)
        dya_ref[...] = dya
        dyb_ref[...] = dyb
        dq_ref[...] = _nt(dyb, wb_ref[...])
        da3 = _nt(dya, wa_ref[...])
        lng = lng_ref[...]
        xh, rs, a2, sg = _layernorm_silu(a1_ref[...], lng, lnb_ref[...])
        a3_ref[0] = (a2 * sg).astype(BF)
        da2 = da3 * (sg * (1.0 + a2 * (1.0 - sg)))
        s_ref[0:1, :] += jnp.sum(da2 * xh, axis=0, keepdims=True)
        s_ref[1:2, :] += jnp.sum(da2, axis=0, keepdims=True)
        dxh = da2 * lng
        da1 = rs * (dxh - jnp.mean(dxh, axis=-1, keepdims=True) - xh * jnp.mean(dxh * xh, axis=-1, keepdims=True))
        da1_ref[...] = da1
        s_ref[2:3, :] += jnp.sum(da1, axis=0, keepdims=True)

    row = lambda i: (i, 0)
    row3 = lambda i: (0, i, 0)
    vec = pl.BlockSpec((1, D), lambda i: (0, 0))
    return _call(
        body, name="mix_out_bwd", grid=(T // TM,), args=[dh2, ya, yb, z, z, a1, lng, lnb, wsq, wsq, wsq], comm=comm,
        in_specs=[pl.BlockSpec((TM, D), row), pl.BlockSpec((TM, D), row), pl.BlockSpec((TM, D), row),
                  pl.BlockSpec((None, TM, D), lambda i: (5, i, 0)), pl.BlockSpec((None, TM, D), lambda i: (6, i, 0)),
                  pl.BlockSpec((TM, D), row), vec, vec] + _square_specs((0, 1, 2)),
        out_shape=[jax.ShapeDtypeStruct((2, T, D), BF), jax.ShapeDtypeStruct((T, D), F32),
                   jax.ShapeDtypeStruct((T, D), F32), jax.ShapeDtypeStruct((1, T, D), BF),
                   jax.ShapeDtypeStruct((1, T, D), BF), jax.ShapeDtypeStruct((T, D), BF),
                   jax.ShapeDtypeStruct((T, D), BF), jax.ShapeDtypeStruct((8, D), F32)],
        out_specs=[pl.BlockSpec((2, TM, D), row3), pl.BlockSpec((TM, D), row), pl.BlockSpec((TM, D), row),
                   pl.BlockSpec((1, TM, D), row3), pl.BlockSpec((1, TM, D), row3), pl.BlockSpec((TM, D), row),
                   pl.BlockSpec((TM, D), row), pl.BlockSpec((8, D), lambda i: (0, 0))])


def _conv_bwd(z, da1, dq, dzg, cw, comm=None):
    nt = T // TB
    hb = TB // HALO
    last_h = T // HALO - 1

    def body(z_ref, zp_ref, zn_ref, da1_ref, da1n_ref, dq_ref, dqn_ref, dzg_ref, cw_ref,
             dz_ref, dwa_ref, dwb_ref, apad, dypad, ppad, dvpad, acc_a, acc_b):
        t = pl.program_id(1)
        first, last = t == 0, t == nt - 1
        apad[0:HALO, :] = jnp.where(first, 0.0, zp_ref[0] * _sig(zp_ref[1]))
        apad[HALO:, :] = z_ref[0] * _sig(z_ref[1])
        ppad[0:HALO, :] = jnp.where(first, 0.0, zp_ref[3] * zp_ref[4])
        ppad[HALO:, :] = z_ref[3] * z_ref[4]
        dypad[0:TB, :] = da1_ref[...]
        dypad[TB:, :] = jnp.where(last, 0.0, da1n_ref[...])
        dvpad[0:TB, :] = dq_ref[...] * z_ref[2]
        dvpad[TB:, :] = jnp.where(last, 0.0, dqn_ref[...] * zn_ref[2])

        @pl.when(t == 0)
        def _():
            acc_a[...] = jnp.zeros_like(acc_a)
            acc_b[...] = jnp.zeros_like(acc_b)

        def chunk(r, carry):
            base = pl.multiple_of(r * CH, CH)
            rows = pl.ds(base, CH)
            dw_ = dypad[pl.ds(base, CH + HALO), :]
            da0 = jnp.zeros((CH, LANE), F32)
            for b in range(8):
                wb = _shift_up(dw_, b)
                for a in range(4):
                    o = 8 * a + b
                    if o <= KA - 1:
                        da0 = da0 + cw_ref[pl.ds(KA - 1 - o, 1), :] * wb[8 * a:8 * a + CH, :]
            z0, z1 = z_ref[0, rows, :], z_ref[1, rows, :]
            s1 = _sig(z1)
            dz_ref[0, rows, :] = (da0 * s1).astype(BF)
            dz_ref[1, rows, :] = (da0 * z0 * (s1 * (1.0 - s1))).astype(BF)
            dyc = dypad[rows, :]
            aw = apad[pl.ds(base, CH + HALO), :]
            for b in range(8):
                wb = _shift_up(aw, b)
                for a in range(5):
                    s = 8 * a + b
                    if 2 <= s <= HALO:
                        k8 = 8 * (s - 2)
                        acc_a[k8:k8 + 8, :] += _fold8(dyc * wb[8 * a:8 * a + CH, :])
            pw = ppad[pl.ds(base, CH + HALO), :]
            p6 = _shift_up(pw, 6)[24:24 + CH, :]
            p7 = _shift_up(pw, 7)[24:24 + CH, :]
            p8 = pw[32:32 + CH, :]
            wb0, wb1, wb2 = cw_ref[pl.ds(32, 1), :], cw_ref[pl.ds(33, 1), :], cw_ref[pl.ds(34, 1), :]
            v = wb0 * p6 + wb1 * p7 + wb2 * p8
            dz_ref[2, rows, :] = (dq_ref[rows, :] * v).astype(BF)
            dvw = dvpad[pl.ds(base, CH + HALO), :]
            dvc = dvw[0:CH, :]
            dp = wb2 * dvc + wb1 * _shift_up(dvw, 1)[0:CH, :] + wb0 * _shift_up(dvw, 2)[0:CH, :]
            dz_ref[3, rows, :] = (dp * z_ref[4, rows, :]).astype(BF)
            dz_ref[4, rows, :] = (dp * z_ref[3, rows, :]).astype(BF)
            acc_b[0:8, :] += _fold8(dvc * p6)
            acc_b[8:16, :] += _fold8(dvc * p7)
            acc_b[16:24, :] += _fold8(dvc * p8)
            dz_ref[5, rows, :] = dzg_ref[0, rows, :]
            dz_ref[6, rows, :] = dzg_ref[1, rows, :]
            return carry

        lax.fori_loop(0, TB // CH, chunk, 0)

        @pl.when(t == nt - 1)
        def _():
            for k in range(KA):
                dwa_ref[k:k + 1, :] = jnp.sum(acc_a[8 * k:8 * k + 8, :], axis=0, keepdims=True)
            dwa_ref[KA:32, :] = jnp.zeros((32 - KA, LANE), F32)
            for k in range(KB):
                dwb_ref[k:k + 1, :] = jnp.sum(acc_b[8 * k:8 * k + 8, :], axis=0, keepdims=True)
            dwb_ref[KB:8, :] = jnp.zeros((8 - KB, LANE), F32)

    blk = lambda c, t: (t, c)
    nxt = lambda c, t: (jnp.minimum((t + 1) * hb, last_h), c)
    return _call(
        body, name="conv_bwd", grid=(D // LANE, nt), args=[z, z, z, da1, da1, dq, dq, dzg, cw], comm=comm,
        in_specs=[pl.BlockSpec((5, TB, LANE), lambda c, t: (0, t, c)),
                  pl.BlockSpec((5, HALO, LANE), lambda c, t: (0, jnp.maximum(t * hb - 1, 0), c)),
                  pl.BlockSpec((5, HALO, LANE), lambda c, t: (0, jnp.minimum((t + 1) * hb, last_h), c)),
                  pl.BlockSpec((TB, LANE), blk), pl.BlockSpec((HALO, LANE), nxt),
                  pl.BlockSpec((TB, LANE), blk), pl.BlockSpec((HALO, LANE), nxt),
                  pl.BlockSpec((2, TB, LANE), lambda c, t: (0, t, c)),
                  pl.BlockSpec((None, 40, LANE), lambda c, t: (c, 0, 0))],
        out_shape=[jax.ShapeDtypeStruct((NG, T, D), BF), jax.ShapeDtypeStruct((32, D), F32),
                   jax.ShapeDtypeStruct((8, D), F32)],
        out_specs=[pl.BlockSpec((NG, TB, LANE), lambda c, t: (0, t, c)),
                   pl.BlockSpec((32, LANE), lambda c, t: (0, c)), pl.BlockSpec((8, LANE), lambda c, t: (0, c))],
        scratch_shapes=[pltpu.VMEM((TB + HALO, LANE), F32)] * 4
                       + [pltpu.VMEM((8 * 32, LANE), F32), pltpu.VMEM((24, LANE), F32)])


def _mix_in_bwd(dz, dh2, h1, gm, win, comm=None):
    def body(dz_ref, w_ref, dh_ref, h_ref, g_ref, o_ref, s_ref, acc):
        i, j = pl.program_id(0), pl.program_id(1)

        @pl.when((i == 0) & (j == 0))
        def _():
            s_ref[...] = jnp.zeros_like(s_ref)

        @pl.when(j == 0)
        def _():
            acc[...] = jnp.zeros_like(acc)

        acc[...] += _nn(dz_ref[...], w_ref[...])

        @pl.when(j == NG - 1)
        def _():
            dx, dg = _rmsnorm_bwd(h_ref[...], g_ref[...], acc[...])
            o_ref[...] = dh_ref[...] + dx
            s_ref[0:1, :] += dg

    row = lambda i, j: (i, 0)
    return _call(
        body, name="mix_in_bwd", grid=(T // TMI, NG), args=[dz, win, dh2, h1, gm], comm=comm,
        in_specs=[pl.BlockSpec((None, TMI, D), lambda i, j: (j, i, 0)),
                  pl.BlockSpec((D, D), lambda i, j: (j, 0)),
                  pl.BlockSpec((TMI, D), row), pl.BlockSpec((TMI, D), row), pl.BlockSpec((1, D), lambda i, j: (0, 0))],
        out_shape=[jax.ShapeDtypeStruct((T, D), F32), jax.ShapeDtypeStruct((8, D), F32)],
        out_specs=[pl.BlockSpec((TMI, D), row), pl.BlockSpec((8, D), lambda i, j: (0, 0))],
        scratch_shapes=[pltpu.VMEM((TMI, D), F32)])


def _row_tile(n, want, mult):
    for t in range(min(want, n), 0, -1):
        if n % t == 0 and t % mult == 0:
            return t
    return n


def _sum_slots(recv, name):
    ns, rows, cols = recv.shape
    tr = _row_tile(rows, 1024, 16)

    def body(r_ref, o_ref):
        s = r_ref[0].astype(F32)
        for k in range(1, ns):
            s = s + r_ref[k].astype(F32)
        o_ref[...] = s

    return _call(
        body, name=name, grid=(rows // tr,), args=[recv],
        in_specs=[pl.BlockSpec((ns, tr, cols), lambda i: (0, i, 0))],
        out_shape=[jax.ShapeDtypeStruct((rows, cols), F32)],
        out_specs=[pl.BlockSpec((tr, cols), lambda i: (i, 0))])[0]


def _pack_small(s_ffn1, s_in, s_mix, s_ffn2, s_final, dwa, dwb):
    def body(f1, mi, mo, f2, fl, wa_ref, wb_ref, v_ref, k_ref):
        for dst, (ref, row) in enumerate(((f1, 0), (mi, 0), (mo, 0), (mo, 1), (mo, 2), (f2, 0), (fl, 0), (fl, 1))):
            v_ref[dst:dst + 1, :] = ref[row:row + 1, :]
        for k in range(NDEV):
            k_ref[k, 0:32, :] = wa_ref[:, k * LANE:(k + 1) * LANE]
            k_ref[k, 32:40, :] = wb_ref[:, k * LANE:(k + 1) * LANE]

    return pl.pallas_call(
        body, name="pack_small",
        out_shape=(jax.ShapeDtypeStruct((8, D), F32), jax.ShapeDtypeStruct((NDEV, 40, LANE), F32)),
    )(s_ffn1, s_in, s_mix, s_ffn2, s_final, dwa, dwb)


def _sum_small(vecs, convs):
    def body(v_ref, k_ref, vs_ref, ks_ref, l_ref):
        s, c = v_ref[0], k_ref[0]
        for k in range(1, NDEV):
            s = s + v_ref[k]
            c = c + k_ref[k]
        vs_ref[...] = s
        ks_ref[...] = c
        l_ref[...] = jnp.broadcast_to(jnp.sum(s[7:8, :], axis=-1, keepdims=True), (8, LANE))

    return pl.pallas_call(
        body, name="sum_small",
        out_shape=(jax.ShapeDtypeStruct((8, D), F32), jax.ShapeDtypeStruct((40, LANE), F32),
                   jax.ShapeDtypeStruct((8, LANE), F32)),
    )(vecs, convs)


def _adam(gs, ws, ms, vs, name, comm=None):
    n = len(gs)
    rows, cols = ws[0].shape
    tr = _row_tile(rows, 256, 8)
    c1 = 1.0 - ADAM_B1 ** ADAM_STEP
    c2 = 1.0 - ADAM_B2 ** ADAM_STEP

    def body(*refs):
        for i in range(n):
            g, w, m, v = (refs[4 * i + k][...] for k in range(4))
            d_ref, m_ref, v_ref = refs[4 * n + 3 * i: 4 * n + 3 * i + 3]
            m2 = ADAM_B1 * m + (1.0 - ADAM_B1) * g
            v2 = ADAM_B2 * v + (1.0 - ADAM_B2) * (g * g)
            d_ref[...] = -ADAM_LR * ((m2 / c1) / (jnp.sqrt(v2 / c2) + ADAM_EPS) + ADAM_WD * w)
            m_ref[...] = m2
            v_ref[...] = v2

    spec = pl.BlockSpec((tr, cols), lambda i: (i, 0))
    args = []
    for i in range(n):
        args += [gs[i], ws[i], ms[i], vs[i]]
    outs = _call(body, name=name, grid=(rows // tr,), args=args, comm=comm, in_specs=[spec] * (4 * n),
                 out_shape=[jax.ShapeDtypeStruct((rows, cols), F32)] * (3 * n), out_specs=[spec] * (3 * n))
    return [tuple(outs[3 * i: 3 * i + 3]) for i in range(n)], outs[3 * n:]


def kernel(x, ffn1_norm, ffn1_w_gate, ffn1_w_up, ffn1_w_down, mix_norm, w_in, a_dw_w, a_dw_b, a_ln_g, a_ln_b, a_w_out, b_conv_w, b_w_out, w_o, ffn2_norm, ffn2_w_gate, ffn2_w_up, ffn2_w_down, final_norm, loss_target, m_ffn1_norm, m_ffn1_w_gate, m_ffn1_w_up, m_ffn1_w_down, m_mix_norm, m_w_in, m_a_dw_w, m_a_dw_b, m_a_ln_g, m_a_ln_b, m_a_w_out, m_b_conv_w, m_b_w_out, m_w_o, m_ffn2_norm, m_ffn2_w_gate, m_ffn2_w_up, m_ffn2_w_down, m_final_norm, v_ffn1_norm, v_ffn1_w_gate, v_ffn1_w_up, v_ffn1_w_down, v_mix_norm, v_w_in, v_a_dw_w, v_a_dw_b, v_a_ln_g, v_a_ln_b, v_a_w_out, v_b_conv_w, v_b_w_out, v_w_o, v_ffn2_norm, v_ffn2_w_gate, v_ffn2_w_up, v_ffn2_w_down, v_final_norm):
    names = ("ffn1_norm", "ffn1_w_gate", "ffn1_w_up", "ffn1_w_down", "mix_norm", "w_in", "a_dw_w", "a_dw_b",
             "a_ln_g", "a_ln_b", "a_w_out", "b_conv_w", "b_w_out", "w_o", "ffn2_norm", "ffn2_w_gate", "ffn2_w_up",
             "ffn2_w_down", "final_norm")
    w = dict(ffn1_norm=ffn1_norm, ffn1_w_gate=ffn1_w_gate, ffn1_w_up=ffn1_w_up, ffn1_w_down=ffn1_w_down,
             mix_norm=mix_norm, w_in=w_in, a_dw_w=a_dw_w, a_dw_b=a_dw_b, a_ln_g=a_ln_g, a_ln_b=a_ln_b,
             a_w_out=a_w_out, b_conv_w=b_conv_w, b_w_out=b_w_out, w_o=w_o, ffn2_norm=ffn2_norm,
             ffn2_w_gate=ffn2_w_gate, ffn2_w_up=ffn2_w_up, ffn2_w_down=ffn2_w_down, final_norm=final_norm)
    m = dict(ffn1_norm=m_ffn1_norm, ffn1_w_gate=m_ffn1_w_gate, ffn1_w_up=m_ffn1_w_up, ffn1_w_down=m_ffn1_w_down,
             mix_norm=m_mix_norm, w_in=m_w_in, a_dw_w=m_a_dw_w, a_dw_b=m_a_dw_b, a_ln_g=m_a_ln_g, a_ln_b=m_a_ln_b,
             a_w_out=m_a_w_out, b_conv_w=m_b_conv_w, b_w_out=m_b_w_out, w_o=m_w_o, ffn2_norm=m_ffn2_norm,
             ffn2_w_gate=m_ffn2_w_gate, ffn2_w_up=m_ffn2_w_up, ffn2_w_down=m_ffn2_w_down, final_norm=m_final_norm)
    v = dict(ffn1_norm=v_ffn1_norm, ffn1_w_gate=v_ffn1_w_gate, ffn1_w_up=v_ffn1_w_up, ffn1_w_down=v_ffn1_w_down,
             mix_norm=v_mix_norm, w_in=v_w_in, a_dw_w=v_a_dw_w, a_dw_b=v_a_dw_b, a_ln_g=v_a_ln_g, a_ln_b=v_a_ln_b,
             a_w_out=v_a_w_out, b_conv_w=v_b_conv_w, b_w_out=v_b_w_out, w_o=v_w_o, ffn2_norm=v_ffn2_norm,
             ffn2_w_gate=v_ffn2_w_gate, ffn2_w_up=v_ffn2_w_up, ffn2_w_down=v_ffn2_w_down, final_norm=v_final_norm)
    flat = _pack_weights(dict(wg1=ffn1_w_gate[0].T, wu1=ffn1_w_up[0].T, wd1=ffn1_w_down[0], wg2=ffn2_w_gate[0].T,
                              wu2=ffn2_w_up[0].T, wd2=ffn2_w_down[0], win=w_in[0], wa=a_w_out[0], wb=b_w_out[0],
                              wo=w_o[0]))
    cw_shard = jnp.concatenate([a_dw_w[0], jnp.zeros((1, LANE), F32), b_conv_w[0], jnp.zeros((5, LANE), F32)], axis=0)

    x2, tgt = x[0], loss_target[0]
    st_a, st_b, st_c, st_d = ("wg1", "wu1", "wd1"), ("win",), ("wa", "wb", "wo", "wg2"), ("wu2", "wd2")

    buf_a, cw = _run_comm(_join(_ag_comm(st_a, flat), _direct_comm(cw_shard, False)), "ag_ffn1")
    h1, n1, gg1, uu1, buf_b = _ffn_fwd(x2, ffn1_norm, (buf_a,) * 3, (0, F, 2 * F), "ffn1_fwd", _ag_comm(st_b, flat))
    u, z, buf_c = _mix_in(h1, mix_norm, buf_b, _ag_comm(st_c, flat))
    a1, q, buf_d = _conv_fwd(z, cw, a_dw_b, _ag_comm(st_d, flat))
    h2, ya, yb = _mix_out(a1, q, z, h1, a_ln_g, a_ln_b, buf_c)
    ffn2_bufs, ffn2_offs = (buf_c, buf_d, buf_d), (3 * D, 0, F)
    h3, n2, gg2, uu2 = _ffn_fwd(h2, ffn2_norm, ffn2_bufs, ffn2_offs, "ffn2_fwd")
    dh3, s_final = _final_loss(h3, final_norm.reshape(1, D), tgt)

    tr_f = F // 2 if (F // 2) % LANE == 0 else F
    def pair(stage, src):
        return _rs_pair_comm(stage, src)

    def chip(stage, src, pair_buf, tag):
        return _rs_chip_comm(_pair_add(stage, src, pair_buf, "pair_add_" + tag))

    dgu2, act2 = _ffn_bwd_hidden(dh3, gg2, uu2, buf_d, F, "ffn2_bwd_h")
    (gu2,) = _tn_matmul(dgu2, n2, tr_f, "dw_gu2")
    s2a, src2a = ("wg2", "wu2"), dict(wg2=(gu2, 0), wu2=(gu2, F))
    gd2, pair2a = _tn_matmul(act2, dh3, tr_f, "dw_d2", pair(s2a, src2a), scale=0.5)
    s2b, src2b = ("wd2",), dict(wd2=(gd2, 0))
    dh2, s_ffn2, recv2a, pair2b = _ffn_bwd_input(dgu2, dh3, h2, ffn2_norm, (buf_c, buf_d), (3 * D, 0), "ffn2_bwd_x",
                                                 _join(chip(s2a, src2a, pair2a, "2a"), pair(s2b, src2b)))
    dzg, da1, dq, mb, a3b, dya, dyb, s_mix, recv2b = _mix_out_bwd(dh2, ya, yb, z, a1, a_ln_g, a_ln_b, buf_c,
                                                                   chip(s2b, src2b, pair2b, "2b"))
    (go,) = _tn_matmul(mb, dh2, D, "dw_o")
    (ga,) = _tn_matmul(a3b, dya, D, "dw_a")
    (gb,) = _tn_matmul(q.reshape(1, T, D), dyb, D, "dw_b")
    ssq, srcsq = ("wa", "wb", "wo"), dict(wa=(ga, 0), wb=(gb, 0), wo=(go, 0))
    dz, dwa, dwb, pairsq = _conv_bwd(z, da1, dq, dzg, cw, pair(ssq, srcsq))
    gin, recvsq = _tn_matmul(dz, u, D, "dw_in", chip(ssq, srcsq, pairsq, "sq"))
    sin_a, sin_b, srcin = ("win/0/2",), ("win/1/2",), {"win/0/2": (gin, 0), "win/1/2": (gin, 0)}
    dh1, s_in, pairin_a, pairin_b = _mix_in_bwd(dz, dh2, h1, mix_norm, buf_b,
                                                _join(pair(sin_a, srcin), pair(sin_b, srcin)))
    dgu1, act1, recvin_a = _ffn_bwd_hidden(dh1, gg1, uu1, buf_a, 2 * F, "ffn1_bwd_h",
                                           chip(sin_a, srcin, pairin_a, "in_a"))
    gu1, recvin_b = _tn_matmul(dgu1, n1, tr_f, "dw_gu1", chip(sin_b, srcin, pairin_b, "in_b"))
    s1a, src1a = ("wg1", "wu1"), dict(wg1=(gu1, 0), wu1=(gu1, F))
    gd1, pair1a = _tn_matmul(act1, dh1, tr_f, "dw_d1", pair(s1a, src1a), scale=0.5)
    s1b, src1b = ("wd1",), dict(wd1=(gd1, 0))
    dx, s_ffn1, recv1a, pair1b = _ffn_bwd_input(dgu1, dh1, x2, ffn1_norm, (buf_a, buf_a), (0, F), "ffn1_bwd_x",
                                                _join(chip(s1a, src1a, pair1a, "1a"), pair(s1b, src1b)))
    (recv1b,) = _run_comm(chip(s1b, src1b, pair1b, "1b"), "rs_chip_1b")
    stages = ((s2a, recv2a, "2a"), (s2b, recv2b, "2b"), (ssq, recvsq, "sq"), (sin_a, recvin_a, "in_a"),
              (sin_b, recvin_b, "in_b"), (s1a, recv1a, "1a"), (s1b, recv1b, "1b"))

    gsum = {}
    for stage, recv, tag in stages:
        st, total = _Stage(stage), _sum_slots(recv, "sum_" + tag)
        for n in stage:
            gsum[n] = total[st.off[n]:st.off[n] + st.rows[n]]
    gsum["win"] = jnp.concatenate([gsum["win/0/2"], gsum["win/1/2"]], axis=0)

    vec8, convk = _pack_small(s_ffn1, s_in, s_mix, s_ffn2, s_final, dwa, dwb)
    vec_all, conv_all = _run_comm(_join(_direct_comm(vec8, False), _direct_comm(convk, True)), "xchg_small")
    vec_sum, conv_sum, loss_blk = _sum_small(vec_all, conv_all)
    loss = loss_blk[0, 0]

    g = dict(ffn1_w_gate=gsum["wg1"], ffn1_w_up=gsum["wu1"], ffn1_w_down=gsum["wd1"],
             ffn2_w_gate=gsum["wg2"], ffn2_w_up=gsum["wu2"], ffn2_w_down=gsum["wd2"], w_in=gsum["win"].T,
             a_w_out=gsum["wa"], b_w_out=gsum["wb"], w_o=gsum["wo"],
             ffn1_norm=vec_sum[0:1], mix_norm=vec_sum[1:2], a_ln_g=vec_sum[2:3], a_ln_b=vec_sum[3:4],
             a_dw_b=vec_sum[4:5], ffn2_norm=vec_sum[5:6], final_norm=vec_sum[6:7],
             a_dw_w=conv_sum[0:KA], b_conv_w=conv_sum[32:32 + KB])
    gate_up = ("ffn1_w_gate", "ffn1_w_up", "ffn2_w_gate", "ffn2_w_up")

    upd = {}

    def run(group, name, as2d=lambda a: a[0], back=lambda a, n: a.reshape(w[n].shape)):
        res, _ = _adam([g[n] for n in group], [as2d(w[n]) for n in group], [as2d(m[n]) for n in group],
                       [as2d(v[n]) for n in group], name)
        for n, r in zip(group, res):
            upd[n] = tuple(back(a, n) for a in r)

    run(gate_up, "adam_gate_up", as2d=lambda a: a[0].T, back=lambda a, n: a.T[None])
    for n in gate_up:
        g[n] = g[n].T
    run(("ffn1_w_down", "ffn2_w_down"), "adam_down")
    run(("w_in",), "adam_in")
    run(("a_w_out", "b_w_out", "w_o"), "adam_square")
    run(("a_dw_w",), "adam_dw")
    run(("b_conv_w",), "adam_conv")
    vecs = ("ffn1_norm", "mix_norm", "a_dw_b", "a_ln_g", "a_ln_b", "ffn2_norm", "final_norm")
    run(vecs, "adam_vec", as2d=lambda a: a.reshape(1, D))

    grads = [g[n].reshape(w[n].shape) for n in names]
    return (loss, dx.reshape(x.shape), *grads, *[upd[n][0] for n in names], *[upd[n][1] for n in names],
            *[upd[n][2] for n in names])
```

```python
import jax
import jax.numpy as jnp
from jax import lax
from jax.experimental import pallas as pl
from jax.experimental.pallas import tpu as pltpu

T = 4096
D = 1024
F = 2816
NG = 7
NDEV = 8
NCHIP = 4
KA, KB = 31, 3
EPS = 1e-6
ADAM_LR, ADAM_B1, ADAM_B2, ADAM_EPS, ADAM_WD, ADAM_STEP = 0.001, 0.9, 0.999, 1e-08, 0.01, 10

TM = 512
TMI = 1024
FC = 256
TB = 1024
CH = 128
HALO = 32
LANE = 128
TK = 1024
VMEM_LIMIT = 56 * 1024 * 1024

BF = jnp.bfloat16
F32 = jnp.float32
MESH = pl.DeviceIdType.MESH
ANY = pl.BlockSpec(memory_space=pl.ANY)

ORDER = ("wg1", "wu1", "wd1", "wg2", "wu2", "wd2", "win", "wa", "wb", "wo")


class _Layout:
    def __init__(self):
        fs, dis, ds = F // NDEV, NG * D // NDEV, D // NDEV
        self.rows = dict(wg1=fs, wu1=fs, wd1=fs, wg2=fs, wu2=fs, wd2=fs, win=dis, wa=ds, wb=ds, wo=ds)
        self.fl, off = {}, 0
        for n in ORDER:
            self.fl[n] = off
            off += self.rows[n]
        self.RT = off


class _Stage:
    def __init__(self, names):
        lay = _Layout()
        self.names = names
        self.rows, self.full, self.sub, self.fl = {}, {}, {}, {}
        for n in names:
            base, i, k = (n.split("/") + ["0", "1"])[:3]
            self.full[n] = lay.rows[base]
            self.rows[n] = lay.rows[base] // int(k)
            self.sub[n] = int(i) * self.rows[n]
            self.fl[n] = lay.fl[base] + self.sub[n]
        self.off, self.wc, o, w = {}, {}, 0, 0
        for n in names:
            self.off[n], self.wc[n] = o, w
            o += self.rows[n]
            w += NDEV * self.rows[n]
        self.R, self.W = o, w

    def grad_row(self, n, first, dev_lin):
        return first + dev_lin * self.full[n] + self.sub[n]


def _nt(a, b):
    return lax.dot_general(a, b, (((1,), (1,)), ((), ())), preferred_element_type=F32)


def _nn(a, b):
    return lax.dot_general(a, b, (((1,), (0,)), ((), ())), preferred_element_type=F32)


def _tn(a, b):
    return lax.dot_general(a, b, (((0,), (0,)), ((), ())), preferred_element_type=F32)


def _sig(x):
    return 1.0 / (1.0 + jnp.exp(-x))


def _position():
    return lax.axis_index("x"), lax.axis_index("y"), lax.axis_index("c")


def _peer(pos, j):
    x, y, c = pos
    return (1 - x if j & 4 else x, 1 - y if j & 2 else y, 1 - c if j & 1 else c)


def _lin(pos):
    return 4 * pos[0] + 2 * pos[1] + pos[2]


def _chip(pos):
    return 2 * pos[0] + pos[1]


class _Comm:
    def __init__(self, inputs, out_shapes, scratch, start, finish):
        self.inputs, self.out_shapes, self.scratch, self.start, self.finish = inputs, out_shapes, scratch, start, finish


def _call(body, *, name, grid, args, in_specs, out_shape, out_specs, scratch_shapes=(), comm=None,
          num_scalar_prefetch=0):
    in_specs, out_shape, out_specs, scratch_shapes = list(in_specs), list(out_shape), list(out_specs), list(scratch_shapes)
    n_in, n_out, n_scr = len(in_specs), len(out_shape), len(scratch_shapes)
    sp = num_scalar_prefetch
    if comm is None:
        kernel_fn = lambda *refs: body(*refs)
        c_in = c_out = c_scr = 0
    else:
        c_in, c_out, c_scr = len(comm.inputs), len(comm.out_shapes), len(comm.scratch)

        def kernel_fn(*refs):
            pre, refs = refs[:sp], refs[sp:]
            ins, cins = refs[:n_in], refs[n_in:n_in + c_in]
            o0 = n_in + c_in
            outs, couts = refs[o0:o0 + n_out], refs[o0 + n_out:o0 + n_out + c_out]
            s0 = o0 + n_out + c_out
            scr, cscr = refs[s0:s0 + n_scr], refs[s0 + n_scr:]
            first = pl.program_id(0) == 0
            last = pl.program_id(0) == grid[0] - 1
            for a in range(1, len(grid)):
                first = first & (pl.program_id(a) == 0)
                last = last & (pl.program_id(a) == grid[a] - 1)

            @pl.when(first)
            def _():
                comm.start(cins, couts, cscr)

            body(*pre, *ins, *outs, *scr)

            @pl.when(last)
            def _():
                comm.finish(cins, couts, cscr)

        args = list(args) + list(comm.inputs)
        in_specs += [ANY] * c_in
        out_shape += list(comm.out_shapes)
        out_specs += [ANY] * c_out
        scratch_shapes += list(comm.scratch)
    params = pltpu.CompilerParams(dimension_semantics=("arbitrary",) * len(grid), vmem_limit_bytes=VMEM_LIMIT)
    if sp:
        grid_spec = pltpu.PrefetchScalarGridSpec(num_scalar_prefetch=sp, grid=grid, in_specs=in_specs,
                                                 out_specs=out_specs, scratch_shapes=scratch_shapes)
        return pl.pallas_call(kernel_fn, name=name, grid_spec=grid_spec, out_shape=out_shape,
                              compiler_params=params)(*args)
    return pl.pallas_call(kernel_fn, name=name, grid=grid, in_specs=in_specs, out_shape=out_shape, out_specs=out_specs,
                          scratch_shapes=scratch_shapes, compiler_params=params)(*args)


def _join(a, b):
    na = (len(a.inputs), len(a.out_shapes), len(a.scratch))

    def split(refs):
        return ([r[:n] for r, n in zip(refs, na)], [r[n:] for r, n in zip(refs, na)])

    def start(*refs):
        ra, rb = split(refs)
        a.start(*ra)
        b.start(*rb)

    def finish(*refs):
        ra, rb = split(refs)
        a.finish(*ra)
        b.finish(*rb)

    return _Comm(list(a.inputs) + list(b.inputs), list(a.out_shapes) + list(b.out_shapes),
                 list(a.scratch) + list(b.scratch), start, finish)


def _run_comm(comm, name):
    def body(*refs):
        c_in, c_out = len(comm.inputs), len(comm.out_shapes)
        comm.start(refs[:c_in], refs[c_in:c_in + c_out], refs[c_in + c_out:])
        comm.finish(refs[:c_in], refs[c_in:c_in + c_out], refs[c_in + c_out:])

    return pl.pallas_call(
        body, name=name, out_shape=list(comm.out_shapes), in_specs=[ANY] * len(comm.inputs),
        out_specs=[ANY] * len(comm.out_shapes), scratch_shapes=list(comm.scratch))(*comm.inputs)


def _ag_comm(names, flat):
    st = _Stage(names)

    def parts(refs):
        (flat_ref,), (out_ref,), (send_sems, recv_sems, local_sem) = refs
        me = _position()

        def region(name, dev):
            r = st.rows[name]
            return out_ref.at[pl.ds(st.wc[name] + _lin(dev) * r, r), :]

        def own(name):
            return flat_ref.at[pl.ds(st.fl[name], st.rows[name]), :]

        def copies(k, dev, to, from_flat):
            return [pltpu.make_async_remote_copy(
                src_ref=own(n) if from_flat else region(n, dev), dst_ref=region(n, dev), send_sem=send_sems.at[k],
                recv_sem=recv_sems.at[k], device_id=to, device_id_type=MESH) for n in names]

        def whole(k):
            return pltpu.make_async_remote_copy(
                src_ref=flat_ref.at[pl.ds(0, st.R), :], dst_ref=out_ref.at[pl.ds(0, st.R), :],
                send_sem=send_sems.at[k], recv_sem=recv_sems.at[k], device_id=me, device_id_type=MESH)

        return me, region, own, copies, whole, flat_ref, out_ref, local_sem

    def start(*refs):
        me, region, own, copies, _, _, _, local_sem = parts(refs)
        for n in names:
            pltpu.make_async_copy(own(n), region(n, me), local_sem).start()
        for cp in copies(0, me, _peer(me, 1), True):
            cp.start()
        for j, bits in enumerate((4, 2, 6)):
            for cp in copies(1 + j, me, _peer(me, bits), True):
                cp.start()

    def finish(*refs):
        me, _, _, copies, whole, flat_ref, out_ref, local_sem = parts(refs)
        for j, bits in enumerate((4, 2, 6)):
            whole(1 + j).wait_recv()
            for cp in copies(4 + j, _peer(me, bits), _peer(me, 1), False):
                cp.start()
        whole(0).wait_recv()
        for j in range(3):
            whole(4 + j).wait_recv()
        for k in range(7):
            whole(k).wait_send()
        pltpu.make_async_copy(flat_ref.at[pl.ds(0, st.R), :], out_ref.at[pl.ds(0, st.R), :], local_sem).wait()

    return _Comm([flat], [jax.ShapeDtypeStruct((st.W, D), BF)],
                 [pltpu.SemaphoreType.DMA((7,)), pltpu.SemaphoreType.DMA((7,)), pltpu.SemaphoreType.DMA],
                 start, finish)


def _rs_pair_comm(names, src):
    st = _Stage(names)
    arrays = []
    for n in names:
        if not any(src[n][0] is a for a in arrays):
            arrays.append(src[n][0])
    idx = {n: [i for i, a in enumerate(arrays) if a is src[n][0]][0] for n in names}

    def slot_wait(refs):
        recv = refs[1][0]
        send_sem, recv_sem = refs[2]
        return pltpu.make_async_remote_copy(src_ref=recv, dst_ref=recv, send_sem=send_sem, recv_sem=recv_sem,
                                            device_id=_position(), device_id_type=MESH)

    def start(*refs):
        ins, (recv,), (send_sem, recv_sem) = refs
        me = _position()
        sib = _peer(me, 1)
        for q in range(NCHIP):
            dev = (q // 2, q % 2, sib[2])
            for n in names:
                r = st.rows[n]
                pltpu.make_async_remote_copy(
                    src_ref=ins[idx[n]].at[pl.ds(st.grad_row(n, src[n][1], _lin(dev)), r), :],
                    dst_ref=recv.at[q, pl.ds(st.off[n], r), :], send_sem=send_sem, recv_sem=recv_sem,
                    device_id=sib, device_id_type=MESH).start()

    def finish(*refs):
        w = slot_wait(refs)
        w.wait_recv()
        w.wait_send()

    return _Comm(arrays, [jax.ShapeDtypeStruct((NCHIP, st.R, D), BF)],
                 [pltpu.SemaphoreType.DMA, pltpu.SemaphoreType.DMA], start, finish)


def _pair_add(names, src, recv, name):
    st = _Stage(names)
    c_arr = jnp.reshape(lax.axis_index("c"), (1,)).astype(jnp.int32)

    def body(c_ref, *refs):
        r_ref, o_ref = refs[len(names)], refs[len(names) + 1]
        for a_ref, n in zip(refs, names):
            rows = slice(st.off[n], st.off[n] + st.rows[n])
            o_ref[rows, :] = (a_ref[...].astype(F32) + r_ref[rows, :].astype(F32)).astype(BF)

    def shard_spec(n):
        r = st.rows[n]
        base, step = st.grad_row(n, src[n][1], 0) // r, st.full[n] // r
        return pl.BlockSpec((r, D), lambda q, c_ref: (base + step * (2 * q + c_ref[0]), 0))

    slot = pl.BlockSpec((None, st.R, D), lambda q, c_ref: (q, 0, 0))
    return _call(body, name=name, grid=(NCHIP,), args=[c_arr] + [src[n][0] for n in names] + [recv],
                 in_specs=[shard_spec(n) for n in names] + [slot],
                 out_shape=[jax.ShapeDtypeStruct((NCHIP, st.R, D), BF)], out_specs=[slot], num_scalar_prefetch=1)[0]


def _rs_chip_comm(part):
    def copies(refs):
        (p_ref,), (recv,), (send_sems, recv_sems, local_sem) = refs
        me = _position()
        mine = pltpu.make_async_copy(p_ref.at[_chip(me)], recv.at[_chip(me)], local_sem)
        out = []
        for j, bits in enumerate((4, 2, 6)):
            to = _peer(me, bits)
            out.append(pltpu.make_async_remote_copy(
                src_ref=p_ref.at[_chip(to)], dst_ref=recv.at[_chip(me)], send_sem=send_sems.at[j],
                recv_sem=recv_sems.at[j], device_id=to, device_id_type=MESH))
        return mine, out

    def start(*refs):
        mine, out = copies(refs)
        mine.start()
        for cp in out:
            cp.start()

    def finish(*refs):
        mine, out = copies(refs)
        for cp in out:
            cp.wait_recv()
        for cp in out:
            cp.wait_send()
        mine.wait()

    return _Comm([part], [jax.ShapeDtypeStruct(part.shape, BF)],
                 [pltpu.SemaphoreType.DMA((3,)), pltpu.SemaphoreType.DMA((3,)), pltpu.SemaphoreType.DMA],
                 start, finish)


def _direct_comm(x, scatter):
    def copies(refs):
        (x_ref,), (out_ref,), (send_sems, recv_sems, local_sem) = refs
        me = _position()

        def piece(dev):
            return x_ref.at[_lin(dev)] if scatter else x_ref

        mine = pltpu.make_async_copy(piece(me), out_ref.at[_lin(me)], local_sem)
        return mine, [pltpu.make_async_remote_copy(
            src_ref=piece(_peer(me, j)), dst_ref=out_ref.at[_lin(me)], send_sem=send_sems.at[j - 1],
            recv_sem=recv_sems.at[j - 1], device_id=_peer(me, j), device_id_type=MESH) for j in range(1, NDEV)]

    def start(*refs):
        mine, cps = copies(refs)
        mine.start()
        for cp in cps:
            cp.start()

    def finish(*refs):
        mine, cps = copies(refs)
        for cp in cps:
            cp.wait_recv()
        for cp in cps:
            cp.wait_send()
        mine.wait()

    shape = x.shape if scatter else (NDEV,) + x.shape
    return _Comm([x], [jax.ShapeDtypeStruct(shape, x.dtype)],
                 [pltpu.SemaphoreType.DMA((7,)), pltpu.SemaphoreType.DMA((7,)), pltpu.SemaphoreType.DMA],
                 start, finish)


def _pack_weights(shards):
    lay = _Layout()

    def body(*refs):
        o_ref = refs[-1]
        for ref, n in zip(refs, ORDER):
            x = ref[...].T if n == "win" else ref[...]
            o_ref[lay.fl[n]:lay.fl[n] + lay.rows[n], :] = x.astype(BF)

    return pl.pallas_call(
        body, name="pack_weights", out_shape=jax.ShapeDtypeStruct((lay.RT, D), BF),
        compiler_params=pltpu.CompilerParams(vmem_limit_bytes=VMEM_LIMIT))(*[shards[n] for n in ORDER])


def _load_ffn_weights(srcs, offs, scratch, sem):
    @pl.when(pl.program_id(0) == 0)
    def _():
        cps = [pltpu.make_async_copy(s.at[pl.ds(off, F), :], dst, sem.at[i])
               for i, (s, off, dst) in enumerate(zip(srcs, offs, scratch))]
        for cp in cps:
            cp.start()
        for cp in cps:
            cp.wait()


def _ffn_fwd(x, g, wbufs, offs, name, comm=None):
    nf = F // FC

    def body(x_ref, g_ref, b0, b1, b2, h_ref, n_ref, gg_ref, uu_ref, wg_s, wu_s, wd_s, sem):
        _load_ffn_weights((b0, b1, b2), offs, (wg_s, wu_s, wd_s), sem)
        xf = x_ref[...]
        r = lax.rsqrt(jnp.mean(xf * xf, axis=-1, keepdims=True) + EPS)
        nb = (xf * r * g_ref[...]).astype(BF)
        n_ref[...] = nb
        acc = jnp.zeros((TM, D), F32)
        for c in range(nf):
            sl = slice(c * FC, (c + 1) * FC)
            gb = _nt(nb, wg_s[sl, :]).astype(BF)
            ub = _nt(nb, wu_s[sl, :]).astype(BF)
            gg_ref[:, sl] = gb
            uu_ref[:, sl] = ub
            acc = acc + _nn((gb * _sig(gb)) * ub, wd_s[sl, :])
        h_ref[...] = xf + 0.5 * acc

    row = lambda i: (i, 0)
    return _call(
        body, name=name, grid=(T // TM,), args=[x, g, *wbufs], comm=comm,
        in_specs=[pl.BlockSpec((TM, D), row), pl.BlockSpec((1, D), lambda i: (0, 0)), ANY, ANY, ANY],
        out_shape=[jax.ShapeDtypeStruct((T, D), F32), jax.ShapeDtypeStruct((T, D), BF),
                   jax.ShapeDtypeStruct((T, F), BF), jax.ShapeDtypeStruct((T, F), BF)],
        out_specs=[pl.BlockSpec((TM, D), row), pl.BlockSpec((TM, D), row),
                   pl.BlockSpec((TM, F), row), pl.BlockSpec((TM, F), row)],
        scratch_shapes=[pltpu.VMEM((F, D), BF)] * 3 + [pltpu.SemaphoreType.DMA((3,))])


def _mix_in(h1, gm, win, comm=None):
    def body(h_ref, g_ref, w_ref, u_ref, z_ref, u_s):
        @pl.when(pl.program_id(1) == 0)
        def _():
            xf = h_ref[...]
            r = lax.rsqrt(jnp.mean(xf * xf, axis=-1, keepdims=True) + EPS)
            ub = (xf * r * g_ref[...]).astype(BF)
            u_s[...] = ub
            u_ref[...] = ub
        z_ref[...] = _nt(u_s[...], w_ref[...])

    return _call(
        body, name="mix_in", grid=(T // TMI, NG), args=[h1, gm, win], comm=comm,
        in_specs=[pl.BlockSpec((TMI, D), lambda i, j: (i, 0)), pl.BlockSpec((1, D), lambda i, j: (0, 0)),
                  pl.BlockSpec((D, D), lambda i, j: (j, 0))],
        out_shape=[jax.ShapeDtypeStruct((T, D), BF), jax.ShapeDtypeStruct((NG, T, D), F32)],
        out_specs=[pl.BlockSpec((TMI, D), lambda i, j: (i, 0)), pl.BlockSpec((None, TMI, D), lambda i, j: (j, i, 0))],
        scratch_shapes=[pltpu.VMEM((TMI, D), BF)])


def _shift_up(w, b):
    return w if b == 0 else pltpu.roll(w, w.shape[0] - b, 0)


def _fold8(p):
    red = p[0:8, :]
    for i in range(1, p.shape[0] // 8):
        red = red + p[8 * i:8 * i + 8, :]
    return red


def _conv_fwd(z, cw, bias, comm=None):
    nt = T // TB
    hb = TB // HALO

    def body(z_ref, zh_ref, cw_ref, b_ref, a1_ref, q_ref, apad, ppad):
        first = pl.program_id(1) == 0
        apad[0:HALO, :] = jnp.where(first, 0.0, zh_ref[0] * _sig(zh_ref[1]))
        apad[HALO:, :] = z_ref[0] * _sig(z_ref[1])
        ppad[0:HALO, :] = jnp.where(first, 0.0, zh_ref[3] * zh_ref[4])
        ppad[HALO:, :] = z_ref[3] * z_ref[4]
        bias_row = b_ref[...]

        def chunk(r, carry):
            base = pl.multiple_of(r * CH, CH)
            w = apad[pl.ds(base, CH + HALO), :]
            acc = jnp.broadcast_to(bias_row, (CH, LANE))
            for b in range(8):
                wb = _shift_up(w, b)
                for a in range(5):
                    s = 8 * a + b
                    if 2 <= s <= HALO:
                        acc = acc + cw_ref[pl.ds(s - 2, 1), :] * wb[8 * a:8 * a + CH, :]
            a1_ref[pl.ds(base, CH), :] = acc
            pw = ppad[pl.ds(base, CH + HALO), :]
            v = (cw_ref[pl.ds(32, 1), :] * _shift_up(pw, 6)[24:24 + CH, :]
                 + cw_ref[pl.ds(33, 1), :] * _shift_up(pw, 7)[24:24 + CH, :]
                 + cw_ref[pl.ds(34, 1), :] * pw[32:32 + CH, :])
            q_ref[pl.ds(base, CH), :] = (z_ref[2, pl.ds(base, CH), :] * v).astype(BF)
            return carry

        lax.fori_loop(0, TB // CH, chunk, 0)

    return _call(
        body, name="conv_fwd", grid=(D // LANE, nt), args=[z, z, cw, bias], comm=comm,
        in_specs=[pl.BlockSpec((5, TB, LANE), lambda c, t: (0, t, c)),
                  pl.BlockSpec((5, HALO, LANE), lambda c, t: (0, jnp.maximum(t * hb - 1, 0), c)),
                  pl.BlockSpec((None, 40, LANE), lambda c, t: (c, 0, 0)),
                  pl.BlockSpec((1, LANE), lambda c, t: (0, c))],
        out_shape=[jax.ShapeDtypeStruct((T, D), F32), jax.ShapeDtypeStruct((T, D), BF)],
        out_specs=[pl.BlockSpec((TB, LANE), lambda c, t: (t, c)), pl.BlockSpec((TB, LANE), lambda c, t: (t, c))],
        scratch_shapes=[pltpu.VMEM((TB + HALO, LANE), F32), pltpu.VMEM((TB + HALO, LANE), F32)])


def _layernorm_silu(a1, lng, lnb):
    mu = jnp.mean(a1, axis=-1, keepdims=True)
    xc = a1 - mu
    rs = lax.rsqrt(jnp.mean(xc * xc, axis=-1, keepdims=True) + EPS)
    xh = xc * rs
    a2 = xh * lng + lnb
    sg = _sig(a2)
    return xh, rs, a2, sg


def _square_specs(blocks):
    return [pl.BlockSpec((D, D), lambda i, b=b: (b, 0)) for b in blocks]


def _mix_out(a1, q, z, h1, lng, lnb, wsq, comm=None):
    def body(a1_ref, q_ref, ga_ref, gb_ref, h_ref, lng_ref, lnb_ref, wa_ref, wb_ref, wo_ref, h2_ref, ya_ref, yb_ref):
        _, _, a2, sg = _layernorm_silu(a1_ref[...], lng_ref[...], lnb_ref[...])
        ya = _nn((a2 * sg).astype(BF), wa_ref[...])
        yb = _nn(q_ref[...], wb_ref[...])
        ya_ref[...] = ya
        yb_ref[...] = yb
        m = _sig(ga_ref[...]) * ya + _sig(gb_ref[...]) * yb
        h2_ref[...] = h_ref[...] + _nn(m.astype(BF), wo_ref[...])

    row = lambda i: (i, 0)
    vec = pl.BlockSpec((1, D), lambda i: (0, 0))
    return _call(
        body, name="mix_out", grid=(T // TM,), args=[a1, q, z, z, h1, lng, lnb, wsq, wsq, wsq], comm=comm,
        in_specs=[pl.BlockSpec((TM, D), row), pl.BlockSpec((TM, D), row),
                  pl.BlockSpec((None, TM, D), lambda i: (5, i, 0)), pl.BlockSpec((None, TM, D), lambda i: (6, i, 0)),
                  pl.BlockSpec((TM, D), row), vec, vec] + _square_specs((0, 1, 2)),
        out_shape=[jax.ShapeDtypeStruct((T, D), F32)] * 3,
        out_specs=[pl.BlockSpec((TM, D), row)] * 3)


def _final_loss(h3, gf, tgt, comm=None):
    def body(h_ref, g_ref, t_ref, dh_ref, s_ref):
        @pl.when(pl.program_id(0) == 0)
        def _():
            s_ref[...] = jnp.zeros_like(s_ref)
        xf = h_ref[...]
        g = g_ref[...]
        r = lax.rsqrt(jnp.mean(xf * xf, axis=-1, keepdims=True) + EPS)
        xr = xf * r
        e = xr * g - t_ref[...]
        s_ref[1:2, :] += jnp.sum(e * e, axis=0, keepdims=True) * (0.5 / D)
        dy = e * (1.0 / D)
        s_ref[0:1, :] += jnp.sum(dy * xr, axis=0, keepdims=True)
        gdy = dy * g
        dh_ref[...] = r * gdy - xr * (r * jnp.mean(gdy * xr, axis=-1, keepdims=True))

    row = lambda i: (i, 0)
    return _call(
        body, name="final_loss", grid=(T // TM,), args=[h3, gf, tgt], comm=comm,
        in_specs=[pl.BlockSpec((TM, D), row), pl.BlockSpec((1, D), lambda i: (0, 0)), pl.BlockSpec((TM, D), row)],
        out_shape=[jax.ShapeDtypeStruct((T, D), F32), jax.ShapeDtypeStruct((8, D), F32)],
        out_specs=[pl.BlockSpec((TM, D), row), pl.BlockSpec((8, D), lambda i: (0, 0))])


def _rmsnorm_bwd(xf, g, dn):
    r = lax.rsqrt(jnp.mean(xf * xf, axis=-1, keepdims=True) + EPS)
    xr = xf * r
    gdn = dn * g
    dx = r * gdn - xr * (r * jnp.mean(gdn * xr, axis=-1, keepdims=True))
    return dx, jnp.sum(dn * xr, axis=0, keepdims=True)


def _ffn_bwd_hidden(dh, gg, uu, wbuf, off, name, comm=None):
    nf = F // FC

    def body(dh_ref, gg_ref, uu_ref, b0, dgu_ref, a_ref, wd_s, sem):
        _load_ffn_weights((b0,), (off,), (wd_s,), sem)
        dhb = (0.5 * dh_ref[...]).astype(BF)
        for c in range(nf):
            sl = slice(c * FC, (c + 1) * FC)
            da = _nt(dhb, wd_s[sl, :]).astype(BF)
            gb, ub = gg_ref[:, sl], uu_ref[:, sl]
            sg = _sig(gb)
            silu = gb * sg
            dgu_ref[0, :, sl] = (da * ub) * (sg * (1.0 + gb * (1.0 - sg)))
            dgu_ref[1, :, sl] = da * silu
            a_ref[0, :, sl] = silu * ub

    row = lambda i: (i, 0)
    return _call(
        body, name=name, grid=(T // TM,), args=[dh, gg, uu, wbuf], comm=comm,
        in_specs=[pl.BlockSpec((TM, D), row), pl.BlockSpec((TM, F), row), pl.BlockSpec((TM, F), row), ANY],
        out_shape=[jax.ShapeDtypeStruct((2, T, F), BF), jax.ShapeDtypeStruct((1, T, F), BF)],
        out_specs=[pl.BlockSpec((2, TM, F), lambda i: (0, i, 0)), pl.BlockSpec((1, TM, F), lambda i: (0, i, 0))],
        scratch_shapes=[pltpu.VMEM((F, D), BF), pltpu.SemaphoreType.DMA((1,))])


def _ffn_bwd_input(dgu, dh, x, g, wbufs, offs, name, comm=None):
    def body(dgu_ref, dh_ref, x_ref, g_ref, b0, b1, dx_ref, s_ref, wg_s, wu_s, sem):
        _load_ffn_weights((b0, b1), offs, (wg_s, wu_s), sem)

        @pl.when(pl.program_id(0) == 0)
        def _():
            s_ref[...] = jnp.zeros_like(s_ref)

        dn = _nn(dgu_ref[0], wg_s[...]) + _nn(dgu_ref[1], wu_s[...])
        dxn, dg = _rmsnorm_bwd(x_ref[...], g_ref[...], dn)
        dx_ref[...] = dh_ref[...] + dxn
        s_ref[0:1, :] += dg

    row = lambda i: (i, 0)
    return _call(
        body, name=name, grid=(T // TM,), args=[dgu, dh, x, g, *wbufs], comm=comm,
        in_specs=[pl.BlockSpec((2, TM, F), lambda i: (0, i, 0)), pl.BlockSpec((TM, D), row),
                  pl.BlockSpec((TM, D), row), pl.BlockSpec((1, D), lambda i: (0, 0)), ANY, ANY],
        out_shape=[jax.ShapeDtypeStruct((T, D), F32), jax.ShapeDtypeStruct((8, D), F32)],
        out_specs=[pl.BlockSpec((TM, D), row), pl.BlockSpec((8, D), lambda i: (0, 0))],
        scratch_shapes=[pltpu.VMEM((F, D), BF)] * 2 + [pltpu.SemaphoreType.DMA((2,))])


def _tn_matmul(lhs, rhs, tr, name, comm=None, scale=None):
    ng, _, cdim = lhs.shape
    nc, nk = cdim // tr, T // TK

    def body(l_ref, r_ref, o_ref, acc):
        k = pl.program_id(2)

        @pl.when(k == 0)
        def _():
            acc[...] = jnp.zeros_like(acc)

        r = r_ref[...] if scale is None else scale * r_ref[...]
        acc[...] += _tn(l_ref[...], r.astype(BF))

        @pl.when(k == nk - 1)
        def _():
            o_ref[...] = acc[...].astype(BF)

    return _call(
        body, name=name, grid=(ng, nc, nk), args=[lhs, rhs], comm=comm,
        in_specs=[pl.BlockSpec((None, TK, tr), lambda g, c, k: (g, k, c)),
                  pl.BlockSpec((TK, D), lambda g, c, k: (k, 0))],
        out_shape=[jax.ShapeDtypeStruct((ng * cdim, D), BF)],
        out_specs=[pl.BlockSpec((tr, D), lambda g, c, k: (g * nc + c, 0))],
        scratch_shapes=[pltpu.VMEM((tr, D), F32)])


def _mix_out_bwd(dh2, ya, yb, z, a1, lng, lnb, wsq, comm=None):
    def body(dh_ref, ya_ref, yb_ref, ga_ref, gb_ref, a1_ref, lng_ref, lnb_ref, wa_ref, wb_ref, wo_ref,
             dzg_ref, da1_ref, dq_ref, m_ref, a3_ref, dya_ref, dyb_ref, s_ref):
        @pl.when(pl.program_id(0) == 0)
        def _():
            s_ref[...] = jnp.zeros_like(s_ref)

        dm = _nt(dh_ref[...].astype(BF), wo_ref[...])
        ya, yb = ya_ref[...], yb_ref[...]
        sa, sb = _sig(ga_ref[...]), _sig(gb_ref[...])
        m_ref[0] = (sa * ya + sb * yb).astype(BF)
        dzg_ref[0] = (dm * ya * (sa * (1.0 - sa))).astype(BF)
        dzg_ref[1] = (dm * yb * (sb * (1.0 - sb))).astype(BF)
        dya = (dm * sa).astype(BF)
        dyb = (dm * sb).astype(BF)
        dya_ref[...] = dya
        dyb_ref[...] = dyb
        dq_ref[...] = _nt(dyb, wb_ref[...])
        da3 = _nt(dya, wa_ref[...])
        lng = lng_ref[...]
        xh, rs, a2, sg = _layernorm_silu(a1_ref[...], lng, lnb_ref[...])
        a3_ref[0] = (a2 * sg).astype(BF)
        da2 = da3 * (sg * (1.0 + a2 * (1.0 - sg)))
        s_ref[0:1, :] += jnp.sum(da2 * xh, axis=0, keepdims=True)
        s_ref[1:2, :] += jnp.sum(da2, axis=0, keepdims=True)
        dxh = da2 * lng
        da1 = rs * (dxh - jnp.mean(dxh, axis=-1, keepdims=True) - xh * jnp.mean(dxh * xh, axis=-1, keepdims=True))
        da1_ref[...] = da1
        s_ref[2:3, :] += jnp.sum(da1, axis=0, keepdims=True)

    row = lambda i: (i, 0)
    row3 = lambda i: (0, i, 0)
    vec = pl.BlockSpec((1, D), lambda i: (0, 0))
    return _call(
        body, name="mix_out_bwd", grid=(T // TM,), args=[dh2, ya, yb, z, z, a1, lng, lnb, wsq, wsq, wsq], comm=comm,
        in_specs=[pl.BlockSpec((TM, D), row), pl.BlockSpec((TM, D), row), pl.BlockSpec((TM, D), row),
                  pl.BlockSpec((None, TM, D), lambda i: (5, i, 0)), pl.BlockSpec((None, TM, D), lambda i: (6, i, 0)),
                  pl.BlockSpec((TM, D), row), vec, vec] + _square_specs((0, 1, 2)),
        out_shape=[jax.ShapeDtypeStruct((2, T, D), BF), jax.ShapeDtypeStruct((T, D), F32),
                   jax.ShapeDtypeStruct((T, D), F32), jax.ShapeDtypeStruct((1, T, D), BF),
                   jax.ShapeDtypeStruct((1, T, D), BF), jax.ShapeDtypeStruct((T, D), BF),
                   jax.ShapeDtypeStruct((T, D), BF), jax.ShapeDtypeStruct((8, D), F32)],
        out_specs=[pl.BlockSpec((2, TM, D), row3), pl.BlockSpec((TM, D), row), pl.BlockSpec((TM, D), row),
                   pl.BlockSpec((1, TM, D), row3), pl.BlockSpec((1, TM, D), row3), pl.BlockSpec((TM, D), row),
                   pl.BlockSpec((TM, D), row), pl.BlockSpec((8, D), lambda i: (0, 0))])


def _conv_bwd(z, da1, dq, dzg, cw, comm=None):
    nt = T // TB
    hb = TB // HALO
    last_h = T // HALO - 1

    def body(z_ref, zp_ref, zn_ref, da1_ref, da1n_ref, dq_ref, dqn_ref, dzg_ref, cw_ref,
             dz_ref, dwa_ref, dwb_ref, apad, dypad, ppad, dvpad, acc_a, acc_b):
        t = pl.program_id(1)
        first, last = t == 0, t == nt - 1
        apad[0:HALO, :] = jnp.where(first, 0.0, zp_ref[0] * _sig(zp_ref[1]))
        apad[HALO:, :] = z_ref[0] * _sig(z_ref[1])
        ppad[0:HALO, :] = jnp.where(first, 0.0, zp_ref[3] * zp_ref[4])
        ppad[HALO:, :] = z_ref[3] * z_ref[4]
        dypad[0:TB, :] = da1_ref[...]
        dypad[TB:, :] = jnp.where(last, 0.0, da1n_ref[...])
        dvpad[0:TB, :] = dq_ref[...] * z_ref[2]
        dvpad[TB:, :] = jnp.where(last, 0.0, dqn_ref[...] * zn_ref[2])

        @pl.when(t == 0)
        def _():
            acc_a[...] = jnp.zeros_like(acc_a)
            acc_b[...] = jnp.zeros_like(acc_b)

        def chunk(r, carry):
            base = pl.multiple_of(r * CH, CH)
            rows = pl.ds(base, CH)
            dw_ = dypad[pl.ds(base, CH + HALO), :]
            da0 = jnp.zeros((CH, LANE), F32)
            for b in range(8):
                wb = _shift_up(dw_, b)
                for a in range(4):
                    o = 8 * a + b
                    if o <= KA - 1:
                        da0 = da0 + cw_ref[pl.ds(KA - 1 - o, 1), :] * wb[8 * a:8 * a + CH, :]
            z0, z1 = z_ref[0, rows, :], z_ref[1, rows, :]
            s1 = _sig(z1)
            dz_ref[0, rows, :] = (da0 * s1).astype(BF)
            dz_ref[1, rows, :] = (da0 * z0 * (s1 * (1.0 - s1))).astype(BF)
            dyc = dypad[rows, :]
            aw = apad[pl.ds(base, CH + HALO), :]
            for b in range(8):
                wb = _shift_up(aw, b)
                for a in range(5):
                    s = 8 * a + b
                    if 2 <= s <= HALO:
                        k8 = 8 * (s - 2)
                        acc_a[k8:k8 + 8, :] += _fold8(dyc * wb[8 * a:8 * a + CH, :])
            pw = ppad[pl.ds(base, CH + HALO), :]
            p6 = _shift_up(pw, 6)[24:24 + CH, :]
            p7 = _shift_up(pw, 7)[24:24 + CH, :]
            p8 = pw[32:32 + CH, :]
            wb0, wb1, wb2 = cw_ref[pl.ds(32, 1), :], cw_ref[pl.ds(33, 1), :], cw_ref[pl.ds(34, 1), :]
            v = wb0 * p6 + wb1 * p7 + wb2 * p8
            dz_ref[2, rows, :] = (dq_ref[rows, :] * v).astype(BF)
            dvw = dvpad[pl.ds(base, CH + HALO), :]
            dvc = dvw[0:CH, :]
            dp = wb2 * dvc + wb1 * _shift_up(dvw, 1)[0:CH, :] + wb0 * _shift_up(dvw, 2)[0:CH, :]
            dz_ref[3, rows, :] = (dp * z_ref[4, rows, :]).astype(BF)
            dz_ref[4, rows, :] = (dp * z_ref[3, rows, :]).astype(BF)
            acc_b[0:8, :] += _fold8(dvc * p6)
            acc_b[8:16, :] += _fold8(dvc * p7)
            acc_b[16:24, :] += _fold8(dvc * p8)
            dz_ref[5, rows, :] = dzg_ref[0, rows, :]
            dz_ref[6, rows, :] = dzg_ref[1, rows, :]
            return carry

        lax.fori_loop(0, TB // CH, chunk, 0)

        @pl.when(t == nt - 1)
        def _():
            for k in range(KA):
                dwa_ref[k:k + 1, :] = jnp.sum(acc_a[8 * k:8 * k + 8, :], axis=0, keepdims=True)
            dwa_ref[KA:32, :] = jnp.zeros((32 - KA, LANE), F32)
            for k in range(KB):
                dwb_ref[k:k + 1, :] = jnp.sum(acc_b[8 * k:8 * k + 8, :], axis=0, keepdims=True)
            dwb_ref[KB:8, :] = jnp.zeros((8 - KB, LANE), F32)

    blk = lambda c, t: (t, c)
    nxt = lambda c, t: (jnp.minimum((t + 1) * hb, last_h), c)
    return _call(
        body, name="conv_bwd", grid=(D // LANE, nt), args=[z, z, z, da1, da1, dq, dq, dzg, cw], comm=comm,
        in_specs=[pl.BlockSpec((5, TB, LANE), lambda c, t: (0, t, c)),
                  pl.BlockSpec((5, HALO, LANE), lambda c, t: (0, jnp.maximum(t * hb - 1, 0), c)),
                  pl.BlockSpec((5, HALO, LANE), lambda c, t: (0, jnp.minimum((t + 1) * hb, last_h), c)),
                  pl.BlockSpec((TB, LANE), blk), pl.BlockSpec((HALO, LANE), nxt),
                  pl.BlockSpec((TB, LANE), blk), pl.BlockSpec((HALO, LANE), nxt),
                  pl.BlockSpec((2, TB, LANE), lambda c, t: (0, t, c)),
                  pl.BlockSpec((None, 40, LANE), lambda c, t: (c, 0, 0))],
        out_shape=[jax.ShapeDtypeStruct((NG, T, D), BF), jax.ShapeDtypeStruct((32, D), F32),
                   jax.ShapeDtypeStruct((8, D), F32)],
        out_specs=[pl.BlockSpec((NG, TB, LANE), lambda c, t: (0, t, c)),
                   pl.BlockSpec((32, LANE), lambda c, t: (0, c)), pl.BlockSpec((8, LANE), lambda c, t: (0, c))],
        scratch_shapes=[pltpu.VMEM((TB + HALO, LANE), F32)] * 4
                       + [pltpu.VMEM((8 * 32, LANE), F32), pltpu.VMEM((24, LANE), F32)])


def _mix_in_bwd(dz, dh2, h1, gm, win, comm=None):
    def body(dz_ref, w_ref, dh_ref, h_ref, g_ref, o_ref, s_ref, acc):
        i, j = pl.program_id(0), pl.program_id(1)

        @pl.when((i == 0) & (j == 0))
        def _():
            s_ref[...] = jnp.zeros_like(s_ref)

        @pl.when(j == 0)
        def _():
            acc[...] = jnp.zeros_like(acc)

        acc[...] += _nn(dz_ref[...], w_ref[...])

        @pl.when(j == NG - 1)
        def _():
            dx, dg = _rmsnorm_bwd(h_ref[...], g_ref[...], acc[...])
            o_ref[...] = dh_ref[...] + dx
            s_ref[0:1, :] += dg

    row = lambda i, j: (i, 0)
    return _call(
        body, name="mix_in_bwd", grid=(T // TMI, NG), args=[dz, win, dh2, h1, gm], comm=comm,
        in_specs=[pl.BlockSpec((None, TMI, D), lambda i, j: (j, i, 0)),
                  pl.BlockSpec((D, D), lambda i, j: (j, 0)),
                  pl.BlockSpec((TMI, D), row), pl.BlockSpec((TMI, D), row), pl.BlockSpec((1, D), lambda i, j: (0, 0))],
        out_shape=[jax.ShapeDtypeStruct((T, D), F32), jax.ShapeDtypeStruct((8, D), F32)],
        out_specs=[pl.BlockSpec((TMI, D), row), pl.BlockSpec((8, D), lambda i, j: (0, 0))],
        scratch_shapes=[pltpu.VMEM((TMI, D), F32)])


def _row_tile(n, want, mult):
    for t in range(min(want, n), 0, -1):
        if n % t == 0 and t % mult == 0:
            return t
    return n


def _sum_slots(recv, name):
    ns, rows, cols = recv.shape
    tr = _row_tile(rows, 1024, 16)

    def body(r_ref, o_ref):
        s = r_ref[0].astype(F32)
        for k in range(1, ns):
            s = s + r_ref[k].astype(F32)
        o_ref[...] = s

    return _call(
        body, name=name, grid=(rows // tr,), args=[recv],
        in_specs=[pl.BlockSpec((ns, tr, cols), lambda i: (0, i, 0))],
        out_shape=[jax.ShapeDtypeStruct((rows, cols), F32)],
        out_specs=[pl.BlockSpec((tr, cols), lambda i: (i, 0))])[0]


def _pack_small(s_ffn1, s_in, s_mix, s_ffn2, s_final, dwa, dwb):
    def body(f1, mi, mo, f2, fl, wa_ref, wb_ref, v_ref, k_ref):
        for dst, (ref, row) in enumerate(((f1, 0), (mi, 0), (mo, 0), (mo, 1), (mo, 2), (f2, 0), (fl, 0), (fl, 1))):
            v_ref[dst:dst + 1, :] = ref[row:row + 1, :]
        for k in range(NDEV):
            k_ref[k, 0:32, :] = wa_ref[:, k * LANE:(k + 1) * LANE]
            k_ref[k, 32:40, :] = wb_ref[:, k * LANE:(k + 1) * LANE]

    return pl.pallas_call(
        body, name="pack_small",
        out_shape=(jax.ShapeDtypeStruct((8, D), F32), jax.ShapeDtypeStruct((NDEV, 40, LANE), F32)),
    )(s_ffn1, s_in, s_mix, s_ffn2, s_final, dwa, dwb)


def _sum_small(vecs, convs):
    def body(v_ref, k_ref, vs_ref, ks_ref, l_ref):
        s, c = v_ref[0], k_ref[0]
        for k in range(1, NDEV):
            s = s + v_ref[k]
            c = c + k_ref[k]
        vs_ref[...] = s
        ks_ref[...] = c
        l_ref[...] = jnp.broadcast_to(jnp.sum(s[7:8, :], axis=-1, keepdims=True), (8, LANE))

    return pl.pallas_call(
        body, name="sum_small",
        out_shape=(jax.ShapeDtypeStruct((8, D), F32), jax.ShapeDtypeStruct((40, LANE), F32),
                   jax.ShapeDtypeStruct((8, LANE), F32)),
    )(vecs, convs)


def _adam(gs, ws, ms, vs, name, comm=None):
    n = len(gs)
    rows, cols = ws[0].shape
    tr = _row_tile(rows, 256, 8)
    c1 = 1.0 - ADAM_B1 ** ADAM_STEP
    c2 = 1.0 - ADAM_B2 ** ADAM_STEP

    def body(*refs):
        for i in range(n):
            g, w, m, v = (refs[4 * i + k][...] for k in range(4))
            d_ref, m_ref, v_ref = refs[4 * n + 3 * i: 4 * n + 3 * i + 3]
            m2 = ADAM_B1 * m + (1.0 - ADAM_B1) * g
            v2 = ADAM_B2 * v + (1.0 - ADAM_B2) * (g * g)
            d_ref[...] = -ADAM_LR * ((m2 / c1) / (jnp.sqrt(v2 / c2) + ADAM_EPS) + ADAM_WD * w)
            m_ref[...] = m2
            v_ref[...] = v2

    spec = pl.BlockSpec((tr, cols), lambda i: (i, 0))
    args = []
    for i in range(n):
        args += [gs[i], ws[i], ms[i], vs[i]]
    outs = _call(body, name=name, grid=(rows // tr,), args=args, comm=comm, in_specs=[spec] * (4 * n),
                 out_shape=[jax.ShapeDtypeStruct((rows, cols), F32)] * (3 * n), out_specs=[spec] * (3 * n))
    return [tuple(outs[3 * i: 3 * i + 3]) for i in range(n)], outs[3 * n:]


def kernel(x, ffn1_norm, ffn1_w_gate, ffn1_w_up, ffn1_w_down, mix_norm, w_in, a_dw_w, a_dw_b, a_ln_g, a_ln_b, a_w_out, b_conv_w, b_w_out, w_o, ffn2_norm, ffn2_w_gate, ffn2_w_up, ffn2_w_down, final_norm, loss_target, m_ffn1_norm, m_ffn1_w_gate, m_ffn1_w_up, m_ffn1_w_down, m_mix_norm, m_w_in, m_a_dw_w, m_a_dw_b, m_a_ln_g, m_a_ln_b, m_a_w_out, m_b_conv_w, m_b_w_out, m_w_o, m_ffn2_norm, m_ffn2_w_gate, m_ffn2_w_up, m_ffn2_w_down, m_final_norm, v_ffn1_norm, v_ffn1_w_gate, v_ffn1_w_up, v_ffn1_w_down, v_mix_norm, v_w_in, v_a_dw_w, v_a_dw_b, v_a_ln_g, v_a_ln_b, v_a_w_out, v_b_conv_w, v_b_w_out, v_w_o, v_ffn2_norm, v_ffn2_w_gate, v_ffn2_w_up, v_ffn2_w_down, v_final_norm):
    names = ("ffn1_norm", "ffn1_w_gate", "ffn1_w_up", "ffn1_w_down", "mix_norm", "w_in", "a_dw_w", "a_dw_b",
             "a_ln_g", "a_ln_b", "a_w_out", "b_conv_w", "b_w_out", "w_o", "ffn2_norm", "ffn2_w_gate", "ffn2_w_up",
             "ffn2_w_down", "final_norm")
    w = dict(ffn1_norm=ffn1_norm, ffn1_w_gate=ffn1_w_gate, ffn1_w_up=ffn1_w_up, ffn1_w_down=ffn1_w_down,
             mix_norm=mix_norm, w_in=w_in, a_dw_w=a_dw_w, a_dw_b=a_dw_b, a_ln_g=a_ln_g, a_ln_b=a_ln_b,
             a_w_out=a_w_out, b_conv_w=b_conv_w, b_w_out=b_w_out, w_o=w_o, ffn2_norm=ffn2_norm,
             ffn2_w_gate=ffn2_w_gate, ffn2_w_up=ffn2_w_up, ffn2_w_down=ffn2_w_down, final_norm=final_norm)
    m = dict(ffn1_norm=m_ffn1_norm, ffn1_w_gate=m_ffn1_w_gate, ffn1_w_up=m_ffn1_w_up, ffn1_w_down=m_ffn1_w_down,
             mix_norm=m_mix_norm, w_in=m_w_in, a_dw_w=m_a_dw_w, a_dw_b=m_a_dw_b, a_ln_g=m_a_ln_g, a_ln_b=m_a_ln_b,
             a_w_out=m_a_w_out, b_conv_w=m_b_conv_w, b_w_out=m_b_w_out, w_o=m_w_o, ffn2_norm=m_ffn2_norm,
             ffn2_w_gate=m_ffn2_w_gate, ffn2_w_up=m_ffn2_w_up, ffn2_w_down=m_ffn2_w_down, final_norm=m_final_norm)
    v = dict(ffn1_norm=v_ffn1_norm, ffn1_w_gate=v_ffn1_w_gate, ffn1_w_up=v_ffn1_w_up, ffn1_w_down=v_ffn1_w_down,
             mix_norm=v_mix_norm, w_in=v_w_in, a_dw_w=v_a_dw_w, a_dw_b=v_a_dw_b, a_ln_g=v_a_ln_g, a_ln_b=v_a_ln_b,
             a_w_out=v_a_w_out, b_conv_w=v_b_conv_w, b_w_out=v_b_w_out, w_o=v_w_o, ffn2_norm=v_ffn2_norm,
             ffn2_w_gate=v_ffn2_w_gate, ffn2_w_up=v_ffn2_w_up, ffn2_w_down=v_ffn2_w_down, final_norm=v_final_norm)
    flat = _pack_weights(dict(wg1=ffn1_w_gate[0].T, wu1=ffn1_w_up[0].T, wd1=ffn1_w_down[0], wg2=ffn2_w_gate[0].T,
                              wu2=ffn2_w_up[0].T, wd2=ffn2_w_down[0], win=w_in[0], wa=a_w_out[0], wb=b_w_out[0],
                              wo=w_o[0]))
    cw_shard = jnp.concatenate([a_dw_w[0], jnp.zeros((1, LANE), F32), b_conv_w[0], jnp.zeros((5, LANE), F32)], axis=0)

    x2, tgt = x[0], loss_target[0]
    st_a, st_b, st_c, st_d = ("wg1", "wu1", "wd1"), ("win",), ("wa", "wb", "wo", "wg2"), ("wu2", "wd2")

    buf_a, cw = _run_comm(_join(_ag_comm(st_a, flat), _direct_comm(cw_shard, False)), "ag_ffn1")
    h1, n1, gg1, uu1, buf_b = _ffn_fwd(x2, ffn1_norm, (buf_a,) * 3, (0, F, 2 * F), "ffn1_fwd", _ag_comm(st_b, flat))
    u, z, buf_c = _mix_in(h1, mix_norm, buf_b, _ag_comm(st_c, flat))
    a1, q, buf_d = _conv_fwd(z, cw, a_dw_b, _ag_comm(st_d, flat))
    h2, ya, yb = _mix_out(a1, q, z, h1, a_ln_g, a_ln_b, buf_c)
    ffn2_bufs, ffn2_offs = (buf_c, buf_d, buf_d), (3 * D, 0, F)
    h3, n2, gg2, uu2 = _ffn_fwd(h2, ffn2_norm, ffn2_bufs, ffn2_offs, "ffn2_fwd")
    dh3, s_final = _final_loss(h3, final_norm.reshape(1, D), tgt)

    tr_f = F // 2 if (F // 2) % LANE == 0 else F
    def pair(stage, src):
        return _rs_pair_comm(stage, src)

    def chip(stage, src, pair_buf, tag):
        return _rs_chip_comm(_pair_add(stage, src, pair_buf, "pair_add_" + tag))

    dgu2, act2 = _ffn_bwd_hidden(dh3, gg2, uu2, buf_d, F, "ffn2_bwd_h")
    (gu2,) = _tn_matmul(dgu2, n2, tr_f, "dw_gu2")
    s2a, src2a = ("wg2", "wu2"), dict(wg2=(gu2, 0), wu2=(gu2, F))
    gd2, pair2a = _tn_matmul(act2, dh3, tr_f, "dw_d2", pair(s2a, src2a), scale=0.5)
    s2b, src2b = ("wd2",), dict(wd2=(gd2, 0))
    dh2, s_ffn2, recv2a, pair2b = _ffn_bwd_input(dgu2, dh3, h2, ffn2_norm, (buf_c, buf_d), (3 * D, 0), "ffn2_bwd_x",
                                                 _join(chip(s2a, src2a, pair2a, "2a"), pair(s2b, src2b)))
    dzg, da1, dq, mb, a3b, dya, dyb, s_mix, recv2b = _mix_out_bwd(dh2, ya, yb, z, a1, a_ln_g, a_ln_b, buf_c,
                                                                   chip(s2b, src2b, pair2b, "2b"))
    (go,) = _tn_matmul(mb, dh2, D, "dw_o")
    (ga,) = _tn_matmul(a3b, dya, D, "dw_a")
    (gb,) = _tn_matmul(q.reshape(1, T, D), dyb, D, "dw_b")
    ssq, srcsq = ("wa", "wb", "wo"), dict(wa=(ga, 0), wb=(gb, 0), wo=(go, 0))
    dz, dwa, dwb, pairsq = _conv_bwd(z, da1, dq, dzg, cw, pair(ssq, srcsq))
    gin, recvsq = _tn_matmul(dz, u, D, "dw_in", chip(ssq, srcsq, pairsq, "sq"))
    sin_a, sin_b, srcin = ("win/0/2",), ("win/1/2",), {"win/0/2": (gin, 0), "win/1/2": (gin, 0)}
    dh1, s_in, pairin_a, pairin_b = _mix_in_bwd(dz, dh2, h1, mix_norm, buf_b,
                                                _join(pair(sin_a, srcin), pair(sin_b, srcin)))
    dgu1, act1, recvin_a = _ffn_bwd_hidden(dh1, gg1, uu1, buf_a, 2 * F, "ffn1_bwd_h",
                                           chip(sin_a, srcin, pairin_a, "in_a"))
    gu1, recvin_b = _tn_matmul(dgu1, n1, tr_f, "dw_gu1", chip(sin_b, srcin, pairin_b, "in_b"))
    s1a, src1a = ("wg1", "wu1"), dict(wg1=(gu1, 0), wu1=(gu1, F))
    gd1, pair1a = _tn_matmul(act1, dh1, tr_f, "dw_d1", pair(s1a, src1a), scale=0.5)
    s1b, src1b = ("wd1",), dict(wd1=(gd1, 0))
    dx, s_ffn1, recv1a, pair1b = _ffn_bwd_input(dgu1, dh1, x2, ffn1_norm, (buf_a, buf_a), (0, F), "ffn1_bwd_x",
                                                _join(chip(s1a, src1a, pair1a, "1a"), pair(s1b, src1b)))
    (recv1b,) = _run_comm(chip(s1b, src1b, pair1b, "1b"), "rs_chip_1b")
    stages = ((s2a, recv2a, "2a"), (s2b, recv2b, "2b"), (ssq, recvsq, "sq"), (sin_a, recvin_a, "in_a"),
              (sin_b, recvin_b, "in_b"), (s1a, recv1a, "1a"), (s1b, recv1b, "1b"))

    gsum = {}
    for stage, recv, tag in stages:
        st, total = _Stage(stage), _sum_slots(recv, "sum_" + tag)
        for n in stage:
            gsum[n] = total[st.off[n]:st.off[n] + st.rows[n]]
    gsum["win"] = jnp.concatenate([gsum["win/0/2"], gsum["win/1/2"]], axis=0)

    vec8, convk = _pack_small(s_ffn1, s_in, s_mix, s_ffn2, s_final, dwa, dwb)
    vec_all, conv_all = _run_comm(_join(_direct_comm(vec8, False), _direct_comm(convk, True)), "xchg_small")
    vec_sum, conv_sum, loss_blk = _sum_small(vec_all, conv_all)
    loss = loss_blk[0, 0]

    g = dict(ffn1_w_gate=gsum["wg1"], ffn1_w_up=gsum["wu1"], ffn1_w_down=gsum["wd1"],
             ffn2_w_gate=gsum["wg2"], ffn2_w_up=gsum["wu2"], ffn2_w_down=gsum["wd2"], w_in=gsum["win"].T,
             a_w_out=gsum["wa"], b_w_out=gsum["wb"], w_o=gsum["wo"],
             ffn1_norm=vec_sum[0:1], mix_norm=vec_sum[1:2], a_ln_g=vec_sum[2:3], a_ln_b=vec_sum[3:4],
             a_dw_b=vec_sum[4:5], ffn2_norm=vec_sum[5:6], final_norm=vec_sum[6:7],
             a_dw_w=conv_sum[0:KA], b_conv_w=conv_sum[32:32 + KB])
    gate_up = ("ffn1_w_gate", "ffn1_w_up", "ffn2_w_gate", "ffn2_w_up")

    upd = {}

    def run(group, name, as2d=lambda a: a[0], back=lambda a, n: a.reshape(w[n].shape)):
        res, _ = _adam([g[n] for n in group], [as2d(w[n]) for n in group], [as2d(m[n]) for n in group],
                       [as2d(v[n]) for n in group], name)
        for n, r in zip(group, res):
            upd[n] = tuple(back(a, n) for a in r)

    run(gate_up, "adam_gate_up", as2d=lambda a: a[0].T, back=lambda a, n: a.T[None])
    for n in gate_up:
        g[n] = g[n].T
    run(("ffn1_w_down", "ffn2_w_down"), "adam_down")
    run(("w_in",), "adam_in")
    run(("a_w_out", "b_w_out", "w_o"), "adam_square")
    run(("a_dw_w",), "adam_dw")
    run(("b_conv_w",), "adam_conv")
    vecs = ("ffn1_norm", "mix_norm", "a_dw_b", "a_ln_g", "a_ln_b", "ffn2_norm", "final_norm")
    run(vecs, "adam_vec", as2d=lambda a: a.reshape(1, D))

    grads = [g[n].reshape(w[n].shape) for n in names]
    return (loss, dx.reshape(x.shape), *grads, *[upd[n][0] for n in names], *[upd[n][1] for n in names],
            *[upd[n][2] for n in names])
```

```python
import jax
import jax.numpy as jnp
from jax import lax
from jax.experimental import pallas as pl
from jax.experimental.pallas import tpu as pltpu

T = 4096
D = 1024
F = 2816
NG = 7
NDEV = 8
NCHIP = 4
KA, KB = 31, 3
EPS = 1e-6
ADAM_LR, ADAM_B1, ADAM_B2, ADAM_EPS, ADAM_WD, ADAM_STEP = 0.001, 0.9, 0.999, 1e-08, 0.01, 10

TM = 512
FC = 256
TB = 1024
CH = 128
HALO = 32
LANE = 128
TK = 1024
VMEM_LIMIT = 56 * 1024 * 1024

BF = jnp.bfloat16
F32 = jnp.float32
MESH = pl.DeviceIdType.MESH
ANY = pl.BlockSpec(memory_space=pl.ANY)

ORDER = ("wg1", "wu1", "wd1", "wg2", "wu2", "wd2", "win", "wa", "wb", "wo")


class _Layout:
    def __init__(self):
        fs, dis, ds = F // NDEV, NG * D // NDEV, D // NDEV
        self.rows = dict(wg1=fs, wu1=fs, wd1=fs, wg2=fs, wu2=fs, wd2=fs, win=dis, wa=ds, wb=ds, wo=ds)
        self.fl, off = {}, 0
        for n in ORDER:
            self.fl[n] = off
            off += self.rows[n]
        self.RT = off


class _Stage:
    def __init__(self, names):
        lay = _Layout()
        self.names = names
        self.rows, self.full, self.sub, self.fl = {}, {}, {}, {}
        for n in names:
            base, i, k = (n.split("/") + ["0", "1"])[:3]
            self.full[n] = lay.rows[base]
            self.rows[n] = lay.rows[base] // int(k)
            self.sub[n] = int(i) * self.rows[n]
            self.fl[n] = lay.fl[base] + self.sub[n]
        self.off, self.wc, o, w = {}, {}, 0, 0
        for n in names:
            self.off[n], self.wc[n] = o, w
            o += self.rows[n]
            w += NDEV * self.rows[n]
        self.R, self.W = o, w

    def grad_row(self, n, first, dev_lin):
        return first + dev_lin * self.full[n] + self.sub[n]


def _nt(a, b):
    return lax.dot_general(a, b, (((1,), (1,)), ((), ())), preferred_element_type=F32)


def _nn(a, b):
    return lax.dot_general(a, b, (((1,), (0,)), ((), ())), preferred_element_type=F32)


def _tn(a, b):
    return lax.dot_general(a, b, (((0,), (0,)), ((), ())), preferred_element_type=F32)


def _sig(x):
    return 1.0 / (1.0 + jnp.exp(-x))


def _position():
    return lax.axis_index("x"), lax.axis_index("y"), lax.axis_index("c")


def _peer(pos, j):
    x, y, c = pos
    return (1 - x if j & 4 else x, 1 - y if j & 2 else y, 1 - c if j & 1 else c)


def _lin(pos):
    return 4 * pos[0] + 2 * pos[1] + pos[2]


def _chip(pos):
    return 2 * pos[0] + pos[1]


class _Comm:
    def __init__(self, inputs, out_shapes, scratch, start, finish):
        self.inputs, self.out_shapes, self.scratch, self.start, self.finish = inputs, out_shapes, scratch, start, finish


def _call(body, *, name, grid, args, in_specs, out_shape, out_specs, scratch_shapes=(), comm=None,
          num_scalar_prefetch=0):
    in_specs, out_shape, out_specs, scratch_shapes = list(in_specs), list(out_shape), list(out_specs), list(scratch_shapes)
    n_in, n_out, n_scr = len(in_specs), len(out_shape), len(scratch_shapes)
    sp = num_scalar_prefetch
    if comm is None:
        kernel_fn = lambda *refs: body(*refs)
        c_in = c_out = c_scr = 0
    else:
        c_in, c_out, c_scr = len(comm.inputs), len(comm.out_shapes), len(comm.scratch)

        def kernel_fn(*refs):
            pre, refs = refs[:sp], refs[sp:]
            ins, cins = refs[:n_in], refs[n_in:n_in + c_in]
            o0 = n_in + c_in
            outs, couts = refs[o0:o0 + n_out], refs[o0 + n_out:o0 + n_out + c_out]
            s0 = o0 + n_out + c_out
            scr, cscr = refs[s0:s0 + n_scr], refs[s0 + n_scr:]
            first = pl.program_id(0) == 0
            last = pl.program_id(0) == grid[0] - 1
            for a in range(1, len(grid)):
                first = first & (pl.program_id(a) == 0)
                last = last & (pl.program_id(a) == grid[a] - 1)

            @pl.when(first)
            def _():
                comm.start(cins, couts, cscr)

            body(*pre, *ins, *outs, *scr)

            @pl.when(last)
            def _():
                comm.finish(cins, couts, cscr)

        args = list(args) + list(comm.inputs)
        in_specs += [ANY] * c_in
        out_shape += list(comm.out_shapes)
        out_specs += [ANY] * c_out
        scratch_shapes += list(comm.scratch)
    params = pltpu.CompilerParams(dimension_semantics=("arbitrary",) * len(grid), vmem_limit_bytes=VMEM_LIMIT)
    if sp:
        grid_spec = pltpu.PrefetchScalarGridSpec(num_scalar_prefetch=sp, grid=grid, in_specs=in_specs,
                                                 out_specs=out_specs, scratch_shapes=scratch_shapes)
        return pl.pallas_call(kernel_fn, name=name, grid_spec=grid_spec, out_shape=out_shape,
                              compiler_params=params)(*args)
    return pl.pallas_call(kernel_fn, name=name, grid=grid, in_specs=in_specs, out_shape=out_shape, out_specs=out_specs,
                          scratch_shapes=scratch_shapes, compiler_params=params)(*args)


def _join(a, b):
    na = (len(a.inputs), len(a.out_shapes), len(a.scratch))

    def split(refs):
        return ([r[:n] for r, n in zip(refs, na)], [r[n:] for r, n in zip(refs, na)])

    def start(*refs):
        ra, rb = split(refs)
        a.start(*ra)
        b.start(*rb)

    def finish(*refs):
        ra, rb = split(refs)
        a.finish(*ra)
        b.finish(*rb)

    return _Comm(list(a.inputs) + list(b.inputs), list(a.out_shapes) + list(b.out_shapes),
                 list(a.scratch) + list(b.scratch), start, finish)


def _run_comm(comm, name):
    def body(*refs):
        c_in, c_out = len(comm.inputs), len(comm.out_shapes)
        comm.start(refs[:c_in], refs[c_in:c_in + c_out], refs[c_in + c_out:])
        comm.finish(refs[:c_in], refs[c_in:c_in + c_out], refs[c_in + c_out:])

    return pl.pallas_call(
        body, name=name, out_shape=list(comm.out_shapes), in_specs=[ANY] * len(comm.inputs),
        out_specs=[ANY] * len(comm.out_shapes), scratch_shapes=list(comm.scratch))(*comm.inputs)


def _ag_comm(names, flat):
    st = _Stage(names)

    def parts(refs):
        (flat_ref,), (out_ref,), (send_sems, recv_sems, local_sem) = refs
        me = _position()

        def region(name, dev):
            r = st.rows[name]
            return out_ref.at[pl.ds(st.wc[name] + _lin(dev) * r, r), :]

        def own(name):
            return flat_ref.at[pl.ds(st.fl[name], st.rows[name]), :]

        def copies(k, dev, to, from_flat):
            return [pltpu.make_async_remote_copy(
                src_ref=own(n) if from_flat else region(n, dev), dst_ref=region(n, dev), send_sem=send_sems.at[k],
                recv_sem=recv_sems.at[k], device_id=to, device_id_type=MESH) for n in names]

        def whole(k):
            return pltpu.make_async_remote_copy(
                src_ref=flat_ref.at[pl.ds(0, st.R), :], dst_ref=out_ref.at[pl.ds(0, st.R), :],
                send_sem=send_sems.at[k], recv_sem=recv_sems.at[k], device_id=me, device_id_type=MESH)

        return me, region, own, copies, whole, flat_ref, out_ref, local_sem

    def start(*refs):
        me, region, own, copies, _, _, _, local_sem = parts(refs)
        for n in names:
            pltpu.make_async_copy(own(n), region(n, me), local_sem).start()
        for cp in copies(0, me, _peer(me, 1), True):
            cp.start()
        for j, bits in enumerate((4, 2, 6)):
            for cp in copies(1 + j, me, _peer(me, bits), True):
                cp.start()

    def finish(*refs):
        me, _, _, copies, whole, flat_ref, out_ref, local_sem = parts(refs)
        for j, bits in enumerate((4, 2, 6)):
            whole(1 + j).wait_recv()
            for cp in copies(4 + j, _peer(me, bits), _peer(me, 1), False):
                cp.start()
        whole(0).wait_recv()
        for j in range(3):
            whole(4 + j).wait_recv()
        for k in range(7):
            whole(k).wait_send()
        pltpu.make_async_copy(flat_ref.at[pl.ds(0, st.R), :], out_ref.at[pl.ds(0, st.R), :], local_sem).wait()

    return _Comm([flat], [jax.ShapeDtypeStruct((st.W, D), BF)],
                 [pltpu.SemaphoreType.DMA((7,)), pltpu.SemaphoreType.DMA((7,)), pltpu.SemaphoreType.DMA],
                 start, finish)


def _rs_pair_comm(names, src):
    st = _Stage(names)
    arrays = []
    for n in names:
        if not any(src[n][0] is a for a in arrays):
            arrays.append(src[n][0])
    idx = {n: [i for i, a in enumerate(arrays) if a is src[n][0]][0] for n in names}

    def slot_wait(refs):
        recv = refs[1][0]
        send_sem, recv_sem = refs[2]
        return pltpu.make_async_remote_copy(src_ref=recv, dst_ref=recv, send_sem=send_sem, recv_sem=recv_sem,
                                            device_id=_position(), device_id_type=MESH)

    def start(*refs):
        ins, (recv,), (send_sem, recv_sem) = refs
        me = _position()
        sib = _peer(me, 1)
        for q in range(NCHIP):
            dev = (q // 2, q % 2, sib[2])
            for n in names:
                r = st.rows[n]
                pltpu.make_async_remote_copy(
                    src_ref=ins[idx[n]].at[pl.ds(st.grad_row(n, src[n][1], _lin(dev)), r), :],
                    dst_ref=recv.at[q, pl.ds(st.off[n], r), :], send_sem=send_sem, recv_sem=recv_sem,
                    device_id=sib, device_id_type=MESH).start()

    def finish(*refs):
        w = slot_wait(refs)
        w.wait_recv()
        w.wait_send()

    return _Comm(arrays, [jax.ShapeDtypeStruct((NCHIP, st.R, D), BF)],
                 [pltpu.SemaphoreType.DMA, pltpu.SemaphoreType.DMA], start, finish)


def _pair_add(names, src, recv, name):
    st = _Stage(names)
    c_arr = jnp.reshape(lax.axis_index("c"), (1,)).astype(jnp.int32)

    def body(c_ref, *refs):
        r_ref, o_ref = refs[len(names)], refs[len(names) + 1]
        for a_ref, n in zip(refs, names):
            rows = slice(st.off[n], st.off[n] + st.rows[n])
            o_ref[rows, :] = (a_ref[...].astype(F32) + r_ref[rows, :].astype(F32)).astype(BF)

    def shard_spec(n):
        r = st.rows[n]
        base, step = st.grad_row(n, src[n][1], 0) // r, st.full[n] // r
        return pl.BlockSpec((r, D), lambda q, c_ref: (base + step * (2 * q + c_ref[0]), 0))

    slot = pl.BlockSpec((None, st.R, D), lambda q, c_ref: (q, 0, 0))
    return _call(body, name=name, grid=(NCHIP,), args=[c_arr] + [src[n][0] for n in names] + [recv],
                 in_specs=[shard_spec(n) for n in names] + [slot],
                 out_shape=[jax.ShapeDtypeStruct((NCHIP, st.R, D), BF)], out_specs=[slot], num_scalar_prefetch=1)[0]


def _rs_chip_comm(part):
    def copies(refs):
        (p_ref,), (recv,), (send_sems, recv_sems, local_sem) = refs
        me = _position()
        mine = pltpu.make_async_copy(p_ref.at[_chip(me)], recv.at[_chip(me)], local_sem)
        out = []
        for j, bits in enumerate((4, 2, 6)):
            to = _peer(me, bits)
            out.append(pltpu.make_async_remote_copy(
                src_ref=p_ref.at[_chip(to)], dst_ref=recv.at[_chip(me)], send_sem=send_sems.at[j],
                recv_sem=recv_sems.at[j], device_id=to, device_id_type=MESH))
        return mine, out

    def start(*refs):
        mine, out = copies(refs)
        mine.start()
        for cp in out:
            cp.start()

    def finish(*refs):
        mine, out = copies(refs)
        for cp in out:
            cp.wait_recv()
        for cp in out:
            cp.wait_send()
        mine.wait()

    return _Comm([part], [jax.ShapeDtypeStruct(part.shape, BF)],
                 [pltpu.SemaphoreType.DMA((3,)), pltpu.SemaphoreType.DMA((3,)), pltpu.SemaphoreType.DMA],
                 start, finish)


def _direct_comm(x, scatter):
    def copies(refs):
        (x_ref,), (out_ref,), (send_sems, recv_sems, local_sem) = refs
        me = _position()

        def piece(dev):
            return x_ref.at[_lin(dev)] if scatter else x_ref

        mine = pltpu.make_async_copy(piece(me), out_ref.at[_lin(me)], local_sem)
        return mine, [pltpu.make_async_remote_copy(
            src_ref=piece(_peer(me, j)), dst_ref=out_ref.at[_lin(me)], send_sem=send_sems.at[j - 1],
            recv_sem=recv_sems.at[j - 1], device_id=_peer(me, j), device_id_type=MESH) for j in range(1, NDEV)]

    def start(*refs):
        mine, cps = copies(refs)
        mine.start()
        for cp in cps:
            cp.start()

    def finish(*refs):
        mine, cps = copies(refs)
        for cp in cps:
            cp.wait_recv()
        for cp in cps:
            cp.wait_send()
        mine.wait()

    shape = x.shape if scatter else (NDEV,) + x.shape
    return _Comm([x], [jax.ShapeDtypeStruct(shape, x.dtype)],
                 [pltpu.SemaphoreType.DMA((7,)), pltpu.SemaphoreType.DMA((7,)), pltpu.SemaphoreType.DMA],
                 start, finish)


def _pack_weights(shards):
    lay = _Layout()

    def body(*refs):
        o_ref = refs[-1]
        for ref, n in zip(refs, ORDER):
            x = ref[...].T if n == "win" else ref[...]
            o_ref[lay.fl[n]:lay.fl[n] + lay.rows[n], :] = x.astype(BF)

    return pl.pallas_call(
        body, name="pack_weights", out_shape=jax.ShapeDtypeStruct((lay.RT, D), BF),
        compiler_params=pltpu.CompilerParams(vmem_limit_bytes=VMEM_LIMIT))(*[shards[n] for n in ORDER])


def _load_ffn_weights(srcs, offs, scratch, sem):
    @pl.when(pl.program_id(0) == 0)
    def _():
        cps = [pltpu.make_async_copy(s.at[pl.ds(off, dst.shape[0]), :], dst, sem.at[i])
               for i, (s, off, dst) in enumerate(zip(srcs, offs, scratch))]
        for cp in cps:
            cp.start()
        for cp in cps:
            cp.wait()


def _final_loss_tile(xf, g, tgt, s_ref):
    r = lax.rsqrt(jnp.mean(xf * xf, axis=-1, keepdims=True) + EPS)
    xr = xf * r
    e = xr * g - tgt
    s_ref[1:2, :] += jnp.sum(e * e, axis=0, keepdims=True) * (0.5 / D)
    dy = e * (1.0 / D)
    s_ref[0:1, :] += jnp.sum(dy * xr, axis=0, keepdims=True)
    gdy = dy * g
    return r * gdy - xr * (r * jnp.mean(gdy * xr, axis=-1, keepdims=True))


def _ffn_fwd(x, g, wbufs, offs, name, comm=None, final=None):
    nf = F // FC

    def body(x_ref, g_ref, b0, b1, b2, *rest):
        if final is None:
            h_ref, n_ref, gg_ref, uu_ref, wg_s, wu_s, wd_s, sem = rest
        else:
            gf_ref, t_ref, dh_ref, s_ref, n_ref, gg_ref, uu_ref, wg_s, wu_s, wd_s, sem = rest

            @pl.when(pl.program_id(0) == 0)
            def _():
                s_ref[...] = jnp.zeros_like(s_ref)

        _load_ffn_weights((b0, b1, b2), offs, (wg_s, wu_s, wd_s), sem)
        xf = x_ref[...]
        r = lax.rsqrt(jnp.mean(xf * xf, axis=-1, keepdims=True) + EPS)
        nb = (xf * r * g_ref[...]).astype(BF)
        n_ref[...] = nb
        acc = jnp.zeros((TM, D), F32)
        for c in range(nf):
            sl = slice(c * FC, (c + 1) * FC)
            gb = _nt(nb, wg_s[sl, :]).astype(BF)
            ub = _nt(nb, wu_s[sl, :]).astype(BF)
            gg_ref[:, sl] = gb
            uu_ref[:, sl] = ub
            acc = acc + _nn((gb * _sig(gb)) * ub, wd_s[sl, :])
        h = xf + 0.5 * acc
        if final is None:
            h_ref[...] = h
        else:
            dh_ref[...] = _final_loss_tile(h, gf_ref[...], t_ref[...], s_ref)

    row = lambda i: (i, 0)
    vec = pl.BlockSpec((1, D), lambda i: (0, 0))
    tile = pl.BlockSpec((TM, D), row)
    saved_shapes = [jax.ShapeDtypeStruct((T, D), BF), jax.ShapeDtypeStruct((T, F), BF), jax.ShapeDtypeStruct((T, F), BF)]
    saved_specs = [tile, pl.BlockSpec((TM, F), row), pl.BlockSpec((TM, F), row)]
    if final is None:
        extra_args, extra_specs = [], []
        head_shapes, head_specs = [jax.ShapeDtypeStruct((T, D), F32)], [tile]
    else:
        extra_args, extra_specs = list(final), [vec, tile]
        head_shapes = [jax.ShapeDtypeStruct((T, D), F32), jax.ShapeDtypeStruct((8, D), F32)]
        head_specs = [tile, pl.BlockSpec((8, D), lambda i: (0, 0))]
    return _call(
        body, name=name, grid=(T // TM,), args=[x, g, *wbufs, *extra_args], comm=comm,
        in_specs=[tile, vec, ANY, ANY, ANY] + extra_specs,
        out_shape=head_shapes + saved_shapes, out_specs=head_specs + saved_specs,
        scratch_shapes=[pltpu.VMEM((F, D), BF)] * 3 + [pltpu.SemaphoreType.DMA((3,))])


def _mix_in(h1, gm, win, comm=None):
    def body(h_ref, g_ref, w_any, u_ref, z_ref, w_s, sem):
        _load_ffn_weights((w_any,), (0,), (w_s,), sem)
        xf = h_ref[...]
        r = lax.rsqrt(jnp.mean(xf * xf, axis=-1, keepdims=True) + EPS)
        ub = (xf * r * g_ref[...]).astype(BF)
        u_ref[...] = ub
        for j in range(NG):
            z_ref[j] = _nt(ub, w_s[j * D:(j + 1) * D, :]).astype(BF)

    row = lambda i: (i, 0)
    return _call(
        body, name="mix_in", grid=(T // TM,), args=[h1, gm, win], comm=comm,
        in_specs=[pl.BlockSpec((TM, D), row), pl.BlockSpec((1, D), lambda i: (0, 0)), ANY],
        out_shape=[jax.ShapeDtypeStruct((T, D), BF), jax.ShapeDtypeStruct((NG, T, D), BF)],
        out_specs=[pl.BlockSpec((TM, D), row), pl.BlockSpec((NG, TM, D), lambda i: (0, i, 0))],
        scratch_shapes=[pltpu.VMEM((NG * D, D), BF), pltpu.SemaphoreType.DMA((1,))])


def _shift_up(w, b):
    return w if b == 0 else pltpu.roll(w, w.shape[0] - b, 0)


def _fold8(p):
    red = p[0:8, :]
    for i in range(1, p.shape[0] // 8):
        red = red + p[8 * i:8 * i + 8, :]
    return red


def _conv_fwd(z, cw, bias, comm=None):
    nt = T // TB
    hb = TB // HALO

    def body(z_ref, zh_ref, cw_ref, b_ref, a1_ref, q_ref, apad, ppad):
        first = pl.program_id(1) == 0
        f = lambda ref, j: ref[j].astype(F32)
        apad[0:HALO, :] = jnp.where(first, 0.0, f(zh_ref, 0) * _sig(f(zh_ref, 1)))
        apad[HALO:, :] = f(z_ref, 0) * _sig(f(z_ref, 1))
        ppad[0:HALO, :] = jnp.where(first, 0.0, f(zh_ref, 3) * f(zh_ref, 4))
        ppad[HALO:, :] = f(z_ref, 3) * f(z_ref, 4)
        bias_row = b_ref[...]

        def chunk(r, carry):
            base = pl.multiple_of(r * CH, CH)
            w = apad[pl.ds(base, CH + HALO), :]
            acc = jnp.broadcast_to(bias_row, (CH, LANE))
            for b in range(8):
                wb = _shift_up(w, b)
                for a in range(5):
                    s = 8 * a + b
                    if 2 <= s <= HALO:
                        acc = acc + cw_ref[pl.ds(s - 2, 1), :] * wb[8 * a:8 * a + CH, :]
            a1_ref[pl.ds(base, CH), :] = acc.astype(BF)
            pw = ppad[pl.ds(base, CH + HALO), :]
            v = (cw_ref[pl.ds(32, 1), :] * _shift_up(pw, 6)[24:24 + CH, :]
                 + cw_ref[pl.ds(33, 1), :] * _shift_up(pw, 7)[24:24 + CH, :]
                 + cw_ref[pl.ds(34, 1), :] * pw[32:32 + CH, :])
            q_ref[pl.ds(base, CH), :] = (z_ref[2, pl.ds(base, CH), :].astype(F32) * v).astype(BF)
            return carry

        lax.fori_loop(0, TB // CH, chunk, 0)

    return _call(
        body, name="conv_fwd", grid=(D // LANE, nt), args=[z, z, cw, bias], comm=comm,
        in_specs=[pl.BlockSpec((5, TB, LANE), lambda c, t: (0, t, c)),
                  pl.BlockSpec((5, HALO, LANE), lambda c, t: (0, jnp.maximum(t * hb - 1, 0), c)),
                  pl.BlockSpec((None, 40, LANE), lambda c, t: (c, 0, 0)),
                  pl.BlockSpec((1, LANE), lambda c, t: (0, c))],
        out_shape=[jax.ShapeDtypeStruct((T, D), BF), jax.ShapeDtypeStruct((T, D), BF)],
        out_specs=[pl.BlockSpec((TB, LANE), lambda c, t: (t, c)), pl.BlockSpec((TB, LANE), lambda c, t: (t, c))],
        scratch_shapes=[pltpu.VMEM((TB + HALO, LANE), F32), pltpu.VMEM((TB + HALO, LANE), F32)])


def _layernorm_silu(a1, lng, lnb):
    mu = jnp.mean(a1, axis=-1, keepdims=True)
    xc = a1 - mu
    rs = lax.rsqrt(jnp.mean(xc * xc, axis=-1, keepdims=True) + EPS)
    xh = xc * rs
    a2 = xh * lng + lnb
    sg = _sig(a2)
    return xh, rs, a2, sg


def _square_specs(blocks):
    return [pl.BlockSpec((D, D), lambda i, b=b: (b, 0)) for b in blocks]


def _mix_out(a1, q, z, h1, lng, lnb, wsq, comm=None):
    def body(a1_ref, q_ref, ga_ref, gb_ref, h_ref, lng_ref, lnb_ref, wa_ref, wb_ref, wo_ref, h2_ref, ya_ref, yb_ref):
        _, _, a2, sg = _layernorm_silu(a1_ref[...].astype(F32), lng_ref[...], lnb_ref[...])
        ya = _nn((a2 * sg).astype(BF), wa_ref[...])
        yb = _nn(q_ref[...], wb_ref[...])
        ya_ref[...] = ya.astype(BF)
        yb_ref[...] = yb.astype(BF)
        m = _sig(ga_ref[...].astype(F32)) * ya + _sig(gb_ref[...].astype(F32)) * yb
        h2_ref[...] = h_ref[...] + _nn(m.astype(BF), wo_ref[...])

    row = lambda i: (i, 0)
    vec = pl.BlockSpec((1, D), lambda i: (0, 0))
    return _call(
        body, name="mix_out", grid=(T // TM,), args=[a1, q, z, z, h1, lng, lnb, wsq, wsq, wsq], comm=comm,
        in_specs=[pl.BlockSpec((TM, D), row), pl.BlockSpec((TM, D), row),
                  pl.BlockSpec((None, TM, D), lambda i: (5, i, 0)), pl.BlockSpec((None, TM, D), lambda i: (6, i, 0)),
                  pl.BlockSpec((TM, D), row), vec, vec] + _square_specs((0, 1, 2)),
        out_shape=[jax.ShapeDtypeStruct((T, D), F32), jax.ShapeDtypeStruct((T, D), BF), jax.ShapeDtypeStruct((T, D), BF)],
        out_specs=[pl.BlockSpec((TM, D), row)] * 3)


def _rmsnorm_bwd(xf, g, dn):
    r = lax.rsqrt(jnp.mean(xf * xf, axis=-1, keepdims=True) + EPS)
    xr = xf * r
    gdn = dn * g
    dx = r * gdn - xr * (r * jnp.mean(gdn * xr, axis=-1, keepdims=True))
    return dx, jnp.sum(dn * xr, axis=0, keepdims=True)


def _ffn_bwd_hidden(dh, gg, uu, wbuf, off, name, comm=None):
    nf = F // FC

    def body(dh_ref, gg_ref, uu_ref, b0, dgu_ref, a_ref, wd_s, sem):
        _load_ffn_weights((b0,), (off,), (wd_s,), sem)
        dhb = (0.5 * dh_ref[...]).astype(BF)
        for c in range(nf):
            sl = slice(c * FC, (c + 1) * FC)
            da = _nt(dhb, wd_s[sl, :]).astype(BF)
            gb, ub = gg_ref[:, sl], uu_ref[:, sl]
            sg = _sig(gb)
            silu = gb * sg
            dgu_ref[0, :, sl] = (da * ub) * (sg * (1.0 + gb * (1.0 - sg)))
            dgu_ref[1, :, sl] = da * silu
            a_ref[0, :, sl] = silu * ub

    row = lambda i: (i, 0)
    return _call(
        body, name=name, grid=(T // TM,), args=[dh, gg, uu, wbuf], comm=comm,
        in_specs=[pl.BlockSpec((TM, D), row), pl.BlockSpec((TM, F), row), pl.BlockSpec((TM, F), row), ANY],
        out_shape=[jax.ShapeDtypeStruct((2, T, F), BF), jax.ShapeDtypeStruct((1, T, F), BF)],
        out_specs=[pl.BlockSpec((2, TM, F), lambda i: (0, i, 0)), pl.BlockSpec((1, TM, F), lambda i: (0, i, 0))],
        scratch_shapes=[pltpu.VMEM((F, D), BF), pltpu.SemaphoreType.DMA((1,))])


def _ffn_bwd_input(dgu, dh, x, g, wbufs, offs, name, comm=None):
    def body(dgu_ref, dh_ref, x_ref, g_ref, b0, b1, dx_ref, s_ref, wg_s, wu_s, sem):
        _load_ffn_weights((b0, b1), offs, (wg_s, wu_s), sem)

        @pl.when(pl.program_id(0) == 0)
        def _():
            s_ref[...] = jnp.zeros_like(s_ref)

        dn = _nn(dgu_ref[0], wg_s[...]) + _nn(dgu_ref[1], wu_s[...])
        dxn, dg = _rmsnorm_bwd(x_ref[...], g_ref[...], dn)
        dx_ref[...] = dh_ref[...] + dxn
        s_ref[0:1, :] += dg

    row = lambda i: (i, 0)
    return _call(
        body, name=name, grid=(T // TM,), args=[dgu, dh, x, g, *wbufs], comm=comm,
        in_specs=[pl.BlockSpec((2, TM, F), lambda i: (0, i, 0)), pl.BlockSpec((TM, D), row),
                  pl.BlockSpec((TM, D), row), pl.BlockSpec((1, D), lambda i: (0, 0)), ANY, ANY],
        out_shape=[jax.ShapeDtypeStruct((T, D), F32), jax.ShapeDtypeStruct((8, D), F32)],
        out_specs=[pl.BlockSpec((TM, D), row), pl.BlockSpec((8, D), lambda i: (0, 0))],
        scratch_shapes=[pltpu.VMEM((F, D), BF)] * 2 + [pltpu.SemaphoreType.DMA((2,))])


def _tn_matmul(lhs, rhs, tr, name, comm=None, scale=None):
    ng, _, cdim = lhs.shape
    nc, nk = cdim // tr, T // TK

    def body(l_ref, r_ref, o_ref, acc):
        k = pl.program_id(2)

        @pl.when(k == 0)
        def _():
            acc[...] = jnp.zeros_like(acc)

        r = r_ref[...] if scale is None else scale * r_ref[...]
        acc[...] += _tn(l_ref[...], r.astype(BF))

        @pl.when(k == nk - 1)
        def _():
            o_ref[...] = acc[...].astype(BF)

    return _call(
        body, name=name, grid=(ng, nc, nk), args=[lhs, rhs], comm=comm,
        in_specs=[pl.BlockSpec((None, TK, tr), lambda g, c, k: (g, k, c)),
                  pl.BlockSpec((TK, D), lambda g, c, k: (k, 0))],
        out_shape=[jax.ShapeDtypeStruct((ng * cdim, D), BF)],
        out_specs=[pl.BlockSpec((tr, D), lambda g, c, k: (g * nc + c, 0))],
        scratch_shapes=[pltpu.VMEM((tr, D), F32)])


def _mix_out_bwd(dh2, ya, yb, z, a1, lng, lnb, wsq, comm=None):
    def body(dh_ref, ya_ref, yb_ref, ga_ref, gb_ref, a1_ref, lng_ref, lnb_ref, wa_ref, wb_ref, wo_ref,
             dzg_ref, da1_ref, dq_ref, m_ref, a3_ref, dya_ref, dyb_ref, s_ref):
        @pl.when(pl.program_id(0) == 0)
        def _():
            s_ref[...] = jnp.zeros_like(s_ref)

        dm = _nt(dh_ref[...].astype(BF), wo_ref[...])
        ya, yb = ya_ref[...].astype(F32), yb_ref[...].astype(F32)
        sa, sb = _sig(ga_ref[...].astype(F32)), _sig(gb_ref[...].astype(F32))
        m_ref[0] = (sa * ya + sb * yb).astype(BF)
        dzg_ref[0] = (dm * ya * (sa * (1.0 - sa))).astype(BF)
        dzg_ref[1] = (dm * yb * (sb * (1.0 - sb))).astype(BF)
        dya = (dm * sa).astype(BF)
        dyb = (dm * sb).astype(BF)
        dya_ref[...] = dya
        dyb_ref[...] = dyb
        dq_ref[...] = _nt(dyb, wb_ref[...]).astype(BF)
        da3 = _nt(dya, wa_ref[...])
        lng = lng_ref[...]
        xh, rs, a2, sg = _layernorm_silu(a1_ref[...].astype(F32), lng, lnb_ref[...])
        a3_ref[0] = (a2 * sg).astype(BF)
        da2 = da3 * (sg * (1.0 + a2 * (1.0 - sg)))
        s_ref[0:1, :] += jnp.sum(da2 * xh, axis=0, keepdims=True)
        s_ref[1:2, :] += jnp.sum(da2, axis=0, keepdims=True)
        dxh = da2 * lng
        da1 = rs * (dxh - jnp.mean(dxh, axis=-1, keepdims=True) - xh * jnp.mean(dxh * xh, axis=-1, keepdims=True))
        da1_ref[...] = da1.astype(BF)
        s_ref[2:3, :] += jnp.sum(da1, axis=0, keepdims=True)

    row = lambda i: (i, 0)
    row3 = lambda i: (0, i, 0)
    vec = pl.BlockSpec((1, D), lambda i: (0, 0))
    return _call(
        body, name="mix_out_bwd", grid=(T // TM,), args=[dh2, ya, yb, z, z, a1, lng, lnb, wsq, wsq, wsq], comm=comm,
        in_specs=[pl.BlockSpec((TM, D), row), pl.BlockSpec((TM, D), row), pl.BlockSpec((TM, D), row),
                  pl.BlockSpec((None, TM, D), lambda i: (5, i, 0)), pl.BlockSpec((None, TM, D), lambda i: (6, i, 0)),
                  pl.BlockSpec((TM, D), row), vec, vec] + _square_specs((0, 1, 2)),
        out_shape=[jax.ShapeDtypeStruct((2, T, D), BF), jax.ShapeDtypeStruct((T, D), BF),
                   jax.ShapeDtypeStruct((T, D), BF), jax.ShapeDtypeStruct((1, T, D), BF),
                   jax.ShapeDtypeStruct((1, T, D), BF), jax.ShapeDtypeStruct((T, D), BF),
                   jax.ShapeDtypeStruct((T, D), BF), jax.ShapeDtypeStruct((8, D), F32)],
        out_specs=[pl.BlockSpec((2, TM, D), row3), pl.BlockSpec((TM, D), row), pl.BlockSpec((TM, D), row),
                   pl.BlockSpec((1, TM, D), row3), pl.BlockSpec((1, TM, D), row3), pl.BlockSpec((TM, D), row),
                   pl.BlockSpec((TM, D), row), pl.BlockSpec((8, D), lambda i: (0, 0))])


def _conv_bwd(z, da1, dq, dzg, cw, comm=None):
    nt = T // TB
    hb = TB // HALO
    last_h = T // HALO - 1

    def body(z_ref, zp_ref, zn_ref, da1_ref, da1n_ref, dq_ref, dqn_ref, dzg_ref, cw_ref,
             dz_ref, dwa_ref, dwb_ref, apad, dypad, ppad, dvpad, acc_a, acc_b):
        t = pl.program_id(1)
        first, last = t == 0, t == nt - 1
        f = lambda ref, j: ref[j].astype(F32)
        apad[0:HALO, :] = jnp.where(first, 0.0, f(zp_ref, 0) * _sig(f(zp_ref, 1)))
        apad[HALO:, :] = f(z_ref, 0) * _sig(f(z_ref, 1))
        ppad[0:HALO, :] = jnp.where(first, 0.0, f(zp_ref, 3) * f(zp_ref, 4))
        ppad[HALO:, :] = f(z_ref, 3) * f(z_ref, 4)
        dypad[0:TB, :] = da1_ref[...].astype(F32)
        dypad[TB:, :] = jnp.where(last, 0.0, da1n_ref[...].astype(F32))
        dvpad[0:TB, :] = dq_ref[...].astype(F32) * f(z_ref, 2)
        dvpad[TB:, :] = jnp.where(last, 0.0, dqn_ref[...].astype(F32) * f(zn_ref, 2))

        @pl.when(t == 0)
        def _():
            acc_a[...] = jnp.zeros_like(acc_a)
            acc_b[...] = jnp.zeros_like(acc_b)

        def chunk(r, carry):
            base = pl.multiple_of(r * CH, CH)
            rows = pl.ds(base, CH)
            dw_ = dypad[pl.ds(base, CH + HALO), :]
            da0 = jnp.zeros((CH, LANE), F32)
            for b in range(8):
                wb = _shift_up(dw_, b)
                for a in range(4):
                    o = 8 * a + b
                    if o <= KA - 1:
                        da0 = da0 + cw_ref[pl.ds(KA - 1 - o, 1), :] * wb[8 * a:8 * a + CH, :]
            z0, z1 = z_ref[0, rows, :].astype(F32), z_ref[1, rows, :].astype(F32)
            s1 = _sig(z1)
            dz_ref[0, rows, :] = (da0 * s1).astype(BF)
            dz_ref[1, rows, :] = (da0 * z0 * (s1 * (1.0 - s1))).astype(BF)
            dyc = dypad[rows, :]
            aw = apad[pl.ds(base, CH + HALO), :]
            for b in range(8):
                wb = _shift_up(aw, b)
                for a in range(5):
                    s = 8 * a + b
                    if 2 <= s <= HALO:
                        k8 = 8 * (s - 2)
                        acc_a[k8:k8 + 8, :] += _fold8(dyc * wb[8 * a:8 * a + CH, :])
            pw = ppad[pl.ds(base, CH + HALO), :]
            p6 = _shift_up(pw, 6)[24:24 + CH, :]
            p7 = _shift_up(pw, 7)[24:24 + CH, :]
            p8 = pw[32:32 + CH, :]
            wb0, wb1, wb2 = cw_ref[pl.ds(32, 1), :], cw_ref[pl.ds(33, 1), :], cw_ref[pl.ds(34, 1), :]
            v = wb0 * p6 + wb1 * p7 + wb2 * p8
            dz_ref[2, rows, :] = (dq_ref[rows, :].astype(F32) * v).astype(BF)
            dvw = dvpad[pl.ds(base, CH + HALO), :]
            dvc = dvw[0:CH, :]
            dp = wb2 * dvc + wb1 * _shift_up(dvw, 1)[0:CH, :] + wb0 * _shift_up(dvw, 2)[0:CH, :]
            dz_ref[3, rows, :] = (dp * z_ref[4, rows, :].astype(F32)).astype(BF)
            dz_ref[4, rows, :] = (dp * z_ref[3, rows, :].astype(F32)).astype(BF)
            acc_b[0:8, :] += _fold8(dvc * p6)
            acc_b[8:16, :] += _fold8(dvc * p7)
            acc_b[16:24, :] += _fold8(dvc * p8)
            dz_ref[5, rows, :] = dzg_ref[0, rows, :]
            dz_ref[6, rows, :] = dzg_ref[1, rows, :]
            return carry

        lax.fori_loop(0, TB // CH, chunk, 0)

        @pl.when(t == nt - 1)
        def _():
            for k in range(KA):
                dwa_ref[k:k + 1, :] = jnp.sum(acc_a[8 * k:8 * k + 8, :], axis=0, keepdims=True)
            dwa_ref[KA:32, :] = jnp.zeros((32 - KA, LANE), F32)
            for k in range(KB):
                dwb_ref[k:k + 1, :] = jnp.sum(acc_b[8 * k:8 * k + 8, :], axis=0, keepdims=True)
            dwb_ref[KB:8, :] = jnp.zeros((8 - KB, LANE), F32)

    blk = lambda c, t: (t, c)
    nxt = lambda c, t: (jnp.minimum((t + 1) * hb, last_h), c)
    return _call(
        body, name="conv_bwd", grid=(D // LANE, nt), args=[z, z, z, da1, da1, dq, dq, dzg, cw], comm=comm,
        in_specs=[pl.BlockSpec((5, TB, LANE), lambda c, t: (0, t, c)),
                  pl.BlockSpec((5, HALO, LANE), lambda c, t: (0, jnp.maximum(t * hb - 1, 0), c)),
                  pl.BlockSpec((5, HALO, LANE), lambda c, t: (0, jnp.minimum((t + 1) * hb, last_h), c)),
                  pl.BlockSpec((TB, LANE), blk), pl.BlockSpec((HALO, LANE), nxt),
                  pl.BlockSpec((TB, LANE), blk), pl.BlockSpec((HALO, LANE), nxt),
                  pl.BlockSpec((2, TB, LANE), lambda c, t: (0, t, c)),
                  pl.BlockSpec((None, 40, LANE), lambda c, t: (c, 0, 0))],
        out_shape=[jax.ShapeDtypeStruct((NG, T, D), BF), jax.ShapeDtypeStruct((32, D), F32),
                   jax.ShapeDtypeStruct((8, D), F32)],
        out_specs=[pl.BlockSpec((NG, TB, LANE), lambda c, t: (0, t, c)),
                   pl.BlockSpec((32, LANE), lambda c, t: (0, c)), pl.BlockSpec((8, LANE), lambda c, t: (0, c))],
        scratch_shapes=[pltpu.VMEM((TB + HALO, LANE), F32)] * 4
                       + [pltpu.VMEM((8 * 32, LANE), F32), pltpu.VMEM((24, LANE), F32)])


def _mix_in_bwd(dz, dh2, h1, gm, win, comm=None):
    def body(dz_ref, w_any, dh_ref, h_ref, g_ref, o_ref, s_ref, w_s, sem):
        _load_ffn_weights((w_any,), (0,), (w_s,), sem)

        @pl.when(pl.program_id(0) == 0)
        def _():
            s_ref[...] = jnp.zeros_like(s_ref)

        du = _nn(dz_ref[0], w_s[0:D, :])
        for j in range(1, NG):
            du = du + _nn(dz_ref[j], w_s[j * D:(j + 1) * D, :])
        dx, dg = _rmsnorm_bwd(h_ref[...], g_ref[...], du)
        o_ref[...] = dh_ref[...] + dx
        s_ref[0:1, :] += dg

    row = lambda i: (i, 0)
    return _call(
        body, name="mix_in_bwd", grid=(T // TM,), args=[dz, win, dh2, h1, gm], comm=comm,
        in_specs=[pl.BlockSpec((NG, TM, D), lambda i: (0, i, 0)), ANY,
                  pl.BlockSpec((TM, D), row), pl.BlockSpec((TM, D), row), pl.BlockSpec((1, D), lambda i: (0, 0))],
        out_shape=[jax.ShapeDtypeStruct((T, D), F32), jax.ShapeDtypeStruct((8, D), F32)],
        out_specs=[pl.BlockSpec((TM, D), row), pl.BlockSpec((8, D), lambda i: (0, 0))],
        scratch_shapes=[pltpu.VMEM((NG * D, D), BF), pltpu.SemaphoreType.DMA((1,))])


def _row_tile(n, want, mult):
    for t in range(min(want, n), 0, -1):
        if n % t == 0 and t % mult == 0:
            return t
    return n


def _sum_slots(recv, name):
    ns, rows, cols = recv.shape
    tr = _row_tile(rows, 1024, 16)

    def body(r_ref, o_ref):
        s = r_ref[0].astype(F32)
        for k in range(1, ns):
            s = s + r_ref[k].astype(F32)
        o_ref[...] = s

    return _call(
        body, name=name, grid=(rows // tr,), args=[recv],
        in_specs=[pl.BlockSpec((ns, tr, cols), lambda i: (0, i, 0))],
        out_shape=[jax.ShapeDtypeStruct((rows, cols), F32)],
        out_specs=[pl.BlockSpec((tr, cols), lambda i: (i, 0))])[0]


def _pack_small(s_ffn1, s_in, s_mix, s_ffn2, s_final, dwa, dwb):
    def body(f1, mi, mo, f2, fl, wa_ref, wb_ref, v_ref, k_ref):
        for dst, (ref, row) in enumerate(((f1, 0), (mi, 0), (mo, 0), (mo, 1), (mo, 2), (f2, 0), (fl, 0), (fl, 1))):
            v_ref[dst:dst + 1, :] = ref[row:row + 1, :]
        for k in range(NDEV):
            k_ref[k, 0:32, :] = wa_ref[:, k * LANE:(k + 1) * LANE]
            k_ref[k, 32:40, :] = wb_ref[:, k * LANE:(k + 1) * LANE]

    return pl.pallas_call(
        body, name="pack_small",
        out_shape=(jax.ShapeDtypeStruct((8, D), F32), jax.ShapeDtypeStruct((NDEV, 40, LANE), F32)),
    )(s_ffn1, s_in, s_mix, s_ffn2, s_final, dwa, dwb)


def _sum_small(vecs, convs):
    def body(v_ref, k_ref, vs_ref, ks_ref, l_ref):
        s, c = v_ref[0], k_ref[0]
        for k in range(1, NDEV):
            s = s + v_ref[k]
            c = c + k_ref[k]
        vs_ref[...] = s
        ks_ref[...] = c
        l_ref[...] = jnp.broadcast_to(jnp.sum(s[7:8, :], axis=-1, keepdims=True), (8, LANE))

    return pl.pallas_call(
        body, name="sum_small",
        out_shape=(jax.ShapeDtypeStruct((8, D), F32), jax.ShapeDtypeStruct((40, LANE), F32),
                   jax.ShapeDtypeStruct((8, LANE), F32)),
    )(vecs, convs)


def _adam(gs, ws, ms, vs, name, comm=None):
    n = len(gs)
    rows, cols = ws[0].shape
    tr = _row_tile(rows, 256, 8)
    c1 = 1.0 - ADAM_B1 ** ADAM_STEP
    c2 = 1.0 - ADAM_B2 ** ADAM_STEP

    def body(*refs):
        for i in range(n):
            g, w, m, v = (refs[4 * i + k][...] for k in range(4))
            d_ref, m_ref, v_ref = refs[4 * n + 3 * i: 4 * n + 3 * i + 3]
            m2 = ADAM_B1 * m + (1.0 - ADAM_B1) * g
            v2 = ADAM_B2 * v + (1.0 - ADAM_B2) * (g * g)
            d_ref[...] = -ADAM_LR * ((m2 / c1) / (jnp.sqrt(v2 / c2) + ADAM_EPS) + ADAM_WD * w)
            m_ref[...] = m2
            v_ref[...] = v2

    spec = pl.BlockSpec((tr, cols), lambda i: (i, 0))
    args = []
    for i in range(n):
        args += [gs[i], ws[i], ms[i], vs[i]]
    outs = _call(body, name=name, grid=(rows // tr,), args=args, comm=comm, in_specs=[spec] * (4 * n),
                 out_shape=[jax.ShapeDtypeStruct((rows, cols), F32)] * (3 * n), out_specs=[spec] * (3 * n))
    return [tuple(outs[3 * i: 3 * i + 3]) for i in range(n)], outs[3 * n:]


def kernel(x, ffn1_norm, ffn1_w_gate, ffn1_w_up, ffn1_w_down, mix_norm, w_in, a_dw_w, a_dw_b, a_ln_g, a_ln_b, a_w_out, b_conv_w, b_w_out, w_o, ffn2_norm, ffn2_w_gate, ffn2_w_up, ffn2_w_down, final_norm, loss_target, m_ffn1_norm, m_ffn1_w_gate, m_ffn1_w_up, m_ffn1_w_down, m_mix_norm, m_w_in, m_a_dw_w, m_a_dw_b, m_a_ln_g, m_a_ln_b, m_a_w_out, m_b_conv_w, m_b_w_out, m_w_o, m_ffn2_norm, m_ffn2_w_gate, m_ffn2_w_up, m_ffn2_w_down, m_final_norm, v_ffn1_norm, v_ffn1_w_gate, v_ffn1_w_up, v_ffn1_w_down, v_mix_norm, v_w_in, v_a_dw_w, v_a_dw_b, v_a_ln_g, v_a_ln_b, v_a_w_out, v_b_conv_w, v_b_w_out, v_w_o, v_ffn2_norm, v_ffn2_w_gate, v_ffn2_w_up, v_ffn2_w_down, v_final_norm):
    names = ("ffn1_norm", "ffn1_w_gate", "ffn1_w_up", "ffn1_w_down", "mix_norm", "w_in", "a_dw_w", "a_dw_b",
             "a_ln_g", "a_ln_b", "a_w_out", "b_conv_w", "b_w_out", "w_o", "ffn2_norm", "ffn2_w_gate", "ffn2_w_up",
             "ffn2_w_down", "final_norm")
    w = dict(ffn1_norm=ffn1_norm, ffn1_w_gate=ffn1_w_gate, ffn1_w_up=ffn1_w_up, ffn1_w_down=ffn1_w_down,
             mix_norm=mix_norm, w_in=w_in, a_dw_w=a_dw_w, a_dw_b=a_dw_b, a_ln_g=a_ln_g, a_ln_b=a_ln_b,
             a_w_out=a_w_out, b_conv_w=b_conv_w, b_w_out=b_w_out, w_o=w_o, ffn2_norm=ffn2_norm,
             ffn2_w_gate=ffn2_w_gate, ffn2_w_up=ffn2_w_up, ffn2_w_down=ffn2_w_down, final_norm=final_norm)
    m = dict(ffn1_norm=m_ffn1_norm, ffn1_w_gate=m_ffn1_w_gate, ffn1_w_up=m_ffn1_w_up, ffn1_w_down=m_ffn1_w_down,
             mix_norm=m_mix_norm, w_in=m_w_in, a_dw_w=m_a_dw_w, a_dw_b=m_a_dw_b, a_ln_g=m_a_ln_g, a_ln_b=m_a_ln_b,
             a_w_out=m_a_w_out, b_conv_w=m_b_conv_w, b_w_out=m_b_w_out, w_o=m_w_o, ffn2_norm=m_ffn2_norm,
             ffn2_w_gate=m_ffn2_w_gate, ffn2_w_up=m_ffn2_w_up, ffn2_w_down=m_ffn2_w_down, final_norm=m_final_norm)
    v = dict(ffn1_norm=v_ffn1_norm, ffn1_w_gate=v_ffn1_w_gate, ffn1_w_up=v_ffn1_w_up, ffn1_w_down=v_ffn1_w_down,
             mix_norm=v_mix_norm, w_in=v_w_in, a_dw_w=v_a_dw_w, a_dw_b=v_a_dw_b, a_ln_g=v_a_ln_g, a_ln_b=v_a_ln_b,
             a_w_out=v_a_w_out, b_conv_w=v_b_conv_w, b_w_out=v_b_w_out, w_o=v_w_o, ffn2_norm=v_ffn2_norm,
             ffn2_w_gate=v_ffn2_w_gate, ffn2_w_up=v_ffn2_w_up, ffn2_w_down=v_ffn2_w_down, final_norm=v_final_norm)
    flat = _pack_weights(dict(wg1=ffn1_w_gate[0].T, wu1=ffn1_w_up[0].T, wd1=ffn1_w_down[0], wg2=ffn2_w_gate[0].T,
                              wu2=ffn2_w_up[0].T, wd2=ffn2_w_down[0], win=w_in[0], wa=a_w_out[0], wb=b_w_out[0],
                              wo=w_o[0]))
    cw_shard = jnp.concatenate([a_dw_w[0], jnp.zeros((1, LANE), F32), b_conv_w[0], jnp.zeros((5, LANE), F32)], axis=0)

    x2, tgt = x[0], loss_target[0]
    st_a, st_b, st_c, st_d = ("wg1", "wu1", "wd1"), ("win",), ("wa", "wb", "wo", "wg2"), ("wu2", "wd2")

    buf_a, cw = _run_comm(_join(_ag_comm(st_a, flat), _direct_comm(cw_shard, False)), "ag_ffn1")
    h1, n1, gg1, uu1, buf_b = _ffn_fwd(x2, ffn1_norm, (buf_a,) * 3, (0, F, 2 * F), "ffn1_fwd", _ag_comm(st_b, flat))
    u, z, buf_c = _mix_in(h1, mix_norm, buf_b, _ag_comm(st_c, flat))
    a1, q, buf_d = _conv_fwd(z, cw, a_dw_b, _ag_comm(st_d, flat))
    h2, ya, yb = _mix_out(a1, q, z, h1, a_ln_g, a_ln_b, buf_c)
    ffn2_bufs, ffn2_offs = (buf_c, buf_d, buf_d), (3 * D, 0, F)
    dh3, s_final, n2, gg2, uu2 = _ffn_fwd(h2, ffn2_norm, ffn2_bufs, ffn2_offs, "ffn2_fwd",
                                          final=(final_norm.reshape(1, D), tgt))

    tr_f = F // 2 if (F // 2) % LANE == 0 else F
    def pair(stage, src):
        return _rs_pair_comm(stage, src)

    def chip(stage, src, pair_buf, tag):
        return _rs_chip_comm(_pair_add(stage, src, pair_buf, "pair_add_" + tag))

    dgu2, act2 = _ffn_bwd_hidden(dh3, gg2, uu2, buf_d, F, "ffn2_bwd_h")
    (gu2,) = _tn_matmul(dgu2, n2, tr_f, "dw_gu2")
    s2a, src2a = ("wg2", "wu2"), dict(wg2=(gu2, 0), wu2=(gu2, F))
    gd2, pair2a = _tn_matmul(act2, dh3, tr_f, "dw_d2", pair(s2a, src2a), scale=0.5)
    s2b, src2b = ("wd2",), dict(wd2=(gd2, 0))
    dh2, s_ffn2, recv2a, pair2b = _ffn_bwd_input(dgu2, dh3, h2, ffn2_norm, (buf_c, buf_d), (3 * D, 0), "ffn2_bwd_x",
                                                 _join(chip(s2a, src2a, pair2a, "2a"), pair(s2b, src2b)))
    dzg, da1, dq, mb, a3b, dya, dyb, s_mix, recv2b = _mix_out_bwd(dh2, ya, yb, z, a1, a_ln_g, a_ln_b, buf_c,
                                                                   chip(s2b, src2b, pair2b, "2b"))
    (go,) = _tn_matmul(mb, dh2, D, "dw_o")
    (ga,) = _tn_matmul(a3b, dya, D, "dw_a")
    (gb,) = _tn_matmul(q.reshape(1, T, D), dyb, D, "dw_b")
    ssq, srcsq = ("wa", "wb", "wo"), dict(wa=(ga, 0), wb=(gb, 0), wo=(go, 0))
    dz, dwa, dwb, pairsq = _conv_bwd(z, da1, dq, dzg, cw, pair(ssq, srcsq))
    gin, recvsq = _tn_matmul(dz, u, D, "dw_in", chip(ssq, srcsq, pairsq, "sq"))
    sin_a, sin_b, srcin = ("win/0/2",), ("win/1/2",), {"win/0/2": (gin, 0), "win/1/2": (gin, 0)}
    dh1, s_in, pairin_a, pairin_b = _mix_in_bwd(dz, dh2, h1, mix_norm, buf_b,
                                                _join(pair(sin_a, srcin), pair(sin_b, srcin)))
    dgu1, act1, recvin_a = _ffn_bwd_hidden(dh1, gg1, uu1, buf_a, 2 * F, "ffn1_bwd_h",
                                           chip(sin_a, srcin, pairin_a, "in_a"))
    gu1, recvin_b = _tn_matmul(dgu1, n1, tr_f, "dw_gu1", chip(sin_b, srcin, pairin_b, "in_b"))
    s1a, src1a = ("wg1", "wu1"), dict(wg1=(gu1, 0), wu1=(gu1, F))
    gd1, pair1a = _tn_matmul(act1, dh1, tr_f, "dw_d1", pair(s1a, src1a), scale=0.5)
    s1b, src1b = ("wd1",), dict(wd1=(gd1, 0))
    dx, s_ffn1, recv1a, pair1b = _ffn_bwd_input(dgu1, dh1, x2, ffn1_norm, (buf_a, buf_a), (0, F), "ffn1_bwd_x",
                                                _join(chip(s1a, src1a, pair1a, "1a"), pair(s1b, src1b)))
    (recv1b,) = _run_comm(chip(s1b, src1b, pair1b, "1b"), "rs_chip_1b")
    stages = ((s2a, recv2a, "2a"), (s2b, recv2b, "2b"), (ssq, recvsq, "sq"), (sin_a, recvin_a, "in_a"),
              (sin_b, recvin_b, "in_b"), (s1a, recv1a, "1a"), (s1b, recv1b, "1b"))

    gsum = {}
    for stage, recv, tag in stages:
        st, total = _Stage(stage), _sum_slots(recv, "sum_" + tag)
        for n in stage:
            gsum[n] = total[st.off[n]:st.off[n] + st.rows[n]]
    gsum["win"] = jnp.concatenate([gsum["win/0/2"], gsum["win/1/2"]], axis=0)

    vec8, convk = _pack_small(s_ffn1, s_in, s_mix, s_ffn2, s_final, dwa, dwb)
    vec_all, conv_all = _run_comm(_join(_direct_comm(vec8, False), _direct_comm(convk, True)), "xchg_small")
    vec_sum, conv_sum, loss_blk = _sum_small(vec_all, conv_all)
    loss = loss_blk[0, 0]

    g = dict(ffn1_w_gate=gsum["wg1"], ffn1_w_up=gsum["wu1"], ffn1_w_down=gsum["wd1"],
             ffn2_w_gate=gsum["wg2"], ffn2_w_up=gsum["wu2"], ffn2_w_down=gsum["wd2"], w_in=gsum["win"].T,
             a_w_out=gsum["wa"], b_w_out=gsum["wb"], w_o=gsum["wo"],
             ffn1_norm=vec_sum[0:1], mix_norm=vec_sum[1:2], a_ln_g=vec_sum[2:3], a_ln_b=vec_sum[3:4],
             a_dw_b=vec_sum[4:5], ffn2_norm=vec_sum[5:6], final_norm=vec_sum[6:7],
             a_dw_w=conv_sum[0:KA], b_conv_w=conv_sum[32:32 + KB])
    gate_up = ("ffn1_w_gate", "ffn1_w_up", "ffn2_w_gate", "ffn2_w_up")

    upd = {}

    def run(group, name, as2d=lambda a: a[0], back=lambda a, n: a.reshape(w[n].shape)):
        res, _ = _adam([g[n] for n in group], [as2d(w[n]) for n in group], [as2d(m[n]) for n in group],
                       [as2d(v[n]) for n in group], name)
        for n, r in zip(group, res):
            upd[n] = tuple(back(a, n) for a in r)

    run(gate_up, "adam_gate_up", as2d=lambda a: a[0].T, back=lambda a, n: a.T[None])
    for n in gate_up:
        g[n] = g[n].T
    run(("ffn1_w_down", "ffn2_w_down"), "adam_down")
    run(("w_in",), "adam_in")
    run(("a_w_out", "b_w_out", "w_o"), "adam_square")
    run(("a_dw_w",), "adam_dw")
    run(("b_conv_w",), "adam_conv")
    vecs = ("ffn1_norm", "mix_norm", "a_dw_b", "a_ln_g", "a_ln_b", "ffn2_norm", "final_norm")
    run(vecs, "adam_vec", as2d=lambda a: a.reshape(1, D))

    grads = [g[n].reshape(w[n].shape) for n in names]
    return (loss, dx.reshape(x.shape), *grads, *[upd[n][0] for n in names], *[upd[n][1] for n in names],
            *[upd[n][2] for n in names])
```

```python
import jax
import jax.numpy as jnp
from jax import lax
from jax.experimental import pallas as pl
from jax.experimental.pallas import tpu as pltpu

T = 4096
D = 1024
F = 2816
NG = 7
NDEV = 8
NCHIP = 4
KA, KB = 31, 3
EPS = 1e-6
ADAM_LR, ADAM_B1, ADAM_B2, ADAM_EPS, ADAM_WD, ADAM_STEP = 0.001, 0.9, 0.999, 1e-08, 0.01, 10

TM = 512
FC = 256
TB = 1024
NB = 256
HB = NB // 2
CW = 256
CHB = 64
LANE = 128
TK = 1024
VMEM_LIMIT = 56 * 1024 * 1024

BF = jnp.bfloat16
F32 = jnp.float32
MESH = pl.DeviceIdType.MESH
ANY = pl.BlockSpec(memory_space=pl.ANY)

ORDER = ("wg1", "wu1", "wd1", "wg2", "wu2", "wd2", "win", "wa", "wb", "wo")


class _Layout:
    def __init__(self):
        fs, dis, ds = F // NDEV, NG * D // NDEV, D // NDEV
        self.rows = dict(wg1=fs, wu1=fs, wd1=fs, wg2=fs, wu2=fs, wd2=fs, win=dis, wa=ds, wb=ds, wo=ds)
        self.fl, off = {}, 0
        for n in ORDER:
            self.fl[n] = off
            off += self.rows[n]
        self.RT = off


class _Stage:
    def __init__(self, names):
        lay = _Layout()
        self.names = names
        self.rows, self.full, self.sub, self.fl = {}, {}, {}, {}
        for n in names:
            base, i, k = (n.split("/") + ["0", "1"])[:3]
            self.full[n] = lay.rows[base]
            self.rows[n] = lay.rows[base] // int(k)
            self.sub[n] = int(i) * self.rows[n]
            self.fl[n] = lay.fl[base] + self.sub[n]
        self.off, self.wc, o, w = {}, {}, 0, 0
        for n in names:
            self.off[n], self.wc[n] = o, w
            o += self.rows[n]
            w += NDEV * self.rows[n]
        self.R, self.W = o, w

    def grad_row(self, n, first, dev_lin):
        return first + dev_lin * self.full[n] + self.sub[n]


def _nt(a, b):
    return lax.dot_general(a, b, (((1,), (1,)), ((), ())), preferred_element_type=F32)


def _nn(a, b):
    return lax.dot_general(a, b, (((1,), (0,)), ((), ())), preferred_element_type=F32)


def _tn(a, b):
    return lax.dot_general(a, b, (((0,), (0,)), ((), ())), preferred_element_type=F32)


def _sig(x):
    return 1.0 / (1.0 + jnp.exp(-x))


def _position():
    return lax.axis_index("x"), lax.axis_index("y"), lax.axis_index("c")


def _peer(pos, j):
    x, y, c = pos
    return (1 - x if j & 4 else x, 1 - y if j & 2 else y, 1 - c if j & 1 else c)


def _lin(pos):
    return 4 * pos[0] + 2 * pos[1] + pos[2]


def _chip(pos):
    return 2 * pos[0] + pos[1]


class _Comm:
    def __init__(self, inputs, out_shapes, scratch, start, finish):
        self.inputs, self.out_shapes, self.scratch, self.start, self.finish = inputs, out_shapes, scratch, start, finish


def _call(body, *, name, grid, args, in_specs, out_shape, out_specs, scratch_shapes=(), comm=None,
          num_scalar_prefetch=0):
    in_specs, out_shape, out_specs, scratch_shapes = list(in_specs), list(out_shape), list(out_specs), list(scratch_shapes)
    n_in, n_out, n_scr = len(in_specs), len(out_shape), len(scratch_shapes)
    sp = num_scalar_prefetch
    if comm is None:
        kernel_fn = lambda *refs: body(*refs)
        c_in = c_out = c_scr = 0
    else:
        c_in, c_out, c_scr = len(comm.inputs), len(comm.out_shapes), len(comm.scratch)

        def kernel_fn(*refs):
            pre, refs = refs[:sp], refs[sp:]
            ins, cins = refs[:n_in], refs[n_in:n_in + c_in]
            o0 = n_in + c_in
            outs, couts = refs[o0:o0 + n_out], refs[o0 + n_out:o0 + n_out + c_out]
            s0 = o0 + n_out + c_out
            scr, cscr = refs[s0:s0 + n_scr], refs[s0 + n_scr:]
            first = pl.program_id(0) == 0
            last = pl.program_id(0) == grid[0] - 1
            for a in range(1, len(grid)):
                first = first & (pl.program_id(a) == 0)
                last = last & (pl.program_id(a) == grid[a] - 1)

            @pl.when(first)
            def _():
                comm.start(cins, couts, cscr)

            body(*pre, *ins, *outs, *scr)

            @pl.when(last)
            def _():
                comm.finish(cins, couts, cscr)

        args = list(args) + list(comm.inputs)
        in_specs += [ANY] * c_in
        out_shape += list(comm.out_shapes)
        out_specs += [ANY] * c_out
        scratch_shapes += list(comm.scratch)
    params = pltpu.CompilerParams(dimension_semantics=("arbitrary",) * len(grid), vmem_limit_bytes=VMEM_LIMIT)
    if sp:
        grid_spec = pltpu.PrefetchScalarGridSpec(num_scalar_prefetch=sp, grid=grid, in_specs=in_specs,
                                                 out_specs=out_specs, scratch_shapes=scratch_shapes)
        return pl.pallas_call(kernel_fn, name=name, grid_spec=grid_spec, out_shape=out_shape,
                              compiler_params=params)(*args)
    return pl.pallas_call(kernel_fn, name=name, grid=grid, in_specs=in_specs, out_shape=out_shape, out_specs=out_specs,
                          scratch_shapes=scratch_shapes, compiler_params=params)(*args)


def _join(a, b):
    na = (len(a.inputs), len(a.out_shapes), len(a.scratch))

    def split(refs):
        return ([r[:n] for r, n in zip(refs, na)], [r[n:] for r, n in zip(refs, na)])

    def start(*refs):
        ra, rb = split(refs)
        a.start(*ra)
        b.start(*rb)

    def finish(*refs):
        ra, rb = split(refs)
        a.finish(*ra)
        b.finish(*rb)

    return _Comm(list(a.inputs) + list(b.inputs), list(a.out_shapes) + list(b.out_shapes),
                 list(a.scratch) + list(b.scratch), start, finish)


def _run_comm(comm, name):
    def body(*refs):
        c_in, c_out = len(comm.inputs), len(comm.out_shapes)
        comm.start(refs[:c_in], refs[c_in:c_in + c_out], refs[c_in + c_out:])
        comm.finish(refs[:c_in], refs[c_in:c_in + c_out], refs[c_in + c_out:])

    return pl.pallas_call(
        body, name=name, out_shape=list(comm.out_shapes), in_specs=[ANY] * len(comm.inputs),
        out_specs=[ANY] * len(comm.out_shapes), scratch_shapes=list(comm.scratch))(*comm.inputs)


def _ag_comm(names, flat):
    st = _Stage(names)

    def parts(refs):
        (flat_ref,), (out_ref,), (send_sems, recv_sems, local_sem) = refs
        me = _position()

        def region(name, dev):
            r = st.rows[name]
            return out_ref.at[pl.ds(st.wc[name] + _lin(dev) * r, r), :]

        def own(name):
            return flat_ref.at[pl.ds(st.fl[name], st.rows[name]), :]

        def copies(k, dev, to, from_flat):
            return [pltpu.make_async_remote_copy(
                src_ref=own(n) if from_flat else region(n, dev), dst_ref=region(n, dev), send_sem=send_sems.at[k],
                recv_sem=recv_sems.at[k], device_id=to, device_id_type=MESH) for n in names]

        def whole(k):
            return pltpu.make_async_remote_copy(
                src_ref=flat_ref.at[pl.ds(0, st.R), :], dst_ref=out_ref.at[pl.ds(0, st.R), :],
                send_sem=send_sems.at[k], recv_sem=recv_sems.at[k], device_id=me, device_id_type=MESH)

        return me, region, own, copies, whole, flat_ref, out_ref, local_sem

    def start(*refs):
        me, region, own, copies, _, _, _, local_sem = parts(refs)
        for n in names:
            pltpu.make_async_copy(own(n), region(n, me), local_sem).start()
        for cp in copies(0, me, _peer(me, 1), True):
            cp.start()
        for j, bits in enumerate((4, 2, 6)):
            for cp in copies(1 + j, me, _peer(me, bits), True):
                cp.start()

    def finish(*refs):
        me, _, _, copies, whole, flat_ref, out_ref, local_sem = parts(refs)
        for j, bits in enumerate((4, 2, 6)):
            whole(1 + j).wait_recv()
            for cp in copies(4 + j, _peer(me, bits), _peer(me, 1), False):
                cp.start()
        whole(0).wait_recv()
        for j in range(3):
            whole(4 + j).wait_recv()
        for k in range(7):
            whole(k).wait_send()
        pltpu.make_async_copy(flat_ref.at[pl.ds(0, st.R), :], out_ref.at[pl.ds(0, st.R), :], local_sem).wait()

    return _Comm([flat], [jax.ShapeDtypeStruct((st.W, D), BF)],
                 [pltpu.SemaphoreType.DMA((7,)), pltpu.SemaphoreType.DMA((7,)), pltpu.SemaphoreType.DMA],
                 start, finish)


def _rs_pair_comm(names, src):
    st = _Stage(names)
    arrays = []
    for n in names:
        if not any(src[n][0] is a for a in arrays):
            arrays.append(src[n][0])
    idx = {n: [i for i, a in enumerate(arrays) if a is src[n][0]][0] for n in names}

    def slot_wait(refs):
        recv = refs[1][0]
        send_sem, recv_sem = refs[2]
        return pltpu.make_async_remote_copy(src_ref=recv, dst_ref=recv, send_sem=send_sem, recv_sem=recv_sem,
                                            device_id=_position(), device_id_type=MESH)

    def start(*refs):
        ins, (recv,), (send_sem, recv_sem) = refs
        me = _position()
        sib = _peer(me, 1)
        for q in range(NCHIP):
            dev = (q // 2, q % 2, sib[2])
            for n in names:
                r = st.rows[n]
                pltpu.make_async_remote_copy(
                    src_ref=ins[idx[n]].at[pl.ds(st.grad_row(n, src[n][1], _lin(dev)), r), :],
                    dst_ref=recv.at[q, pl.ds(st.off[n], r), :], send_sem=send_sem, recv_sem=recv_sem,
                    device_id=sib, device_id_type=MESH).start()

    def finish(*refs):
        w = slot_wait(refs)
        w.wait_recv()
        w.wait_send()

    return _Comm(arrays, [jax.ShapeDtypeStruct((NCHIP, st.R, D), BF)],
                 [pltpu.SemaphoreType.DMA, pltpu.SemaphoreType.DMA], start, finish)


def _pair_add(names, src, recv, name):
    st = _Stage(names)
    c_arr = jnp.reshape(lax.axis_index("c"), (1,)).astype(jnp.int32)

    def body(c_ref, *refs):
        r_ref, o_ref = refs[len(names)], refs[len(names) + 1]
        for a_ref, n in zip(refs, names):
            rows = slice(st.off[n], st.off[n] + st.rows[n])
            o_ref[rows, :] = (a_ref[...].astype(F32) + r_ref[rows, :].astype(F32)).astype(BF)

    def shard_spec(n):
        r = st.rows[n]
        base, step = st.grad_row(n, src[n][1], 0) // r, st.full[n] // r
        return pl.BlockSpec((r, D), lambda q, c_ref: (base + step * (2 * q + c_ref[0]), 0))

    slot = pl.BlockSpec((None, st.R, D), lambda q, c_ref: (q, 0, 0))
    return _call(body, name=name, grid=(NCHIP,), args=[c_arr] + [src[n][0] for n in names] + [recv],
                 in_specs=[shard_spec(n) for n in names] + [slot],
                 out_shape=[jax.ShapeDtypeStruct((NCHIP, st.R, D), BF)], out_specs=[slot], num_scalar_prefetch=1)[0]


def _rs_chip_comm(part):
    def copies(refs):
        (p_ref,), (recv,), (send_sems, recv_sems, local_sem) = refs
        me = _position()
        mine = pltpu.make_async_copy(p_ref.at[_chip(me)], recv.at[_chip(me)], local_sem)
        out = []
        for j, bits in enumerate((4, 2, 6)):
            to = _peer(me, bits)
            out.append(pltpu.make_async_remote_copy(
                src_ref=p_ref.at[_chip(to)], dst_ref=recv.at[_chip(me)], send_sem=send_sems.at[j],
                recv_sem=recv_sems.at[j], device_id=to, device_id_type=MESH))
        return mine, out

    def start(*refs):
        mine, out = copies(refs)
        mine.start()
        for cp in out:
            cp.start()

    def finish(*refs):
        mine, out = copies(refs)
        for cp in out:
            cp.wait_recv()
        for cp in out:
            cp.wait_send()
        mine.wait()

    return _Comm([part], [jax.ShapeDtypeStruct(part.shape, BF)],
                 [pltpu.SemaphoreType.DMA((3,)), pltpu.SemaphoreType.DMA((3,)), pltpu.SemaphoreType.DMA],
                 start, finish)


def _direct_comm(x, scatter):
    def copies(refs):
        (x_ref,), (out_ref,), (send_sems, recv_sems, local_sem) = refs
        me = _position()

        def piece(dev):
            return x_ref.at[_lin(dev)] if scatter else x_ref

        mine = pltpu.make_async_copy(piece(me), out_ref.at[_lin(me)], local_sem)
        return mine, [pltpu.make_async_remote_copy(
            src_ref=piece(_peer(me, j)), dst_ref=out_ref.at[_lin(me)], send_sem=send_sems.at[j - 1],
            recv_sem=recv_sems.at[j - 1], device_id=_peer(me, j), device_id_type=MESH) for j in range(1, NDEV)]

    def start(*refs):
        mine, cps = copies(refs)
        mine.start()
        for cp in cps:
            cp.start()

    def finish(*refs):
        mine, cps = copies(refs)
        for cp in cps:
            cp.wait_recv()
        for cp in cps:
            cp.wait_send()
        mine.wait()

    shape = x.shape if scatter else (NDEV,) + x.shape
    return _Comm([x], [jax.ShapeDtypeStruct(shape, x.dtype)],
                 [pltpu.SemaphoreType.DMA((7,)), pltpu.SemaphoreType.DMA((7,)), pltpu.SemaphoreType.DMA],
                 start, finish)


def _pack_weights(shards):
    lay = _Layout()

    def body(*refs):
        o_ref = refs[-1]
        for ref, n in zip(refs, ORDER):
            x = ref[...].T if n == "win" else ref[...]
            o_ref[lay.fl[n]:lay.fl[n] + lay.rows[n], :] = x.astype(BF)

    return pl.pallas_call(
        body, name="pack_weights", out_shape=jax.ShapeDtypeStruct((lay.RT, D), BF),
        compiler_params=pltpu.CompilerParams(vmem_limit_bytes=VMEM_LIMIT))(*[shards[n] for n in ORDER])


def _load_ffn_weights(srcs, offs, scratch, sem):
    @pl.when(pl.program_id(0) == 0)
    def _():
        cps = [pltpu.make_async_copy(s.at[pl.ds(off, dst.shape[0]), :], dst, sem.at[i])
               for i, (s, off, dst) in enumerate(zip(srcs, offs, scratch))]
        for cp in cps:
            cp.start()
        for cp in cps:
            cp.wait()


def _final_loss_tile(xf, g, tgt, s_ref):
    r = lax.rsqrt(jnp.mean(xf * xf, axis=-1, keepdims=True) + EPS)
    xr = xf * r
    e = xr * g - tgt
    s_ref[1:2, :] += jnp.sum(e * e, axis=0, keepdims=True) * (0.5 / D)
    dy = e * (1.0 / D)
    s_ref[0:1, :] += jnp.sum(dy * xr, axis=0, keepdims=True)
    gdy = dy * g
    return r * gdy - xr * (r * jnp.mean(gdy * xr, axis=-1, keepdims=True))


def _ffn_fwd(x, g, wbufs, offs, name, comm=None, final=None):
    nf = F // FC

    def body(x_ref, g_ref, b0, b1, b2, *rest):
        if final is None:
            h_ref, n_ref, gg_ref, uu_ref, wg_s, wu_s, wd_s, sem = rest
        else:
            gf_ref, t_ref, dh_ref, s_ref, n_ref, gg_ref, uu_ref, wg_s, wu_s, wd_s, sem = rest

            @pl.when(pl.program_id(0) == 0)
            def _():
                s_ref[...] = jnp.zeros_like(s_ref)

        _load_ffn_weights((b0, b1, b2), offs, (wg_s, wu_s, wd_s), sem)
        xf = x_ref[...]
        r = lax.rsqrt(jnp.mean(xf * xf, axis=-1, keepdims=True) + EPS)
        nb = (xf * r * g_ref[...]).astype(BF)
        n_ref[...] = nb
        acc = jnp.zeros((TM, D), F32)
        for c in range(nf):
            sl = slice(c * FC, (c + 1) * FC)
            gb = _nt(nb, wg_s[sl, :]).astype(BF)
            ub = _nt(nb, wu_s[sl, :]).astype(BF)
            gg_ref[:, sl] = gb
            uu_ref[:, sl] = ub
            acc = acc + _nn((gb * _sig(gb)) * ub, wd_s[sl, :])
        h = xf + 0.5 * acc
        if final is None:
            h_ref[...] = h
        else:
            dh_ref[...] = _final_loss_tile(h, gf_ref[...], t_ref[...], s_ref)

    row = lambda i: (i, 0)
    vec = pl.BlockSpec((1, D), lambda i: (0, 0))
    tile = pl.BlockSpec((TM, D), row)
    saved_shapes = [jax.ShapeDtypeStruct((T, D), BF), jax.ShapeDtypeStruct((T, F), BF), jax.ShapeDtypeStruct((T, F), BF)]
    saved_specs = [tile, pl.BlockSpec((TM, F), row), pl.BlockSpec((TM, F), row)]
    if final is None:
        extra_args, extra_specs = [], []
        head_shapes, head_specs = [jax.ShapeDtypeStruct((T, D), F32)], [tile]
    else:
        extra_args, extra_specs = list(final), [vec, tile]
        head_shapes = [jax.ShapeDtypeStruct((T, D), F32), jax.ShapeDtypeStruct((8, D), F32)]
        head_specs = [tile, pl.BlockSpec((8, D), lambda i: (0, 0))]
    return _call(
        body, name=name, grid=(T // TM,), args=[x, g, *wbufs, *extra_args], comm=comm,
        in_specs=[tile, vec, ANY, ANY, ANY] + extra_specs,
        out_shape=head_shapes + saved_shapes, out_specs=head_specs + saved_specs,
        scratch_shapes=[pltpu.VMEM((F, D), BF)] * 3 + [pltpu.SemaphoreType.DMA((3,))])


def _mix_in(h1, gm, win, comm=None):
    def body(h_ref, g_ref, w_any, u_ref, z_ref, w_s, sem):
        _load_ffn_weights((w_any,), (0,), (w_s,), sem)
        xf = h_ref[...]
        r = lax.rsqrt(jnp.mean(xf * xf, axis=-1, keepdims=True) + EPS)
        ub = (xf * r * g_ref[...]).astype(BF)
        u_ref[...] = ub
        for j in range(NG):
            z_ref[j] = _nt(ub, w_s[j * D:(j + 1) * D, :]).astype(BF)

    row = lambda i: (i, 0)
    return _call(
        body, name="mix_in", grid=(T // TM,), args=[h1, gm, win], comm=comm,
        in_specs=[pl.BlockSpec((TM, D), row), pl.BlockSpec((1, D), lambda i: (0, 0)), ANY],
        out_shape=[jax.ShapeDtypeStruct((T, D), BF), jax.ShapeDtypeStruct((NG, T, D), BF)],
        out_specs=[pl.BlockSpec((TM, D), row), pl.BlockSpec((NG, TM, D), lambda i: (0, i, 0))],
        scratch_shapes=[pltpu.VMEM((NG * D, D), BF), pltpu.SemaphoreType.DMA((1,))])


def _shift_up(w, b):
    return w if b == 0 else pltpu.roll(w, w.shape[0] - b, 0)


def _fold8(p):
    red = p[0:8, :]
    for i in range(1, p.shape[0] // 8):
        red = red + p[8 * i:8 * i + 8, :]
    return red


def _dft_constants():
    import numpy as np
    n = np.arange(NB)
    ang = 2.0 * np.pi / NB * np.outer(n, n)
    taps = np.arange(32)
    angk = 2.0 * np.pi / NB * np.outer(n, KA - 1 - taps)
    valid = (taps < KA)[None, :]
    kc = np.where(valid, np.cos(angk), 0.0) / NB
    ks = np.where(valid, np.sin(angk), 0.0) / NB

    def split(a):
        hi = jnp.asarray(a, F32).astype(BF)
        lo = (jnp.asarray(a, F32) - hi.astype(F32)).astype(BF)
        return hi, lo

    return dict(cf=jnp.asarray(np.cos(ang), F32).astype(BF), sf=jnp.asarray(np.sin(ang), F32).astype(BF),
                kc=split(kc), ks=split(ks), rc=split(kc.T), rs=split(ks.T))


def _dot3(m_hi, m_lo, x):
    x_hi = x.astype(BF)
    x_lo = (x - x_hi.astype(F32)).astype(BF)
    return _nn(m_hi, x_hi) + _nn(m_hi, x_lo) + _nn(m_lo, x_hi)


def _dft_specs():
    full = lambda shape: pl.BlockSpec(shape, lambda c, t: (0,) * len(shape))
    return [full((NB, NB)), full((NB, NB)), full((NB, 32)), full((NB, 32)), full((NB, 32)), full((NB, 32))]


def _conv_fwd_dft(z, cw, bias, dft, comm=None):
    nt = T // TB
    hb = TB // HB

    def body(z_ref, zh_ref, cw_ref, b_ref, cf_ref, sf_ref, kch, kcl, ksh, ksl, a1_ref, q_ref, aext, ppad, hc, hs):
        first = pl.program_id(1) == 0
        f = lambda ref, j: ref[j].astype(F32)

        @pl.when(first)
        def _():
            w32 = cw_ref[0:32, :]
            hc[...] = _dot3(kch[...], kcl[...], w32)
            hs[...] = _dot3(ksh[...], ksl[...], w32)

        aext[0:HB, :] = jnp.where(first, 0.0, f(zh_ref, 0) * _sig(f(zh_ref, 1))).astype(BF)
        aext[HB:, :] = (f(z_ref, 0) * _sig(f(z_ref, 1))).astype(BF)
        ppad[0:8, :] = jnp.where(first, 0.0, f(zh_ref, 3)[HB - 8:HB, :] * f(zh_ref, 4)[HB - 8:HB, :])
        ppad[8:, :] = f(z_ref, 3) * f(z_ref, 4)
        bias_row = b_ref[...]

        for j in range(TB // HB):
            xs = aext[j * HB:j * HB + NB, :]
            xa, xb = _nn(cf_ref[...], xs), _nn(sf_ref[...], xs)
            yc = (hc[...] * xa - hs[...] * xb).astype(BF)
            ys = (hc[...] * xb + hs[...] * xa).astype(BF)
            y = _nn(cf_ref[HB:NB, :], yc) + _nn(sf_ref[HB:NB, :], ys)
            a1_ref[j * HB:(j + 1) * HB, :] = (y + bias_row).astype(BF)

        def chunk(r, carry):
            base = pl.multiple_of(r * CHB, CHB)
            pw = ppad[pl.ds(base, CHB + 8), :]
            v = (cw_ref[pl.ds(32, 1), :] * _shift_up(pw, 6)[0:CHB, :]
                 + cw_ref[pl.ds(33, 1), :] * _shift_up(pw, 7)[0:CHB, :]
                 + cw_ref[pl.ds(34, 1), :] * pw[8:8 + CHB, :])
            q_ref[pl.ds(base, CHB), :] = (z_ref[2, pl.ds(base, CHB), :].astype(F32) * v).astype(BF)
            return carry

        lax.fori_loop(0, TB // CHB, chunk, 0)

    blk = pl.BlockSpec((TB, CW), lambda c, t: (t, c))
    return _call(
        body, name="conv_fwd", grid=(D // CW, nt), comm=comm,
        args=[z, z, cw, bias, dft["cf"], dft["sf"], *dft["kc"], *dft["ks"]],
        in_specs=[pl.BlockSpec((5, TB, CW), lambda c, t: (0, t, c)),
                  pl.BlockSpec((5, HB, CW), lambda c, t: (0, jnp.maximum(t * hb - 1, 0), c)),
                  pl.BlockSpec((40, CW), lambda c, t: (0, c)), pl.BlockSpec((1, CW), lambda c, t: (0, c))]
                 + _dft_specs(),
        out_shape=[jax.ShapeDtypeStruct((T, D), BF), jax.ShapeDtypeStruct((T, D), BF)], out_specs=[blk, blk],
        scratch_shapes=[pltpu.VMEM((TB + HB, CW), BF), pltpu.VMEM((TB + 8, CW), F32),
                        pltpu.VMEM((NB, CW), F32), pltpu.VMEM((NB, CW), F32)])


def _conv_bwd_dft(z, da1, dq, dzg, cw, dft, comm=None):
    nt = T // TB
    hb = TB // HB
    last_h = T // HB - 1

    def body(z_ref, zp_ref, zn_ref, da1_ref, da1n_ref, dq_ref, dqn_ref, dzg_ref, cw_ref,
             cf_ref, sf_ref, kch, kcl, ksh, ksl, rch, rcl, rsh, rsl,
             dz_ref, dwa_ref, dwb_ref, aext, dyext, ppad, dvpad, hc, hs, rc, rs, acc_b):
        t = pl.program_id(1)
        first, last = t == 0, t == nt - 1
        f = lambda ref, j: ref[j].astype(F32)

        @pl.when(first)
        def _():
            w32 = cw_ref[0:32, :]
            hc[...] = _dot3(kch[...], kcl[...], w32)
            hs[...] = _dot3(ksh[...], ksl[...], w32)
            rc[...] = jnp.zeros_like(rc)
            rs[...] = jnp.zeros_like(rs)
            acc_b[...] = jnp.zeros_like(acc_b)

        aext[0:HB, :] = jnp.where(first, 0.0, f(zp_ref, 0) * _sig(f(zp_ref, 1))).astype(BF)
        aext[HB:, :] = (f(z_ref, 0) * _sig(f(z_ref, 1))).astype(BF)
        dyext[0:TB, :] = da1_ref[...]
        dyext[TB:, :] = jnp.where(last, 0.0, da1n_ref[...].astype(F32)).astype(BF)
        ppad[0:8, :] = jnp.where(first, 0.0, f(zp_ref, 3)[HB - 8:HB, :] * f(zp_ref, 4)[HB - 8:HB, :])
        ppad[8:, :] = f(z_ref, 3) * f(z_ref, 4)
        dvpad[0:TB, :] = dq_ref[...].astype(F32) * f(z_ref, 2)
        dvpad[TB:, :] = jnp.where(last, 0.0, dqn_ref[...].astype(F32)[0:8, :] * f(zn_ref, 2)[0:8, :])

        for j in range(TB // HB):
            rows = slice(j * HB, (j + 1) * HB)
            dys = dyext[j * HB:j * HB + NB, :]
            da, db = _nn(cf_ref[...], dys), _nn(sf_ref[...], dys)
            gc = (hc[...] * da + hs[...] * db).astype(BF)
            gs = (hc[...] * db - hs[...] * da).astype(BF)
            da0 = _nn(cf_ref[0:HB, :], gc) + _nn(sf_ref[0:HB, :], gs)
            z0, z1 = z_ref[0, rows, :].astype(F32), z_ref[1, rows, :].astype(F32)
            s1 = _sig(z1)
            dz_ref[0, rows, :] = (da0 * s1).astype(BF)
            dz_ref[1, rows, :] = (da0 * z0 * (s1 * (1.0 - s1))).astype(BF)
            xs = aext[j * HB:j * HB + NB, :]
            xa, xb = _nn(cf_ref[...], xs), _nn(sf_ref[...], xs)
            dyb = dyext[rows, :]
            pa, pb = _nn(cf_ref[:, HB:NB], dyb), _nn(sf_ref[:, HB:NB], dyb)
            rc[...] += pa * xa + pb * xb
            rs[...] += pb * xa - pa * xb

        def chunk(r, carry):
            base = pl.multiple_of(r * CHB, CHB)
            rows = pl.ds(base, CHB)
            pw = ppad[pl.ds(base, CHB + 8), :]
            p6 = _shift_up(pw, 6)[0:CHB, :]
            p7 = _shift_up(pw, 7)[0:CHB, :]
            p8 = pw[8:8 + CHB, :]
            wb0, wb1, wb2 = cw_ref[pl.ds(32, 1), :], cw_ref[pl.ds(33, 1), :], cw_ref[pl.ds(34, 1), :]
            v = wb0 * p6 + wb1 * p7 + wb2 * p8
            dz_ref[2, rows, :] = (dq_ref[rows, :].astype(F32) * v).astype(BF)
            dvw = dvpad[pl.ds(base, CHB + 8), :]
            dvc = dvw[0:CHB, :]
            dp = wb2 * dvc + wb1 * _shift_up(dvw, 1)[0:CHB, :] + wb0 * _shift_up(dvw, 2)[0:CHB, :]
            dz_ref[3, rows, :] = (dp * z_ref[4, rows, :].astype(F32)).astype(BF)
            dz_ref[4, rows, :] = (dp * z_ref[3, rows, :].astype(F32)).astype(BF)
            acc_b[0:8, :] += _fold8(dvc * p6)
            acc_b[8:16, :] += _fold8(dvc * p7)
            acc_b[16:24, :] += _fold8(dvc * p8)
            dz_ref[5, rows, :] = dzg_ref[0, rows, :]
            dz_ref[6, rows, :] = dzg_ref[1, rows, :]
            return carry

        lax.fori_loop(0, TB // CHB, chunk, 0)

        @pl.when(last)
        def _():
            dwa_ref[...] = _dot3(rch[...], rcl[...], rc[...]) + _dot3(rsh[...], rsl[...], rs[...])
            for k in range(KB):
                dwb_ref[k:k + 1, :] = jnp.sum(acc_b[8 * k:8 * k + 8, :], axis=0, keepdims=True)
            dwb_ref[KB:8, :] = jnp.zeros((8 - KB, CW), F32)

    blk = lambda c, t: (t, c)
    nxt = lambda c, t: (jnp.minimum((t + 1) * hb, last_h), c)
    full = lambda shape: pl.BlockSpec(shape, lambda c, t: (0,) * len(shape))
    return _call(
        body, name="conv_bwd", grid=(D // CW, nt), comm=comm,
        args=[z, z, z, da1, da1, dq, dq, dzg, cw, dft["cf"], dft["sf"], *dft["kc"], *dft["ks"], *dft["rc"], *dft["rs"]],
        in_specs=[pl.BlockSpec((5, TB, CW), lambda c, t: (0, t, c)),
                  pl.BlockSpec((5, HB, CW), lambda c, t: (0, jnp.maximum(t * hb - 1, 0), c)),
                  pl.BlockSpec((5, HB, CW), lambda c, t: (0, jnp.minimum((t + 1) * hb, last_h), c)),
                  pl.BlockSpec((TB, CW), blk), pl.BlockSpec((HB, CW), nxt),
                  pl.BlockSpec((TB, CW), blk), pl.BlockSpec((HB, CW), nxt),
                  pl.BlockSpec((2, TB, CW), lambda c, t: (0, t, c)),
                  pl.BlockSpec((40, CW), lambda c, t: (0, c))]
                 + _dft_specs() + [full((32, NB))] * 4,
        out_shape=[jax.ShapeDtypeStruct((NG, T, D), BF), jax.ShapeDtypeStruct((32, D), F32),
                   jax.ShapeDtypeStruct((8, D), F32)],
        out_specs=[pl.BlockSpec((NG, TB, CW), lambda c, t: (0, t, c)),
                   pl.BlockSpec((32, CW), lambda c, t: (0, c)), pl.BlockSpec((8, CW), lambda c, t: (0, c))],
        scratch_shapes=[pltpu.VMEM((TB + HB, CW), BF), pltpu.VMEM((TB + HB, CW), BF),
                        pltpu.VMEM((TB + 8, CW), F32), pltpu.VMEM((TB + 8, CW), F32),
                        pltpu.VMEM((NB, CW), F32), pltpu.VMEM((NB, CW), F32),
                        pltpu.VMEM((NB, CW), F32), pltpu.VMEM((NB, CW), F32), pltpu.VMEM((24, CW), F32)])


def _layernorm_silu(a1, lng, lnb):
    mu = jnp.mean(a1, axis=-1, keepdims=True)
    xc = a1 - mu
    rs = lax.rsqrt(jnp.mean(xc * xc, axis=-1, keepdims=True) + EPS)
    xh = xc * rs
    a2 = xh * lng + lnb
    sg = _sig(a2)
    return xh, rs, a2, sg


def _square_specs(blocks):
    return [pl.BlockSpec((D, D), lambda i, b=b: (b, 0)) for b in blocks]


def _mix_out(a1, q, z, h1, lng, lnb, wsq, comm=None):
    def body(a1_ref, q_ref, ga_ref, gb_ref, h_ref, lng_ref, lnb_ref, wa_ref, wb_ref, wo_ref, h2_ref, ya_ref, yb_ref):
        _, _, a2, sg = _layernorm_silu(a1_ref[...].astype(F32), lng_ref[...], lnb_ref[...])
        ya = _nn((a2 * sg).astype(BF), wa_ref[...])
        yb = _nn(q_ref[...], wb_ref[...])
        ya_ref[...] = ya.astype(BF)
        yb_ref[...] = yb.astype(BF)
        m = _sig(ga_ref[...].astype(F32)) * ya + _sig(gb_ref[...].astype(F32)) * yb
        h2_ref[...] = h_ref[...] + _nn(m.astype(BF), wo_ref[...])

    row = lambda i: (i, 0)
    vec = pl.BlockSpec((1, D), lambda i: (0, 0))
    return _call(
        body, name="mix_out", grid=(T // TM,), args=[a1, q, z, z, h1, lng, lnb, wsq, wsq, wsq], comm=comm,
        in_specs=[pl.BlockSpec((TM, D), row), pl.BlockSpec((TM, D), row),
                  pl.BlockSpec((None, TM, D), lambda i: (5, i, 0)), pl.BlockSpec((None, TM, D), lambda i: (6, i, 0)),
                  pl.BlockSpec((TM, D), row), vec, vec] + _square_specs((0, 1, 2)),
        out_shape=[jax.ShapeDtypeStruct((T, D), F32), jax.ShapeDtypeStruct((T, D), BF), jax.ShapeDtypeStruct((T, D), BF)],
        out_specs=[pl.BlockSpec((TM, D), row)] * 3)


def _rmsnorm_bwd(xf, g, dn):
    r = lax.rsqrt(jnp.mean(xf * xf, axis=-1, keepdims=True) + EPS)
    xr = xf * r
    gdn = dn * g
    dx = r * gdn - xr * (r * jnp.mean(gdn * xr, axis=-1, keepdims=True))
    return dx, jnp.sum(dn * xr, axis=0, keepdims=True)


def _ffn_bwd_hidden(dh, gg, uu, wbuf, off, name, comm=None):
    nf = F // FC

    def body(dh_ref, gg_ref, uu_ref, b0, dgu_ref, a_ref, wd_s, sem):
        _load_ffn_weights((b0,), (off,), (wd_s,), sem)
        dhb = (0.5 * dh_ref[...]).astype(BF)
        for c in range(nf):
            sl = slice(c * FC, (c + 1) * FC)
            da = _nt(dhb, wd_s[sl, :]).astype(BF)
            gb, ub = gg_ref[:, sl], uu_ref[:, sl]
            sg = _sig(gb)
            silu = gb * sg
            dgu_ref[0, :, sl] = (da * ub) * (sg * (1.0 + gb * (1.0 - sg)))
            dgu_ref[1, :, sl] = da * silu
            a_ref[0, :, sl] = silu * ub

    row = lambda i: (i, 0)
    return _call(
        body, name=name, grid=(T // TM,), args=[dh, gg, uu, wbuf], comm=comm,
        in_specs=[pl.BlockSpec((TM, D), row), pl.BlockSpec((TM, F), row), pl.BlockSpec((TM, F), row), ANY],
        out_shape=[jax.ShapeDtypeStruct((2, T, F), BF), jax.ShapeDtypeStruct((1, T, F), BF)],
        out_specs=[pl.BlockSpec((2, TM, F), lambda i: (0, i, 0)), pl.BlockSpec((1, TM, F), lambda i: (0, i, 0))],
        scratch_shapes=[pltpu.VMEM((F, D), BF), pltpu.SemaphoreType.DMA((1,))])


def _ffn_bwd_input(dgu, dh, x, g, wbufs, offs, name, comm=None):
    def body(dgu_ref, dh_ref, x_ref, g_ref, b0, b1, dx_ref, s_ref, wg_s, wu_s, sem):
        _load_ffn_weights((b0, b1), offs, (wg_s, wu_s), sem)

        @pl.when(pl.program_id(0) == 0)
        def _():
            s_ref[...] = jnp.zeros_like(s_ref)

        dn = _nn(dgu_ref[0], wg_s[...]) + _nn(dgu_ref[1], wu_s[...])
        dxn, dg = _rmsnorm_bwd(x_ref[...], g_ref[...], dn)
        dx_ref[...] = dh_ref[...] + dxn
        s_ref[0:1, :] += dg

    row = lambda i: (i, 0)
    return _call(
        body, name=name, grid=(T // TM,), args=[dgu, dh, x, g, *wbufs], comm=comm,
        in_specs=[pl.BlockSpec((2, TM, F), lambda i: (0, i, 0)), pl.BlockSpec((TM, D), row),
                  pl.BlockSpec((TM, D), row), pl.BlockSpec((1, D), lambda i: (0, 0)), ANY, ANY],
        out_shape=[jax.ShapeDtypeStruct((T, D), F32), jax.ShapeDtypeStruct((8, D), F32)],
        out_specs=[pl.BlockSpec((TM, D), row), pl.BlockSpec((8, D), lambda i: (0, 0))],
        scratch_shapes=[pltpu.VMEM((F, D), BF)] * 2 + [pltpu.SemaphoreType.DMA((2,))])


def _tn_matmul(lhs, rhs, tr, name, comm=None, scale=None):
    ng, _, cdim = lhs.shape
    nc, nk = cdim // tr, T // TK

    def body(l_ref, r_ref, o_ref, acc):
        k = pl.program_id(2)

        @pl.when(k == 0)
        def _():
            acc[...] = jnp.zeros_like(acc)

        r = r_ref[...] if scale is None else scale * r_ref[...]
        acc[...] += _tn(l_ref[...], r.astype(BF))

        @pl.when(k == nk - 1)
        def _():
            o_ref[...] = acc[...].astype(BF)

    return _call(
        body, name=name, grid=(ng, nc, nk), args=[lhs, rhs], comm=comm,
        in_specs=[pl.BlockSpec((None, TK, tr), lambda g, c, k: (g, k, c)),
                  pl.BlockSpec((TK, D), lambda g, c, k: (k, 0))],
        out_shape=[jax.ShapeDtypeStruct((ng * cdim, D), BF)],
        out_specs=[pl.BlockSpec((tr, D), lambda g, c, k: (g * nc + c, 0))],
        scratch_shapes=[pltpu.VMEM((tr, D), F32)])


def _mix_out_bwd(dh2, ya, yb, z, a1, lng, lnb, wsq, comm=None):
    def body(dh_ref, ya_ref, yb_ref, ga_ref, gb_ref, a1_ref, lng_ref, lnb_ref, wa_ref, wb_ref, wo_ref,
             dzg_ref, da1_ref, dq_ref, m_ref, a3_ref, dya_ref, dyb_ref, s_ref):
        @pl.when(pl.program_id(0) == 0)
        def _():
            s_ref[...] = jnp.zeros_like(s_ref)

        dm = _nt(dh_ref[...].astype(BF), wo_ref[...])
        ya, yb = ya_ref[...].astype(F32), yb_ref[...].astype(F32)
        sa, sb = _sig(ga_ref[...].astype(F32)), _sig(gb_ref[...].astype(F32))
        m_ref[0] = (sa * ya + sb * yb).astype(BF)
        dzg_ref[0] = (dm * ya * (sa * (1.0 - sa))).astype(BF)
        dzg_ref[1] = (dm * yb * (sb * (1.0 - sb))).astype(BF)
        dya = (dm * sa).astype(BF)
        dyb = (dm * sb).astype(BF)
        dya_ref[...] = dya
        dyb_ref[...] = dyb
        dq_ref[...] = _nt(dyb, wb_ref[...]).astype(BF)
        da3 = _nt(dya, wa_ref[...])
        lng = lng_ref[...]
        xh, rs, a2, sg = _layernorm_silu(a1_ref[...].astype(F32), lng, lnb_ref[...])
        a3_ref[0] = (a2 * sg).astype(BF)
        da2 = da3 * (sg * (1.0 + a2 * (1.0 - sg)))
        s_ref[0:1, :] += jnp.sum(da2 * xh, axis=0, keepdims=True)
        s_ref[1:2, :] += jnp.sum(da2, axis=0, keepdims=True)
        dxh = da2 * lng
        da1 = rs * (dxh - jnp.mean(dxh, axis=-1, keepdims=True) - xh * jnp.mean(dxh * xh, axis=-1, keepdims=True))
        da1_ref[...] = da1.astype(BF)
        s_ref[2:3, :] += jnp.sum(da1, axis=0, keepdims=True)

    row = lambda i: (i, 0)
    row3 = lambda i: (0, i, 0)
    vec = pl.BlockSpec((1, D), lambda i: (0, 0))
    return _call(
        body, name="mix_out_bwd", grid=(T // TM,), args=[dh2, ya, yb, z, z, a1, lng, lnb, wsq, wsq, wsq], comm=comm,
        in_specs=[pl.BlockSpec((TM, D), row), pl.BlockSpec((TM, D), row), pl.BlockSpec((TM, D), row),
                  pl.BlockSpec((None, TM, D), lambda i: (5, i, 0)), pl.BlockSpec((None, TM, D), lambda i: (6, i, 0)),
                  pl.BlockSpec((TM, D), row), vec, vec] + _square_specs((0, 1, 2)),
        out_shape=[jax.ShapeDtypeStruct((2, T, D), BF), jax.ShapeDtypeStruct((T, D), BF),
                   jax.ShapeDtypeStruct((T, D), BF), jax.ShapeDtypeStruct((1, T, D), BF),
                   jax.ShapeDtypeStruct((1, T, D), BF), jax.ShapeDtypeStruct((T, D), BF),
                   jax.ShapeDtypeStruct((T, D), BF), jax.ShapeDtypeStruct((8, D), F32)],
        out_specs=[pl.BlockSpec((2, TM, D), row3), pl.BlockSpec((TM, D), row), pl.BlockSpec((TM, D), row),
                   pl.BlockSpec((1, TM, D), row3), pl.BlockSpec((1, TM, D), row3), pl.BlockSpec((TM, D), row),
                   pl.BlockSpec((TM, D), row), pl.BlockSpec((8, D), lambda i: (0, 0))])


def _mix_in_bwd(dz, dh2, h1, gm, win, comm=None):
    def body(dz_ref, w_any, dh_ref, h_ref, g_ref, o_ref, s_ref, w_s, sem):
        _load_ffn_weights((w_any,), (0,), (w_s,), sem)

        @pl.when(pl.program_id(0) == 0)
        def _():
            s_ref[...] = jnp.zeros_like(s_ref)

        du = _nn(dz_ref[0], w_s[0:D, :])
        for j in range(1, NG):
            du = du + _nn(dz_ref[j], w_s[j * D:(j + 1) * D, :])
        dx, dg = _rmsnorm_bwd(h_ref[...], g_ref[...], du)
        o_ref[...] = dh_ref[...] + dx
        s_ref[0:1, :] += dg

    row = lambda i: (i, 0)
    return _call(
        body, name="mix_in_bwd", grid=(T // TM,), args=[dz, win, dh2, h1, gm], comm=comm,
        in_specs=[pl.BlockSpec((NG, TM, D), lambda i: (0, i, 0)), ANY,
                  pl.BlockSpec((TM, D), row), pl.BlockSpec((TM, D), row), pl.BlockSpec((1, D), lambda i: (0, 0))],
        out_shape=[jax.ShapeDtypeStruct((T, D), F32), jax.ShapeDtypeStruct((8, D), F32)],
        out_specs=[pl.BlockSpec((TM, D), row), pl.BlockSpec((8, D), lambda i: (0, 0))],
        scratch_shapes=[pltpu.VMEM((NG * D, D), BF), pltpu.SemaphoreType.DMA((1,))])


def _row_tile(n, want, mult):
    for t in range(min(want, n), 0, -1):
        if n % t == 0 and t % mult == 0:
            return t
    return n


def _sum_slots(recv, name):
    ns, rows, cols = recv.shape
    tr = _row_tile(rows, 1024, 16)

    def body(r_ref, o_ref):
        s = r_ref[0].astype(F32)
        for k in range(1, ns):
            s = s + r_ref[k].astype(F32)
        o_ref[...] = s

    return _call(
        body, name=name, grid=(rows // tr,), args=[recv],
        in_specs=[pl.BlockSpec((ns, tr, cols), lambda i: (0, i, 0))],
        out_shape=[jax.ShapeDtypeStruct((rows, cols), F32)],
        out_specs=[pl.BlockSpec((tr, cols), lambda i: (i, 0))])[0]


def _pack_small(s_ffn1, s_in, s_mix, s_ffn2, s_final, dwa, dwb):
    def body(f1, mi, mo, f2, fl, wa_ref, wb_ref, v_ref, k_ref):
        for dst, (ref, row) in enumerate(((f1, 0), (mi, 0), (mo, 0), (mo, 1), (mo, 2), (f2, 0), (fl, 0), (fl, 1))):
            v_ref[dst:dst + 1, :] = ref[row:row + 1, :]
        for k in range(NDEV):
            k_ref[k, 0:32, :] = wa_ref[:, k * LANE:(k + 1) * LANE]
            k_ref[k, 32:40, :] = wb_ref[:, k * LANE:(k + 1) * LANE]

    return pl.pallas_call(
        body, name="pack_small",
        out_shape=(jax.ShapeDtypeStruct((8, D), F32), jax.ShapeDtypeStruct((NDEV, 40, LANE), F32)),
    )(s_ffn1, s_in, s_mix, s_ffn2, s_final, dwa, dwb)


def _sum_small(vecs, convs):
    def body(v_ref, k_ref, vs_ref, ks_ref, l_ref):
        s, c = v_ref[0], k_ref[0]
        for k in range(1, NDEV):
            s = s + v_ref[k]
            c = c + k_ref[k]
        vs_ref[...] = s
        ks_ref[...] = c
        l_ref[...] = jnp.broadcast_to(jnp.sum(s[7:8, :], axis=-1, keepdims=True), (8, LANE))

    return pl.pallas_call(
        body, name="sum_small",
        out_shape=(jax.ShapeDtypeStruct((8, D), F32), jax.ShapeDtypeStruct((40, LANE), F32),
                   jax.ShapeDtypeStruct((8, LANE), F32)),
    )(vecs, convs)


def _adam(gs, ws, ms, vs, name, comm=None):
    n = len(gs)
    rows, cols = ws[0].shape
    tr = _row_tile(rows, 256, 8)
    c1 = 1.0 - ADAM_B1 ** ADAM_STEP
    c2 = 1.0 - ADAM_B2 ** ADAM_STEP

    def body(*refs):
        for i in range(n):
            g, w, m, v = (refs[4 * i + k][...] for k in range(4))
            d_ref, m_ref, v_ref = refs[4 * n + 3 * i: 4 * n + 3 * i + 3]
            m2 = ADAM_B1 * m + (1.0 - ADAM_B1) * g
            v2 = ADAM_B2 * v + (1.0 - ADAM_B2) * (g * g)
            d_ref[...] = -ADAM_LR * ((m2 / c1) / (jnp.sqrt(v2 / c2) + ADAM_EPS) + ADAM_WD * w)
            m_ref[...] = m2
            v_ref[...] = v2

    spec = pl.BlockSpec((tr, cols), lambda i: (i, 0))
    args = []
    for i in range(n):
        args += [gs[i], ws[i], ms[i], vs[i]]
    outs = _call(body, name=name, grid=(rows // tr,), args=args, comm=comm, in_specs=[spec] * (4 * n),
                 out_shape=[jax.ShapeDtypeStruct((rows, cols), F32)] * (3 * n), out_specs=[spec] * (3 * n))
    return [tuple(outs[3 * i: 3 * i + 3]) for i in range(n)], outs[3 * n:]


def kernel(x, ffn1_norm, ffn1_w_gate, ffn1_w_up, ffn1_w_down, mix_norm, w_in, a_dw_w, a_dw_b, a_ln_g, a_ln_b, a_w_out, b_conv_w, b_w_out, w_o, ffn2_norm, ffn2_w_gate, ffn2_w_up, ffn2_w_down, final_norm, loss_target, m_ffn1_norm, m_ffn1_w_gate, m_ffn1_w_up, m_ffn1_w_down, m_mix_norm, m_w_in, m_a_dw_w, m_a_dw_b, m_a_ln_g, m_a_ln_b, m_a_w_out, m_b_conv_w, m_b_w_out, m_w_o, m_ffn2_norm, m_ffn2_w_gate, m_ffn2_w_up, m_ffn2_w_down, m_final_norm, v_ffn1_norm, v_ffn1_w_gate, v_ffn1_w_up, v_ffn1_w_down, v_mix_norm, v_w_in, v_a_dw_w, v_a_dw_b, v_a_ln_g, v_a_ln_b, v_a_w_out, v_b_conv_w, v_b_w_out, v_w_o, v_ffn2_norm, v_ffn2_w_gate, v_ffn2_w_up, v_ffn2_w_down, v_final_norm):
    names = ("ffn1_norm", "ffn1_w_gate", "ffn1_w_up", "ffn1_w_down", "mix_norm", "w_in", "a_dw_w", "a_dw_b",
             "a_ln_g", "a_ln_b", "a_w_out", "b_conv_w", "b_w_out", "w_o", "ffn2_norm", "ffn2_w_gate", "ffn2_w_up",
             "ffn2_w_down", "final_norm")
    w = dict(ffn1_norm=ffn1_norm, ffn1_w_gate=ffn1_w_gate, ffn1_w_up=ffn1_w_up, ffn1_w_down=ffn1_w_down,
             mix_norm=mix_norm, w_in=w_in, a_dw_w=a_dw_w, a_dw_b=a_dw_b, a_ln_g=a_ln_g, a_ln_b=a_ln_b,
             a_w_out=a_w_out, b_conv_w=b_conv_w, b_w_out=b_w_out, w_o=w_o, ffn2_norm=ffn2_norm,
             ffn2_w_gate=ffn2_w_gate, ffn2_w_up=ffn2_w_up, ffn2_w_down=ffn2_w_down, final_norm=final_norm)
    m = dict(ffn1_norm=m_ffn1_norm, ffn1_w_gate=m_ffn1_w_gate, ffn1_w_up=m_ffn1_w_up, ffn1_w_down=m_ffn1_w_down,
             mix_norm=m_mix_norm, w_in=m_w_in, a_dw_w=m_a_dw_w, a_dw_b=m_a_dw_b, a_ln_g=m_a_ln_g, a_ln_b=m_a_ln_b,
             a_w_out=m_a_w_out, b_conv_w=m_b_conv_w, b_w_out=m_b_w_out, w_o=m_w_o, ffn2_norm=m_ffn2_norm,
             ffn2_w_gate=m_ffn2_w_gate, ffn2_w_up=m_ffn2_w_up, ffn2_w_down=m_ffn2_w_down, final_norm=m_final_norm)
    v = dict(ffn1_norm=v_ffn1_norm, ffn1_w_gate=v_ffn1_w_gate, ffn1_w_up=v_ffn1_w_up, ffn1_w_down=v_ffn1_w_down,
             mix_norm=v_mix_norm, w_in=v_w_in, a_dw_w=v_a_dw_w, a_dw_b=v_a_dw_b, a_ln_g=v_a_ln_g, a_ln_b=v_a_ln_b,
             a_w_out=v_a_w_out, b_conv_w=v_b_conv_w, b_w_out=v_b_w_out, w_o=v_w_o, ffn2_norm=v_ffn2_norm,
             ffn2_w_gate=v_ffn2_w_gate, ffn2_w_up=v_ffn2_w_up, ffn2_w_down=v_ffn2_w_down, final_norm=v_final_norm)
    flat = _pack_weights(dict(wg1=ffn1_w_gate[0].T, wu1=ffn1_w_up[0].T, wd1=ffn1_w_down[0], wg2=ffn2_w_gate[0].T,
                              wu2=ffn2_w_up[0].T, wd2=ffn2_w_down[0], win=w_in[0], wa=a_w_out[0], wb=b_w_out[0],
                              wo=w_o[0]))
    cw_shard = jnp.concatenate([a_dw_w[0], jnp.zeros((1, LANE), F32), b_conv_w[0], jnp.zeros((5, LANE), F32)], axis=0)

    x2, tgt = x[0], loss_target[0]
    st_a, st_b, st_c, st_d = ("wg1", "wu1", "wd1"), ("win",), ("wa", "wb", "wo", "wg2"), ("wu2", "wd2")

    buf_a, cw = _run_comm(_join(_ag_comm(st_a, flat), _direct_comm(cw_shard, False)), "ag_ffn1")
    h1, n1, gg1, uu1, buf_b = _ffn_fwd(x2, ffn1_norm, (buf_a,) * 3, (0, F, 2 * F), "ffn1_fwd", _ag_comm(st_b, flat))
    u, z, buf_c = _mix_in(h1, mix_norm, buf_b, _ag_comm(st_c, flat))
    dft = _dft_constants()
    cw = jnp.transpose(cw, (1, 0, 2)).reshape(40, D)
    a1, q, buf_d = _conv_fwd_dft(z, cw, a_dw_b, dft, _ag_comm(st_d, flat))
    h2, ya, yb = _mix_out(a1, q, z, h1, a_ln_g, a_ln_b, buf_c)
    ffn2_bufs, ffn2_offs = (buf_c, buf_d, buf_d), (3 * D, 0, F)
    dh3, s_final, n2, gg2, uu2 = _ffn_fwd(h2, ffn2_norm, ffn2_bufs, ffn2_offs, "ffn2_fwd",
                                          final=(final_norm.reshape(1, D), tgt))

    tr_f = F // 2 if (F // 2) % LANE == 0 else F
    def pair(stage, src):
        return _rs_pair_comm(stage, src)

    def chip(stage, src, pair_buf, tag):
        return _rs_chip_comm(_pair_add(stage, src, pair_buf, "pair_add_" + tag))

    dgu2, act2 = _ffn_bwd_hidden(dh3, gg2, uu2, buf_d, F, "ffn2_bwd_h")
    (gu2,) = _tn_matmul(dgu2, n2, tr_f, "dw_gu2")
    s2a, src2a = ("wg2", "wu2"), dict(wg2=(gu2, 0), wu2=(gu2, F))
    gd2, pair2a = _tn_matmul(act2, dh3, tr_f, "dw_d2", pair(s2a, src2a), scale=0.5)
    s2b, src2b = ("wd2",), dict(wd2=(gd2, 0))
    dh2, s_ffn2, recv2a, pair2b = _ffn_bwd_input(dgu2, dh3, h2, ffn2_norm, (buf_c, buf_d), (3 * D, 0), "ffn2_bwd_x",
                                                 _join(chip(s2a, src2a, pair2a, "2a"), pair(s2b, src2b)))
    dzg, da1, dq, mb, a3b, dya, dyb, s_mix, recv2b = _mix_out_bwd(dh2, ya, yb, z, a1, a_ln_g, a_ln_b, buf_c,
                                                                   chip(s2b, src2b, pair2b, "2b"))
    (go,) = _tn_matmul(mb, dh2, D, "dw_o")
    (ga,) = _tn_matmul(a3b, dya, D, "dw_a")
    (gb,) = _tn_matmul(q.reshape(1, T, D), dyb, D, "dw_b")
    ssq, srcsq = ("wa", "wb", "wo"), dict(wa=(ga, 0), wb=(gb, 0), wo=(go, 0))
    dz, dwa, dwb, pairsq = _conv_bwd_dft(z, da1, dq, dzg, cw, dft, pair(ssq, srcsq))
    gin, recvsq = _tn_matmul(dz, u, D, "dw_in", chip(ssq, srcsq, pairsq, "sq"))
    sin_a, sin_b, srcin = ("win/0/2",), ("win/1/2",), {"win/0/2": (gin, 0), "win/1/2": (gin, 0)}
    dh1, s_in, pairin_a, pairin_b = _mix_in_bwd(dz, dh2, h1, mix_norm, buf_b,
                                                _join(pair(sin_a, srcin), pair(sin_b, srcin)))
    dgu1, act1, recvin_a = _ffn_bwd_hidden(dh1, gg1, uu1, buf_a, 2 * F, "ffn1_bwd_h",
                                           chip(sin_a, srcin, pairin_a, "in_a"))
    gu1, recvin_b = _tn_matmul(dgu1, n1, tr_f, "dw_gu1", chip(sin_b, srcin, pairin_b, "in_b"))
    s1a, src1a = ("wg1", "wu1"), dict(wg1=(gu1, 0), wu1=(gu1, F))
    gd1, pair1a = _tn_matmul(act1, dh1, tr_f, "dw_d1", pair(s1a, src1a), scale=0.5)
    s1b, src1b = ("wd1",), dict(wd1=(gd1, 0))
    dx, s_ffn1, recv1a, pair1b = _ffn_bwd_input(dgu1, dh1, x2, ffn1_norm, (buf_a, buf_a), (0, F), "ffn1_bwd_x",
                                                _join(chip(s1a, src1a, pair1a, "1a"), pair(s1b, src1b)))
    (recv1b,) = _run_comm(chip(s1b, src1b, pair1b, "1b"), "rs_chip_1b")
    stages = ((s2a, recv2a, "2a"), (s2b, recv2b, "2b"), (ssq, recvsq, "sq"), (sin_a, recvin_a, "in_a"),
              (sin_b, recvin_b, "in_b"), (s1a, recv1a, "1a"), (s1b, recv1b, "1b"))

    gsum = {}
    for stage, recv, tag in stages:
        st, total = _Stage(stage), _sum_slots(recv, "sum_" + tag)
        for n in stage:
            gsum[n] = total[st.off[n]:st.off[n] + st.rows[n]]
    gsum["win"] = jnp.concatenate([gsum["win/0/2"], gsum["win/1/2"]], axis=0)

    vec8, convk = _pack_small(s_ffn1, s_in, s_mix, s_ffn2, s_final, dwa, dwb)
    vec_all, conv_all = _run_comm(_join(_direct_comm(vec8, False), _direct_comm(convk, True)), "xchg_small")
    vec_sum, conv_sum, loss_blk = _sum_small(vec_all, conv_all)
    loss = loss_blk[0, 0]

    g = dict(ffn1_w_gate=gsum["wg1"], ffn1_w_up=gsum["wu1"], ffn1_w_down=gsum["wd1"],
             ffn2_w_gate=gsum["wg2"], ffn2_w_up=gsum["wu2"], ffn2_w_down=gsum["wd2"], w_in=gsum["win"].T,
             a_w_out=gsum["wa"], b_w_out=gsum["wb"], w_o=gsum["wo"],
             ffn1_norm=vec_sum[0:1], mix_norm=vec_sum[1:2], a_ln_g=vec_sum[2:3], a_ln_b=vec_sum[3:4],
             a_dw_b=vec_sum[4:5], ffn2_norm=vec_sum[5:6], final_norm=vec_sum[6:7],
             a_dw_w=conv_sum[0:KA], b_conv_w=conv_sum[32:32 + KB])
    gate_up = ("ffn1_w_gate", "ffn1_w_up", "ffn2_w_gate", "ffn2_w_up")

    upd = {}

    def run(group, name, as2d=lambda a: a[0], back=lambda a, n: a.reshape(w[n].shape)):
        res, _ = _adam([g[n] for n in group], [as2d(w[n]) for n in group], [as2d(m[n]) for n in group],
                       [as2d(v[n]) for n in group], name)
        for n, r in zip(group, res):
            upd[n] = tuple(back(a, n) for a in r)

    run(gate_up, "adam_gate_up", as2d=lambda a: a[0].T, back=lambda a, n: a.T[None])
    for n in gate_up:
        g[n] = g[n].T
    run(("ffn1_w_down", "ffn2_w_down"), "adam_down")
    run(("w_in",), "adam_in")
    run(("a_w_out", "b_w_out", "w_o"), "adam_square")
    run(("a_dw_w",), "adam_dw")
    run(("b_conv_w",), "adam_conv")
    vecs = ("ffn1_norm", "mix_norm", "a_dw_b", "a_ln_g", "a_ln_b", "ffn2_norm", "final_norm")
    run(vecs, "adam_vec", as2d=lambda a: a.reshape(1, D))

    grads = [g[n].reshape(w[n].shape) for n in names]
    return (loss, dx.reshape(x.shape), *grads, *[upd[n][0] for n in names], *[upd[n][1] for n in names],
            *[upd[n][2] for n in names])
```

```python
import jax
import jax.numpy as jnp
from jax import lax
from jax.experimental import pallas as pl
from jax.experimental.pallas import tpu as pltpu

T = 4096
D = 1024
F = 2816
NG = 7
NDEV = 8
NCHIP = 4
KA, KB = 31, 3
EPS = 1e-6
ADAM_LR, ADAM_B1, ADAM_B2, ADAM_EPS, ADAM_WD, ADAM_STEP = 0.001, 0.9, 0.999, 1e-08, 0.01, 10

TM = 512
FC = 256
TB = 1024
NB = 256
HB = NB // 2
CW = 256
CHB = 64
LANE = 128
TK = 1024
VMEM_LIMIT = 56 * 1024 * 1024

BF = jnp.bfloat16
F32 = jnp.float32
MESH = pl.DeviceIdType.MESH
ANY = pl.BlockSpec(memory_space=pl.ANY)

ORDER = ("wg1", "wu1", "wd1", "wg2", "wu2", "wd2", "win", "wa", "wb", "wo")


class _Layout:
    def __init__(self):
        fs, dis, ds = F // NDEV, NG * D // NDEV, D // NDEV
        self.rows = dict(wg1=fs, wu1=fs, wd1=fs, wg2=fs, wu2=fs, wd2=fs, win=dis, wa=ds, wb=ds, wo=ds)
        self.fl, off = {}, 0
        for n in ORDER:
            self.fl[n] = off
            off += self.rows[n]
        self.RT = off


class _Stage:
    def __init__(self, names):
        lay = _Layout()
        self.names = names
        self.rows, self.full, self.sub, self.fl = {}, {}, {}, {}
        for n in names:
            base, i, k = (n.split("/") + ["0", "1"])[:3]
            self.full[n] = lay.rows[base]
            self.rows[n] = lay.rows[base] // int(k)
            self.sub[n] = int(i) * self.rows[n]
            self.fl[n] = lay.fl[base] + self.sub[n]
        self.off, self.wc, o, w = {}, {}, 0, 0
        for n in names:
            self.off[n], self.wc[n] = o, w
            o += self.rows[n]
            w += NDEV * self.rows[n]
        self.R, self.W = o, w

    def grad_row(self, n, first, dev_lin):
        return first + dev_lin * self.full[n] + self.sub[n]


def _nt(a, b):
    return lax.dot_general(a, b, (((1,), (1,)), ((), ())), preferred_element_type=F32)


def _nn(a, b):
    return lax.dot_general(a, b, (((1,), (0,)), ((), ())), preferred_element_type=F32)


def _tn(a, b):
    return lax.dot_general(a, b, (((0,), (0,)), ((), ())), preferred_element_type=F32)


def _sig(x):
    return 1.0 / (1.0 + jnp.exp(-x))


def _position():
    return lax.axis_index("x"), lax.axis_index("y"), lax.axis_index("c")


def _peer(pos, j):
    x, y, c = pos
    return (1 - x if j & 4 else x, 1 - y if j & 2 else y, 1 - c if j & 1 else c)


def _lin(pos):
    return 4 * pos[0] + 2 * pos[1] + pos[2]


def _chip(pos):
    return 2 * pos[0] + pos[1]


class _Comm:
    def __init__(self, inputs, out_shapes, scratch, start, finish):
        self.inputs, self.out_shapes, self.scratch, self.start, self.finish = inputs, out_shapes, scratch, start, finish


def _call(body, *, name, grid, args, in_specs, out_shape, out_specs, scratch_shapes=(), comm=None,
          num_scalar_prefetch=0):
    in_specs, out_shape, out_specs, scratch_shapes = list(in_specs), list(out_shape), list(out_specs), list(scratch_shapes)
    n_in, n_out, n_scr = len(in_specs), len(out_shape), len(scratch_shapes)
    sp = num_scalar_prefetch
    if comm is None:
        kernel_fn = lambda *refs: body(*refs)
        c_in = c_out = c_scr = 0
    else:
        c_in, c_out, c_scr = len(comm.inputs), len(comm.out_shapes), len(comm.scratch)

        def kernel_fn(*refs):
            pre, refs = refs[:sp], refs[sp:]
            ins, cins = refs[:n_in], refs[n_in:n_in + c_in]
            o0 = n_in + c_in
            outs, couts = refs[o0:o0 + n_out], refs[o0 + n_out:o0 + n_out + c_out]
            s0 = o0 + n_out + c_out
            scr, cscr = refs[s0:s0 + n_scr], refs[s0 + n_scr:]
            first = pl.program_id(0) == 0
            last = pl.program_id(0) == grid[0] - 1
            for a in range(1, len(grid)):
                first = first & (pl.program_id(a) == 0)
                last = last & (pl.program_id(a) == grid[a] - 1)

            @pl.when(first)
            def _():
                comm.start(cins, couts, cscr)

            body(*pre, *ins, *outs, *scr)

            @pl.when(last)
            def _():
                comm.finish(cins, couts, cscr)

        args = list(args) + list(comm.inputs)
        in_specs += [ANY] * c_in
        out_shape += list(comm.out_shapes)
        out_specs += [ANY] * c_out
        scratch_shapes += list(comm.scratch)
    params = pltpu.CompilerParams(dimension_semantics=("arbitrary",) * len(grid), vmem_limit_bytes=VMEM_LIMIT)
    if sp:
        grid_spec = pltpu.PrefetchScalarGridSpec(num_scalar_prefetch=sp, grid=grid, in_specs=in_specs,
                                                 out_specs=out_specs, scratch_shapes=scratch_shapes)
        return pl.pallas_call(kernel_fn, name=name, grid_spec=grid_spec, out_shape=out_shape,
                              compiler_params=params)(*args)
    return pl.pallas_call(kernel_fn, name=name, grid=grid, in_specs=in_specs, out_shape=out_shape, out_specs=out_specs,
                          scratch_shapes=scratch_shapes, compiler_params=params)(*args)


def _join(a, b):
    na = (len(a.inputs), len(a.out_shapes), len(a.scratch))

    def split(refs):
        return ([r[:n] for r, n in zip(refs, na)], [r[n:] for r, n in zip(refs, na)])

    def start(*refs):
        ra, rb = split(refs)
        a.start(*ra)
        b.start(*rb)

    def finish(*refs):
        ra, rb = split(refs)
        a.finish(*ra)
        b.finish(*rb)

    return _Comm(list(a.inputs) + list(b.inputs), list(a.out_shapes) + list(b.out_shapes),
                 list(a.scratch) + list(b.scratch), start, finish)


def _run_comm(comm, name):
    def body(*refs):
        c_in, c_out = len(comm.inputs), len(comm.out_shapes)
        comm.start(refs[:c_in], refs[c_in:c_in + c_out], refs[c_in + c_out:])
        comm.finish(refs[:c_in], refs[c_in:c_in + c_out], refs[c_in + c_out:])

    return pl.pallas_call(
        body, name=name, out_shape=list(comm.out_shapes), in_specs=[ANY] * len(comm.inputs),
        out_specs=[ANY] * len(comm.out_shapes), scratch_shapes=list(comm.scratch))(*comm.inputs)


def _ag_comm(names, flat):
    st = _Stage(names)

    def parts(refs):
        (flat_ref,), (out_ref,), (send_sems, recv_sems, local_sem) = refs
        me = _position()

        def region(name, dev):
            r = st.rows[name]
            return out_ref.at[pl.ds(st.wc[name] + _lin(dev) * r, r), :]

        def own(name):
            return flat_ref.at[pl.ds(st.fl[name], st.rows[name]), :]

        def copies(k, dev, to, from_flat):
            return [pltpu.make_async_remote_copy(
                src_ref=own(n) if from_flat else region(n, dev), dst_ref=region(n, dev), send_sem=send_sems.at[k],
                recv_sem=recv_sems.at[k], device_id=to, device_id_type=MESH) for n in names]

        def whole(k):
            return pltpu.make_async_remote_copy(
                src_ref=flat_ref.at[pl.ds(0, st.R), :], dst_ref=out_ref.at[pl.ds(0, st.R), :],
                send_sem=send_sems.at[k], recv_sem=recv_sems.at[k], device_id=me, device_id_type=MESH)

        return me, region, own, copies, whole, flat_ref, out_ref, local_sem

    def start(*refs):
        me, region, own, copies, _, _, _, local_sem = parts(refs)
        for n in names:
            pltpu.make_async_copy(own(n), region(n, me), local_sem).start()
        for cp in copies(0, me, _peer(me, 1), True):
            cp.start()
        for j, bits in enumerate((4, 2, 6)):
            for cp in copies(1 + j, me, _peer(me, bits), True):
                cp.start()

    def finish(*refs):
        me, _, _, copies, whole, flat_ref, out_ref, local_sem = parts(refs)
        for j, bits in enumerate((4, 2, 6)):
            whole(1 + j).wait_recv()
            for cp in copies(4 + j, _peer(me, bits), _peer(me, 1), False):
                cp.start()
        whole(0).wait_recv()
        for j in range(3):
            whole(4 + j).wait_recv()
        for k in range(7):
            whole(k).wait_send()
        pltpu.make_async_copy(flat_ref.at[pl.ds(0, st.R), :], out_ref.at[pl.ds(0, st.R), :], local_sem).wait()

    return _Comm([flat], [jax.ShapeDtypeStruct((st.W, D), BF)],
                 [pltpu.SemaphoreType.DMA((7,)), pltpu.SemaphoreType.DMA((7,)), pltpu.SemaphoreType.DMA],
                 start, finish)


def _rs_pair_comm(names, src):
    st = _Stage(names)
    arrays = []
    for n in names:
        if not any(src[n][0] is a for a in arrays):
            arrays.append(src[n][0])
    idx = {n: [i for i, a in enumerate(arrays) if a is src[n][0]][0] for n in names}

    def slot_wait(refs):
        recv = refs[1][0]
        send_sem, recv_sem = refs[2]
        return pltpu.make_async_remote_copy(src_ref=recv, dst_ref=recv, send_sem=send_sem, recv_sem=recv_sem,
                                            device_id=_position(), device_id_type=MESH)

    def start(*refs):
        ins, (recv,), (send_sem, recv_sem) = refs
        me = _position()
        sib = _peer(me, 1)
        for q in range(NCHIP):
            dev = (q // 2, q % 2, sib[2])
            for n in names:
                r = st.rows[n]
                pltpu.make_async_remote_copy(
                    src_ref=ins[idx[n]].at[pl.ds(st.grad_row(n, src[n][1], _lin(dev)), r), :],
                    dst_ref=recv.at[q, pl.ds(st.off[n], r), :], send_sem=send_sem, recv_sem=recv_sem,
                    device_id=sib, device_id_type=MESH).start()

    def finish(*refs):
        w = slot_wait(refs)
        w.wait_recv()
        w.wait_send()

    return _Comm(arrays, [jax.ShapeDtypeStruct((NCHIP, st.R, D), BF)],
                 [pltpu.SemaphoreType.DMA, pltpu.SemaphoreType.DMA], start, finish)


def _pair_add(names, src, recv, name):
    st = _Stage(names)
    c_arr = jnp.reshape(lax.axis_index("c"), (1,)).astype(jnp.int32)

    def body(c_ref, *refs):
        r_ref, o_ref = refs[len(names)], refs[len(names) + 1]
        for a_ref, n in zip(refs, names):
            rows = slice(st.off[n], st.off[n] + st.rows[n])
            o_ref[rows, :] = (a_ref[...].astype(F32) + r_ref[rows, :].astype(F32)).astype(BF)

    def shard_spec(n):
        r = st.rows[n]
        base, step = st.grad_row(n, src[n][1], 0) // r, st.full[n] // r
        return pl.BlockSpec((r, D), lambda q, c_ref: (base + step * (2 * q + c_ref[0]), 0))

    slot = pl.BlockSpec((None, st.R, D), lambda q, c_ref: (q, 0, 0))
    return _call(body, name=name, grid=(NCHIP,), args=[c_arr] + [src[n][0] for n in names] + [recv],
                 in_specs=[shard_spec(n) for n in names] + [slot],
                 out_shape=[jax.ShapeDtypeStruct((NCHIP, st.R, D), BF)], out_specs=[slot], num_scalar_prefetch=1)[0]


def _rs_chip_comm(part):
    def copies(refs):
        (p_ref,), (recv,), (send_sems, recv_sems, local_sem) = refs
        me = _position()
        mine = pltpu.make_async_copy(p_ref.at[_chip(me)], recv.at[_chip(me)], local_sem)
        out = []
        for j, bits in enumerate((4, 2, 6)):
            to = _peer(me, bits)
            out.append(pltpu.make_async_remote_copy(
                src_ref=p_ref.at[_chip(to)], dst_ref=recv.at[_chip(me)], send_sem=send_sems.at[j],
                recv_sem=recv_sems.at[j], device_id=to, device_id_type=MESH))
        return mine, out

    def start(*refs):
        mine, out = copies(refs)
        mine.start()
        for cp in out:
            cp.start()

    def finish(*refs):
        mine, out = copies(refs)
        for cp in out:
            cp.wait_recv()
        for cp in out:
            cp.wait_send()
        mine.wait()

    return _Comm([part], [jax.ShapeDtypeStruct(part.shape, BF)],
                 [pltpu.SemaphoreType.DMA((3,)), pltpu.SemaphoreType.DMA((3,)), pltpu.SemaphoreType.DMA],
                 start, finish)


def _direct_comm(x, scatter):
    def copies(refs):
        (x_ref,), (out_ref,), (send_sems, recv_sems, local_sem) = refs
        me = _position()

        def piece(dev):
            return x_ref.at[_lin(dev)] if scatter else x_ref

        mine = pltpu.make_async_copy(piece(me), out_ref.at[_lin(me)], local_sem)
        return mine, [pltpu.make_async_remote_copy(
            src_ref=piece(_peer(me, j)), dst_ref=out_ref.at[_lin(me)], send_sem=send_sems.at[j - 1],
            recv_sem=recv_sems.at[j - 1], device_id=_peer(me, j), device_id_type=MESH) for j in range(1, NDEV)]

    def start(*refs):
        mine, cps = copies(refs)
        mine.start()
        for cp in cps:
            cp.start()

    def finish(*refs):
        mine, cps = copies(refs)
        for cp in cps:
            cp.wait_recv()
        for cp in cps:
            cp.wait_send()
        mine.wait()

    shape = x.shape if scatter else (NDEV,) + x.shape
    return _Comm([x], [jax.ShapeDtypeStruct(shape, x.dtype)],
                 [pltpu.SemaphoreType.DMA((7,)), pltpu.SemaphoreType.DMA((7,)), pltpu.SemaphoreType.DMA],
                 start, finish)


def _pack_weights(shards):
    lay = _Layout()

    def body(*refs):
        o_ref = refs[-1]
        for ref, n in zip(refs, ORDER):
            x = ref[...].T if n == "win" else ref[...]
            o_ref[lay.fl[n]:lay.fl[n] + lay.rows[n], :] = x.astype(BF)

    return pl.pallas_call(
        body, name="pack_weights", out_shape=jax.ShapeDtypeStruct((lay.RT, D), BF),
        compiler_params=pltpu.CompilerParams(vmem_limit_bytes=VMEM_LIMIT))(*[shards[n] for n in ORDER])


def _load_ffn_weights(srcs, offs, scratch, sem):
    @pl.when(pl.program_id(0) == 0)
    def _():
        cps = [pltpu.make_async_copy(s.at[pl.ds(off, dst.shape[0]), :], dst, sem.at[i])
               for i, (s, off, dst) in enumerate(zip(srcs, offs, scratch))]
        for cp in cps:
            cp.start()
        for cp in cps:
            cp.wait()


def _final_loss_tile(xf, g, tgt, s_ref):
    r = lax.rsqrt(jnp.mean(xf * xf, axis=-1, keepdims=True) + EPS)
    xr = xf * r
    e = xr * g - tgt
    s_ref[1:2, :] += jnp.sum(e * e, axis=0, keepdims=True) * (0.5 / D)
    dy = e * (1.0 / D)
    s_ref[0:1, :] += jnp.sum(dy * xr, axis=0, keepdims=True)
    gdy = dy * g
    return r * gdy - xr * (r * jnp.mean(gdy * xr, axis=-1, keepdims=True))


def _ffn_fwd(x, g, wbufs, offs, name, comm=None, final=None):
    nf = F // FC

    def body(x_ref, g_ref, b0, b1, b2, *rest):
        if final is None:
            h_ref, n_ref, gg_ref, uu_ref, wg_s, wu_s, wd_s, sem = rest
        else:
            gf_ref, t_ref, dh_ref, s_ref, n_ref, gg_ref, uu_ref, wg_s, wu_s, wd_s, sem = rest

            @pl.when(pl.program_id(0) == 0)
            def _():
                s_ref[...] = jnp.zeros_like(s_ref)

        _load_ffn_weights((b0, b1, b2), offs, (wg_s, wu_s, wd_s), sem)
        xf = x_ref[...]
        r = lax.rsqrt(jnp.mean(xf * xf, axis=-1, keepdims=True) + EPS)
        nb = (xf * r * g_ref[...]).astype(BF)
        n_ref[...] = nb
        acc = jnp.zeros((TM, D), F32)
        for c in range(nf):
            sl = slice(c * FC, (c + 1) * FC)
            gb = _nt(nb, wg_s[sl, :]).astype(BF)
            ub = _nt(nb, wu_s[sl, :]).astype(BF)
            gg_ref[:, sl] = gb
            uu_ref[:, sl] = ub
            acc = acc + _nn((gb * _sig(gb)) * ub, wd_s[sl, :])
        h = xf + 0.5 * acc
        if final is None:
            h_ref[...] = h
        else:
            dh_ref[...] = _final_loss_tile(h, gf_ref[...], t_ref[...], s_ref)

    row = lambda i: (i, 0)
    vec = pl.BlockSpec((1, D), lambda i: (0, 0))
    tile = pl.BlockSpec((TM, D), row)
    saved_shapes = [jax.ShapeDtypeStruct((T, D), BF), jax.ShapeDtypeStruct((T, F), BF), jax.ShapeDtypeStruct((T, F), BF)]
    saved_specs = [tile, pl.BlockSpec((TM, F), row), pl.BlockSpec((TM, F), row)]
    if final is None:
        extra_args, extra_specs = [], []
        head_shapes, head_specs = [jax.ShapeDtypeStruct((T, D), F32)], [tile]
    else:
        extra_args, extra_specs = list(final), [vec, tile]
        head_shapes = [jax.ShapeDtypeStruct((T, D), F32), jax.ShapeDtypeStruct((8, D), F32)]
        head_specs = [tile, pl.BlockSpec((8, D), lambda i: (0, 0))]
    return _call(
        body, name=name, grid=(T // TM,), args=[x, g, *wbufs, *extra_args], comm=comm,
        in_specs=[tile, vec, ANY, ANY, ANY] + extra_specs,
        out_shape=head_shapes + saved_shapes, out_specs=head_specs + saved_specs,
        scratch_shapes=[pltpu.VMEM((F, D), BF)] * 3 + [pltpu.SemaphoreType.DMA((3,))])


def _mix_in(h1, gm, win, comm=None):
    def body(h_ref, g_ref, w_any, u_ref, z_ref, w_s, sem):
        _load_ffn_weights((w_any,), (0,), (w_s,), sem)
        xf = h_ref[...]
        r = lax.rsqrt(jnp.mean(xf * xf, axis=-1, keepdims=True) + EPS)
        ub = (xf * r * g_ref[...]).astype(BF)
        u_ref[...] = ub
        for j in range(NG):
            z_ref[j] = _nt(ub, w_s[j * D:(j + 1) * D, :]).astype(BF)

    row = lambda i: (i, 0)
    return _call(
        body, name="mix_in", grid=(T // TM,), args=[h1, gm, win], comm=comm,
        in_specs=[pl.BlockSpec((TM, D), row), pl.BlockSpec((1, D), lambda i: (0, 0)), ANY],
        out_shape=[jax.ShapeDtypeStruct((T, D), BF), jax.ShapeDtypeStruct((NG, T, D), BF)],
        out_specs=[pl.BlockSpec((TM, D), row), pl.BlockSpec((NG, TM, D), lambda i: (0, i, 0))],
        scratch_shapes=[pltpu.VMEM((NG * D, D), BF), pltpu.SemaphoreType.DMA((1,))])


def _shift_up(w, b):
    return w if b == 0 else pltpu.roll(w, w.shape[0] - b, 0)


def _fold8(p):
    red = p[0:8, :]
    for i in range(1, p.shape[0] // 8):
        red = red + p[8 * i:8 * i + 8, :]
    return red


def _dft_constants():
    import numpy as np
    nh = NB // 2
    f, n = np.arange(nh)[:, None], np.arange(NB)[None, :]
    ang = 2.0 * np.pi / NB * f * n
    fc = np.cos(ang)
    fs = np.where(f == 0, (-1.0) ** n, np.sin(ang))
    scale = np.where(f == 0, 1.0, 2.0) / NB
    ic = (scale * np.cos(ang)).T
    isn = np.where(f == 0, (-1.0) ** n / NB, scale * np.sin(ang)).T
    d = (KA - 1 - np.arange(32))[None, :]
    valid = (np.arange(32) < KA)[None, :]
    angk = 2.0 * np.pi / NB * f * d
    kc = np.where(valid, np.cos(angk), 0.0)
    ks = np.where(valid, np.sin(angk), 0.0)
    k2 = np.where(valid, np.where(f == 0, (-1.0) ** d, np.cos(angk)), 0.0)
    rtc = np.where(valid, scale * np.cos(angk), 0.0).T
    rts = np.where(valid, np.where(f == 0, (-1.0) ** d / NB, scale * np.sin(angk)), 0.0).T

    def bf(a):
        return jnp.asarray(a, F32).astype(BF)

    def split(a):
        hi = bf(a)
        return hi, (jnp.asarray(a, F32) - hi.astype(F32)).astype(BF)

    return dict(fc=bf(fc), fs=bf(fs), ic_hi=bf(ic[HB:]), is_hi=bf(isn[HB:]), ic_lo=bf(ic[:HB]), is_lo=bf(isn[:HB]),
                kc=split(kc), ks=split(ks), k2=split(k2), rtc=split(rtc), rts=split(rts))


def _dot3(m_hi, m_lo, x):
    x_hi = x.astype(BF)
    x_lo = (x - x_hi.astype(F32)).astype(BF)
    return _nn(m_hi, x_hi) + _nn(m_hi, x_lo) + _nn(m_lo, x_hi)


def _whole(a):
    return pl.BlockSpec(a.shape, lambda c, t: (0,) * a.ndim)


def _filter_spectrum(cw_ref, tabs, hc, hs, h2):
    w32 = cw_ref[0:32, :]
    for (hi, lo), dst in zip(tabs, (hc, hs, h2)):
        dst[...] = _dot3(hi[...], lo[...], w32)


def _conv_fwd_dft(z, cw, bias, dft, comm=None):
    nt = T // TB
    hb = TB // HB

    def body(z_ref, zh_ref, cw_ref, b_ref, fc_ref, fs_ref, ic_ref, is_ref, kch, kcl, ksh, ksl, k2h, k2l,
             a1_ref, q_ref, aext, ppad, hc, hs, h2):
        first = pl.program_id(1) == 0
        f = lambda ref, j: ref[j].astype(F32)

        @pl.when(first)
        def _():
            _filter_spectrum(cw_ref, ((kch, kcl), (ksh, ksl), (k2h, k2l)), hc, hs, h2)

        aext[0:HB, :] = jnp.where(first, 0.0, f(zh_ref, 0) * _sig(f(zh_ref, 1))).astype(BF)
        aext[HB:, :] = (f(z_ref, 0) * _sig(f(z_ref, 1))).astype(BF)
        ppad[0:8, :] = jnp.where(first, 0.0, f(zh_ref, 3)[HB - 8:HB, :] * f(zh_ref, 4)[HB - 8:HB, :])
        ppad[8:, :] = f(z_ref, 3) * f(z_ref, 4)
        bias_row = b_ref[...]

        for j in range(TB // HB):
            xs = aext[j * HB:j * HB + NB, :]
            xa, xb = _nn(fc_ref[...], xs), _nn(fs_ref[...], xs)
            yc = (hc[...] * xa - hs[...] * xb).astype(BF)
            ys = (h2[...] * xb + hs[...] * xa).astype(BF)
            y = _nn(ic_ref[...], yc) + _nn(is_ref[...], ys)
            a1_ref[j * HB:(j + 1) * HB, :] = (y + bias_row).astype(BF)

        def chunk(r, carry):
            base = pl.multiple_of(r * CHB, CHB)
            pw = ppad[pl.ds(base, CHB + 8), :]
            v = (cw_ref[pl.ds(32, 1), :] * _shift_up(pw, 6)[0:CHB, :]
                 + cw_ref[pl.ds(33, 1), :] * _shift_up(pw, 7)[0:CHB, :]
                 + cw_ref[pl.ds(34, 1), :] * pw[8:8 + CHB, :])
            q_ref[pl.ds(base, CHB), :] = (z_ref[2, pl.ds(base, CHB), :].astype(F32) * v).astype(BF)
            return carry

        lax.fori_loop(0, TB // CHB, chunk, 0)

    blk = pl.BlockSpec((TB, CW), lambda c, t: (t, c))
    tabs = [dft["fc"], dft["fs"], dft["ic_hi"], dft["is_hi"], *dft["kc"], *dft["ks"], *dft["k2"]]
    return _call(
        body, name="conv_fwd", grid=(D // CW, nt), comm=comm, args=[z, z, cw, bias] + tabs,
        in_specs=[pl.BlockSpec((5, TB, CW), lambda c, t: (0, t, c)),
                  pl.BlockSpec((5, HB, CW), lambda c, t: (0, jnp.maximum(t * hb - 1, 0), c)),
                  pl.BlockSpec((40, CW), lambda c, t: (0, c)), pl.BlockSpec((1, CW), lambda c, t: (0, c))]
                 + [_whole(a) for a in tabs],
        out_shape=[jax.ShapeDtypeStruct((T, D), BF), jax.ShapeDtypeStruct((T, D), BF)], out_specs=[blk, blk],
        scratch_shapes=[pltpu.VMEM((TB + HB, CW), BF), pltpu.VMEM((TB + 8, CW), F32)]
                       + [pltpu.VMEM((NB // 2, CW), F32)] * 3)


def _conv_bwd_dft(z, da1, dq, dzg, cw, dft, comm=None):
    nt = T // TB
    hb = TB // HB
    last_h = T // HB - 1

    def body(z_ref, zp_ref, zn_ref, da1_ref, da1n_ref, dq_ref, dqn_ref, dzg_ref, cw_ref,
             fc_ref, fs_ref, ic_ref, is_ref, kch, kcl, ksh, ksl, k2h, k2l, rch, rcl, rsh, rsl,
             dz_ref, dwa_ref, dwb_ref, aext, dyext, ppad, dvpad, hc, hs, h2, rc, rs, nyq, acc_b):
        t = pl.program_id(1)
        first, last = t == 0, t == nt - 1
        f = lambda ref, j: ref[j].astype(F32)

        @pl.when(first)
        def _():
            _filter_spectrum(cw_ref, ((kch, kcl), (ksh, ksl), (k2h, k2l)), hc, hs, h2)
            rc[...] = jnp.zeros_like(rc)
            rs[...] = jnp.zeros_like(rs)
            nyq[...] = jnp.zeros_like(nyq)
            acc_b[...] = jnp.zeros_like(acc_b)

        aext[0:HB, :] = jnp.where(first, 0.0, f(zp_ref, 0) * _sig(f(zp_ref, 1))).astype(BF)
        aext[HB:, :] = (f(z_ref, 0) * _sig(f(z_ref, 1))).astype(BF)
        dyext[0:TB, :] = da1_ref[...]
        dyext[TB:, :] = jnp.where(last, 0.0, da1n_ref[...].astype(F32)).astype(BF)
        ppad[0:8, :] = jnp.where(first, 0.0, f(zp_ref, 3)[HB - 8:HB, :] * f(zp_ref, 4)[HB - 8:HB, :])
        ppad[8:, :] = f(z_ref, 3) * f(z_ref, 4)
        dvpad[0:TB, :] = dq_ref[...].astype(F32) * f(z_ref, 2)
        dvpad[TB:, :] = jnp.where(last, 0.0, dqn_ref[...].astype(F32)[0:8, :] * f(zn_ref, 2)[0:8, :])

        for j in range(TB // HB):
            rows = slice(j * HB, (j + 1) * HB)
            dys = dyext[j * HB:j * HB + NB, :]
            da, db = _nn(fc_ref[...], dys), _nn(fs_ref[...], dys)
            gc = (hc[...] * da + hs[...] * db).astype(BF)
            gs = (h2[...] * db - hs[...] * da).astype(BF)
            da0 = _nn(ic_ref[...], gc) + _nn(is_ref[...], gs)
            z0, z1 = z_ref[0, rows, :].astype(F32), z_ref[1, rows, :].astype(F32)
            s1 = _sig(z1)
            dz_ref[0, rows, :] = (da0 * s1).astype(BF)
            dz_ref[1, rows, :] = (da0 * z0 * (s1 * (1.0 - s1))).astype(BF)
            xs = aext[j * HB:j * HB + NB, :]
            xa, xb = _nn(fc_ref[...], xs), _nn(fs_ref[...], xs)
            dyb = dyext[rows, :]
            pa, pb = _nn(fc_ref[:, HB:NB], dyb), _nn(fs_ref[:, HB:NB], dyb)
            rc[...] += pa * xa + pb * xb
            rs[...] += pb * xa - pa * xb
            nyq[...] += pb[0:8, :] * xb[0:8, :]

        def chunk(r, carry):
            base = pl.multiple_of(r * CHB, CHB)
            rows = pl.ds(base, CHB)
            pw = ppad[pl.ds(base, CHB + 8), :]
            p6 = _shift_up(pw, 6)[0:CHB, :]
            p7 = _shift_up(pw, 7)[0:CHB, :]
            p8 = pw[8:8 + CHB, :]
            wb0, wb1, wb2 = cw_ref[pl.ds(32, 1), :], cw_ref[pl.ds(33, 1), :], cw_ref[pl.ds(34, 1), :]
            v = wb0 * p6 + wb1 * p7 + wb2 * p8
            dz_ref[2, rows, :] = (dq_ref[rows, :].astype(F32) * v).astype(BF)
            dvw = dvpad[pl.ds(base, CHB + 8), :]
            dvc = dvw[0:CHB, :]
            dp = wb2 * dvc + wb1 * _shift_up(dvw, 1)[0:CHB, :] + wb0 * _shift_up(dvw, 2)[0:CHB, :]
            dz_ref[3, rows, :] = (dp * z_ref[4, rows, :].astype(F32)).astype(BF)
            dz_ref[4, rows, :] = (dp * z_ref[3, rows, :].astype(F32)).astype(BF)
            acc_b[0:8, :] += _fold8(dvc * p6)
            acc_b[8:16, :] += _fold8(dvc * p7)
            acc_b[16:24, :] += _fold8(dvc * p8)
            dz_ref[5, rows, :] = dzg_ref[0, rows, :]
            dz_ref[6, rows, :] = dzg_ref[1, rows, :]
            return carry

        lax.fori_loop(0, TB // CHB, chunk, 0)

        @pl.when(last)
        def _():
            row0 = lax.broadcasted_iota(jnp.int32, (NB // 2, CW), 0) == 0
            ny = jnp.broadcast_to(nyq[0:1, :], (NB // 2, CW))
            rcv = jnp.where(row0, rc[...] - ny, rc[...])
            rsv = jnp.where(row0, ny, rs[...])
            dwa_ref[...] = _dot3(rch[...], rcl[...], rcv) + _dot3(rsh[...], rsl[...], rsv)
            for k in range(KB):
                dwb_ref[k:k + 1, :] = jnp.sum(acc_b[8 * k:8 * k + 8, :], axis=0, keepdims=True)
            dwb_ref[KB:8, :] = jnp.zeros((8 - KB, CW), F32)

    blk = lambda c, t: (t, c)
    nxt = lambda c, t: (jnp.minimum((t + 1) * hb, last_h), c)
    tabs = [dft["fc"], dft["fs"], dft["ic_lo"], dft["is_lo"], *dft["kc"], *dft["ks"], *dft["k2"], *dft["rtc"], *dft["rts"]]
    return _call(
        body, name="conv_bwd", grid=(D // CW, nt), comm=comm, args=[z, z, z, da1, da1, dq, dq, dzg, cw] + tabs,
        in_specs=[pl.BlockSpec((5, TB, CW), lambda c, t: (0, t, c)),
                  pl.BlockSpec((5, HB, CW), lambda c, t: (0, jnp.maximum(t * hb - 1, 0), c)),
                  pl.BlockSpec((5, HB, CW), lambda c, t: (0, jnp.minimum((t + 1) * hb, last_h), c)),
                  pl.BlockSpec((TB, CW), blk), pl.BlockSpec((HB, CW), nxt),
                  pl.BlockSpec((TB, CW), blk), pl.BlockSpec((HB, CW), nxt),
                  pl.BlockSpec((2, TB, CW), lambda c, t: (0, t, c)),
                  pl.BlockSpec((40, CW), lambda c, t: (0, c))]
                 + [_whole(a) for a in tabs],
        out_shape=[jax.ShapeDtypeStruct((NG, T, D), BF), jax.ShapeDtypeStruct((32, D), F32),
                   jax.ShapeDtypeStruct((8, D), F32)],
        out_specs=[pl.BlockSpec((NG, TB, CW), lambda c, t: (0, t, c)),
                   pl.BlockSpec((32, CW), lambda c, t: (0, c)), pl.BlockSpec((8, CW), lambda c, t: (0, c))],
        scratch_shapes=[pltpu.VMEM((TB + HB, CW), BF), pltpu.VMEM((TB + HB, CW), BF),
                        pltpu.VMEM((TB + 8, CW), F32), pltpu.VMEM((TB + 8, CW), F32)]
                       + [pltpu.VMEM((NB // 2, CW), F32)] * 5 + [pltpu.VMEM((8, CW), F32), pltpu.VMEM((24, CW), F32)])


def _layernorm_silu(a1, lng, lnb):
    mu = jnp.mean(a1, axis=-1, keepdims=True)
    xc = a1 - mu
    rs = lax.rsqrt(jnp.mean(xc * xc, axis=-1, keepdims=True) + EPS)
    xh = xc * rs
    a2 = xh * lng + lnb
    sg = _sig(a2)
    return xh, rs, a2, sg


def _square_specs(blocks):
    return [pl.BlockSpec((D, D), lambda i, b=b: (b, 0)) for b in blocks]


def _mix_out(a1, q, z, h1, lng, lnb, wsq, comm=None):
    def body(a1_ref, q_ref, ga_ref, gb_ref, h_ref, lng_ref, lnb_ref, wa_ref, wb_ref, wo_ref, h2_ref, ya_ref, yb_ref):
        _, _, a2, sg = _layernorm_silu(a1_ref[...].astype(F32), lng_ref[...], lnb_ref[...])
        ya = _nn((a2 * sg).astype(BF), wa_ref[...])
        yb = _nn(q_ref[...], wb_ref[...])
        ya_ref[...] = ya.astype(BF)
        yb_ref[...] = yb.astype(BF)
        m = _sig(ga_ref[...].astype(F32)) * ya + _sig(gb_ref[...].astype(F32)) * yb
        h2_ref[...] = h_ref[...] + _nn(m.astype(BF), wo_ref[...])

    row = lambda i: (i, 0)
    vec = pl.BlockSpec((1, D), lambda i: (0, 0))
    return _call(
        body, name="mix_out", grid=(T // TM,), args=[a1, q, z, z, h1, lng, lnb, wsq, wsq, wsq], comm=comm,
        in_specs=[pl.BlockSpec((TM, D), row), pl.BlockSpec((TM, D), row),
                  pl.BlockSpec((None, TM, D), lambda i: (5, i, 0)), pl.BlockSpec((None, TM, D), lambda i: (6, i, 0)),
                  pl.BlockSpec((TM, D), row), vec, vec] + _square_specs((0, 1, 2)),
        out_shape=[jax.ShapeDtypeStruct((T, D), F32), jax.ShapeDtypeStruct((T, D), BF), jax.ShapeDtypeStruct((T, D), BF)],
        out_specs=[pl.BlockSpec((TM, D), row)] * 3)


def _rmsnorm_bwd(xf, g, dn):
    r = lax.rsqrt(jnp.mean(xf * xf, axis=-1, keepdims=True) + EPS)
    xr = xf * r
    gdn = dn * g
    dx = r * gdn - xr * (r * jnp.mean(gdn * xr, axis=-1, keepdims=True))
    return dx, jnp.sum(dn * xr, axis=0, keepdims=True)


def _ffn_bwd_hidden(dh, gg, uu, wbuf, off, name, comm=None):
    nf = F // FC

    def body(dh_ref, gg_ref, uu_ref, b0, dgu_ref, a_ref, wd_s, sem):
        _load_ffn_weights((b0,), (off,), (wd_s,), sem)
        dhb = (0.5 * dh_ref[...]).astype(BF)
        for c in range(nf):
            sl = slice(c * FC, (c + 1) * FC)
            da = _nt(dhb, wd_s[sl, :]).astype(BF)
            gb, ub = gg_ref[:, sl], uu_ref[:, sl]
            sg = _sig(gb)
            silu = gb * sg
            dgu_ref[0, :, sl] = (da * ub) * (sg * (1.0 + gb * (1.0 - sg)))
            dgu_ref[1, :, sl] = da * silu
            a_ref[0, :, sl] = silu * ub

    row = lambda i: (i, 0)
    return _call(
        body, name=name, grid=(T // TM,), args=[dh, gg, uu, wbuf], comm=comm,
        in_specs=[pl.BlockSpec((TM, D), row), pl.BlockSpec((TM, F), row), pl.BlockSpec((TM, F), row), ANY],
        out_shape=[jax.ShapeDtypeStruct((2, T, F), BF), jax.ShapeDtypeStruct((1, T, F), BF)],
        out_specs=[pl.BlockSpec((2, TM, F), lambda i: (0, i, 0)), pl.BlockSpec((1, TM, F), lambda i: (0, i, 0))],
        scratch_shapes=[pltpu.VMEM((F, D), BF), pltpu.SemaphoreType.DMA((1,))])


def _ffn_bwd_input(dgu, dh, x, g, wbufs, offs, name, comm=None):
    def body(dgu_ref, dh_ref, x_ref, g_ref, b0, b1, dx_ref, s_ref, wg_s, wu_s, sem):
        _load_ffn_weights((b0, b1), offs, (wg_s, wu_s), sem)

        @pl.when(pl.program_id(0) == 0)
        def _():
            s_ref[...] = jnp.zeros_like(s_ref)

        dn = _nn(dgu_ref[0], wg_s[...]) + _nn(dgu_ref[1], wu_s[...])
        dxn, dg = _rmsnorm_bwd(x_ref[...], g_ref[...], dn)
        dx_ref[...] = dh_ref[...] + dxn
        s_ref[0:1, :] += dg

    row = lambda i: (i, 0)
    return _call(
        body, name=name, grid=(T // TM,), args=[dgu, dh, x, g, *wbufs], comm=comm,
        in_specs=[pl.BlockSpec((2, TM, F), lambda i: (0, i, 0)), pl.BlockSpec((TM, D), row),
                  pl.BlockSpec((TM, D), row), pl.BlockSpec((1, D), lambda i: (0, 0)), ANY, ANY],
        out_shape=[jax.ShapeDtypeStruct((T, D), F32), jax.ShapeDtypeStruct((8, D), F32)],
        out_specs=[pl.BlockSpec((TM, D), row), pl.BlockSpec((8, D), lambda i: (0, 0))],
        scratch_shapes=[pltpu.VMEM((F, D), BF)] * 2 + [pltpu.SemaphoreType.DMA((2,))])


def _tn_matmul(lhs, rhs, tr, name, comm=None, scale=None):
    ng, _, cdim = lhs.shape
    nc, nk = cdim // tr, T // TK

    def body(l_ref, r_ref, o_ref, acc):
        k = pl.program_id(2)

        @pl.when(k == 0)
        def _():
            acc[...] = jnp.zeros_like(acc)

        r = r_ref[...] if scale is None else scale * r_ref[...]
        acc[...] += _tn(l_ref[...], r.astype(BF))

        @pl.when(k == nk - 1)
        def _():
            o_ref[...] = acc[...].astype(BF)

    return _call(
        body, name=name, grid=(ng, nc, nk), args=[lhs, rhs], comm=comm,
        in_specs=[pl.BlockSpec((None, TK, tr), lambda g, c, k: (g, k, c)),
                  pl.BlockSpec((TK, D), lambda g, c, k: (k, 0))],
        out_shape=[jax.ShapeDtypeStruct((ng * cdim, D), BF)],
        out_specs=[pl.BlockSpec((tr, D), lambda g, c, k: (g * nc + c, 0))],
        scratch_shapes=[pltpu.VMEM((tr, D), F32)])


def _mix_out_bwd(dh2, ya, yb, z, a1, lng, lnb, wsq, comm=None):
    def body(dh_ref, ya_ref, yb_ref, ga_ref, gb_ref, a1_ref, lng_ref, lnb_ref, wa_ref, wb_ref, wo_ref,
             dzg_ref, da1_ref, dq_ref, m_ref, a3_ref, dya_ref, dyb_ref, s_ref):
        @pl.when(pl.program_id(0) == 0)
        def _():
            s_ref[...] = jnp.zeros_like(s_ref)

        dm = _nt(dh_ref[...].astype(BF), wo_ref[...])
        ya, yb = ya_ref[...].astype(F32), yb_ref[...].astype(F32)
        sa, sb = _sig(ga_ref[...].astype(F32)), _sig(gb_ref[...].astype(F32))
        m_ref[0] = (sa * ya + sb * yb).astype(BF)
        dzg_ref[0] = (dm * ya * (sa * (1.0 - sa))).astype(BF)
        dzg_ref[1] = (dm * yb * (sb * (1.0 - sb))).astype(BF)
        dya = (dm * sa).astype(BF)
        dyb = (dm * sb).astype(BF)
        dya_ref[...] = dya
        dyb_ref[...] = dyb
        dq_ref[...] = _nt(dyb, wb_ref[...]).astype(BF)
        da3 = _nt(dya, wa_ref[...])
        lng = lng_ref[...]
        xh, rs, a2, sg = _layernorm_silu(a1_ref[...].astype(F32), lng, lnb_ref[...])
        a3_ref[0] = (a2 * sg).astype(BF)
        da2 = da3 * (sg * (1.0 + a2 * (1.0 - sg)))
        s_ref[0:1, :] += jnp.sum(da2 * xh, axis=0, keepdims=True)
        s_ref[1:2, :] += jnp.sum(da2, axis=0, keepdims=True)
        dxh = da2 * lng
        da1 = rs * (dxh - jnp.mean(dxh, axis=-1, keepdims=True) - xh * jnp.mean(dxh * xh, axis=-1, keepdims=True))
        da1_ref[...] = da1.astype(BF)
        s_ref[2:3, :] += jnp.sum(da1, axis=0, keepdims=True)

    row = lambda i: (i, 0)
    row3 = lambda i: (0, i, 0)
    vec = pl.BlockSpec((1, D), lambda i: (0, 0))
    return _call(
        body, name="mix_out_bwd", grid=(T // TM,), args=[dh2, ya, yb, z, z, a1, lng, lnb, wsq, wsq, wsq], comm=comm,
        in_specs=[pl.BlockSpec((TM, D), row), pl.BlockSpec((TM, D), row), pl.BlockSpec((TM, D), row),
                  pl.BlockSpec((None, TM, D), lambda i: (5, i, 0)), pl.BlockSpec((None, TM, D), lambda i: (6, i, 0)),
                  pl.BlockSpec((TM, D), row), vec, vec] + _square_specs((0, 1, 2)),
        out_shape=[jax.ShapeDtypeStruct((2, T, D), BF), jax.ShapeDtypeStruct((T, D), BF),
                   jax.ShapeDtypeStruct((T, D), BF), jax.ShapeDtypeStruct((1, T, D), BF),
                   jax.ShapeDtypeStruct((1, T, D), BF), jax.ShapeDtypeStruct((T, D), BF),
                   jax.ShapeDtypeStruct((T, D), BF), jax.ShapeDtypeStruct((8, D), F32)],
        out_specs=[pl.BlockSpec((2, TM, D), row3), pl.BlockSpec((TM, D), row), pl.BlockSpec((TM, D), row),
                   pl.BlockSpec((1, TM, D), row3), pl.BlockSpec((1, TM, D), row3), pl.BlockSpec((TM, D), row),
                   pl.BlockSpec((TM, D), row), pl.BlockSpec((8, D), lambda i: (0, 0))])


def _mix_in_bwd(dz, dh2, h1, gm, win, comm=None):
    def body(dz_ref, w_any, dh_ref, h_ref, g_ref, o_ref, s_ref, w_s, sem):
        _load_ffn_weights((w_any,), (0,), (w_s,), sem)

        @pl.when(pl.program_id(0) == 0)
        def _():
            s_ref[...] = jnp.zeros_like(s_ref)

        du = _nn(dz_ref[0], w_s[0:D, :])
        for j in range(1, NG):
            du = du + _nn(dz_ref[j], w_s[j * D:(j + 1) * D, :])
        dx, dg = _rmsnorm_bwd(h_ref[...], g_ref[...], du)
        o_ref[...] = dh_ref[...] + dx
        s_ref[0:1, :] += dg

    row = lambda i: (i, 0)
    return _call(
        body, name="mix_in_bwd", grid=(T // TM,), args=[dz, win, dh2, h1, gm], comm=comm,
        in_specs=[pl.BlockSpec((NG, TM, D), lambda i: (0, i, 0)), ANY,
                  pl.BlockSpec((TM, D), row), pl.BlockSpec((TM, D), row), pl.BlockSpec((1, D), lambda i: (0, 0))],
        out_shape=[jax.ShapeDtypeStruct((T, D), F32), jax.ShapeDtypeStruct((8, D), F32)],
        out_specs=[pl.BlockSpec((TM, D), row), pl.BlockSpec((8, D), lambda i: (0, 0))],
        scratch_shapes=[pltpu.VMEM((NG * D, D), BF), pltpu.SemaphoreType.DMA((1,))])


def _row_tile(n, want, mult):
    for t in range(min(want, n), 0, -1):
        if n % t == 0 and t % mult == 0:
            return t
    return n


def _sum_slots(recv, name):
    ns, rows, cols = recv.shape
    tr = _row_tile(rows, 1024, 16)

    def body(r_ref, o_ref):
        s = r_ref[0].astype(F32)
        for k in range(1, ns):
            s = s + r_ref[k].astype(F32)
        o_ref[...] = s

    return _call(
        body, name=name, grid=(rows // tr,), args=[recv],
        in_specs=[pl.BlockSpec((ns, tr, cols), lambda i: (0, i, 0))],
        out_shape=[jax.ShapeDtypeStruct((rows, cols), F32)],
        out_specs=[pl.BlockSpec((tr, cols), lambda i: (i, 0))])[0]


def _pack_small(s_ffn1, s_in, s_mix, s_ffn2, s_final, dwa, dwb):
    def body(f1, mi, mo, f2, fl, wa_ref, wb_ref, v_ref, k_ref):
        for dst, (ref, row) in enumerate(((f1, 0), (mi, 0), (mo, 0), (mo, 1), (mo, 2), (f2, 0), (fl, 0), (fl, 1))):
            v_ref[dst:dst + 1, :] = ref[row:row + 1, :]
        for k in range(NDEV):
            k_ref[k, 0:32, :] = wa_ref[:, k * LANE:(k + 1) * LANE]
            k_ref[k, 32:40, :] = wb_ref[:, k * LANE:(k + 1) * LANE]

    return pl.pallas_call(
        body, name="pack_small",
        out_shape=(jax.ShapeDtypeStruct((8, D), F32), jax.ShapeDtypeStruct((NDEV, 40, LANE), F32)),
    )(s_ffn1, s_in, s_mix, s_ffn2, s_final, dwa, dwb)


def _sum_small(vecs, convs):
    def body(v_ref, k_ref, vs_ref, ks_ref, l_ref):
        s, c = v_ref[0], k_ref[0]
        for k in range(1, NDEV):
            s = s + v_ref[k]
            c = c + k_ref[k]
        vs_ref[...] = s
        ks_ref[...] = c
        l_ref[...] = jnp.broadcast_to(jnp.sum(s[7:8, :], axis=-1, keepdims=True), (8, LANE))

    return pl.pallas_call(
        body, name="sum_small",
        out_shape=(jax.ShapeDtypeStruct((8, D), F32), jax.ShapeDtypeStruct((40, LANE), F32),
                   jax.ShapeDtypeStruct((8, LANE), F32)),
    )(vecs, convs)


def _adam(gs, ws, ms, vs, name, comm=None):
    n = len(gs)
    rows, cols = ws[0].shape
    tr = _row_tile(rows, 256, 8)
    c1 = 1.0 - ADAM_B1 ** ADAM_STEP
    c2 = 1.0 - ADAM_B2 ** ADAM_STEP

    def body(*refs):
        for i in range(n):
            g, w, m, v = (refs[4 * i + k][...] for k in range(4))
            d_ref, m_ref, v_ref = refs[4 * n + 3 * i: 4 * n + 3 * i + 3]
            m2 = ADAM_B1 * m + (1.0 - ADAM_B1) * g
            v2 = ADAM_B2 * v + (1.0 - ADAM_B2) * (g * g)
            d_ref[...] = -ADAM_LR * ((m2 / c1) / (jnp.sqrt(v2 / c2) + ADAM_EPS) + ADAM_WD * w)
            m_ref[...] = m2
            v_ref[...] = v2

    spec = pl.BlockSpec((tr, cols), lambda i: (i, 0))
    args = []
    for i in range(n):
        args += [gs[i], ws[i], ms[i], vs[i]]
    outs = _call(body, name=name, grid=(rows // tr,), args=args, comm=comm, in_specs=[spec] * (4 * n),
                 out_shape=[jax.ShapeDtypeStruct((rows, cols), F32)] * (3 * n), out_specs=[spec] * (3 * n))
    return [tuple(outs[3 * i: 3 * i + 3]) for i in range(n)], outs[3 * n:]


def kernel(x, ffn1_norm, ffn1_w_gate, ffn1_w_up, ffn1_w_down, mix_norm, w_in, a_dw_w, a_dw_b, a_ln_g, a_ln_b, a_w_out, b_conv_w, b_w_out, w_o, ffn2_norm, ffn2_w_gate, ffn2_w_up, ffn2_w_down, final_norm, loss_target, m_ffn1_norm, m_ffn1_w_gate, m_ffn1_w_up, m_ffn1_w_down, m_mix_norm, m_w_in, m_a_dw_w, m_a_dw_b, m_a_ln_g, m_a_ln_b, m_a_w_out, m_b_conv_w, m_b_w_out, m_w_o, m_ffn2_norm, m_ffn2_w_gate, m_ffn2_w_up, m_ffn2_w_down, m_final_norm, v_ffn1_norm, v_ffn1_w_gate, v_ffn1_w_up, v_ffn1_w_down, v_mix_norm, v_w_in, v_a_dw_w, v_a_dw_b, v_a_ln_g, v_a_ln_b, v_a_w_out, v_b_conv_w, v_b_w_out, v_w_o, v_ffn2_norm, v_ffn2_w_gate, v_ffn2_w_up, v_ffn2_w_down, v_final_norm):
    names = ("ffn1_norm", "ffn1_w_gate", "ffn1_w_up", "ffn1_w_down", "mix_norm", "w_in", "a_dw_w", "a_dw_b",
             "a_ln_g", "a_ln_b", "a_w_out", "b_conv_w", "b_w_out", "w_o", "ffn2_norm", "ffn2_w_gate", "ffn2_w_up",
             "ffn2_w_down", "final_norm")
    w = dict(ffn1_norm=ffn1_norm, ffn1_w_gate=ffn1_w_gate, ffn1_w_up=ffn1_w_up, ffn1_w_down=ffn1_w_down,
             mix_norm=mix_norm, w_in=w_in, a_dw_w=a_dw_w, a_dw_b=a_dw_b, a_ln_g=a_ln_g, a_ln_b=a_ln_b,
             a_w_out=a_w_out, b_conv_w=b_conv_w, b_w_out=b_w_out, w_o=w_o, ffn2_norm=ffn2_norm,
             ffn2_w_gate=ffn2_w_gate, ffn2_w_up=ffn2_w_up, ffn2_w_down=ffn2_w_down, final_norm=final_norm)
    m = dict(ffn1_norm=m_ffn1_norm, ffn1_w_gate=m_ffn1_w_gate, ffn1_w_up=m_ffn1_w_up, ffn1_w_down=m_ffn1_w_down,
             mix_norm=m_mix_norm, w_in=m_w_in, a_dw_w=m_a_dw_w, a_dw_b=m_a_dw_b, a_ln_g=m_a_ln_g, a_ln_b=m_a_ln_b,
             a_w_out=m_a_w_out, b_conv_w=m_b_conv_w, b_w_out=m_b_w_out, w_o=m_w_o, ffn2_norm=m_ffn2_norm,
             ffn2_w_gate=m_ffn2_w_gate, ffn2_w_up=m_ffn2_w_up, ffn2_w_down=m_ffn2_w_down, final_norm=m_final_norm)
    v = dict(ffn1_norm=v_ffn1_norm, ffn1_w_gate=v_ffn1_w_gate, ffn1_w_up=v_ffn1_w_up, ffn1_w_down=v_ffn1_w_down,
             mix_norm=v_mix_norm, w_in=v_w_in, a_dw_w=v_a_dw_w, a_dw_b=v_a_dw_b, a_ln_g=v_a_ln_g, a_ln_b=v_a_ln_b,
             a_w_out=v_a_w_out, b_conv_w=v_b_conv_w, b_w_out=v_b_w_out, w_o=v_w_o, ffn2_norm=v_ffn2_norm,
             ffn2_w_gate=v_ffn2_w_gate, ffn2_w_up=v_ffn2_w_up, ffn2_w_down=v_ffn2_w_down, final_norm=v_final_norm)
    flat = _pack_weights(dict(wg1=ffn1_w_gate[0].T, wu1=ffn1_w_up[0].T, wd1=ffn1_w_down[0], wg2=ffn2_w_gate[0].T,
                              wu2=ffn2_w_up[0].T, wd2=ffn2_w_down[0], win=w_in[0], wa=a_w_out[0], wb=b_w_out[0],
                              wo=w_o[0]))
    cw_shard = jnp.concatenate([a_dw_w[0], jnp.zeros((1, LANE), F32), b_conv_w[0], jnp.zeros((5, LANE), F32)], axis=0)

    x2, tgt = x[0], loss_target[0]
    st_a, st_b, st_c, st_d, st_e = ("wg1", "wu1", "wd1"), ("win",), ("wa", "wb", "wo", "wg2"), ("wu2",), ("wd2",)

    buf_a, cw = _run_comm(_join(_ag_comm(st_a, flat), _direct_comm(cw_shard, False)), "ag_ffn1")
    h1, n1, gg1, uu1, buf_b = _ffn_fwd(x2, ffn1_norm, (buf_a,) * 3, (0, F, 2 * F), "ffn1_fwd", _ag_comm(st_b, flat))
    u, z, buf_c = _mix_in(h1, mix_norm, buf_b, _ag_comm(st_c, flat))
    dft = _dft_constants()
    cw = jnp.transpose(cw, (1, 0, 2)).reshape(40, D)
    a1, q, buf_d = _conv_fwd_dft(z, cw, a_dw_b, dft, _ag_comm(st_d, flat))
    h2, ya, yb, buf_e = _mix_out(a1, q, z, h1, a_ln_g, a_ln_b, buf_c, _ag_comm(st_e, flat))
    ffn2_bufs, ffn2_offs = (buf_c, buf_d, buf_e), (3 * D, 0, 0)
    dh3, s_final, n2, gg2, uu2 = _ffn_fwd(h2, ffn2_norm, ffn2_bufs, ffn2_offs, "ffn2_fwd",
                                          final=(final_norm.reshape(1, D), tgt))

    tr_f = F // 2 if (F // 2) % LANE == 0 else F
    def pair(stage, src):
        return _rs_pair_comm(stage, src)

    def chip(stage, src, pair_buf, tag):
        return _rs_chip_comm(_pair_add(stage, src, pair_buf, "pair_add_" + tag))

    dgu2, act2 = _ffn_bwd_hidden(dh3, gg2, uu2, buf_e, 0, "ffn2_bwd_h")
    (gu2,) = _tn_matmul(dgu2, n2, tr_f, "dw_gu2")
    s2a, src2a = ("wg2", "wu2"), dict(wg2=(gu2, 0), wu2=(gu2, F))
    gd2, pair2a = _tn_matmul(act2, dh3, tr_f, "dw_d2", pair(s2a, src2a), scale=0.5)
    s2b, src2b = ("wd2",), dict(wd2=(gd2, 0))
    dh2, s_ffn2, recv2a, pair2b = _ffn_bwd_input(dgu2, dh3, h2, ffn2_norm, (buf_c, buf_d), (3 * D, 0), "ffn2_bwd_x",
                                                 _join(chip(s2a, src2a, pair2a, "2a"), pair(s2b, src2b)))
    dzg, da1, dq, mb, a3b, dya, dyb, s_mix, recv2b = _mix_out_bwd(dh2, ya, yb, z, a1, a_ln_g, a_ln_b, buf_c,
                                                                   chip(s2b, src2b, pair2b, "2b"))
    (go,) = _tn_matmul(mb, dh2, D, "dw_o")
    (ga,) = _tn_matmul(a3b, dya, D, "dw_a")
    (gb,) = _tn_matmul(q.reshape(1, T, D), dyb, D, "dw_b")
    ssq, srcsq = ("wa", "wb", "wo"), dict(wa=(ga, 0), wb=(gb, 0), wo=(go, 0))
    dz, dwa, dwb, pairsq = _conv_bwd_dft(z, da1, dq, dzg, cw, dft, pair(ssq, srcsq))
    gin, recvsq = _tn_matmul(dz, u, D, "dw_in", chip(ssq, srcsq, pairsq, "sq"))
    sin_a, sin_b, srcin = ("win/0/2",), ("win/1/2",), {"win/0/2": (gin, 0), "win/1/2": (gin, 0)}
    dh1, s_in, pairin_a, pairin_b = _mix_in_bwd(dz, dh2, h1, mix_norm, buf_b,
                                                _join(pair(sin_a, srcin), pair(sin_b, srcin)))
    dgu1, act1, recvin_a = _ffn_bwd_hidden(dh1, gg1, uu1, buf_a, 2 * F, "ffn1_bwd_h",
                                           chip(sin_a, srcin, pairin_a, "in_a"))
    gu1, recvin_b = _tn_matmul(dgu1, n1, tr_f, "dw_gu1", chip(sin_b, srcin, pairin_b, "in_b"))
    s1a, src1a = ("wg1", "wu1"), dict(wg1=(gu1, 0), wu1=(gu1, F))
    gd1, pair1a = _tn_matmul(act1, dh1, tr_f, "dw_d1", pair(s1a, src1a), scale=0.5)
    s1b, src1b = ("wd1",), dict(wd1=(gd1, 0))
    dx, s_ffn1, recv1a, pair1b = _ffn_bwd_input(dgu1, dh1, x2, ffn1_norm, (buf_a, buf_a), (0, F), "ffn1_bwd_x",
                                                _join(chip(s1a, src1a, pair1a, "1a"), pair(s1b, src1b)))
    (recv1b,) = _run_comm(chip(s1b, src1b, pair1b, "1b"), "rs_chip_1b")
    stages = ((s2a, recv2a, "2a"), (s2b, recv2b, "2b"), (ssq, recvsq, "sq"), (sin_a, recvin_a, "in_a"),
              (sin_b, recvin_b, "in_b"), (s1a, recv1a, "1a"), (s1b, recv1b, "1b"))

    gsum = {}
    for stage, recv, tag in stages:
        st, total = _Stage(stage), _sum_slots(recv, "sum_" + tag)
        for n in stage:
            gsum[n] = total[st.off[n]:st.off[n] + st.rows[n]]
    gsum["win"] = jnp.concatenate([gsum["win/0/2"], gsum["win/1/2"]], axis=0)

    vec8, convk = _pack_small(s_ffn1, s_in, s_mix, s_ffn2, s_final, dwa, dwb)
    vec_all, conv_all = _run_comm(_join(_direct_comm(vec8, False), _direct_comm(convk, True)), "xchg_small")
    vec_sum, conv_sum, loss_blk = _sum_small(vec_all, conv_all)
    loss = loss_blk[0, 0]

    g = dict(ffn1_w_gate=gsum["wg1"], ffn1_w_up=gsum["wu1"], ffn1_w_down=gsum["wd1"],
             ffn2_w_gate=gsum["wg2"], ffn2_w_up=gsum["wu2"], ffn2_w_down=gsum["wd2"], w_in=gsum["win"].T,
             a_w_out=gsum["wa"], b_w_out=gsum["wb"], w_o=gsum["wo"],
             ffn1_norm=vec_sum[0:1], mix_norm=vec_sum[1:2], a_ln_g=vec_sum[2:3], a_ln_b=vec_sum[3:4],
             a_dw_b=vec_sum[4:5], ffn2_norm=vec_sum[5:6], final_norm=vec_sum[6:7],
             a_dw_w=conv_sum[0:KA], b_conv_w=conv_sum[32:32 + KB])
    gate_up = ("ffn1_w_gate", "ffn1_w_up", "ffn2_w_gate", "ffn2_w_up")

    upd = {}

    def run(group, name, as2d=lambda a: a[0], back=lambda a, n: a.reshape(w[n].shape)):
        res, _ = _adam([g[n] for n in group], [as2d(w[n]) for n in group], [as2d(m[n]) for n in group],
                       [as2d(v[n]) for n in group], name)
        for n, r in zip(group, res):
            upd[n] = tuple(back(a, n) for a in r)

    run(gate_up, "adam_gate_up", as2d=lambda a: a[0].T, back=lambda a, n: a.T[None])
    for n in gate_up:
        g[n] = g[n].T
    run(("ffn1_w_down", "ffn2_w_down"), "adam_down")
    run(("w_in",), "adam_in")
    run(("a_w_out", "b_w_out", "w_o"), "adam_square")
    run(("a_dw_w",), "adam_dw")
    run(("b_conv_w",), "adam_conv")
    vecs = ("ffn1_norm", "mix_norm", "a_dw_b", "a_ln_g", "a_ln_b", "ffn2_norm", "final_norm")
    run(vecs, "adam_vec", as2d=lambda a: a.reshape(1, D))

    grads = [g[n].reshape(w[n].shape) for n in names]
    return (loss, dx.reshape(x.shape), *grads, *[upd[n][0] for n in names], *[upd[n][1] for n in names],
            *[upd[n][2] for n in names])
```

```python
import jax
import jax.numpy as jnp
from jax import lax
from jax.experimental import pallas as pl
from jax.experimental.pallas import tpu as pltpu

T = 4096
D = 1024
F = 2816
NG = 7
NDEV = 8
NCHIP = 4
KA, KB = 31, 3
EPS = 1e-6
ADAM_LR, ADAM_B1, ADAM_B2, ADAM_EPS, ADAM_WD, ADAM_STEP = 0.001, 0.9, 0.999, 1e-08, 0.01, 10

TM = 512
FC = 256
TB = 1024
NB = 256
HB = NB // 2
CW = 256
CHB = 64
LANE = 128
TK = 1024
VMEM_LIMIT = 56 * 1024 * 1024

BF = jnp.bfloat16
F32 = jnp.float32
MESH = pl.DeviceIdType.MESH
ANY = pl.BlockSpec(memory_space=pl.ANY)

ORDER = ("wg1", "wu1", "wd1", "wg2", "wu2", "wd2", "win", "wa", "wb", "wo")


class _Layout:
    def __init__(self):
        fs, dis, ds = F // NDEV, NG * D // NDEV, D // NDEV
        self.rows = dict(wg1=fs, wu1=fs, wd1=fs, wg2=fs, wu2=fs, wd2=fs, win=dis, wa=ds, wb=ds, wo=ds)
        self.fl, off = {}, 0
        for n in ORDER:
            self.fl[n] = off
            off += self.rows[n]
        self.RT = off


class _Stage:
    def __init__(self, names):
        lay = _Layout()
        self.names = names
        self.rows, self.full, self.sub, self.fl = {}, {}, {}, {}
        for n in names:
            base, i, k = (n.split("/") + ["0", "1"])[:3]
            self.full[n] = lay.rows[base]
            self.rows[n] = lay.rows[base] // int(k)
            self.sub[n] = int(i) * self.rows[n]
            self.fl[n] = lay.fl[base] + self.sub[n]
        self.off, self.wc, o, w = {}, {}, 0, 0
        for n in names:
            self.off[n], self.wc[n] = o, w
            o += self.rows[n]
            w += NDEV * self.rows[n]
        self.R, self.W = o, w

    def grad_row(self, n, first, dev_lin):
        return first + dev_lin * self.full[n] + self.sub[n]


def _nt(a, b):
    return lax.dot_general(a, b, (((1,), (1,)), ((), ())), preferred_element_type=F32)


def _nn(a, b):
    return lax.dot_general(a, b, (((1,), (0,)), ((), ())), preferred_element_type=F32)


def _tn(a, b):
    return lax.dot_general(a, b, (((0,), (0,)), ((), ())), preferred_element_type=F32)


def _sig(x):
    return 1.0 / (1.0 + jnp.exp(-x))


def _position():
    return lax.axis_index("x"), lax.axis_index("y"), lax.axis_index("c")


def _peer(pos, j):
    x, y, c = pos
    return (1 - x if j & 4 else x, 1 - y if j & 2 else y, 1 - c if j & 1 else c)


def _lin(pos):
    return 4 * pos[0] + 2 * pos[1] + pos[2]


def _chip(pos):
    return 2 * pos[0] + pos[1]


class _Comm:
    def __init__(self, inputs, out_shapes, scratch, start, finish):
        self.inputs, self.out_shapes, self.scratch, self.start, self.finish = inputs, out_shapes, scratch, start, finish


def _call(body, *, name, grid, args, in_specs, out_shape, out_specs, scratch_shapes=(), comm=None,
          num_scalar_prefetch=0):
    in_specs, out_shape, out_specs, scratch_shapes = list(in_specs), list(out_shape), list(out_specs), list(scratch_shapes)
    n_in, n_out, n_scr = len(in_specs), len(out_shape), len(scratch_shapes)
    sp = num_scalar_prefetch
    if comm is None:
        kernel_fn = lambda *refs: body(*refs)
        c_in = c_out = c_scr = 0
    else:
        c_in, c_out, c_scr = len(comm.inputs), len(comm.out_shapes), len(comm.scratch)

        def kernel_fn(*refs):
            pre, refs = refs[:sp], refs[sp:]
            ins, cins = refs[:n_in], refs[n_in:n_in + c_in]
            o0 = n_in + c_in
            outs, couts = refs[o0:o0 + n_out], refs[o0 + n_out:o0 + n_out + c_out]
            s0 = o0 + n_out + c_out
            scr, cscr = refs[s0:s0 + n_scr], refs[s0 + n_scr:]
            first = pl.program_id(0) == 0
            last = pl.program_id(0) == grid[0] - 1
            for a in range(1, len(grid)):
                first = first & (pl.program_id(a) == 0)
                last = last & (pl.program_id(a) == grid[a] - 1)

            @pl.when(first)
            def _():
                comm.start(cins, couts, cscr)

            body(*pre, *ins, *outs, *scr)

            @pl.when(last)
            def _():
                comm.finish(cins, couts, cscr)

        args = list(args) + list(comm.inputs)
        in_specs += [ANY] * c_in
        out_shape += list(comm.out_shapes)
        out_specs += [ANY] * c_out
        scratch_shapes += list(comm.scratch)
    params = pltpu.CompilerParams(dimension_semantics=("arbitrary",) * len(grid), vmem_limit_bytes=VMEM_LIMIT)
    if sp:
        grid_spec = pltpu.PrefetchScalarGridSpec(num_scalar_prefetch=sp, grid=grid, in_specs=in_specs,
                                                 out_specs=out_specs, scratch_shapes=scratch_shapes)
        return pl.pallas_call(kernel_fn, name=name, grid_spec=grid_spec, out_shape=out_shape,
                              compiler_params=params)(*args)
    return pl.pallas_call(kernel_fn, name=name, grid=grid, in_specs=in_specs, out_shape=out_shape, out_specs=out_specs,
                          scratch_shapes=scratch_shapes, compiler_params=params)(*args)


def _join(a, b):
    na = (len(a.inputs), len(a.out_shapes), len(a.scratch))

    def split(refs):
        return ([r[:n] for r, n in zip(refs, na)], [r[n:] for r, n in zip(refs, na)])

    def start(*refs):
        ra, rb = split(refs)
        a.start(*ra)
        b.start(*rb)

    def finish(*refs):
        ra, rb = split(refs)
        a.finish(*ra)
        b.finish(*rb)

    return _Comm(list(a.inputs) + list(b.inputs), list(a.out_shapes) + list(b.out_shapes),
                 list(a.scratch) + list(b.scratch), start, finish)


def _run_comm(comm, name):
    def body(*refs):
        c_in, c_out = len(comm.inputs), len(comm.out_shapes)
        comm.start(refs[:c_in], refs[c_in:c_in + c_out], refs[c_in + c_out:])
        comm.finish(refs[:c_in], refs[c_in:c_in + c_out], refs[c_in + c_out:])

    return pl.pallas_call(
        body, name=name, out_shape=list(comm.out_shapes), in_specs=[ANY] * len(comm.inputs),
        out_specs=[ANY] * len(comm.out_shapes), scratch_shapes=list(comm.scratch))(*comm.inputs)


def _ag_comm(names, flat):
    st = _Stage(names)

    def ring(me):
        x, y, c = me
        diagonal = x == y
        up = (jnp.where(diagonal, x, 1 - x), jnp.where(diagonal, 1 - y, y), c)
        down = (jnp.where(diagonal, 1 - x, x), jnp.where(diagonal, y, 1 - y), c)
        low = c == 0
        passed = tuple(jnp.where(low, d, u) for d, u in zip(down, up))
        target = tuple(jnp.where(low, u, d) for d, u in zip(down, up))
        return up, down, (1 - x, 1 - y, c), passed, target

    def parts(refs):
        (flat_ref,), (out_ref,), (send_sems, recv_sems, local_sem) = refs
        me = _position()

        def region(name, dev):
            r = st.rows[name]
            return out_ref.at[pl.ds(st.wc[name] + _lin(dev) * r, r), :]

        def own(name):
            return flat_ref.at[pl.ds(st.fl[name], st.rows[name]), :]

        def copies(k, dev, to, from_flat):
            return [pltpu.make_async_remote_copy(
                src_ref=own(n) if from_flat else region(n, dev), dst_ref=region(n, dev), send_sem=send_sems.at[k],
                recv_sem=recv_sems.at[k], device_id=to, device_id_type=MESH) for n in names]

        def whole(k):
            return pltpu.make_async_remote_copy(
                src_ref=flat_ref.at[pl.ds(0, st.R), :], dst_ref=out_ref.at[pl.ds(0, st.R), :],
                send_sem=send_sems.at[k], recv_sem=recv_sems.at[k], device_id=me, device_id_type=MESH)

        return me, region, own, copies, whole, flat_ref, out_ref, local_sem

    def start(*refs):
        me, region, own, copies, _, _, _, local_sem = parts(refs)
        for n in names:
            pltpu.make_async_copy(own(n), region(n, me), local_sem).start()
        up, down, _, _, _ = ring(me)
        for k, to in ((1, up), (2, down), (0, _peer(me, 1))):
            for cp in copies(k, me, to, True):
                cp.start()

    def finish(*refs):
        me, _, _, copies, whole, flat_ref, out_ref, local_sem = parts(refs)
        up, down, across, passed, target = ring(me)
        sib = _peer(me, 1)
        whole(1).wait_recv()
        whole(2).wait_recv()
        for k, dev, to in ((3, passed, target), (4, down, sib), (5, up, sib)):
            for cp in copies(k, dev, to, False):
                cp.start()
        whole(3).wait_recv()
        for cp in copies(6, across, sib, False):
            cp.start()
        whole(0).wait_recv()
        for j in range(3):
            whole(4 + j).wait_recv()
        for k in range(7):
            whole(k).wait_send()
        pltpu.make_async_copy(flat_ref.at[pl.ds(0, st.R), :], out_ref.at[pl.ds(0, st.R), :], local_sem).wait()

    return _Comm([flat], [jax.ShapeDtypeStruct((st.W, D), BF)],
                 [pltpu.SemaphoreType.DMA((7,)), pltpu.SemaphoreType.DMA((7,)), pltpu.SemaphoreType.DMA],
                 start, finish)


def _rs_pair_comm(names, src):
    st = _Stage(names)
    arrays = []
    for n in names:
        if not any(src[n][0] is a for a in arrays):
            arrays.append(src[n][0])
    idx = {n: [i for i, a in enumerate(arrays) if a is src[n][0]][0] for n in names}

    def slot_wait(refs):
        recv = refs[1][0]
        send_sem, recv_sem = refs[2]
        return pltpu.make_async_remote_copy(src_ref=recv, dst_ref=recv, send_sem=send_sem, recv_sem=recv_sem,
                                            device_id=_position(), device_id_type=MESH)

    def start(*refs):
        ins, (recv,), (send_sem, recv_sem) = refs
        me = _position()
        sib = _peer(me, 1)
        for q in range(NCHIP):
            dev = (q // 2, q % 2, sib[2])
            for n in names:
                r = st.rows[n]
                pltpu.make_async_remote_copy(
                    src_ref=ins[idx[n]].at[pl.ds(st.grad_row(n, src[n][1], _lin(dev)), r), :],
                    dst_ref=recv.at[q, pl.ds(st.off[n], r), :], send_sem=send_sem, recv_sem=recv_sem,
                    device_id=sib, device_id_type=MESH).start()

    def finish(*refs):
        w = slot_wait(refs)
        w.wait_recv()
        w.wait_send()

    return _Comm(arrays, [jax.ShapeDtypeStruct((NCHIP, st.R, D), BF)],
                 [pltpu.SemaphoreType.DMA, pltpu.SemaphoreType.DMA], start, finish)


def _pair_add(names, src, recv, name):
    st = _Stage(names)
    c_arr = jnp.reshape(lax.axis_index("c"), (1,)).astype(jnp.int32)

    def body(c_ref, *refs):
        r_ref, o_ref = refs[len(names)], refs[len(names) + 1]
        for a_ref, n in zip(refs, names):
            rows = slice(st.off[n], st.off[n] + st.rows[n])
            o_ref[rows, :] = (a_ref[...].astype(F32) + r_ref[rows, :].astype(F32)).astype(BF)

    def shard_spec(n):
        r = st.rows[n]
        base, step = st.grad_row(n, src[n][1], 0) // r, st.full[n] // r
        return pl.BlockSpec((r, D), lambda q, c_ref: (base + step * (2 * q + c_ref[0]), 0))

    slot = pl.BlockSpec((None, st.R, D), lambda q, c_ref: (q, 0, 0))
    return _call(body, name=name, grid=(NCHIP,), args=[c_arr] + [src[n][0] for n in names] + [recv],
                 in_specs=[shard_spec(n) for n in names] + [slot],
                 out_shape=[jax.ShapeDtypeStruct((NCHIP, st.R, D), BF)], out_specs=[slot], num_scalar_prefetch=1)[0]


def _rs_chip_comm(part):
    def copies(refs):
        (p_ref,), (recv,), (send_sems, recv_sems, local_sem) = refs
        me = _position()
        mine = pltpu.make_async_copy(p_ref.at[_chip(me)], recv.at[_chip(me)], local_sem)
        out = []
        for j, bits in enumerate((4, 2, 6)):
            to = _peer(me, bits)
            out.append(pltpu.make_async_remote_copy(
                src_ref=p_ref.at[_chip(to)], dst_ref=recv.at[_chip(me)], send_sem=send_sems.at[j],
                recv_sem=recv_sems.at[j], device_id=to, device_id_type=MESH))
        return mine, out

    def start(*refs):
        mine, out = copies(refs)
        mine.start()
        for cp in out:
            cp.start()

    def finish(*refs):
        mine, out = copies(refs)
        for cp in out:
            cp.wait_recv()
        for cp in out:
            cp.wait_send()
        mine.wait()

    return _Comm([part], [jax.ShapeDtypeStruct(part.shape, BF)],
                 [pltpu.SemaphoreType.DMA((3,)), pltpu.SemaphoreType.DMA((3,)), pltpu.SemaphoreType.DMA],
                 start, finish)


def _direct_comm(x, scatter):
    def copies(refs):
        (x_ref,), (out_ref,), (send_sems, recv_sems, local_sem) = refs
        me = _position()

        def piece(dev):
            return x_ref.at[_lin(dev)] if scatter else x_ref

        mine = pltpu.make_async_copy(piece(me), out_ref.at[_lin(me)], local_sem)
        return mine, [pltpu.make_async_remote_copy(
            src_ref=piece(_peer(me, j)), dst_ref=out_ref.at[_lin(me)], send_sem=send_sems.at[j - 1],
            recv_sem=recv_sems.at[j - 1], device_id=_peer(me, j), device_id_type=MESH) for j in range(1, NDEV)]

    def start(*refs):
        mine, cps = copies(refs)
        mine.start()
        for cp in cps:
            cp.start()

    def finish(*refs):
        mine, cps = copies(refs)
        for cp in cps:
            cp.wait_recv()
        for cp in cps:
            cp.wait_send()
        mine.wait()

    shape = x.shape if scatter else (NDEV,) + x.shape
    return _Comm([x], [jax.ShapeDtypeStruct(shape, x.dtype)],
                 [pltpu.SemaphoreType.DMA((7,)), pltpu.SemaphoreType.DMA((7,)), pltpu.SemaphoreType.DMA],
                 start, finish)


def _pack_weights(shards):
    lay = _Layout()

    def body(*refs):
        o_ref = refs[-1]
        for ref, n in zip(refs, ORDER):
            x = ref[...].T if n == "win" else ref[...]
            o_ref[lay.fl[n]:lay.fl[n] + lay.rows[n], :] = x.astype(BF)

    return pl.pallas_call(
        body, name="pack_weights", out_shape=jax.ShapeDtypeStruct((lay.RT, D), BF),
        compiler_params=pltpu.CompilerParams(vmem_limit_bytes=VMEM_LIMIT))(*[shards[n] for n in ORDER])


def _load_ffn_weights(srcs, offs, scratch, sem):
    @pl.when(pl.program_id(0) == 0)
    def _():
        cps = [pltpu.make_async_copy(s.at[pl.ds(off, dst.shape[0]), :], dst, sem.at[i])
               for i, (s, off, dst) in enumerate(zip(srcs, offs, scratch))]
        for cp in cps:
            cp.start()
        for cp in cps:
            cp.wait()


def _final_loss_tile(xf, g, tgt, s_ref):
    r = lax.rsqrt(jnp.mean(xf * xf, axis=-1, keepdims=True) + EPS)
    xr = xf * r
    e = xr * g - tgt
    s_ref[1:2, :] += jnp.sum(e * e, axis=0, keepdims=True) * (0.5 / D)
    dy = e * (1.0 / D)
    s_ref[0:1, :] += jnp.sum(dy * xr, axis=0, keepdims=True)
    gdy = dy * g
    return r * gdy - xr * (r * jnp.mean(gdy * xr, axis=-1, keepdims=True))


def _ffn_fwd(x, g, wbufs, offs, name, comm=None, final=None):
    nf = F // FC

    def body(x_ref, g_ref, b0, b1, b2, *rest):
        if final is None:
            h_ref, n_ref, gg_ref, uu_ref, wg_s, wu_s, wd_s, sem = rest
        else:
            gf_ref, t_ref, dh_ref, s_ref, n_ref, gg_ref, uu_ref, wg_s, wu_s, wd_s, sem = rest

            @pl.when(pl.program_id(0) == 0)
            def _():
                s_ref[...] = jnp.zeros_like(s_ref)

        _load_ffn_weights((b0, b1, b2), offs, (wg_s, wu_s, wd_s), sem)
        xf = x_ref[...]
        r = lax.rsqrt(jnp.mean(xf * xf, axis=-1, keepdims=True) + EPS)
        nb = (xf * r * g_ref[...]).astype(BF)
        n_ref[...] = nb
        acc = jnp.zeros((TM, D), F32)
        for c in range(nf):
            sl = slice(c * FC, (c + 1) * FC)
            gb = _nt(nb, wg_s[sl, :]).astype(BF)
            ub = _nt(nb, wu_s[sl, :]).astype(BF)
            gg_ref[:, sl] = gb
            uu_ref[:, sl] = ub
            acc = acc + _nn((gb * _sig(gb)) * ub, wd_s[sl, :])
        h = xf + 0.5 * acc
        if final is None:
            h_ref[...] = h
        else:
            dh_ref[...] = _final_loss_tile(h, gf_ref[...], t_ref[...], s_ref)

    row = lambda i: (i, 0)
    vec = pl.BlockSpec((1, D), lambda i: (0, 0))
    tile = pl.BlockSpec((TM, D), row)
    saved_shapes = [jax.ShapeDtypeStruct((T, D), BF), jax.ShapeDtypeStruct((T, F), BF), jax.ShapeDtypeStruct((T, F), BF)]
    saved_specs = [tile, pl.BlockSpec((TM, F), row), pl.BlockSpec((TM, F), row)]
    if final is None:
        extra_args, extra_specs = [], []
        head_shapes, head_specs = [jax.ShapeDtypeStruct((T, D), F32)], [tile]
    else:
        extra_args, extra_specs = list(final), [vec, tile]
        head_shapes = [jax.ShapeDtypeStruct((T, D), F32), jax.ShapeDtypeStruct((8, D), F32)]
        head_specs = [tile, pl.BlockSpec((8, D), lambda i: (0, 0))]
    return _call(
        body, name=name, grid=(T // TM,), args=[x, g, *wbufs, *extra_args], comm=comm,
        in_specs=[tile, vec, ANY, ANY, ANY] + extra_specs,
        out_shape=head_shapes + saved_shapes, out_specs=head_specs + saved_specs,
        scratch_shapes=[pltpu.VMEM((F, D), BF)] * 3 + [pltpu.SemaphoreType.DMA((3,))])


def _mix_in(h1, gm, win, comm=None):
    def body(h_ref, g_ref, w_any, u_ref, z_ref, w_s, sem):
        _load_ffn_weights((w_any,), (0,), (w_s,), sem)
        xf = h_ref[...]
        r = lax.rsqrt(jnp.mean(xf * xf, axis=-1, keepdims=True) + EPS)
        ub = (xf * r * g_ref[...]).astype(BF)
        u_ref[...] = ub
        for j in range(NG):
            z_ref[j] = _nt(ub, w_s[j * D:(j + 1) * D, :]).astype(BF)

    row = lambda i: (i, 0)
    return _call(
        body, name="mix_in", grid=(T // TM,), args=[h1, gm, win], comm=comm,
        in_specs=[pl.BlockSpec((TM, D), row), pl.BlockSpec((1, D), lambda i: (0, 0)), ANY],
        out_shape=[jax.ShapeDtypeStruct((T, D), BF), jax.ShapeDtypeStruct((NG, T, D), BF)],
        out_specs=[pl.BlockSpec((TM, D), row), pl.BlockSpec((NG, TM, D), lambda i: (0, i, 0))],
        scratch_shapes=[pltpu.VMEM((NG * D, D), BF), pltpu.SemaphoreType.DMA((1,))])


def _shift_up(w, b):
    return w if b == 0 else pltpu.roll(w, w.shape[0] - b, 0)


def _fold8(p):
    red = p[0:8, :]
    for i in range(1, p.shape[0] // 8):
        red = red + p[8 * i:8 * i + 8, :]
    return red


def _dft_constants():
    import numpy as np
    nh = NB // 2
    f, n = np.arange(nh)[:, None], np.arange(NB)[None, :]
    ang = 2.0 * np.pi / NB * f * n
    fc = np.cos(ang)
    fs = np.where(f == 0, (-1.0) ** n, np.sin(ang))
    scale = np.where(f == 0, 1.0, 2.0) / NB
    ic = (scale * np.cos(ang)).T
    isn = np.where(f == 0, (-1.0) ** n / NB, scale * np.sin(ang)).T
    d = (KA - 1 - np.arange(32))[None, :]
    valid = (np.arange(32) < KA)[None, :]
    angk = 2.0 * np.pi / NB * f * d
    kc = np.where(valid, np.cos(angk), 0.0)
    ks = np.where(valid, np.sin(angk), 0.0)
    k2 = np.where(valid, np.where(f == 0, (-1.0) ** d, np.cos(angk)), 0.0)
    rtc = np.where(valid, scale * np.cos(angk), 0.0).T
    rts = np.where(valid, np.where(f == 0, (-1.0) ** d / NB, scale * np.sin(angk)), 0.0).T

    def bf(a):
        return jnp.asarray(a, F32).astype(BF)

    def split(a):
        hi = bf(a)
        return hi, (jnp.asarray(a, F32) - hi.astype(F32)).astype(BF)

    return dict(fc=bf(fc), fs=bf(fs), ic_hi=bf(ic[HB:]), is_hi=bf(isn[HB:]), ic_lo=bf(ic[:HB]), is_lo=bf(isn[:HB]),
                kc=split(kc), ks=split(ks), k2=split(k2), rtc=split(rtc), rts=split(rts))


def _dot3(m_hi, m_lo, x):
    x_hi = x.astype(BF)
    x_lo = (x - x_hi.astype(F32)).astype(BF)
    return _nn(m_hi, x_hi) + _nn(m_hi, x_lo) + _nn(m_lo, x_hi)


def _whole(a):
    return pl.BlockSpec(a.shape, lambda c, t: (0,) * a.ndim)


def _filter_spectrum(cw_ref, tabs, hc, hs, h2):
    w32 = cw_ref[0:32, :]
    for (hi, lo), dst in zip(tabs, (hc, hs, h2)):
        dst[...] = _dot3(hi[...], lo[...], w32)


def _conv_fwd_dft(z, cw, bias, dft, comm=None):
    nt = T // TB
    hb = TB // HB

    def body(z_ref, zh_ref, cw_ref, b_ref, fc_ref, fs_ref, ic_ref, is_ref, kch, kcl, ksh, ksl, k2h, k2l,
             a1_ref, q_ref, aext, ppad, hc, hs, h2):
        first = pl.program_id(1) == 0
        f = lambda ref, j: ref[j].astype(F32)

        @pl.when(first)
        def _():
            _filter_spectrum(cw_ref, ((kch, kcl), (ksh, ksl), (k2h, k2l)), hc, hs, h2)

        aext[0:HB, :] = jnp.where(first, 0.0, f(zh_ref, 0) * _sig(f(zh_ref, 1))).astype(BF)
        aext[HB:, :] = (f(z_ref, 0) * _sig(f(z_ref, 1))).astype(BF)
        ppad[0:8, :] = jnp.where(first, 0.0, f(zh_ref, 3)[HB - 8:HB, :] * f(zh_ref, 4)[HB - 8:HB, :])
        ppad[8:, :] = f(z_ref, 3) * f(z_ref, 4)
        bias_row = b_ref[...]

        for j in range(TB // HB):
            xs = aext[j * HB:j * HB + NB, :]
            xa, xb = _nn(fc_ref[...], xs), _nn(fs_ref[...], xs)
            yc = (hc[...] * xa - hs[...] * xb).astype(BF)
            ys = (h2[...] * xb + hs[...] * xa).astype(BF)
            y = _nn(ic_ref[...], yc) + _nn(is_ref[...], ys)
            a1_ref[j * HB:(j + 1) * HB, :] = (y + bias_row).astype(BF)

        def chunk(r, carry):
            base = pl.multiple_of(r * CHB, CHB)
            pw = ppad[pl.ds(base, CHB + 8), :]
            v = (cw_ref[pl.ds(32, 1), :] * _shift_up(pw, 6)[0:CHB, :]
                 + cw_ref[pl.ds(33, 1), :] * _shift_up(pw, 7)[0:CHB, :]
                 + cw_ref[pl.ds(34, 1), :] * pw[8:8 + CHB, :])
            q_ref[pl.ds(base, CHB), :] = (z_ref[2, pl.ds(base, CHB), :].astype(F32) * v).astype(BF)
            return carry

        lax.fori_loop(0, TB // CHB, chunk, 0)

    blk = pl.BlockSpec((TB, CW), lambda c, t: (t, c))
    tabs = [dft["fc"], dft["fs"], dft["ic_hi"], dft["is_hi"], *dft["kc"], *dft["ks"], *dft["k2"]]
    return _call(
        body, name="conv_fwd", grid=(D // CW, nt), comm=comm, args=[z, z, cw, bias] + tabs,
        in_specs=[pl.BlockSpec((5, TB, CW), lambda c, t: (0, t, c)),
                  pl.BlockSpec((5, HB, CW), lambda c, t: (0, jnp.maximum(t * hb - 1, 0), c)),
                  pl.BlockSpec((40, CW), lambda c, t: (0, c)), pl.BlockSpec((1, CW), lambda c, t: (0, c))]
                 + [_whole(a) for a in tabs],
        out_shape=[jax.ShapeDtypeStruct((T, D), BF), jax.ShapeDtypeStruct((T, D), BF)], out_specs=[blk, blk],
        scratch_shapes=[pltpu.VMEM((TB + HB, CW), BF), pltpu.VMEM((TB + 8, CW), F32)]
                       + [pltpu.VMEM((NB // 2, CW), F32)] * 3)


def _conv_bwd_dft(z, da1, dq, dzg, cw, dft, comm=None):
    nt = T // TB
    hb = TB // HB
    last_h = T // HB - 1

    def body(z_ref, zp_ref, zn_ref, da1_ref, da1n_ref, dq_ref, dqn_ref, dzg_ref, cw_ref,
             fc_ref, fs_ref, ic_ref, is_ref, kch, kcl, ksh, ksl, k2h, k2l, rch, rcl, rsh, rsl,
             dz_ref, dwa_ref, dwb_ref, aext, dyext, ppad, dvpad, hc, hs, h2, rc, rs, nyq, acc_b):
        t = pl.program_id(1)
        first, last = t == 0, t == nt - 1
        f = lambda ref, j: ref[j].astype(F32)

        @pl.when(first)
        def _():
            _filter_spectrum(cw_ref, ((kch, kcl), (ksh, ksl), (k2h, k2l)), hc, hs, h2)
            rc[...] = jnp.zeros_like(rc)
            rs[...] = jnp.zeros_like(rs)
            nyq[...] = jnp.zeros_like(nyq)
            acc_b[...] = jnp.zeros_like(acc_b)

        aext[0:HB, :] = jnp.where(first, 0.0, f(zp_ref, 0) * _sig(f(zp_ref, 1))).astype(BF)
        aext[HB:, :] = (f(z_ref, 0) * _sig(f(z_ref, 1))).astype(BF)
        dyext[0:TB, :] = da1_ref[...]
        dyext[TB:, :] = jnp.where(last, 0.0, da1n_ref[...].astype(F32)).astype(BF)
        ppad[0:8, :] = jnp.where(first, 0.0, f(zp_ref, 3)[HB - 8:HB, :] * f(zp_ref, 4)[HB - 8:HB, :])
        ppad[8:, :] = f(z_ref, 3) * f(z_ref, 4)
        dvpad[0:TB, :] = dq_ref[...].astype(F32) * f(z_ref, 2)
        dvpad[TB:, :] = jnp.where(last, 0.0, dqn_ref[...].astype(F32)[0:8, :] * f(zn_ref, 2)[0:8, :])

        for j in range(TB // HB):
            rows = slice(j * HB, (j + 1) * HB)
            dys = dyext[j * HB:j * HB + NB, :]
            da, db = _nn(fc_ref[...], dys), _nn(fs_ref[...], dys)
            gc = (hc[...] * da + hs[...] * db).astype(BF)
            gs = (h2[...] * db - hs[...] * da).astype(BF)
            da0 = _nn(ic_ref[...], gc) + _nn(is_ref[...], gs)
            z0, z1 = z_ref[0, rows, :].astype(F32), z_ref[1, rows, :].astype(F32)
            s1 = _sig(z1)
            dz_ref[0, rows, :] = (da0 * s1).astype(BF)
            dz_ref[1, rows, :] = (da0 * z0 * (s1 * (1.0 - s1))).astype(BF)
            xs = aext[j * HB:j * HB + NB, :]
            xa, xb = _nn(fc_ref[...], xs), _nn(fs_ref[...], xs)
            dyb = dyext[rows, :]
            pa, pb = _nn(fc_ref[:, HB:NB], dyb), _nn(fs_ref[:, HB:NB], dyb)
            rc[...] += pa * xa + pb * xb
            rs[...] += pb * xa - pa * xb
            nyq[...] += pb[0:8, :] * xb[0:8, :]

        def chunk(r, carry):
            base = pl.multiple_of(r * CHB, CHB)
            rows = pl.ds(base, CHB)
            pw = ppad[pl.ds(base, CHB + 8), :]
            p6 = _shift_up(pw, 6)[0:CHB, :]
            p7 = _shift_up(pw, 7)[0:CHB, :]
            p8 = pw[8:8 + CHB, :]
            wb0, wb1, wb2 = cw_ref[pl.ds(32, 1), :], cw_ref[pl.ds(33, 1), :], cw_ref[pl.ds(34, 1), :]
            v = wb0 * p6 + wb1 * p7 + wb2 * p8
            dz_ref[2, rows, :] = (dq_ref[rows, :].astype(F32) * v).astype(BF)
            dvw = dvpad[pl.ds(base, CHB + 8), :]
            dvc = dvw[0:CHB, :]
            dp = wb2 * dvc + wb1 * _shift_up(dvw, 1)[0:CHB, :] + wb0 * _shift_up(dvw, 2)[0:CHB, :]
            dz_ref[3, rows, :] = (dp * z_ref[4, rows, :].astype(F32)).astype(BF)
            dz_ref[4, rows, :] = (dp * z_ref[3, rows, :].astype(F32)).astype(BF)
            acc_b[0:8, :] += _fold8(dvc * p6)
            acc_b[8:16, :] += _fold8(dvc * p7)
            acc_b[16:24, :] += _fold8(dvc * p8)
            dz_ref[5, rows, :] = dzg_ref[0, rows, :]
            dz_ref[6, rows, :] = dzg_ref[1, rows, :]
            return carry

        lax.fori_loop(0, TB // CHB, chunk, 0)

        @pl.when(last)
        def _():
            row0 = lax.broadcasted_iota(jnp.int32, (NB // 2, CW), 0) == 0
            ny = jnp.broadcast_to(nyq[0:1, :], (NB // 2, CW))
            rcv = jnp.where(row0, rc[...] - ny, rc[...])
            rsv = jnp.where(row0, ny, rs[...])
            dwa_ref[...] = _dot3(rch[...], rcl[...], rcv) + _dot3(rsh[...], rsl[...], rsv)
            for k in range(KB):
                dwb_ref[k:k + 1, :] = jnp.sum(acc_b[8 * k:8 * k + 8, :], axis=0, keepdims=True)
            dwb_ref[KB:8, :] = jnp.zeros((8 - KB, CW), F32)

    blk = lambda c, t: (t, c)
    nxt = lambda c, t: (jnp.minimum((t + 1) * hb, last_h), c)
    tabs = [dft["fc"], dft["fs"], dft["ic_lo"], dft["is_lo"], *dft["kc"], *dft["ks"], *dft["k2"], *dft["rtc"], *dft["rts"]]
    return _call(
        body, name="conv_bwd", grid=(D // CW, nt), comm=comm, args=[z, z, z, da1, da1, dq, dq, dzg, cw] + tabs,
        in_specs=[pl.BlockSpec((5, TB, CW), lambda c, t: (0, t, c)),
                  pl.BlockSpec((5, HB, CW), lambda c, t: (0, jnp.maximum(t * hb - 1, 0), c)),
                  pl.BlockSpec((5, HB, CW), lambda c, t: (0, jnp.minimum((t + 1) * hb, last_h), c)),
                  pl.BlockSpec((TB, CW), blk), pl.BlockSpec((HB, CW), nxt),
                  pl.BlockSpec((TB, CW), blk), pl.BlockSpec((HB, CW), nxt),
                  pl.BlockSpec((2, TB, CW), lambda c, t: (0, t, c)),
                  pl.BlockSpec((40, CW), lambda c, t: (0, c))]
                 + [_whole(a) for a in tabs],
        out_shape=[jax.ShapeDtypeStruct((NG, T, D), BF), jax.ShapeDtypeStruct((32, D), F32),
                   jax.ShapeDtypeStruct((8, D), F32)],
        out_specs=[pl.BlockSpec((NG, TB, CW), lambda c, t: (0, t, c)),
                   pl.BlockSpec((32, CW), lambda c, t: (0, c)), pl.BlockSpec((8, CW), lambda c, t: (0, c))],
        scratch_shapes=[pltpu.VMEM((TB + HB, CW), BF), pltpu.VMEM((TB + HB, CW), BF),
                        pltpu.VMEM((TB + 8, CW), F32), pltpu.VMEM((TB + 8, CW), F32)]
                       + [pltpu.VMEM((NB // 2, CW), F32)] * 5 + [pltpu.VMEM((8, CW), F32), pltpu.VMEM((24, CW), F32)])


def _layernorm_silu(a1, lng, lnb):
    mu = jnp.mean(a1, axis=-1, keepdims=True)
    xc = a1 - mu
    rs = lax.rsqrt(jnp.mean(xc * xc, axis=-1, keepdims=True) + EPS)
    xh = xc * rs
    a2 = xh * lng + lnb
    sg = _sig(a2)
    return xh, rs, a2, sg


def _square_specs(blocks):
    return [pl.BlockSpec((D, D), lambda i, b=b: (b, 0)) for b in blocks]


def _mix_out(a1, q, z, h1, lng, lnb, wsq, comm=None):
    def body(a1_ref, q_ref, ga_ref, gb_ref, h_ref, lng_ref, lnb_ref, wa_ref, wb_ref, wo_ref, h2_ref, ya_ref, yb_ref):
        _, _, a2, sg = _layernorm_silu(a1_ref[...].astype(F32), lng_ref[...], lnb_ref[...])
        ya = _nn((a2 * sg).astype(BF), wa_ref[...])
        yb = _nn(q_ref[...], wb_ref[...])
        ya_ref[...] = ya.astype(BF)
        yb_ref[...] = yb.astype(BF)
        m = _sig(ga_ref[...].astype(F32)) * ya + _sig(gb_ref[...].astype(F32)) * yb
        h2_ref[...] = h_ref[...] + _nn(m.astype(BF), wo_ref[...])

    row = lambda i: (i, 0)
    vec = pl.BlockSpec((1, D), lambda i: (0, 0))
    return _call(
        body, name="mix_out", grid=(T // TM,), args=[a1, q, z, z, h1, lng, lnb, wsq, wsq, wsq], comm=comm,
        in_specs=[pl.BlockSpec((TM, D), row), pl.BlockSpec((TM, D), row),
                  pl.BlockSpec((None, TM, D), lambda i: (5, i, 0)), pl.BlockSpec((None, TM, D), lambda i: (6, i, 0)),
                  pl.BlockSpec((TM, D), row), vec, vec] + _square_specs((0, 1, 2)),
        out_shape=[jax.ShapeDtypeStruct((T, D), F32), jax.ShapeDtypeStruct((T, D), BF), jax.ShapeDtypeStruct((T, D), BF)],
        out_specs=[pl.BlockSpec((TM, D), row)] * 3)


def _rmsnorm_bwd(xf, g, dn):
    r = lax.rsqrt(jnp.mean(xf * xf, axis=-1, keepdims=True) + EPS)
    xr = xf * r
    gdn = dn * g
    dx = r * gdn - xr * (r * jnp.mean(gdn * xr, axis=-1, keepdims=True))
    return dx, jnp.sum(dn * xr, axis=0, keepdims=True)


def _ffn_bwd_hidden(dh, gg, uu, wbuf, off, name, comm=None):
    nf = F // FC

    def body(dh_ref, gg_ref, uu_ref, b0, dgu_ref, a_ref, wd_s, sem):
        _load_ffn_weights((b0,), (off,), (wd_s,), sem)
        dhb = (0.5 * dh_ref[...]).astype(BF)
        for c in range(nf):
            sl = slice(c * FC, (c + 1) * FC)
            da = _nt(dhb, wd_s[sl, :]).astype(BF)
            gb, ub = gg_ref[:, sl], uu_ref[:, sl]
            sg = _sig(gb)
            silu = gb * sg
            dgu_ref[0, :, sl] = (da * ub) * (sg * (1.0 + gb * (1.0 - sg)))
            dgu_ref[1, :, sl] = da * silu
            a_ref[0, :, sl] = silu * ub

    row = lambda i: (i, 0)
    return _call(
        body, name=name, grid=(T // TM,), args=[dh, gg, uu, wbuf], comm=comm,
        in_specs=[pl.BlockSpec((TM, D), row), pl.BlockSpec((TM, F), row), pl.BlockSpec((TM, F), row), ANY],
        out_shape=[jax.ShapeDtypeStruct((2, T, F), BF), jax.ShapeDtypeStruct((1, T, F), BF)],
        out_specs=[pl.BlockSpec((2, TM, F), lambda i: (0, i, 0)), pl.BlockSpec((1, TM, F), lambda i: (0, i, 0))],
        scratch_shapes=[pltpu.VMEM((F, D), BF), pltpu.SemaphoreType.DMA((1,))])


def _ffn_bwd_input(dgu, dh, x, g, wbufs, offs, name, comm=None):
    def body(dgu_ref, dh_ref, x_ref, g_ref, b0, b1, dx_ref, s_ref, wg_s, wu_s, sem):
        _load_ffn_weights((b0, b1), offs, (wg_s, wu_s), sem)

        @pl.when(pl.program_id(0) == 0)
        def _():
            s_ref[...] = jnp.zeros_like(s_ref)

        dn = _nn(dgu_ref[0], wg_s[...]) + _nn(dgu_ref[1], wu_s[...])
        dxn, dg = _rmsnorm_bwd(x_ref[...], g_ref[...], dn)
        dx_ref[...] = dh_ref[...] + dxn
        s_ref[0:1, :] += dg

    row = lambda i: (i, 0)
    return _call(
        body, name=name, grid=(T // TM,), args=[dgu, dh, x, g, *wbufs], comm=comm,
        in_specs=[pl.BlockSpec((2, TM, F), lambda i: (0, i, 0)), pl.BlockSpec((TM, D), row),
                  pl.BlockSpec((TM, D), row), pl.BlockSpec((1, D), lambda i: (0, 0)), ANY, ANY],
        out_shape=[jax.ShapeDtypeStruct((T, D), F32), jax.ShapeDtypeStruct((8, D), F32)],
        out_specs=[pl.BlockSpec((TM, D), row), pl.BlockSpec((8, D), lambda i: (0, 0))],
        scratch_shapes=[pltpu.VMEM((F, D), BF)] * 2 + [pltpu.SemaphoreType.DMA((2,))])


def _tn_matmul(lhs, rhs, tr, name, comm=None, scale=None):
    ng, _, cdim = lhs.shape
    nc, nk = cdim // tr, T // TK

    def body(l_ref, r_ref, o_ref, acc):
        k = pl.program_id(2)

        @pl.when(k == 0)
        def _():
            acc[...] = jnp.zeros_like(acc)

        r = r_ref[...] if scale is None else scale * r_ref[...]
        acc[...] += _tn(l_ref[...], r.astype(BF))

        @pl.when(k == nk - 1)
        def _():
            o_ref[...] = acc[...].astype(BF)

    return _call(
        body, name=name, grid=(ng, nc, nk), args=[lhs, rhs], comm=comm,
        in_specs=[pl.BlockSpec((None, TK, tr), lambda g, c, k: (g, k, c)),
                  pl.BlockSpec((TK, D), lambda g, c, k: (k, 0))],
        out_shape=[jax.ShapeDtypeStruct((ng * cdim, D), BF)],
        out_specs=[pl.BlockSpec((tr, D), lambda g, c, k: (g * nc + c, 0))],
        scratch_shapes=[pltpu.VMEM((tr, D), F32)])


def _mix_out_bwd(dh2, ya, yb, z, a1, lng, lnb, wsq, comm=None):
    def body(dh_ref, ya_ref, yb_ref, ga_ref, gb_ref, a1_ref, lng_ref, lnb_ref, wa_ref, wb_ref, wo_ref,
             dzg_ref, da1_ref, dq_ref, m_ref, a3_ref, dya_ref, dyb_ref, s_ref):
        @pl.when(pl.program_id(0) == 0)
        def _():
            s_ref[...] = jnp.zeros_like(s_ref)

        dm = _nt(dh_ref[...].astype(BF), wo_ref[...])
        ya, yb = ya_ref[...].astype(F32), yb_ref[...].astype(F32)
        sa, sb = _sig(ga_ref[...].astype(F32)), _sig(gb_ref[...].astype(F32))
        m_ref[0] = (sa * ya + sb * yb).astype(BF)
        dzg_ref[0] = (dm * ya * (sa * (1.0 - sa))).astype(BF)
        dzg_ref[1] = (dm * yb * (sb * (1.0 - sb))).astype(BF)
        dya = (dm * sa).astype(BF)
        dyb = (dm * sb).astype(BF)
        dya_ref[...] = dya
        dyb_ref[...] = dyb
        dq_ref[...] = _nt(dyb, wb_ref[...]).astype(BF)
        da3 = _nt(dya, wa_ref[...])
        lng = lng_ref[...]
        xh, rs, a2, sg = _layernorm_silu(a1_ref[...].astype(F32), lng, lnb_ref[...])
        a3_ref[0] = (a2 * sg).astype(BF)
        da2 = da3 * (sg * (1.0 + a2 * (1.0 - sg)))
        s_ref[0:1, :] += jnp.sum(da2 * xh, axis=0, keepdims=True)
        s_ref[1:2, :] += jnp.sum(da2, axis=0, keepdims=True)
        dxh = da2 * lng
        da1 = rs * (dxh - jnp.mean(dxh, axis=-1, keepdims=True) - xh * jnp.mean(dxh * xh, axis=-1, keepdims=True))
        da1_ref[...] = da1.astype(BF)
        s_ref[2:3, :] += jnp.sum(da1, axis=0, keepdims=True)

    row = lambda i: (i, 0)
    row3 = lambda i: (0, i, 0)
    vec = pl.BlockSpec((1, D), lambda i: (0, 0))
    return _call(
        body, name="mix_out_bwd", grid=(T // TM,), args=[dh2, ya, yb, z, z, a1, lng, lnb, wsq, wsq, wsq], comm=comm,
        in_specs=[pl.BlockSpec((TM, D), row), pl.BlockSpec((TM, D), row), pl.BlockSpec((TM, D), row),
                  pl.BlockSpec((None, TM, D), lambda i: (5, i, 0)), pl.BlockSpec((None, TM, D), lambda i: (6, i, 0)),
                  pl.BlockSpec((TM, D), row), vec, vec] + _square_specs((0, 1, 2)),
        out_shape=[jax.ShapeDtypeStruct((2, T, D), BF), jax.ShapeDtypeStruct((T, D), BF),
                   jax.ShapeDtypeStruct((T, D), BF), jax.ShapeDtypeStruct((1, T, D), BF),
                   jax.ShapeDtypeStruct((1, T, D), BF), jax.ShapeDtypeStruct((T, D), BF),
                   jax.ShapeDtypeStruct((T, D), BF), jax.ShapeDtypeStruct((8, D), F32)],
        out_specs=[pl.BlockSpec((2, TM, D), row3), pl.BlockSpec((TM, D), row), pl.BlockSpec((TM, D), row),
                   pl.BlockSpec((1, TM, D), row3), pl.BlockSpec((1, TM, D), row3), pl.BlockSpec((TM, D), row),
                   pl.BlockSpec((TM, D), row), pl.BlockSpec((8, D), lambda i: (0, 0))])


def _mix_in_bwd(dz, dh2, h1, gm, win, comm=None):
    def body(dz_ref, w_any, dh_ref, h_ref, g_ref, o_ref, s_ref, w_s, sem):
        _load_ffn_weights((w_any,), (0,), (w_s,), sem)

        @pl.when(pl.program_id(0) == 0)
        def _():
            s_ref[...] = jnp.zeros_like(s_ref)

        du = _nn(dz_ref[0], w_s[0:D, :])
        for j in range(1, NG):
            du = du + _nn(dz_ref[j], w_s[j * D:(j + 1) * D, :])
        dx, dg = _rmsnorm_bwd(h_ref[...], g_ref[...], du)
        o_ref[...] = dh_ref[...] + dx
        s_ref[0:1, :] += dg

    row = lambda i: (i, 0)
    return _call(
        body, name="mix_in_bwd", grid=(T // TM,), args=[dz, win, dh2, h1, gm], comm=comm,
        in_specs=[pl.BlockSpec((NG, TM, D), lambda i: (0, i, 0)), ANY,
                  pl.BlockSpec((TM, D), row), pl.BlockSpec((TM, D), row), pl.BlockSpec((1, D), lambda i: (0, 0))],
        out_shape=[jax.ShapeDtypeStruct((T, D), F32), jax.ShapeDtypeStruct((8, D), F32)],
        out_specs=[pl.BlockSpec((TM, D), row), pl.BlockSpec((8, D), lambda i: (0, 0))],
        scratch_shapes=[pltpu.VMEM((NG * D, D), BF), pltpu.SemaphoreType.DMA((1,))])


def _row_tile(n, want, mult):
    for t in range(min(want, n), 0, -1):
        if n % t == 0 and t % mult == 0:
            return t
    return n


def _sum_slots(recv, name):
    ns, rows, cols = recv.shape
    tr = _row_tile(rows, 1024, 16)

    def body(r_ref, o_ref):
        s = r_ref[0].astype(F32)
        for k in range(1, ns):
            s = s + r_ref[k].astype(F32)
        o_ref[...] = s

    return _call(
        body, name=name, grid=(rows // tr,), args=[recv],
        in_specs=[pl.BlockSpec((ns, tr, cols), lambda i: (0, i, 0))],
        out_shape=[jax.ShapeDtypeStruct((rows, cols), F32)],
        out_specs=[pl.BlockSpec((tr, cols), lambda i: (i, 0))])[0]


def _pack_small(s_ffn1, s_in, s_mix, s_ffn2, s_final, dwa, dwb):
    def body(f1, mi, mo, f2, fl, wa_ref, wb_ref, v_ref, k_ref):
        for dst, (ref, row) in enumerate(((f1, 0), (mi, 0), (mo, 0), (mo, 1), (mo, 2), (f2, 0), (fl, 0), (fl, 1))):
            v_ref[dst:dst + 1, :] = ref[row:row + 1, :]
        for k in range(NDEV):
            k_ref[k, 0:32, :] = wa_ref[:, k * LANE:(k + 1) * LANE]
            k_ref[k, 32:40, :] = wb_ref[:, k * LANE:(k + 1) * LANE]

    return pl.pallas_call(
        body, name="pack_small",
        out_shape=(jax.ShapeDtypeStruct((8, D), F32), jax.ShapeDtypeStruct((NDEV, 40, LANE), F32)),
    )(s_ffn1, s_in, s_mix, s_ffn2, s_final, dwa, dwb)


def _sum_small(vecs, convs):
    def body(v_ref, k_ref, vs_ref, ks_ref, l_ref):
        s, c = v_ref[0], k_ref[0]
        for k in range(1, NDEV):
            s = s + v_ref[k]
            c = c + k_ref[k]
        vs_ref[...] = s
        ks_ref[...] = c
        l_ref[...] = jnp.broadcast_to(jnp.sum(s[7:8, :], axis=-1, keepdims=True), (8, LANE))

    return pl.pallas_call(
        body, name="sum_small",
        out_shape=(jax.ShapeDtypeStruct((8, D), F32), jax.ShapeDtypeStruct((40, LANE), F32),
                   jax.ShapeDtypeStruct((8, LANE), F32)),
    )(vecs, convs)


def _adam(gs, ws, ms, vs, name, comm=None):
    n = len(gs)
    rows, cols = ws[0].shape
    tr = _row_tile(rows, 256, 8)
    c1 = 1.0 - ADAM_B1 ** ADAM_STEP
    c2 = 1.0 - ADAM_B2 ** ADAM_STEP

    def body(*refs):
        for i in range(n):
            g, w, m, v = (refs[4 * i + k][...] for k in range(4))
            d_ref, m_ref, v_ref = refs[4 * n + 3 * i: 4 * n + 3 * i + 3]
            m2 = ADAM_B1 * m + (1.0 - ADAM_B1) * g
            v2 = ADAM_B2 * v + (1.0 - ADAM_B2) * (g * g)
            d_ref[...] = -ADAM_LR * ((m2 / c1) / (jnp.sqrt(v2 / c2) + ADAM_EPS) + ADAM_WD * w)
            m_ref[...] = m2
            v_ref[...] = v2

    spec = pl.BlockSpec((tr, cols), lambda i: (i, 0))
    args = []
    for i in range(n):
        args += [gs[i], ws[i], ms[i], vs[i]]
    outs = _call(body, name=name, grid=(rows // tr,), args=args, comm=comm, in_specs=[spec] * (4 * n),
                 out_shape=[jax.ShapeDtypeStruct((rows, cols), F32)] * (3 * n), out_specs=[spec] * (3 * n))
    return [tuple(outs[3 * i: 3 * i + 3]) for i in range(n)], outs[3 * n:]


def kernel(x, ffn1_norm, ffn1_w_gate, ffn1_w_up, ffn1_w_down, mix_norm, w_in, a_dw_w, a_dw_b, a_ln_g, a_ln_b, a_w_out, b_conv_w, b_w_out, w_o, ffn2_norm, ffn2_w_gate, ffn2_w_up, ffn2_w_down, final_norm, loss_target, m_ffn1_norm, m_ffn1_w_gate, m_ffn1_w_up, m_ffn1_w_down, m_mix_norm, m_w_in, m_a_dw_w, m_a_dw_b, m_a_ln_g, m_a_ln_b, m_a_w_out, m_b_conv_w, m_b_w_out, m_w_o, m_ffn2_norm, m_ffn2_w_gate, m_ffn2_w_up, m_ffn2_w_down, m_final_norm, v_ffn1_norm, v_ffn1_w_gate, v_ffn1_w_up, v_ffn1_w_down, v_mix_norm, v_w_in, v_a_dw_w, v_a_dw_b, v_a_ln_g, v_a_ln_b, v_a_w_out, v_b_conv_w, v_b_w_out, v_w_o, v_ffn2_norm, v_ffn2_w_gate, v_ffn2_w_up, v_ffn2_w_down, v_final_norm):
    names = ("ffn1_norm", "ffn1_w_gate", "ffn1_w_up", "ffn1_w_down", "mix_norm", "w_in", "a_dw_w", "a_dw_b",
             "a_ln_g", "a_ln_b", "a_w_out", "b_conv_w", "b_w_out", "w_o", "ffn2_norm", "ffn2_w_gate", "ffn2_w_up",
             "ffn2_w_down", "final_norm")
    w = dict(ffn1_norm=ffn1_norm, ffn1_w_gate=ffn1_w_gate, ffn1_w_up=ffn1_w_up, ffn1_w_down=ffn1_w_down,
             mix_norm=mix_norm, w_in=w_in, a_dw_w=a_dw_w, a_dw_b=a_dw_b, a_ln_g=a_ln_g, a_ln_b=a_ln_b,
             a_w_out=a_w_out, b_conv_w=b_conv_w, b_w_out=b_w_out, w_o=w_o, ffn2_norm=ffn2_norm,
             ffn2_w_gate=ffn2_w_gate, ffn2_w_up=ffn2_w_up, ffn2_w_down=ffn2_w_down, final_norm=final_norm)
    m = dict(ffn1_norm=m_ffn1_norm, ffn1_w_gate=m_ffn1_w_gate, ffn1_w_up=m_ffn1_w_up, ffn1_w_down=m_ffn1_w_down,
             mix_norm=m_mix_norm, w_in=m_w_in, a_dw_w=m_a_dw_w, a_dw_b=m_a_dw_b, a_ln_g=m_a_ln_g, a_ln_b=m_a_ln_b,
             a_w_out=m_a_w_out, b_conv_w=m_b_conv_w, b_w_out=m_b_w_out, w_o=m_w_o, ffn2_norm=m_ffn2_norm,
             ffn2_w_gate=m_ffn2_w_gate, ffn2_w_up=m_ffn2_w_up, ffn2_w_down=m_ffn2_w_down, final_norm=m_final_norm)
    v = dict(ffn1_norm=v_ffn1_norm, ffn1_w_gate=v_ffn1_w_gate, ffn1_w_up=v_ffn1_w_up, ffn1_w_down=v_ffn1_w_down,
             mix_norm=v_mix_norm, w_in=v_w_in, a_dw_w=v_a_dw_w, a_dw_b=v_a_dw_b, a_ln_g=v_a_ln_g, a_ln_b=v_a_ln_b,
             a_w_out=v_a_w_out, b_conv_w=v_b_conv_w, b_w_out=v_b_w_out, w_o=v_w_o, ffn2_norm=v_ffn2_norm,
             ffn2_w_gate=v_ffn2_w_gate, ffn2_w_up=v_ffn2_w_up, ffn2_w_down=v_ffn2_w_down, final_norm=v_final_norm)
    flat = _pack_weights(dict(wg1=ffn1_w_gate[0].T, wu1=ffn1_w_up[0].T, wd1=ffn1_w_down[0], wg2=ffn2_w_gate[0].T,
                              wu2=ffn2_w_up[0].T, wd2=ffn2_w_down[0], win=w_in[0], wa=a_w_out[0], wb=b_w_out[0],
                              wo=w_o[0]))
    cw_shard = jnp.concatenate([a_dw_w[0], jnp.zeros((1, LANE), F32), b_conv_w[0], jnp.zeros((5, LANE), F32)], axis=0)

    x2, tgt = x[0], loss_target[0]
    st_a, st_b, st_c, st_d, st_e = ("wg1", "wu1", "wd1"), ("win",), ("wa", "wb", "wo", "wg2"), ("wu2",), ("wd2",)

    buf_a, cw = _run_comm(_join(_ag_comm(st_a, flat), _direct_comm(cw_shard, False)), "ag_ffn1")
    h1, n1, gg1, uu1, buf_b = _ffn_fwd(x2, ffn1_norm, (buf_a,) * 3, (0, F, 2 * F), "ffn1_fwd", _ag_comm(st_b, flat))
    u, z, buf_c = _mix_in(h1, mix_norm, buf_b, _ag_comm(st_c, flat))
    dft = _dft_constants()
    cw = jnp.transpose(cw, (1, 0, 2)).reshape(40, D)
    a1, q, buf_d = _conv_fwd_dft(z, cw, a_dw_b, dft, _ag_comm(st_d, flat))
    h2, ya, yb, buf_e = _mix_out(a1, q, z, h1, a_ln_g, a_ln_b, buf_c, _ag_comm(st_e, flat))
    ffn2_bufs, ffn2_offs = (buf_c, buf_d, buf_e), (3 * D, 0, 0)
    dh3, s_final, n2, gg2, uu2 = _ffn_fwd(h2, ffn2_norm, ffn2_bufs, ffn2_offs, "ffn2_fwd",
                                          final=(final_norm.reshape(1, D), tgt))

    tr_f = F // 2 if (F // 2) % LANE == 0 else F
    def pair(stage, src):
        return _rs_pair_comm(stage, src)

    def chip(stage, src, pair_buf, tag):
        return _rs_chip_comm(_pair_add(stage, src, pair_buf, "pair_add_" + tag))

    dgu2, act2 = _ffn_bwd_hidden(dh3, gg2, uu2, buf_e, 0, "ffn2_bwd_h")
    (gu2,) = _tn_matmul(dgu2, n2, tr_f, "dw_gu2")
    s2a, src2a = ("wg2", "wu2"), dict(wg2=(gu2, 0), wu2=(gu2, F))
    gd2, pair2a = _tn_matmul(act2, dh3, tr_f, "dw_d2", pair(s2a, src2a), scale=0.5)
    s2b, src2b = ("wd2",), dict(wd2=(gd2, 0))
    dh2, s_ffn2, recv2a, pair2b = _ffn_bwd_input(dgu2, dh3, h2, ffn2_norm, (buf_c, buf_d), (3 * D, 0), "ffn2_bwd_x",
                                                 _join(chip(s2a, src2a, pair2a, "2a"), pair(s2b, src2b)))
    dzg, da1, dq, mb, a3b, dya, dyb, s_mix, recv2b = _mix_out_bwd(dh2, ya, yb, z, a1, a_ln_g, a_ln_b, buf_c,
                                                                   chip(s2b, src2b, pair2b, "2b"))
    (go,) = _tn_matmul(mb, dh2, D, "dw_o")
    (ga,) = _tn_matmul(a3b, dya, D, "dw_a")
    (gb,) = _tn_matmul(q.reshape(1, T, D), dyb, D, "dw_b")
    ssq, srcsq = ("wa", "wb", "wo"), dict(wa=(ga, 0), wb=(gb, 0), wo=(go, 0))
    dz, dwa, dwb, pairsq = _conv_bwd_dft(z, da1, dq, dzg, cw, dft, pair(ssq, srcsq))
    gin, recvsq = _tn_matmul(dz, u, D, "dw_in", chip(ssq, srcsq, pairsq, "sq"))
    sin_a, sin_b, srcin = ("win/0/2",), ("win/1/2",), {"win/0/2": (gin, 0), "win/1/2": (gin, 0)}
    dh1, s_in, pairin_a, pairin_b = _mix_in_bwd(dz, dh2, h1, mix_norm, buf_b,
                                                _join(pair(sin_a, srcin), pair(sin_b, srcin)))
    dgu1, act1, recvin_a = _ffn_bwd_hidden(dh1, gg1, uu1, buf_a, 2 * F, "ffn1_bwd_h",
                                           chip(sin_a, srcin, pairin_a, "in_a"))
    gu1, recvin_b = _tn_matmul(dgu1, n1, tr_f, "dw_gu1", chip(sin_b, srcin, pairin_b, "in_b"))
    s1a, src1a = ("wg1", "wu1"), dict(wg1=(gu1, 0), wu1=(gu1, F))
    gd1, pair1a = _tn_matmul(act1, dh1, tr_f, "dw_d1", pair(s1a, src1a), scale=0.5)
    s1b, src1b = ("wd1",), dict(wd1=(gd1, 0))
    dx, s_ffn1, recv1a, pair1b = _ffn_bwd_input(dgu1, dh1, x2, ffn1_norm, (buf_a, buf_a), (0, F), "ffn1_bwd_x",
                                                _join(chip(s1a, src1a, pair1a, "1a"), pair(s1b, src1b)))
    (recv1b,) = _run_comm(chip(s1b, src1b, pair1b, "1b"), "rs_chip_1b")
    stages = ((s2a, recv2a, "2a"), (s2b, recv2b, "2b"), (ssq, recvsq, "sq"), (sin_a, recvin_a, "in_a"),
              (sin_b, recvin_b, "in_b"), (s1a, recv1a, "1a"), (s1b, recv1b, "1b"))

    gsum = {}
    for stage, recv, tag in stages:
        st, total = _Stage(stage), _sum_slots(recv, "sum_" + tag)
        for n in stage:
            gsum[n] = total[st.off[n]:st.off[n] + st.rows[n]]
    gsum["win"] = jnp.concatenate([gsum["win/0/2"], gsum["win/1/2"]], axis=0)

    vec8, convk = _pack_small(s_ffn1, s_in, s_mix, s_ffn2, s_final, dwa, dwb)
    vec_all, conv_all = _run_comm(_join(_direct_comm(vec8, False), _direct_comm(convk, True)), "xchg_small")
    vec_sum, conv_sum, loss_blk = _sum_small(vec_all, conv_all)
    loss = loss_blk[0, 0]

    g = dict(ffn1_w_gate=gsum["wg1"], ffn1_w_up=gsum["wu1"], ffn1_w_down=gsum["wd1"],
             ffn2_w_gate=gsum["wg2"], ffn2_w_up=gsum["wu2"], ffn2_w_down=gsum["wd2"], w_in=gsum["win"].T,
             a_w_out=gsum["wa"], b_w_out=gsum["wb"], w_o=gsum["wo"],
             ffn1_norm=vec_sum[0:1], mix_norm=vec_sum[1:2], a_ln_g=vec_sum[2:3], a_ln_b=vec_sum[3:4],
             a_dw_b=vec_sum[4:5], ffn2_norm=vec_sum[5:6], final_norm=vec_sum[6:7],
             a_dw_w=conv_sum[0:KA], b_conv_w=conv_sum[32:32 + KB])
    gate_up = ("ffn1_w_gate", "ffn1_w_up", "ffn2_w_gate", "ffn2_w_up")

    upd = {}

    def run(group, name, as2d=lambda a: a[0], back=lambda a, n: a.reshape(w[n].shape)):
        res, _ = _adam([g[n] for n in group], [as2d(w[n]) for n in group], [as2d(m[n]) for n in group],
                       [as2d(v[n]) for n in group], name)
        for n, r in zip(group, res):
            upd[n] = tuple(back(a, n) for a in r)

    run(gate_up, "adam_gate_up", as2d=lambda a: a[0].T, back=lambda a, n: a.T[None])
    for n in gate_up:
        g[n] = g[n].T
    run(("ffn1_w_down", "ffn2_w_down"), "adam_down")
    run(("w_in",), "adam_in")
    run(("a_w_out", "b_w_out", "w_o"), "adam_square")
    run(("a_dw_w",), "adam_dw")
    run(("b_conv_w",), "adam_conv")
    vecs = ("ffn1_norm", "mix_norm", "a_dw_b", "a_ln_g", "a_ln_b", "ffn2_norm", "final_norm")
    run(vecs, "adam_vec", as2d=lambda a: a.reshape(1, D))

    grads = [g[n].reshape(w[n].shape) for n in names]
    return (loss, dx.reshape(x.shape), *grads, *[upd[n][0] for n in names], *[upd[n][1] for n in names],
            *[upd[n][2] for n in names])
```

```python
import jax
import jax.numpy as jnp
from jax import lax
from jax.experimental import pallas as pl
from jax.experimental.pallas import tpu as pltpu

T = 4096
D = 1024
F = 2816
NG = 7
NDEV = 8
NCHIP = 4
KA, KB = 31, 3
EPS = 1e-6
ADAM_LR, ADAM_B1, ADAM_B2, ADAM_EPS, ADAM_WD, ADAM_STEP = 0.001, 0.9, 0.999, 1e-08, 0.01, 10

TM = 512
FC = 256
TB = 1024
NB = 256
HB = NB // 2
CW = 256
CHB = 64
LANE = 128
TK = 1024
VMEM_LIMIT = 56 * 1024 * 1024

BF = jnp.bfloat16
F32 = jnp.float32
MESH = pl.DeviceIdType.MESH
ANY = pl.BlockSpec(memory_space=pl.ANY)

ORDER = ("wg1", "wu1", "wd1", "wg2", "wu2", "wd2", "win", "wa", "wb", "wo")


class _Layout:
    def __init__(self):
        fs, dis, ds = F // NDEV, NG * D // NDEV, D // NDEV
        self.rows = dict(wg1=fs, wu1=fs, wd1=fs, wg2=fs, wu2=fs, wd2=fs, win=dis, wa=ds, wb=ds, wo=ds)
        self.fl, off = {}, 0
        for n in ORDER:
            self.fl[n] = off
            off += self.rows[n]
        self.RT = off


class _Stage:
    def __init__(self, names):
        lay = _Layout()
        self.names = names
        self.rows, self.full, self.sub, self.fl = {}, {}, {}, {}
        for n in names:
            base, i, k = (n.split("/") + ["0", "1"])[:3]
            self.full[n] = lay.rows[base]
            self.rows[n] = lay.rows[base] // int(k)
            self.sub[n] = int(i) * self.rows[n]
            self.fl[n] = lay.fl[base] + self.sub[n]
        self.off, self.wc, o, w = {}, {}, 0, 0
        for n in names:
            self.off[n], self.wc[n] = o, w
            o += self.rows[n]
            w += NDEV * self.rows[n]
        self.R, self.W = o, w

    def grad_row(self, n, first, dev_lin):
        return first + dev_lin * self.full[n] + self.sub[n]


def _nt(a, b):
    return lax.dot_general(a, b, (((1,), (1,)), ((), ())), preferred_element_type=F32)


def _nn(a, b):
    return lax.dot_general(a, b, (((1,), (0,)), ((), ())), preferred_element_type=F32)


def _tn(a, b):
    return lax.dot_general(a, b, (((0,), (0,)), ((), ())), preferred_element_type=F32)


def _sig(x):
    return 1.0 / (1.0 + jnp.exp(-x))


def _position():
    return lax.axis_index("x"), lax.axis_index("y"), lax.axis_index("c")


def _peer(pos, j):
    x, y, c = pos
    return (1 - x if j & 4 else x, 1 - y if j & 2 else y, 1 - c if j & 1 else c)


def _lin(pos):
    return 4 * pos[0] + 2 * pos[1] + pos[2]


def _chip(pos):
    return 2 * pos[0] + pos[1]


class _Comm:
    def __init__(self, inputs, out_shapes, scratch, start, finish, middle=None):
        self.inputs, self.out_shapes, self.scratch = inputs, out_shapes, scratch
        self.start, self.finish, self.middle = start, finish, middle


def _call(body, *, name, grid, args, in_specs, out_shape, out_specs, scratch_shapes=(), comm=None,
          num_scalar_prefetch=0):
    in_specs, out_shape, out_specs, scratch_shapes = list(in_specs), list(out_shape), list(out_specs), list(scratch_shapes)
    n_in, n_out, n_scr = len(in_specs), len(out_shape), len(scratch_shapes)
    sp = num_scalar_prefetch
    if comm is None:
        kernel_fn = lambda *refs: body(*refs)
        c_in = c_out = c_scr = 0
    else:
        c_in, c_out, c_scr = len(comm.inputs), len(comm.out_shapes), len(comm.scratch)

        def kernel_fn(*refs):
            pre, refs = refs[:sp], refs[sp:]
            ins, cins = refs[:n_in], refs[n_in:n_in + c_in]
            o0 = n_in + c_in
            outs, couts = refs[o0:o0 + n_out], refs[o0 + n_out:o0 + n_out + c_out]
            s0 = o0 + n_out + c_out
            scr, cscr = refs[s0:s0 + n_scr], refs[s0 + n_scr:]
            step, steps = pl.program_id(0), grid[0]
            for a in range(1, len(grid)):
                step, steps = step * grid[a] + pl.program_id(a), steps * grid[a]
            first, last = step == 0, step == steps - 1

            @pl.when(first)
            def _():
                comm.start(cins, couts, cscr)

            if comm.middle is not None:
                @pl.when(step == (steps // 2 if steps > 2 else steps - 1))
                def _():
                    comm.middle(cins, couts, cscr)

            body(*pre, *ins, *outs, *scr)

            @pl.when(last)
            def _():
                comm.finish(cins, couts, cscr)

        args = list(args) + list(comm.inputs)
        in_specs += [ANY] * c_in
        out_shape += list(comm.out_shapes)
        out_specs += [ANY] * c_out
        scratch_shapes += list(comm.scratch)
    params = pltpu.CompilerParams(dimension_semantics=("arbitrary",) * len(grid), vmem_limit_bytes=VMEM_LIMIT)
    if sp:
        grid_spec = pltpu.PrefetchScalarGridSpec(num_scalar_prefetch=sp, grid=grid, in_specs=in_specs,
                                                 out_specs=out_specs, scratch_shapes=scratch_shapes)
        return pl.pallas_call(kernel_fn, name=name, grid_spec=grid_spec, out_shape=out_shape,
                              compiler_params=params)(*args)
    return pl.pallas_call(kernel_fn, name=name, grid=grid, in_specs=in_specs, out_shape=out_shape, out_specs=out_specs,
                          scratch_shapes=scratch_shapes, compiler_params=params)(*args)


def _join(a, b):
    na = (len(a.inputs), len(a.out_shapes), len(a.scratch))

    def split(refs):
        return ([r[:n] for r, n in zip(refs, na)], [r[n:] for r, n in zip(refs, na)])

    def start(*refs):
        ra, rb = split(refs)
        a.start(*ra)
        b.start(*rb)

    def finish(*refs):
        ra, rb = split(refs)
        a.finish(*ra)
        b.finish(*rb)

    def middle(*refs):
        for stage, r in zip((a, b), split(refs)):
            if stage.middle is not None:
                stage.middle(*r)

    return _Comm(list(a.inputs) + list(b.inputs), list(a.out_shapes) + list(b.out_shapes),
                 list(a.scratch) + list(b.scratch), start, finish,
                 middle if (a.middle is not None or b.middle is not None) else None)


def _run_comm(comm, name):
    def body(*refs):
        c_in, c_out = len(comm.inputs), len(comm.out_shapes)
        parts = (refs[:c_in], refs[c_in:c_in + c_out], refs[c_in + c_out:])
        comm.start(*parts)
        if comm.middle is not None:
            comm.middle(*parts)
        comm.finish(*parts)

    return pl.pallas_call(
        body, name=name, out_shape=list(comm.out_shapes), in_specs=[ANY] * len(comm.inputs),
        out_specs=[ANY] * len(comm.out_shapes), scratch_shapes=list(comm.scratch))(*comm.inputs)


def _ag_comm(names, flat):
    st = _Stage(names)

    def ring(me):
        x, y, c = me
        diagonal = x == y
        up = (jnp.where(diagonal, x, 1 - x), jnp.where(diagonal, 1 - y, y), c)
        down = (jnp.where(diagonal, 1 - x, x), jnp.where(diagonal, y, 1 - y), c)
        low = c == 0
        passed = tuple(jnp.where(low, d, u) for d, u in zip(down, up))
        target = tuple(jnp.where(low, u, d) for d, u in zip(down, up))
        return up, down, (1 - x, 1 - y, c), passed, target

    def parts(refs):
        (flat_ref,), (out_ref,), (send_sems, recv_sems, local_sem) = refs
        me = _position()

        def region(name, dev):
            r = st.rows[name]
            return out_ref.at[pl.ds(st.wc[name] + _lin(dev) * r, r), :]

        def own(name):
            return flat_ref.at[pl.ds(st.fl[name], st.rows[name]), :]

        def copies(k, dev, to, from_flat):
            return [pltpu.make_async_remote_copy(
                src_ref=own(n) if from_flat else region(n, dev), dst_ref=region(n, dev), send_sem=send_sems.at[k],
                recv_sem=recv_sems.at[k], device_id=to, device_id_type=MESH) for n in names]

        def whole(k):
            return pltpu.make_async_remote_copy(
                src_ref=flat_ref.at[pl.ds(0, st.R), :], dst_ref=out_ref.at[pl.ds(0, st.R), :],
                send_sem=send_sems.at[k], recv_sem=recv_sems.at[k], device_id=me, device_id_type=MESH)

        return me, region, own, copies, whole, flat_ref, out_ref, local_sem

    def start(*refs):
        me, region, own, copies, _, _, _, local_sem = parts(refs)
        for n in names:
            pltpu.make_async_copy(own(n), region(n, me), local_sem).start()
        up, down, _, _, _ = ring(me)
        for k, to in ((1, up), (2, down), (0, _peer(me, 1))):
            for cp in copies(k, me, to, True):
                cp.start()

    def middle(*refs):
        me, _, _, copies, whole, _, _, _ = parts(refs)
        up, down, _, passed, target = ring(me)
        sib = _peer(me, 1)
        whole(1).wait_recv()
        whole(2).wait_recv()
        for k, dev, to in ((3, passed, target), (4, down, sib), (5, up, sib)):
            for cp in copies(k, dev, to, False):
                cp.start()

    def finish(*refs):
        me, _, _, copies, whole, flat_ref, out_ref, local_sem = parts(refs)
        _, _, across, _, _ = ring(me)
        whole(3).wait_recv()
        for cp in copies(6, across, _peer(me, 1), False):
            cp.start()
        whole(0).wait_recv()
        for j in range(3):
            whole(4 + j).wait_recv()
        for k in range(7):
            whole(k).wait_send()
        pltpu.make_async_copy(flat_ref.at[pl.ds(0, st.R), :], out_ref.at[pl.ds(0, st.R), :], local_sem).wait()

    return _Comm([flat], [jax.ShapeDtypeStruct((st.W, D), BF)],
                 [pltpu.SemaphoreType.DMA((7,)), pltpu.SemaphoreType.DMA((7,)), pltpu.SemaphoreType.DMA],
                 start, finish, middle)


def _rs_pair_comm(names, src):
    st = _Stage(names)
    arrays = []
    for n in names:
        if not any(src[n][0] is a for a in arrays):
            arrays.append(src[n][0])
    idx = {n: [i for i, a in enumerate(arrays) if a is src[n][0]][0] for n in names}

    def slot_wait(refs):
        recv = refs[1][0]
        send_sem, recv_sem = refs[2]
        return pltpu.make_async_remote_copy(src_ref=recv, dst_ref=recv, send_sem=send_sem, recv_sem=recv_sem,
                                            device_id=_position(), device_id_type=MESH)

    def start(*refs):
        ins, (recv,), (send_sem, recv_sem) = refs
        me = _position()
        sib = _peer(me, 1)
        for q in range(NCHIP):
            dev = (q // 2, q % 2, sib[2])
            for n in names:
                r = st.rows[n]
                pltpu.make_async_remote_copy(
                    src_ref=ins[idx[n]].at[pl.ds(st.grad_row(n, src[n][1], _lin(dev)), r), :],
                    dst_ref=recv.at[q, pl.ds(st.off[n], r), :], send_sem=send_sem, recv_sem=recv_sem,
                    device_id=sib, device_id_type=MESH).start()

    def finish(*refs):
        w = slot_wait(refs)
        w.wait_recv()
        w.wait_send()

    return _Comm(arrays, [jax.ShapeDtypeStruct((NCHIP, st.R, D), BF)],
                 [pltpu.SemaphoreType.DMA, pltpu.SemaphoreType.DMA], start, finish)


def _pair_add(names, src, recv, name):
    st = _Stage(names)
    c_arr = jnp.reshape(lax.axis_index("c"), (1,)).astype(jnp.int32)

    def body(c_ref, *refs):
        r_ref, o_ref = refs[len(names)], refs[len(names) + 1]
        for a_ref, n in zip(refs, names):
            rows = slice(st.off[n], st.off[n] + st.rows[n])
            o_ref[rows, :] = (a_ref[...].astype(F32) + r_ref[rows, :].astype(F32)).astype(BF)

    def shard_spec(n):
        r = st.rows[n]
        base, step = st.grad_row(n, src[n][1], 0) // r, st.full[n] // r
        return pl.BlockSpec((r, D), lambda q, c_ref: (base + step * (2 * q + c_ref[0]), 0))

    slot = pl.BlockSpec((None, st.R, D), lambda q, c_ref: (q, 0, 0))
    return _call(body, name=name, grid=(NCHIP,), args=[c_arr] + [src[n][0] for n in names] + [recv],
                 in_specs=[shard_spec(n) for n in names] + [slot],
                 out_shape=[jax.ShapeDtypeStruct((NCHIP, st.R, D), BF)], out_specs=[slot], num_scalar_prefetch=1)[0]


def _rs_chip_comm(part):
    def copies(refs):
        (p_ref,), (recv,), (send_sems, recv_sems, local_sem) = refs
        me = _position()
        mine = pltpu.make_async_copy(p_ref.at[_chip(me)], recv.at[_chip(me)], local_sem)
        out = []
        for j, bits in enumerate((4, 2, 6)):
            to = _peer(me, bits)
            out.append(pltpu.make_async_remote_copy(
                src_ref=p_ref.at[_chip(to)], dst_ref=recv.at[_chip(me)], send_sem=send_sems.at[j],
                recv_sem=recv_sems.at[j], device_id=to, device_id_type=MESH))
        return mine, out

    def start(*refs):
        mine, out = copies(refs)
        mine.start()
        for cp in out:
            cp.start()

    def finish(*refs):
        mine, out = copies(refs)
        for cp in out:
            cp.wait_recv()
        for cp in out:
            cp.wait_send()
        mine.wait()

    return _Comm([part], [jax.ShapeDtypeStruct(part.shape, BF)],
                 [pltpu.SemaphoreType.DMA((3,)), pltpu.SemaphoreType.DMA((3,)), pltpu.SemaphoreType.DMA],
                 start, finish)


def _direct_comm(x, scatter):
    def copies(refs):
        (x_ref,), (out_ref,), (send_sems, recv_sems, local_sem) = refs
        me = _position()

        def piece(dev):
            return x_ref.at[_lin(dev)] if scatter else x_ref

        mine = pltpu.make_async_copy(piece(me), out_ref.at[_lin(me)], local_sem)
        return mine, [pltpu.make_async_remote_copy(
            src_ref=piece(_peer(me, j)), dst_ref=out_ref.at[_lin(me)], send_sem=send_sems.at[j - 1],
            recv_sem=recv_sems.at[j - 1], device_id=_peer(me, j), device_id_type=MESH) for j in range(1, NDEV)]

    def start(*refs):
        mine, cps = copies(refs)
        mine.start()
        for cp in cps:
            cp.start()

    def finish(*refs):
        mine, cps = copies(refs)
        for cp in cps:
            cp.wait_recv()
        for cp in cps:
            cp.wait_send()
        mine.wait()

    shape = x.shape if scatter else (NDEV,) + x.shape
    return _Comm([x], [jax.ShapeDtypeStruct(shape, x.dtype)],
                 [pltpu.SemaphoreType.DMA((7,)), pltpu.SemaphoreType.DMA((7,)), pltpu.SemaphoreType.DMA],
                 start, finish)


def _pack_weights(shards):
    lay = _Layout()

    def body(*refs):
        o_ref = refs[-1]
        for ref, n in zip(refs, ORDER):
            x = ref[...].T if n == "win" else ref[...]
            o_ref[lay.fl[n]:lay.fl[n] + lay.rows[n], :] = x.astype(BF)

    return pl.pallas_call(
        body, name="pack_weights", out_shape=jax.ShapeDtypeStruct((lay.RT, D), BF),
        compiler_params=pltpu.CompilerParams(vmem_limit_bytes=VMEM_LIMIT))(*[shards[n] for n in ORDER])


def _load_ffn_weights(srcs, offs, scratch, sem):
    @pl.when(pl.program_id(0) == 0)
    def _():
        cps = [pltpu.make_async_copy(s.at[pl.ds(off, dst.shape[0]), :], dst, sem.at[i])
               for i, (s, off, dst) in enumerate(zip(srcs, offs, scratch))]
        for cp in cps:
            cp.start()
        for cp in cps:
            cp.wait()


def _final_loss_tile(xf, g, tgt, s_ref):
    r = lax.rsqrt(jnp.mean(xf * xf, axis=-1, keepdims=True) + EPS)
    xr = xf * r
    e = xr * g - tgt
    s_ref[1:2, :] += jnp.sum(e * e, axis=0, keepdims=True) * (0.5 / D)
    dy = e * (1.0 / D)
    s_ref[0:1, :] += jnp.sum(dy * xr, axis=0, keepdims=True)
    gdy = dy * g
    return r * gdy - xr * (r * jnp.mean(gdy * xr, axis=-1, keepdims=True))


def _ffn_fwd(x, g, wbufs, offs, name, comm=None, final=None):
    nf = F // FC

    def body(x_ref, g_ref, b0, b1, b2, *rest):
        if final is None:
            h_ref, n_ref, gg_ref, uu_ref, wg_s, wu_s, wd_s, sem = rest
        else:
            gf_ref, t_ref, dh_ref, s_ref, n_ref, gg_ref, uu_ref, wg_s, wu_s, wd_s, sem = rest

            @pl.when(pl.program_id(0) == 0)
            def _():
                s_ref[...] = jnp.zeros_like(s_ref)

        _load_ffn_weights((b0, b1, b2), offs, (wg_s, wu_s, wd_s), sem)
        xf = x_ref[...]
        r = lax.rsqrt(jnp.mean(xf * xf, axis=-1, keepdims=True) + EPS)
        nb = (xf * r * g_ref[...]).astype(BF)
        n_ref[...] = nb
        acc = jnp.zeros((TM, D), F32)
        for c in range(nf):
            sl = slice(c * FC, (c + 1) * FC)
            gb = _nt(nb, wg_s[sl, :]).astype(BF)
            ub = _nt(nb, wu_s[sl, :]).astype(BF)
            gg_ref[:, sl] = gb
            uu_ref[:, sl] = ub
            acc = acc + _nn((gb * _sig(gb)) * ub, wd_s[sl, :])
        h = xf + 0.5 * acc
        if final is None:
            h_ref[...] = h
        else:
            dh_ref[...] = _final_loss_tile(h, gf_ref[...], t_ref[...], s_ref)

    row = lambda i: (i, 0)
    vec = pl.BlockSpec((1, D), lambda i: (0, 0))
    tile = pl.BlockSpec((TM, D), row)
    saved_shapes = [jax.ShapeDtypeStruct((T, D), BF), jax.ShapeDtypeStruct((T, F), BF), jax.ShapeDtypeStruct((T, F), BF)]
    saved_specs = [tile, pl.BlockSpec((TM, F), row), pl.BlockSpec((TM, F), row)]
    if final is None:
        extra_args, extra_specs = [], []
        head_shapes, head_specs = [jax.ShapeDtypeStruct((T, D), F32)], [tile]
    else:
        extra_args, extra_specs = list(final), [vec, tile]
        head_shapes = [jax.ShapeDtypeStruct((T, D), F32), jax.ShapeDtypeStruct((8, D), F32)]
        head_specs = [tile, pl.BlockSpec((8, D), lambda i: (0, 0))]
    return _call(
        body, name=name, grid=(T // TM,), args=[x, g, *wbufs, *extra_args], comm=comm,
        in_specs=[tile, vec, ANY, ANY, ANY] + extra_specs,
        out_shape=head_shapes + saved_shapes, out_specs=head_specs + saved_specs,
        scratch_shapes=[pltpu.VMEM((F, D), BF)] * 3 + [pltpu.SemaphoreType.DMA((3,))])


def _mix_in(h1, gm, win, comm=None):
    def body(h_ref, g_ref, w_any, u_ref, z_ref, w_s, sem):
        _load_ffn_weights((w_any,), (0,), (w_s,), sem)
        xf = h_ref[...]
        r = lax.rsqrt(jnp.mean(xf * xf, axis=-1, keepdims=True) + EPS)
        ub = (xf * r * g_ref[...]).astype(BF)
        u_ref[...] = ub
        for j in range(NG):
            z_ref[j] = _nt(ub, w_s[j * D:(j + 1) * D, :]).astype(BF)

    row = lambda i: (i, 0)
    return _call(
        body, name="mix_in", grid=(T // TM,), args=[h1, gm, win], comm=comm,
        in_specs=[pl.BlockSpec((TM, D), row), pl.BlockSpec((1, D), lambda i: (0, 0)), ANY],
        out_shape=[jax.ShapeDtypeStruct((T, D), BF), jax.ShapeDtypeStruct((NG, T, D), BF)],
        out_specs=[pl.BlockSpec((TM, D), row), pl.BlockSpec((NG, TM, D), lambda i: (0, i, 0))],
        scratch_shapes=[pltpu.VMEM((NG * D, D), BF), pltpu.SemaphoreType.DMA((1,))])


def _shift_up(w, b):
    return w if b == 0 else pltpu.roll(w, w.shape[0] - b, 0)


def _fold8(p):
    red = p[0:8, :]
    for i in range(1, p.shape[0] // 8):
        red = red + p[8 * i:8 * i + 8, :]
    return red


def _dft_constants():
    import numpy as np
    nh = NB // 2
    f, n = np.arange(nh)[:, None], np.arange(NB)[None, :]
    ang = 2.0 * np.pi / NB * f * n
    fc = np.cos(ang)
    fs = np.where(f == 0, (-1.0) ** n, np.sin(ang))
    scale = np.where(f == 0, 1.0, 2.0) / NB
    ic = (scale * np.cos(ang)).T
    isn = np.where(f == 0, (-1.0) ** n / NB, scale * np.sin(ang)).T
    d = (KA - 1 - np.arange(32))[None, :]
    valid = (np.arange(32) < KA)[None, :]
    angk = 2.0 * np.pi / NB * f * d
    kc = np.where(valid, np.cos(angk), 0.0)
    ks = np.where(valid, np.sin(angk), 0.0)
    k2 = np.where(valid, np.where(f == 0, (-1.0) ** d, np.cos(angk)), 0.0)
    rtc = np.where(valid, scale * np.cos(angk), 0.0).T
    rts = np.where(valid, np.where(f == 0, (-1.0) ** d / NB, scale * np.sin(angk)), 0.0).T

    def bf(a):
        return jnp.asarray(a, F32).astype(BF)

    def split(a):
        hi = bf(a)
        return hi, (jnp.asarray(a, F32) - hi.astype(F32)).astype(BF)

    return dict(fc=bf(fc), fs=bf(fs), ic_hi=bf(ic[HB:]), is_hi=bf(isn[HB:]), ic_lo=bf(ic[:HB]), is_lo=bf(isn[:HB]),
                kc=split(kc), ks=split(ks), k2=split(k2), rtc=split(rtc), rts=split(rts))


def _dot3(m_hi, m_lo, x):
    x_hi = x.astype(BF)
    x_lo = (x - x_hi.astype(F32)).astype(BF)
    return _nn(m_hi, x_hi) + _nn(m_hi, x_lo) + _nn(m_lo, x_hi)


def _whole(a):
    return pl.BlockSpec(a.shape, lambda c, t: (0,) * a.ndim)


def _filter_spectrum(cw_ref, tabs, hc, hs, h2):
    w32 = cw_ref[0:32, :]
    for (hi, lo), dst in zip(tabs, (hc, hs, h2)):
        dst[...] = _dot3(hi[...], lo[...], w32)


def _conv_fwd_dft(z, cw, bias, dft, comm=None):
    nt = T // TB
    hb = TB // HB

    def body(z_ref, zh_ref, cw_ref, b_ref, fc_ref, fs_ref, ic_ref, is_ref, kch, kcl, ksh, ksl, k2h, k2l,
             a1_ref, q_ref, aext, ppad, hc, hs, h2):
        first = pl.program_id(1) == 0
        f = lambda ref, j: ref[j].astype(F32)

        @pl.when(first)
        def _():
            _filter_spectrum(cw_ref, ((kch, kcl), (ksh, ksl), (k2h, k2l)), hc, hs, h2)

        aext[0:HB, :] = jnp.where(first, 0.0, f(zh_ref, 0) * _sig(f(zh_ref, 1))).astype(BF)
        aext[HB:, :] = (f(z_ref, 0) * _sig(f(z_ref, 1))).astype(BF)
        ppad[0:8, :] = jnp.where(first, 0.0, f(zh_ref, 3)[HB - 8:HB, :] * f(zh_ref, 4)[HB - 8:HB, :])
        ppad[8:, :] = f(z_ref, 3) * f(z_ref, 4)
        bias_row = b_ref[...]

        for j in range(TB // HB):
            xs = aext[j * HB:j * HB + NB, :]
            xa, xb = _nn(fc_ref[...], xs), _nn(fs_ref[...], xs)
            yc = (hc[...] * xa - hs[...] * xb).astype(BF)
            ys = (h2[...] * xb + hs[...] * xa).astype(BF)
            y = _nn(ic_ref[...], yc) + _nn(is_ref[...], ys)
            a1_ref[j * HB:(j + 1) * HB, :] = (y + bias_row).astype(BF)

        def chunk(r, carry):
            base = pl.multiple_of(r * CHB, CHB)
            pw = ppad[pl.ds(base, CHB + 8), :]
            v = (cw_ref[pl.ds(32, 1), :] * _shift_up(pw, 6)[0:CHB, :]
                 + cw_ref[pl.ds(33, 1), :] * _shift_up(pw, 7)[0:CHB, :]
                 + cw_ref[pl.ds(34, 1), :] * pw[8:8 + CHB, :])
            q_ref[pl.ds(base, CHB), :] = (z_ref[2, pl.ds(base, CHB), :].astype(F32) * v).astype(BF)
            return carry

        lax.fori_loop(0, TB // CHB, chunk, 0)

    blk = pl.BlockSpec((TB, CW), lambda c, t: (t, c))
    tabs = [dft["fc"], dft["fs"], dft["ic_hi"], dft["is_hi"], *dft["kc"], *dft["ks"], *dft["k2"]]
    return _call(
        body, name="conv_fwd", grid=(D // CW, nt), comm=comm, args=[z, z, cw, bias] + tabs,
        in_specs=[pl.BlockSpec((5, TB, CW), lambda c, t: (0, t, c)),
                  pl.BlockSpec((5, HB, CW), lambda c, t: (0, jnp.maximum(t * hb - 1, 0), c)),
                  pl.BlockSpec((40, CW), lambda c, t: (0, c)), pl.BlockSpec((1, CW), lambda c, t: (0, c))]
                 + [_whole(a) for a in tabs],
        out_shape=[jax.ShapeDtypeStruct((T, D), BF), jax.ShapeDtypeStruct((T, D), BF)], out_specs=[blk, blk],
        scratch_shapes=[pltpu.VMEM((TB + HB, CW), BF), pltpu.VMEM((TB + 8, CW), F32)]
                       + [pltpu.VMEM((NB // 2, CW), F32)] * 3)


def _conv_bwd_dft(z, da1, dq, dzg, cw, dft, comm=None):
    nt = T // TB
    hb = TB // HB
    last_h = T // HB - 1

    def body(z_ref, zp_ref, zn_ref, da1_ref, da1n_ref, dq_ref, dqn_ref, dzg_ref, cw_ref,
             fc_ref, fs_ref, ic_ref, is_ref, kch, kcl, ksh, ksl, k2h, k2l, rch, rcl, rsh, rsl,
             dz_ref, dwa_ref, dwb_ref, aext, dyext, ppad, dvpad, hc, hs, h2, rc, rs, nyq, acc_b):
        t = pl.program_id(1)
        first, last = t == 0, t == nt - 1
        f = lambda ref, j: ref[j].astype(F32)

        @pl.when(first)
        def _():
            _filter_spectrum(cw_ref, ((kch, kcl), (ksh, ksl), (k2h, k2l)), hc, hs, h2)
            rc[...] = jnp.zeros_like(rc)
            rs[...] = jnp.zeros_like(rs)
            nyq[...] = jnp.zeros_like(nyq)
            acc_b[...] = jnp.zeros_like(acc_b)

        aext[0:HB, :] = jnp.where(first, 0.0, f(zp_ref, 0) * _sig(f(zp_ref, 1))).astype(BF)
        aext[HB:, :] = (f(z_ref, 0) * _sig(f(z_ref, 1))).astype(BF)
        dyext[0:TB, :] = da1_ref[...]
        dyext[TB:, :] = jnp.where(last, 0.0, da1n_ref[...].astype(F32)).astype(BF)
        ppad[0:8, :] = jnp.where(first, 0.0, f(zp_ref, 3)[HB - 8:HB, :] * f(zp_ref, 4)[HB - 8:HB, :])
        ppad[8:, :] = f(z_ref, 3) * f(z_ref, 4)
        dvpad[0:TB, :] = dq_ref[...].astype(F32) * f(z_ref, 2)
        dvpad[TB:, :] = jnp.where(last, 0.0, dqn_ref[...].astype(F32)[0:8, :] * f(zn_ref, 2)[0:8, :])

        for j in range(TB // HB):
            rows = slice(j * HB, (j + 1) * HB)
            dys = dyext[j * HB:j * HB + NB, :]
            da, db = _nn(fc_ref[...], dys), _nn(fs_ref[...], dys)
            gc = (hc[...] * da + hs[...] * db).astype(BF)
            gs = (h2[...] * db - hs[...] * da).astype(BF)
            da0 = _nn(ic_ref[...], gc) + _nn(is_ref[...], gs)
            z0, z1 = z_ref[0, rows, :].astype(F32), z_ref[1, rows, :].astype(F32)
            s1 = _sig(z1)
            dz_ref[0, rows, :] = (da0 * s1).astype(BF)
            dz_ref[1, rows, :] = (da0 * z0 * (s1 * (1.0 - s1))).astype(BF)
            xs = aext[j * HB:j * HB + NB, :]
            xa, xb = _nn(fc_ref[...], xs), _nn(fs_ref[...], xs)
            dyb = dyext[rows, :]
            pa, pb = _nn(fc_ref[:, HB:NB], dyb), _nn(fs_ref[:, HB:NB], dyb)
            rc[...] += pa * xa + pb * xb
            rs[...] += pb * xa - pa * xb
            nyq[...] += pb[0:8, :] * xb[0:8, :]

        def chunk(r, carry):
            base = pl.multiple_of(r * CHB, CHB)
            rows = pl.ds(base, CHB)
            pw = ppad[pl.ds(base, CHB + 8), :]
            p6 = _shift_up(pw, 6)[0:CHB, :]
            p7 = _shift_up(pw, 7)[0:CHB, :]
            p8 = pw[8:8 + CHB, :]
            wb0, wb1, wb2 = cw_ref[pl.ds(32, 1), :], cw_ref[pl.ds(33, 1), :], cw_ref[pl.ds(34, 1), :]
            v = wb0 * p6 + wb1 * p7 + wb2 * p8
            dz_ref[2, rows, :] = (dq_ref[rows, :].astype(F32) * v).astype(BF)
            dvw = dvpad[pl.ds(base, CHB + 8), :]
            dvc = dvw[0:CHB, :]
            dp = wb2 * dvc + wb1 * _shift_up(dvw, 1)[0:CHB, :] + wb0 * _shift_up(dvw, 2)[0:CHB, :]
            dz_ref[3, rows, :] = (dp * z_ref[4, rows, :].astype(F32)).astype(BF)
            dz_ref[4, rows, :] = (dp * z_ref[3, rows, :].astype(F32)).astype(BF)
            acc_b[0:8, :] += _fold8(dvc * p6)
            acc_b[8:16, :] += _fold8(dvc * p7)
            acc_b[16:24, :] += _fold8(dvc * p8)
            dz_ref[5, rows, :] = dzg_ref[0, rows, :]
            dz_ref[6, rows, :] = dzg_ref[1, rows, :]
            return carry

        lax.fori_loop(0, TB // CHB, chunk, 0)

        @pl.when(last)
        def _():
            row0 = lax.broadcasted_iota(jnp.int32, (NB // 2, CW), 0) == 0
            ny = jnp.broadcast_to(nyq[0:1, :], (NB // 2, CW))
            rcv = jnp.where(row0, rc[...] - ny, rc[...])
            rsv = jnp.where(row0, ny, rs[...])
            dwa_ref[...] = _dot3(rch[...], rcl[...], rcv) + _dot3(rsh[...], rsl[...], rsv)
            for k in range(KB):
                dwb_ref[k:k + 1, :] = jnp.sum(acc_b[8 * k:8 * k + 8, :], axis=0, keepdims=True)
            dwb_ref[KB:8, :] = jnp.zeros((8 - KB, CW), F32)

    blk = lambda c, t: (t, c)
    nxt = lambda c, t: (jnp.minimum((t + 1) * hb, last_h), c)
    tabs = [dft["fc"], dft["fs"], dft["ic_lo"], dft["is_lo"], *dft["kc"], *dft["ks"], *dft["k2"], *dft["rtc"], *dft["rts"]]
    return _call(
        body, name="conv_bwd", grid=(D // CW, nt), comm=comm, args=[z, z, z, da1, da1, dq, dq, dzg, cw] + tabs,
        in_specs=[pl.BlockSpec((5, TB, CW), lambda c, t: (0, t, c)),
                  pl.BlockSpec((5, HB, CW), lambda c, t: (0, jnp.maximum(t * hb - 1, 0), c)),
                  pl.BlockSpec((5, HB, CW), lambda c, t: (0, jnp.minimum((t + 1) * hb, last_h), c)),
                  pl.BlockSpec((TB, CW), blk), pl.BlockSpec((HB, CW), nxt),
                  pl.BlockSpec((TB, CW), blk), pl.BlockSpec((HB, CW), nxt),
                  pl.BlockSpec((2, TB, CW), lambda c, t: (0, t, c)),
                  pl.BlockSpec((40, CW), lambda c, t: (0, c))]
                 + [_whole(a) for a in tabs],
        out_shape=[jax.ShapeDtypeStruct((NG, T, D), BF), jax.ShapeDtypeStruct((32, D), F32),
                   jax.ShapeDtypeStruct((8, D), F32)],
        out_specs=[pl.BlockSpec((NG, TB, CW), lambda c, t: (0, t, c)),
                   pl.BlockSpec((32, CW), lambda c, t: (0, c)), pl.BlockSpec((8, CW), lambda c, t: (0, c))],
        scratch_shapes=[pltpu.VMEM((TB + HB, CW), BF), pltpu.VMEM((TB + HB, CW), BF),
                        pltpu.VMEM((TB + 8, CW), F32), pltpu.VMEM((TB + 8, CW), F32)]
                       + [pltpu.VMEM((NB // 2, CW), F32)] * 5 + [pltpu.VMEM((8, CW), F32), pltpu.VMEM((24, CW), F32)])


def _layernorm_silu(a1, lng, lnb):
    mu = jnp.mean(a1, axis=-1, keepdims=True)
    xc = a1 - mu
    rs = lax.rsqrt(jnp.mean(xc * xc, axis=-1, keepdims=True) + EPS)
    xh = xc * rs
    a2 = xh * lng + lnb
    sg = _sig(a2)
    return xh, rs, a2, sg


def _square_specs(blocks):
    return [pl.BlockSpec((D, D), lambda i, b=b: (b, 0)) for b in blocks]


def _mix_out(a1, q, z, h1, lng, lnb, wsq, comm=None):
    def body(a1_ref, q_ref, ga_ref, gb_ref, h_ref, lng_ref, lnb_ref, wa_ref, wb_ref, wo_ref, h2_ref, ya_ref, yb_ref):
        _, _, a2, sg = _layernorm_silu(a1_ref[...].astype(F32), lng_ref[...], lnb_ref[...])
        ya = _nn((a2 * sg).astype(BF), wa_ref[...])
        yb = _nn(q_ref[...], wb_ref[...])
        ya_ref[...] = ya.astype(BF)
        yb_ref[...] = yb.astype(BF)
        m = _sig(ga_ref[...].astype(F32)) * ya + _sig(gb_ref[...].astype(F32)) * yb
        h2_ref[...] = h_ref[...] + _nn(m.astype(BF), wo_ref[...])

    row = lambda i: (i, 0)
    vec = pl.BlockSpec((1, D), lambda i: (0, 0))
    return _call(
        body, name="mix_out", grid=(T // TM,), args=[a1, q, z, z, h1, lng, lnb, wsq, wsq, wsq], comm=comm,
        in_specs=[pl.BlockSpec((TM, D), row), pl.BlockSpec((TM, D), row),
                  pl.BlockSpec((None, TM, D), lambda i: (5, i, 0)), pl.BlockSpec((None, TM, D), lambda i: (6, i, 0)),
                  pl.BlockSpec((TM, D), row), vec, vec] + _square_specs((0, 1, 2)),
        out_shape=[jax.ShapeDtypeStruct((T, D), F32), jax.ShapeDtypeStruct((T, D), BF), jax.ShapeDtypeStruct((T, D), BF)],
        out_specs=[pl.BlockSpec((TM, D), row)] * 3)


def _rmsnorm_bwd(xf, g, dn):
    r = lax.rsqrt(jnp.mean(xf * xf, axis=-1, keepdims=True) + EPS)
    xr = xf * r
    gdn = dn * g
    dx = r * gdn - xr * (r * jnp.mean(gdn * xr, axis=-1, keepdims=True))
    return dx, jnp.sum(dn * xr, axis=0, keepdims=True)


def _ffn_bwd_hidden(dh, gg, uu, wbuf, off, name, comm=None):
    nf = F // FC

    def body(dh_ref, gg_ref, uu_ref, b0, dgu_ref, a_ref, wd_s, sem):
        _load_ffn_weights((b0,), (off,), (wd_s,), sem)
        dhb = (0.5 * dh_ref[...]).astype(BF)
        for c in range(nf):
            sl = slice(c * FC, (c + 1) * FC)
            da = _nt(dhb, wd_s[sl, :]).astype(BF)
            gb, ub = gg_ref[:, sl], uu_ref[:, sl]
            sg = _sig(gb)
            silu = gb * sg
            dgu_ref[0, :, sl] = (da * ub) * (sg * (1.0 + gb * (1.0 - sg)))
            dgu_ref[1, :, sl] = da * silu
            a_ref[0, :, sl] = silu * ub

    row = lambda i: (i, 0)
    return _call(
        body, name=name, grid=(T // TM,), args=[dh, gg, uu, wbuf], comm=comm,
        in_specs=[pl.BlockSpec((TM, D), row), pl.BlockSpec((TM, F), row), pl.BlockSpec((TM, F), row), ANY],
        out_shape=[jax.ShapeDtypeStruct((2, T, F), BF), jax.ShapeDtypeStruct((1, T, F), BF)],
        out_specs=[pl.BlockSpec((2, TM, F), lambda i: (0, i, 0)), pl.BlockSpec((1, TM, F), lambda i: (0, i, 0))],
        scratch_shapes=[pltpu.VMEM((F, D), BF), pltpu.SemaphoreType.DMA((1,))])


def _ffn_bwd_input(dgu, dh, x, g, wbufs, offs, name, comm=None):
    def body(dgu_ref, dh_ref, x_ref, g_ref, b0, b1, dx_ref, s_ref, wg_s, wu_s, sem):
        _load_ffn_weights((b0, b1), offs, (wg_s, wu_s), sem)

        @pl.when(pl.program_id(0) == 0)
        def _():
            s_ref[...] = jnp.zeros_like(s_ref)

        dn = _nn(dgu_ref[0], wg_s[...]) + _nn(dgu_ref[1], wu_s[...])
        dxn, dg = _rmsnorm_bwd(x_ref[...], g_ref[...], dn)
        dx_ref[...] = dh_ref[...] + dxn
        s_ref[0:1, :] += dg

    row = lambda i: (i, 0)
    return _call(
        body, name=name, grid=(T // TM,), args=[dgu, dh, x, g, *wbufs], comm=comm,
        in_specs=[pl.BlockSpec((2, TM, F), lambda i: (0, i, 0)), pl.BlockSpec((TM, D), row),
                  pl.BlockSpec((TM, D), row), pl.BlockSpec((1, D), lambda i: (0, 0)), ANY, ANY],
        out_shape=[jax.ShapeDtypeStruct((T, D), F32), jax.ShapeDtypeStruct((8, D), F32)],
        out_specs=[pl.BlockSpec((TM, D), row), pl.BlockSpec((8, D), lambda i: (0, 0))],
        scratch_shapes=[pltpu.VMEM((F, D), BF)] * 2 + [pltpu.SemaphoreType.DMA((2,))])


def _tn_matmul(lhs, rhs, tr, name, comm=None, scale=None):
    ng, _, cdim = lhs.shape
    nc, nk = cdim // tr, T // TK

    def body(l_ref, r_ref, o_ref, acc):
        k = pl.program_id(2)

        @pl.when(k == 0)
        def _():
            acc[...] = jnp.zeros_like(acc)

        r = r_ref[...] if scale is None else scale * r_ref[...]
        acc[...] += _tn(l_ref[...], r.astype(BF))

        @pl.when(k == nk - 1)
        def _():
            o_ref[...] = acc[...].astype(BF)

    return _call(
        body, name=name, grid=(ng, nc, nk), args=[lhs, rhs], comm=comm,
        in_specs=[pl.BlockSpec((None, TK, tr), lambda g, c, k: (g, k, c)),
                  pl.BlockSpec((TK, D), lambda g, c, k: (k, 0))],
        out_shape=[jax.ShapeDtypeStruct((ng * cdim, D), BF)],
        out_specs=[pl.BlockSpec((tr, D), lambda g, c, k: (g * nc + c, 0))],
        scratch_shapes=[pltpu.VMEM((tr, D), F32)])


def _mix_out_bwd(dh2, ya, yb, z, a1, lng, lnb, wsq, comm=None):
    def body(dh_ref, ya_ref, yb_ref, ga_ref, gb_ref, a1_ref, lng_ref, lnb_ref, wa_ref, wb_ref, wo_ref,
             dzg_ref, da1_ref, dq_ref, m_ref, a3_ref, dya_ref, dyb_ref, s_ref):
        @pl.when(pl.program_id(0) == 0)
        def _():
            s_ref[...] = jnp.zeros_like(s_ref)

        dm = _nt(dh_ref[...].astype(BF), wo_ref[...])
        ya, yb = ya_ref[...].astype(F32), yb_ref[...].astype(F32)
        sa, sb = _sig(ga_ref[...].astype(F32)), _sig(gb_ref[...].astype(F32))
        m_ref[0] = (sa * ya + sb * yb).astype(BF)
        dzg_ref[0] = (dm * ya * (sa * (1.0 - sa))).astype(BF)
        dzg_ref[1] = (dm * yb * (sb * (1.0 - sb))).astype(BF)
        dya = (dm * sa).astype(BF)
        dyb = (dm * sb).astype(BF)
        dya_ref[...] = dya
        dyb_ref[...] = dyb
        dq_ref[...] = _nt(dyb, wb_ref[...]).astype(BF)
        da3 = _nt(dya, wa_ref[...])
        lng = lng_ref[...]
        xh, rs, a2, sg = _layernorm_silu(a1_ref[...].astype(F32), lng, lnb_ref[...])
        a3_ref[0] = (a2 * sg).astype(BF)
        da2 = da3 * (sg * (1.0 + a2 * (1.0 - sg)))
        s_ref[0:1, :] += jnp.sum(da2 * xh, axis=0, keepdims=True)
        s_ref[1:2, :] += jnp.sum(da2, axis=0, keepdims=True)
        dxh = da2 * lng
        da1 = rs * (dxh - jnp.mean(dxh, axis=-1, keepdims=True) - xh * jnp.mean(dxh * xh, axis=-1, keepdims=True))
        da1_ref[...] = da1.astype(BF)
        s_ref[2:3, :] += jnp.sum(da1, axis=0, keepdims=True)

    row = lambda i: (i, 0)
    row3 = lambda i: (0, i, 0)
    vec = pl.BlockSpec((1, D), lambda i: (0, 0))
    return _call(
        body, name="mix_out_bwd", grid=(T // TM,), args=[dh2, ya, yb, z, z, a1, lng, lnb, wsq, wsq, wsq], comm=comm,
        in_specs=[pl.BlockSpec((TM, D), row), pl.BlockSpec((TM, D), row), pl.BlockSpec((TM, D), row),
                  pl.BlockSpec((None, TM, D), lambda i: (5, i, 0)), pl.BlockSpec((None, TM, D), lambda i: (6, i, 0)),
                  pl.BlockSpec((TM, D), row), vec, vec] + _square_specs((0, 1, 2)),
        out_shape=[jax.ShapeDtypeStruct((2, T, D), BF), jax.ShapeDtypeStruct((T, D), BF),
                   jax.ShapeDtypeStruct((T, D), BF), jax.ShapeDtypeStruct((1, T, D), BF),
                   jax.ShapeDtypeStruct((1, T, D), BF), jax.ShapeDtypeStruct((T, D), BF),
                   jax.ShapeDtypeStruct((T, D), BF), jax.ShapeDtypeStruct((8, D), F32)],
        out_specs=[pl.BlockSpec((2, TM, D), row3), pl.BlockSpec((TM, D), row), pl.BlockSpec((TM, D), row),
                   pl.BlockSpec((1, TM, D), row3), pl.BlockSpec((1, TM, D), row3), pl.BlockSpec((TM, D), row),
                   pl.BlockSpec((TM, D), row), pl.BlockSpec((8, D), lambda i: (0, 0))])


def _mix_in_bwd(dz, dh2, h1, gm, win, comm=None):
    def body(dz_ref, w_any, dh_ref, h_ref, g_ref, o_ref, s_ref, w_s, sem):
        _load_ffn_weights((w_any,), (0,), (w_s,), sem)

        @pl.when(pl.program_id(0) == 0)
        def _():
            s_ref[...] = jnp.zeros_like(s_ref)

        du = _nn(dz_ref[0], w_s[0:D, :])
        for j in range(1, NG):
            du = du + _nn(dz_ref[j], w_s[j * D:(j + 1) * D, :])
        dx, dg = _rmsnorm_bwd(h_ref[...], g_ref[...], du)
        o_ref[...] = dh_ref[...] + dx
        s_ref[0:1, :] += dg

    row = lambda i: (i, 0)
    return _call(
        body, name="mix_in_bwd", grid=(T // TM,), args=[dz, win, dh2, h1, gm], comm=comm,
        in_specs=[pl.BlockSpec((NG, TM, D), lambda i: (0, i, 0)), ANY,
                  pl.BlockSpec((TM, D), row), pl.BlockSpec((TM, D), row), pl.BlockSpec((1, D), lambda i: (0, 0))],
        out_shape=[jax.ShapeDtypeStruct((T, D), F32), jax.ShapeDtypeStruct((8, D), F32)],
        out_specs=[pl.BlockSpec((TM, D), row), pl.BlockSpec((8, D), lambda i: (0, 0))],
        scratch_shapes=[pltpu.VMEM((NG * D, D), BF), pltpu.SemaphoreType.DMA((1,))])


def _row_tile(n, want, mult):
    for t in range(min(want, n), 0, -1):
        if n % t == 0 and t % mult == 0:
            return t
    return n


def _sum_slots(recv, name):
    ns, rows, cols = recv.shape
    tr = _row_tile(rows, 1024, 16)

    def body(r_ref, o_ref):
        s = r_ref[0].astype(F32)
        for k in range(1, ns):
            s = s + r_ref[k].astype(F32)
        o_ref[...] = s

    return _call(
        body, name=name, grid=(rows // tr,), args=[recv],
        in_specs=[pl.BlockSpec((ns, tr, cols), lambda i: (0, i, 0))],
        out_shape=[jax.ShapeDtypeStruct((rows, cols), F32)],
        out_specs=[pl.BlockSpec((tr, cols), lambda i: (i, 0))])[0]


def _pack_small(s_ffn1, s_in, s_mix, s_ffn2, s_final, dwa, dwb):
    def body(f1, mi, mo, f2, fl, wa_ref, wb_ref, v_ref, k_ref):
        for dst, (ref, row) in enumerate(((f1, 0), (mi, 0), (mo, 0), (mo, 1), (mo, 2), (f2, 0), (fl, 0), (fl, 1))):
            v_ref[dst:dst + 1, :] = ref[row:row + 1, :]
        for k in range(NDEV):
            k_ref[k, 0:32, :] = wa_ref[:, k * LANE:(k + 1) * LANE]
            k_ref[k, 32:40, :] = wb_ref[:, k * LANE:(k + 1) * LANE]

    return pl.pallas_call(
        body, name="pack_small",
        out_shape=(jax.ShapeDtypeStruct((8, D), F32), jax.ShapeDtypeStruct((NDEV, 40, LANE), F32)),
    )(s_ffn1, s_in, s_mix, s_ffn2, s_final, dwa, dwb)


def _sum_small(vecs, convs):
    def body(v_ref, k_ref, vs_ref, ks_ref, l_ref):
        s, c = v_ref[0], k_ref[0]
        for k in range(1, NDEV):
            s = s + v_ref[k]
            c = c + k_ref[k]
        vs_ref[...] = s
        ks_ref[...] = c
        l_ref[...] = jnp.broadcast_to(jnp.sum(s[7:8, :], axis=-1, keepdims=True), (8, LANE))

    return pl.pallas_call(
        body, name="sum_small",
        out_shape=(jax.ShapeDtypeStruct((8, D), F32), jax.ShapeDtypeStruct((40, LANE), F32),
                   jax.ShapeDtypeStruct((8, LANE), F32)),
    )(vecs, convs)


def _adam(gs, ws, ms, vs, name, comm=None):
    n = len(gs)
    rows, cols = ws[0].shape
    tr = _row_tile(rows, 256, 8)
    c1 = 1.0 - ADAM_B1 ** ADAM_STEP
    c2 = 1.0 - ADAM_B2 ** ADAM_STEP

    def body(*refs):
        for i in range(n):
            g, w, m, v = (refs[4 * i + k][...] for k in range(4))
            d_ref, m_ref, v_ref = refs[4 * n + 3 * i: 4 * n + 3 * i + 3]
            m2 = ADAM_B1 * m + (1.0 - ADAM_B1) * g
            v2 = ADAM_B2 * v + (1.0 - ADAM_B2) * (g * g)
            d_ref[...] = -ADAM_LR * ((m2 / c1) / (jnp.sqrt(v2 / c2) + ADAM_EPS) + ADAM_WD * w)
            m_ref[...] = m2
            v_ref[...] = v2

    spec = pl.BlockSpec((tr, cols), lambda i: (i, 0))
    args = []
    for i in range(n):
        args += [gs[i], ws[i], ms[i], vs[i]]
    outs = _call(body, name=name, grid=(rows // tr,), args=args, comm=comm, in_specs=[spec] * (4 * n),
                 out_shape=[jax.ShapeDtypeStruct((rows, cols), F32)] * (3 * n), out_specs=[spec] * (3 * n))
    return [tuple(outs[3 * i: 3 * i + 3]) for i in range(n)], outs[3 * n:]


def kernel(x, ffn1_norm, ffn1_w_gate, ffn1_w_up, ffn1_w_down, mix_norm, w_in, a_dw_w, a_dw_b, a_ln_g, a_ln_b, a_w_out, b_conv_w, b_w_out, w_o, ffn2_norm, ffn2_w_gate, ffn2_w_up, ffn2_w_down, final_norm, loss_target, m_ffn1_norm, m_ffn1_w_gate, m_ffn1_w_up, m_ffn1_w_down, m_mix_norm, m_w_in, m_a_dw_w, m_a_dw_b, m_a_ln_g, m_a_ln_b, m_a_w_out, m_b_conv_w, m_b_w_out, m_w_o, m_ffn2_norm, m_ffn2_w_gate, m_ffn2_w_up, m_ffn2_w_down, m_final_norm, v_ffn1_norm, v_ffn1_w_gate, v_ffn1_w_up, v_ffn1_w_down, v_mix_norm, v_w_in, v_a_dw_w, v_a_dw_b, v_a_ln_g, v_a_ln_b, v_a_w_out, v_b_conv_w, v_b_w_out, v_w_o, v_ffn2_norm, v_ffn2_w_gate, v_ffn2_w_up, v_ffn2_w_down, v_final_norm):
    names = ("ffn1_norm", "ffn1_w_gate", "ffn1_w_up", "ffn1_w_down", "mix_norm", "w_in", "a_dw_w", "a_dw_b",
             "a_ln_g", "a_ln_b", "a_w_out", "b_conv_w", "b_w_out", "w_o", "ffn2_norm", "ffn2_w_gate", "ffn2_w_up",
             "ffn2_w_down", "final_norm")
    w = dict(ffn1_norm=ffn1_norm, ffn1_w_gate=ffn1_w_gate, ffn1_w_up=ffn1_w_up, ffn1_w_down=ffn1_w_down,
             mix_norm=mix_norm, w_in=w_in, a_dw_w=a_dw_w, a_dw_b=a_dw_b, a_ln_g=a_ln_g, a_ln_b=a_ln_b,
             a_w_out=a_w_out, b_conv_w=b_conv_w, b_w_out=b_w_out, w_o=w_o, ffn2_norm=ffn2_norm,
             ffn2_w_gate=ffn2_w_gate, ffn2_w_up=ffn2_w_up, ffn2_w_down=ffn2_w_down, final_norm=final_norm)
    m = dict(ffn1_norm=m_ffn1_norm, ffn1_w_gate=m_ffn1_w_gate, ffn1_w_up=m_ffn1_w_up, ffn1_w_down=m_ffn1_w_down,
             mix_norm=m_mix_norm, w_in=m_w_in, a_dw_w=m_a_dw_w, a_dw_b=m_a_dw_b, a_ln_g=m_a_ln_g, a_ln_b=m_a_ln_b,
             a_w_out=m_a_w_out, b_conv_w=m_b_conv_w, b_w_out=m_b_w_out, w_o=m_w_o, ffn2_norm=m_ffn2_norm,
             ffn2_w_gate=m_ffn2_w_gate, ffn2_w_up=m_ffn2_w_up, ffn2_w_down=m_ffn2_w_down, final_norm=m_final_norm)
    v = dict(ffn1_norm=v_ffn1_norm, ffn1_w_gate=v_ffn1_w_gate, ffn1_w_up=v_ffn1_w_up, ffn1_w_down=v_ffn1_w_down,
             mix_norm=v_mix_norm, w_in=v_w_in, a_dw_w=v_a_dw_w, a_dw_b=v_a_dw_b, a_ln_g=v_a_ln_g, a_ln_b=v_a_ln_b,
             a_w_out=v_a_w_out, b_conv_w=v_b_conv_w, b_w_out=v_b_w_out, w_o=v_w_o, ffn2_norm=v_ffn2_norm,
             ffn2_w_gate=v_ffn2_w_gate, ffn2_w_up=v_ffn2_w_up, ffn2_w_down=v_ffn2_w_down, final_norm=v_final_norm)
    flat = _pack_weights(dict(wg1=ffn1_w_gate[0].T, wu1=ffn1_w_up[0].T, wd1=ffn1_w_down[0], wg2=ffn2_w_gate[0].T,
                              wu2=ffn2_w_up[0].T, wd2=ffn2_w_down[0], win=w_in[0], wa=a_w_out[0], wb=b_w_out[0],
                              wo=w_o[0]))
    cw_shard = jnp.concatenate([a_dw_w[0], jnp.zeros((1, LANE), F32), b_conv_w[0], jnp.zeros((5, LANE), F32)], axis=0)

    x2, tgt = x[0], loss_target[0]
    st_a, st_b, st_c, st_d, st_e = ("wg1", "wu1", "wd1"), ("win",), ("wa", "wb", "wo", "wg2"), ("wu2",), ("wd2",)

    buf_a, cw = _run_comm(_join(_ag_comm(st_a, flat), _direct_comm(cw_shard, False)), "ag_ffn1")
    h1, n1, gg1, uu1, buf_b = _ffn_fwd(x2, ffn1_norm, (buf_a,) * 3, (0, F, 2 * F), "ffn1_fwd", _ag_comm(st_b, flat))
    u, z, buf_c = _mix_in(h1, mix_norm, buf_b, _ag_comm(st_c, flat))
    dft = _dft_constants()
    cw = jnp.transpose(cw, (1, 0, 2)).reshape(40, D)
    a1, q, buf_d = _conv_fwd_dft(z, cw, a_dw_b, dft, _ag_comm(st_d, flat))
    h2, ya, yb, buf_e = _mix_out(a1, q, z, h1, a_ln_g, a_ln_b, buf_c, _ag_comm(st_e, flat))
    ffn2_bufs, ffn2_offs = (buf_c, buf_d, buf_e), (3 * D, 0, 0)
    dh3, s_final, n2, gg2, uu2 = _ffn_fwd(h2, ffn2_norm, ffn2_bufs, ffn2_offs, "ffn2_fwd",
                                          final=(final_norm.reshape(1, D), tgt))

    tr_f = F // 2 if (F // 2) % LANE == 0 else F
    def pair(stage, src):
        return _rs_pair_comm(stage, src)

    def chip(stage, src, pair_buf, tag):
        return _rs_chip_comm(_pair_add(stage, src, pair_buf, "pair_add_" + tag))

    dgu2, act2 = _ffn_bwd_hidden(dh3, gg2, uu2, buf_e, 0, "ffn2_bwd_h")
    (gu2,) = _tn_matmul(dgu2, n2, tr_f, "dw_gu2")
    s2a, src2a = ("wg2", "wu2"), dict(wg2=(gu2, 0), wu2=(gu2, F))
    gd2, pair2a = _tn_matmul(act2, dh3, tr_f, "dw_d2", pair(s2a, src2a), scale=0.5)
    s2b, src2b = ("wd2",), dict(wd2=(gd2, 0))
    dh2, s_ffn2, recv2a, pair2b = _ffn_bwd_input(dgu2, dh3, h2, ffn2_norm, (buf_c, buf_d), (3 * D, 0), "ffn2_bwd_x",
                                                 _join(chip(s2a, src2a, pair2a, "2a"), pair(s2b, src2b)))
    dzg, da1, dq, mb, a3b, dya, dyb, s_mix, recv2b = _mix_out_bwd(dh2, ya, yb, z, a1, a_ln_g, a_ln_b, buf_c,
                                                                   chip(s2b, src2b, pair2b, "2b"))
    (go,) = _tn_matmul(mb, dh2, D, "dw_o")
    (ga,) = _tn_matmul(a3b, dya, D, "dw_a")
    (gb,) = _tn_matmul(q.reshape(1, T, D), dyb, D, "dw_b")
    ssq, srcsq = ("wa", "wb", "wo"), dict(wa=(ga, 0), wb=(gb, 0), wo=(go, 0))
    dz, dwa, dwb, pairsq = _conv_bwd_dft(z, da1, dq, dzg, cw, dft, pair(ssq, srcsq))
    gin, recvsq = _tn_matmul(dz, u, D, "dw_in", chip(ssq, srcsq, pairsq, "sq"))
    sin_a, sin_b, srcin = ("win/0/2",), ("win/1/2",), {"win/0/2": (gin, 0), "win/1/2": (gin, 0)}
    dh1, s_in, pairin_a, pairin_b = _mix_in_bwd(dz, dh2, h1, mix_norm, buf_b,
                                                _join(pair(sin_a, srcin), pair(sin_b, srcin)))
    dgu1, act1, recvin_a = _ffn_bwd_hidden(dh1, gg1, uu1, buf_a, 2 * F, "ffn1_bwd_h",
                                           chip(sin_a, srcin, pairin_a, "in_a"))
    gu1, recvin_b = _tn_matmul(dgu1, n1, tr_f, "dw_gu1", chip(sin_b, srcin, pairin_b, "in_b"))
    s1a, src1a = ("wg1", "wu1"), dict(wg1=(gu1, 0), wu1=(gu1, F))
    gd1, pair1a = _tn_matmul(act1, dh1, tr_f, "dw_d1", pair(s1a, src1a), scale=0.5)
    s1b, src1b = ("wd1",), dict(wd1=(gd1, 0))
    dx, s_ffn1, recv1a, pair1b = _ffn_bwd_input(dgu1, dh1, x2, ffn1_norm, (buf_a, buf_a), (0, F), "ffn1_bwd_x",
                                                _join(chip(s1a, src1a, pair1a, "1a"), pair(s1b, src1b)))
    (recv1b,) = _run_comm(chip(s1b, src1b, pair1b, "1b"), "rs_chip_1b")
    stages = ((s2a, recv2a, "2a"), (s2b, recv2b, "2b"), (ssq, recvsq, "sq"), (sin_a, recvin_a, "in_a"),
              (sin_b, recvin_b, "in_b"), (s1a, recv1a, "1a"), (s1b, recv1b, "1b"))

    gsum = {}
    for stage, recv, tag in stages:
        st, total = _Stage(stage), _sum_slots(recv, "sum_" + tag)
        for n in stage:
            gsum[n] = total[st.off[n]:st.off[n] + st.rows[n]]
    gsum["win"] = jnp.concatenate([gsum["win/0/2"], gsum["win/1/2"]], axis=0)

    vec8, convk = _pack_small(s_ffn1, s_in, s_mix, s_ffn2, s_final, dwa, dwb)
    vec_all, conv_all = _run_comm(_join(_direct_comm(vec8, False), _direct_comm(convk, True)), "xchg_small")
    vec_sum, conv_sum, loss_blk = _sum_small(vec_all, conv_all)
    loss = loss_blk[0, 0]

    g = dict(ffn1_w_gate=gsum["wg1"], ffn1_w_up=gsum["wu1"], ffn1_w_down=gsum["wd1"],
             ffn2_w_gate=gsum["wg2"], ffn2_w_up=gsum["wu2"], ffn2_w_down=gsum["wd2"], w_in=gsum["win"].T,
             a_w_out=gsum["wa"], b_w_out=gsum["wb"], w_o=gsum["wo"],
             ffn1_norm=vec_sum[0:1], mix_norm=vec_sum[1:2], a_ln_g=vec_sum[2:3], a_ln_b=vec_sum[3:4],
             a_dw_b=vec_sum[4:5], ffn2_norm=vec_sum[5:6], final_norm=vec_sum[6:7],
             a_dw_w=conv_sum[0:KA], b_conv_w=conv_sum[32:32 + KB])
    gate_up = ("ffn1_w_gate", "ffn1_w_up", "ffn2_w_gate", "ffn2_w_up")

    upd = {}

    def run(group, name, as2d=lambda a: a[0], back=lambda a, n: a.reshape(w[n].shape)):
        res, _ = _adam([g[n] for n in group], [as2d(w[n]) for n in group], [as2d(m[n]) for n in group],
                       [as2d(v[n]) for n in group], name)
        for n, r in zip(group, res):
            upd[n] = tuple(back(a, n) for a in r)

    run(gate_up, "adam_gate_up", as2d=lambda a: a[0].T, back=lambda a, n: a.T[None])
    for n in gate_up:
        g[n] = g[n].T
    run(("ffn1_w_down", "ffn2_w_down"), "adam_down")
    run(("w_in",), "adam_in")
    run(("a_w_out", "b_w_out", "w_o"), "adam_square")
    run(("a_dw_w",), "adam_dw")
    run(("b_conv_w",), "adam_conv")
    vecs = ("ffn1_norm", "mix_norm", "a_dw_b", "a_ln_g", "a_ln_b", "ffn2_norm", "final_norm")
    run(vecs, "adam_vec", as2d=lambda a: a.reshape(1, D))

    grads = [g[n].reshape(w[n].shape) for n in names]
    return (loss, dx.reshape(x.shape), *grads, *[upd[n][0] for n in names], *[upd[n][1] for n in names],
            *[upd[n][2] for n in names])
```

```python
import jax
import jax.numpy as jnp
from jax import lax
from jax.experimental import pallas as pl
from jax.experimental.pallas import tpu as pltpu

T = 4096
D = 1024
F = 2816
NG = 7
NDEV = 8
NCHIP = 4
KA, KB = 31, 3
EPS = 1e-6
ADAM_LR, ADAM_B1, ADAM_B2, ADAM_EPS, ADAM_WD, ADAM_STEP = 0.001, 0.9, 0.999, 1e-08, 0.01, 10

TM = 512
FC = 256
TB = 1024
NB = 256
HB = NB // 2
CW = 256
CHB = 64
LANE = 128
TK = 2048
VMEM_LIMIT = 56 * 1024 * 1024

BF = jnp.bfloat16
F32 = jnp.float32
MESH = pl.DeviceIdType.MESH
ANY = pl.BlockSpec(memory_space=pl.ANY)

ORDER = ("wg1", "wu1", "wd1", "wg2", "wu2", "wd2", "win", "wa", "wb", "wo")


class _Layout:
    def __init__(self):
        fs, dis, ds = F // NDEV, NG * D // NDEV, D // NDEV
        self.rows = dict(wg1=fs, wu1=fs, wd1=fs, wg2=fs, wu2=fs, wd2=fs, win=dis, wa=ds, wb=ds, wo=ds)
        self.fl, off = {}, 0
        for n in ORDER:
            self.fl[n] = off
            off += self.rows[n]
        self.RT = off


class _Stage:
    def __init__(self, names):
        lay = _Layout()
        self.names = names
        self.rows, self.full, self.sub, self.fl = {}, {}, {}, {}
        for n in names:
            base, i, k = (n.split("/") + ["0", "1"])[:3]
            self.full[n] = lay.rows[base]
            self.rows[n] = lay.rows[base] // int(k)
            self.sub[n] = int(i) * self.rows[n]
            self.fl[n] = lay.fl[base] + self.sub[n]
        self.off, self.wc, o, w = {}, {}, 0, 0
        for n in names:
            self.off[n], self.wc[n] = o, w
            o += self.rows[n]
            w += NDEV * self.rows[n]
        self.R, self.W = o, w

    def grad_row(self, n, first, dev_lin):
        return first + dev_lin * self.full[n] + self.sub[n]


def _nt(a, b):
    return lax.dot_general(a, b, (((1,), (1,)), ((), ())), preferred_element_type=F32)


def _nn(a, b):
    return lax.dot_general(a, b, (((1,), (0,)), ((), ())), preferred_element_type=F32)


def _tn(a, b):
    return lax.dot_general(a, b, (((0,), (0,)), ((), ())), preferred_element_type=F32)


def _sig(x):
    return 1.0 / (1.0 + jnp.exp(-x))


def _position():
    return lax.axis_index("x"), lax.axis_index("y"), lax.axis_index("c")


def _peer(pos, j):
    x, y, c = pos
    return (1 - x if j & 4 else x, 1 - y if j & 2 else y, 1 - c if j & 1 else c)


def _lin(pos):
    return 4 * pos[0] + 2 * pos[1] + pos[2]


def _chip(pos):
    return 2 * pos[0] + pos[1]


class _Comm:
    def __init__(self, inputs, out_shapes, scratch, start, finish, middle=None):
        self.inputs, self.out_shapes, self.scratch = inputs, out_shapes, scratch
        self.start, self.finish, self.middle = start, finish, middle


def _call(body, *, name, grid, args, in_specs, out_shape, out_specs, scratch_shapes=(), comm=None,
          num_scalar_prefetch=0):
    in_specs, out_shape, out_specs, scratch_shapes = list(in_specs), list(out_shape), list(out_specs), list(scratch_shapes)
    n_in, n_out, n_scr = len(in_specs), len(out_shape), len(scratch_shapes)
    sp = num_scalar_prefetch
    if comm is None:
        kernel_fn = lambda *refs: body(*refs)
        c_in = c_out = c_scr = 0
    else:
        c_in, c_out, c_scr = len(comm.inputs), len(comm.out_shapes), len(comm.scratch)

        def kernel_fn(*refs):
            pre, refs = refs[:sp], refs[sp:]
            ins, cins = refs[:n_in], refs[n_in:n_in + c_in]
            o0 = n_in + c_in
            outs, couts = refs[o0:o0 + n_out], refs[o0 + n_out:o0 + n_out + c_out]
            s0 = o0 + n_out + c_out
            scr, cscr = refs[s0:s0 + n_scr], refs[s0 + n_scr:]
            step, steps = pl.program_id(0), grid[0]
            for a in range(1, len(grid)):
                step, steps = step * grid[a] + pl.program_id(a), steps * grid[a]
            first, last = step == 0, step == steps - 1

            @pl.when(first)
            def _():
                comm.start(cins, couts, cscr)

            if comm.middle is not None:
                @pl.when(step == (steps // 2 if steps > 2 else steps - 1))
                def _():
                    comm.middle(cins, couts, cscr)

            body(*pre, *ins, *outs, *scr)

            @pl.when(last)
            def _():
                comm.finish(cins, couts, cscr)

        args = list(args) + list(comm.inputs)
        in_specs += [ANY] * c_in
        out_shape += list(comm.out_shapes)
        out_specs += [ANY] * c_out
        scratch_shapes += list(comm.scratch)
    params = pltpu.CompilerParams(dimension_semantics=("arbitrary",) * len(grid), vmem_limit_bytes=VMEM_LIMIT)
    if sp:
        grid_spec = pltpu.PrefetchScalarGridSpec(num_scalar_prefetch=sp, grid=grid, in_specs=in_specs,
                                                 out_specs=out_specs, scratch_shapes=scratch_shapes)
        return pl.pallas_call(kernel_fn, name=name, grid_spec=grid_spec, out_shape=out_shape,
                              compiler_params=params)(*args)
    return pl.pallas_call(kernel_fn, name=name, grid=grid, in_specs=in_specs, out_shape=out_shape, out_specs=out_specs,
                          scratch_shapes=scratch_shapes, compiler_params=params)(*args)


def _join(a, b):
    na = (len(a.inputs), len(a.out_shapes), len(a.scratch))

    def split(refs):
        return ([r[:n] for r, n in zip(refs, na)], [r[n:] for r, n in zip(refs, na)])

    def start(*refs):
        ra, rb = split(refs)
        a.start(*ra)
        b.start(*rb)

    def finish(*refs):
        ra, rb = split(refs)
        a.finish(*ra)
        b.finish(*rb)

    def middle(*refs):
        for stage, r in zip((a, b), split(refs)):
            if stage.middle is not None:
                stage.middle(*r)

    return _Comm(list(a.inputs) + list(b.inputs), list(a.out_shapes) + list(b.out_shapes),
                 list(a.scratch) + list(b.scratch), start, finish,
                 middle if (a.middle is not None or b.middle is not None) else None)


def _run_comm(comm, name):
    def body(*refs):
        c_in, c_out = len(comm.inputs), len(comm.out_shapes)
        parts = (refs[:c_in], refs[c_in:c_in + c_out], refs[c_in + c_out:])
        comm.start(*parts)
        if comm.middle is not None:
            comm.middle(*parts)
        comm.finish(*parts)

    return pl.pallas_call(
        body, name=name, out_shape=list(comm.out_shapes), in_specs=[ANY] * len(comm.inputs),
        out_specs=[ANY] * len(comm.out_shapes), scratch_shapes=list(comm.scratch))(*comm.inputs)


def _ag_comm(names, flat):
    st = _Stage(names)

    def ring(me):
        x, y, c = me
        diagonal = x == y
        up = (jnp.where(diagonal, x, 1 - x), jnp.where(diagonal, 1 - y, y), c)
        down = (jnp.where(diagonal, 1 - x, x), jnp.where(diagonal, y, 1 - y), c)
        low = c == 0
        passed = tuple(jnp.where(low, d, u) for d, u in zip(down, up))
        target = tuple(jnp.where(low, u, d) for d, u in zip(down, up))
        return up, down, (1 - x, 1 - y, c), passed, target

    def parts(refs):
        (flat_ref,), (out_ref,), (send_sems, recv_sems, local_sem) = refs
        me = _position()

        def region(name, dev):
            r = st.rows[name]
            return out_ref.at[pl.ds(st.wc[name] + _lin(dev) * r, r), :]

        def own(name):
            return flat_ref.at[pl.ds(st.fl[name], st.rows[name]), :]

        def copies(k, dev, to, from_flat):
            return [pltpu.make_async_remote_copy(
                src_ref=own(n) if from_flat else region(n, dev), dst_ref=region(n, dev), send_sem=send_sems.at[k],
                recv_sem=recv_sems.at[k], device_id=to, device_id_type=MESH) for n in names]

        def whole(k):
            return pltpu.make_async_remote_copy(
                src_ref=flat_ref.at[pl.ds(0, st.R), :], dst_ref=out_ref.at[pl.ds(0, st.R), :],
                send_sem=send_sems.at[k], recv_sem=recv_sems.at[k], device_id=me, device_id_type=MESH)

        return me, region, own, copies, whole, flat_ref, out_ref, local_sem

    def start(*refs):
        me, region, own, copies, _, _, _, local_sem = parts(refs)
        for n in names:
            pltpu.make_async_copy(own(n), region(n, me), local_sem).start()
        up, down, _, _, _ = ring(me)
        for k, to in ((1, up), (2, down), (0, _peer(me, 1))):
            for cp in copies(k, me, to, True):
                cp.start()

    def middle(*refs):
        me, _, _, copies, whole, _, _, _ = parts(refs)
        up, down, _, passed, target = ring(me)
        sib = _peer(me, 1)
        whole(1).wait_recv()
        whole(2).wait_recv()
        for k, dev, to in ((3, passed, target), (4, down, sib), (5, up, sib)):
            for cp in copies(k, dev, to, False):
                cp.start()

    def finish(*refs):
        me, _, _, copies, whole, flat_ref, out_ref, local_sem = parts(refs)
        _, _, across, _, _ = ring(me)
        whole(3).wait_recv()
        for cp in copies(6, across, _peer(me, 1), False):
            cp.start()
        whole(0).wait_recv()
        for j in range(3):
            whole(4 + j).wait_recv()
        for k in range(7):
            whole(k).wait_send()
        pltpu.make_async_copy(flat_ref.at[pl.ds(0, st.R), :], out_ref.at[pl.ds(0, st.R), :], local_sem).wait()

    return _Comm([flat], [jax.ShapeDtypeStruct((st.W, D), BF)],
                 [pltpu.SemaphoreType.DMA((7,)), pltpu.SemaphoreType.DMA((7,)), pltpu.SemaphoreType.DMA],
                 start, finish, middle)


def _rs_pair_comm(names, src):
    st = _Stage(names)
    arrays = []
    for n in names:
        if not any(src[n][0] is a for a in arrays):
            arrays.append(src[n][0])
    idx = {n: [i for i, a in enumerate(arrays) if a is src[n][0]][0] for n in names}

    def slot_wait(refs):
        recv = refs[1][0]
        send_sem, recv_sem = refs[2]
        return pltpu.make_async_remote_copy(src_ref=recv, dst_ref=recv, send_sem=send_sem, recv_sem=recv_sem,
                                            device_id=_position(), device_id_type=MESH)

    def start(*refs):
        ins, (recv,), (send_sem, recv_sem) = refs
        me = _position()
        sib = _peer(me, 1)
        for q in range(NCHIP):
            dev = (q // 2, q % 2, sib[2])
            for n in names:
                r = st.rows[n]
                pltpu.make_async_remote_copy(
                    src_ref=ins[idx[n]].at[pl.ds(st.grad_row(n, src[n][1], _lin(dev)), r), :],
                    dst_ref=recv.at[q, pl.ds(st.off[n], r), :], send_sem=send_sem, recv_sem=recv_sem,
                    device_id=sib, device_id_type=MESH).start()

    def finish(*refs):
        w = slot_wait(refs)
        w.wait_recv()
        w.wait_send()

    return _Comm(arrays, [jax.ShapeDtypeStruct((NCHIP, st.R, D), BF)],
                 [pltpu.SemaphoreType.DMA, pltpu.SemaphoreType.DMA], start, finish)


def _pair_add(names, src, recv, name):
    st = _Stage(names)
    c_arr = jnp.reshape(lax.axis_index("c"), (1,)).astype(jnp.int32)

    def body(c_ref, *refs):
        r_ref, o_ref = refs[len(names)], refs[len(names) + 1]
        for a_ref, n in zip(refs, names):
            rows = slice(st.off[n], st.off[n] + st.rows[n])
            o_ref[rows, :] = (a_ref[...].astype(F32) + r_ref[rows, :].astype(F32)).astype(BF)

    def shard_spec(n):
        r = st.rows[n]
        base, step = st.grad_row(n, src[n][1], 0) // r, st.full[n] // r
        return pl.BlockSpec((r, D), lambda q, c_ref: (base + step * (2 * q + c_ref[0]), 0))

    slot = pl.BlockSpec((None, st.R, D), lambda q, c_ref: (q, 0, 0))
    return _call(body, name=name, grid=(NCHIP,), args=[c_arr] + [src[n][0] for n in names] + [recv],
                 in_specs=[shard_spec(n) for n in names] + [slot],
                 out_shape=[jax.ShapeDtypeStruct((NCHIP, st.R, D), BF)], out_specs=[slot], num_scalar_prefetch=1)[0]


def _rs_chip_comm(part):
    def copies(refs):
        (p_ref,), (recv,), (send_sems, recv_sems, local_sem) = refs
        me = _position()
        mine = pltpu.make_async_copy(p_ref.at[_chip(me)], recv.at[_chip(me)], local_sem)
        out = []
        for j, bits in enumerate((4, 2, 6)):
            to = _peer(me, bits)
            out.append(pltpu.make_async_remote_copy(
                src_ref=p_ref.at[_chip(to)], dst_ref=recv.at[_chip(me)], send_sem=send_sems.at[j],
                recv_sem=recv_sems.at[j], device_id=to, device_id_type=MESH))
        return mine, out

    def start(*refs):
        mine, out = copies(refs)
        mine.start()
        for cp in out:
            cp.start()

    def finish(*refs):
        mine, out = copies(refs)
        for cp in out:
            cp.wait_recv()
        for cp in out:
            cp.wait_send()
        mine.wait()

    return _Comm([part], [jax.ShapeDtypeStruct(part.shape, BF)],
                 [pltpu.SemaphoreType.DMA((3,)), pltpu.SemaphoreType.DMA((3,)), pltpu.SemaphoreType.DMA],
                 start, finish)


def _direct_comm(x, scatter):
    def copies(refs):
        (x_ref,), (out_ref,), (send_sems, recv_sems, local_sem) = refs
        me = _position()

        def piece(dev):
            return x_ref.at[_lin(dev)] if scatter else x_ref

        mine = pltpu.make_async_copy(piece(me), out_ref.at[_lin(me)], local_sem)
        return mine, [pltpu.make_async_remote_copy(
            src_ref=piece(_peer(me, j)), dst_ref=out_ref.at[_lin(me)], send_sem=send_sems.at[j - 1],
            recv_sem=recv_sems.at[j - 1], device_id=_peer(me, j), device_id_type=MESH) for j in range(1, NDEV)]

    def start(*refs):
        mine, cps = copies(refs)
        mine.start()
        for cp in cps:
            cp.start()

    def finish(*refs):
        mine, cps = copies(refs)
        for cp in cps:
            cp.wait_recv()
        for cp in cps:
            cp.wait_send()
        mine.wait()

    shape = x.shape if scatter else (NDEV,) + x.shape
    return _Comm([x], [jax.ShapeDtypeStruct(shape, x.dtype)],
                 [pltpu.SemaphoreType.DMA((7,)), pltpu.SemaphoreType.DMA((7,)), pltpu.SemaphoreType.DMA],
                 start, finish)


def _pack_weights(shards):
    lay = _Layout()

    def body(*refs):
        o_ref = refs[-1]
        for ref, n in zip(refs, ORDER):
            x = ref[...].T if n == "win" else ref[...]
            o_ref[lay.fl[n]:lay.fl[n] + lay.rows[n], :] = x.astype(BF)

    return pl.pallas_call(
        body, name="pack_weights", out_shape=jax.ShapeDtypeStruct((lay.RT, D), BF),
        compiler_params=pltpu.CompilerParams(vmem_limit_bytes=VMEM_LIMIT))(*[shards[n] for n in ORDER])


def _load_ffn_weights(srcs, offs, scratch, sem):
    @pl.when(pl.program_id(0) == 0)
    def _():
        cps = [pltpu.make_async_copy(s.at[pl.ds(off, dst.shape[0]), :], dst, sem.at[i])
               for i, (s, off, dst) in enumerate(zip(srcs, offs, scratch))]
        for cp in cps:
            cp.start()
        for cp in cps:
            cp.wait()


def _final_loss_tile(xf, g, tgt, s_ref):
    r = lax.rsqrt(jnp.mean(xf * xf, axis=-1, keepdims=True) + EPS)
    xr = xf * r
    e = xr * g - tgt
    s_ref[1:2, :] += jnp.sum(e * e, axis=0, keepdims=True) * (0.5 / D)
    dy = e * (1.0 / D)
    s_ref[0:1, :] += jnp.sum(dy * xr, axis=0, keepdims=True)
    gdy = dy * g
    return r * gdy - xr * (r * jnp.mean(gdy * xr, axis=-1, keepdims=True))


def _ffn_fwd(x, g, wbufs, offs, name, comm=None, final=None):
    nf = F // FC

    def body(x_ref, g_ref, b0, b1, b2, *rest):
        if final is None:
            h_ref, n_ref, gg_ref, uu_ref, a_ref, wg_s, wu_s, wd_s, sem = rest
        else:
            gf_ref, t_ref, dh_ref, s_ref, n_ref, gg_ref, uu_ref, a_ref, wg_s, wu_s, wd_s, sem = rest

            @pl.when(pl.program_id(0) == 0)
            def _():
                s_ref[...] = jnp.zeros_like(s_ref)

        _load_ffn_weights((b0, b1, b2), offs, (wg_s, wu_s, wd_s), sem)
        xf = x_ref[...]
        r = lax.rsqrt(jnp.mean(xf * xf, axis=-1, keepdims=True) + EPS)
        nb = (xf * r * g_ref[...]).astype(BF)
        n_ref[...] = nb
        acc = jnp.zeros((TM, D), F32)
        for c in range(nf):
            sl = slice(c * FC, (c + 1) * FC)
            gb = _nt(nb, wg_s[sl, :]).astype(BF)
            ub = _nt(nb, wu_s[sl, :]).astype(BF)
            gg_ref[:, sl] = gb
            uu_ref[:, sl] = ub
            a = (gb * _sig(gb)) * ub
            a_ref[0, :, sl] = a
            acc = acc + _nn(a, wd_s[sl, :])
        h = xf + 0.5 * acc
        if final is None:
            h_ref[...] = h
        else:
            dh_ref[...] = _final_loss_tile(h, gf_ref[...], t_ref[...], s_ref)

    row = lambda i: (i, 0)
    vec = pl.BlockSpec((1, D), lambda i: (0, 0))
    tile = pl.BlockSpec((TM, D), row)
    saved_shapes = [jax.ShapeDtypeStruct((T, D), BF), jax.ShapeDtypeStruct((T, F), BF), jax.ShapeDtypeStruct((T, F), BF),
                    jax.ShapeDtypeStruct((1, T, F), BF)]
    saved_specs = [tile, pl.BlockSpec((TM, F), row), pl.BlockSpec((TM, F), row),
                   pl.BlockSpec((1, TM, F), lambda i: (0, i, 0))]
    if final is None:
        extra_args, extra_specs = [], []
        head_shapes, head_specs = [jax.ShapeDtypeStruct((T, D), F32)], [tile]
    else:
        extra_args, extra_specs = list(final), [vec, tile]
        head_shapes = [jax.ShapeDtypeStruct((T, D), F32), jax.ShapeDtypeStruct((8, D), F32)]
        head_specs = [tile, pl.BlockSpec((8, D), lambda i: (0, 0))]
    return _call(
        body, name=name, grid=(T // TM,), args=[x, g, *wbufs, *extra_args], comm=comm,
        in_specs=[tile, vec, ANY, ANY, ANY] + extra_specs,
        out_shape=head_shapes + saved_shapes, out_specs=head_specs + saved_specs,
        scratch_shapes=[pltpu.VMEM((F, D), BF)] * 3 + [pltpu.SemaphoreType.DMA((3,))])


def _mix_in(h1, gm, win, comm=None):
    def body(h_ref, g_ref, w_any, u_ref, z_ref, w_s, sem):
        _load_ffn_weights((w_any,), (0,), (w_s,), sem)
        xf = h_ref[...]
        r = lax.rsqrt(jnp.mean(xf * xf, axis=-1, keepdims=True) + EPS)
        ub = (xf * r * g_ref[...]).astype(BF)
        u_ref[...] = ub
        for j in range(NG):
            z_ref[j] = _nt(ub, w_s[j * D:(j + 1) * D, :]).astype(BF)

    row = lambda i: (i, 0)
    return _call(
        body, name="mix_in", grid=(T // TM,), args=[h1, gm, win], comm=comm,
        in_specs=[pl.BlockSpec((TM, D), row), pl.BlockSpec((1, D), lambda i: (0, 0)), ANY],
        out_shape=[jax.ShapeDtypeStruct((T, D), BF), jax.ShapeDtypeStruct((NG, T, D), BF)],
        out_specs=[pl.BlockSpec((TM, D), row), pl.BlockSpec((NG, TM, D), lambda i: (0, i, 0))],
        scratch_shapes=[pltpu.VMEM((NG * D, D), BF), pltpu.SemaphoreType.DMA((1,))])


def _shift_up(w, b):
    return w if b == 0 else pltpu.roll(w, w.shape[0] - b, 0)


def _fold8(p):
    red = p[0:8, :]
    for i in range(1, p.shape[0] // 8):
        red = red + p[8 * i:8 * i + 8, :]
    return red


def _dft_constants():
    import numpy as np
    nh = NB // 2
    f, n = np.arange(nh)[:, None], np.arange(NB)[None, :]
    ang = 2.0 * np.pi / NB * f * n
    fc = np.cos(ang)
    fs = np.where(f == 0, (-1.0) ** n, np.sin(ang))
    scale = np.where(f == 0, 1.0, 2.0) / NB
    ic = (scale * np.cos(ang)).T
    isn = np.where(f == 0, (-1.0) ** n / NB, scale * np.sin(ang)).T
    d = (KA - 1 - np.arange(32))[None, :]
    valid = (np.arange(32) < KA)[None, :]
    angk = 2.0 * np.pi / NB * f * d
    kc = np.where(valid, np.cos(angk), 0.0)
    ks = np.where(valid, np.sin(angk), 0.0)
    k2 = np.where(valid, np.where(f == 0, (-1.0) ** d, np.cos(angk)), 0.0)
    rtc = np.where(valid, scale * np.cos(angk), 0.0).T
    rts = np.where(valid, np.where(f == 0, (-1.0) ** d / NB, scale * np.sin(angk)), 0.0).T

    def bf(a):
        return jnp.asarray(a, F32).astype(BF)

    def split(a):
        hi = bf(a)
        return hi, (jnp.asarray(a, F32) - hi.astype(F32)).astype(BF)

    return dict(fc=bf(fc), fs=bf(fs), ic_hi=bf(ic[HB:]), is_hi=bf(isn[HB:]), ic_lo=bf(ic[:HB]), is_lo=bf(isn[:HB]),
                kc=split(kc), ks=split(ks), k2=split(k2), rtc=split(rtc), rts=split(rts))


def _dot3(m_hi, m_lo, x):
    x_hi = x.astype(BF)
    x_lo = (x - x_hi.astype(F32)).astype(BF)
    return _nn(m_hi, x_hi) + _nn(m_hi, x_lo) + _nn(m_lo, x_hi)


def _whole(a):
    return pl.BlockSpec(a.shape, lambda c, t: (0,) * a.ndim)


def _filter_spectrum(cw_ref, tabs, hc, hs, h2):
    w32 = cw_ref[0:32, :]
    for (hi, lo), dst in zip(tabs, (hc, hs, h2)):
        dst[...] = _dot3(hi[...], lo[...], w32)


def _conv_fwd_dft(z, cw, bias, dft, comm=None):
    nt = T // TB
    hb = TB // HB

    def body(z_ref, zh_ref, cw_ref, b_ref, fc_ref, fs_ref, ic_ref, is_ref, kch, kcl, ksh, ksl, k2h, k2l,
             a1_ref, q_ref, aext, ppad, hc, hs, h2):
        first = pl.program_id(1) == 0
        f = lambda ref, j: ref[j].astype(F32)

        @pl.when(first)
        def _():
            _filter_spectrum(cw_ref, ((kch, kcl), (ksh, ksl), (k2h, k2l)), hc, hs, h2)

        aext[0:HB, :] = jnp.where(first, 0.0, f(zh_ref, 0) * _sig(f(zh_ref, 1))).astype(BF)
        aext[HB:, :] = (f(z_ref, 0) * _sig(f(z_ref, 1))).astype(BF)
        ppad[0:8, :] = jnp.where(first, 0.0, f(zh_ref, 3)[HB - 8:HB, :] * f(zh_ref, 4)[HB - 8:HB, :])
        ppad[8:, :] = f(z_ref, 3) * f(z_ref, 4)
        bias_row = b_ref[...]

        for j in range(TB // HB):
            xs = aext[j * HB:j * HB + NB, :]
            xa, xb = _nn(fc_ref[...], xs), _nn(fs_ref[...], xs)
            yc = (hc[...] * xa - hs[...] * xb).astype(BF)
            ys = (h2[...] * xb + hs[...] * xa).astype(BF)
            y = _nn(ic_ref[...], yc) + _nn(is_ref[...], ys)
            a1_ref[j * HB:(j + 1) * HB, :] = (y + bias_row).astype(BF)

        def chunk(r, carry):
            base = pl.multiple_of(r * CHB, CHB)
            pw = ppad[pl.ds(base, CHB + 8), :]
            v = (cw_ref[pl.ds(32, 1), :] * _shift_up(pw, 6)[0:CHB, :]
                 + cw_ref[pl.ds(33, 1), :] * _shift_up(pw, 7)[0:CHB, :]
                 + cw_ref[pl.ds(34, 1), :] * pw[8:8 + CHB, :])
            q_ref[pl.ds(base, CHB), :] = (z_ref[2, pl.ds(base, CHB), :].astype(F32) * v).astype(BF)
            return carry

        lax.fori_loop(0, TB // CHB, chunk, 0)

    blk = pl.BlockSpec((TB, CW), lambda c, t: (t, c))
    tabs = [dft["fc"], dft["fs"], dft["ic_hi"], dft["is_hi"], *dft["kc"], *dft["ks"], *dft["k2"]]
    return _call(
        body, name="conv_fwd", grid=(D // CW, nt), comm=comm, args=[z, z, cw, bias] + tabs,
        in_specs=[pl.BlockSpec((5, TB, CW), lambda c, t: (0, t, c)),
                  pl.BlockSpec((5, HB, CW), lambda c, t: (0, jnp.maximum(t * hb - 1, 0), c)),
                  pl.BlockSpec((40, CW), lambda c, t: (0, c)), pl.BlockSpec((1, CW), lambda c, t: (0, c))]
                 + [_whole(a) for a in tabs],
        out_shape=[jax.ShapeDtypeStruct((T, D), BF), jax.ShapeDtypeStruct((T, D), BF)], out_specs=[blk, blk],
        scratch_shapes=[pltpu.VMEM((TB + HB, CW), BF), pltpu.VMEM((TB + 8, CW), F32)]
                       + [pltpu.VMEM((NB // 2, CW), F32)] * 3)


def _conv_bwd_dft(z, da1, dq, dzg, cw, dft, comm=None):
    nt = T // TB
    hb = TB // HB
    last_h = T // HB - 1

    def body(z_ref, zp_ref, zn_ref, da1_ref, da1n_ref, dq_ref, dqn_ref, dzg_ref, cw_ref,
             fc_ref, fs_ref, ic_ref, is_ref, kch, kcl, ksh, ksl, k2h, k2l, rch, rcl, rsh, rsl,
             dz_ref, dwa_ref, dwb_ref, aext, dyext, ppad, dvpad, hc, hs, h2, rc, rs, nyq, acc_b):
        t = pl.program_id(1)
        first, last = t == 0, t == nt - 1
        f = lambda ref, j: ref[j].astype(F32)

        @pl.when(first)
        def _():
            _filter_spectrum(cw_ref, ((kch, kcl), (ksh, ksl), (k2h, k2l)), hc, hs, h2)
            rc[...] = jnp.zeros_like(rc)
            rs[...] = jnp.zeros_like(rs)
            nyq[...] = jnp.zeros_like(nyq)
            acc_b[...] = jnp.zeros_like(acc_b)

        aext[0:HB, :] = jnp.where(first, 0.0, f(zp_ref, 0) * _sig(f(zp_ref, 1))).astype(BF)
        aext[HB:, :] = (f(z_ref, 0) * _sig(f(z_ref, 1))).astype(BF)
        dyext[0:TB, :] = da1_ref[...]
        dyext[TB:, :] = jnp.where(last, 0.0, da1n_ref[...].astype(F32)).astype(BF)
        ppad[0:8, :] = jnp.where(first, 0.0, f(zp_ref, 3)[HB - 8:HB, :] * f(zp_ref, 4)[HB - 8:HB, :])
        ppad[8:, :] = f(z_ref, 3) * f(z_ref, 4)
        dvpad[0:TB, :] = dq_ref[...].astype(F32) * f(z_ref, 2)
        dvpad[TB:, :] = jnp.where(last, 0.0, dqn_ref[...].astype(F32)[0:8, :] * f(zn_ref, 2)[0:8, :])

        for j in range(TB // HB):
            rows = slice(j * HB, (j + 1) * HB)
            dys = dyext[j * HB:j * HB + NB, :]
            da, db = _nn(fc_ref[...], dys), _nn(fs_ref[...], dys)
            gc = (hc[...] * da + hs[...] * db).astype(BF)
            gs = (h2[...] * db - hs[...] * da).astype(BF)
            da0 = _nn(ic_ref[...], gc) + _nn(is_ref[...], gs)
            z0, z1 = z_ref[0, rows, :].astype(F32), z_ref[1, rows, :].astype(F32)
            s1 = _sig(z1)
            dz_ref[0, rows, :] = (da0 * s1).astype(BF)
            dz_ref[1, rows, :] = (da0 * z0 * (s1 * (1.0 - s1))).astype(BF)
            xs = aext[j * HB:j * HB + NB, :]
            xa, xb = _nn(fc_ref[...], xs), _nn(fs_ref[...], xs)
            dyb = dyext[rows, :]
            pa, pb = _nn(fc_ref[:, HB:NB], dyb), _nn(fs_ref[:, HB:NB], dyb)
            rc[...] += pa * xa + pb * xb
            rs[...] += pb * xa - pa * xb
            nyq[...] += pb[0:8, :] * xb[0:8, :]

        def chunk(r, carry):
            base = pl.multiple_of(r * CHB, CHB)
            rows = pl.ds(base, CHB)
            pw = ppad[pl.ds(base, CHB + 8), :]
            p6 = _shift_up(pw, 6)[0:CHB, :]
            p7 = _shift_up(pw, 7)[0:CHB, :]
            p8 = pw[8:8 + CHB, :]
            wb0, wb1, wb2 = cw_ref[pl.ds(32, 1), :], cw_ref[pl.ds(33, 1), :], cw_ref[pl.ds(34, 1), :]
            v = wb0 * p6 + wb1 * p7 + wb2 * p8
            dz_ref[2, rows, :] = (dq_ref[rows, :].astype(F32) * v).astype(BF)
            dvw = dvpad[pl.ds(base, CHB + 8), :]
            dvc = dvw[0:CHB, :]
            dp = wb2 * dvc + wb1 * _shift_up(dvw, 1)[0:CHB, :] + wb0 * _shift_up(dvw, 2)[0:CHB, :]
            dz_ref[3, rows, :] = (dp * z_ref[4, rows, :].astype(F32)).astype(BF)
            dz_ref[4, rows, :] = (dp * z_ref[3, rows, :].astype(F32)).astype(BF)
            acc_b[0:8, :] += _fold8(dvc * p6)
            acc_b[8:16, :] += _fold8(dvc * p7)
            acc_b[16:24, :] += _fold8(dvc * p8)
            dz_ref[5, rows, :] = dzg_ref[0, rows, :]
            dz_ref[6, rows, :] = dzg_ref[1, rows, :]
            return carry

        lax.fori_loop(0, TB // CHB, chunk, 0)

        @pl.when(last)
        def _():
            row0 = lax.broadcasted_iota(jnp.int32, (NB // 2, CW), 0) == 0
            ny = jnp.broadcast_to(nyq[0:1, :], (NB // 2, CW))
            rcv = jnp.where(row0, rc[...] - ny, rc[...])
            rsv = jnp.where(row0, ny, rs[...])
            dwa_ref[...] = _dot3(rch[...], rcl[...], rcv) + _dot3(rsh[...], rsl[...], rsv)
            for k in range(KB):
                dwb_ref[k:k + 1, :] = jnp.sum(acc_b[8 * k:8 * k + 8, :], axis=0, keepdims=True)
            dwb_ref[KB:8, :] = jnp.zeros((8 - KB, CW), F32)

    blk = lambda c, t: (t, c)
    nxt = lambda c, t: (jnp.minimum((t + 1) * hb, last_h), c)
    tabs = [dft["fc"], dft["fs"], dft["ic_lo"], dft["is_lo"], *dft["kc"], *dft["ks"], *dft["k2"], *dft["rtc"], *dft["rts"]]
    return _call(
        body, name="conv_bwd", grid=(D // CW, nt), comm=comm, args=[z, z, z, da1, da1, dq, dq, dzg, cw] + tabs,
        in_specs=[pl.BlockSpec((5, TB, CW), lambda c, t: (0, t, c)),
                  pl.BlockSpec((5, HB, CW), lambda c, t: (0, jnp.maximum(t * hb - 1, 0), c)),
                  pl.BlockSpec((5, HB, CW), lambda c, t: (0, jnp.minimum((t + 1) * hb, last_h), c)),
                  pl.BlockSpec((TB, CW), blk), pl.BlockSpec((HB, CW), nxt),
                  pl.BlockSpec((TB, CW), blk), pl.BlockSpec((HB, CW), nxt),
                  pl.BlockSpec((2, TB, CW), lambda c, t: (0, t, c)),
                  pl.BlockSpec((40, CW), lambda c, t: (0, c))]
                 + [_whole(a) for a in tabs],
        out_shape=[jax.ShapeDtypeStruct((NG, T, D), BF), jax.ShapeDtypeStruct((32, D), F32),
                   jax.ShapeDtypeStruct((8, D), F32)],
        out_specs=[pl.BlockSpec((NG, TB, CW), lambda c, t: (0, t, c)),
                   pl.BlockSpec((32, CW), lambda c, t: (0, c)), pl.BlockSpec((8, CW), lambda c, t: (0, c))],
        scratch_shapes=[pltpu.VMEM((TB + HB, CW), BF), pltpu.VMEM((TB + HB, CW), BF),
                        pltpu.VMEM((TB + 8, CW), F32), pltpu.VMEM((TB + 8, CW), F32)]
                       + [pltpu.VMEM((NB // 2, CW), F32)] * 5 + [pltpu.VMEM((8, CW), F32), pltpu.VMEM((24, CW), F32)])


def _layernorm_silu(a1, lng, lnb):
    mu = jnp.mean(a1, axis=-1, keepdims=True)
    xc = a1 - mu
    rs = lax.rsqrt(jnp.mean(xc * xc, axis=-1, keepdims=True) + EPS)
    xh = xc * rs
    a2 = xh * lng + lnb
    sg = _sig(a2)
    return xh, rs, a2, sg


def _square_specs(blocks):
    return [pl.BlockSpec((D, D), lambda i, b=b: (b, 0)) for b in blocks]


def _mix_out(a1, q, z, h1, lng, lnb, wsq, comm=None):
    def body(a1_ref, q_ref, ga_ref, gb_ref, h_ref, lng_ref, lnb_ref, wa_ref, wb_ref, wo_ref, h2_ref, ya_ref, yb_ref):
        _, _, a2, sg = _layernorm_silu(a1_ref[...].astype(F32), lng_ref[...], lnb_ref[...])
        ya = _nn((a2 * sg).astype(BF), wa_ref[...])
        yb = _nn(q_ref[...], wb_ref[...])
        ya_ref[...] = ya.astype(BF)
        yb_ref[...] = yb.astype(BF)
        m = _sig(ga_ref[...].astype(F32)) * ya + _sig(gb_ref[...].astype(F32)) * yb
        h2_ref[...] = h_ref[...] + _nn(m.astype(BF), wo_ref[...])

    row = lambda i: (i, 0)
    vec = pl.BlockSpec((1, D), lambda i: (0, 0))
    return _call(
        body, name="mix_out", grid=(T // TM,), args=[a1, q, z, z, h1, lng, lnb, wsq, wsq, wsq], comm=comm,
        in_specs=[pl.BlockSpec((TM, D), row), pl.BlockSpec((TM, D), row),
                  pl.BlockSpec((None, TM, D), lambda i: (5, i, 0)), pl.BlockSpec((None, TM, D), lambda i: (6, i, 0)),
                  pl.BlockSpec((TM, D), row), vec, vec] + _square_specs((0, 1, 2)),
        out_shape=[jax.ShapeDtypeStruct((T, D), F32), jax.ShapeDtypeStruct((T, D), BF), jax.ShapeDtypeStruct((T, D), BF)],
        out_specs=[pl.BlockSpec((TM, D), row)] * 3)


def _rmsnorm_bwd(xf, g, dn):
    r = lax.rsqrt(jnp.mean(xf * xf, axis=-1, keepdims=True) + EPS)
    xr = xf * r
    gdn = dn * g
    dx = r * gdn - xr * (r * jnp.mean(gdn * xr, axis=-1, keepdims=True))
    return dx, jnp.sum(dn * xr, axis=0, keepdims=True)


def _ffn_bwd_hidden(dh, gg, uu, wbuf, off, name, comm=None):
    nf = F // FC

    def body(dh_ref, gg_ref, uu_ref, b0, dgu_ref, wd_s, sem):
        _load_ffn_weights((b0,), (off,), (wd_s,), sem)
        dhb = (0.5 * dh_ref[...]).astype(BF)
        for c in range(nf):
            sl = slice(c * FC, (c + 1) * FC)
            da = _nt(dhb, wd_s[sl, :]).astype(BF)
            gb, ub = gg_ref[:, sl], uu_ref[:, sl]
            sg = _sig(gb)
            dgu_ref[0, :, sl] = (da * ub) * (sg * (1.0 + gb * (1.0 - sg)))
            dgu_ref[0, :, F + c * FC:F + (c + 1) * FC] = da * (gb * sg)

    row = lambda i: (i, 0)
    return _call(
        body, name=name, grid=(T // TM,), args=[dh, gg, uu, wbuf], comm=comm,
        in_specs=[pl.BlockSpec((TM, D), row), pl.BlockSpec((TM, F), row), pl.BlockSpec((TM, F), row), ANY],
        out_shape=[jax.ShapeDtypeStruct((1, T, 2 * F), BF)],
        out_specs=[pl.BlockSpec((1, TM, 2 * F), lambda i: (0, i, 0))],
        scratch_shapes=[pltpu.VMEM((F, D), BF), pltpu.SemaphoreType.DMA((1,))])


def _ffn_bwd_input(dgu, dh, x, g, wbufs, offs, name, comm=None):
    def body(dgu_ref, dh_ref, x_ref, g_ref, b0, b1, dx_ref, s_ref, w_s, sem):
        _load_ffn_weights((b0, b1), offs, (w_s.at[pl.ds(0, F), :], w_s.at[pl.ds(F, F), :]), sem)

        @pl.when(pl.program_id(0) == 0)
        def _():
            s_ref[...] = jnp.zeros_like(s_ref)

        dn = _nn(dgu_ref[0], w_s[...])
        dxn, dg = _rmsnorm_bwd(x_ref[...], g_ref[...], dn)
        dx_ref[...] = dh_ref[...] + dxn
        s_ref[0:1, :] += dg

    row = lambda i: (i, 0)
    return _call(
        body, name=name, grid=(T // TM,), args=[dgu, dh, x, g, *wbufs], comm=comm,
        in_specs=[pl.BlockSpec((1, TM, 2 * F), lambda i: (0, i, 0)), pl.BlockSpec((TM, D), row),
                  pl.BlockSpec((TM, D), row), pl.BlockSpec((1, D), lambda i: (0, 0)), ANY, ANY],
        out_shape=[jax.ShapeDtypeStruct((T, D), F32), jax.ShapeDtypeStruct((8, D), F32)],
        out_specs=[pl.BlockSpec((TM, D), row), pl.BlockSpec((8, D), lambda i: (0, 0))],
        scratch_shapes=[pltpu.VMEM((2 * F, D), BF), pltpu.SemaphoreType.DMA((2,))])


def _tn_matmul(lhs, rhs, tr, name, comm=None, scale=None):
    ng, _, cdim = lhs.shape
    nc, nk = cdim // tr, T // TK

    def body(l_ref, r_ref, o_ref, acc):
        k = pl.program_id(2)

        @pl.when(k == 0)
        def _():
            acc[...] = jnp.zeros_like(acc)

        r = r_ref[...] if scale is None else scale * r_ref[...]
        acc[...] += _tn(l_ref[...], r.astype(BF))

        @pl.when(k == nk - 1)
        def _():
            o_ref[...] = acc[...].astype(BF)

    return _call(
        body, name=name, grid=(ng, nc, nk), args=[lhs, rhs], comm=comm,
        in_specs=[pl.BlockSpec((None, TK, tr), lambda g, c, k: (g, k, c)),
                  pl.BlockSpec((TK, D), lambda g, c, k: (k, 0))],
        out_shape=[jax.ShapeDtypeStruct((ng * cdim, D), BF)],
        out_specs=[pl.BlockSpec((tr, D), lambda g, c, k: (g * nc + c, 0))],
        scratch_shapes=[pltpu.VMEM((tr, D), F32)])


def _mix_out_bwd(dh2, ya, yb, z, a1, lng, lnb, wsq, comm=None):
    def body(dh_ref, ya_ref, yb_ref, ga_ref, gb_ref, a1_ref, lng_ref, lnb_ref, wa_ref, wb_ref, wo_ref,
             dzg_ref, da1_ref, dq_ref, m_ref, a3_ref, dya_ref, dyb_ref, s_ref):
        @pl.when(pl.program_id(0) == 0)
        def _():
            s_ref[...] = jnp.zeros_like(s_ref)

        dm = _nt(dh_ref[...].astype(BF), wo_ref[...])
        ya, yb = ya_ref[...].astype(F32), yb_ref[...].astype(F32)
        sa, sb = _sig(ga_ref[...].astype(F32)), _sig(gb_ref[...].astype(F32))
        m_ref[0] = (sa * ya + sb * yb).astype(BF)
        dzg_ref[0] = (dm * ya * (sa * (1.0 - sa))).astype(BF)
        dzg_ref[1] = (dm * yb * (sb * (1.0 - sb))).astype(BF)
        dya = (dm * sa).astype(BF)
        dyb = (dm * sb).astype(BF)
        dya_ref[...] = dya
        dyb_ref[...] = dyb
        dq_ref[...] = _nt(dyb, wb_ref[...]).astype(BF)
        da3 = _nt(dya, wa_ref[...])
        lng = lng_ref[...]
        xh, rs, a2, sg = _layernorm_silu(a1_ref[...].astype(F32), lng, lnb_ref[...])
        a3_ref[0] = (a2 * sg).astype(BF)
        da2 = da3 * (sg * (1.0 + a2 * (1.0 - sg)))
        s_ref[0:1, :] += jnp.sum(da2 * xh, axis=0, keepdims=True)
        s_ref[1:2, :] += jnp.sum(da2, axis=0, keepdims=True)
        dxh = da2 * lng
        da1 = rs * (dxh - jnp.mean(dxh, axis=-1, keepdims=True) - xh * jnp.mean(dxh * xh, axis=-1, keepdims=True))
        da1_ref[...] = da1.astype(BF)
        s_ref[2:3, :] += jnp.sum(da1, axis=0, keepdims=True)

    row = lambda i: (i, 0)
    row3 = lambda i: (0, i, 0)
    vec = pl.BlockSpec((1, D), lambda i: (0, 0))
    return _call(
        body, name="mix_out_bwd", grid=(T // TM,), args=[dh2, ya, yb, z, z, a1, lng, lnb, wsq, wsq, wsq], comm=comm,
        in_specs=[pl.BlockSpec((TM, D), row), pl.BlockSpec((TM, D), row), pl.BlockSpec((TM, D), row),
                  pl.BlockSpec((None, TM, D), lambda i: (5, i, 0)), pl.BlockSpec((None, TM, D), lambda i: (6, i, 0)),
                  pl.BlockSpec((TM, D), row), vec, vec] + _square_specs((0, 1, 2)),
        out_shape=[jax.ShapeDtypeStruct((2, T, D), BF), jax.ShapeDtypeStruct((T, D), BF),
                   jax.ShapeDtypeStruct((T, D), BF), jax.ShapeDtypeStruct((1, T, D), BF),
                   jax.ShapeDtypeStruct((1, T, D), BF), jax.ShapeDtypeStruct((T, D), BF),
                   jax.ShapeDtypeStruct((T, D), BF), jax.ShapeDtypeStruct((8, D), F32)],
        out_specs=[pl.BlockSpec((2, TM, D), row3), pl.BlockSpec((TM, D), row), pl.BlockSpec((TM, D), row),
                   pl.BlockSpec((1, TM, D), row3), pl.BlockSpec((1, TM, D), row3), pl.BlockSpec((TM, D), row),
                   pl.BlockSpec((TM, D), row), pl.BlockSpec((8, D), lambda i: (0, 0))])


def _mix_in_bwd(dz, dh2, h1, gm, win, comm=None):
    def body(dz_ref, w_any, dh_ref, h_ref, g_ref, o_ref, s_ref, w_s, sem):
        _load_ffn_weights((w_any,), (0,), (w_s,), sem)

        @pl.when(pl.program_id(0) == 0)
        def _():
            s_ref[...] = jnp.zeros_like(s_ref)

        du = _nn(dz_ref[0], w_s[0:D, :])
        for j in range(1, NG):
            du = du + _nn(dz_ref[j], w_s[j * D:(j + 1) * D, :])
        dx, dg = _rmsnorm_bwd(h_ref[...], g_ref[...], du)
        o_ref[...] = dh_ref[...] + dx
        s_ref[0:1, :] += dg

    row = lambda i: (i, 0)
    return _call(
        body, name="mix_in_bwd", grid=(T // TM,), args=[dz, win, dh2, h1, gm], comm=comm,
        in_specs=[pl.BlockSpec((NG, TM, D), lambda i: (0, i, 0)), ANY,
                  pl.BlockSpec((TM, D), row), pl.BlockSpec((TM, D), row), pl.BlockSpec((1, D), lambda i: (0, 0))],
        out_shape=[jax.ShapeDtypeStruct((T, D), F32), jax.ShapeDtypeStruct((8, D), F32)],
        out_specs=[pl.BlockSpec((TM, D), row), pl.BlockSpec((8, D), lambda i: (0, 0))],
        scratch_shapes=[pltpu.VMEM((NG * D, D), BF), pltpu.SemaphoreType.DMA((1,))])


def _row_tile(n, want, mult):
    for t in range(min(want, n), 0, -1):
        if n % t == 0 and t % mult == 0:
            return t
    return n


def _sum_slots(recv, name):
    ns, rows, cols = recv.shape
    tr = _row_tile(rows, 1024, 16)

    def body(r_ref, o_ref):
        s = r_ref[0].astype(F32)
        for k in range(1, ns):
            s = s + r_ref[k].astype(F32)
        o_ref[...] = s

    return _call(
        body, name=name, grid=(rows // tr,), args=[recv],
        in_specs=[pl.BlockSpec((ns, tr, cols), lambda i: (0, i, 0))],
        out_shape=[jax.ShapeDtypeStruct((rows, cols), F32)],
        out_specs=[pl.BlockSpec((tr, cols), lambda i: (i, 0))])[0]


def _pack_small(s_ffn1, s_in, s_mix, s_ffn2, s_final, dwa, dwb):
    def body(f1, mi, mo, f2, fl, wa_ref, wb_ref, v_ref, k_ref):
        for dst, (ref, row) in enumerate(((f1, 0), (mi, 0), (mo, 0), (mo, 1), (mo, 2), (f2, 0), (fl, 0), (fl, 1))):
            v_ref[dst:dst + 1, :] = ref[row:row + 1, :]
        for k in range(NDEV):
            k_ref[k, 0:32, :] = wa_ref[:, k * LANE:(k + 1) * LANE]
            k_ref[k, 32:40, :] = wb_ref[:, k * LANE:(k + 1) * LANE]

    return pl.pallas_call(
        body, name="pack_small",
        out_shape=(jax.ShapeDtypeStruct((8, D), F32), jax.ShapeDtypeStruct((NDEV, 40, LANE), F32)),
    )(s_ffn1, s_in, s_mix, s_ffn2, s_final, dwa, dwb)


def _sum_small(vecs, convs):
    def body(v_ref, k_ref, vs_ref, ks_ref, l_ref):
        s, c = v_ref[0], k_ref[0]
        for k in range(1, NDEV):
            s = s + v_ref[k]
            c = c + k_ref[k]
        vs_ref[...] = s
        ks_ref[...] = c
        l_ref[...] = jnp.broadcast_to(jnp.sum(s[7:8, :], axis=-1, keepdims=True), (8, LANE))

    return pl.pallas_call(
        body, name="sum_small",
        out_shape=(jax.ShapeDtypeStruct((8, D), F32), jax.ShapeDtypeStruct((40, LANE), F32),
                   jax.ShapeDtypeStruct((8, LANE), F32)),
    )(vecs, convs)


def _adam(gs, ws, ms, vs, name, comm=None):
    n = len(gs)
    rows, cols = ws[0].shape
    tr = _row_tile(rows, 256, 8)
    c1 = 1.0 - ADAM_B1 ** ADAM_STEP
    c2 = 1.0 - ADAM_B2 ** ADAM_STEP

    def body(*refs):
        for i in range(n):
            g, w, m, v = (refs[4 * i + k][...] for k in range(4))
            d_ref, m_ref, v_ref = refs[4 * n + 3 * i: 4 * n + 3 * i + 3]
            m2 = ADAM_B1 * m + (1.0 - ADAM_B1) * g
            v2 = ADAM_B2 * v + (1.0 - ADAM_B2) * (g * g)
            d_ref[...] = -ADAM_LR * ((m2 / c1) / (jnp.sqrt(v2 / c2) + ADAM_EPS) + ADAM_WD * w)
            m_ref[...] = m2
            v_ref[...] = v2

    spec = pl.BlockSpec((tr, cols), lambda i: (i, 0))
    args = []
    for i in range(n):
        args += [gs[i], ws[i], ms[i], vs[i]]
    outs = _call(body, name=name, grid=(rows // tr,), args=args, comm=comm, in_specs=[spec] * (4 * n),
                 out_shape=[jax.ShapeDtypeStruct((rows, cols), F32)] * (3 * n), out_specs=[spec] * (3 * n))
    return [tuple(outs[3 * i: 3 * i + 3]) for i in range(n)], outs[3 * n:]


def kernel(x, ffn1_norm, ffn1_w_gate, ffn1_w_up, ffn1_w_down, mix_norm, w_in, a_dw_w, a_dw_b, a_ln_g, a_ln_b, a_w_out, b_conv_w, b_w_out, w_o, ffn2_norm, ffn2_w_gate, ffn2_w_up, ffn2_w_down, final_norm, loss_target, m_ffn1_norm, m_ffn1_w_gate, m_ffn1_w_up, m_ffn1_w_down, m_mix_norm, m_w_in, m_a_dw_w, m_a_dw_b, m_a_ln_g, m_a_ln_b, m_a_w_out, m_b_conv_w, m_b_w_out, m_w_o, m_ffn2_norm, m_ffn2_w_gate, m_ffn2_w_up, m_ffn2_w_down, m_final_norm, v_ffn1_norm, v_ffn1_w_gate, v_ffn1_w_up, v_ffn1_w_down, v_mix_norm, v_w_in, v_a_dw_w, v_a_dw_b, v_a_ln_g, v_a_ln_b, v_a_w_out, v_b_conv_w, v_b_w_out, v_w_o, v_ffn2_norm, v_ffn2_w_gate, v_ffn2_w_up, v_ffn2_w_down, v_final_norm):
    names = ("ffn1_norm", "ffn1_w_gate", "ffn1_w_up", "ffn1_w_down", "mix_norm", "w_in", "a_dw_w", "a_dw_b",
             "a_ln_g", "a_ln_b", "a_w_out", "b_conv_w", "b_w_out", "w_o", "ffn2_norm", "ffn2_w_gate", "ffn2_w_up",
             "ffn2_w_down", "final_norm")
    w = dict(ffn1_norm=ffn1_norm, ffn1_w_gate=ffn1_w_gate, ffn1_w_up=ffn1_w_up, ffn1_w_down=ffn1_w_down,
             mix_norm=mix_norm, w_in=w_in, a_dw_w=a_dw_w, a_dw_b=a_dw_b, a_ln_g=a_ln_g, a_ln_b=a_ln_b,
             a_w_out=a_w_out, b_conv_w=b_conv_w, b_w_out=b_w_out, w_o=w_o, ffn2_norm=ffn2_norm,
             ffn2_w_gate=ffn2_w_gate, ffn2_w_up=ffn2_w_up, ffn2_w_down=ffn2_w_down, final_norm=final_norm)
    m = dict(ffn1_norm=m_ffn1_norm, ffn1_w_gate=m_ffn1_w_gate, ffn1_w_up=m_ffn1_w_up, ffn1_w_down=m_ffn1_w_down,
             mix_norm=m_mix_norm, w_in=m_w_in, a_dw_w=m_a_dw_w, a_dw_b=m_a_dw_b, a_ln_g=m_a_ln_g, a_ln_b=m_a_ln_b,
             a_w_out=m_a_w_out, b_conv_w=m_b_conv_w, b_w_out=m_b_w_out, w_o=m_w_o, ffn2_norm=m_ffn2_norm,
             ffn2_w_gate=m_ffn2_w_gate, ffn2_w_up=m_ffn2_w_up, ffn2_w_down=m_ffn2_w_down, final_norm=m_final_norm)
    v = dict(ffn1_norm=v_ffn1_norm, ffn1_w_gate=v_ffn1_w_gate, ffn1_w_up=v_ffn1_w_up, ffn1_w_down=v_ffn1_w_down,
             mix_norm=v_mix_norm, w_in=v_w_in, a_dw_w=v_a_dw_w, a_dw_b=v_a_dw_b, a_ln_g=v_a_ln_g, a_ln_b=v_a_ln_b,
             a_w_out=v_a_w_out, b_conv_w=v_b_conv_w, b_w_out=v_b_w_out, w_o=v_w_o, ffn2_norm=v_ffn2_norm,
             ffn2_w_gate=v_ffn2_w_gate, ffn2_w_up=v_ffn2_w_up, ffn2_w_down=v_ffn2_w_down, final_norm=v_final_norm)
    flat = _pack_weights(dict(wg1=ffn1_w_gate[0].T, wu1=ffn1_w_up[0].T, wd1=ffn1_w_down[0], wg2=ffn2_w_gate[0].T,
                              wu2=ffn2_w_up[0].T, wd2=ffn2_w_down[0], win=w_in[0], wa=a_w_out[0], wb=b_w_out[0],
                              wo=w_o[0]))
    cw_shard = jnp.concatenate([a_dw_w[0], jnp.zeros((1, LANE), F32), b_conv_w[0], jnp.zeros((5, LANE), F32)], axis=0)

    x2, tgt = x[0], loss_target[0]
    st_a, st_b, st_c, st_d, st_e = ("wg1", "wu1", "wd1"), ("win",), ("wa", "wb", "wo", "wg2"), ("wu2",), ("wd2",)

    buf_a, cw = _run_comm(_join(_ag_comm(st_a, flat), _direct_comm(cw_shard, False)), "ag_ffn1")
    h1, n1, gg1, uu1, act1, buf_b = _ffn_fwd(x2, ffn1_norm, (buf_a,) * 3, (0, F, 2 * F), "ffn1_fwd", _ag_comm(st_b, flat))
    u, z, buf_c = _mix_in(h1, mix_norm, buf_b, _ag_comm(st_c, flat))
    dft = _dft_constants()
    cw = jnp.transpose(cw, (1, 0, 2)).reshape(40, D)
    a1, q, buf_d = _conv_fwd_dft(z, cw, a_dw_b, dft, _ag_comm(st_d, flat))
    h2, ya, yb, buf_e = _mix_out(a1, q, z, h1, a_ln_g, a_ln_b, buf_c, _ag_comm(st_e, flat))
    ffn2_bufs, ffn2_offs = (buf_c, buf_d, buf_e), (3 * D, 0, 0)
    dh3, s_final, n2, gg2, uu2, act2 = _ffn_fwd(h2, ffn2_norm, ffn2_bufs, ffn2_offs, "ffn2_fwd",
                                          final=(final_norm.reshape(1, D), tgt))

    tr_f = F // 2 if (F // 2) % LANE == 0 else F
    def pair(stage, src):
        return _rs_pair_comm(stage, src)

    def chip(stage, src, pair_buf, tag):
        return _rs_chip_comm(_pair_add(stage, src, pair_buf, "pair_add_" + tag))

    (dgu2,) = _ffn_bwd_hidden(dh3, gg2, uu2, buf_e, 0, "ffn2_bwd_h")
    (gu2,) = _tn_matmul(dgu2, n2, tr_f, "dw_gu2")
    s2a, src2a = ("wg2", "wu2"), dict(wg2=(gu2, 0), wu2=(gu2, F))
    gd2, pair2a = _tn_matmul(act2, dh3, tr_f, "dw_d2", pair(s2a, src2a), scale=0.5)
    s2b, src2b = ("wd2",), dict(wd2=(gd2, 0))
    dh2, s_ffn2, recv2a, pair2b = _ffn_bwd_input(dgu2, dh3, h2, ffn2_norm, (buf_c, buf_d), (3 * D, 0), "ffn2_bwd_x",
                                                 _join(chip(s2a, src2a, pair2a, "2a"), pair(s2b, src2b)))
    dzg, da1, dq, mb, a3b, dya, dyb, s_mix, recv2b = _mix_out_bwd(dh2, ya, yb, z, a1, a_ln_g, a_ln_b, buf_c,
                                                                   chip(s2b, src2b, pair2b, "2b"))
    (go,) = _tn_matmul(mb, dh2, D, "dw_o")
    (ga,) = _tn_matmul(a3b, dya, D, "dw_a")
    (gb,) = _tn_matmul(q.reshape(1, T, D), dyb, D, "dw_b")
    ssq, srcsq = ("wa", "wb", "wo"), dict(wa=(ga, 0), wb=(gb, 0), wo=(go, 0))
    dz, dwa, dwb, pairsq = _conv_bwd_dft(z, da1, dq, dzg, cw, dft, pair(ssq, srcsq))
    gin, recvsq = _tn_matmul(dz, u, D, "dw_in", chip(ssq, srcsq, pairsq, "sq"))
    sin_a, sin_b, srcin = ("win/0/2",), ("win/1/2",), {"win/0/2": (gin, 0), "win/1/2": (gin, 0)}
    dh1, s_in, pairin_a, pairin_b = _mix_in_bwd(dz, dh2, h1, mix_norm, buf_b,
                                                _join(pair(sin_a, srcin), pair(sin_b, srcin)))
    dgu1, recvin_a = _ffn_bwd_hidden(dh1, gg1, uu1, buf_a, 2 * F, "ffn1_bwd_h",
                                           chip(sin_a, srcin, pairin_a, "in_a"))
    gu1, recvin_b = _tn_matmul(dgu1, n1, tr_f, "dw_gu1", chip(sin_b, srcin, pairin_b, "in_b"))
    s1a, src1a = ("wg1", "wu1"), dict(wg1=(gu1, 0), wu1=(gu1, F))
    gd1, pair1a = _tn_matmul(act1, dh1, tr_f, "dw_d1", pair(s1a, src1a), scale=0.5)
    s1b, src1b = ("wd1",), dict(wd1=(gd1, 0))
    dx, s_ffn1, recv1a, pair1b = _ffn_bwd_input(dgu1, dh1, x2, ffn1_norm, (buf_a, buf_a), (0, F), "ffn1_bwd_x",
                                                _join(chip(s1a, src1a, pair1a, "1a"), pair(s1b, src1b)))
    gsum = {}

    def sum_stage(stage, recv, tag):
        st, total = _Stage(stage), _sum_slots(recv, "sum_" + tag)
        for n in stage:
            gsum[n] = total[st.off[n]:st.off[n] + st.rows[n]]

    for stage, recv, tag in ((s2a, recv2a, "2a"), (s2b, recv2b, "2b"), (ssq, recvsq, "sq"), (sin_a, recvin_a, "in_a"),
                             (sin_b, recvin_b, "in_b"), (s1a, recv1a, "1a")):
        sum_stage(stage, recv, tag)
    gsum["win"] = jnp.concatenate([gsum["win/0/2"], gsum["win/1/2"]], axis=0)

    vec8, convk = _pack_small(s_ffn1, s_in, s_mix, s_ffn2, s_final, dwa, dwb)
    vec_all, conv_all = _run_comm(_join(_direct_comm(vec8, False), _direct_comm(convk, True)), "xchg_small")
    vec_sum, conv_sum, loss_blk = _sum_small(vec_all, conv_all)
    loss = loss_blk[0, 0]

    g = dict(ffn1_w_gate=gsum["wg1"], ffn1_w_up=gsum["wu1"],
             ffn2_w_gate=gsum["wg2"], ffn2_w_up=gsum["wu2"], ffn2_w_down=gsum["wd2"], w_in=gsum["win"].T,
             a_w_out=gsum["wa"], b_w_out=gsum["wb"], w_o=gsum["wo"],
             ffn1_norm=vec_sum[0:1], mix_norm=vec_sum[1:2], a_ln_g=vec_sum[2:3], a_ln_b=vec_sum[3:4],
             a_dw_b=vec_sum[4:5], ffn2_norm=vec_sum[5:6], final_norm=vec_sum[6:7],
             a_dw_w=conv_sum[0:KA], b_conv_w=conv_sum[32:32 + KB])
    gate_up = ("ffn1_w_gate", "ffn1_w_up", "ffn2_w_gate", "ffn2_w_up")

    upd = {}

    def run(group, name, as2d=lambda a: a[0], back=lambda a, n: a.reshape(w[n].shape), comm=None):
        res, extra = _adam([g[n] for n in group], [as2d(w[n]) for n in group], [as2d(m[n]) for n in group],
                           [as2d(v[n]) for n in group], name, comm)
        for n, r in zip(group, res):
            upd[n] = tuple(back(a, n) for a in r)
        return extra

    (recv1b,) = run(gate_up, "adam_gate_up", as2d=lambda a: a[0].T, back=lambda a, n: a.T[None],
                    comm=chip(s1b, src1b, pair1b, "1b"))
    for n in gate_up:
        g[n] = g[n].T
    sum_stage(s1b, recv1b, "1b")
    g["ffn1_w_down"] = gsum["wd1"]
    run(("ffn1_w_down", "ffn2_w_down"), "adam_down")
    run(("w_in",), "adam_in")
    run(("a_w_out", "b_w_out", "w_o"), "adam_square")
    run(("a_dw_w",), "adam_dw")
    run(("b_conv_w",), "adam_conv")
    vecs = ("ffn1_norm", "mix_norm", "a_dw_b", "a_ln_g", "a_ln_b", "ffn2_norm", "final_norm")
    run(vecs, "adam_vec", as2d=lambda a: a.reshape(1, D))

    grads = [g[n].reshape(w[n].shape) for n in names]
    return (loss, dx.reshape(x.shape), *grads, *[upd[n][0] for n in names], *[upd[n][1] for n in names],
            *[upd[n][2] for n in names])
```

```python
import jax
import jax.numpy as jnp
from jax import lax
from jax.experimental import pallas as pl
from jax.experimental.pallas import tpu as pltpu

T = 4096
D = 1024
F = 2816
NG = 7
NDEV = 8
NCHIP = 4
KA, KB = 31, 3
EPS = 1e-6
ADAM_LR, ADAM_B1, ADAM_B2, ADAM_EPS, ADAM_WD, ADAM_STEP = 0.001, 0.9, 0.999, 1e-08, 0.01, 10

TM = 512
FC = 256
TB = 1024
NB = 256
HB = NB // 2
CW = 256
CHB = 64
LANE = 128
TK = 2048
VMEM_LIMIT = 56 * 1024 * 1024

BF = jnp.bfloat16
F32 = jnp.float32
MESH = pl.DeviceIdType.MESH
ANY = pl.BlockSpec(memory_space=pl.ANY)

ORDER = ("wg1", "wu1", "wd1", "wg2", "wu2", "wd2", "win", "wa", "wb", "wo")


class _Layout:
    def __init__(self):
        fs, dis, ds = F // NDEV, NG * D // NDEV, D // NDEV
        self.rows = dict(wg1=fs, wu1=fs, wd1=fs, wg2=fs, wu2=fs, wd2=fs, win=dis, wa=ds, wb=ds, wo=ds)
        self.fl, off = {}, 0
        for n in ORDER:
            self.fl[n] = off
            off += self.rows[n]
        self.RT = off


class _Stage:
    def __init__(self, names):
        lay = _Layout()
        self.names = names
        self.rows, self.full, self.sub, self.fl = {}, {}, {}, {}
        for n in names:
            base, i, k = (n.split("/") + ["0", "1"])[:3]
            self.full[n] = lay.rows[base]
            self.rows[n] = lay.rows[base] // int(k)
            self.sub[n] = int(i) * self.rows[n]
            self.fl[n] = lay.fl[base] + self.sub[n]
        self.off, self.wc, o, w = {}, {}, 0, 0
        for n in names:
            self.off[n], self.wc[n] = o, w
            o += self.rows[n]
            w += NDEV * self.rows[n]
        self.R, self.W = o, w

    def grad_row(self, n, first, dev_lin):
        return first + dev_lin * self.full[n] + self.sub[n]


def _nt(a, b):
    return lax.dot_general(a, b, (((1,), (1,)), ((), ())), preferred_element_type=F32)


def _nn(a, b):
    return lax.dot_general(a, b, (((1,), (0,)), ((), ())), preferred_element_type=F32)


def _tn(a, b):
    return lax.dot_general(a, b, (((0,), (0,)), ((), ())), preferred_element_type=F32)


def _sig(x):
    return 1.0 / (1.0 + jnp.exp(-x))


def _position():
    return lax.axis_index("x"), lax.axis_index("y"), lax.axis_index("c")


def _peer(pos, j):
    x, y, c = pos
    return (1 - x if j & 4 else x, 1 - y if j & 2 else y, 1 - c if j & 1 else c)


def _lin(pos):
    return 4 * pos[0] + 2 * pos[1] + pos[2]


def _chip(pos):
    return 2 * pos[0] + pos[1]


class _Comm:
    def __init__(self, inputs, out_shapes, scratch, start, finish, middle=None):
        self.inputs, self.out_shapes, self.scratch = inputs, out_shapes, scratch
        self.start, self.finish, self.middle = start, finish, middle


def _call(body, *, name, grid, args, in_specs, out_shape, out_specs, scratch_shapes=(), comm=None,
          num_scalar_prefetch=0):
    in_specs, out_shape, out_specs, scratch_shapes = list(in_specs), list(out_shape), list(out_specs), list(scratch_shapes)
    n_in, n_out, n_scr = len(in_specs), len(out_shape), len(scratch_shapes)
    sp = num_scalar_prefetch
    if comm is None:
        kernel_fn = lambda *refs: body(*refs)
        c_in = c_out = c_scr = 0
    else:
        c_in, c_out, c_scr = len(comm.inputs), len(comm.out_shapes), len(comm.scratch)

        def kernel_fn(*refs):
            pre, refs = refs[:sp], refs[sp:]
            ins, cins = refs[:n_in], refs[n_in:n_in + c_in]
            o0 = n_in + c_in
            outs, couts = refs[o0:o0 + n_out], refs[o0 + n_out:o0 + n_out + c_out]
            s0 = o0 + n_out + c_out
            scr, cscr = refs[s0:s0 + n_scr], refs[s0 + n_scr:]
            step, steps = pl.program_id(0), grid[0]
            for a in range(1, len(grid)):
                step, steps = step * grid[a] + pl.program_id(a), steps * grid[a]
            first, last = step == 0, step == steps - 1

            @pl.when(first)
            def _():
                comm.start(cins, couts, cscr)

            if comm.middle is not None:
                @pl.when(step == (steps // 2 if steps > 2 else steps - 1))
                def _():
                    comm.middle(cins, couts, cscr)

            body(*pre, *ins, *outs, *scr)

            @pl.when(last)
            def _():
                comm.finish(cins, couts, cscr)

        args = list(args) + list(comm.inputs)
        in_specs += [ANY] * c_in
        out_shape += list(comm.out_shapes)
        out_specs += [ANY] * c_out
        scratch_shapes += list(comm.scratch)
    params = pltpu.CompilerParams(dimension_semantics=("arbitrary",) * len(grid), vmem_limit_bytes=VMEM_LIMIT)
    if sp:
        grid_spec = pltpu.PrefetchScalarGridSpec(num_scalar_prefetch=sp, grid=grid, in_specs=in_specs,
                                                 out_specs=out_specs, scratch_shapes=scratch_shapes)
        return pl.pallas_call(kernel_fn, name=name, grid_spec=grid_spec, out_shape=out_shape,
                              compiler_params=params)(*args)
    return pl.pallas_call(kernel_fn, name=name, grid=grid, in_specs=in_specs, out_shape=out_shape, out_specs=out_specs,
                          scratch_shapes=scratch_shapes, compiler_params=params)(*args)


def _join(a, b):
    na = (len(a.inputs), len(a.out_shapes), len(a.scratch))

    def split(refs):
        return ([r[:n] for r, n in zip(refs, na)], [r[n:] for r, n in zip(refs, na)])

    def start(*refs):
        ra, rb = split(refs)
        a.start(*ra)
        b.start(*rb)

    def finish(*refs):
        ra, rb = split(refs)
        a.finish(*ra)
        b.finish(*rb)

    def middle(*refs):
        for stage, r in zip((a, b), split(refs)):
            if stage.middle is not None:
                stage.middle(*r)

    return _Comm(list(a.inputs) + list(b.inputs), list(a.out_shapes) + list(b.out_shapes),
                 list(a.scratch) + list(b.scratch), start, finish,
                 middle if (a.middle is not None or b.middle is not None) else None)


def _run_comm(comm, name):
    def body(*refs):
        c_in, c_out = len(comm.inputs), len(comm.out_shapes)
        parts = (refs[:c_in], refs[c_in:c_in + c_out], refs[c_in + c_out:])
        comm.start(*parts)
        if comm.middle is not None:
            comm.middle(*parts)
        comm.finish(*parts)

    return pl.pallas_call(
        body, name=name, out_shape=list(comm.out_shapes), in_specs=[ANY] * len(comm.inputs),
        out_specs=[ANY] * len(comm.out_shapes), scratch_shapes=list(comm.scratch))(*comm.inputs)


def _ag_comm(names, flat):
    st = _Stage(names)

    def ring(me):
        x, y, c = me
        diagonal = x == y
        up = (jnp.where(diagonal, x, 1 - x), jnp.where(diagonal, 1 - y, y), c)
        down = (jnp.where(diagonal, 1 - x, x), jnp.where(diagonal, y, 1 - y), c)
        low = c == 0
        passed = tuple(jnp.where(low, d, u) for d, u in zip(down, up))
        target = tuple(jnp.where(low, u, d) for d, u in zip(down, up))
        return up, down, (1 - x, 1 - y, c), passed, target

    def parts(refs):
        (flat_ref,), (out_ref,), (send_sems, recv_sems, local_sem) = refs
        me = _position()

        def region(name, dev):
            r = st.rows[name]
            return out_ref.at[pl.ds(st.wc[name] + _lin(dev) * r, r), :]

        def own(name):
            return flat_ref.at[pl.ds(st.fl[name], st.rows[name]), :]

        def copies(k, dev, to, from_flat):
            return [pltpu.make_async_remote_copy(
                src_ref=own(n) if from_flat else region(n, dev), dst_ref=region(n, dev), send_sem=send_sems.at[k],
                recv_sem=recv_sems.at[k], device_id=to, device_id_type=MESH) for n in names]

        def whole(k):
            return pltpu.make_async_remote_copy(
                src_ref=flat_ref.at[pl.ds(0, st.R), :], dst_ref=out_ref.at[pl.ds(0, st.R), :],
                send_sem=send_sems.at[k], recv_sem=recv_sems.at[k], device_id=me, device_id_type=MESH)

        return me, region, own, copies, whole, flat_ref, out_ref, local_sem

    def start(*refs):
        me, region, own, copies, _, _, _, local_sem = parts(refs)
        for n in names:
            pltpu.make_async_copy(own(n), region(n, me), local_sem).start()
        up, down, _, _, _ = ring(me)
        for k, to in ((1, up), (2, down), (0, _peer(me, 1))):
            for cp in copies(k, me, to, True):
                cp.start()

    def middle(*refs):
        me, _, _, copies, whole, _, _, _ = parts(refs)
        up, down, _, passed, target = ring(me)
        sib = _peer(me, 1)
        whole(1).wait_recv()
        whole(2).wait_recv()
        for k, dev, to in ((3, passed, target), (4, down, sib), (5, up, sib)):
            for cp in copies(k, dev, to, False):
                cp.start()

    def finish(*refs):
        me, _, _, copies, whole, flat_ref, out_ref, local_sem = parts(refs)
        _, _, across, _, _ = ring(me)
        whole(3).wait_recv()
        for cp in copies(6, across, _peer(me, 1), False):
            cp.start()
        whole(0).wait_recv()
        for j in range(3):
            whole(4 + j).wait_recv()
        for k in range(7):
            whole(k).wait_send()
        pltpu.make_async_copy(flat_ref.at[pl.ds(0, st.R), :], out_ref.at[pl.ds(0, st.R), :], local_sem).wait()

    return _Comm([flat], [jax.ShapeDtypeStruct((st.W, D), BF)],
                 [pltpu.SemaphoreType.DMA((7,)), pltpu.SemaphoreType.DMA((7,)), pltpu.SemaphoreType.DMA],
                 start, finish, middle)


def _rs_pair_comm(names, src):
    st = _Stage(names)
    arrays = []
    for n in names:
        if not any(src[n][0] is a for a in arrays):
            arrays.append(src[n][0])
    idx = {n: [i for i, a in enumerate(arrays) if a is src[n][0]][0] for n in names}

    def slot_wait(refs):
        recv = refs[1][0]
        send_sem, recv_sem = refs[2]
        return pltpu.make_async_remote_copy(src_ref=recv, dst_ref=recv, send_sem=send_sem, recv_sem=recv_sem,
                                            device_id=_position(), device_id_type=MESH)

    def start(*refs):
        ins, (recv,), (send_sem, recv_sem) = refs
        me = _position()
        sib = _peer(me, 1)
        for q in range(NCHIP):
            dev = (q // 2, q % 2, sib[2])
            for n in names:
                r = st.rows[n]
                pltpu.make_async_remote_copy(
                    src_ref=ins[idx[n]].at[pl.ds(st.grad_row(n, src[n][1], _lin(dev)), r), :],
                    dst_ref=recv.at[q, pl.ds(st.off[n], r), :], send_sem=send_sem, recv_sem=recv_sem,
                    device_id=sib, device_id_type=MESH).start()

    def finish(*refs):
        w = slot_wait(refs)
        w.wait_recv()
        w.wait_send()

    return _Comm(arrays, [jax.ShapeDtypeStruct((NCHIP, st.R, D), BF)],
                 [pltpu.SemaphoreType.DMA, pltpu.SemaphoreType.DMA], start, finish)


def _pair_add(names, src, recv, name):
    st = _Stage(names)
    c_arr = jnp.reshape(lax.axis_index("c"), (1,)).astype(jnp.int32)

    def body(c_ref, *refs):
        r_ref, o_ref = refs[len(names)], refs[len(names) + 1]
        for a_ref, n in zip(refs, names):
            rows = slice(st.off[n], st.off[n] + st.rows[n])
            o_ref[rows, :] = (a_ref[...].astype(F32) + r_ref[rows, :].astype(F32)).astype(BF)

    def shard_spec(n):
        r = st.rows[n]
        base, step = st.grad_row(n, src[n][1], 0) // r, st.full[n] // r
        return pl.BlockSpec((r, D), lambda q, c_ref: (base + step * (2 * q + c_ref[0]), 0))

    slot = pl.BlockSpec((None, st.R, D), lambda q, c_ref: (q, 0, 0))
    return _call(body, name=name, grid=(NCHIP,), args=[c_arr] + [src[n][0] for n in names] + [recv],
                 in_specs=[shard_spec(n) for n in names] + [slot],
                 out_shape=[jax.ShapeDtypeStruct((NCHIP, st.R, D), BF)], out_specs=[slot], num_scalar_prefetch=1)[0]


def _rs_chip_comm(part):
    def copies(refs):
        (p_ref,), (recv,), (send_sems, recv_sems, local_sem) = refs
        me = _position()
        mine = pltpu.make_async_copy(p_ref.at[_chip(me)], recv.at[_chip(me)], local_sem)
        out = []
        for j, bits in enumerate((4, 2, 6)):
            to = _peer(me, bits)
            out.append(pltpu.make_async_remote_copy(
                src_ref=p_ref.at[_chip(to)], dst_ref=recv.at[_chip(me)], send_sem=send_sems.at[j],
                recv_sem=recv_sems.at[j], device_id=to, device_id_type=MESH))
        return mine, out

    def start(*refs):
        mine, out = copies(refs)
        mine.start()
        for cp in out:
            cp.start()

    def finish(*refs):
        mine, out = copies(refs)
        for cp in out:
            cp.wait_recv()
        for cp in out:
            cp.wait_send()
        mine.wait()

    return _Comm([part], [jax.ShapeDtypeStruct(part.shape, BF)],
                 [pltpu.SemaphoreType.DMA((3,)), pltpu.SemaphoreType.DMA((3,)), pltpu.SemaphoreType.DMA],
                 start, finish)


def _direct_comm(x, scatter):
    def copies(refs):
        (x_ref,), (out_ref,), (send_sems, recv_sems, local_sem) = refs
        me = _position()

        def piece(dev):
            return x_ref.at[_lin(dev)] if scatter else x_ref

        mine = pltpu.make_async_copy(piece(me), out_ref.at[_lin(me)], local_sem)
        return mine, [pltpu.make_async_remote_copy(
            src_ref=piece(_peer(me, j)), dst_ref=out_ref.at[_lin(me)], send_sem=send_sems.at[j - 1],
            recv_sem=recv_sems.at[j - 1], device_id=_peer(me, j), device_id_type=MESH) for j in range(1, NDEV)]

    def start(*refs):
        mine, cps = copies(refs)
        mine.start()
        for cp in cps:
            cp.start()

    def finish(*refs):
        mine, cps = copies(refs)
        for cp in cps:
            cp.wait_recv()
        for cp in cps:
            cp.wait_send()
        mine.wait()

    shape = x.shape if scatter else (NDEV,) + x.shape
    return _Comm([x], [jax.ShapeDtypeStruct(shape, x.dtype)],
                 [pltpu.SemaphoreType.DMA((7,)), pltpu.SemaphoreType.DMA((7,)), pltpu.SemaphoreType.DMA],
                 start, finish)


def _pack_weights(shards):
    lay = _Layout()

    def body(*refs):
        o_ref = refs[-1]
        for ref, n in zip(refs, ORDER):
            x = ref[...].T if n == "win" else ref[...]
            o_ref[lay.fl[n]:lay.fl[n] + lay.rows[n], :] = x.astype(BF)

    return pl.pallas_call(
        body, name="pack_weights", out_shape=jax.ShapeDtypeStruct((lay.RT, D), BF),
        compiler_params=pltpu.CompilerParams(vmem_limit_bytes=VMEM_LIMIT))(*[shards[n] for n in ORDER])


def _load_ffn_weights(srcs, offs, scratch, sem):
    @pl.when(pl.program_id(0) == 0)
    def _():
        cps = [pltpu.make_async_copy(s.at[pl.ds(off, dst.shape[0]), :], dst, sem.at[i])
               for i, (s, off, dst) in enumerate(zip(srcs, offs, scratch))]
        for cp in cps:
            cp.start()
        for cp in cps:
            cp.wait()


def _final_loss_tile(xf, g, tgt, s_ref):
    r = lax.rsqrt(jnp.mean(xf * xf, axis=-1, keepdims=True) + EPS)
    xr = xf * r
    e = xr * g - tgt
    s_ref[1:2, :] += jnp.sum(e * e, axis=0, keepdims=True) * (0.5 / D)
    dy = e * (1.0 / D)
    s_ref[0:1, :] += jnp.sum(dy * xr, axis=0, keepdims=True)
    gdy = dy * g
    return r * gdy - xr * (r * jnp.mean(gdy * xr, axis=-1, keepdims=True))


def _ffn_fwd(x, g, wbufs, offs, name, comm=None, final=None):
    nf = F // FC

    def body(x_ref, g_ref, b0, b1, b2, *rest):
        if final is None:
            h_ref, n_ref, gg_ref, uu_ref, a_ref, wg_s, wu_s, wd_s, sem = rest
        else:
            gf_ref, t_ref, dh_ref, s_ref, n_ref, gg_ref, uu_ref, a_ref, wg_s, wu_s, wd_s, sem = rest

            @pl.when(pl.program_id(0) == 0)
            def _():
                s_ref[...] = jnp.zeros_like(s_ref)

        _load_ffn_weights((b0, b1, b2), offs, (wg_s, wu_s, wd_s), sem)
        xf = x_ref[...]
        r = lax.rsqrt(jnp.mean(xf * xf, axis=-1, keepdims=True) + EPS)
        nb = (xf * r * g_ref[...]).astype(BF)
        n_ref[...] = nb
        acc = jnp.zeros((TM, D), F32)
        for c in range(nf):
            sl = slice(c * FC, (c + 1) * FC)
            gb = _nt(nb, wg_s[sl, :]).astype(BF)
            ub = _nt(nb, wu_s[sl, :]).astype(BF)
            gg_ref[:, sl] = gb
            uu_ref[:, sl] = ub
            a = (gb * _sig(gb)) * ub
            a_ref[0, :, sl] = a
            acc = acc + _nn(a, wd_s[sl, :])
        h = xf + 0.5 * acc
        if final is None:
            h_ref[...] = h
        else:
            dh_ref[...] = _final_loss_tile(h, gf_ref[...], t_ref[...], s_ref)

    row = lambda i: (i, 0)
    vec = pl.BlockSpec((1, D), lambda i: (0, 0))
    tile = pl.BlockSpec((TM, D), row)
    saved_shapes = [jax.ShapeDtypeStruct((T, D), BF), jax.ShapeDtypeStruct((T, F), BF), jax.ShapeDtypeStruct((T, F), BF),
                    jax.ShapeDtypeStruct((1, T, F), BF)]
    saved_specs = [tile, pl.BlockSpec((TM, F), row), pl.BlockSpec((TM, F), row),
                   pl.BlockSpec((1, TM, F), lambda i: (0, i, 0))]
    if final is None:
        extra_args, extra_specs = [], []
        head_shapes, head_specs = [jax.ShapeDtypeStruct((T, D), F32)], [tile]
    else:
        extra_args, extra_specs = list(final), [vec, tile]
        head_shapes = [jax.ShapeDtypeStruct((T, D), F32), jax.ShapeDtypeStruct((8, D), F32)]
        head_specs = [tile, pl.BlockSpec((8, D), lambda i: (0, 0))]
    return _call(
        body, name=name, grid=(T // TM,), args=[x, g, *wbufs, *extra_args], comm=comm,
        in_specs=[tile, vec, ANY, ANY, ANY] + extra_specs,
        out_shape=head_shapes + saved_shapes, out_specs=head_specs + saved_specs,
        scratch_shapes=[pltpu.VMEM((F, D), BF)] * 3 + [pltpu.SemaphoreType.DMA((3,))])


def _mix_in(h1, gm, win, comm=None):
    def body(h_ref, g_ref, w_any, u_ref, z_ref, w_s, sem):
        _load_ffn_weights((w_any,), (0,), (w_s,), sem)
        xf = h_ref[...]
        r = lax.rsqrt(jnp.mean(xf * xf, axis=-1, keepdims=True) + EPS)
        ub = (xf * r * g_ref[...]).astype(BF)
        u_ref[...] = ub
        for j in range(NG):
            z_ref[j] = _nt(ub, w_s[j * D:(j + 1) * D, :]).astype(BF)

    row = lambda i: (i, 0)
    return _call(
        body, name="mix_in", grid=(T // TM,), args=[h1, gm, win], comm=comm,
        in_specs=[pl.BlockSpec((TM, D), row), pl.BlockSpec((1, D), lambda i: (0, 0)), ANY],
        out_shape=[jax.ShapeDtypeStruct((T, D), BF), jax.ShapeDtypeStruct((NG, T, D), BF)],
        out_specs=[pl.BlockSpec((TM, D), row), pl.BlockSpec((NG, TM, D), lambda i: (0, i, 0))],
        scratch_shapes=[pltpu.VMEM((NG * D, D), BF), pltpu.SemaphoreType.DMA((1,))])


def _shift_up(w, b):
    return w if b == 0 else pltpu.roll(w, w.shape[0] - b, 0)


def _fold8(p):
    red = p[0:8, :]
    for i in range(1, p.shape[0] // 8):
        red = red + p[8 * i:8 * i + 8, :]
    return red


def _dft_constants():
    import numpy as np
    nh = NB // 2
    f, n = np.arange(nh)[:, None], np.arange(NB)[None, :]
    ang = 2.0 * np.pi / NB * f * n
    fc = np.cos(ang)
    fs = np.where(f == 0, (-1.0) ** n, np.sin(ang))
    scale = np.where(f == 0, 1.0, 2.0) / NB
    ic = (scale * np.cos(ang)).T
    isn = np.where(f == 0, (-1.0) ** n / NB, scale * np.sin(ang)).T
    d = (KA - 1 - np.arange(32))[None, :]
    valid = (np.arange(32) < KA)[None, :]
    angk = 2.0 * np.pi / NB * f * d
    kc = np.where(valid, np.cos(angk), 0.0)
    ks = np.where(valid, np.sin(angk), 0.0)
    k2 = np.where(valid, np.where(f == 0, (-1.0) ** d, np.cos(angk)), 0.0)
    rtc = np.where(valid, scale * np.cos(angk), 0.0).T
    rts = np.where(valid, np.where(f == 0, (-1.0) ** d / NB, scale * np.sin(angk)), 0.0).T

    def bf(a):
        return jnp.asarray(a, F32).astype(BF)

    def split(a):
        hi = bf(a)
        return hi, (jnp.asarray(a, F32) - hi.astype(F32)).astype(BF)

    return dict(fc=bf(fc), fs=bf(fs), ic_hi=bf(ic[HB:]), is_hi=bf(isn[HB:]), ic_lo=bf(ic[:HB]), is_lo=bf(isn[:HB]),
                kc=split(kc), ks=split(ks), k2=split(k2), rtc=split(rtc), rts=split(rts))


def _dot3(m_hi, m_lo, x):
    x_hi = x.astype(BF)
    x_lo = (x - x_hi.astype(F32)).astype(BF)
    return _nn(m_hi, x_hi) + _nn(m_hi, x_lo) + _nn(m_lo, x_hi)


def _whole(a):
    return pl.BlockSpec(a.shape, lambda c, t: (0,) * a.ndim)


def _filter_spectrum(cw_ref, tabs, hc, hs, h2):
    w32 = cw_ref[0:32, :]
    for (hi, lo), dst in zip(tabs, (hc, hs, h2)):
        dst[...] = _dot3(hi[...], lo[...], w32)


def _conv_fwd_dft(z, cw, bias, dft, comm=None):
    nt = T // TB
    hb = TB // HB

    def body(z_ref, zh_ref, cw_ref, b_ref, fc_ref, fs_ref, ic_ref, is_ref, kch, kcl, ksh, ksl, k2h, k2l,
             a1_ref, q_ref, aext, ppad, hc, hs, h2):
        first = pl.program_id(1) == 0
        f = lambda ref, j: ref[j].astype(F32)

        @pl.when(first)
        def _():
            _filter_spectrum(cw_ref, ((kch, kcl), (ksh, ksl), (k2h, k2l)), hc, hs, h2)

        aext[0:HB, :] = jnp.where(first, 0.0, f(zh_ref, 0) * _sig(f(zh_ref, 1))).astype(BF)
        aext[HB:, :] = (f(z_ref, 0) * _sig(f(z_ref, 1))).astype(BF)
        ppad[0:8, :] = jnp.where(first, 0.0, f(zh_ref, 3)[HB - 8:HB, :] * f(zh_ref, 4)[HB - 8:HB, :])
        ppad[8:, :] = f(z_ref, 3) * f(z_ref, 4)
        bias_row = b_ref[...]

        for j in range(TB // HB):
            xs = aext[j * HB:j * HB + NB, :]
            xa, xb = _nn(fc_ref[...], xs), _nn(fs_ref[...], xs)
            yc = (hc[...] * xa - hs[...] * xb).astype(BF)
            ys = (h2[...] * xb + hs[...] * xa).astype(BF)
            y = _nn(ic_ref[...], yc) + _nn(is_ref[...], ys)
            a1_ref[j * HB:(j + 1) * HB, :] = (y + bias_row).astype(BF)

        def chunk(r, carry):
            base = pl.multiple_of(r * CHB, CHB)
            pw = ppad[pl.ds(base, CHB + 8), :]
            v = (cw_ref[pl.ds(32, 1), :] * _shift_up(pw, 6)[0:CHB, :]
                 + cw_ref[pl.ds(33, 1), :] * _shift_up(pw, 7)[0:CHB, :]
                 + cw_ref[pl.ds(34, 1), :] * pw[8:8 + CHB, :])
            q_ref[pl.ds(base, CHB), :] = (z_ref[2, pl.ds(base, CHB), :].astype(F32) * v).astype(BF)
            return carry

        lax.fori_loop(0, TB // CHB, chunk, 0)

    blk = pl.BlockSpec((TB, CW), lambda c, t: (t, c))
    tabs = [dft["fc"], dft["fs"], dft["ic_hi"], dft["is_hi"], *dft["kc"], *dft["ks"], *dft["k2"]]
    return _call(
        body, name="conv_fwd", grid=(D // CW, nt), comm=comm, args=[z, z, cw, bias] + tabs,
        in_specs=[pl.BlockSpec((5, TB, CW), lambda c, t: (0, t, c)),
                  pl.BlockSpec((5, HB, CW), lambda c, t: (0, jnp.maximum(t * hb - 1, 0), c)),
                  pl.BlockSpec((40, CW), lambda c, t: (0, c)), pl.BlockSpec((1, CW), lambda c, t: (0, c))]
                 + [_whole(a) for a in tabs],
        out_shape=[jax.ShapeDtypeStruct((T, D), BF), jax.ShapeDtypeStruct((T, D), BF)], out_specs=[blk, blk],
        scratch_shapes=[pltpu.VMEM((TB + HB, CW), BF), pltpu.VMEM((TB + 8, CW), F32)]
                       + [pltpu.VMEM((NB // 2, CW), F32)] * 3)


def _conv_bwd_dft(z, da1, dq, dzg, cw, dft, comm=None):
    nt = T // TB
    hb = TB // HB
    last_h = T // HB - 1

    def body(z_ref, zp_ref, zn_ref, da1_ref, da1n_ref, dq_ref, dqn_ref, dzg_ref, cw_ref,
             fc_ref, fs_ref, ic_ref, is_ref, kch, kcl, ksh, ksl, k2h, k2l, rch, rcl, rsh, rsl,
             dz_ref, dwa_ref, dwb_ref, aext, dyext, ppad, dvpad, hc, hs, h2, rc, rs, nyq, acc_b):
        t = pl.program_id(1)
        first, last = t == 0, t == nt - 1
        f = lambda ref, j: ref[j].astype(F32)

        @pl.when(first)
        def _():
            _filter_spectrum(cw_ref, ((kch, kcl), (ksh, ksl), (k2h, k2l)), hc, hs, h2)
            rc[...] = jnp.zeros_like(rc)
            rs[...] = jnp.zeros_like(rs)
            nyq[...] = jnp.zeros_like(nyq)
            acc_b[...] = jnp.zeros_like(acc_b)

        aext[0:HB, :] = jnp.where(first, 0.0, f(zp_ref, 0) * _sig(f(zp_ref, 1))).astype(BF)
        aext[HB:, :] = (f(z_ref, 0) * _sig(f(z_ref, 1))).astype(BF)
        dyext[0:TB, :] = da1_ref[...]
        dyext[TB:, :] = jnp.where(last, 0.0, da1n_ref[...].astype(F32)).astype(BF)
        ppad[0:8, :] = jnp.where(first, 0.0, f(zp_ref, 3)[HB - 8:HB, :] * f(zp_ref, 4)[HB - 8:HB, :])
        ppad[8:, :] = f(z_ref, 3) * f(z_ref, 4)
        dvpad[0:TB, :] = dq_ref[...].astype(F32) * f(z_ref, 2)
        dvpad[TB:, :] = jnp.where(last, 0.0, dqn_ref[...].astype(F32)[0:8, :] * f(zn_ref, 2)[0:8, :])

        for j in range(TB // HB):
            rows = slice(j * HB, (j + 1) * HB)
            dys = dyext[j * HB:j * HB + NB, :]
            da, db = _nn(fc_ref[...], dys), _nn(fs_ref[...], dys)
            gc = (hc[...] * da + hs[...] * db).astype(BF)
            gs = (h2[...] * db - hs[...] * da).astype(BF)
            da0 = _nn(ic_ref[...], gc) + _nn(is_ref[...], gs)
            z0, z1 = z_ref[0, rows, :].astype(F32), z_ref[1, rows, :].astype(F32)
            s1 = _sig(z1)
            dz_ref[0, rows, :] = (da0 * s1).astype(BF)
            dz_ref[1, rows, :] = (da0 * z0 * (s1 * (1.0 - s1))).astype(BF)
            xs = aext[j * HB:j * HB + NB, :]
            xa, xb = _nn(fc_ref[...], xs), _nn(fs_ref[...], xs)
            dyb = dyext[rows, :]
            pa, pb = _nn(fc_ref[:, HB:NB], dyb), _nn(fs_ref[:, HB:NB], dyb)
            rc[...] += pa * xa + pb * xb
            rs[...] += pb * xa - pa * xb
            nyq[...] += pb[0:8, :] * xb[0:8, :]

        def chunk(r, carry):
            base = pl.multiple_of(r * CHB, CHB)
            rows = pl.ds(base, CHB)
            pw = ppad[pl.ds(base, CHB + 8), :]
            p6 = _shift_up(pw, 6)[0:CHB, :]
            p7 = _shift_up(pw, 7)[0:CHB, :]
            p8 = pw[8:8 + CHB, :]
            wb0, wb1, wb2 = cw_ref[pl.ds(32, 1), :], cw_ref[pl.ds(33, 1), :], cw_ref[pl.ds(34, 1), :]
            v = wb0 * p6 + wb1 * p7 + wb2 * p8
            dz_ref[2, rows, :] = (dq_ref[rows, :].astype(F32) * v).astype(BF)
            dvw = dvpad[pl.ds(base, CHB + 8), :]
            dvc = dvw[0:CHB, :]
            dp = wb2 * dvc + wb1 * _shift_up(dvw, 1)[0:CHB, :] + wb0 * _shift_up(dvw, 2)[0:CHB, :]
            dz_ref[3, rows, :] = (dp * z_ref[4, rows, :].astype(F32)).astype(BF)
            dz_ref[4, rows, :] = (dp * z_ref[3, rows, :].astype(F32)).astype(BF)
            acc_b[0:8, :] += _fold8(dvc * p6)
            acc_b[8:16, :] += _fold8(dvc * p7)
            acc_b[16:24, :] += _fold8(dvc * p8)
            dz_ref[5, rows, :] = dzg_ref[0, rows, :]
            dz_ref[6, rows, :] = dzg_ref[1, rows, :]
            return carry

        lax.fori_loop(0, TB // CHB, chunk, 0)

        @pl.when(last)
        def _():
            row0 = lax.broadcasted_iota(jnp.int32, (NB // 2, CW), 0) == 0
            ny = jnp.broadcast_to(nyq[0:1, :], (NB // 2, CW))
            rcv = jnp.where(row0, rc[...] - ny, rc[...])
            rsv = jnp.where(row0, ny, rs[...])
            dwa_ref[...] = _dot3(rch[...], rcl[...], rcv) + _dot3(rsh[...], rsl[...], rsv)
            for k in range(KB):
                dwb_ref[k:k + 1, :] = jnp.sum(acc_b[8 * k:8 * k + 8, :], axis=0, keepdims=True)
            dwb_ref[KB:8, :] = jnp.zeros((8 - KB, CW), F32)

    blk = lambda c, t: (t, c)
    nxt = lambda c, t: (jnp.minimum((t + 1) * hb, last_h), c)
    tabs = [dft["fc"], dft["fs"], dft["ic_lo"], dft["is_lo"], *dft["kc"], *dft["ks"], *dft["k2"], *dft["rtc"], *dft["rts"]]
    return _call(
        body, name="conv_bwd", grid=(D // CW, nt), comm=comm, args=[z, z, z, da1, da1, dq, dq, dzg, cw] + tabs,
        in_specs=[pl.BlockSpec((5, TB, CW), lambda c, t: (0, t, c)),
                  pl.BlockSpec((5, HB, CW), lambda c, t: (0, jnp.maximum(t * hb - 1, 0), c)),
                  pl.BlockSpec((5, HB, CW), lambda c, t: (0, jnp.minimum((t + 1) * hb, last_h), c)),
                  pl.BlockSpec((TB, CW), blk), pl.BlockSpec((HB, CW), nxt),
                  pl.BlockSpec((TB, CW), blk), pl.BlockSpec((HB, CW), nxt),
                  pl.BlockSpec((2, TB, CW), lambda c, t: (0, t, c)),
                  pl.BlockSpec((40, CW), lambda c, t: (0, c))]
                 + [_whole(a) for a in tabs],
        out_shape=[jax.ShapeDtypeStruct((NG, T, D), BF), jax.ShapeDtypeStruct((32, D), F32),
                   jax.ShapeDtypeStruct((8, D), F32)],
        out_specs=[pl.BlockSpec((NG, TB, CW), lambda c, t: (0, t, c)),
                   pl.BlockSpec((32, CW), lambda c, t: (0, c)), pl.BlockSpec((8, CW), lambda c, t: (0, c))],
        scratch_shapes=[pltpu.VMEM((TB + HB, CW), BF), pltpu.VMEM((TB + HB, CW), BF),
                        pltpu.VMEM((TB + 8, CW), F32), pltpu.VMEM((TB + 8, CW), F32)]
                       + [pltpu.VMEM((NB // 2, CW), F32)] * 5 + [pltpu.VMEM((8, CW), F32), pltpu.VMEM((24, CW), F32)])


def _layernorm_silu(a1, lng, lnb):
    mu = jnp.mean(a1, axis=-1, keepdims=True)
    xc = a1 - mu
    rs = lax.rsqrt(jnp.mean(xc * xc, axis=-1, keepdims=True) + EPS)
    xh = xc * rs
    a2 = xh * lng + lnb
    sg = _sig(a2)
    return xh, rs, a2, sg


def _square_specs(blocks):
    return [pl.BlockSpec((D, D), lambda i, b=b: (b, 0)) for b in blocks]


def _mix_out(a1, q, z, h1, lng, lnb, wsq, comm=None):
    def body(a1_ref, q_ref, ga_ref, gb_ref, h_ref, lng_ref, lnb_ref, wa_ref, wb_ref, wo_ref, h2_ref, ya_ref, yb_ref):
        _, _, a2, sg = _layernorm_silu(a1_ref[...].astype(F32), lng_ref[...], lnb_ref[...])
        ya = _nn((a2 * sg).astype(BF), wa_ref[...])
        yb = _nn(q_ref[...], wb_ref[...])
        ya_ref[...] = ya.astype(BF)
        yb_ref[...] = yb.astype(BF)
        m = _sig(ga_ref[...].astype(F32)) * ya + _sig(gb_ref[...].astype(F32)) * yb
        h2_ref[...] = h_ref[...] + _nn(m.astype(BF), wo_ref[...])

    row = lambda i: (i, 0)
    vec = pl.BlockSpec((1, D), lambda i: (0, 0))
    return _call(
        body, name="mix_out", grid=(T // TM,), args=[a1, q, z, z, h1, lng, lnb, wsq, wsq, wsq], comm=comm,
        in_specs=[pl.BlockSpec((TM, D), row), pl.BlockSpec((TM, D), row),
                  pl.BlockSpec((None, TM, D), lambda i: (5, i, 0)), pl.BlockSpec((None, TM, D), lambda i: (6, i, 0)),
                  pl.BlockSpec((TM, D), row), vec, vec] + _square_specs((0, 1, 2)),
        out_shape=[jax.ShapeDtypeStruct((T, D), F32), jax.ShapeDtypeStruct((T, D), BF), jax.ShapeDtypeStruct((T, D), BF)],
        out_specs=[pl.BlockSpec((TM, D), row)] * 3)


def _rmsnorm_bwd(xf, g, dn):
    r = lax.rsqrt(jnp.mean(xf * xf, axis=-1, keepdims=True) + EPS)
    xr = xf * r
    gdn = dn * g
    dx = r * gdn - xr * (r * jnp.mean(gdn * xr, axis=-1, keepdims=True))
    return dx, jnp.sum(dn * xr, axis=0, keepdims=True)


def _ffn_bwd_hidden(dh, gg, uu, wbuf, off, name, comm=None):
    nf = F // FC

    def body(dh_ref, gg_ref, uu_ref, b0, dgu_ref, wd_s, sem):
        _load_ffn_weights((b0,), (off,), (wd_s,), sem)
        dhb = (0.5 * dh_ref[...]).astype(BF)
        for c in range(nf):
            sl = slice(c * FC, (c + 1) * FC)
            da = _nt(dhb, wd_s[sl, :]).astype(BF)
            gb, ub = gg_ref[:, sl], uu_ref[:, sl]
            sg = _sig(gb)
            dgu_ref[0, :, sl] = (da * ub) * (sg * (1.0 + gb * (1.0 - sg)))
            dgu_ref[0, :, F + c * FC:F + (c + 1) * FC] = da * (gb * sg)

    row = lambda i: (i, 0)
    return _call(
        body, name=name, grid=(T // TM,), args=[dh, gg, uu, wbuf], comm=comm,
        in_specs=[pl.BlockSpec((TM, D), row), pl.BlockSpec((TM, F), row), pl.BlockSpec((TM, F), row), ANY],
        out_shape=[jax.ShapeDtypeStruct((1, T, 2 * F), BF)],
        out_specs=[pl.BlockSpec((1, TM, 2 * F), lambda i: (0, i, 0))],
        scratch_shapes=[pltpu.VMEM((F, D), BF), pltpu.SemaphoreType.DMA((1,))])


def _ffn_bwd_input(dgu, dh, x, g, wbufs, offs, name, comm=None):
    def body(dgu_ref, dh_ref, x_ref, g_ref, b0, b1, dx_ref, s_ref, w_s, sem):
        _load_ffn_weights((b0, b1), offs, (w_s.at[pl.ds(0, F), :], w_s.at[pl.ds(F, F), :]), sem)

        @pl.when(pl.program_id(0) == 0)
        def _():
            s_ref[...] = jnp.zeros_like(s_ref)

        dn = _nn(dgu_ref[0], w_s[...])
        dxn, dg = _rmsnorm_bwd(x_ref[...], g_ref[...], dn)
        dx_ref[...] = dh_ref[...] + dxn
        s_ref[0:1, :] += dg

    row = lambda i: (i, 0)
    return _call(
        body, name=name, grid=(T // TM,), args=[dgu, dh, x, g, *wbufs], comm=comm,
        in_specs=[pl.BlockSpec((1, TM, 2 * F), lambda i: (0, i, 0)), pl.BlockSpec((TM, D), row),
                  pl.BlockSpec((TM, D), row), pl.BlockSpec((1, D), lambda i: (0, 0)), ANY, ANY],
        out_shape=[jax.ShapeDtypeStruct((T, D), F32), jax.ShapeDtypeStruct((8, D), F32)],
        out_specs=[pl.BlockSpec((TM, D), row), pl.BlockSpec((8, D), lambda i: (0, 0))],
        scratch_shapes=[pltpu.VMEM((2 * F, D), BF), pltpu.SemaphoreType.DMA((2,))])


def _tn_matmul(lhs, rhs, tr, name, comm=None, scale=None):
    ng, _, cdim = lhs.shape
    nc, nk = cdim // tr, T // TK
    if rhs.ndim == 2:
        r_spec = pl.BlockSpec((TK, D), lambda g, c, k: (k, 0))
    else:
        r_spec = pl.BlockSpec((None, TK, D), lambda g, c, k: (g, k, 0))

    def body(l_ref, r_ref, o_ref, acc):
        k = pl.program_id(2)

        @pl.when(k == 0)
        def _():
            acc[...] = jnp.zeros_like(acc)

        r = r_ref[...] if scale is None else scale * r_ref[...]
        acc[...] += _tn(l_ref[...], r.astype(BF))

        @pl.when(k == nk - 1)
        def _():
            o_ref[...] = acc[...].astype(BF)

    return _call(
        body, name=name, grid=(ng, nc, nk), args=[lhs, rhs], comm=comm,
        in_specs=[pl.BlockSpec((None, TK, tr), lambda g, c, k: (g, k, c)), r_spec],
        out_shape=[jax.ShapeDtypeStruct((ng * cdim, D), BF)],
        out_specs=[pl.BlockSpec((tr, D), lambda g, c, k: (g * nc + c, 0))],
        scratch_shapes=[pltpu.VMEM((tr, D), F32)])


def _mix_out_bwd(dh2, ya, yb, z, a1, q, lng, lnb, wsq, comm=None):
    def body(dh_ref, ya_ref, yb_ref, ga_ref, gb_ref, a1_ref, q_ref, lng_ref, lnb_ref, wa_ref, wb_ref, wo_ref,
             dzg_ref, da1_ref, dq_ref, l_ref, r_ref, s_ref):
        @pl.when(pl.program_id(0) == 0)
        def _():
            s_ref[...] = jnp.zeros_like(s_ref)

        dhb = dh_ref[...].astype(BF)
        dm = _nt(dhb, wo_ref[...])
        ya, yb = ya_ref[...].astype(F32), yb_ref[...].astype(F32)
        sa, sb = _sig(ga_ref[...].astype(F32)), _sig(gb_ref[...].astype(F32))
        l_ref[0] = (sa * ya + sb * yb).astype(BF)
        l_ref[2] = q_ref[...]
        dzg_ref[0] = (dm * ya * (sa * (1.0 - sa))).astype(BF)
        dzg_ref[1] = (dm * yb * (sb * (1.0 - sb))).astype(BF)
        dya = (dm * sa).astype(BF)
        dyb = (dm * sb).astype(BF)
        r_ref[0] = dhb
        r_ref[1] = dya
        r_ref[2] = dyb
        dq_ref[...] = _nt(dyb, wb_ref[...]).astype(BF)
        da3 = _nt(dya, wa_ref[...])
        lng = lng_ref[...]
        xh, rs, a2, sg = _layernorm_silu(a1_ref[...].astype(F32), lng, lnb_ref[...])
        l_ref[1] = (a2 * sg).astype(BF)
        da2 = da3 * (sg * (1.0 + a2 * (1.0 - sg)))
        s_ref[0:1, :] += jnp.sum(da2 * xh, axis=0, keepdims=True)
        s_ref[1:2, :] += jnp.sum(da2, axis=0, keepdims=True)
        dxh = da2 * lng
        da1 = rs * (dxh - jnp.mean(dxh, axis=-1, keepdims=True) - xh * jnp.mean(dxh * xh, axis=-1, keepdims=True))
        da1_ref[...] = da1.astype(BF)
        s_ref[2:3, :] += jnp.sum(da1, axis=0, keepdims=True)

    row = lambda i: (i, 0)
    row3 = lambda i: (0, i, 0)
    vec = pl.BlockSpec((1, D), lambda i: (0, 0))
    return _call(
        body, name="mix_out_bwd", grid=(T // TM,), args=[dh2, ya, yb, z, z, a1, q, lng, lnb, wsq, wsq, wsq], comm=comm,
        in_specs=[pl.BlockSpec((TM, D), row), pl.BlockSpec((TM, D), row), pl.BlockSpec((TM, D), row),
                  pl.BlockSpec((None, TM, D), lambda i: (5, i, 0)), pl.BlockSpec((None, TM, D), lambda i: (6, i, 0)),
                  pl.BlockSpec((TM, D), row), pl.BlockSpec((TM, D), row), vec, vec] + _square_specs((0, 1, 2)),
        out_shape=[jax.ShapeDtypeStruct((2, T, D), BF), jax.ShapeDtypeStruct((T, D), BF),
                   jax.ShapeDtypeStruct((T, D), BF), jax.ShapeDtypeStruct((3, T, D), BF),
                   jax.ShapeDtypeStruct((3, T, D), BF), jax.ShapeDtypeStruct((8, D), F32)],
        out_specs=[pl.BlockSpec((2, TM, D), row3), pl.BlockSpec((TM, D), row), pl.BlockSpec((TM, D), row),
                   pl.BlockSpec((3, TM, D), row3), pl.BlockSpec((3, TM, D), row3), pl.BlockSpec((8, D), lambda i: (0, 0))])


def _mix_in_bwd(dz, dh2, h1, gm, win, comm=None):
    def body(dz_ref, w_any, dh_ref, h_ref, g_ref, o_ref, s_ref, w_s, sem):
        _load_ffn_weights((w_any,), (0,), (w_s,), sem)

        @pl.when(pl.program_id(0) == 0)
        def _():
            s_ref[...] = jnp.zeros_like(s_ref)

        du = _nn(dz_ref[0], w_s[0:D, :])
        for j in range(1, NG):
            du = du + _nn(dz_ref[j], w_s[j * D:(j + 1) * D, :])
        dx, dg = _rmsnorm_bwd(h_ref[...], g_ref[...], du)
        o_ref[...] = dh_ref[...] + dx
        s_ref[0:1, :] += dg

    row = lambda i: (i, 0)
    return _call(
        body, name="mix_in_bwd", grid=(T // TM,), args=[dz, win, dh2, h1, gm], comm=comm,
        in_specs=[pl.BlockSpec((NG, TM, D), lambda i: (0, i, 0)), ANY,
                  pl.BlockSpec((TM, D), row), pl.BlockSpec((TM, D), row), pl.BlockSpec((1, D), lambda i: (0, 0))],
        out_shape=[jax.ShapeDtypeStruct((T, D), F32), jax.ShapeDtypeStruct((8, D), F32)],
        out_specs=[pl.BlockSpec((TM, D), row), pl.BlockSpec((8, D), lambda i: (0, 0))],
        scratch_shapes=[pltpu.VMEM((NG * D, D), BF), pltpu.SemaphoreType.DMA((1,))])


def _row_tile(n, want, mult):
    for t in range(min(want, n), 0, -1):
        if n % t == 0 and t % mult == 0:
            return t
    return n


def _sum_slots(recv, name):
    ns, rows, cols = recv.shape
    tr = _row_tile(rows, 1024, 16)

    def body(r_ref, o_ref):
        s = r_ref[0].astype(F32)
        for k in range(1, ns):
            s = s + r_ref[k].astype(F32)
        o_ref[...] = s

    return _call(
        body, name=name, grid=(rows // tr,), args=[recv],
        in_specs=[pl.BlockSpec((ns, tr, cols), lambda i: (0, i, 0))],
        out_shape=[jax.ShapeDtypeStruct((rows, cols), F32)],
        out_specs=[pl.BlockSpec((tr, cols), lambda i: (i, 0))])[0]


def _pack_small(s_ffn1, s_in, s_mix, s_ffn2, s_final, dwa, dwb):
    def body(f1, mi, mo, f2, fl, wa_ref, wb_ref, v_ref, k_ref):
        for dst, (ref, row) in enumerate(((f1, 0), (mi, 0), (mo, 0), (mo, 1), (mo, 2), (f2, 0), (fl, 0), (fl, 1))):
            v_ref[dst:dst + 1, :] = ref[row:row + 1, :]
        for k in range(NDEV):
            k_ref[k, 0:32, :] = wa_ref[:, k * LANE:(k + 1) * LANE]
            k_ref[k, 32:40, :] = wb_ref[:, k * LANE:(k + 1) * LANE]

    return pl.pallas_call(
        body, name="pack_small",
        out_shape=(jax.ShapeDtypeStruct((8, D), F32), jax.ShapeDtypeStruct((NDEV, 40, LANE), F32)),
    )(s_ffn1, s_in, s_mix, s_ffn2, s_final, dwa, dwb)


def _sum_small(vecs, convs):
    def body(v_ref, k_ref, vs_ref, ks_ref, l_ref):
        s, c = v_ref[0], k_ref[0]
        for k in range(1, NDEV):
            s = s + v_ref[k]
            c = c + k_ref[k]
        vs_ref[...] = s
        ks_ref[...] = c
        l_ref[...] = jnp.broadcast_to(jnp.sum(s[7:8, :], axis=-1, keepdims=True), (8, LANE))

    return pl.pallas_call(
        body, name="sum_small",
        out_shape=(jax.ShapeDtypeStruct((8, D), F32), jax.ShapeDtypeStruct((40, LANE), F32),
                   jax.ShapeDtypeStruct((8, LANE), F32)),
    )(vecs, convs)


def _adam(gs, ws, ms, vs, name, comm=None):
    n = len(gs)
    rows, cols = ws[0].shape
    tr = _row_tile(rows, 256, 8)
    c1 = 1.0 - ADAM_B1 ** ADAM_STEP
    c2 = 1.0 - ADAM_B2 ** ADAM_STEP

    def body(*refs):
        for i in range(n):
            g, w, m, v = (refs[4 * i + k][...] for k in range(4))
            d_ref, m_ref, v_ref = refs[4 * n + 3 * i: 4 * n + 3 * i + 3]
            m2 = ADAM_B1 * m + (1.0 - ADAM_B1) * g
            v2 = ADAM_B2 * v + (1.0 - ADAM_B2) * (g * g)
            d_ref[...] = -ADAM_LR * ((m2 / c1) / (jnp.sqrt(v2 / c2) + ADAM_EPS) + ADAM_WD * w)
            m_ref[...] = m2
            v_ref[...] = v2

    spec = pl.BlockSpec((tr, cols), lambda i: (i, 0))
    args = []
    for i in range(n):
        args += [gs[i], ws[i], ms[i], vs[i]]
    outs = _call(body, name=name, grid=(rows // tr,), args=args, comm=comm, in_specs=[spec] * (4 * n),
                 out_shape=[jax.ShapeDtypeStruct((rows, cols), F32)] * (3 * n), out_specs=[spec] * (3 * n))
    return [tuple(outs[3 * i: 3 * i + 3]) for i in range(n)], outs[3 * n:]


def kernel(x, ffn1_norm, ffn1_w_gate, ffn1_w_up, ffn1_w_down, mix_norm, w_in, a_dw_w, a_dw_b, a_ln_g, a_ln_b, a_w_out, b_conv_w, b_w_out, w_o, ffn2_norm, ffn2_w_gate, ffn2_w_up, ffn2_w_down, final_norm, loss_target, m_ffn1_norm, m_ffn1_w_gate, m_ffn1_w_up, m_ffn1_w_down, m_mix_norm, m_w_in, m_a_dw_w, m_a_dw_b, m_a_ln_g, m_a_ln_b, m_a_w_out, m_b_conv_w, m_b_w_out, m_w_o, m_ffn2_norm, m_ffn2_w_gate, m_ffn2_w_up, m_ffn2_w_down, m_final_norm, v_ffn1_norm, v_ffn1_w_gate, v_ffn1_w_up, v_ffn1_w_down, v_mix_norm, v_w_in, v_a_dw_w, v_a_dw_b, v_a_ln_g, v_a_ln_b, v_a_w_out, v_b_conv_w, v_b_w_out, v_w_o, v_ffn2_norm, v_ffn2_w_gate, v_ffn2_w_up, v_ffn2_w_down, v_final_norm):
    names = ("ffn1_norm", "ffn1_w_gate", "ffn1_w_up", "ffn1_w_down", "mix_norm", "w_in", "a_dw_w", "a_dw_b",
             "a_ln_g", "a_ln_b", "a_w_out", "b_conv_w", "b_w_out", "w_o", "ffn2_norm", "ffn2_w_gate", "ffn2_w_up",
             "ffn2_w_down", "final_norm")
    w = dict(ffn1_norm=ffn1_norm, ffn1_w_gate=ffn1_w_gate, ffn1_w_up=ffn1_w_up, ffn1_w_down=ffn1_w_down,
             mix_norm=mix_norm, w_in=w_in, a_dw_w=a_dw_w, a_dw_b=a_dw_b, a_ln_g=a_ln_g, a_ln_b=a_ln_b,
             a_w_out=a_w_out, b_conv_w=b_conv_w, b_w_out=b_w_out, w_o=w_o, ffn2_norm=ffn2_norm,
             ffn2_w_gate=ffn2_w_gate, ffn2_w_up=ffn2_w_up, ffn2_w_down=ffn2_w_down, final_norm=final_norm)
    m = dict(ffn1_norm=m_ffn1_norm, ffn1_w_gate=m_ffn1_w_gate, ffn1_w_up=m_ffn1_w_up, ffn1_w_down=m_ffn1_w_down,
             mix_norm=m_mix_norm, w_in=m_w_in, a_dw_w=m_a_dw_w, a_dw_b=m_a_dw_b, a_ln_g=m_a_ln_g, a_ln_b=m_a_ln_b,
             a_w_out=m_a_w_out, b_conv_w=m_b_conv_w, b_w_out=m_b_w_out, w_o=m_w_o, ffn2_norm=m_ffn2_norm,
             ffn2_w_gate=m_ffn2_w_gate, ffn2_w_up=m_ffn2_w_up, ffn2_w_down=m_ffn2_w_down, final_norm=m_final_norm)
    v = dict(ffn1_norm=v_ffn1_norm, ffn1_w_gate=v_ffn1_w_gate, ffn1_w_up=v_ffn1_w_up, ffn1_w_down=v_ffn1_w_down,
             mix_norm=v_mix_norm, w_in=v_w_in, a_dw_w=v_a_dw_w, a_dw_b=v_a_dw_b, a_ln_g=v_a_ln_g, a_ln_b=v_a_ln_b,
             a_w_out=v_a_w_out, b_conv_w=v_b_conv_w, b_w_out=v_b_w_out, w_o=v_w_o, ffn2_norm=v_ffn2_norm,
             ffn2_w_gate=v_ffn2_w_gate, ffn2_w_up=v_ffn2_w_up, ffn2_w_down=v_ffn2_w_down, final_norm=v_final_norm)
    flat = _pack_weights(dict(wg1=ffn1_w_gate[0].T, wu1=ffn1_w_up[0].T, wd1=ffn1_w_down[0], wg2=ffn2_w_gate[0].T,
                              wu2=ffn2_w_up[0].T, wd2=ffn2_w_down[0], win=w_in[0], wa=a_w_out[0], wb=b_w_out[0],
                              wo=w_o[0]))
    cw_shard = jnp.concatenate([a_dw_w[0], jnp.zeros((1, LANE), F32), b_conv_w[0], jnp.zeros((5, LANE), F32)], axis=0)

    x2, tgt = x[0], loss_target[0]
    st_a, st_b, st_c, st_d, st_e = ("wg1", "wu1", "wd1"), ("win",), ("wa", "wb", "wo", "wg2"), ("wu2",), ("wd2",)

    buf_a, cw = _run_comm(_join(_ag_comm(st_a, flat), _direct_comm(cw_shard, False)), "ag_ffn1")
    h1, n1, gg1, uu1, act1, buf_b = _ffn_fwd(x2, ffn1_norm, (buf_a,) * 3, (0, F, 2 * F), "ffn1_fwd", _ag_comm(st_b, flat))
    u, z, buf_c = _mix_in(h1, mix_norm, buf_b, _ag_comm(st_c, flat))
    dft = _dft_constants()
    cw = jnp.transpose(cw, (1, 0, 2)).reshape(40, D)
    a1, q, buf_d = _conv_fwd_dft(z, cw, a_dw_b, dft, _ag_comm(st_d, flat))
    h2, ya, yb, buf_e = _mix_out(a1, q, z, h1, a_ln_g, a_ln_b, buf_c, _ag_comm(st_e, flat))
    ffn2_bufs, ffn2_offs = (buf_c, buf_d, buf_e), (3 * D, 0, 0)
    dh3, s_final, n2, gg2, uu2, act2 = _ffn_fwd(h2, ffn2_norm, ffn2_bufs, ffn2_offs, "ffn2_fwd",
                                          final=(final_norm.reshape(1, D), tgt))

    tr_f = F // 2 if (F // 2) % LANE == 0 else F
    def pair(stage, src):
        return _rs_pair_comm(stage, src)

    def chip(stage, src, pair_buf, tag):
        return _rs_chip_comm(_pair_add(stage, src, pair_buf, "pair_add_" + tag))

    (dgu2,) = _ffn_bwd_hidden(dh3, gg2, uu2, buf_e, 0, "ffn2_bwd_h")
    (gu2,) = _tn_matmul(dgu2, n2, tr_f, "dw_gu2")
    s2a, src2a = ("wg2", "wu2"), dict(wg2=(gu2, 0), wu2=(gu2, F))
    gd2, pair2a = _tn_matmul(act2, dh3, tr_f, "dw_d2", pair(s2a, src2a), scale=0.5)
    s2b, src2b = ("wd2",), dict(wd2=(gd2, 0))
    dh2, s_ffn2, pair2b = _ffn_bwd_input(dgu2, dh3, h2, ffn2_norm, (buf_c, buf_d), (3 * D, 0), "ffn2_bwd_x",
                                         pair(s2b, src2b))
    dzg, da1, dq, lsq, rsq, s_mix, recv2a = _mix_out_bwd(dh2, ya, yb, z, a1, q, a_ln_g, a_ln_b, buf_c,
                                                          chip(s2a, src2a, pair2a, "2a"))
    (gsq,) = _tn_matmul(lsq, rsq, D, "dw_square")
    ssq, srcsq = ("wa", "wb", "wo"), dict(wa=(gsq, D), wb=(gsq, 2 * D), wo=(gsq, 0))
    dz, dwa, dwb, recv2b, pairsq = _conv_bwd_dft(z, da1, dq, dzg, cw, dft,
                                                 _join(chip(s2b, src2b, pair2b, "2b"), pair(ssq, srcsq)))
    gin, recvsq = _tn_matmul(dz, u, D, "dw_in", chip(ssq, srcsq, pairsq, "sq"))
    sin_a, sin_b, srcin = ("win/0/2",), ("win/1/2",), {"win/0/2": (gin, 0), "win/1/2": (gin, 0)}
    dh1, s_in, pairin_a, pairin_b = _mix_in_bwd(dz, dh2, h1, mix_norm, buf_b,
                                                _join(pair(sin_a, srcin), pair(sin_b, srcin)))
    dgu1, recvin_a = _ffn_bwd_hidden(dh1, gg1, uu1, buf_a, 2 * F, "ffn1_bwd_h",
                                           chip(sin_a, srcin, pairin_a, "in_a"))
    gu1, recvin_b = _tn_matmul(dgu1, n1, tr_f, "dw_gu1", chip(sin_b, srcin, pairin_b, "in_b"))
    s1a, src1a = ("wg1", "wu1"), dict(wg1=(gu1, 0), wu1=(gu1, F))
    gd1, pair1a = _tn_matmul(act1, dh1, tr_f, "dw_d1", pair(s1a, src1a), scale=0.5)
    s1b, src1b = ("wd1",), dict(wd1=(gd1, 0))
    dx, s_ffn1, recv1a, pair1b = _ffn_bwd_input(dgu1, dh1, x2, ffn1_norm, (buf_a, buf_a), (0, F), "ffn1_bwd_x",
                                                _join(chip(s1a, src1a, pair1a, "1a"), pair(s1b, src1b)))
    gsum = {}

    def sum_stage(stage, recv, tag):
        st, total = _Stage(stage), _sum_slots(recv, "sum_" + tag)
        for n in stage:
            gsum[n] = total[st.off[n]:st.off[n] + st.rows[n]]

    for stage, recv, tag in ((s2a, recv2a, "2a"), (s2b, recv2b, "2b"), (ssq, recvsq, "sq"), (sin_a, recvin_a, "in_a"),
                             (sin_b, recvin_b, "in_b"), (s1a, recv1a, "1a")):
        sum_stage(stage, recv, tag)
    gsum["win"] = jnp.concatenate([gsum["win/0/2"], gsum["win/1/2"]], axis=0)

    vec8, convk = _pack_small(s_ffn1, s_in, s_mix, s_ffn2, s_final, dwa, dwb)
    vec_all, conv_all = _run_comm(_join(_direct_comm(vec8, False), _direct_comm(convk, True)), "xchg_small")
    vec_sum, conv_sum, loss_blk = _sum_small(vec_all, conv_all)
    loss = loss_blk[0, 0]

    g = dict(ffn1_w_gate=gsum["wg1"], ffn1_w_up=gsum["wu1"],
             ffn2_w_gate=gsum["wg2"], ffn2_w_up=gsum["wu2"], ffn2_w_down=gsum["wd2"], w_in=gsum["win"].T,
             a_w_out=gsum["wa"], b_w_out=gsum["wb"], w_o=gsum["wo"],
             ffn1_norm=vec_sum[0:1], mix_norm=vec_sum[1:2], a_ln_g=vec_sum[2:3], a_ln_b=vec_sum[3:4],
             a_dw_b=vec_sum[4:5], ffn2_norm=vec_sum[5:6], final_norm=vec_sum[6:7],
             a_dw_w=conv_sum[0:KA], b_conv_w=conv_sum[32:32 + KB])
    gate_up = ("ffn1_w_gate", "ffn1_w_up", "ffn2_w_gate", "ffn2_w_up")

    upd = {}

    def run(group, name, as2d=lambda a: a[0], back=lambda a, n: a.reshape(w[n].shape), comm=None):
        res, extra = _adam([g[n] for n in group], [as2d(w[n]) for n in group], [as2d(m[n]) for n in group],
                           [as2d(v[n]) for n in group], name, comm)
        for n, r in zip(group, res):
            upd[n] = tuple(back(a, n) for a in r)
        return extra

    (recv1b,) = run(gate_up, "adam_gate_up", as2d=lambda a: a[0].T, back=lambda a, n: a.T[None],
                    comm=chip(s1b, src1b, pair1b, "1b"))
    for n in gate_up:
        g[n] = g[n].T
    sum_stage(s1b, recv1b, "1b")
    g["ffn1_w_down"] = gsum["wd1"]
    run(("ffn1_w_down", "ffn2_w_down"), "adam_down")
    run(("w_in",), "adam_in")
    run(("a_w_out", "b_w_out", "w_o"), "adam_square")
    run(("a_dw_w",), "adam_dw")
    run(("b_conv_w",), "adam_conv")
    vecs = ("ffn1_norm", "mix_norm", "a_dw_b", "a_ln_g", "a_ln_b", "ffn2_norm", "final_norm")
    run(vecs, "adam_vec", as2d=lambda a: a.reshape(1, D))

    grads = [g[n].reshape(w[n].shape) for n in names]
    return (loss, dx.reshape(x.shape), *grads, *[upd[n][0] for n in names], *[upd[n][1] for n in names],
            *[upd[n][2] for n in names])
```

```python
import jax
import jax.numpy as jnp
from jax import lax
from jax.experimental import pallas as pl
from jax.experimental.pallas import tpu as pltpu

T = 4096
D = 1024
F = 2816
NG = 7
NDEV = 8
NCHIP = 4
KA, KB = 31, 3
EPS = 1e-6
ADAM_LR, ADAM_B1, ADAM_B2, ADAM_EPS, ADAM_WD, ADAM_STEP = 0.001, 0.9, 0.999, 1e-08, 0.01, 10

TM = 512
FC = 256
TB = 1024
NB = 256
HB = NB // 2
CW = 256
CHB = 64
LANE = 128
TK = 2048
VMEM_LIMIT = 56 * 1024 * 1024

BF = jnp.bfloat16
F32 = jnp.float32
MESH = pl.DeviceIdType.MESH
ANY = pl.BlockSpec(memory_space=pl.ANY)

ORDER = ("wg1", "wu1", "wd1", "wg2", "wu2", "wd2", "win", "wa", "wb", "wo")


class _Layout:
    def __init__(self):
        fs, dis, ds = F // NDEV, NG * D // NDEV, D // NDEV
        self.rows = dict(wg1=fs, wu1=fs, wd1=fs, wg2=fs, wu2=fs, wd2=fs, win=dis, wa=ds, wb=ds, wo=ds)
        self.fl, off = {}, 0
        for n in ORDER:
            self.fl[n] = off
            off += self.rows[n]
        self.RT = off


class _Stage:
    def __init__(self, names):
        lay = _Layout()
        self.names = names
        self.rows, self.full, self.sub, self.fl = {}, {}, {}, {}
        for n in names:
            base, i, k = (n.split("/") + ["0", "1"])[:3]
            self.full[n] = lay.rows[base]
            self.rows[n] = lay.rows[base] // int(k)
            self.sub[n] = int(i) * self.rows[n]
            self.fl[n] = lay.fl[base] + self.sub[n]
        self.off, self.wc, o, w = {}, {}, 0, 0
        for n in names:
            self.off[n], self.wc[n] = o, w
            o += self.rows[n]
            w += NDEV * self.rows[n]
        self.R, self.W = o, w

    def grad_row(self, n, first, dev_lin):
        return first + dev_lin * self.full[n] + self.sub[n]


def _nt(a, b):
    return lax.dot_general(a, b, (((1,), (1,)), ((), ())), preferred_element_type=F32)


def _nn(a, b):
    return lax.dot_general(a, b, (((1,), (0,)), ((), ())), preferred_element_type=F32)


def _tn(a, b):
    return lax.dot_general(a, b, (((0,), (0,)), ((), ())), preferred_element_type=F32)


def _sig(x):
    return 1.0 / (1.0 + jnp.exp(-x))


def _position():
    return lax.axis_index("x"), lax.axis_index("y"), lax.axis_index("c")


def _peer(pos, j):
    x, y, c = pos
    return (1 - x if j & 4 else x, 1 - y if j & 2 else y, 1 - c if j & 1 else c)


def _lin(pos):
    return 4 * pos[0] + 2 * pos[1] + pos[2]


def _chip(pos):
    return 2 * pos[0] + pos[1]


class _Comm:
    def __init__(self, inputs, out_shapes, scratch, start, finish, middle=None):
        self.inputs, self.out_shapes, self.scratch = inputs, out_shapes, scratch
        self.start, self.finish, self.middle = start, finish, middle


def _call(body, *, name, grid, args, in_specs, out_shape, out_specs, scratch_shapes=(), comm=None,
          num_scalar_prefetch=0):
    in_specs, out_shape, out_specs, scratch_shapes = list(in_specs), list(out_shape), list(out_specs), list(scratch_shapes)
    n_in, n_out, n_scr = len(in_specs), len(out_shape), len(scratch_shapes)
    sp = num_scalar_prefetch
    if comm is None:
        kernel_fn = lambda *refs: body(*refs)
        c_in = c_out = c_scr = 0
    else:
        c_in, c_out, c_scr = len(comm.inputs), len(comm.out_shapes), len(comm.scratch)

        def kernel_fn(*refs):
            pre, refs = refs[:sp], refs[sp:]
            ins, cins = refs[:n_in], refs[n_in:n_in + c_in]
            o0 = n_in + c_in
            outs, couts = refs[o0:o0 + n_out], refs[o0 + n_out:o0 + n_out + c_out]
            s0 = o0 + n_out + c_out
            scr, cscr = refs[s0:s0 + n_scr], refs[s0 + n_scr:]
            step, steps = pl.program_id(0), grid[0]
            for a in range(1, len(grid)):
                step, steps = step * grid[a] + pl.program_id(a), steps * grid[a]
            first, last = step == 0, step == steps - 1

            @pl.when(first)
            def _():
                comm.start(cins, couts, cscr)

            if comm.middle is not None:
                @pl.when(step == (steps // 2 if steps > 2 else steps - 1))
                def _():
                    comm.middle(cins, couts, cscr)

            body(*pre, *ins, *outs, *scr)

            @pl.when(last)
            def _():
                comm.finish(cins, couts, cscr)

        args = list(args) + list(comm.inputs)
        in_specs += [ANY] * c_in
        out_shape += list(comm.out_shapes)
        out_specs += [ANY] * c_out
        scratch_shapes += list(comm.scratch)
    params = pltpu.CompilerParams(dimension_semantics=("arbitrary",) * len(grid), vmem_limit_bytes=VMEM_LIMIT)
    if sp:
        grid_spec = pltpu.PrefetchScalarGridSpec(num_scalar_prefetch=sp, grid=grid, in_specs=in_specs,
                                                 out_specs=out_specs, scratch_shapes=scratch_shapes)
        return pl.pallas_call(kernel_fn, name=name, grid_spec=grid_spec, out_shape=out_shape,
                              compiler_params=params)(*args)
    return pl.pallas_call(kernel_fn, name=name, grid=grid, in_specs=in_specs, out_shape=out_shape, out_specs=out_specs,
                          scratch_shapes=scratch_shapes, compiler_params=params)(*args)


def _join(a, b):
    na = (len(a.inputs), len(a.out_shapes), len(a.scratch))

    def split(refs):
        return ([r[:n] for r, n in zip(refs, na)], [r[n:] for r, n in zip(refs, na)])

    def start(*refs):
        ra, rb = split(refs)
        a.start(*ra)
        b.start(*rb)

    def finish(*refs):
        ra, rb = split(refs)
        a.finish(*ra)
        b.finish(*rb)

    def middle(*refs):
        for stage, r in zip((a, b), split(refs)):
            if stage.middle is not None:
                stage.middle(*r)

    return _Comm(list(a.inputs) + list(b.inputs), list(a.out_shapes) + list(b.out_shapes),
                 list(a.scratch) + list(b.scratch), start, finish,
                 middle if (a.middle is not None or b.middle is not None) else None)


def _run_comm(comm, name):
    def body(*refs):
        c_in, c_out = len(comm.inputs), len(comm.out_shapes)
        parts = (refs[:c_in], refs[c_in:c_in + c_out], refs[c_in + c_out:])
        comm.start(*parts)
        if comm.middle is not None:
            comm.middle(*parts)
        comm.finish(*parts)

    return pl.pallas_call(
        body, name=name, out_shape=list(comm.out_shapes), in_specs=[ANY] * len(comm.inputs),
        out_specs=[ANY] * len(comm.out_shapes), scratch_shapes=list(comm.scratch))(*comm.inputs)


def _ag_comm(names, flat):
    st = _Stage(names)

    def ring(me):
        x, y, c = me
        diagonal = x == y
        up = (jnp.where(diagonal, x, 1 - x), jnp.where(diagonal, 1 - y, y), c)
        down = (jnp.where(diagonal, 1 - x, x), jnp.where(diagonal, y, 1 - y), c)
        low = c == 0
        passed = tuple(jnp.where(low, d, u) for d, u in zip(down, up))
        target = tuple(jnp.where(low, u, d) for d, u in zip(down, up))
        return up, down, (1 - x, 1 - y, c), passed, target

    def parts(refs):
        (flat_ref,), (out_ref,), (send_sems, recv_sems, local_sem) = refs
        me = _position()

        def region(name, dev):
            r = st.rows[name]
            return out_ref.at[pl.ds(st.wc[name] + _lin(dev) * r, r), :]

        def own(name):
            return flat_ref.at[pl.ds(st.fl[name], st.rows[name]), :]

        def copies(k, dev, to, from_flat):
            return [pltpu.make_async_remote_copy(
                src_ref=own(n) if from_flat else region(n, dev), dst_ref=region(n, dev), send_sem=send_sems.at[k],
                recv_sem=recv_sems.at[k], device_id=to, device_id_type=MESH) for n in names]

        def whole(k):
            return pltpu.make_async_remote_copy(
                src_ref=flat_ref.at[pl.ds(0, st.R), :], dst_ref=out_ref.at[pl.ds(0, st.R), :],
                send_sem=send_sems.at[k], recv_sem=recv_sems.at[k], device_id=me, device_id_type=MESH)

        return me, region, own, copies, whole, flat_ref, out_ref, local_sem

    def start(*refs):
        me, region, own, copies, _, _, _, local_sem = parts(refs)
        for n in names:
            pltpu.make_async_copy(own(n), region(n, me), local_sem).start()
        up, down, _, _, _ = ring(me)
        for k, to in ((1, up), (2, down), (0, _peer(me, 1))):
            for cp in copies(k, me, to, True):
                cp.start()

    def middle(*refs):
        me, _, _, copies, whole, _, _, _ = parts(refs)
        up, down, _, passed, target = ring(me)
        sib = _peer(me, 1)
        whole(1).wait_recv()
        whole(2).wait_recv()
        for k, dev, to in ((3, passed, target), (4, down, sib), (5, up, sib)):
            for cp in copies(k, dev, to, False):
                cp.start()

    def finish(*refs):
        me, _, _, copies, whole, flat_ref, out_ref, local_sem = parts(refs)
        _, _, across, _, _ = ring(me)
        whole(3).wait_recv()
        for cp in copies(6, across, _peer(me, 1), False):
            cp.start()
        whole(0).wait_recv()
        for j in range(3):
            whole(4 + j).wait_recv()
        for k in range(7):
            whole(k).wait_send()
        pltpu.make_async_copy(flat_ref.at[pl.ds(0, st.R), :], out_ref.at[pl.ds(0, st.R), :], local_sem).wait()

    return _Comm([flat], [jax.ShapeDtypeStruct((st.W, D), BF)],
                 [pltpu.SemaphoreType.DMA((7,)), pltpu.SemaphoreType.DMA((7,)), pltpu.SemaphoreType.DMA],
                 start, finish, middle)


def _rs_pair_comm(names, src):
    st = _Stage(names)
    arrays = []
    for n in names:
        if not any(src[n][0] is a for a in arrays):
            arrays.append(src[n][0])
    idx = {n: [i for i, a in enumerate(arrays) if a is src[n][0]][0] for n in names}

    def slot_wait(refs):
        recv = refs[1][0]
        send_sem, recv_sem = refs[2]
        return pltpu.make_async_remote_copy(src_ref=recv, dst_ref=recv, send_sem=send_sem, recv_sem=recv_sem,
                                            device_id=_position(), device_id_type=MESH)

    def start(*refs):
        ins, (recv,), (send_sem, recv_sem) = refs
        me = _position()
        sib = _peer(me, 1)
        for q in range(NCHIP):
            dev = (q // 2, q % 2, sib[2])
            for n in names:
                r = st.rows[n]
                pltpu.make_async_remote_copy(
                    src_ref=ins[idx[n]].at[pl.ds(st.grad_row(n, src[n][1], _lin(dev)), r), :],
                    dst_ref=recv.at[q, pl.ds(st.off[n], r), :], send_sem=send_sem, recv_sem=recv_sem,
                    device_id=sib, device_id_type=MESH).start()

    def finish(*refs):
        w = slot_wait(refs)
        w.wait_recv()
        w.wait_send()

    return _Comm(arrays, [jax.ShapeDtypeStruct((NCHIP, st.R, D), BF)],
                 [pltpu.SemaphoreType.DMA, pltpu.SemaphoreType.DMA], start, finish)


def _pair_add(names, src, recv, name):
    st = _Stage(names)
    c_arr = jnp.reshape(lax.axis_index("c"), (1,)).astype(jnp.int32)

    def body(c_ref, *refs):
        r_ref, o_ref = refs[len(names)], refs[len(names) + 1]
        for a_ref, n in zip(refs, names):
            rows = slice(st.off[n], st.off[n] + st.rows[n])
            o_ref[rows, :] = (a_ref[...].astype(F32) + r_ref[rows, :].astype(F32)).astype(BF)

    def shard_spec(n):
        r = st.rows[n]
        base, step = st.grad_row(n, src[n][1], 0) // r, st.full[n] // r
        return pl.BlockSpec((r, D), lambda q, c_ref: (base + step * (2 * q + c_ref[0]), 0))

    slot = pl.BlockSpec((None, st.R, D), lambda q, c_ref: (q, 0, 0))
    return _call(body, name=name, grid=(NCHIP,), args=[c_arr] + [src[n][0] for n in names] + [recv],
                 in_specs=[shard_spec(n) for n in names] + [slot],
                 out_shape=[jax.ShapeDtypeStruct((NCHIP, st.R, D), BF)], out_specs=[slot], num_scalar_prefetch=1)[0]


def _rs_chip_comm(part):
    def copies(refs):
        (p_ref,), (recv,), (send_sems, recv_sems, local_sem) = refs
        me = _position()
        mine = pltpu.make_async_copy(p_ref.at[_chip(me)], recv.at[_chip(me)], local_sem)
        out = []
        for j, bits in enumerate((4, 2, 6)):
            to = _peer(me, bits)
            out.append(pltpu.make_async_remote_copy(
                src_ref=p_ref.at[_chip(to)], dst_ref=recv.at[_chip(me)], send_sem=send_sems.at[j],
                recv_sem=recv_sems.at[j], device_id=to, device_id_type=MESH))
        return mine, out

    def start(*refs):
        mine, out = copies(refs)
        mine.start()
        for cp in out:
            cp.start()

    def finish(*refs):
        mine, out = copies(refs)
        for cp in out:
            cp.wait_recv()
        for cp in out:
            cp.wait_send()
        mine.wait()

    return _Comm([part], [jax.ShapeDtypeStruct(part.shape, BF)],
                 [pltpu.SemaphoreType.DMA((3,)), pltpu.SemaphoreType.DMA((3,)), pltpu.SemaphoreType.DMA],
                 start, finish)


def _direct_comm(x, scatter):
    def copies(refs):
        (x_ref,), (out_ref,), (send_sems, recv_sems, local_sem) = refs
        me = _position()

        def piece(dev):
            return x_ref.at[_lin(dev)] if scatter else x_ref

        mine = pltpu.make_async_copy(piece(me), out_ref.at[_lin(me)], local_sem)
        return mine, [pltpu.make_async_remote_copy(
            src_ref=piece(_peer(me, j)), dst_ref=out_ref.at[_lin(me)], send_sem=send_sems.at[j - 1],
            recv_sem=recv_sems.at[j - 1], device_id=_peer(me, j), device_id_type=MESH) for j in range(1, NDEV)]

    def start(*refs):
        mine, cps = copies(refs)
        mine.start()
        for cp in cps:
            cp.start()

    def finish(*refs):
        mine, cps = copies(refs)
        for cp in cps:
            cp.wait_recv()
        for cp in cps:
            cp.wait_send()
        mine.wait()

    shape = x.shape if scatter else (NDEV,) + x.shape
    return _Comm([x], [jax.ShapeDtypeStruct(shape, x.dtype)],
                 [pltpu.SemaphoreType.DMA((7,)), pltpu.SemaphoreType.DMA((7,)), pltpu.SemaphoreType.DMA],
                 start, finish)


def _pack_weights(shards):
    lay = _Layout()

    def body(*refs):
        o_ref = refs[-1]
        for ref, n in zip(refs, ORDER):
            x = ref[...].T if n == "win" else ref[...]
            o_ref[lay.fl[n]:lay.fl[n] + lay.rows[n], :] = x.astype(BF)

    return pl.pallas_call(
        body, name="pack_weights", out_shape=jax.ShapeDtypeStruct((lay.RT, D), BF),
        compiler_params=pltpu.CompilerParams(vmem_limit_bytes=VMEM_LIMIT))(*[shards[n] for n in ORDER])


def _load_ffn_weights(srcs, offs, scratch, sem):
    @pl.when(pl.program_id(0) == 0)
    def _():
        cps = [pltpu.make_async_copy(s.at[pl.ds(off, dst.shape[0]), :], dst, sem.at[i])
               for i, (s, off, dst) in enumerate(zip(srcs, offs, scratch))]
        for cp in cps:
            cp.start()
        for cp in cps:
            cp.wait()


def _final_loss_tile(xf, g, tgt, s_ref):
    r = lax.rsqrt(jnp.mean(xf * xf, axis=-1, keepdims=True) + EPS)
    xr = xf * r
    e = xr * g - tgt
    s_ref[1:2, :] += jnp.sum(e * e, axis=0, keepdims=True) * (0.5 / D)
    dy = e * (1.0 / D)
    s_ref[0:1, :] += jnp.sum(dy * xr, axis=0, keepdims=True)
    gdy = dy * g
    return r * gdy - xr * (r * jnp.mean(gdy * xr, axis=-1, keepdims=True))


def _ffn_fwd(x, g, wbufs, offs, name, comm=None, final=None):
    nf = F // FC

    def body(x_ref, g_ref, b0, b1, b2, *rest):
        if final is None:
            h_ref, n_ref, gg_ref, uu_ref, a_ref, wg_s, wu_s, wd_s, sem = rest
        else:
            gf_ref, t_ref, dh_ref, s_ref, n_ref, gg_ref, uu_ref, a_ref, wg_s, wu_s, wd_s, sem = rest

            @pl.when(pl.program_id(0) == 0)
            def _():
                s_ref[...] = jnp.zeros_like(s_ref)

        _load_ffn_weights((b0, b1, b2), offs, (wg_s, wu_s, wd_s), sem)
        xf = x_ref[...]
        r = lax.rsqrt(jnp.mean(xf * xf, axis=-1, keepdims=True) + EPS)
        nb = (xf * r * g_ref[...]).astype(BF)
        n_ref[...] = nb
        acc = jnp.zeros((TM, D), F32)
        for c in range(nf):
            sl = slice(c * FC, (c + 1) * FC)
            gb = _nt(nb, wg_s[sl, :]).astype(BF)
            ub = _nt(nb, wu_s[sl, :]).astype(BF)
            gg_ref[:, sl] = gb
            uu_ref[:, sl] = ub
            a = (gb * _sig(gb)) * ub
            a_ref[0, :, sl] = a
            acc = acc + _nn(a, wd_s[sl, :])
        h = xf + 0.5 * acc
        if final is None:
            h_ref[...] = h
        else:
            dh_ref[...] = _final_loss_tile(h, gf_ref[...], t_ref[...], s_ref)

    row = lambda i: (i, 0)
    vec = pl.BlockSpec((1, D), lambda i: (0, 0))
    tile = pl.BlockSpec((TM, D), row)
    saved_shapes = [jax.ShapeDtypeStruct((T, D), BF), jax.ShapeDtypeStruct((T, F), BF), jax.ShapeDtypeStruct((T, F), BF),
                    jax.ShapeDtypeStruct((1, T, F), BF)]
    saved_specs = [tile, pl.BlockSpec((TM, F), row), pl.BlockSpec((TM, F), row),
                   pl.BlockSpec((1, TM, F), lambda i: (0, i, 0))]
    if final is None:
        extra_args, extra_specs = [], []
        head_shapes, head_specs = [jax.ShapeDtypeStruct((T, D), F32)], [tile]
    else:
        extra_args, extra_specs = list(final), [vec, tile]
        head_shapes = [jax.ShapeDtypeStruct((T, D), F32), jax.ShapeDtypeStruct((8, D), F32)]
        head_specs = [tile, pl.BlockSpec((8, D), lambda i: (0, 0))]
    return _call(
        body, name=name, grid=(T // TM,), args=[x, g, *wbufs, *extra_args], comm=comm,
        in_specs=[tile, vec, ANY, ANY, ANY] + extra_specs,
        out_shape=head_shapes + saved_shapes, out_specs=head_specs + saved_specs,
        scratch_shapes=[pltpu.VMEM((F, D), BF)] * 3 + [pltpu.SemaphoreType.DMA((3,))])


def _mix_in(h1, gm, win, comm=None):
    def body(h_ref, g_ref, w_any, u_ref, z_ref, w_s, sem):
        _load_ffn_weights((w_any,), (0,), (w_s,), sem)
        xf = h_ref[...]
        r = lax.rsqrt(jnp.mean(xf * xf, axis=-1, keepdims=True) + EPS)
        ub = (xf * r * g_ref[...]).astype(BF)
        u_ref[...] = ub
        for j in range(NG):
            z_ref[j] = _nt(ub, w_s[j * D:(j + 1) * D, :]).astype(BF)

    row = lambda i: (i, 0)
    return _call(
        body, name="mix_in", grid=(T // TM,), args=[h1, gm, win], comm=comm,
        in_specs=[pl.BlockSpec((TM, D), row), pl.BlockSpec((1, D), lambda i: (0, 0)), ANY],
        out_shape=[jax.ShapeDtypeStruct((T, D), BF), jax.ShapeDtypeStruct((NG, T, D), BF)],
        out_specs=[pl.BlockSpec((TM, D), row), pl.BlockSpec((NG, TM, D), lambda i: (0, i, 0))],
        scratch_shapes=[pltpu.VMEM((NG * D, D), BF), pltpu.SemaphoreType.DMA((1,))])


def _shift_up(w, b):
    return w if b == 0 else pltpu.roll(w, w.shape[0] - b, 0)


def _fold8(p):
    red = p[0:8, :]
    for i in range(1, p.shape[0] // 8):
        red = red + p[8 * i:8 * i + 8, :]
    return red


def _dft_constants():
    import numpy as np
    nh = NB // 2
    f, n = np.arange(nh)[:, None], np.arange(NB)[None, :]
    ang = 2.0 * np.pi / NB * f * n
    fc = np.cos(ang)
    fs = np.where(f == 0, (-1.0) ** n, np.sin(ang))
    scale = np.where(f == 0, 1.0, 2.0) / NB
    ic = (scale * np.cos(ang)).T
    isn = np.where(f == 0, (-1.0) ** n / NB, scale * np.sin(ang)).T
    d = (KA - 1 - np.arange(32))[None, :]
    valid = (np.arange(32) < KA)[None, :]
    angk = 2.0 * np.pi / NB * f * d
    kc = np.where(valid, np.cos(angk), 0.0)
    ks = np.where(valid, np.sin(angk), 0.0)
    k2 = np.where(valid, np.where(f == 0, (-1.0) ** d, np.cos(angk)), 0.0)
    rtc = np.where(valid, scale * np.cos(angk), 0.0).T
    rts = np.where(valid, np.where(f == 0, (-1.0) ** d / NB, scale * np.sin(angk)), 0.0).T

    def bf(a):
        return jnp.asarray(a, F32).astype(BF)

    def split(a):
        hi = bf(a)
        return hi, (jnp.asarray(a, F32) - hi.astype(F32)).astype(BF)

    return dict(fc=bf(fc), fs=bf(fs), ic_hi=bf(ic[HB:]), is_hi=bf(isn[HB:]), ic_lo=bf(ic[:HB]), is_lo=bf(isn[:HB]),
                kc=split(kc), ks=split(ks), k2=split(k2), rtc=split(rtc), rts=split(rts))


def _dot3(m_hi, m_lo, x):
    x_hi = x.astype(BF)
    x_lo = (x - x_hi.astype(F32)).astype(BF)
    return _nn(m_hi, x_hi) + _nn(m_hi, x_lo) + _nn(m_lo, x_hi)


def _whole(a):
    return pl.BlockSpec(a.shape, lambda c, t: (0,) * a.ndim)


def _filter_spectrum(cw_ref, tabs, hc, hs, h2):
    w32 = cw_ref[0:32, :]
    for (hi, lo), dst in zip(tabs, (hc, hs, h2)):
        dst[...] = _dot3(hi[...], lo[...], w32)


def _conv_fwd_dft(z, cw, bias, dft, comm=None):
    nt = T // TB
    hb = TB // HB

    def body(z_ref, zh_ref, cw_ref, b_ref, fc_ref, fs_ref, ic_ref, is_ref, kch, kcl, ksh, ksl, k2h, k2l,
             a1_ref, q_ref, aext, ppad, hc, hs, h2):
        first = pl.program_id(1) == 0
        f = lambda ref, j: ref[j].astype(F32)

        @pl.when(first)
        def _():
            _filter_spectrum(cw_ref, ((kch, kcl), (ksh, ksl), (k2h, k2l)), hc, hs, h2)

        aext[0:HB, :] = jnp.where(first, 0.0, f(zh_ref, 0) * _sig(f(zh_ref, 1))).astype(BF)
        aext[HB:, :] = (f(z_ref, 0) * _sig(f(z_ref, 1))).astype(BF)
        ppad[0:8, :] = jnp.where(first, 0.0, f(zh_ref, 3)[HB - 8:HB, :] * f(zh_ref, 4)[HB - 8:HB, :])
        ppad[8:, :] = f(z_ref, 3) * f(z_ref, 4)
        bias_row = b_ref[...]

        for j in range(TB // HB):
            xs = aext[j * HB:j * HB + NB, :]
            xa, xb = _nn(fc_ref[...], xs), _nn(fs_ref[...], xs)
            yc = (hc[...] * xa - hs[...] * xb).astype(BF)
            ys = (h2[...] * xb + hs[...] * xa).astype(BF)
            y = _nn(ic_ref[...], yc) + _nn(is_ref[...], ys)
            a1_ref[j * HB:(j + 1) * HB, :] = (y + bias_row).astype(BF)

        def chunk(r, carry):
            base = pl.multiple_of(r * CHB, CHB)
            pw = ppad[pl.ds(base, CHB + 8), :]
            v = (cw_ref[pl.ds(32, 1), :] * _shift_up(pw, 6)[0:CHB, :]
                 + cw_ref[pl.ds(33, 1), :] * _shift_up(pw, 7)[0:CHB, :]
                 + cw_ref[pl.ds(34, 1), :] * pw[8:8 + CHB, :])
            q_ref[pl.ds(base, CHB), :] = (z_ref[2, pl.ds(base, CHB), :].astype(F32) * v).astype(BF)
            return carry

        lax.fori_loop(0, TB // CHB, chunk, 0)

    blk = pl.BlockSpec((TB, CW), lambda c, t: (t, c))
    tabs = [dft["fc"], dft["fs"], dft["ic_hi"], dft["is_hi"], *dft["kc"], *dft["ks"], *dft["k2"]]
    return _call(
        body, name="conv_fwd", grid=(D // CW, nt), comm=comm, args=[z, z, cw, bias] + tabs,
        in_specs=[pl.BlockSpec((5, TB, CW), lambda c, t: (0, t, c)),
                  pl.BlockSpec((5, HB, CW), lambda c, t: (0, jnp.maximum(t * hb - 1, 0), c)),
                  pl.BlockSpec((40, CW), lambda c, t: (0, c)), pl.BlockSpec((1, CW), lambda c, t: (0, c))]
                 + [_whole(a) for a in tabs],
        out_shape=[jax.ShapeDtypeStruct((T, D), BF), jax.ShapeDtypeStruct((T, D), BF)], out_specs=[blk, blk],
        scratch_shapes=[pltpu.VMEM((TB + HB, CW), BF), pltpu.VMEM((TB + 8, CW), F32)]
                       + [pltpu.VMEM((NB // 2, CW), F32)] * 3)


def _conv_bwd_dft(z, da1, dq, dzg, cw, dft, comm=None):
    nt = T // TB
    hb = TB // HB
    last_h = T // HB - 1

    def body(z_ref, zp_ref, zn_ref, da1_ref, da1n_ref, dq_ref, dqn_ref, dzg_ref, cw_ref,
             fc_ref, fs_ref, ic_ref, is_ref, kch, kcl, ksh, ksl, k2h, k2l, rch, rcl, rsh, rsl,
             dz_ref, dwa_ref, dwb_ref, aext, dyext, ppad, dvpad, hc, hs, h2, rc, rs, nyq, acc_b):
        t = pl.program_id(1)
        first, last = t == 0, t == nt - 1
        f = lambda ref, j: ref[j].astype(F32)

        @pl.when(first)
        def _():
            _filter_spectrum(cw_ref, ((kch, kcl), (ksh, ksl), (k2h, k2l)), hc, hs, h2)
            rc[...] = jnp.zeros_like(rc)
            rs[...] = jnp.zeros_like(rs)
            nyq[...] = jnp.zeros_like(nyq)
            acc_b[...] = jnp.zeros_like(acc_b)

        aext[0:HB, :] = jnp.where(first, 0.0, f(zp_ref, 0) * _sig(f(zp_ref, 1))).astype(BF)
        aext[HB:, :] = (f(z_ref, 0) * _sig(f(z_ref, 1))).astype(BF)
        dyext[0:TB, :] = da1_ref[...]
        dyext[TB:, :] = jnp.where(last, 0.0, da1n_ref[...].astype(F32)).astype(BF)
        ppad[0:8, :] = jnp.where(first, 0.0, f(zp_ref, 3)[HB - 8:HB, :] * f(zp_ref, 4)[HB - 8:HB, :])
        ppad[8:, :] = f(z_ref, 3) * f(z_ref, 4)
        dvpad[0:TB, :] = dq_ref[...].astype(F32) * f(z_ref, 2)
        dvpad[TB:, :] = jnp.where(last, 0.0, dqn_ref[...].astype(F32)[0:8, :] * f(zn_ref, 2)[0:8, :])

        for j in range(TB // HB):
            rows = slice(j * HB, (j + 1) * HB)
            dys = dyext[j * HB:j * HB + NB, :]
            da, db = _nn(fc_ref[...], dys), _nn(fs_ref[...], dys)
            gc = (hc[...] * da + hs[...] * db).astype(BF)
            gs = (h2[...] * db - hs[...] * da).astype(BF)
            da0 = _nn(ic_ref[...], gc) + _nn(is_ref[...], gs)
            z0, z1 = z_ref[0, rows, :].astype(F32), z_ref[1, rows, :].astype(F32)
            s1 = _sig(z1)
            dz_ref[0, rows, :] = (da0 * s1).astype(BF)
            dz_ref[1, rows, :] = (da0 * z0 * (s1 * (1.0 - s1))).astype(BF)
            xs = aext[j * HB:j * HB + NB, :]
            xa, xb = _nn(fc_ref[...], xs), _nn(fs_ref[...], xs)
            dyb = dyext[rows, :]
            pa, pb = _nn(fc_ref[:, HB:NB], dyb), _nn(fs_ref[:, HB:NB], dyb)
            rc[...] += pa * xa + pb * xb
            rs[...] += pb * xa - pa * xb
            nyq[...] += pb[0:8, :] * xb[0:8, :]

        def chunk(r, carry):
            base = pl.multiple_of(r * CHB, CHB)
            rows = pl.ds(base, CHB)
            pw = ppad[pl.ds(base, CHB + 8), :]
            p6 = _shift_up(pw, 6)[0:CHB, :]
            p7 = _shift_up(pw, 7)[0:CHB, :]
            p8 = pw[8:8 + CHB, :]
            wb0, wb1, wb2 = cw_ref[pl.ds(32, 1), :], cw_ref[pl.ds(33, 1), :], cw_ref[pl.ds(34, 1), :]
            v = wb0 * p6 + wb1 * p7 + wb2 * p8
            dz_ref[2, rows, :] = (dq_ref[rows, :].astype(F32) * v).astype(BF)
            dvw = dvpad[pl.ds(base, CHB + 8), :]
            dvc = dvw[0:CHB, :]
            dp = wb2 * dvc + wb1 * _shift_up(dvw, 1)[0:CHB, :] + wb0 * _shift_up(dvw, 2)[0:CHB, :]
            dz_ref[3, rows, :] = (dp * z_ref[4, rows, :].astype(F32)).astype(BF)
            dz_ref[4, rows, :] = (dp * z_ref[3, rows, :].astype(F32)).astype(BF)
            acc_b[0:8, :] += _fold8(dvc * p6)
            acc_b[8:16, :] += _fold8(dvc * p7)
            acc_b[16:24, :] += _fold8(dvc * p8)
            dz_ref[5, rows, :] = dzg_ref[0, rows, :]
            dz_ref[6, rows, :] = dzg_ref[1, rows, :]
            return carry

        lax.fori_loop(0, TB // CHB, chunk, 0)

        @pl.when(last)
        def _():
            row0 = lax.broadcasted_iota(jnp.int32, (NB // 2, CW), 0) == 0
            ny = jnp.broadcast_to(nyq[0:1, :], (NB // 2, CW))
            rcv = jnp.where(row0, rc[...] - ny, rc[...])
            rsv = jnp.where(row0, ny, rs[...])
            dwa_ref[...] = _dot3(rch[...], rcl[...], rcv) + _dot3(rsh[...], rsl[...], rsv)
            for k in range(KB):
                dwb_ref[k:k + 1, :] = jnp.sum(acc_b[8 * k:8 * k + 8, :], axis=0, keepdims=True)
            dwb_ref[KB:8, :] = jnp.zeros((8 - KB, CW), F32)

    blk = lambda c, t: (t, c)
    nxt = lambda c, t: (jnp.minimum((t + 1) * hb, last_h), c)
    tabs = [dft["fc"], dft["fs"], dft["ic_lo"], dft["is_lo"], *dft["kc"], *dft["ks"], *dft["k2"], *dft["rtc"], *dft["rts"]]
    return _call(
        body, name="conv_bwd", grid=(D // CW, nt), comm=comm, args=[z, z, z, da1, da1, dq, dq, dzg, cw] + tabs,
        in_specs=[pl.BlockSpec((5, TB, CW), lambda c, t: (0, t, c)),
                  pl.BlockSpec((5, HB, CW), lambda c, t: (0, jnp.maximum(t * hb - 1, 0), c)),
                  pl.BlockSpec((5, HB, CW), lambda c, t: (0, jnp.minimum((t + 1) * hb, last_h), c)),
                  pl.BlockSpec((TB, CW), blk), pl.BlockSpec((HB, CW), nxt),
                  pl.BlockSpec((TB, CW), blk), pl.BlockSpec((HB, CW), nxt),
                  pl.BlockSpec((2, TB, CW), lambda c, t: (0, t, c)),
                  pl.BlockSpec((40, CW), lambda c, t: (0, c))]
                 + [_whole(a) for a in tabs],
        out_shape=[jax.ShapeDtypeStruct((NG, T, D), BF), jax.ShapeDtypeStruct((32, D), F32),
                   jax.ShapeDtypeStruct((8, D), F32)],
        out_specs=[pl.BlockSpec((NG, TB, CW), lambda c, t: (0, t, c)),
                   pl.BlockSpec((32, CW), lambda c, t: (0, c)), pl.BlockSpec((8, CW), lambda c, t: (0, c))],
        scratch_shapes=[pltpu.VMEM((TB + HB, CW), BF), pltpu.VMEM((TB + HB, CW), BF),
                        pltpu.VMEM((TB + 8, CW), F32), pltpu.VMEM((TB + 8, CW), F32)]
                       + [pltpu.VMEM((NB // 2, CW), F32)] * 5 + [pltpu.VMEM((8, CW), F32), pltpu.VMEM((24, CW), F32)])


def _layernorm_silu(a1, lng, lnb):
    mu = jnp.mean(a1, axis=-1, keepdims=True)
    xc = a1 - mu
    rs = lax.rsqrt(jnp.mean(xc * xc, axis=-1, keepdims=True) + EPS)
    xh = xc * rs
    a2 = xh * lng + lnb
    sg = _sig(a2)
    return xh, rs, a2, sg


def _square_specs(blocks):
    return [pl.BlockSpec((D, D), lambda i, b=b: (b, 0)) for b in blocks]


def _mix_out(a1, q, z, h1, lng, lnb, wsq, comm=None):
    def body(a1_ref, q_ref, ga_ref, gb_ref, h_ref, lng_ref, lnb_ref, wa_ref, wb_ref, wo_ref, h2_ref, ya_ref, yb_ref):
        _, _, a2, sg = _layernorm_silu(a1_ref[...].astype(F32), lng_ref[...], lnb_ref[...])
        ya = _nn((a2 * sg).astype(BF), wa_ref[...])
        yb = _nn(q_ref[...], wb_ref[...])
        ya_ref[...] = ya.astype(BF)
        yb_ref[...] = yb.astype(BF)
        m = _sig(ga_ref[...].astype(F32)) * ya + _sig(gb_ref[...].astype(F32)) * yb
        h2_ref[...] = h_ref[...] + _nn(m.astype(BF), wo_ref[...])

    row = lambda i: (i, 0)
    vec = pl.BlockSpec((1, D), lambda i: (0, 0))
    return _call(
        body, name="mix_out", grid=(T // TM,), args=[a1, q, z, z, h1, lng, lnb, wsq, wsq, wsq], comm=comm,
        in_specs=[pl.BlockSpec((TM, D), row), pl.BlockSpec((TM, D), row),
                  pl.BlockSpec((None, TM, D), lambda i: (5, i, 0)), pl.BlockSpec((None, TM, D), lambda i: (6, i, 0)),
                  pl.BlockSpec((TM, D), row), vec, vec] + _square_specs((0, 1, 2)),
        out_shape=[jax.ShapeDtypeStruct((T, D), F32), jax.ShapeDtypeStruct((T, D), BF), jax.ShapeDtypeStruct((T, D), BF)],
        out_specs=[pl.BlockSpec((TM, D), row)] * 3)


def _rmsnorm_bwd(xf, g, dn):
    r = lax.rsqrt(jnp.mean(xf * xf, axis=-1, keepdims=True) + EPS)
    xr = xf * r
    gdn = dn * g
    dx = r * gdn - xr * (r * jnp.mean(gdn * xr, axis=-1, keepdims=True))
    return dx, jnp.sum(dn * xr, axis=0, keepdims=True)


def _ffn_bwd_hidden(dh, gg, uu, wbuf, off, name, comm=None):
    nf = F // FC

    def body(dh_ref, gg_ref, uu_ref, b0, dgu_ref, wd_s, sem):
        _load_ffn_weights((b0,), (off,), (wd_s,), sem)
        dhb = (0.5 * dh_ref[...]).astype(BF)
        for c in range(nf):
            sl = slice(c * FC, (c + 1) * FC)
            da = _nt(dhb, wd_s[sl, :]).astype(BF)
            gb, ub = gg_ref[:, sl], uu_ref[:, sl]
            sg = _sig(gb)
            dgu_ref[0, :, sl] = (da * ub) * (sg * (1.0 + gb * (1.0 - sg)))
            dgu_ref[0, :, F + c * FC:F + (c + 1) * FC] = da * (gb * sg)

    row = lambda i: (i, 0)
    return _call(
        body, name=name, grid=(T // TM,), args=[dh, gg, uu, wbuf], comm=comm,
        in_specs=[pl.BlockSpec((TM, D), row), pl.BlockSpec((TM, F), row), pl.BlockSpec((TM, F), row), ANY],
        out_shape=[jax.ShapeDtypeStruct((1, T, 2 * F), BF)],
        out_specs=[pl.BlockSpec((1, TM, 2 * F), lambda i: (0, i, 0))],
        scratch_shapes=[pltpu.VMEM((F, D), BF), pltpu.SemaphoreType.DMA((1,))])


def _ffn_bwd_input(dgu, dh, x, g, wbufs, offs, name, comm=None):
    def body(dgu_ref, dh_ref, x_ref, g_ref, b0, b1, dx_ref, s_ref, w_s, sem):
        _load_ffn_weights((b0, b1), offs, (w_s.at[pl.ds(0, F), :], w_s.at[pl.ds(F, F), :]), sem)

        @pl.when(pl.program_id(0) == 0)
        def _():
            s_ref[...] = jnp.zeros_like(s_ref)

        dn = _nn(dgu_ref[0], w_s[...])
        dxn, dg = _rmsnorm_bwd(x_ref[...], g_ref[...], dn)
        dx_ref[...] = dh_ref[...] + dxn
        s_ref[0:1, :] += dg

    row = lambda i: (i, 0)
    return _call(
        body, name=name, grid=(T // TM,), args=[dgu, dh, x, g, *wbufs], comm=comm,
        in_specs=[pl.BlockSpec((1, TM, 2 * F), lambda i: (0, i, 0)), pl.BlockSpec((TM, D), row),
                  pl.BlockSpec((TM, D), row), pl.BlockSpec((1, D), lambda i: (0, 0)), ANY, ANY],
        out_shape=[jax.ShapeDtypeStruct((T, D), F32), jax.ShapeDtypeStruct((8, D), F32)],
        out_specs=[pl.BlockSpec((TM, D), row), pl.BlockSpec((8, D), lambda i: (0, 0))],
        scratch_shapes=[pltpu.VMEM((2 * F, D), BF), pltpu.SemaphoreType.DMA((2,))])


def _tn_matmul(lhs, rhs, tr, name, comm=None, scale=None):
    ng, _, cdim = lhs.shape
    nc, nk = cdim // tr, T // TK
    if rhs.ndim == 2:
        r_spec = pl.BlockSpec((TK, D), lambda g, c, k: (k, 0))
    else:
        r_spec = pl.BlockSpec((None, TK, D), lambda g, c, k: (g, k, 0))

    def body(l_ref, r_ref, o_ref, acc):
        k = pl.program_id(2)

        @pl.when(k == 0)
        def _():
            acc[...] = jnp.zeros_like(acc)

        r = r_ref[...] if scale is None else scale * r_ref[...]
        acc[...] += _tn(l_ref[...], r.astype(BF))

        @pl.when(k == nk - 1)
        def _():
            o_ref[...] = acc[...].astype(BF)

    return _call(
        body, name=name, grid=(ng, nc, nk), args=[lhs, rhs], comm=comm,
        in_specs=[pl.BlockSpec((None, TK, tr), lambda g, c, k: (g, k, c)), r_spec],
        out_shape=[jax.ShapeDtypeStruct((ng * cdim, D), BF)],
        out_specs=[pl.BlockSpec((tr, D), lambda g, c, k: (g * nc + c, 0))],
        scratch_shapes=[pltpu.VMEM((tr, D), F32)])


def _mix_out_bwd(dh2, ya, yb, z, a1, q, lng, lnb, wsq, comm=None):
    def body(dh_ref, ya_ref, yb_ref, ga_ref, gb_ref, a1_ref, q_ref, lng_ref, lnb_ref, wa_ref, wb_ref, wo_ref,
             dzg_ref, da1_ref, dq_ref, l_ref, r_ref, s_ref):
        @pl.when(pl.program_id(0) == 0)
        def _():
            s_ref[...] = jnp.zeros_like(s_ref)

        dhb = dh_ref[...].astype(BF)
        dm = _nt(dhb, wo_ref[...])
        ya, yb = ya_ref[...].astype(F32), yb_ref[...].astype(F32)
        sa, sb = _sig(ga_ref[...].astype(F32)), _sig(gb_ref[...].astype(F32))
        l_ref[0] = (sa * ya + sb * yb).astype(BF)
        l_ref[2] = q_ref[...]
        dzg_ref[0] = (dm * ya * (sa * (1.0 - sa))).astype(BF)
        dzg_ref[1] = (dm * yb * (sb * (1.0 - sb))).astype(BF)
        dya = (dm * sa).astype(BF)
        dyb = (dm * sb).astype(BF)
        r_ref[0] = dhb
        r_ref[1] = dya
        r_ref[2] = dyb
        dq_ref[...] = _nt(dyb, wb_ref[...]).astype(BF)
        da3 = _nt(dya, wa_ref[...])
        lng = lng_ref[...]
        xh, rs, a2, sg = _layernorm_silu(a1_ref[...].astype(F32), lng, lnb_ref[...])
        l_ref[1] = (a2 * sg).astype(BF)
        da2 = da3 * (sg * (1.0 + a2 * (1.0 - sg)))
        s_ref[0:1, :] += jnp.sum(da2 * xh, axis=0, keepdims=True)
        s_ref[1:2, :] += jnp.sum(da2, axis=0, keepdims=True)
        dxh = da2 * lng
        da1 = rs * (dxh - jnp.mean(dxh, axis=-1, keepdims=True) - xh * jnp.mean(dxh * xh, axis=-1, keepdims=True))
        da1_ref[...] = da1.astype(BF)
        s_ref[2:3, :] += jnp.sum(da1, axis=0, keepdims=True)

    row = lambda i: (i, 0)
    row3 = lambda i: (0, i, 0)
    vec = pl.BlockSpec((1, D), lambda i: (0, 0))
    return _call(
        body, name="mix_out_bwd", grid=(T // TM,), args=[dh2, ya, yb, z, z, a1, q, lng, lnb, wsq, wsq, wsq], comm=comm,
        in_specs=[pl.BlockSpec((TM, D), row), pl.BlockSpec((TM, D), row), pl.BlockSpec((TM, D), row),
                  pl.BlockSpec((None, TM, D), lambda i: (5, i, 0)), pl.BlockSpec((None, TM, D), lambda i: (6, i, 0)),
                  pl.BlockSpec((TM, D), row), pl.BlockSpec((TM, D), row), vec, vec] + _square_specs((0, 1, 2)),
        out_shape=[jax.ShapeDtypeStruct((2, T, D), BF), jax.ShapeDtypeStruct((T, D), BF),
                   jax.ShapeDtypeStruct((T, D), BF), jax.ShapeDtypeStruct((3, T, D), BF),
                   jax.ShapeDtypeStruct((3, T, D), BF), jax.ShapeDtypeStruct((8, D), F32)],
        out_specs=[pl.BlockSpec((2, TM, D), row3), pl.BlockSpec((TM, D), row), pl.BlockSpec((TM, D), row),
                   pl.BlockSpec((3, TM, D), row3), pl.BlockSpec((3, TM, D), row3), pl.BlockSpec((8, D), lambda i: (0, 0))])


def _mix_in_bwd(dz, dh2, h1, gm, win, comm=None):
    def body(dz_ref, w_any, dh_ref, h_ref, g_ref, o_ref, s_ref, w_s, sem):
        _load_ffn_weights((w_any,), (0,), (w_s,), sem)

        @pl.when(pl.program_id(0) == 0)
        def _():
            s_ref[...] = jnp.zeros_like(s_ref)

        du = _nn(dz_ref[0], w_s[0:D, :])
        for j in range(1, NG):
            du = du + _nn(dz_ref[j], w_s[j * D:(j + 1) * D, :])
        dx, dg = _rmsnorm_bwd(h_ref[...], g_ref[...], du)
        o_ref[...] = dh_ref[...] + dx
        s_ref[0:1, :] += dg

    row = lambda i: (i, 0)
    return _call(
        body, name="mix_in_bwd", grid=(T // TM,), args=[dz, win, dh2, h1, gm], comm=comm,
        in_specs=[pl.BlockSpec((NG, TM, D), lambda i: (0, i, 0)), ANY,
                  pl.BlockSpec((TM, D), row), pl.BlockSpec((TM, D), row), pl.BlockSpec((1, D), lambda i: (0, 0))],
        out_shape=[jax.ShapeDtypeStruct((T, D), F32), jax.ShapeDtypeStruct((8, D), F32)],
        out_specs=[pl.BlockSpec((TM, D), row), pl.BlockSpec((8, D), lambda i: (0, 0))],
        scratch_shapes=[pltpu.VMEM((NG * D, D), BF), pltpu.SemaphoreType.DMA((1,))])


def _row_tile(n, want, mult):
    for t in range(min(want, n), 0, -1):
        if n % t == 0 and t % mult == 0:
            return t
    return n


def _sum_slots(recv, name):
    ns, rows, cols = recv.shape
    tr = _row_tile(rows, 1024, 16)

    def body(r_ref, o_ref):
        s = r_ref[0].astype(F32)
        for k in range(1, ns):
            s = s + r_ref[k].astype(F32)
        o_ref[...] = s

    return _call(
        body, name=name, grid=(rows // tr,), args=[recv],
        in_specs=[pl.BlockSpec((ns, tr, cols), lambda i: (0, i, 0))],
        out_shape=[jax.ShapeDtypeStruct((rows, cols), F32)],
        out_specs=[pl.BlockSpec((tr, cols), lambda i: (i, 0))])[0]


def _pack_small(s_ffn1, s_in, s_mix, s_ffn2, s_final, dwa, dwb):
    def body(f1, mi, mo, f2, fl, wa_ref, wb_ref, v_ref, k_ref):
        for dst, (ref, row) in enumerate(((f1, 0), (mi, 0), (mo, 0), (mo, 1), (mo, 2), (f2, 0), (fl, 0), (fl, 1))):
            v_ref[dst:dst + 1, :] = ref[row:row + 1, :]
        for k in range(NDEV):
            k_ref[k, 0:32, :] = wa_ref[:, k * LANE:(k + 1) * LANE]
            k_ref[k, 32:40, :] = wb_ref[:, k * LANE:(k + 1) * LANE]

    return pl.pallas_call(
        body, name="pack_small",
        out_shape=(jax.ShapeDtypeStruct((8, D), F32), jax.ShapeDtypeStruct((NDEV, 40, LANE), F32)),
    )(s_ffn1, s_in, s_mix, s_ffn2, s_final, dwa, dwb)


def _sum_small(vecs, convs):
    def body(v_ref, k_ref, vs_ref, ks_ref, l_ref):
        s, c = v_ref[0], k_ref[0]
        for k in range(1, NDEV):
            s = s + v_ref[k]
            c = c + k_ref[k]
        vs_ref[...] = s
        ks_ref[...] = c
        l_ref[...] = jnp.broadcast_to(jnp.sum(s[7:8, :], axis=-1, keepdims=True), (8, LANE))

    return pl.pallas_call(
        body, name="sum_small",
        out_shape=(jax.ShapeDtypeStruct((8, D), F32), jax.ShapeDtypeStruct((40, LANE), F32),
                   jax.ShapeDtypeStruct((8, LANE), F32)),
    )(vecs, convs)


def _adam(gs, ws, ms, vs, name, comm=None):
    n = len(gs)
    rows, cols = ws[0].shape
    tr = _row_tile(rows, 256, 16)
    c1 = 1.0 - ADAM_B1 ** ADAM_STEP
    c2 = 1.0 - ADAM_B2 ** ADAM_STEP
    summed = [isinstance(g, tuple) for g in gs]

    def body(*refs):
        for i in range(n):
            g_in, w, m, v = refs[4 * i], refs[4 * i + 1][...], refs[4 * i + 2][...], refs[4 * i + 3][...]
            g_ref, d_ref, m_ref, v_ref = refs[4 * n + 4 * i: 4 * n + 4 * i + 4]
            if summed[i]:
                g = g_in[0].astype(F32)
                for k in range(1, g_in.shape[0]):
                    g = g + g_in[k].astype(F32)
            else:
                g = g_in[...]
            g_ref[...] = g
            m2 = ADAM_B1 * m + (1.0 - ADAM_B1) * g
            v2 = ADAM_B2 * v + (1.0 - ADAM_B2) * (g * g)
            d_ref[...] = -ADAM_LR * ((m2 / c1) / (jnp.sqrt(v2 / c2) + ADAM_EPS) + ADAM_WD * w)
            m_ref[...] = m2
            v_ref[...] = v2

    spec = pl.BlockSpec((tr, cols), lambda i: (i, 0))
    args, in_specs = [], []
    for i in range(n):
        if summed[i]:
            slots, first = gs[i]
            args.append(slots)
            in_specs.append(pl.BlockSpec((slots.shape[0], tr, cols), lambda i, b=first // tr: (0, b + i, 0)))
        else:
            args.append(gs[i])
            in_specs.append(spec)
        args += [ws[i], ms[i], vs[i]]
        in_specs += [spec] * 3
    outs = _call(body, name=name, grid=(rows // tr,), args=args, comm=comm, in_specs=in_specs,
                 out_shape=[jax.ShapeDtypeStruct((rows, cols), F32)] * (4 * n), out_specs=[spec] * (4 * n))
    return [tuple(outs[4 * i: 4 * i + 4]) for i in range(n)], outs[4 * n:]


def kernel(x, ffn1_norm, ffn1_w_gate, ffn1_w_up, ffn1_w_down, mix_norm, w_in, a_dw_w, a_dw_b, a_ln_g, a_ln_b, a_w_out, b_conv_w, b_w_out, w_o, ffn2_norm, ffn2_w_gate, ffn2_w_up, ffn2_w_down, final_norm, loss_target, m_ffn1_norm, m_ffn1_w_gate, m_ffn1_w_up, m_ffn1_w_down, m_mix_norm, m_w_in, m_a_dw_w, m_a_dw_b, m_a_ln_g, m_a_ln_b, m_a_w_out, m_b_conv_w, m_b_w_out, m_w_o, m_ffn2_norm, m_ffn2_w_gate, m_ffn2_w_up, m_ffn2_w_down, m_final_norm, v_ffn1_norm, v_ffn1_w_gate, v_ffn1_w_up, v_ffn1_w_down, v_mix_norm, v_w_in, v_a_dw_w, v_a_dw_b, v_a_ln_g, v_a_ln_b, v_a_w_out, v_b_conv_w, v_b_w_out, v_w_o, v_ffn2_norm, v_ffn2_w_gate, v_ffn2_w_up, v_ffn2_w_down, v_final_norm):
    names = ("ffn1_norm", "ffn1_w_gate", "ffn1_w_up", "ffn1_w_down", "mix_norm", "w_in", "a_dw_w", "a_dw_b",
             "a_ln_g", "a_ln_b", "a_w_out", "b_conv_w", "b_w_out", "w_o", "ffn2_norm", "ffn2_w_gate", "ffn2_w_up",
             "ffn2_w_down", "final_norm")
    w = dict(ffn1_norm=ffn1_norm, ffn1_w_gate=ffn1_w_gate, ffn1_w_up=ffn1_w_up, ffn1_w_down=ffn1_w_down,
             mix_norm=mix_norm, w_in=w_in, a_dw_w=a_dw_w, a_dw_b=a_dw_b, a_ln_g=a_ln_g, a_ln_b=a_ln_b,
             a_w_out=a_w_out, b_conv_w=b_conv_w, b_w_out=b_w_out, w_o=w_o, ffn2_norm=ffn2_norm,
             ffn2_w_gate=ffn2_w_gate, ffn2_w_up=ffn2_w_up, ffn2_w_down=ffn2_w_down, final_norm=final_norm)
    m = dict(ffn1_norm=m_ffn1_norm, ffn1_w_gate=m_ffn1_w_gate, ffn1_w_up=m_ffn1_w_up, ffn1_w_down=m_ffn1_w_down,
             mix_norm=m_mix_norm, w_in=m_w_in, a_dw_w=m_a_dw_w, a_dw_b=m_a_dw_b, a_ln_g=m_a_ln_g, a_ln_b=m_a_ln_b,
             a_w_out=m_a_w_out, b_conv_w=m_b_conv_w, b_w_out=m_b_w_out, w_o=m_w_o, ffn2_norm=m_ffn2_norm,
             ffn2_w_gate=m_ffn2_w_gate, ffn2_w_up=m_ffn2_w_up, ffn2_w_down=m_ffn2_w_down, final_norm=m_final_norm)
    v = dict(ffn1_norm=v_ffn1_norm, ffn1_w_gate=v_ffn1_w_gate, ffn1_w_up=v_ffn1_w_up, ffn1_w_down=v_ffn1_w_down,
             mix_norm=v_mix_norm, w_in=v_w_in, a_dw_w=v_a_dw_w, a_dw_b=v_a_dw_b, a_ln_g=v_a_ln_g, a_ln_b=v_a_ln_b,
             a_w_out=v_a_w_out, b_conv_w=v_b_conv_w, b_w_out=v_b_w_out, w_o=v_w_o, ffn2_norm=v_ffn2_norm,
             ffn2_w_gate=v_ffn2_w_gate, ffn2_w_up=v_ffn2_w_up, ffn2_w_down=v_ffn2_w_down, final_norm=v_final_norm)
    flat = _pack_weights(dict(wg1=ffn1_w_gate[0].T, wu1=ffn1_w_up[0].T, wd1=ffn1_w_down[0], wg2=ffn2_w_gate[0].T,
                              wu2=ffn2_w_up[0].T, wd2=ffn2_w_down[0], win=w_in[0], wa=a_w_out[0], wb=b_w_out[0],
                              wo=w_o[0]))
    cw_shard = jnp.concatenate([a_dw_w[0], jnp.zeros((1, LANE), F32), b_conv_w[0], jnp.zeros((5, LANE), F32)], axis=0)

    x2, tgt = x[0], loss_target[0]
    st_a, st_b, st_c, st_d, st_e = ("wg1", "wu1", "wd1"), ("win",), ("wa", "wb", "wo", "wg2"), ("wu2",), ("wd2",)

    buf_a, cw = _run_comm(_join(_ag_comm(st_a, flat), _direct_comm(cw_shard, False)), "ag_ffn1")
    h1, n1, gg1, uu1, act1, buf_b = _ffn_fwd(x2, ffn1_norm, (buf_a,) * 3, (0, F, 2 * F), "ffn1_fwd", _ag_comm(st_b, flat))
    u, z, buf_c = _mix_in(h1, mix_norm, buf_b, _ag_comm(st_c, flat))
    dft = _dft_constants()
    cw = jnp.transpose(cw, (1, 0, 2)).reshape(40, D)
    a1, q, buf_d = _conv_fwd_dft(z, cw, a_dw_b, dft, _ag_comm(st_d, flat))
    h2, ya, yb, buf_e = _mix_out(a1, q, z, h1, a_ln_g, a_ln_b, buf_c, _ag_comm(st_e, flat))
    ffn2_bufs, ffn2_offs = (buf_c, buf_d, buf_e), (3 * D, 0, 0)
    dh3, s_final, n2, gg2, uu2, act2 = _ffn_fwd(h2, ffn2_norm, ffn2_bufs, ffn2_offs, "ffn2_fwd",
                                          final=(final_norm.reshape(1, D), tgt))

    tr_f = F // 2 if (F // 2) % LANE == 0 else F
    def pair(stage, src):
        return _rs_pair_comm(stage, src)

    def chip(stage, src, pair_buf, tag):
        return _rs_chip_comm(_pair_add(stage, src, pair_buf, "pair_add_" + tag))

    (dgu2,) = _ffn_bwd_hidden(dh3, gg2, uu2, buf_e, 0, "ffn2_bwd_h")
    (gu2,) = _tn_matmul(dgu2, n2, tr_f, "dw_gu2")
    s2a, src2a = ("wg2", "wu2"), dict(wg2=(gu2, 0), wu2=(gu2, F))
    gd2, pair2a = _tn_matmul(act2, dh3, tr_f, "dw_d2", pair(s2a, src2a), scale=0.5)
    s2b, src2b = ("wd2",), dict(wd2=(gd2, 0))
    dh2, s_ffn2, pair2b = _ffn_bwd_input(dgu2, dh3, h2, ffn2_norm, (buf_c, buf_d), (3 * D, 0), "ffn2_bwd_x",
                                         pair(s2b, src2b))
    dzg, da1, dq, lsq, rsq, s_mix, recv2b = _mix_out_bwd(dh2, ya, yb, z, a1, q, a_ln_g, a_ln_b, buf_c,
                                                          chip(s2b, src2b, pair2b, "2b"))
    (gsq,) = _tn_matmul(lsq, rsq, D, "dw_square")
    ssq, srcsq = ("wa", "wb", "wo"), dict(wa=(gsq, D), wb=(gsq, 2 * D), wo=(gsq, 0))
    dz, dwa, dwb, recv2a, pairsq = _conv_bwd_dft(z, da1, dq, dzg, cw, dft,
                                                 _join(chip(s2a, src2a, pair2a, "2a"), pair(ssq, srcsq)))
    gin, recvsq = _tn_matmul(dz, u, D, "dw_in", chip(ssq, srcsq, pairsq, "sq"))
    sin_a, sin_b, srcin = ("win/0/2",), ("win/1/2",), {"win/0/2": (gin, 0), "win/1/2": (gin, 0)}
    dh1, s_in, pairin_a, pairin_b = _mix_in_bwd(dz, dh2, h1, mix_norm, buf_b,
                                                _join(pair(sin_a, srcin), pair(sin_b, srcin)))
    dgu1, recvin_a = _ffn_bwd_hidden(dh1, gg1, uu1, buf_a, 2 * F, "ffn1_bwd_h",
                                           chip(sin_a, srcin, pairin_a, "in_a"))
    gu1, recvin_b = _tn_matmul(dgu1, n1, tr_f, "dw_gu1", chip(sin_b, srcin, pairin_b, "in_b"))
    s1a, src1a = ("wg1", "wu1"), dict(wg1=(gu1, 0), wu1=(gu1, F))
    gd1, pair1a = _tn_matmul(act1, dh1, tr_f, "dw_d1", pair(s1a, src1a), scale=0.5)
    s1b, src1b = ("wd1",), dict(wd1=(gd1, 0))
    dx, s_ffn1, recv1a, pair1b = _ffn_bwd_input(dgu1, dh1, x2, ffn1_norm, (buf_a, buf_a), (0, F), "ffn1_bwd_x",
                                                _join(chip(s1a, src1a, pair1a, "1a"), pair(s1b, src1b)))
    win_sum = jnp.concatenate([_sum_slots(recvin_a, "sum_in_a"), _sum_slots(recvin_b, "sum_in_b")], axis=0)

    vec8, convk = _pack_small(s_ffn1, s_in, s_mix, s_ffn2, s_final, dwa, dwb)
    vec_all, conv_all = _run_comm(_join(_direct_comm(vec8, False), _direct_comm(convk, True)), "xchg_small")
    vec_sum, conv_sum, loss_blk = _sum_small(vec_all, conv_all)
    loss = loss_blk[0, 0]

    fs = F // NDEV
    g = dict(ffn1_w_gate=(recv1a, 0), ffn1_w_up=(recv1a, fs), ffn2_w_gate=(recv2a, 0), ffn2_w_up=(recv2a, fs),
             ffn2_w_down=(recv2b, 0), a_w_out=(recvsq, 0), b_w_out=(recvsq, D // NDEV), w_o=(recvsq, 2 * (D // NDEV)),
             w_in=win_sum.T, ffn1_norm=vec_sum[0:1], mix_norm=vec_sum[1:2], a_ln_g=vec_sum[2:3], a_ln_b=vec_sum[3:4],
             a_dw_b=vec_sum[4:5], ffn2_norm=vec_sum[5:6], final_norm=vec_sum[6:7],
             a_dw_w=conv_sum[0:KA], b_conv_w=conv_sum[32:32 + KB])
    grad, upd = {}, {}

    def run(group, name, as2d=lambda a: a[0], back=lambda a, n: a.reshape(w[n].shape), comm=None):
        res, extra = _adam([g[n] for n in group], [as2d(w[n]) for n in group], [as2d(m[n]) for n in group],
                           [as2d(v[n]) for n in group], name, comm)
        for n, r in zip(group, res):
            grad[n], upd[n] = back(r[0], n), tuple(back(a, n) for a in r[1:])
        return extra

    rows_first = dict(as2d=lambda a: a[0].T, back=lambda a, n: a.T[None])
    (recv1b,) = run(("ffn1_w_gate", "ffn1_w_up", "ffn2_w_gate", "ffn2_w_up"), "adam_gate_up",
                    comm=chip(s1b, src1b, pair1b, "1b"), **rows_first)
    g["ffn1_w_down"] = (recv1b, 0)
    run(("ffn1_w_down", "ffn2_w_down"), "adam_down")
    run(("w_in",), "adam_in")
    run(("a_w_out", "b_w_out", "w_o"), "adam_square")
    run(("a_dw_w",), "adam_dw")
    run(("b_conv_w",), "adam_conv")
    run(("ffn1_norm", "mix_norm", "a_dw_b", "a_ln_g", "a_ln_b", "ffn2_norm", "final_norm"), "adam_vec",
        as2d=lambda a: a.reshape(1, D))

    return (loss, dx.reshape(x.shape), *[grad[n] for n in names], *[upd[n][0] for n in names],
            *[upd[n][1] for n in names], *[upd[n][2] for n in names])
```

```python
import jax
import jax.numpy as jnp
from jax import lax
from jax.experimental import pallas as pl
from jax.experimental.pallas import tpu as pltpu

T = 4096
D = 1024
F = 2816
NG = 7
NDEV = 8
NCHIP = 4
KA, KB = 31, 3
EPS = 1e-6
ADAM_LR, ADAM_B1, ADAM_B2, ADAM_EPS, ADAM_WD, ADAM_STEP = 0.001, 0.9, 0.999, 1e-08, 0.01, 10

TM = 512
FC = 256
TB = 1024
NB = 256
HB = NB // 2
CW = 256
CHB = 64
LANE = 128
TK = 2048
VMEM_LIMIT = 56 * 1024 * 1024

BF = jnp.bfloat16
F32 = jnp.float32
MESH = pl.DeviceIdType.MESH
ANY = pl.BlockSpec(memory_space=pl.ANY)

ORDER = ("wg1", "wu1", "wd1", "wg2", "wu2", "wd2", "win", "wa", "wb", "wo")


class _Layout:
    def __init__(self):
        fs, dis, ds = F // NDEV, NG * D // NDEV, D // NDEV
        self.rows = dict(wg1=fs, wu1=fs, wd1=fs, wg2=fs, wu2=fs, wd2=fs, win=dis, wa=ds, wb=ds, wo=ds)
        self.fl, off = {}, 0
        for n in ORDER:
            self.fl[n] = off
            off += self.rows[n]
        self.RT = off


class _Stage:
    def __init__(self, names):
        lay = _Layout()
        self.names = names
        self.rows, self.full, self.sub, self.fl = {}, {}, {}, {}
        for n in names:
            base, i, k = (n.split("/") + ["0", "1"])[:3]
            self.full[n] = lay.rows[base]
            self.rows[n] = lay.rows[base] // int(k)
            self.sub[n] = int(i) * self.rows[n]
            self.fl[n] = lay.fl[base] + self.sub[n]
        self.off, self.wc, o, w = {}, {}, 0, 0
        for n in names:
            self.off[n], self.wc[n] = o, w
            o += self.rows[n]
            w += NDEV * self.rows[n]
        self.R, self.W = o, w

    def grad_row(self, n, first, dev_lin):
        return first + dev_lin * self.full[n] + self.sub[n]


def _nt(a, b):
    return lax.dot_general(a, b, (((1,), (1,)), ((), ())), preferred_element_type=F32)


def _nn(a, b):
    return lax.dot_general(a, b, (((1,), (0,)), ((), ())), preferred_element_type=F32)


def _tn(a, b):
    return lax.dot_general(a, b, (((0,), (0,)), ((), ())), preferred_element_type=F32)


def _sig(x):
    return 1.0 / (1.0 + jnp.exp(-x))


def _position():
    return lax.axis_index("x"), lax.axis_index("y"), lax.axis_index("c")


def _peer(pos, j):
    x, y, c = pos
    return (1 - x if j & 4 else x, 1 - y if j & 2 else y, 1 - c if j & 1 else c)


def _lin(pos):
    return 4 * pos[0] + 2 * pos[1] + pos[2]


def _chip(pos):
    return 2 * pos[0] + pos[1]


class _Comm:
    def __init__(self, inputs, out_shapes, scratch, start, finish, middle=None):
        self.inputs, self.out_shapes, self.scratch = inputs, out_shapes, scratch
        self.start, self.finish, self.middle = start, finish, middle


def _call(body, *, name, grid, args, in_specs, out_shape, out_specs, scratch_shapes=(), comm=None,
          num_scalar_prefetch=0):
    in_specs, out_shape, out_specs, scratch_shapes = list(in_specs), list(out_shape), list(out_specs), list(scratch_shapes)
    n_in, n_out, n_scr = len(in_specs), len(out_shape), len(scratch_shapes)
    sp = num_scalar_prefetch
    if comm is None:
        kernel_fn = lambda *refs: body(*refs)
        c_in = c_out = c_scr = 0
    else:
        c_in, c_out, c_scr = len(comm.inputs), len(comm.out_shapes), len(comm.scratch)

        def kernel_fn(*refs):
            pre, refs = refs[:sp], refs[sp:]
            ins, cins = refs[:n_in], refs[n_in:n_in + c_in]
            o0 = n_in + c_in
            outs, couts = refs[o0:o0 + n_out], refs[o0 + n_out:o0 + n_out + c_out]
            s0 = o0 + n_out + c_out
            scr, cscr = refs[s0:s0 + n_scr], refs[s0 + n_scr:]
            step, steps = pl.program_id(0), grid[0]
            for a in range(1, len(grid)):
                step, steps = step * grid[a] + pl.program_id(a), steps * grid[a]
            first, last = step == 0, step == steps - 1

            @pl.when(first)
            def _():
                comm.start(cins, couts, cscr)

            if comm.middle is not None:
                @pl.when(step == (steps // 2 if steps > 2 else steps - 1))
                def _():
                    comm.middle(cins, couts, cscr)

            body(*pre, *ins, *outs, *scr)

            @pl.when(last)
            def _():
                comm.finish(cins, couts, cscr)

        args = list(args) + list(comm.inputs)
        in_specs += [ANY] * c_in
        out_shape += list(comm.out_shapes)
        out_specs += [ANY] * c_out
        scratch_shapes += list(comm.scratch)
    params = pltpu.CompilerParams(dimension_semantics=("arbitrary",) * len(grid), vmem_limit_bytes=VMEM_LIMIT)
    if sp:
        grid_spec = pltpu.PrefetchScalarGridSpec(num_scalar_prefetch=sp, grid=grid, in_specs=in_specs,
                                                 out_specs=out_specs, scratch_shapes=scratch_shapes)
        return pl.pallas_call(kernel_fn, name=name, grid_spec=grid_spec, out_shape=out_shape,
                              compiler_params=params)(*args)
    return pl.pallas_call(kernel_fn, name=name, grid=grid, in_specs=in_specs, out_shape=out_shape, out_specs=out_specs,
                          scratch_shapes=scratch_shapes, compiler_params=params)(*args)


def _join(a, b):
    na = (len(a.inputs), len(a.out_shapes), len(a.scratch))

    def split(refs):
        return ([r[:n] for r, n in zip(refs, na)], [r[n:] for r, n in zip(refs, na)])

    def start(*refs):
        ra, rb = split(refs)
        a.start(*ra)
        b.start(*rb)

    def finish(*refs):
        ra, rb = split(refs)
        a.finish(*ra)
        b.finish(*rb)

    def middle(*refs):
        for stage, r in zip((a, b), split(refs)):
            if stage.middle is not None:
                stage.middle(*r)

    return _Comm(list(a.inputs) + list(b.inputs), list(a.out_shapes) + list(b.out_shapes),
                 list(a.scratch) + list(b.scratch), start, finish,
                 middle if (a.middle is not None or b.middle is not None) else None)


def _run_comm(comm, name):
    def body(*refs):
        c_in, c_out = len(comm.inputs), len(comm.out_shapes)
        parts = (refs[:c_in], refs[c_in:c_in + c_out], refs[c_in + c_out:])
        comm.start(*parts)
        if comm.middle is not None:
            comm.middle(*parts)
        comm.finish(*parts)

    return pl.pallas_call(
        body, name=name, out_shape=list(comm.out_shapes), in_specs=[ANY] * len(comm.inputs),
        out_specs=[ANY] * len(comm.out_shapes), scratch_shapes=list(comm.scratch))(*comm.inputs)


def _ag_comm(names, flat):
    st = _Stage(names)

    def ring(me):
        x, y, c = me
        diagonal = x == y
        up = (jnp.where(diagonal, x, 1 - x), jnp.where(diagonal, 1 - y, y), c)
        down = (jnp.where(diagonal, 1 - x, x), jnp.where(diagonal, y, 1 - y), c)
        low = c == 0
        passed = tuple(jnp.where(low, d, u) for d, u in zip(down, up))
        target = tuple(jnp.where(low, u, d) for d, u in zip(down, up))
        return up, down, (1 - x, 1 - y, c), passed, target

    def parts(refs):
        (flat_ref,), (out_ref,), (send_sems, recv_sems, local_sem) = refs
        me = _position()

        def region(name, dev):
            r = st.rows[name]
            return out_ref.at[pl.ds(st.wc[name] + _lin(dev) * r, r), :]

        def own(name):
            return flat_ref.at[pl.ds(st.fl[name], st.rows[name]), :]

        def copies(k, dev, to, from_flat):
            return [pltpu.make_async_remote_copy(
                src_ref=own(n) if from_flat else region(n, dev), dst_ref=region(n, dev), send_sem=send_sems.at[k],
                recv_sem=recv_sems.at[k], device_id=to, device_id_type=MESH) for n in names]

        def whole(k):
            return pltpu.make_async_remote_copy(
                src_ref=flat_ref.at[pl.ds(0, st.R), :], dst_ref=out_ref.at[pl.ds(0, st.R), :],
                send_sem=send_sems.at[k], recv_sem=recv_sems.at[k], device_id=me, device_id_type=MESH)

        return me, region, own, copies, whole, flat_ref, out_ref, local_sem

    def start(*refs):
        me, region, own, copies, _, _, _, local_sem = parts(refs)
        for n in names:
            pltpu.make_async_copy(own(n), region(n, me), local_sem).start()
        up, down, _, _, _ = ring(me)
        for k, to in ((1, up), (2, down), (0, _peer(me, 1))):
            for cp in copies(k, me, to, True):
                cp.start()

    def middle(*refs):
        me, _, _, copies, whole, _, _, _ = parts(refs)
        up, down, _, passed, target = ring(me)
        sib = _peer(me, 1)
        whole(1).wait_recv()
        whole(2).wait_recv()
        for k, dev, to in ((3, passed, target), (4, down, sib), (5, up, sib)):
            for cp in copies(k, dev, to, False):
                cp.start()

    def finish(*refs):
        me, _, _, copies, whole, flat_ref, out_ref, local_sem = parts(refs)
        _, _, across, _, _ = ring(me)
        whole(3).wait_recv()
        for cp in copies(6, across, _peer(me, 1), False):
            cp.start()
        whole(0).wait_recv()
        for j in range(3):
            whole(4 + j).wait_recv()
        for k in range(7):
            whole(k).wait_send()
        pltpu.make_async_copy(flat_ref.at[pl.ds(0, st.R), :], out_ref.at[pl.ds(0, st.R), :], local_sem).wait()

    return _Comm([flat], [jax.ShapeDtypeStruct((st.W, D), BF)],
                 [pltpu.SemaphoreType.DMA((7,)), pltpu.SemaphoreType.DMA((7,)), pltpu.SemaphoreType.DMA],
                 start, finish, middle)


def _rs_pair_comm(names, src):
    st = _Stage(names)
    arrays = []
    for n in names:
        if not any(src[n][0] is a for a in arrays):
            arrays.append(src[n][0])
    idx = {n: [i for i, a in enumerate(arrays) if a is src[n][0]][0] for n in names}

    def slot_wait(refs):
        recv = refs[1][0]
        send_sem, recv_sem = refs[2]
        return pltpu.make_async_remote_copy(src_ref=recv, dst_ref=recv, send_sem=send_sem, recv_sem=recv_sem,
                                            device_id=_position(), device_id_type=MESH)

    def start(*refs):
        ins, (recv,), (send_sem, recv_sem) = refs
        me = _position()
        sib = _peer(me, 1)
        for q in range(NCHIP):
            dev = (q // 2, q % 2, sib[2])
            for n in names:
                r = st.rows[n]
                pltpu.make_async_remote_copy(
                    src_ref=ins[idx[n]].at[pl.ds(st.grad_row(n, src[n][1], _lin(dev)), r), :],
                    dst_ref=recv.at[q, pl.ds(st.off[n], r), :], send_sem=send_sem, recv_sem=recv_sem,
                    device_id=sib, device_id_type=MESH).start()

    def finish(*refs):
        w = slot_wait(refs)
        w.wait_recv()
        w.wait_send()

    return _Comm(arrays, [jax.ShapeDtypeStruct((NCHIP, st.R, D), BF)],
                 [pltpu.SemaphoreType.DMA, pltpu.SemaphoreType.DMA], start, finish)


def _pair_add(names, src, recv, name):
    st = _Stage(names)
    c_arr = jnp.reshape(lax.axis_index("c"), (1,)).astype(jnp.int32)

    def body(c_ref, *refs):
        r_ref, o_ref = refs[len(names)], refs[len(names) + 1]
        for a_ref, n in zip(refs, names):
            rows = slice(st.off[n], st.off[n] + st.rows[n])
            o_ref[rows, :] = (a_ref[...].astype(F32) + r_ref[rows, :].astype(F32)).astype(BF)

    def shard_spec(n):
        r = st.rows[n]
        base, step = st.grad_row(n, src[n][1], 0) // r, st.full[n] // r
        return pl.BlockSpec((r, D), lambda q, c_ref: (base + step * (2 * q + c_ref[0]), 0))

    slot = pl.BlockSpec((None, st.R, D), lambda q, c_ref: (q, 0, 0))
    return _call(body, name=name, grid=(NCHIP,), args=[c_arr] + [src[n][0] for n in names] + [recv],
                 in_specs=[shard_spec(n) for n in names] + [slot],
                 out_shape=[jax.ShapeDtypeStruct((NCHIP, st.R, D), BF)], out_specs=[slot], num_scalar_prefetch=1)[0]


def _rs_chip_comm(part):
    def copies(refs):
        (p_ref,), (recv,), (send_sems, recv_sems, local_sem) = refs
        me = _position()
        mine = pltpu.make_async_copy(p_ref.at[_chip(me)], recv.at[_chip(me)], local_sem)
        out = []
        for j, bits in enumerate((4, 2, 6)):
            to = _peer(me, bits)
            out.append(pltpu.make_async_remote_copy(
                src_ref=p_ref.at[_chip(to)], dst_ref=recv.at[_chip(me)], send_sem=send_sems.at[j],
                recv_sem=recv_sems.at[j], device_id=to, device_id_type=MESH))
        return mine, out

    def start(*refs):
        mine, out = copies(refs)
        mine.start()
        for cp in out:
            cp.start()

    def finish(*refs):
        mine, out = copies(refs)
        for cp in out:
            cp.wait_recv()
        for cp in out:
            cp.wait_send()
        mine.wait()

    return _Comm([part], [jax.ShapeDtypeStruct(part.shape, BF)],
                 [pltpu.SemaphoreType.DMA((3,)), pltpu.SemaphoreType.DMA((3,)), pltpu.SemaphoreType.DMA],
                 start, finish)


def _direct_comm(x, scatter):
    def copies(refs):
        (x_ref,), (out_ref,), (send_sems, recv_sems, local_sem) = refs
        me = _position()

        def piece(dev):
            return x_ref.at[_lin(dev)] if scatter else x_ref

        mine = pltpu.make_async_copy(piece(me), out_ref.at[_lin(me)], local_sem)
        return mine, [pltpu.make_async_remote_copy(
            src_ref=piece(_peer(me, j)), dst_ref=out_ref.at[_lin(me)], send_sem=send_sems.at[j - 1],
            recv_sem=recv_sems.at[j - 1], device_id=_peer(me, j), device_id_type=MESH) for j in range(1, NDEV)]

    def start(*refs):
        mine, cps = copies(refs)
        mine.start()
        for cp in cps:
            cp.start()

    def finish(*refs):
        mine, cps = copies(refs)
        for cp in cps:
            cp.wait_recv()
        for cp in cps:
            cp.wait_send()
        mine.wait()

    shape = x.shape if scatter else (NDEV,) + x.shape
    return _Comm([x], [jax.ShapeDtypeStruct(shape, x.dtype)],
                 [pltpu.SemaphoreType.DMA((7,)), pltpu.SemaphoreType.DMA((7,)), pltpu.SemaphoreType.DMA],
                 start, finish)


def _pack_weights(shards):
    lay = _Layout()

    def body(*refs):
        o_ref = refs[-1]
        for ref, n in zip(refs, ORDER):
            x = ref[...].T if n == "win" else ref[...]
            o_ref[lay.fl[n]:lay.fl[n] + lay.rows[n], :] = x.astype(BF)

    return pl.pallas_call(
        body, name="pack_weights", out_shape=jax.ShapeDtypeStruct((lay.RT, D), BF),
        compiler_params=pltpu.CompilerParams(vmem_limit_bytes=VMEM_LIMIT))(*[shards[n] for n in ORDER])


def _load_ffn_weights(srcs, offs, scratch, sem):
    @pl.when(pl.program_id(0) == 0)
    def _():
        cps = [pltpu.make_async_copy(s.at[pl.ds(off, dst.shape[0]), :], dst, sem.at[i])
               for i, (s, off, dst) in enumerate(zip(srcs, offs, scratch))]
        for cp in cps:
            cp.start()
        for cp in cps:
            cp.wait()


def _final_loss_tile(xf, g, tgt, s_ref):
    r = lax.rsqrt(jnp.mean(xf * xf, axis=-1, keepdims=True) + EPS)
    xr = xf * r
    e = xr * g - tgt
    s_ref[1:2, :] += jnp.sum(e * e, axis=0, keepdims=True) * (0.5 / D)
    dy = e * (1.0 / D)
    s_ref[0:1, :] += jnp.sum(dy * xr, axis=0, keepdims=True)
    gdy = dy * g
    return r * gdy - xr * (r * jnp.mean(gdy * xr, axis=-1, keepdims=True))


def _ffn_fwd(x, g, wbufs, offs, name, comm=None, final=None):
    nf = F // FC

    def body(x_ref, g_ref, b0, b1, b2, *rest):
        if final is None:
            h_ref, n_ref, gg_ref, uu_ref, a_ref, wg_s, wu_s, wd_s, sem = rest
        else:
            gf_ref, t_ref, dh_ref, s_ref, n_ref, gg_ref, uu_ref, a_ref, wg_s, wu_s, wd_s, sem = rest

            @pl.when(pl.program_id(0) == 0)
            def _():
                s_ref[...] = jnp.zeros_like(s_ref)

        _load_ffn_weights((b0, b1, b2), offs, (wg_s, wu_s, wd_s), sem)
        xf = x_ref[...]
        r = lax.rsqrt(jnp.mean(xf * xf, axis=-1, keepdims=True) + EPS)
        nb = (xf * r * g_ref[...]).astype(BF)
        n_ref[...] = nb
        acc = jnp.zeros((TM, D), F32)
        for c in range(nf):
            sl = slice(c * FC, (c + 1) * FC)
            gb = _nt(nb, wg_s[sl, :]).astype(BF)
            ub = _nt(nb, wu_s[sl, :]).astype(BF)
            gg_ref[:, sl] = gb
            uu_ref[:, sl] = ub
            a = (gb * _sig(gb)) * ub
            a_ref[0, :, sl] = a
            acc = acc + _nn(a, wd_s[sl, :])
        h = xf + 0.5 * acc
        if final is None:
            h_ref[...] = h
        else:
            dh_ref[...] = _final_loss_tile(h, gf_ref[...], t_ref[...], s_ref)

    row = lambda i: (i, 0)
    vec = pl.BlockSpec((1, D), lambda i: (0, 0))
    tile = pl.BlockSpec((TM, D), row)
    saved_shapes = [jax.ShapeDtypeStruct((T, D), BF), jax.ShapeDtypeStruct((T, F), BF), jax.ShapeDtypeStruct((T, F), BF),
                    jax.ShapeDtypeStruct((1, T, F), BF)]
    saved_specs = [tile, pl.BlockSpec((TM, F), row), pl.BlockSpec((TM, F), row),
                   pl.BlockSpec((1, TM, F), lambda i: (0, i, 0))]
    if final is None:
        extra_args, extra_specs = [], []
        head_shapes, head_specs = [jax.ShapeDtypeStruct((T, D), F32)], [tile]
    else:
        extra_args, extra_specs = list(final), [vec, tile]
        head_shapes = [jax.ShapeDtypeStruct((T, D), F32), jax.ShapeDtypeStruct((8, D), F32)]
        head_specs = [tile, pl.BlockSpec((8, D), lambda i: (0, 0))]
    return _call(
        body, name=name, grid=(T // TM,), args=[x, g, *wbufs, *extra_args], comm=comm,
        in_specs=[tile, vec, ANY, ANY, ANY] + extra_specs,
        out_shape=head_shapes + saved_shapes, out_specs=head_specs + saved_specs,
        scratch_shapes=[pltpu.VMEM((F, D), BF)] * 3 + [pltpu.SemaphoreType.DMA((3,))])


def _mix_in(h1, gm, win, comm=None):
    def body(h_ref, g_ref, w_any, u_ref, z_ref, w_s, sem):
        _load_ffn_weights((w_any,), (0,), (w_s,), sem)
        xf = h_ref[...]
        r = lax.rsqrt(jnp.mean(xf * xf, axis=-1, keepdims=True) + EPS)
        ub = (xf * r * g_ref[...]).astype(BF)
        u_ref[...] = ub
        for j in range(NG):
            z_ref[j] = _nt(ub, w_s[j * D:(j + 1) * D, :]).astype(BF)

    row = lambda i: (i, 0)
    return _call(
        body, name="mix_in", grid=(T // TM,), args=[h1, gm, win], comm=comm,
        in_specs=[pl.BlockSpec((TM, D), row), pl.BlockSpec((1, D), lambda i: (0, 0)), ANY],
        out_shape=[jax.ShapeDtypeStruct((T, D), BF), jax.ShapeDtypeStruct((NG, T, D), BF)],
        out_specs=[pl.BlockSpec((TM, D), row), pl.BlockSpec((NG, TM, D), lambda i: (0, i, 0))],
        scratch_shapes=[pltpu.VMEM((NG * D, D), BF), pltpu.SemaphoreType.DMA((1,))])


def _shift_up(w, b):
    return w if b == 0 else pltpu.roll(w, w.shape[0] - b, 0)


def _fold8(p):
    red = p[0:8, :]
    for i in range(1, p.shape[0] // 8):
        red = red + p[8 * i:8 * i + 8, :]
    return red


def _dft_constants():
    import numpy as np
    nh = NB // 2
    f, n = np.arange(nh)[:, None], np.arange(NB)[None, :]
    ang = 2.0 * np.pi / NB * f * n
    fc = np.cos(ang)
    fs = np.where(f == 0, (-1.0) ** n, np.sin(ang))
    scale = np.where(f == 0, 1.0, 2.0) / NB
    ic = (scale * np.cos(ang)).T
    isn = np.where(f == 0, (-1.0) ** n / NB, scale * np.sin(ang)).T
    d = (KA - 1 - np.arange(32))[None, :]
    valid = (np.arange(32) < KA)[None, :]
    angk = 2.0 * np.pi / NB * f * d
    kc = np.where(valid, np.cos(angk), 0.0)
    ks = np.where(valid, np.sin(angk), 0.0)
    k2 = np.where(valid, np.where(f == 0, (-1.0) ** d, np.cos(angk)), 0.0)
    rtc = np.where(valid, scale * np.cos(angk), 0.0).T
    rts = np.where(valid, np.where(f == 0, (-1.0) ** d / NB, scale * np.sin(angk)), 0.0).T

    def bf(a):
        return jnp.asarray(a, F32).astype(BF)

    def split(a):
        hi = bf(a)
        return hi, (jnp.asarray(a, F32) - hi.astype(F32)).astype(BF)

    return dict(fc=bf(fc), fs=bf(fs), ic_hi=bf(ic[HB:]), is_hi=bf(isn[HB:]), ic_lo=bf(ic[:HB]), is_lo=bf(isn[:HB]),
                kc=split(kc), ks=split(ks), k2=split(k2), rtc=split(rtc), rts=split(rts))


def _dot3(m_hi, m_lo, x):
    x_hi = x.astype(BF)
    x_lo = (x - x_hi.astype(F32)).astype(BF)
    return _nn(m_hi, x_hi) + _nn(m_hi, x_lo) + _nn(m_lo, x_hi)


def _whole(a):
    return pl.BlockSpec(a.shape, lambda c, t: (0,) * a.ndim)


def _filter_spectrum(cw_ref, tabs, hc, hs, h2):
    w32 = cw_ref[0:32, :]
    for (hi, lo), dst in zip(tabs, (hc, hs, h2)):
        dst[...] = _dot3(hi[...], lo[...], w32)


def _conv_fwd_dft(z, cw, bias, dft, comm=None):
    nt = T // TB
    hb = TB // HB

    def body(z_ref, zh_ref, cw_ref, b_ref, fc_ref, fs_ref, ic_ref, is_ref, kch, kcl, ksh, ksl, k2h, k2l,
             a1_ref, q_ref, aext, ppad, hc, hs, h2):
        first = pl.program_id(1) == 0
        f = lambda ref, j: ref[j].astype(F32)

        @pl.when(first)
        def _():
            _filter_spectrum(cw_ref, ((kch, kcl), (ksh, ksl), (k2h, k2l)), hc, hs, h2)

        aext[0:HB, :] = jnp.where(first, 0.0, f(zh_ref, 0) * _sig(f(zh_ref, 1))).astype(BF)
        aext[HB:, :] = (f(z_ref, 0) * _sig(f(z_ref, 1))).astype(BF)
        ppad[0:8, :] = jnp.where(first, 0.0, f(zh_ref, 3)[HB - 8:HB, :] * f(zh_ref, 4)[HB - 8:HB, :])
        ppad[8:, :] = f(z_ref, 3) * f(z_ref, 4)
        bias_row = b_ref[...]

        for j in range(TB // HB):
            xs = aext[j * HB:j * HB + NB, :]
            xa, xb = _nn(fc_ref[...], xs), _nn(fs_ref[...], xs)
            yc = (hc[...] * xa - hs[...] * xb).astype(BF)
            ys = (h2[...] * xb + hs[...] * xa).astype(BF)
            y = _nn(ic_ref[...], yc) + _nn(is_ref[...], ys)
            a1_ref[j * HB:(j + 1) * HB, :] = (y + bias_row).astype(BF)

        def chunk(r, carry):
            base = pl.multiple_of(r * CHB, CHB)
            pw = ppad[pl.ds(base, CHB + 8), :]
            v = (cw_ref[pl.ds(32, 1), :] * _shift_up(pw, 6)[0:CHB, :]
                 + cw_ref[pl.ds(33, 1), :] * _shift_up(pw, 7)[0:CHB, :]
                 + cw_ref[pl.ds(34, 1), :] * pw[8:8 + CHB, :])
            q_ref[pl.ds(base, CHB), :] = (z_ref[2, pl.ds(base, CHB), :].astype(F32) * v).astype(BF)
            return carry

        lax.fori_loop(0, TB // CHB, chunk, 0)

    blk = pl.BlockSpec((TB, CW), lambda c, t: (t, c))
    tabs = [dft["fc"], dft["fs"], dft["ic_hi"], dft["is_hi"], *dft["kc"], *dft["ks"], *dft["k2"]]
    return _call(
        body, name="conv_fwd", grid=(D // CW, nt), comm=comm, args=[z, z, cw, bias] + tabs,
        in_specs=[pl.BlockSpec((5, TB, CW), lambda c, t: (0, t, c)),
                  pl.BlockSpec((5, HB, CW), lambda c, t: (0, jnp.maximum(t * hb - 1, 0), c)),
                  pl.BlockSpec((40, CW), lambda c, t: (0, c)), pl.BlockSpec((1, CW), lambda c, t: (0, c))]
                 + [_whole(a) for a in tabs],
        out_shape=[jax.ShapeDtypeStruct((T, D), BF), jax.ShapeDtypeStruct((T, D), BF)], out_specs=[blk, blk],
        scratch_shapes=[pltpu.VMEM((TB + HB, CW), BF), pltpu.VMEM((TB + 8, CW), F32)]
                       + [pltpu.VMEM((NB // 2, CW), F32)] * 3)


def _conv_bwd_dft(z, da1, dq, dzg, cw, dft, comm=None):
    nt = T // TB
    hb = TB // HB
    last_h = T // HB - 1

    def body(z_ref, zp_ref, zn_ref, da1_ref, da1n_ref, dq_ref, dqn_ref, dzg_ref, cw_ref,
             fc_ref, fs_ref, ic_ref, is_ref, kch, kcl, ksh, ksl, k2h, k2l, rch, rcl, rsh, rsl,
             dz_ref, dwa_ref, dwb_ref, aext, dyext, ppad, dvpad, hc, hs, h2, rc, rs, nyq, acc_b):
        t = pl.program_id(1)
        first, last = t == 0, t == nt - 1
        f = lambda ref, j: ref[j].astype(F32)

        @pl.when(first)
        def _():
            _filter_spectrum(cw_ref, ((kch, kcl), (ksh, ksl), (k2h, k2l)), hc, hs, h2)
            rc[...] = jnp.zeros_like(rc)
            rs[...] = jnp.zeros_like(rs)
            nyq[...] = jnp.zeros_like(nyq)
            acc_b[...] = jnp.zeros_like(acc_b)

        aext[0:HB, :] = jnp.where(first, 0.0, f(zp_ref, 0) * _sig(f(zp_ref, 1))).astype(BF)
        aext[HB:, :] = (f(z_ref, 0) * _sig(f(z_ref, 1))).astype(BF)
        dyext[0:TB, :] = da1_ref[...]
        dyext[TB:, :] = jnp.where(last, 0.0, da1n_ref[...].astype(F32)).astype(BF)
        ppad[0:8, :] = jnp.where(first, 0.0, f(zp_ref, 3)[HB - 8:HB, :] * f(zp_ref, 4)[HB - 8:HB, :])
        ppad[8:, :] = f(z_ref, 3) * f(z_ref, 4)
        dvpad[0:TB, :] = dq_ref[...].astype(F32) * f(z_ref, 2)
        dvpad[TB:, :] = jnp.where(last, 0.0, dqn_ref[...].astype(F32)[0:8, :] * f(zn_ref, 2)[0:8, :])

        for j in range(TB // HB):
            rows = slice(j * HB, (j + 1) * HB)
            dys = dyext[j * HB:j * HB + NB, :]
            da, db = _nn(fc_ref[...], dys), _nn(fs_ref[...], dys)
            gc = (hc[...] * da + hs[...] * db).astype(BF)
            gs = (h2[...] * db - hs[...] * da).astype(BF)
            da0 = _nn(ic_ref[...], gc) + _nn(is_ref[...], gs)
            z0, z1 = z_ref[0, rows, :].astype(F32), z_ref[1, rows, :].astype(F32)
            s1 = _sig(z1)
            dz_ref[0, rows, :] = (da0 * s1).astype(BF)
            dz_ref[1, rows, :] = (da0 * z0 * (s1 * (1.0 - s1))).astype(BF)
            xs = aext[j * HB:j * HB + NB, :]
            xa, xb = _nn(fc_ref[...], xs), _nn(fs_ref[...], xs)
            dyb = dyext[rows, :]
            pa, pb = _nn(fc_ref[:, HB:NB], dyb), _nn(fs_ref[:, HB:NB], dyb)
            rc[...] += pa * xa + pb * xb
            rs[...] += pb * xa - pa * xb
            nyq[...] += pb[0:8, :] * xb[0:8, :]

        def chunk(r, carry):
            base = pl.multiple_of(r * CHB, CHB)
            rows = pl.ds(base, CHB)
            pw = ppad[pl.ds(base, CHB + 8), :]
            p6 = _shift_up(pw, 6)[0:CHB, :]
            p7 = _shift_up(pw, 7)[0:CHB, :]
            p8 = pw[8:8 + CHB, :]
            wb0, wb1, wb2 = cw_ref[pl.ds(32, 1), :], cw_ref[pl.ds(33, 1), :], cw_ref[pl.ds(34, 1), :]
            v = wb0 * p6 + wb1 * p7 + wb2 * p8
            dz_ref[2, rows, :] = (dq_ref[rows, :].astype(F32) * v).astype(BF)
            dvw = dvpad[pl.ds(base, CHB + 8), :]
            dvc = dvw[0:CHB, :]
            dp = wb2 * dvc + wb1 * _shift_up(dvw, 1)[0:CHB, :] + wb0 * _shift_up(dvw, 2)[0:CHB, :]
            dz_ref[3, rows, :] = (dp * z_ref[4, rows, :].astype(F32)).astype(BF)
            dz_ref[4, rows, :] = (dp * z_ref[3, rows, :].astype(F32)).astype(BF)
            acc_b[0:8, :] += _fold8(dvc * p6)
            acc_b[8:16, :] += _fold8(dvc * p7)
            acc_b[16:24, :] += _fold8(dvc * p8)
            dz_ref[5, rows, :] = dzg_ref[0, rows, :]
            dz_ref[6, rows, :] = dzg_ref[1, rows, :]
            return carry

        lax.fori_loop(0, TB // CHB, chunk, 0)

        @pl.when(last)
        def _():
            row0 = lax.broadcasted_iota(jnp.int32, (NB // 2, CW), 0) == 0
            ny = jnp.broadcast_to(nyq[0:1, :], (NB // 2, CW))
            rcv = jnp.where(row0, rc[...] - ny, rc[...])
            rsv = jnp.where(row0, ny, rs[...])
            dwa_ref[...] = _dot3(rch[...], rcl[...], rcv) + _dot3(rsh[...], rsl[...], rsv)
            for k in range(KB):
                dwb_ref[k:k + 1, :] = jnp.sum(acc_b[8 * k:8 * k + 8, :], axis=0, keepdims=True)
            dwb_ref[KB:8, :] = jnp.zeros((8 - KB, CW), F32)

    blk = lambda c, t: (t, c)
    nxt = lambda c, t: (jnp.minimum((t + 1) * hb, last_h), c)
    tabs = [dft["fc"], dft["fs"], dft["ic_lo"], dft["is_lo"], *dft["kc"], *dft["ks"], *dft["k2"], *dft["rtc"], *dft["rts"]]
    return _call(
        body, name="conv_bwd", grid=(D // CW, nt), comm=comm, args=[z, z, z, da1, da1, dq, dq, dzg, cw] + tabs,
        in_specs=[pl.BlockSpec((5, TB, CW), lambda c, t: (0, t, c)),
                  pl.BlockSpec((5, HB, CW), lambda c, t: (0, jnp.maximum(t * hb - 1, 0), c)),
                  pl.BlockSpec((5, HB, CW), lambda c, t: (0, jnp.minimum((t + 1) * hb, last_h), c)),
                  pl.BlockSpec((TB, CW), blk), pl.BlockSpec((HB, CW), nxt),
                  pl.BlockSpec((TB, CW), blk), pl.BlockSpec((HB, CW), nxt),
                  pl.BlockSpec((2, TB, CW), lambda c, t: (0, t, c)),
                  pl.BlockSpec((40, CW), lambda c, t: (0, c))]
                 + [_whole(a) for a in tabs],
        out_shape=[jax.ShapeDtypeStruct((NG, T, D), BF), jax.ShapeDtypeStruct((32, D), F32),
                   jax.ShapeDtypeStruct((8, D), F32)],
        out_specs=[pl.BlockSpec((NG, TB, CW), lambda c, t: (0, t, c)),
                   pl.BlockSpec((32, CW), lambda c, t: (0, c)), pl.BlockSpec((8, CW), lambda c, t: (0, c))],
        scratch_shapes=[pltpu.VMEM((TB + HB, CW), BF), pltpu.VMEM((TB + HB, CW), BF),
                        pltpu.VMEM((TB + 8, CW), F32), pltpu.VMEM((TB + 8, CW), F32)]
                       + [pltpu.VMEM((NB // 2, CW), F32)] * 5 + [pltpu.VMEM((8, CW), F32), pltpu.VMEM((24, CW), F32)])


def _layernorm_silu(a1, lng, lnb):
    mu = jnp.mean(a1, axis=-1, keepdims=True)
    xc = a1 - mu
    rs = lax.rsqrt(jnp.mean(xc * xc, axis=-1, keepdims=True) + EPS)
    xh = xc * rs
    a2 = xh * lng + lnb
    sg = _sig(a2)
    return xh, rs, a2, sg


def _square_specs(blocks):
    return [pl.BlockSpec((D, D), lambda i, b=b: (b, 0)) for b in blocks]


def _mix_out(a1, q, z, h1, lng, lnb, wsq, comm=None):
    def body(a1_ref, q_ref, ga_ref, gb_ref, h_ref, lng_ref, lnb_ref, wa_ref, wb_ref, wo_ref, h2_ref, ya_ref, yb_ref):
        _, _, a2, sg = _layernorm_silu(a1_ref[...].astype(F32), lng_ref[...], lnb_ref[...])
        ya = _nn((a2 * sg).astype(BF), wa_ref[...])
        yb = _nn(q_ref[...], wb_ref[...])
        ya_ref[...] = ya.astype(BF)
        yb_ref[...] = yb.astype(BF)
        m = _sig(ga_ref[...].astype(F32)) * ya + _sig(gb_ref[...].astype(F32)) * yb
        h2_ref[...] = h_ref[...] + _nn(m.astype(BF), wo_ref[...])

    row = lambda i: (i, 0)
    vec = pl.BlockSpec((1, D), lambda i: (0, 0))
    return _call(
        body, name="mix_out", grid=(T // TM,), args=[a1, q, z, z, h1, lng, lnb, wsq, wsq, wsq], comm=comm,
        in_specs=[pl.BlockSpec((TM, D), row), pl.BlockSpec((TM, D), row),
                  pl.BlockSpec((None, TM, D), lambda i: (5, i, 0)), pl.BlockSpec((None, TM, D), lambda i: (6, i, 0)),
                  pl.BlockSpec((TM, D), row), vec, vec] + _square_specs((0, 1, 2)),
        out_shape=[jax.ShapeDtypeStruct((T, D), F32), jax.ShapeDtypeStruct((T, D), BF), jax.ShapeDtypeStruct((T, D), BF)],
        out_specs=[pl.BlockSpec((TM, D), row)] * 3)


def _rmsnorm_bwd(xf, g, dn):
    r = lax.rsqrt(jnp.mean(xf * xf, axis=-1, keepdims=True) + EPS)
    xr = xf * r
    gdn = dn * g
    dx = r * gdn - xr * (r * jnp.mean(gdn * xr, axis=-1, keepdims=True))
    return dx, jnp.sum(dn * xr, axis=0, keepdims=True)


def _ffn_bwd_hidden(dh, gg, uu, wbuf, off, name, comm=None):
    nf = F // FC

    def body(dh_ref, gg_ref, uu_ref, b0, dgu_ref, wd_s, sem):
        _load_ffn_weights((b0,), (off,), (wd_s,), sem)
        dhb = (0.5 * dh_ref[...]).astype(BF)
        for c in range(nf):
            sl = slice(c * FC, (c + 1) * FC)
            da = _nt(dhb, wd_s[sl, :]).astype(BF)
            gb, ub = gg_ref[:, sl], uu_ref[:, sl]
            sg = _sig(gb)
            dgu_ref[0, :, sl] = (da * ub) * (sg * (1.0 + gb * (1.0 - sg)))
            dgu_ref[0, :, F + c * FC:F + (c + 1) * FC] = da * (gb * sg)

    row = lambda i: (i, 0)
    return _call(
        body, name=name, grid=(T // TM,), args=[dh, gg, uu, wbuf], comm=comm,
        in_specs=[pl.BlockSpec((TM, D), row), pl.BlockSpec((TM, F), row), pl.BlockSpec((TM, F), row), ANY],
        out_shape=[jax.ShapeDtypeStruct((1, T, 2 * F), BF)],
        out_specs=[pl.BlockSpec((1, TM, 2 * F), lambda i: (0, i, 0))],
        scratch_shapes=[pltpu.VMEM((F, D), BF), pltpu.SemaphoreType.DMA((1,))])


def _ffn_bwd_input(dgu, dh, x, g, wbufs, offs, name, comm=None):
    def body(dgu_ref, dh_ref, x_ref, g_ref, b0, b1, dx_ref, s_ref, w_s, sem):
        _load_ffn_weights((b0, b1), offs, (w_s.at[pl.ds(0, F), :], w_s.at[pl.ds(F, F), :]), sem)

        @pl.when(pl.program_id(0) == 0)
        def _():
            s_ref[...] = jnp.zeros_like(s_ref)

        dn = _nn(dgu_ref[0], w_s[...])
        dxn, dg = _rmsnorm_bwd(x_ref[...], g_ref[...], dn)
        dx_ref[...] = dh_ref[...] + dxn
        s_ref[0:1, :] += dg

    row = lambda i: (i, 0)
    return _call(
        body, name=name, grid=(T // TM,), args=[dgu, dh, x, g, *wbufs], comm=comm,
        in_specs=[pl.BlockSpec((1, TM, 2 * F), lambda i: (0, i, 0)), pl.BlockSpec((TM, D), row),
                  pl.BlockSpec((TM, D), row), pl.BlockSpec((1, D), lambda i: (0, 0)), ANY, ANY],
        out_shape=[jax.ShapeDtypeStruct((T, D), F32), jax.ShapeDtypeStruct((8, D), F32)],
        out_specs=[pl.BlockSpec((TM, D), row), pl.BlockSpec((8, D), lambda i: (0, 0))],
        scratch_shapes=[pltpu.VMEM((2 * F, D), BF), pltpu.SemaphoreType.DMA((2,))])


def _tn_matmul(lhs, rhs, tr, name, comm=None, scale=None, cols=None):
    ng = lhs.shape[0]
    first, cdim = cols if cols is not None else (0, lhs.shape[2])
    c0, nc, nk = first // tr, cdim // tr, T // TK
    if rhs.ndim == 2:
        r_spec = pl.BlockSpec((TK, D), lambda g, c, k: (k, 0))
    else:
        r_spec = pl.BlockSpec((None, TK, D), lambda g, c, k: (g, k, 0))

    def body(l_ref, r_ref, o_ref, acc):
        k = pl.program_id(2)

        @pl.when(k == 0)
        def _():
            acc[...] = jnp.zeros_like(acc)

        r = r_ref[...] if scale is None else scale * r_ref[...]
        acc[...] += _tn(l_ref[...], r.astype(BF))

        @pl.when(k == nk - 1)
        def _():
            o_ref[...] = acc[...].astype(BF)

    return _call(
        body, name=name, grid=(ng, nc, nk), args=[lhs, rhs], comm=comm,
        in_specs=[pl.BlockSpec((None, TK, tr), lambda g, c, k: (g, k, c0 + c)), r_spec],
        out_shape=[jax.ShapeDtypeStruct((ng * cdim, D), BF)],
        out_specs=[pl.BlockSpec((tr, D), lambda g, c, k: (g * nc + c, 0))],
        scratch_shapes=[pltpu.VMEM((tr, D), F32)])


def _mix_out_bwd(dh2, ya, yb, z, a1, q, lng, lnb, wsq, comm=None):
    def body(dh_ref, ya_ref, yb_ref, ga_ref, gb_ref, a1_ref, q_ref, lng_ref, lnb_ref, wa_ref, wb_ref, wo_ref,
             dzg_ref, da1_ref, dq_ref, l_ref, r_ref, s_ref):
        @pl.when(pl.program_id(0) == 0)
        def _():
            s_ref[...] = jnp.zeros_like(s_ref)

        dhb = dh_ref[...].astype(BF)
        dm = _nt(dhb, wo_ref[...])
        ya, yb = ya_ref[...].astype(F32), yb_ref[...].astype(F32)
        sa, sb = _sig(ga_ref[...].astype(F32)), _sig(gb_ref[...].astype(F32))
        l_ref[0] = (sa * ya + sb * yb).astype(BF)
        l_ref[2] = q_ref[...]
        dzg_ref[0] = (dm * ya * (sa * (1.0 - sa))).astype(BF)
        dzg_ref[1] = (dm * yb * (sb * (1.0 - sb))).astype(BF)
        dya = (dm * sa).astype(BF)
        dyb = (dm * sb).astype(BF)
        r_ref[0] = dhb
        r_ref[1] = dya
        r_ref[2] = dyb
        dq_ref[...] = _nt(dyb, wb_ref[...]).astype(BF)
        da3 = _nt(dya, wa_ref[...])
        lng = lng_ref[...]
        xh, rs, a2, sg = _layernorm_silu(a1_ref[...].astype(F32), lng, lnb_ref[...])
        l_ref[1] = (a2 * sg).astype(BF)
        da2 = da3 * (sg * (1.0 + a2 * (1.0 - sg)))
        s_ref[0:1, :] += jnp.sum(da2 * xh, axis=0, keepdims=True)
        s_ref[1:2, :] += jnp.sum(da2, axis=0, keepdims=True)
        dxh = da2 * lng
        da1 = rs * (dxh - jnp.mean(dxh, axis=-1, keepdims=True) - xh * jnp.mean(dxh * xh, axis=-1, keepdims=True))
        da1_ref[...] = da1.astype(BF)
        s_ref[2:3, :] += jnp.sum(da1, axis=0, keepdims=True)

    row = lambda i: (i, 0)
    row3 = lambda i: (0, i, 0)
    vec = pl.BlockSpec((1, D), lambda i: (0, 0))
    return _call(
        body, name="mix_out_bwd", grid=(T // TM,), args=[dh2, ya, yb, z, z, a1, q, lng, lnb, wsq, wsq, wsq], comm=comm,
        in_specs=[pl.BlockSpec((TM, D), row), pl.BlockSpec((TM, D), row), pl.BlockSpec((TM, D), row),
                  pl.BlockSpec((None, TM, D), lambda i: (5, i, 0)), pl.BlockSpec((None, TM, D), lambda i: (6, i, 0)),
                  pl.BlockSpec((TM, D), row), pl.BlockSpec((TM, D), row), vec, vec] + _square_specs((0, 1, 2)),
        out_shape=[jax.ShapeDtypeStruct((2, T, D), BF), jax.ShapeDtypeStruct((T, D), BF),
                   jax.ShapeDtypeStruct((T, D), BF), jax.ShapeDtypeStruct((3, T, D), BF),
                   jax.ShapeDtypeStruct((3, T, D), BF), jax.ShapeDtypeStruct((8, D), F32)],
        out_specs=[pl.BlockSpec((2, TM, D), row3), pl.BlockSpec((TM, D), row), pl.BlockSpec((TM, D), row),
                   pl.BlockSpec((3, TM, D), row3), pl.BlockSpec((3, TM, D), row3), pl.BlockSpec((8, D), lambda i: (0, 0))])


def _mix_in_bwd(dz, dh2, h1, gm, win, comm=None):
    def body(dz_ref, w_any, dh_ref, h_ref, g_ref, o_ref, s_ref, w_s, sem):
        _load_ffn_weights((w_any,), (0,), (w_s,), sem)

        @pl.when(pl.program_id(0) == 0)
        def _():
            s_ref[...] = jnp.zeros_like(s_ref)

        du = _nn(dz_ref[0], w_s[0:D, :])
        for j in range(1, NG):
            du = du + _nn(dz_ref[j], w_s[j * D:(j + 1) * D, :])
        dx, dg = _rmsnorm_bwd(h_ref[...], g_ref[...], du)
        o_ref[...] = dh_ref[...] + dx
        s_ref[0:1, :] += dg

    row = lambda i: (i, 0)
    return _call(
        body, name="mix_in_bwd", grid=(T // TM,), args=[dz, win, dh2, h1, gm], comm=comm,
        in_specs=[pl.BlockSpec((NG, TM, D), lambda i: (0, i, 0)), ANY,
                  pl.BlockSpec((TM, D), row), pl.BlockSpec((TM, D), row), pl.BlockSpec((1, D), lambda i: (0, 0))],
        out_shape=[jax.ShapeDtypeStruct((T, D), F32), jax.ShapeDtypeStruct((8, D), F32)],
        out_specs=[pl.BlockSpec((TM, D), row), pl.BlockSpec((8, D), lambda i: (0, 0))],
        scratch_shapes=[pltpu.VMEM((NG * D, D), BF), pltpu.SemaphoreType.DMA((1,))])


def _row_tile(n, want, mult):
    for t in range(min(want, n), 0, -1):
        if n % t == 0 and t % mult == 0:
            return t
    return n


def _sum_slots(recv, name):
    ns, rows, cols = recv.shape
    tr = _row_tile(rows, 1024, 16)

    def body(r_ref, o_ref):
        s = r_ref[0].astype(F32)
        for k in range(1, ns):
            s = s + r_ref[k].astype(F32)
        o_ref[...] = s

    return _call(
        body, name=name, grid=(rows // tr,), args=[recv],
        in_specs=[pl.BlockSpec((ns, tr, cols), lambda i: (0, i, 0))],
        out_shape=[jax.ShapeDtypeStruct((rows, cols), F32)],
        out_specs=[pl.BlockSpec((tr, cols), lambda i: (i, 0))])[0]


def _pack_small(s_ffn1, s_in, s_mix, s_ffn2, s_final, dwa, dwb):
    def body(f1, mi, mo, f2, fl, wa_ref, wb_ref, v_ref, k_ref):
        for dst, (ref, row) in enumerate(((f1, 0), (mi, 0), (mo, 0), (mo, 1), (mo, 2), (f2, 0), (fl, 0), (fl, 1))):
            v_ref[dst:dst + 1, :] = ref[row:row + 1, :]
        for k in range(NDEV):
            k_ref[k, 0:32, :] = wa_ref[:, k * LANE:(k + 1) * LANE]
            k_ref[k, 32:40, :] = wb_ref[:, k * LANE:(k + 1) * LANE]

    return pl.pallas_call(
        body, name="pack_small",
        out_shape=(jax.ShapeDtypeStruct((8, D), F32), jax.ShapeDtypeStruct((NDEV, 40, LANE), F32)),
    )(s_ffn1, s_in, s_mix, s_ffn2, s_final, dwa, dwb)


def _sum_small(vecs, convs):
    def body(v_ref, k_ref, vs_ref, ks_ref, l_ref):
        s, c = v_ref[0], k_ref[0]
        for k in range(1, NDEV):
            s = s + v_ref[k]
            c = c + k_ref[k]
        vs_ref[...] = s
        ks_ref[...] = c
        l_ref[...] = jnp.broadcast_to(jnp.sum(s[7:8, :], axis=-1, keepdims=True), (8, LANE))

    return pl.pallas_call(
        body, name="sum_small",
        out_shape=(jax.ShapeDtypeStruct((8, D), F32), jax.ShapeDtypeStruct((40, LANE), F32),
                   jax.ShapeDtypeStruct((8, LANE), F32)),
    )(vecs, convs)


def _adam(gs, ws, ms, vs, name, comm=None):
    n = len(gs)
    rows, cols = ws[0].shape
    tr = _row_tile(rows, 256, 16)
    c1 = 1.0 - ADAM_B1 ** ADAM_STEP
    c2 = 1.0 - ADAM_B2 ** ADAM_STEP
    summed = [isinstance(g, tuple) for g in gs]

    def body(*refs):
        for i in range(n):
            g_in, w, m, v = refs[4 * i], refs[4 * i + 1][...], refs[4 * i + 2][...], refs[4 * i + 3][...]
            g_ref, d_ref, m_ref, v_ref = refs[4 * n + 4 * i: 4 * n + 4 * i + 4]
            if summed[i]:
                g = g_in[0].astype(F32)
                for k in range(1, g_in.shape[0]):
                    g = g + g_in[k].astype(F32)
            else:
                g = g_in[...]
            g_ref[...] = g
            m2 = ADAM_B1 * m + (1.0 - ADAM_B1) * g
            v2 = ADAM_B2 * v + (1.0 - ADAM_B2) * (g * g)
            d_ref[...] = -ADAM_LR * ((m2 / c1) / (jnp.sqrt(v2 / c2) + ADAM_EPS) + ADAM_WD * w)
            m_ref[...] = m2
            v_ref[...] = v2

    spec = pl.BlockSpec((tr, cols), lambda i: (i, 0))
    args, in_specs = [], []
    for i in range(n):
        if summed[i]:
            slots, first = gs[i]
            args.append(slots)
            in_specs.append(pl.BlockSpec((slots.shape[0], tr, cols), lambda i, b=first // tr: (0, b + i, 0)))
        else:
            args.append(gs[i])
            in_specs.append(spec)
        args += [ws[i], ms[i], vs[i]]
        in_specs += [spec] * 3
    outs = _call(body, name=name, grid=(rows // tr,), args=args, comm=comm, in_specs=in_specs,
                 out_shape=[jax.ShapeDtypeStruct((rows, cols), F32)] * (4 * n), out_specs=[spec] * (4 * n))
    return [tuple(outs[4 * i: 4 * i + 4]) for i in range(n)], outs[4 * n:]


def kernel(x, ffn1_norm, ffn1_w_gate, ffn1_w_up, ffn1_w_down, mix_norm, w_in, a_dw_w, a_dw_b, a_ln_g, a_ln_b, a_w_out, b_conv_w, b_w_out, w_o, ffn2_norm, ffn2_w_gate, ffn2_w_up, ffn2_w_down, final_norm, loss_target, m_ffn1_norm, m_ffn1_w_gate, m_ffn1_w_up, m_ffn1_w_down, m_mix_norm, m_w_in, m_a_dw_w, m_a_dw_b, m_a_ln_g, m_a_ln_b, m_a_w_out, m_b_conv_w, m_b_w_out, m_w_o, m_ffn2_norm, m_ffn2_w_gate, m_ffn2_w_up, m_ffn2_w_down, m_final_norm, v_ffn1_norm, v_ffn1_w_gate, v_ffn1_w_up, v_ffn1_w_down, v_mix_norm, v_w_in, v_a_dw_w, v_a_dw_b, v_a_ln_g, v_a_ln_b, v_a_w_out, v_b_conv_w, v_b_w_out, v_w_o, v_ffn2_norm, v_ffn2_w_gate, v_ffn2_w_up, v_ffn2_w_down, v_final_norm):
    names = ("ffn1_norm", "ffn1_w_gate", "ffn1_w_up", "ffn1_w_down", "mix_norm", "w_in", "a_dw_w", "a_dw_b",
             "a_ln_g", "a_ln_b", "a_w_out", "b_conv_w", "b_w_out", "w_o", "ffn2_norm", "ffn2_w_gate", "ffn2_w_up",
             "ffn2_w_down", "final_norm")
    w = dict(ffn1_norm=ffn1_norm, ffn1_w_gate=ffn1_w_gate, ffn1_w_up=ffn1_w_up, ffn1_w_down=ffn1_w_down,
             mix_norm=mix_norm, w_in=w_in, a_dw_w=a_dw_w, a_dw_b=a_dw_b, a_ln_g=a_ln_g, a_ln_b=a_ln_b,
             a_w_out=a_w_out, b_conv_w=b_conv_w, b_w_out=b_w_out, w_o=w_o, ffn2_norm=ffn2_norm,
             ffn2_w_gate=ffn2_w_gate, ffn2_w_up=ffn2_w_up, ffn2_w_down=ffn2_w_down, final_norm=final_norm)
    m = dict(ffn1_norm=m_ffn1_norm, ffn1_w_gate=m_ffn1_w_gate, ffn1_w_up=m_ffn1_w_up, ffn1_w_down=m_ffn1_w_down,
             mix_norm=m_mix_norm, w_in=m_w_in, a_dw_w=m_a_dw_w, a_dw_b=m_a_dw_b, a_ln_g=m_a_ln_g, a_ln_b=m_a_ln_b,
             a_w_out=m_a_w_out, b_conv_w=m_b_conv_w, b_w_out=m_b_w_out, w_o=m_w_o, ffn2_norm=m_ffn2_norm,
             ffn2_w_gate=m_ffn2_w_gate, ffn2_w_up=m_ffn2_w_up, ffn2_w_down=m_ffn2_w_down, final_norm=m_final_norm)
    v = dict(ffn1_norm=v_ffn1_norm, ffn1_w_gate=v_ffn1_w_gate, ffn1_w_up=v_ffn1_w_up, ffn1_w_down=v_ffn1_w_down,
             mix_norm=v_mix_norm, w_in=v_w_in, a_dw_w=v_a_dw_w, a_dw_b=v_a_dw_b, a_ln_g=v_a_ln_g, a_ln_b=v_a_ln_b,
             a_w_out=v_a_w_out, b_conv_w=v_b_conv_w, b_w_out=v_b_w_out, w_o=v_w_o, ffn2_norm=v_ffn2_norm,
             ffn2_w_gate=v_ffn2_w_gate, ffn2_w_up=v_ffn2_w_up, ffn2_w_down=v_ffn2_w_down, final_norm=v_final_norm)
    flat = _pack_weights(dict(wg1=ffn1_w_gate[0].T, wu1=ffn1_w_up[0].T, wd1=ffn1_w_down[0], wg2=ffn2_w_gate[0].T,
                              wu2=ffn2_w_up[0].T, wd2=ffn2_w_down[0], win=w_in[0], wa=a_w_out[0], wb=b_w_out[0],
                              wo=w_o[0]))
    cw_shard = jnp.concatenate([a_dw_w[0], jnp.zeros((1, LANE), F32), b_conv_w[0], jnp.zeros((5, LANE), F32)], axis=0)

    x2, tgt = x[0], loss_target[0]
    st_a, st_b, st_c, st_d, st_e = ("wg1", "wu1", "wd1"), ("win",), ("wa", "wb", "wo", "wg2"), ("wu2",), ("wd2",)

    buf_a, cw = _run_comm(_join(_ag_comm(st_a, flat), _direct_comm(cw_shard, False)), "ag_ffn1")
    h1, n1, gg1, uu1, act1, buf_b = _ffn_fwd(x2, ffn1_norm, (buf_a,) * 3, (0, F, 2 * F), "ffn1_fwd", _ag_comm(st_b, flat))
    u, z, buf_c = _mix_in(h1, mix_norm, buf_b, _ag_comm(st_c, flat))
    dft = _dft_constants()
    cw = jnp.transpose(cw, (1, 0, 2)).reshape(40, D)
    a1, q, buf_d = _conv_fwd_dft(z, cw, a_dw_b, dft, _ag_comm(st_d, flat))
    h2, ya, yb, buf_e = _mix_out(a1, q, z, h1, a_ln_g, a_ln_b, buf_c, _ag_comm(st_e, flat))
    ffn2_bufs, ffn2_offs = (buf_c, buf_d, buf_e), (3 * D, 0, 0)
    dh3, s_final, n2, gg2, uu2, act2 = _ffn_fwd(h2, ffn2_norm, ffn2_bufs, ffn2_offs, "ffn2_fwd",
                                          final=(final_norm.reshape(1, D), tgt))

    tr_f = F // 2 if (F // 2) % LANE == 0 else F
    def pair(stage, src):
        return _rs_pair_comm(stage, src)

    def chip(stage, src, pair_buf, tag):
        return _rs_chip_comm(_pair_add(stage, src, pair_buf, "pair_add_" + tag))

    (dgu2,) = _ffn_bwd_hidden(dh3, gg2, uu2, buf_e, 0, "ffn2_bwd_h")
    (gu2,) = _tn_matmul(dgu2, n2, tr_f, "dw_gu2")
    s2a, src2a = ("wg2", "wu2"), dict(wg2=(gu2, 0), wu2=(gu2, F))
    gd2, pair2a = _tn_matmul(act2, dh3, tr_f, "dw_d2", pair(s2a, src2a), scale=0.5)
    s2b, src2b = ("wd2",), dict(wd2=(gd2, 0))
    dh2, s_ffn2, pair2b = _ffn_bwd_input(dgu2, dh3, h2, ffn2_norm, (buf_c, buf_d), (3 * D, 0), "ffn2_bwd_x",
                                         pair(s2b, src2b))
    dzg, da1, dq, lsq, rsq, s_mix, recv2b = _mix_out_bwd(dh2, ya, yb, z, a1, q, a_ln_g, a_ln_b, buf_c,
                                                          chip(s2b, src2b, pair2b, "2b"))
    (gsq,) = _tn_matmul(lsq, rsq, D, "dw_square")
    ssq, srcsq = ("wa", "wb", "wo"), dict(wa=(gsq, D), wb=(gsq, 2 * D), wo=(gsq, 0))
    dz, dwa, dwb, recv2a, pairsq = _conv_bwd_dft(z, da1, dq, dzg, cw, dft,
                                                 _join(chip(s2a, src2a, pair2a, "2a"), pair(ssq, srcsq)))
    gin, recvsq = _tn_matmul(dz, u, D, "dw_in", chip(ssq, srcsq, pairsq, "sq"))
    sin_a, sin_b, srcin = ("win/0/2",), ("win/1/2",), {"win/0/2": (gin, 0), "win/1/2": (gin, 0)}
    dh1, s_in, pairin_a, pairin_b = _mix_in_bwd(dz, dh2, h1, mix_norm, buf_b,
                                                _join(pair(sin_a, srcin), pair(sin_b, srcin)))
    dgu1, recvin_a = _ffn_bwd_hidden(dh1, gg1, uu1, buf_a, 2 * F, "ffn1_bwd_h",
                                           chip(sin_a, srcin, pairin_a, "in_a"))
    gg1w, recvin_b = _tn_matmul(dgu1, n1, tr_f, "dw_g1", chip(sin_b, srcin, pairin_b, "in_b"), cols=(0, F))
    s1g, src1g = ("wg1",), dict(wg1=(gg1w, 0))
    gu1w, pair1g = _tn_matmul(dgu1, n1, tr_f, "dw_u1", pair(s1g, src1g), cols=(F, F))
    s1u, src1u = ("wu1",), dict(wu1=(gu1w, 0))
    gd1, recv1g, pair1u = _tn_matmul(act1, dh1, tr_f, "dw_d1", _join(chip(s1g, src1g, pair1g, "1g"), pair(s1u, src1u)),
                                     scale=0.5)
    s1b, src1b = ("wd1",), dict(wd1=(gd1, 0))
    dx, s_ffn1, recv1u, pair1b = _ffn_bwd_input(dgu1, dh1, x2, ffn1_norm, (buf_a, buf_a), (0, F), "ffn1_bwd_x",
                                                _join(chip(s1u, src1u, pair1u, "1u"), pair(s1b, src1b)))
    win_sum = jnp.concatenate([_sum_slots(recvin_a, "sum_in_a"), _sum_slots(recvin_b, "sum_in_b")], axis=0)

    vec8, convk = _pack_small(s_ffn1, s_in, s_mix, s_ffn2, s_final, dwa, dwb)
    vec_all, conv_all = _run_comm(_join(_direct_comm(vec8, False), _direct_comm(convk, True)), "xchg_small")
    vec_sum, conv_sum, loss_blk = _sum_small(vec_all, conv_all)
    loss = loss_blk[0, 0]

    fs = F // NDEV
    g = dict(ffn1_w_gate=(recv1g, 0), ffn1_w_up=(recv1u, 0), ffn2_w_gate=(recv2a, 0), ffn2_w_up=(recv2a, fs),
             ffn2_w_down=(recv2b, 0), a_w_out=(recvsq, 0), b_w_out=(recvsq, D // NDEV), w_o=(recvsq, 2 * (D // NDEV)),
             w_in=win_sum.T, ffn1_norm=vec_sum[0:1], mix_norm=vec_sum[1:2], a_ln_g=vec_sum[2:3], a_ln_b=vec_sum[3:4],
             a_dw_b=vec_sum[4:5], ffn2_norm=vec_sum[5:6], final_norm=vec_sum[6:7],
             a_dw_w=conv_sum[0:KA], b_conv_w=conv_sum[32:32 + KB])
    grad, upd = {}, {}

    def run(group, name, as2d=lambda a: a[0], back=lambda a, n: a.reshape(w[n].shape), comm=None):
        res, extra = _adam([g[n] for n in group], [as2d(w[n]) for n in group], [as2d(m[n]) for n in group],
                           [as2d(v[n]) for n in group], name, comm)
        for n, r in zip(group, res):
            grad[n], upd[n] = back(r[0], n), tuple(back(a, n) for a in r[1:])
        return extra

    rows_first = dict(as2d=lambda a: a[0].T, back=lambda a, n: a.T[None])
    (recv1b,) = run(("ffn1_w_gate", "ffn1_w_up", "ffn2_w_gate", "ffn2_w_up"), "adam_gate_up",
                    comm=chip(s1b, src1b, pair1b, "1b"), **rows_first)
    g["ffn1_w_down"] = (recv1b, 0)
    run(("ffn1_w_down", "ffn2_w_down"), "adam_down")
    run(("w_in",), "adam_in")
    run(("a_w_out", "b_w_out", "w_o"), "adam_square")
    run(("a_dw_w",), "adam_dw")
    run(("b_conv_w",), "adam_conv")
    run(("ffn1_norm", "mix_norm", "a_dw_b", "a_ln_g", "a_ln_b", "ffn2_norm", "final_norm"), "adam_vec",
        as2d=lambda a: a.reshape(1, D))

    return (loss, dx.reshape(x.shape), *[grad[n] for n in names], *[upd[n][0] for n in names],
            *[upd[n][1] for n in names], *[upd[n][2] for n in names])
```

```python
import jax
import jax.numpy as jnp
from jax import lax
from jax.experimental import pallas as pl
from jax.experimental.pallas import tpu as pltpu

T = 4096
D = 1024
F = 2816
NG = 7
NDEV = 8
NCHIP = 4
KA, KB = 31, 3
EPS = 1e-6
ADAM_LR, ADAM_B1, ADAM_B2, ADAM_EPS, ADAM_WD, ADAM_STEP = 0.001, 0.9, 0.999, 1e-08, 0.01, 10

TM = 512
FC = 256
TB = 1024
NB = 256
HB = NB // 2
CW = 256
CHB = 64
LANE = 128
TK = 2048
VMEM_LIMIT = 56 * 1024 * 1024

BF = jnp.bfloat16
F32 = jnp.float32
MESH = pl.DeviceIdType.MESH
ANY = pl.BlockSpec(memory_space=pl.ANY)

ORDER = ("wg1", "wu1", "wd1", "wg2", "wu2", "wd2", "win", "wa", "wb", "wo")


class _Layout:
    def __init__(self):
        fs, dis, ds = F // NDEV, NG * D // NDEV, D // NDEV
        self.rows = dict(wg1=fs, wu1=fs, wd1=fs, wg2=fs, wu2=fs, wd2=fs, win=dis, wa=ds, wb=ds, wo=ds)
        self.fl, off = {}, 0
        for n in ORDER:
            self.fl[n] = off
            off += self.rows[n]
        self.RT = off


class _Stage:
    def __init__(self, names):
        lay = _Layout()
        self.names = names
        self.rows, self.full, self.sub, self.fl = {}, {}, {}, {}
        for n in names:
            base, i, k = (n.split("/") + ["0", "1"])[:3]
            self.full[n] = lay.rows[base]
            self.rows[n] = lay.rows[base] // int(k)
            self.sub[n] = int(i) * self.rows[n]
            self.fl[n] = lay.fl[base] + self.sub[n]
        self.off, self.wc, o, w = {}, {}, 0, 0
        for n in names:
            self.off[n], self.wc[n] = o, w
            o += self.rows[n]
            w += NDEV * self.rows[n]
        self.R, self.W = o, w

    def grad_row(self, n, first, dev_lin):
        return first + dev_lin * self.full[n] + self.sub[n]


def _nt(a, b):
    return lax.dot_general(a, b, (((1,), (1,)), ((), ())), preferred_element_type=F32)


def _nn(a, b):
    return lax.dot_general(a, b, (((1,), (0,)), ((), ())), preferred_element_type=F32)


def _tn(a, b):
    return lax.dot_general(a, b, (((0,), (0,)), ((), ())), preferred_element_type=F32)


def _sig(x):
    return 1.0 / (1.0 + jnp.exp(-x))


def _position():
    return lax.axis_index("x"), lax.axis_index("y"), lax.axis_index("c")


def _peer(pos, j):
    x, y, c = pos
    return (1 - x if j & 4 else x, 1 - y if j & 2 else y, 1 - c if j & 1 else c)


def _lin(pos):
    return 4 * pos[0] + 2 * pos[1] + pos[2]


def _chip(pos):
    return 2 * pos[0] + pos[1]


class _Comm:
    def __init__(self, inputs, out_shapes, scratch, start, finish, middle=None):
        self.inputs, self.out_shapes, self.scratch = inputs, out_shapes, scratch
        self.start, self.finish, self.middle = start, finish, middle


def _call(body, *, name, grid, args, in_specs, out_shape, out_specs, scratch_shapes=(), comm=None,
          num_scalar_prefetch=0):
    in_specs, out_shape, out_specs, scratch_shapes = list(in_specs), list(out_shape), list(out_specs), list(scratch_shapes)
    n_in, n_out, n_scr = len(in_specs), len(out_shape), len(scratch_shapes)
    sp = num_scalar_prefetch
    if comm is None:
        kernel_fn = lambda *refs: body(*refs)
        c_in = c_out = c_scr = 0
    else:
        c_in, c_out, c_scr = len(comm.inputs), len(comm.out_shapes), len(comm.scratch)

        def kernel_fn(*refs):
            pre, refs = refs[:sp], refs[sp:]
            ins, cins = refs[:n_in], refs[n_in:n_in + c_in]
            o0 = n_in + c_in
            outs, couts = refs[o0:o0 + n_out], refs[o0 + n_out:o0 + n_out + c_out]
            s0 = o0 + n_out + c_out
            scr, cscr = refs[s0:s0 + n_scr], refs[s0 + n_scr:]
            step, steps = pl.program_id(0), grid[0]
            for a in range(1, len(grid)):
                step, steps = step * grid[a] + pl.program_id(a), steps * grid[a]
            first, last = step == 0, step == steps - 1

            @pl.when(first)
            def _():
                comm.start(cins, couts, cscr)

            if comm.middle is not None:
                @pl.when(step == (steps // 2 if steps > 2 else steps - 1))
                def _():
                    comm.middle(cins, couts, cscr)

            body(*pre, *ins, *outs, *scr)

            @pl.when(last)
            def _():
                comm.finish(cins, couts, cscr)

        args = list(args) + list(comm.inputs)
        in_specs += [ANY] * c_in
        out_shape += list(comm.out_shapes)
        out_specs += [ANY] * c_out
        scratch_shapes += list(comm.scratch)
    params = pltpu.CompilerParams(dimension_semantics=("arbitrary",) * len(grid), vmem_limit_bytes=VMEM_LIMIT)
    if sp:
        grid_spec = pltpu.PrefetchScalarGridSpec(num_scalar_prefetch=sp, grid=grid, in_specs=in_specs,
                                                 out_specs=out_specs, scratch_shapes=scratch_shapes)
        return pl.pallas_call(kernel_fn, name=name, grid_spec=grid_spec, out_shape=out_shape,
                              compiler_params=params)(*args)
    return pl.pallas_call(kernel_fn, name=name, grid=grid, in_specs=in_specs, out_shape=out_shape, out_specs=out_specs,
                          scratch_shapes=scratch_shapes, compiler_params=params)(*args)


def _join(a, b):
    na = (len(a.inputs), len(a.out_shapes), len(a.scratch))

    def split(refs):
        return ([r[:n] for r, n in zip(refs, na)], [r[n:] for r, n in zip(refs, na)])

    def start(*refs):
        ra, rb = split(refs)
        a.start(*ra)
        b.start(*rb)

    def finish(*refs):
        ra, rb = split(refs)
        a.finish(*ra)
        b.finish(*rb)

    def middle(*refs):
        for stage, r in zip((a, b), split(refs)):
            if stage.middle is not None:
                stage.middle(*r)

    return _Comm(list(a.inputs) + list(b.inputs), list(a.out_shapes) + list(b.out_shapes),
                 list(a.scratch) + list(b.scratch), start, finish,
                 middle if (a.middle is not None or b.middle is not None) else None)


def _run_comm(comm, name):
    def body(*refs):
        c_in, c_out = len(comm.inputs), len(comm.out_shapes)
        parts = (refs[:c_in], refs[c_in:c_in + c_out], refs[c_in + c_out:])
        comm.start(*parts)
        if comm.middle is not None:
            comm.middle(*parts)
        comm.finish(*parts)

    return pl.pallas_call(
        body, name=name, out_shape=list(comm.out_shapes), in_specs=[ANY] * len(comm.inputs),
        out_specs=[ANY] * len(comm.out_shapes), scratch_shapes=list(comm.scratch))(*comm.inputs)


def _ag_comm(names, flat):
    st = _Stage(names)

    def ring(me):
        x, y, c = me
        diagonal = x == y
        up = (jnp.where(diagonal, x, 1 - x), jnp.where(diagonal, 1 - y, y), c)
        down = (jnp.where(diagonal, 1 - x, x), jnp.where(diagonal, y, 1 - y), c)
        low = c == 0
        passed = tuple(jnp.where(low, d, u) for d, u in zip(down, up))
        target = tuple(jnp.where(low, u, d) for d, u in zip(down, up))
        return up, down, (1 - x, 1 - y, c), passed, target

    def parts(refs):
        (flat_ref,), (out_ref,), (send_sems, recv_sems, local_sem) = refs
        me = _position()

        def region(name, dev):
            r = st.rows[name]
            return out_ref.at[pl.ds(st.wc[name] + _lin(dev) * r, r), :]

        def own(name):
            return flat_ref.at[pl.ds(st.fl[name], st.rows[name]), :]

        def copies(k, dev, to, from_flat):
            return [pltpu.make_async_remote_copy(
                src_ref=own(n) if from_flat else region(n, dev), dst_ref=region(n, dev), send_sem=send_sems.at[k],
                recv_sem=recv_sems.at[k], device_id=to, device_id_type=MESH) for n in names]

        def whole(k):
            return pltpu.make_async_remote_copy(
                src_ref=flat_ref.at[pl.ds(0, st.R), :], dst_ref=out_ref.at[pl.ds(0, st.R), :],
                send_sem=send_sems.at[k], recv_sem=recv_sems.at[k], device_id=me, device_id_type=MESH)

        return me, region, own, copies, whole, flat_ref, out_ref, local_sem

    def start(*refs):
        me, region, own, copies, _, _, _, local_sem = parts(refs)
        for n in names:
            pltpu.make_async_copy(own(n), region(n, me), local_sem).start()
        up, down, _, _, _ = ring(me)
        for k, to in ((1, up), (2, down), (0, _peer(me, 1))):
            for cp in copies(k, me, to, True):
                cp.start()

    def middle(*refs):
        me, _, _, copies, whole, _, _, _ = parts(refs)
        up, down, _, passed, target = ring(me)
        sib = _peer(me, 1)
        whole(1).wait_recv()
        whole(2).wait_recv()
        for k, dev, to in ((3, passed, target), (4, down, sib), (5, up, sib)):
            for cp in copies(k, dev, to, False):
                cp.start()

    def finish(*refs):
        me, _, _, copies, whole, flat_ref, out_ref, local_sem = parts(refs)
        _, _, across, _, _ = ring(me)
        whole(3).wait_recv()
        for cp in copies(6, across, _peer(me, 1), False):
            cp.start()
        whole(0).wait_recv()
        for j in range(3):
            whole(4 + j).wait_recv()
        for k in range(7):
            whole(k).wait_send()
        pltpu.make_async_copy(flat_ref.at[pl.ds(0, st.R), :], out_ref.at[pl.ds(0, st.R), :], local_sem).wait()

    return _Comm([flat], [jax.ShapeDtypeStruct((st.W, D), BF)],
                 [pltpu.SemaphoreType.DMA((7,)), pltpu.SemaphoreType.DMA((7,)), pltpu.SemaphoreType.DMA],
                 start, finish, middle)


def _rs_pair_comm(names, src):
    st = _Stage(names)
    arrays = []
    for n in names:
        if not any(src[n][0] is a for a in arrays):
            arrays.append(src[n][0])
    idx = {n: [i for i, a in enumerate(arrays) if a is src[n][0]][0] for n in names}

    def slot_wait(refs):
        recv = refs[1][0]
        send_sem, recv_sem = refs[2]
        return pltpu.make_async_remote_copy(src_ref=recv, dst_ref=recv, send_sem=send_sem, recv_sem=recv_sem,
                                            device_id=_position(), device_id_type=MESH)

    def start(*refs):
        ins, (recv,), (send_sem, recv_sem) = refs
        me = _position()
        sib = _peer(me, 1)
        for q in range(NCHIP):
            dev = (q // 2, q % 2, sib[2])
            for n in names:
                r = st.rows[n]
                pltpu.make_async_remote_copy(
                    src_ref=ins[idx[n]].at[pl.ds(st.grad_row(n, src[n][1], _lin(dev)), r), :],
                    dst_ref=recv.at[q, pl.ds(st.off[n], r), :], send_sem=send_sem, recv_sem=recv_sem,
                    device_id=sib, device_id_type=MESH).start()

    def finish(*refs):
        w = slot_wait(refs)
        w.wait_recv()
        w.wait_send()

    return _Comm(arrays, [jax.ShapeDtypeStruct((NCHIP, st.R, D), BF)],
                 [pltpu.SemaphoreType.DMA, pltpu.SemaphoreType.DMA], start, finish)


def _pair_add(names, src, recv, name):
    st = _Stage(names)
    c_arr = jnp.reshape(lax.axis_index("c"), (1,)).astype(jnp.int32)

    def body(c_ref, *refs):
        r_ref, o_ref = refs[len(names)], refs[len(names) + 1]
        for a_ref, n in zip(refs, names):
            rows = slice(st.off[n], st.off[n] + st.rows[n])
            o_ref[rows, :] = (a_ref[...].astype(F32) + r_ref[rows, :].astype(F32)).astype(BF)

    def shard_spec(n):
        r = st.rows[n]
        base, step = st.grad_row(n, src[n][1], 0) // r, st.full[n] // r
        return pl.BlockSpec((r, D), lambda q, c_ref: (base + step * (2 * q + c_ref[0]), 0))

    slot = pl.BlockSpec((None, st.R, D), lambda q, c_ref: (q, 0, 0))
    return _call(body, name=name, grid=(NCHIP,), args=[c_arr] + [src[n][0] for n in names] + [recv],
                 in_specs=[shard_spec(n) for n in names] + [slot],
                 out_shape=[jax.ShapeDtypeStruct((NCHIP, st.R, D), BF)], out_specs=[slot], num_scalar_prefetch=1)[0]


def _rs_chip_comm(part):
    def copies(refs):
        (p_ref,), (recv,), (send_sems, recv_sems, local_sem) = refs
        me = _position()
        mine = pltpu.make_async_copy(p_ref.at[_chip(me)], recv.at[_chip(me)], local_sem)
        out = []
        for j, bits in enumerate((4, 2, 6)):
            to = _peer(me, bits)
            out.append(pltpu.make_async_remote_copy(
                src_ref=p_ref.at[_chip(to)], dst_ref=recv.at[_chip(me)], send_sem=send_sems.at[j],
                recv_sem=recv_sems.at[j], device_id=to, device_id_type=MESH))
        return mine, out

    def start(*refs):
        mine, out = copies(refs)
        mine.start()
        for cp in out:
            cp.start()

    def finish(*refs):
        mine, out = copies(refs)
        for cp in out:
            cp.wait_recv()
        for cp in out:
            cp.wait_send()
        mine.wait()

    return _Comm([part], [jax.ShapeDtypeStruct(part.shape, BF)],
                 [pltpu.SemaphoreType.DMA((3,)), pltpu.SemaphoreType.DMA((3,)), pltpu.SemaphoreType.DMA],
                 start, finish)


def _direct_comm(x, scatter):
    def copies(refs):
        (x_ref,), (out_ref,), (send_sems, recv_sems, local_sem) = refs
        me = _position()

        def piece(dev):
            return x_ref.at[_lin(dev)] if scatter else x_ref

        mine = pltpu.make_async_copy(piece(me), out_ref.at[_lin(me)], local_sem)
        return mine, [pltpu.make_async_remote_copy(
            src_ref=piece(_peer(me, j)), dst_ref=out_ref.at[_lin(me)], send_sem=send_sems.at[j - 1],
            recv_sem=recv_sems.at[j - 1], device_id=_peer(me, j), device_id_type=MESH) for j in range(1, NDEV)]

    def start(*refs):
        mine, cps = copies(refs)
        mine.start()
        for cp in cps:
            cp.start()

    def finish(*refs):
        mine, cps = copies(refs)
        for cp in cps:
            cp.wait_recv()
        for cp in cps:
            cp.wait_send()
        mine.wait()

    shape = x.shape if scatter else (NDEV,) + x.shape
    return _Comm([x], [jax.ShapeDtypeStruct(shape, x.dtype)],
                 [pltpu.SemaphoreType.DMA((7,)), pltpu.SemaphoreType.DMA((7,)), pltpu.SemaphoreType.DMA],
                 start, finish)


def _pack_weights(shards):
    lay = _Layout()

    def body(*refs):
        o_ref = refs[-1]
        for ref, n in zip(refs, ORDER):
            x = ref[...].T if n == "win" else ref[...]
            o_ref[lay.fl[n]:lay.fl[n] + lay.rows[n], :] = x.astype(BF)

    return pl.pallas_call(
        body, name="pack_weights", out_shape=jax.ShapeDtypeStruct((lay.RT, D), BF),
        compiler_params=pltpu.CompilerParams(vmem_limit_bytes=VMEM_LIMIT))(*[shards[n] for n in ORDER])


def _load_ffn_weights(srcs, offs, scratch, sem):
    @pl.when(pl.program_id(0) == 0)
    def _():
        cps = [pltpu.make_async_copy(s.at[pl.ds(off, dst.shape[0]), :], dst, sem.at[i])
               for i, (s, off, dst) in enumerate(zip(srcs, offs, scratch))]
        for cp in cps:
            cp.start()
        for cp in cps:
            cp.wait()


def _final_loss_tile(xf, g, tgt, s_ref):
    r = lax.rsqrt(jnp.mean(xf * xf, axis=-1, keepdims=True) + EPS)
    xr = xf * r
    e = xr * g - tgt
    s_ref[1:2, :] += jnp.sum(e * e, axis=0, keepdims=True) * (0.5 / D)
    dy = e * (1.0 / D)
    s_ref[0:1, :] += jnp.sum(dy * xr, axis=0, keepdims=True)
    gdy = dy * g
    return r * gdy - xr * (r * jnp.mean(gdy * xr, axis=-1, keepdims=True))


def _ffn_fwd(x, g, wbufs, offs, name, comm=None, final=None):
    nf = F // FC

    def body(x_ref, g_ref, b0, b1, b2, *rest):
        if final is None:
            h_ref, n_ref, gg_ref, uu_ref, a_ref, wg_s, wu_s, wd_s, sem = rest
        else:
            gf_ref, t_ref, dh_ref, s_ref, n_ref, gg_ref, uu_ref, a_ref, wg_s, wu_s, wd_s, sem = rest

            @pl.when(pl.program_id(0) == 0)
            def _():
                s_ref[...] = jnp.zeros_like(s_ref)

        _load_ffn_weights((b0, b1, b2), offs, (wg_s, wu_s, wd_s), sem)
        xf = x_ref[...]
        r = lax.rsqrt(jnp.mean(xf * xf, axis=-1, keepdims=True) + EPS)
        nb = (xf * r * g_ref[...]).astype(BF)
        n_ref[...] = nb
        acc = jnp.zeros((TM, D), F32)
        for c in range(nf):
            sl = slice(c * FC, (c + 1) * FC)
            gb = _nt(nb, wg_s[sl, :]).astype(BF)
            ub = _nt(nb, wu_s[sl, :]).astype(BF)
            gg_ref[:, sl] = gb
            uu_ref[:, sl] = ub
            a = (gb * _sig(gb)) * ub
            a_ref[0, :, sl] = a
            acc = acc + _nn(a, wd_s[sl, :])
        h = xf + 0.5 * acc
        if final is None:
            h_ref[...] = h
        else:
            dh_ref[...] = _final_loss_tile(h, gf_ref[...], t_ref[...], s_ref)

    row = lambda i: (i, 0)
    vec = pl.BlockSpec((1, D), lambda i: (0, 0))
    tile = pl.BlockSpec((TM, D), row)
    saved_shapes = [jax.ShapeDtypeStruct((T, D), BF), jax.ShapeDtypeStruct((T, F), BF), jax.ShapeDtypeStruct((T, F), BF),
                    jax.ShapeDtypeStruct((1, T, F), BF)]
    saved_specs = [tile, pl.BlockSpec((TM, F), row), pl.BlockSpec((TM, F), row),
                   pl.BlockSpec((1, TM, F), lambda i: (0, i, 0))]
    if final is None:
        extra_args, extra_specs = [], []
        head_shapes, head_specs = [jax.ShapeDtypeStruct((T, D), F32)], [tile]
    else:
        extra_args, extra_specs = list(final), [vec, tile]
        head_shapes = [jax.ShapeDtypeStruct((T, D), F32), jax.ShapeDtypeStruct((8, D), F32)]
        head_specs = [tile, pl.BlockSpec((8, D), lambda i: (0, 0))]
    return _call(
        body, name=name, grid=(T // TM,), args=[x, g, *wbufs, *extra_args], comm=comm,
        in_specs=[tile, vec, ANY, ANY, ANY] + extra_specs,
        out_shape=head_shapes + saved_shapes, out_specs=head_specs + saved_specs,
        scratch_shapes=[pltpu.VMEM((F, D), BF)] * 3 + [pltpu.SemaphoreType.DMA((3,))])


def _ffn_gate_up(x, g, wbufs, offs, name, comm=None):
    nf = F // FC

    def body(x_ref, g_ref, b0, b1, n_ref, gg_ref, uu_ref, a_ref, wg_s, wu_s, sem):
        _load_ffn_weights((b0, b1), offs, (wg_s, wu_s), sem)
        xf = x_ref[...]
        r = lax.rsqrt(jnp.mean(xf * xf, axis=-1, keepdims=True) + EPS)
        nb = (xf * r * g_ref[...]).astype(BF)
        n_ref[...] = nb
        for c in range(nf):
            sl = slice(c * FC, (c + 1) * FC)
            gb = _nt(nb, wg_s[sl, :]).astype(BF)
            ub = _nt(nb, wu_s[sl, :]).astype(BF)
            gg_ref[:, sl] = gb
            uu_ref[:, sl] = ub
            a_ref[0, :, sl] = (gb * _sig(gb)) * ub

    row = lambda i: (i, 0)
    tile = pl.BlockSpec((TM, D), row)
    return _call(
        body, name=name, grid=(T // TM,), args=[x, g, *wbufs], comm=comm,
        in_specs=[tile, pl.BlockSpec((1, D), lambda i: (0, 0)), ANY, ANY],
        out_shape=[jax.ShapeDtypeStruct((T, D), BF), jax.ShapeDtypeStruct((T, F), BF), jax.ShapeDtypeStruct((T, F), BF),
                   jax.ShapeDtypeStruct((1, T, F), BF)],
        out_specs=[tile, pl.BlockSpec((TM, F), row), pl.BlockSpec((TM, F), row),
                   pl.BlockSpec((1, TM, F), lambda i: (0, i, 0))],
        scratch_shapes=[pltpu.VMEM((F, D), BF)] * 2 + [pltpu.SemaphoreType.DMA((2,))])


def _ffn_down(x, act, wbuf, off, name, comm=None):
    def body(x_ref, a_ref, b0, h_ref, wd_s, sem):
        _load_ffn_weights((b0,), (off,), (wd_s,), sem)
        h_ref[...] = x_ref[...] + 0.5 * _nn(a_ref[0], wd_s[...])

    tile = pl.BlockSpec((TM, D), lambda i: (i, 0))
    return _call(
        body, name=name, grid=(T // TM,), args=[x, act, wbuf], comm=comm,
        in_specs=[tile, pl.BlockSpec((1, TM, F), lambda i: (0, i, 0)), ANY],
        out_shape=[jax.ShapeDtypeStruct((T, D), F32)], out_specs=[tile],
        scratch_shapes=[pltpu.VMEM((F, D), BF), pltpu.SemaphoreType.DMA((1,))])


def _load_in_proj(parts, w_s, sem):
    @pl.when(pl.program_id(0) == 0)
    def _():
        shard = NG * D // NDEV
        rows = shard // len(parts)
        cps = [pltpu.make_async_copy(buf.at[pl.ds(first + k * rows, rows), :],
                                     w_s.at[pl.ds(k * shard + p * rows, rows), :], sem.at[p * NDEV + k])
               for p, (buf, first) in enumerate(parts) for k in range(NDEV)]
        for cp in cps:
            cp.start()
        for cp in cps:
            cp.wait()


def _mix_in(h1, gm, win, comm=None):
    def body(h_ref, g_ref, *rest):
        w_any, (u_ref, z_ref, w_s, sem) = rest[:len(win)], rest[len(win):]
        _load_in_proj([(b, first) for b, (_, first) in zip(w_any, win)], w_s, sem)
        xf = h_ref[...]
        r = lax.rsqrt(jnp.mean(xf * xf, axis=-1, keepdims=True) + EPS)
        ub = (xf * r * g_ref[...]).astype(BF)
        u_ref[...] = ub
        for j in range(NG):
            z_ref[j] = _nt(ub, w_s[j * D:(j + 1) * D, :]).astype(BF)

    row = lambda i: (i, 0)
    return _call(
        body, name="mix_in", grid=(T // TM,), args=[h1, gm] + [b for b, _ in win], comm=comm,
        in_specs=[pl.BlockSpec((TM, D), row), pl.BlockSpec((1, D), lambda i: (0, 0))] + [ANY] * len(win),
        out_shape=[jax.ShapeDtypeStruct((T, D), BF), jax.ShapeDtypeStruct((NG, T, D), BF)],
        out_specs=[pl.BlockSpec((TM, D), row), pl.BlockSpec((NG, TM, D), lambda i: (0, i, 0))],
        scratch_shapes=[pltpu.VMEM((NG * D, D), BF), pltpu.SemaphoreType.DMA((NDEV * len(win),))])


def _shift_up(w, b):
    return w if b == 0 else pltpu.roll(w, w.shape[0] - b, 0)


def _fold8(p):
    red = p[0:8, :]
    for i in range(1, p.shape[0] // 8):
        red = red + p[8 * i:8 * i + 8, :]
    return red


def _dft_constants():
    import numpy as np
    nh = NB // 2
    f, n = np.arange(nh)[:, None], np.arange(NB)[None, :]
    ang = 2.0 * np.pi / NB * f * n
    fc = np.cos(ang)
    fs = np.where(f == 0, (-1.0) ** n, np.sin(ang))
    scale = np.where(f == 0, 1.0, 2.0) / NB
    ic = (scale * np.cos(ang)).T
    isn = np.where(f == 0, (-1.0) ** n / NB, scale * np.sin(ang)).T
    d = (KA - 1 - np.arange(32))[None, :]
    valid = (np.arange(32) < KA)[None, :]
    angk = 2.0 * np.pi / NB * f * d
    kc = np.where(valid, np.cos(angk), 0.0)
    ks = np.where(valid, np.sin(angk), 0.0)
    k2 = np.where(valid, np.where(f == 0, (-1.0) ** d, np.cos(angk)), 0.0)
    rtc = np.where(valid, scale * np.cos(angk), 0.0).T
    rts = np.where(valid, np.where(f == 0, (-1.0) ** d / NB, scale * np.sin(angk)), 0.0).T

    def bf(a):
        return jnp.asarray(a, F32).astype(BF)

    def split(a):
        hi = bf(a)
        return hi, (jnp.asarray(a, F32) - hi.astype(F32)).astype(BF)

    return dict(fc=bf(fc), fs=bf(fs), ic_hi=bf(ic[HB:]), is_hi=bf(isn[HB:]), ic_lo=bf(ic[:HB]), is_lo=bf(isn[:HB]),
                kc=split(kc), ks=split(ks), k2=split(k2), rtc=split(rtc), rts=split(rts))


def _dot3(m_hi, m_lo, x):
    x_hi = x.astype(BF)
    x_lo = (x - x_hi.astype(F32)).astype(BF)
    return _nn(m_hi, x_hi) + _nn(m_hi, x_lo) + _nn(m_lo, x_hi)


def _whole(a):
    return pl.BlockSpec(a.shape, lambda c, t: (0,) * a.ndim)


def _filter_spectrum(cw_ref, tabs, hc, hs, h2):
    w32 = cw_ref[0:32, :]
    for (hi, lo), dst in zip(tabs, (hc, hs, h2)):
        dst[...] = _dot3(hi[...], lo[...], w32)


def _conv_fwd_dft(z, cw, bias, dft, comm=None):
    nt = T // TB
    hb = TB // HB

    def body(z_ref, zh_ref, cw_ref, b_ref, fc_ref, fs_ref, ic_ref, is_ref, kch, kcl, ksh, ksl, k2h, k2l,
             a1_ref, q_ref, aext, ppad, hc, hs, h2):
        first = pl.program_id(1) == 0
        f = lambda ref, j: ref[j].astype(F32)

        @pl.when(first)
        def _():
            _filter_spectrum(cw_ref, ((kch, kcl), (ksh, ksl), (k2h, k2l)), hc, hs, h2)

        aext[0:HB, :] = jnp.where(first, 0.0, f(zh_ref, 0) * _sig(f(zh_ref, 1))).astype(BF)
        aext[HB:, :] = (f(z_ref, 0) * _sig(f(z_ref, 1))).astype(BF)
        ppad[0:8, :] = jnp.where(first, 0.0, f(zh_ref, 3)[HB - 8:HB, :] * f(zh_ref, 4)[HB - 8:HB, :])
        ppad[8:, :] = f(z_ref, 3) * f(z_ref, 4)
        bias_row = b_ref[...]

        for j in range(TB // HB):
            xs = aext[j * HB:j * HB + NB, :]
            xa, xb = _nn(fc_ref[...], xs), _nn(fs_ref[...], xs)
            yc = (hc[...] * xa - hs[...] * xb).astype(BF)
            ys = (h2[...] * xb + hs[...] * xa).astype(BF)
            y = _nn(ic_ref[...], yc) + _nn(is_ref[...], ys)
            a1_ref[j * HB:(j + 1) * HB, :] = (y + bias_row).astype(BF)

        def chunk(r, carry):
            base = pl.multiple_of(r * CHB, CHB)
            pw = ppad[pl.ds(base, CHB + 8), :]
            v = (cw_ref[pl.ds(32, 1), :] * _shift_up(pw, 6)[0:CHB, :]
                 + cw_ref[pl.ds(33, 1), :] * _shift_up(pw, 7)[0:CHB, :]
                 + cw_ref[pl.ds(34, 1), :] * pw[8:8 + CHB, :])
            q_ref[pl.ds(base, CHB), :] = (z_ref[2, pl.ds(base, CHB), :].astype(F32) * v).astype(BF)
            return carry

        lax.fori_loop(0, TB // CHB, chunk, 0)

    blk = pl.BlockSpec((TB, CW), lambda c, t: (t, c))
    tabs = [dft["fc"], dft["fs"], dft["ic_hi"], dft["is_hi"], *dft["kc"], *dft["ks"], *dft["k2"]]
    return _call(
        body, name="conv_fwd", grid=(D // CW, nt), comm=comm, args=[z, z, cw, bias] + tabs,
        in_specs=[pl.BlockSpec((5, TB, CW), lambda c, t: (0, t, c)),
                  pl.BlockSpec((5, HB, CW), lambda c, t: (0, jnp.maximum(t * hb - 1, 0), c)),
                  pl.BlockSpec((40, CW), lambda c, t: (0, c)), pl.BlockSpec((1, CW), lambda c, t: (0, c))]
                 + [_whole(a) for a in tabs],
        out_shape=[jax.ShapeDtypeStruct((T, D), BF), jax.ShapeDtypeStruct((T, D), BF)], out_specs=[blk, blk],
        scratch_shapes=[pltpu.VMEM((TB + HB, CW), BF), pltpu.VMEM((TB + 8, CW), F32)]
                       + [pltpu.VMEM((NB // 2, CW), F32)] * 3)


def _conv_bwd_dft(z, da1, dq, dzg, cw, dft, comm=None):
    nt = T // TB
    hb = TB // HB
    last_h = T // HB - 1

    def body(z_ref, zp_ref, zn_ref, da1_ref, da1n_ref, dq_ref, dqn_ref, dzg_ref, cw_ref,
             fc_ref, fs_ref, ic_ref, is_ref, kch, kcl, ksh, ksl, k2h, k2l, rch, rcl, rsh, rsl,
             dz_ref, dwa_ref, dwb_ref, aext, dyext, ppad, dvpad, hc, hs, h2, rc, rs, nyq, acc_b):
        t = pl.program_id(1)
        first, last = t == 0, t == nt - 1
        f = lambda ref, j: ref[j].astype(F32)

        @pl.when(first)
        def _():
            _filter_spectrum(cw_ref, ((kch, kcl), (ksh, ksl), (k2h, k2l)), hc, hs, h2)
            rc[...] = jnp.zeros_like(rc)
            rs[...] = jnp.zeros_like(rs)
            nyq[...] = jnp.zeros_like(nyq)
            acc_b[...] = jnp.zeros_like(acc_b)

        aext[0:HB, :] = jnp.where(first, 0.0, f(zp_ref, 0) * _sig(f(zp_ref, 1))).astype(BF)
        aext[HB:, :] = (f(z_ref, 0) * _sig(f(z_ref, 1))).astype(BF)
        dyext[0:TB, :] = da1_ref[...]
        dyext[TB:, :] = jnp.where(last, 0.0, da1n_ref[...].astype(F32)).astype(BF)
        ppad[0:8, :] = jnp.where(first, 0.0, f(zp_ref, 3)[HB - 8:HB, :] * f(zp_ref, 4)[HB - 8:HB, :])
        ppad[8:, :] = f(z_ref, 3) * f(z_ref, 4)
        dvpad[0:TB, :] = dq_ref[...].astype(F32) * f(z_ref, 2)
        dvpad[TB:, :] = jnp.where(last, 0.0, dqn_ref[...].astype(F32)[0:8, :] * f(zn_ref, 2)[0:8, :])

        for j in range(TB // HB):
            rows = slice(j * HB, (j + 1) * HB)
            dys = dyext[j * HB:j * HB + NB, :]
            da, db = _nn(fc_ref[...], dys), _nn(fs_ref[...], dys)
            gc = (hc[...] * da + hs[...] * db).astype(BF)
            gs = (h2[...] * db - hs[...] * da).astype(BF)
            da0 = _nn(ic_ref[...], gc) + _nn(is_ref[...], gs)
            z0, z1 = z_ref[0, rows, :].astype(F32), z_ref[1, rows, :].astype(F32)
            s1 = _sig(z1)
            dz_ref[0, rows, :] = (da0 * s1).astype(BF)
            dz_ref[1, rows, :] = (da0 * z0 * (s1 * (1.0 - s1))).astype(BF)
            xs = aext[j * HB:j * HB + NB, :]
            xa, xb = _nn(fc_ref[...], xs), _nn(fs_ref[...], xs)
            dyb = dyext[rows, :]
            pa, pb = _nn(fc_ref[:, HB:NB], dyb), _nn(fs_ref[:, HB:NB], dyb)
            rc[...] += pa * xa + pb * xb
            rs[...] += pb * xa - pa * xb
            nyq[...] += pb[0:8, :] * xb[0:8, :]

        def chunk(r, carry):
            base = pl.multiple_of(r * CHB, CHB)
            rows = pl.ds(base, CHB)
            pw = ppad[pl.ds(base, CHB + 8), :]
            p6 = _shift_up(pw, 6)[0:CHB, :]
            p7 = _shift_up(pw, 7)[0:CHB, :]
            p8 = pw[8:8 + CHB, :]
            wb0, wb1, wb2 = cw_ref[pl.ds(32, 1), :], cw_ref[pl.ds(33, 1), :], cw_ref[pl.ds(34, 1), :]
            v = wb0 * p6 + wb1 * p7 + wb2 * p8
            dz_ref[2, rows, :] = (dq_ref[rows, :].astype(F32) * v).astype(BF)
            dvw = dvpad[pl.ds(base, CHB + 8), :]
            dvc = dvw[0:CHB, :]
            dp = wb2 * dvc + wb1 * _shift_up(dvw, 1)[0:CHB, :] + wb0 * _shift_up(dvw, 2)[0:CHB, :]
            dz_ref[3, rows, :] = (dp * z_ref[4, rows, :].astype(F32)).astype(BF)
            dz_ref[4, rows, :] = (dp * z_ref[3, rows, :].astype(F32)).astype(BF)
            acc_b[0:8, :] += _fold8(dvc * p6)
            acc_b[8:16, :] += _fold8(dvc * p7)
            acc_b[16:24, :] += _fold8(dvc * p8)
            dz_ref[5, rows, :] = dzg_ref[0, rows, :]
            dz_ref[6, rows, :] = dzg_ref[1, rows, :]
            return carry

        lax.fori_loop(0, TB // CHB, chunk, 0)

        @pl.when(last)
        def _():
            row0 = lax.broadcasted_iota(jnp.int32, (NB // 2, CW), 0) == 0
            ny = jnp.broadcast_to(nyq[0:1, :], (NB // 2, CW))
            rcv = jnp.where(row0, rc[...] - ny, rc[...])
            rsv = jnp.where(row0, ny, rs[...])
            dwa_ref[...] = _dot3(rch[...], rcl[...], rcv) + _dot3(rsh[...], rsl[...], rsv)
            for k in range(KB):
                dwb_ref[k:k + 1, :] = jnp.sum(acc_b[8 * k:8 * k + 8, :], axis=0, keepdims=True)
            dwb_ref[KB:8, :] = jnp.zeros((8 - KB, CW), F32)

    blk = lambda c, t: (t, c)
    nxt = lambda c, t: (jnp.minimum((t + 1) * hb, last_h), c)
    tabs = [dft["fc"], dft["fs"], dft["ic_lo"], dft["is_lo"], *dft["kc"], *dft["ks"], *dft["k2"], *dft["rtc"], *dft["rts"]]
    return _call(
        body, name="conv_bwd", grid=(D // CW, nt), comm=comm, args=[z, z, z, da1, da1, dq, dq, dzg, cw] + tabs,
        in_specs=[pl.BlockSpec((5, TB, CW), lambda c, t: (0, t, c)),
                  pl.BlockSpec((5, HB, CW), lambda c, t: (0, jnp.maximum(t * hb - 1, 0), c)),
                  pl.BlockSpec((5, HB, CW), lambda c, t: (0, jnp.minimum((t + 1) * hb, last_h), c)),
                  pl.BlockSpec((TB, CW), blk), pl.BlockSpec((HB, CW), nxt),
                  pl.BlockSpec((TB, CW), blk), pl.BlockSpec((HB, CW), nxt),
                  pl.BlockSpec((2, TB, CW), lambda c, t: (0, t, c)),
                  pl.BlockSpec((40, CW), lambda c, t: (0, c))]
                 + [_whole(a) for a in tabs],
        out_shape=[jax.ShapeDtypeStruct((NG, T, D), BF), jax.ShapeDtypeStruct((32, D), F32),
                   jax.ShapeDtypeStruct((8, D), F32)],
        out_specs=[pl.BlockSpec((NG, TB, CW), lambda c, t: (0, t, c)),
                   pl.BlockSpec((32, CW), lambda c, t: (0, c)), pl.BlockSpec((8, CW), lambda c, t: (0, c))],
        scratch_shapes=[pltpu.VMEM((TB + HB, CW), BF), pltpu.VMEM((TB + HB, CW), BF),
                        pltpu.VMEM((TB + 8, CW), F32), pltpu.VMEM((TB + 8, CW), F32)]
                       + [pltpu.VMEM((NB // 2, CW), F32)] * 5 + [pltpu.VMEM((8, CW), F32), pltpu.VMEM((24, CW), F32)])


def _layernorm_silu(a1, lng, lnb):
    mu = jnp.mean(a1, axis=-1, keepdims=True)
    xc = a1 - mu
    rs = lax.rsqrt(jnp.mean(xc * xc, axis=-1, keepdims=True) + EPS)
    xh = xc * rs
    a2 = xh * lng + lnb
    sg = _sig(a2)
    return xh, rs, a2, sg


def _square_specs(blocks):
    return [pl.BlockSpec((D, D), lambda i, b=b: (b, 0)) for b in blocks]


def _mix_out(a1, q, z, h1, lng, lnb, wsq, comm=None):
    def body(a1_ref, q_ref, ga_ref, gb_ref, h_ref, lng_ref, lnb_ref, wa_ref, wb_ref, wo_ref, h2_ref, ya_ref, yb_ref):
        _, _, a2, sg = _layernorm_silu(a1_ref[...].astype(F32), lng_ref[...], lnb_ref[...])
        ya = _nn((a2 * sg).astype(BF), wa_ref[...])
        yb = _nn(q_ref[...], wb_ref[...])
        ya_ref[...] = ya.astype(BF)
        yb_ref[...] = yb.astype(BF)
        m = _sig(ga_ref[...].astype(F32)) * ya + _sig(gb_ref[...].astype(F32)) * yb
        h2_ref[...] = h_ref[...] + _nn(m.astype(BF), wo_ref[...])

    row = lambda i: (i, 0)
    vec = pl.BlockSpec((1, D), lambda i: (0, 0))
    return _call(
        body, name="mix_out", grid=(T // TM,), args=[a1, q, z, z, h1, lng, lnb, wsq, wsq, wsq], comm=comm,
        in_specs=[pl.BlockSpec((TM, D), row), pl.BlockSpec((TM, D), row),
                  pl.BlockSpec((None, TM, D), lambda i: (5, i, 0)), pl.BlockSpec((None, TM, D), lambda i: (6, i, 0)),
                  pl.BlockSpec((TM, D), row), vec, vec] + _square_specs((0, 1, 2)),
        out_shape=[jax.ShapeDtypeStruct((T, D), F32), jax.ShapeDtypeStruct((T, D), BF), jax.ShapeDtypeStruct((T, D), BF)],
        out_specs=[pl.BlockSpec((TM, D), row)] * 3)


def _rmsnorm_bwd(xf, g, dn):
    r = lax.rsqrt(jnp.mean(xf * xf, axis=-1, keepdims=True) + EPS)
    xr = xf * r
    gdn = dn * g
    dx = r * gdn - xr * (r * jnp.mean(gdn * xr, axis=-1, keepdims=True))
    return dx, jnp.sum(dn * xr, axis=0, keepdims=True)


def _ffn_bwd_hidden(dh, gg, uu, wbuf, off, name, comm=None):
    nf = F // FC

    def body(dh_ref, gg_ref, uu_ref, b0, dgu_ref, wd_s, sem):
        _load_ffn_weights((b0,), (off,), (wd_s,), sem)
        dhb = (0.5 * dh_ref[...]).astype(BF)
        for c in range(nf):
            sl = slice(c * FC, (c + 1) * FC)
            da = _nt(dhb, wd_s[sl, :]).astype(BF)
            gb, ub = gg_ref[:, sl], uu_ref[:, sl]
            sg = _sig(gb)
            dgu_ref[0, :, sl] = (da * ub) * (sg * (1.0 + gb * (1.0 - sg)))
            dgu_ref[0, :, F + c * FC:F + (c + 1) * FC] = da * (gb * sg)

    row = lambda i: (i, 0)
    return _call(
        body, name=name, grid=(T // TM,), args=[dh, gg, uu, wbuf], comm=comm,
        in_specs=[pl.BlockSpec((TM, D), row), pl.BlockSpec((TM, F), row), pl.BlockSpec((TM, F), row), ANY],
        out_shape=[jax.ShapeDtypeStruct((1, T, 2 * F), BF)],
        out_specs=[pl.BlockSpec((1, TM, 2 * F), lambda i: (0, i, 0))],
        scratch_shapes=[pltpu.VMEM((F, D), BF), pltpu.SemaphoreType.DMA((1,))])


def _ffn_bwd_input(dgu, dh, x, g, wbufs, offs, name, comm=None):
    def body(dgu_ref, dh_ref, x_ref, g_ref, b0, b1, dx_ref, s_ref, w_s, sem):
        _load_ffn_weights((b0, b1), offs, (w_s.at[pl.ds(0, F), :], w_s.at[pl.ds(F, F), :]), sem)

        @pl.when(pl.program_id(0) == 0)
        def _():
            s_ref[...] = jnp.zeros_like(s_ref)

        dn = _nn(dgu_ref[0], w_s[...])
        dxn, dg = _rmsnorm_bwd(x_ref[...], g_ref[...], dn)
        dx_ref[...] = dh_ref[...] + dxn
        s_ref[0:1, :] += dg

    row = lambda i: (i, 0)
    return _call(
        body, name=name, grid=(T // TM,), args=[dgu, dh, x, g, *wbufs], comm=comm,
        in_specs=[pl.BlockSpec((1, TM, 2 * F), lambda i: (0, i, 0)), pl.BlockSpec((TM, D), row),
                  pl.BlockSpec((TM, D), row), pl.BlockSpec((1, D), lambda i: (0, 0)), ANY, ANY],
        out_shape=[jax.ShapeDtypeStruct((T, D), F32), jax.ShapeDtypeStruct((8, D), F32)],
        out_specs=[pl.BlockSpec((TM, D), row), pl.BlockSpec((8, D), lambda i: (0, 0))],
        scratch_shapes=[pltpu.VMEM((2 * F, D), BF), pltpu.SemaphoreType.DMA((2,))])


def _tn_matmul(lhs, rhs, tr, name, comm=None, scale=None):
    ng, _, cdim = lhs.shape
    nc, nk = cdim // tr, T // TK
    if rhs.ndim == 2:
        r_spec = pl.BlockSpec((TK, D), lambda g, c, k: (k, 0))
    else:
        r_spec = pl.BlockSpec((None, TK, D), lambda g, c, k: (g, k, 0))

    def body(l_ref, r_ref, o_ref, acc):
        k = pl.program_id(2)

        @pl.when(k == 0)
        def _():
            acc[...] = jnp.zeros_like(acc)

        r = r_ref[...] if scale is None else scale * r_ref[...]
        acc[...] += _tn(l_ref[...], r.astype(BF))

        @pl.when(k == nk - 1)
        def _():
            o_ref[...] = acc[...].astype(BF)

    return _call(
        body, name=name, grid=(ng, nc, nk), args=[lhs, rhs], comm=comm,
        in_specs=[pl.BlockSpec((None, TK, tr), lambda g, c, k: (g, k, c)), r_spec],
        out_shape=[jax.ShapeDtypeStruct((ng * cdim, D), BF)],
        out_specs=[pl.BlockSpec((tr, D), lambda g, c, k: (g * nc + c, 0))],
        scratch_shapes=[pltpu.VMEM((tr, D), F32)])


def _mix_out_bwd(dh2, ya, yb, z, a1, q, lng, lnb, wsq, comm=None):
    def body(dh_ref, ya_ref, yb_ref, ga_ref, gb_ref, a1_ref, q_ref, lng_ref, lnb_ref, wa_ref, wb_ref, wo_ref,
             dzg_ref, da1_ref, dq_ref, l_ref, r_ref, s_ref):
        @pl.when(pl.program_id(0) == 0)
        def _():
            s_ref[...] = jnp.zeros_like(s_ref)

        dhb = dh_ref[...].astype(BF)
        dm = _nt(dhb, wo_ref[...])
        ya, yb = ya_ref[...].astype(F32), yb_ref[...].astype(F32)
        sa, sb = _sig(ga_ref[...].astype(F32)), _sig(gb_ref[...].astype(F32))
        l_ref[0] = (sa * ya + sb * yb).astype(BF)
        l_ref[2] = q_ref[...]
        dzg_ref[0] = (dm * ya * (sa * (1.0 - sa))).astype(BF)
        dzg_ref[1] = (dm * yb * (sb * (1.0 - sb))).astype(BF)
        dya = (dm * sa).astype(BF)
        dyb = (dm * sb).astype(BF)
        r_ref[0] = dhb
        r_ref[1] = dya
        r_ref[2] = dyb
        dq_ref[...] = _nt(dyb, wb_ref[...]).astype(BF)
        da3 = _nt(dya, wa_ref[...])
        lng = lng_ref[...]
        xh, rs, a2, sg = _layernorm_silu(a1_ref[...].astype(F32), lng, lnb_ref[...])
        l_ref[1] = (a2 * sg).astype(BF)
        da2 = da3 * (sg * (1.0 + a2 * (1.0 - sg)))
        s_ref[0:1, :] += jnp.sum(da2 * xh, axis=0, keepdims=True)
        s_ref[1:2, :] += jnp.sum(da2, axis=0, keepdims=True)
        dxh = da2 * lng
        da1 = rs * (dxh - jnp.mean(dxh, axis=-1, keepdims=True) - xh * jnp.mean(dxh * xh, axis=-1, keepdims=True))
        da1_ref[...] = da1.astype(BF)
        s_ref[2:3, :] += jnp.sum(da1, axis=0, keepdims=True)

    row = lambda i: (i, 0)
    row3 = lambda i: (0, i, 0)
    vec = pl.BlockSpec((1, D), lambda i: (0, 0))
    return _call(
        body, name="mix_out_bwd", grid=(T // TM,), args=[dh2, ya, yb, z, z, a1, q, lng, lnb, wsq, wsq, wsq], comm=comm,
        in_specs=[pl.BlockSpec((TM, D), row), pl.BlockSpec((TM, D), row), pl.BlockSpec((TM, D), row),
                  pl.BlockSpec((None, TM, D), lambda i: (5, i, 0)), pl.BlockSpec((None, TM, D), lambda i: (6, i, 0)),
                  pl.BlockSpec((TM, D), row), pl.BlockSpec((TM, D), row), vec, vec] + _square_specs((0, 1, 2)),
        out_shape=[jax.ShapeDtypeStruct((2, T, D), BF), jax.ShapeDtypeStruct((T, D), BF),
                   jax.ShapeDtypeStruct((T, D), BF), jax.ShapeDtypeStruct((3, T, D), BF),
                   jax.ShapeDtypeStruct((3, T, D), BF), jax.ShapeDtypeStruct((8, D), F32)],
        out_specs=[pl.BlockSpec((2, TM, D), row3), pl.BlockSpec((TM, D), row), pl.BlockSpec((TM, D), row),
                   pl.BlockSpec((3, TM, D), row3), pl.BlockSpec((3, TM, D), row3), pl.BlockSpec((8, D), lambda i: (0, 0))])


def _mix_in_bwd(dz, dh2, h1, gm, win, comm=None):
    def body(dz_ref, dh_ref, h_ref, g_ref, *rest):
        w_any, (o_ref, s_ref, w_s, sem) = rest[:len(win)], rest[len(win):]
        _load_in_proj([(b, first) for b, (_, first) in zip(w_any, win)], w_s, sem)

        @pl.when(pl.program_id(0) == 0)
        def _():
            s_ref[...] = jnp.zeros_like(s_ref)

        du = _nn(dz_ref[0], w_s[0:D, :])
        for j in range(1, NG):
            du = du + _nn(dz_ref[j], w_s[j * D:(j + 1) * D, :])
        dx, dg = _rmsnorm_bwd(h_ref[...], g_ref[...], du)
        o_ref[...] = dh_ref[...] + dx
        s_ref[0:1, :] += dg

    row = lambda i: (i, 0)
    return _call(
        body, name="mix_in_bwd", grid=(T // TM,), args=[dz, dh2, h1, gm] + [b for b, _ in win], comm=comm,
        in_specs=[pl.BlockSpec((NG, TM, D), lambda i: (0, i, 0)), pl.BlockSpec((TM, D), row),
                  pl.BlockSpec((TM, D), row), pl.BlockSpec((1, D), lambda i: (0, 0))] + [ANY] * len(win),
        out_shape=[jax.ShapeDtypeStruct((T, D), F32), jax.ShapeDtypeStruct((8, D), F32)],
        out_specs=[pl.BlockSpec((TM, D), row), pl.BlockSpec((8, D), lambda i: (0, 0))],
        scratch_shapes=[pltpu.VMEM((NG * D, D), BF), pltpu.SemaphoreType.DMA((NDEV * len(win),))])


def _row_tile(n, want, mult):
    for t in range(min(want, n), 0, -1):
        if n % t == 0 and t % mult == 0:
            return t
    return n


def _sum_slots(recv, name):
    ns, rows, cols = recv.shape
    tr = _row_tile(rows, 1024, 16)

    def body(r_ref, o_ref):
        s = r_ref[0].astype(F32)
        for k in range(1, ns):
            s = s + r_ref[k].astype(F32)
        o_ref[...] = s

    return _call(
        body, name=name, grid=(rows // tr,), args=[recv],
        in_specs=[pl.BlockSpec((ns, tr, cols), lambda i: (0, i, 0))],
        out_shape=[jax.ShapeDtypeStruct((rows, cols), F32)],
        out_specs=[pl.BlockSpec((tr, cols), lambda i: (i, 0))])[0]


def _pack_small(s_ffn1, s_in, s_mix, s_ffn2, s_final, dwa, dwb):
    def body(f1, mi, mo, f2, fl, wa_ref, wb_ref, v_ref, k_ref):
        for dst, (ref, row) in enumerate(((f1, 0), (mi, 0), (mo, 0), (mo, 1), (mo, 2), (f2, 0), (fl, 0), (fl, 1))):
            v_ref[dst:dst + 1, :] = ref[row:row + 1, :]
        for k in range(NDEV):
            k_ref[k, 0:32, :] = wa_ref[:, k * LANE:(k + 1) * LANE]
            k_ref[k, 32:40, :] = wb_ref[:, k * LANE:(k + 1) * LANE]

    return pl.pallas_call(
        body, name="pack_small",
        out_shape=(jax.ShapeDtypeStruct((8, D), F32), jax.ShapeDtypeStruct((NDEV, 40, LANE), F32)),
    )(s_ffn1, s_in, s_mix, s_ffn2, s_final, dwa, dwb)


def _sum_small(vecs, convs):
    def body(v_ref, k_ref, vs_ref, ks_ref, l_ref):
        s, c = v_ref[0], k_ref[0]
        for k in range(1, NDEV):
            s = s + v_ref[k]
            c = c + k_ref[k]
        vs_ref[...] = s
        ks_ref[...] = c
        l_ref[...] = jnp.broadcast_to(jnp.sum(s[7:8, :], axis=-1, keepdims=True), (8, LANE))

    return pl.pallas_call(
        body, name="sum_small",
        out_shape=(jax.ShapeDtypeStruct((8, D), F32), jax.ShapeDtypeStruct((40, LANE), F32),
                   jax.ShapeDtypeStruct((8, LANE), F32)),
    )(vecs, convs)


def _adam(gs, ws, ms, vs, name, comm=None):
    n = len(gs)
    rows, cols = ws[0].shape
    tr = _row_tile(rows, 256, 16)
    c1 = 1.0 - ADAM_B1 ** ADAM_STEP
    c2 = 1.0 - ADAM_B2 ** ADAM_STEP
    summed = [isinstance(g, tuple) for g in gs]

    def body(*refs):
        for i in range(n):
            g_in, w, m, v = refs[4 * i], refs[4 * i + 1][...], refs[4 * i + 2][...], refs[4 * i + 3][...]
            g_ref, d_ref, m_ref, v_ref = refs[4 * n + 4 * i: 4 * n + 4 * i + 4]
            if summed[i]:
                g = g_in[0].astype(F32)
                for k in range(1, g_in.shape[0]):
                    g = g + g_in[k].astype(F32)
            else:
                g = g_in[...]
            g_ref[...] = g
            m2 = ADAM_B1 * m + (1.0 - ADAM_B1) * g
            v2 = ADAM_B2 * v + (1.0 - ADAM_B2) * (g * g)
            d_ref[...] = -ADAM_LR * ((m2 / c1) / (jnp.sqrt(v2 / c2) + ADAM_EPS) + ADAM_WD * w)
            m_ref[...] = m2
            v_ref[...] = v2

    spec = pl.BlockSpec((tr, cols), lambda i: (i, 0))
    args, in_specs = [], []
    for i in range(n):
        if summed[i]:
            slots, first = gs[i]
            args.append(slots)
            in_specs.append(pl.BlockSpec((slots.shape[0], tr, cols), lambda i, b=first // tr: (0, b + i, 0)))
        else:
            args.append(gs[i])
            in_specs.append(spec)
        args += [ws[i], ms[i], vs[i]]
        in_specs += [spec] * 3
    outs = _call(body, name=name, grid=(rows // tr,), args=args, comm=comm, in_specs=in_specs,
                 out_shape=[jax.ShapeDtypeStruct((rows, cols), F32)] * (4 * n), out_specs=[spec] * (4 * n))
    return [tuple(outs[4 * i: 4 * i + 4]) for i in range(n)], outs[4 * n:]


def kernel(x, ffn1_norm, ffn1_w_gate, ffn1_w_up, ffn1_w_down, mix_norm, w_in, a_dw_w, a_dw_b, a_ln_g, a_ln_b, a_w_out, b_conv_w, b_w_out, w_o, ffn2_norm, ffn2_w_gate, ffn2_w_up, ffn2_w_down, final_norm, loss_target, m_ffn1_norm, m_ffn1_w_gate, m_ffn1_w_up, m_ffn1_w_down, m_mix_norm, m_w_in, m_a_dw_w, m_a_dw_b, m_a_ln_g, m_a_ln_b, m_a_w_out, m_b_conv_w, m_b_w_out, m_w_o, m_ffn2_norm, m_ffn2_w_gate, m_ffn2_w_up, m_ffn2_w_down, m_final_norm, v_ffn1_norm, v_ffn1_w_gate, v_ffn1_w_up, v_ffn1_w_down, v_mix_norm, v_w_in, v_a_dw_w, v_a_dw_b, v_a_ln_g, v_a_ln_b, v_a_w_out, v_b_conv_w, v_b_w_out, v_w_o, v_ffn2_norm, v_ffn2_w_gate, v_ffn2_w_up, v_ffn2_w_down, v_final_norm):
    names = ("ffn1_norm", "ffn1_w_gate", "ffn1_w_up", "ffn1_w_down", "mix_norm", "w_in", "a_dw_w", "a_dw_b",
             "a_ln_g", "a_ln_b", "a_w_out", "b_conv_w", "b_w_out", "w_o", "ffn2_norm", "ffn2_w_gate", "ffn2_w_up",
             "ffn2_w_down", "final_norm")
    w = dict(ffn1_norm=ffn1_norm, ffn1_w_gate=ffn1_w_gate, ffn1_w_up=ffn1_w_up, ffn1_w_down=ffn1_w_down,
             mix_norm=mix_norm, w_in=w_in, a_dw_w=a_dw_w, a_dw_b=a_dw_b, a_ln_g=a_ln_g, a_ln_b=a_ln_b,
             a_w_out=a_w_out, b_conv_w=b_conv_w, b_w_out=b_w_out, w_o=w_o, ffn2_norm=ffn2_norm,
             ffn2_w_gate=ffn2_w_gate, ffn2_w_up=ffn2_w_up, ffn2_w_down=ffn2_w_down, final_norm=final_norm)
    m = dict(ffn1_norm=m_ffn1_norm, ffn1_w_gate=m_ffn1_w_gate, ffn1_w_up=m_ffn1_w_up, ffn1_w_down=m_ffn1_w_down,
             mix_norm=m_mix_norm, w_in=m_w_in, a_dw_w=m_a_dw_w, a_dw_b=m_a_dw_b, a_ln_g=m_a_ln_g, a_ln_b=m_a_ln_b,
             a_w_out=m_a_w_out, b_conv_w=m_b_conv_w, b_w_out=m_b_w_out, w_o=m_w_o, ffn2_norm=m_ffn2_norm,
             ffn2_w_gate=m_ffn2_w_gate, ffn2_w_up=m_ffn2_w_up, ffn2_w_down=m_ffn2_w_down, final_norm=m_final_norm)
    v = dict(ffn1_norm=v_ffn1_norm, ffn1_w_gate=v_ffn1_w_gate, ffn1_w_up=v_ffn1_w_up, ffn1_w_down=v_ffn1_w_down,
             mix_norm=v_mix_norm, w_in=v_w_in, a_dw_w=v_a_dw_w, a_dw_b=v_a_dw_b, a_ln_g=v_a_ln_g, a_ln_b=v_a_ln_b,
             a_w_out=v_a_w_out, b_conv_w=v_b_conv_w, b_w_out=v_b_w_out, w_o=v_w_o, ffn2_norm=v_ffn2_norm,
             ffn2_w_gate=v_ffn2_w_gate, ffn2_w_up=v_ffn2_w_up, ffn2_w_down=v_ffn2_w_down, final_norm=v_final_norm)
    flat = _pack_weights(dict(wg1=ffn1_w_gate[0].T, wu1=ffn1_w_up[0].T, wd1=ffn1_w_down[0], wg2=ffn2_w_gate[0].T,
                              wu2=ffn2_w_up[0].T, wd2=ffn2_w_down[0], win=w_in[0], wa=a_w_out[0], wb=b_w_out[0],
                              wo=w_o[0]))
    cw_shard = jnp.concatenate([a_dw_w[0], jnp.zeros((1, LANE), F32), b_conv_w[0], jnp.zeros((5, LANE), F32)], axis=0)

    x2, tgt = x[0], loss_target[0]
    st_a, st_b, st_b2 = ("wg1", "wu1"), ("wd1", "win/0/2"), ("win/1/2",)
    st_c, st_d, st_e = ("wa", "wb", "wo", "wg2"), ("wu2",), ("wd2",)

    buf_a, cw = _run_comm(_join(_ag_comm(st_a, flat), _direct_comm(cw_shard, False)), "ag_ffn1")
    n1, gg1, uu1, act1, buf_b = _ffn_gate_up(x2, ffn1_norm, (buf_a, buf_a), (0, F), "ffn1_gate_up", _ag_comm(st_b, flat))
    h1, buf_b2 = _ffn_down(x2, act1, buf_b, 0, "ffn1_down", _ag_comm(st_b2, flat))
    win = ((buf_b, F), (buf_b2, 0))
    u, z, buf_c = _mix_in(h1, mix_norm, win, _ag_comm(st_c, flat))
    dft = _dft_constants()
    cw = jnp.transpose(cw, (1, 0, 2)).reshape(40, D)
    a1, q, buf_d = _conv_fwd_dft(z, cw, a_dw_b, dft, _ag_comm(st_d, flat))
    h2, ya, yb, buf_e = _mix_out(a1, q, z, h1, a_ln_g, a_ln_b, buf_c, _ag_comm(st_e, flat))
    ffn2_bufs, ffn2_offs = (buf_c, buf_d, buf_e), (3 * D, 0, 0)
    dh3, s_final, n2, gg2, uu2, act2 = _ffn_fwd(h2, ffn2_norm, ffn2_bufs, ffn2_offs, "ffn2_fwd",
                                          final=(final_norm.reshape(1, D), tgt))

    tr_f = F // 2 if (F // 2) % LANE == 0 else F
    def pair(stage, src):
        return _rs_pair_comm(stage, src)

    def chip(stage, src, pair_buf, tag):
        return _rs_chip_comm(_pair_add(stage, src, pair_buf, "pair_add_" + tag))

    (dgu2,) = _ffn_bwd_hidden(dh3, gg2, uu2, buf_e, 0, "ffn2_bwd_h")
    (gu2,) = _tn_matmul(dgu2, n2, tr_f, "dw_gu2")
    s2a, src2a = ("wg2", "wu2"), dict(wg2=(gu2, 0), wu2=(gu2, F))
    gd2, pair2a = _tn_matmul(act2, dh3, tr_f, "dw_d2", pair(s2a, src2a), scale=0.5)
    s2b, src2b = ("wd2",), dict(wd2=(gd2, 0))
    dh2, s_ffn2, pair2b = _ffn_bwd_input(dgu2, dh3, h2, ffn2_norm, (buf_c, buf_d), (3 * D, 0), "ffn2_bwd_x",
                                         pair(s2b, src2b))
    dzg, da1, dq, lsq, rsq, s_mix, recv2b = _mix_out_bwd(dh2, ya, yb, z, a1, q, a_ln_g, a_ln_b, buf_c,
                                                          chip(s2b, src2b, pair2b, "2b"))
    (gsq,) = _tn_matmul(lsq, rsq, D, "dw_square")
    ssq, srcsq = ("wa", "wb", "wo"), dict(wa=(gsq, D), wb=(gsq, 2 * D), wo=(gsq, 0))
    dz, dwa, dwb, recv2a, pairsq = _conv_bwd_dft(z, da1, dq, dzg, cw, dft,
                                                 _join(chip(s2a, src2a, pair2a, "2a"), pair(ssq, srcsq)))
    gin, recvsq = _tn_matmul(dz, u, D, "dw_in", chip(ssq, srcsq, pairsq, "sq"))
    sin_a, sin_b, srcin = ("win/0/2",), ("win/1/2",), {"win/0/2": (gin, 0), "win/1/2": (gin, 0)}
    dh1, s_in, pairin_a, pairin_b = _mix_in_bwd(dz, dh2, h1, mix_norm, win,
                                                _join(pair(sin_a, srcin), pair(sin_b, srcin)))
    dgu1, recvin_a = _ffn_bwd_hidden(dh1, gg1, uu1, buf_b, 0, "ffn1_bwd_h",
                                           chip(sin_a, srcin, pairin_a, "in_a"))
    gu1, recvin_b = _tn_matmul(dgu1, n1, tr_f, "dw_gu1", chip(sin_b, srcin, pairin_b, "in_b"))
    s1a, src1a = ("wg1", "wu1"), dict(wg1=(gu1, 0), wu1=(gu1, F))
    gd1, pair1a = _tn_matmul(act1, dh1, tr_f, "dw_d1", pair(s1a, src1a), scale=0.5)
    s1b, src1b = ("wd1",), dict(wd1=(gd1, 0))
    dx, s_ffn1, recv1a, pair1b = _ffn_bwd_input(dgu1, dh1, x2, ffn1_norm, (buf_a, buf_a), (0, F), "ffn1_bwd_x",
                                                _join(chip(s1a, src1a, pair1a, "1a"), pair(s1b, src1b)))
    win_sum = jnp.concatenate([_sum_slots(recvin_a, "sum_in_a"), _sum_slots(recvin_b, "sum_in_b")], axis=0)

    vec8, convk = _pack_small(s_ffn1, s_in, s_mix, s_ffn2, s_final, dwa, dwb)
    vec_all, conv_all = _run_comm(_join(_direct_comm(vec8, False), _direct_comm(convk, True)), "xchg_small")
    vec_sum, conv_sum, loss_blk = _sum_small(vec_all, conv_all)
    loss = loss_blk[0, 0]

    fs = F // NDEV
    g = dict(ffn1_w_gate=(recv1a, 0), ffn1_w_up=(recv1a, fs), ffn2_w_gate=(recv2a, 0), ffn2_w_up=(recv2a, fs),
             ffn2_w_down=(recv2b, 0), a_w_out=(recvsq, 0), b_w_out=(recvsq, D // NDEV), w_o=(recvsq, 2 * (D // NDEV)),
             w_in=win_sum.T, ffn1_norm=vec_sum[0:1], mix_norm=vec_sum[1:2], a_ln_g=vec_sum[2:3], a_ln_b=vec_sum[3:4],
             a_dw_b=vec_sum[4:5], ffn2_norm=vec_sum[5:6], final_norm=vec_sum[6:7],
             a_dw_w=conv_sum[0:KA], b_conv_w=conv_sum[32:32 + KB])
    grad, upd = {}, {}

    def run(group, name, as2d=lambda a: a[0], back=lambda a, n: a.reshape(w[n].shape), comm=None):
        res, extra = _adam([g[n] for n in group], [as2d(w[n]) for n in group], [as2d(m[n]) for n in group],
                           [as2d(v[n]) for n in group], name, comm)
        for n, r in zip(group, res):
            grad[n], upd[n] = back(r[0], n), tuple(back(a, n) for a in r[1:])
        return extra

    rows_first = dict(as2d=lambda a: a[0].T, back=lambda a, n: a.T[None])
    (recv1b,) = run(("ffn1_w_gate", "ffn1_w_up", "ffn2_w_gate", "ffn2_w_up"), "adam_gate_up",
                    comm=chip(s1b, src1b, pair1b, "1b"), **rows_first)
    g["ffn1_w_down"] = (recv1b, 0)
    run(("ffn1_w_down", "ffn2_w_down"), "adam_down")
    run(("w_in",), "adam_in")
    run(("a_w_out", "b_w_out", "w_o"), "adam_square")
    run(("a_dw_w",), "adam_dw")
    run(("b_conv_w",), "adam_conv")
    run(("ffn1_norm", "mix_norm", "a_dw_b", "a_ln_g", "a_ln_b", "ffn2_norm", "final_norm"), "adam_vec",
        as2d=lambda a: a.reshape(1, D))

    return (loss, dx.reshape(x.shape), *[grad[n] for n in names], *[upd[n][0] for n in names],
            *[upd[n][1] for n in names], *[upd[n][2] for n in names])
```

```python
import jax
import jax.numpy as jnp
from jax import lax
from jax.experimental import pallas as pl
from jax.experimental.pallas import tpu as pltpu

T = 4096
D = 1024
F = 2816
NG = 7
NDEV = 8
NCHIP = 4
KA, KB = 31, 3
EPS = 1e-6
ADAM_LR, ADAM_B1, ADAM_B2, ADAM_EPS, ADAM_WD, ADAM_STEP = 0.001, 0.9, 0.999, 1e-08, 0.01, 10

TM = 512
FC = 256
TB = 1024
NB = 256
HB = NB // 2
CW = 256
CHB = 64
LANE = 128
TK = 2048
VMEM_LIMIT = 56 * 1024 * 1024

BF = jnp.bfloat16
F32 = jnp.float32
MESH = pl.DeviceIdType.MESH
ANY = pl.BlockSpec(memory_space=pl.ANY)

ORDER = ("wg1", "wu1", "wd1", "wg2", "wu2", "wd2", "win", "wa", "wb", "wo")


class _Layout:
    def __init__(self):
        fs, dis, ds = F // NDEV, NG * D // NDEV, D // NDEV
        self.rows = dict(wg1=fs, wu1=fs, wd1=fs, wg2=fs, wu2=fs, wd2=fs, win=dis, wa=ds, wb=ds, wo=ds)
        self.fl, off = {}, 0
        for n in ORDER:
            self.fl[n] = off
            off += self.rows[n]
        self.RT = off


class _Stage:
    def __init__(self, names):
        lay = _Layout()
        self.names = names
        self.rows, self.full, self.sub, self.fl = {}, {}, {}, {}
        for n in names:
            base, i, k = (n.split("/") + ["0", "1"])[:3]
            self.full[n] = lay.rows[base]
            self.rows[n] = lay.rows[base] // int(k)
            self.sub[n] = int(i) * self.rows[n]
            self.fl[n] = lay.fl[base] + self.sub[n]
        self.off, self.wc, o, w = {}, {}, 0, 0
        for n in names:
            self.off[n], self.wc[n] = o, w
            o += self.rows[n]
            w += NDEV * self.rows[n]
        self.R, self.W = o, w

    def grad_row(self, n, first, dev_lin):
        return first + dev_lin * self.full[n] + self.sub[n]


def _nt(a, b):
    return lax.dot_general(a, b, (((1,), (1,)), ((), ())), preferred_element_type=F32)


def _nn(a, b):
    return lax.dot_general(a, b, (((1,), (0,)), ((), ())), preferred_element_type=F32)


def _tn(a, b):
    return lax.dot_general(a, b, (((0,), (0,)), ((), ())), preferred_element_type=F32)


def _sig(x):
    return 1.0 / (1.0 + jnp.exp(-x))


def _position():
    return lax.axis_index("x"), lax.axis_index("y"), lax.axis_index("c")


def _peer(pos, j):
    x, y, c = pos
    return (1 - x if j & 4 else x, 1 - y if j & 2 else y, 1 - c if j & 1 else c)


def _lin(pos):
    return 4 * pos[0] + 2 * pos[1] + pos[2]


def _chip(pos):
    return 2 * pos[0] + pos[1]


class _Comm:
    def __init__(self, inputs, out_shapes, scratch, start, finish, middle=None):
        self.inputs, self.out_shapes, self.scratch = inputs, out_shapes, scratch
        self.start, self.finish, self.middle = start, finish, middle


def _call(body, *, name, grid, args, in_specs, out_shape, out_specs, scratch_shapes=(), comm=None,
          num_scalar_prefetch=0):
    in_specs, out_shape, out_specs, scratch_shapes = list(in_specs), list(out_shape), list(out_specs), list(scratch_shapes)
    n_in, n_out, n_scr = len(in_specs), len(out_shape), len(scratch_shapes)
    sp = num_scalar_prefetch
    if comm is None:
        kernel_fn = lambda *refs: body(*refs)
        c_in = c_out = c_scr = 0
    else:
        c_in, c_out, c_scr = len(comm.inputs), len(comm.out_shapes), len(comm.scratch)

        def kernel_fn(*refs):
            pre, refs = refs[:sp], refs[sp:]
            ins, cins = refs[:n_in], refs[n_in:n_in + c_in]
            o0 = n_in + c_in
            outs, couts = refs[o0:o0 + n_out], refs[o0 + n_out:o0 + n_out + c_out]
            s0 = o0 + n_out + c_out
            scr, cscr = refs[s0:s0 + n_scr], refs[s0 + n_scr:]
            step, steps = pl.program_id(0), grid[0]
            for a in range(1, len(grid)):
                step, steps = step * grid[a] + pl.program_id(a), steps * grid[a]
            first, last = step == 0, step == steps - 1

            @pl.when(first)
            def _():
                comm.start(cins, couts, cscr)

            if comm.middle is not None:
                @pl.when(step == (steps // 2 if steps > 2 else steps - 1))
                def _():
                    comm.middle(cins, couts, cscr)

            body(*pre, *ins, *outs, *scr)

            @pl.when(last)
            def _():
                comm.finish(cins, couts, cscr)

        args = list(args) + list(comm.inputs)
        in_specs += [ANY] * c_in
        out_shape += list(comm.out_shapes)
        out_specs += [ANY] * c_out
        scratch_shapes += list(comm.scratch)
    params = pltpu.CompilerParams(dimension_semantics=("arbitrary",) * len(grid), vmem_limit_bytes=VMEM_LIMIT)
    if sp:
        grid_spec = pltpu.PrefetchScalarGridSpec(num_scalar_prefetch=sp, grid=grid, in_specs=in_specs,
                                                 out_specs=out_specs, scratch_shapes=scratch_shapes)
        return pl.pallas_call(kernel_fn, name=name, grid_spec=grid_spec, out_shape=out_shape,
                              compiler_params=params)(*args)
    return pl.pallas_call(kernel_fn, name=name, grid=grid, in_specs=in_specs, out_shape=out_shape, out_specs=out_specs,
                          scratch_shapes=scratch_shapes, compiler_params=params)(*args)


def _join(a, b):
    na = (len(a.inputs), len(a.out_shapes), len(a.scratch))

    def split(refs):
        return ([r[:n] for r, n in zip(refs, na)], [r[n:] for r, n in zip(refs, na)])

    def start(*refs):
        ra, rb = split(refs)
        a.start(*ra)
        b.start(*rb)

    def finish(*refs):
        ra, rb = split(refs)
        a.finish(*ra)
        b.finish(*rb)

    def middle(*refs):
        for stage, r in zip((a, b), split(refs)):
            if stage.middle is not None:
                stage.middle(*r)

    return _Comm(list(a.inputs) + list(b.inputs), list(a.out_shapes) + list(b.out_shapes),
                 list(a.scratch) + list(b.scratch), start, finish,
                 middle if (a.middle is not None or b.middle is not None) else None)


def _run_comm(comm, name):
    def body(*refs):
        c_in, c_out = len(comm.inputs), len(comm.out_shapes)
        parts = (refs[:c_in], refs[c_in:c_in + c_out], refs[c_in + c_out:])
        comm.start(*parts)
        if comm.middle is not None:
            comm.middle(*parts)
        comm.finish(*parts)

    return pl.pallas_call(
        body, name=name, out_shape=list(comm.out_shapes), in_specs=[ANY] * len(comm.inputs),
        out_specs=[ANY] * len(comm.out_shapes), scratch_shapes=list(comm.scratch))(*comm.inputs)


def _ag_comm(names, flat):
    st = _Stage(names)

    def ring(me):
        x, y, c = me
        diagonal = x == y
        up = (jnp.where(diagonal, x, 1 - x), jnp.where(diagonal, 1 - y, y), c)
        down = (jnp.where(diagonal, 1 - x, x), jnp.where(diagonal, y, 1 - y), c)
        low = c == 0
        passed = tuple(jnp.where(low, d, u) for d, u in zip(down, up))
        target = tuple(jnp.where(low, u, d) for d, u in zip(down, up))
        return up, down, (1 - x, 1 - y, c), passed, target

    def parts(refs):
        (flat_ref,), (out_ref,), (send_sems, recv_sems, local_sem) = refs
        me = _position()

        def region(name, dev):
            r = st.rows[name]
            return out_ref.at[pl.ds(st.wc[name] + _lin(dev) * r, r), :]

        def own(name):
            return flat_ref.at[pl.ds(st.fl[name], st.rows[name]), :]

        def copies(k, dev, to, from_flat):
            return [pltpu.make_async_remote_copy(
                src_ref=own(n) if from_flat else region(n, dev), dst_ref=region(n, dev), send_sem=send_sems.at[k],
                recv_sem=recv_sems.at[k], device_id=to, device_id_type=MESH) for n in names]

        def whole(k):
            return pltpu.make_async_remote_copy(
                src_ref=flat_ref.at[pl.ds(0, st.R), :], dst_ref=out_ref.at[pl.ds(0, st.R), :],
                send_sem=send_sems.at[k], recv_sem=recv_sems.at[k], device_id=me, device_id_type=MESH)

        return me, region, own, copies, whole, flat_ref, out_ref, local_sem

    def start(*refs):
        me, region, own, copies, _, _, _, local_sem = parts(refs)
        for n in names:
            pltpu.make_async_copy(own(n), region(n, me), local_sem).start()
        up, down, _, _, _ = ring(me)
        for k, to in ((1, up), (2, down), (0, _peer(me, 1))):
            for cp in copies(k, me, to, True):
                cp.start()

    def middle(*refs):
        me, _, _, copies, whole, _, _, _ = parts(refs)
        up, down, _, passed, target = ring(me)
        sib = _peer(me, 1)
        whole(1).wait_recv()
        whole(2).wait_recv()
        for k, dev, to in ((3, passed, target), (4, down, sib), (5, up, sib)):
            for cp in copies(k, dev, to, False):
                cp.start()

    def finish(*refs):
        me, _, _, copies, whole, flat_ref, out_ref, local_sem = parts(refs)
        _, _, across, _, _ = ring(me)
        whole(3).wait_recv()
        for cp in copies(6, across, _peer(me, 1), False):
            cp.start()
        whole(0).wait_recv()
        for j in range(3):
            whole(4 + j).wait_recv()
        for k in range(7):
            whole(k).wait_send()
        pltpu.make_async_copy(flat_ref.at[pl.ds(0, st.R), :], out_ref.at[pl.ds(0, st.R), :], local_sem).wait()

    return _Comm([flat], [jax.ShapeDtypeStruct((st.W, D), BF)],
                 [pltpu.SemaphoreType.DMA((7,)), pltpu.SemaphoreType.DMA((7,)), pltpu.SemaphoreType.DMA],
                 start, finish, middle)


def _rs_pair_comm(names, src):
    st = _Stage(names)
    arrays = []
    for n in names:
        if not any(src[n][0] is a for a in arrays):
            arrays.append(src[n][0])
    idx = {n: [i for i, a in enumerate(arrays) if a is src[n][0]][0] for n in names}

    def slot_wait(refs):
        recv = refs[1][0]
        send_sem, recv_sem = refs[2]
        return pltpu.make_async_remote_copy(src_ref=recv, dst_ref=recv, send_sem=send_sem, recv_sem=recv_sem,
                                            device_id=_position(), device_id_type=MESH)

    def start(*refs):
        ins, (recv,), (send_sem, recv_sem) = refs
        me = _position()
        sib = _peer(me, 1)
        for q in range(NCHIP):
            dev = (q // 2, q % 2, sib[2])
            for n in names:
                r = st.rows[n]
                pltpu.make_async_remote_copy(
                    src_ref=ins[idx[n]].at[pl.ds(st.grad_row(n, src[n][1], _lin(dev)), r), :],
                    dst_ref=recv.at[q, pl.ds(st.off[n], r), :], send_sem=send_sem, recv_sem=recv_sem,
                    device_id=sib, device_id_type=MESH).start()

    def finish(*refs):
        w = slot_wait(refs)
        w.wait_recv()
        w.wait_send()

    return _Comm(arrays, [jax.ShapeDtypeStruct((NCHIP, st.R, D), BF)],
                 [pltpu.SemaphoreType.DMA, pltpu.SemaphoreType.DMA], start, finish)


def _pair_add(names, src, recv, name):
    st = _Stage(names)
    c_arr = jnp.reshape(lax.axis_index("c"), (1,)).astype(jnp.int32)

    def body(c_ref, *refs):
        r_ref, o_ref = refs[len(names)], refs[len(names) + 1]
        for a_ref, n in zip(refs, names):
            rows = slice(st.off[n], st.off[n] + st.rows[n])
            o_ref[rows, :] = (a_ref[...].astype(F32) + r_ref[rows, :].astype(F32)).astype(BF)

    def shard_spec(n):
        r = st.rows[n]
        base, step = st.grad_row(n, src[n][1], 0) // r, st.full[n] // r
        return pl.BlockSpec((r, D), lambda q, c_ref: (base + step * (2 * q + c_ref[0]), 0))

    slot = pl.BlockSpec((None, st.R, D), lambda q, c_ref: (q, 0, 0))
    return _call(body, name=name, grid=(NCHIP,), args=[c_arr] + [src[n][0] for n in names] + [recv],
                 in_specs=[shard_spec(n) for n in names] + [slot],
                 out_shape=[jax.ShapeDtypeStruct((NCHIP, st.R, D), BF)], out_specs=[slot], num_scalar_prefetch=1)[0]


def _rs_chip_comm(part):
    def copies(refs):
        (p_ref,), (recv,), (send_sems, recv_sems, local_sem) = refs
        me = _position()
        mine = pltpu.make_async_copy(p_ref.at[_chip(me)], recv.at[_chip(me)], local_sem)
        out = []
        for j, bits in enumerate((4, 2, 6)):
            to = _peer(me, bits)
            out.append(pltpu.make_async_remote_copy(
                src_ref=p_ref.at[_chip(to)], dst_ref=recv.at[_chip(me)], send_sem=send_sems.at[j],
                recv_sem=recv_sems.at[j], device_id=to, device_id_type=MESH))
        return mine, out

    def start(*refs):
        mine, out = copies(refs)
        mine.start()
        for cp in out:
            cp.start()

    def finish(*refs):
        mine, out = copies(refs)
        for cp in out:
            cp.wait_recv()
        for cp in out:
            cp.wait_send()
        mine.wait()

    return _Comm([part], [jax.ShapeDtypeStruct(part.shape, BF)],
                 [pltpu.SemaphoreType.DMA((3,)), pltpu.SemaphoreType.DMA((3,)), pltpu.SemaphoreType.DMA],
                 start, finish)


def _direct_comm(x, scatter):
    def copies(refs):
        (x_ref,), (out_ref,), (send_sems, recv_sems, local_sem) = refs
        me = _position()

        def piece(dev):
            return x_ref.at[_lin(dev)] if scatter else x_ref

        mine = pltpu.make_async_copy(piece(me), out_ref.at[_lin(me)], local_sem)
        return mine, [pltpu.make_async_remote_copy(
            src_ref=piece(_peer(me, j)), dst_ref=out_ref.at[_lin(me)], send_sem=send_sems.at[j - 1],
            recv_sem=recv_sems.at[j - 1], device_id=_peer(me, j), device_id_type=MESH) for j in range(1, NDEV)]

    def start(*refs):
        mine, cps = copies(refs)
        mine.start()
        for cp in cps:
            cp.start()

    def finish(*refs):
        mine, cps = copies(refs)
        for cp in cps:
            cp.wait_recv()
        for cp in cps:
            cp.wait_send()
        mine.wait()

    shape = x.shape if scatter else (NDEV,) + x.shape
    return _Comm([x], [jax.ShapeDtypeStruct(shape, x.dtype)],
                 [pltpu.SemaphoreType.DMA((7,)), pltpu.SemaphoreType.DMA((7,)), pltpu.SemaphoreType.DMA],
                 start, finish)


def _pack_weights(shards):
    lay = _Layout()

    def body(*refs):
        o_ref = refs[-1]
        for ref, n in zip(refs, ORDER):
            x = ref[...].T if n == "win" else ref[...]
            o_ref[lay.fl[n]:lay.fl[n] + lay.rows[n], :] = x.astype(BF)

    return pl.pallas_call(
        body, name="pack_weights", out_shape=jax.ShapeDtypeStruct((lay.RT, D), BF),
        compiler_params=pltpu.CompilerParams(vmem_limit_bytes=VMEM_LIMIT))(*[shards[n] for n in ORDER])


WCH = 8


def _weight_sems(n):
    return pltpu.SemaphoreType.DMA((n * WCH,))


def _load_ffn_weights(srcs, offs, scratch, sem):
    @pl.when(pl.program_id(0) == 0)
    def _():
        cps = []
        for i, (s, off, dst) in enumerate(zip(srcs, offs, scratch)):
            rows = dst.shape[0] // WCH
            cps += [pltpu.make_async_copy(s.at[pl.ds(off + j * rows, rows), :], dst.at[pl.ds(j * rows, rows), :],
                                          sem.at[i * WCH + j]) for j in range(WCH)]
        for cp in cps:
            cp.start()
        for cp in cps:
            cp.wait()


def _final_loss_tile(xf, g, tgt, s_ref):
    r = lax.rsqrt(jnp.mean(xf * xf, axis=-1, keepdims=True) + EPS)
    xr = xf * r
    e = xr * g - tgt
    s_ref[1:2, :] += jnp.sum(e * e, axis=0, keepdims=True) * (0.5 / D)
    dy = e * (1.0 / D)
    s_ref[0:1, :] += jnp.sum(dy * xr, axis=0, keepdims=True)
    gdy = dy * g
    return r * gdy - xr * (r * jnp.mean(gdy * xr, axis=-1, keepdims=True))


def _ffn_fwd(x, g, wbufs, offs, name, comm=None, final=None):
    nf = F // FC

    def body(x_ref, g_ref, b0, b1, b2, *rest):
        if final is None:
            h_ref, n_ref, gg_ref, uu_ref, a_ref, wg_s, wu_s, wd_s, sem = rest
        else:
            gf_ref, t_ref, dh_ref, s_ref, n_ref, gg_ref, uu_ref, a_ref, wg_s, wu_s, wd_s, sem = rest

            @pl.when(pl.program_id(0) == 0)
            def _():
                s_ref[...] = jnp.zeros_like(s_ref)

        _load_ffn_weights((b0, b1, b2), offs, (wg_s, wu_s, wd_s), sem)
        xf = x_ref[...]
        r = lax.rsqrt(jnp.mean(xf * xf, axis=-1, keepdims=True) + EPS)
        nb = (xf * r * g_ref[...]).astype(BF)
        n_ref[...] = nb
        acc = jnp.zeros((TM, D), F32)
        for c in range(nf):
            sl = slice(c * FC, (c + 1) * FC)
            gb = _nt(nb, wg_s[sl, :]).astype(BF)
            ub = _nt(nb, wu_s[sl, :]).astype(BF)
            gg_ref[:, sl] = gb
            uu_ref[:, sl] = ub
            a = (gb * _sig(gb)) * ub
            a_ref[0, :, sl] = a
            acc = acc + _nn(a, wd_s[sl, :])
        h = xf + 0.5 * acc
        if final is None:
            h_ref[...] = h
        else:
            dh_ref[...] = _final_loss_tile(h, gf_ref[...], t_ref[...], s_ref)

    row = lambda i: (i, 0)
    vec = pl.BlockSpec((1, D), lambda i: (0, 0))
    tile = pl.BlockSpec((TM, D), row)
    saved_shapes = [jax.ShapeDtypeStruct((T, D), BF), jax.ShapeDtypeStruct((T, F), BF), jax.ShapeDtypeStruct((T, F), BF),
                    jax.ShapeDtypeStruct((1, T, F), BF)]
    saved_specs = [tile, pl.BlockSpec((TM, F), row), pl.BlockSpec((TM, F), row),
                   pl.BlockSpec((1, TM, F), lambda i: (0, i, 0))]
    if final is None:
        extra_args, extra_specs = [], []
        head_shapes, head_specs = [jax.ShapeDtypeStruct((T, D), F32)], [tile]
    else:
        extra_args, extra_specs = list(final), [vec, tile]
        head_shapes = [jax.ShapeDtypeStruct((T, D), F32), jax.ShapeDtypeStruct((8, D), F32)]
        head_specs = [tile, pl.BlockSpec((8, D), lambda i: (0, 0))]
    return _call(
        body, name=name, grid=(T // TM,), args=[x, g, *wbufs, *extra_args], comm=comm,
        in_specs=[tile, vec, ANY, ANY, ANY] + extra_specs,
        out_shape=head_shapes + saved_shapes, out_specs=head_specs + saved_specs,
        scratch_shapes=[pltpu.VMEM((F, D), BF)] * 3 + [_weight_sems(3)])


def _ffn_gate_up(x, g, wbufs, offs, name, comm=None):
    nf = F // FC

    def body(x_ref, g_ref, b0, b1, n_ref, gg_ref, uu_ref, a_ref, wg_s, wu_s, sem):
        _load_ffn_weights((b0, b1), offs, (wg_s, wu_s), sem)
        xf = x_ref[...]
        r = lax.rsqrt(jnp.mean(xf * xf, axis=-1, keepdims=True) + EPS)
        nb = (xf * r * g_ref[...]).astype(BF)
        n_ref[...] = nb
        for c in range(nf):
            sl = slice(c * FC, (c + 1) * FC)
            gb = _nt(nb, wg_s[sl, :]).astype(BF)
            ub = _nt(nb, wu_s[sl, :]).astype(BF)
            gg_ref[:, sl] = gb
            uu_ref[:, sl] = ub
            a_ref[0, :, sl] = (gb * _sig(gb)) * ub

    row = lambda i: (i, 0)
    tile = pl.BlockSpec((TM, D), row)
    return _call(
        body, name=name, grid=(T // TM,), args=[x, g, *wbufs], comm=comm,
        in_specs=[tile, pl.BlockSpec((1, D), lambda i: (0, 0)), ANY, ANY],
        out_shape=[jax.ShapeDtypeStruct((T, D), BF), jax.ShapeDtypeStruct((T, F), BF), jax.ShapeDtypeStruct((T, F), BF),
                   jax.ShapeDtypeStruct((1, T, F), BF)],
        out_specs=[tile, pl.BlockSpec((TM, F), row), pl.BlockSpec((TM, F), row),
                   pl.BlockSpec((1, TM, F), lambda i: (0, i, 0))],
        scratch_shapes=[pltpu.VMEM((F, D), BF)] * 2 + [_weight_sems(2)])


def _ffn_down(x, act, wbuf, off, name, comm=None):
    def body(x_ref, a_ref, b0, h_ref, wd_s, sem):
        _load_ffn_weights((b0,), (off,), (wd_s,), sem)
        h_ref[...] = x_ref[...] + 0.5 * _nn(a_ref[0], wd_s[...])

    tile = pl.BlockSpec((TM, D), lambda i: (i, 0))
    return _call(
        body, name=name, grid=(T // TM,), args=[x, act, wbuf], comm=comm,
        in_specs=[tile, pl.BlockSpec((1, TM, F), lambda i: (0, i, 0)), ANY],
        out_shape=[jax.ShapeDtypeStruct((T, D), F32)], out_specs=[tile],
        scratch_shapes=[pltpu.VMEM((F, D), BF), _weight_sems(1)])


def _load_in_proj(parts, w_s, sem):
    @pl.when(pl.program_id(0) == 0)
    def _():
        shard = NG * D // NDEV
        rows = shard // len(parts)
        cps = [pltpu.make_async_copy(buf.at[pl.ds(first + k * rows, rows), :],
                                     w_s.at[pl.ds(k * shard + p * rows, rows), :], sem.at[p * NDEV + k])
               for p, (buf, first) in enumerate(parts) for k in range(NDEV)]
        for cp in cps:
            cp.start()
        for cp in cps:
            cp.wait()


def _mix_in(h1, gm, win, comm=None):
    def body(h_ref, g_ref, *rest):
        w_any, (u_ref, z_ref, w_s, sem) = rest[:len(win)], rest[len(win):]
        _load_in_proj([(b, first) for b, (_, first) in zip(w_any, win)], w_s, sem)
        xf = h_ref[...]
        r = lax.rsqrt(jnp.mean(xf * xf, axis=-1, keepdims=True) + EPS)
        ub = (xf * r * g_ref[...]).astype(BF)
        u_ref[...] = ub
        for j in range(NG):
            z_ref[j] = _nt(ub, w_s[j * D:(j + 1) * D, :]).astype(BF)

    row = lambda i: (i, 0)
    return _call(
        body, name="mix_in", grid=(T // TM,), args=[h1, gm] + [b for b, _ in win], comm=comm,
        in_specs=[pl.BlockSpec((TM, D), row), pl.BlockSpec((1, D), lambda i: (0, 0))] + [ANY] * len(win),
        out_shape=[jax.ShapeDtypeStruct((T, D), BF), jax.ShapeDtypeStruct((NG, T, D), BF)],
        out_specs=[pl.BlockSpec((TM, D), row), pl.BlockSpec((NG, TM, D), lambda i: (0, i, 0))],
        scratch_shapes=[pltpu.VMEM((NG * D, D), BF), pltpu.SemaphoreType.DMA((NDEV * len(win),))])


def _shift_up(w, b):
    return w if b == 0 else pltpu.roll(w, w.shape[0] - b, 0)


def _fold8(p):
    red = p[0:8, :]
    for i in range(1, p.shape[0] // 8):
        red = red + p[8 * i:8 * i + 8, :]
    return red


def _dft_constants():
    import numpy as np
    nh = NB // 2
    f, n = np.arange(nh)[:, None], np.arange(NB)[None, :]
    ang = 2.0 * np.pi / NB * f * n
    fc = np.cos(ang)
    fs = np.where(f == 0, (-1.0) ** n, np.sin(ang))
    scale = np.where(f == 0, 1.0, 2.0) / NB
    ic = (scale * np.cos(ang)).T
    isn = np.where(f == 0, (-1.0) ** n / NB, scale * np.sin(ang)).T
    d = (KA - 1 - np.arange(32))[None, :]
    valid = (np.arange(32) < KA)[None, :]
    angk = 2.0 * np.pi / NB * f * d
    kc = np.where(valid, np.cos(angk), 0.0)
    ks = np.where(valid, np.sin(angk), 0.0)
    k2 = np.where(valid, np.where(f == 0, (-1.0) ** d, np.cos(angk)), 0.0)
    rtc = np.where(valid, scale * np.cos(angk), 0.0).T
    rts = np.where(valid, np.where(f == 0, (-1.0) ** d / NB, scale * np.sin(angk)), 0.0).T

    def bf(a):
        return jnp.asarray(a, F32).astype(BF)

    def split(a):
        hi = bf(a)
        return hi, (jnp.asarray(a, F32) - hi.astype(F32)).astype(BF)

    return dict(fc=bf(fc), fs=bf(fs), ic_hi=bf(ic[HB:]), is_hi=bf(isn[HB:]), ic_lo=bf(ic[:HB]), is_lo=bf(isn[:HB]),
                kc=split(kc), ks=split(ks), k2=split(k2), rtc=split(rtc), rts=split(rts))


def _dot3(m_hi, m_lo, x):
    x_hi = x.astype(BF)
    x_lo = (x - x_hi.astype(F32)).astype(BF)
    return _nn(m_hi, x_hi) + _nn(m_hi, x_lo) + _nn(m_lo, x_hi)


def _whole(a):
    return pl.BlockSpec(a.shape, lambda c, t: (0,) * a.ndim)


def _filter_spectrum(cw_ref, tabs, hc, hs, h2):
    w32 = cw_ref[0:32, :]
    for (hi, lo), dst in zip(tabs, (hc, hs, h2)):
        dst[...] = _dot3(hi[...], lo[...], w32)


def _conv_fwd_dft(z, cw, bias, dft, comm=None):
    nt = T // TB
    hb = TB // HB

    def body(z_ref, zh_ref, cw_ref, b_ref, fc_ref, fs_ref, ic_ref, is_ref, kch, kcl, ksh, ksl, k2h, k2l,
             a1_ref, q_ref, aext, ppad, hc, hs, h2):
        first = pl.program_id(1) == 0
        f = lambda ref, j: ref[j].astype(F32)

        @pl.when(first)
        def _():
            _filter_spectrum(cw_ref, ((kch, kcl), (ksh, ksl), (k2h, k2l)), hc, hs, h2)

        aext[0:HB, :] = jnp.where(first, 0.0, f(zh_ref, 0) * _sig(f(zh_ref, 1))).astype(BF)
        aext[HB:, :] = (f(z_ref, 0) * _sig(f(z_ref, 1))).astype(BF)
        ppad[0:8, :] = jnp.where(first, 0.0, f(zh_ref, 3)[HB - 8:HB, :] * f(zh_ref, 4)[HB - 8:HB, :])
        ppad[8:, :] = f(z_ref, 3) * f(z_ref, 4)
        bias_row = b_ref[...]

        for j in range(TB // HB):
            xs = aext[j * HB:j * HB + NB, :]
            xa, xb = _nn(fc_ref[...], xs), _nn(fs_ref[...], xs)
            yc = (hc[...] * xa - hs[...] * xb).astype(BF)
            ys = (h2[...] * xb + hs[...] * xa).astype(BF)
            y = _nn(ic_ref[...], yc) + _nn(is_ref[...], ys)
            a1_ref[j * HB:(j + 1) * HB, :] = (y + bias_row).astype(BF)

        def chunk(r, carry):
            base = pl.multiple_of(r * CHB, CHB)
            pw = ppad[pl.ds(base, CHB + 8), :]
            v = (cw_ref[pl.ds(32, 1), :] * _shift_up(pw, 6)[0:CHB, :]
                 + cw_ref[pl.ds(33, 1), :] * _shift_up(pw, 7)[0:CHB, :]
                 + cw_ref[pl.ds(34, 1), :] * pw[8:8 + CHB, :])
            q_ref[pl.ds(base, CHB), :] = (z_ref[2, pl.ds(base, CHB), :].astype(F32) * v).astype(BF)
            return carry

        lax.fori_loop(0, TB // CHB, chunk, 0)

    blk = pl.BlockSpec((TB, CW), lambda c, t: (t, c))
    tabs = [dft["fc"], dft["fs"], dft["ic_hi"], dft["is_hi"], *dft["kc"], *dft["ks"], *dft["k2"]]
    return _call(
        body, name="conv_fwd", grid=(D // CW, nt), comm=comm, args=[z, z, cw, bias] + tabs,
        in_specs=[pl.BlockSpec((5, TB, CW), lambda c, t: (0, t, c)),
                  pl.BlockSpec((5, HB, CW), lambda c, t: (0, jnp.maximum(t * hb - 1, 0), c)),
                  pl.BlockSpec((40, CW), lambda c, t: (0, c)), pl.BlockSpec((1, CW), lambda c, t: (0, c))]
                 + [_whole(a) for a in tabs],
        out_shape=[jax.ShapeDtypeStruct((T, D), BF), jax.ShapeDtypeStruct((T, D), BF)], out_specs=[blk, blk],
        scratch_shapes=[pltpu.VMEM((TB + HB, CW), BF), pltpu.VMEM((TB + 8, CW), F32)]
                       + [pltpu.VMEM((NB // 2, CW), F32)] * 3)


def _conv_bwd_dft(z, da1, dq, dzg, cw, dft, comm=None):
    nt = T // TB
    hb = TB // HB
    last_h = T // HB - 1

    def body(z_ref, zp_ref, zn_ref, da1_ref, da1n_ref, dq_ref, dqn_ref, dzg_ref, cw_ref,
             fc_ref, fs_ref, ic_ref, is_ref, kch, kcl, ksh, ksl, k2h, k2l, rch, rcl, rsh, rsl,
             dz_ref, dwa_ref, dwb_ref, aext, dyext, ppad, dvpad, hc, hs, h2, rc, rs, nyq, acc_b):
        t = pl.program_id(1)
        first, last = t == 0, t == nt - 1
        f = lambda ref, j: ref[j].astype(F32)

        @pl.when(first)
        def _():
            _filter_spectrum(cw_ref, ((kch, kcl), (ksh, ksl), (k2h, k2l)), hc, hs, h2)
            rc[...] = jnp.zeros_like(rc)
            rs[...] = jnp.zeros_like(rs)
            nyq[...] = jnp.zeros_like(nyq)
            acc_b[...] = jnp.zeros_like(acc_b)

        aext[0:HB, :] = jnp.where(first, 0.0, f(zp_ref, 0) * _sig(f(zp_ref, 1))).astype(BF)
        aext[HB:, :] = (f(z_ref, 0) * _sig(f(z_ref, 1))).astype(BF)
        dyext[0:TB, :] = da1_ref[...]
        dyext[TB:, :] = jnp.where(last, 0.0, da1n_ref[...].astype(F32)).astype(BF)
        ppad[0:8, :] = jnp.where(first, 0.0, f(zp_ref, 3)[HB - 8:HB, :] * f(zp_ref, 4)[HB - 8:HB, :])
        ppad[8:, :] = f(z_ref, 3) * f(z_ref, 4)
        dvpad[0:TB, :] = dq_ref[...].astype(F32) * f(z_ref, 2)
        dvpad[TB:, :] = jnp.where(last, 0.0, dqn_ref[...].astype(F32)[0:8, :] * f(zn_ref, 2)[0:8, :])

        for j in range(TB // HB):
            rows = slice(j * HB, (j + 1) * HB)
            dys = dyext[j * HB:j * HB + NB, :]
            da, db = _nn(fc_ref[...], dys), _nn(fs_ref[...], dys)
            gc = (hc[...] * da + hs[...] * db).astype(BF)
            gs = (h2[...] * db - hs[...] * da).astype(BF)
            da0 = _nn(ic_ref[...], gc) + _nn(is_ref[...], gs)
            z0, z1 = z_ref[0, rows, :].astype(F32), z_ref[1, rows, :].astype(F32)
            s1 = _sig(z1)
            dz_ref[0, rows, :] = (da0 * s1).astype(BF)
            dz_ref[1, rows, :] = (da0 * z0 * (s1 * (1.0 - s1))).astype(BF)
            xs = aext[j * HB:j * HB + NB, :]
            xa, xb = _nn(fc_ref[...], xs), _nn(fs_ref[...], xs)
            dyb = dyext[rows, :]
            pa, pb = _nn(fc_ref[:, HB:NB], dyb), _nn(fs_ref[:, HB:NB], dyb)
            rc[...] += pa * xa + pb * xb
            rs[...] += pb * xa - pa * xb
            nyq[...] += pb[0:8, :] * xb[0:8, :]

        def chunk(r, carry):
            base = pl.multiple_of(r * CHB, CHB)
            rows = pl.ds(base, CHB)
            pw = ppad[pl.ds(base, CHB + 8), :]
            p6 = _shift_up(pw, 6)[0:CHB, :]
            p7 = _shift_up(pw, 7)[0:CHB, :]
            p8 = pw[8:8 + CHB, :]
            wb0, wb1, wb2 = cw_ref[pl.ds(32, 1), :], cw_ref[pl.ds(33, 1), :], cw_ref[pl.ds(34, 1), :]
            v = wb0 * p6 + wb1 * p7 + wb2 * p8
            dz_ref[2, rows, :] = (dq_ref[rows, :].astype(F32) * v).astype(BF)
            dvw = dvpad[pl.ds(base, CHB + 8), :]
            dvc = dvw[0:CHB, :]
            dp = wb2 * dvc + wb1 * _shift_up(dvw, 1)[0:CHB, :] + wb0 * _shift_up(dvw, 2)[0:CHB, :]
            dz_ref[3, rows, :] = (dp * z_ref[4, rows, :].astype(F32)).astype(BF)
            dz_ref[4, rows, :] = (dp * z_ref[3, rows, :].astype(F32)).astype(BF)
            acc_b[0:8, :] += _fold8(dvc * p6)
            acc_b[8:16, :] += _fold8(dvc * p7)
            acc_b[16:24, :] += _fold8(dvc * p8)
            dz_ref[5, rows, :] = dzg_ref[0, rows, :]
            dz_ref[6, rows, :] = dzg_ref[1, rows, :]
            return carry

        lax.fori_loop(0, TB // CHB, chunk, 0)

        @pl.when(last)
        def _():
            row0 = lax.broadcasted_iota(jnp.int32, (NB // 2, CW), 0) == 0
            ny = jnp.broadcast_to(nyq[0:1, :], (NB // 2, CW))
            rcv = jnp.where(row0, rc[...] - ny, rc[...])
            rsv = jnp.where(row0, ny, rs[...])
            dwa_ref[...] = _dot3(rch[...], rcl[...], rcv) + _dot3(rsh[...], rsl[...], rsv)
            for k in range(KB):
                dwb_ref[k:k + 1, :] = jnp.sum(acc_b[8 * k:8 * k + 8, :], axis=0, keepdims=True)
            dwb_ref[KB:8, :] = jnp.zeros((8 - KB, CW), F32)

    blk = lambda c, t: (t, c)
    nxt = lambda c, t: (jnp.minimum((t + 1) * hb, last_h), c)
    tabs = [dft["fc"], dft["fs"], dft["ic_lo"], dft["is_lo"], *dft["kc"], *dft["ks"], *dft["k2"], *dft["rtc"], *dft["rts"]]
    return _call(
        body, name="conv_bwd", grid=(D // CW, nt), comm=comm, args=[z, z, z, da1, da1, dq, dq, dzg, cw] + tabs,
        in_specs=[pl.BlockSpec((5, TB, CW), lambda c, t: (0, t, c)),
                  pl.BlockSpec((5, HB, CW), lambda c, t: (0, jnp.maximum(t * hb - 1, 0), c)),
                  pl.BlockSpec((5, HB, CW), lambda c, t: (0, jnp.minimum((t + 1) * hb, last_h), c)),
                  pl.BlockSpec((TB, CW), blk), pl.BlockSpec((HB, CW), nxt),
                  pl.BlockSpec((TB, CW), blk), pl.BlockSpec((HB, CW), nxt),
                  pl.BlockSpec((2, TB, CW), lambda c, t: (0, t, c)),
                  pl.BlockSpec((40, CW), lambda c, t: (0, c))]
                 + [_whole(a) for a in tabs],
        out_shape=[jax.ShapeDtypeStruct((NG, T, D), BF), jax.ShapeDtypeStruct((32, D), F32),
                   jax.ShapeDtypeStruct((8, D), F32)],
        out_specs=[pl.BlockSpec((NG, TB, CW), lambda c, t: (0, t, c)),
                   pl.BlockSpec((32, CW), lambda c, t: (0, c)), pl.BlockSpec((8, CW), lambda c, t: (0, c))],
        scratch_shapes=[pltpu.VMEM((TB + HB, CW), BF), pltpu.VMEM((TB + HB, CW), BF),
                        pltpu.VMEM((TB + 8, CW), F32), pltpu.VMEM((TB + 8, CW), F32)]
                       + [pltpu.VMEM((NB // 2, CW), F32)] * 5 + [pltpu.VMEM((8, CW), F32), pltpu.VMEM((24, CW), F32)])


def _layernorm_silu(a1, lng, lnb):
    mu = jnp.mean(a1, axis=-1, keepdims=True)
    xc = a1 - mu
    rs = lax.rsqrt(jnp.mean(xc * xc, axis=-1, keepdims=True) + EPS)
    xh = xc * rs
    a2 = xh * lng + lnb
    sg = _sig(a2)
    return xh, rs, a2, sg


def _square_specs(blocks):
    return [pl.BlockSpec((D, D), lambda i, b=b: (b, 0)) for b in blocks]


def _mix_out(a1, q, z, h1, lng, lnb, wsq, comm=None):
    def body(a1_ref, q_ref, ga_ref, gb_ref, h_ref, lng_ref, lnb_ref, wa_ref, wb_ref, wo_ref, h2_ref, ya_ref, yb_ref):
        _, _, a2, sg = _layernorm_silu(a1_ref[...].astype(F32), lng_ref[...], lnb_ref[...])
        ya = _nn((a2 * sg).astype(BF), wa_ref[...])
        yb = _nn(q_ref[...], wb_ref[...])
        ya_ref[...] = ya.astype(BF)
        yb_ref[...] = yb.astype(BF)
        m = _sig(ga_ref[...].astype(F32)) * ya + _sig(gb_ref[...].astype(F32)) * yb
        h2_ref[...] = h_ref[...] + _nn(m.astype(BF), wo_ref[...])

    row = lambda i: (i, 0)
    vec = pl.BlockSpec((1, D), lambda i: (0, 0))
    return _call(
        body, name="mix_out", grid=(T // TM,), args=[a1, q, z, z, h1, lng, lnb, wsq, wsq, wsq], comm=comm,
        in_specs=[pl.BlockSpec((TM, D), row), pl.BlockSpec((TM, D), row),
                  pl.BlockSpec((None, TM, D), lambda i: (5, i, 0)), pl.BlockSpec((None, TM, D), lambda i: (6, i, 0)),
                  pl.BlockSpec((TM, D), row), vec, vec] + _square_specs((0, 1, 2)),
        out_shape=[jax.ShapeDtypeStruct((T, D), F32), jax.ShapeDtypeStruct((T, D), BF), jax.ShapeDtypeStruct((T, D), BF)],
        out_specs=[pl.BlockSpec((TM, D), row)] * 3)


def _rmsnorm_bwd(xf, g, dn):
    r = lax.rsqrt(jnp.mean(xf * xf, axis=-1, keepdims=True) + EPS)
    xr = xf * r
    gdn = dn * g
    dx = r * gdn - xr * (r * jnp.mean(gdn * xr, axis=-1, keepdims=True))
    return dx, jnp.sum(dn * xr, axis=0, keepdims=True)


def _ffn_bwd_hidden(dh, gg, uu, wbuf, off, name, comm=None):
    nf = F // FC

    def body(dh_ref, gg_ref, uu_ref, b0, dgu_ref, wd_s, sem):
        _load_ffn_weights((b0,), (off,), (wd_s,), sem)
        dhb = (0.5 * dh_ref[...]).astype(BF)
        for c in range(nf):
            sl = slice(c * FC, (c + 1) * FC)
            da = _nt(dhb, wd_s[sl, :]).astype(BF)
            gb, ub = gg_ref[:, sl], uu_ref[:, sl]
            sg = _sig(gb)
            dgu_ref[0, :, sl] = (da * ub) * (sg * (1.0 + gb * (1.0 - sg)))
            dgu_ref[0, :, F + c * FC:F + (c + 1) * FC] = da * (gb * sg)

    row = lambda i: (i, 0)
    return _call(
        body, name=name, grid=(T // TM,), args=[dh, gg, uu, wbuf], comm=comm,
        in_specs=[pl.BlockSpec((TM, D), row), pl.BlockSpec((TM, F), row), pl.BlockSpec((TM, F), row), ANY],
        out_shape=[jax.ShapeDtypeStruct((1, T, 2 * F), BF)],
        out_specs=[pl.BlockSpec((1, TM, 2 * F), lambda i: (0, i, 0))],
        scratch_shapes=[pltpu.VMEM((F, D), BF), _weight_sems(1)])


def _ffn_bwd_input(dgu, dh, x, g, wbufs, offs, name, comm=None):
    def body(dgu_ref, dh_ref, x_ref, g_ref, b0, b1, dx_ref, s_ref, w_s, sem):
        _load_ffn_weights((b0, b1), offs, (w_s.at[pl.ds(0, F), :], w_s.at[pl.ds(F, F), :]), sem)

        @pl.when(pl.program_id(0) == 0)
        def _():
            s_ref[...] = jnp.zeros_like(s_ref)

        dn = _nn(dgu_ref[0], w_s[...])
        dxn, dg = _rmsnorm_bwd(x_ref[...], g_ref[...], dn)
        dx_ref[...] = dh_ref[...] + dxn
        s_ref[0:1, :] += dg

    row = lambda i: (i, 0)
    return _call(
        body, name=name, grid=(T // TM,), args=[dgu, dh, x, g, *wbufs], comm=comm,
        in_specs=[pl.BlockSpec((1, TM, 2 * F), lambda i: (0, i, 0)), pl.BlockSpec((TM, D), row),
                  pl.BlockSpec((TM, D), row), pl.BlockSpec((1, D), lambda i: (0, 0)), ANY, ANY],
        out_shape=[jax.ShapeDtypeStruct((T, D), F32), jax.ShapeDtypeStruct((8, D), F32)],
        out_specs=[pl.BlockSpec((TM, D), row), pl.BlockSpec((8, D), lambda i: (0, 0))],
        scratch_shapes=[pltpu.VMEM((2 * F, D), BF), _weight_sems(2)])


def _tn_matmul(lhs, rhs, tr, name, comm=None, scale=None):
    ng, _, cdim = lhs.shape
    nc, nk = cdim // tr, T // TK
    if rhs.ndim == 2:
        r_spec = pl.BlockSpec((TK, D), lambda g, c, k: (k, 0))
    else:
        r_spec = pl.BlockSpec((None, TK, D), lambda g, c, k: (g, k, 0))

    def body(l_ref, r_ref, o_ref, acc):
        k = pl.program_id(2)

        @pl.when(k == 0)
        def _():
            acc[...] = jnp.zeros_like(acc)

        r = r_ref[...] if scale is None else scale * r_ref[...]
        acc[...] += _tn(l_ref[...], r.astype(BF))

        @pl.when(k == nk - 1)
        def _():
            o_ref[...] = acc[...].astype(BF)

    return _call(
        body, name=name, grid=(ng, nc, nk), args=[lhs, rhs], comm=comm,
        in_specs=[pl.BlockSpec((None, TK, tr), lambda g, c, k: (g, k, c)), r_spec],
        out_shape=[jax.ShapeDtypeStruct((ng * cdim, D), BF)],
        out_specs=[pl.BlockSpec((tr, D), lambda g, c, k: (g * nc + c, 0))],
        scratch_shapes=[pltpu.VMEM((tr, D), F32)])


def _mix_out_bwd(dh2, ya, yb, z, a1, q, lng, lnb, wsq, comm=None):
    def body(dh_ref, ya_ref, yb_ref, ga_ref, gb_ref, a1_ref, q_ref, lng_ref, lnb_ref, wa_ref, wb_ref, wo_ref,
             dzg_ref, da1_ref, dq_ref, l_ref, r_ref, s_ref):
        @pl.when(pl.program_id(0) == 0)
        def _():
            s_ref[...] = jnp.zeros_like(s_ref)

        dhb = dh_ref[...].astype(BF)
        dm = _nt(dhb, wo_ref[...])
        ya, yb = ya_ref[...].astype(F32), yb_ref[...].astype(F32)
        sa, sb = _sig(ga_ref[...].astype(F32)), _sig(gb_ref[...].astype(F32))
        l_ref[0] = (sa * ya + sb * yb).astype(BF)
        l_ref[2] = q_ref[...]
        dzg_ref[0] = (dm * ya * (sa * (1.0 - sa))).astype(BF)
        dzg_ref[1] = (dm * yb * (sb * (1.0 - sb))).astype(BF)
        dya = (dm * sa).astype(BF)
        dyb = (dm * sb).astype(BF)
        r_ref[0] = dhb
        r_ref[1] = dya
        r_ref[2] = dyb
        dq_ref[...] = _nt(dyb, wb_ref[...]).astype(BF)
        da3 = _nt(dya, wa_ref[...])
        lng = lng_ref[...]
        xh, rs, a2, sg = _layernorm_silu(a1_ref[...].astype(F32), lng, lnb_ref[...])
        l_ref[1] = (a2 * sg).astype(BF)
        da2 = da3 * (sg * (1.0 + a2 * (1.0 - sg)))
        s_ref[0:1, :] += jnp.sum(da2 * xh, axis=0, keepdims=True)
        s_ref[1:2, :] += jnp.sum(da2, axis=0, keepdims=True)
        dxh = da2 * lng
        da1 = rs * (dxh - jnp.mean(dxh, axis=-1, keepdims=True) - xh * jnp.mean(dxh * xh, axis=-1, keepdims=True))
        da1_ref[...] = da1.astype(BF)
        s_ref[2:3, :] += jnp.sum(da1, axis=0, keepdims=True)

    row = lambda i: (i, 0)
    row3 = lambda i: (0, i, 0)
    vec = pl.BlockSpec((1, D), lambda i: (0, 0))
    return _call(
        body, name="mix_out_bwd", grid=(T // TM,), args=[dh2, ya, yb, z, z, a1, q, lng, lnb, wsq, wsq, wsq], comm=comm,
        in_specs=[pl.BlockSpec((TM, D), row), pl.BlockSpec((TM, D), row), pl.BlockSpec((TM, D), row),
                  pl.BlockSpec((None, TM, D), lambda i: (5, i, 0)), pl.BlockSpec((None, TM, D), lambda i: (6, i, 0)),
                  pl.BlockSpec((TM, D), row), pl.BlockSpec((TM, D), row), vec, vec] + _square_specs((0, 1, 2)),
        out_shape=[jax.ShapeDtypeStruct((2, T, D), BF), jax.ShapeDtypeStruct((T, D), BF),
                   jax.ShapeDtypeStruct((T, D), BF), jax.ShapeDtypeStruct((3, T, D), BF),
                   jax.ShapeDtypeStruct((3, T, D), BF), jax.ShapeDtypeStruct((8, D), F32)],
        out_specs=[pl.BlockSpec((2, TM, D), row3), pl.BlockSpec((TM, D), row), pl.BlockSpec((TM, D), row),
                   pl.BlockSpec((3, TM, D), row3), pl.BlockSpec((3, TM, D), row3), pl.BlockSpec((8, D), lambda i: (0, 0))])


def _mix_in_bwd(dz, dh2, h1, gm, win, comm=None):
    def body(dz_ref, dh_ref, h_ref, g_ref, *rest):
        w_any, (o_ref, s_ref, w_s, sem) = rest[:len(win)], rest[len(win):]
        _load_in_proj([(b, first) for b, (_, first) in zip(w_any, win)], w_s, sem)

        @pl.when(pl.program_id(0) == 0)
        def _():
            s_ref[...] = jnp.zeros_like(s_ref)

        du = _nn(dz_ref[0], w_s[0:D, :])
        for j in range(1, NG):
            du = du + _nn(dz_ref[j], w_s[j * D:(j + 1) * D, :])
        dx, dg = _rmsnorm_bwd(h_ref[...], g_ref[...], du)
        o_ref[...] = dh_ref[...] + dx
        s_ref[0:1, :] += dg

    row = lambda i: (i, 0)
    return _call(
        body, name="mix_in_bwd", grid=(T // TM,), args=[dz, dh2, h1, gm] + [b for b, _ in win], comm=comm,
        in_specs=[pl.BlockSpec((NG, TM, D), lambda i: (0, i, 0)), pl.BlockSpec((TM, D), row),
                  pl.BlockSpec((TM, D), row), pl.BlockSpec((1, D), lambda i: (0, 0))] + [ANY] * len(win),
        out_shape=[jax.ShapeDtypeStruct((T, D), F32), jax.ShapeDtypeStruct((8, D), F32)],
        out_specs=[pl.BlockSpec((TM, D), row), pl.BlockSpec((8, D), lambda i: (0, 0))],
        scratch_shapes=[pltpu.VMEM((NG * D, D), BF), pltpu.SemaphoreType.DMA((NDEV * len(win),))])


def _row_tile(n, want, mult):
    for t in range(min(want, n), 0, -1):
        if n % t == 0 and t % mult == 0:
            return t
    return n


def _sum_slots(recv, name):
    ns, rows, cols = recv.shape
    tr = _row_tile(rows, 1024, 16)

    def body(r_ref, o_ref):
        s = r_ref[0].astype(F32)
        for k in range(1, ns):
            s = s + r_ref[k].astype(F32)
        o_ref[...] = s

    return _call(
        body, name=name, grid=(rows // tr,), args=[recv],
        in_specs=[pl.BlockSpec((ns, tr, cols), lambda i: (0, i, 0))],
        out_shape=[jax.ShapeDtypeStruct((rows, cols), F32)],
        out_specs=[pl.BlockSpec((tr, cols), lambda i: (i, 0))])[0]


def _pack_small(s_ffn1, s_in, s_mix, s_ffn2, s_final, dwa, dwb):
    def body(f1, mi, mo, f2, fl, wa_ref, wb_ref, v_ref, k_ref):
        for dst, (ref, row) in enumerate(((f1, 0), (mi, 0), (mo, 0), (mo, 1), (mo, 2), (f2, 0), (fl, 0), (fl, 1))):
            v_ref[dst:dst + 1, :] = ref[row:row + 1, :]
        for k in range(NDEV):
            k_ref[k, 0:32, :] = wa_ref[:, k * LANE:(k + 1) * LANE]
            k_ref[k, 32:40, :] = wb_ref[:, k * LANE:(k + 1) * LANE]

    return pl.pallas_call(
        body, name="pack_small",
        out_shape=(jax.ShapeDtypeStruct((8, D), F32), jax.ShapeDtypeStruct((NDEV, 40, LANE), F32)),
    )(s_ffn1, s_in, s_mix, s_ffn2, s_final, dwa, dwb)


def _sum_small(vecs, convs):
    def body(v_ref, k_ref, vs_ref, ks_ref, l_ref):
        s, c = v_ref[0], k_ref[0]
        for k in range(1, NDEV):
            s = s + v_ref[k]
            c = c + k_ref[k]
        vs_ref[...] = s
        ks_ref[...] = c
        l_ref[...] = jnp.broadcast_to(jnp.sum(s[7:8, :], axis=-1, keepdims=True), (8, LANE))

    return pl.pallas_call(
        body, name="sum_small",
        out_shape=(jax.ShapeDtypeStruct((8, D), F32), jax.ShapeDtypeStruct((40, LANE), F32),
                   jax.ShapeDtypeStruct((8, LANE), F32)),
    )(vecs, convs)


def _adam(gs, ws, ms, vs, name, comm=None):
    n = len(gs)
    rows, cols = ws[0].shape
    tr = _row_tile(rows, 256, 16)
    c1 = 1.0 - ADAM_B1 ** ADAM_STEP
    c2 = 1.0 - ADAM_B2 ** ADAM_STEP
    summed = [isinstance(g, tuple) for g in gs]

    def body(*refs):
        for i in range(n):
            g_in, w, m, v = refs[4 * i], refs[4 * i + 1][...], refs[4 * i + 2][...], refs[4 * i + 3][...]
            g_ref, d_ref, m_ref, v_ref = refs[4 * n + 4 * i: 4 * n + 4 * i + 4]
            if summed[i]:
                g = g_in[0].astype(F32)
                for k in range(1, g_in.shape[0]):
                    g = g + g_in[k].astype(F32)
            else:
                g = g_in[...]
            g_ref[...] = g
            m2 = ADAM_B1 * m + (1.0 - ADAM_B1) * g
            v2 = ADAM_B2 * v + (1.0 - ADAM_B2) * (g * g)
            d_ref[...] = -ADAM_LR * ((m2 / c1) / (jnp.sqrt(v2 / c2) + ADAM_EPS) + ADAM_WD * w)
            m_ref[...] = m2
            v_ref[...] = v2

    spec = pl.BlockSpec((tr, cols), lambda i: (i, 0))
    args, in_specs = [], []
    for i in range(n):
        if summed[i]:
            slots, first = gs[i]
            args.append(slots)
            in_specs.append(pl.BlockSpec((slots.shape[0], tr, cols), lambda i, b=first // tr: (0, b + i, 0)))
        else:
            args.append(gs[i])
            in_specs.append(spec)
        args += [ws[i], ms[i], vs[i]]
        in_specs += [spec] * 3
    outs = _call(body, name=name, grid=(rows // tr,), args=args, comm=comm, in_specs=in_specs,
                 out_shape=[jax.ShapeDtypeStruct((rows, cols), F32)] * (4 * n), out_specs=[spec] * (4 * n))
    return [tuple(outs[4 * i: 4 * i + 4]) for i in range(n)], outs[4 * n:]


def kernel(x, ffn1_norm, ffn1_w_gate, ffn1_w_up, ffn1_w_down, mix_norm, w_in, a_dw_w, a_dw_b, a_ln_g, a_ln_b, a_w_out, b_conv_w, b_w_out, w_o, ffn2_norm, ffn2_w_gate, ffn2_w_up, ffn2_w_down, final_norm, loss_target, m_ffn1_norm, m_ffn1_w_gate, m_ffn1_w_up, m_ffn1_w_down, m_mix_norm, m_w_in, m_a_dw_w, m_a_dw_b, m_a_ln_g, m_a_ln_b, m_a_w_out, m_b_conv_w, m_b_w_out, m_w_o, m_ffn2_norm, m_ffn2_w_gate, m_ffn2_w_up, m_ffn2_w_down, m_final_norm, v_ffn1_norm, v_ffn1_w_gate, v_ffn1_w_up, v_ffn1_w_down, v_mix_norm, v_w_in, v_a_dw_w, v_a_dw_b, v_a_ln_g, v_a_ln_b, v_a_w_out, v_b_conv_w, v_b_w_out, v_w_o, v_ffn2_norm, v_ffn2_w_gate, v_ffn2_w_up, v_ffn2_w_down, v_final_norm):
    names = ("ffn1_norm", "ffn1_w_gate", "ffn1_w_up", "ffn1_w_down", "mix_norm", "w_in", "a_dw_w", "a_dw_b",
             "a_ln_g", "a_ln_b", "a_w_out", "b_conv_w", "b_w_out", "w_o", "ffn2_norm", "ffn2_w_gate", "ffn2_w_up",
             "ffn2_w_down", "final_norm")
    w = dict(ffn1_norm=ffn1_norm, ffn1_w_gate=ffn1_w_gate, ffn1_w_up=ffn1_w_up, ffn1_w_down=ffn1_w_down,
             mix_norm=mix_norm, w_in=w_in, a_dw_w=a_dw_w, a_dw_b=a_dw_b, a_ln_g=a_ln_g, a_ln_b=a_ln_b,
             a_w_out=a_w_out, b_conv_w=b_conv_w, b_w_out=b_w_out, w_o=w_o, ffn2_norm=ffn2_norm,
             ffn2_w_gate=ffn2_w_gate, ffn2_w_up=ffn2_w_up, ffn2_w_down=ffn2_w_down, final_norm=final_norm)
    m = dict(ffn1_norm=m_ffn1_norm, ffn1_w_gate=m_ffn1_w_gate, ffn1_w_up=m_ffn1_w_up, ffn1_w_down=m_ffn1_w_down,
             mix_norm=m_mix_norm, w_in=m_w_in, a_dw_w=m_a_dw_w, a_dw_b=m_a_dw_b, a_ln_g=m_a_ln_g, a_ln_b=m_a_ln_b,
             a_w_out=m_a_w_out, b_conv_w=m_b_conv_w, b_w_out=m_b_w_out, w_o=m_w_o, ffn2_norm=m_ffn2_norm,
             ffn2_w_gate=m_ffn2_w_gate, ffn2_w_up=m_ffn2_w_up, ffn2_w_down=m_ffn2_w_down, final_norm=m_final_norm)
    v = dict(ffn1_norm=v_ffn1_norm, ffn1_w_gate=v_ffn1_w_gate, ffn1_w_up=v_ffn1_w_up, ffn1_w_down=v_ffn1_w_down,
             mix_norm=v_mix_norm, w_in=v_w_in, a_dw_w=v_a_dw_w, a_dw_b=v_a_dw_b, a_ln_g=v_a_ln_g, a_ln_b=v_a_ln_b,
             a_w_out=v_a_w_out, b_conv_w=v_b_conv_w, b_w_out=v_b_w_out, w_o=v_w_o, ffn2_norm=v_ffn2_norm,
             ffn2_w_gate=v_ffn2_w_gate, ffn2_w_up=v_ffn2_w_up, ffn2_w_down=v_ffn2_w_down, final_norm=v_final_norm)
    flat = _pack_weights(dict(wg1=ffn1_w_gate[0].T, wu1=ffn1_w_up[0].T, wd1=ffn1_w_down[0], wg2=ffn2_w_gate[0].T,
                              wu2=ffn2_w_up[0].T, wd2=ffn2_w_down[0], win=w_in[0], wa=a_w_out[0], wb=b_w_out[0],
                              wo=w_o[0]))
    cw_shard = jnp.concatenate([a_dw_w[0], jnp.zeros((1, LANE), F32), b_conv_w[0], jnp.zeros((5, LANE), F32)], axis=0)

    x2, tgt = x[0], loss_target[0]
    st_a, st_b, st_b2 = ("wg1", "wu1"), ("wd1", "win/0/2"), ("win/1/2",)
    st_c, st_d, st_e = ("wa", "wb", "wo", "wg2"), ("wu2",), ("wd2",)

    buf_a, cw = _run_comm(_join(_ag_comm(st_a, flat), _direct_comm(cw_shard, False)), "ag_ffn1")
    n1, gg1, uu1, act1, buf_b = _ffn_gate_up(x2, ffn1_norm, (buf_a, buf_a), (0, F), "ffn1_gate_up", _ag_comm(st_b, flat))
    h1, buf_b2 = _ffn_down(x2, act1, buf_b, 0, "ffn1_down", _ag_comm(st_b2, flat))
    win = ((buf_b, F), (buf_b2, 0))
    u, z, buf_c = _mix_in(h1, mix_norm, win, _ag_comm(st_c, flat))
    dft = _dft_constants()
    cw = jnp.transpose(cw, (1, 0, 2)).reshape(40, D)
    a1, q, buf_d = _conv_fwd_dft(z, cw, a_dw_b, dft, _ag_comm(st_d, flat))
    h2, ya, yb, buf_e = _mix_out(a1, q, z, h1, a_ln_g, a_ln_b, buf_c, _ag_comm(st_e, flat))
    ffn2_bufs, ffn2_offs = (buf_c, buf_d, buf_e), (3 * D, 0, 0)
    dh3, s_final, n2, gg2, uu2, act2 = _ffn_fwd(h2, ffn2_norm, ffn2_bufs, ffn2_offs, "ffn2_fwd",
                                          final=(final_norm.reshape(1, D), tgt))

    tr_f = F // 2 if (F // 2) % LANE == 0 else F
    def pair(stage, src):
        return _rs_pair_comm(stage, src)

    def chip(stage, src, pair_buf, tag):
        return _rs_chip_comm(_pair_add(stage, src, pair_buf, "pair_add_" + tag))

    (dgu2,) = _ffn_bwd_hidden(dh3, gg2, uu2, buf_e, 0, "ffn2_bwd_h")
    (gu2,) = _tn_matmul(dgu2, n2, tr_f, "dw_gu2")
    s2a, src2a = ("wg2", "wu2"), dict(wg2=(gu2, 0), wu2=(gu2, F))
    gd2, pair2a = _tn_matmul(act2, dh3, tr_f, "dw_d2", pair(s2a, src2a), scale=0.5)
    s2b, src2b = ("wd2",), dict(wd2=(gd2, 0))
    dh2, s_ffn2, pair2b = _ffn_bwd_input(dgu2, dh3, h2, ffn2_norm, (buf_c, buf_d), (3 * D, 0), "ffn2_bwd_x",
                                         pair(s2b, src2b))
    dzg, da1, dq, lsq, rsq, s_mix, recv2b = _mix_out_bwd(dh2, ya, yb, z, a1, q, a_ln_g, a_ln_b, buf_c,
                                                          chip(s2b, src2b, pair2b, "2b"))
    (gsq,) = _tn_matmul(lsq, rsq, D, "dw_square")
    ssq, srcsq = ("wa", "wb", "wo"), dict(wa=(gsq, D), wb=(gsq, 2 * D), wo=(gsq, 0))
    dz, dwa, dwb, recv2a, pairsq = _conv_bwd_dft(z, da1, dq, dzg, cw, dft,
                                                 _join(chip(s2a, src2a, pair2a, "2a"), pair(ssq, srcsq)))
    gin, recvsq = _tn_matmul(dz, u, D, "dw_in", chip(ssq, srcsq, pairsq, "sq"))
    sin_a, sin_b, srcin = ("win/0/2",), ("win/1/2",), {"win/0/2": (gin, 0), "win/1/2": (gin, 0)}
    dh1, s_in, pairin_a, pairin_b = _mix_in_bwd(dz, dh2, h1, mix_norm, win,
                                                _join(pair(sin_a, srcin), pair(sin_b, srcin)))
    dgu1, recvin_a = _ffn_bwd_hidden(dh1, gg1, uu1, buf_b, 0, "ffn1_bwd_h",
                                           chip(sin_a, srcin, pairin_a, "in_a"))
    gu1, recvin_b = _tn_matmul(dgu1, n1, tr_f, "dw_gu1", chip(sin_b, srcin, pairin_b, "in_b"))
    s1a, src1a = ("wg1", "wu1"), dict(wg1=(gu1, 0), wu1=(gu1, F))
    gd1, pair1a = _tn_matmul(act1, dh1, tr_f, "dw_d1", pair(s1a, src1a), scale=0.5)
    s1b, src1b = ("wd1",), dict(wd1=(gd1, 0))
    dx, s_ffn1, recv1a, pair1b = _ffn_bwd_input(dgu1, dh1, x2, ffn1_norm, (buf_a, buf_a), (0, F), "ffn1_bwd_x",
                                                _join(chip(s1a, src1a, pair1a, "1a"), pair(s1b, src1b)))
    win_sum = jnp.concatenate([_sum_slots(recvin_a, "sum_in_a"), _sum_slots(recvin_b, "sum_in_b")], axis=0)

    vec8, convk = _pack_small(s_ffn1, s_in, s_mix, s_ffn2, s_final, dwa, dwb)
    vec_all, conv_all = _run_comm(_join(_direct_comm(vec8, False), _direct_comm(convk, True)), "xchg_small")
    vec_sum, conv_sum, loss_blk = _sum_small(vec_all, conv_all)
    loss = loss_blk[0, 0]

    fs = F // NDEV
    g = dict(ffn1_w_gate=(recv1a, 0), ffn1_w_up=(recv1a, fs), ffn2_w_gate=(recv2a, 0), ffn2_w_up=(recv2a, fs),
             ffn2_w_down=(recv2b, 0), a_w_out=(recvsq, 0), b_w_out=(recvsq, D // NDEV), w_o=(recvsq, 2 * (D // NDEV)),
             w_in=win_sum.T, ffn1_norm=vec_sum[0:1], mix_norm=vec_sum[1:2], a_ln_g=vec_sum[2:3], a_ln_b=vec_sum[3:4],
             a_dw_b=vec_sum[4:5], ffn2_norm=vec_sum[5:6], final_norm=vec_sum[6:7],
             a_dw_w=conv_sum[0:KA], b_conv_w=conv_sum[32:32 + KB])
    grad, upd = {}, {}

    def run(group, name, as2d=lambda a: a[0], back=lambda a, n: a.reshape(w[n].shape), comm=None):
        res, extra = _adam([g[n] for n in group], [as2d(w[n]) for n in group], [as2d(m[n]) for n in group],
                           [as2d(v[n]) for n in group], name, comm)
        for n, r in zip(group, res):
            grad[n], upd[n] = back(r[0], n), tuple(back(a, n) for a in r[1:])
        return extra

    rows_first = dict(as2d=lambda a: a[0].T, back=lambda a, n: a.T[None])
    (recv1b,) = run(("ffn1_w_gate", "ffn1_w_up", "ffn2_w_gate", "ffn2_w_up"), "adam_gate_up",
                    comm=chip(s1b, src1b, pair1b, "1b"), **rows_first)
    g["ffn1_w_down"] = (recv1b, 0)
    run(("ffn1_w_down", "ffn2_w_down"), "adam_down")
    run(("w_in",), "adam_in")
    run(("a_w_out", "b_w_out", "w_o"), "adam_square")
    run(("a_dw_w",), "adam_dw")
    run(("b_conv_w",), "adam_conv")
    run(("ffn1_norm", "mix_norm", "a_dw_b", "a_ln_g", "a_ln_b", "ffn2_norm", "final_norm"), "adam_vec",
        as2d=lambda a: a.reshape(1, D))

    return (loss, dx.reshape(x.shape), *[grad[n] for n in names], *[upd[n][0] for n in names],
            *[upd[n][1] for n in names], *[upd[n][2] for n in names])
```

```python
import jax
import jax.numpy as jnp
from jax import lax
from jax.experimental import pallas as pl
from jax.experimental.pallas import tpu as pltpu

T = 4096
D = 1024
F = 2816
NG = 7
NDEV = 8
NCHIP = 4
KA, KB = 31, 3
EPS = 1e-6
ADAM_LR, ADAM_B1, ADAM_B2, ADAM_EPS, ADAM_WD, ADAM_STEP = 0.001, 0.9, 0.999, 1e-08, 0.01, 10

TM = 512
FC = 256
TB = 1024
NB = 256
HB = NB // 2
CW = 256
CHB = 64
LANE = 128
TK = 2048
VMEM_LIMIT = 56 * 1024 * 1024

BF = jnp.bfloat16
F32 = jnp.float32
MESH = pl.DeviceIdType.MESH
ANY = pl.BlockSpec(memory_space=pl.ANY)

ORDER = ("wg1", "wu1", "wd1", "wg2", "wu2", "wd2", "win", "wa", "wb", "wo")


class _Layout:
    def __init__(self):
        fs, dis, ds = F // NDEV, NG * D // NDEV, D // NDEV
        self.rows = dict(wg1=fs, wu1=fs, wd1=fs, wg2=fs, wu2=fs, wd2=fs, win=dis, wa=ds, wb=ds, wo=ds)
        self.fl, off = {}, 0
        for n in ORDER:
            self.fl[n] = off
            off += self.rows[n]
        self.RT = off


class _Stage:
    def __init__(self, names):
        lay = _Layout()
        self.names = names
        self.rows, self.full, self.sub, self.fl = {}, {}, {}, {}
        for n in names:
            base, i, k = (n.split("/") + ["0", "1"])[:3]
            self.full[n] = lay.rows[base]
            self.rows[n] = lay.rows[base] // int(k)
            self.sub[n] = int(i) * self.rows[n]
            self.fl[n] = lay.fl[base] + self.sub[n]
        self.off, self.wc, o, w = {}, {}, 0, 0
        for n in names:
            self.off[n], self.wc[n] = o, w
            o += self.rows[n]
            w += NDEV * self.rows[n]
        self.R, self.W = o, w

    def grad_row(self, n, first, dev_lin):
        return first + dev_lin * self.full[n] + self.sub[n]


def _nt(a, b):
    return lax.dot_general(a, b, (((1,), (1,)), ((), ())), preferred_element_type=F32)


def _nn(a, b):
    return lax.dot_general(a, b, (((1,), (0,)), ((), ())), preferred_element_type=F32)


def _tn(a, b):
    return lax.dot_general(a, b, (((0,), (0,)), ((), ())), preferred_element_type=F32)


def _sig(x):
    return 1.0 / (1.0 + jnp.exp(-x))


def _position():
    return lax.axis_index("x"), lax.axis_index("y"), lax.axis_index("c")


def _peer(pos, j):
    x, y, c = pos
    return (1 - x if j & 4 else x, 1 - y if j & 2 else y, 1 - c if j & 1 else c)


def _lin(pos):
    return 4 * pos[0] + 2 * pos[1] + pos[2]


def _chip(pos):
    return 2 * pos[0] + pos[1]


class _Comm:
    def __init__(self, inputs, out_shapes, scratch, start, finish, middle=None):
        self.inputs, self.out_shapes, self.scratch = inputs, out_shapes, scratch
        self.start, self.finish, self.middle = start, finish, middle


def _call(body, *, name, grid, args, in_specs, out_shape, out_specs, scratch_shapes=(), comm=None,
          num_scalar_prefetch=0):
    in_specs, out_shape, out_specs, scratch_shapes = list(in_specs), list(out_shape), list(out_specs), list(scratch_shapes)
    n_in, n_out, n_scr = len(in_specs), len(out_shape), len(scratch_shapes)
    sp = num_scalar_prefetch
    if comm is None:
        kernel_fn = lambda *refs: body(*refs)
        c_in = c_out = c_scr = 0
    else:
        c_in, c_out, c_scr = len(comm.inputs), len(comm.out_shapes), len(comm.scratch)

        def kernel_fn(*refs):
            pre, refs = refs[:sp], refs[sp:]
            ins, cins = refs[:n_in], refs[n_in:n_in + c_in]
            o0 = n_in + c_in
            outs, couts = refs[o0:o0 + n_out], refs[o0 + n_out:o0 + n_out + c_out]
            s0 = o0 + n_out + c_out
            scr, cscr = refs[s0:s0 + n_scr], refs[s0 + n_scr:]
            step, steps = pl.program_id(0), grid[0]
            for a in range(1, len(grid)):
                step, steps = step * grid[a] + pl.program_id(a), steps * grid[a]
            first, last = step == 0, step == steps - 1

            @pl.when(first)
            def _():
                comm.start(cins, couts, cscr)

            if comm.middle is not None:
                @pl.when(step == (steps // 2 if steps > 2 else steps - 1))
                def _():
                    comm.middle(cins, couts, cscr)

            body(*pre, *ins, *outs, *scr)

            @pl.when(last)
            def _():
                comm.finish(cins, couts, cscr)

        args = list(args) + list(comm.inputs)
        in_specs += [ANY] * c_in
        out_shape += list(comm.out_shapes)
        out_specs += [ANY] * c_out
        scratch_shapes += list(comm.scratch)
    params = pltpu.CompilerParams(dimension_semantics=("arbitrary",) * len(grid), vmem_limit_bytes=VMEM_LIMIT)
    if sp:
        grid_spec = pltpu.PrefetchScalarGridSpec(num_scalar_prefetch=sp, grid=grid, in_specs=in_specs,
                                                 out_specs=out_specs, scratch_shapes=scratch_shapes)
        return pl.pallas_call(kernel_fn, name=name, grid_spec=grid_spec, out_shape=out_shape,
                              compiler_params=params)(*args)
    return pl.pallas_call(kernel_fn, name=name, grid=grid, in_specs=in_specs, out_shape=out_shape, out_specs=out_specs,
                          scratch_shapes=scratch_shapes, compiler_params=params)(*args)


def _join(a, b):
    na = (len(a.inputs), len(a.out_shapes), len(a.scratch))

    def split(refs):
        return ([r[:n] for r, n in zip(refs, na)], [r[n:] for r, n in zip(refs, na)])

    def start(*refs):
        ra, rb = split(refs)
        a.start(*ra)
        b.start(*rb)

    def finish(*refs):
        ra, rb = split(refs)
        a.finish(*ra)
        b.finish(*rb)

    def middle(*refs):
        for stage, r in zip((a, b), split(refs)):
            if stage.middle is not None:
                stage.middle(*r)

    return _Comm(list(a.inputs) + list(b.inputs), list(a.out_shapes) + list(b.out_shapes),
                 list(a.scratch) + list(b.scratch), start, finish,
                 middle if (a.middle is not None or b.middle is not None) else None)


def _run_comm(comm, name):
    def body(*refs):
        c_in, c_out = len(comm.inputs), len(comm.out_shapes)
        parts = (refs[:c_in], refs[c_in:c_in + c_out], refs[c_in + c_out:])
        comm.start(*parts)
        if comm.middle is not None:
            comm.middle(*parts)
        comm.finish(*parts)

    return pl.pallas_call(
        body, name=name, out_shape=list(comm.out_shapes), in_specs=[ANY] * len(comm.inputs),
        out_specs=[ANY] * len(comm.out_shapes), scratch_shapes=list(comm.scratch))(*comm.inputs)


def _ag_comm(names, flat):
    st = _Stage(names)

    def ring(me):
        x, y, c = me
        diagonal = x == y
        up = (jnp.where(diagonal, x, 1 - x), jnp.where(diagonal, 1 - y, y), c)
        down = (jnp.where(diagonal, 1 - x, x), jnp.where(diagonal, y, 1 - y), c)
        low = c == 0
        passed = tuple(jnp.where(low, d, u) for d, u in zip(down, up))
        target = tuple(jnp.where(low, u, d) for d, u in zip(down, up))
        return up, down, (1 - x, 1 - y, c), passed, target

    def parts(refs):
        (flat_ref,), (out_ref,), (send_sems, recv_sems, local_sem) = refs
        me = _position()

        def region(name, dev):
            r = st.rows[name]
            return out_ref.at[pl.ds(st.wc[name] + _lin(dev) * r, r), :]

        def own(name):
            return flat_ref.at[pl.ds(st.fl[name], st.rows[name]), :]

        def copies(k, dev, to, from_flat):
            return [pltpu.make_async_remote_copy(
                src_ref=own(n) if from_flat else region(n, dev), dst_ref=region(n, dev), send_sem=send_sems.at[k],
                recv_sem=recv_sems.at[k], device_id=to, device_id_type=MESH) for n in names]

        def whole(k):
            return pltpu.make_async_remote_copy(
                src_ref=flat_ref.at[pl.ds(0, st.R), :], dst_ref=out_ref.at[pl.ds(0, st.R), :],
                send_sem=send_sems.at[k], recv_sem=recv_sems.at[k], device_id=me, device_id_type=MESH)

        return me, region, own, copies, whole, flat_ref, out_ref, local_sem

    def start(*refs):
        me, region, own, copies, _, _, _, local_sem = parts(refs)
        for n in names:
            pltpu.make_async_copy(own(n), region(n, me), local_sem).start()
        up, down, _, _, _ = ring(me)
        for k, to in ((1, up), (2, down), (0, _peer(me, 1))):
            for cp in copies(k, me, to, True):
                cp.start()

    def middle(*refs):
        me, _, _, copies, whole, _, _, _ = parts(refs)
        up, down, _, passed, target = ring(me)
        sib = _peer(me, 1)
        whole(1).wait_recv()
        whole(2).wait_recv()
        for k, dev, to in ((3, passed, target), (4, down, sib), (5, up, sib)):
            for cp in copies(k, dev, to, False):
                cp.start()

    def finish(*refs):
        me, _, _, copies, whole, flat_ref, out_ref, local_sem = parts(refs)
        _, _, across, _, _ = ring(me)
        whole(3).wait_recv()
        for cp in copies(6, across, _peer(me, 1), False):
            cp.start()
        whole(0).wait_recv()
        for j in range(3):
            whole(4 + j).wait_recv()
        for k in range(7):
            whole(k).wait_send()
        pltpu.make_async_copy(flat_ref.at[pl.ds(0, st.R), :], out_ref.at[pl.ds(0, st.R), :], local_sem).wait()

    return _Comm([flat], [jax.ShapeDtypeStruct((st.W, D), BF)],
                 [pltpu.SemaphoreType.DMA((7,)), pltpu.SemaphoreType.DMA((7,)), pltpu.SemaphoreType.DMA],
                 start, finish, middle)


def _rs_pair_comm(names, src):
    st = _Stage(names)
    arrays = []
    for n in names:
        if not any(src[n][0] is a for a in arrays):
            arrays.append(src[n][0])
    idx = {n: [i for i, a in enumerate(arrays) if a is src[n][0]][0] for n in names}

    def slot_wait(refs):
        recv = refs[1][0]
        send_sem, recv_sem = refs[2]
        return pltpu.make_async_remote_copy(src_ref=recv, dst_ref=recv, send_sem=send_sem, recv_sem=recv_sem,
                                            device_id=_position(), device_id_type=MESH)

    def start(*refs):
        ins, (recv,), (send_sem, recv_sem) = refs
        me = _position()
        sib = _peer(me, 1)
        for q in range(NCHIP):
            dev = (q // 2, q % 2, sib[2])
            for n in names:
                r = st.rows[n]
                pltpu.make_async_remote_copy(
                    src_ref=ins[idx[n]].at[pl.ds(st.grad_row(n, src[n][1], _lin(dev)), r), :],
                    dst_ref=recv.at[q, pl.ds(st.off[n], r), :], send_sem=send_sem, recv_sem=recv_sem,
                    device_id=sib, device_id_type=MESH).start()

    def finish(*refs):
        w = slot_wait(refs)
        w.wait_recv()
        w.wait_send()

    return _Comm(arrays, [jax.ShapeDtypeStruct((NCHIP, st.R, D), BF)],
                 [pltpu.SemaphoreType.DMA, pltpu.SemaphoreType.DMA], start, finish)


def _pair_add(names, src, recv, name):
    st = _Stage(names)
    c_arr = jnp.reshape(lax.axis_index("c"), (1,)).astype(jnp.int32)

    def body(c_ref, *refs):
        r_ref, o_ref = refs[len(names)], refs[len(names) + 1]
        for a_ref, n in zip(refs, names):
            rows = slice(st.off[n], st.off[n] + st.rows[n])
            o_ref[rows, :] = (a_ref[...].astype(F32) + r_ref[rows, :].astype(F32)).astype(BF)

    def shard_spec(n):
        r = st.rows[n]
        base, step = st.grad_row(n, src[n][1], 0) // r, st.full[n] // r
        return pl.BlockSpec((r, D), lambda q, c_ref: (base + step * (2 * q + c_ref[0]), 0))

    slot = pl.BlockSpec((None, st.R, D), lambda q, c_ref: (q, 0, 0))
    return _call(body, name=name, grid=(NCHIP,), args=[c_arr] + [src[n][0] for n in names] + [recv],
                 in_specs=[shard_spec(n) for n in names] + [slot],
                 out_shape=[jax.ShapeDtypeStruct((NCHIP, st.R, D), BF)], out_specs=[slot], num_scalar_prefetch=1)[0]


def _rs_chip_comm(part):
    def copies(refs):
        (p_ref,), (recv,), (send_sems, recv_sems, local_sem) = refs
        me = _position()
        mine = pltpu.make_async_copy(p_ref.at[_chip(me)], recv.at[_chip(me)], local_sem)
        out = []
        for j, bits in enumerate((4, 2, 6)):
            to = _peer(me, bits)
            out.append(pltpu.make_async_remote_copy(
                src_ref=p_ref.at[_chip(to)], dst_ref=recv.at[_chip(me)], send_sem=send_sems.at[j],
                recv_sem=recv_sems.at[j], device_id=to, device_id_type=MESH))
        return mine, out

    def start(*refs):
        mine, out = copies(refs)
        mine.start()
        for cp in out:
            cp.start()

    def finish(*refs):
        mine, out = copies(refs)
        for cp in out:
            cp.wait_recv()
        for cp in out:
            cp.wait_send()
        mine.wait()

    return _Comm([part], [jax.ShapeDtypeStruct(part.shape, BF)],
                 [pltpu.SemaphoreType.DMA((3,)), pltpu.SemaphoreType.DMA((3,)), pltpu.SemaphoreType.DMA],
                 start, finish)


def _direct_comm(x, scatter):
    def copies(refs):
        (x_ref,), (out_ref,), (send_sems, recv_sems, local_sem) = refs
        me = _position()

        def piece(dev):
            return x_ref.at[_lin(dev)] if scatter else x_ref

        mine = pltpu.make_async_copy(piece(me), out_ref.at[_lin(me)], local_sem)
        return mine, [pltpu.make_async_remote_copy(
            src_ref=piece(_peer(me, j)), dst_ref=out_ref.at[_lin(me)], send_sem=send_sems.at[j - 1],
            recv_sem=recv_sems.at[j - 1], device_id=_peer(me, j), device_id_type=MESH) for j in range(1, NDEV)]

    def start(*refs):
        mine, cps = copies(refs)
        mine.start()
        for cp in cps:
            cp.start()

    def finish(*refs):
        mine, cps = copies(refs)
        for cp in cps:
            cp.wait_recv()
        for cp in cps:
            cp.wait_send()
        mine.wait()

    shape = x.shape if scatter else (NDEV,) + x.shape
    return _Comm([x], [jax.ShapeDtypeStruct(shape, x.dtype)],
                 [pltpu.SemaphoreType.DMA((7,)), pltpu.SemaphoreType.DMA((7,)), pltpu.SemaphoreType.DMA],
                 start, finish)


def _pack_weights(shards):
    lay = _Layout()

    def body(*refs):
        o_ref = refs[-1]
        for ref, n in zip(refs, ORDER):
            x = ref[...].T if n == "win" else ref[...]
            o_ref[lay.fl[n]:lay.fl[n] + lay.rows[n], :] = x.astype(BF)

    return pl.pallas_call(
        body, name="pack_weights", out_shape=jax.ShapeDtypeStruct((lay.RT, D), BF),
        compiler_params=pltpu.CompilerParams(vmem_limit_bytes=VMEM_LIMIT))(*[shards[n] for n in ORDER])


def _load_ffn_weights(srcs, offs, scratch, sem):
    @pl.when(pl.program_id(0) == 0)
    def _():
        cps = [pltpu.make_async_copy(s.at[pl.ds(off, dst.shape[0]), :], dst, sem.at[i])
               for i, (s, off, dst) in enumerate(zip(srcs, offs, scratch))]
        for cp in cps:
            cp.start()
        for cp in cps:
            cp.wait()


def _final_loss_tile(xf, g, tgt, s_ref):
    r = lax.rsqrt(jnp.mean(xf * xf, axis=-1, keepdims=True) + EPS)
    xr = xf * r
    e = xr * g - tgt
    s_ref[1:2, :] += jnp.sum(e * e, axis=0, keepdims=True) * (0.5 / D)
    dy = e * (1.0 / D)
    s_ref[0:1, :] += jnp.sum(dy * xr, axis=0, keepdims=True)
    gdy = dy * g
    return r * gdy - xr * (r * jnp.mean(gdy * xr, axis=-1, keepdims=True))


def _ffn_fwd(x, g, wbufs, offs, name, comm=None, final=None):
    nf = F // FC

    def body(x_ref, g_ref, b0, b1, b2, *rest):
        if final is None:
            h_ref, n_ref, gg_ref, uu_ref, a_ref, wg_s, wu_s, wd_s, sem = rest
        else:
            gf_ref, t_ref, dh_ref, s_ref, n_ref, gg_ref, uu_ref, a_ref, wg_s, wu_s, wd_s, sem = rest

            @pl.when(pl.program_id(0) == 0)
            def _():
                s_ref[...] = jnp.zeros_like(s_ref)

        _load_ffn_weights((b0, b1, b2), offs, (wg_s, wu_s, wd_s), sem)
        xf = x_ref[...]
        r = lax.rsqrt(jnp.mean(xf * xf, axis=-1, keepdims=True) + EPS)
        nb = (xf * r * g_ref[...]).astype(BF)
        n_ref[...] = nb
        acc = jnp.zeros((TM, D), F32)
        for c in range(nf):
            sl = slice(c * FC, (c + 1) * FC)
            gb = _nt(nb, wg_s[sl, :]).astype(BF)
            ub = _nt(nb, wu_s[sl, :]).astype(BF)
            gg_ref[:, sl] = gb
            uu_ref[:, sl] = ub
            a = (gb * _sig(gb)) * ub
            a_ref[0, :, sl] = a
            acc = acc + _nn(a, wd_s[sl, :])
        h = xf + 0.5 * acc
        if final is None:
            h_ref[...] = h
        else:
            dh_ref[...] = _final_loss_tile(h, gf_ref[...], t_ref[...], s_ref)

    row = lambda i: (i, 0)
    vec = pl.BlockSpec((1, D), lambda i: (0, 0))
    tile = pl.BlockSpec((TM, D), row)
    saved_shapes = [jax.ShapeDtypeStruct((T, D), BF), jax.ShapeDtypeStruct((T, F), BF), jax.ShapeDtypeStruct((T, F), BF),
                    jax.ShapeDtypeStruct((1, T, F), BF)]
    saved_specs = [tile, pl.BlockSpec((TM, F), row), pl.BlockSpec((TM, F), row),
                   pl.BlockSpec((1, TM, F), lambda i: (0, i, 0))]
    if final is None:
        extra_args, extra_specs = [], []
        head_shapes, head_specs = [jax.ShapeDtypeStruct((T, D), F32)], [tile]
    else:
        extra_args, extra_specs = list(final), [vec, tile]
        head_shapes = [jax.ShapeDtypeStruct((T, D), F32), jax.ShapeDtypeStruct((8, D), F32)]
        head_specs = [tile, pl.BlockSpec((8, D), lambda i: (0, 0))]
    return _call(
        body, name=name, grid=(T // TM,), args=[x, g, *wbufs, *extra_args], comm=comm,
        in_specs=[tile, vec, ANY, ANY, ANY] + extra_specs,
        out_shape=head_shapes + saved_shapes, out_specs=head_specs + saved_specs,
        scratch_shapes=[pltpu.VMEM((F, D), BF)] * 3 + [pltpu.SemaphoreType.DMA((3,))])


def _ffn_gate_up(x, g, wbufs, offs, name, comm=None):
    nf = F // FC

    def body(x_ref, g_ref, b0, b1, n_ref, gg_ref, uu_ref, a_ref, wg_s, wu_s, sem):
        _load_ffn_weights((b0, b1), offs, (wg_s, wu_s), sem)
        xf = x_ref[...]
        r = lax.rsqrt(jnp.mean(xf * xf, axis=-1, keepdims=True) + EPS)
        nb = (xf * r * g_ref[...]).astype(BF)
        n_ref[...] = nb
        for c in range(nf):
            sl = slice(c * FC, (c + 1) * FC)
            gb = _nt(nb, wg_s[sl, :]).astype(BF)
            ub = _nt(nb, wu_s[sl, :]).astype(BF)
            gg_ref[:, sl] = gb
            uu_ref[:, sl] = ub
            a_ref[0, :, sl] = (gb * _sig(gb)) * ub

    row = lambda i: (i, 0)
    tile = pl.BlockSpec((TM, D), row)
    return _call(
        body, name=name, grid=(T // TM,), args=[x, g, *wbufs], comm=comm,
        in_specs=[tile, pl.BlockSpec((1, D), lambda i: (0, 0)), ANY, ANY],
        out_shape=[jax.ShapeDtypeStruct((T, D), BF), jax.ShapeDtypeStruct((T, F), BF), jax.ShapeDtypeStruct((T, F), BF),
                   jax.ShapeDtypeStruct((1, T, F), BF)],
        out_specs=[tile, pl.BlockSpec((TM, F), row), pl.BlockSpec((TM, F), row),
                   pl.BlockSpec((1, TM, F), lambda i: (0, i, 0))],
        scratch_shapes=[pltpu.VMEM((F, D), BF)] * 2 + [pltpu.SemaphoreType.DMA((2,))])


def _ffn_down(x, act, wbuf, off, name, comm=None):
    def body(x_ref, a_ref, b0, h_ref, wd_s, sem):
        _load_ffn_weights((b0,), (off,), (wd_s,), sem)
        h_ref[...] = x_ref[...] + 0.5 * _nn(a_ref[0], wd_s[...])

    tile = pl.BlockSpec((TM, D), lambda i: (i, 0))
    return _call(
        body, name=name, grid=(T // TM,), args=[x, act, wbuf], comm=comm,
        in_specs=[tile, pl.BlockSpec((1, TM, F), lambda i: (0, i, 0)), ANY],
        out_shape=[jax.ShapeDtypeStruct((T, D), F32)], out_specs=[tile],
        scratch_shapes=[pltpu.VMEM((F, D), BF), pltpu.SemaphoreType.DMA((1,))])


def _load_in_proj(parts, w_s, sem):
    @pl.when(pl.program_id(0) == 0)
    def _():
        shard = NG * D // NDEV
        rows = shard // len(parts)
        cps = [pltpu.make_async_copy(buf.at[pl.ds(first + k * rows, rows), :],
                                     w_s.at[pl.ds(k * shard + p * rows, rows), :], sem.at[p * NDEV + k])
               for p, (buf, first) in enumerate(parts) for k in range(NDEV)]
        for cp in cps:
            cp.start()
        for cp in cps:
            cp.wait()


def _mix_in(h1, gm, win, comm=None):
    def body(h_ref, g_ref, *rest):
        w_any, (u_ref, z_ref, w_s, sem) = rest[:len(win)], rest[len(win):]
        _load_in_proj([(b, first) for b, (_, first) in zip(w_any, win)], w_s, sem)
        xf = h_ref[...]
        r = lax.rsqrt(jnp.mean(xf * xf, axis=-1, keepdims=True) + EPS)
        ub = (xf * r * g_ref[...]).astype(BF)
        u_ref[...] = ub
        for j in range(NG):
            z_ref[j] = _nt(ub, w_s[j * D:(j + 1) * D, :]).astype(BF)

    row = lambda i: (i, 0)
    return _call(
        body, name="mix_in", grid=(T // TM,), args=[h1, gm] + [b for b, _ in win], comm=comm,
        in_specs=[pl.BlockSpec((TM, D), row), pl.BlockSpec((1, D), lambda i: (0, 0))] + [ANY] * len(win),
        out_shape=[jax.ShapeDtypeStruct((T, D), BF), jax.ShapeDtypeStruct((NG, T, D), BF)],
        out_specs=[pl.BlockSpec((TM, D), row), pl.BlockSpec((NG, TM, D), lambda i: (0, i, 0))],
        scratch_shapes=[pltpu.VMEM((NG * D, D), BF), pltpu.SemaphoreType.DMA((NDEV * len(win),))])


def _shift_up(w, b):
    return w if b == 0 else pltpu.roll(w, w.shape[0] - b, 0)


def _fold8(p):
    red = p[0:8, :]
    for i in range(1, p.shape[0] // 8):
        red = red + p[8 * i:8 * i + 8, :]
    return red


def _dft_constants():
    import numpy as np
    nh = NB // 2
    f, n = np.arange(nh)[:, None], np.arange(NB)[None, :]
    ang = 2.0 * np.pi / NB * f * n
    fc = np.cos(ang)
    fs = np.where(f == 0, (-1.0) ** n, np.sin(ang))
    scale = np.where(f == 0, 1.0, 2.0) / NB
    ic = (scale * np.cos(ang)).T
    isn = np.where(f == 0, (-1.0) ** n / NB, scale * np.sin(ang)).T
    d = (KA - 1 - np.arange(32))[None, :]
    valid = (np.arange(32) < KA)[None, :]
    angk = 2.0 * np.pi / NB * f * d
    kc = np.where(valid, np.cos(angk), 0.0)
    ks = np.where(valid, np.sin(angk), 0.0)
    k2 = np.where(valid, np.where(f == 0, (-1.0) ** d, np.cos(angk)), 0.0)
    rtc = np.where(valid, scale * np.cos(angk), 0.0).T
    rts = np.where(valid, np.where(f == 0, (-1.0) ** d / NB, scale * np.sin(angk)), 0.0).T

    def bf(a):
        return jnp.asarray(a, F32).astype(BF)

    def split(a):
        hi = bf(a)
        return hi, (jnp.asarray(a, F32) - hi.astype(F32)).astype(BF)

    return dict(fc=bf(fc), fs=bf(fs), ic_hi=bf(ic[HB:]), is_hi=bf(isn[HB:]), ic_lo=bf(ic[:HB]), is_lo=bf(isn[:HB]),
                kc=split(kc), ks=split(ks), k2=split(k2), rtc=split(rtc), rts=split(rts))


def _dot3(m_hi, m_lo, x):
    x_hi = x.astype(BF)
    x_lo = (x - x_hi.astype(F32)).astype(BF)
    return _nn(m_hi, x_hi) + _nn(m_hi, x_lo) + _nn(m_lo, x_hi)


def _whole(a):
    return pl.BlockSpec(a.shape, lambda c, t: (0,) * a.ndim)


def _filter_spectrum(cw_ref, tabs, hc, hs, h2):
    w32 = cw_ref[0:32, :]
    for (hi, lo), dst in zip(tabs, (hc, hs, h2)):
        dst[...] = _dot3(hi[...], lo[...], w32)


def _conv_fwd_dft(z, cw, bias, dft, comm=None):
    nt = T // TB
    hb = TB // HB

    def body(z_ref, zh_ref, cw_ref, b_ref, fc_ref, fs_ref, ic_ref, is_ref, kch, kcl, ksh, ksl, k2h, k2l,
             a1_ref, q_ref, aext, ppad, hc, hs, h2):
        first = pl.program_id(1) == 0
        f = lambda ref, j: ref[j].astype(F32)

        @pl.when(first)
        def _():
            _filter_spectrum(cw_ref, ((kch, kcl), (ksh, ksl), (k2h, k2l)), hc, hs, h2)

        aext[0:HB, :] = jnp.where(first, 0.0, f(zh_ref, 0) * _sig(f(zh_ref, 1))).astype(BF)
        aext[HB:, :] = (f(z_ref, 0) * _sig(f(z_ref, 1))).astype(BF)
        ppad[0:8, :] = jnp.where(first, 0.0, f(zh_ref, 3)[HB - 8:HB, :] * f(zh_ref, 4)[HB - 8:HB, :])
        ppad[8:, :] = f(z_ref, 3) * f(z_ref, 4)
        bias_row = b_ref[...]

        for j in range(TB // HB):
            xs = aext[j * HB:j * HB + NB, :]
            xa, xb = _nn(fc_ref[...], xs), _nn(fs_ref[...], xs)
            yc = (hc[...] * xa - hs[...] * xb).astype(BF)
            ys = (h2[...] * xb + hs[...] * xa).astype(BF)
            y = _nn(ic_ref[...], yc) + _nn(is_ref[...], ys)
            a1_ref[j * HB:(j + 1) * HB, :] = (y + bias_row).astype(BF)

        def chunk(r, carry):
            base = pl.multiple_of(r * CHB, CHB)
            pw = ppad[pl.ds(base, CHB + 8), :]
            v = (cw_ref[pl.ds(32, 1), :] * _shift_up(pw, 6)[0:CHB, :]
                 + cw_ref[pl.ds(33, 1), :] * _shift_up(pw, 7)[0:CHB, :]
                 + cw_ref[pl.ds(34, 1), :] * pw[8:8 + CHB, :])
            q_ref[pl.ds(base, CHB), :] = (z_ref[2, pl.ds(base, CHB), :].astype(F32) * v).astype(BF)
            return carry

        lax.fori_loop(0, TB // CHB, chunk, 0)

    blk = pl.BlockSpec((TB, CW), lambda c, t: (t, c))
    tabs = [dft["fc"], dft["fs"], dft["ic_hi"], dft["is_hi"], *dft["kc"], *dft["ks"], *dft["k2"]]
    return _call(
        body, name="conv_fwd", grid=(D // CW, nt), comm=comm, args=[z, z, cw, bias] + tabs,
        in_specs=[pl.BlockSpec((5, TB, CW), lambda c, t: (0, t, c)),
                  pl.BlockSpec((5, HB, CW), lambda c, t: (0, jnp.maximum(t * hb - 1, 0), c)),
                  pl.BlockSpec((40, CW), lambda c, t: (0, c)), pl.BlockSpec((1, CW), lambda c, t: (0, c))]
                 + [_whole(a) for a in tabs],
        out_shape=[jax.ShapeDtypeStruct((T, D), BF), jax.ShapeDtypeStruct((T, D), BF)], out_specs=[blk, blk],
        scratch_shapes=[pltpu.VMEM((TB + HB, CW), BF), pltpu.VMEM((TB + 8, CW), F32)]
                       + [pltpu.VMEM((NB // 2, CW), F32)] * 3)


def _conv_bwd_dft(z, da1, dq, dzg, cw, dft, comm=None):
    nt = T // TB
    hb = TB // HB
    last_h = T // HB - 1

    def body(z_ref, zp_ref, zn_ref, da1_ref, da1n_ref, dq_ref, dqn_ref, dzg_ref, cw_ref,
             fc_ref, fs_ref, ic_ref, is_ref, kch, kcl, ksh, ksl, k2h, k2l, rch, rcl, rsh, rsl,
             dz_ref, dwa_ref, dwb_ref, aext, dyext, ppad, dvpad, hc, hs, h2, rc, rs, nyq, acc_b):
        t = pl.program_id(1)
        first, last = t == 0, t == nt - 1
        f = lambda ref, j: ref[j].astype(F32)

        @pl.when(first)
        def _():
            _filter_spectrum(cw_ref, ((kch, kcl), (ksh, ksl), (k2h, k2l)), hc, hs, h2)
            rc[...] = jnp.zeros_like(rc)
            rs[...] = jnp.zeros_like(rs)
            nyq[...] = jnp.zeros_like(nyq)
            acc_b[...] = jnp.zeros_like(acc_b)

        aext[0:HB, :] = jnp.where(first, 0.0, f(zp_ref, 0) * _sig(f(zp_ref, 1))).astype(BF)
        aext[HB:, :] = (f(z_ref, 0) * _sig(f(z_ref, 1))).astype(BF)
        dyext[0:TB, :] = da1_ref[...]
        dyext[TB:, :] = jnp.where(last, 0.0, da1n_ref[...].astype(F32)).astype(BF)
        ppad[0:8, :] = jnp.where(first, 0.0, f(zp_ref, 3)[HB - 8:HB, :] * f(zp_ref, 4)[HB - 8:HB, :])
        ppad[8:, :] = f(z_ref, 3) * f(z_ref, 4)
        dvpad[0:TB, :] = dq_ref[...].astype(F32) * f(z_ref, 2)
        dvpad[TB:, :] = jnp.where(last, 0.0, dqn_ref[...].astype(F32)[0:8, :] * f(zn_ref, 2)[0:8, :])

        for j in range(TB // HB):
            rows = slice(j * HB, (j + 1) * HB)
            dys = dyext[j * HB:j * HB + NB, :]
            da, db = _nn(fc_ref[...], dys), _nn(fs_ref[...], dys)
            gc = (hc[...] * da + hs[...] * db).astype(BF)
            gs = (h2[...] * db - hs[...] * da).astype(BF)
            da0 = _nn(ic_ref[...], gc) + _nn(is_ref[...], gs)
            z0, z1 = z_ref[0, rows, :].astype(F32), z_ref[1, rows, :].astype(F32)
            s1 = _sig(z1)
            dz_ref[0, rows, :] = (da0 * s1).astype(BF)
            dz_ref[1, rows, :] = (da0 * z0 * (s1 * (1.0 - s1))).astype(BF)
            xs = aext[j * HB:j * HB + NB, :]
            xa, xb = _nn(fc_ref[...], xs), _nn(fs_ref[...], xs)
            dyb = dyext[rows, :]
            pa, pb = _nn(fc_ref[:, HB:NB], dyb), _nn(fs_ref[:, HB:NB], dyb)
            rc[...] += pa * xa + pb * xb
            rs[...] += pb * xa - pa * xb
            nyq[...] += pb[0:8, :] * xb[0:8, :]

        def chunk(r, carry):
            base = pl.multiple_of(r * CHB, CHB)
            rows = pl.ds(base, CHB)
            pw = ppad[pl.ds(base, CHB + 8), :]
            p6 = _shift_up(pw, 6)[0:CHB, :]
            p7 = _shift_up(pw, 7)[0:CHB, :]
            p8 = pw[8:8 + CHB, :]
            wb0, wb1, wb2 = cw_ref[pl.ds(32, 1), :], cw_ref[pl.ds(33, 1), :], cw_ref[pl.ds(34, 1), :]
            v = wb0 * p6 + wb1 * p7 + wb2 * p8
            dz_ref[2, rows, :] = (dq_ref[rows, :].astype(F32) * v).astype(BF)
            dvw = dvpad[pl.ds(base, CHB + 8), :]
            dvc = dvw[0:CHB, :]
            dp = wb2 * dvc + wb1 * _shift_up(dvw, 1)[0:CHB, :] + wb0 * _shift_up(dvw, 2)[0:CHB, :]
            dz_ref[3, rows, :] = (dp * z_ref[4, rows, :].astype(F32)).astype(BF)
            dz_ref[4, rows, :] = (dp * z_ref[3, rows, :].astype(F32)).astype(BF)
            acc_b[0:8, :] += _fold8(dvc * p6)
            acc_b[8:16, :] += _fold8(dvc * p7)
            acc_b[16:24, :] += _fold8(dvc * p8)
            dz_ref[5, rows, :] = dzg_ref[0, rows, :]
            dz_ref[6, rows, :] = dzg_ref[1, rows, :]
            return carry

        lax.fori_loop(0, TB // CHB, chunk, 0)

        @pl.when(last)
        def _():
            row0 = lax.broadcasted_iota(jnp.int32, (NB // 2, CW), 0) == 0
            ny = jnp.broadcast_to(nyq[0:1, :], (NB // 2, CW))
            rcv = jnp.where(row0, rc[...] - ny, rc[...])
            rsv = jnp.where(row0, ny, rs[...])
            dwa_ref[...] = _dot3(rch[...], rcl[...], rcv) + _dot3(rsh[...], rsl[...], rsv)
            for k in range(KB):
                dwb_ref[k:k + 1, :] = jnp.sum(acc_b[8 * k:8 * k + 8, :], axis=0, keepdims=True)
            dwb_ref[KB:8, :] = jnp.zeros((8 - KB, CW), F32)

    blk = lambda c, t: (t, c)
    nxt = lambda c, t: (jnp.minimum((t + 1) * hb, last_h), c)
    tabs = [dft["fc"], dft["fs"], dft["ic_lo"], dft["is_lo"], *dft["kc"], *dft["ks"], *dft["k2"], *dft["rtc"], *dft["rts"]]
    return _call(
        body, name="conv_bwd", grid=(D // CW, nt), comm=comm, args=[z, z, z, da1, da1, dq, dq, dzg, cw] + tabs,
        in_specs=[pl.BlockSpec((5, TB, CW), lambda c, t: (0, t, c)),
                  pl.BlockSpec((5, HB, CW), lambda c, t: (0, jnp.maximum(t * hb - 1, 0), c)),
                  pl.BlockSpec((5, HB, CW), lambda c, t: (0, jnp.minimum((t + 1) * hb, last_h), c)),
                  pl.BlockSpec((TB, CW), blk), pl.BlockSpec((HB, CW), nxt),
                  pl.BlockSpec((TB, CW), blk), pl.BlockSpec((HB, CW), nxt),
                  pl.BlockSpec((2, TB, CW), lambda c, t: (0, t, c)),
                  pl.BlockSpec((40, CW), lambda c, t: (0, c))]
                 + [_whole(a) for a in tabs],
        out_shape=[jax.ShapeDtypeStruct((NG, T, D), BF), jax.ShapeDtypeStruct((32, D), F32),
                   jax.ShapeDtypeStruct((8, D), F32)],
        out_specs=[pl.BlockSpec((NG, TB, CW), lambda c, t: (0, t, c)),
                   pl.BlockSpec((32, CW), lambda c, t: (0, c)), pl.BlockSpec((8, CW), lambda c, t: (0, c))],
        scratch_shapes=[pltpu.VMEM((TB + HB, CW), BF), pltpu.VMEM((TB + HB, CW), BF),
                        pltpu.VMEM((TB + 8, CW), F32), pltpu.VMEM((TB + 8, CW), F32)]
                       + [pltpu.VMEM((NB // 2, CW), F32)] * 5 + [pltpu.VMEM((8, CW), F32), pltpu.VMEM((24, CW), F32)])


def _layernorm_silu(a1, lng, lnb):
    mu = jnp.mean(a1, axis=-1, keepdims=True)
    xc = a1 - mu
    rs = lax.rsqrt(jnp.mean(xc * xc, axis=-1, keepdims=True) + EPS)
    xh = xc * rs
    a2 = xh * lng + lnb
    sg = _sig(a2)
    return xh, rs, a2, sg


def _square_specs(blocks):
    return [pl.BlockSpec((D, D), lambda i, b=b: (b, 0)) for b in blocks]


def _mix_out(a1, q, z, h1, lng, lnb, wsq, comm=None):
    def body(a1_ref, q_ref, ga_ref, gb_ref, h_ref, lng_ref, lnb_ref, wa_ref, wb_ref, wo_ref, h2_ref, ya_ref, yb_ref):
        _, _, a2, sg = _layernorm_silu(a1_ref[...].astype(F32), lng_ref[...], lnb_ref[...])
        ya = _nn((a2 * sg).astype(BF), wa_ref[...])
        yb = _nn(q_ref[...], wb_ref[...])
        ya_ref[...] = ya.astype(BF)
        yb_ref[...] = yb.astype(BF)
        m = _sig(ga_ref[...].astype(F32)) * ya + _sig(gb_ref[...].astype(F32)) * yb
        h2_ref[...] = h_ref[...] + _nn(m.astype(BF), wo_ref[...])

    row = lambda i: (i, 0)
    vec = pl.BlockSpec((1, D), lambda i: (0, 0))
    return _call(
        body, name="mix_out", grid=(T // TM,), args=[a1, q, z, z, h1, lng, lnb, wsq, wsq, wsq], comm=comm,
        in_specs=[pl.BlockSpec((TM, D), row), pl.BlockSpec((TM, D), row),
                  pl.BlockSpec((None, TM, D), lambda i: (5, i, 0)), pl.BlockSpec((None, TM, D), lambda i: (6, i, 0)),
                  pl.BlockSpec((TM, D), row), vec, vec] + _square_specs((0, 1, 2)),
        out_shape=[jax.ShapeDtypeStruct((T, D), F32), jax.ShapeDtypeStruct((T, D), BF), jax.ShapeDtypeStruct((T, D), BF)],
        out_specs=[pl.BlockSpec((TM, D), row)] * 3)


def _rmsnorm_bwd(xf, g, dn):
    r = lax.rsqrt(jnp.mean(xf * xf, axis=-1, keepdims=True) + EPS)
    xr = xf * r
    gdn = dn * g
    dx = r * gdn - xr * (r * jnp.mean(gdn * xr, axis=-1, keepdims=True))
    return dx, jnp.sum(dn * xr, axis=0, keepdims=True)


def _ffn_bwd_hidden(dh, gg, uu, wbuf, off, name, comm=None):
    nf = F // FC

    def body(dh_ref, gg_ref, uu_ref, b0, dgu_ref, wd_s, sem):
        _load_ffn_weights((b0,), (off,), (wd_s,), sem)
        dhb = (0.5 * dh_ref[...]).astype(BF)
        for c in range(nf):
            sl = slice(c * FC, (c + 1) * FC)
            da = _nt(dhb, wd_s[sl, :]).astype(BF)
            gb, ub = gg_ref[:, sl], uu_ref[:, sl]
            sg = _sig(gb)
            dgu_ref[0, :, sl] = (da * ub) * (sg * (1.0 + gb * (1.0 - sg)))
            dgu_ref[0, :, F + c * FC:F + (c + 1) * FC] = da * (gb * sg)

    row = lambda i: (i, 0)
    return _call(
        body, name=name, grid=(T // TM,), args=[dh, gg, uu, wbuf], comm=comm,
        in_specs=[pl.BlockSpec((TM, D), row), pl.BlockSpec((TM, F), row), pl.BlockSpec((TM, F), row), ANY],
        out_shape=[jax.ShapeDtypeStruct((1, T, 2 * F), BF)],
        out_specs=[pl.BlockSpec((1, TM, 2 * F), lambda i: (0, i, 0))],
        scratch_shapes=[pltpu.VMEM((F, D), BF), pltpu.SemaphoreType.DMA((1,))])


def _ffn_bwd_input(dgu, dh, x, g, wbufs, offs, name, comm=None):
    def body(dgu_ref, dh_ref, x_ref, g_ref, b0, b1, dx_ref, s_ref, w_s, sem):
        _load_ffn_weights((b0, b1), offs, (w_s.at[pl.ds(0, F), :], w_s.at[pl.ds(F, F), :]), sem)

        @pl.when(pl.program_id(0) == 0)
        def _():
            s_ref[...] = jnp.zeros_like(s_ref)

        dn = _nn(dgu_ref[0], w_s[...])
        dxn, dg = _rmsnorm_bwd(x_ref[...], g_ref[...], dn)
        dx_ref[...] = dh_ref[...] + dxn
        s_ref[0:1, :] += dg

    row = lambda i: (i, 0)
    return _call(
        body, name=name, grid=(T // TM,), args=[dgu, dh, x, g, *wbufs], comm=comm,
        in_specs=[pl.BlockSpec((1, TM, 2 * F), lambda i: (0, i, 0)), pl.BlockSpec((TM, D), row),
                  pl.BlockSpec((TM, D), row), pl.BlockSpec((1, D), lambda i: (0, 0)), ANY, ANY],
        out_shape=[jax.ShapeDtypeStruct((T, D), F32), jax.ShapeDtypeStruct((8, D), F32)],
        out_specs=[pl.BlockSpec((TM, D), row), pl.BlockSpec((8, D), lambda i: (0, 0))],
        scratch_shapes=[pltpu.VMEM((2 * F, D), BF), pltpu.SemaphoreType.DMA((2,))])


def _tn_matmul(lhs, rhs, tr, name, comm=None, scale=None):
    ng, _, cdim = lhs.shape
    nc, nk = cdim // tr, T // TK
    if rhs.ndim == 2:
        r_spec = pl.BlockSpec((TK, D), lambda g, c, k: (k, 0))
    else:
        r_spec = pl.BlockSpec((None, TK, D), lambda g, c, k: (g, k, 0))

    def body(l_ref, r_ref, o_ref, acc):
        k = pl.program_id(2)

        @pl.when(k == 0)
        def _():
            acc[...] = jnp.zeros_like(acc)

        r = r_ref[...] if scale is None else scale * r_ref[...]
        acc[...] += _tn(l_ref[...], r.astype(BF))

        @pl.when(k == nk - 1)
        def _():
            o_ref[...] = acc[...].astype(BF)

    return _call(
        body, name=name, grid=(ng, nc, nk), args=[lhs, rhs], comm=comm,
        in_specs=[pl.BlockSpec((None, TK, tr), lambda g, c, k: (g, k, c)), r_spec],
        out_shape=[jax.ShapeDtypeStruct((ng * cdim, D), BF)],
        out_specs=[pl.BlockSpec((tr, D), lambda g, c, k: (g * nc + c, 0))],
        scratch_shapes=[pltpu.VMEM((tr, D), F32)])


def _mix_out_bwd(dh2, ya, yb, z, a1, q, lng, lnb, wsq, comm=None):
    def body(dh_ref, ya_ref, yb_ref, ga_ref, gb_ref, a1_ref, q_ref, lng_ref, lnb_ref, wa_ref, wb_ref, wo_ref,
             dzg_ref, da1_ref, dq_ref, l_ref, r_ref, s_ref):
        @pl.when(pl.program_id(0) == 0)
        def _():
            s_ref[...] = jnp.zeros_like(s_ref)

        dhb = dh_ref[...].astype(BF)
        dm = _nt(dhb, wo_ref[...])
        ya, yb = ya_ref[...].astype(F32), yb_ref[...].astype(F32)
        sa, sb = _sig(ga_ref[...].astype(F32)), _sig(gb_ref[...].astype(F32))
        l_ref[0] = (sa * ya + sb * yb).astype(BF)
        l_ref[2] = q_ref[...]
        dzg_ref[0] = (dm * ya * (sa * (1.0 - sa))).astype(BF)
        dzg_ref[1] = (dm * yb * (sb * (1.0 - sb))).astype(BF)
        dya = (dm * sa).astype(BF)
        dyb = (dm * sb).astype(BF)
        r_ref[0] = dhb
        r_ref[1] = dya
        r_ref[2] = dyb
        dq_ref[...] = _nt(dyb, wb_ref[...]).astype(BF)
        da3 = _nt(dya, wa_ref[...])
        lng = lng_ref[...]
        xh, rs, a2, sg = _layernorm_silu(a1_ref[...].astype(F32), lng, lnb_ref[...])
        l_ref[1] = (a2 * sg).astype(BF)
        da2 = da3 * (sg * (1.0 + a2 * (1.0 - sg)))
        s_ref[0:1, :] += jnp.sum(da2 * xh, axis=0, keepdims=True)
        s_ref[1:2, :] += jnp.sum(da2, axis=0, keepdims=True)
        dxh = da2 * lng
        da1 = rs * (dxh - jnp.mean(dxh, axis=-1, keepdims=True) - xh * jnp.mean(dxh * xh, axis=-1, keepdims=True))
        da1_ref[...] = da1.astype(BF)
        s_ref[2:3, :] += jnp.sum(da1, axis=0, keepdims=True)

    row = lambda i: (i, 0)
    row3 = lambda i: (0, i, 0)
    vec = pl.BlockSpec((1, D), lambda i: (0, 0))
    return _call(
        body, name="mix_out_bwd", grid=(T // TM,), args=[dh2, ya, yb, z, z, a1, q, lng, lnb, wsq, wsq, wsq], comm=comm,
        in_specs=[pl.BlockSpec((TM, D), row), pl.BlockSpec((TM, D), row), pl.BlockSpec((TM, D), row),
                  pl.BlockSpec((None, TM, D), lambda i: (5, i, 0)), pl.BlockSpec((None, TM, D), lambda i: (6, i, 0)),
                  pl.BlockSpec((TM, D), row), pl.BlockSpec((TM, D), row), vec, vec] + _square_specs((0, 1, 2)),
        out_shape=[jax.ShapeDtypeStruct((2, T, D), BF), jax.ShapeDtypeStruct((T, D), BF),
                   jax.ShapeDtypeStruct((T, D), BF), jax.ShapeDtypeStruct((3, T, D), BF),
                   jax.ShapeDtypeStruct((3, T, D), BF), jax.ShapeDtypeStruct((8, D), F32)],
        out_specs=[pl.BlockSpec((2, TM, D), row3), pl.BlockSpec((TM, D), row), pl.BlockSpec((TM, D), row),
                   pl.BlockSpec((3, TM, D), row3), pl.BlockSpec((3, TM, D), row3), pl.BlockSpec((8, D), lambda i: (0, 0))])


def _mix_in_bwd(dz, dh2, h1, gm, win, comm=None):
    def body(dz_ref, dh_ref, h_ref, g_ref, *rest):
        w_any, (o_ref, s_ref, w_s, sem) = rest[:len(win)], rest[len(win):]
        _load_in_proj([(b, first) for b, (_, first) in zip(w_any, win)], w_s, sem)

        @pl.when(pl.program_id(0) == 0)
        def _():
            s_ref[...] = jnp.zeros_like(s_ref)

        du = _nn(dz_ref[0], w_s[0:D, :])
        for j in range(1, NG):
            du = du + _nn(dz_ref[j], w_s[j * D:(j + 1) * D, :])
        dx, dg = _rmsnorm_bwd(h_ref[...], g_ref[...], du)
        o_ref[...] = dh_ref[...] + dx
        s_ref[0:1, :] += dg

    row = lambda i: (i, 0)
    return _call(
        body, name="mix_in_bwd", grid=(T // TM,), args=[dz, dh2, h1, gm] + [b for b, _ in win], comm=comm,
        in_specs=[pl.BlockSpec((NG, TM, D), lambda i: (0, i, 0)), pl.BlockSpec((TM, D), row),
                  pl.BlockSpec((TM, D), row), pl.BlockSpec((1, D), lambda i: (0, 0))] + [ANY] * len(win),
        out_shape=[jax.ShapeDtypeStruct((T, D), F32), jax.ShapeDtypeStruct((8, D), F32)],
        out_specs=[pl.BlockSpec((TM, D), row), pl.BlockSpec((8, D), lambda i: (0, 0))],
        scratch_shapes=[pltpu.VMEM((NG * D, D), BF), pltpu.SemaphoreType.DMA((NDEV * len(win),))])


def _row_tile(n, want, mult):
    for t in range(min(want, n), 0, -1):
        if n % t == 0 and t % mult == 0:
            return t
    return n


def _sum_slots(recv, name):
    ns, rows, cols = recv.shape
    tr = _row_tile(rows, 1024, 16)

    def body(r_ref, o_ref):
        s = r_ref[0].astype(F32)
        for k in range(1, ns):
            s = s + r_ref[k].astype(F32)
        o_ref[...] = s

    return _call(
        body, name=name, grid=(rows // tr,), args=[recv],
        in_specs=[pl.BlockSpec((ns, tr, cols), lambda i: (0, i, 0))],
        out_shape=[jax.ShapeDtypeStruct((rows, cols), F32)],
        out_specs=[pl.BlockSpec((tr, cols), lambda i: (i, 0))])[0]


def _pack_small(s_ffn1, s_in, s_mix, s_ffn2, s_final, dwa, dwb):
    def body(f1, mi, mo, f2, fl, wa_ref, wb_ref, v_ref, k_ref):
        for dst, (ref, row) in enumerate(((f1, 0), (mi, 0), (mo, 0), (mo, 1), (mo, 2), (f2, 0), (fl, 0), (fl, 1))):
            v_ref[dst:dst + 1, :] = ref[row:row + 1, :]
        for k in range(NDEV):
            k_ref[k, 0:32, :] = wa_ref[:, k * LANE:(k + 1) * LANE]
            k_ref[k, 32:40, :] = wb_ref[:, k * LANE:(k + 1) * LANE]

    return pl.pallas_call(
        body, name="pack_small",
        out_shape=(jax.ShapeDtypeStruct((8, D), F32), jax.ShapeDtypeStruct((NDEV, 40, LANE), F32)),
    )(s_ffn1, s_in, s_mix, s_ffn2, s_final, dwa, dwb)


def _sum_small(vecs, convs):
    def body(v_ref, k_ref, vs_ref, ks_ref, l_ref):
        s, c = v_ref[0], k_ref[0]
        for k in range(1, NDEV):
            s = s + v_ref[k]
            c = c + k_ref[k]
        vs_ref[...] = s
        ks_ref[...] = c
        l_ref[...] = jnp.broadcast_to(jnp.sum(s[7:8, :], axis=-1, keepdims=True), (8, LANE))

    return pl.pallas_call(
        body, name="sum_small",
        out_shape=(jax.ShapeDtypeStruct((8, D), F32), jax.ShapeDtypeStruct((40, LANE), F32),
                   jax.ShapeDtypeStruct((8, LANE), F32)),
    )(vecs, convs)


def _adam(gs, ws, ms, vs, name, comm=None):
    n = len(gs)
    rows, cols = ws[0].shape
    tr = _row_tile(rows, 256, 16)
    c1 = 1.0 - ADAM_B1 ** ADAM_STEP
    c2 = 1.0 - ADAM_B2 ** ADAM_STEP
    summed = [isinstance(g, tuple) for g in gs]

    def body(*refs):
        for i in range(n):
            g_in, w, m, v = refs[4 * i], refs[4 * i + 1][...], refs[4 * i + 2][...], refs[4 * i + 3][...]
            g_ref, d_ref, m_ref, v_ref = refs[4 * n + 4 * i: 4 * n + 4 * i + 4]
            if summed[i]:
                g = g_in[0].astype(F32)
                for k in range(1, g_in.shape[0]):
                    g = g + g_in[k].astype(F32)
            else:
                g = g_in[...]
            g_ref[...] = g
            m2 = ADAM_B1 * m + (1.0 - ADAM_B1) * g
            v2 = ADAM_B2 * v + (1.0 - ADAM_B2) * (g * g)
            d_ref[...] = -ADAM_LR * ((m2 / c1) / (jnp.sqrt(v2 / c2) + ADAM_EPS) + ADAM_WD * w)
            m_ref[...] = m2
            v_ref[...] = v2

    spec = pl.BlockSpec((tr, cols), lambda i: (i, 0))
    args, in_specs = [], []
    for i in range(n):
        if summed[i]:
            slots, first = gs[i]
            args.append(slots)
            in_specs.append(pl.BlockSpec((slots.shape[0], tr, cols), lambda i, b=first // tr: (0, b + i, 0)))
        else:
            args.append(gs[i])
            in_specs.append(spec)
        args += [ws[i], ms[i], vs[i]]
        in_specs += [spec] * 3
    outs = _call(body, name=name, grid=(rows // tr,), args=args, comm=comm, in_specs=in_specs,
                 out_shape=[jax.ShapeDtypeStruct((rows, cols), F32)] * (4 * n), out_specs=[spec] * (4 * n))
    return [tuple(outs[4 * i: 4 * i + 4]) for i in range(n)], outs[4 * n:]


def kernel(x, ffn1_norm, ffn1_w_gate, ffn1_w_up, ffn1_w_down, mix_norm, w_in, a_dw_w, a_dw_b, a_ln_g, a_ln_b, a_w_out, b_conv_w, b_w_out, w_o, ffn2_norm, ffn2_w_gate, ffn2_w_up, ffn2_w_down, final_norm, loss_target, m_ffn1_norm, m_ffn1_w_gate, m_ffn1_w_up, m_ffn1_w_down, m_mix_norm, m_w_in, m_a_dw_w, m_a_dw_b, m_a_ln_g, m_a_ln_b, m_a_w_out, m_b_conv_w, m_b_w_out, m_w_o, m_ffn2_norm, m_ffn2_w_gate, m_ffn2_w_up, m_ffn2_w_down, m_final_norm, v_ffn1_norm, v_ffn1_w_gate, v_ffn1_w_up, v_ffn1_w_down, v_mix_norm, v_w_in, v_a_dw_w, v_a_dw_b, v_a_ln_g, v_a_ln_b, v_a_w_out, v_b_conv_w, v_b_w_out, v_w_o, v_ffn2_norm, v_ffn2_w_gate, v_ffn2_w_up, v_ffn2_w_down, v_final_norm):
    names = ("ffn1_norm", "ffn1_w_gate", "ffn1_w_up", "ffn1_w_down", "mix_norm", "w_in", "a_dw_w", "a_dw_b",
             "a_ln_g", "a_ln_b", "a_w_out", "b_conv_w", "b_w_out", "w_o", "ffn2_norm", "ffn2_w_gate", "ffn2_w_up",
             "ffn2_w_down", "final_norm")
    w = dict(ffn1_norm=ffn1_norm, ffn1_w_gate=ffn1_w_gate, ffn1_w_up=ffn1_w_up, ffn1_w_down=ffn1_w_down,
             mix_norm=mix_norm, w_in=w_in, a_dw_w=a_dw_w, a_dw_b=a_dw_b, a_ln_g=a_ln_g, a_ln_b=a_ln_b,
             a_w_out=a_w_out, b_conv_w=b_conv_w, b_w_out=b_w_out, w_o=w_o, ffn2_norm=ffn2_norm,
             ffn2_w_gate=ffn2_w_gate, ffn2_w_up=ffn2_w_up, ffn2_w_down=ffn2_w_down, final_norm=final_norm)
    m = dict(ffn1_norm=m_ffn1_norm, ffn1_w_gate=m_ffn1_w_gate, ffn1_w_up=m_ffn1_w_up, ffn1_w_down=m_ffn1_w_down,
             mix_norm=m_mix_norm, w_in=m_w_in, a_dw_w=m_a_dw_w, a_dw_b=m_a_dw_b, a_ln_g=m_a_ln_g, a_ln_b=m_a_ln_b,
             a_w_out=m_a_w_out, b_conv_w=m_b_conv_w, b_w_out=m_b_w_out, w_o=m_w_o, ffn2_norm=m_ffn2_norm,
             ffn2_w_gate=m_ffn2_w_gate, ffn2_w_up=m_ffn2_w_up, ffn2_w_down=m_ffn2_w_down, final_norm=m_final_norm)
    v = dict(ffn1_norm=v_ffn1_norm, ffn1_w_gate=v_ffn1_w_gate, ffn1_w_up=v_ffn1_w_up, ffn1_w_down=v_ffn1_w_down,
             mix_norm=v_mix_norm, w_in=v_w_in, a_dw_w=v_a_dw_w, a_dw_b=v_a_dw_b, a_ln_g=v_a_ln_g, a_ln_b=v_a_ln_b,
             a_w_out=v_a_w_out, b_conv_w=v_b_conv_w, b_w_out=v_b_w_out, w_o=v_w_o, ffn2_norm=v_ffn2_norm,
             ffn2_w_gate=v_ffn2_w_gate, ffn2_w_up=v_ffn2_w_up, ffn2_w_down=v_ffn2_w_down, final_norm=v_final_norm)
    flat = _pack_weights(dict(wg1=ffn1_w_gate[0].T, wu1=ffn1_w_up[0].T, wd1=ffn1_w_down[0], wg2=ffn2_w_gate[0].T,
                              wu2=ffn2_w_up[0].T, wd2=ffn2_w_down[0], win=w_in[0], wa=a_w_out[0], wb=b_w_out[0],
                              wo=w_o[0]))
    cw_shard = jnp.concatenate([a_dw_w[0], jnp.zeros((1, LANE), F32), b_conv_w[0], jnp.zeros((5, LANE), F32)], axis=0)

    x2, tgt = x[0], loss_target[0]
    st_a, st_b, st_b2 = ("wg1", "wu1"), ("wd1", "win/0/2"), ("win/1/2",)
    st_c, st_d, st_e = ("wa", "wb", "wo", "wg2"), ("wu2",), ("wd2",)

    buf_a, cw = _run_comm(_join(_ag_comm(st_a, flat), _direct_comm(cw_shard, False)), "ag_ffn1")
    n1, gg1, uu1, act1, buf_b = _ffn_gate_up(x2, ffn1_norm, (buf_a, buf_a), (0, F), "ffn1_gate_up", _ag_comm(st_b, flat))
    h1, buf_b2 = _ffn_down(x2, act1, buf_b, 0, "ffn1_down", _ag_comm(st_b2, flat))
    win = ((buf_b, F), (buf_b2, 0))
    u, z, buf_c = _mix_in(h1, mix_norm, win, _ag_comm(st_c, flat))
    dft = _dft_constants()
    cw = jnp.transpose(cw, (1, 0, 2)).reshape(40, D)
    a1, q, buf_d = _conv_fwd_dft(z, cw, a_dw_b, dft, _ag_comm(st_d, flat))
    h2, ya, yb, buf_e = _mix_out(a1, q, z, h1, a_ln_g, a_ln_b, buf_c, _ag_comm(st_e, flat))
    ffn2_bufs, ffn2_offs = (buf_c, buf_d, buf_e), (3 * D, 0, 0)
    dh3, s_final, n2, gg2, uu2, act2 = _ffn_fwd(h2, ffn2_norm, ffn2_bufs, ffn2_offs, "ffn2_fwd",
                                          final=(final_norm.reshape(1, D), tgt))

    tr_f = F // 2 if (F // 2) % LANE == 0 else F
    def pair(stage, src):
        return _rs_pair_comm(stage, src)

    def chip(stage, src, pair_buf, tag):
        return _rs_chip_comm(_pair_add(stage, src, pair_buf, "pair_add_" + tag))

    (dgu2,) = _ffn_bwd_hidden(dh3, gg2, uu2, buf_e, 0, "ffn2_bwd_h")
    (gu2,) = _tn_matmul(dgu2, n2, tr_f, "dw_gu2")
    s2a, src2a = ("wg2", "wu2"), dict(wg2=(gu2, 0), wu2=(gu2, F))
    gd2, pair2a = _tn_matmul(act2, dh3, tr_f, "dw_d2", pair(s2a, src2a), scale=0.5)
    s2b, src2b = ("wd2",), dict(wd2=(gd2, 0))
    dh2, s_ffn2, pair2b = _ffn_bwd_input(dgu2, dh3, h2, ffn2_norm, (buf_c, buf_d), (3 * D, 0), "ffn2_bwd_x",
                                         pair(s2b, src2b))
    dzg, da1, dq, lsq, rsq, s_mix, recv2b = _mix_out_bwd(dh2, ya, yb, z, a1, q, a_ln_g, a_ln_b, buf_c,
                                                          chip(s2b, src2b, pair2b, "2b"))
    (gsq,) = _tn_matmul(lsq, rsq, D, "dw_square")
    ssq, srcsq = ("wa", "wb", "wo"), dict(wa=(gsq, D), wb=(gsq, 2 * D), wo=(gsq, 0))
    dz, dwa, dwb, recv2a, pairsq = _conv_bwd_dft(z, da1, dq, dzg, cw, dft,
                                                 _join(chip(s2a, src2a, pair2a, "2a"), pair(ssq, srcsq)))
    gin, recvsq = _tn_matmul(dz, u, D, "dw_in", chip(ssq, srcsq, pairsq, "sq"))
    sin_a, sin_b, srcin = ("win/0/2",), ("win/1/2",), {"win/0/2": (gin, 0), "win/1/2": (gin, 0)}
    dh1, s_in, pairin_a, pairin_b = _mix_in_bwd(dz, dh2, h1, mix_norm, win,
                                                _join(pair(sin_a, srcin), pair(sin_b, srcin)))
    dgu1, recvin_a = _ffn_bwd_hidden(dh1, gg1, uu1, buf_b, 0, "ffn1_bwd_h",
                                           chip(sin_a, srcin, pairin_a, "in_a"))
    gu1, recvin_b = _tn_matmul(dgu1, n1, tr_f, "dw_gu1", chip(sin_b, srcin, pairin_b, "in_b"))
    s1a, src1a = ("wg1", "wu1"), dict(wg1=(gu1, 0), wu1=(gu1, F))
    gd1, pair1a = _tn_matmul(act1, dh1, tr_f, "dw_d1", pair(s1a, src1a), scale=0.5)
    s1b, src1b = ("wd1",), dict(wd1=(gd1, 0))
    dx, s_ffn1, recv1a, pair1b = _ffn_bwd_input(dgu1, dh1, x2, ffn1_norm, (buf_a, buf_a), (0, F), "ffn1_bwd_x",
                                                _join(chip(s1a, src1a, pair1a, "1a"), pair(s1b, src1b)))
    win_sum = jnp.concatenate([_sum_slots(recvin_a, "sum_in_a"), _sum_slots(recvin_b, "sum_in_b")], axis=0)

    vec8, convk = _pack_small(s_ffn1, s_in, s_mix, s_ffn2, s_final, dwa, dwb)
    recv1b, vec_all, conv_all = _run_comm(
        _join(chip(s1b, src1b, pair1b, "1b"), _join(_direct_comm(vec8, False), _direct_comm(convk, True))), "xchg_tail")
    vec_sum, conv_sum, loss_blk = _sum_small(vec_all, conv_all)
    loss = loss_blk[0, 0]

    fs = F // NDEV
    g = dict(ffn1_w_gate=(recv1a, 0), ffn1_w_up=(recv1a, fs), ffn2_w_gate=(recv2a, 0), ffn2_w_up=(recv2a, fs),
             ffn1_w_down=(recv1b, 0),
             ffn2_w_down=(recv2b, 0), a_w_out=(recvsq, 0), b_w_out=(recvsq, D // NDEV), w_o=(recvsq, 2 * (D // NDEV)),
             w_in=win_sum.T, ffn1_norm=vec_sum[0:1], mix_norm=vec_sum[1:2], a_ln_g=vec_sum[2:3], a_ln_b=vec_sum[3:4],
             a_dw_b=vec_sum[4:5], ffn2_norm=vec_sum[5:6], final_norm=vec_sum[6:7],
             a_dw_w=conv_sum[0:KA], b_conv_w=conv_sum[32:32 + KB])
    grad, upd = {}, {}

    def run(group, name, as2d=lambda a: a[0], back=lambda a, n: a.reshape(w[n].shape)):
        res, _ = _adam([g[n] for n in group], [as2d(w[n]) for n in group], [as2d(m[n]) for n in group],
                       [as2d(v[n]) for n in group], name)
        for n, r in zip(group, res):
            grad[n], upd[n] = back(r[0], n), tuple(back(a, n) for a in r[1:])

    run(("ffn1_w_gate", "ffn1_w_up", "ffn2_w_gate", "ffn2_w_up"), "adam_gate_up",
        as2d=lambda a: a[0].T, back=lambda a, n: a.T[None])
    run(("ffn1_w_down", "ffn2_w_down"), "adam_down")
    run(("w_in",), "adam_in")
    run(("a_w_out", "b_w_out", "w_o"), "adam_square")
    run(("a_dw_w",), "adam_dw")
    run(("b_conv_w",), "adam_conv")
    run(("ffn1_norm", "mix_norm", "a_dw_b", "a_ln_g", "a_ln_b", "ffn2_norm", "final_norm"), "adam_vec",
        as2d=lambda a: a.reshape(1, D))

    return (loss, dx.reshape(x.shape), *[grad[n] for n in names], *[upd[n][0] for n in names],
            *[upd[n][1] for n in names], *[upd[n][2] for n in names])
```

```python
import jax
import jax.numpy as jnp
from jax import lax
from jax.experimental import pallas as pl
from jax.experimental.pallas import tpu as pltpu

T = 4096
D = 1024
F = 2816
NG = 7
NDEV = 8
NCHIP = 4
KA, KB = 31, 3
EPS = 1e-6
ADAM_LR, ADAM_B1, ADAM_B2, ADAM_EPS, ADAM_WD, ADAM_STEP = 0.001, 0.9, 0.999, 1e-08, 0.01, 10

TM = 512
FC = 256
TB = 1024
NB = 256
HB = NB // 2
CW = 256
CHB = 64
LANE = 128
TK = 2048
VMEM_LIMIT = 56 * 1024 * 1024

BF = jnp.bfloat16
F32 = jnp.float32
MESH = pl.DeviceIdType.MESH
ANY = pl.BlockSpec(memory_space=pl.ANY)

ORDER = ("wg1", "wu1", "wd1", "wg2", "wu2", "wd2", "win", "wa", "wb", "wo")


class _Layout:
    def __init__(self):
        fs, dis, ds = F // NDEV, NG * D // NDEV, D // NDEV
        self.rows = dict(wg1=fs, wu1=fs, wd1=fs, wg2=fs, wu2=fs, wd2=fs, win=dis, wa=ds, wb=ds, wo=ds)
        self.fl, off = {}, 0
        for n in ORDER:
            self.fl[n] = off
            off += self.rows[n]
        self.RT = off


class _Stage:
    def __init__(self, names):
        lay = _Layout()
        self.names = names
        self.rows, self.full, self.sub, self.fl = {}, {}, {}, {}
        for n in names:
            base, i, k = (n.split("/") + ["0", "1"])[:3]
            self.full[n] = lay.rows[base]
            self.rows[n] = lay.rows[base] // int(k)
            self.sub[n] = int(i) * self.rows[n]
            self.fl[n] = lay.fl[base] + self.sub[n]
        self.off, self.wc, o, w = {}, {}, 0, 0
        for n in names:
            self.off[n], self.wc[n] = o, w
            o += self.rows[n]
            w += NDEV * self.rows[n]
        self.R, self.W = o, w

    def grad_row(self, n, first, dev_lin):
        return first + dev_lin * self.full[n] + self.sub[n]


def _nt(a, b):
    return lax.dot_general(a, b, (((1,), (1,)), ((), ())), preferred_element_type=F32)


def _nn(a, b):
    return lax.dot_general(a, b, (((1,), (0,)), ((), ())), preferred_element_type=F32)


def _tn(a, b):
    return lax.dot_general(a, b, (((0,), (0,)), ((), ())), preferred_element_type=F32)


def _sig(x):
    return 1.0 / (1.0 + jnp.exp(-x))


def _position():
    return lax.axis_index("x"), lax.axis_index("y"), lax.axis_index("c")


def _peer(pos, j):
    x, y, c = pos
    return (1 - x if j & 4 else x, 1 - y if j & 2 else y, 1 - c if j & 1 else c)


def _lin(pos):
    return 4 * pos[0] + 2 * pos[1] + pos[2]


def _chip(pos):
    return 2 * pos[0] + pos[1]


class _Comm:
    def __init__(self, inputs, out_shapes, scratch, start, finish, middle=None):
        self.inputs, self.out_shapes, self.scratch = inputs, out_shapes, scratch
        self.start, self.finish, self.middle = start, finish, middle


def _call(body, *, name, grid, args, in_specs, out_shape, out_specs, scratch_shapes=(), comm=None,
          num_scalar_prefetch=0):
    in_specs, out_shape, out_specs, scratch_shapes = list(in_specs), list(out_shape), list(out_specs), list(scratch_shapes)
    n_in, n_out, n_scr = len(in_specs), len(out_shape), len(scratch_shapes)
    sp = num_scalar_prefetch
    if comm is None:
        kernel_fn = lambda *refs: body(*refs)
        c_in = c_out = c_scr = 0
    else:
        c_in, c_out, c_scr = len(comm.inputs), len(comm.out_shapes), len(comm.scratch)

        def kernel_fn(*refs):
            pre, refs = refs[:sp], refs[sp:]
            ins, cins = refs[:n_in], refs[n_in:n_in + c_in]
            o0 = n_in + c_in
            outs, couts = refs[o0:o0 + n_out], refs[o0 + n_out:o0 + n_out + c_out]
            s0 = o0 + n_out + c_out
            scr, cscr = refs[s0:s0 + n_scr], refs[s0 + n_scr:]
            step, steps = pl.program_id(0), grid[0]
            for a in range(1, len(grid)):
                step, steps = step * grid[a] + pl.program_id(a), steps * grid[a]
            first, last = step == 0, step == steps - 1

            @pl.when(first)
            def _():
                comm.start(cins, couts, cscr)

            if comm.middle is not None:
                @pl.when(step == (steps // 2 if steps > 2 else steps - 1))
                def _():
                    comm.middle(cins, couts, cscr)

            body(*pre, *ins, *outs, *scr)

            @pl.when(last)
            def _():
                comm.finish(cins, couts, cscr)

        args = list(args) + list(comm.inputs)
        in_specs += [ANY] * c_in
        out_shape += list(comm.out_shapes)
        out_specs += [ANY] * c_out
        scratch_shapes += list(comm.scratch)
    params = pltpu.CompilerParams(dimension_semantics=("arbitrary",) * len(grid), vmem_limit_bytes=VMEM_LIMIT)
    if sp:
        grid_spec = pltpu.PrefetchScalarGridSpec(num_scalar_prefetch=sp, grid=grid, in_specs=in_specs,
                                                 out_specs=out_specs, scratch_shapes=scratch_shapes)
        return pl.pallas_call(kernel_fn, name=name, grid_spec=grid_spec, out_shape=out_shape,
                              compiler_params=params)(*args)
    return pl.pallas_call(kernel_fn, name=name, grid=grid, in_specs=in_specs, out_shape=out_shape, out_specs=out_specs,
                          scratch_shapes=scratch_shapes, compiler_params=params)(*args)


def _join(a, b):
    na = (len(a.inputs), len(a.out_shapes), len(a.scratch))

    def split(refs):
        return ([r[:n] for r, n in zip(refs, na)], [r[n:] for r, n in zip(refs, na)])

    def start(*refs):
        ra, rb = split(refs)
        a.start(*ra)
        b.start(*rb)

    def finish(*refs):
        ra, rb = split(refs)
        a.finish(*ra)
        b.finish(*rb)

    def middle(*refs):
        for stage, r in zip((a, b), split(refs)):
            if stage.middle is not None:
                stage.middle(*r)

    return _Comm(list(a.inputs) + list(b.inputs), list(a.out_shapes) + list(b.out_shapes),
                 list(a.scratch) + list(b.scratch), start, finish,
                 middle if (a.middle is not None or b.middle is not None) else None)


def _run_comm(comm, name):
    def body(*refs):
        c_in, c_out = len(comm.inputs), len(comm.out_shapes)
        parts = (refs[:c_in], refs[c_in:c_in + c_out], refs[c_in + c_out:])
        comm.start(*parts)
        if comm.middle is not None:
            comm.middle(*parts)
        comm.finish(*parts)

    return pl.pallas_call(
        body, name=name, out_shape=list(comm.out_shapes), in_specs=[ANY] * len(comm.inputs),
        out_specs=[ANY] * len(comm.out_shapes), scratch_shapes=list(comm.scratch))(*comm.inputs)


def _ag_comm(names, flat):
    st = _Stage(names)

    def ring(me):
        x, y, c = me
        diagonal = x == y
        up = (jnp.where(diagonal, x, 1 - x), jnp.where(diagonal, 1 - y, y), c)
        down = (jnp.where(diagonal, 1 - x, x), jnp.where(diagonal, y, 1 - y), c)
        low = c == 0
        passed = tuple(jnp.where(low, d, u) for d, u in zip(down, up))
        target = tuple(jnp.where(low, u, d) for d, u in zip(down, up))
        return up, down, (1 - x, 1 - y, c), passed, target

    def parts(refs):
        (flat_ref,), (out_ref,), (send_sems, recv_sems, local_sem) = refs
        me = _position()

        def region(name, dev):
            r = st.rows[name]
            return out_ref.at[pl.ds(st.wc[name] + _lin(dev) * r, r), :]

        def own(name):
            return flat_ref.at[pl.ds(st.fl[name], st.rows[name]), :]

        def copies(k, dev, to, from_flat):
            return [pltpu.make_async_remote_copy(
                src_ref=own(n) if from_flat else region(n, dev), dst_ref=region(n, dev), send_sem=send_sems.at[k],
                recv_sem=recv_sems.at[k], device_id=to, device_id_type=MESH) for n in names]

        def whole(k):
            return pltpu.make_async_remote_copy(
                src_ref=flat_ref.at[pl.ds(0, st.R), :], dst_ref=out_ref.at[pl.ds(0, st.R), :],
                send_sem=send_sems.at[k], recv_sem=recv_sems.at[k], device_id=me, device_id_type=MESH)

        return me, region, own, copies, whole, flat_ref, out_ref, local_sem

    def start(*refs):
        me, region, own, copies, _, _, _, local_sem = parts(refs)
        for n in names:
            pltpu.make_async_copy(own(n), region(n, me), local_sem).start()
        up, down, _, _, _ = ring(me)
        for k, to in ((1, up), (2, down), (0, _peer(me, 1))):
            for cp in copies(k, me, to, True):
                cp.start()

    def middle(*refs):
        me, _, _, copies, whole, _, _, _ = parts(refs)
        up, down, _, passed, target = ring(me)
        sib = _peer(me, 1)
        whole(1).wait_recv()
        whole(2).wait_recv()
        for k, dev, to in ((3, passed, target), (4, down, sib), (5, up, sib)):
            for cp in copies(k, dev, to, False):
                cp.start()

    def finish(*refs):
        me, _, _, copies, whole, flat_ref, out_ref, local_sem = parts(refs)
        _, _, across, _, _ = ring(me)
        whole(3).wait_recv()
        for cp in copies(6, across, _peer(me, 1), False):
            cp.start()
        whole(0).wait_recv()
        for j in range(3):
            whole(4 + j).wait_recv()
        for k in range(7):
            whole(k).wait_send()
        pltpu.make_async_copy(flat_ref.at[pl.ds(0, st.R), :], out_ref.at[pl.ds(0, st.R), :], local_sem).wait()

    return _Comm([flat], [jax.ShapeDtypeStruct((st.W, D), BF)],
                 [pltpu.SemaphoreType.DMA((7,)), pltpu.SemaphoreType.DMA((7,)), pltpu.SemaphoreType.DMA],
                 start, finish, middle)


def _rs_pair_comm(names, src):
    st = _Stage(names)
    arrays = []
    for n in names:
        if not any(src[n][0] is a for a in arrays):
            arrays.append(src[n][0])
    idx = {n: [i for i, a in enumerate(arrays) if a is src[n][0]][0] for n in names}

    def slot_wait(refs):
        recv = refs[1][0]
        send_sem, recv_sem = refs[2]
        return pltpu.make_async_remote_copy(src_ref=recv, dst_ref=recv, send_sem=send_sem, recv_sem=recv_sem,
                                            device_id=_position(), device_id_type=MESH)

    def start(*refs):
        ins, (recv,), (send_sem, recv_sem) = refs
        me = _position()
        sib = _peer(me, 1)
        for q in range(NCHIP):
            dev = (q // 2, q % 2, sib[2])
            for n in names:
                r = st.rows[n]
                pltpu.make_async_remote_copy(
                    src_ref=ins[idx[n]].at[pl.ds(st.grad_row(n, src[n][1], _lin(dev)), r), :],
                    dst_ref=recv.at[q, pl.ds(st.off[n], r), :], send_sem=send_sem, recv_sem=recv_sem,
                    device_id=sib, device_id_type=MESH).start()

    def finish(*refs):
        w = slot_wait(refs)
        w.wait_recv()
        w.wait_send()

    return _Comm(arrays, [jax.ShapeDtypeStruct((NCHIP, st.R, D), BF)],
                 [pltpu.SemaphoreType.DMA, pltpu.SemaphoreType.DMA], start, finish)


def _pair_add(names, src, recv, name):
    st = _Stage(names)
    c_arr = jnp.reshape(lax.axis_index("c"), (1,)).astype(jnp.int32)

    def body(c_ref, *refs):
        r_ref, o_ref = refs[len(names)], refs[len(names) + 1]
        for a_ref, n in zip(refs, names):
            rows = slice(st.off[n], st.off[n] + st.rows[n])
            o_ref[rows, :] = (a_ref[...].astype(F32) + r_ref[rows, :].astype(F32)).astype(BF)

    def shard_spec(n):
        r = st.rows[n]
        base, step = st.grad_row(n, src[n][1], 0) // r, st.full[n] // r
        return pl.BlockSpec((r, D), lambda q, c_ref: (base + step * (2 * q + c_ref[0]), 0))

    slot = pl.BlockSpec((None, st.R, D), lambda q, c_ref: (q, 0, 0))
    return _call(body, name=name, grid=(NCHIP,), args=[c_arr] + [src[n][0] for n in names] + [recv],
                 in_specs=[shard_spec(n) for n in names] + [slot],
                 out_shape=[jax.ShapeDtypeStruct((NCHIP, st.R, D), BF)], out_specs=[slot], num_scalar_prefetch=1)[0]


def _rs_chip_comm(part):
    def copies(refs):
        (p_ref,), (recv,), (send_sems, recv_sems, local_sem) = refs
        me = _position()
        mine = pltpu.make_async_copy(p_ref.at[_chip(me)], recv.at[_chip(me)], local_sem)
        out = []
        for j, bits in enumerate((4, 2, 6)):
            to = _peer(me, bits)
            out.append(pltpu.make_async_remote_copy(
                src_ref=p_ref.at[_chip(to)], dst_ref=recv.at[_chip(me)], send_sem=send_sems.at[j],
                recv_sem=recv_sems.at[j], device_id=to, device_id_type=MESH))
        return mine, out

    def start(*refs):
        mine, out = copies(refs)
        mine.start()
        for cp in out:
            cp.start()

    def finish(*refs):
        mine, out = copies(refs)
        for cp in out:
            cp.wait_recv()
        for cp in out:
            cp.wait_send()
        mine.wait()

    return _Comm([part], [jax.ShapeDtypeStruct(part.shape, BF)],
                 [pltpu.SemaphoreType.DMA((3,)), pltpu.SemaphoreType.DMA((3,)), pltpu.SemaphoreType.DMA],
                 start, finish)


def _direct_comm(x, scatter):
    def copies(refs):
        (x_ref,), (out_ref,), (send_sems, recv_sems, local_sem) = refs
        me = _position()

        def piece(dev):
            return x_ref.at[_lin(dev)] if scatter else x_ref

        mine = pltpu.make_async_copy(piece(me), out_ref.at[_lin(me)], local_sem)
        return mine, [pltpu.make_async_remote_copy(
            src_ref=piece(_peer(me, j)), dst_ref=out_ref.at[_lin(me)], send_sem=send_sems.at[j - 1],
            recv_sem=recv_sems.at[j - 1], device_id=_peer(me, j), device_id_type=MESH) for j in range(1, NDEV)]

    def start(*refs):
        mine, cps = copies(refs)
        mine.start()
        for cp in cps:
            cp.start()

    def finish(*refs):
        mine, cps = copies(refs)
        for cp in cps:
            cp.wait_recv()
        for cp in cps:
            cp.wait_send()
        mine.wait()

    shape = x.shape if scatter else (NDEV,) + x.shape
    return _Comm([x], [jax.ShapeDtypeStruct(shape, x.dtype)],
                 [pltpu.SemaphoreType.DMA((7,)), pltpu.SemaphoreType.DMA((7,)), pltpu.SemaphoreType.DMA],
                 start, finish)


def _pack_weights(shards):
    lay = _Layout()

    def body(*refs):
        o_ref = refs[-1]
        for ref, n in zip(refs, ORDER):
            x = ref[...].T if n == "win" else ref[...]
            o_ref[lay.fl[n]:lay.fl[n] + lay.rows[n], :] = x.astype(BF)

    return pl.pallas_call(
        body, name="pack_weights", out_shape=jax.ShapeDtypeStruct((lay.RT, D), BF),
        compiler_params=pltpu.CompilerParams(vmem_limit_bytes=VMEM_LIMIT))(*[shards[n] for n in ORDER])


def _load_ffn_weights(srcs, offs, scratch, sem):
    @pl.when(pl.program_id(0) == 0)
    def _():
        cps = [pltpu.make_async_copy(s.at[pl.ds(off, dst.shape[0]), :], dst, sem.at[i])
               for i, (s, off, dst) in enumerate(zip(srcs, offs, scratch))]
        for cp in cps:
            cp.start()
        for cp in cps:
            cp.wait()


def _final_loss_tile(xf, g, tgt, s_ref):
    r = lax.rsqrt(jnp.mean(xf * xf, axis=-1, keepdims=True) + EPS)
    xr = xf * r
    e = xr * g - tgt
    s_ref[1:2, :] += jnp.sum(e * e, axis=0, keepdims=True) * (0.5 / D)
    dy = e * (1.0 / D)
    s_ref[0:1, :] += jnp.sum(dy * xr, axis=0, keepdims=True)
    gdy = dy * g
    return r * gdy - xr * (r * jnp.mean(gdy * xr, axis=-1, keepdims=True))


def _ffn_fwd(x, g, wbufs, offs, name, comm=None, final=None):
    nf = F // FC

    def body(x_ref, g_ref, b0, b1, b2, *rest):
        if final is None:
            h_ref, n_ref, gg_ref, uu_ref, a_ref, wg_s, wu_s, wd_s, sem = rest
        else:
            gf_ref, t_ref, dh_ref, dhb_ref, s_ref, n_ref, gg_ref, uu_ref, a_ref, wg_s, wu_s, wd_s, sem = rest

            @pl.when(pl.program_id(0) == 0)
            def _():
                s_ref[...] = jnp.zeros_like(s_ref)

        _load_ffn_weights((b0, b1, b2), offs, (wg_s, wu_s, wd_s), sem)
        xf = x_ref[...]
        r = lax.rsqrt(jnp.mean(xf * xf, axis=-1, keepdims=True) + EPS)
        nb = (xf * r * g_ref[...]).astype(BF)
        n_ref[...] = nb
        acc = jnp.zeros((TM, D), F32)
        for c in range(nf):
            sl = slice(c * FC, (c + 1) * FC)
            gb = _nt(nb, wg_s[sl, :]).astype(BF)
            ub = _nt(nb, wu_s[sl, :]).astype(BF)
            gg_ref[:, sl] = gb
            uu_ref[:, sl] = ub
            a = (gb * _sig(gb)) * ub
            a_ref[0, :, sl] = a
            acc = acc + _nn(a, wd_s[sl, :])
        h = xf + 0.5 * acc
        if final is None:
            h_ref[...] = h
        else:
            dh = _final_loss_tile(h, gf_ref[...], t_ref[...], s_ref)
            dh_ref[...] = dh
            dhb_ref[...] = (0.5 * dh).astype(BF)

    row = lambda i: (i, 0)
    vec = pl.BlockSpec((1, D), lambda i: (0, 0))
    tile = pl.BlockSpec((TM, D), row)
    saved_shapes = [jax.ShapeDtypeStruct((T, D), BF), jax.ShapeDtypeStruct((T, F), BF), jax.ShapeDtypeStruct((T, F), BF),
                    jax.ShapeDtypeStruct((1, T, F), BF)]
    saved_specs = [tile, pl.BlockSpec((TM, F), row), pl.BlockSpec((TM, F), row),
                   pl.BlockSpec((1, TM, F), lambda i: (0, i, 0))]
    if final is None:
        extra_args, extra_specs = [], []
        head_shapes, head_specs = [jax.ShapeDtypeStruct((T, D), F32)], [tile]
    else:
        extra_args, extra_specs = list(final), [vec, tile]
        head_shapes = [jax.ShapeDtypeStruct((T, D), F32), jax.ShapeDtypeStruct((T, D), BF), jax.ShapeDtypeStruct((8, D), F32)]
        head_specs = [tile, tile, pl.BlockSpec((8, D), lambda i: (0, 0))]
    return _call(
        body, name=name, grid=(T // TM,), args=[x, g, *wbufs, *extra_args], comm=comm,
        in_specs=[tile, vec, ANY, ANY, ANY] + extra_specs,
        out_shape=head_shapes + saved_shapes, out_specs=head_specs + saved_specs,
        scratch_shapes=[pltpu.VMEM((F, D), BF)] * 3 + [pltpu.SemaphoreType.DMA((3,))])


def _ffn_gate_up(x, g, wbufs, offs, name, comm=None):
    nf = F // FC

    def body(x_ref, g_ref, b0, b1, n_ref, gg_ref, uu_ref, a_ref, wg_s, wu_s, sem):
        _load_ffn_weights((b0, b1), offs, (wg_s, wu_s), sem)
        xf = x_ref[...]
        r = lax.rsqrt(jnp.mean(xf * xf, axis=-1, keepdims=True) + EPS)
        nb = (xf * r * g_ref[...]).astype(BF)
        n_ref[...] = nb
        for c in range(nf):
            sl = slice(c * FC, (c + 1) * FC)
            gb = _nt(nb, wg_s[sl, :]).astype(BF)
            ub = _nt(nb, wu_s[sl, :]).astype(BF)
            gg_ref[:, sl] = gb
            uu_ref[:, sl] = ub
            a_ref[0, :, sl] = (gb * _sig(gb)) * ub

    row = lambda i: (i, 0)
    tile = pl.BlockSpec((TM, D), row)
    return _call(
        body, name=name, grid=(T // TM,), args=[x, g, *wbufs], comm=comm,
        in_specs=[tile, pl.BlockSpec((1, D), lambda i: (0, 0)), ANY, ANY],
        out_shape=[jax.ShapeDtypeStruct((T, D), BF), jax.ShapeDtypeStruct((T, F), BF), jax.ShapeDtypeStruct((T, F), BF),
                   jax.ShapeDtypeStruct((1, T, F), BF)],
        out_specs=[tile, pl.BlockSpec((TM, F), row), pl.BlockSpec((TM, F), row),
                   pl.BlockSpec((1, TM, F), lambda i: (0, i, 0))],
        scratch_shapes=[pltpu.VMEM((F, D), BF)] * 2 + [pltpu.SemaphoreType.DMA((2,))])


def _ffn_down(x, act, wbuf, off, name, comm=None):
    def body(x_ref, a_ref, b0, h_ref, wd_s, sem):
        _load_ffn_weights((b0,), (off,), (wd_s,), sem)
        h_ref[...] = x_ref[...] + 0.5 * _nn(a_ref[0], wd_s[...])

    tile = pl.BlockSpec((TM, D), lambda i: (i, 0))
    return _call(
        body, name=name, grid=(T // TM,), args=[x, act, wbuf], comm=comm,
        in_specs=[tile, pl.BlockSpec((1, TM, F), lambda i: (0, i, 0)), ANY],
        out_shape=[jax.ShapeDtypeStruct((T, D), F32)], out_specs=[tile],
        scratch_shapes=[pltpu.VMEM((F, D), BF), pltpu.SemaphoreType.DMA((1,))])


def _load_in_proj(parts, w_s, sem):
    @pl.when(pl.program_id(0) == 0)
    def _():
        shard = NG * D // NDEV
        rows = shard // len(parts)
        cps = [pltpu.make_async_copy(buf.at[pl.ds(first + k * rows, rows), :],
                                     w_s.at[pl.ds(k * shard + p * rows, rows), :], sem.at[p * NDEV + k])
               for p, (buf, first) in enumerate(parts) for k in range(NDEV)]
        for cp in cps:
            cp.start()
        for cp in cps:
            cp.wait()


def _mix_in(h1, gm, win, comm=None):
    def body(h_ref, g_ref, *rest):
        w_any, (u_ref, z_ref, w_s, sem) = rest[:len(win)], rest[len(win):]
        _load_in_proj([(b, first) for b, (_, first) in zip(w_any, win)], w_s, sem)
        xf = h_ref[...]
        r = lax.rsqrt(jnp.mean(xf * xf, axis=-1, keepdims=True) + EPS)
        ub = (xf * r * g_ref[...]).astype(BF)
        u_ref[...] = ub
        for j in range(NG):
            z_ref[j] = _nt(ub, w_s[j * D:(j + 1) * D, :]).astype(BF)

    row = lambda i: (i, 0)
    return _call(
        body, name="mix_in", grid=(T // TM,), args=[h1, gm] + [b for b, _ in win], comm=comm,
        in_specs=[pl.BlockSpec((TM, D), row), pl.BlockSpec((1, D), lambda i: (0, 0))] + [ANY] * len(win),
        out_shape=[jax.ShapeDtypeStruct((T, D), BF), jax.ShapeDtypeStruct((NG, T, D), BF)],
        out_specs=[pl.BlockSpec((TM, D), row), pl.BlockSpec((NG, TM, D), lambda i: (0, i, 0))],
        scratch_shapes=[pltpu.VMEM((NG * D, D), BF), pltpu.SemaphoreType.DMA((NDEV * len(win),))])


def _shift_up(w, b):
    return w if b == 0 else pltpu.roll(w, w.shape[0] - b, 0)


def _fold8(p):
    red = p[0:8, :]
    for i in range(1, p.shape[0] // 8):
        red = red + p[8 * i:8 * i + 8, :]
    return red


def _dft_constants():
    import numpy as np
    nh = NB // 2
    f, n = np.arange(nh)[:, None], np.arange(NB)[None, :]
    ang = 2.0 * np.pi / NB * f * n
    fc = np.cos(ang)
    fs = np.where(f == 0, (-1.0) ** n, np.sin(ang))
    scale = np.where(f == 0, 1.0, 2.0) / NB
    ic = (scale * np.cos(ang)).T
    isn = np.where(f == 0, (-1.0) ** n / NB, scale * np.sin(ang)).T
    d = (KA - 1 - np.arange(32))[None, :]
    valid = (np.arange(32) < KA)[None, :]
    angk = 2.0 * np.pi / NB * f * d
    kc = np.where(valid, np.cos(angk), 0.0)
    ks = np.where(valid, np.sin(angk), 0.0)
    k2 = np.where(valid, np.where(f == 0, (-1.0) ** d, np.cos(angk)), 0.0)
    rtc = np.where(valid, scale * np.cos(angk), 0.0).T
    rts = np.where(valid, np.where(f == 0, (-1.0) ** d / NB, scale * np.sin(angk)), 0.0).T

    def bf(a):
        return jnp.asarray(a, F32).astype(BF)

    def split(a):
        hi = bf(a)
        return hi, (jnp.asarray(a, F32) - hi.astype(F32)).astype(BF)

    return dict(fc=bf(fc), fs=bf(fs), ic_hi=bf(ic[HB:]), is_hi=bf(isn[HB:]), ic_lo=bf(ic[:HB]), is_lo=bf(isn[:HB]),
                kc=split(kc), ks=split(ks), k2=split(k2), rtc=split(rtc), rts=split(rts))


def _dot3(m_hi, m_lo, x):
    x_hi = x.astype(BF)
    x_lo = (x - x_hi.astype(F32)).astype(BF)
    return _nn(m_hi, x_hi) + _nn(m_hi, x_lo) + _nn(m_lo, x_hi)


def _whole(a):
    return pl.BlockSpec(a.shape, lambda c, t: (0,) * a.ndim)


def _filter_spectrum(cw_ref, tabs, hc, hs, h2):
    w32 = cw_ref[0:32, :]
    for (hi, lo), dst in zip(tabs, (hc, hs, h2)):
        dst[...] = _dot3(hi[...], lo[...], w32)


def _conv_fwd_dft(z, cw, bias, dft, comm=None):
    nt = T // TB
    hb = TB // HB

    def body(z_ref, zh_ref, cw_ref, b_ref, fc_ref, fs_ref, ic_ref, is_ref, kch, kcl, ksh, ksl, k2h, k2l,
             a1_ref, q_ref, aext, ppad, hc, hs, h2):
        first = pl.program_id(1) == 0
        f = lambda ref, j: ref[j].astype(F32)

        @pl.when(first)
        def _():
            _filter_spectrum(cw_ref, ((kch, kcl), (ksh, ksl), (k2h, k2l)), hc, hs, h2)

        aext[0:HB, :] = jnp.where(first, 0.0, f(zh_ref, 0) * _sig(f(zh_ref, 1))).astype(BF)
        aext[HB:, :] = (f(z_ref, 0) * _sig(f(z_ref, 1))).astype(BF)
        ppad[0:8, :] = jnp.where(first, 0.0, f(zh_ref, 3)[HB - 8:HB, :] * f(zh_ref, 4)[HB - 8:HB, :])
        ppad[8:, :] = f(z_ref, 3) * f(z_ref, 4)
        bias_row = b_ref[...]

        for j in range(TB // HB):
            xs = aext[j * HB:j * HB + NB, :]
            xa, xb = _nn(fc_ref[...], xs), _nn(fs_ref[...], xs)
            yc = (hc[...] * xa - hs[...] * xb).astype(BF)
            ys = (h2[...] * xb + hs[...] * xa).astype(BF)
            y = _nn(ic_ref[...], yc) + _nn(is_ref[...], ys)
            a1_ref[j * HB:(j + 1) * HB, :] = (y + bias_row).astype(BF)

        def chunk(r, carry):
            base = pl.multiple_of(r * CHB, CHB)
            pw = ppad[pl.ds(base, CHB + 8), :]
            v = (cw_ref[pl.ds(32, 1), :] * _shift_up(pw, 6)[0:CHB, :]
                 + cw_ref[pl.ds(33, 1), :] * _shift_up(pw, 7)[0:CHB, :]
                 + cw_ref[pl.ds(34, 1), :] * pw[8:8 + CHB, :])
            q_ref[pl.ds(base, CHB), :] = (z_ref[2, pl.ds(base, CHB), :].astype(F32) * v).astype(BF)
            return carry

        lax.fori_loop(0, TB // CHB, chunk, 0)

    blk = pl.BlockSpec((TB, CW), lambda c, t: (t, c))
    tabs = [dft["fc"], dft["fs"], dft["ic_hi"], dft["is_hi"], *dft["kc"], *dft["ks"], *dft["k2"]]
    return _call(
        body, name="conv_fwd", grid=(D // CW, nt), comm=comm, args=[z, z, cw, bias] + tabs,
        in_specs=[pl.BlockSpec((5, TB, CW), lambda c, t: (0, t, c)),
                  pl.BlockSpec((5, HB, CW), lambda c, t: (0, jnp.maximum(t * hb - 1, 0), c)),
                  pl.BlockSpec((40, CW), lambda c, t: (0, c)), pl.BlockSpec((1, CW), lambda c, t: (0, c))]
                 + [_whole(a) for a in tabs],
        out_shape=[jax.ShapeDtypeStruct((T, D), BF), jax.ShapeDtypeStruct((T, D), BF)], out_specs=[blk, blk],
        scratch_shapes=[pltpu.VMEM((TB + HB, CW), BF), pltpu.VMEM((TB + 8, CW), F32)]
                       + [pltpu.VMEM((NB // 2, CW), F32)] * 3)


def _conv_bwd_dft(z, da1, dq, dzg, cw, dft, comm=None):
    nt = T // TB
    hb = TB // HB
    last_h = T // HB - 1

    def body(z_ref, zp_ref, zn_ref, da1_ref, da1n_ref, dq_ref, dqn_ref, dzg_ref, cw_ref,
             fc_ref, fs_ref, ic_ref, is_ref, kch, kcl, ksh, ksl, k2h, k2l, rch, rcl, rsh, rsl,
             dz_ref, dwa_ref, dwb_ref, aext, dyext, ppad, dvpad, hc, hs, h2, rc, rs, nyq, acc_b):
        t = pl.program_id(1)
        first, last = t == 0, t == nt - 1
        f = lambda ref, j: ref[j].astype(F32)

        @pl.when(first)
        def _():
            _filter_spectrum(cw_ref, ((kch, kcl), (ksh, ksl), (k2h, k2l)), hc, hs, h2)
            rc[...] = jnp.zeros_like(rc)
            rs[...] = jnp.zeros_like(rs)
            nyq[...] = jnp.zeros_like(nyq)
            acc_b[...] = jnp.zeros_like(acc_b)

        aext[0:HB, :] = jnp.where(first, 0.0, f(zp_ref, 0) * _sig(f(zp_ref, 1))).astype(BF)
        aext[HB:, :] = (f(z_ref, 0) * _sig(f(z_ref, 1))).astype(BF)
        dyext[0:TB, :] = da1_ref[...]
        dyext[TB:, :] = jnp.where(last, 0.0, da1n_ref[...].astype(F32)).astype(BF)
        ppad[0:8, :] = jnp.where(first, 0.0, f(zp_ref, 3)[HB - 8:HB, :] * f(zp_ref, 4)[HB - 8:HB, :])
        ppad[8:, :] = f(z_ref, 3) * f(z_ref, 4)
        dvpad[0:TB, :] = dq_ref[...].astype(F32) * f(z_ref, 2)
        dvpad[TB:, :] = jnp.where(last, 0.0, dqn_ref[...].astype(F32)[0:8, :] * f(zn_ref, 2)[0:8, :])

        for j in range(TB // HB):
            rows = slice(j * HB, (j + 1) * HB)
            dys = dyext[j * HB:j * HB + NB, :]
            da, db = _nn(fc_ref[...], dys), _nn(fs_ref[...], dys)
            gc = (hc[...] * da + hs[...] * db).astype(BF)
            gs = (h2[...] * db - hs[...] * da).astype(BF)
            da0 = _nn(ic_ref[...], gc) + _nn(is_ref[...], gs)
            z0, z1 = z_ref[0, rows, :].astype(F32), z_ref[1, rows, :].astype(F32)
            s1 = _sig(z1)
            dz_ref[0, rows, :] = (da0 * s1).astype(BF)
            dz_ref[1, rows, :] = (da0 * z0 * (s1 * (1.0 - s1))).astype(BF)
            xs = aext[j * HB:j * HB + NB, :]
            xa, xb = _nn(fc_ref[...], xs), _nn(fs_ref[...], xs)
            dyb = dyext[rows, :]
            pa, pb = _nn(fc_ref[:, HB:NB], dyb), _nn(fs_ref[:, HB:NB], dyb)
            rc[...] += pa * xa + pb * xb
            rs[...] += pb * xa - pa * xb
            nyq[...] += pb[0:8, :] * xb[0:8, :]

        def chunk(r, carry):
            base = pl.multiple_of(r * CHB, CHB)
            rows = pl.ds(base, CHB)
            pw = ppad[pl.ds(base, CHB + 8), :]
            p6 = _shift_up(pw, 6)[0:CHB, :]
            p7 = _shift_up(pw, 7)[0:CHB, :]
            p8 = pw[8:8 + CHB, :]
            wb0, wb1, wb2 = cw_ref[pl.ds(32, 1), :], cw_ref[pl.ds(33, 1), :], cw_ref[pl.ds(34, 1), :]
            v = wb0 * p6 + wb1 * p7 + wb2 * p8
            dz_ref[2, rows, :] = (dq_ref[rows, :].astype(F32) * v).astype(BF)
            dvw = dvpad[pl.ds(base, CHB + 8), :]
            dvc = dvw[0:CHB, :]
            dp = wb2 * dvc + wb1 * _shift_up(dvw, 1)[0:CHB, :] + wb0 * _shift_up(dvw, 2)[0:CHB, :]
            dz_ref[3, rows, :] = (dp * z_ref[4, rows, :].astype(F32)).astype(BF)
            dz_ref[4, rows, :] = (dp * z_ref[3, rows, :].astype(F32)).astype(BF)
            acc_b[0:8, :] += _fold8(dvc * p6)
            acc_b[8:16, :] += _fold8(dvc * p7)
            acc_b[16:24, :] += _fold8(dvc * p8)
            dz_ref[5, rows, :] = dzg_ref[0, rows, :]
            dz_ref[6, rows, :] = dzg_ref[1, rows, :]
            return carry

        lax.fori_loop(0, TB // CHB, chunk, 0)

        @pl.when(last)
        def _():
            row0 = lax.broadcasted_iota(jnp.int32, (NB // 2, CW), 0) == 0
            ny = jnp.broadcast_to(nyq[0:1, :], (NB // 2, CW))
            rcv = jnp.where(row0, rc[...] - ny, rc[...])
            rsv = jnp.where(row0, ny, rs[...])
            dwa_ref[...] = _dot3(rch[...], rcl[...], rcv) + _dot3(rsh[...], rsl[...], rsv)
            for k in range(KB):
                dwb_ref[k:k + 1, :] = jnp.sum(acc_b[8 * k:8 * k + 8, :], axis=0, keepdims=True)
            dwb_ref[KB:8, :] = jnp.zeros((8 - KB, CW), F32)

    blk = lambda c, t: (t, c)
    nxt = lambda c, t: (jnp.minimum((t + 1) * hb, last_h), c)
    tabs = [dft["fc"], dft["fs"], dft["ic_lo"], dft["is_lo"], *dft["kc"], *dft["ks"], *dft["k2"], *dft["rtc"], *dft["rts"]]
    return _call(
        body, name="conv_bwd", grid=(D // CW, nt), comm=comm, args=[z, z, z, da1, da1, dq, dq, dzg, cw] + tabs,
        in_specs=[pl.BlockSpec((5, TB, CW), lambda c, t: (0, t, c)),
                  pl.BlockSpec((5, HB, CW), lambda c, t: (0, jnp.maximum(t * hb - 1, 0), c)),
                  pl.BlockSpec((5, HB, CW), lambda c, t: (0, jnp.minimum((t + 1) * hb, last_h), c)),
                  pl.BlockSpec((TB, CW), blk), pl.BlockSpec((HB, CW), nxt),
                  pl.BlockSpec((TB, CW), blk), pl.BlockSpec((HB, CW), nxt),
                  pl.BlockSpec((2, TB, CW), lambda c, t: (0, t, c)),
                  pl.BlockSpec((40, CW), lambda c, t: (0, c))]
                 + [_whole(a) for a in tabs],
        out_shape=[jax.ShapeDtypeStruct((NG, T, D), BF), jax.ShapeDtypeStruct((32, D), F32),
                   jax.ShapeDtypeStruct((8, D), F32)],
        out_specs=[pl.BlockSpec((NG, TB, CW), lambda c, t: (0, t, c)),
                   pl.BlockSpec((32, CW), lambda c, t: (0, c)), pl.BlockSpec((8, CW), lambda c, t: (0, c))],
        scratch_shapes=[pltpu.VMEM((TB + HB, CW), BF), pltpu.VMEM((TB + HB, CW), BF),
                        pltpu.VMEM((TB + 8, CW), F32), pltpu.VMEM((TB + 8, CW), F32)]
                       + [pltpu.VMEM((NB // 2, CW), F32)] * 5 + [pltpu.VMEM((8, CW), F32), pltpu.VMEM((24, CW), F32)])


def _layernorm_silu(a1, lng, lnb):
    mu = jnp.mean(a1, axis=-1, keepdims=True)
    xc = a1 - mu
    rs = lax.rsqrt(jnp.mean(xc * xc, axis=-1, keepdims=True) + EPS)
    xh = xc * rs
    a2 = xh * lng + lnb
    sg = _sig(a2)
    return xh, rs, a2, sg


def _square_specs(blocks):
    return [pl.BlockSpec((D, D), lambda i, b=b: (b, 0)) for b in blocks]


def _mix_out(a1, q, z, h1, lng, lnb, wsq, comm=None):
    def body(a1_ref, q_ref, ga_ref, gb_ref, h_ref, lng_ref, lnb_ref, wa_ref, wb_ref, wo_ref, h2_ref, ya_ref, yb_ref):
        _, _, a2, sg = _layernorm_silu(a1_ref[...].astype(F32), lng_ref[...], lnb_ref[...])
        ya = _nn((a2 * sg).astype(BF), wa_ref[...])
        yb = _nn(q_ref[...], wb_ref[...])
        ya_ref[...] = ya.astype(BF)
        yb_ref[...] = yb.astype(BF)
        m = _sig(ga_ref[...].astype(F32)) * ya + _sig(gb_ref[...].astype(F32)) * yb
        h2_ref[...] = h_ref[...] + _nn(m.astype(BF), wo_ref[...])

    row = lambda i: (i, 0)
    vec = pl.BlockSpec((1, D), lambda i: (0, 0))
    return _call(
        body, name="mix_out", grid=(T // TM,), args=[a1, q, z, z, h1, lng, lnb, wsq, wsq, wsq], comm=comm,
        in_specs=[pl.BlockSpec((TM, D), row), pl.BlockSpec((TM, D), row),
                  pl.BlockSpec((None, TM, D), lambda i: (5, i, 0)), pl.BlockSpec((None, TM, D), lambda i: (6, i, 0)),
                  pl.BlockSpec((TM, D), row), vec, vec] + _square_specs((0, 1, 2)),
        out_shape=[jax.ShapeDtypeStruct((T, D), F32), jax.ShapeDtypeStruct((T, D), BF), jax.ShapeDtypeStruct((T, D), BF)],
        out_specs=[pl.BlockSpec((TM, D), row)] * 3)


def _rmsnorm_bwd(xf, g, dn):
    r = lax.rsqrt(jnp.mean(xf * xf, axis=-1, keepdims=True) + EPS)
    xr = xf * r
    gdn = dn * g
    dx = r * gdn - xr * (r * jnp.mean(gdn * xr, axis=-1, keepdims=True))
    return dx, jnp.sum(dn * xr, axis=0, keepdims=True)


def _ffn_bwd_hidden(dh, gg, uu, wbuf, off, name, comm=None):
    nf = F // FC

    def body(dh_ref, gg_ref, uu_ref, b0, dgu_ref, wd_s, sem):
        _load_ffn_weights((b0,), (off,), (wd_s,), sem)
        dhb = dh_ref[...]
        for c in range(nf):
            sl = slice(c * FC, (c + 1) * FC)
            da = _nt(dhb, wd_s[sl, :]).astype(BF)
            gb, ub = gg_ref[:, sl], uu_ref[:, sl]
            sg = _sig(gb)
            dgu_ref[0, :, sl] = (da * ub) * (sg * (1.0 + gb * (1.0 - sg)))
            dgu_ref[0, :, F + c * FC:F + (c + 1) * FC] = da * (gb * sg)

    row = lambda i: (i, 0)
    return _call(
        body, name=name, grid=(T // TM,), args=[dh, gg, uu, wbuf], comm=comm,
        in_specs=[pl.BlockSpec((TM, D), row), pl.BlockSpec((TM, F), row), pl.BlockSpec((TM, F), row), ANY],
        out_shape=[jax.ShapeDtypeStruct((1, T, 2 * F), BF)],
        out_specs=[pl.BlockSpec((1, TM, 2 * F), lambda i: (0, i, 0))],
        scratch_shapes=[pltpu.VMEM((F, D), BF), pltpu.SemaphoreType.DMA((1,))])


def _ffn_bwd_input(dgu, dh, x, g, wbufs, offs, name, comm=None):
    def body(dgu_ref, dh_ref, x_ref, g_ref, b0, b1, dx_ref, s_ref, w_s, sem):
        _load_ffn_weights((b0, b1), offs, (w_s.at[pl.ds(0, F), :], w_s.at[pl.ds(F, F), :]), sem)

        @pl.when(pl.program_id(0) == 0)
        def _():
            s_ref[...] = jnp.zeros_like(s_ref)

        dn = _nn(dgu_ref[0], w_s[...])
        dxn, dg = _rmsnorm_bwd(x_ref[...], g_ref[...], dn)
        dx_ref[...] = dh_ref[...] + dxn
        s_ref[0:1, :] += dg

    row = lambda i: (i, 0)
    return _call(
        body, name=name, grid=(T // TM,), args=[dgu, dh, x, g, *wbufs], comm=comm,
        in_specs=[pl.BlockSpec((1, TM, 2 * F), lambda i: (0, i, 0)), pl.BlockSpec((TM, D), row),
                  pl.BlockSpec((TM, D), row), pl.BlockSpec((1, D), lambda i: (0, 0)), ANY, ANY],
        out_shape=[jax.ShapeDtypeStruct((T, D), F32), jax.ShapeDtypeStruct((8, D), F32)],
        out_specs=[pl.BlockSpec((TM, D), row), pl.BlockSpec((8, D), lambda i: (0, 0))],
        scratch_shapes=[pltpu.VMEM((2 * F, D), BF), pltpu.SemaphoreType.DMA((2,))])


def _tn_matmul(lhs, rhs, tr, name, comm=None):
    ng, _, cdim = lhs.shape
    nc, nk = cdim // tr, T // TK
    if rhs.ndim == 2:
        r_spec = pl.BlockSpec((TK, D), lambda g, c, k: (k, 0))
    else:
        r_spec = pl.BlockSpec((None, TK, D), lambda g, c, k: (g, k, 0))

    def body(l_ref, r_ref, o_ref, acc):
        k = pl.program_id(2)

        @pl.when(k == 0)
        def _():
            acc[...] = jnp.zeros_like(acc)

        acc[...] += _tn(l_ref[...], r_ref[...])

        @pl.when(k == nk - 1)
        def _():
            o_ref[...] = acc[...].astype(BF)

    return _call(
        body, name=name, grid=(ng, nc, nk), args=[lhs, rhs], comm=comm,
        in_specs=[pl.BlockSpec((None, TK, tr), lambda g, c, k: (g, k, c)), r_spec],
        out_shape=[jax.ShapeDtypeStruct((ng * cdim, D), BF)],
        out_specs=[pl.BlockSpec((tr, D), lambda g, c, k: (g * nc + c, 0))],
        scratch_shapes=[pltpu.VMEM((tr, D), F32)])


def _mix_out_bwd(dh2, ya, yb, z, a1, q, lng, lnb, wsq, comm=None):
    def body(dh_ref, ya_ref, yb_ref, ga_ref, gb_ref, a1_ref, q_ref, lng_ref, lnb_ref, wa_ref, wb_ref, wo_ref,
             dzg_ref, da1_ref, dq_ref, l_ref, r_ref, s_ref):
        @pl.when(pl.program_id(0) == 0)
        def _():
            s_ref[...] = jnp.zeros_like(s_ref)

        dhb = dh_ref[...].astype(BF)
        dm = _nt(dhb, wo_ref[...]).astype(BF)
        ya, yb = ya_ref[...], yb_ref[...]
        sa, sb = _sig(ga_ref[...]), _sig(gb_ref[...])
        l_ref[0] = sa * ya + sb * yb
        l_ref[2] = q_ref[...]
        dzg_ref[0] = (dm * ya) * (sa * (1.0 - sa))
        dzg_ref[1] = (dm * yb) * (sb * (1.0 - sb))
        dya = dm * sa
        dyb = dm * sb
        r_ref[0] = dhb
        r_ref[1] = dya
        r_ref[2] = dyb
        dq_ref[...] = _nt(dyb, wb_ref[...]).astype(BF)
        da3 = _nt(dya, wa_ref[...])
        lng = lng_ref[...]
        xh, rs, a2, sg = _layernorm_silu(a1_ref[...].astype(F32), lng, lnb_ref[...])
        l_ref[1] = (a2 * sg).astype(BF)
        da2 = da3 * (sg * (1.0 + a2 * (1.0 - sg)))
        s_ref[0:1, :] += jnp.sum(da2 * xh, axis=0, keepdims=True)
        s_ref[1:2, :] += jnp.sum(da2, axis=0, keepdims=True)
        dxh = da2 * lng
        da1 = rs * (dxh - jnp.mean(dxh, axis=-1, keepdims=True) - xh * jnp.mean(dxh * xh, axis=-1, keepdims=True))
        da1_ref[...] = da1.astype(BF)
        s_ref[2:3, :] += jnp.sum(da1, axis=0, keepdims=True)

    row = lambda i: (i, 0)
    row3 = lambda i: (0, i, 0)
    vec = pl.BlockSpec((1, D), lambda i: (0, 0))
    return _call(
        body, name="mix_out_bwd", grid=(T // TM,), args=[dh2, ya, yb, z, z, a1, q, lng, lnb, wsq, wsq, wsq], comm=comm,
        in_specs=[pl.BlockSpec((TM, D), row), pl.BlockSpec((TM, D), row), pl.BlockSpec((TM, D), row),
                  pl.BlockSpec((None, TM, D), lambda i: (5, i, 0)), pl.BlockSpec((None, TM, D), lambda i: (6, i, 0)),
                  pl.BlockSpec((TM, D), row), pl.BlockSpec((TM, D), row), vec, vec] + _square_specs((0, 1, 2)),
        out_shape=[jax.ShapeDtypeStruct((2, T, D), BF), jax.ShapeDtypeStruct((T, D), BF),
                   jax.ShapeDtypeStruct((T, D), BF), jax.ShapeDtypeStruct((3, T, D), BF),
                   jax.ShapeDtypeStruct((3, T, D), BF), jax.ShapeDtypeStruct((8, D), F32)],
        out_specs=[pl.BlockSpec((2, TM, D), row3), pl.BlockSpec((TM, D), row), pl.BlockSpec((TM, D), row),
                   pl.BlockSpec((3, TM, D), row3), pl.BlockSpec((3, TM, D), row3), pl.BlockSpec((8, D), lambda i: (0, 0))])


def _mix_in_bwd(dz, dh2, h1, gm, win, comm=None):
    def body(dz_ref, dh_ref, h_ref, g_ref, *rest):
        w_any, (o_ref, ob_ref, s_ref, w_s, sem) = rest[:len(win)], rest[len(win):]
        _load_in_proj([(b, first) for b, (_, first) in zip(w_any, win)], w_s, sem)

        @pl.when(pl.program_id(0) == 0)
        def _():
            s_ref[...] = jnp.zeros_like(s_ref)

        du = _nn(dz_ref[0], w_s[0:D, :])
        for j in range(1, NG):
            du = du + _nn(dz_ref[j], w_s[j * D:(j + 1) * D, :])
        dx, dg = _rmsnorm_bwd(h_ref[...], g_ref[...], du)
        dh1 = dh_ref[...] + dx
        o_ref[...] = dh1
        ob_ref[...] = (0.5 * dh1).astype(BF)
        s_ref[0:1, :] += dg

    row = lambda i: (i, 0)
    return _call(
        body, name="mix_in_bwd", grid=(T // TM,), args=[dz, dh2, h1, gm] + [b for b, _ in win], comm=comm,
        in_specs=[pl.BlockSpec((NG, TM, D), lambda i: (0, i, 0)), pl.BlockSpec((TM, D), row),
                  pl.BlockSpec((TM, D), row), pl.BlockSpec((1, D), lambda i: (0, 0))] + [ANY] * len(win),
        out_shape=[jax.ShapeDtypeStruct((T, D), F32), jax.ShapeDtypeStruct((T, D), BF), jax.ShapeDtypeStruct((8, D), F32)],
        out_specs=[pl.BlockSpec((TM, D), row), pl.BlockSpec((TM, D), row), pl.BlockSpec((8, D), lambda i: (0, 0))],
        scratch_shapes=[pltpu.VMEM((NG * D, D), BF), pltpu.SemaphoreType.DMA((NDEV * len(win),))])


def _row_tile(n, want, mult):
    for t in range(min(want, n), 0, -1):
        if n % t == 0 and t % mult == 0:
            return t
    return n


def _sum_slots(recv, name):
    ns, rows, cols = recv.shape
    tr = _row_tile(rows, 1024, 16)

    def body(r_ref, o_ref):
        s = r_ref[0].astype(F32)
        for k in range(1, ns):
            s = s + r_ref[k].astype(F32)
        o_ref[...] = s

    return _call(
        body, name=name, grid=(rows // tr,), args=[recv],
        in_specs=[pl.BlockSpec((ns, tr, cols), lambda i: (0, i, 0))],
        out_shape=[jax.ShapeDtypeStruct((rows, cols), F32)],
        out_specs=[pl.BlockSpec((tr, cols), lambda i: (i, 0))])[0]


def _pack_small(s_ffn1, s_in, s_mix, s_ffn2, s_final, dwa, dwb):
    def body(f1, mi, mo, f2, fl, wa_ref, wb_ref, v_ref, k_ref):
        for dst, (ref, row) in enumerate(((f1, 0), (mi, 0), (mo, 0), (mo, 1), (mo, 2), (f2, 0), (fl, 0), (fl, 1))):
            v_ref[dst:dst + 1, :] = ref[row:row + 1, :]
        for k in range(NDEV):
            k_ref[k, 0:32, :] = wa_ref[:, k * LANE:(k + 1) * LANE]
            k_ref[k, 32:40, :] = wb_ref[:, k * LANE:(k + 1) * LANE]

    return pl.pallas_call(
        body, name="pack_small",
        out_shape=(jax.ShapeDtypeStruct((8, D), F32), jax.ShapeDtypeStruct((NDEV, 40, LANE), F32)),
    )(s_ffn1, s_in, s_mix, s_ffn2, s_final, dwa, dwb)


def _sum_small(vecs, convs):
    def body(v_ref, k_ref, vs_ref, ks_ref, l_ref):
        s, c = v_ref[0], k_ref[0]
        for k in range(1, NDEV):
            s = s + v_ref[k]
            c = c + k_ref[k]
        vs_ref[...] = s
        ks_ref[...] = c
        l_ref[...] = jnp.broadcast_to(jnp.sum(s[7:8, :], axis=-1, keepdims=True), (8, LANE))

    return pl.pallas_call(
        body, name="sum_small",
        out_shape=(jax.ShapeDtypeStruct((8, D), F32), jax.ShapeDtypeStruct((40, LANE), F32),
                   jax.ShapeDtypeStruct((8, LANE), F32)),
    )(vecs, convs)


def _adam(gs, ws, ms, vs, name, comm=None):
    n = len(gs)
    rows, cols = ws[0].shape
    tr = _row_tile(rows, 256, 16)
    c1 = 1.0 - ADAM_B1 ** ADAM_STEP
    c2 = 1.0 - ADAM_B2 ** ADAM_STEP
    summed = [isinstance(g, tuple) for g in gs]

    def body(*refs):
        for i in range(n):
            g_in, w, m, v = refs[4 * i], refs[4 * i + 1][...], refs[4 * i + 2][...], refs[4 * i + 3][...]
            g_ref, d_ref, m_ref, v_ref = refs[4 * n + 4 * i: 4 * n + 4 * i + 4]
            if summed[i]:
                g = g_in[0].astype(F32)
                for k in range(1, g_in.shape[0]):
                    g = g + g_in[k].astype(F32)
            else:
                g = g_in[...]
            g_ref[...] = g
            m2 = ADAM_B1 * m + (1.0 - ADAM_B1) * g
            v2 = ADAM_B2 * v + (1.0 - ADAM_B2) * (g * g)
            d_ref[...] = -ADAM_LR * ((m2 / c1) / (jnp.sqrt(v2 / c2) + ADAM_EPS) + ADAM_WD * w)
            m_ref[...] = m2
            v_ref[...] = v2

    spec = pl.BlockSpec((tr, cols), lambda i: (i, 0))
    args, in_specs = [], []
    for i in range(n):
        if summed[i]:
            slots, first = gs[i]
            args.append(slots)
            in_specs.append(pl.BlockSpec((slots.shape[0], tr, cols), lambda i, b=first // tr: (0, b + i, 0)))
        else:
            args.append(gs[i])
            in_specs.append(spec)
        args += [ws[i], ms[i], vs[i]]
        in_specs += [spec] * 3
    outs = _call(body, name=name, grid=(rows // tr,), args=args, comm=comm, in_specs=in_specs,
                 out_shape=[jax.ShapeDtypeStruct((rows, cols), F32)] * (4 * n), out_specs=[spec] * (4 * n))
    return [tuple(outs[4 * i: 4 * i + 4]) for i in range(n)], outs[4 * n:]


def kernel(x, ffn1_norm, ffn1_w_gate, ffn1_w_up, ffn1_w_down, mix_norm, w_in, a_dw_w, a_dw_b, a_ln_g, a_ln_b, a_w_out, b_conv_w, b_w_out, w_o, ffn2_norm, ffn2_w_gate, ffn2_w_up, ffn2_w_down, final_norm, loss_target, m_ffn1_norm, m_ffn1_w_gate, m_ffn1_w_up, m_ffn1_w_down, m_mix_norm, m_w_in, m_a_dw_w, m_a_dw_b, m_a_ln_g, m_a_ln_b, m_a_w_out, m_b_conv_w, m_b_w_out, m_w_o, m_ffn2_norm, m_ffn2_w_gate, m_ffn2_w_up, m_ffn2_w_down, m_final_norm, v_ffn1_norm, v_ffn1_w_gate, v_ffn1_w_up, v_ffn1_w_down, v_mix_norm, v_w_in, v_a_dw_w, v_a_dw_b, v_a_ln_g, v_a_ln_b, v_a_w_out, v_b_conv_w, v_b_w_out, v_w_o, v_ffn2_norm, v_ffn2_w_gate, v_ffn2_w_up, v_ffn2_w_down, v_final_norm):
    names = ("ffn1_norm", "ffn1_w_gate", "ffn1_w_up", "ffn1_w_down", "mix_norm", "w_in", "a_dw_w", "a_dw_b",
             "a_ln_g", "a_ln_b", "a_w_out", "b_conv_w", "b_w_out", "w_o", "ffn2_norm", "ffn2_w_gate", "ffn2_w_up",
             "ffn2_w_down", "final_norm")
    w = dict(ffn1_norm=ffn1_norm, ffn1_w_gate=ffn1_w_gate, ffn1_w_up=ffn1_w_up, ffn1_w_down=ffn1_w_down,
             mix_norm=mix_norm, w_in=w_in, a_dw_w=a_dw_w, a_dw_b=a_dw_b, a_ln_g=a_ln_g, a_ln_b=a_ln_b,
             a_w_out=a_w_out, b_conv_w=b_conv_w, b_w_out=b_w_out, w_o=w_o, ffn2_norm=ffn2_norm,
             ffn2_w_gate=ffn2_w_gate, ffn2_w_up=ffn2_w_up, ffn2_w_down=ffn2_w_down, final_norm=final_norm)
    m = dict(ffn1_norm=m_ffn1_norm, ffn1_w_gate=m_ffn1_w_gate, ffn1_w_up=m_ffn1_w_up, ffn1_w_down=m_ffn1_w_down,
             mix_norm=m_mix_norm, w_in=m_w_in, a_dw_w=m_a_dw_w, a_dw_b=m_a_dw_b, a_ln_g=m_a_ln_g, a_ln_b=m_a_ln_b,
             a_w_out=m_a_w_out, b_conv_w=m_b_conv_w, b_w_out=m_b_w_out, w_o=m_w_o, ffn2_norm=m_ffn2_norm,
             ffn2_w_gate=m_ffn2_w_gate, ffn2_w_up=m_ffn2_w_up, ffn2_w_down=m_ffn2_w_down, final_norm=m_final_norm)
    v = dict(ffn1_norm=v_ffn1_norm, ffn1_w_gate=v_ffn1_w_gate, ffn1_w_up=v_ffn1_w_up, ffn1_w_down=v_ffn1_w_down,
             mix_norm=v_mix_norm, w_in=v_w_in, a_dw_w=v_a_dw_w, a_dw_b=v_a_dw_b, a_ln_g=v_a_ln_g, a_ln_b=v_a_ln_b,
             a_w_out=v_a_w_out, b_conv_w=v_b_conv_w, b_w_out=v_b_w_out, w_o=v_w_o, ffn2_norm=v_ffn2_norm,
             ffn2_w_gate=v_ffn2_w_gate, ffn2_w_up=v_ffn2_w_up, ffn2_w_down=v_ffn2_w_down, final_norm=v_final_norm)
    flat = _pack_weights(dict(wg1=ffn1_w_gate[0].T, wu1=ffn1_w_up[0].T, wd1=ffn1_w_down[0], wg2=ffn2_w_gate[0].T,
                              wu2=ffn2_w_up[0].T, wd2=ffn2_w_down[0], win=w_in[0], wa=a_w_out[0], wb=b_w_out[0],
                              wo=w_o[0]))
    cw_shard = jnp.concatenate([a_dw_w[0], jnp.zeros((1, LANE), F32), b_conv_w[0], jnp.zeros((5, LANE), F32)], axis=0)

    x2, tgt = x[0], loss_target[0]
    st_a, st_b, st_b2 = ("wg1", "wu1"), ("wd1", "win/0/2"), ("win/1/2",)
    st_c, st_d, st_e = ("wa", "wb", "wo", "wg2"), ("wu2",), ("wd2",)

    buf_a, cw = _run_comm(_join(_ag_comm(st_a, flat), _direct_comm(cw_shard, False)), "ag_ffn1")
    n1, gg1, uu1, act1, buf_b = _ffn_gate_up(x2, ffn1_norm, (buf_a, buf_a), (0, F), "ffn1_gate_up", _ag_comm(st_b, flat))
    h1, buf_b2 = _ffn_down(x2, act1, buf_b, 0, "ffn1_down", _ag_comm(st_b2, flat))
    win = ((buf_b, F), (buf_b2, 0))
    u, z, buf_c = _mix_in(h1, mix_norm, win, _ag_comm(st_c, flat))
    dft = _dft_constants()
    cw = jnp.transpose(cw, (1, 0, 2)).reshape(40, D)
    a1, q, buf_d = _conv_fwd_dft(z, cw, a_dw_b, dft, _ag_comm(st_d, flat))
    h2, ya, yb, buf_e = _mix_out(a1, q, z, h1, a_ln_g, a_ln_b, buf_c, _ag_comm(st_e, flat))
    ffn2_bufs, ffn2_offs = (buf_c, buf_d, buf_e), (3 * D, 0, 0)
    dh3, dhb3, s_final, n2, gg2, uu2, act2 = _ffn_fwd(h2, ffn2_norm, ffn2_bufs, ffn2_offs, "ffn2_fwd",
                                          final=(final_norm.reshape(1, D), tgt))

    tr_f = F // 2 if (F // 2) % LANE == 0 else F
    def pair(stage, src):
        return _rs_pair_comm(stage, src)

    def chip(stage, src, pair_buf, tag):
        return _rs_chip_comm(_pair_add(stage, src, pair_buf, "pair_add_" + tag))

    (dgu2,) = _ffn_bwd_hidden(dhb3, gg2, uu2, buf_e, 0, "ffn2_bwd_h")
    (gu2,) = _tn_matmul(dgu2, n2, tr_f, "dw_gu2")
    s2a, src2a = ("wg2", "wu2"), dict(wg2=(gu2, 0), wu2=(gu2, F))
    gd2, pair2a = _tn_matmul(act2, dhb3, tr_f, "dw_d2", pair(s2a, src2a))
    s2b, src2b = ("wd2",), dict(wd2=(gd2, 0))
    dh2, s_ffn2, pair2b = _ffn_bwd_input(dgu2, dh3, h2, ffn2_norm, (buf_c, buf_d), (3 * D, 0), "ffn2_bwd_x",
                                         pair(s2b, src2b))
    dzg, da1, dq, lsq, rsq, s_mix, recv2b = _mix_out_bwd(dh2, ya, yb, z, a1, q, a_ln_g, a_ln_b, buf_c,
                                                          chip(s2b, src2b, pair2b, "2b"))
    (gsq,) = _tn_matmul(lsq, rsq, D, "dw_square")
    ssq, srcsq = ("wa", "wb", "wo"), dict(wa=(gsq, D), wb=(gsq, 2 * D), wo=(gsq, 0))
    dz, dwa, dwb, recv2a, pairsq = _conv_bwd_dft(z, da1, dq, dzg, cw, dft,
                                                 _join(chip(s2a, src2a, pair2a, "2a"), pair(ssq, srcsq)))
    gin, recvsq = _tn_matmul(dz, u, D, "dw_in", chip(ssq, srcsq, pairsq, "sq"))
    sin_a, sin_b, srcin = ("win/0/2",), ("win/1/2",), {"win/0/2": (gin, 0), "win/1/2": (gin, 0)}
    dh1, dhb1, s_in, pairin_a, pairin_b = _mix_in_bwd(dz, dh2, h1, mix_norm, win,
                                                _join(pair(sin_a, srcin), pair(sin_b, srcin)))
    dgu1, recvin_a = _ffn_bwd_hidden(dhb1, gg1, uu1, buf_b, 0, "ffn1_bwd_h",
                                           chip(sin_a, srcin, pairin_a, "in_a"))
    gu1, recvin_b = _tn_matmul(dgu1, n1, tr_f, "dw_gu1", chip(sin_b, srcin, pairin_b, "in_b"))
    s1a, src1a = ("wg1", "wu1"), dict(wg1=(gu1, 0), wu1=(gu1, F))
    gd1, pair1a = _tn_matmul(act1, dhb1, tr_f, "dw_d1", pair(s1a, src1a))
    s1b, src1b = ("wd1",), dict(wd1=(gd1, 0))
    dx, s_ffn1, recv1a, pair1b = _ffn_bwd_input(dgu1, dh1, x2, ffn1_norm, (buf_a, buf_a), (0, F), "ffn1_bwd_x",
                                                _join(chip(s1a, src1a, pair1a, "1a"), pair(s1b, src1b)))
    win_sum = jnp.concatenate([_sum_slots(recvin_a, "sum_in_a"), _sum_slots(recvin_b, "sum_in_b")], axis=0)

    vec8, convk = _pack_small(s_ffn1, s_in, s_mix, s_ffn2, s_final, dwa, dwb)
    recv1b, vec_all, conv_all = _run_comm(
        _join(chip(s1b, src1b, pair1b, "1b"), _join(_direct_comm(vec8, False), _direct_comm(convk, True))), "xchg_tail")
    vec_sum, conv_sum, loss_blk = _sum_small(vec_all, conv_all)
    loss = loss_blk[0, 0]

    fs = F // NDEV
    g = dict(ffn1_w_gate=(recv1a, 0), ffn1_w_up=(recv1a, fs), ffn2_w_gate=(recv2a, 0), ffn2_w_up=(recv2a, fs),
             ffn1_w_down=(recv1b, 0),
             ffn2_w_down=(recv2b, 0), a_w_out=(recvsq, 0), b_w_out=(recvsq, D // NDEV), w_o=(recvsq, 2 * (D // NDEV)),
             w_in=win_sum.T, ffn1_norm=vec_sum[0:1], mix_norm=vec_sum[1:2], a_ln_g=vec_sum[2:3], a_ln_b=vec_sum[3:4],
             a_dw_b=vec_sum[4:5], ffn2_norm=vec_sum[5:6], final_norm=vec_sum[6:7],
             a_dw_w=conv_sum[0:KA], b_conv_w=conv_sum[32:32 + KB])
    grad, upd = {}, {}

    def run(group, name, as2d=lambda a: a[0], back=lambda a, n: a.reshape(w[n].shape)):
        res, _ = _adam([g[n] for n in group], [as2d(w[n]) for n in group], [as2d(m[n]) for n in group],
                       [as2d(v[n]) for n in group], name)
        for n, r in zip(group, res):
            grad[n], upd[n] = back(r[0], n), tuple(back(a, n) for a in r[1:])

    run(("ffn1_w_gate", "ffn1_w_up", "ffn2_w_gate", "ffn2_w_up"), "adam_gate_up",
        as2d=lambda a: a[0].T, back=lambda a, n: a.T[None])
    run(("ffn1_w_down", "ffn2_w_down"), "adam_down")
    run(("w_in",), "adam_in")
    run(("a_w_out", "b_w_out", "w_o"), "adam_square")
    run(("a_dw_w",), "adam_dw")
    run(("b_conv_w",), "adam_conv")
    run(("ffn1_norm", "mix_norm", "a_dw_b", "a_ln_g", "a_ln_b", "ffn2_norm", "final_norm"), "adam_vec",
        as2d=lambda a: a.reshape(1, D))

    return (loss, dx.reshape(x.shape), *[grad[n] for n in names], *[upd[n][0] for n in names],
            *[upd[n][1] for n in names], *[upd[n][2] for n in names])
```

```python
import jax
import jax.numpy as jnp
from jax import lax
from jax.experimental import pallas as pl
from jax.experimental.pallas import tpu as pltpu

T = 4096
D = 1024
F = 2816
NG = 7
NDEV = 8
NCHIP = 4
KA, KB = 31, 3
EPS = 1e-6
ADAM_LR, ADAM_B1, ADAM_B2, ADAM_EPS, ADAM_WD, ADAM_STEP = 0.001, 0.9, 0.999, 1e-08, 0.01, 10

TM = 512
FC = 256
TB = 1024
NB = 256
HB = NB // 2
CW = 256
CHB = 64
LANE = 128
TK = 2048
VMEM_LIMIT = 56 * 1024 * 1024

BF = jnp.bfloat16
F32 = jnp.float32
MESH = pl.DeviceIdType.MESH
ANY = pl.BlockSpec(memory_space=pl.ANY)

ORDER = ("wg1", "wu1", "wd1", "wg2", "wu2", "wd2", "win", "wa", "wb", "wo")


class _Layout:
    def __init__(self):
        fs, dis, ds = F // NDEV, NG * D // NDEV, D // NDEV
        self.rows = dict(wg1=fs, wu1=fs, wd1=fs, wg2=fs, wu2=fs, wd2=fs, win=dis, wa=ds, wb=ds, wo=ds)
        self.fl, off = {}, 0
        for n in ORDER:
            self.fl[n] = off
            off += self.rows[n]
        self.RT = off


class _Stage:
    def __init__(self, names):
        lay = _Layout()
        self.names = names
        self.rows, self.full, self.sub, self.fl = {}, {}, {}, {}
        for n in names:
            base, i, k = (n.split("/") + ["0", "1"])[:3]
            self.full[n] = lay.rows[base]
            self.rows[n] = lay.rows[base] // int(k)
            self.sub[n] = int(i) * self.rows[n]
            self.fl[n] = lay.fl[base] + self.sub[n]
        self.off, self.wc, o, w = {}, {}, 0, 0
        for n in names:
            self.off[n], self.wc[n] = o, w
            o += self.rows[n]
            w += NDEV * self.rows[n]
        self.R, self.W = o, w

    def grad_row(self, n, first, dev_lin):
        return first + dev_lin * self.full[n] + self.sub[n]


def _nt(a, b):
    return lax.dot_general(a, b, (((1,), (1,)), ((), ())), preferred_element_type=F32)


def _nn(a, b):
    return lax.dot_general(a, b, (((1,), (0,)), ((), ())), preferred_element_type=F32)


def _tn(a, b):
    return lax.dot_general(a, b, (((0,), (0,)), ((), ())), preferred_element_type=F32)


def _sig(x):
    return 1.0 / (1.0 + jnp.exp(-x))


def _position():
    return lax.axis_index("x"), lax.axis_index("y"), lax.axis_index("c")


def _peer(pos, j):
    x, y, c = pos
    return (1 - x if j & 4 else x, 1 - y if j & 2 else y, 1 - c if j & 1 else c)


def _lin(pos):
    return 4 * pos[0] + 2 * pos[1] + pos[2]


def _chip(pos):
    return 2 * pos[0] + pos[1]


class _Comm:
    def __init__(self, inputs, out_shapes, scratch, start, finish, middle=None):
        self.inputs, self.out_shapes, self.scratch = inputs, out_shapes, scratch
        self.start, self.finish, self.middle = start, finish, middle


def _call(body, *, name, grid, args, in_specs, out_shape, out_specs, scratch_shapes=(), comm=None,
          num_scalar_prefetch=0):
    in_specs, out_shape, out_specs, scratch_shapes = list(in_specs), list(out_shape), list(out_specs), list(scratch_shapes)
    n_in, n_out, n_scr = len(in_specs), len(out_shape), len(scratch_shapes)
    sp = num_scalar_prefetch
    if comm is None:
        kernel_fn = lambda *refs: body(*refs)
        c_in = c_out = c_scr = 0
    else:
        c_in, c_out, c_scr = len(comm.inputs), len(comm.out_shapes), len(comm.scratch)

        def kernel_fn(*refs):
            pre, refs = refs[:sp], refs[sp:]
            ins, cins = refs[:n_in], refs[n_in:n_in + c_in]
            o0 = n_in + c_in
            outs, couts = refs[o0:o0 + n_out], refs[o0 + n_out:o0 + n_out + c_out]
            s0 = o0 + n_out + c_out
            scr, cscr = refs[s0:s0 + n_scr], refs[s0 + n_scr:]
            step, steps = pl.program_id(0), grid[0]
            for a in range(1, len(grid)):
                step, steps = step * grid[a] + pl.program_id(a), steps * grid[a]
            first, last = step == 0, step == steps - 1

            @pl.when(first)
            def _():
                comm.start(cins, couts, cscr)

            if comm.middle is not None:
                @pl.when(step == (steps // 2 if steps > 2 else steps - 1))
                def _():
                    comm.middle(cins, couts, cscr)

            body(*pre, *ins, *outs, *scr)

            @pl.when(last)
            def _():
                comm.finish(cins, couts, cscr)

        args = list(args) + list(comm.inputs)
        in_specs += [ANY] * c_in
        out_shape += list(comm.out_shapes)
        out_specs += [ANY] * c_out
        scratch_shapes += list(comm.scratch)
    params = pltpu.CompilerParams(dimension_semantics=("arbitrary",) * len(grid), vmem_limit_bytes=VMEM_LIMIT)
    if sp:
        grid_spec = pltpu.PrefetchScalarGridSpec(num_scalar_prefetch=sp, grid=grid, in_specs=in_specs,
                                                 out_specs=out_specs, scratch_shapes=scratch_shapes)
        return pl.pallas_call(kernel_fn, name=name, grid_spec=grid_spec, out_shape=out_shape,
                              compiler_params=params)(*args)
    return pl.pallas_call(kernel_fn, name=name, grid=grid, in_specs=in_specs, out_shape=out_shape, out_specs=out_specs,
                          scratch_shapes=scratch_shapes, compiler_params=params)(*args)


def _join(a, b):
    na = (len(a.inputs), len(a.out_shapes), len(a.scratch))

    def split(refs):
        return ([r[:n] for r, n in zip(refs, na)], [r[n:] for r, n in zip(refs, na)])

    def start(*refs):
        ra, rb = split(refs)
        a.start(*ra)
        b.start(*rb)

    def finish(*refs):
        ra, rb = split(refs)
        a.finish(*ra)
        b.finish(*rb)

    def middle(*refs):
        for stage, r in zip((a, b), split(refs)):
            if stage.middle is not None:
                stage.middle(*r)

    return _Comm(list(a.inputs) + list(b.inputs), list(a.out_shapes) + list(b.out_shapes),
                 list(a.scratch) + list(b.scratch), start, finish,
                 middle if (a.middle is not None or b.middle is not None) else None)


def _run_comm(comm, name):
    def body(*refs):
        c_in, c_out = len(comm.inputs), len(comm.out_shapes)
        parts = (refs[:c_in], refs[c_in:c_in + c_out], refs[c_in + c_out:])
        comm.start(*parts)
        if comm.middle is not None:
            comm.middle(*parts)
        comm.finish(*parts)

    return pl.pallas_call(
        body, name=name, out_shape=list(comm.out_shapes), in_specs=[ANY] * len(comm.inputs),
        out_specs=[ANY] * len(comm.out_shapes), scratch_shapes=list(comm.scratch))(*comm.inputs)


def _ag_comm(names, flat):
    st = _Stage(names)

    def ring(me):
        x, y, c = me
        diagonal = x == y
        up = (jnp.where(diagonal, x, 1 - x), jnp.where(diagonal, 1 - y, y), c)
        down = (jnp.where(diagonal, 1 - x, x), jnp.where(diagonal, y, 1 - y), c)
        low = c == 0
        passed = tuple(jnp.where(low, d, u) for d, u in zip(down, up))
        target = tuple(jnp.where(low, u, d) for d, u in zip(down, up))
        return up, down, (1 - x, 1 - y, c), passed, target

    def parts(refs):
        (flat_ref,), (out_ref,), (send_sems, recv_sems, local_sem) = refs
        me = _position()

        def region(name, dev):
            r = st.rows[name]
            return out_ref.at[pl.ds(st.wc[name] + _lin(dev) * r, r), :]

        def own(name):
            return flat_ref.at[pl.ds(st.fl[name], st.rows[name]), :]

        def copies(k, dev, to, from_flat):
            return [pltpu.make_async_remote_copy(
                src_ref=own(n) if from_flat else region(n, dev), dst_ref=region(n, dev), send_sem=send_sems.at[k],
                recv_sem=recv_sems.at[k], device_id=to, device_id_type=MESH) for n in names]

        def whole(k):
            return pltpu.make_async_remote_copy(
                src_ref=flat_ref.at[pl.ds(0, st.R), :], dst_ref=out_ref.at[pl.ds(0, st.R), :],
                send_sem=send_sems.at[k], recv_sem=recv_sems.at[k], device_id=me, device_id_type=MESH)

        return me, region, own, copies, whole, flat_ref, out_ref, local_sem

    def start(*refs):
        me, region, own, copies, _, _, _, local_sem = parts(refs)
        for n in names:
            pltpu.make_async_copy(own(n), region(n, me), local_sem).start()
        up, down, _, _, _ = ring(me)
        for k, to in ((1, up), (2, down), (0, _peer(me, 1))):
            for cp in copies(k, me, to, True):
                cp.start()

    def middle(*refs):
        me, _, _, copies, whole, _, _, _ = parts(refs)
        up, down, _, passed, target = ring(me)
        sib = _peer(me, 1)
        whole(1).wait_recv()
        whole(2).wait_recv()
        for k, dev, to in ((3, passed, target), (4, down, sib), (5, up, sib)):
            for cp in copies(k, dev, to, False):
                cp.start()

    def finish(*refs):
        me, _, _, copies, whole, flat_ref, out_ref, local_sem = parts(refs)
        _, _, across, _, _ = ring(me)
        whole(3).wait_recv()
        for cp in copies(6, across, _peer(me, 1), False):
            cp.start()
        whole(0).wait_recv()
        for j in range(3):
            whole(4 + j).wait_recv()
        for k in range(7):
            whole(k).wait_send()
        pltpu.make_async_copy(flat_ref.at[pl.ds(0, st.R), :], out_ref.at[pl.ds(0, st.R), :], local_sem).wait()

    return _Comm([flat], [jax.ShapeDtypeStruct((st.W, D), BF)],
                 [pltpu.SemaphoreType.DMA((7,)), pltpu.SemaphoreType.DMA((7,)), pltpu.SemaphoreType.DMA],
                 start, finish, middle)


def _rs_pair_comm(names, src):
    st = _Stage(names)
    arrays = []
    for n in names:
        if not any(src[n][0] is a for a in arrays):
            arrays.append(src[n][0])
    idx = {n: [i for i, a in enumerate(arrays) if a is src[n][0]][0] for n in names}

    def slot_wait(refs):
        recv = refs[1][0]
        send_sem, recv_sem = refs[2]
        return pltpu.make_async_remote_copy(src_ref=recv, dst_ref=recv, send_sem=send_sem, recv_sem=recv_sem,
                                            device_id=_position(), device_id_type=MESH)

    def start(*refs):
        ins, (recv,), (send_sem, recv_sem) = refs
        me = _position()
        sib = _peer(me, 1)
        for q in range(NCHIP):
            dev = (q // 2, q % 2, sib[2])
            for n in names:
                r = st.rows[n]
                pltpu.make_async_remote_copy(
                    src_ref=ins[idx[n]].at[pl.ds(st.grad_row(n, src[n][1], _lin(dev)), r), :],
                    dst_ref=recv.at[q, pl.ds(st.off[n], r), :], send_sem=send_sem, recv_sem=recv_sem,
                    device_id=sib, device_id_type=MESH).start()

    def finish(*refs):
        w = slot_wait(refs)
        w.wait_recv()
        w.wait_send()

    return _Comm(arrays, [jax.ShapeDtypeStruct((NCHIP, st.R, D), BF)],
                 [pltpu.SemaphoreType.DMA, pltpu.SemaphoreType.DMA], start, finish)


def _pair_add(names, src, recv, name):
    st = _Stage(names)
    c_arr = jnp.reshape(lax.axis_index("c"), (1,)).astype(jnp.int32)

    def body(c_ref, *refs):
        r_ref, o_ref = refs[len(names)], refs[len(names) + 1]
        for a_ref, n in zip(refs, names):
            rows = slice(st.off[n], st.off[n] + st.rows[n])
            o_ref[rows, :] = (a_ref[...].astype(F32) + r_ref[rows, :].astype(F32)).astype(BF)

    def shard_spec(n):
        r = st.rows[n]
        base, step = st.grad_row(n, src[n][1], 0) // r, st.full[n] // r
        return pl.BlockSpec((r, D), lambda q, c_ref: (base + step * (2 * q + c_ref[0]), 0))

    slot = pl.BlockSpec((None, st.R, D), lambda q, c_ref: (q, 0, 0))
    return _call(body, name=name, grid=(NCHIP,), args=[c_arr] + [src[n][0] for n in names] + [recv],
                 in_specs=[shard_spec(n) for n in names] + [slot],
                 out_shape=[jax.ShapeDtypeStruct((NCHIP, st.R, D), BF)], out_specs=[slot], num_scalar_prefetch=1)[0]


def _rs_chip_comm(part):
    def copies(refs):
        (p_ref,), (recv,), (send_sems, recv_sems, local_sem) = refs
        me = _position()
        mine = pltpu.make_async_copy(p_ref.at[_chip(me)], recv.at[_chip(me)], local_sem)
        out = []
        for j, bits in enumerate((4, 2, 6)):
            to = _peer(me, bits)
            out.append(pltpu.make_async_remote_copy(
                src_ref=p_ref.at[_chip(to)], dst_ref=recv.at[_chip(me)], send_sem=send_sems.at[j],
                recv_sem=recv_sems.at[j], device_id=to, device_id_type=MESH))
        return mine, out

    def start(*refs):
        mine, out = copies(refs)
        mine.start()
        for cp in out:
            cp.start()

    def finish(*refs):
        mine, out = copies(refs)
        for cp in out:
            cp.wait_recv()
        for cp in out:
            cp.wait_send()
        mine.wait()

    return _Comm([part], [jax.ShapeDtypeStruct(part.shape, BF)],
                 [pltpu.SemaphoreType.DMA((3,)), pltpu.SemaphoreType.DMA((3,)), pltpu.SemaphoreType.DMA],
                 start, finish)


def _direct_comm(x, scatter):
    def copies(refs):
        (x_ref,), (out_ref,), (send_sems, recv_sems, local_sem) = refs
        me = _position()

        def piece(dev):
            return x_ref.at[_lin(dev)] if scatter else x_ref

        mine = pltpu.make_async_copy(piece(me), out_ref.at[_lin(me)], local_sem)
        return mine, [pltpu.make_async_remote_copy(
            src_ref=piece(_peer(me, j)), dst_ref=out_ref.at[_lin(me)], send_sem=send_sems.at[j - 1],
            recv_sem=recv_sems.at[j - 1], device_id=_peer(me, j), device_id_type=MESH) for j in range(1, NDEV)]

    def start(*refs):
        mine, cps = copies(refs)
        mine.start()
        for cp in cps:
            cp.start()

    def finish(*refs):
        mine, cps = copies(refs)
        for cp in cps:
            cp.wait_recv()
        for cp in cps:
            cp.wait_send()
        mine.wait()

    shape = x.shape if scatter else (NDEV,) + x.shape
    return _Comm([x], [jax.ShapeDtypeStruct(shape, x.dtype)],
                 [pltpu.SemaphoreType.DMA((7,)), pltpu.SemaphoreType.DMA((7,)), pltpu.SemaphoreType.DMA],
                 start, finish)


def _pack_weights(shards):
    lay = _Layout()

    def body(*refs):
        o_ref = refs[-1]
        for ref, n in zip(refs, ORDER):
            x = ref[...].T if n == "win" else ref[...]
            o_ref[lay.fl[n]:lay.fl[n] + lay.rows[n], :] = x.astype(BF)

    return pl.pallas_call(
        body, name="pack_weights", out_shape=jax.ShapeDtypeStruct((lay.RT, D), BF),
        compiler_params=pltpu.CompilerParams(vmem_limit_bytes=VMEM_LIMIT))(*[shards[n] for n in ORDER])


def _load_ffn_weights(srcs, offs, scratch, sem):
    @pl.when(pl.program_id(0) == 0)
    def _():
        cps = [pltpu.make_async_copy(s.at[pl.ds(off, dst.shape[0]), :], dst, sem.at[i])
               for i, (s, off, dst) in enumerate(zip(srcs, offs, scratch))]
        for cp in cps:
            cp.start()
        for cp in cps:
            cp.wait()


def _final_loss_tile(xf, g, tgt, s_ref):
    r = lax.rsqrt(jnp.mean(xf * xf, axis=-1, keepdims=True) + EPS)
    xr = xf * r
    e = xr * g - tgt
    s_ref[1:2, :] += jnp.sum(e * e, axis=0, keepdims=True) * (0.5 / D)
    dy = e * (1.0 / D)
    s_ref[0:1, :] += jnp.sum(dy * xr, axis=0, keepdims=True)
    gdy = dy * g
    return r * gdy - xr * (r * jnp.mean(gdy * xr, axis=-1, keepdims=True))


def _ffn_fwd(x, g, wbufs, offs, name, comm=None, final=None):
    nf = F // FC

    def body(x_ref, g_ref, b0, b1, b2, *rest):
        if final is None:
            h_ref, n_ref, gg_ref, uu_ref, a_ref, wg_s, wu_s, wd_s, sem = rest
        else:
            gf_ref, t_ref, dh_ref, dhb_ref, s_ref, n_ref, gg_ref, uu_ref, a_ref, wg_s, wu_s, wd_s, sem = rest

            @pl.when(pl.program_id(0) == 0)
            def _():
                s_ref[...] = jnp.zeros_like(s_ref)

        _load_ffn_weights((b0, b1, b2), offs, (wg_s, wu_s, wd_s), sem)
        xf = x_ref[...]
        r = lax.rsqrt(jnp.mean(xf * xf, axis=-1, keepdims=True) + EPS)
        nb = (xf * r * g_ref[...]).astype(BF)
        n_ref[...] = nb
        acc = jnp.zeros((TM, D), F32)
        for c in range(nf):
            sl = slice(c * FC, (c + 1) * FC)
            gb = _nt(nb, wg_s[sl, :]).astype(BF)
            ub = _nt(nb, wu_s[sl, :]).astype(BF)
            gg_ref[:, sl] = gb
            uu_ref[:, sl] = ub
            a = (gb * _sig(gb)) * ub
            a_ref[0, :, sl] = a
            acc = acc + _nn(a, wd_s[sl, :])
        h = xf + 0.5 * acc
        if final is None:
            h_ref[...] = h
        else:
            dh = _final_loss_tile(h, gf_ref[...], t_ref[...], s_ref)
            dh_ref[...] = dh
            dhb_ref[...] = (0.5 * dh).astype(BF)

    row = lambda i: (i, 0)
    vec = pl.BlockSpec((1, D), lambda i: (0, 0))
    tile = pl.BlockSpec((TM, D), row)
    saved_shapes = [jax.ShapeDtypeStruct((T, D), BF), jax.ShapeDtypeStruct((T, F), BF), jax.ShapeDtypeStruct((T, F), BF),
                    jax.ShapeDtypeStruct((1, T, F), BF)]
    saved_specs = [tile, pl.BlockSpec((TM, F), row), pl.BlockSpec((TM, F), row),
                   pl.BlockSpec((1, TM, F), lambda i: (0, i, 0))]
    if final is None:
        extra_args, extra_specs = [], []
        head_shapes, head_specs = [jax.ShapeDtypeStruct((T, D), F32)], [tile]
    else:
        extra_args, extra_specs = list(final), [vec, tile]
        head_shapes = [jax.ShapeDtypeStruct((T, D), F32), jax.ShapeDtypeStruct((T, D), BF), jax.ShapeDtypeStruct((8, D), F32)]
        head_specs = [tile, tile, pl.BlockSpec((8, D), lambda i: (0, 0))]
    return _call(
        body, name=name, grid=(T // TM,), args=[x, g, *wbufs, *extra_args], comm=comm,
        in_specs=[tile, vec, ANY, ANY, ANY] + extra_specs,
        out_shape=head_shapes + saved_shapes, out_specs=head_specs + saved_specs,
        scratch_shapes=[pltpu.VMEM((F, D), BF)] * 3 + [pltpu.SemaphoreType.DMA((3,))])


def _ffn_gate_up(x, g, wbufs, offs, name, comm=None):
    nf = F // FC

    def body(x_ref, g_ref, b0, b1, n_ref, gg_ref, uu_ref, a_ref, wg_s, wu_s, sem):
        _load_ffn_weights((b0, b1), offs, (wg_s, wu_s), sem)
        xf = x_ref[...]
        r = lax.rsqrt(jnp.mean(xf * xf, axis=-1, keepdims=True) + EPS)
        nb = (xf * r * g_ref[...]).astype(BF)
        n_ref[...] = nb
        for c in range(nf):
            sl = slice(c * FC, (c + 1) * FC)
            gb = _nt(nb, wg_s[sl, :]).astype(BF)
            ub = _nt(nb, wu_s[sl, :]).astype(BF)
            gg_ref[:, sl] = gb
            uu_ref[:, sl] = ub
            a_ref[0, :, sl] = (gb * _sig(gb)) * ub

    row = lambda i: (i, 0)
    tile = pl.BlockSpec((TM, D), row)
    return _call(
        body, name=name, grid=(T // TM,), args=[x, g, *wbufs], comm=comm,
        in_specs=[tile, pl.BlockSpec((1, D), lambda i: (0, 0)), ANY, ANY],
        out_shape=[jax.ShapeDtypeStruct((T, D), BF), jax.ShapeDtypeStruct((T, F), BF), jax.ShapeDtypeStruct((T, F), BF),
                   jax.ShapeDtypeStruct((1, T, F), BF)],
        out_specs=[tile, pl.BlockSpec((TM, F), row), pl.BlockSpec((TM, F), row),
                   pl.BlockSpec((1, TM, F), lambda i: (0, i, 0))],
        scratch_shapes=[pltpu.VMEM((F, D), BF)] * 2 + [pltpu.SemaphoreType.DMA((2,))])


def _ffn_down(x, act, wbuf, off, name, comm=None):
    def body(x_ref, a_ref, b0, h_ref, wd_s, sem):
        _load_ffn_weights((b0,), (off,), (wd_s,), sem)
        h_ref[...] = x_ref[...] + 0.5 * _nn(a_ref[0], wd_s[...])

    tile = pl.BlockSpec((TM, D), lambda i: (i, 0))
    return _call(
        body, name=name, grid=(T // TM,), args=[x, act, wbuf], comm=comm,
        in_specs=[tile, pl.BlockSpec((1, TM, F), lambda i: (0, i, 0)), ANY],
        out_shape=[jax.ShapeDtypeStruct((T, D), F32)], out_specs=[tile],
        scratch_shapes=[pltpu.VMEM((F, D), BF), pltpu.SemaphoreType.DMA((1,))])


def _load_in_proj(parts, w_s, sem):
    @pl.when(pl.program_id(0) == 0)
    def _():
        shard = NG * D // NDEV
        rows = shard // len(parts)
        cps = [pltpu.make_async_copy(buf.at[pl.ds(first + k * rows, rows), :],
                                     w_s.at[pl.ds(k * shard + p * rows, rows), :], sem.at[p * NDEV + k])
               for p, (buf, first) in enumerate(parts) for k in range(NDEV)]
        for cp in cps:
            cp.start()
        for cp in cps:
            cp.wait()


def _mix_in(h1, gm, win, comm=None):
    def body(h_ref, g_ref, *rest):
        w_any, (u_ref, z_ref, w_s, sem) = rest[:len(win)], rest[len(win):]
        _load_in_proj([(b, first) for b, (_, first) in zip(w_any, win)], w_s, sem)
        xf = h_ref[...]
        r = lax.rsqrt(jnp.mean(xf * xf, axis=-1, keepdims=True) + EPS)
        ub = (xf * r * g_ref[...]).astype(BF)
        u_ref[...] = ub
        for j in range(NG):
            z_ref[j] = _nt(ub, w_s[j * D:(j + 1) * D, :]).astype(BF)

    row = lambda i: (i, 0)
    return _call(
        body, name="mix_in", grid=(T // TM,), args=[h1, gm] + [b for b, _ in win], comm=comm,
        in_specs=[pl.BlockSpec((TM, D), row), pl.BlockSpec((1, D), lambda i: (0, 0))] + [ANY] * len(win),
        out_shape=[jax.ShapeDtypeStruct((T, D), BF), jax.ShapeDtypeStruct((NG, T, D), BF)],
        out_specs=[pl.BlockSpec((TM, D), row), pl.BlockSpec((NG, TM, D), lambda i: (0, i, 0))],
        scratch_shapes=[pltpu.VMEM((NG * D, D), BF), pltpu.SemaphoreType.DMA((NDEV * len(win),))])


def _shift_up(w, b):
    return w if b == 0 else pltpu.roll(w, w.shape[0] - b, 0)


def _fold8(p):
    red = p[0:8, :]
    for i in range(1, p.shape[0] // 8):
        red = red + p[8 * i:8 * i + 8, :]
    return red


def _dft_constants():
    import numpy as np
    nh = NB // 2
    f, n = np.arange(nh)[:, None], np.arange(NB)[None, :]
    ang = 2.0 * np.pi / NB * f * n
    fc = np.cos(ang)
    fs = np.where(f == 0, (-1.0) ** n, np.sin(ang))
    scale = np.where(f == 0, 1.0, 2.0) / NB
    ic = (scale * np.cos(ang)).T
    isn = np.where(f == 0, (-1.0) ** n / NB, scale * np.sin(ang)).T
    d = (KA - 1 - np.arange(32))[None, :]
    valid = (np.arange(32) < KA)[None, :]
    angk = 2.0 * np.pi / NB * f * d
    kc = np.where(valid, np.cos(angk), 0.0)
    ks = np.where(valid, np.sin(angk), 0.0)
    k2 = np.where(valid, np.where(f == 0, (-1.0) ** d, np.cos(angk)), 0.0)
    rtc = np.where(valid, scale * np.cos(angk), 0.0).T
    rts = np.where(valid, np.where(f == 0, (-1.0) ** d / NB, scale * np.sin(angk)), 0.0).T

    def bf(a):
        return jnp.asarray(a, F32).astype(BF)

    def split(a):
        hi = bf(a)
        return hi, (jnp.asarray(a, F32) - hi.astype(F32)).astype(BF)

    return dict(fc=bf(fc), fs=bf(fs), ic_hi=bf(ic[HB:]), is_hi=bf(isn[HB:]), ic_lo=bf(ic[:HB]), is_lo=bf(isn[:HB]),
                kc=split(kc), ks=split(ks), k2=split(k2), rtc=split(rtc), rts=split(rts))


def _dot3(m_hi, m_lo, x):
    x_hi = x.astype(BF)
    x_lo = (x - x_hi.astype(F32)).astype(BF)
    return _nn(m_hi, x_hi) + _nn(m_hi, x_lo) + _nn(m_lo, x_hi)


def _whole(a):
    return pl.BlockSpec(a.shape, lambda c, t: (0,) * a.ndim)


def _filter_spectrum(cw_ref, tabs, hc, hs, h2):
    w32 = cw_ref[0:32, :]
    for (hi, lo), dst in zip(tabs, (hc, hs, h2)):
        dst[...] = _dot3(hi[...], lo[...], w32)


def _conv_fwd_dft(z, cw, bias, dft, comm=None):
    nt = T // TB
    hb = TB // HB

    def body(z_ref, zh_ref, cw_ref, b_ref, fc_ref, fs_ref, ic_ref, is_ref, kch, kcl, ksh, ksl, k2h, k2l,
             a1_ref, q_ref, aext, ppad, hc, hs, h2):
        first = pl.program_id(1) == 0
        f = lambda ref, j: ref[j].astype(F32)

        @pl.when(first)
        def _():
            _filter_spectrum(cw_ref, ((kch, kcl), (ksh, ksl), (k2h, k2l)), hc, hs, h2)

        aext[0:HB, :] = jnp.where(first, 0.0, zh_ref[0] * _sig(zh_ref[1])).astype(BF)
        aext[HB:, :] = z_ref[0] * _sig(z_ref[1])
        ppad[0:8, :] = jnp.where(first, 0.0, f(zh_ref, 3)[HB - 8:HB, :] * f(zh_ref, 4)[HB - 8:HB, :])
        ppad[8:, :] = f(z_ref, 3) * f(z_ref, 4)
        bias_row = b_ref[...]

        for j in range(TB // HB):
            xs = aext[j * HB:j * HB + NB, :]
            xa, xb = _nn(fc_ref[...], xs), _nn(fs_ref[...], xs)
            yc = (hc[...] * xa - hs[...] * xb).astype(BF)
            ys = (h2[...] * xb + hs[...] * xa).astype(BF)
            y = _nn(ic_ref[...], yc) + _nn(is_ref[...], ys)
            a1_ref[j * HB:(j + 1) * HB, :] = (y + bias_row).astype(BF)

        def chunk(r, carry):
            base = pl.multiple_of(r * CHB, CHB)
            pw = ppad[pl.ds(base, CHB + 8), :]
            v = (cw_ref[pl.ds(32, 1), :] * _shift_up(pw, 6)[0:CHB, :]
                 + cw_ref[pl.ds(33, 1), :] * _shift_up(pw, 7)[0:CHB, :]
                 + cw_ref[pl.ds(34, 1), :] * pw[8:8 + CHB, :])
            q_ref[pl.ds(base, CHB), :] = (z_ref[2, pl.ds(base, CHB), :].astype(F32) * v).astype(BF)
            return carry

        lax.fori_loop(0, TB // CHB, chunk, 0)

    blk = pl.BlockSpec((TB, CW), lambda c, t: (t, c))
    tabs = [dft["fc"], dft["fs"], dft["ic_hi"], dft["is_hi"], *dft["kc"], *dft["ks"], *dft["k2"]]
    return _call(
        body, name="conv_fwd", grid=(D // CW, nt), comm=comm, args=[z, z, cw, bias] + tabs,
        in_specs=[pl.BlockSpec((5, TB, CW), lambda c, t: (0, t, c)),
                  pl.BlockSpec((5, HB, CW), lambda c, t: (0, jnp.maximum(t * hb - 1, 0), c)),
                  pl.BlockSpec((40, CW), lambda c, t: (0, c)), pl.BlockSpec((1, CW), lambda c, t: (0, c))]
                 + [_whole(a) for a in tabs],
        out_shape=[jax.ShapeDtypeStruct((T, D), BF), jax.ShapeDtypeStruct((T, D), BF)], out_specs=[blk, blk],
        scratch_shapes=[pltpu.VMEM((TB + HB, CW), BF), pltpu.VMEM((TB + 8, CW), F32)]
                       + [pltpu.VMEM((NB // 2, CW), F32)] * 3)


def _conv_bwd_dft(z, da1, dq, dzg, cw, dft, comm=None):
    nt = T // TB
    hb = TB // HB
    last_h = T // HB - 1

    def body(z_ref, zp_ref, zn_ref, da1_ref, da1n_ref, dq_ref, dqn_ref, dzg_ref, cw_ref,
             fc_ref, fs_ref, ic_ref, is_ref, kch, kcl, ksh, ksl, k2h, k2l, rch, rcl, rsh, rsl,
             dz_ref, dwa_ref, dwb_ref, aext, dyext, ppad, dvpad, hc, hs, h2, rc, rs, nyq, acc_b):
        t = pl.program_id(1)
        first, last = t == 0, t == nt - 1
        f = lambda ref, j: ref[j].astype(F32)

        @pl.when(first)
        def _():
            _filter_spectrum(cw_ref, ((kch, kcl), (ksh, ksl), (k2h, k2l)), hc, hs, h2)
            rc[...] = jnp.zeros_like(rc)
            rs[...] = jnp.zeros_like(rs)
            nyq[...] = jnp.zeros_like(nyq)
            acc_b[...] = jnp.zeros_like(acc_b)

        aext[0:HB, :] = jnp.where(first, 0.0, zp_ref[0] * _sig(zp_ref[1])).astype(BF)
        aext[HB:, :] = z_ref[0] * _sig(z_ref[1])
        dyext[0:TB, :] = da1_ref[...]
        dyext[TB:, :] = jnp.where(last, 0.0, da1n_ref[...].astype(F32)).astype(BF)
        ppad[0:8, :] = jnp.where(first, 0.0, f(zp_ref, 3)[HB - 8:HB, :] * f(zp_ref, 4)[HB - 8:HB, :])
        ppad[8:, :] = f(z_ref, 3) * f(z_ref, 4)
        dvpad[0:TB, :] = dq_ref[...].astype(F32) * f(z_ref, 2)
        dvpad[TB:, :] = jnp.where(last, 0.0, dqn_ref[...].astype(F32)[0:8, :] * f(zn_ref, 2)[0:8, :])

        for j in range(TB // HB):
            rows = slice(j * HB, (j + 1) * HB)
            dys = dyext[j * HB:j * HB + NB, :]
            da, db = _nn(fc_ref[...], dys), _nn(fs_ref[...], dys)
            gc = (hc[...] * da + hs[...] * db).astype(BF)
            gs = (h2[...] * db - hs[...] * da).astype(BF)
            da0 = _nn(ic_ref[...], gc) + _nn(is_ref[...], gs)
            z0, z1 = z_ref[0, rows, :].astype(F32), z_ref[1, rows, :].astype(F32)
            s1 = _sig(z1)
            dz_ref[0, rows, :] = (da0 * s1).astype(BF)
            dz_ref[1, rows, :] = (da0 * z0 * (s1 * (1.0 - s1))).astype(BF)
            xs = aext[j * HB:j * HB + NB, :]
            xa, xb = _nn(fc_ref[...], xs), _nn(fs_ref[...], xs)
            dyb = dyext[rows, :]
            pa, pb = _nn(fc_ref[:, HB:NB], dyb), _nn(fs_ref[:, HB:NB], dyb)
            rc[...] += pa * xa + pb * xb
            rs[...] += pb * xa - pa * xb
            nyq[...] += pb[0:8, :] * xb[0:8, :]

        def chunk(r, carry):
            base = pl.multiple_of(r * CHB, CHB)
            rows = pl.ds(base, CHB)
            pw = ppad[pl.ds(base, CHB + 8), :]
            p6 = _shift_up(pw, 6)[0:CHB, :]
            p7 = _shift_up(pw, 7)[0:CHB, :]
            p8 = pw[8:8 + CHB, :]
            wb0, wb1, wb2 = cw_ref[pl.ds(32, 1), :], cw_ref[pl.ds(33, 1), :], cw_ref[pl.ds(34, 1), :]
            v = wb0 * p6 + wb1 * p7 + wb2 * p8
            dz_ref[2, rows, :] = (dq_ref[rows, :].astype(F32) * v).astype(BF)
            dvw = dvpad[pl.ds(base, CHB + 8), :]
            dvc = dvw[0:CHB, :]
            dp = wb2 * dvc + wb1 * _shift_up(dvw, 1)[0:CHB, :] + wb0 * _shift_up(dvw, 2)[0:CHB, :]
            dz_ref[3, rows, :] = (dp * z_ref[4, rows, :].astype(F32)).astype(BF)
            dz_ref[4, rows, :] = (dp * z_ref[3, rows, :].astype(F32)).astype(BF)
            acc_b[0:8, :] += _fold8(dvc * p6)
            acc_b[8:16, :] += _fold8(dvc * p7)
            acc_b[16:24, :] += _fold8(dvc * p8)
            dz_ref[5, rows, :] = dzg_ref[0, rows, :]
            dz_ref[6, rows, :] = dzg_ref[1, rows, :]
            return carry

        lax.fori_loop(0, TB // CHB, chunk, 0)

        @pl.when(last)
        def _():
            row0 = lax.broadcasted_iota(jnp.int32, (NB // 2, CW), 0) == 0
            ny = jnp.broadcast_to(nyq[0:1, :], (NB // 2, CW))
            rcv = jnp.where(row0, rc[...] - ny, rc[...])
            rsv = jnp.where(row0, ny, rs[...])
            dwa_ref[...] = _dot3(rch[...], rcl[...], rcv) + _dot3(rsh[...], rsl[...], rsv)
            for k in range(KB):
                dwb_ref[k:k + 1, :] = jnp.sum(acc_b[8 * k:8 * k + 8, :], axis=0, keepdims=True)
            dwb_ref[KB:8, :] = jnp.zeros((8 - KB, CW), F32)

    blk = lambda c, t: (t, c)
    nxt = lambda c, t: (jnp.minimum((t + 1) * hb, last_h), c)
    tabs = [dft["fc"], dft["fs"], dft["ic_lo"], dft["is_lo"], *dft["kc"], *dft["ks"], *dft["k2"], *dft["rtc"], *dft["rts"]]
    return _call(
        body, name="conv_bwd", grid=(D // CW, nt), comm=comm, args=[z, z, z, da1, da1, dq, dq, dzg, cw] + tabs,
        in_specs=[pl.BlockSpec((5, TB, CW), lambda c, t: (0, t, c)),
                  pl.BlockSpec((5, HB, CW), lambda c, t: (0, jnp.maximum(t * hb - 1, 0), c)),
                  pl.BlockSpec((5, HB, CW), lambda c, t: (0, jnp.minimum((t + 1) * hb, last_h), c)),
                  pl.BlockSpec((TB, CW), blk), pl.BlockSpec((HB, CW), nxt),
                  pl.BlockSpec((TB, CW), blk), pl.BlockSpec((HB, CW), nxt),
                  pl.BlockSpec((2, TB, CW), lambda c, t: (0, t, c)),
                  pl.BlockSpec((40, CW), lambda c, t: (0, c))]
                 + [_whole(a) for a in tabs],
        out_shape=[jax.ShapeDtypeStruct((NG, T, D), BF), jax.ShapeDtypeStruct((32, D), F32),
                   jax.ShapeDtypeStruct((8, D), F32)],
        out_specs=[pl.BlockSpec((NG, TB, CW), lambda c, t: (0, t, c)),
                   pl.BlockSpec((32, CW), lambda c, t: (0, c)), pl.BlockSpec((8, CW), lambda c, t: (0, c))],
        scratch_shapes=[pltpu.VMEM((TB + HB, CW), BF), pltpu.VMEM((TB + HB, CW), BF),
                        pltpu.VMEM((TB + 8, CW), F32), pltpu.VMEM((TB + 8, CW), F32)]
                       + [pltpu.VMEM((NB // 2, CW), F32)] * 5 + [pltpu.VMEM((8, CW), F32), pltpu.VMEM((24, CW), F32)])


def _layernorm_silu(a1, lng, lnb):
    mu = jnp.mean(a1, axis=-1, keepdims=True)
    xc = a1 - mu
    rs = lax.rsqrt(jnp.mean(xc * xc, axis=-1, keepdims=True) + EPS)
    xh = xc * rs
    a2 = xh * lng + lnb
    sg = _sig(a2)
    return xh, rs, a2, sg


def _square_specs(blocks):
    return [pl.BlockSpec((D, D), lambda i, b=b: (b, 0)) for b in blocks]


def _mix_out(a1, q, z, h1, lng, lnb, wsq, comm=None):
    def body(a1_ref, q_ref, ga_ref, gb_ref, h_ref, lng_ref, lnb_ref, wa_ref, wb_ref, wo_ref, h2_ref, ya_ref, yb_ref):
        _, _, a2, sg = _layernorm_silu(a1_ref[...].astype(F32), lng_ref[...], lnb_ref[...])
        ya = _nn((a2 * sg).astype(BF), wa_ref[...]).astype(BF)
        yb = _nn(q_ref[...], wb_ref[...]).astype(BF)
        ya_ref[...] = ya
        yb_ref[...] = yb
        m = _sig(ga_ref[...]) * ya + _sig(gb_ref[...]) * yb
        h2_ref[...] = h_ref[...] + _nn(m, wo_ref[...])

    row = lambda i: (i, 0)
    vec = pl.BlockSpec((1, D), lambda i: (0, 0))
    return _call(
        body, name="mix_out", grid=(T // TM,), args=[a1, q, z, z, h1, lng, lnb, wsq, wsq, wsq], comm=comm,
        in_specs=[pl.BlockSpec((TM, D), row), pl.BlockSpec((TM, D), row),
                  pl.BlockSpec((None, TM, D), lambda i: (5, i, 0)), pl.BlockSpec((None, TM, D), lambda i: (6, i, 0)),
                  pl.BlockSpec((TM, D), row), vec, vec] + _square_specs((0, 1, 2)),
        out_shape=[jax.ShapeDtypeStruct((T, D), F32), jax.ShapeDtypeStruct((T, D), BF), jax.ShapeDtypeStruct((T, D), BF)],
        out_specs=[pl.BlockSpec((TM, D), row)] * 3)


def _rmsnorm_bwd(xf, g, dn):
    r = lax.rsqrt(jnp.mean(xf * xf, axis=-1, keepdims=True) + EPS)
    xr = xf * r
    gdn = dn * g
    dx = r * gdn - xr * (r * jnp.mean(gdn * xr, axis=-1, keepdims=True))
    return dx, jnp.sum(dn * xr, axis=0, keepdims=True)


def _ffn_bwd_hidden(dh, gg, uu, wbuf, off, name, comm=None):
    nf = F // FC

    def body(dh_ref, gg_ref, uu_ref, b0, dgu_ref, wd_s, sem):
        _load_ffn_weights((b0,), (off,), (wd_s,), sem)
        dhb = dh_ref[...]
        for c in range(nf):
            sl = slice(c * FC, (c + 1) * FC)
            da = _nt(dhb, wd_s[sl, :]).astype(BF)
            gb, ub = gg_ref[:, sl], uu_ref[:, sl]
            sg = _sig(gb)
            dgu_ref[0, :, sl] = (da * ub) * (sg * (1.0 + gb * (1.0 - sg)))
            dgu_ref[0, :, F + c * FC:F + (c + 1) * FC] = da * (gb * sg)

    row = lambda i: (i, 0)
    return _call(
        body, name=name, grid=(T // TM,), args=[dh, gg, uu, wbuf], comm=comm,
        in_specs=[pl.BlockSpec((TM, D), row), pl.BlockSpec((TM, F), row), pl.BlockSpec((TM, F), row), ANY],
        out_shape=[jax.ShapeDtypeStruct((1, T, 2 * F), BF)],
        out_specs=[pl.BlockSpec((1, TM, 2 * F), lambda i: (0, i, 0))],
        scratch_shapes=[pltpu.VMEM((F, D), BF), pltpu.SemaphoreType.DMA((1,))])


def _ffn_bwd_input(dgu, dh, x, g, wbufs, offs, name, comm=None):
    def body(dgu_ref, dh_ref, x_ref, g_ref, b0, b1, dx_ref, s_ref, w_s, sem):
        _load_ffn_weights((b0, b1), offs, (w_s.at[pl.ds(0, F), :], w_s.at[pl.ds(F, F), :]), sem)

        @pl.when(pl.program_id(0) == 0)
        def _():
            s_ref[...] = jnp.zeros_like(s_ref)

        dn = _nn(dgu_ref[0], w_s[...])
        dxn, dg = _rmsnorm_bwd(x_ref[...], g_ref[...], dn)
        dx_ref[...] = dh_ref[...] + dxn
        s_ref[0:1, :] += dg

    row = lambda i: (i, 0)
    return _call(
        body, name=name, grid=(T // TM,), args=[dgu, dh, x, g, *wbufs], comm=comm,
        in_specs=[pl.BlockSpec((1, TM, 2 * F), lambda i: (0, i, 0)), pl.BlockSpec((TM, D), row),
                  pl.BlockSpec((TM, D), row), pl.BlockSpec((1, D), lambda i: (0, 0)), ANY, ANY],
        out_shape=[jax.ShapeDtypeStruct((T, D), F32), jax.ShapeDtypeStruct((8, D), F32)],
        out_specs=[pl.BlockSpec((TM, D), row), pl.BlockSpec((8, D), lambda i: (0, 0))],
        scratch_shapes=[pltpu.VMEM((2 * F, D), BF), pltpu.SemaphoreType.DMA((2,))])


def _tn_matmul(lhs, rhs, tr, name, comm=None):
    ng, _, cdim = lhs.shape
    nc, nk = cdim // tr, T // TK
    if rhs.ndim == 2:
        r_spec = pl.BlockSpec((TK, D), lambda g, c, k: (k, 0))
    else:
        r_spec = pl.BlockSpec((None, TK, D), lambda g, c, k: (g, k, 0))

    def body(l_ref, r_ref, o_ref, acc):
        k = pl.program_id(2)

        @pl.when(k == 0)
        def _():
            acc[...] = jnp.zeros_like(acc)

        acc[...] += _tn(l_ref[...], r_ref[...])

        @pl.when(k == nk - 1)
        def _():
            o_ref[...] = acc[...].astype(BF)

    return _call(
        body, name=name, grid=(ng, nc, nk), args=[lhs, rhs], comm=comm,
        in_specs=[pl.BlockSpec((None, TK, tr), lambda g, c, k: (g, k, c)), r_spec],
        out_shape=[jax.ShapeDtypeStruct((ng * cdim, D), BF)],
        out_specs=[pl.BlockSpec((tr, D), lambda g, c, k: (g * nc + c, 0))],
        scratch_shapes=[pltpu.VMEM((tr, D), F32)])


def _mix_out_bwd(dh2, ya, yb, z, a1, q, lng, lnb, wsq, comm=None):
    def body(dh_ref, ya_ref, yb_ref, ga_ref, gb_ref, a1_ref, q_ref, lng_ref, lnb_ref, wa_ref, wb_ref, wo_ref,
             dzg_ref, da1_ref, dq_ref, l_ref, r_ref, s_ref):
        @pl.when(pl.program_id(0) == 0)
        def _():
            s_ref[...] = jnp.zeros_like(s_ref)

        dhb = dh_ref[...].astype(BF)
        dm = _nt(dhb, wo_ref[...]).astype(BF)
        ya, yb = ya_ref[...], yb_ref[...]
        sa, sb = _sig(ga_ref[...]), _sig(gb_ref[...])
        l_ref[0] = sa * ya + sb * yb
        l_ref[2] = q_ref[...]
        dzg_ref[0] = (dm * ya) * (sa * (1.0 - sa))
        dzg_ref[1] = (dm * yb) * (sb * (1.0 - sb))
        dya = dm * sa
        dyb = dm * sb
        r_ref[0] = dhb
        r_ref[1] = dya
        r_ref[2] = dyb
        dq_ref[...] = _nt(dyb, wb_ref[...]).astype(BF)
        da3 = _nt(dya, wa_ref[...])
        lng = lng_ref[...]
        xh, rs, a2, sg = _layernorm_silu(a1_ref[...].astype(F32), lng, lnb_ref[...])
        l_ref[1] = (a2 * sg).astype(BF)
        da2 = da3 * (sg * (1.0 + a2 * (1.0 - sg)))
        s_ref[0:1, :] += jnp.sum(da2 * xh, axis=0, keepdims=True)
        s_ref[1:2, :] += jnp.sum(da2, axis=0, keepdims=True)
        dxh = da2 * lng
        da1 = rs * (dxh - jnp.mean(dxh, axis=-1, keepdims=True) - xh * jnp.mean(dxh * xh, axis=-1, keepdims=True))
        da1_ref[...] = da1.astype(BF)
        s_ref[2:3, :] += jnp.sum(da1, axis=0, keepdims=True)

    row = lambda i: (i, 0)
    row3 = lambda i: (0, i, 0)
    vec = pl.BlockSpec((1, D), lambda i: (0, 0))
    return _call(
        body, name="mix_out_bwd", grid=(T // TM,), args=[dh2, ya, yb, z, z, a1, q, lng, lnb, wsq, wsq, wsq], comm=comm,
        in_specs=[pl.BlockSpec((TM, D), row), pl.BlockSpec((TM, D), row), pl.BlockSpec((TM, D), row),
                  pl.BlockSpec((None, TM, D), lambda i: (5, i, 0)), pl.BlockSpec((None, TM, D), lambda i: (6, i, 0)),
                  pl.BlockSpec((TM, D), row), pl.BlockSpec((TM, D), row), vec, vec] + _square_specs((0, 1, 2)),
        out_shape=[jax.ShapeDtypeStruct((2, T, D), BF), jax.ShapeDtypeStruct((T, D), BF),
                   jax.ShapeDtypeStruct((T, D), BF), jax.ShapeDtypeStruct((3, T, D), BF),
                   jax.ShapeDtypeStruct((3, T, D), BF), jax.ShapeDtypeStruct((8, D), F32)],
        out_specs=[pl.BlockSpec((2, TM, D), row3), pl.BlockSpec((TM, D), row), pl.BlockSpec((TM, D), row),
                   pl.BlockSpec((3, TM, D), row3), pl.BlockSpec((3, TM, D), row3), pl.BlockSpec((8, D), lambda i: (0, 0))])


def _mix_in_bwd(dz, dh2, h1, gm, win, comm=None):
    def body(dz_ref, dh_ref, h_ref, g_ref, *rest):
        w_any, (o_ref, ob_ref, s_ref, w_s, sem) = rest[:len(win)], rest[len(win):]
        _load_in_proj([(b, first) for b, (_, first) in zip(w_any, win)], w_s, sem)

        @pl.when(pl.program_id(0) == 0)
        def _():
            s_ref[...] = jnp.zeros_like(s_ref)

        du = _nn(dz_ref[0], w_s[0:D, :])
        for j in range(1, NG):
            du = du + _nn(dz_ref[j], w_s[j * D:(j + 1) * D, :])
        dx, dg = _rmsnorm_bwd(h_ref[...], g_ref[...], du)
        dh1 = dh_ref[...] + dx
        o_ref[...] = dh1
        ob_ref[...] = (0.5 * dh1).astype(BF)
        s_ref[0:1, :] += dg

    row = lambda i: (i, 0)
    return _call(
        body, name="mix_in_bwd", grid=(T // TM,), args=[dz, dh2, h1, gm] + [b for b, _ in win], comm=comm,
        in_specs=[pl.BlockSpec((NG, TM, D), lambda i: (0, i, 0)), pl.BlockSpec((TM, D), row),
                  pl.BlockSpec((TM, D), row), pl.BlockSpec((1, D), lambda i: (0, 0))] + [ANY] * len(win),
        out_shape=[jax.ShapeDtypeStruct((T, D), F32), jax.ShapeDtypeStruct((T, D), BF), jax.ShapeDtypeStruct((8, D), F32)],
        out_specs=[pl.BlockSpec((TM, D), row), pl.BlockSpec((TM, D), row), pl.BlockSpec((8, D), lambda i: (0, 0))],
        scratch_shapes=[pltpu.VMEM((NG * D, D), BF), pltpu.SemaphoreType.DMA((NDEV * len(win),))])


def _row_tile(n, want, mult):
    for t in range(min(want, n), 0, -1):
        if n % t == 0 and t % mult == 0:
            return t
    return n


def _sum_slots(recv, name):
    ns, rows, cols = recv.shape
    tr = _row_tile(rows, 1024, 16)

    def body(r_ref, o_ref):
        s = r_ref[0].astype(F32)
        for k in range(1, ns):
            s = s + r_ref[k].astype(F32)
        o_ref[...] = s

    return _call(
        body, name=name, grid=(rows // tr,), args=[recv],
        in_specs=[pl.BlockSpec((ns, tr, cols), lambda i: (0, i, 0))],
        out_shape=[jax.ShapeDtypeStruct((rows, cols), F32)],
        out_specs=[pl.BlockSpec((tr, cols), lambda i: (i, 0))])[0]


def _pack_small(s_ffn1, s_in, s_mix, s_ffn2, s_final, dwa, dwb):
    def body(f1, mi, mo, f2, fl, wa_ref, wb_ref, v_ref, k_ref):
        for dst, (ref, row) in enumerate(((f1, 0), (mi, 0), (mo, 0), (mo, 1), (mo, 2), (f2, 0), (fl, 0), (fl, 1))):
            v_ref[dst:dst + 1, :] = ref[row:row + 1, :]
        for k in range(NDEV):
            k_ref[k, 0:32, :] = wa_ref[:, k * LANE:(k + 1) * LANE]
            k_ref[k, 32:40, :] = wb_ref[:, k * LANE:(k + 1) * LANE]

    return pl.pallas_call(
        body, name="pack_small",
        out_shape=(jax.ShapeDtypeStruct((8, D), F32), jax.ShapeDtypeStruct((NDEV, 40, LANE), F32)),
    )(s_ffn1, s_in, s_mix, s_ffn2, s_final, dwa, dwb)


def _sum_small(vecs, convs):
    def body(v_ref, k_ref, vs_ref, ks_ref, l_ref):
        s, c = v_ref[0], k_ref[0]
        for k in range(1, NDEV):
            s = s + v_ref[k]
            c = c + k_ref[k]
        vs_ref[...] = s
        ks_ref[...] = c
        l_ref[...] = jnp.broadcast_to(jnp.sum(s[7:8, :], axis=-1, keepdims=True), (8, LANE))

    return pl.pallas_call(
        body, name="sum_small",
        out_shape=(jax.ShapeDtypeStruct((8, D), F32), jax.ShapeDtypeStruct((40, LANE), F32),
                   jax.ShapeDtypeStruct((8, LANE), F32)),
    )(vecs, convs)


def _adam(gs, ws, ms, vs, name, comm=None):
    n = len(gs)
    rows, cols = ws[0].shape
    tr = _row_tile(rows, 256, 16)
    c1 = 1.0 - ADAM_B1 ** ADAM_STEP
    c2 = 1.0 - ADAM_B2 ** ADAM_STEP
    summed = [isinstance(g, tuple) for g in gs]

    def body(*refs):
        for i in range(n):
            g_in, w, m, v = refs[4 * i], refs[4 * i + 1][...], refs[4 * i + 2][...], refs[4 * i + 3][...]
            g_ref, d_ref, m_ref, v_ref = refs[4 * n + 4 * i: 4 * n + 4 * i + 4]
            if summed[i]:
                g = g_in[0].astype(F32)
                for k in range(1, g_in.shape[0]):
                    g = g + g_in[k].astype(F32)
            else:
                g = g_in[...]
            g_ref[...] = g
            m2 = ADAM_B1 * m + (1.0 - ADAM_B1) * g
            v2 = ADAM_B2 * v + (1.0 - ADAM_B2) * (g * g)
            d_ref[...] = -ADAM_LR * ((m2 / c1) / (jnp.sqrt(v2 / c2) + ADAM_EPS) + ADAM_WD * w)
            m_ref[...] = m2
            v_ref[...] = v2

    spec = pl.BlockSpec((tr, cols), lambda i: (i, 0))
    args, in_specs = [], []
    for i in range(n):
        if summed[i]:
            slots, first = gs[i]
            args.append(slots)
            in_specs.append(pl.BlockSpec((slots.shape[0], tr, cols), lambda i, b=first // tr: (0, b + i, 0)))
        else:
            args.append(gs[i])
            in_specs.append(spec)
        args += [ws[i], ms[i], vs[i]]
        in_specs += [spec] * 3
    outs = _call(body, name=name, grid=(rows // tr,), args=args, comm=comm, in_specs=in_specs,
                 out_shape=[jax.ShapeDtypeStruct((rows, cols), F32)] * (4 * n), out_specs=[spec] * (4 * n))
    return [tuple(outs[4 * i: 4 * i + 4]) for i in range(n)], outs[4 * n:]


def kernel(x, ffn1_norm, ffn1_w_gate, ffn1_w_up, ffn1_w_down, mix_norm, w_in, a_dw_w, a_dw_b, a_ln_g, a_ln_b, a_w_out, b_conv_w, b_w_out, w_o, ffn2_norm, ffn2_w_gate, ffn2_w_up, ffn2_w_down, final_norm, loss_target, m_ffn1_norm, m_ffn1_w_gate, m_ffn1_w_up, m_ffn1_w_down, m_mix_norm, m_w_in, m_a_dw_w, m_a_dw_b, m_a_ln_g, m_a_ln_b, m_a_w_out, m_b_conv_w, m_b_w_out, m_w_o, m_ffn2_norm, m_ffn2_w_gate, m_ffn2_w_up, m_ffn2_w_down, m_final_norm, v_ffn1_norm, v_ffn1_w_gate, v_ffn1_w_up, v_ffn1_w_down, v_mix_norm, v_w_in, v_a_dw_w, v_a_dw_b, v_a_ln_g, v_a_ln_b, v_a_w_out, v_b_conv_w, v_b_w_out, v_w_o, v_ffn2_norm, v_ffn2_w_gate, v_ffn2_w_up, v_ffn2_w_down, v_final_norm):
    names = ("ffn1_norm", "ffn1_w_gate", "ffn1_w_up", "ffn1_w_down", "mix_norm", "w_in", "a_dw_w", "a_dw_b",
             "a_ln_g", "a_ln_b", "a_w_out", "b_conv_w", "b_w_out", "w_o", "ffn2_norm", "ffn2_w_gate", "ffn2_w_up",
             "ffn2_w_down", "final_norm")
    w = dict(ffn1_norm=ffn1_norm, ffn1_w_gate=ffn1_w_gate, ffn1_w_up=ffn1_w_up, ffn1_w_down=ffn1_w_down,
             mix_norm=mix_norm, w_in=w_in, a_dw_w=a_dw_w, a_dw_b=a_dw_b, a_ln_g=a_ln_g, a_ln_b=a_ln_b,
             a_w_out=a_w_out, b_conv_w=b_conv_w, b_w_out=b_w_out, w_o=w_o, ffn2_norm=ffn2_norm,
             ffn2_w_gate=ffn2_w_gate, ffn2_w_up=ffn2_w_up, ffn2_w_down=ffn2_w_down, final_norm=final_norm)
    m = dict(ffn1_norm=m_ffn1_norm, ffn1_w_gate=m_ffn1_w_gate, ffn1_w_up=m_ffn1_w_up, ffn1_w_down=m_ffn1_w_down,
             mix_norm=m_mix_norm, w_in=m_w_in, a_dw_w=m_a_dw_w, a_dw_b=m_a_dw_b, a_ln_g=m_a_ln_g, a_ln_b=m_a_ln_b,
             a_w_out=m_a_w_out, b_conv_w=m_b_conv_w, b_w_out=m_b_w_out, w_o=m_w_o, ffn2_norm=m_ffn2_norm,
             ffn2_w_gate=m_ffn2_w_gate, ffn2_w_up=m_ffn2_w_up, ffn2_w_down=m_ffn2_w_down, final_norm=m_final_norm)
    v = dict(ffn1_norm=v_ffn1_norm, ffn1_w_gate=v_ffn1_w_gate, ffn1_w_up=v_ffn1_w_up, ffn1_w_down=v_ffn1_w_down,
             mix_norm=v_mix_norm, w_in=v_w_in, a_dw_w=v_a_dw_w, a_dw_b=v_a_dw_b, a_ln_g=v_a_ln_g, a_ln_b=v_a_ln_b,
             a_w_out=v_a_w_out, b_conv_w=v_b_conv_w, b_w_out=v_b_w_out, w_o=v_w_o, ffn2_norm=v_ffn2_norm,
             ffn2_w_gate=v_ffn2_w_gate, ffn2_w_up=v_ffn2_w_up, ffn2_w_down=v_ffn2_w_down, final_norm=v_final_norm)
    flat = _pack_weights(dict(wg1=ffn1_w_gate[0].T, wu1=ffn1_w_up[0].T, wd1=ffn1_w_down[0], wg2=ffn2_w_gate[0].T,
                              wu2=ffn2_w_up[0].T, wd2=ffn2_w_down[0], win=w_in[0], wa=a_w_out[0], wb=b_w_out[0],
                              wo=w_o[0]))
    cw_shard = jnp.concatenate([a_dw_w[0], jnp.zeros((1, LANE), F32), b_conv_w[0], jnp.zeros((5, LANE), F32)], axis=0)

    x2, tgt = x[0], loss_target[0]
    st_a, st_b, st_b2 = ("wg1", "wu1"), ("wd1", "win/0/2"), ("win/1/2",)
    st_c, st_d, st_e = ("wa", "wb", "wo", "wg2"), ("wu2",), ("wd2",)

    buf_a, cw = _run_comm(_join(_ag_comm(st_a, flat), _direct_comm(cw_shard, False)), "ag_ffn1")
    n1, gg1, uu1, act1, buf_b = _ffn_gate_up(x2, ffn1_norm, (buf_a, buf_a), (0, F), "ffn1_gate_up", _ag_comm(st_b, flat))
    h1, buf_b2 = _ffn_down(x2, act1, buf_b, 0, "ffn1_down", _ag_comm(st_b2, flat))
    win = ((buf_b, F), (buf_b2, 0))
    u, z, buf_c = _mix_in(h1, mix_norm, win, _ag_comm(st_c, flat))
    dft = _dft_constants()
    cw = jnp.transpose(cw, (1, 0, 2)).reshape(40, D)
    a1, q, buf_d = _conv_fwd_dft(z, cw, a_dw_b, dft, _ag_comm(st_d, flat))
    h2, ya, yb, buf_e = _mix_out(a1, q, z, h1, a_ln_g, a_ln_b, buf_c, _ag_comm(st_e, flat))
    ffn2_bufs, ffn2_offs = (buf_c, buf_d, buf_e), (3 * D, 0, 0)
    dh3, dhb3, s_final, n2, gg2, uu2, act2 = _ffn_fwd(h2, ffn2_norm, ffn2_bufs, ffn2_offs, "ffn2_fwd",
                                          final=(final_norm.reshape(1, D), tgt))

    tr_f = F // 2 if (F // 2) % LANE == 0 else F
    def pair(stage, src):
        return _rs_pair_comm(stage, src)

    def chip(stage, src, pair_buf, tag):
        return _rs_chip_comm(_pair_add(stage, src, pair_buf, "pair_add_" + tag))

    (dgu2,) = _ffn_bwd_hidden(dhb3, gg2, uu2, buf_e, 0, "ffn2_bwd_h")
    (gu2,) = _tn_matmul(dgu2, n2, tr_f, "dw_gu2")
    s2a, src2a = ("wg2", "wu2"), dict(wg2=(gu2, 0), wu2=(gu2, F))
    gd2, pair2a = _tn_matmul(act2, dhb3, tr_f, "dw_d2", pair(s2a, src2a))
    s2b, src2b = ("wd2",), dict(wd2=(gd2, 0))
    dh2, s_ffn2, pair2b = _ffn_bwd_input(dgu2, dh3, h2, ffn2_norm, (buf_c, buf_d), (3 * D, 0), "ffn2_bwd_x",
                                         pair(s2b, src2b))
    dzg, da1, dq, lsq, rsq, s_mix, recv2b = _mix_out_bwd(dh2, ya, yb, z, a1, q, a_ln_g, a_ln_b, buf_c,
                                                          chip(s2b, src2b, pair2b, "2b"))
    (gsq,) = _tn_matmul(lsq, rsq, D, "dw_square")
    ssq, srcsq = ("wa", "wb", "wo"), dict(wa=(gsq, D), wb=(gsq, 2 * D), wo=(gsq, 0))
    dz, dwa, dwb, recv2a, pairsq = _conv_bwd_dft(z, da1, dq, dzg, cw, dft,
                                                 _join(chip(s2a, src2a, pair2a, "2a"), pair(ssq, srcsq)))
    gin, recvsq = _tn_matmul(dz, u, D, "dw_in", chip(ssq, srcsq, pairsq, "sq"))
    sin_a, sin_b, srcin = ("win/0/2",), ("win/1/2",), {"win/0/2": (gin, 0), "win/1/2": (gin, 0)}
    dh1, dhb1, s_in, pairin_a, pairin_b = _mix_in_bwd(dz, dh2, h1, mix_norm, win,
                                                _join(pair(sin_a, srcin), pair(sin_b, srcin)))
    dgu1, recvin_a = _ffn_bwd_hidden(dhb1, gg1, uu1, buf_b, 0, "ffn1_bwd_h",
                                           chip(sin_a, srcin, pairin_a, "in_a"))
    gu1, recvin_b = _tn_matmul(dgu1, n1, tr_f, "dw_gu1", chip(sin_b, srcin, pairin_b, "in_b"))
    s1a, src1a = ("wg1", "wu1"), dict(wg1=(gu1, 0), wu1=(gu1, F))
    gd1, pair1a = _tn_matmul(act1, dhb1, tr_f, "dw_d1", pair(s1a, src1a))
    s1b, src1b = ("wd1",), dict(wd1=(gd1, 0))
    dx, s_ffn1, recv1a, pair1b = _ffn_bwd_input(dgu1, dh1, x2, ffn1_norm, (buf_a, buf_a), (0, F), "ffn1_bwd_x",
                                                _join(chip(s1a, src1a, pair1a, "1a"), pair(s1b, src1b)))
    win_sum = jnp.concatenate([_sum_slots(recvin_a, "sum_in_a"), _sum_slots(recvin_b, "sum_in_b")], axis=0)

    vec8, convk = _pack_small(s_ffn1, s_in, s_mix, s_ffn2, s_final, dwa, dwb)
    recv1b, vec_all, conv_all = _run_comm(
        _join(chip(s1b, src1b, pair1b, "1b"), _join(_direct_comm(vec8, False), _direct_comm(convk, True))), "xchg_tail")
    vec_sum, conv_sum, loss_blk = _sum_small(vec_all, conv_all)
    loss = loss_blk[0, 0]

    fs = F // NDEV
    g = dict(ffn1_w_gate=(recv1a, 0), ffn1_w_up=(recv1a, fs), ffn2_w_gate=(recv2a, 0), ffn2_w_up=(recv2a, fs),
             ffn1_w_down=(recv1b, 0),
             ffn2_w_down=(recv2b, 0), a_w_out=(recvsq, 0), b_w_out=(recvsq, D // NDEV), w_o=(recvsq, 2 * (D // NDEV)),
             w_in=win_sum.T, ffn1_norm=vec_sum[0:1], mix_norm=vec_sum[1:2], a_ln_g=vec_sum[2:3], a_ln_b=vec_sum[3:4],
             a_dw_b=vec_sum[4:5], ffn2_norm=vec_sum[5:6], final_norm=vec_sum[6:7],
             a_dw_w=conv_sum[0:KA], b_conv_w=conv_sum[32:32 + KB])
    grad, upd = {}, {}

    def run(group, name, as2d=lambda a: a[0], back=lambda a, n: a.reshape(w[n].shape)):
        res, _ = _adam([g[n] for n in group], [as2d(w[n]) for n in group], [as2d(m[n]) for n in group],
                       [as2d(v[n]) for n in group], name)
        for n, r in zip(group, res):
            grad[n], upd[n] = back(r[0], n), tuple(back(a, n) for a in r[1:])

    run(("ffn1_w_gate", "ffn1_w_up", "ffn2_w_gate", "ffn2_w_up"), "adam_gate_up",
        as2d=lambda a: a[0].T, back=lambda a, n: a.T[None])
    run(("ffn1_w_down", "ffn2_w_down"), "adam_down")
    run(("w_in",), "adam_in")
    run(("a_w_out", "b_w_out", "w_o"), "adam_square")
    run(("a_dw_w",), "adam_dw")
    run(("b_conv_w",), "adam_conv")
    run(("ffn1_norm", "mix_norm", "a_dw_b", "a_ln_g", "a_ln_b", "ffn2_norm", "final_norm"), "adam_vec",
        as2d=lambda a: a.reshape(1, D))

    return (loss, dx.reshape(x.shape), *[grad[n] for n in names], *[upd[n][0] for n in names],
            *[upd[n][1] for n in names], *[upd[n][2] for n in names])
```

```python
import jax
import jax.numpy as jnp
from jax import lax
from jax.experimental import pallas as pl
from jax.experimental.pallas import tpu as pltpu

T = 4096
D = 1024
F = 2816
NG = 7
NDEV = 8
NCHIP = 4
KA, KB = 31, 3
EPS = 1e-6
ADAM_LR, ADAM_B1, ADAM_B2, ADAM_EPS, ADAM_WD, ADAM_STEP = 0.001, 0.9, 0.999, 1e-08, 0.01, 10

TM = 512
FC = 256
TB = 1024
NB = 256
HB = NB // 2
CW = 256
CHB = 64
LANE = 128
TK = 2048
VMEM_LIMIT = 56 * 1024 * 1024

BF = jnp.bfloat16
F32 = jnp.float32
MESH = pl.DeviceIdType.MESH
ANY = pl.BlockSpec(memory_space=pl.ANY)

ORDER = ("wg1", "wu1", "wd1", "wg2", "wu2", "wd2", "win", "wa", "wb", "wo")


class _Layout:
    def __init__(self):
        fs, dis, ds = F // NDEV, NG * D // NDEV, D // NDEV
        self.rows = dict(wg1=fs, wu1=fs, wd1=fs, wg2=fs, wu2=fs, wd2=fs, win=dis, wa=ds, wb=ds, wo=ds)
        self.fl, off = {}, 0
        for n in ORDER:
            self.fl[n] = off
            off += self.rows[n]
        self.RT = off


class _Stage:
    def __init__(self, names):
        lay = _Layout()
        self.names = names
        self.rows, self.full, self.sub, self.fl = {}, {}, {}, {}
        for n in names:
            base, i, k = (n.split("/") + ["0", "1"])[:3]
            self.full[n] = lay.rows[base]
            self.rows[n] = lay.rows[base] // int(k)
            self.sub[n] = int(i) * self.rows[n]
            self.fl[n] = lay.fl[base] + self.sub[n]
        self.off, self.wc, o, w = {}, {}, 0, 0
        for n in names:
            self.off[n], self.wc[n] = o, w
            o += self.rows[n]
            w += NDEV * self.rows[n]
        self.R, self.W = o, w

    def grad_row(self, n, first, dev_lin):
        return first + dev_lin * self.full[n] + self.sub[n]


def _nt(a, b):
    return lax.dot_general(a, b, (((1,), (1,)), ((), ())), preferred_element_type=F32)


def _nn(a, b):
    return lax.dot_general(a, b, (((1,), (0,)), ((), ())), preferred_element_type=F32)


def _tn(a, b):
    return lax.dot_general(a, b, (((0,), (0,)), ((), ())), preferred_element_type=F32)


def _sig(x):
    return 1.0 / (1.0 + jnp.exp(-x))


def _position():
    return lax.axis_index("x"), lax.axis_index("y"), lax.axis_index("c")


def _peer(pos, j):
    x, y, c = pos
    return (1 - x if j & 4 else x, 1 - y if j & 2 else y, 1 - c if j & 1 else c)


def _lin(pos):
    return 4 * pos[0] + 2 * pos[1] + pos[2]


def _chip(pos):
    return 2 * pos[0] + pos[1]


class _Comm:
    def __init__(self, inputs, out_shapes, scratch, start, finish, middle=None):
        self.inputs, self.out_shapes, self.scratch = inputs, out_shapes, scratch
        self.start, self.finish, self.middle = start, finish, middle


def _call(body, *, name, grid, args, in_specs, out_shape, out_specs, scratch_shapes=(), comm=None,
          num_scalar_prefetch=0):
    in_specs, out_shape, out_specs, scratch_shapes = list(in_specs), list(out_shape), list(out_specs), list(scratch_shapes)
    n_in, n_out, n_scr = len(in_specs), len(out_shape), len(scratch_shapes)
    sp = num_scalar_prefetch
    if comm is None:
        kernel_fn = lambda *refs: body(*refs)
        c_in = c_out = c_scr = 0
    else:
        c_in, c_out, c_scr = len(comm.inputs), len(comm.out_shapes), len(comm.scratch)

        def kernel_fn(*refs):
            pre, refs = refs[:sp], refs[sp:]
            ins, cins = refs[:n_in], refs[n_in:n_in + c_in]
            o0 = n_in + c_in
            outs, couts = refs[o0:o0 + n_out], refs[o0 + n_out:o0 + n_out + c_out]
            s0 = o0 + n_out + c_out
            scr, cscr = refs[s0:s0 + n_scr], refs[s0 + n_scr:]
            step, steps = pl.program_id(0), grid[0]
            for a in range(1, len(grid)):
                step, steps = step * grid[a] + pl.program_id(a), steps * grid[a]
            first, last = step == 0, step == steps - 1

            @pl.when(first)
            def _():
                comm.start(cins, couts, cscr)

            if comm.middle is not None:
                @pl.when(step == (steps // 2 if steps > 2 else steps - 1))
                def _():
                    comm.middle(cins, couts, cscr)

            body(*pre, *ins, *outs, *scr)

            @pl.when(last)
            def _():
                comm.finish(cins, couts, cscr)

        args = list(args) + list(comm.inputs)
        in_specs += [ANY] * c_in
        out_shape += list(comm.out_shapes)
        out_specs += [ANY] * c_out
        scratch_shapes += list(comm.scratch)
    params = pltpu.CompilerParams(dimension_semantics=("arbitrary",) * len(grid), vmem_limit_bytes=VMEM_LIMIT)
    if sp:
        grid_spec = pltpu.PrefetchScalarGridSpec(num_scalar_prefetch=sp, grid=grid, in_specs=in_specs,
                                                 out_specs=out_specs, scratch_shapes=scratch_shapes)
        return pl.pallas_call(kernel_fn, name=name, grid_spec=grid_spec, out_shape=out_shape,
                              compiler_params=params)(*args)
    return pl.pallas_call(kernel_fn, name=name, grid=grid, in_specs=in_specs, out_shape=out_shape, out_specs=out_specs,
                          scratch_shapes=scratch_shapes, compiler_params=params)(*args)


def _join(a, b):
    na = (len(a.inputs), len(a.out_shapes), len(a.scratch))

    def split(refs):
        return ([r[:n] for r, n in zip(refs, na)], [r[n:] for r, n in zip(refs, na)])

    def start(*refs):
        ra, rb = split(refs)
        a.start(*ra)
        b.start(*rb)

    def finish(*refs):
        ra, rb = split(refs)
        a.finish(*ra)
        b.finish(*rb)

    def middle(*refs):
        for stage, r in zip((a, b), split(refs)):
            if stage.middle is not None:
                stage.middle(*r)

    return _Comm(list(a.inputs) + list(b.inputs), list(a.out_shapes) + list(b.out_shapes),
                 list(a.scratch) + list(b.scratch), start, finish,
                 middle if (a.middle is not None or b.middle is not None) else None)


def _run_comm(comm, name):
    def body(*refs):
        c_in, c_out = len(comm.inputs), len(comm.out_shapes)
        parts = (refs[:c_in], refs[c_in:c_in + c_out], refs[c_in + c_out:])
        comm.start(*parts)
        if comm.middle is not None:
            comm.middle(*parts)
        comm.finish(*parts)

    return pl.pallas_call(
        body, name=name, out_shape=list(comm.out_shapes), in_specs=[ANY] * len(comm.inputs),
        out_specs=[ANY] * len(comm.out_shapes), scratch_shapes=list(comm.scratch))(*comm.inputs)


def _ag_comm(names, flat):
    st = _Stage(names)

    def ring(me):
        x, y, c = me
        diagonal = x == y
        up = (jnp.where(diagonal, x, 1 - x), jnp.where(diagonal, 1 - y, y), c)
        down = (jnp.where(diagonal, 1 - x, x), jnp.where(diagonal, y, 1 - y), c)
        low = c == 0
        passed = tuple(jnp.where(low, d, u) for d, u in zip(down, up))
        target = tuple(jnp.where(low, u, d) for d, u in zip(down, up))
        return up, down, (1 - x, 1 - y, c), passed, target

    def parts(refs):
        (flat_ref,), (out_ref,), (send_sems, recv_sems, local_sem) = refs
        me = _position()

        def region(name, dev):
            r = st.rows[name]
            return out_ref.at[pl.ds(st.wc[name] + _lin(dev) * r, r), :]

        def own(name):
            return flat_ref.at[pl.ds(st.fl[name], st.rows[name]), :]

        def copies(k, dev, to, from_flat):
            return [pltpu.make_async_remote_copy(
                src_ref=own(n) if from_flat else region(n, dev), dst_ref=region(n, dev), send_sem=send_sems.at[k],
                recv_sem=recv_sems.at[k], device_id=to, device_id_type=MESH) for n in names]

        def whole(k):
            return pltpu.make_async_remote_copy(
                src_ref=flat_ref.at[pl.ds(0, st.R), :], dst_ref=out_ref.at[pl.ds(0, st.R), :],
                send_sem=send_sems.at[k], recv_sem=recv_sems.at[k], device_id=me, device_id_type=MESH)

        return me, region, own, copies, whole, flat_ref, out_ref, local_sem

    def start(*refs):
        me, region, own, copies, _, _, _, local_sem = parts(refs)
        for n in names:
            pltpu.make_async_copy(own(n), region(n, me), local_sem).start()
        up, down, _, _, _ = ring(me)
        for k, to in ((1, up), (2, down), (0, _peer(me, 1))):
            for cp in copies(k, me, to, True):
                cp.start()

    def middle(*refs):
        me, _, _, copies, whole, _, _, _ = parts(refs)
        up, down, _, passed, target = ring(me)
        sib = _peer(me, 1)
        whole(1).wait_recv()
        whole(2).wait_recv()
        for k, dev, to in ((3, passed, target), (4, down, sib), (5, up, sib)):
            for cp in copies(k, dev, to, False):
                cp.start()

    def finish(*refs):
        me, _, _, copies, whole, flat_ref, out_ref, local_sem = parts(refs)
        _, _, across, _, _ = ring(me)
        whole(3).wait_recv()
        for cp in copies(6, across, _peer(me, 1), False):
            cp.start()
        whole(0).wait_recv()
        for j in range(3):
            whole(4 + j).wait_recv()
        for k in range(7):
            whole(k).wait_send()
        pltpu.make_async_copy(flat_ref.at[pl.ds(0, st.R), :], out_ref.at[pl.ds(0, st.R), :], local_sem).wait()

    return _Comm([flat], [jax.ShapeDtypeStruct((st.W, D), BF)],
                 [pltpu.SemaphoreType.DMA((7,)), pltpu.SemaphoreType.DMA((7,)), pltpu.SemaphoreType.DMA],
                 start, finish, middle)


def _rs_pair_comm(names, src):
    st = _Stage(names)
    arrays = []
    for n in names:
        if not any(src[n][0] is a for a in arrays):
            arrays.append(src[n][0])
    idx = {n: [i for i, a in enumerate(arrays) if a is src[n][0]][0] for n in names}

    def slot_wait(refs):
        recv = refs[1][0]
        send_sem, recv_sem = refs[2]
        return pltpu.make_async_remote_copy(src_ref=recv, dst_ref=recv, send_sem=send_sem, recv_sem=recv_sem,
                                            device_id=_position(), device_id_type=MESH)

    def start(*refs):
        ins, (recv,), (send_sem, recv_sem) = refs
        me = _position()
        sib = _peer(me, 1)
        for q in range(NCHIP):
            dev = (q // 2, q % 2, sib[2])
            for n in names:
                r = st.rows[n]
                pltpu.make_async_remote_copy(
                    src_ref=ins[idx[n]].at[pl.ds(st.grad_row(n, src[n][1], _lin(dev)), r), :],
                    dst_ref=recv.at[q, pl.ds(st.off[n], r), :], send_sem=send_sem, recv_sem=recv_sem,
                    device_id=sib, device_id_type=MESH).start()

    def finish(*refs):
        w = slot_wait(refs)
        w.wait_recv()
        w.wait_send()

    return _Comm(arrays, [jax.ShapeDtypeStruct((NCHIP, st.R, D), BF)],
                 [pltpu.SemaphoreType.DMA, pltpu.SemaphoreType.DMA], start, finish)


def _pair_add(names, src, recv, name):
    st = _Stage(names)
    c_arr = jnp.reshape(lax.axis_index("c"), (1,)).astype(jnp.int32)

    def body(c_ref, *refs):
        r_ref, o_ref = refs[len(names)], refs[len(names) + 1]
        for a_ref, n in zip(refs, names):
            rows = slice(st.off[n], st.off[n] + st.rows[n])
            o_ref[rows, :] = (a_ref[...].astype(F32) + r_ref[rows, :].astype(F32)).astype(BF)

    def shard_spec(n):
        r = st.rows[n]
        base, step = st.grad_row(n, src[n][1], 0) // r, st.full[n] // r
        return pl.BlockSpec((r, D), lambda q, c_ref: (base + step * (2 * q + c_ref[0]), 0))

    slot = pl.BlockSpec((None, st.R, D), lambda q, c_ref: (q, 0, 0))
    return _call(body, name=name, grid=(NCHIP,), args=[c_arr] + [src[n][0] for n in names] + [recv],
                 in_specs=[shard_spec(n) for n in names] + [slot],
                 out_shape=[jax.ShapeDtypeStruct((NCHIP, st.R, D), BF)], out_specs=[slot], num_scalar_prefetch=1)[0]


def _rs_chip_comm(part):
    def copies(refs):
        (p_ref,), (recv,), (send_sems, recv_sems, local_sem) = refs
        me = _position()
        mine = pltpu.make_async_copy(p_ref.at[_chip(me)], recv.at[_chip(me)], local_sem)
        out = []
        for j, bits in enumerate((4, 2, 6)):
            to = _peer(me, bits)
            out.append(pltpu.make_async_remote_copy(
                src_ref=p_ref.at[_chip(to)], dst_ref=recv.at[_chip(me)], send_sem=send_sems.at[j],
                recv_sem=recv_sems.at[j], device_id=to, device_id_type=MESH))
        return mine, out

    def start(*refs):
        mine, out = copies(refs)
        mine.start()
        for cp in out:
            cp.start()

    def finish(*refs):
        mine, out = copies(refs)
        for cp in out:
            cp.wait_recv()
        for cp in out:
            cp.wait_send()
        mine.wait()

    return _Comm([part], [jax.ShapeDtypeStruct(part.shape, BF)],
                 [pltpu.SemaphoreType.DMA((3,)), pltpu.SemaphoreType.DMA((3,)), pltpu.SemaphoreType.DMA],
                 start, finish)


def _direct_comm(x, scatter):
    def copies(refs):
        (x_ref,), (out_ref,), (send_sems, recv_sems, local_sem) = refs
        me = _position()

        def piece(dev):
            return x_ref.at[_lin(dev)] if scatter else x_ref

        mine = pltpu.make_async_copy(piece(me), out_ref.at[_lin(me)], local_sem)
        return mine, [pltpu.make_async_remote_copy(
            src_ref=piece(_peer(me, j)), dst_ref=out_ref.at[_lin(me)], send_sem=send_sems.at[j - 1],
            recv_sem=recv_sems.at[j - 1], device_id=_peer(me, j), device_id_type=MESH) for j in range(1, NDEV)]

    def start(*refs):
        mine, cps = copies(refs)
        mine.start()
        for cp in cps:
            cp.start()

    def finish(*refs):
        mine, cps = copies(refs)
        for cp in cps:
            cp.wait_recv()
        for cp in cps:
            cp.wait_send()
        mine.wait()

    shape = x.shape if scatter else (NDEV,) + x.shape
    return _Comm([x], [jax.ShapeDtypeStruct(shape, x.dtype)],
                 [pltpu.SemaphoreType.DMA((7,)), pltpu.SemaphoreType.DMA((7,)), pltpu.SemaphoreType.DMA],
                 start, finish)


def _pack_weights(shards):
    lay = _Layout()

    def body(*refs):
        o_ref = refs[-1]
        for ref, n in zip(refs, ORDER):
            x = ref[...].T if n == "win" else ref[...]
            o_ref[lay.fl[n]:lay.fl[n] + lay.rows[n], :] = x.astype(BF)

    return pl.pallas_call(
        body, name="pack_weights", out_shape=jax.ShapeDtypeStruct((lay.RT, D), BF),
        compiler_params=pltpu.CompilerParams(vmem_limit_bytes=VMEM_LIMIT))(*[shards[n] for n in ORDER])


def _load_ffn_weights(srcs, offs, scratch, sem):
    @pl.when(pl.program_id(0) == 0)
    def _():
        cps = [pltpu.make_async_copy(s.at[pl.ds(off, dst.shape[0]), :], dst, sem.at[i])
               for i, (s, off, dst) in enumerate(zip(srcs, offs, scratch))]
        for cp in cps:
            cp.start()
        for cp in cps:
            cp.wait()


def _final_loss_tile(xf, g, tgt, s_ref):
    r = lax.rsqrt(jnp.mean(xf * xf, axis=-1, keepdims=True) + EPS)
    xr = xf * r
    e = xr * g - tgt
    s_ref[1:2, :] += jnp.sum(e * e, axis=0, keepdims=True) * (0.5 / D)
    dy = e * (1.0 / D)
    s_ref[0:1, :] += jnp.sum(dy * xr, axis=0, keepdims=True)
    gdy = dy * g
    return r * gdy - xr * (r * jnp.mean(gdy * xr, axis=-1, keepdims=True))


def _ffn_fwd(x, g, wbufs, offs, name, comm=None, final=None):
    nf = F // FC

    def body(x_ref, g_ref, b0, b1, b2, *rest):
        if final is None:
            h_ref, n_ref, gg_ref, uu_ref, a_ref, wg_s, wu_s, wd_s, sem = rest
        else:
            gf_ref, t_ref, dh_ref, dhb_ref, s_ref, n_ref, gg_ref, uu_ref, a_ref, wg_s, wu_s, wd_s, sem = rest

            @pl.when(pl.program_id(0) == 0)
            def _():
                s_ref[...] = jnp.zeros_like(s_ref)

        _load_ffn_weights((b0, b1, b2), offs, (wg_s, wu_s, wd_s), sem)
        xf = x_ref[...]
        r = lax.rsqrt(jnp.mean(xf * xf, axis=-1, keepdims=True) + EPS)
        nb = (xf * r * g_ref[...]).astype(BF)
        n_ref[...] = nb
        acc = jnp.zeros((TM, D), F32)
        for c in range(nf):
            sl = slice(c * FC, (c + 1) * FC)
            gb = _nt(nb, wg_s[sl, :]).astype(BF)
            ub = _nt(nb, wu_s[sl, :]).astype(BF)
            gg_ref[:, sl] = gb
            uu_ref[:, sl] = ub
            a = (gb * _sig(gb)) * ub
            a_ref[0, :, sl] = a
            acc = acc + _nn(a, wd_s[sl, :])
        h = xf + 0.5 * acc
        if final is None:
            h_ref[...] = h
        else:
            dh = _final_loss_tile(h, gf_ref[...], t_ref[...], s_ref)
            dh_ref[...] = dh
            dhb_ref[...] = (0.5 * dh).astype(BF)

    row = lambda i: (i, 0)
    vec = pl.BlockSpec((1, D), lambda i: (0, 0))
    tile = pl.BlockSpec((TM, D), row)
    saved_shapes = [jax.ShapeDtypeStruct((T, D), BF), jax.ShapeDtypeStruct((T, F), BF), jax.ShapeDtypeStruct((T, F), BF),
                    jax.ShapeDtypeStruct((1, T, F), BF)]
    saved_specs = [tile, pl.BlockSpec((TM, F), row), pl.BlockSpec((TM, F), row),
                   pl.BlockSpec((1, TM, F), lambda i: (0, i, 0))]
    if final is None:
        extra_args, extra_specs = [], []
        head_shapes, head_specs = [jax.ShapeDtypeStruct((T, D), F32)], [tile]
    else:
        extra_args, extra_specs = list(final), [vec, tile]
        head_shapes = [jax.ShapeDtypeStruct((T, D), F32), jax.ShapeDtypeStruct((T, D), BF), jax.ShapeDtypeStruct((8, D), F32)]
        head_specs = [tile, tile, pl.BlockSpec((8, D), lambda i: (0, 0))]
    return _call(
        body, name=name, grid=(T // TM,), args=[x, g, *wbufs, *extra_args], comm=comm,
        in_specs=[tile, vec, ANY, ANY, ANY] + extra_specs,
        out_shape=head_shapes + saved_shapes, out_specs=head_specs + saved_specs,
        scratch_shapes=[pltpu.VMEM((F, D), BF)] * 3 + [pltpu.SemaphoreType.DMA((3,))])


def _ffn_gate_up(x, g, wbufs, offs, name, comm=None):
    nf = F // FC

    def body(x_ref, g_ref, b0, b1, n_ref, gg_ref, uu_ref, a_ref, wg_s, wu_s, sem):
        _load_ffn_weights((b0, b1), offs, (wg_s, wu_s), sem)
        xf = x_ref[...]
        r = lax.rsqrt(jnp.mean(xf * xf, axis=-1, keepdims=True) + EPS)
        nb = (xf * r * g_ref[...]).astype(BF)
        n_ref[...] = nb
        for c in range(nf):
            sl = slice(c * FC, (c + 1) * FC)
            gb = _nt(nb, wg_s[sl, :]).astype(BF)
            ub = _nt(nb, wu_s[sl, :]).astype(BF)
            gg_ref[:, sl] = gb
            uu_ref[:, sl] = ub
            a_ref[0, :, sl] = (gb * _sig(gb)) * ub

    row = lambda i: (i, 0)
    tile = pl.BlockSpec((TM, D), row)
    return _call(
        body, name=name, grid=(T // TM,), args=[x, g, *wbufs], comm=comm,
        in_specs=[tile, pl.BlockSpec((1, D), lambda i: (0, 0)), ANY, ANY],
        out_shape=[jax.ShapeDtypeStruct((T, D), BF), jax.ShapeDtypeStruct((T, F), BF), jax.ShapeDtypeStruct((T, F), BF),
                   jax.ShapeDtypeStruct((1, T, F), BF)],
        out_specs=[tile, pl.BlockSpec((TM, F), row), pl.BlockSpec((TM, F), row),
                   pl.BlockSpec((1, TM, F), lambda i: (0, i, 0))],
        scratch_shapes=[pltpu.VMEM((F, D), BF)] * 2 + [pltpu.SemaphoreType.DMA((2,))])


def _ffn_down(x, act, wbuf, off, name, comm=None):
    def body(x_ref, a_ref, b0, h_ref, wd_s, sem):
        _load_ffn_weights((b0,), (off,), (wd_s,), sem)
        h_ref[...] = x_ref[...] + 0.5 * _nn(a_ref[0], wd_s[...])

    tile = pl.BlockSpec((TM, D), lambda i: (i, 0))
    return _call(
        body, name=name, grid=(T // TM,), args=[x, act, wbuf], comm=comm,
        in_specs=[tile, pl.BlockSpec((1, TM, F), lambda i: (0, i, 0)), ANY],
        out_shape=[jax.ShapeDtypeStruct((T, D), F32)], out_specs=[tile],
        scratch_shapes=[pltpu.VMEM((F, D), BF), pltpu.SemaphoreType.DMA((1,))])


def _load_in_proj(parts, w_s, sem):
    @pl.when(pl.program_id(0) == 0)
    def _():
        shard = NG * D // NDEV
        rows = shard // len(parts)
        cps = [pltpu.make_async_copy(buf.at[pl.ds(first + k * rows, rows), :],
                                     w_s.at[pl.ds(k * shard + p * rows, rows), :], sem.at[p * NDEV + k])
               for p, (buf, first) in enumerate(parts) for k in range(NDEV)]
        for cp in cps:
            cp.start()
        for cp in cps:
            cp.wait()


def _mix_in(h1, gm, win, comm=None):
    def body(h_ref, g_ref, *rest):
        w_any, (u_ref, z_ref, w_s, sem) = rest[:len(win)], rest[len(win):]
        _load_in_proj([(b, first) for b, (_, first) in zip(w_any, win)], w_s, sem)
        xf = h_ref[...]
        r = lax.rsqrt(jnp.mean(xf * xf, axis=-1, keepdims=True) + EPS)
        ub = (xf * r * g_ref[...]).astype(BF)
        u_ref[...] = ub
        for j in range(NG):
            z_ref[j] = _nt(ub, w_s[j * D:(j + 1) * D, :]).astype(BF)

    row = lambda i: (i, 0)
    return _call(
        body, name="mix_in", grid=(T // TM,), args=[h1, gm] + [b for b, _ in win], comm=comm,
        in_specs=[pl.BlockSpec((TM, D), row), pl.BlockSpec((1, D), lambda i: (0, 0))] + [ANY] * len(win),
        out_shape=[jax.ShapeDtypeStruct((T, D), BF), jax.ShapeDtypeStruct((NG, T, D), BF)],
        out_specs=[pl.BlockSpec((TM, D), row), pl.BlockSpec((NG, TM, D), lambda i: (0, i, 0))],
        scratch_shapes=[pltpu.VMEM((NG * D, D), BF), pltpu.SemaphoreType.DMA((NDEV * len(win),))])


def _shift_up(w, b):
    return w if b == 0 else pltpu.roll(w, w.shape[0] - b, 0)


def _fold8(p):
    red = p[0:8, :]
    for i in range(1, p.shape[0] // 8):
        red = red + p[8 * i:8 * i + 8, :]
    return red


def _dft_constants():
    import numpy as np
    nh = NB // 2
    f, n = np.arange(nh)[:, None], np.arange(NB)[None, :]
    ang = 2.0 * np.pi / NB * f * n
    fc = np.cos(ang)
    fs = np.where(f == 0, (-1.0) ** n, np.sin(ang))
    scale = np.where(f == 0, 1.0, 2.0) / NB
    ic = (scale * np.cos(ang)).T
    isn = np.where(f == 0, (-1.0) ** n / NB, scale * np.sin(ang)).T
    d = (KA - 1 - np.arange(32))[None, :]
    valid = (np.arange(32) < KA)[None, :]
    angk = 2.0 * np.pi / NB * f * d
    kc = np.where(valid, np.cos(angk), 0.0)
    ks = np.where(valid, np.sin(angk), 0.0)
    k2 = np.where(valid, np.where(f == 0, (-1.0) ** d, np.cos(angk)), 0.0)
    rtc = np.where(valid, scale * np.cos(angk), 0.0).T
    rts = np.where(valid, np.where(f == 0, (-1.0) ** d / NB, scale * np.sin(angk)), 0.0).T

    def bf(a):
        return jnp.asarray(a, F32).astype(BF)

    def split(a):
        hi = bf(a)
        return hi, (jnp.asarray(a, F32) - hi.astype(F32)).astype(BF)

    return dict(fc=bf(fc), fs=bf(fs), ic_hi=bf(ic[HB:]), is_hi=bf(isn[HB:]), ic_lo=bf(ic[:HB]), is_lo=bf(isn[:HB]),
                kc=split(kc), ks=split(ks), k2=split(k2), rtc=split(rtc), rts=split(rts))


def _dot3(m_hi, m_lo, x):
    x_hi = x.astype(BF)
    x_lo = (x - x_hi.astype(F32)).astype(BF)
    return _nn(m_hi, x_hi) + _nn(m_hi, x_lo) + _nn(m_lo, x_hi)


def _whole(a):
    return pl.BlockSpec(a.shape, lambda c, t: (0,) * a.ndim)


def _filter_spectrum(cw_ref, tabs, hc, hs, h2):
    w32 = cw_ref[0:32, :]
    for (hi, lo), dst in zip(tabs, (hc, hs, h2)):
        dst[...] = _dot3(hi[...], lo[...], w32)


def _conv_fwd_dft(z, cw, bias, dft, comm=None):
    nt = T // TB
    hb = TB // HB

    def body(z_ref, zh_ref, cw_ref, b_ref, fc_ref, fs_ref, ic_ref, is_ref, kch, kcl, ksh, ksl, k2h, k2l,
             a1_ref, q_ref, aext, ppad, hc, hs, h2):
        first = pl.program_id(1) == 0
        f = lambda ref, j: ref[j].astype(F32)

        @pl.when(first)
        def _():
            _filter_spectrum(cw_ref, ((kch, kcl), (ksh, ksl), (k2h, k2l)), hc, hs, h2)

        aext[0:HB, :] = jnp.where(first, 0.0, f(zh_ref, 0) * _sig(f(zh_ref, 1))).astype(BF)
        aext[HB:, :] = (f(z_ref, 0) * _sig(f(z_ref, 1))).astype(BF)
        ppad[0:8, :] = jnp.where(first, 0.0, f(zh_ref, 3)[HB - 8:HB, :] * f(zh_ref, 4)[HB - 8:HB, :])
        ppad[8:, :] = f(z_ref, 3) * f(z_ref, 4)
        bias_row = b_ref[...]

        for j in range(TB // HB):
            xs = aext[j * HB:j * HB + NB, :]
            xa, xb = _nn(fc_ref[...], xs), _nn(fs_ref[...], xs)
            yc = (hc[...] * xa - hs[...] * xb).astype(BF)
            ys = (h2[...] * xb + hs[...] * xa).astype(BF)
            y = _nn(ic_ref[...], yc) + _nn(is_ref[...], ys)
            a1_ref[j * HB:(j + 1) * HB, :] = (y + bias_row).astype(BF)

        def chunk(r, carry):
            base = pl.multiple_of(r * CHB, CHB)
            pw = ppad[pl.ds(base, CHB + 8), :]
            v = (cw_ref[pl.ds(32, 1), :] * _shift_up(pw, 6)[0:CHB, :]
                 + cw_ref[pl.ds(33, 1), :] * _shift_up(pw, 7)[0:CHB, :]
                 + cw_ref[pl.ds(34, 1), :] * pw[8:8 + CHB, :])
            q_ref[pl.ds(base, CHB), :] = (z_ref[2, pl.ds(base, CHB), :].astype(F32) * v).astype(BF)
            return carry

        lax.fori_loop(0, TB // CHB, chunk, 0)

    blk = pl.BlockSpec((TB, CW), lambda c, t: (t, c))
    tabs = [dft["fc"], dft["fs"], dft["ic_hi"], dft["is_hi"], *dft["kc"], *dft["ks"], *dft["k2"]]
    return _call(
        body, name="conv_fwd", grid=(D // CW, nt), comm=comm, args=[z, z, cw, bias] + tabs,
        in_specs=[pl.BlockSpec((5, TB, CW), lambda c, t: (0, t, c)),
                  pl.BlockSpec((5, HB, CW), lambda c, t: (0, jnp.maximum(t * hb - 1, 0), c)),
                  pl.BlockSpec((40, CW), lambda c, t: (0, c)), pl.BlockSpec((1, CW), lambda c, t: (0, c))]
                 + [_whole(a) for a in tabs],
        out_shape=[jax.ShapeDtypeStruct((T, D), BF), jax.ShapeDtypeStruct((T, D), BF)], out_specs=[blk, blk],
        scratch_shapes=[pltpu.VMEM((TB + HB, CW), BF), pltpu.VMEM((TB + 8, CW), F32)]
                       + [pltpu.VMEM((NB // 2, CW), F32)] * 3)


def _conv_bwd_dft(z, da1, dq, dzg, cw, dft, comm=None):
    nt = T // TB
    hb = TB // HB
    last_h = T // HB - 1

    def body(z_ref, zp_ref, zn_ref, da1_ref, da1n_ref, dq_ref, dqn_ref, dzg_ref, cw_ref,
             fc_ref, fs_ref, ic_ref, is_ref, kch, kcl, ksh, ksl, k2h, k2l, rch, rcl, rsh, rsl,
             dz_ref, dwa_ref, dwb_ref, aext, dyext, ppad, dvpad, hc, hs, h2, rc, rs, nyq, acc_b):
        t = pl.program_id(1)
        first, last = t == 0, t == nt - 1
        f = lambda ref, j: ref[j].astype(F32)

        @pl.when(first)
        def _():
            _filter_spectrum(cw_ref, ((kch, kcl), (ksh, ksl), (k2h, k2l)), hc, hs, h2)
            rc[...] = jnp.zeros_like(rc)
            rs[...] = jnp.zeros_like(rs)
            nyq[...] = jnp.zeros_like(nyq)
            acc_b[...] = jnp.zeros_like(acc_b)

        aext[0:HB, :] = jnp.where(first, 0.0, f(zp_ref, 0) * _sig(f(zp_ref, 1))).astype(BF)
        aext[HB:, :] = (f(z_ref, 0) * _sig(f(z_ref, 1))).astype(BF)
        dyext[0:TB, :] = da1_ref[...]
        dyext[TB:, :] = jnp.where(last, 0.0, da1n_ref[...].astype(F32)).astype(BF)
        ppad[0:8, :] = jnp.where(first, 0.0, f(zp_ref, 3)[HB - 8:HB, :] * f(zp_ref, 4)[HB - 8:HB, :])
        ppad[8:, :] = f(z_ref, 3) * f(z_ref, 4)
        dvpad[0:TB, :] = dq_ref[...].astype(F32) * f(z_ref, 2)
        dvpad[TB:, :] = jnp.where(last, 0.0, dqn_ref[...].astype(F32)[0:8, :] * f(zn_ref, 2)[0:8, :])

        for j in range(TB // HB):
            rows = slice(j * HB, (j + 1) * HB)
            dys = dyext[j * HB:j * HB + NB, :]
            da, db = _nn(fc_ref[...], dys), _nn(fs_ref[...], dys)
            gc = (hc[...] * da + hs[...] * db).astype(BF)
            gs = (h2[...] * db - hs[...] * da).astype(BF)
            da0 = _nn(ic_ref[...], gc) + _nn(is_ref[...], gs)
            z0, z1 = z_ref[0, rows, :].astype(F32), z_ref[1, rows, :].astype(F32)
            s1 = _sig(z1)
            dz_ref[0, rows, :] = (da0 * s1).astype(BF)
            dz_ref[1, rows, :] = (da0 * z0 * (s1 * (1.0 - s1))).astype(BF)
            xs = aext[j * HB:j * HB + NB, :]
            xa, xb = _nn(fc_ref[...], xs), _nn(fs_ref[...], xs)
            dyb = dyext[rows, :]
            pa, pb = _nn(fc_ref[:, HB:NB], dyb), _nn(fs_ref[:, HB:NB], dyb)
            rc[...] += pa * xa + pb * xb
            rs[...] += pb * xa - pa * xb
            nyq[...] += pb[0:8, :] * xb[0:8, :]

        def chunk(r, carry):
            base = pl.multiple_of(r * CHB, CHB)
            rows = pl.ds(base, CHB)
            pw = ppad[pl.ds(base, CHB + 8), :]
            p6 = _shift_up(pw, 6)[0:CHB, :]
            p7 = _shift_up(pw, 7)[0:CHB, :]
            p8 = pw[8:8 + CHB, :]
            wb0, wb1, wb2 = cw_ref[pl.ds(32, 1), :], cw_ref[pl.ds(33, 1), :], cw_ref[pl.ds(34, 1), :]
            v = wb0 * p6 + wb1 * p7 + wb2 * p8
            dz_ref[2, rows, :] = (dq_ref[rows, :].astype(F32) * v).astype(BF)
            dvw = dvpad[pl.ds(base, CHB + 8), :]
            dvc = dvw[0:CHB, :]
            dp = wb2 * dvc + wb1 * _shift_up(dvw, 1)[0:CHB, :] + wb0 * _shift_up(dvw, 2)[0:CHB, :]
            dz_ref[3, rows, :] = (dp * z_ref[4, rows, :].astype(F32)).astype(BF)
            dz_ref[4, rows, :] = (dp * z_ref[3, rows, :].astype(F32)).astype(BF)
            acc_b[0:8, :] += _fold8(dvc * p6)
            acc_b[8:16, :] += _fold8(dvc * p7)
            acc_b[16:24, :] += _fold8(dvc * p8)
            dz_ref[5, rows, :] = dzg_ref[0, rows, :]
            dz_ref[6, rows, :] = dzg_ref[1, rows, :]
            return carry

        lax.fori_loop(0, TB // CHB, chunk, 0)

        @pl.when(last)
        def _():
            row0 = lax.broadcasted_iota(jnp.int32, (NB // 2, CW), 0) == 0
            ny = jnp.broadcast_to(nyq[0:1, :], (NB // 2, CW))
            rcv = jnp.where(row0, rc[...] - ny, rc[...])
            rsv = jnp.where(row0, ny, rs[...])
            dwa_ref[...] = _dot3(rch[...], rcl[...], rcv) + _dot3(rsh[...], rsl[...], rsv)
            for k in range(KB):
                dwb_ref[k:k + 1, :] = jnp.sum(acc_b[8 * k:8 * k + 8, :], axis=0, keepdims=True)
            dwb_ref[KB:8, :] = jnp.zeros((8 - KB, CW), F32)

    blk = lambda c, t: (t, c)
    nxt = lambda c, t: (jnp.minimum((t + 1) * hb, last_h), c)
    tabs = [dft["fc"], dft["fs"], dft["ic_lo"], dft["is_lo"], *dft["kc"], *dft["ks"], *dft["k2"], *dft["rtc"], *dft["rts"]]
    return _call(
        body, name="conv_bwd", grid=(D // CW, nt), comm=comm, args=[z, z, z, da1, da1, dq, dq, dzg, cw] + tabs,
        in_specs=[pl.BlockSpec((5, TB, CW), lambda c, t: (0, t, c)),
                  pl.BlockSpec((5, HB, CW), lambda c, t: (0, jnp.maximum(t * hb - 1, 0), c)),
                  pl.BlockSpec((5, HB, CW), lambda c, t: (0, jnp.minimum((t + 1) * hb, last_h), c)),
                  pl.BlockSpec((TB, CW), blk), pl.BlockSpec((HB, CW), nxt),
                  pl.BlockSpec((TB, CW), blk), pl.BlockSpec((HB, CW), nxt),
                  pl.BlockSpec((2, TB, CW), lambda c, t: (0, t, c)),
                  pl.BlockSpec((40, CW), lambda c, t: (0, c))]
                 + [_whole(a) for a in tabs],
        out_shape=[jax.ShapeDtypeStruct((NG, T, D), BF), jax.ShapeDtypeStruct((32, D), F32),
                   jax.ShapeDtypeStruct((8, D), F32)],
        out_specs=[pl.BlockSpec((NG, TB, CW), lambda c, t: (0, t, c)),
                   pl.BlockSpec((32, CW), lambda c, t: (0, c)), pl.BlockSpec((8, CW), lambda c, t: (0, c))],
        scratch_shapes=[pltpu.VMEM((TB + HB, CW), BF), pltpu.VMEM((TB + HB, CW), BF),
                        pltpu.VMEM((TB + 8, CW), F32), pltpu.VMEM((TB + 8, CW), F32)]
                       + [pltpu.VMEM((NB // 2, CW), F32)] * 5 + [pltpu.VMEM((8, CW), F32), pltpu.VMEM((24, CW), F32)])


def _layernorm_silu(a1, lng, lnb):
    mu = jnp.mean(a1, axis=-1, keepdims=True)
    xc = a1 - mu
    rs = lax.rsqrt(jnp.mean(xc * xc, axis=-1, keepdims=True) + EPS)
    xh = xc * rs
    a2 = xh * lng + lnb
    sg = _sig(a2)
    return xh, rs, a2, sg


def _square_specs(blocks):
    return [pl.BlockSpec((D, D), lambda i, b=b: (b, 0)) for b in blocks]


def _mix_out(a1, q, z, h1, lng, lnb, wsq, comm=None):
    def body(a1_ref, q_ref, ga_ref, gb_ref, h_ref, lng_ref, lnb_ref, wa_ref, wb_ref, wo_ref, h2_ref, ya_ref, yb_ref):
        _, _, a2, sg = _layernorm_silu(a1_ref[...].astype(F32), lng_ref[...], lnb_ref[...])
        ya = _nn((a2 * sg).astype(BF), wa_ref[...])
        yb = _nn(q_ref[...], wb_ref[...])
        ya_ref[...] = ya.astype(BF)
        yb_ref[...] = yb.astype(BF)
        m = _sig(ga_ref[...].astype(F32)) * ya + _sig(gb_ref[...].astype(F32)) * yb
        h2_ref[...] = h_ref[...] + _nn(m.astype(BF), wo_ref[...])

    row = lambda i: (i, 0)
    vec = pl.BlockSpec((1, D), lambda i: (0, 0))
    return _call(
        body, name="mix_out", grid=(T // TM,), args=[a1, q, z, z, h1, lng, lnb, wsq, wsq, wsq], comm=comm,
        in_specs=[pl.BlockSpec((TM, D), row), pl.BlockSpec((TM, D), row),
                  pl.BlockSpec((None, TM, D), lambda i: (5, i, 0)), pl.BlockSpec((None, TM, D), lambda i: (6, i, 0)),
                  pl.BlockSpec((TM, D), row), vec, vec] + _square_specs((0, 1, 2)),
        out_shape=[jax.ShapeDtypeStruct((T, D), F32), jax.ShapeDtypeStruct((T, D), BF), jax.ShapeDtypeStruct((T, D), BF)],
        out_specs=[pl.BlockSpec((TM, D), row)] * 3)


def _rmsnorm_bwd(xf, g, dn):
    r = lax.rsqrt(jnp.mean(xf * xf, axis=-1, keepdims=True) + EPS)
    xr = xf * r
    gdn = dn * g
    dx = r * gdn - xr * (r * jnp.mean(gdn * xr, axis=-1, keepdims=True))
    return dx, jnp.sum(dn * xr, axis=0, keepdims=True)


def _ffn_bwd_hidden(dh, gg, uu, wbuf, off, name, comm=None):
    nf = F // FC

    def body(dh_ref, gg_ref, uu_ref, b0, dgu_ref, wd_s, sem):
        _load_ffn_weights((b0,), (off,), (wd_s,), sem)
        dhb = dh_ref[...]
        for c in range(nf):
            sl = slice(c * FC, (c + 1) * FC)
            da = _nt(dhb, wd_s[sl, :]).astype(BF)
            gb, ub = gg_ref[:, sl], uu_ref[:, sl]
            sg = _sig(gb)
            dgu_ref[0, :, sl] = (da * ub) * (sg * (1.0 + gb * (1.0 - sg)))
            dgu_ref[0, :, F + c * FC:F + (c + 1) * FC] = da * (gb * sg)

    row = lambda i: (i, 0)
    return _call(
        body, name=name, grid=(T // TM,), args=[dh, gg, uu, wbuf], comm=comm,
        in_specs=[pl.BlockSpec((TM, D), row), pl.BlockSpec((TM, F), row), pl.BlockSpec((TM, F), row), ANY],
        out_shape=[jax.ShapeDtypeStruct((1, T, 2 * F), BF)],
        out_specs=[pl.BlockSpec((1, TM, 2 * F), lambda i: (0, i, 0))],
        scratch_shapes=[pltpu.VMEM((F, D), BF), pltpu.SemaphoreType.DMA((1,))])


def _ffn_bwd_input(dgu, dh, x, g, wbufs, offs, name, comm=None):
    def body(dgu_ref, dh_ref, x_ref, g_ref, b0, b1, dx_ref, s_ref, w_s, sem):
        _load_ffn_weights((b0, b1), offs, (w_s.at[pl.ds(0, F), :], w_s.at[pl.ds(F, F), :]), sem)

        @pl.when(pl.program_id(0) == 0)
        def _():
            s_ref[...] = jnp.zeros_like(s_ref)

        dn = _nn(dgu_ref[0], w_s[...])
        dxn, dg = _rmsnorm_bwd(x_ref[...], g_ref[...], dn)
        dx_ref[...] = dh_ref[...] + dxn
        s_ref[0:1, :] += dg

    row = lambda i: (i, 0)
    return _call(
        body, name=name, grid=(T // TM,), args=[dgu, dh, x, g, *wbufs], comm=comm,
        in_specs=[pl.BlockSpec((1, TM, 2 * F), lambda i: (0, i, 0)), pl.BlockSpec((TM, D), row),
                  pl.BlockSpec((TM, D), row), pl.BlockSpec((1, D), lambda i: (0, 0)), ANY, ANY],
        out_shape=[jax.ShapeDtypeStruct((T, D), F32), jax.ShapeDtypeStruct((8, D), F32)],
        out_specs=[pl.BlockSpec((TM, D), row), pl.BlockSpec((8, D), lambda i: (0, 0))],
        scratch_shapes=[pltpu.VMEM((2 * F, D), BF), pltpu.SemaphoreType.DMA((2,))])


def _tn_matmul(lhs, rhs, tr, name, comm=None):
    ng, _, cdim = lhs.shape
    nc, nk = cdim // tr, T // TK
    if rhs.ndim == 2:
        r_spec = pl.BlockSpec((TK, D), lambda g, c, k: (k, 0))
    else:
        r_spec = pl.BlockSpec((None, TK, D), lambda g, c, k: (g, k, 0))

    def body(l_ref, r_ref, o_ref, acc):
        k = pl.program_id(2)

        @pl.when(k == 0)
        def _():
            acc[...] = jnp.zeros_like(acc)

        acc[...] += _tn(l_ref[...], r_ref[...])

        @pl.when(k == nk - 1)
        def _():
            o_ref[...] = acc[...].astype(BF)

    return _call(
        body, name=name, grid=(ng, nc, nk), args=[lhs, rhs], comm=comm,
        in_specs=[pl.BlockSpec((None, TK, tr), lambda g, c, k: (g, k, c)), r_spec],
        out_shape=[jax.ShapeDtypeStruct((ng * cdim, D), BF)],
        out_specs=[pl.BlockSpec((tr, D), lambda g, c, k: (g * nc + c, 0))],
        scratch_shapes=[pltpu.VMEM((tr, D), F32)])


def _mix_out_bwd(dh2, ya, yb, z, a1, q, lng, lnb, wsq, comm=None):
    def body(dh_ref, ya_ref, yb_ref, ga_ref, gb_ref, a1_ref, q_ref, lng_ref, lnb_ref, wa_ref, wb_ref, wo_ref,
             dzg_ref, da1_ref, dq_ref, l_ref, r_ref, s_ref):
        @pl.when(pl.program_id(0) == 0)
        def _():
            s_ref[...] = jnp.zeros_like(s_ref)

        dhb = dh_ref[...].astype(BF)
        dm = _nt(dhb, wo_ref[...]).astype(BF)
        ya, yb = ya_ref[...], yb_ref[...]
        sa, sb = _sig(ga_ref[...]), _sig(gb_ref[...])
        l_ref[0] = sa * ya + sb * yb
        l_ref[2] = q_ref[...]
        dzg_ref[0] = (dm * ya) * (sa * (1.0 - sa))
        dzg_ref[1] = (dm * yb) * (sb * (1.0 - sb))
        dya = dm * sa
        dyb = dm * sb
        r_ref[0] = dhb
        r_ref[1] = dya
        r_ref[2] = dyb
        dq_ref[...] = _nt(dyb, wb_ref[...]).astype(BF)
        da3 = _nt(dya, wa_ref[...])
        lng = lng_ref[...]
        xh, rs, a2, sg = _layernorm_silu(a1_ref[...].astype(F32), lng, lnb_ref[...])
        l_ref[1] = (a2 * sg).astype(BF)
        da2 = da3 * (sg * (1.0 + a2 * (1.0 - sg)))
        s_ref[0:1, :] += jnp.sum(da2 * xh, axis=0, keepdims=True)
        s_ref[1:2, :] += jnp.sum(da2, axis=0, keepdims=True)
        dxh = da2 * lng
        da1 = rs * (dxh - jnp.mean(dxh, axis=-1, keepdims=True) - xh * jnp.mean(dxh * xh, axis=-1, keepdims=True))
        da1_ref[...] = da1.astype(BF)
        s_ref[2:3, :] += jnp.sum(da1, axis=0, keepdims=True)

    row = lambda i: (i, 0)
    row3 = lambda i: (0, i, 0)
    vec = pl.BlockSpec((1, D), lambda i: (0, 0))
    return _call(
        body, name="mix_out_bwd", grid=(T // TM,), args=[dh2, ya, yb, z, z, a1, q, lng, lnb, wsq, wsq, wsq], comm=comm,
        in_specs=[pl.BlockSpec((TM, D), row), pl.BlockSpec((TM, D), row), pl.BlockSpec((TM, D), row),
                  pl.BlockSpec((None, TM, D), lambda i: (5, i, 0)), pl.BlockSpec((None, TM, D), lambda i: (6, i, 0)),
                  pl.BlockSpec((TM, D), row), pl.BlockSpec((TM, D), row), vec, vec] + _square_specs((0, 1, 2)),
        out_shape=[jax.ShapeDtypeStruct((2, T, D), BF), jax.ShapeDtypeStruct((T, D), BF),
                   jax.ShapeDtypeStruct((T, D), BF), jax.ShapeDtypeStruct((3, T, D), BF),
                   jax.ShapeDtypeStruct((3, T, D), BF), jax.ShapeDtypeStruct((8, D), F32)],
        out_specs=[pl.BlockSpec((2, TM, D), row3), pl.BlockSpec((TM, D), row), pl.BlockSpec((TM, D), row),
                   pl.BlockSpec((3, TM, D), row3), pl.BlockSpec((3, TM, D), row3), pl.BlockSpec((8, D), lambda i: (0, 0))])


def _mix_in_bwd(dz, dh2, h1, gm, win, comm=None):
    def body(dz_ref, dh_ref, h_ref, g_ref, *rest):
        w_any, (o_ref, ob_ref, s_ref, w_s, sem) = rest[:len(win)], rest[len(win):]
        _load_in_proj([(b, first) for b, (_, first) in zip(w_any, win)], w_s, sem)

        @pl.when(pl.program_id(0) == 0)
        def _():
            s_ref[...] = jnp.zeros_like(s_ref)

        du = _nn(dz_ref[0], w_s[0:D, :])
        for j in range(1, NG):
            du = du + _nn(dz_ref[j], w_s[j * D:(j + 1) * D, :])
        dx, dg = _rmsnorm_bwd(h_ref[...], g_ref[...], du)
        dh1 = dh_ref[...] + dx
        o_ref[...] = dh1
        ob_ref[...] = (0.5 * dh1).astype(BF)
        s_ref[0:1, :] += dg

    row = lambda i: (i, 0)
    return _call(
        body, name="mix_in_bwd", grid=(T // TM,), args=[dz, dh2, h1, gm] + [b for b, _ in win], comm=comm,
        in_specs=[pl.BlockSpec((NG, TM, D), lambda i: (0, i, 0)), pl.BlockSpec((TM, D), row),
                  pl.BlockSpec((TM, D), row), pl.BlockSpec((1, D), lambda i: (0, 0))] + [ANY] * len(win),
        out_shape=[jax.ShapeDtypeStruct((T, D), F32), jax.ShapeDtypeStruct((T, D), BF), jax.ShapeDtypeStruct((8, D), F32)],
        out_specs=[pl.BlockSpec((TM, D), row), pl.BlockSpec((TM, D), row), pl.BlockSpec((8, D), lambda i: (0, 0))],
        scratch_shapes=[pltpu.VMEM((NG * D, D), BF), pltpu.SemaphoreType.DMA((NDEV * len(win),))])


def _row_tile(n, want, mult):
    for t in range(min(want, n), 0, -1):
        if n % t == 0 and t % mult == 0:
            return t
    return n


def _pack_small(s_ffn1, s_in, s_mix, s_ffn2, s_final, dwa, dwb):
    def body(f1, mi, mo, f2, fl, wa_ref, wb_ref, v_ref, k_ref):
        for dst, (ref, row) in enumerate(((f1, 0), (mi, 0), (mo, 0), (mo, 1), (mo, 2), (f2, 0), (fl, 0), (fl, 1))):
            v_ref[dst:dst + 1, :] = ref[row:row + 1, :]
        for k in range(NDEV):
            k_ref[k, 0:32, :] = wa_ref[:, k * LANE:(k + 1) * LANE]
            k_ref[k, 32:40, :] = wb_ref[:, k * LANE:(k + 1) * LANE]

    return pl.pallas_call(
        body, name="pack_small",
        out_shape=(jax.ShapeDtypeStruct((8, D), F32), jax.ShapeDtypeStruct((NDEV, 40, LANE), F32)),
    )(s_ffn1, s_in, s_mix, s_ffn2, s_final, dwa, dwb)


def _sum_small(vecs, convs):
    def body(v_ref, k_ref, vs_ref, ks_ref, l_ref):
        s, c = v_ref[0], k_ref[0]
        for k in range(1, NDEV):
            s = s + v_ref[k]
            c = c + k_ref[k]
        vs_ref[...] = s
        ks_ref[...] = c
        l_ref[...] = jnp.broadcast_to(jnp.sum(s[7:8, :], axis=-1, keepdims=True), (8, LANE))

    return pl.pallas_call(
        body, name="sum_small",
        out_shape=(jax.ShapeDtypeStruct((8, D), F32), jax.ShapeDtypeStruct((40, LANE), F32),
                   jax.ShapeDtypeStruct((8, LANE), F32)),
    )(vecs, convs)


def _adam_update(g, w, m, v):
    m2 = ADAM_B1 * m + (1.0 - ADAM_B1) * g
    v2 = ADAM_B2 * v + (1.0 - ADAM_B2) * (g * g)
    c1 = 1.0 - ADAM_B1 ** ADAM_STEP
    c2 = 1.0 - ADAM_B2 ** ADAM_STEP
    return -ADAM_LR * ((m2 / c1) / (jnp.sqrt(v2 / c2) + ADAM_EPS) + ADAM_WD * w), m2, v2


def _adam_in_proj(parts, w, m, v):
    rows = w.shape[1]
    tr = _row_tile(D, 256, LANE)

    def body(*refs):
        p_refs = refs[:len(parts)]
        w_ref, m_ref, v_ref, g_ref, d_ref, m2_ref, v2_ref = refs[len(parts):]
        sums = []
        for p in p_refs:
            s = p[0].astype(F32)
            for k in range(1, p.shape[0]):
                s = s + p[k].astype(F32)
            sums.append(s)
        g = jnp.concatenate(sums, axis=0).T
        g_ref[...] = g
        d_ref[...], m2_ref[...], v2_ref[...] = _adam_update(g, w_ref[...], m_ref[...], v_ref[...])

    spec = pl.BlockSpec((tr, rows), lambda i: (i, 0))
    return _call(body, name="adam_in", grid=(D // tr,), args=list(parts) + [w, m, v],
                 in_specs=[pl.BlockSpec((p.shape[0], p.shape[1], tr), lambda i: (0, 0, i)) for p in parts] + [spec] * 3,
                 out_shape=[jax.ShapeDtypeStruct((D, rows), F32)] * 4, out_specs=[spec] * 4)


def _adam(gs, ws, ms, vs, name, comm=None):
    n = len(gs)
    rows, cols = ws[0].shape
    tr = _row_tile(rows, 256, 16)
    summed = [isinstance(g, tuple) for g in gs]

    def body(*refs):
        for i in range(n):
            g_in, w, m, v = refs[4 * i], refs[4 * i + 1][...], refs[4 * i + 2][...], refs[4 * i + 3][...]
            g_ref, d_ref, m_ref, v_ref = refs[4 * n + 4 * i: 4 * n + 4 * i + 4]
            if summed[i]:
                g = g_in[0].astype(F32)
                for k in range(1, g_in.shape[0]):
                    g = g + g_in[k].astype(F32)
            else:
                g = g_in[...]
            g_ref[...] = g
            d_ref[...], m_ref[...], v_ref[...] = _adam_update(g, w, m, v)

    spec = pl.BlockSpec((tr, cols), lambda i: (i, 0))
    args, in_specs = [], []
    for i in range(n):
        if summed[i]:
            slots, first = gs[i]
            args.append(slots)
            in_specs.append(pl.BlockSpec((slots.shape[0], tr, cols), lambda i, b=first // tr: (0, b + i, 0)))
        else:
            args.append(gs[i])
            in_specs.append(spec)
        args += [ws[i], ms[i], vs[i]]
        in_specs += [spec] * 3
    outs = _call(body, name=name, grid=(rows // tr,), args=args, comm=comm, in_specs=in_specs,
                 out_shape=[jax.ShapeDtypeStruct((rows, cols), F32)] * (4 * n), out_specs=[spec] * (4 * n))
    return [tuple(outs[4 * i: 4 * i + 4]) for i in range(n)], outs[4 * n:]


def kernel(x, ffn1_norm, ffn1_w_gate, ffn1_w_up, ffn1_w_down, mix_norm, w_in, a_dw_w, a_dw_b, a_ln_g, a_ln_b, a_w_out, b_conv_w, b_w_out, w_o, ffn2_norm, ffn2_w_gate, ffn2_w_up, ffn2_w_down, final_norm, loss_target, m_ffn1_norm, m_ffn1_w_gate, m_ffn1_w_up, m_ffn1_w_down, m_mix_norm, m_w_in, m_a_dw_w, m_a_dw_b, m_a_ln_g, m_a_ln_b, m_a_w_out, m_b_conv_w, m_b_w_out, m_w_o, m_ffn2_norm, m_ffn2_w_gate, m_ffn2_w_up, m_ffn2_w_down, m_final_norm, v_ffn1_norm, v_ffn1_w_gate, v_ffn1_w_up, v_ffn1_w_down, v_mix_norm, v_w_in, v_a_dw_w, v_a_dw_b, v_a_ln_g, v_a_ln_b, v_a_w_out, v_b_conv_w, v_b_w_out, v_w_o, v_ffn2_norm, v_ffn2_w_gate, v_ffn2_w_up, v_ffn2_w_down, v_final_norm):
    names = ("ffn1_norm", "ffn1_w_gate", "ffn1_w_up", "ffn1_w_down", "mix_norm", "w_in", "a_dw_w", "a_dw_b",
             "a_ln_g", "a_ln_b", "a_w_out", "b_conv_w", "b_w_out", "w_o", "ffn2_norm", "ffn2_w_gate", "ffn2_w_up",
             "ffn2_w_down", "final_norm")
    w = dict(ffn1_norm=ffn1_norm, ffn1_w_gate=ffn1_w_gate, ffn1_w_up=ffn1_w_up, ffn1_w_down=ffn1_w_down,
             mix_norm=mix_norm, w_in=w_in, a_dw_w=a_dw_w, a_dw_b=a_dw_b, a_ln_g=a_ln_g, a_ln_b=a_ln_b,
             a_w_out=a_w_out, b_conv_w=b_conv_w, b_w_out=b_w_out, w_o=w_o, ffn2_norm=ffn2_norm,
             ffn2_w_gate=ffn2_w_gate, ffn2_w_up=ffn2_w_up, ffn2_w_down=ffn2_w_down, final_norm=final_norm)
    m = dict(ffn1_norm=m_ffn1_norm, ffn1_w_gate=m_ffn1_w_gate, ffn1_w_up=m_ffn1_w_up, ffn1_w_down=m_ffn1_w_down,
             mix_norm=m_mix_norm, w_in=m_w_in, a_dw_w=m_a_dw_w, a_dw_b=m_a_dw_b, a_ln_g=m_a_ln_g, a_ln_b=m_a_ln_b,
             a_w_out=m_a_w_out, b_conv_w=m_b_conv_w, b_w_out=m_b_w_out, w_o=m_w_o, ffn2_norm=m_ffn2_norm,
             ffn2_w_gate=m_ffn2_w_gate, ffn2_w_up=m_ffn2_w_up, ffn2_w_down=m_ffn2_w_down, final_norm=m_final_norm)
    v = dict(ffn1_norm=v_ffn1_norm, ffn1_w_gate=v_ffn1_w_gate, ffn1_w_up=v_ffn1_w_up, ffn1_w_down=v_ffn1_w_down,
             mix_norm=v_mix_norm, w_in=v_w_in, a_dw_w=v_a_dw_w, a_dw_b=v_a_dw_b, a_ln_g=v_a_ln_g, a_ln_b=v_a_ln_b,
             a_w_out=v_a_w_out, b_conv_w=v_b_conv_w, b_w_out=v_b_w_out, w_o=v_w_o, ffn2_norm=v_ffn2_norm,
             ffn2_w_gate=v_ffn2_w_gate, ffn2_w_up=v_ffn2_w_up, ffn2_w_down=v_ffn2_w_down, final_norm=v_final_norm)
    flat = _pack_weights(dict(wg1=ffn1_w_gate[0].T, wu1=ffn1_w_up[0].T, wd1=ffn1_w_down[0], wg2=ffn2_w_gate[0].T,
                              wu2=ffn2_w_up[0].T, wd2=ffn2_w_down[0], win=w_in[0], wa=a_w_out[0], wb=b_w_out[0],
                              wo=w_o[0]))
    cw_shard = jnp.concatenate([a_dw_w[0], jnp.zeros((1, LANE), F32), b_conv_w[0], jnp.zeros((5, LANE), F32)], axis=0)

    x2, tgt = x[0], loss_target[0]
    st_a, st_b, st_b2 = ("wg1", "wu1"), ("wd1", "win/0/2"), ("win/1/2",)
    st_c, st_d, st_e = ("wa", "wb", "wo", "wg2"), ("wu2",), ("wd2",)

    buf_a, cw = _run_comm(_join(_ag_comm(st_a, flat), _direct_comm(cw_shard, False)), "ag_ffn1")
    n1, gg1, uu1, act1, buf_b = _ffn_gate_up(x2, ffn1_norm, (buf_a, buf_a), (0, F), "ffn1_gate_up", _ag_comm(st_b, flat))
    h1, buf_b2 = _ffn_down(x2, act1, buf_b, 0, "ffn1_down", _ag_comm(st_b2, flat))
    win = ((buf_b, F), (buf_b2, 0))
    u, z, buf_c = _mix_in(h1, mix_norm, win, _ag_comm(st_c, flat))
    dft = _dft_constants()
    cw = jnp.transpose(cw, (1, 0, 2)).reshape(40, D)
    a1, q, buf_d = _conv_fwd_dft(z, cw, a_dw_b, dft, _ag_comm(st_d, flat))
    h2, ya, yb, buf_e = _mix_out(a1, q, z, h1, a_ln_g, a_ln_b, buf_c, _ag_comm(st_e, flat))
    ffn2_bufs, ffn2_offs = (buf_c, buf_d, buf_e), (3 * D, 0, 0)
    dh3, dhb3, s_final, n2, gg2, uu2, act2 = _ffn_fwd(h2, ffn2_norm, ffn2_bufs, ffn2_offs, "ffn2_fwd",
                                          final=(final_norm.reshape(1, D), tgt))

    tr_f = F // 2 if (F // 2) % LANE == 0 else F
    def pair(stage, src):
        return _rs_pair_comm(stage, src)

    def chip(stage, src, pair_buf, tag):
        return _rs_chip_comm(_pair_add(stage, src, pair_buf, "pair_add_" + tag))

    (dgu2,) = _ffn_bwd_hidden(dhb3, gg2, uu2, buf_e, 0, "ffn2_bwd_h")
    (gu2,) = _tn_matmul(dgu2, n2, tr_f, "dw_gu2")
    s2a, src2a = ("wg2", "wu2"), dict(wg2=(gu2, 0), wu2=(gu2, F))
    gd2, pair2a = _tn_matmul(act2, dhb3, tr_f, "dw_d2", pair(s2a, src2a))
    s2b, src2b = ("wd2",), dict(wd2=(gd2, 0))
    dh2, s_ffn2, pair2b = _ffn_bwd_input(dgu2, dh3, h2, ffn2_norm, (buf_c, buf_d), (3 * D, 0), "ffn2_bwd_x",
                                         pair(s2b, src2b))
    dzg, da1, dq, lsq, rsq, s_mix, recv2b = _mix_out_bwd(dh2, ya, yb, z, a1, q, a_ln_g, a_ln_b, buf_c,
                                                          chip(s2b, src2b, pair2b, "2b"))
    (gsq,) = _tn_matmul(lsq, rsq, D, "dw_square")
    ssq, srcsq = ("wa", "wb", "wo"), dict(wa=(gsq, D), wb=(gsq, 2 * D), wo=(gsq, 0))
    dz, dwa, dwb, recv2a, pairsq = _conv_bwd_dft(z, da1, dq, dzg, cw, dft,
                                                 _join(chip(s2a, src2a, pair2a, "2a"), pair(ssq, srcsq)))
    gin, recvsq = _tn_matmul(dz, u, D, "dw_in", chip(ssq, srcsq, pairsq, "sq"))
    sin_a, sin_b, srcin = ("win/0/2",), ("win/1/2",), {"win/0/2": (gin, 0), "win/1/2": (gin, 0)}
    dh1, dhb1, s_in, pairin_a, pairin_b = _mix_in_bwd(dz, dh2, h1, mix_norm, win,
                                                _join(pair(sin_a, srcin), pair(sin_b, srcin)))
    dgu1, recvin_a = _ffn_bwd_hidden(dhb1, gg1, uu1, buf_b, 0, "ffn1_bwd_h",
                                           chip(sin_a, srcin, pairin_a, "in_a"))
    gu1, recvin_b = _tn_matmul(dgu1, n1, tr_f, "dw_gu1", chip(sin_b, srcin, pairin_b, "in_b"))
    s1a, src1a = ("wg1", "wu1"), dict(wg1=(gu1, 0), wu1=(gu1, F))
    gd1, pair1a = _tn_matmul(act1, dhb1, tr_f, "dw_d1", pair(s1a, src1a))
    s1b, src1b = ("wd1",), dict(wd1=(gd1, 0))
    dx, s_ffn1, recv1a, pair1b = _ffn_bwd_input(dgu1, dh1, x2, ffn1_norm, (buf_a, buf_a), (0, F), "ffn1_bwd_x",
                                                _join(chip(s1a, src1a, pair1a, "1a"), pair(s1b, src1b)))

    vec8, convk = _pack_small(s_ffn1, s_in, s_mix, s_ffn2, s_final, dwa, dwb)
    recv1b, vec_all, conv_all = _run_comm(
        _join(chip(s1b, src1b, pair1b, "1b"), _join(_direct_comm(vec8, False), _direct_comm(convk, True))), "xchg_tail")
    vec_sum, conv_sum, loss_blk = _sum_small(vec_all, conv_all)
    loss = loss_blk[0, 0]

    fs = F // NDEV
    g = dict(ffn1_w_gate=(recv1a, 0), ffn1_w_up=(recv1a, fs), ffn2_w_gate=(recv2a, 0), ffn2_w_up=(recv2a, fs),
             ffn1_w_down=(recv1b, 0),
             ffn2_w_down=(recv2b, 0), a_w_out=(recvsq, 0), b_w_out=(recvsq, D // NDEV), w_o=(recvsq, 2 * (D // NDEV)),
             ffn1_norm=vec_sum[0:1], mix_norm=vec_sum[1:2], a_ln_g=vec_sum[2:3], a_ln_b=vec_sum[3:4],
             a_dw_b=vec_sum[4:5], ffn2_norm=vec_sum[5:6], final_norm=vec_sum[6:7],
             a_dw_w=conv_sum[0:KA], b_conv_w=conv_sum[32:32 + KB])
    grad, upd = {}, {}

    def run(group, name, as2d=lambda a: a[0], back=lambda a, n: a.reshape(w[n].shape)):
        res, _ = _adam([g[n] for n in group], [as2d(w[n]) for n in group], [as2d(m[n]) for n in group],
                       [as2d(v[n]) for n in group], name)
        for n, r in zip(group, res):
            grad[n], upd[n] = back(r[0], n), tuple(back(a, n) for a in r[1:])

    run(("ffn1_w_gate", "ffn1_w_up", "ffn2_w_gate", "ffn2_w_up"), "adam_gate_up",
        as2d=lambda a: a[0].T, back=lambda a, n: a.T[None])
    run(("ffn1_w_down", "ffn2_w_down"), "adam_down")
    r_in = _adam_in_proj([recvin_a, recvin_b], w_in[0], m_w_in[0], v_w_in[0])
    grad["w_in"], upd["w_in"] = r_in[0][None], tuple(a[None] for a in r_in[1:])
    run(("a_w_out", "b_w_out", "w_o"), "adam_square")
    run(("a_dw_w",), "adam_dw")
    run(("b_conv_w",), "adam_conv")
    run(("ffn1_norm", "mix_norm", "a_dw_b", "a_ln_g", "a_ln_b", "ffn2_norm", "final_norm"), "adam_vec",
        as2d=lambda a: a.reshape(1, D))

    return (loss, dx.reshape(x.shape), *[grad[n] for n in names], *[upd[n][0] for n in names],
            *[upd[n][1] for n in names], *[upd[n][2] for n in names])
```

```python
import jax
import jax.numpy as jnp
from jax import lax
from jax.experimental import pallas as pl
from jax.experimental.pallas import tpu as pltpu

T = 4096
D = 1024
F = 2816
NG = 7
NDEV = 8
NCHIP = 4
KA, KB = 31, 3
EPS = 1e-6
ADAM_LR, ADAM_B1, ADAM_B2, ADAM_EPS, ADAM_WD, ADAM_STEP = 0.001, 0.9, 0.999, 1e-08, 0.01, 10

TM = 512
FC = 256
TB = 1024
NB = 256
HB = NB // 2
CW = 256
CHB = 64
LANE = 128
TK = 2048
VMEM_LIMIT = 56 * 1024 * 1024

BF = jnp.bfloat16
F32 = jnp.float32
MESH = pl.DeviceIdType.MESH
ANY = pl.BlockSpec(memory_space=pl.ANY)

ORDER = ("wg1", "wu1", "wd1", "wg2", "wu2", "wd2", "win", "wa", "wb", "wo")


class _Layout:
    def __init__(self):
        fs, dis, ds = F // NDEV, NG * D // NDEV, D // NDEV
        self.rows = dict(wg1=fs, wu1=fs, wd1=fs, wg2=fs, wu2=fs, wd2=fs, win=dis, wa=ds, wb=ds, wo=ds)
        self.fl, off = {}, 0
        for n in ORDER:
            self.fl[n] = off
            off += self.rows[n]
        self.RT = off


class _Stage:
    def __init__(self, names):
        lay = _Layout()
        self.names = names
        self.rows, self.full, self.sub, self.fl = {}, {}, {}, {}
        for n in names:
            base, i, k = (n.split("/") + ["0", "1"])[:3]
            self.full[n] = lay.rows[base]
            self.rows[n] = lay.rows[base] // int(k)
            self.sub[n] = int(i) * self.rows[n]
            self.fl[n] = lay.fl[base] + self.sub[n]
        self.off, self.wc, o, w = {}, {}, 0, 0
        for n in names:
            self.off[n], self.wc[n] = o, w
            o += self.rows[n]
            w += NDEV * self.rows[n]
        self.R, self.W = o, w

    def grad_row(self, n, first, dev_lin):
        return first + dev_lin * self.full[n] + self.sub[n]


def _nt(a, b):
    return lax.dot_general(a, b, (((1,), (1,)), ((), ())), preferred_element_type=F32)


def _nn(a, b):
    return lax.dot_general(a, b, (((1,), (0,)), ((), ())), preferred_element_type=F32)


def _tn(a, b):
    return lax.dot_general(a, b, (((0,), (0,)), ((), ())), preferred_element_type=F32)


def _sig(x):
    return 1.0 / (1.0 + jnp.exp(-x))


def _position():
    return lax.axis_index("x"), lax.axis_index("y"), lax.axis_index("c")


def _peer(pos, j):
    x, y, c = pos
    return (1 - x if j & 4 else x, 1 - y if j & 2 else y, 1 - c if j & 1 else c)


def _lin(pos):
    return 4 * pos[0] + 2 * pos[1] + pos[2]


def _chip(pos):
    return 2 * pos[0] + pos[1]


class _Comm:
    def __init__(self, inputs, out_shapes, scratch, start, finish, middle=None):
        self.inputs, self.out_shapes, self.scratch = inputs, out_shapes, scratch
        self.start, self.finish, self.middle = start, finish, middle


def _call(body, *, name, grid, args, in_specs, out_shape, out_specs, scratch_shapes=(), comm=None,
          num_scalar_prefetch=0):
    in_specs, out_shape, out_specs, scratch_shapes = list(in_specs), list(out_shape), list(out_specs), list(scratch_shapes)
    n_in, n_out, n_scr = len(in_specs), len(out_shape), len(scratch_shapes)
    sp = num_scalar_prefetch
    if comm is None:
        kernel_fn = lambda *refs: body(*refs)
        c_in = c_out = c_scr = 0
    else:
        c_in, c_out, c_scr = len(comm.inputs), len(comm.out_shapes), len(comm.scratch)

        def kernel_fn(*refs):
            pre, refs = refs[:sp], refs[sp:]
            ins, cins = refs[:n_in], refs[n_in:n_in + c_in]
            o0 = n_in + c_in
            outs, couts = refs[o0:o0 + n_out], refs[o0 + n_out:o0 + n_out + c_out]
            s0 = o0 + n_out + c_out
            scr, cscr = refs[s0:s0 + n_scr], refs[s0 + n_scr:]
            step, steps = pl.program_id(0), grid[0]
            for a in range(1, len(grid)):
                step, steps = step * grid[a] + pl.program_id(a), steps * grid[a]
            first, last = step == 0, step == steps - 1

            @pl.when(first)
            def _():
                comm.start(cins, couts, cscr)

            if comm.middle is not None:
                @pl.when(step == (steps // 2 if steps > 2 else steps - 1))
                def _():
                    comm.middle(cins, couts, cscr)

            body(*pre, *ins, *outs, *scr)

            @pl.when(last)
            def _():
                comm.finish(cins, couts, cscr)

        args = list(args) + list(comm.inputs)
        in_specs += [ANY] * c_in
        out_shape += list(comm.out_shapes)
        out_specs += [ANY] * c_out
        scratch_shapes += list(comm.scratch)
    params = pltpu.CompilerParams(dimension_semantics=("arbitrary",) * len(grid), vmem_limit_bytes=VMEM_LIMIT)
    if sp:
        grid_spec = pltpu.PrefetchScalarGridSpec(num_scalar_prefetch=sp, grid=grid, in_specs=in_specs,
                                                 out_specs=out_specs, scratch_shapes=scratch_shapes)
        return pl.pallas_call(kernel_fn, name=name, grid_spec=grid_spec, out_shape=out_shape,
                              compiler_params=params)(*args)
    return pl.pallas_call(kernel_fn, name=name, grid=grid, in_specs=in_specs, out_shape=out_shape, out_specs=out_specs,
                          scratch_shapes=scratch_shapes, compiler_params=params)(*args)


def _join(a, b):
    na = (len(a.inputs), len(a.out_shapes), len(a.scratch))

    def split(refs):
        return ([r[:n] for r, n in zip(refs, na)], [r[n:] for r, n in zip(refs, na)])

    def start(*refs):
        ra, rb = split(refs)
        a.start(*ra)
        b.start(*rb)

    def finish(*refs):
        ra, rb = split(refs)
        a.finish(*ra)
        b.finish(*rb)

    def middle(*refs):
        for stage, r in zip((a, b), split(refs)):
            if stage.middle is not None:
                stage.middle(*r)

    return _Comm(list(a.inputs) + list(b.inputs), list(a.out_shapes) + list(b.out_shapes),
                 list(a.scratch) + list(b.scratch), start, finish,
                 middle if (a.middle is not None or b.middle is not None) else None)


def _run_comm(comm, name):
    def body(*refs):
        c_in, c_out = len(comm.inputs), len(comm.out_shapes)
        parts = (refs[:c_in], refs[c_in:c_in + c_out], refs[c_in + c_out:])
        comm.start(*parts)
        if comm.middle is not None:
            comm.middle(*parts)
        comm.finish(*parts)

    return pl.pallas_call(
        body, name=name, out_shape=list(comm.out_shapes), in_specs=[ANY] * len(comm.inputs),
        out_specs=[ANY] * len(comm.out_shapes), scratch_shapes=list(comm.scratch))(*comm.inputs)


def _ag_comm(names, flat):
    st = _Stage(names)

    def ring(me):
        x, y, c = me
        diagonal = x == y
        up = (jnp.where(diagonal, x, 1 - x), jnp.where(diagonal, 1 - y, y), c)
        down = (jnp.where(diagonal, 1 - x, x), jnp.where(diagonal, y, 1 - y), c)
        low = c == 0
        passed = tuple(jnp.where(low, d, u) for d, u in zip(down, up))
        target = tuple(jnp.where(low, u, d) for d, u in zip(down, up))
        return up, down, (1 - x, 1 - y, c), passed, target

    def parts(refs):
        (flat_ref,), (out_ref,), (send_sems, recv_sems, local_sem) = refs
        me = _position()

        def region(name, dev):
            r = st.rows[name]
            return out_ref.at[pl.ds(st.wc[name] + _lin(dev) * r, r), :]

        def own(name):
            return flat_ref.at[pl.ds(st.fl[name], st.rows[name]), :]

        def copies(k, dev, to, from_flat):
            return [pltpu.make_async_remote_copy(
                src_ref=own(n) if from_flat else region(n, dev), dst_ref=region(n, dev), send_sem=send_sems.at[k],
                recv_sem=recv_sems.at[k], device_id=to, device_id_type=MESH) for n in names]

        def whole(k):
            return pltpu.make_async_remote_copy(
                src_ref=flat_ref.at[pl.ds(0, st.R), :], dst_ref=out_ref.at[pl.ds(0, st.R), :],
                send_sem=send_sems.at[k], recv_sem=recv_sems.at[k], device_id=me, device_id_type=MESH)

        return me, region, own, copies, whole, flat_ref, out_ref, local_sem

    def start(*refs):
        me, region, own, copies, _, _, _, local_sem = parts(refs)
        for n in names:
            pltpu.make_async_copy(own(n), region(n, me), local_sem).start()
        up, down, _, _, _ = ring(me)
        for k, to in ((1, up), (2, down), (0, _peer(me, 1))):
            for cp in copies(k, me, to, True):
                cp.start()

    def middle(*refs):
        me, _, _, copies, whole, _, _, _ = parts(refs)
        up, down, _, passed, target = ring(me)
        sib = _peer(me, 1)
        whole(1).wait_recv()
        whole(2).wait_recv()
        for k, dev, to in ((3, passed, target), (4, down, sib), (5, up, sib)):
            for cp in copies(k, dev, to, False):
                cp.start()

    def finish(*refs):
        me, _, _, copies, whole, flat_ref, out_ref, local_sem = parts(refs)
        _, _, across, _, _ = ring(me)
        whole(3).wait_recv()
        for cp in copies(6, across, _peer(me, 1), False):
            cp.start()
        whole(0).wait_recv()
        for j in range(3):
            whole(4 + j).wait_recv()
        for k in range(7):
            whole(k).wait_send()
        pltpu.make_async_copy(flat_ref.at[pl.ds(0, st.R), :], out_ref.at[pl.ds(0, st.R), :], local_sem).wait()

    return _Comm([flat], [jax.ShapeDtypeStruct((st.W, D), BF)],
                 [pltpu.SemaphoreType.DMA((7,)), pltpu.SemaphoreType.DMA((7,)), pltpu.SemaphoreType.DMA],
                 start, finish, middle)


def _rs_pair_comm(names, src):
    st = _Stage(names)
    arrays = []
    for n in names:
        if not any(src[n][0] is a for a in arrays):
            arrays.append(src[n][0])
    idx = {n: [i for i, a in enumerate(arrays) if a is src[n][0]][0] for n in names}

    def slot_wait(refs):
        recv = refs[1][0]
        send_sem, recv_sem = refs[2]
        return pltpu.make_async_remote_copy(src_ref=recv, dst_ref=recv, send_sem=send_sem, recv_sem=recv_sem,
                                            device_id=_position(), device_id_type=MESH)

    def start(*refs):
        ins, (recv,), (send_sem, recv_sem) = refs
        me = _position()
        sib = _peer(me, 1)
        for q in range(NCHIP):
            dev = (q // 2, q % 2, sib[2])
            for n in names:
                r = st.rows[n]
                pltpu.make_async_remote_copy(
                    src_ref=ins[idx[n]].at[pl.ds(st.grad_row(n, src[n][1], _lin(dev)), r), :],
                    dst_ref=recv.at[q, pl.ds(st.off[n], r), :], send_sem=send_sem, recv_sem=recv_sem,
                    device_id=sib, device_id_type=MESH).start()

    def finish(*refs):
        w = slot_wait(refs)
        w.wait_recv()
        w.wait_send()

    return _Comm(arrays, [jax.ShapeDtypeStruct((NCHIP, st.R, D), BF)],
                 [pltpu.SemaphoreType.DMA, pltpu.SemaphoreType.DMA], start, finish)


def _pair_add(names, src, recv, name):
    st = _Stage(names)
    c_arr = jnp.reshape(lax.axis_index("c"), (1,)).astype(jnp.int32)

    def body(c_ref, *refs):
        r_ref, o_ref = refs[len(names)], refs[len(names) + 1]
        for a_ref, n in zip(refs, names):
            rows = slice(st.off[n], st.off[n] + st.rows[n])
            o_ref[rows, :] = (a_ref[...].astype(F32) + r_ref[rows, :].astype(F32)).astype(BF)

    def shard_spec(n):
        r = st.rows[n]
        base, step = st.grad_row(n, src[n][1], 0) // r, st.full[n] // r
        return pl.BlockSpec((r, D), lambda q, c_ref: (base + step * (2 * q + c_ref[0]), 0))

    slot = pl.BlockSpec((None, st.R, D), lambda q, c_ref: (q, 0, 0))
    return _call(body, name=name, grid=(NCHIP,), args=[c_arr] + [src[n][0] for n in names] + [recv],
                 in_specs=[shard_spec(n) for n in names] + [slot],
                 out_shape=[jax.ShapeDtypeStruct((NCHIP, st.R, D), BF)], out_specs=[slot], num_scalar_prefetch=1)[0]


def _rs_chip_comm(part):
    def copies(refs):
        (p_ref,), (recv,), (send_sems, recv_sems, local_sem) = refs
        me = _position()
        mine = pltpu.make_async_copy(p_ref.at[_chip(me)], recv.at[_chip(me)], local_sem)
        out = []
        for j, bits in enumerate((4, 2, 6)):
            to = _peer(me, bits)
            out.append(pltpu.make_async_remote_copy(
                src_ref=p_ref.at[_chip(to)], dst_ref=recv.at[_chip(me)], send_sem=send_sems.at[j],
                recv_sem=recv_sems.at[j], device_id=to, device_id_type=MESH))
        return mine, out

    def start(*refs):
        mine, out = copies(refs)
        mine.start()
        for cp in out:
            cp.start()

    def finish(*refs):
        mine, out = copies(refs)
        for cp in out:
            cp.wait_recv()
        for cp in out:
            cp.wait_send()
        mine.wait()

    return _Comm([part], [jax.ShapeDtypeStruct(part.shape, BF)],
                 [pltpu.SemaphoreType.DMA((3,)), pltpu.SemaphoreType.DMA((3,)), pltpu.SemaphoreType.DMA],
                 start, finish)


def _direct_comm(x, scatter):
    def copies(refs):
        (x_ref,), (out_ref,), (send_sems, recv_sems, local_sem) = refs
        me = _position()

        def piece(dev):
            return x_ref.at[_lin(dev)] if scatter else x_ref

        mine = pltpu.make_async_copy(piece(me), out_ref.at[_lin(me)], local_sem)
        return mine, [pltpu.make_async_remote_copy(
            src_ref=piece(_peer(me, j)), dst_ref=out_ref.at[_lin(me)], send_sem=send_sems.at[j - 1],
            recv_sem=recv_sems.at[j - 1], device_id=_peer(me, j), device_id_type=MESH) for j in range(1, NDEV)]

    def start(*refs):
        mine, cps = copies(refs)
        mine.start()
        for cp in cps:
            cp.start()

    def finish(*refs):
        mine, cps = copies(refs)
        for cp in cps:
            cp.wait_recv()
        for cp in cps:
            cp.wait_send()
        mine.wait()

    shape = x.shape if scatter else (NDEV,) + x.shape
    return _Comm([x], [jax.ShapeDtypeStruct(shape, x.dtype)],
                 [pltpu.SemaphoreType.DMA((7,)), pltpu.SemaphoreType.DMA((7,)), pltpu.SemaphoreType.DMA],
                 start, finish)


def _pack_weights(shards):
    lay = _Layout()

    def body(*refs):
        o_ref = refs[-1]
        for ref, n in zip(refs, ORDER):
            x = ref[...].T if n == "win" else ref[...]
            o_ref[lay.fl[n]:lay.fl[n] + lay.rows[n], :] = x.astype(BF)

    return pl.pallas_call(
        body, name="pack_weights", out_shape=jax.ShapeDtypeStruct((lay.RT, D), BF),
        compiler_params=pltpu.CompilerParams(vmem_limit_bytes=VMEM_LIMIT))(*[shards[n] for n in ORDER])


def _load_ffn_weights(srcs, offs, scratch, sem):
    @pl.when(pl.program_id(0) == 0)
    def _():
        cps = [pltpu.make_async_copy(s.at[pl.ds(off, dst.shape[0]), :], dst, sem.at[i])
               for i, (s, off, dst) in enumerate(zip(srcs, offs, scratch))]
        for cp in cps:
            cp.start()
        for cp in cps:
            cp.wait()


def _final_loss_tile(xf, g, tgt, s_ref):
    r = lax.rsqrt(jnp.mean(xf * xf, axis=-1, keepdims=True) + EPS)
    xr = xf * r
    e = xr * g - tgt
    s_ref[1:2, :] += jnp.sum(e * e, axis=0, keepdims=True) * (0.5 / D)
    dy = e * (1.0 / D)
    s_ref[0:1, :] += jnp.sum(dy * xr, axis=0, keepdims=True)
    gdy = dy * g
    return r * gdy - xr * (r * jnp.mean(gdy * xr, axis=-1, keepdims=True))


def _ffn_fwd(x, g, wbufs, offs, name, comm=None, final=None):
    nf = F // FC

    def body(x_ref, g_ref, b0, b1, b2, *rest):
        if final is None:
            h_ref, n_ref, gg_ref, uu_ref, a_ref, wg_s, wu_s, wd_s, sem = rest
        else:
            gf_ref, t_ref, dh_ref, dhb_ref, s_ref, n_ref, gg_ref, uu_ref, a_ref, wg_s, wu_s, wd_s, sem = rest

            @pl.when(pl.program_id(0) == 0)
            def _():
                s_ref[...] = jnp.zeros_like(s_ref)

        _load_ffn_weights((b0, b1, b2), offs, (wg_s, wu_s, wd_s), sem)
        xf = x_ref[...]
        r = lax.rsqrt(jnp.mean(xf * xf, axis=-1, keepdims=True) + EPS)
        nb = (xf * r * g_ref[...]).astype(BF)
        n_ref[...] = nb
        acc = jnp.zeros((TM, D), F32)
        for c in range(nf):
            sl = slice(c * FC, (c + 1) * FC)
            gb = _nt(nb, wg_s[sl, :]).astype(BF)
            ub = _nt(nb, wu_s[sl, :]).astype(BF)
            gg_ref[:, sl] = gb
            uu_ref[:, sl] = ub
            a = (gb * _sig(gb)) * ub
            a_ref[0, :, sl] = a
            acc = acc + _nn(a, wd_s[sl, :])
        h = xf + 0.5 * acc
        if final is None:
            h_ref[...] = h
        else:
            dh = _final_loss_tile(h, gf_ref[...], t_ref[...], s_ref)
            dh_ref[...] = dh
            dhb_ref[...] = (0.5 * dh).astype(BF)

    row = lambda i: (i, 0)
    vec = pl.BlockSpec((1, D), lambda i: (0, 0))
    tile = pl.BlockSpec((TM, D), row)
    saved_shapes = [jax.ShapeDtypeStruct((T, D), BF), jax.ShapeDtypeStruct((T, F), BF), jax.ShapeDtypeStruct((T, F), BF),
                    jax.ShapeDtypeStruct((1, T, F), BF)]
    saved_specs = [tile, pl.BlockSpec((TM, F), row), pl.BlockSpec((TM, F), row),
                   pl.BlockSpec((1, TM, F), lambda i: (0, i, 0))]
    if final is None:
        extra_args, extra_specs = [], []
        head_shapes, head_specs = [jax.ShapeDtypeStruct((T, D), F32)], [tile]
    else:
        extra_args, extra_specs = list(final), [vec, tile]
        head_shapes = [jax.ShapeDtypeStruct((T, D), F32), jax.ShapeDtypeStruct((T, D), BF), jax.ShapeDtypeStruct((8, D), F32)]
        head_specs = [tile, tile, pl.BlockSpec((8, D), lambda i: (0, 0))]
    return _call(
        body, name=name, grid=(T // TM,), args=[x, g, *wbufs, *extra_args], comm=comm,
        in_specs=[tile, vec, ANY, ANY, ANY] + extra_specs,
        out_shape=head_shapes + saved_shapes, out_specs=head_specs + saved_specs,
        scratch_shapes=[pltpu.VMEM((F, D), BF)] * 3 + [pltpu.SemaphoreType.DMA((3,))])


def _ffn_gate_up(x, g, wbufs, offs, name, comm=None):
    nf = F // FC

    def body(x_ref, g_ref, b0, b1, n_ref, gg_ref, uu_ref, a_ref, wg_s, wu_s, sem):
        _load_ffn_weights((b0, b1), offs, (wg_s, wu_s), sem)
        xf = x_ref[...]
        r = lax.rsqrt(jnp.mean(xf * xf, axis=-1, keepdims=True) + EPS)
        nb = (xf * r * g_ref[...]).astype(BF)
        n_ref[...] = nb
        for c in range(nf):
            sl = slice(c * FC, (c + 1) * FC)
            gb = _nt(nb, wg_s[sl, :]).astype(BF)
            ub = _nt(nb, wu_s[sl, :]).astype(BF)
            gg_ref[:, sl] = gb
            uu_ref[:, sl] = ub
            a_ref[0, :, sl] = (gb * _sig(gb)) * ub

    row = lambda i: (i, 0)
    tile = pl.BlockSpec((TM, D), row)
    return _call(
        body, name=name, grid=(T // TM,), args=[x, g, *wbufs], comm=comm,
        in_specs=[tile, pl.BlockSpec((1, D), lambda i: (0, 0)), ANY, ANY],
        out_shape=[jax.ShapeDtypeStruct((T, D), BF), jax.ShapeDtypeStruct((T, F), BF), jax.ShapeDtypeStruct((T, F), BF),
                   jax.ShapeDtypeStruct((1, T, F), BF)],
        out_specs=[tile, pl.BlockSpec((TM, F), row), pl.BlockSpec((TM, F), row),
                   pl.BlockSpec((1, TM, F), lambda i: (0, i, 0))],
        scratch_shapes=[pltpu.VMEM((F, D), BF)] * 2 + [pltpu.SemaphoreType.DMA((2,))])


def _ffn_down(x, act, wbuf, off, name, comm=None):
    def body(x_ref, a_ref, b0, h_ref, wd_s, sem):
        _load_ffn_weights((b0,), (off,), (wd_s,), sem)
        h_ref[...] = x_ref[...] + 0.5 * _nn(a_ref[0], wd_s[...])

    tile = pl.BlockSpec((TM, D), lambda i: (i, 0))
    return _call(
        body, name=name, grid=(T // TM,), args=[x, act, wbuf], comm=comm,
        in_specs=[tile, pl.BlockSpec((1, TM, F), lambda i: (0, i, 0)), ANY],
        out_shape=[jax.ShapeDtypeStruct((T, D), F32)], out_specs=[tile],
        scratch_shapes=[pltpu.VMEM((F, D), BF), pltpu.SemaphoreType.DMA((1,))])


def _load_in_proj(parts, w_s, sem):
    @pl.when(pl.program_id(0) == 0)
    def _():
        shard = NG * D // NDEV
        rows = shard // len(parts)
        cps = [pltpu.make_async_copy(buf.at[pl.ds(first + k * rows, rows), :],
                                     w_s.at[pl.ds(k * shard + p * rows, rows), :], sem.at[p * NDEV + k])
               for p, (buf, first) in enumerate(parts) for k in range(NDEV)]
        for cp in cps:
            cp.start()
        for cp in cps:
            cp.wait()


def _mix_in(h1, gm, win, comm=None):
    def body(h_ref, g_ref, *rest):
        w_any, (u_ref, z_ref, w_s, sem) = rest[:len(win)], rest[len(win):]
        _load_in_proj([(b, first) for b, (_, first) in zip(w_any, win)], w_s, sem)
        xf = h_ref[...]
        r = lax.rsqrt(jnp.mean(xf * xf, axis=-1, keepdims=True) + EPS)
        ub = (xf * r * g_ref[...]).astype(BF)
        u_ref[...] = ub
        for j in range(NG):
            z_ref[j] = _nt(ub, w_s[j * D:(j + 1) * D, :]).astype(BF)

    row = lambda i: (i, 0)
    return _call(
        body, name="mix_in", grid=(T // TM,), args=[h1, gm] + [b for b, _ in win], comm=comm,
        in_specs=[pl.BlockSpec((TM, D), row), pl.BlockSpec((1, D), lambda i: (0, 0))] + [ANY] * len(win),
        out_shape=[jax.ShapeDtypeStruct((T, D), BF), jax.ShapeDtypeStruct((NG, T, D), BF)],
        out_specs=[pl.BlockSpec((TM, D), row), pl.BlockSpec((NG, TM, D), lambda i: (0, i, 0))],
        scratch_shapes=[pltpu.VMEM((NG * D, D), BF), pltpu.SemaphoreType.DMA((NDEV * len(win),))])


def _shift_up(w, b):
    return w if b == 0 else pltpu.roll(w, w.shape[0] - b, 0)


def _fold8(p):
    red = p[0:8, :]
    for i in range(1, p.shape[0] // 8):
        red = red + p[8 * i:8 * i + 8, :]
    return red


def _dft_constants():
    import numpy as np
    nh = NB // 2
    f, n = np.arange(nh)[:, None], np.arange(NB)[None, :]
    ang = 2.0 * np.pi / NB * f * n
    fc = np.cos(ang)
    fs = np.where(f == 0, (-1.0) ** n, np.sin(ang))
    scale = np.where(f == 0, 1.0, 2.0) / NB
    ic = (scale * np.cos(ang)).T
    isn = np.where(f == 0, (-1.0) ** n / NB, scale * np.sin(ang)).T
    d = (KA - 1 - np.arange(32))[None, :]
    valid = (np.arange(32) < KA)[None, :]
    angk = 2.0 * np.pi / NB * f * d
    kc = np.where(valid, np.cos(angk), 0.0)
    ks = np.where(valid, np.sin(angk), 0.0)
    k2 = np.where(valid, np.where(f == 0, (-1.0) ** d, np.cos(angk)), 0.0)
    rtc = np.where(valid, scale * np.cos(angk), 0.0).T
    rts = np.where(valid, np.where(f == 0, (-1.0) ** d / NB, scale * np.sin(angk)), 0.0).T

    def bf(a):
        return jnp.asarray(a, F32).astype(BF)

    def split(a):
        hi = bf(a)
        return hi, (jnp.asarray(a, F32) - hi.astype(F32)).astype(BF)

    return dict(fc=bf(fc), fs=bf(fs), ic_hi=bf(ic[HB:]), is_hi=bf(isn[HB:]), ic_lo=bf(ic[:HB]), is_lo=bf(isn[:HB]),
                kc=split(kc), ks=split(ks), k2=split(k2), rtc=split(rtc), rts=split(rts))


def _dot3(m_hi, m_lo, x):
    x_hi = x.astype(BF)
    x_lo = (x - x_hi.astype(F32)).astype(BF)
    return _nn(m_hi, x_hi) + _nn(m_hi, x_lo) + _nn(m_lo, x_hi)


def _whole(a):
    return pl.BlockSpec(a.shape, lambda c, t: (0,) * a.ndim)


def _filter_spectrum(cw_ref, tabs, hc, hs, h2):
    w32 = cw_ref[0:32, :]
    for (hi, lo), dst in zip(tabs, (hc, hs, h2)):
        dst[...] = _dot3(hi[...], lo[...], w32)


def _conv_fwd_dft(z, cw, bias, dft, comm=None):
    nt = T // TB
    hb = TB // HB

    def body(z_ref, zh_ref, cw_ref, b_ref, fc_ref, fs_ref, ic_ref, is_ref, kch, kcl, ksh, ksl, k2h, k2l,
             a1_ref, q_ref, aext, ppad, hc, hs, h2):
        first = pl.program_id(1) == 0
        f = lambda ref, j: ref[j].astype(F32)

        @pl.when(first)
        def _():
            _filter_spectrum(cw_ref, ((kch, kcl), (ksh, ksl), (k2h, k2l)), hc, hs, h2)

        aext[0:HB, :] = jnp.where(first, 0.0, f(zh_ref, 0) * _sig(f(zh_ref, 1))).astype(BF)
        aext[HB:, :] = (f(z_ref, 0) * _sig(f(z_ref, 1))).astype(BF)
        ppad[0:8, :] = jnp.where(first, 0.0, f(zh_ref, 3)[HB - 8:HB, :] * f(zh_ref, 4)[HB - 8:HB, :])
        ppad[8:, :] = f(z_ref, 3) * f(z_ref, 4)
        bias_row = b_ref[...]

        for j in range(TB // HB):
            xs = aext[j * HB:j * HB + NB, :]
            xa, xb = _nn(fc_ref[...], xs), _nn(fs_ref[...], xs)
            yc = (hc[...] * xa - hs[...] * xb).astype(BF)
            ys = (h2[...] * xb + hs[...] * xa).astype(BF)
            y = _nn(ic_ref[...], yc) + _nn(is_ref[...], ys)
            a1_ref[j * HB:(j + 1) * HB, :] = (y + bias_row).astype(BF)

        def chunk(r, carry):
            base = pl.multiple_of(r * CHB, CHB)
            pw = ppad[pl.ds(base, CHB + 8), :]
            v = (cw_ref[pl.ds(32, 1), :] * _shift_up(pw, 6)[0:CHB, :]
                 + cw_ref[pl.ds(33, 1), :] * _shift_up(pw, 7)[0:CHB, :]
                 + cw_ref[pl.ds(34, 1), :] * pw[8:8 + CHB, :])
            q_ref[pl.ds(base, CHB), :] = (z_ref[2, pl.ds(base, CHB), :].astype(F32) * v).astype(BF)
            return carry

        lax.fori_loop(0, TB // CHB, chunk, 0)

    blk = pl.BlockSpec((TB, CW), lambda c, t: (t, c))
    tabs = [dft["fc"], dft["fs"], dft["ic_hi"], dft["is_hi"], *dft["kc"], *dft["ks"], *dft["k2"]]
    return _call(
        body, name="conv_fwd", grid=(D // CW, nt), comm=comm, args=[z, z, cw, bias] + tabs,
        in_specs=[pl.BlockSpec((5, TB, CW), lambda c, t: (0, t, c)),
                  pl.BlockSpec((5, HB, CW), lambda c, t: (0, jnp.maximum(t * hb - 1, 0), c)),
                  pl.BlockSpec((40, CW), lambda c, t: (0, c)), pl.BlockSpec((1, CW), lambda c, t: (0, c))]
                 + [_whole(a) for a in tabs],
        out_shape=[jax.ShapeDtypeStruct((T, D), BF), jax.ShapeDtypeStruct((T, D), BF)], out_specs=[blk, blk],
        scratch_shapes=[pltpu.VMEM((TB + HB, CW), BF), pltpu.VMEM((TB + 8, CW), F32)]
                       + [pltpu.VMEM((NB // 2, CW), F32)] * 3)


def _conv_bwd_dft(z, da1, dq, dzg, cw, dft, comm=None):
    nt = T // TB
    hb = TB // HB
    last_h = T // HB - 1

    def body(z_ref, zp_ref, zn_ref, da1_ref, da1n_ref, dq_ref, dqn_ref, dzg_ref, cw_ref,
             fc_ref, fs_ref, ic_ref, is_ref, kch, kcl, ksh, ksl, k2h, k2l, rch, rcl, rsh, rsl,
             dz_ref, dwa_ref, dwb_ref, aext, dyext, ppad, dvpad, hc, hs, h2, rc, rs, nyq, acc_b):
        t = pl.program_id(1)
        first, last = t == 0, t == nt - 1
        f = lambda ref, j: ref[j].astype(F32)

        @pl.when(first)
        def _():
            _filter_spectrum(cw_ref, ((kch, kcl), (ksh, ksl), (k2h, k2l)), hc, hs, h2)
            rc[...] = jnp.zeros_like(rc)
            rs[...] = jnp.zeros_like(rs)
            nyq[...] = jnp.zeros_like(nyq)
            acc_b[...] = jnp.zeros_like(acc_b)

        aext[0:HB, :] = jnp.where(first, 0.0, f(zp_ref, 0) * _sig(f(zp_ref, 1))).astype(BF)
        aext[HB:, :] = (f(z_ref, 0) * _sig(f(z_ref, 1))).astype(BF)
        dyext[0:TB, :] = da1_ref[...]
        dyext[TB:, :] = jnp.where(last, 0.0, da1n_ref[...].astype(F32)).astype(BF)
        ppad[0:8, :] = jnp.where(first, 0.0, f(zp_ref, 3)[HB - 8:HB, :] * f(zp_ref, 4)[HB - 8:HB, :])
        ppad[8:, :] = f(z_ref, 3) * f(z_ref, 4)
        dvpad[0:TB, :] = dq_ref[...].astype(F32) * f(z_ref, 2)
        dvpad[TB:, :] = jnp.where(last, 0.0, dqn_ref[...].astype(F32)[0:8, :] * f(zn_ref, 2)[0:8, :])

        for j in range(TB // HB):
            rows = slice(j * HB, (j + 1) * HB)
            dys = dyext[j * HB:j * HB + NB, :]
            da, db = _nn(fc_ref[...], dys), _nn(fs_ref[...], dys)
            gc = (hc[...] * da + hs[...] * db).astype(BF)
            gs = (h2[...] * db - hs[...] * da).astype(BF)
            da0 = _nn(ic_ref[...], gc) + _nn(is_ref[...], gs)
            z0, z1 = z_ref[0, rows, :].astype(F32), z_ref[1, rows, :].astype(F32)
            s1 = _sig(z1)
            dz_ref[0, rows, :] = (da0 * s1).astype(BF)
            dz_ref[1, rows, :] = (da0 * z0 * (s1 * (1.0 - s1))).astype(BF)
            xs = aext[j * HB:j * HB + NB, :]
            xa, xb = _nn(fc_ref[...], xs), _nn(fs_ref[...], xs)
            dyb = dyext[rows, :]
            pa, pb = _nn(fc_ref[:, HB:NB], dyb), _nn(fs_ref[:, HB:NB], dyb)
            rc[...] += pa * xa + pb * xb
            rs[...] += pb * xa - pa * xb
            nyq[...] += pb[0:8, :] * xb[0:8, :]

        def chunk(r, carry):
            base = pl.multiple_of(r * CHB, CHB)
            rows = pl.ds(base, CHB)
            pw = ppad[pl.ds(base, CHB + 8), :]
            p6 = _shift_up(pw, 6)[0:CHB, :]
            p7 = _shift_up(pw, 7)[0:CHB, :]
            p8 = pw[8:8 + CHB, :]
            wb0, wb1, wb2 = cw_ref[pl.ds(32, 1), :], cw_ref[pl.ds(33, 1), :], cw_ref[pl.ds(34, 1), :]
            v = wb0 * p6 + wb1 * p7 + wb2 * p8
            dz_ref[2, rows, :] = (dq_ref[rows, :].astype(F32) * v).astype(BF)
            dvw = dvpad[pl.ds(base, CHB + 8), :]
            dvc = dvw[0:CHB, :]
            dp = wb2 * dvc + wb1 * _shift_up(dvw, 1)[0:CHB, :] + wb0 * _shift_up(dvw, 2)[0:CHB, :]
            dz_ref[3, rows, :] = (dp * z_ref[4, rows, :].astype(F32)).astype(BF)
            dz_ref[4, rows, :] = (dp * z_ref[3, rows, :].astype(F32)).astype(BF)
            acc_b[0:8, :] += _fold8(dvc * p6)
            acc_b[8:16, :] += _fold8(dvc * p7)
            acc_b[16:24, :] += _fold8(dvc * p8)
            dz_ref[5, rows, :] = dzg_ref[0, rows, :]
            dz_ref[6, rows, :] = dzg_ref[1, rows, :]
            return carry

        lax.fori_loop(0, TB // CHB, chunk, 0)

        @pl.when(last)
        def _():
            row0 = lax.broadcasted_iota(jnp.int32, (NB // 2, CW), 0) == 0
            ny = jnp.broadcast_to(nyq[0:1, :], (NB // 2, CW))
            rcv = jnp.where(row0, rc[...] - ny, rc[...])
            rsv = jnp.where(row0, ny, rs[...])
            dwa_ref[...] = _dot3(rch[...], rcl[...], rcv) + _dot3(rsh[...], rsl[...], rsv)
            for k in range(KB):
                dwb_ref[k:k + 1, :] = jnp.sum(acc_b[8 * k:8 * k + 8, :], axis=0, keepdims=True)
            dwb_ref[KB:8, :] = jnp.zeros((8 - KB, CW), F32)

    blk = lambda c, t: (t, c)
    nxt = lambda c, t: (jnp.minimum((t + 1) * hb, last_h), c)
    tabs = [dft["fc"], dft["fs"], dft["ic_lo"], dft["is_lo"], *dft["kc"], *dft["ks"], *dft["k2"], *dft["rtc"], *dft["rts"]]
    return _call(
        body, name="conv_bwd", grid=(D // CW, nt), comm=comm, args=[z, z, z, da1, da1, dq, dq, dzg, cw] + tabs,
        in_specs=[pl.BlockSpec((5, TB, CW), lambda c, t: (0, t, c)),
                  pl.BlockSpec((5, HB, CW), lambda c, t: (0, jnp.maximum(t * hb - 1, 0), c)),
                  pl.BlockSpec((5, HB, CW), lambda c, t: (0, jnp.minimum((t + 1) * hb, last_h), c)),
                  pl.BlockSpec((TB, CW), blk), pl.BlockSpec((HB, CW), nxt),
                  pl.BlockSpec((TB, CW), blk), pl.BlockSpec((HB, CW), nxt),
                  pl.BlockSpec((2, TB, CW), lambda c, t: (0, t, c)),
                  pl.BlockSpec((40, CW), lambda c, t: (0, c))]
                 + [_whole(a) for a in tabs],
        out_shape=[jax.ShapeDtypeStruct((NG, T, D), BF), jax.ShapeDtypeStruct((32, D), F32),
                   jax.ShapeDtypeStruct((8, D), F32)],
        out_specs=[pl.BlockSpec((NG, TB, CW), lambda c, t: (0, t, c)),
                   pl.BlockSpec((32, CW), lambda c, t: (0, c)), pl.BlockSpec((8, CW), lambda c, t: (0, c))],
        scratch_shapes=[pltpu.VMEM((TB + HB, CW), BF), pltpu.VMEM((TB + HB, CW), BF),
                        pltpu.VMEM((TB + 8, CW), F32), pltpu.VMEM((TB + 8, CW), F32)]
                       + [pltpu.VMEM((NB // 2, CW), F32)] * 5 + [pltpu.VMEM((8, CW), F32), pltpu.VMEM((24, CW), F32)])


def _layernorm_silu(a1, lng, lnb):
    mu = jnp.mean(a1, axis=-1, keepdims=True)
    xc = a1 - mu
    rs = lax.rsqrt(jnp.mean(xc * xc, axis=-1, keepdims=True) + EPS)
    xh = xc * rs
    a2 = xh * lng + lnb
    sg = _sig(a2)
    return xh, rs, a2, sg


def _square_specs(blocks):
    return [pl.BlockSpec((D, D), lambda i, b=b: (b, 0)) for b in blocks]


def _mix_out(a1, q, z, h1, lng, lnb, wsq, comm=None):
    def body(a1_ref, q_ref, ga_ref, gb_ref, h_ref, lng_ref, lnb_ref, wa_ref, wb_ref, wo_ref, h2_ref, ya_ref, yb_ref):
        _, _, a2, sg = _layernorm_silu(a1_ref[...].astype(F32), lng_ref[...], lnb_ref[...])
        ya = _nn((a2 * sg).astype(BF), wa_ref[...])
        yb = _nn(q_ref[...], wb_ref[...])
        ya_ref[...] = ya.astype(BF)
        yb_ref[...] = yb.astype(BF)
        m = _sig(ga_ref[...].astype(F32)) * ya + _sig(gb_ref[...].astype(F32)) * yb
        h2_ref[...] = h_ref[...] + _nn(m.astype(BF), wo_ref[...])

    row = lambda i: (i, 0)
    vec = pl.BlockSpec((1, D), lambda i: (0, 0))
    return _call(
        body, name="mix_out", grid=(T // TM,), args=[a1, q, z, z, h1, lng, lnb, wsq, wsq, wsq], comm=comm,
        in_specs=[pl.BlockSpec((TM, D), row), pl.BlockSpec((TM, D), row),
                  pl.BlockSpec((None, TM, D), lambda i: (5, i, 0)), pl.BlockSpec((None, TM, D), lambda i: (6, i, 0)),
                  pl.BlockSpec((TM, D), row), vec, vec] + _square_specs((0, 1, 2)),
        out_shape=[jax.ShapeDtypeStruct((T, D), F32), jax.ShapeDtypeStruct((T, D), BF), jax.ShapeDtypeStruct((T, D), BF)],
        out_specs=[pl.BlockSpec((TM, D), row)] * 3)


def _rmsnorm_bwd(xf, g, dn):
    r = lax.rsqrt(jnp.mean(xf * xf, axis=-1, keepdims=True) + EPS)
    xr = xf * r
    gdn = dn * g
    dx = r * gdn - xr * (r * jnp.mean(gdn * xr, axis=-1, keepdims=True))
    return dx, jnp.sum(dn * xr, axis=0, keepdims=True)


def _ffn_bwd_hidden(dh, gg, uu, wbuf, off, name, comm=None):
    nf = F // FC

    def body(dh_ref, gg_ref, uu_ref, b0, dgu_ref, wd_s, sem):
        _load_ffn_weights((b0,), (off,), (wd_s,), sem)
        dhb = dh_ref[...]
        for c in range(nf):
            sl = slice(c * FC, (c + 1) * FC)
            da = _nt(dhb, wd_s[sl, :]).astype(BF)
            gb, ub = gg_ref[:, sl], uu_ref[:, sl]
            sg = _sig(gb)
            dgu_ref[0, :, sl] = (da * ub) * (sg * (1.0 + gb * (1.0 - sg)))
            dgu_ref[0, :, F + c * FC:F + (c + 1) * FC] = da * (gb * sg)

    row = lambda i: (i, 0)
    return _call(
        body, name=name, grid=(T // TM,), args=[dh, gg, uu, wbuf], comm=comm,
        in_specs=[pl.BlockSpec((TM, D), row), pl.BlockSpec((TM, F), row), pl.BlockSpec((TM, F), row), ANY],
        out_shape=[jax.ShapeDtypeStruct((1, T, 2 * F), BF)],
        out_specs=[pl.BlockSpec((1, TM, 2 * F), lambda i: (0, i, 0))],
        scratch_shapes=[pltpu.VMEM((F, D), BF), pltpu.SemaphoreType.DMA((1,))])


def _ffn_bwd_input(dgu, dh, x, g, wbufs, offs, name, comm=None):
    def body(dgu_ref, dh_ref, x_ref, g_ref, b0, b1, dx_ref, s_ref, w_s, sem):
        _load_ffn_weights((b0, b1), offs, (w_s.at[pl.ds(0, F), :], w_s.at[pl.ds(F, F), :]), sem)

        @pl.when(pl.program_id(0) == 0)
        def _():
            s_ref[...] = jnp.zeros_like(s_ref)

        dn = _nn(dgu_ref[0], w_s[...])
        dxn, dg = _rmsnorm_bwd(x_ref[...], g_ref[...], dn)
        dx_ref[...] = dh_ref[...] + dxn
        s_ref[0:1, :] += dg

    row = lambda i: (i, 0)
    return _call(
        body, name=name, grid=(T // TM,), args=[dgu, dh, x, g, *wbufs], comm=comm,
        in_specs=[pl.BlockSpec((1, TM, 2 * F), lambda i: (0, i, 0)), pl.BlockSpec((TM, D), row),
                  pl.BlockSpec((TM, D), row), pl.BlockSpec((1, D), lambda i: (0, 0)), ANY, ANY],
        out_shape=[jax.ShapeDtypeStruct((T, D), F32), jax.ShapeDtypeStruct((8, D), F32)],
        out_specs=[pl.BlockSpec((TM, D), row), pl.BlockSpec((8, D), lambda i: (0, 0))],
        scratch_shapes=[pltpu.VMEM((2 * F, D), BF), pltpu.SemaphoreType.DMA((2,))])


def _tn_matmul(lhs, rhs, tr, name, comm=None):
    ng, _, cdim = lhs.shape
    nc, nk = cdim // tr, T // TK
    if rhs.ndim == 2:
        r_spec = pl.BlockSpec((TK, D), lambda g, c, k: (k, 0))
    else:
        r_spec = pl.BlockSpec((None, TK, D), lambda g, c, k: (g, k, 0))

    def body(l_ref, r_ref, o_ref, acc):
        k = pl.program_id(2)

        @pl.when(k == 0)
        def _():
            acc[...] = jnp.zeros_like(acc)

        acc[...] += _tn(l_ref[...], r_ref[...])

        @pl.when(k == nk - 1)
        def _():
            o_ref[...] = acc[...].astype(BF)

    return _call(
        body, name=name, grid=(ng, nc, nk), args=[lhs, rhs], comm=comm,
        in_specs=[pl.BlockSpec((None, TK, tr), lambda g, c, k: (g, k, c)), r_spec],
        out_shape=[jax.ShapeDtypeStruct((ng * cdim, D), BF)],
        out_specs=[pl.BlockSpec((tr, D), lambda g, c, k: (g * nc + c, 0))],
        scratch_shapes=[pltpu.VMEM((tr, D), F32)])


def _mix_out_bwd(dh2, ya, yb, z, a1, q, lng, lnb, wsq, comm=None):
    def body(dh_ref, ya_ref, yb_ref, ga_ref, gb_ref, a1_ref, q_ref, lng_ref, lnb_ref, wa_ref, wb_ref, wo_ref,
             dzg_ref, da1_ref, dq_ref, l_ref, r_ref, s_ref):
        @pl.when(pl.program_id(0) == 0)
        def _():
            s_ref[...] = jnp.zeros_like(s_ref)

        dhb = dh_ref[...].astype(BF)
        dm = _nt(dhb, wo_ref[...]).astype(BF)
        ya, yb = ya_ref[...], yb_ref[...]
        sa, sb = _sig(ga_ref[...]), _sig(gb_ref[...])
        l_ref[0] = sa * ya + sb * yb
        l_ref[2] = q_ref[...]
        dzg_ref[0] = (dm * ya) * (sa * (1.0 - sa))
        dzg_ref[1] = (dm * yb) * (sb * (1.0 - sb))
        dya = dm * sa
        dyb = dm * sb
        r_ref[0] = dhb
        r_ref[1] = dya
        r_ref[2] = dyb
        dq_ref[...] = _nt(dyb, wb_ref[...]).astype(BF)
        da3 = _nt(dya, wa_ref[...])
        lng = lng_ref[...]
        xh, rs, a2, sg = _layernorm_silu(a1_ref[...].astype(F32), lng, lnb_ref[...])
        l_ref[1] = (a2 * sg).astype(BF)
        da2 = da3 * (sg * (1.0 + a2 * (1.0 - sg)))
        s_ref[0:1, :] += jnp.sum(da2 * xh, axis=0, keepdims=True)
        s_ref[1:2, :] += jnp.sum(da2, axis=0, keepdims=True)
        dxh = da2 * lng
        da1 = rs * (dxh - jnp.mean(dxh, axis=-1, keepdims=True) - xh * jnp.mean(dxh * xh, axis=-1, keepdims=True))
        da1_ref[...] = da1.astype(BF)
        s_ref[2:3, :] += jnp.sum(da1, axis=0, keepdims=True)

    row = lambda i: (i, 0)
    row3 = lambda i: (0, i, 0)
    vec = pl.BlockSpec((1, D), lambda i: (0, 0))
    return _call(
        body, name="mix_out_bwd", grid=(T // TM,), args=[dh2, ya, yb, z, z, a1, q, lng, lnb, wsq, wsq, wsq], comm=comm,
        in_specs=[pl.BlockSpec((TM, D), row), pl.BlockSpec((TM, D), row), pl.BlockSpec((TM, D), row),
                  pl.BlockSpec((None, TM, D), lambda i: (5, i, 0)), pl.BlockSpec((None, TM, D), lambda i: (6, i, 0)),
                  pl.BlockSpec((TM, D), row), pl.BlockSpec((TM, D), row), vec, vec] + _square_specs((0, 1, 2)),
        out_shape=[jax.ShapeDtypeStruct((2, T, D), BF), jax.ShapeDtypeStruct((T, D), BF),
                   jax.ShapeDtypeStruct((T, D), BF), jax.ShapeDtypeStruct((3, T, D), BF),
                   jax.ShapeDtypeStruct((3, T, D), BF), jax.ShapeDtypeStruct((8, D), F32)],
        out_specs=[pl.BlockSpec((2, TM, D), row3), pl.BlockSpec((TM, D), row), pl.BlockSpec((TM, D), row),
                   pl.BlockSpec((3, TM, D), row3), pl.BlockSpec((3, TM, D), row3), pl.BlockSpec((8, D), lambda i: (0, 0))])


def _mix_in_bwd(dz, dh2, h1, gm, win, comm=None):
    def body(dz_ref, dh_ref, h_ref, g_ref, *rest):
        w_any, (o_ref, ob_ref, s_ref, w_s, sem) = rest[:len(win)], rest[len(win):]
        _load_in_proj([(b, first) for b, (_, first) in zip(w_any, win)], w_s, sem)

        @pl.when(pl.program_id(0) == 0)
        def _():
            s_ref[...] = jnp.zeros_like(s_ref)

        du = _nn(dz_ref[0], w_s[0:D, :])
        for j in range(1, NG):
            du = du + _nn(dz_ref[j], w_s[j * D:(j + 1) * D, :])
        dx, dg = _rmsnorm_bwd(h_ref[...], g_ref[...], du)
        dh1 = dh_ref[...] + dx
        o_ref[...] = dh1
        ob_ref[...] = (0.5 * dh1).astype(BF)
        s_ref[0:1, :] += dg

    row = lambda i: (i, 0)
    return _call(
        body, name="mix_in_bwd", grid=(T // TM,), args=[dz, dh2, h1, gm] + [b for b, _ in win], comm=comm,
        in_specs=[pl.BlockSpec((NG, TM, D), lambda i: (0, i, 0)), pl.BlockSpec((TM, D), row),
                  pl.BlockSpec((TM, D), row), pl.BlockSpec((1, D), lambda i: (0, 0))] + [ANY] * len(win),
        out_shape=[jax.ShapeDtypeStruct((T, D), F32), jax.ShapeDtypeStruct((T, D), BF), jax.ShapeDtypeStruct((8, D), F32)],
        out_specs=[pl.BlockSpec((TM, D), row), pl.BlockSpec((TM, D), row), pl.BlockSpec((8, D), lambda i: (0, 0))],
        scratch_shapes=[pltpu.VMEM((NG * D, D), BF), pltpu.SemaphoreType.DMA((NDEV * len(win),))])


def _row_tile(n, want, mult):
    for t in range(min(want, n), 0, -1):
        if n % t == 0 and t % mult == 0:
            return t
    return n


def _pack_small(s_ffn1, s_in, s_mix, s_ffn2, s_final, dwa, dwb):
    def body(f1, mi, mo, f2, fl, wa_ref, wb_ref, v_ref, k_ref):
        for dst, (ref, row) in enumerate(((f1, 0), (mi, 0), (mo, 0), (mo, 1), (mo, 2), (f2, 0), (fl, 0), (fl, 1))):
            v_ref[dst:dst + 1, :] = ref[row:row + 1, :]
        for k in range(NDEV):
            k_ref[k, 0:32, :] = wa_ref[:, k * LANE:(k + 1) * LANE]
            k_ref[k, 32:40, :] = wb_ref[:, k * LANE:(k + 1) * LANE]

    return pl.pallas_call(
        body, name="pack_small",
        out_shape=(jax.ShapeDtypeStruct((8, D), F32), jax.ShapeDtypeStruct((NDEV, 40, LANE), F32)),
    )(s_ffn1, s_in, s_mix, s_ffn2, s_final, dwa, dwb)


def _adam_update(g, w, m, v):
    m2 = ADAM_B1 * m + (1.0 - ADAM_B1) * g
    v2 = ADAM_B2 * v + (1.0 - ADAM_B2) * (g * g)
    c1 = 1.0 - ADAM_B1 ** ADAM_STEP
    c2 = 1.0 - ADAM_B2 ** ADAM_STEP
    return -ADAM_LR * ((m2 / c1) / (jnp.sqrt(v2 / c2) + ADAM_EPS) + ADAM_WD * w), m2, v2


def _adam_small(vecs, convs, vec_params, tap_params):
    nv, nt = len(vec_params), len(tap_params)

    def body(*refs):
        v_ref, k_ref = refs[:2]
        p_refs = refs[2:2 + 3 * (nv + nt)]
        l_ref = refs[2 + 3 * (nv + nt)]
        o_refs = refs[3 + 3 * (nv + nt):]
        s, c = v_ref[0], k_ref[0]
        for k in range(1, NDEV):
            s = s + v_ref[k]
            c = c + k_ref[k]
        l_ref[...] = jnp.sum(s[7:8, :], axis=-1, keepdims=True)
        for i in range(nv):
            w_ref, m_ref, u_ref = p_refs[3 * i: 3 * i + 3]
            g_ref, d_ref, m2_ref, u2_ref = o_refs[4 * i: 4 * i + 4]
            g = s[i:i + 1, :]
            g_ref[...] = g
            d_ref[...], m2_ref[...], u2_ref[...] = _adam_update(g, w_ref[...], m_ref[...], u_ref[...])
        for i in range(nt):
            w_ref, m_ref, u_ref = p_refs[3 * (nv + i): 3 * (nv + i) + 3]
            g_ref, d_ref, m2_ref, u2_ref = o_refs[4 * (nv + i): 4 * (nv + i) + 4]
            first = tap_params[i][0]
            for k in range(w_ref.shape[0]):
                g = c[first + k:first + k + 1, :]
                g_ref[k] = g
                d_ref[k], m2_ref[k], u2_ref[k] = _adam_update(g, w_ref[k], m_ref[k], u_ref[k])

    params = [a for p in vec_params for a in p] + [a for p in tap_params for a in p[1:]]
    out_shape = [jax.ShapeDtypeStruct((1, 1), F32)]
    for p in list(vec_params) + [p[1:] for p in tap_params]:
        out_shape += [jax.ShapeDtypeStruct(p[0].shape, F32)] * 4
    outs = pl.pallas_call(body, name="adam_small", out_shape=tuple(out_shape))(vecs, convs, *params)
    groups = [tuple(outs[1 + 4 * i: 5 + 4 * i]) for i in range(nv + nt)]
    return outs[0], groups[:nv], groups[nv:]


def _adam_in_proj(parts, w, m, v):
    rows = w.shape[1]
    tr = _row_tile(D, 256, LANE)

    def body(*refs):
        p_refs = refs[:len(parts)]
        w_ref, m_ref, v_ref, g_ref, d_ref, m2_ref, v2_ref = refs[len(parts):]
        sums = []
        for p in p_refs:
            s = p[0].astype(F32)
            for k in range(1, p.shape[0]):
                s = s + p[k].astype(F32)
            sums.append(s)
        g = jnp.concatenate(sums, axis=0).T
        g_ref[...] = g
        d_ref[...], m2_ref[...], v2_ref[...] = _adam_update(g, w_ref[...], m_ref[...], v_ref[...])

    spec = pl.BlockSpec((tr, rows), lambda i: (i, 0))
    return _call(body, name="adam_in", grid=(D // tr,), args=list(parts) + [w, m, v],
                 in_specs=[pl.BlockSpec((p.shape[0], p.shape[1], tr), lambda i: (0, 0, i)) for p in parts] + [spec] * 3,
                 out_shape=[jax.ShapeDtypeStruct((D, rows), F32)] * 4, out_specs=[spec] * 4)


def _adam(gs, ws, ms, vs, name, comm=None):
    n = len(gs)
    rows, cols = ws[0].shape
    tr = _row_tile(rows, 256, 16)
    summed = [isinstance(g, tuple) for g in gs]

    def body(*refs):
        for i in range(n):
            g_in, w, m, v = refs[4 * i], refs[4 * i + 1][...], refs[4 * i + 2][...], refs[4 * i + 3][...]
            g_ref, d_ref, m_ref, v_ref = refs[4 * n + 4 * i: 4 * n + 4 * i + 4]
            if summed[i]:
                g = g_in[0].astype(F32)
                for k in range(1, g_in.shape[0]):
                    g = g + g_in[k].astype(F32)
            else:
                g = g_in[...]
            g_ref[...] = g
            d_ref[...], m_ref[...], v_ref[...] = _adam_update(g, w, m, v)

    spec = pl.BlockSpec((tr, cols), lambda i: (i, 0))
    args, in_specs = [], []
    for i in range(n):
        if summed[i]:
            slots, first = gs[i]
            args.append(slots)
            in_specs.append(pl.BlockSpec((slots.shape[0], tr, cols), lambda i, b=first // tr: (0, b + i, 0)))
        else:
            args.append(gs[i])
            in_specs.append(spec)
        args += [ws[i], ms[i], vs[i]]
        in_specs += [spec] * 3
    outs = _call(body, name=name, grid=(rows // tr,), args=args, comm=comm, in_specs=in_specs,
                 out_shape=[jax.ShapeDtypeStruct((rows, cols), F32)] * (4 * n), out_specs=[spec] * (4 * n))
    return [tuple(outs[4 * i: 4 * i + 4]) for i in range(n)], outs[4 * n:]


def kernel(x, ffn1_norm, ffn1_w_gate, ffn1_w_up, ffn1_w_down, mix_norm, w_in, a_dw_w, a_dw_b, a_ln_g, a_ln_b, a_w_out, b_conv_w, b_w_out, w_o, ffn2_norm, ffn2_w_gate, ffn2_w_up, ffn2_w_down, final_norm, loss_target, m_ffn1_norm, m_ffn1_w_gate, m_ffn1_w_up, m_ffn1_w_down, m_mix_norm, m_w_in, m_a_dw_w, m_a_dw_b, m_a_ln_g, m_a_ln_b, m_a_w_out, m_b_conv_w, m_b_w_out, m_w_o, m_ffn2_norm, m_ffn2_w_gate, m_ffn2_w_up, m_ffn2_w_down, m_final_norm, v_ffn1_norm, v_ffn1_w_gate, v_ffn1_w_up, v_ffn1_w_down, v_mix_norm, v_w_in, v_a_dw_w, v_a_dw_b, v_a_ln_g, v_a_ln_b, v_a_w_out, v_b_conv_w, v_b_w_out, v_w_o, v_ffn2_norm, v_ffn2_w_gate, v_ffn2_w_up, v_ffn2_w_down, v_final_norm):
    names = ("ffn1_norm", "ffn1_w_gate", "ffn1_w_up", "ffn1_w_down", "mix_norm", "w_in", "a_dw_w", "a_dw_b",
             "a_ln_g", "a_ln_b", "a_w_out", "b_conv_w", "b_w_out", "w_o", "ffn2_norm", "ffn2_w_gate", "ffn2_w_up",
             "ffn2_w_down", "final_norm")
    w = dict(ffn1_norm=ffn1_norm, ffn1_w_gate=ffn1_w_gate, ffn1_w_up=ffn1_w_up, ffn1_w_down=ffn1_w_down,
             mix_norm=mix_norm, w_in=w_in, a_dw_w=a_dw_w, a_dw_b=a_dw_b, a_ln_g=a_ln_g, a_ln_b=a_ln_b,
             a_w_out=a_w_out, b_conv_w=b_conv_w, b_w_out=b_w_out, w_o=w_o, ffn2_norm=ffn2_norm,
             ffn2_w_gate=ffn2_w_gate, ffn2_w_up=ffn2_w_up, ffn2_w_down=ffn2_w_down, final_norm=final_norm)
    m = dict(ffn1_norm=m_ffn1_norm, ffn1_w_gate=m_ffn1_w_gate, ffn1_w_up=m_ffn1_w_up, ffn1_w_down=m_ffn1_w_down,
             mix_norm=m_mix_norm, w_in=m_w_in, a_dw_w=m_a_dw_w, a_dw_b=m_a_dw_b, a_ln_g=m_a_ln_g, a_ln_b=m_a_ln_b,
             a_w_out=m_a_w_out, b_conv_w=m_b_conv_w, b_w_out=m_b_w_out, w_o=m_w_o, ffn2_norm=m_ffn2_norm,
             ffn2_w_gate=m_ffn2_w_gate, ffn2_w_up=m_ffn2_w_up, ffn2_w_down=m_ffn2_w_down, final_norm=m_final_norm)
    v = dict(ffn1_norm=v_ffn1_norm, ffn1_w_gate=v_ffn1_w_gate, ffn1_w_up=v_ffn1_w_up, ffn1_w_down=v_ffn1_w_down,
             mix_norm=v_mix_norm, w_in=v_w_in, a_dw_w=v_a_dw_w, a_dw_b=v_a_dw_b, a_ln_g=v_a_ln_g, a_ln_b=v_a_ln_b,
             a_w_out=v_a_w_out, b_conv_w=v_b_conv_w, b_w_out=v_b_w_out, w_o=v_w_o, ffn2_norm=v_ffn2_norm,
             ffn2_w_gate=v_ffn2_w_gate, ffn2_w_up=v_ffn2_w_up, ffn2_w_down=v_ffn2_w_down, final_norm=v_final_norm)
    flat = _pack_weights(dict(wg1=ffn1_w_gate[0].T, wu1=ffn1_w_up[0].T, wd1=ffn1_w_down[0], wg2=ffn2_w_gate[0].T,
                              wu2=ffn2_w_up[0].T, wd2=ffn2_w_down[0], win=w_in[0], wa=a_w_out[0], wb=b_w_out[0],
                              wo=w_o[0]))
    cw_shard = jnp.concatenate([a_dw_w[0], jnp.zeros((1, LANE), F32), b_conv_w[0], jnp.zeros((5, LANE), F32)], axis=0)

    x2, tgt = x[0], loss_target[0]
    st_a, st_b, st_b2 = ("wg1", "wu1"), ("wd1", "win/0/2"), ("win/1/2",)
    st_c, st_d, st_e = ("wa", "wb", "wo", "wg2"), ("wu2",), ("wd2",)

    buf_a, cw = _run_comm(_join(_ag_comm(st_a, flat), _direct_comm(cw_shard, False)), "ag_ffn1")
    n1, gg1, uu1, act1, buf_b = _ffn_gate_up(x2, ffn1_norm, (buf_a, buf_a), (0, F), "ffn1_gate_up", _ag_comm(st_b, flat))
    h1, buf_b2 = _ffn_down(x2, act1, buf_b, 0, "ffn1_down", _ag_comm(st_b2, flat))
    win = ((buf_b, F), (buf_b2, 0))
    u, z, buf_c = _mix_in(h1, mix_norm, win, _ag_comm(st_c, flat))
    dft = _dft_constants()
    cw = jnp.transpose(cw, (1, 0, 2)).reshape(40, D)
    a1, q, buf_d = _conv_fwd_dft(z, cw, a_dw_b, dft, _ag_comm(st_d, flat))
    h2, ya, yb, buf_e = _mix_out(a1, q, z, h1, a_ln_g, a_ln_b, buf_c, _ag_comm(st_e, flat))
    ffn2_bufs, ffn2_offs = (buf_c, buf_d, buf_e), (3 * D, 0, 0)
    dh3, dhb3, s_final, n2, gg2, uu2, act2 = _ffn_fwd(h2, ffn2_norm, ffn2_bufs, ffn2_offs, "ffn2_fwd",
                                          final=(final_norm.reshape(1, D), tgt))

    tr_f = F // 2 if (F // 2) % LANE == 0 else F
    def pair(stage, src):
        return _rs_pair_comm(stage, src)

    def chip(stage, src, pair_buf, tag):
        return _rs_chip_comm(_pair_add(stage, src, pair_buf, "pair_add_" + tag))

    (dgu2,) = _ffn_bwd_hidden(dhb3, gg2, uu2, buf_e, 0, "ffn2_bwd_h")
    (gu2,) = _tn_matmul(dgu2, n2, tr_f, "dw_gu2")
    s2a, src2a = ("wg2", "wu2"), dict(wg2=(gu2, 0), wu2=(gu2, F))
    gd2, pair2a = _tn_matmul(act2, dhb3, tr_f, "dw_d2", pair(s2a, src2a))
    s2b, src2b = ("wd2",), dict(wd2=(gd2, 0))
    dh2, s_ffn2, pair2b = _ffn_bwd_input(dgu2, dh3, h2, ffn2_norm, (buf_c, buf_d), (3 * D, 0), "ffn2_bwd_x",
                                         pair(s2b, src2b))
    dzg, da1, dq, lsq, rsq, s_mix, recv2b = _mix_out_bwd(dh2, ya, yb, z, a1, q, a_ln_g, a_ln_b, buf_c,
                                                          chip(s2b, src2b, pair2b, "2b"))
    (gsq,) = _tn_matmul(lsq, rsq, D, "dw_square")
    ssq, srcsq = ("wa", "wb", "wo"), dict(wa=(gsq, D), wb=(gsq, 2 * D), wo=(gsq, 0))
    dz, dwa, dwb, recv2a, pairsq = _conv_bwd_dft(z, da1, dq, dzg, cw, dft,
                                                 _join(chip(s2a, src2a, pair2a, "2a"), pair(ssq, srcsq)))
    gin, recvsq = _tn_matmul(dz, u, D, "dw_in", chip(ssq, srcsq, pairsq, "sq"))
    sin_a, sin_b, srcin = ("win/0/2",), ("win/1/2",), {"win/0/2": (gin, 0), "win/1/2": (gin, 0)}
    dh1, dhb1, s_in, pairin_a, pairin_b = _mix_in_bwd(dz, dh2, h1, mix_norm, win,
                                                _join(pair(sin_a, srcin), pair(sin_b, srcin)))
    dgu1, recvin_a = _ffn_bwd_hidden(dhb1, gg1, uu1, buf_b, 0, "ffn1_bwd_h",
                                           chip(sin_a, srcin, pairin_a, "in_a"))
    gu1, recvin_b = _tn_matmul(dgu1, n1, tr_f, "dw_gu1", chip(sin_b, srcin, pairin_b, "in_b"))
    s1a, src1a = ("wg1", "wu1"), dict(wg1=(gu1, 0), wu1=(gu1, F))
    gd1, pair1a = _tn_matmul(act1, dhb1, tr_f, "dw_d1", pair(s1a, src1a))
    s1b, src1b = ("wd1",), dict(wd1=(gd1, 0))
    dx, s_ffn1, recv1a, pair1b = _ffn_bwd_input(dgu1, dh1, x2, ffn1_norm, (buf_a, buf_a), (0, F), "ffn1_bwd_x",
                                                _join(chip(s1a, src1a, pair1a, "1a"), pair(s1b, src1b)))

    vec8, convk = _pack_small(s_ffn1, s_in, s_mix, s_ffn2, s_final, dwa, dwb)
    recv1b, vec_all, conv_all = _run_comm(
        _join(chip(s1b, src1b, pair1b, "1b"), _join(_direct_comm(vec8, False), _direct_comm(convk, True))), "xchg_tail")

    fs = F // NDEV
    g = dict(ffn1_w_gate=(recv1a, 0), ffn1_w_up=(recv1a, fs), ffn2_w_gate=(recv2a, 0), ffn2_w_up=(recv2a, fs),
             ffn1_w_down=(recv1b, 0),
             ffn2_w_down=(recv2b, 0), a_w_out=(recvsq, 0), b_w_out=(recvsq, D // NDEV), w_o=(recvsq, 2 * (D // NDEV)))
    grad, upd = {}, {}

    def run(group, name, as2d=lambda a: a[0], back=lambda a, n: a.reshape(w[n].shape)):
        res, _ = _adam([g[n] for n in group], [as2d(w[n]) for n in group], [as2d(m[n]) for n in group],
                       [as2d(v[n]) for n in group], name)
        for n, r in zip(group, res):
            grad[n], upd[n] = back(r[0], n), tuple(back(a, n) for a in r[1:])

    run(("ffn1_w_gate", "ffn1_w_up", "ffn2_w_gate", "ffn2_w_up"), "adam_gate_up",
        as2d=lambda a: a[0].T, back=lambda a, n: a.T[None])
    run(("ffn1_w_down", "ffn2_w_down"), "adam_down")
    r_in = _adam_in_proj([recvin_a, recvin_b], w_in[0], m_w_in[0], v_w_in[0])
    grad["w_in"], upd["w_in"] = r_in[0][None], tuple(a[None] for a in r_in[1:])
    run(("a_w_out", "b_w_out", "w_o"), "adam_square")
    vec_names = ("ffn1_norm", "mix_norm", "a_ln_g", "a_ln_b", "a_dw_b", "ffn2_norm", "final_norm")
    tap_names, tap_rows = ("a_dw_w", "b_conv_w"), (0, 32)
    taps = lambda a: jnp.transpose(a, (1, 0, 2))
    loss, vec_res, tap_res = _adam_small(
        vec_all, conv_all, [tuple(t[n].reshape(1, D) for t in (w, m, v)) for n in vec_names],
        [(r,) + tuple(taps(t[n]) for t in (w, m, v)) for n, r in zip(tap_names, tap_rows)])
    for n, r in zip(vec_names, vec_res):
        grad[n], upd[n] = r[0].reshape(w[n].shape), tuple(a.reshape(w[n].shape) for a in r[1:])
    for n, r in zip(tap_names, tap_res):
        grad[n], upd[n] = taps(r[0]), tuple(taps(a) for a in r[1:])

    return (loss.reshape(()), dx.reshape(x.shape), *[grad[n] for n in names], *[upd[n][0] for n in names],
            *[upd[n][1] for n in names], *[upd[n][2] for n in names])
```

```python
import jax
import jax.numpy as jnp
from jax import lax
from jax.experimental import pallas as pl
from jax.experimental.pallas import tpu as pltpu

T = 4096
D = 1024
F = 2816
NG = 7
NDEV = 8
NCHIP = 4
KA, KB = 31, 3
EPS = 1e-6
ADAM_LR, ADAM_B1, ADAM_B2, ADAM_EPS, ADAM_WD, ADAM_STEP = 0.001, 0.9, 0.999, 1e-08, 0.01, 10

TM = 512
FC = 256
TB = 1024
NB = 256
HB = NB // 2
CW = 256
CHB = 64
LANE = 128
TK = 2048
VMEM_LIMIT = 56 * 1024 * 1024

BF = jnp.bfloat16
F32 = jnp.float32
MESH = pl.DeviceIdType.MESH
ANY = pl.BlockSpec(memory_space=pl.ANY)

ORDER = ("wg1", "wu1", "wd1", "wg2", "wu2", "wd2", "win", "wa", "wb", "wo")


class _Layout:
    def __init__(self):
        fs, dis, ds = F // NDEV, NG * D // NDEV, D // NDEV
        self.rows = dict(wg1=fs, wu1=fs, wd1=fs, wg2=fs, wu2=fs, wd2=fs, win=dis, wa=ds, wb=ds, wo=ds)
        self.fl, off = {}, 0
        for n in ORDER:
            self.fl[n] = off
            off += self.rows[n]
        self.RT = off


class _Stage:
    def __init__(self, names):
        lay = _Layout()
        self.names = names
        self.rows, self.full, self.sub, self.fl = {}, {}, {}, {}
        for n in names:
            base, i, k = (n.split("/") + ["0", "1"])[:3]
            self.full[n] = lay.rows[base]
            self.rows[n] = lay.rows[base] // int(k)
            self.sub[n] = int(i) * self.rows[n]
            self.fl[n] = lay.fl[base] + self.sub[n]
        self.off, self.wc, o, w = {}, {}, 0, 0
        for n in names:
            self.off[n], self.wc[n] = o, w
            o += self.rows[n]
            w += NDEV * self.rows[n]
        self.R, self.W = o, w

    def grad_row(self, n, first, dev_lin):
        return first + dev_lin * self.full[n] + self.sub[n]


def _nt(a, b):
    return lax.dot_general(a, b, (((1,), (1,)), ((), ())), preferred_element_type=F32)


def _nn(a, b):
    return lax.dot_general(a, b, (((1,), (0,)), ((), ())), preferred_element_type=F32)


def _tn(a, b):
    return lax.dot_general(a, b, (((0,), (0,)), ((), ())), preferred_element_type=F32)


def _sig(x):
    return 1.0 / (1.0 + jnp.exp(-x))


def _position():
    return lax.axis_index("x"), lax.axis_index("y"), lax.axis_index("c")


def _peer(pos, j):
    x, y, c = pos
    return (1 - x if j & 4 else x, 1 - y if j & 2 else y, 1 - c if j & 1 else c)


def _lin(pos):
    return 4 * pos[0] + 2 * pos[1] + pos[2]


def _chip(pos):
    return 2 * pos[0] + pos[1]


class _Comm:
    def __init__(self, inputs, out_shapes, scratch, start, finish, middle=None):
        self.inputs, self.out_shapes, self.scratch = inputs, out_shapes, scratch
        self.start, self.finish, self.middle = start, finish, middle


def _call(body, *, name, grid, args, in_specs, out_shape, out_specs, scratch_shapes=(), comm=None,
          num_scalar_prefetch=0):
    in_specs, out_shape, out_specs, scratch_shapes = list(in_specs), list(out_shape), list(out_specs), list(scratch_shapes)
    n_in, n_out, n_scr = len(in_specs), len(out_shape), len(scratch_shapes)
    sp = num_scalar_prefetch
    if comm is None:
        kernel_fn = lambda *refs: body(*refs)
        c_in = c_out = c_scr = 0
    else:
        c_in, c_out, c_scr = len(comm.inputs), len(comm.out_shapes), len(comm.scratch)

        def kernel_fn(*refs):
            pre, refs = refs[:sp], refs[sp:]
            ins, cins = refs[:n_in], refs[n_in:n_in + c_in]
            o0 = n_in + c_in
            outs, couts = refs[o0:o0 + n_out], refs[o0 + n_out:o0 + n_out + c_out]
            s0 = o0 + n_out + c_out
            scr, cscr = refs[s0:s0 + n_scr], refs[s0 + n_scr:]
            step, steps = pl.program_id(0), grid[0]
            for a in range(1, len(grid)):
                step, steps = step * grid[a] + pl.program_id(a), steps * grid[a]
            first, last = step == 0, step == steps - 1

            @pl.when(first)
            def _():
                comm.start(cins, couts, cscr)

            if comm.middle is not None:
                @pl.when(step == (steps // 2 if steps > 2 else steps - 1))
                def _():
                    comm.middle(cins, couts, cscr)

            body(*pre, *ins, *outs, *scr)

            @pl.when(last)
            def _():
                comm.finish(cins, couts, cscr)

        args = list(args) + list(comm.inputs)
        in_specs += [ANY] * c_in
        out_shape += list(comm.out_shapes)
        out_specs += [ANY] * c_out
        scratch_shapes += list(comm.scratch)
    params = pltpu.CompilerParams(dimension_semantics=("arbitrary",) * len(grid), vmem_limit_bytes=VMEM_LIMIT)
    if sp:
        grid_spec = pltpu.PrefetchScalarGridSpec(num_scalar_prefetch=sp, grid=grid, in_specs=in_specs,
                                                 out_specs=out_specs, scratch_shapes=scratch_shapes)
        return pl.pallas_call(kernel_fn, name=name, grid_spec=grid_spec, out_shape=out_shape,
                              compiler_params=params)(*args)
    return pl.pallas_call(kernel_fn, name=name, grid=grid, in_specs=in_specs, out_shape=out_shape, out_specs=out_specs,
                          scratch_shapes=scratch_shapes, compiler_params=params)(*args)


def _join(a, b):
    na = (len(a.inputs), len(a.out_shapes), len(a.scratch))

    def split(refs):
        return ([r[:n] for r, n in zip(refs, na)], [r[n:] for r, n in zip(refs, na)])

    def start(*refs):
        ra, rb = split(refs)
        a.start(*ra)
        b.start(*rb)

    def finish(*refs):
        ra, rb = split(refs)
        a.finish(*ra)
        b.finish(*rb)

    def middle(*refs):
        for stage, r in zip((a, b), split(refs)):
            if stage.middle is not None:
                stage.middle(*r)

    return _Comm(list(a.inputs) + list(b.inputs), list(a.out_shapes) + list(b.out_shapes),
                 list(a.scratch) + list(b.scratch), start, finish,
                 middle if (a.middle is not None or b.middle is not None) else None)


def _run_comm(comm, name):
    def body(*refs):
        c_in, c_out = len(comm.inputs), len(comm.out_shapes)
        parts = (refs[:c_in], refs[c_in:c_in + c_out], refs[c_in + c_out:])
        comm.start(*parts)
        if comm.middle is not None:
            comm.middle(*parts)
        comm.finish(*parts)

    return pl.pallas_call(
        body, name=name, out_shape=list(comm.out_shapes), in_specs=[ANY] * len(comm.inputs),
        out_specs=[ANY] * len(comm.out_shapes), scratch_shapes=list(comm.scratch))(*comm.inputs)


HBM = pl.BlockSpec(memory_space=pltpu.HBM)
SEM = pl.BlockSpec(memory_space=pltpu.SEMAPHORE)
DATAFLOW = pltpu.SideEffectType.DATAFLOW_SIDE_EFFECTING


def _comm_start(comm, name):
    c_in, c_out = len(comm.inputs), len(comm.out_shapes)
    sems = [s(()) if s is pltpu.SemaphoreType.DMA else s for s in comm.scratch]
    bufs = list(comm.inputs) + [lax.empty(s.shape, s.dtype) for s in comm.out_shapes]

    def body(*refs):
        sem_refs = refs[c_in + c_out:c_in + c_out + len(sems)]
        comm.start(refs[:c_in], refs[c_in:c_in + c_out], sem_refs)
        refs[-1][...] = jnp.zeros_like(refs[-1])

    outs = pl.pallas_call(
        body, name=name,
        out_shape=sems + [pltpu.HBM(b.shape, b.dtype) for b in bufs] + [jax.ShapeDtypeStruct((8, LANE), F32)],
        in_specs=[HBM] * len(bufs),
        out_specs=[SEM] * len(sems) + [HBM] * len(bufs) + [pl.BlockSpec(memory_space=pltpu.VMEM)],
        input_output_aliases={i: len(sems) + i for i in range(len(bufs))},
        compiler_params=pltpu.CompilerParams(has_side_effects=DATAFLOW),
    )(*[pltpu.with_memory_space_constraint(b, pltpu.HBM) for b in bufs])
    return outs[:len(sems)], outs[len(sems):-1], outs[-1]


def _comm_wait(comm, name, sems, bufs, after):
    c_in, c_out = len(comm.inputs), len(comm.out_shapes)

    def body(*refs):
        sem_refs = refs[c_in + c_out:c_in + c_out + len(sems)]
        comm.finish(refs[:c_in], refs[c_in:c_in + c_out], sem_refs)

    outs = pl.pallas_call(
        body, name=name, out_shape=[pltpu.HBM(b.shape, b.dtype) for b in bufs],
        in_specs=[HBM] * len(bufs) + [SEM] * len(sems) + [ANY], out_specs=[HBM] * len(bufs),
        input_output_aliases={i: i for i in range(len(bufs))},
        compiler_params=pltpu.CompilerParams(has_side_effects=DATAFLOW),
    )(*bufs, *sems, after)
    return outs[c_in:]


def _ag_comm(names, flat):
    st = _Stage(names)

    def ring(me):
        x, y, c = me
        diagonal = x == y
        up = (jnp.where(diagonal, x, 1 - x), jnp.where(diagonal, 1 - y, y), c)
        down = (jnp.where(diagonal, 1 - x, x), jnp.where(diagonal, y, 1 - y), c)
        low = c == 0
        passed = tuple(jnp.where(low, d, u) for d, u in zip(down, up))
        target = tuple(jnp.where(low, u, d) for d, u in zip(down, up))
        return up, down, (1 - x, 1 - y, c), passed, target

    def parts(refs):
        (flat_ref,), (out_ref,), (send_sems, recv_sems, local_sem) = refs
        me = _position()

        def region(name, dev):
            r = st.rows[name]
            return out_ref.at[pl.ds(st.wc[name] + _lin(dev) * r, r), :]

        def own(name):
            return flat_ref.at[pl.ds(st.fl[name], st.rows[name]), :]

        def copies(k, dev, to, from_flat):
            return [pltpu.make_async_remote_copy(
                src_ref=own(n) if from_flat else region(n, dev), dst_ref=region(n, dev), send_sem=send_sems.at[k],
                recv_sem=recv_sems.at[k], device_id=to, device_id_type=MESH) for n in names]

        def whole(k):
            return pltpu.make_async_remote_copy(
                src_ref=flat_ref.at[pl.ds(0, st.R), :], dst_ref=out_ref.at[pl.ds(0, st.R), :],
                send_sem=send_sems.at[k], recv_sem=recv_sems.at[k], device_id=me, device_id_type=MESH)

        return me, region, own, copies, whole, flat_ref, out_ref, local_sem

    def start(*refs):
        me, region, own, copies, _, _, _, local_sem = parts(refs)
        for n in names:
            pltpu.make_async_copy(own(n), region(n, me), local_sem).start()
        up, down, _, _, _ = ring(me)
        for k, to in ((1, up), (2, down), (0, _peer(me, 1))):
            for cp in copies(k, me, to, True):
                cp.start()

    def middle(*refs):
        me, _, _, copies, whole, _, _, _ = parts(refs)
        up, down, _, passed, target = ring(me)
        sib = _peer(me, 1)
        whole(1).wait_recv()
        whole(2).wait_recv()
        for k, dev, to in ((3, passed, target), (4, down, sib), (5, up, sib)):
            for cp in copies(k, dev, to, False):
                cp.start()

    def finish(*refs):
        me, _, _, copies, whole, flat_ref, out_ref, local_sem = parts(refs)
        _, _, across, _, _ = ring(me)
        whole(3).wait_recv()
        for cp in copies(6, across, _peer(me, 1), False):
            cp.start()
        whole(0).wait_recv()
        for j in range(3):
            whole(4 + j).wait_recv()
        for k in range(7):
            whole(k).wait_send()
        pltpu.make_async_copy(flat_ref.at[pl.ds(0, st.R), :], out_ref.at[pl.ds(0, st.R), :], local_sem).wait()

    return _Comm([flat], [jax.ShapeDtypeStruct((st.W, D), BF)],
                 [pltpu.SemaphoreType.DMA((7,)), pltpu.SemaphoreType.DMA((7,)), pltpu.SemaphoreType.DMA],
                 start, finish, middle)


def _rs_pair_comm(names, src):
    st = _Stage(names)
    arrays = []
    for n in names:
        if not any(src[n][0] is a for a in arrays):
            arrays.append(src[n][0])
    idx = {n: [i for i, a in enumerate(arrays) if a is src[n][0]][0] for n in names}

    def slot_wait(refs):
        recv = refs[1][0]
        send_sem, recv_sem = refs[2]
        return pltpu.make_async_remote_copy(src_ref=recv, dst_ref=recv, send_sem=send_sem, recv_sem=recv_sem,
                                            device_id=_position(), device_id_type=MESH)

    def start(*refs):
        ins, (recv,), (send_sem, recv_sem) = refs
        me = _position()
        sib = _peer(me, 1)
        for q in range(NCHIP):
            dev = (q // 2, q % 2, sib[2])
            for n in names:
                r = st.rows[n]
                pltpu.make_async_remote_copy(
                    src_ref=ins[idx[n]].at[pl.ds(st.grad_row(n, src[n][1], _lin(dev)), r), :],
                    dst_ref=recv.at[q, pl.ds(st.off[n], r), :], send_sem=send_sem, recv_sem=recv_sem,
                    device_id=sib, device_id_type=MESH).start()

    def finish(*refs):
        w = slot_wait(refs)
        w.wait_recv()
        w.wait_send()

    return _Comm(arrays, [jax.ShapeDtypeStruct((NCHIP, st.R, D), BF)],
                 [pltpu.SemaphoreType.DMA, pltpu.SemaphoreType.DMA], start, finish)


def _pair_add(names, src, recv, name):
    st = _Stage(names)
    c_arr = jnp.reshape(lax.axis_index("c"), (1,)).astype(jnp.int32)

    def body(c_ref, *refs):
        r_ref, o_ref = refs[len(names)], refs[len(names) + 1]
        for a_ref, n in zip(refs, names):
            rows = slice(st.off[n], st.off[n] + st.rows[n])
            o_ref[rows, :] = (a_ref[...].astype(F32) + r_ref[rows, :].astype(F32)).astype(BF)

    def shard_spec(n):
        r = st.rows[n]
        base, step = st.grad_row(n, src[n][1], 0) // r, st.full[n] // r
        return pl.BlockSpec((r, D), lambda q, c_ref: (base + step * (2 * q + c_ref[0]), 0))

    slot = pl.BlockSpec((None, st.R, D), lambda q, c_ref: (q, 0, 0))
    return _call(body, name=name, grid=(NCHIP,), args=[c_arr] + [src[n][0] for n in names] + [recv],
                 in_specs=[shard_spec(n) for n in names] + [slot],
                 out_shape=[jax.ShapeDtypeStruct((NCHIP, st.R, D), BF)], out_specs=[slot], num_scalar_prefetch=1)[0]


def _rs_chip_comm(part):
    def copies(refs):
        (p_ref,), (recv,), (send_sems, recv_sems, local_sem) = refs
        me = _position()
        mine = pltpu.make_async_copy(p_ref.at[_chip(me)], recv.at[_chip(me)], local_sem)
        out = []
        for j, bits in enumerate((4, 2, 6)):
            to = _peer(me, bits)
            out.append(pltpu.make_async_remote_copy(
                src_ref=p_ref.at[_chip(to)], dst_ref=recv.at[_chip(me)], send_sem=send_sems.at[j],
                recv_sem=recv_sems.at[j], device_id=to, device_id_type=MESH))
        return mine, out

    def start(*refs):
        mine, out = copies(refs)
        mine.start()
        for cp in out:
            cp.start()

    def finish(*refs):
        mine, out = copies(refs)
        for cp in out:
            cp.wait_recv()
        for cp in out:
            cp.wait_send()
        mine.wait()

    return _Comm([part], [jax.ShapeDtypeStruct(part.shape, BF)],
                 [pltpu.SemaphoreType.DMA((3,)), pltpu.SemaphoreType.DMA((3,)), pltpu.SemaphoreType.DMA],
                 start, finish)


def _direct_comm(x, scatter):
    def copies(refs):
        (x_ref,), (out_ref,), (send_sems, recv_sems, local_sem) = refs
        me = _position()

        def piece(dev):
            return x_ref.at[_lin(dev)] if scatter else x_ref

        mine = pltpu.make_async_copy(piece(me), out_ref.at[_lin(me)], local_sem)
        return mine, [pltpu.make_async_remote_copy(
            src_ref=piece(_peer(me, j)), dst_ref=out_ref.at[_lin(me)], send_sem=send_sems.at[j - 1],
            recv_sem=recv_sems.at[j - 1], device_id=_peer(me, j), device_id_type=MESH) for j in range(1, NDEV)]

    def start(*refs):
        mine, cps = copies(refs)
        mine.start()
        for cp in cps:
            cp.start()

    def finish(*refs):
        mine, cps = copies(refs)
        for cp in cps:
            cp.wait_recv()
        for cp in cps:
            cp.wait_send()
        mine.wait()

    shape = x.shape if scatter else (NDEV,) + x.shape
    return _Comm([x], [jax.ShapeDtypeStruct(shape, x.dtype)],
                 [pltpu.SemaphoreType.DMA((7,)), pltpu.SemaphoreType.DMA((7,)), pltpu.SemaphoreType.DMA],
                 start, finish)


def _pack_weights(shards):
    lay = _Layout()

    def body(*refs):
        o_ref = refs[-1]
        for ref, n in zip(refs, ORDER):
            x = ref[...].T if n == "win" else ref[...]
            o_ref[lay.fl[n]:lay.fl[n] + lay.rows[n], :] = x.astype(BF)

    return pl.pallas_call(
        body, name="pack_weights", out_shape=jax.ShapeDtypeStruct((lay.RT, D), BF),
        compiler_params=pltpu.CompilerParams(vmem_limit_bytes=VMEM_LIMIT))(*[shards[n] for n in ORDER])


def _load_ffn_weights(srcs, offs, scratch, sem):
    @pl.when(pl.program_id(0) == 0)
    def _():
        cps = [pltpu.make_async_copy(s.at[pl.ds(off, dst.shape[0]), :], dst, sem.at[i])
               for i, (s, off, dst) in enumerate(zip(srcs, offs, scratch))]
        for cp in cps:
            cp.start()
        for cp in cps:
            cp.wait()


def _final_loss_tile(xf, g, tgt, s_ref):
    r = lax.rsqrt(jnp.mean(xf * xf, axis=-1, keepdims=True) + EPS)
    xr = xf * r
    e = xr * g - tgt
    s_ref[1:2, :] += jnp.sum(e * e, axis=0, keepdims=True) * (0.5 / D)
    dy = e * (1.0 / D)
    s_ref[0:1, :] += jnp.sum(dy * xr, axis=0, keepdims=True)
    gdy = dy * g
    return r * gdy - xr * (r * jnp.mean(gdy * xr, axis=-1, keepdims=True))


def _ffn_fwd(x, g, wbufs, offs, name, comm=None, final=None):
    nf = F // FC

    def body(x_ref, g_ref, b0, b1, b2, *rest):
        if final is None:
            h_ref, n_ref, gg_ref, uu_ref, a_ref, wg_s, wu_s, wd_s, sem = rest
        else:
            gf_ref, t_ref, dh_ref, dhb_ref, s_ref, n_ref, gg_ref, uu_ref, a_ref, wg_s, wu_s, wd_s, sem = rest

            @pl.when(pl.program_id(0) == 0)
            def _():
                s_ref[...] = jnp.zeros_like(s_ref)

        _load_ffn_weights((b0, b1, b2), offs, (wg_s, wu_s, wd_s), sem)
        xf = x_ref[...]
        r = lax.rsqrt(jnp.mean(xf * xf, axis=-1, keepdims=True) + EPS)
        nb = (xf * r * g_ref[...]).astype(BF)
        n_ref[...] = nb
        acc = jnp.zeros((TM, D), F32)
        for c in range(nf):
            sl = slice(c * FC, (c + 1) * FC)
            gb = _nt(nb, wg_s[sl, :]).astype(BF)
            ub = _nt(nb, wu_s[sl, :]).astype(BF)
            gg_ref[:, sl] = gb
            uu_ref[:, sl] = ub
            a = (gb * _sig(gb)) * ub
            a_ref[0, :, sl] = a
            acc = acc + _nn(a, wd_s[sl, :])
        h = xf + 0.5 * acc
        if final is None:
            h_ref[...] = h
        else:
            dh = _final_loss_tile(h, gf_ref[...], t_ref[...], s_ref)
            dh_ref[...] = dh
            dhb_ref[...] = (0.5 * dh).astype(BF)

    row = lambda i: (i, 0)
    vec = pl.BlockSpec((1, D), lambda i: (0, 0))
    tile = pl.BlockSpec((TM, D), row)
    saved_shapes = [jax.ShapeDtypeStruct((T, D), BF), jax.ShapeDtypeStruct((T, F), BF), jax.ShapeDtypeStruct((T, F), BF),
                    jax.ShapeDtypeStruct((1, T, F), BF)]
    saved_specs = [tile, pl.BlockSpec((TM, F), row), pl.BlockSpec((TM, F), row),
                   pl.BlockSpec((1, TM, F), lambda i: (0, i, 0))]
    if final is None:
        extra_args, extra_specs = [], []
        head_shapes, head_specs = [jax.ShapeDtypeStruct((T, D), F32)], [tile]
    else:
        extra_args, extra_specs = list(final), [vec, tile]
        head_shapes = [jax.ShapeDtypeStruct((T, D), F32), jax.ShapeDtypeStruct((T, D), BF), jax.ShapeDtypeStruct((8, D), F32)]
        head_specs = [tile, tile, pl.BlockSpec((8, D), lambda i: (0, 0))]
    return _call(
        body, name=name, grid=(T // TM,), args=[x, g, *wbufs, *extra_args], comm=comm,
        in_specs=[tile, vec, ANY, ANY, ANY] + extra_specs,
        out_shape=head_shapes + saved_shapes, out_specs=head_specs + saved_specs,
        scratch_shapes=[pltpu.VMEM((F, D), BF)] * 3 + [pltpu.SemaphoreType.DMA((3,))])


def _ffn_gate_up(x, g, wbufs, offs, name, comm=None):
    nf = F // FC

    def body(x_ref, g_ref, b0, b1, n_ref, gg_ref, uu_ref, a_ref, wg_s, wu_s, sem):
        _load_ffn_weights((b0, b1), offs, (wg_s, wu_s), sem)
        xf = x_ref[...]
        r = lax.rsqrt(jnp.mean(xf * xf, axis=-1, keepdims=True) + EPS)
        nb = (xf * r * g_ref[...]).astype(BF)
        n_ref[...] = nb
        for c in range(nf):
            sl = slice(c * FC, (c + 1) * FC)
            gb = _nt(nb, wg_s[sl, :]).astype(BF)
            ub = _nt(nb, wu_s[sl, :]).astype(BF)
            gg_ref[:, sl] = gb
            uu_ref[:, sl] = ub
            a_ref[0, :, sl] = (gb * _sig(gb)) * ub

    row = lambda i: (i, 0)
    tile = pl.BlockSpec((TM, D), row)
    return _call(
        body, name=name, grid=(T // TM,), args=[x, g, *wbufs], comm=comm,
        in_specs=[tile, pl.BlockSpec((1, D), lambda i: (0, 0)), ANY, ANY],
        out_shape=[jax.ShapeDtypeStruct((T, D), BF), jax.ShapeDtypeStruct((T, F), BF), jax.ShapeDtypeStruct((T, F), BF),
                   jax.ShapeDtypeStruct((1, T, F), BF)],
        out_specs=[tile, pl.BlockSpec((TM, F), row), pl.BlockSpec((TM, F), row),
                   pl.BlockSpec((1, TM, F), lambda i: (0, i, 0))],
        scratch_shapes=[pltpu.VMEM((F, D), BF)] * 2 + [pltpu.SemaphoreType.DMA((2,))])


def _ffn_down(x, act, wbuf, off, name, comm=None):
    def body(x_ref, a_ref, b0, h_ref, wd_s, sem):
        _load_ffn_weights((b0,), (off,), (wd_s,), sem)
        h_ref[...] = x_ref[...] + 0.5 * _nn(a_ref[0], wd_s[...])

    tile = pl.BlockSpec((TM, D), lambda i: (i, 0))
    return _call(
        body, name=name, grid=(T // TM,), args=[x, act, wbuf], comm=comm,
        in_specs=[tile, pl.BlockSpec((1, TM, F), lambda i: (0, i, 0)), ANY],
        out_shape=[jax.ShapeDtypeStruct((T, D), F32)], out_specs=[tile],
        scratch_shapes=[pltpu.VMEM((F, D), BF), pltpu.SemaphoreType.DMA((1,))])


def _load_in_proj(parts, w_s, sem):
    @pl.when(pl.program_id(0) == 0)
    def _():
        shard = NG * D // NDEV
        rows = shard // len(parts)
        cps = [pltpu.make_async_copy(buf.at[pl.ds(first + k * rows, rows), :],
                                     w_s.at[pl.ds(k * shard + p * rows, rows), :], sem.at[p * NDEV + k])
               for p, (buf, first) in enumerate(parts) for k in range(NDEV)]
        for cp in cps:
            cp.start()
        for cp in cps:
            cp.wait()


def _mix_in(h1, gm, win, comm=None):
    def body(h_ref, g_ref, *rest):
        w_any, (u_ref, z_ref, w_s, sem) = rest[:len(win)], rest[len(win):]
        _load_in_proj([(b, first) for b, (_, first) in zip(w_any, win)], w_s, sem)
        xf = h_ref[...]
        r = lax.rsqrt(jnp.mean(xf * xf, axis=-1, keepdims=True) + EPS)
        ub = (xf * r * g_ref[...]).astype(BF)
        u_ref[...] = ub
        for j in range(NG):
            z_ref[j] = _nt(ub, w_s[j * D:(j + 1) * D, :]).astype(BF)

    row = lambda i: (i, 0)
    return _call(
        body, name="mix_in", grid=(T // TM,), args=[h1, gm] + [b for b, _ in win], comm=comm,
        in_specs=[pl.BlockSpec((TM, D), row), pl.BlockSpec((1, D), lambda i: (0, 0))] + [ANY] * len(win),
        out_shape=[jax.ShapeDtypeStruct((T, D), BF), jax.ShapeDtypeStruct((NG, T, D), BF)],
        out_specs=[pl.BlockSpec((TM, D), row), pl.BlockSpec((NG, TM, D), lambda i: (0, i, 0))],
        scratch_shapes=[pltpu.VMEM((NG * D, D), BF), pltpu.SemaphoreType.DMA((NDEV * len(win),))])


def _shift_up(w, b):
    return w if b == 0 else pltpu.roll(w, w.shape[0] - b, 0)


def _fold8(p):
    red = p[0:8, :]
    for i in range(1, p.shape[0] // 8):
        red = red + p[8 * i:8 * i + 8, :]
    return red


def _dft_constants():
    import numpy as np
    nh = NB // 2
    f, n = np.arange(nh)[:, None], np.arange(NB)[None, :]
    ang = 2.0 * np.pi / NB * f * n
    fc = np.cos(ang)
    fs = np.where(f == 0, (-1.0) ** n, np.sin(ang))
    scale = np.where(f == 0, 1.0, 2.0) / NB
    ic = (scale * np.cos(ang)).T
    isn = np.where(f == 0, (-1.0) ** n / NB, scale * np.sin(ang)).T
    d = (KA - 1 - np.arange(32))[None, :]
    valid = (np.arange(32) < KA)[None, :]
    angk = 2.0 * np.pi / NB * f * d
    kc = np.where(valid, np.cos(angk), 0.0)
    ks = np.where(valid, np.sin(angk), 0.0)
    k2 = np.where(valid, np.where(f == 0, (-1.0) ** d, np.cos(angk)), 0.0)
    rtc = np.where(valid, scale * np.cos(angk), 0.0).T
    rts = np.where(valid, np.where(f == 0, (-1.0) ** d / NB, scale * np.sin(angk)), 0.0).T

    def bf(a):
        return jnp.asarray(a, F32).astype(BF)

    def split(a):
        hi = bf(a)
        return hi, (jnp.asarray(a, F32) - hi.astype(F32)).astype(BF)

    return dict(fc=bf(fc), fs=bf(fs), ic_hi=bf(ic[HB:]), is_hi=bf(isn[HB:]), ic_lo=bf(ic[:HB]), is_lo=bf(isn[:HB]),
                kc=split(kc), ks=split(ks), k2=split(k2), rtc=split(rtc), rts=split(rts))


def _dot3(m_hi, m_lo, x):
    x_hi = x.astype(BF)
    x_lo = (x - x_hi.astype(F32)).astype(BF)
    return _nn(m_hi, x_hi) + _nn(m_hi, x_lo) + _nn(m_lo, x_hi)


def _whole(a):
    return pl.BlockSpec(a.shape, lambda c, t: (0,) * a.ndim)


def _filter_spectrum(cw_ref, tabs, hc, hs, h2):
    w32 = cw_ref[0:32, :]
    for (hi, lo), dst in zip(tabs, (hc, hs, h2)):
        dst[...] = _dot3(hi[...], lo[...], w32)


def _conv_fwd_dft(z, cw, bias, dft, comm=None):
    nt = T // TB
    hb = TB // HB

    def body(z_ref, zh_ref, cw_ref, b_ref, fc_ref, fs_ref, ic_ref, is_ref, kch, kcl, ksh, ksl, k2h, k2l,
             a1_ref, q_ref, aext, ppad, hc, hs, h2):
        first = pl.program_id(1) == 0
        f = lambda ref, j: ref[j].astype(F32)

        @pl.when(first)
        def _():
            _filter_spectrum(cw_ref, ((kch, kcl), (ksh, ksl), (k2h, k2l)), hc, hs, h2)

        aext[0:HB, :] = jnp.where(first, 0.0, f(zh_ref, 0) * _sig(f(zh_ref, 1))).astype(BF)
        aext[HB:, :] = (f(z_ref, 0) * _sig(f(z_ref, 1))).astype(BF)
        ppad[0:8, :] = jnp.where(first, 0.0, f(zh_ref, 3)[HB - 8:HB, :] * f(zh_ref, 4)[HB - 8:HB, :])
        ppad[8:, :] = f(z_ref, 3) * f(z_ref, 4)
        bias_row = b_ref[...]

        for j in range(TB // HB):
            xs = aext[j * HB:j * HB + NB, :]
            xa, xb = _nn(fc_ref[...], xs), _nn(fs_ref[...], xs)
            yc = (hc[...] * xa - hs[...] * xb).astype(BF)
            ys = (h2[...] * xb + hs[...] * xa).astype(BF)
            y = _nn(ic_ref[...], yc) + _nn(is_ref[...], ys)
            a1_ref[j * HB:(j + 1) * HB, :] = (y + bias_row).astype(BF)

        def chunk(r, carry):
            base = pl.multiple_of(r * CHB, CHB)
            pw = ppad[pl.ds(base, CHB + 8), :]
            v = (cw_ref[pl.ds(32, 1), :] * _shift_up(pw, 6)[0:CHB, :]
                 + cw_ref[pl.ds(33, 1), :] * _shift_up(pw, 7)[0:CHB, :]
                 + cw_ref[pl.ds(34, 1), :] * pw[8:8 + CHB, :])
            q_ref[pl.ds(base, CHB), :] = (z_ref[2, pl.ds(base, CHB), :].astype(F32) * v).astype(BF)
            return carry

        lax.fori_loop(0, TB // CHB, chunk, 0)

    blk = pl.BlockSpec((TB, CW), lambda c, t: (t, c))
    tabs = [dft["fc"], dft["fs"], dft["ic_hi"], dft["is_hi"], *dft["kc"], *dft["ks"], *dft["k2"]]
    return _call(
        body, name="conv_fwd", grid=(D // CW, nt), comm=comm, args=[z, z, cw, bias] + tabs,
        in_specs=[pl.BlockSpec((5, TB, CW), lambda c, t: (0, t, c)),
                  pl.BlockSpec((5, HB, CW), lambda c, t: (0, jnp.maximum(t * hb - 1, 0), c)),
                  pl.BlockSpec((40, CW), lambda c, t: (0, c)), pl.BlockSpec((1, CW), lambda c, t: (0, c))]
                 + [_whole(a) for a in tabs],
        out_shape=[jax.ShapeDtypeStruct((T, D), BF), jax.ShapeDtypeStruct((T, D), BF)], out_specs=[blk, blk],
        scratch_shapes=[pltpu.VMEM((TB + HB, CW), BF), pltpu.VMEM((TB + 8, CW), F32)]
                       + [pltpu.VMEM((NB // 2, CW), F32)] * 3)


def _conv_bwd_dft(z, da1, dq, dzg, cw, dft, comm=None):
    nt = T // TB
    hb = TB // HB
    last_h = T // HB - 1

    def body(z_ref, zp_ref, zn_ref, da1_ref, da1n_ref, dq_ref, dqn_ref, dzg_ref, cw_ref,
             fc_ref, fs_ref, ic_ref, is_ref, kch, kcl, ksh, ksl, k2h, k2l, rch, rcl, rsh, rsl,
             dz_ref, dwa_ref, dwb_ref, aext, dyext, ppad, dvpad, hc, hs, h2, rc, rs, nyq, acc_b):
        t = pl.program_id(1)
        first, last = t == 0, t == nt - 1
        f = lambda ref, j: ref[j].astype(F32)

        @pl.when(first)
        def _():
            _filter_spectrum(cw_ref, ((kch, kcl), (ksh, ksl), (k2h, k2l)), hc, hs, h2)
            rc[...] = jnp.zeros_like(rc)
            rs[...] = jnp.zeros_like(rs)
            nyq[...] = jnp.zeros_like(nyq)
            acc_b[...] = jnp.zeros_like(acc_b)

        aext[0:HB, :] = jnp.where(first, 0.0, f(zp_ref, 0) * _sig(f(zp_ref, 1))).astype(BF)
        aext[HB:, :] = (f(z_ref, 0) * _sig(f(z_ref, 1))).astype(BF)
        dyext[0:TB, :] = da1_ref[...]
        dyext[TB:, :] = jnp.where(last, 0.0, da1n_ref[...].astype(F32)).astype(BF)
        ppad[0:8, :] = jnp.where(first, 0.0, f(zp_ref, 3)[HB - 8:HB, :] * f(zp_ref, 4)[HB - 8:HB, :])
        ppad[8:, :] = f(z_ref, 3) * f(z_ref, 4)
        dvpad[0:TB, :] = dq_ref[...].astype(F32) * f(z_ref, 2)
        dvpad[TB:, :] = jnp.where(last, 0.0, dqn_ref[...].astype(F32)[0:8, :] * f(zn_ref, 2)[0:8, :])

        for j in range(TB // HB):
            rows = slice(j * HB, (j + 1) * HB)
            dys = dyext[j * HB:j * HB + NB, :]
            da, db = _nn(fc_ref[...], dys), _nn(fs_ref[...], dys)
            gc = (hc[...] * da + hs[...] * db).astype(BF)
            gs = (h2[...] * db - hs[...] * da).astype(BF)
            da0 = _nn(ic_ref[...], gc) + _nn(is_ref[...], gs)
            z0, z1 = z_ref[0, rows, :].astype(F32), z_ref[1, rows, :].astype(F32)
            s1 = _sig(z1)
            dz_ref[0, rows, :] = (da0 * s1).astype(BF)
            dz_ref[1, rows, :] = (da0 * z0 * (s1 * (1.0 - s1))).astype(BF)
            xs = aext[j * HB:j * HB + NB, :]
            xa, xb = _nn(fc_ref[...], xs), _nn(fs_ref[...], xs)
            dyb = dyext[rows, :]
            pa, pb = _nn(fc_ref[:, HB:NB], dyb), _nn(fs_ref[:, HB:NB], dyb)
            rc[...] += pa * xa + pb * xb
            rs[...] += pb * xa - pa * xb
            nyq[...] += pb[0:8, :] * xb[0:8, :]

        def chunk(r, carry):
            base = pl.multiple_of(r * CHB, CHB)
            rows = pl.ds(base, CHB)
            pw = ppad[pl.ds(base, CHB + 8), :]
            p6 = _shift_up(pw, 6)[0:CHB, :]
            p7 = _shift_up(pw, 7)[0:CHB, :]
            p8 = pw[8:8 + CHB, :]
            wb0, wb1, wb2 = cw_ref[pl.ds(32, 1), :], cw_ref[pl.ds(33, 1), :], cw_ref[pl.ds(34, 1), :]
            v = wb0 * p6 + wb1 * p7 + wb2 * p8
            dz_ref[2, rows, :] = (dq_ref[rows, :].astype(F32) * v).astype(BF)
            dvw = dvpad[pl.ds(base, CHB + 8), :]
            dvc = dvw[0:CHB, :]
            dp = wb2 * dvc + wb1 * _shift_up(dvw, 1)[0:CHB, :] + wb0 * _shift_up(dvw, 2)[0:CHB, :]
            dz_ref[3, rows, :] = (dp * z_ref[4, rows, :].astype(F32)).astype(BF)
            dz_ref[4, rows, :] = (dp * z_ref[3, rows, :].astype(F32)).astype(BF)
            acc_b[0:8, :] += _fold8(dvc * p6)
            acc_b[8:16, :] += _fold8(dvc * p7)
            acc_b[16:24, :] += _fold8(dvc * p8)
            dz_ref[5, rows, :] = dzg_ref[0, rows, :]
            dz_ref[6, rows, :] = dzg_ref[1, rows, :]
            return carry

        lax.fori_loop(0, TB // CHB, chunk, 0)

        @pl.when(last)
        def _():
            row0 = lax.broadcasted_iota(jnp.int32, (NB // 2, CW), 0) == 0
            ny = jnp.broadcast_to(nyq[0:1, :], (NB // 2, CW))
            rcv = jnp.where(row0, rc[...] - ny, rc[...])
            rsv = jnp.where(row0, ny, rs[...])
            dwa_ref[...] = _dot3(rch[...], rcl[...], rcv) + _dot3(rsh[...], rsl[...], rsv)
            for k in range(KB):
                dwb_ref[k:k + 1, :] = jnp.sum(acc_b[8 * k:8 * k + 8, :], axis=0, keepdims=True)
            dwb_ref[KB:8, :] = jnp.zeros((8 - KB, CW), F32)

    blk = lambda c, t: (t, c)
    nxt = lambda c, t: (jnp.minimum((t + 1) * hb, last_h), c)
    tabs = [dft["fc"], dft["fs"], dft["ic_lo"], dft["is_lo"], *dft["kc"], *dft["ks"], *dft["k2"], *dft["rtc"], *dft["rts"]]
    return _call(
        body, name="conv_bwd", grid=(D // CW, nt), comm=comm, args=[z, z, z, da1, da1, dq, dq, dzg, cw] + tabs,
        in_specs=[pl.BlockSpec((5, TB, CW), lambda c, t: (0, t, c)),
                  pl.BlockSpec((5, HB, CW), lambda c, t: (0, jnp.maximum(t * hb - 1, 0), c)),
                  pl.BlockSpec((5, HB, CW), lambda c, t: (0, jnp.minimum((t + 1) * hb, last_h), c)),
                  pl.BlockSpec((TB, CW), blk), pl.BlockSpec((HB, CW), nxt),
                  pl.BlockSpec((TB, CW), blk), pl.BlockSpec((HB, CW), nxt),
                  pl.BlockSpec((2, TB, CW), lambda c, t: (0, t, c)),
                  pl.BlockSpec((40, CW), lambda c, t: (0, c))]
                 + [_whole(a) for a in tabs],
        out_shape=[jax.ShapeDtypeStruct((NG, T, D), BF), jax.ShapeDtypeStruct((32, D), F32),
                   jax.ShapeDtypeStruct((8, D), F32)],
        out_specs=[pl.BlockSpec((NG, TB, CW), lambda c, t: (0, t, c)),
                   pl.BlockSpec((32, CW), lambda c, t: (0, c)), pl.BlockSpec((8, CW), lambda c, t: (0, c))],
        scratch_shapes=[pltpu.VMEM((TB + HB, CW), BF), pltpu.VMEM((TB + HB, CW), BF),
                        pltpu.VMEM((TB + 8, CW), F32), pltpu.VMEM((TB + 8, CW), F32)]
                       + [pltpu.VMEM((NB // 2, CW), F32)] * 5 + [pltpu.VMEM((8, CW), F32), pltpu.VMEM((24, CW), F32)])


def _layernorm_silu(a1, lng, lnb):
    mu = jnp.mean(a1, axis=-1, keepdims=True)
    xc = a1 - mu
    rs = lax.rsqrt(jnp.mean(xc * xc, axis=-1, keepdims=True) + EPS)
    xh = xc * rs
    a2 = xh * lng + lnb
    sg = _sig(a2)
    return xh, rs, a2, sg


def _square_specs(blocks):
    return [pl.BlockSpec((D, D), lambda i, b=b: (b, 0)) for b in blocks]


def _mix_out(a1, q, z, h1, lng, lnb, wsq, comm=None):
    def body(a1_ref, q_ref, ga_ref, gb_ref, h_ref, lng_ref, lnb_ref, wa_ref, wb_ref, wo_ref, h2_ref, ya_ref, yb_ref):
        _, _, a2, sg = _layernorm_silu(a1_ref[...].astype(F32), lng_ref[...], lnb_ref[...])
        ya = _nn((a2 * sg).astype(BF), wa_ref[...])
        yb = _nn(q_ref[...], wb_ref[...])
        ya_ref[...] = ya.astype(BF)
        yb_ref[...] = yb.astype(BF)
        m = _sig(ga_ref[...].astype(F32)) * ya + _sig(gb_ref[...].astype(F32)) * yb
        h2_ref[...] = h_ref[...] + _nn(m.astype(BF), wo_ref[...])

    row = lambda i: (i, 0)
    vec = pl.BlockSpec((1, D), lambda i: (0, 0))
    return _call(
        body, name="mix_out", grid=(T // TM,), args=[a1, q, z, z, h1, lng, lnb, wsq, wsq, wsq], comm=comm,
        in_specs=[pl.BlockSpec((TM, D), row), pl.BlockSpec((TM, D), row),
                  pl.BlockSpec((None, TM, D), lambda i: (5, i, 0)), pl.BlockSpec((None, TM, D), lambda i: (6, i, 0)),
                  pl.BlockSpec((TM, D), row), vec, vec] + _square_specs((0, 1, 2)),
        out_shape=[jax.ShapeDtypeStruct((T, D), F32), jax.ShapeDtypeStruct((T, D), BF), jax.ShapeDtypeStruct((T, D), BF)],
        out_specs=[pl.BlockSpec((TM, D), row)] * 3)


def _rmsnorm_bwd(xf, g, dn):
    r = lax.rsqrt(jnp.mean(xf * xf, axis=-1, keepdims=True) + EPS)
    xr = xf * r
    gdn = dn * g
    dx = r * gdn - xr * (r * jnp.mean(gdn * xr, axis=-1, keepdims=True))
    return dx, jnp.sum(dn * xr, axis=0, keepdims=True)


def _ffn_bwd_hidden(dh, gg, uu, wbuf, off, name, comm=None):
    nf = F // FC

    def body(dh_ref, gg_ref, uu_ref, b0, dgu_ref, wd_s, sem):
        _load_ffn_weights((b0,), (off,), (wd_s,), sem)
        dhb = dh_ref[...]
        for c in range(nf):
            sl = slice(c * FC, (c + 1) * FC)
            da = _nt(dhb, wd_s[sl, :]).astype(BF)
            gb, ub = gg_ref[:, sl], uu_ref[:, sl]
            sg = _sig(gb)
            dgu_ref[0, :, sl] = (da * ub) * (sg * (1.0 + gb * (1.0 - sg)))
            dgu_ref[0, :, F + c * FC:F + (c + 1) * FC] = da * (gb * sg)

    row = lambda i: (i, 0)
    return _call(
        body, name=name, grid=(T // TM,), args=[dh, gg, uu, wbuf], comm=comm,
        in_specs=[pl.BlockSpec((TM, D), row), pl.BlockSpec((TM, F), row), pl.BlockSpec((TM, F), row), ANY],
        out_shape=[jax.ShapeDtypeStruct((1, T, 2 * F), BF)],
        out_specs=[pl.BlockSpec((1, TM, 2 * F), lambda i: (0, i, 0))],
        scratch_shapes=[pltpu.VMEM((F, D), BF), pltpu.SemaphoreType.DMA((1,))])


def _ffn_bwd_input(dgu, dh, x, g, wbufs, offs, name, comm=None):
    def body(dgu_ref, dh_ref, x_ref, g_ref, b0, b1, dx_ref, s_ref, w_s, sem):
        _load_ffn_weights((b0, b1), offs, (w_s.at[pl.ds(0, F), :], w_s.at[pl.ds(F, F), :]), sem)

        @pl.when(pl.program_id(0) == 0)
        def _():
            s_ref[...] = jnp.zeros_like(s_ref)

        dn = _nn(dgu_ref[0], w_s[...])
        dxn, dg = _rmsnorm_bwd(x_ref[...], g_ref[...], dn)
        dx_ref[...] = dh_ref[...] + dxn
        s_ref[0:1, :] += dg

    row = lambda i: (i, 0)
    return _call(
        body, name=name, grid=(T // TM,), args=[dgu, dh, x, g, *wbufs], comm=comm,
        in_specs=[pl.BlockSpec((1, TM, 2 * F), lambda i: (0, i, 0)), pl.BlockSpec((TM, D), row),
                  pl.BlockSpec((TM, D), row), pl.BlockSpec((1, D), lambda i: (0, 0)), ANY, ANY],
        out_shape=[jax.ShapeDtypeStruct((T, D), F32), jax.ShapeDtypeStruct((8, D), F32)],
        out_specs=[pl.BlockSpec((TM, D), row), pl.BlockSpec((8, D), lambda i: (0, 0))],
        scratch_shapes=[pltpu.VMEM((2 * F, D), BF), pltpu.SemaphoreType.DMA((2,))])


def _tn_matmul(lhs, rhs, tr, name, comm=None):
    ng, _, cdim = lhs.shape
    nc, nk = cdim // tr, T // TK
    if rhs.ndim == 2:
        r_spec = pl.BlockSpec((TK, D), lambda g, c, k: (k, 0))
    else:
        r_spec = pl.BlockSpec((None, TK, D), lambda g, c, k: (g, k, 0))

    def body(l_ref, r_ref, o_ref, acc):
        k = pl.program_id(2)

        @pl.when(k == 0)
        def _():
            acc[...] = jnp.zeros_like(acc)

        acc[...] += _tn(l_ref[...], r_ref[...])

        @pl.when(k == nk - 1)
        def _():
            o_ref[...] = acc[...].astype(BF)

    return _call(
        body, name=name, grid=(ng, nc, nk), args=[lhs, rhs], comm=comm,
        in_specs=[pl.BlockSpec((None, TK, tr), lambda g, c, k: (g, k, c)), r_spec],
        out_shape=[jax.ShapeDtypeStruct((ng * cdim, D), BF)],
        out_specs=[pl.BlockSpec((tr, D), lambda g, c, k: (g * nc + c, 0))],
        scratch_shapes=[pltpu.VMEM((tr, D), F32)])


def _mix_out_bwd(dh2, ya, yb, z, a1, q, lng, lnb, wsq, comm=None):
    def body(dh_ref, ya_ref, yb_ref, ga_ref, gb_ref, a1_ref, q_ref, lng_ref, lnb_ref, wa_ref, wb_ref, wo_ref,
             dzg_ref, da1_ref, dq_ref, l_ref, r_ref, s_ref):
        @pl.when(pl.program_id(0) == 0)
        def _():
            s_ref[...] = jnp.zeros_like(s_ref)

        dhb = dh_ref[...].astype(BF)
        dm = _nt(dhb, wo_ref[...]).astype(BF)
        ya, yb = ya_ref[...], yb_ref[...]
        sa, sb = _sig(ga_ref[...]), _sig(gb_ref[...])
        l_ref[0] = sa * ya + sb * yb
        l_ref[2] = q_ref[...]
        dzg_ref[0] = (dm * ya) * (sa * (1.0 - sa))
        dzg_ref[1] = (dm * yb) * (sb * (1.0 - sb))
        dya = dm * sa
        dyb = dm * sb
        r_ref[0] = dhb
        r_ref[1] = dya
        r_ref[2] = dyb
        dq_ref[...] = _nt(dyb, wb_ref[...]).astype(BF)
        da3 = _nt(dya, wa_ref[...])
        lng = lng_ref[...]
        xh, rs, a2, sg = _layernorm_silu(a1_ref[...].astype(F32), lng, lnb_ref[...])
        l_ref[1] = (a2 * sg).astype(BF)
        da2 = da3 * (sg * (1.0 + a2 * (1.0 - sg)))
        s_ref[0:1, :] += jnp.sum(da2 * xh, axis=0, keepdims=True)
        s_ref[1:2, :] += jnp.sum(da2, axis=0, keepdims=True)
        dxh = da2 * lng
        da1 = rs * (dxh - jnp.mean(dxh, axis=-1, keepdims=True) - xh * jnp.mean(dxh * xh, axis=-1, keepdims=True))
        da1_ref[...] = da1.astype(BF)
        s_ref[2:3, :] += jnp.sum(da1, axis=0, keepdims=True)

    row = lambda i: (i, 0)
    row3 = lambda i: (0, i, 0)
    vec = pl.BlockSpec((1, D), lambda i: (0, 0))
    return _call(
        body, name="mix_out_bwd", grid=(T // TM,), args=[dh2, ya, yb, z, z, a1, q, lng, lnb, wsq, wsq, wsq], comm=comm,
        in_specs=[pl.BlockSpec((TM, D), row), pl.BlockSpec((TM, D), row), pl.BlockSpec((TM, D), row),
                  pl.BlockSpec((None, TM, D), lambda i: (5, i, 0)), pl.BlockSpec((None, TM, D), lambda i: (6, i, 0)),
                  pl.BlockSpec((TM, D), row), pl.BlockSpec((TM, D), row), vec, vec] + _square_specs((0, 1, 2)),
        out_shape=[jax.ShapeDtypeStruct((2, T, D), BF), jax.ShapeDtypeStruct((T, D), BF),
                   jax.ShapeDtypeStruct((T, D), BF), jax.ShapeDtypeStruct((3, T, D), BF),
                   jax.ShapeDtypeStruct((3, T, D), BF), jax.ShapeDtypeStruct((8, D), F32)],
        out_specs=[pl.BlockSpec((2, TM, D), row3), pl.BlockSpec((TM, D), row), pl.BlockSpec((TM, D), row),
                   pl.BlockSpec((3, TM, D), row3), pl.BlockSpec((3, TM, D), row3), pl.BlockSpec((8, D), lambda i: (0, 0))])


def _mix_in_bwd(dz, dh2, h1, gm, win, comm=None):
    def body(dz_ref, dh_ref, h_ref, g_ref, *rest):
        w_any, (o_ref, ob_ref, s_ref, w_s, sem) = rest[:len(win)], rest[len(win):]
        _load_in_proj([(b, first) for b, (_, first) in zip(w_any, win)], w_s, sem)

        @pl.when(pl.program_id(0) == 0)
        def _():
            s_ref[...] = jnp.zeros_like(s_ref)

        du = _nn(dz_ref[0], w_s[0:D, :])
        for j in range(1, NG):
            du = du + _nn(dz_ref[j], w_s[j * D:(j + 1) * D, :])
        dx, dg = _rmsnorm_bwd(h_ref[...], g_ref[...], du)
        dh1 = dh_ref[...] + dx
        o_ref[...] = dh1
        ob_ref[...] = (0.5 * dh1).astype(BF)
        s_ref[0:1, :] += dg

    row = lambda i: (i, 0)
    return _call(
        body, name="mix_in_bwd", grid=(T // TM,), args=[dz, dh2, h1, gm] + [b for b, _ in win], comm=comm,
        in_specs=[pl.BlockSpec((NG, TM, D), lambda i: (0, i, 0)), pl.BlockSpec((TM, D), row),
                  pl.BlockSpec((TM, D), row), pl.BlockSpec((1, D), lambda i: (0, 0))] + [ANY] * len(win),
        out_shape=[jax.ShapeDtypeStruct((T, D), F32), jax.ShapeDtypeStruct((T, D), BF), jax.ShapeDtypeStruct((8, D), F32)],
        out_specs=[pl.BlockSpec((TM, D), row), pl.BlockSpec((TM, D), row), pl.BlockSpec((8, D), lambda i: (0, 0))],
        scratch_shapes=[pltpu.VMEM((NG * D, D), BF), pltpu.SemaphoreType.DMA((NDEV * len(win),))])


def _row_tile(n, want, mult):
    for t in range(min(want, n), 0, -1):
        if n % t == 0 and t % mult == 0:
            return t
    return n


def _pack_small(s_ffn1, s_in, s_mix, s_ffn2, s_final, dwa, dwb):
    def body(f1, mi, mo, f2, fl, wa_ref, wb_ref, v_ref, k_ref):
        for dst, (ref, row) in enumerate(((f1, 0), (mi, 0), (mo, 0), (mo, 1), (mo, 2), (f2, 0), (fl, 0), (fl, 1))):
            v_ref[dst:dst + 1, :] = ref[row:row + 1, :]
        for k in range(NDEV):
            k_ref[k, 0:32, :] = wa_ref[:, k * LANE:(k + 1) * LANE]
            k_ref[k, 32:40, :] = wb_ref[:, k * LANE:(k + 1) * LANE]

    return pl.pallas_call(
        body, name="pack_small",
        out_shape=(jax.ShapeDtypeStruct((8, D), F32), jax.ShapeDtypeStruct((NDEV, 40, LANE), F32)),
    )(s_ffn1, s_in, s_mix, s_ffn2, s_final, dwa, dwb)


def _adam_update(g, w, m, v):
    m2 = ADAM_B1 * m + (1.0 - ADAM_B1) * g
    v2 = ADAM_B2 * v + (1.0 - ADAM_B2) * (g * g)
    c1 = 1.0 - ADAM_B1 ** ADAM_STEP
    c2 = 1.0 - ADAM_B2 ** ADAM_STEP
    return -ADAM_LR * ((m2 / c1) / (jnp.sqrt(v2 / c2) + ADAM_EPS) + ADAM_WD * w), m2, v2


def _adam_small(vecs, convs, vec_params, tap_params):
    nv, nt = len(vec_params), len(tap_params)

    def body(*refs):
        v_ref, k_ref = refs[:2]
        p_refs = refs[2:2 + 3 * (nv + nt)]
        l_ref = refs[2 + 3 * (nv + nt)]
        o_refs = refs[3 + 3 * (nv + nt):]
        s, c = v_ref[0], k_ref[0]
        for k in range(1, NDEV):
            s = s + v_ref[k]
            c = c + k_ref[k]
        l_ref[...] = jnp.sum(s[7:8, :], axis=-1, keepdims=True)
        for i in range(nv):
            w_ref, m_ref, u_ref = p_refs[3 * i: 3 * i + 3]
            g_ref, d_ref, m2_ref, u2_ref = o_refs[4 * i: 4 * i + 4]
            g = s[i:i + 1, :]
            g_ref[...] = g
            d_ref[...], m2_ref[...], u2_ref[...] = _adam_update(g, w_ref[...], m_ref[...], u_ref[...])
        for i in range(nt):
            w_ref, m_ref, u_ref = p_refs[3 * (nv + i): 3 * (nv + i) + 3]
            g_ref, d_ref, m2_ref, u2_ref = o_refs[4 * (nv + i): 4 * (nv + i) + 4]
            first = tap_params[i][0]
            for k in range(w_ref.shape[0]):
                g = c[first + k:first + k + 1, :]
                g_ref[k] = g
                d_ref[k], m2_ref[k], u2_ref[k] = _adam_update(g, w_ref[k], m_ref[k], u_ref[k])

    params = [a for p in vec_params for a in p] + [a for p in tap_params for a in p[1:]]
    out_shape = [jax.ShapeDtypeStruct((1, 1), F32)]
    for p in list(vec_params) + [p[1:] for p in tap_params]:
        out_shape += [jax.ShapeDtypeStruct(p[0].shape, F32)] * 4
    outs = pl.pallas_call(body, name="adam_small", out_shape=tuple(out_shape))(vecs, convs, *params)
    groups = [tuple(outs[1 + 4 * i: 5 + 4 * i]) for i in range(nv + nt)]
    return outs[0], groups[:nv], groups[nv:]


def _adam_in_proj(parts, w, m, v, after):
    rows = w.shape[1]
    tr = _row_tile(D, 256, LANE)

    def body(*refs):
        p_refs = refs[:len(parts)]
        w_ref, m_ref, v_ref = refs[len(parts):len(parts) + 3]
        g_ref, d_ref, m2_ref, v2_ref = refs[-4:]
        sums = []
        for p in p_refs:
            s = p[0].astype(F32)
            for k in range(1, p.shape[0]):
                s = s + p[k].astype(F32)
            sums.append(s)
        g = jnp.concatenate(sums, axis=0).T
        g_ref[...] = g
        d_ref[...], m2_ref[...], v2_ref[...] = _adam_update(g, w_ref[...], m_ref[...], v_ref[...])

    spec = pl.BlockSpec((tr, rows), lambda i: (i, 0))
    return _call(body, name="adam_in", grid=(D // tr,), args=list(parts) + [w, m, v, after],
                 in_specs=[pl.BlockSpec((p.shape[0], p.shape[1], tr), lambda i: (0, 0, i)) for p in parts] + [spec] * 3 + [ANY],
                 out_shape=[jax.ShapeDtypeStruct((D, rows), F32)] * 4, out_specs=[spec] * 4)


def _adam(gs, ws, ms, vs, name, after):
    n = len(gs)
    rows, cols = ws[0].shape
    tr = _row_tile(rows, 256, 16)

    def body(*refs):
        for i in range(n):
            g_in, w, m, v = refs[4 * i], refs[4 * i + 1][...], refs[4 * i + 2][...], refs[4 * i + 3][...]
            g_ref, d_ref, m_ref, v_ref = refs[4 * n + 1 + 4 * i: 4 * n + 1 + 4 * i + 4]
            g = g_in[0].astype(F32)
            for k in range(1, g_in.shape[0]):
                g = g + g_in[k].astype(F32)
            g_ref[...] = g
            d_ref[...], m_ref[...], v_ref[...] = _adam_update(g, w, m, v)

    spec = pl.BlockSpec((tr, cols), lambda i: (i, 0))
    args, in_specs = [], []
    for i in range(n):
        slots, first = gs[i]
        args += [slots, ws[i], ms[i], vs[i]]
        in_specs += [pl.BlockSpec((slots.shape[0], tr, cols), lambda i, b=first // tr: (0, b + i, 0))] + [spec] * 3
    outs = _call(body, name=name, grid=(rows // tr,), args=args + [after], in_specs=in_specs + [ANY],
                 out_shape=[jax.ShapeDtypeStruct((rows, cols), F32)] * (4 * n), out_specs=[spec] * (4 * n))
    return [tuple(outs[4 * i: 4 * i + 4]) for i in range(n)]


def kernel(x, ffn1_norm, ffn1_w_gate, ffn1_w_up, ffn1_w_down, mix_norm, w_in, a_dw_w, a_dw_b, a_ln_g, a_ln_b, a_w_out, b_conv_w, b_w_out, w_o, ffn2_norm, ffn2_w_gate, ffn2_w_up, ffn2_w_down, final_norm, loss_target, m_ffn1_norm, m_ffn1_w_gate, m_ffn1_w_up, m_ffn1_w_down, m_mix_norm, m_w_in, m_a_dw_w, m_a_dw_b, m_a_ln_g, m_a_ln_b, m_a_w_out, m_b_conv_w, m_b_w_out, m_w_o, m_ffn2_norm, m_ffn2_w_gate, m_ffn2_w_up, m_ffn2_w_down, m_final_norm, v_ffn1_norm, v_ffn1_w_gate, v_ffn1_w_up, v_ffn1_w_down, v_mix_norm, v_w_in, v_a_dw_w, v_a_dw_b, v_a_ln_g, v_a_ln_b, v_a_w_out, v_b_conv_w, v_b_w_out, v_w_o, v_ffn2_norm, v_ffn2_w_gate, v_ffn2_w_up, v_ffn2_w_down, v_final_norm):
    names = ("ffn1_norm", "ffn1_w_gate", "ffn1_w_up", "ffn1_w_down", "mix_norm", "w_in", "a_dw_w", "a_dw_b",
             "a_ln_g", "a_ln_b", "a_w_out", "b_conv_w", "b_w_out", "w_o", "ffn2_norm", "ffn2_w_gate", "ffn2_w_up",
             "ffn2_w_down", "final_norm")
    w = dict(ffn1_norm=ffn1_norm, ffn1_w_gate=ffn1_w_gate, ffn1_w_up=ffn1_w_up, ffn1_w_down=ffn1_w_down,
             mix_norm=mix_norm, w_in=w_in, a_dw_w=a_dw_w, a_dw_b=a_dw_b, a_ln_g=a_ln_g, a_ln_b=a_ln_b,
             a_w_out=a_w_out, b_conv_w=b_conv_w, b_w_out=b_w_out, w_o=w_o, ffn2_norm=ffn2_norm,
             ffn2_w_gate=ffn2_w_gate, ffn2_w_up=ffn2_w_up, ffn2_w_down=ffn2_w_down, final_norm=final_norm)
    m = dict(ffn1_norm=m_ffn1_norm, ffn1_w_gate=m_ffn1_w_gate, ffn1_w_up=m_ffn1_w_up, ffn1_w_down=m_ffn1_w_down,
             mix_norm=m_mix_norm, w_in=m_w_in, a_dw_w=m_a_dw_w, a_dw_b=m_a_dw_b, a_ln_g=m_a_ln_g, a_ln_b=m_a_ln_b,
             a_w_out=m_a_w_out, b_conv_w=m_b_conv_w, b_w_out=m_b_w_out, w_o=m_w_o, ffn2_norm=m_ffn2_norm,
             ffn2_w_gate=m_ffn2_w_gate, ffn2_w_up=m_ffn2_w_up, ffn2_w_down=m_ffn2_w_down, final_norm=m_final_norm)
    v = dict(ffn1_norm=v_ffn1_norm, ffn1_w_gate=v_ffn1_w_gate, ffn1_w_up=v_ffn1_w_up, ffn1_w_down=v_ffn1_w_down,
             mix_norm=v_mix_norm, w_in=v_w_in, a_dw_w=v_a_dw_w, a_dw_b=v_a_dw_b, a_ln_g=v_a_ln_g, a_ln_b=v_a_ln_b,
             a_w_out=v_a_w_out, b_conv_w=v_b_conv_w, b_w_out=v_b_w_out, w_o=v_w_o, ffn2_norm=v_ffn2_norm,
             ffn2_w_gate=v_ffn2_w_gate, ffn2_w_up=v_ffn2_w_up, ffn2_w_down=v_ffn2_w_down, final_norm=v_final_norm)
    flat = _pack_weights(dict(wg1=ffn1_w_gate[0].T, wu1=ffn1_w_up[0].T, wd1=ffn1_w_down[0], wg2=ffn2_w_gate[0].T,
                              wu2=ffn2_w_up[0].T, wd2=ffn2_w_down[0], win=w_in[0], wa=a_w_out[0], wb=b_w_out[0],
                              wo=w_o[0]))
    cw_shard = jnp.concatenate([a_dw_w[0], jnp.zeros((1, LANE), F32), b_conv_w[0], jnp.zeros((5, LANE), F32)], axis=0)

    x2, tgt = x[0], loss_target[0]
    st_a, st_b, st_b2 = ("wg1", "wu1"), ("wd1", "win/0/2"), ("win/1/2",)
    st_c, st_d, st_e = ("wa", "wb", "wo", "wg2"), ("wu2",), ("wd2",)

    buf_a, cw = _run_comm(_join(_ag_comm(st_a, flat), _direct_comm(cw_shard, False)), "ag_ffn1")
    n1, gg1, uu1, act1, buf_b = _ffn_gate_up(x2, ffn1_norm, (buf_a, buf_a), (0, F), "ffn1_gate_up", _ag_comm(st_b, flat))
    h1, buf_b2 = _ffn_down(x2, act1, buf_b, 0, "ffn1_down", _ag_comm(st_b2, flat))
    win = ((buf_b, F), (buf_b2, 0))
    u, z, buf_c = _mix_in(h1, mix_norm, win, _ag_comm(st_c, flat))
    dft = _dft_constants()
    cw = jnp.transpose(cw, (1, 0, 2)).reshape(40, D)
    a1, q, buf_d = _conv_fwd_dft(z, cw, a_dw_b, dft, _ag_comm(st_d, flat))
    h2, ya, yb, buf_e = _mix_out(a1, q, z, h1, a_ln_g, a_ln_b, buf_c, _ag_comm(st_e, flat))
    ffn2_bufs, ffn2_offs = (buf_c, buf_d, buf_e), (3 * D, 0, 0)
    dh3, dhb3, s_final, n2, gg2, uu2, act2 = _ffn_fwd(h2, ffn2_norm, ffn2_bufs, ffn2_offs, "ffn2_fwd",
                                          final=(final_norm.reshape(1, D), tgt))

    tr_f = F // 2 if (F // 2) % LANE == 0 else F
    def pair(stage, src):
        return _rs_pair_comm(stage, src)

    def chip(stage, src, pair_buf, tag):
        return _rs_chip_comm(_pair_add(stage, src, pair_buf, "pair_add_" + tag))

    (dgu2,) = _ffn_bwd_hidden(dhb3, gg2, uu2, buf_e, 0, "ffn2_bwd_h")
    (gu2,) = _tn_matmul(dgu2, n2, tr_f, "dw_gu2")
    s2a, src2a = ("wg2", "wu2"), dict(wg2=(gu2, 0), wu2=(gu2, F))
    gd2, pair2a = _tn_matmul(act2, dhb3, tr_f, "dw_d2", pair(s2a, src2a))
    s2b, src2b = ("wd2",), dict(wd2=(gd2, 0))
    dh2, s_ffn2, pair2b = _ffn_bwd_input(dgu2, dh3, h2, ffn2_norm, (buf_c, buf_d), (3 * D, 0), "ffn2_bwd_x",
                                         pair(s2b, src2b))
    dzg, da1, dq, lsq, rsq, s_mix, recv2b = _mix_out_bwd(dh2, ya, yb, z, a1, q, a_ln_g, a_ln_b, buf_c,
                                                          chip(s2b, src2b, pair2b, "2b"))
    (gsq,) = _tn_matmul(lsq, rsq, D, "dw_square")
    ssq, srcsq = ("wa", "wb", "wo"), dict(wa=(gsq, D), wb=(gsq, 2 * D), wo=(gsq, 0))
    dz, dwa, dwb, recv2a, pairsq = _conv_bwd_dft(z, da1, dq, dzg, cw, dft,
                                                 _join(chip(s2a, src2a, pair2a, "2a"), pair(ssq, srcsq)))
    gin, recvsq = _tn_matmul(dz, u, D, "dw_in", chip(ssq, srcsq, pairsq, "sq"))
    sin_a, sin_b, srcin = ("win/0/2",), ("win/1/2",), {"win/0/2": (gin, 0), "win/1/2": (gin, 0)}
    dh1, dhb1, s_in, pairin_a, pairin_b = _mix_in_bwd(dz, dh2, h1, mix_norm, win,
                                                _join(pair(sin_a, srcin), pair(sin_b, srcin)))
    dgu1, recvin_a = _ffn_bwd_hidden(dhb1, gg1, uu1, buf_b, 0, "ffn1_bwd_h",
                                           chip(sin_a, srcin, pairin_a, "in_a"))
    gu1, recvin_b = _tn_matmul(dgu1, n1, tr_f, "dw_gu1", chip(sin_b, srcin, pairin_b, "in_b"))
    s1a, src1a = ("wg1", "wu1"), dict(wg1=(gu1, 0), wu1=(gu1, F))
    gd1, pair1a = _tn_matmul(act1, dhb1, tr_f, "dw_d1", pair(s1a, src1a))
    s1b, src1b = ("wd1",), dict(wd1=(gd1, 0))
    dx, s_ffn1, recv1a, pair1b = _ffn_bwd_input(dgu1, dh1, x2, ffn1_norm, (buf_a, buf_a), (0, F), "ffn1_bwd_x",
                                                _join(chip(s1a, src1a, pair1a, "1a"), pair(s1b, src1b)))

    vec8, convk = _pack_small(s_ffn1, s_in, s_mix, s_ffn2, s_final, dwa, dwb)
    tail = _join(chip(s1b, src1b, pair1b, "1b"), _join(_direct_comm(vec8, False), _direct_comm(convk, True)))
    tail_sems, tail_bufs, token = _comm_start(tail, "xchg_tail_start")

    fs = F // NDEV
    g = dict(ffn1_w_gate=(recv1a, 0), ffn1_w_up=(recv1a, fs), ffn2_w_gate=(recv2a, 0), ffn2_w_up=(recv2a, fs),
             ffn2_w_down=(recv2b, 0), a_w_out=(recvsq, 0), b_w_out=(recvsq, D // NDEV), w_o=(recvsq, 2 * (D // NDEV)))
    grad, upd = {}, {}

    def run(group, name, after, as2d=lambda a: a[0], back=lambda a, n: a.reshape(w[n].shape)):
        res = _adam([g[n] for n in group], [as2d(w[n]) for n in group], [as2d(m[n]) for n in group],
                    [as2d(v[n]) for n in group], name, after)
        for n, r in zip(group, res):
            grad[n], upd[n] = back(r[0], n), tuple(back(a, n) for a in r[1:])
        return res[0][0]

    done = run(("ffn1_w_gate", "ffn1_w_up", "ffn2_w_gate", "ffn2_w_up"), "adam_gate_up", token,
               as2d=lambda a: a[0].T, back=lambda a, n: a.T[None])
    r_in = _adam_in_proj([recvin_a, recvin_b], w_in[0], m_w_in[0], v_w_in[0], done)
    grad["w_in"], upd["w_in"] = r_in[0][None], tuple(a[None] for a in r_in[1:])
    done = run(("a_w_out", "b_w_out", "w_o"), "adam_square", r_in[0])
    recv1b, vec_all, conv_all = _comm_wait(tail, "xchg_tail_wait", tail_sems, tail_bufs, done)
    g["ffn1_w_down"] = (recv1b, 0)
    run(("ffn1_w_down", "ffn2_w_down"), "adam_down", done)
    vec_names = ("ffn1_norm", "mix_norm", "a_ln_g", "a_ln_b", "a_dw_b", "ffn2_norm", "final_norm")
    tap_names, tap_rows = ("a_dw_w", "b_conv_w"), (0, 32)
    taps = lambda a: jnp.transpose(a, (1, 0, 2))
    loss, vec_res, tap_res = _adam_small(
        vec_all, conv_all, [tuple(t[n].reshape(1, D) for t in (w, m, v)) for n in vec_names],
        [(r,) + tuple(taps(t[n]) for t in (w, m, v)) for n, r in zip(tap_names, tap_rows)])
    for n, r in zip(vec_names, vec_res):
        grad[n], upd[n] = r[0].reshape(w[n].shape), tuple(a.reshape(w[n].shape) for a in r[1:])
    for n, r in zip(tap_names, tap_res):
        grad[n], upd[n] = taps(r[0]), tuple(taps(a) for a in r[1:])

    return (loss.reshape(()), dx.reshape(x.shape), *[grad[n] for n in names], *[upd[n][0] for n in names],
            *[upd[n][1] for n in names], *[upd[n][2] for n in names])
```

```python
import jax
import jax.numpy as jnp
from jax import lax
from jax.experimental import pallas as pl
from jax.experimental.pallas import tpu as pltpu

T = 4096
D = 1024
F = 2816
NG = 7
NDEV = 8
NCHIP = 4
KA, KB = 31, 3
EPS = 1e-6
ADAM_LR, ADAM_B1, ADAM_B2, ADAM_EPS, ADAM_WD, ADAM_STEP = 0.001, 0.9, 0.999, 1e-08, 0.01, 10

TM = 512
FC = 256
TB = 1024
NB = 256
HB = NB // 2
CW = 256
CHB = 64
LANE = 128
TK = 2048
VMEM_LIMIT = 56 * 1024 * 1024

BF = jnp.bfloat16
F32 = jnp.float32
MESH = pl.DeviceIdType.MESH
ANY = pl.BlockSpec(memory_space=pl.ANY)

ORDER = ("wg1", "wu1", "wd1", "wg2", "wu2", "wd2", "win", "wa", "wb", "wo")


class _Layout:
    def __init__(self):
        fs, dis, ds = F // NDEV, NG * D // NDEV, D // NDEV
        self.rows = dict(wg1=fs, wu1=fs, wd1=fs, wg2=fs, wu2=fs, wd2=fs, win=dis, wa=ds, wb=ds, wo=ds)
        self.fl, off = {}, 0
        for n in ORDER:
            self.fl[n] = off
            off += self.rows[n]
        self.RT = off


class _Stage:
    def __init__(self, names):
        lay = _Layout()
        self.names = names
        self.rows, self.full, self.sub, self.fl = {}, {}, {}, {}
        for n in names:
            base, i, k = (n.split("/") + ["0", "1"])[:3]
            self.full[n] = lay.rows[base]
            self.rows[n] = lay.rows[base] // int(k)
            self.sub[n] = int(i) * self.rows[n]
            self.fl[n] = lay.fl[base] + self.sub[n]
        self.off, self.wc, o, w = {}, {}, 0, 0
        for n in names:
            self.off[n], self.wc[n] = o, w
            o += self.rows[n]
            w += NDEV * self.rows[n]
        self.R, self.W = o, w

    def grad_row(self, n, first, dev_lin):
        return first + dev_lin * self.full[n] + self.sub[n]


def _nt(a, b):
    return lax.dot_general(a, b, (((1,), (1,)), ((), ())), preferred_element_type=F32)


def _nn(a, b):
    return lax.dot_general(a, b, (((1,), (0,)), ((), ())), preferred_element_type=F32)


def _tn(a, b):
    return lax.dot_general(a, b, (((0,), (0,)), ((), ())), preferred_element_type=F32)


def _sig(x):
    return 1.0 / (1.0 + jnp.exp(-x))


def _position():
    return lax.axis_index("x"), lax.axis_index("y"), lax.axis_index("c")


def _peer(pos, j):
    x, y, c = pos
    return (1 - x if j & 4 else x, 1 - y if j & 2 else y, 1 - c if j & 1 else c)


def _lin(pos):
    return 4 * pos[0] + 2 * pos[1] + pos[2]


def _chip(pos):
    return 2 * pos[0] + pos[1]


class _Comm:
    def __init__(self, inputs, out_shapes, scratch, start, finish, middle=None):
        self.inputs, self.out_shapes, self.scratch = inputs, out_shapes, scratch
        self.start, self.finish, self.middle = start, finish, middle


def _call(body, *, name, grid, args, in_specs, out_shape, out_specs, scratch_shapes=(), comm=None,
          num_scalar_prefetch=0, after=None):
    in_specs, out_shape, out_specs, scratch_shapes = list(in_specs), list(out_shape), list(out_specs), list(scratch_shapes)
    if after is not None:
        inner, pos = body, num_scalar_prefetch + len(in_specs)
        body = lambda *refs: inner(*refs[:pos], *refs[pos + 1:])
        args, in_specs = list(args) + [after], in_specs + [ANY]
    n_in, n_out, n_scr = len(in_specs), len(out_shape), len(scratch_shapes)
    sp = num_scalar_prefetch
    if comm is None:
        kernel_fn = lambda *refs: body(*refs)
        c_in = c_out = c_scr = 0
    else:
        c_in, c_out, c_scr = len(comm.inputs), len(comm.out_shapes), len(comm.scratch)

        def kernel_fn(*refs):
            pre, refs = refs[:sp], refs[sp:]
            ins, cins = refs[:n_in], refs[n_in:n_in + c_in]
            o0 = n_in + c_in
            outs, couts = refs[o0:o0 + n_out], refs[o0 + n_out:o0 + n_out + c_out]
            s0 = o0 + n_out + c_out
            scr, cscr = refs[s0:s0 + n_scr], refs[s0 + n_scr:]
            step, steps = pl.program_id(0), grid[0]
            for a in range(1, len(grid)):
                step, steps = step * grid[a] + pl.program_id(a), steps * grid[a]
            first, last = step == 0, step == steps - 1

            @pl.when(first)
            def _():
                comm.start(cins, couts, cscr)

            if comm.middle is not None:
                @pl.when(step == (steps // 2 if steps > 2 else steps - 1))
                def _():
                    comm.middle(cins, couts, cscr)

            body(*pre, *ins, *outs, *scr)

            @pl.when(last)
            def _():
                comm.finish(cins, couts, cscr)

        args = list(args) + list(comm.inputs)
        in_specs += [ANY] * c_in
        out_shape += list(comm.out_shapes)
        out_specs += [ANY] * c_out
        scratch_shapes += list(comm.scratch)
    params = pltpu.CompilerParams(dimension_semantics=("arbitrary",) * len(grid), vmem_limit_bytes=VMEM_LIMIT)
    if sp:
        grid_spec = pltpu.PrefetchScalarGridSpec(num_scalar_prefetch=sp, grid=grid, in_specs=in_specs,
                                                 out_specs=out_specs, scratch_shapes=scratch_shapes)
        return pl.pallas_call(kernel_fn, name=name, grid_spec=grid_spec, out_shape=out_shape,
                              compiler_params=params)(*args)
    return pl.pallas_call(kernel_fn, name=name, grid=grid, in_specs=in_specs, out_shape=out_shape, out_specs=out_specs,
                          scratch_shapes=scratch_shapes, compiler_params=params)(*args)


def _join(a, b):
    na = (len(a.inputs), len(a.out_shapes), len(a.scratch))

    def split(refs):
        return ([r[:n] for r, n in zip(refs, na)], [r[n:] for r, n in zip(refs, na)])

    def start(*refs):
        ra, rb = split(refs)
        a.start(*ra)
        b.start(*rb)

    def finish(*refs):
        ra, rb = split(refs)
        a.finish(*ra)
        b.finish(*rb)

    def middle(*refs):
        for stage, r in zip((a, b), split(refs)):
            if stage.middle is not None:
                stage.middle(*r)

    return _Comm(list(a.inputs) + list(b.inputs), list(a.out_shapes) + list(b.out_shapes),
                 list(a.scratch) + list(b.scratch), start, finish,
                 middle if (a.middle is not None or b.middle is not None) else None)


def _run_comm(comm, name):
    def body(*refs):
        c_in, c_out = len(comm.inputs), len(comm.out_shapes)
        parts = (refs[:c_in], refs[c_in:c_in + c_out], refs[c_in + c_out:])
        comm.start(*parts)
        if comm.middle is not None:
            comm.middle(*parts)
        comm.finish(*parts)

    return pl.pallas_call(
        body, name=name, out_shape=list(comm.out_shapes), in_specs=[ANY] * len(comm.inputs),
        out_specs=[ANY] * len(comm.out_shapes), scratch_shapes=list(comm.scratch))(*comm.inputs)


HBM = pl.BlockSpec(memory_space=pltpu.HBM)
SEM = pl.BlockSpec(memory_space=pltpu.SEMAPHORE)
DATAFLOW = pltpu.SideEffectType.DATAFLOW_SIDE_EFFECTING


def _comm_start(comm, name):
    c_in, c_out = len(comm.inputs), len(comm.out_shapes)
    sems = [s(()) if s is pltpu.SemaphoreType.DMA else s for s in comm.scratch]
    bufs = list(comm.inputs) + [lax.empty(s.shape, s.dtype) for s in comm.out_shapes]

    def body(*refs):
        sem_refs = refs[c_in + c_out:c_in + c_out + len(sems)]
        comm.start(refs[:c_in], refs[c_in:c_in + c_out], sem_refs)
        refs[-1][...] = jnp.zeros_like(refs[-1])

    outs = pl.pallas_call(
        body, name=name,
        out_shape=sems + [pltpu.HBM(b.shape, b.dtype) for b in bufs] + [jax.ShapeDtypeStruct((8, LANE), F32)],
        in_specs=[HBM] * len(bufs),
        out_specs=[SEM] * len(sems) + [HBM] * len(bufs) + [pl.BlockSpec(memory_space=pltpu.VMEM)],
        input_output_aliases={i: len(sems) + i for i in range(len(bufs))},
        compiler_params=pltpu.CompilerParams(has_side_effects=DATAFLOW),
    )(*[pltpu.with_memory_space_constraint(b, pltpu.HBM) for b in bufs])
    return outs[:len(sems)], outs[len(sems):-1], outs[-1]


def _comm_wait(comm, name, sems, bufs, after):
    c_in, c_out = len(comm.inputs), len(comm.out_shapes)

    def body(*refs):
        sem_refs = refs[c_in + c_out:c_in + c_out + len(sems)]
        comm.finish(refs[:c_in], refs[c_in:c_in + c_out], sem_refs)

    outs = pl.pallas_call(
        body, name=name, out_shape=[pltpu.HBM(b.shape, b.dtype) for b in bufs],
        in_specs=[HBM] * len(bufs) + [SEM] * len(sems) + [ANY], out_specs=[HBM] * len(bufs),
        input_output_aliases={i: i for i in range(len(bufs))},
        compiler_params=pltpu.CompilerParams(has_side_effects=DATAFLOW),
    )(*bufs, *sems, after)
    return outs[:c_in], outs[c_in:]


def _ag_comm(names, flat):
    st = _Stage(names)

    def ring(me):
        x, y, c = me
        diagonal = x == y
        up = (jnp.where(diagonal, x, 1 - x), jnp.where(diagonal, 1 - y, y), c)
        down = (jnp.where(diagonal, 1 - x, x), jnp.where(diagonal, y, 1 - y), c)
        low = c == 0
        passed = tuple(jnp.where(low, d, u) for d, u in zip(down, up))
        target = tuple(jnp.where(low, u, d) for d, u in zip(down, up))
        return up, down, (1 - x, 1 - y, c), passed, target

    def parts(refs):
        (flat_ref,), (out_ref,), (send_sems, recv_sems, local_sem) = refs
        me = _position()

        def region(name, dev):
            r = st.rows[name]
            return out_ref.at[pl.ds(st.wc[name] + _lin(dev) * r, r), :]

        def own(name):
            return flat_ref.at[pl.ds(st.fl[name], st.rows[name]), :]

        def copies(k, dev, to, from_flat):
            return [pltpu.make_async_remote_copy(
                src_ref=own(n) if from_flat else region(n, dev), dst_ref=region(n, dev), send_sem=send_sems.at[k],
                recv_sem=recv_sems.at[k], device_id=to, device_id_type=MESH) for n in names]

        def whole(k):
            return pltpu.make_async_remote_copy(
                src_ref=flat_ref.at[pl.ds(0, st.R), :], dst_ref=out_ref.at[pl.ds(0, st.R), :],
                send_sem=send_sems.at[k], recv_sem=recv_sems.at[k], device_id=me, device_id_type=MESH)

        return me, region, own, copies, whole, flat_ref, out_ref, local_sem

    def start(*refs):
        me, region, own, copies, _, _, _, local_sem = parts(refs)
        for n in names:
            pltpu.make_async_copy(own(n), region(n, me), local_sem).start()
        up, down, _, _, _ = ring(me)
        for k, to in ((1, up), (2, down), (0, _peer(me, 1))):
            for cp in copies(k, me, to, True):
                cp.start()

    def middle(*refs):
        me, _, _, copies, whole, _, _, _ = parts(refs)
        up, down, _, passed, target = ring(me)
        sib = _peer(me, 1)
        whole(1).wait_recv()
        whole(2).wait_recv()
        for k, dev, to in ((3, passed, target), (4, down, sib), (5, up, sib)):
            for cp in copies(k, dev, to, False):
                cp.start()

    def finish(*refs):
        me, _, _, copies, whole, flat_ref, out_ref, local_sem = parts(refs)
        _, _, across, _, _ = ring(me)
        whole(3).wait_recv()
        for cp in copies(6, across, _peer(me, 1), False):
            cp.start()
        whole(0).wait_recv()
        for j in range(3):
            whole(4 + j).wait_recv()
        for k in range(7):
            whole(k).wait_send()
        pltpu.make_async_copy(flat_ref.at[pl.ds(0, st.R), :], out_ref.at[pl.ds(0, st.R), :], local_sem).wait()

    return _Comm([flat], [jax.ShapeDtypeStruct((st.W, D), BF)],
                 [pltpu.SemaphoreType.DMA((7,)), pltpu.SemaphoreType.DMA((7,)), pltpu.SemaphoreType.DMA],
                 start, finish, middle)


def _rs_pair_comm(names, src):
    st = _Stage(names)
    arrays = []
    for n in names:
        if not any(src[n][0] is a for a in arrays):
            arrays.append(src[n][0])
    idx = {n: [i for i, a in enumerate(arrays) if a is src[n][0]][0] for n in names}

    def slot_wait(refs):
        recv = refs[1][0]
        send_sem, recv_sem = refs[2]
        return pltpu.make_async_remote_copy(src_ref=recv, dst_ref=recv, send_sem=send_sem, recv_sem=recv_sem,
                                            device_id=_position(), device_id_type=MESH)

    def start(*refs):
        ins, (recv,), (send_sem, recv_sem) = refs
        me = _position()
        sib = _peer(me, 1)
        for q in range(NCHIP):
            dev = (q // 2, q % 2, sib[2])
            for n in names:
                r = st.rows[n]
                pltpu.make_async_remote_copy(
                    src_ref=ins[idx[n]].at[pl.ds(st.grad_row(n, src[n][1], _lin(dev)), r), :],
                    dst_ref=recv.at[q, pl.ds(st.off[n], r), :], send_sem=send_sem, recv_sem=recv_sem,
                    device_id=sib, device_id_type=MESH).start()

    def finish(*refs):
        w = slot_wait(refs)
        w.wait_recv()
        w.wait_send()

    return _Comm(arrays, [jax.ShapeDtypeStruct((NCHIP, st.R, D), BF)],
                 [pltpu.SemaphoreType.DMA, pltpu.SemaphoreType.DMA], start, finish)


def _pair_add(names, src, recv, name):
    st = _Stage(names)
    c_arr = jnp.reshape(lax.axis_index("c"), (1,)).astype(jnp.int32)

    def body(c_ref, *refs):
        r_ref, o_ref = refs[len(names)], refs[len(names) + 1]
        for a_ref, n in zip(refs, names):
            rows = slice(st.off[n], st.off[n] + st.rows[n])
            o_ref[rows, :] = (a_ref[...].astype(F32) + r_ref[rows, :].astype(F32)).astype(BF)

    def shard_spec(n):
        r = st.rows[n]
        base, step = st.grad_row(n, src[n][1], 0) // r, st.full[n] // r
        return pl.BlockSpec((r, D), lambda q, c_ref: (base + step * (2 * q + c_ref[0]), 0))

    slot = pl.BlockSpec((None, st.R, D), lambda q, c_ref: (q, 0, 0))
    return _call(body, name=name, grid=(NCHIP,), args=[c_arr] + [src[n][0] for n in names] + [recv],
                 in_specs=[shard_spec(n) for n in names] + [slot],
                 out_shape=[jax.ShapeDtypeStruct((NCHIP, st.R, D), BF)], out_specs=[slot], num_scalar_prefetch=1)[0]


def _rs_chip_comm(part):
    def copies(refs):
        (p_ref,), (recv,), (send_sems, recv_sems, local_sem) = refs
        me = _position()
        mine = pltpu.make_async_copy(p_ref.at[_chip(me)], recv.at[_chip(me)], local_sem)
        out = []
        for j, bits in enumerate((4, 2, 6)):
            to = _peer(me, bits)
            out.append(pltpu.make_async_remote_copy(
                src_ref=p_ref.at[_chip(to)], dst_ref=recv.at[_chip(me)], send_sem=send_sems.at[j],
                recv_sem=recv_sems.at[j], device_id=to, device_id_type=MESH))
        return mine, out

    def start(*refs):
        mine, out = copies(refs)
        mine.start()
        for cp in out:
            cp.start()

    def finish(*refs):
        mine, out = copies(refs)
        for cp in out:
            cp.wait_recv()
        for cp in out:
            cp.wait_send()
        mine.wait()

    return _Comm([part], [jax.ShapeDtypeStruct(part.shape, BF)],
                 [pltpu.SemaphoreType.DMA((3,)), pltpu.SemaphoreType.DMA((3,)), pltpu.SemaphoreType.DMA],
                 start, finish)


def _direct_comm(x, scatter):
    def copies(refs):
        (x_ref,), (out_ref,), (send_sems, recv_sems, local_sem) = refs
        me = _position()

        def piece(dev):
            return x_ref.at[_lin(dev)] if scatter else x_ref

        mine = pltpu.make_async_copy(piece(me), out_ref.at[_lin(me)], local_sem)
        return mine, [pltpu.make_async_remote_copy(
            src_ref=piece(_peer(me, j)), dst_ref=out_ref.at[_lin(me)], send_sem=send_sems.at[j - 1],
            recv_sem=recv_sems.at[j - 1], device_id=_peer(me, j), device_id_type=MESH) for j in range(1, NDEV)]

    def start(*refs):
        mine, cps = copies(refs)
        mine.start()
        for cp in cps:
            cp.start()

    def finish(*refs):
        mine, cps = copies(refs)
        for cp in cps:
            cp.wait_recv()
        for cp in cps:
            cp.wait_send()
        mine.wait()

    shape = x.shape if scatter else (NDEV,) + x.shape
    return _Comm([x], [jax.ShapeDtypeStruct(shape, x.dtype)],
                 [pltpu.SemaphoreType.DMA((7,)), pltpu.SemaphoreType.DMA((7,)), pltpu.SemaphoreType.DMA],
                 start, finish)


def _pack_weights(shards):
    lay = _Layout()

    def body(*refs):
        o_ref = refs[-1]
        for ref, n in zip(refs, ORDER):
            x = ref[...].T if n == "win" else ref[...]
            o_ref[lay.fl[n]:lay.fl[n] + lay.rows[n], :] = x.astype(BF)

    return pl.pallas_call(
        body, name="pack_weights", out_shape=jax.ShapeDtypeStruct((lay.RT, D), BF),
        compiler_params=pltpu.CompilerParams(vmem_limit_bytes=VMEM_LIMIT))(*[shards[n] for n in ORDER])


def _load_ffn_weights(srcs, offs, scratch, sem):
    @pl.when(pl.program_id(0) == 0)
    def _():
        cps = [pltpu.make_async_copy(s.at[pl.ds(off, dst.shape[0]), :], dst, sem.at[i])
               for i, (s, off, dst) in enumerate(zip(srcs, offs, scratch))]
        for cp in cps:
            cp.start()
        for cp in cps:
            cp.wait()


def _final_loss_tile(xf, g, tgt, s_ref):
    r = lax.rsqrt(jnp.mean(xf * xf, axis=-1, keepdims=True) + EPS)
    xr = xf * r
    e = xr * g - tgt
    s_ref[1:2, :] += jnp.sum(e * e, axis=0, keepdims=True) * (0.5 / D)
    dy = e * (1.0 / D)
    s_ref[0:1, :] += jnp.sum(dy * xr, axis=0, keepdims=True)
    gdy = dy * g
    return r * gdy - xr * (r * jnp.mean(gdy * xr, axis=-1, keepdims=True))


def _ffn_fwd(x, g, wbufs, offs, name, comm=None, final=None):
    nf = F // FC

    def body(x_ref, g_ref, b0, b1, b2, *rest):
        if final is None:
            h_ref, n_ref, gg_ref, uu_ref, a_ref, wg_s, wu_s, wd_s, sem = rest
        else:
            gf_ref, t_ref, dh_ref, dhb_ref, s_ref, n_ref, gg_ref, uu_ref, a_ref, wg_s, wu_s, wd_s, sem = rest

            @pl.when(pl.program_id(0) == 0)
            def _():
                s_ref[...] = jnp.zeros_like(s_ref)

        _load_ffn_weights((b0, b1, b2), offs, (wg_s, wu_s, wd_s), sem)
        xf = x_ref[...]
        r = lax.rsqrt(jnp.mean(xf * xf, axis=-1, keepdims=True) + EPS)
        nb = (xf * r * g_ref[...]).astype(BF)
        n_ref[...] = nb
        acc = jnp.zeros((TM, D), F32)
        for c in range(nf):
            sl = slice(c * FC, (c + 1) * FC)
            gb = _nt(nb, wg_s[sl, :]).astype(BF)
            ub = _nt(nb, wu_s[sl, :]).astype(BF)
            gg_ref[:, sl] = gb
            uu_ref[:, sl] = ub
            a = (gb * _sig(gb)) * ub
            a_ref[0, :, sl] = a
            acc = acc + _nn(a, wd_s[sl, :])
        h = xf + 0.5 * acc
        if final is None:
            h_ref[...] = h
        else:
            dh = _final_loss_tile(h, gf_ref[...], t_ref[...], s_ref)
            dh_ref[...] = dh
            dhb_ref[...] = (0.5 * dh).astype(BF)

    row = lambda i: (i, 0)
    vec = pl.BlockSpec((1, D), lambda i: (0, 0))
    tile = pl.BlockSpec((TM, D), row)
    saved_shapes = [jax.ShapeDtypeStruct((T, D), BF), jax.ShapeDtypeStruct((T, F), BF), jax.ShapeDtypeStruct((T, F), BF),
                    jax.ShapeDtypeStruct((1, T, F), BF)]
    saved_specs = [tile, pl.BlockSpec((TM, F), row), pl.BlockSpec((TM, F), row),
                   pl.BlockSpec((1, TM, F), lambda i: (0, i, 0))]
    if final is None:
        extra_args, extra_specs = [], []
        head_shapes, head_specs = [jax.ShapeDtypeStruct((T, D), F32)], [tile]
    else:
        extra_args, extra_specs = list(final), [vec, tile]
        head_shapes = [jax.ShapeDtypeStruct((T, D), F32), jax.ShapeDtypeStruct((T, D), BF), jax.ShapeDtypeStruct((8, D), F32)]
        head_specs = [tile, tile, pl.BlockSpec((8, D), lambda i: (0, 0))]
    return _call(
        body, name=name, grid=(T // TM,), args=[x, g, *wbufs, *extra_args], comm=comm,
        in_specs=[tile, vec, ANY, ANY, ANY] + extra_specs,
        out_shape=head_shapes + saved_shapes, out_specs=head_specs + saved_specs,
        scratch_shapes=[pltpu.VMEM((F, D), BF)] * 3 + [pltpu.SemaphoreType.DMA((3,))])


def _ffn_gate_up(x, g, wbufs, offs, name, comm=None):
    nf = F // FC

    def body(x_ref, g_ref, b0, b1, n_ref, gg_ref, uu_ref, a_ref, wg_s, wu_s, sem):
        _load_ffn_weights((b0, b1), offs, (wg_s, wu_s), sem)
        xf = x_ref[...]
        r = lax.rsqrt(jnp.mean(xf * xf, axis=-1, keepdims=True) + EPS)
        nb = (xf * r * g_ref[...]).astype(BF)
        n_ref[...] = nb
        for c in range(nf):
            sl = slice(c * FC, (c + 1) * FC)
            gb = _nt(nb, wg_s[sl, :]).astype(BF)
            ub = _nt(nb, wu_s[sl, :]).astype(BF)
            gg_ref[:, sl] = gb
            uu_ref[:, sl] = ub
            a_ref[0, :, sl] = (gb * _sig(gb)) * ub

    row = lambda i: (i, 0)
    tile = pl.BlockSpec((TM, D), row)
    return _call(
        body, name=name, grid=(T // TM,), args=[x, g, *wbufs], comm=comm,
        in_specs=[tile, pl.BlockSpec((1, D), lambda i: (0, 0)), ANY, ANY],
        out_shape=[jax.ShapeDtypeStruct((T, D), BF), jax.ShapeDtypeStruct((T, F), BF), jax.ShapeDtypeStruct((T, F), BF),
                   jax.ShapeDtypeStruct((1, T, F), BF)],
        out_specs=[tile, pl.BlockSpec((TM, F), row), pl.BlockSpec((TM, F), row),
                   pl.BlockSpec((1, TM, F), lambda i: (0, i, 0))],
        scratch_shapes=[pltpu.VMEM((F, D), BF)] * 2 + [pltpu.SemaphoreType.DMA((2,))])


def _ffn_down(x, act, wbuf, off, name, comm=None):
    def body(x_ref, a_ref, b0, h_ref, wd_s, sem):
        _load_ffn_weights((b0,), (off,), (wd_s,), sem)
        h_ref[...] = x_ref[...] + 0.5 * _nn(a_ref[0], wd_s[...])

    tile = pl.BlockSpec((TM, D), lambda i: (i, 0))
    return _call(
        body, name=name, grid=(T // TM,), args=[x, act, wbuf], comm=comm,
        in_specs=[tile, pl.BlockSpec((1, TM, F), lambda i: (0, i, 0)), ANY],
        out_shape=[jax.ShapeDtypeStruct((T, D), F32)], out_specs=[tile],
        scratch_shapes=[pltpu.VMEM((F, D), BF), pltpu.SemaphoreType.DMA((1,))])


def _load_in_proj(parts, w_s, sem):
    @pl.when(pl.program_id(0) == 0)
    def _():
        shard = NG * D // NDEV
        rows = shard // len(parts)
        cps = [pltpu.make_async_copy(buf.at[pl.ds(first + k * rows, rows), :],
                                     w_s.at[pl.ds(k * shard + p * rows, rows), :], sem.at[p * NDEV + k])
               for p, (buf, first) in enumerate(parts) for k in range(NDEV)]
        for cp in cps:
            cp.start()
        for cp in cps:
            cp.wait()


def _mix_in(h1, gm, win, comm=None):
    def body(h_ref, g_ref, *rest):
        w_any, (u_ref, z_ref, w_s, sem) = rest[:len(win)], rest[len(win):]
        _load_in_proj([(b, first) for b, (_, first) in zip(w_any, win)], w_s, sem)
        xf = h_ref[...]
        r = lax.rsqrt(jnp.mean(xf * xf, axis=-1, keepdims=True) + EPS)
        ub = (xf * r * g_ref[...]).astype(BF)
        u_ref[...] = ub
        for j in range(NG):
            z_ref[j] = _nt(ub, w_s[j * D:(j + 1) * D, :]).astype(BF)

    row = lambda i: (i, 0)
    return _call(
        body, name="mix_in", grid=(T // TM,), args=[h1, gm] + [b for b, _ in win], comm=comm,
        in_specs=[pl.BlockSpec((TM, D), row), pl.BlockSpec((1, D), lambda i: (0, 0))] + [ANY] * len(win),
        out_shape=[jax.ShapeDtypeStruct((T, D), BF), jax.ShapeDtypeStruct((NG, T, D), BF)],
        out_specs=[pl.BlockSpec((TM, D), row), pl.BlockSpec((NG, TM, D), lambda i: (0, i, 0))],
        scratch_shapes=[pltpu.VMEM((NG * D, D), BF), pltpu.SemaphoreType.DMA((NDEV * len(win),))])


def _shift_up(w, b):
    return w if b == 0 else pltpu.roll(w, w.shape[0] - b, 0)


def _fold8(p):
    red = p[0:8, :]
    for i in range(1, p.shape[0] // 8):
        red = red + p[8 * i:8 * i + 8, :]
    return red


def _dft_constants():
    import numpy as np
    nh = NB // 2
    f, n = np.arange(nh)[:, None], np.arange(NB)[None, :]
    ang = 2.0 * np.pi / NB * f * n
    fc = np.cos(ang)
    fs = np.where(f == 0, (-1.0) ** n, np.sin(ang))
    scale = np.where(f == 0, 1.0, 2.0) / NB
    ic = (scale * np.cos(ang)).T
    isn = np.where(f == 0, (-1.0) ** n / NB, scale * np.sin(ang)).T
    d = (KA - 1 - np.arange(32))[None, :]
    valid = (np.arange(32) < KA)[None, :]
    angk = 2.0 * np.pi / NB * f * d
    kc = np.where(valid, np.cos(angk), 0.0)
    ks = np.where(valid, np.sin(angk), 0.0)
    k2 = np.where(valid, np.where(f == 0, (-1.0) ** d, np.cos(angk)), 0.0)
    rtc = np.where(valid, scale * np.cos(angk), 0.0).T
    rts = np.where(valid, np.where(f == 0, (-1.0) ** d / NB, scale * np.sin(angk)), 0.0).T

    def bf(a):
        return jnp.asarray(a, F32).astype(BF)

    def split(a):
        hi = bf(a)
        return hi, (jnp.asarray(a, F32) - hi.astype(F32)).astype(BF)

    return dict(fc=bf(fc), fs=bf(fs), ic_hi=bf(ic[HB:]), is_hi=bf(isn[HB:]), ic_lo=bf(ic[:HB]), is_lo=bf(isn[:HB]),
                kc=split(kc), ks=split(ks), k2=split(k2), rtc=split(rtc), rts=split(rts))


def _dot3(m_hi, m_lo, x):
    x_hi = x.astype(BF)
    x_lo = (x - x_hi.astype(F32)).astype(BF)
    return _nn(m_hi, x_hi) + _nn(m_hi, x_lo) + _nn(m_lo, x_hi)


def _whole(a):
    return pl.BlockSpec(a.shape, lambda c, t: (0,) * a.ndim)


def _filter_spectrum(cw_ref, tabs, hc, hs, h2):
    w32 = cw_ref[0:32, :]
    for (hi, lo), dst in zip(tabs, (hc, hs, h2)):
        dst[...] = _dot3(hi[...], lo[...], w32)


def _conv_fwd_dft(z, cw, bias, dft, comm=None):
    nt = T // TB
    hb = TB // HB

    def body(z_ref, zh_ref, cw_ref, b_ref, fc_ref, fs_ref, ic_ref, is_ref, kch, kcl, ksh, ksl, k2h, k2l,
             a1_ref, q_ref, aext, ppad, hc, hs, h2):
        first = pl.program_id(1) == 0
        f = lambda ref, j: ref[j].astype(F32)

        @pl.when(first)
        def _():
            _filter_spectrum(cw_ref, ((kch, kcl), (ksh, ksl), (k2h, k2l)), hc, hs, h2)

        aext[0:HB, :] = jnp.where(first, 0.0, f(zh_ref, 0) * _sig(f(zh_ref, 1))).astype(BF)
        aext[HB:, :] = (f(z_ref, 0) * _sig(f(z_ref, 1))).astype(BF)
        ppad[0:8, :] = jnp.where(first, 0.0, f(zh_ref, 3)[HB - 8:HB, :] * f(zh_ref, 4)[HB - 8:HB, :])
        ppad[8:, :] = f(z_ref, 3) * f(z_ref, 4)
        bias_row = b_ref[...]

        for j in range(TB // HB):
            xs = aext[j * HB:j * HB + NB, :]
            xa, xb = _nn(fc_ref[...], xs), _nn(fs_ref[...], xs)
            yc = (hc[...] * xa - hs[...] * xb).astype(BF)
            ys = (h2[...] * xb + hs[...] * xa).astype(BF)
            y = _nn(ic_ref[...], yc) + _nn(is_ref[...], ys)
            a1_ref[j * HB:(j + 1) * HB, :] = (y + bias_row).astype(BF)

        def chunk(r, carry):
            base = pl.multiple_of(r * CHB, CHB)
            pw = ppad[pl.ds(base, CHB + 8), :]
            v = (cw_ref[pl.ds(32, 1), :] * _shift_up(pw, 6)[0:CHB, :]
                 + cw_ref[pl.ds(33, 1), :] * _shift_up(pw, 7)[0:CHB, :]
                 + cw_ref[pl.ds(34, 1), :] * pw[8:8 + CHB, :])
            q_ref[pl.ds(base, CHB), :] = (z_ref[2, pl.ds(base, CHB), :].astype(F32) * v).astype(BF)
            return carry

        lax.fori_loop(0, TB // CHB, chunk, 0)

    blk = pl.BlockSpec((TB, CW), lambda c, t: (t, c))
    tabs = [dft["fc"], dft["fs"], dft["ic_hi"], dft["is_hi"], *dft["kc"], *dft["ks"], *dft["k2"]]
    return _call(
        body, name="conv_fwd", grid=(D // CW, nt), comm=comm, args=[z, z, cw, bias] + tabs,
        in_specs=[pl.BlockSpec((5, TB, CW), lambda c, t: (0, t, c)),
                  pl.BlockSpec((5, HB, CW), lambda c, t: (0, jnp.maximum(t * hb - 1, 0), c)),
                  pl.BlockSpec((40, CW), lambda c, t: (0, c)), pl.BlockSpec((1, CW), lambda c, t: (0, c))]
                 + [_whole(a) for a in tabs],
        out_shape=[jax.ShapeDtypeStruct((T, D), BF), jax.ShapeDtypeStruct((T, D), BF)], out_specs=[blk, blk],
        scratch_shapes=[pltpu.VMEM((TB + HB, CW), BF), pltpu.VMEM((TB + 8, CW), F32)]
                       + [pltpu.VMEM((NB // 2, CW), F32)] * 3)


def _conv_bwd_dft(z, da1, dq, dzg, cw, dft, comm=None):
    nt = T // TB
    hb = TB // HB
    last_h = T // HB - 1

    def body(z_ref, zp_ref, zn_ref, da1_ref, da1n_ref, dq_ref, dqn_ref, dzg_ref, cw_ref,
             fc_ref, fs_ref, ic_ref, is_ref, kch, kcl, ksh, ksl, k2h, k2l, rch, rcl, rsh, rsl,
             dz_ref, dwa_ref, dwb_ref, aext, dyext, ppad, dvpad, hc, hs, h2, rc, rs, nyq, acc_b):
        t = pl.program_id(1)
        first, last = t == 0, t == nt - 1
        f = lambda ref, j: ref[j].astype(F32)

        @pl.when(first)
        def _():
            _filter_spectrum(cw_ref, ((kch, kcl), (ksh, ksl), (k2h, k2l)), hc, hs, h2)
            rc[...] = jnp.zeros_like(rc)
            rs[...] = jnp.zeros_like(rs)
            nyq[...] = jnp.zeros_like(nyq)
            acc_b[...] = jnp.zeros_like(acc_b)

        aext[0:HB, :] = jnp.where(first, 0.0, f(zp_ref, 0) * _sig(f(zp_ref, 1))).astype(BF)
        aext[HB:, :] = (f(z_ref, 0) * _sig(f(z_ref, 1))).astype(BF)
        dyext[0:TB, :] = da1_ref[...]
        dyext[TB:, :] = jnp.where(last, 0.0, da1n_ref[...].astype(F32)).astype(BF)
        ppad[0:8, :] = jnp.where(first, 0.0, f(zp_ref, 3)[HB - 8:HB, :] * f(zp_ref, 4)[HB - 8:HB, :])
        ppad[8:, :] = f(z_ref, 3) * f(z_ref, 4)
        dvpad[0:TB, :] = dq_ref[...].astype(F32) * f(z_ref, 2)
        dvpad[TB:, :] = jnp.where(last, 0.0, dqn_ref[...].astype(F32)[0:8, :] * f(zn_ref, 2)[0:8, :])

        for j in range(TB // HB):
            rows = slice(j * HB, (j + 1) * HB)
            dys = dyext[j * HB:j * HB + NB, :]
            da, db = _nn(fc_ref[...], dys), _nn(fs_ref[...], dys)
            gc = (hc[...] * da + hs[...] * db).astype(BF)
            gs = (h2[...] * db - hs[...] * da).astype(BF)
            da0 = _nn(ic_ref[...], gc) + _nn(is_ref[...], gs)
            z0, z1 = z_ref[0, rows, :].astype(F32), z_ref[1, rows, :].astype(F32)
            s1 = _sig(z1)
            dz_ref[0, rows, :] = (da0 * s1).astype(BF)
            dz_ref[1, rows, :] = (da0 * z0 * (s1 * (1.0 - s1))).astype(BF)
            xs = aext[j * HB:j * HB + NB, :]
            xa, xb = _nn(fc_ref[...], xs), _nn(fs_ref[...], xs)
            dyb = dyext[rows, :]
            pa, pb = _nn(fc_ref[:, HB:NB], dyb), _nn(fs_ref[:, HB:NB], dyb)
            rc[...] += pa * xa + pb * xb
            rs[...] += pb * xa - pa * xb
            nyq[...] += pb[0:8, :] * xb[0:8, :]

        def chunk(r, carry):
            base = pl.multiple_of(r * CHB, CHB)
            rows = pl.ds(base, CHB)
            pw = ppad[pl.ds(base, CHB + 8), :]
            p6 = _shift_up(pw, 6)[0:CHB, :]
            p7 = _shift_up(pw, 7)[0:CHB, :]
            p8 = pw[8:8 + CHB, :]
            wb0, wb1, wb2 = cw_ref[pl.ds(32, 1), :], cw_ref[pl.ds(33, 1), :], cw_ref[pl.ds(34, 1), :]
            v = wb0 * p6 + wb1 * p7 + wb2 * p8
            dz_ref[2, rows, :] = (dq_ref[rows, :].astype(F32) * v).astype(BF)
            dvw = dvpad[pl.ds(base, CHB + 8), :]
            dvc = dvw[0:CHB, :]
            dp = wb2 * dvc + wb1 * _shift_up(dvw, 1)[0:CHB, :] + wb0 * _shift_up(dvw, 2)[0:CHB, :]
            dz_ref[3, rows, :] = (dp * z_ref[4, rows, :].astype(F32)).astype(BF)
            dz_ref[4, rows, :] = (dp * z_ref[3, rows, :].astype(F32)).astype(BF)
            acc_b[0:8, :] += _fold8(dvc * p6)
            acc_b[8:16, :] += _fold8(dvc * p7)
            acc_b[16:24, :] += _fold8(dvc * p8)
            dz_ref[5, rows, :] = dzg_ref[0, rows, :]
            dz_ref[6, rows, :] = dzg_ref[1, rows, :]
            return carry

        lax.fori_loop(0, TB // CHB, chunk, 0)

        @pl.when(last)
        def _():
            row0 = lax.broadcasted_iota(jnp.int32, (NB // 2, CW), 0) == 0
            ny = jnp.broadcast_to(nyq[0:1, :], (NB // 2, CW))
            rcv = jnp.where(row0, rc[...] - ny, rc[...])
            rsv = jnp.where(row0, ny, rs[...])
            dwa_ref[...] = _dot3(rch[...], rcl[...], rcv) + _dot3(rsh[...], rsl[...], rsv)
            for k in range(KB):
                dwb_ref[k:k + 1, :] = jnp.sum(acc_b[8 * k:8 * k + 8, :], axis=0, keepdims=True)
            dwb_ref[KB:8, :] = jnp.zeros((8 - KB, CW), F32)

    blk = lambda c, t: (t, c)
    nxt = lambda c, t: (jnp.minimum((t + 1) * hb, last_h), c)
    tabs = [dft["fc"], dft["fs"], dft["ic_lo"], dft["is_lo"], *dft["kc"], *dft["ks"], *dft["k2"], *dft["rtc"], *dft["rts"]]
    return _call(
        body, name="conv_bwd", grid=(D // CW, nt), comm=comm, args=[z, z, z, da1, da1, dq, dq, dzg, cw] + tabs,
        in_specs=[pl.BlockSpec((5, TB, CW), lambda c, t: (0, t, c)),
                  pl.BlockSpec((5, HB, CW), lambda c, t: (0, jnp.maximum(t * hb - 1, 0), c)),
                  pl.BlockSpec((5, HB, CW), lambda c, t: (0, jnp.minimum((t + 1) * hb, last_h), c)),
                  pl.BlockSpec((TB, CW), blk), pl.BlockSpec((HB, CW), nxt),
                  pl.BlockSpec((TB, CW), blk), pl.BlockSpec((HB, CW), nxt),
                  pl.BlockSpec((2, TB, CW), lambda c, t: (0, t, c)),
                  pl.BlockSpec((40, CW), lambda c, t: (0, c))]
                 + [_whole(a) for a in tabs],
        out_shape=[jax.ShapeDtypeStruct((NG, T, D), BF), jax.ShapeDtypeStruct((32, D), F32),
                   jax.ShapeDtypeStruct((8, D), F32)],
        out_specs=[pl.BlockSpec((NG, TB, CW), lambda c, t: (0, t, c)),
                   pl.BlockSpec((32, CW), lambda c, t: (0, c)), pl.BlockSpec((8, CW), lambda c, t: (0, c))],
        scratch_shapes=[pltpu.VMEM((TB + HB, CW), BF), pltpu.VMEM((TB + HB, CW), BF),
                        pltpu.VMEM((TB + 8, CW), F32), pltpu.VMEM((TB + 8, CW), F32)]
                       + [pltpu.VMEM((NB // 2, CW), F32)] * 5 + [pltpu.VMEM((8, CW), F32), pltpu.VMEM((24, CW), F32)])


def _layernorm_silu(a1, lng, lnb):
    mu = jnp.mean(a1, axis=-1, keepdims=True)
    xc = a1 - mu
    rs = lax.rsqrt(jnp.mean(xc * xc, axis=-1, keepdims=True) + EPS)
    xh = xc * rs
    a2 = xh * lng + lnb
    sg = _sig(a2)
    return xh, rs, a2, sg


def _square_specs(blocks):
    return [pl.BlockSpec((D, D), lambda i, b=b: (b, 0)) for b in blocks]


def _mix_out(a1, q, z, h1, lng, lnb, wsq, comm=None):
    def body(a1_ref, q_ref, ga_ref, gb_ref, h_ref, lng_ref, lnb_ref, wa_ref, wb_ref, wo_ref, h2_ref, ya_ref, yb_ref):
        _, _, a2, sg = _layernorm_silu(a1_ref[...].astype(F32), lng_ref[...], lnb_ref[...])
        ya = _nn((a2 * sg).astype(BF), wa_ref[...])
        yb = _nn(q_ref[...], wb_ref[...])
        ya_ref[...] = ya.astype(BF)
        yb_ref[...] = yb.astype(BF)
        m = _sig(ga_ref[...].astype(F32)) * ya + _sig(gb_ref[...].astype(F32)) * yb
        h2_ref[...] = h_ref[...] + _nn(m.astype(BF), wo_ref[...])

    row = lambda i: (i, 0)
    vec = pl.BlockSpec((1, D), lambda i: (0, 0))
    return _call(
        body, name="mix_out", grid=(T // TM,), args=[a1, q, z, z, h1, lng, lnb, wsq, wsq, wsq], comm=comm,
        in_specs=[pl.BlockSpec((TM, D), row), pl.BlockSpec((TM, D), row),
                  pl.BlockSpec((None, TM, D), lambda i: (5, i, 0)), pl.BlockSpec((None, TM, D), lambda i: (6, i, 0)),
                  pl.BlockSpec((TM, D), row), vec, vec] + _square_specs((0, 1, 2)),
        out_shape=[jax.ShapeDtypeStruct((T, D), F32), jax.ShapeDtypeStruct((T, D), BF), jax.ShapeDtypeStruct((T, D), BF)],
        out_specs=[pl.BlockSpec((TM, D), row)] * 3)


def _rmsnorm_bwd(xf, g, dn):
    r = lax.rsqrt(jnp.mean(xf * xf, axis=-1, keepdims=True) + EPS)
    xr = xf * r
    gdn = dn * g
    dx = r * gdn - xr * (r * jnp.mean(gdn * xr, axis=-1, keepdims=True))
    return dx, jnp.sum(dn * xr, axis=0, keepdims=True)


def _ffn_bwd_hidden(dh, gg, uu, wbuf, off, name, comm=None):
    nf = F // FC

    def body(dh_ref, gg_ref, uu_ref, b0, dgu_ref, wd_s, sem):
        _load_ffn_weights((b0,), (off,), (wd_s,), sem)
        dhb = dh_ref[...]
        for c in range(nf):
            sl = slice(c * FC, (c + 1) * FC)
            da = _nt(dhb, wd_s[sl, :]).astype(BF)
            gb, ub = gg_ref[:, sl], uu_ref[:, sl]
            sg = _sig(gb)
            dgu_ref[0, :, sl] = (da * ub) * (sg * (1.0 + gb * (1.0 - sg)))
            dgu_ref[0, :, F + c * FC:F + (c + 1) * FC] = da * (gb * sg)

    row = lambda i: (i, 0)
    return _call(
        body, name=name, grid=(T // TM,), args=[dh, gg, uu, wbuf], comm=comm,
        in_specs=[pl.BlockSpec((TM, D), row), pl.BlockSpec((TM, F), row), pl.BlockSpec((TM, F), row), ANY],
        out_shape=[jax.ShapeDtypeStruct((1, T, 2 * F), BF)],
        out_specs=[pl.BlockSpec((1, TM, 2 * F), lambda i: (0, i, 0))],
        scratch_shapes=[pltpu.VMEM((F, D), BF), pltpu.SemaphoreType.DMA((1,))])


def _ffn_bwd_input(dgu, dh, x, g, wbufs, offs, name, comm=None, after=None):
    def body(dgu_ref, dh_ref, x_ref, g_ref, b0, b1, dx_ref, s_ref, w_s, sem):
        _load_ffn_weights((b0, b1), offs, (w_s.at[pl.ds(0, F), :], w_s.at[pl.ds(F, F), :]), sem)

        @pl.when(pl.program_id(0) == 0)
        def _():
            s_ref[...] = jnp.zeros_like(s_ref)

        dn = _nn(dgu_ref[0], w_s[...])
        dxn, dg = _rmsnorm_bwd(x_ref[...], g_ref[...], dn)
        dx_ref[...] = dh_ref[...] + dxn
        s_ref[0:1, :] += dg

    row = lambda i: (i, 0)
    return _call(
        body, name=name, grid=(T // TM,), args=[dgu, dh, x, g, *wbufs], comm=comm, after=after,
        in_specs=[pl.BlockSpec((1, TM, 2 * F), lambda i: (0, i, 0)), pl.BlockSpec((TM, D), row),
                  pl.BlockSpec((TM, D), row), pl.BlockSpec((1, D), lambda i: (0, 0)), ANY, ANY],
        out_shape=[jax.ShapeDtypeStruct((T, D), F32), jax.ShapeDtypeStruct((8, D), F32)],
        out_specs=[pl.BlockSpec((TM, D), row), pl.BlockSpec((8, D), lambda i: (0, 0))],
        scratch_shapes=[pltpu.VMEM((2 * F, D), BF), pltpu.SemaphoreType.DMA((2,))])


def _tn_matmul(lhs, rhs, tr, name, comm=None):
    ng, _, cdim = lhs.shape
    nc, nk = cdim // tr, T // TK
    if rhs.ndim == 2:
        r_spec = pl.BlockSpec((TK, D), lambda g, c, k: (k, 0))
    else:
        r_spec = pl.BlockSpec((None, TK, D), lambda g, c, k: (g, k, 0))

    def body(l_ref, r_ref, o_ref, acc):
        k = pl.program_id(2)

        @pl.when(k == 0)
        def _():
            acc[...] = jnp.zeros_like(acc)

        acc[...] += _tn(l_ref[...], r_ref[...])

        @pl.when(k == nk - 1)
        def _():
            o_ref[...] = acc[...].astype(BF)

    return _call(
        body, name=name, grid=(ng, nc, nk), args=[lhs, rhs], comm=comm,
        in_specs=[pl.BlockSpec((None, TK, tr), lambda g, c, k: (g, k, c)), r_spec],
        out_shape=[jax.ShapeDtypeStruct((ng * cdim, D), BF)],
        out_specs=[pl.BlockSpec((tr, D), lambda g, c, k: (g * nc + c, 0))],
        scratch_shapes=[pltpu.VMEM((tr, D), F32)])


def _mix_out_bwd(dh2, ya, yb, z, a1, q, lng, lnb, wsq, comm=None):
    def body(dh_ref, ya_ref, yb_ref, ga_ref, gb_ref, a1_ref, q_ref, lng_ref, lnb_ref, wa_ref, wb_ref, wo_ref,
             dzg_ref, da1_ref, dq_ref, l_ref, r_ref, s_ref):
        @pl.when(pl.program_id(0) == 0)
        def _():
            s_ref[...] = jnp.zeros_like(s_ref)

        dhb = dh_ref[...].astype(BF)
        dm = _nt(dhb, wo_ref[...]).astype(BF)
        ya, yb = ya_ref[...], yb_ref[...]
        sa, sb = _sig(ga_ref[...]), _sig(gb_ref[...])
        l_ref[0] = sa * ya + sb * yb
        l_ref[2] = q_ref[...]
        dzg_ref[0] = (dm * ya) * (sa * (1.0 - sa))
        dzg_ref[1] = (dm * yb) * (sb * (1.0 - sb))
        dya = dm * sa
        dyb = dm * sb
        r_ref[0] = dhb
        r_ref[1] = dya
        r_ref[2] = dyb
        dq_ref[...] = _nt(dyb, wb_ref[...]).astype(BF)
        da3 = _nt(dya, wa_ref[...])
        lng = lng_ref[...]
        xh, rs, a2, sg = _layernorm_silu(a1_ref[...].astype(F32), lng, lnb_ref[...])
        l_ref[1] = (a2 * sg).astype(BF)
        da2 = da3 * (sg * (1.0 + a2 * (1.0 - sg)))
        s_ref[0:1, :] += jnp.sum(da2 * xh, axis=0, keepdims=True)
        s_ref[1:2, :] += jnp.sum(da2, axis=0, keepdims=True)
        dxh = da2 * lng
        da1 = rs * (dxh - jnp.mean(dxh, axis=-1, keepdims=True) - xh * jnp.mean(dxh * xh, axis=-1, keepdims=True))
        da1_ref[...] = da1.astype(BF)
        s_ref[2:3, :] += jnp.sum(da1, axis=0, keepdims=True)

    row = lambda i: (i, 0)
    row3 = lambda i: (0, i, 0)
    vec = pl.BlockSpec((1, D), lambda i: (0, 0))
    return _call(
        body, name="mix_out_bwd", grid=(T // TM,), args=[dh2, ya, yb, z, z, a1, q, lng, lnb, wsq, wsq, wsq], comm=comm,
        in_specs=[pl.BlockSpec((TM, D), row), pl.BlockSpec((TM, D), row), pl.BlockSpec((TM, D), row),
                  pl.BlockSpec((None, TM, D), lambda i: (5, i, 0)), pl.BlockSpec((None, TM, D), lambda i: (6, i, 0)),
                  pl.BlockSpec((TM, D), row), pl.BlockSpec((TM, D), row), vec, vec] + _square_specs((0, 1, 2)),
        out_shape=[jax.ShapeDtypeStruct((2, T, D), BF), jax.ShapeDtypeStruct((T, D), BF),
                   jax.ShapeDtypeStruct((T, D), BF), jax.ShapeDtypeStruct((3, T, D), BF),
                   jax.ShapeDtypeStruct((3, T, D), BF), jax.ShapeDtypeStruct((8, D), F32)],
        out_specs=[pl.BlockSpec((2, TM, D), row3), pl.BlockSpec((TM, D), row), pl.BlockSpec((TM, D), row),
                   pl.BlockSpec((3, TM, D), row3), pl.BlockSpec((3, TM, D), row3), pl.BlockSpec((8, D), lambda i: (0, 0))])


def _mix_in_bwd(dz, dh2, h1, gm, win, comm=None):
    def body(dz_ref, dh_ref, h_ref, g_ref, *rest):
        w_any, (o_ref, ob_ref, s_ref, w_s, sem) = rest[:len(win)], rest[len(win):]
        _load_in_proj([(b, first) for b, (_, first) in zip(w_any, win)], w_s, sem)

        @pl.when(pl.program_id(0) == 0)
        def _():
            s_ref[...] = jnp.zeros_like(s_ref)

        du = _nn(dz_ref[0], w_s[0:D, :])
        for j in range(1, NG):
            du = du + _nn(dz_ref[j], w_s[j * D:(j + 1) * D, :])
        dx, dg = _rmsnorm_bwd(h_ref[...], g_ref[...], du)
        dh1 = dh_ref[...] + dx
        o_ref[...] = dh1
        ob_ref[...] = (0.5 * dh1).astype(BF)
        s_ref[0:1, :] += dg

    row = lambda i: (i, 0)
    return _call(
        body, name="mix_in_bwd", grid=(T // TM,), args=[dz, dh2, h1, gm] + [b for b, _ in win], comm=comm,
        in_specs=[pl.BlockSpec((NG, TM, D), lambda i: (0, i, 0)), pl.BlockSpec((TM, D), row),
                  pl.BlockSpec((TM, D), row), pl.BlockSpec((1, D), lambda i: (0, 0))] + [ANY] * len(win),
        out_shape=[jax.ShapeDtypeStruct((T, D), F32), jax.ShapeDtypeStruct((T, D), BF), jax.ShapeDtypeStruct((8, D), F32)],
        out_specs=[pl.BlockSpec((TM, D), row), pl.BlockSpec((TM, D), row), pl.BlockSpec((8, D), lambda i: (0, 0))],
        scratch_shapes=[pltpu.VMEM((NG * D, D), BF), pltpu.SemaphoreType.DMA((NDEV * len(win),))])


def _row_tile(n, want, mult):
    for t in range(min(want, n), 0, -1):
        if n % t == 0 and t % mult == 0:
            return t
    return n


def _pack_small(s_ffn1, s_in, s_mix, s_ffn2, s_final, dwa, dwb):
    def body(f1, mi, mo, f2, fl, wa_ref, wb_ref, v_ref, k_ref):
        for dst, (ref, row) in enumerate(((f1, 0), (mi, 0), (mo, 0), (mo, 1), (mo, 2), (f2, 0), (fl, 0), (fl, 1))):
            v_ref[dst:dst + 1, :] = ref[row:row + 1, :]
        for k in range(NDEV):
            k_ref[k, 0:32, :] = wa_ref[:, k * LANE:(k + 1) * LANE]
            k_ref[k, 32:40, :] = wb_ref[:, k * LANE:(k + 1) * LANE]

    return pl.pallas_call(
        body, name="pack_small",
        out_shape=(jax.ShapeDtypeStruct((8, D), F32), jax.ShapeDtypeStruct((NDEV, 40, LANE), F32)),
    )(s_ffn1, s_in, s_mix, s_ffn2, s_final, dwa, dwb)


def _adam_update(g, w, m, v):
    m2 = ADAM_B1 * m + (1.0 - ADAM_B1) * g
    v2 = ADAM_B2 * v + (1.0 - ADAM_B2) * (g * g)
    c1 = 1.0 - ADAM_B1 ** ADAM_STEP
    c2 = 1.0 - ADAM_B2 ** ADAM_STEP
    return -ADAM_LR * ((m2 / c1) / (jnp.sqrt(v2 / c2) + ADAM_EPS) + ADAM_WD * w), m2, v2


def _adam_small(vecs, convs, vec_params, tap_params):
    nv, nt = len(vec_params), len(tap_params)

    def body(*refs):
        v_ref, k_ref = refs[:2]
        p_refs = refs[2:2 + 3 * (nv + nt)]
        l_ref = refs[2 + 3 * (nv + nt)]
        o_refs = refs[3 + 3 * (nv + nt):]
        s, c = v_ref[0], k_ref[0]
        for k in range(1, NDEV):
            s = s + v_ref[k]
            c = c + k_ref[k]
        l_ref[...] = jnp.sum(s[7:8, :], axis=-1, keepdims=True)
        for i in range(nv):
            w_ref, m_ref, u_ref = p_refs[3 * i: 3 * i + 3]
            g_ref, d_ref, m2_ref, u2_ref = o_refs[4 * i: 4 * i + 4]
            g = s[i:i + 1, :]
            g_ref[...] = g
            d_ref[...], m2_ref[...], u2_ref[...] = _adam_update(g, w_ref[...], m_ref[...], u_ref[...])
        for i in range(nt):
            w_ref, m_ref, u_ref = p_refs[3 * (nv + i): 3 * (nv + i) + 3]
            g_ref, d_ref, m2_ref, u2_ref = o_refs[4 * (nv + i): 4 * (nv + i) + 4]
            first = tap_params[i][0]
            for k in range(w_ref.shape[0]):
                g = c[first + k:first + k + 1, :]
                g_ref[k] = g
                d_ref[k], m2_ref[k], u2_ref[k] = _adam_update(g, w_ref[k], m_ref[k], u_ref[k])

    params = [a for p in vec_params for a in p] + [a for p in tap_params for a in p[1:]]
    out_shape = [jax.ShapeDtypeStruct((1, 1), F32)]
    for p in list(vec_params) + [p[1:] for p in tap_params]:
        out_shape += [jax.ShapeDtypeStruct(p[0].shape, F32)] * 4
    outs = pl.pallas_call(body, name="adam_small", out_shape=tuple(out_shape))(vecs, convs, *params)
    groups = [tuple(outs[1 + 4 * i: 5 + 4 * i]) for i in range(nv + nt)]
    return outs[0], groups[:nv], groups[nv:]


def _adam_in_proj(parts, w, m, v, after):
    rows = w.shape[1]
    tr = _row_tile(D, 256, LANE)

    def body(*refs):
        p_refs = refs[:len(parts)]
        w_ref, m_ref, v_ref, g_ref, d_ref, m2_ref, v2_ref = refs[len(parts):]
        sums = []
        for p in p_refs:
            s = p[0].astype(F32)
            for k in range(1, p.shape[0]):
                s = s + p[k].astype(F32)
            sums.append(s)
        g = jnp.concatenate(sums, axis=0).T
        g_ref[...] = g
        d_ref[...], m2_ref[...], v2_ref[...] = _adam_update(g, w_ref[...], m_ref[...], v_ref[...])

    spec = pl.BlockSpec((tr, rows), lambda i: (i, 0))
    return _call(body, name="adam_in", grid=(D // tr,), args=list(parts) + [w, m, v], after=after,
                 in_specs=[pl.BlockSpec((p.shape[0], p.shape[1], tr), lambda i: (0, 0, i)) for p in parts] + [spec] * 3,
                 out_shape=[jax.ShapeDtypeStruct((D, rows), F32)] * 4, out_specs=[spec] * 4)


def _adam(gs, ws, ms, vs, name, after):
    n = len(gs)
    rows, cols = ws[0].shape
    tr = _row_tile(rows, 256, 16)

    def body(*refs):
        for i in range(n):
            g_in, w, m, v = refs[4 * i], refs[4 * i + 1][...], refs[4 * i + 2][...], refs[4 * i + 3][...]
            g_ref, d_ref, m_ref, v_ref = refs[4 * n + 4 * i: 4 * n + 4 * i + 4]
            g = g_in[0].astype(F32)
            for k in range(1, g_in.shape[0]):
                g = g + g_in[k].astype(F32)
            g_ref[...] = g
            d_ref[...], m_ref[...], v_ref[...] = _adam_update(g, w, m, v)

    spec = pl.BlockSpec((tr, cols), lambda i: (i, 0))
    args, in_specs = [], []
    for i in range(n):
        slots, first = gs[i]
        args += [slots, ws[i], ms[i], vs[i]]
        in_specs += [pl.BlockSpec((slots.shape[0], tr, cols), lambda i, b=first // tr: (0, b + i, 0))] + [spec] * 3
    outs = _call(body, name=name, grid=(rows // tr,), args=args, in_specs=in_specs, after=after,
                 out_shape=[jax.ShapeDtypeStruct((rows, cols), F32)] * (4 * n), out_specs=[spec] * (4 * n))
    return [tuple(outs[4 * i: 4 * i + 4]) for i in range(n)]


def kernel(x, ffn1_norm, ffn1_w_gate, ffn1_w_up, ffn1_w_down, mix_norm, w_in, a_dw_w, a_dw_b, a_ln_g, a_ln_b, a_w_out, b_conv_w, b_w_out, w_o, ffn2_norm, ffn2_w_gate, ffn2_w_up, ffn2_w_down, final_norm, loss_target, m_ffn1_norm, m_ffn1_w_gate, m_ffn1_w_up, m_ffn1_w_down, m_mix_norm, m_w_in, m_a_dw_w, m_a_dw_b, m_a_ln_g, m_a_ln_b, m_a_w_out, m_b_conv_w, m_b_w_out, m_w_o, m_ffn2_norm, m_ffn2_w_gate, m_ffn2_w_up, m_ffn2_w_down, m_final_norm, v_ffn1_norm, v_ffn1_w_gate, v_ffn1_w_up, v_ffn1_w_down, v_mix_norm, v_w_in, v_a_dw_w, v_a_dw_b, v_a_ln_g, v_a_ln_b, v_a_w_out, v_b_conv_w, v_b_w_out, v_w_o, v_ffn2_norm, v_ffn2_w_gate, v_ffn2_w_up, v_ffn2_w_down, v_final_norm):
    names = ("ffn1_norm", "ffn1_w_gate", "ffn1_w_up", "ffn1_w_down", "mix_norm", "w_in", "a_dw_w", "a_dw_b",
             "a_ln_g", "a_ln_b", "a_w_out", "b_conv_w", "b_w_out", "w_o", "ffn2_norm", "ffn2_w_gate", "ffn2_w_up",
             "ffn2_w_down", "final_norm")
    w = dict(ffn1_norm=ffn1_norm, ffn1_w_gate=ffn1_w_gate, ffn1_w_up=ffn1_w_up, ffn1_w_down=ffn1_w_down,
             mix_norm=mix_norm, w_in=w_in, a_dw_w=a_dw_w, a_dw_b=a_dw_b, a_ln_g=a_ln_g, a_ln_b=a_ln_b,
             a_w_out=a_w_out, b_conv_w=b_conv_w, b_w_out=b_w_out, w_o=w_o, ffn2_norm=ffn2_norm,
             ffn2_w_gate=ffn2_w_gate, ffn2_w_up=ffn2_w_up, ffn2_w_down=ffn2_w_down, final_norm=final_norm)
    m = dict(ffn1_norm=m_ffn1_norm, ffn1_w_gate=m_ffn1_w_gate, ffn1_w_up=m_ffn1_w_up, ffn1_w_down=m_ffn1_w_down,
             mix_norm=m_mix_norm, w_in=m_w_in, a_dw_w=m_a_dw_w, a_dw_b=m_a_dw_b, a_ln_g=m_a_ln_g, a_ln_b=m_a_ln_b,
             a_w_out=m_a_w_out, b_conv_w=m_b_conv_w, b_w_out=m_b_w_out, w_o=m_w_o, ffn2_norm=m_ffn2_norm,
             ffn2_w_gate=m_ffn2_w_gate, ffn2_w_up=m_ffn2_w_up, ffn2_w_down=m_ffn2_w_down, final_norm=m_final_norm)
    v = dict(ffn1_norm=v_ffn1_norm, ffn1_w_gate=v_ffn1_w_gate, ffn1_w_up=v_ffn1_w_up, ffn1_w_down=v_ffn1_w_down,
             mix_norm=v_mix_norm, w_in=v_w_in, a_dw_w=v_a_dw_w, a_dw_b=v_a_dw_b, a_ln_g=v_a_ln_g, a_ln_b=v_a_ln_b,
             a_w_out=v_a_w_out, b_conv_w=v_b_conv_w, b_w_out=v_b_w_out, w_o=v_w_o, ffn2_norm=v_ffn2_norm,
             ffn2_w_gate=v_ffn2_w_gate, ffn2_w_up=v_ffn2_w_up, ffn2_w_down=v_ffn2_w_down, final_norm=v_final_norm)
    flat = _pack_weights(dict(wg1=ffn1_w_gate[0].T, wu1=ffn1_w_up[0].T, wd1=ffn1_w_down[0], wg2=ffn2_w_gate[0].T,
                              wu2=ffn2_w_up[0].T, wd2=ffn2_w_down[0], win=w_in[0], wa=a_w_out[0], wb=b_w_out[0],
                              wo=w_o[0]))
    cw_shard = jnp.concatenate([a_dw_w[0], jnp.zeros((1, LANE), F32), b_conv_w[0], jnp.zeros((5, LANE), F32)], axis=0)

    x2, tgt = x[0], loss_target[0]
    st_a, st_b, st_b2 = ("wg1", "wu1"), ("wd1", "win/0/2"), ("win/1/2",)
    st_c, st_d, st_e = ("wa", "wb", "wo", "wg2"), ("wu2",), ("wd2",)

    buf_a, cw = _run_comm(_join(_ag_comm(st_a, flat), _direct_comm(cw_shard, False)), "ag_ffn1")
    n1, gg1, uu1, act1, buf_b = _ffn_gate_up(x2, ffn1_norm, (buf_a, buf_a), (0, F), "ffn1_gate_up", _ag_comm(st_b, flat))
    h1, buf_b2 = _ffn_down(x2, act1, buf_b, 0, "ffn1_down", _ag_comm(st_b2, flat))
    win = ((buf_b, F), (buf_b2, 0))
    u, z, buf_c = _mix_in(h1, mix_norm, win, _ag_comm(st_c, flat))
    dft = _dft_constants()
    cw = jnp.transpose(cw, (1, 0, 2)).reshape(40, D)
    a1, q, buf_d = _conv_fwd_dft(z, cw, a_dw_b, dft, _ag_comm(st_d, flat))
    h2, ya, yb, buf_e = _mix_out(a1, q, z, h1, a_ln_g, a_ln_b, buf_c, _ag_comm(st_e, flat))
    ffn2_bufs, ffn2_offs = (buf_c, buf_d, buf_e), (3 * D, 0, 0)
    dh3, dhb3, s_final, n2, gg2, uu2, act2 = _ffn_fwd(h2, ffn2_norm, ffn2_bufs, ffn2_offs, "ffn2_fwd",
                                          final=(final_norm.reshape(1, D), tgt))

    tr_f = F // 2 if (F // 2) % LANE == 0 else F
    def pair(stage, src):
        return _rs_pair_comm(stage, src)

    def chip(stage, src, pair_buf, tag):
        return _rs_chip_comm(_pair_add(stage, src, pair_buf, "pair_add_" + tag))

    (dgu2,) = _ffn_bwd_hidden(dhb3, gg2, uu2, buf_e, 0, "ffn2_bwd_h")
    (gu2,) = _tn_matmul(dgu2, n2, tr_f, "dw_gu2")
    s2a, src2a = ("wg2", "wu2"), dict(wg2=(gu2, 0), wu2=(gu2, F))
    gd2, pair2a = _tn_matmul(act2, dhb3, tr_f, "dw_d2", pair(s2a, src2a))
    s2b, src2b = ("wd2",), dict(wd2=(gd2, 0))
    dh2, s_ffn2, pair2b = _ffn_bwd_input(dgu2, dh3, h2, ffn2_norm, (buf_c, buf_d), (3 * D, 0), "ffn2_bwd_x",
                                         pair(s2b, src2b))
    dzg, da1, dq, lsq, rsq, s_mix, recv2b = _mix_out_bwd(dh2, ya, yb, z, a1, q, a_ln_g, a_ln_b, buf_c,
                                                          chip(s2b, src2b, pair2b, "2b"))
    (gsq,) = _tn_matmul(lsq, rsq, D, "dw_square")
    ssq, srcsq = ("wa", "wb", "wo"), dict(wa=(gsq, D), wb=(gsq, 2 * D), wo=(gsq, 0))
    dz, dwa, dwb, recv2a, pairsq = _conv_bwd_dft(z, da1, dq, dzg, cw, dft,
                                                 _join(chip(s2a, src2a, pair2a, "2a"), pair(ssq, srcsq)))
    gin, recvsq = _tn_matmul(dz, u, D, "dw_in", chip(ssq, srcsq, pairsq, "sq"))
    sin_a, sin_b, srcin = ("win/0/2",), ("win/1/2",), {"win/0/2": (gin, 0), "win/1/2": (gin, 0)}
    dh1, dhb1, s_in, pairin_a, pairin_b = _mix_in_bwd(dz, dh2, h1, mix_norm, win,
                                                _join(pair(sin_a, srcin), pair(sin_b, srcin)))
    dgu1, recvin_a = _ffn_bwd_hidden(dhb1, gg1, uu1, buf_b, 0, "ffn1_bwd_h",
                                           chip(sin_a, srcin, pairin_a, "in_a"))
    gu1, recvin_b = _tn_matmul(dgu1, n1, tr_f, "dw_gu1", chip(sin_b, srcin, pairin_b, "in_b"))
    s1a, src1a = ("wg1", "wu1"), dict(wg1=(gu1, 0), wu1=(gu1, F))
    gd1, pair1a = _tn_matmul(act1, dhb1, tr_f, "dw_d1", pair(s1a, src1a))
    s1b, src1b = ("wd1",), dict(wd1=(gd1, 0))
    xchg1 = _join(chip(s1a, src1a, pair1a, "1a"), pair(s1b, src1b))
    xchg1_sems, xchg1_bufs, token = _comm_start(xchg1, "xchg_ffn1_start")
    dx, s_ffn1 = _ffn_bwd_input(dgu1, dh1, x2, ffn1_norm, (buf_a, buf_a), (0, F), "ffn1_bwd_x", after=token)
    (_, gd1), (recv1a, pair1b) = _comm_wait(xchg1, "xchg_ffn1_wait", xchg1_sems, xchg1_bufs, s_ffn1)
    src1b = dict(wd1=(gd1, 0))

    vec8, convk = _pack_small(s_ffn1, s_in, s_mix, s_ffn2, s_final, dwa, dwb)
    tail = _join(chip(s1b, src1b, pair1b, "1b"), _join(_direct_comm(vec8, False), _direct_comm(convk, True)))
    tail_sems, tail_bufs, token = _comm_start(tail, "xchg_tail_start")

    fs = F // NDEV
    g = dict(ffn1_w_gate=(recv1a, 0), ffn1_w_up=(recv1a, fs), ffn2_w_gate=(recv2a, 0), ffn2_w_up=(recv2a, fs),
             ffn2_w_down=(recv2b, 0), a_w_out=(recvsq, 0), b_w_out=(recvsq, D // NDEV), w_o=(recvsq, 2 * (D // NDEV)))
    grad, upd = {}, {}

    def run(group, name, after, as2d=lambda a: a[0], back=lambda a, n: a.reshape(w[n].shape)):
        res = _adam([g[n] for n in group], [as2d(w[n]) for n in group], [as2d(m[n]) for n in group],
                    [as2d(v[n]) for n in group], name, after)
        for n, r in zip(group, res):
            grad[n], upd[n] = back(r[0], n), tuple(back(a, n) for a in r[1:])
        return res[0][0]

    done = run(("ffn1_w_gate", "ffn1_w_up", "ffn2_w_gate", "ffn2_w_up"), "adam_gate_up", token,
               as2d=lambda a: a[0].T, back=lambda a, n: a.T[None])
    r_in = _adam_in_proj([recvin_a, recvin_b], w_in[0], m_w_in[0], v_w_in[0], done)
    grad["w_in"], upd["w_in"] = r_in[0][None], tuple(a[None] for a in r_in[1:])
    done = run(("a_w_out", "b_w_out", "w_o"), "adam_square", r_in[0])
    _, (recv1b, vec_all, conv_all) = _comm_wait(tail, "xchg_tail_wait", tail_sems, tail_bufs, done)
    g["ffn1_w_down"] = (recv1b, 0)
    run(("ffn1_w_down", "ffn2_w_down"), "adam_down", done)
    vec_names = ("ffn1_norm", "mix_norm", "a_ln_g", "a_ln_b", "a_dw_b", "ffn2_norm", "final_norm")
    tap_names, tap_rows = ("a_dw_w", "b_conv_w"), (0, 32)
    taps = lambda a: jnp.transpose(a, (1, 0, 2))
    loss, vec_res, tap_res = _adam_small(
        vec_all, conv_all, [tuple(t[n].reshape(1, D) for t in (w, m, v)) for n in vec_names],
        [(r,) + tuple(taps(t[n]) for t in (w, m, v)) for n, r in zip(tap_names, tap_rows)])
    for n, r in zip(vec_names, vec_res):
        grad[n], upd[n] = r[0].reshape(w[n].shape), tuple(a.reshape(w[n].shape) for a in r[1:])
    for n, r in zip(tap_names, tap_res):
        grad[n], upd[n] = taps(r[0]), tuple(taps(a) for a in r[1:])

    return (loss.reshape(()), dx.reshape(x.shape), *[grad[n] for n in names], *[upd[n][0] for n in names],
            *[upd[n][1] for n in names], *[upd[n][2] for n in names])
```

```python
import jax
import jax.numpy as jnp
from jax import lax
from jax.experimental import pallas as pl
from jax.experimental.pallas import tpu as pltpu

T = 4096
D = 1024
F = 2816
NG = 7
NDEV = 8
NCHIP = 4
KA, KB = 31, 3
EPS = 1e-6
ADAM_LR, ADAM_B1, ADAM_B2, ADAM_EPS, ADAM_WD, ADAM_STEP = 0.001, 0.9, 0.999, 1e-08, 0.01, 10

TM = 512
FC = 256
TB = 1024
NB = 256
HB = NB // 2
CW = 256
CHB = 64
LANE = 128
TK = 2048
VMEM_LIMIT = 56 * 1024 * 1024

BF = jnp.bfloat16
F32 = jnp.float32
MESH = pl.DeviceIdType.MESH
ANY = pl.BlockSpec(memory_space=pl.ANY)

ORDER = ("wg1", "wu1", "wd1", "wg2", "wu2", "wd2", "win", "wa", "wb", "wo")


class _Layout:
    def __init__(self):
        fs, dis, ds = F // NDEV, NG * D // NDEV, D // NDEV
        self.rows = dict(wg1=fs, wu1=fs, wd1=fs, wg2=fs, wu2=fs, wd2=fs, win=dis, wa=ds, wb=ds, wo=ds)
        self.fl, off = {}, 0
        for n in ORDER:
            self.fl[n] = off
            off += self.rows[n]
        self.RT = off


class _Stage:
    def __init__(self, names):
        lay = _Layout()
        self.names = names
        self.rows, self.full, self.sub, self.fl = {}, {}, {}, {}
        for n in names:
            base, i, k = (n.split("/") + ["0", "1"])[:3]
            self.full[n] = lay.rows[base]
            self.rows[n] = lay.rows[base] // int(k)
            self.sub[n] = int(i) * self.rows[n]
            self.fl[n] = lay.fl[base] + self.sub[n]
        self.off, self.wc, o, w = {}, {}, 0, 0
        for n in names:
            self.off[n], self.wc[n] = o, w
            o += self.rows[n]
            w += NDEV * self.rows[n]
        self.R, self.W = o, w

    def grad_row(self, n, first, dev_lin):
        return first + dev_lin * self.full[n] + self.sub[n]


def _nt(a, b):
    return lax.dot_general(a, b, (((1,), (1,)), ((), ())), preferred_element_type=F32)


def _nn(a, b):
    return lax.dot_general(a, b, (((1,), (0,)), ((), ())), preferred_element_type=F32)


def _tn(a, b):
    return lax.dot_general(a, b, (((0,), (0,)), ((), ())), preferred_element_type=F32)


def _sig(x):
    return 1.0 / (1.0 + jnp.exp(-x))


def _position():
    return lax.axis_index("x"), lax.axis_index("y"), lax.axis_index("c")


def _peer(pos, j):
    x, y, c = pos
    return (1 - x if j & 4 else x, 1 - y if j & 2 else y, 1 - c if j & 1 else c)


def _lin(pos):
    return 4 * pos[0] + 2 * pos[1] + pos[2]


def _chip(pos):
    return 2 * pos[0] + pos[1]


class _Comm:
    def __init__(self, inputs, out_shapes, scratch, start, finish, middle=None):
        self.inputs, self.out_shapes, self.scratch = inputs, out_shapes, scratch
        self.start, self.finish, self.middle = start, finish, middle


def _call(body, *, name, grid, args, in_specs, out_shape, out_specs, scratch_shapes=(), comm=None,
          num_scalar_prefetch=0, after=None):
    in_specs, out_shape, out_specs, scratch_shapes = list(in_specs), list(out_shape), list(out_specs), list(scratch_shapes)
    if after is not None:
        inner, pos = body, num_scalar_prefetch + len(in_specs)
        body = lambda *refs: inner(*refs[:pos], *refs[pos + 1:])
        args, in_specs = list(args) + [after], in_specs + [ANY]
    n_in, n_out, n_scr = len(in_specs), len(out_shape), len(scratch_shapes)
    sp = num_scalar_prefetch
    if comm is None:
        kernel_fn = lambda *refs: body(*refs)
        c_in = c_out = c_scr = 0
    else:
        c_in, c_out, c_scr = len(comm.inputs), len(comm.out_shapes), len(comm.scratch)

        def kernel_fn(*refs):
            pre, refs = refs[:sp], refs[sp:]
            ins, cins = refs[:n_in], refs[n_in:n_in + c_in]
            o0 = n_in + c_in
            outs, couts = refs[o0:o0 + n_out], refs[o0 + n_out:o0 + n_out + c_out]
            s0 = o0 + n_out + c_out
            scr, cscr = refs[s0:s0 + n_scr], refs[s0 + n_scr:]
            step, steps = pl.program_id(0), grid[0]
            for a in range(1, len(grid)):
                step, steps = step * grid[a] + pl.program_id(a), steps * grid[a]
            first, last = step == 0, step == steps - 1

            @pl.when(first)
            def _():
                comm.start(cins, couts, cscr)

            if comm.middle is not None:
                @pl.when(step == (steps // 2 if steps > 2 else steps - 1))
                def _():
                    comm.middle(cins, couts, cscr)

            body(*pre, *ins, *outs, *scr)

            @pl.when(last)
            def _():
                comm.finish(cins, couts, cscr)

        args = list(args) + list(comm.inputs)
        in_specs += [ANY] * c_in
        out_shape += list(comm.out_shapes)
        out_specs += [ANY] * c_out
        scratch_shapes += list(comm.scratch)
    params = pltpu.CompilerParams(dimension_semantics=("arbitrary",) * len(grid), vmem_limit_bytes=VMEM_LIMIT)
    if sp:
        grid_spec = pltpu.PrefetchScalarGridSpec(num_scalar_prefetch=sp, grid=grid, in_specs=in_specs,
                                                 out_specs=out_specs, scratch_shapes=scratch_shapes)
        return pl.pallas_call(kernel_fn, name=name, grid_spec=grid_spec, out_shape=out_shape,
                              compiler_params=params)(*args)
    return pl.pallas_call(kernel_fn, name=name, grid=grid, in_specs=in_specs, out_shape=out_shape, out_specs=out_specs,
                          scratch_shapes=scratch_shapes, compiler_params=params)(*args)


def _join(a, b):
    na = (len(a.inputs), len(a.out_shapes), len(a.scratch))

    def split(refs):
        return ([r[:n] for r, n in zip(refs, na)], [r[n:] for r, n in zip(refs, na)])

    def start(*refs):
        ra, rb = split(refs)
        a.start(*ra)
        b.start(*rb)

    def finish(*refs):
        ra, rb = split(refs)
        a.finish(*ra)
        b.finish(*rb)

    def middle(*refs):
        for stage, r in zip((a, b), split(refs)):
            if stage.middle is not None:
                stage.middle(*r)

    return _Comm(list(a.inputs) + list(b.inputs), list(a.out_shapes) + list(b.out_shapes),
                 list(a.scratch) + list(b.scratch), start, finish,
                 middle if (a.middle is not None or b.middle is not None) else None)


def _run_comm(comm, name):
    def body(*refs):
        c_in, c_out = len(comm.inputs), len(comm.out_shapes)
        parts = (refs[:c_in], refs[c_in:c_in + c_out], refs[c_in + c_out:])
        comm.start(*parts)
        if comm.middle is not None:
            comm.middle(*parts)
        comm.finish(*parts)

    return pl.pallas_call(
        body, name=name, out_shape=list(comm.out_shapes), in_specs=[ANY] * len(comm.inputs),
        out_specs=[ANY] * len(comm.out_shapes), scratch_shapes=list(comm.scratch))(*comm.inputs)


HBM = pl.BlockSpec(memory_space=pltpu.HBM)
SEM = pl.BlockSpec(memory_space=pltpu.SEMAPHORE)
DATAFLOW = pltpu.SideEffectType.DATAFLOW_SIDE_EFFECTING


def _comm_start(comm, name):
    c_in, c_out = len(comm.inputs), len(comm.out_shapes)
    sems = [s(()) if s is pltpu.SemaphoreType.DMA else s for s in comm.scratch]
    bufs = list(comm.inputs) + [lax.empty(s.shape, s.dtype) for s in comm.out_shapes]

    def body(*refs):
        sem_refs = refs[c_in + c_out:c_in + c_out + len(sems)]
        comm.start(refs[:c_in], refs[c_in:c_in + c_out], sem_refs)
        refs[-1][...] = jnp.zeros_like(refs[-1])

    outs = pl.pallas_call(
        body, name=name,
        out_shape=sems + [pltpu.HBM(b.shape, b.dtype) for b in bufs] + [jax.ShapeDtypeStruct((8, LANE), F32)],
        in_specs=[HBM] * len(bufs),
        out_specs=[SEM] * len(sems) + [HBM] * len(bufs) + [pl.BlockSpec(memory_space=pltpu.VMEM)],
        input_output_aliases={i: len(sems) + i for i in range(len(bufs))},
        compiler_params=pltpu.CompilerParams(has_side_effects=DATAFLOW),
    )(*[pltpu.with_memory_space_constraint(b, pltpu.HBM) for b in bufs])
    return outs[:len(sems)], outs[len(sems):-1], outs[-1]


def _comm_wait(comm, name, sems, bufs, after):
    c_in, c_out = len(comm.inputs), len(comm.out_shapes)

    def body(*refs):
        sem_refs = refs[c_in + c_out:c_in + c_out + len(sems)]
        comm.finish(refs[:c_in], refs[c_in:c_in + c_out], sem_refs)

    outs = pl.pallas_call(
        body, name=name, out_shape=[pltpu.HBM(b.shape, b.dtype) for b in bufs],
        in_specs=[HBM] * len(bufs) + [SEM] * len(sems) + [ANY], out_specs=[HBM] * len(bufs),
        input_output_aliases={i: i for i in range(len(bufs))},
        compiler_params=pltpu.CompilerParams(has_side_effects=DATAFLOW),
    )(*bufs, *sems, after)
    return outs[:c_in], outs[c_in:]


def _ag_comm(names, flat):
    st = _Stage(names)

    def ring(me):
        x, y, c = me
        diagonal = x == y
        up = (jnp.where(diagonal, x, 1 - x), jnp.where(diagonal, 1 - y, y), c)
        down = (jnp.where(diagonal, 1 - x, x), jnp.where(diagonal, y, 1 - y), c)
        low = c == 0
        passed = tuple(jnp.where(low, d, u) for d, u in zip(down, up))
        target = tuple(jnp.where(low, u, d) for d, u in zip(down, up))
        return up, down, (1 - x, 1 - y, c), passed, target

    def parts(refs):
        (flat_ref,), (out_ref,), (send_sems, recv_sems, local_sem) = refs
        me = _position()

        def region(name, dev):
            r = st.rows[name]
            return out_ref.at[pl.ds(st.wc[name] + _lin(dev) * r, r), :]

        def own(name):
            return flat_ref.at[pl.ds(st.fl[name], st.rows[name]), :]

        def copies(k, dev, to, from_flat):
            return [pltpu.make_async_remote_copy(
                src_ref=own(n) if from_flat else region(n, dev), dst_ref=region(n, dev), send_sem=send_sems.at[k],
                recv_sem=recv_sems.at[k], device_id=to, device_id_type=MESH) for n in names]

        def whole(k):
            return pltpu.make_async_remote_copy(
                src_ref=flat_ref.at[pl.ds(0, st.R), :], dst_ref=out_ref.at[pl.ds(0, st.R), :],
                send_sem=send_sems.at[k], recv_sem=recv_sems.at[k], device_id=me, device_id_type=MESH)

        return me, region, own, copies, whole, flat_ref, out_ref, local_sem

    def start(*refs):
        me, region, own, copies, _, _, _, local_sem = parts(refs)
        for n in names:
            pltpu.make_async_copy(own(n), region(n, me), local_sem).start()
        up, down, _, _, _ = ring(me)
        for k, to in ((1, up), (2, down), (0, _peer(me, 1))):
            for cp in copies(k, me, to, True):
                cp.start()

    def middle(*refs):
        me, _, _, copies, whole, _, _, _ = parts(refs)
        up, down, _, passed, target = ring(me)
        sib = _peer(me, 1)
        whole(1).wait_recv()
        whole(2).wait_recv()
        for k, dev, to in ((3, passed, target), (4, down, sib), (5, up, sib)):
            for cp in copies(k, dev, to, False):
                cp.start()

    def finish(*refs):
        me, _, _, copies, whole, flat_ref, out_ref, local_sem = parts(refs)
        _, _, across, _, _ = ring(me)
        whole(3).wait_recv()
        for cp in copies(6, across, _peer(me, 1), False):
            cp.start()
        whole(0).wait_recv()
        for j in range(3):
            whole(4 + j).wait_recv()
        for k in range(7):
            whole(k).wait_send()
        pltpu.make_async_copy(flat_ref.at[pl.ds(0, st.R), :], out_ref.at[pl.ds(0, st.R), :], local_sem).wait()

    return _Comm([flat], [jax.ShapeDtypeStruct((st.W, D), BF)],
                 [pltpu.SemaphoreType.DMA((7,)), pltpu.SemaphoreType.DMA((7,)), pltpu.SemaphoreType.DMA],
                 start, finish, middle)


def _rs_pair_comm(names, src):
    st = _Stage(names)
    arrays = []
    for n in names:
        if not any(src[n][0] is a for a in arrays):
            arrays.append(src[n][0])
    idx = {n: [i for i, a in enumerate(arrays) if a is src[n][0]][0] for n in names}

    def slot_wait(refs):
        recv = refs[1][0]
        send_sem, recv_sem = refs[2]
        return pltpu.make_async_remote_copy(src_ref=recv, dst_ref=recv, send_sem=send_sem, recv_sem=recv_sem,
                                            device_id=_position(), device_id_type=MESH)

    def start(*refs):
        ins, (recv,), (send_sem, recv_sem) = refs
        me = _position()
        sib = _peer(me, 1)
        for q in range(NCHIP):
            dev = (q // 2, q % 2, sib[2])
            for n in names:
                r = st.rows[n]
                pltpu.make_async_remote_copy(
                    src_ref=ins[idx[n]].at[pl.ds(st.grad_row(n, src[n][1], _lin(dev)), r), :],
                    dst_ref=recv.at[q, pl.ds(st.off[n], r), :], send_sem=send_sem, recv_sem=recv_sem,
                    device_id=sib, device_id_type=MESH).start()

    def finish(*refs):
        w = slot_wait(refs)
        w.wait_recv()
        w.wait_send()

    return _Comm(arrays, [jax.ShapeDtypeStruct((NCHIP, st.R, D), BF)],
                 [pltpu.SemaphoreType.DMA, pltpu.SemaphoreType.DMA], start, finish)


def _pair_add(names, src, recv, name):
    st = _Stage(names)
    c_arr = jnp.reshape(lax.axis_index("c"), (1,)).astype(jnp.int32)

    def body(c_ref, *refs):
        r_ref, o_ref = refs[len(names)], refs[len(names) + 1]
        for a_ref, n in zip(refs, names):
            rows = slice(st.off[n], st.off[n] + st.rows[n])
            o_ref[rows, :] = (a_ref[...].astype(F32) + r_ref[rows, :].astype(F32)).astype(BF)

    def shard_spec(n):
        r = st.rows[n]
        base, step = st.grad_row(n, src[n][1], 0) // r, st.full[n] // r
        return pl.BlockSpec((r, D), lambda q, c_ref: (base + step * (2 * q + c_ref[0]), 0))

    slot = pl.BlockSpec((None, st.R, D), lambda q, c_ref: (q, 0, 0))
    return _call(body, name=name, grid=(NCHIP,), args=[c_arr] + [src[n][0] for n in names] + [recv],
                 in_specs=[shard_spec(n) for n in names] + [slot],
                 out_shape=[jax.ShapeDtypeStruct((NCHIP, st.R, D), BF)], out_specs=[slot], num_scalar_prefetch=1)[0]


def _rs_chip_comm(part):
    def copies(refs):
        (p_ref,), (recv,), (send_sems, recv_sems, local_sem) = refs
        me = _position()
        mine = pltpu.make_async_copy(p_ref.at[_chip(me)], recv.at[_chip(me)], local_sem)
        out = []
        for j, bits in enumerate((4, 2, 6)):
            to = _peer(me, bits)
            out.append(pltpu.make_async_remote_copy(
                src_ref=p_ref.at[_chip(to)], dst_ref=recv.at[_chip(me)], send_sem=send_sems.at[j],
                recv_sem=recv_sems.at[j], device_id=to, device_id_type=MESH))
        return mine, out

    def start(*refs):
        mine, out = copies(refs)
        mine.start()
        for cp in out:
            cp.start()

    def finish(*refs):
        mine, out = copies(refs)
        for cp in out:
            cp.wait_recv()
        for cp in out:
            cp.wait_send()
        mine.wait()

    return _Comm([part], [jax.ShapeDtypeStruct(part.shape, BF)],
                 [pltpu.SemaphoreType.DMA((3,)), pltpu.SemaphoreType.DMA((3,)), pltpu.SemaphoreType.DMA],
                 start, finish)


def _direct_comm(x, scatter):
    def copies(refs):
        (x_ref,), (out_ref,), (send_sems, recv_sems, local_sem) = refs
        me = _position()

        def piece(dev):
            return x_ref.at[_lin(dev)] if scatter else x_ref

        mine = pltpu.make_async_copy(piece(me), out_ref.at[_lin(me)], local_sem)
        return mine, [pltpu.make_async_remote_copy(
            src_ref=piece(_peer(me, j)), dst_ref=out_ref.at[_lin(me)], send_sem=send_sems.at[j - 1],
            recv_sem=recv_sems.at[j - 1], device_id=_peer(me, j), device_id_type=MESH) for j in range(1, NDEV)]

    def start(*refs):
        mine, cps = copies(refs)
        mine.start()
        for cp in cps:
            cp.start()

    def finish(*refs):
        mine, cps = copies(refs)
        for cp in cps:
            cp.wait_recv()
        for cp in cps:
            cp.wait_send()
        mine.wait()

    shape = x.shape if scatter else (NDEV,) + x.shape
    return _Comm([x], [jax.ShapeDtypeStruct(shape, x.dtype)],
                 [pltpu.SemaphoreType.DMA((7,)), pltpu.SemaphoreType.DMA((7,)), pltpu.SemaphoreType.DMA],
                 start, finish)


def _pack_weights(shards):
    lay = _Layout()

    def body(*refs):
        o_ref = refs[-1]
        for ref, n in zip(refs, ORDER):
            x = ref[...].T if n == "win" else ref[...]
            o_ref[lay.fl[n]:lay.fl[n] + lay.rows[n], :] = x.astype(BF)

    return pl.pallas_call(
        body, name="pack_weights", out_shape=jax.ShapeDtypeStruct((lay.RT, D), BF),
        compiler_params=pltpu.CompilerParams(vmem_limit_bytes=VMEM_LIMIT))(*[shards[n] for n in ORDER])


def _load_ffn_weights(srcs, offs, scratch, sem):
    @pl.when(pl.program_id(0) == 0)
    def _():
        cps = [pltpu.make_async_copy(s.at[pl.ds(off, dst.shape[0]), :], dst, sem.at[i])
               for i, (s, off, dst) in enumerate(zip(srcs, offs, scratch))]
        for cp in cps:
            cp.start()
        for cp in cps:
            cp.wait()


def _final_loss_tile(xf, g, tgt, s_ref):
    r = lax.rsqrt(jnp.mean(xf * xf, axis=-1, keepdims=True) + EPS)
    xr = xf * r
    e = xr * g - tgt
    s_ref[1:2, :] += jnp.sum(e * e, axis=0, keepdims=True) * (0.5 / D)
    dy = e * (1.0 / D)
    s_ref[0:1, :] += jnp.sum(dy * xr, axis=0, keepdims=True)
    gdy = dy * g
    return r * gdy - xr * (r * jnp.mean(gdy * xr, axis=-1, keepdims=True))


def _ffn_fwd(x, g, wbufs, offs, name, comm=None, final=None):
    nf = F // FC

    def body(x_ref, g_ref, b0, b1, b2, *rest):
        if final is None:
            h_ref, n_ref, gg_ref, uu_ref, a_ref, wg_s, wu_s, wd_s, sem = rest
        else:
            gf_ref, t_ref, dh_ref, dhb_ref, s_ref, n_ref, gg_ref, uu_ref, a_ref, wg_s, wu_s, wd_s, sem = rest

            @pl.when(pl.program_id(0) == 0)
            def _():
                s_ref[...] = jnp.zeros_like(s_ref)

        _load_ffn_weights((b0, b1, b2), offs, (wg_s, wu_s, wd_s), sem)
        xf = x_ref[...]
        r = lax.rsqrt(jnp.mean(xf * xf, axis=-1, keepdims=True) + EPS)
        nb = (xf * r * g_ref[...]).astype(BF)
        n_ref[...] = nb
        acc = jnp.zeros((TM, D), F32)
        for c in range(nf):
            sl = slice(c * FC, (c + 1) * FC)
            gb = _nt(nb, wg_s[sl, :]).astype(BF)
            ub = _nt(nb, wu_s[sl, :]).astype(BF)
            gg_ref[:, sl] = gb
            uu_ref[:, sl] = ub
            a = (gb * _sig(gb)) * ub
            a_ref[0, :, sl] = a
            acc = acc + _nn(a, wd_s[sl, :])
        h = xf + 0.5 * acc
        if final is None:
            h_ref[...] = h
        else:
            dh = _final_loss_tile(h, gf_ref[...], t_ref[...], s_ref)
            dh_ref[...] = dh
            dhb_ref[...] = (0.5 * dh).astype(BF)

    row = lambda i: (i, 0)
    vec = pl.BlockSpec((1, D), lambda i: (0, 0))
    tile = pl.BlockSpec((TM, D), row)
    saved_shapes = [jax.ShapeDtypeStruct((T, D), BF), jax.ShapeDtypeStruct((T, F), BF), jax.ShapeDtypeStruct((T, F), BF),
                    jax.ShapeDtypeStruct((1, T, F), BF)]
    saved_specs = [tile, pl.BlockSpec((TM, F), row), pl.BlockSpec((TM, F), row),
                   pl.BlockSpec((1, TM, F), lambda i: (0, i, 0))]
    if final is None:
        extra_args, extra_specs = [], []
        head_shapes, head_specs = [jax.ShapeDtypeStruct((T, D), F32)], [tile]
    else:
        extra_args, extra_specs = list(final), [vec, tile]
        head_shapes = [jax.ShapeDtypeStruct((T, D), F32), jax.ShapeDtypeStruct((T, D), BF), jax.ShapeDtypeStruct((8, D), F32)]
        head_specs = [tile, tile, pl.BlockSpec((8, D), lambda i: (0, 0))]
    return _call(
        body, name=name, grid=(T // TM,), args=[x, g, *wbufs, *extra_args], comm=comm,
        in_specs=[tile, vec, ANY, ANY, ANY] + extra_specs,
        out_shape=head_shapes + saved_shapes, out_specs=head_specs + saved_specs,
        scratch_shapes=[pltpu.VMEM((F, D), BF)] * 3 + [pltpu.SemaphoreType.DMA((3,))])


def _ffn_gate_up(x, g, wbufs, offs, name, comm=None):
    nf = F // FC

    def body(x_ref, g_ref, b0, b1, n_ref, gg_ref, uu_ref, a_ref, wg_s, wu_s, sem):
        _load_ffn_weights((b0, b1), offs, (wg_s, wu_s), sem)
        xf = x_ref[...]
        r = lax.rsqrt(jnp.mean(xf * xf, axis=-1, keepdims=True) + EPS)
        nb = (xf * r * g_ref[...]).astype(BF)
        n_ref[...] = nb
        for c in range(nf):
            sl = slice(c * FC, (c + 1) * FC)
            gb = _nt(nb, wg_s[sl, :]).astype(BF)
            ub = _nt(nb, wu_s[sl, :]).astype(BF)
            gg_ref[:, sl] = gb
            uu_ref[:, sl] = ub
            a_ref[0, :, sl] = (gb * _sig(gb)) * ub

    row = lambda i: (i, 0)
    tile = pl.BlockSpec((TM, D), row)
    return _call(
        body, name=name, grid=(T // TM,), args=[x, g, *wbufs], comm=comm,
        in_specs=[tile, pl.BlockSpec((1, D), lambda i: (0, 0)), ANY, ANY],
        out_shape=[jax.ShapeDtypeStruct((T, D), BF), jax.ShapeDtypeStruct((T, F), BF), jax.ShapeDtypeStruct((T, F), BF),
                   jax.ShapeDtypeStruct((1, T, F), BF)],
        out_specs=[tile, pl.BlockSpec((TM, F), row), pl.BlockSpec((TM, F), row),
                   pl.BlockSpec((1, TM, F), lambda i: (0, i, 0))],
        scratch_shapes=[pltpu.VMEM((F, D), BF)] * 2 + [pltpu.SemaphoreType.DMA((2,))])


def _ffn_down(x, act, wbuf, off, name, comm=None):
    def body(x_ref, a_ref, b0, h_ref, wd_s, sem):
        _load_ffn_weights((b0,), (off,), (wd_s,), sem)
        h_ref[...] = x_ref[...] + 0.5 * _nn(a_ref[0], wd_s[...])

    tile = pl.BlockSpec((TM, D), lambda i: (i, 0))
    return _call(
        body, name=name, grid=(T // TM,), args=[x, act, wbuf], comm=comm,
        in_specs=[tile, pl.BlockSpec((1, TM, F), lambda i: (0, i, 0)), ANY],
        out_shape=[jax.ShapeDtypeStruct((T, D), F32)], out_specs=[tile],
        scratch_shapes=[pltpu.VMEM((F, D), BF), pltpu.SemaphoreType.DMA((1,))])


def _load_in_proj(parts, w_s, sem):
    @pl.when(pl.program_id(0) == 0)
    def _():
        shard = NG * D // NDEV
        rows = shard // len(parts)
        cps = [pltpu.make_async_copy(buf.at[pl.ds(first + k * rows, rows), :],
                                     w_s.at[pl.ds(k * shard + p * rows, rows), :], sem.at[p * NDEV + k])
               for p, (buf, first) in enumerate(parts) for k in range(NDEV)]
        for cp in cps:
            cp.start()
        for cp in cps:
            cp.wait()


def _mix_in(h1, gm, win, comm=None):
    def body(h_ref, g_ref, *rest):
        w_any, (u_ref, z_ref, w_s, sem) = rest[:len(win)], rest[len(win):]
        _load_in_proj([(b, first) for b, (_, first) in zip(w_any, win)], w_s, sem)
        xf = h_ref[...]
        r = lax.rsqrt(jnp.mean(xf * xf, axis=-1, keepdims=True) + EPS)
        ub = (xf * r * g_ref[...]).astype(BF)
        u_ref[...] = ub
        for j in range(NG):
            z_ref[j] = _nt(ub, w_s[j * D:(j + 1) * D, :]).astype(BF)

    row = lambda i: (i, 0)
    return _call(
        body, name="mix_in", grid=(T // TM,), args=[h1, gm] + [b for b, _ in win], comm=comm,
        in_specs=[pl.BlockSpec((TM, D), row), pl.BlockSpec((1, D), lambda i: (0, 0))] + [ANY] * len(win),
        out_shape=[jax.ShapeDtypeStruct((T, D), BF), jax.ShapeDtypeStruct((NG, T, D), BF)],
        out_specs=[pl.BlockSpec((TM, D), row), pl.BlockSpec((NG, TM, D), lambda i: (0, i, 0))],
        scratch_shapes=[pltpu.VMEM((NG * D, D), BF), pltpu.SemaphoreType.DMA((NDEV * len(win),))])


def _shift_up(w, b):
    return w if b == 0 else pltpu.roll(w, w.shape[0] - b, 0)


def _fold8(p):
    red = p[0:8, :]
    for i in range(1, p.shape[0] // 8):
        red = red + p[8 * i:8 * i + 8, :]
    return red


def _dft_constants():
    import numpy as np
    nh = NB // 2
    f, n = np.arange(nh)[:, None], np.arange(NB)[None, :]
    ang = 2.0 * np.pi / NB * f * n
    fc = np.cos(ang)
    fs = np.where(f == 0, (-1.0) ** n, np.sin(ang))
    scale = np.where(f == 0, 1.0, 2.0) / NB
    ic = (scale * np.cos(ang)).T
    isn = np.where(f == 0, (-1.0) ** n / NB, scale * np.sin(ang)).T
    d = (KA - 1 - np.arange(32))[None, :]
    valid = (np.arange(32) < KA)[None, :]
    angk = 2.0 * np.pi / NB * f * d
    kc = np.where(valid, np.cos(angk), 0.0)
    ks = np.where(valid, np.sin(angk), 0.0)
    k2 = np.where(valid, np.where(f == 0, (-1.0) ** d, np.cos(angk)), 0.0)
    rtc = np.where(valid, scale * np.cos(angk), 0.0).T
    rts = np.where(valid, np.where(f == 0, (-1.0) ** d / NB, scale * np.sin(angk)), 0.0).T

    def bf(a):
        return jnp.asarray(a, F32).astype(BF)

    def split(a):
        hi = bf(a)
        return hi, (jnp.asarray(a, F32) - hi.astype(F32)).astype(BF)

    return dict(fc=bf(fc), fs=bf(fs), ic_hi=bf(ic[HB:]), is_hi=bf(isn[HB:]), ic_lo=bf(ic[:HB]), is_lo=bf(isn[:HB]),
                kc=split(kc), ks=split(ks), k2=split(k2), rtc=split(rtc), rts=split(rts))


def _dot3(m_hi, m_lo, x):
    x_hi = x.astype(BF)
    x_lo = (x - x_hi.astype(F32)).astype(BF)
    return _nn(m_hi, x_hi) + _nn(m_hi, x_lo) + _nn(m_lo, x_hi)


def _whole(a):
    return pl.BlockSpec(a.shape, lambda c, t: (0,) * a.ndim)


def _filter_spectrum(cw_ref, tabs, hc, hs, h2):
    w32 = cw_ref[0:32, :]
    for (hi, lo), dst in zip(tabs, (hc, hs, h2)):
        dst[...] = _dot3(hi[...], lo[...], w32)


def _conv_fwd_dft(z, cw, bias, dft, comm=None):
    nt = T // TB
    hb = TB // HB

    def body(z_ref, zh_ref, cw_ref, b_ref, fc_ref, fs_ref, ic_ref, is_ref, kch, kcl, ksh, ksl, k2h, k2l,
             a1_ref, q_ref, aext, ppad, hc, hs, h2):
        first = pl.program_id(1) == 0
        f = lambda ref, j: ref[j].astype(F32)

        @pl.when(first)
        def _():
            _filter_spectrum(cw_ref, ((kch, kcl), (ksh, ksl), (k2h, k2l)), hc, hs, h2)

        aext[0:HB, :] = jnp.where(first, 0.0, f(zh_ref, 0) * _sig(f(zh_ref, 1))).astype(BF)
        aext[HB:, :] = (f(z_ref, 0) * _sig(f(z_ref, 1))).astype(BF)
        ppad[0:8, :] = jnp.where(first, 0.0, f(zh_ref, 3)[HB - 8:HB, :] * f(zh_ref, 4)[HB - 8:HB, :])
        ppad[8:, :] = f(z_ref, 3) * f(z_ref, 4)
        bias_row = b_ref[...]

        for j in range(TB // HB):
            xs = aext[j * HB:j * HB + NB, :]
            xa, xb = _nn(fc_ref[...], xs), _nn(fs_ref[...], xs)
            yc = (hc[...] * xa - hs[...] * xb).astype(BF)
            ys = (h2[...] * xb + hs[...] * xa).astype(BF)
            y = _nn(ic_ref[...], yc) + _nn(is_ref[...], ys)
            a1_ref[j * HB:(j + 1) * HB, :] = (y + bias_row).astype(BF)

        def chunk(r, carry):
            base = pl.multiple_of(r * CHB, CHB)
            pw = ppad[pl.ds(base, CHB + 8), :]
            v = (cw_ref[pl.ds(32, 1), :] * _shift_up(pw, 6)[0:CHB, :]
                 + cw_ref[pl.ds(33, 1), :] * _shift_up(pw, 7)[0:CHB, :]
                 + cw_ref[pl.ds(34, 1), :] * pw[8:8 + CHB, :])
            q_ref[pl.ds(base, CHB), :] = (z_ref[2, pl.ds(base, CHB), :].astype(F32) * v).astype(BF)
            return carry

        lax.fori_loop(0, TB // CHB, chunk, 0)

    blk = pl.BlockSpec((TB, CW), lambda c, t: (t, c))
    tabs = [dft["fc"], dft["fs"], dft["ic_hi"], dft["is_hi"], *dft["kc"], *dft["ks"], *dft["k2"]]
    return _call(
        body, name="conv_fwd", grid=(D // CW, nt), comm=comm, args=[z, z, cw, bias] + tabs,
        in_specs=[pl.BlockSpec((5, TB, CW), lambda c, t: (0, t, c)),
                  pl.BlockSpec((5, HB, CW), lambda c, t: (0, jnp.maximum(t * hb - 1, 0), c)),
                  pl.BlockSpec((40, CW), lambda c, t: (0, c)), pl.BlockSpec((1, CW), lambda c, t: (0, c))]
                 + [_whole(a) for a in tabs],
        out_shape=[jax.ShapeDtypeStruct((T, D), BF), jax.ShapeDtypeStruct((T, D), BF)], out_specs=[blk, blk],
        scratch_shapes=[pltpu.VMEM((TB + HB, CW), BF), pltpu.VMEM((TB + 8, CW), F32)]
                       + [pltpu.VMEM((NB // 2, CW), F32)] * 3)


def _conv_bwd_dft(z, da1, dq, dzg, cw, dft, comm=None, after=None):
    nt = T // TB
    hb = TB // HB
    last_h = T // HB - 1

    def body(z_ref, zp_ref, zn_ref, da1_ref, da1n_ref, dq_ref, dqn_ref, dzg_ref, cw_ref,
             fc_ref, fs_ref, ic_ref, is_ref, kch, kcl, ksh, ksl, k2h, k2l, rch, rcl, rsh, rsl,
             dz_ref, dwa_ref, dwb_ref, aext, dyext, ppad, dvpad, hc, hs, h2, rc, rs, nyq, acc_b):
        t = pl.program_id(1)
        first, last = t == 0, t == nt - 1
        f = lambda ref, j: ref[j].astype(F32)

        @pl.when(first)
        def _():
            _filter_spectrum(cw_ref, ((kch, kcl), (ksh, ksl), (k2h, k2l)), hc, hs, h2)
            rc[...] = jnp.zeros_like(rc)
            rs[...] = jnp.zeros_like(rs)
            nyq[...] = jnp.zeros_like(nyq)
            acc_b[...] = jnp.zeros_like(acc_b)

        aext[0:HB, :] = jnp.where(first, 0.0, f(zp_ref, 0) * _sig(f(zp_ref, 1))).astype(BF)
        aext[HB:, :] = (f(z_ref, 0) * _sig(f(z_ref, 1))).astype(BF)
        dyext[0:TB, :] = da1_ref[...]
        dyext[TB:, :] = jnp.where(last, 0.0, da1n_ref[...].astype(F32)).astype(BF)
        ppad[0:8, :] = jnp.where(first, 0.0, f(zp_ref, 3)[HB - 8:HB, :] * f(zp_ref, 4)[HB - 8:HB, :])
        ppad[8:, :] = f(z_ref, 3) * f(z_ref, 4)
        dvpad[0:TB, :] = dq_ref[...].astype(F32) * f(z_ref, 2)
        dvpad[TB:, :] = jnp.where(last, 0.0, dqn_ref[...].astype(F32)[0:8, :] * f(zn_ref, 2)[0:8, :])

        for j in range(TB // HB):
            rows = slice(j * HB, (j + 1) * HB)
            dys = dyext[j * HB:j * HB + NB, :]
            da, db = _nn(fc_ref[...], dys), _nn(fs_ref[...], dys)
            gc = (hc[...] * da + hs[...] * db).astype(BF)
            gs = (h2[...] * db - hs[...] * da).astype(BF)
            da0 = _nn(ic_ref[...], gc) + _nn(is_ref[...], gs)
            z0, z1 = z_ref[0, rows, :].astype(F32), z_ref[1, rows, :].astype(F32)
            s1 = _sig(z1)
            dz_ref[0, rows, :] = (da0 * s1).astype(BF)
            dz_ref[1, rows, :] = (da0 * z0 * (s1 * (1.0 - s1))).astype(BF)
            xs = aext[j * HB:j * HB + NB, :]
            xa, xb = _nn(fc_ref[...], xs), _nn(fs_ref[...], xs)
            dyb = dyext[rows, :]
            pa, pb = _nn(fc_ref[:, HB:NB], dyb), _nn(fs_ref[:, HB:NB], dyb)
            rc[...] += pa * xa + pb * xb
            rs[...] += pb * xa - pa * xb
            nyq[...] += pb[0:8, :] * xb[0:8, :]

        def chunk(r, carry):
            base = pl.multiple_of(r * CHB, CHB)
            rows = pl.ds(base, CHB)
            pw = ppad[pl.ds(base, CHB + 8), :]
            p6 = _shift_up(pw, 6)[0:CHB, :]
            p7 = _shift_up(pw, 7)[0:CHB, :]
            p8 = pw[8:8 + CHB, :]
            wb0, wb1, wb2 = cw_ref[pl.ds(32, 1), :], cw_ref[pl.ds(33, 1), :], cw_ref[pl.ds(34, 1), :]
            v = wb0 * p6 + wb1 * p7 + wb2 * p8
            dz_ref[2, rows, :] = (dq_ref[rows, :].astype(F32) * v).astype(BF)
            dvw = dvpad[pl.ds(base, CHB + 8), :]
            dvc = dvw[0:CHB, :]
            dp = wb2 * dvc + wb1 * _shift_up(dvw, 1)[0:CHB, :] + wb0 * _shift_up(dvw, 2)[0:CHB, :]
            dz_ref[3, rows, :] = (dp * z_ref[4, rows, :].astype(F32)).astype(BF)
            dz_ref[4, rows, :] = (dp * z_ref[3, rows, :].astype(F32)).astype(BF)
            acc_b[0:8, :] += _fold8(dvc * p6)
            acc_b[8:16, :] += _fold8(dvc * p7)
            acc_b[16:24, :] += _fold8(dvc * p8)
            dz_ref[5, rows, :] = dzg_ref[0, rows, :]
            dz_ref[6, rows, :] = dzg_ref[1, rows, :]
            return carry

        lax.fori_loop(0, TB // CHB, chunk, 0)

        @pl.when(last)
        def _():
            row0 = lax.broadcasted_iota(jnp.int32, (NB // 2, CW), 0) == 0
            ny = jnp.broadcast_to(nyq[0:1, :], (NB // 2, CW))
            rcv = jnp.where(row0, rc[...] - ny, rc[...])
            rsv = jnp.where(row0, ny, rs[...])
            dwa_ref[...] = _dot3(rch[...], rcl[...], rcv) + _dot3(rsh[...], rsl[...], rsv)
            for k in range(KB):
                dwb_ref[k:k + 1, :] = jnp.sum(acc_b[8 * k:8 * k + 8, :], axis=0, keepdims=True)
            dwb_ref[KB:8, :] = jnp.zeros((8 - KB, CW), F32)

    blk = lambda c, t: (t, c)
    nxt = lambda c, t: (jnp.minimum((t + 1) * hb, last_h), c)
    tabs = [dft["fc"], dft["fs"], dft["ic_lo"], dft["is_lo"], *dft["kc"], *dft["ks"], *dft["k2"], *dft["rtc"], *dft["rts"]]
    return _call(
        body, name="conv_bwd", grid=(D // CW, nt), comm=comm, after=after, args=[z, z, z, da1, da1, dq, dq, dzg, cw] + tabs,
        in_specs=[pl.BlockSpec((5, TB, CW), lambda c, t: (0, t, c)),
                  pl.BlockSpec((5, HB, CW), lambda c, t: (0, jnp.maximum(t * hb - 1, 0), c)),
                  pl.BlockSpec((5, HB, CW), lambda c, t: (0, jnp.minimum((t + 1) * hb, last_h), c)),
                  pl.BlockSpec((TB, CW), blk), pl.BlockSpec((HB, CW), nxt),
                  pl.BlockSpec((TB, CW), blk), pl.BlockSpec((HB, CW), nxt),
                  pl.BlockSpec((2, TB, CW), lambda c, t: (0, t, c)),
                  pl.BlockSpec((40, CW), lambda c, t: (0, c))]
                 + [_whole(a) for a in tabs],
        out_shape=[jax.ShapeDtypeStruct((NG, T, D), BF), jax.ShapeDtypeStruct((32, D), F32),
                   jax.ShapeDtypeStruct((8, D), F32)],
        out_specs=[pl.BlockSpec((NG, TB, CW), lambda c, t: (0, t, c)),
                   pl.BlockSpec((32, CW), lambda c, t: (0, c)), pl.BlockSpec((8, CW), lambda c, t: (0, c))],
        scratch_shapes=[pltpu.VMEM((TB + HB, CW), BF), pltpu.VMEM((TB + HB, CW), BF),
                        pltpu.VMEM((TB + 8, CW), F32), pltpu.VMEM((TB + 8, CW), F32)]
                       + [pltpu.VMEM((NB // 2, CW), F32)] * 5 + [pltpu.VMEM((8, CW), F32), pltpu.VMEM((24, CW), F32)])


def _layernorm_silu(a1, lng, lnb):
    mu = jnp.mean(a1, axis=-1, keepdims=True)
    xc = a1 - mu
    rs = lax.rsqrt(jnp.mean(xc * xc, axis=-1, keepdims=True) + EPS)
    xh = xc * rs
    a2 = xh * lng + lnb
    sg = _sig(a2)
    return xh, rs, a2, sg


def _square_specs(blocks):
    return [pl.BlockSpec((D, D), lambda i, b=b: (b, 0)) for b in blocks]


def _mix_out(a1, q, z, h1, lng, lnb, wsq, comm=None):
    def body(a1_ref, q_ref, ga_ref, gb_ref, h_ref, lng_ref, lnb_ref, wa_ref, wb_ref, wo_ref, h2_ref, ya_ref, yb_ref):
        _, _, a2, sg = _layernorm_silu(a1_ref[...].astype(F32), lng_ref[...], lnb_ref[...])
        ya = _nn((a2 * sg).astype(BF), wa_ref[...])
        yb = _nn(q_ref[...], wb_ref[...])
        ya_ref[...] = ya.astype(BF)
        yb_ref[...] = yb.astype(BF)
        m = _sig(ga_ref[...].astype(F32)) * ya + _sig(gb_ref[...].astype(F32)) * yb
        h2_ref[...] = h_ref[...] + _nn(m.astype(BF), wo_ref[...])

    row = lambda i: (i, 0)
    vec = pl.BlockSpec((1, D), lambda i: (0, 0))
    return _call(
        body, name="mix_out", grid=(T // TM,), args=[a1, q, z, z, h1, lng, lnb, wsq, wsq, wsq], comm=comm,
        in_specs=[pl.BlockSpec((TM, D), row), pl.BlockSpec((TM, D), row),
                  pl.BlockSpec((None, TM, D), lambda i: (5, i, 0)), pl.BlockSpec((None, TM, D), lambda i: (6, i, 0)),
                  pl.BlockSpec((TM, D), row), vec, vec] + _square_specs((0, 1, 2)),
        out_shape=[jax.ShapeDtypeStruct((T, D), F32), jax.ShapeDtypeStruct((T, D), BF), jax.ShapeDtypeStruct((T, D), BF)],
        out_specs=[pl.BlockSpec((TM, D), row)] * 3)


def _rmsnorm_bwd(xf, g, dn):
    r = lax.rsqrt(jnp.mean(xf * xf, axis=-1, keepdims=True) + EPS)
    xr = xf * r
    gdn = dn * g
    dx = r * gdn - xr * (r * jnp.mean(gdn * xr, axis=-1, keepdims=True))
    return dx, jnp.sum(dn * xr, axis=0, keepdims=True)


def _ffn_bwd_hidden(dh, gg, uu, wbuf, off, name, comm=None, after=None):
    nf = F // FC

    def body(dh_ref, gg_ref, uu_ref, b0, dgu_ref, wd_s, sem):
        _load_ffn_weights((b0,), (off,), (wd_s,), sem)
        dhb = dh_ref[...]
        for c in range(nf):
            sl = slice(c * FC, (c + 1) * FC)
            da = _nt(dhb, wd_s[sl, :]).astype(BF)
            gb, ub = gg_ref[:, sl], uu_ref[:, sl]
            sg = _sig(gb)
            dgu_ref[0, :, sl] = (da * ub) * (sg * (1.0 + gb * (1.0 - sg)))
            dgu_ref[0, :, F + c * FC:F + (c + 1) * FC] = da * (gb * sg)

    row = lambda i: (i, 0)
    return _call(
        body, name=name, grid=(T // TM,), args=[dh, gg, uu, wbuf], comm=comm, after=after,
        in_specs=[pl.BlockSpec((TM, D), row), pl.BlockSpec((TM, F), row), pl.BlockSpec((TM, F), row), ANY],
        out_shape=[jax.ShapeDtypeStruct((1, T, 2 * F), BF)],
        out_specs=[pl.BlockSpec((1, TM, 2 * F), lambda i: (0, i, 0))],
        scratch_shapes=[pltpu.VMEM((F, D), BF), pltpu.SemaphoreType.DMA((1,))])


def _ffn_bwd_input(dgu, dh, x, g, wbufs, offs, name, comm=None, after=None):
    def body(dgu_ref, dh_ref, x_ref, g_ref, b0, b1, dx_ref, s_ref, w_s, sem):
        _load_ffn_weights((b0, b1), offs, (w_s.at[pl.ds(0, F), :], w_s.at[pl.ds(F, F), :]), sem)

        @pl.when(pl.program_id(0) == 0)
        def _():
            s_ref[...] = jnp.zeros_like(s_ref)

        dn = _nn(dgu_ref[0], w_s[...])
        dxn, dg = _rmsnorm_bwd(x_ref[...], g_ref[...], dn)
        dx_ref[...] = dh_ref[...] + dxn
        s_ref[0:1, :] += dg

    row = lambda i: (i, 0)
    return _call(
        body, name=name, grid=(T // TM,), args=[dgu, dh, x, g, *wbufs], comm=comm, after=after,
        in_specs=[pl.BlockSpec((1, TM, 2 * F), lambda i: (0, i, 0)), pl.BlockSpec((TM, D), row),
                  pl.BlockSpec((TM, D), row), pl.BlockSpec((1, D), lambda i: (0, 0)), ANY, ANY],
        out_shape=[jax.ShapeDtypeStruct((T, D), F32), jax.ShapeDtypeStruct((8, D), F32)],
        out_specs=[pl.BlockSpec((TM, D), row), pl.BlockSpec((8, D), lambda i: (0, 0))],
        scratch_shapes=[pltpu.VMEM((2 * F, D), BF), pltpu.SemaphoreType.DMA((2,))])


def _tn_matmul(lhs, rhs, tr, name, comm=None, after=None):
    ng, _, cdim = lhs.shape
    nc, nk = cdim // tr, T // TK
    if rhs.ndim == 2:
        r_spec = pl.BlockSpec((TK, D), lambda g, c, k: (k, 0))
    else:
        r_spec = pl.BlockSpec((None, TK, D), lambda g, c, k: (g, k, 0))

    def body(l_ref, r_ref, o_ref, acc):
        k = pl.program_id(2)

        @pl.when(k == 0)
        def _():
            acc[...] = jnp.zeros_like(acc)

        acc[...] += _tn(l_ref[...], r_ref[...])

        @pl.when(k == nk - 1)
        def _():
            o_ref[...] = acc[...].astype(BF)

    return _call(
        body, name=name, grid=(ng, nc, nk), args=[lhs, rhs], comm=comm, after=after,
        in_specs=[pl.BlockSpec((None, TK, tr), lambda g, c, k: (g, k, c)), r_spec],
        out_shape=[jax.ShapeDtypeStruct((ng * cdim, D), BF)],
        out_specs=[pl.BlockSpec((tr, D), lambda g, c, k: (g * nc + c, 0))],
        scratch_shapes=[pltpu.VMEM((tr, D), F32)])


def _mix_out_bwd(dh2, ya, yb, z, a1, q, lng, lnb, wsq, comm=None, after=None):
    def body(dh_ref, ya_ref, yb_ref, ga_ref, gb_ref, a1_ref, q_ref, lng_ref, lnb_ref, wa_ref, wb_ref, wo_ref,
             dzg_ref, da1_ref, dq_ref, l_ref, r_ref, s_ref):
        @pl.when(pl.program_id(0) == 0)
        def _():
            s_ref[...] = jnp.zeros_like(s_ref)

        dhb = dh_ref[...].astype(BF)
        dm = _nt(dhb, wo_ref[...]).astype(BF)
        ya, yb = ya_ref[...], yb_ref[...]
        sa, sb = _sig(ga_ref[...]), _sig(gb_ref[...])
        l_ref[0] = sa * ya + sb * yb
        l_ref[2] = q_ref[...]
        dzg_ref[0] = (dm * ya) * (sa * (1.0 - sa))
        dzg_ref[1] = (dm * yb) * (sb * (1.0 - sb))
        dya = dm * sa
        dyb = dm * sb
        r_ref[0] = dhb
        r_ref[1] = dya
        r_ref[2] = dyb
        dq_ref[...] = _nt(dyb, wb_ref[...]).astype(BF)
        da3 = _nt(dya, wa_ref[...])
        lng = lng_ref[...]
        xh, rs, a2, sg = _layernorm_silu(a1_ref[...].astype(F32), lng, lnb_ref[...])
        l_ref[1] = (a2 * sg).astype(BF)
        da2 = da3 * (sg * (1.0 + a2 * (1.0 - sg)))
        s_ref[0:1, :] += jnp.sum(da2 * xh, axis=0, keepdims=True)
        s_ref[1:2, :] += jnp.sum(da2, axis=0, keepdims=True)
        dxh = da2 * lng
        da1 = rs * (dxh - jnp.mean(dxh, axis=-1, keepdims=True) - xh * jnp.mean(dxh * xh, axis=-1, keepdims=True))
        da1_ref[...] = da1.astype(BF)
        s_ref[2:3, :] += jnp.sum(da1, axis=0, keepdims=True)

    row = lambda i: (i, 0)
    row3 = lambda i: (0, i, 0)
    vec = pl.BlockSpec((1, D), lambda i: (0, 0))
    return _call(
        body, name="mix_out_bwd", grid=(T // TM,), args=[dh2, ya, yb, z, z, a1, q, lng, lnb, wsq, wsq, wsq], comm=comm, after=after,
        in_specs=[pl.BlockSpec((TM, D), row), pl.BlockSpec((TM, D), row), pl.BlockSpec((TM, D), row),
                  pl.BlockSpec((None, TM, D), lambda i: (5, i, 0)), pl.BlockSpec((None, TM, D), lambda i: (6, i, 0)),
                  pl.BlockSpec((TM, D), row), pl.BlockSpec((TM, D), row), vec, vec] + _square_specs((0, 1, 2)),
        out_shape=[jax.ShapeDtypeStruct((2, T, D), BF), jax.ShapeDtypeStruct((T, D), BF),
                   jax.ShapeDtypeStruct((T, D), BF), jax.ShapeDtypeStruct((3, T, D), BF),
                   jax.ShapeDtypeStruct((3, T, D), BF), jax.ShapeDtypeStruct((8, D), F32)],
        out_specs=[pl.BlockSpec((2, TM, D), row3), pl.BlockSpec((TM, D), row), pl.BlockSpec((TM, D), row),
                   pl.BlockSpec((3, TM, D), row3), pl.BlockSpec((3, TM, D), row3), pl.BlockSpec((8, D), lambda i: (0, 0))])


def _mix_in_bwd(dz, dh2, h1, gm, win, comm=None):
    def body(dz_ref, dh_ref, h_ref, g_ref, *rest):
        w_any, (o_ref, ob_ref, s_ref, w_s, sem) = rest[:len(win)], rest[len(win):]
        _load_in_proj([(b, first) for b, (_, first) in zip(w_any, win)], w_s, sem)

        @pl.when(pl.program_id(0) == 0)
        def _():
            s_ref[...] = jnp.zeros_like(s_ref)

        du = _nn(dz_ref[0], w_s[0:D, :])
        for j in range(1, NG):
            du = du + _nn(dz_ref[j], w_s[j * D:(j + 1) * D, :])
        dx, dg = _rmsnorm_bwd(h_ref[...], g_ref[...], du)
        dh1 = dh_ref[...] + dx
        o_ref[...] = dh1
        ob_ref[...] = (0.5 * dh1).astype(BF)
        s_ref[0:1, :] += dg

    row = lambda i: (i, 0)
    return _call(
        body, name="mix_in_bwd", grid=(T // TM,), args=[dz, dh2, h1, gm] + [b for b, _ in win], comm=comm,
        in_specs=[pl.BlockSpec((NG, TM, D), lambda i: (0, i, 0)), pl.BlockSpec((TM, D), row),
                  pl.BlockSpec((TM, D), row), pl.BlockSpec((1, D), lambda i: (0, 0))] + [ANY] * len(win),
        out_shape=[jax.ShapeDtypeStruct((T, D), F32), jax.ShapeDtypeStruct((T, D), BF), jax.ShapeDtypeStruct((8, D), F32)],
        out_specs=[pl.BlockSpec((TM, D), row), pl.BlockSpec((TM, D), row), pl.BlockSpec((8, D), lambda i: (0, 0))],
        scratch_shapes=[pltpu.VMEM((NG * D, D), BF), pltpu.SemaphoreType.DMA((NDEV * len(win),))])


def _row_tile(n, want, mult):
    for t in range(min(want, n), 0, -1):
        if n % t == 0 and t % mult == 0:
            return t
    return n


def _pack_small(s_ffn1, s_in, s_mix, s_ffn2, s_final, dwa, dwb):
    def body(f1, mi, mo, f2, fl, wa_ref, wb_ref, v_ref, k_ref):
        for dst, (ref, row) in enumerate(((f1, 0), (mi, 0), (mo, 0), (mo, 1), (mo, 2), (f2, 0), (fl, 0), (fl, 1))):
            v_ref[dst:dst + 1, :] = ref[row:row + 1, :]
        for k in range(NDEV):
            k_ref[k, 0:32, :] = wa_ref[:, k * LANE:(k + 1) * LANE]
            k_ref[k, 32:40, :] = wb_ref[:, k * LANE:(k + 1) * LANE]

    return pl.pallas_call(
        body, name="pack_small",
        out_shape=(jax.ShapeDtypeStruct((8, D), F32), jax.ShapeDtypeStruct((NDEV, 40, LANE), F32)),
    )(s_ffn1, s_in, s_mix, s_ffn2, s_final, dwa, dwb)


def _adam_update(g, w, m, v):
    m2 = ADAM_B1 * m + (1.0 - ADAM_B1) * g
    v2 = ADAM_B2 * v + (1.0 - ADAM_B2) * (g * g)
    c1 = 1.0 - ADAM_B1 ** ADAM_STEP
    c2 = 1.0 - ADAM_B2 ** ADAM_STEP
    return -ADAM_LR * ((m2 / c1) / (jnp.sqrt(v2 / c2) + ADAM_EPS) + ADAM_WD * w), m2, v2


def _adam_small(vecs, convs, vec_params, tap_params):
    nv, nt = len(vec_params), len(tap_params)

    def body(*refs):
        v_ref, k_ref = refs[:2]
        p_refs = refs[2:2 + 3 * (nv + nt)]
        l_ref = refs[2 + 3 * (nv + nt)]
        o_refs = refs[3 + 3 * (nv + nt):]
        s, c = v_ref[0], k_ref[0]
        for k in range(1, NDEV):
            s = s + v_ref[k]
            c = c + k_ref[k]
        l_ref[...] = jnp.sum(s[7:8, :], axis=-1, keepdims=True)
        for i in range(nv):
            w_ref, m_ref, u_ref = p_refs[3 * i: 3 * i + 3]
            g_ref, d_ref, m2_ref, u2_ref = o_refs[4 * i: 4 * i + 4]
            g = s[i:i + 1, :]
            g_ref[...] = g
            d_ref[...], m2_ref[...], u2_ref[...] = _adam_update(g, w_ref[...], m_ref[...], u_ref[...])
        for i in range(nt):
            w_ref, m_ref, u_ref = p_refs[3 * (nv + i): 3 * (nv + i) + 3]
            g_ref, d_ref, m2_ref, u2_ref = o_refs[4 * (nv + i): 4 * (nv + i) + 4]
            first = tap_params[i][0]
            for k in range(w_ref.shape[0]):
                g = c[first + k:first + k + 1, :]
                g_ref[k] = g
                d_ref[k], m2_ref[k], u2_ref[k] = _adam_update(g, w_ref[k], m_ref[k], u_ref[k])

    params = [a for p in vec_params for a in p] + [a for p in tap_params for a in p[1:]]
    out_shape = [jax.ShapeDtypeStruct((1, 1), F32)]
    for p in list(vec_params) + [p[1:] for p in tap_params]:
        out_shape += [jax.ShapeDtypeStruct(p[0].shape, F32)] * 4
    outs = pl.pallas_call(body, name="adam_small", out_shape=tuple(out_shape))(vecs, convs, *params)
    groups = [tuple(outs[1 + 4 * i: 5 + 4 * i]) for i in range(nv + nt)]
    return outs[0], groups[:nv], groups[nv:]


def _adam_in_proj(parts, w, m, v, after):
    rows = w.shape[1]
    tr = _row_tile(D, 256, LANE)

    def body(*refs):
        p_refs = refs[:len(parts)]
        w_ref, m_ref, v_ref, g_ref, d_ref, m2_ref, v2_ref = refs[len(parts):]
        sums = []
        for p in p_refs:
            s = p[0].astype(F32)
            for k in range(1, p.shape[0]):
                s = s + p[k].astype(F32)
            sums.append(s)
        g = jnp.concatenate(sums, axis=0).T
        g_ref[...] = g
        d_ref[...], m2_ref[...], v2_ref[...] = _adam_update(g, w_ref[...], m_ref[...], v_ref[...])

    spec = pl.BlockSpec((tr, rows), lambda i: (i, 0))
    return _call(body, name="adam_in", grid=(D // tr,), args=list(parts) + [w, m, v], after=after,
                 in_specs=[pl.BlockSpec((p.shape[0], p.shape[1], tr), lambda i: (0, 0, i)) for p in parts] + [spec] * 3,
                 out_shape=[jax.ShapeDtypeStruct((D, rows), F32)] * 4, out_specs=[spec] * 4)


def _adam(gs, ws, ms, vs, name, after):
    n = len(gs)
    rows, cols = ws[0].shape
    tr = _row_tile(rows, 256, 16)

    def body(*refs):
        for i in range(n):
            g_in, w, m, v = refs[4 * i], refs[4 * i + 1][...], refs[4 * i + 2][...], refs[4 * i + 3][...]
            g_ref, d_ref, m_ref, v_ref = refs[4 * n + 4 * i: 4 * n + 4 * i + 4]
            g = g_in[0].astype(F32)
            for k in range(1, g_in.shape[0]):
                g = g + g_in[k].astype(F32)
            g_ref[...] = g
            d_ref[...], m_ref[...], v_ref[...] = _adam_update(g, w, m, v)

    spec = pl.BlockSpec((tr, cols), lambda i: (i, 0))
    args, in_specs = [], []
    for i in range(n):
        slots, first = gs[i]
        args += [slots, ws[i], ms[i], vs[i]]
        in_specs += [pl.BlockSpec((slots.shape[0], tr, cols), lambda i, b=first // tr: (0, b + i, 0))] + [spec] * 3
    outs = _call(body, name=name, grid=(rows // tr,), args=args, in_specs=in_specs, after=after,
                 out_shape=[jax.ShapeDtypeStruct((rows, cols), F32)] * (4 * n), out_specs=[spec] * (4 * n))
    return [tuple(outs[4 * i: 4 * i + 4]) for i in range(n)]


def kernel(x, ffn1_norm, ffn1_w_gate, ffn1_w_up, ffn1_w_down, mix_norm, w_in, a_dw_w, a_dw_b, a_ln_g, a_ln_b, a_w_out, b_conv_w, b_w_out, w_o, ffn2_norm, ffn2_w_gate, ffn2_w_up, ffn2_w_down, final_norm, loss_target, m_ffn1_norm, m_ffn1_w_gate, m_ffn1_w_up, m_ffn1_w_down, m_mix_norm, m_w_in, m_a_dw_w, m_a_dw_b, m_a_ln_g, m_a_ln_b, m_a_w_out, m_b_conv_w, m_b_w_out, m_w_o, m_ffn2_norm, m_ffn2_w_gate, m_ffn2_w_up, m_ffn2_w_down, m_final_norm, v_ffn1_norm, v_ffn1_w_gate, v_ffn1_w_up, v_ffn1_w_down, v_mix_norm, v_w_in, v_a_dw_w, v_a_dw_b, v_a_ln_g, v_a_ln_b, v_a_w_out, v_b_conv_w, v_b_w_out, v_w_o, v_ffn2_norm, v_ffn2_w_gate, v_ffn2_w_up, v_ffn2_w_down, v_final_norm):
    names = ("ffn1_norm", "ffn1_w_gate", "ffn1_w_up", "ffn1_w_down", "mix_norm", "w_in", "a_dw_w", "a_dw_b",
             "a_ln_g", "a_ln_b", "a_w_out", "b_conv_w", "b_w_out", "w_o", "ffn2_norm", "ffn2_w_gate", "ffn2_w_up",
             "ffn2_w_down", "final_norm")
    w = dict(ffn1_norm=ffn1_norm, ffn1_w_gate=ffn1_w_gate, ffn1_w_up=ffn1_w_up, ffn1_w_down=ffn1_w_down,
             mix_norm=mix_norm, w_in=w_in, a_dw_w=a_dw_w, a_dw_b=a_dw_b, a_ln_g=a_ln_g, a_ln_b=a_ln_b,
             a_w_out=a_w_out, b_conv_w=b_conv_w, b_w_out=b_w_out, w_o=w_o, ffn2_norm=ffn2_norm,
             ffn2_w_gate=ffn2_w_gate, ffn2_w_up=ffn2_w_up, ffn2_w_down=ffn2_w_down, final_norm=final_norm)
    m = dict(ffn1_norm=m_ffn1_norm, ffn1_w_gate=m_ffn1_w_gate, ffn1_w_up=m_ffn1_w_up, ffn1_w_down=m_ffn1_w_down,
             mix_norm=m_mix_norm, w_in=m_w_in, a_dw_w=m_a_dw_w, a_dw_b=m_a_dw_b, a_ln_g=m_a_ln_g, a_ln_b=m_a_ln_b,
             a_w_out=m_a_w_out, b_conv_w=m_b_conv_w, b_w_out=m_b_w_out, w_o=m_w_o, ffn2_norm=m_ffn2_norm,
             ffn2_w_gate=m_ffn2_w_gate, ffn2_w_up=m_ffn2_w_up, ffn2_w_down=m_ffn2_w_down, final_norm=m_final_norm)
    v = dict(ffn1_norm=v_ffn1_norm, ffn1_w_gate=v_ffn1_w_gate, ffn1_w_up=v_ffn1_w_up, ffn1_w_down=v_ffn1_w_down,
             mix_norm=v_mix_norm, w_in=v_w_in, a_dw_w=v_a_dw_w, a_dw_b=v_a_dw_b, a_ln_g=v_a_ln_g, a_ln_b=v_a_ln_b,
             a_w_out=v_a_w_out, b_conv_w=v_b_conv_w, b_w_out=v_b_w_out, w_o=v_w_o, ffn2_norm=v_ffn2_norm,
             ffn2_w_gate=v_ffn2_w_gate, ffn2_w_up=v_ffn2_w_up, ffn2_w_down=v_ffn2_w_down, final_norm=v_final_norm)
    flat = _pack_weights(dict(wg1=ffn1_w_gate[0].T, wu1=ffn1_w_up[0].T, wd1=ffn1_w_down[0], wg2=ffn2_w_gate[0].T,
                              wu2=ffn2_w_up[0].T, wd2=ffn2_w_down[0], win=w_in[0], wa=a_w_out[0], wb=b_w_out[0],
                              wo=w_o[0]))
    cw_shard = jnp.concatenate([a_dw_w[0], jnp.zeros((1, LANE), F32), b_conv_w[0], jnp.zeros((5, LANE), F32)], axis=0)

    x2, tgt = x[0], loss_target[0]
    st_a, st_b, st_b2 = ("wg1", "wu1"), ("wd1", "win/0/2"), ("win/1/2",)
    st_c, st_d, st_e = ("wa", "wb", "wo", "wg2"), ("wu2",), ("wd2",)

    buf_a, cw = _run_comm(_join(_ag_comm(st_a, flat), _direct_comm(cw_shard, False)), "ag_ffn1")
    n1, gg1, uu1, act1, buf_b = _ffn_gate_up(x2, ffn1_norm, (buf_a, buf_a), (0, F), "ffn1_gate_up", _ag_comm(st_b, flat))
    h1, buf_b2 = _ffn_down(x2, act1, buf_b, 0, "ffn1_down", _ag_comm(st_b2, flat))
    win = ((buf_b, F), (buf_b2, 0))
    u, z, buf_c = _mix_in(h1, mix_norm, win, _ag_comm(st_c, flat))
    dft = _dft_constants()
    cw = jnp.transpose(cw, (1, 0, 2)).reshape(40, D)
    a1, q, buf_d = _conv_fwd_dft(z, cw, a_dw_b, dft, _ag_comm(st_d, flat))
    h2, ya, yb, buf_e = _mix_out(a1, q, z, h1, a_ln_g, a_ln_b, buf_c, _ag_comm(st_e, flat))
    ffn2_bufs, ffn2_offs = (buf_c, buf_d, buf_e), (3 * D, 0, 0)
    dh3, dhb3, s_final, n2, gg2, uu2, act2 = _ffn_fwd(h2, ffn2_norm, ffn2_bufs, ffn2_offs, "ffn2_fwd",
                                          final=(final_norm.reshape(1, D), tgt))

    tr_f = F // 2 if (F // 2) % LANE == 0 else F
    def pair(stage, src):
        return _rs_pair_comm(stage, src)

    def chip(stage, src, pair_buf, tag):
        return _rs_chip_comm(_pair_add(stage, src, pair_buf, "pair_add_" + tag))

    (dgu2,) = _ffn_bwd_hidden(dhb3, gg2, uu2, buf_e, 0, "ffn2_bwd_h")
    (gu2,) = _tn_matmul(dgu2, n2, tr_f, "dw_gu2")
    s2a, src2a = ("wg2", "wu2"), dict(wg2=(gu2, 0), wu2=(gu2, F))
    gd2, pair2a = _tn_matmul(act2, dhb3, tr_f, "dw_d2", pair(s2a, src2a))
    s2b, src2b = ("wd2",), dict(wd2=(gd2, 0))
    dh2, s_ffn2, pair2b = _ffn_bwd_input(dgu2, dh3, h2, ffn2_norm, (buf_c, buf_d), (3 * D, 0), "ffn2_bwd_x",
                                         pair(s2b, src2b))
    def behind(comm, tag, work):
        sems, bufs, token = _comm_start(comm, "xchg_" + tag + "_start")
        res = work(token)
        ins, lands = _comm_wait(comm, "xchg_" + tag + "_wait", sems, bufs, res[-1])
        return res, ins, lands

    (dzg, da1, dq, lsq, rsq, s_mix), _, (recv2b,) = behind(
        chip(s2b, src2b, pair2b, "2b"), "2b",
        lambda tok: _mix_out_bwd(dh2, ya, yb, z, a1, q, a_ln_g, a_ln_b, buf_c, after=tok))
    (gsq,) = _tn_matmul(lsq, rsq, D, "dw_square")
    ssq, srcsq = ("wa", "wb", "wo"), dict(wa=(gsq, D), wb=(gsq, 2 * D), wo=(gsq, 0))
    (dz, dwa, dwb), (_, gsq), (recv2a, pairsq) = behind(
        _join(chip(s2a, src2a, pair2a, "2a"), pair(ssq, srcsq)), "2a",
        lambda tok: _conv_bwd_dft(z, da1, dq, dzg, cw, dft, after=tok))
    srcsq = dict(wa=(gsq, D), wb=(gsq, 2 * D), wo=(gsq, 0))
    (gin,), _, (recvsq,) = behind(chip(ssq, srcsq, pairsq, "sq"), "sq",
                                  lambda tok: _tn_matmul(dz, u, D, "dw_in", after=tok))
    sin_a, sin_b, srcin = ("win/0/2",), ("win/1/2",), {"win/0/2": (gin, 0), "win/1/2": (gin, 0)}
    dh1, dhb1, s_in, pairin_a, pairin_b = _mix_in_bwd(dz, dh2, h1, mix_norm, win,
                                                _join(pair(sin_a, srcin), pair(sin_b, srcin)))
    (dgu1,), _, (recvin_a,) = behind(chip(sin_a, srcin, pairin_a, "in_a"), "in_a",
                                     lambda tok: _ffn_bwd_hidden(dhb1, gg1, uu1, buf_b, 0, "ffn1_bwd_h", after=tok))
    (gu1,), _, (recvin_b,) = behind(chip(sin_b, srcin, pairin_b, "in_b"), "in_b",
                                    lambda tok: _tn_matmul(dgu1, n1, tr_f, "dw_gu1", after=tok))
    s1a, src1a = ("wg1", "wu1"), dict(wg1=(gu1, 0), wu1=(gu1, F))
    gd1, pair1a = _tn_matmul(act1, dhb1, tr_f, "dw_d1", pair(s1a, src1a))
    s1b, src1b = ("wd1",), dict(wd1=(gd1, 0))
    (dx, s_ffn1), (_, gd1), (recv1a, pair1b) = behind(
        _join(chip(s1a, src1a, pair1a, "1a"), pair(s1b, src1b)), "1a",
        lambda tok: _ffn_bwd_input(dgu1, dh1, x2, ffn1_norm, (buf_a, buf_a), (0, F), "ffn1_bwd_x", after=tok))
    src1b = dict(wd1=(gd1, 0))

    vec8, convk = _pack_small(s_ffn1, s_in, s_mix, s_ffn2, s_final, dwa, dwb)
    tail = _join(chip(s1b, src1b, pair1b, "1b"), _join(_direct_comm(vec8, False), _direct_comm(convk, True)))
    tail_sems, tail_bufs, token = _comm_start(tail, "xchg_tail_start")

    fs = F // NDEV
    g = dict(ffn1_w_gate=(recv1a, 0), ffn1_w_up=(recv1a, fs), ffn2_w_gate=(recv2a, 0), ffn2_w_up=(recv2a, fs),
             ffn2_w_down=(recv2b, 0), a_w_out=(recvsq, 0), b_w_out=(recvsq, D // NDEV), w_o=(recvsq, 2 * (D // NDEV)))
    grad, upd = {}, {}

    def run(group, name, after, as2d=lambda a: a[0], back=lambda a, n: a.reshape(w[n].shape)):
        res = _adam([g[n] for n in group], [as2d(w[n]) for n in group], [as2d(m[n]) for n in group],
                    [as2d(v[n]) for n in group], name, after)
        for n, r in zip(group, res):
            grad[n], upd[n] = back(r[0], n), tuple(back(a, n) for a in r[1:])
        return res[0][0]

    done = run(("ffn1_w_gate", "ffn1_w_up", "ffn2_w_gate", "ffn2_w_up"), "adam_gate_up", token,
               as2d=lambda a: a[0].T, back=lambda a, n: a.T[None])
    r_in = _adam_in_proj([recvin_a, recvin_b], w_in[0], m_w_in[0], v_w_in[0], done)
    grad["w_in"], upd["w_in"] = r_in[0][None], tuple(a[None] for a in r_in[1:])
    done = run(("a_w_out", "b_w_out", "w_o"), "adam_square", r_in[0])
    _, (recv1b, vec_all, conv_all) = _comm_wait(tail, "xchg_tail_wait", tail_sems, tail_bufs, done)
    g["ffn1_w_down"] = (recv1b, 0)
    run(("ffn1_w_down", "ffn2_w_down"), "adam_down", done)
    vec_names = ("ffn1_norm", "mix_norm", "a_ln_g", "a_ln_b", "a_dw_b", "ffn2_norm", "final_norm")
    tap_names, tap_rows = ("a_dw_w", "b_conv_w"), (0, 32)
    taps = lambda a: jnp.transpose(a, (1, 0, 2))
    loss, vec_res, tap_res = _adam_small(
        vec_all, conv_all, [tuple(t[n].reshape(1, D) for t in (w, m, v)) for n in vec_names],
        [(r,) + tuple(taps(t[n]) for t in (w, m, v)) for n, r in zip(tap_names, tap_rows)])
    for n, r in zip(vec_names, vec_res):
        grad[n], upd[n] = r[0].reshape(w[n].shape), tuple(a.reshape(w[n].shape) for a in r[1:])
    for n, r in zip(tap_names, tap_res):
        grad[n], upd[n] = taps(r[0]), tuple(taps(a) for a in r[1:])

    return (loss.reshape(()), dx.reshape(x.shape), *[grad[n] for n in names], *[upd[n][0] for n in names],
            *[upd[n][1] for n in names], *[upd[n][2] for n in names])
```

```python
import jax
import jax.numpy as jnp
from jax import lax
from jax.experimental import pallas as pl
from jax.experimental.pallas import tpu as pltpu

T = 4096
D = 1024
F = 2816
NG = 7
NDEV = 8
NCHIP = 4
KA, KB = 31, 3
EPS = 1e-6
ADAM_LR, ADAM_B1, ADAM_B2, ADAM_EPS, ADAM_WD, ADAM_STEP = 0.001, 0.9, 0.999, 1e-08, 0.01, 10

TM = 512
FC = 256
TB = 1024
NB = 256
HB = NB // 2
CW = 256
CHB = 64
LANE = 128
TK = 2048
VMEM_LIMIT = 56 * 1024 * 1024

BF = jnp.bfloat16
F32 = jnp.float32
MESH = pl.DeviceIdType.MESH
ANY = pl.BlockSpec(memory_space=pl.ANY)

ORDER = ("wg1", "wu1", "wd1", "wg2", "wu2", "wd2", "win", "wa", "wb", "wo")


class _Layout:
    def __init__(self):
        fs, dis, ds = F // NDEV, NG * D // NDEV, D // NDEV
        self.rows = dict(wg1=fs, wu1=fs, wd1=fs, wg2=fs, wu2=fs, wd2=fs, win=dis, wa=ds, wb=ds, wo=ds)
        self.fl, off = {}, 0
        for n in ORDER:
            self.fl[n] = off
            off += self.rows[n]
        self.RT = off


class _Stage:
    def __init__(self, names):
        lay = _Layout()
        self.names = names
        self.rows, self.full, self.sub, self.fl = {}, {}, {}, {}
        for n in names:
            base, i, k = (n.split("/") + ["0", "1"])[:3]
            self.full[n] = lay.rows[base]
            self.rows[n] = lay.rows[base] // int(k)
            self.sub[n] = int(i) * self.rows[n]
            self.fl[n] = lay.fl[base] + self.sub[n]
        self.off, self.wc, o, w = {}, {}, 0, 0
        for n in names:
            self.off[n], self.wc[n] = o, w
            o += self.rows[n]
            w += NDEV * self.rows[n]
        self.R, self.W = o, w

    def grad_row(self, n, first, dev_lin):
        return first + dev_lin * self.full[n] + self.sub[n]


def _nt(a, b):
    return lax.dot_general(a, b, (((1,), (1,)), ((), ())), preferred_element_type=F32)


def _nn(a, b):
    return lax.dot_general(a, b, (((1,), (0,)), ((), ())), preferred_element_type=F32)


def _tn(a, b):
    return lax.dot_general(a, b, (((0,), (0,)), ((), ())), preferred_element_type=F32)


def _sig(x):
    return 1.0 / (1.0 + jnp.exp(-x))


def _position():
    return lax.axis_index("x"), lax.axis_index("y"), lax.axis_index("c")


def _peer(pos, j):
    x, y, c = pos
    return (1 - x if j & 4 else x, 1 - y if j & 2 else y, 1 - c if j & 1 else c)


def _lin(pos):
    return 4 * pos[0] + 2 * pos[1] + pos[2]


def _chip(pos):
    return 2 * pos[0] + pos[1]


class _Comm:
    def __init__(self, inputs, out_shapes, scratch, start, finish, middle=None):
        self.inputs, self.out_shapes, self.scratch = inputs, out_shapes, scratch
        self.start, self.finish, self.middle = start, finish, middle


def _call(body, *, name, grid, args, in_specs, out_shape, out_specs, scratch_shapes=(), comm=None,
          num_scalar_prefetch=0, after=None):
    in_specs, out_shape, out_specs, scratch_shapes = list(in_specs), list(out_shape), list(out_specs), list(scratch_shapes)
    if after is not None:
        inner, pos = body, num_scalar_prefetch + len(in_specs)
        body = lambda *refs: inner(*refs[:pos], *refs[pos + 1:])
        args, in_specs = list(args) + [after], in_specs + [ANY]
    n_in, n_out, n_scr = len(in_specs), len(out_shape), len(scratch_shapes)
    sp = num_scalar_prefetch
    if comm is None:
        kernel_fn = lambda *refs: body(*refs)
        c_in = c_out = c_scr = 0
    else:
        c_in, c_out, c_scr = len(comm.inputs), len(comm.out_shapes), len(comm.scratch)

        def kernel_fn(*refs):
            pre, refs = refs[:sp], refs[sp:]
            ins, cins = refs[:n_in], refs[n_in:n_in + c_in]
            o0 = n_in + c_in
            outs, couts = refs[o0:o0 + n_out], refs[o0 + n_out:o0 + n_out + c_out]
            s0 = o0 + n_out + c_out
            scr, cscr = refs[s0:s0 + n_scr], refs[s0 + n_scr:]
            step, steps = pl.program_id(0), grid[0]
            for a in range(1, len(grid)):
                step, steps = step * grid[a] + pl.program_id(a), steps * grid[a]
            first, last = step == 0, step == steps - 1

            @pl.when(first)
            def _():
                comm.start(cins, couts, cscr)

            if comm.middle is not None:
                @pl.when(step == (steps // 2 if steps > 2 else steps - 1))
                def _():
                    comm.middle(cins, couts, cscr)

            body(*pre, *ins, *outs, *scr)

            @pl.when(last)
            def _():
                comm.finish(cins, couts, cscr)

        args = list(args) + list(comm.inputs)
        in_specs += [ANY] * c_in
        out_shape += list(comm.out_shapes)
        out_specs += [ANY] * c_out
        scratch_shapes += list(comm.scratch)
    params = pltpu.CompilerParams(dimension_semantics=("arbitrary",) * len(grid), vmem_limit_bytes=VMEM_LIMIT)
    if sp:
        grid_spec = pltpu.PrefetchScalarGridSpec(num_scalar_prefetch=sp, grid=grid, in_specs=in_specs,
                                                 out_specs=out_specs, scratch_shapes=scratch_shapes)
        return pl.pallas_call(kernel_fn, name=name, grid_spec=grid_spec, out_shape=out_shape,
                              compiler_params=params)(*args)
    return pl.pallas_call(kernel_fn, name=name, grid=grid, in_specs=in_specs, out_shape=out_shape, out_specs=out_specs,
                          scratch_shapes=scratch_shapes, compiler_params=params)(*args)


def _join(a, b):
    na = (len(a.inputs), len(a.out_shapes), len(a.scratch))

    def split(refs):
        return ([r[:n] for r, n in zip(refs, na)], [r[n:] for r, n in zip(refs, na)])

    def start(*refs):
        ra, rb = split(refs)
        a.start(*ra)
        b.start(*rb)

    def finish(*refs):
        ra, rb = split(refs)
        a.finish(*ra)
        b.finish(*rb)

    def middle(*refs):
        for stage, r in zip((a, b), split(refs)):
            if stage.middle is not None:
                stage.middle(*r)

    return _Comm(list(a.inputs) + list(b.inputs), list(a.out_shapes) + list(b.out_shapes),
                 list(a.scratch) + list(b.scratch), start, finish,
                 middle if (a.middle is not None or b.middle is not None) else None)


def _run_comm(comm, name):
    def body(*refs):
        c_in, c_out = len(comm.inputs), len(comm.out_shapes)
        parts = (refs[:c_in], refs[c_in:c_in + c_out], refs[c_in + c_out:])
        comm.start(*parts)
        if comm.middle is not None:
            comm.middle(*parts)
        comm.finish(*parts)

    return pl.pallas_call(
        body, name=name, out_shape=list(comm.out_shapes), in_specs=[ANY] * len(comm.inputs),
        out_specs=[ANY] * len(comm.out_shapes), scratch_shapes=list(comm.scratch))(*comm.inputs)


HBM = pl.BlockSpec(memory_space=pltpu.HBM)
SEM = pl.BlockSpec(memory_space=pltpu.SEMAPHORE)
DATAFLOW = pltpu.SideEffectType.DATAFLOW_SIDE_EFFECTING


def _comm_start(comm, name):
    c_in, c_out = len(comm.inputs), len(comm.out_shapes)
    sems = [s(()) if s is pltpu.SemaphoreType.DMA else s for s in comm.scratch]
    bufs = list(comm.inputs) + [lax.empty(s.shape, s.dtype) for s in comm.out_shapes]

    def body(*refs):
        sem_refs = refs[c_in + c_out:c_in + c_out + len(sems)]
        comm.start(refs[:c_in], refs[c_in:c_in + c_out], sem_refs)
        refs[-1][...] = jnp.zeros_like(refs[-1])

    outs = pl.pallas_call(
        body, name=name,
        out_shape=sems + [pltpu.HBM(b.shape, b.dtype) for b in bufs] + [jax.ShapeDtypeStruct((8, LANE), F32)],
        in_specs=[HBM] * len(bufs),
        out_specs=[SEM] * len(sems) + [HBM] * len(bufs) + [pl.BlockSpec(memory_space=pltpu.VMEM)],
        input_output_aliases={i: len(sems) + i for i in range(len(bufs))},
        compiler_params=pltpu.CompilerParams(has_side_effects=DATAFLOW),
    )(*[pltpu.with_memory_space_constraint(b, pltpu.HBM) for b in bufs])
    return outs[:len(sems)], outs[len(sems):-1], outs[-1]


def _comm_wait(comm, name, sems, bufs, after):
    c_in, c_out = len(comm.inputs), len(comm.out_shapes)

    def body(*refs):
        sem_refs = refs[c_in + c_out:c_in + c_out + len(sems)]
        comm.finish(refs[:c_in], refs[c_in:c_in + c_out], sem_refs)

    outs = pl.pallas_call(
        body, name=name, out_shape=[pltpu.HBM(b.shape, b.dtype) for b in bufs],
        in_specs=[HBM] * len(bufs) + [SEM] * len(sems) + [ANY], out_specs=[HBM] * len(bufs),
        input_output_aliases={i: i for i in range(len(bufs))},
        compiler_params=pltpu.CompilerParams(has_side_effects=DATAFLOW),
    )(*bufs, *sems, after)
    return outs[:c_in], outs[c_in:]


def _ag_comm(names, flat):
    st = _Stage(names)

    def ring(me):
        x, y, c = me
        diagonal = x == y
        up = (jnp.where(diagonal, x, 1 - x), jnp.where(diagonal, 1 - y, y), c)
        down = (jnp.where(diagonal, 1 - x, x), jnp.where(diagonal, y, 1 - y), c)
        low = c == 0
        passed = tuple(jnp.where(low, d, u) for d, u in zip(down, up))
        target = tuple(jnp.where(low, u, d) for d, u in zip(down, up))
        return up, down, (1 - x, 1 - y, c), passed, target

    def parts(refs):
        (flat_ref,), (out_ref,), (send_sems, recv_sems, local_sem) = refs
        me = _position()

        def region(name, dev):
            r = st.rows[name]
            return out_ref.at[pl.ds(st.wc[name] + _lin(dev) * r, r), :]

        def own(name):
            return flat_ref.at[pl.ds(st.fl[name], st.rows[name]), :]

        def copies(k, dev, to, from_flat):
            return [pltpu.make_async_remote_copy(
                src_ref=own(n) if from_flat else region(n, dev), dst_ref=region(n, dev), send_sem=send_sems.at[k],
                recv_sem=recv_sems.at[k], device_id=to, device_id_type=MESH) for n in names]

        def whole(k):
            return pltpu.make_async_remote_copy(
                src_ref=flat_ref.at[pl.ds(0, st.R), :], dst_ref=out_ref.at[pl.ds(0, st.R), :],
                send_sem=send_sems.at[k], recv_sem=recv_sems.at[k], device_id=me, device_id_type=MESH)

        return me, region, own, copies, whole, flat_ref, out_ref, local_sem

    def start(*refs):
        me, region, own, copies, _, _, _, local_sem = parts(refs)
        for n in names:
            pltpu.make_async_copy(own(n), region(n, me), local_sem).start()
        up, down, _, _, _ = ring(me)
        for k, to in ((1, up), (2, down), (0, _peer(me, 1))):
            for cp in copies(k, me, to, True):
                cp.start()

    def middle(*refs):
        me, _, _, copies, whole, _, _, _ = parts(refs)
        up, down, _, passed, target = ring(me)
        sib = _peer(me, 1)
        whole(1).wait_recv()
        whole(2).wait_recv()
        for k, dev, to in ((3, passed, target), (4, down, sib), (5, up, sib)):
            for cp in copies(k, dev, to, False):
                cp.start()

    def finish(*refs):
        me, _, _, copies, whole, flat_ref, out_ref, local_sem = parts(refs)
        _, _, across, _, _ = ring(me)
        whole(3).wait_recv()
        for cp in copies(6, across, _peer(me, 1), False):
            cp.start()
        whole(0).wait_recv()
        for j in range(3):
            whole(4 + j).wait_recv()
        for k in range(7):
            whole(k).wait_send()
        pltpu.make_async_copy(flat_ref.at[pl.ds(0, st.R), :], out_ref.at[pl.ds(0, st.R), :], local_sem).wait()

    return _Comm([flat], [jax.ShapeDtypeStruct((st.W, D), BF)],
                 [pltpu.SemaphoreType.DMA((7,)), pltpu.SemaphoreType.DMA((7,)), pltpu.SemaphoreType.DMA],
                 start, finish, middle)


def _rs_pair_comm(names, src):
    st = _Stage(names)
    arrays = []
    for n in names:
        if not any(src[n][0] is a for a in arrays):
            arrays.append(src[n][0])
    idx = {n: [i for i, a in enumerate(arrays) if a is src[n][0]][0] for n in names}

    def slot_wait(refs):
        recv = refs[1][0]
        send_sem, recv_sem = refs[2]
        return pltpu.make_async_remote_copy(src_ref=recv, dst_ref=recv, send_sem=send_sem, recv_sem=recv_sem,
                                            device_id=_position(), device_id_type=MESH)

    def start(*refs):
        ins, (recv,), (send_sem, recv_sem) = refs
        me = _position()
        sib = _peer(me, 1)
        for q in range(NCHIP):
            dev = (q // 2, q % 2, sib[2])
            for n in names:
                r = st.rows[n]
                pltpu.make_async_remote_copy(
                    src_ref=ins[idx[n]].at[pl.ds(st.grad_row(n, src[n][1], _lin(dev)), r), :],
                    dst_ref=recv.at[q, pl.ds(st.off[n], r), :], send_sem=send_sem, recv_sem=recv_sem,
                    device_id=sib, device_id_type=MESH).start()

    def finish(*refs):
        w = slot_wait(refs)
        w.wait_recv()
        w.wait_send()

    return _Comm(arrays, [jax.ShapeDtypeStruct((NCHIP, st.R, D), BF)],
                 [pltpu.SemaphoreType.DMA, pltpu.SemaphoreType.DMA], start, finish)


def _pair_add(names, src, recv, name):
    st = _Stage(names)
    c_arr = jnp.reshape(lax.axis_index("c"), (1,)).astype(jnp.int32)

    def body(c_ref, *refs):
        r_ref, o_ref = refs[len(names)], refs[len(names) + 1]
        for a_ref, n in zip(refs, names):
            rows = slice(st.off[n], st.off[n] + st.rows[n])
            o_ref[rows, :] = (a_ref[...].astype(F32) + r_ref[rows, :].astype(F32)).astype(BF)

    def shard_spec(n):
        r = st.rows[n]
        base, step = st.grad_row(n, src[n][1], 0) // r, st.full[n] // r
        return pl.BlockSpec((r, D), lambda q, c_ref: (base + step * (2 * q + c_ref[0]), 0))

    slot = pl.BlockSpec((None, st.R, D), lambda q, c_ref: (q, 0, 0))
    return _call(body, name=name, grid=(NCHIP,), args=[c_arr] + [src[n][0] for n in names] + [recv],
                 in_specs=[shard_spec(n) for n in names] + [slot],
                 out_shape=[jax.ShapeDtypeStruct((NCHIP, st.R, D), BF)], out_specs=[slot], num_scalar_prefetch=1)[0]


def _rs_chip_comm(part):
    def copies(refs):
        (p_ref,), (recv,), (send_sems, recv_sems, local_sem) = refs
        me = _position()
        mine = pltpu.make_async_copy(p_ref.at[_chip(me)], recv.at[_chip(me)], local_sem)
        out = []
        for j, bits in enumerate((4, 2, 6)):
            to = _peer(me, bits)
            out.append(pltpu.make_async_remote_copy(
                src_ref=p_ref.at[_chip(to)], dst_ref=recv.at[_chip(me)], send_sem=send_sems.at[j],
                recv_sem=recv_sems.at[j], device_id=to, device_id_type=MESH))
        return mine, out

    def start(*refs):
        mine, out = copies(refs)
        mine.start()
        for cp in out:
            cp.start()

    def finish(*refs):
        mine, out = copies(refs)
        for cp in out:
            cp.wait_recv()
        for cp in out:
            cp.wait_send()
        mine.wait()

    return _Comm([part], [jax.ShapeDtypeStruct(part.shape, BF)],
                 [pltpu.SemaphoreType.DMA((3,)), pltpu.SemaphoreType.DMA((3,)), pltpu.SemaphoreType.DMA],
                 start, finish)


def _direct_comm(x, scatter):
    def copies(refs):
        (x_ref,), (out_ref,), (send_sems, recv_sems, local_sem) = refs
        me = _position()

        def piece(dev):
            return x_ref.at[_lin(dev)] if scatter else x_ref

        mine = pltpu.make_async_copy(piece(me), out_ref.at[_lin(me)], local_sem)
        return mine, [pltpu.make_async_remote_copy(
            src_ref=piece(_peer(me, j)), dst_ref=out_ref.at[_lin(me)], send_sem=send_sems.at[j - 1],
            recv_sem=recv_sems.at[j - 1], device_id=_peer(me, j), device_id_type=MESH) for j in range(1, NDEV)]

    def start(*refs):
        mine, cps = copies(refs)
        mine.start()
        for cp in cps:
            cp.start()

    def finish(*refs):
        mine, cps = copies(refs)
        for cp in cps:
            cp.wait_recv()
        for cp in cps:
            cp.wait_send()
        mine.wait()

    shape = x.shape if scatter else (NDEV,) + x.shape
    return _Comm([x], [jax.ShapeDtypeStruct(shape, x.dtype)],
                 [pltpu.SemaphoreType.DMA((7,)), pltpu.SemaphoreType.DMA((7,)), pltpu.SemaphoreType.DMA],
                 start, finish)


def _pack_weights(shards):
    lay = _Layout()

    def body(*refs):
        o_ref = refs[-1]
        for ref, n in zip(refs, ORDER):
            x = ref[...].T if n == "win" else ref[...]
            o_ref[lay.fl[n]:lay.fl[n] + lay.rows[n], :] = x.astype(BF)

    return pl.pallas_call(
        body, name="pack_weights", out_shape=jax.ShapeDtypeStruct((lay.RT, D), BF),
        compiler_params=pltpu.CompilerParams(vmem_limit_bytes=VMEM_LIMIT))(*[shards[n] for n in ORDER])


def _load_ffn_weights(srcs, offs, scratch, sem):
    @pl.when(pl.program_id(0) == 0)
    def _():
        cps = [pltpu.make_async_copy(s.at[pl.ds(off, dst.shape[0]), :], dst, sem.at[i])
               for i, (s, off, dst) in enumerate(zip(srcs, offs, scratch))]
        for cp in cps:
            cp.start()
        for cp in cps:
            cp.wait()


def _final_loss_tile(xf, g, tgt, s_ref):
    r = lax.rsqrt(jnp.mean(xf * xf, axis=-1, keepdims=True) + EPS)
    xr = xf * r
    e = xr * g - tgt
    s_ref[1:2, :] += jnp.sum(e * e, axis=0, keepdims=True) * (0.5 / D)
    dy = e * (1.0 / D)
    s_ref[0:1, :] += jnp.sum(dy * xr, axis=0, keepdims=True)
    gdy = dy * g
    return r * gdy - xr * (r * jnp.mean(gdy * xr, axis=-1, keepdims=True))


def _ffn_fwd(x, g, wbufs, offs, name, comm=None, final=None):
    nf = F // FC

    def body(x_ref, g_ref, b0, b1, b2, *rest):
        if final is None:
            h_ref, n_ref, gg_ref, uu_ref, a_ref, wg_s, wu_s, wd_s, sem = rest
        else:
            gf_ref, t_ref, dh_ref, dhb_ref, s_ref, n_ref, gg_ref, uu_ref, a_ref, wg_s, wu_s, wd_s, sem = rest

            @pl.when(pl.program_id(0) == 0)
            def _():
                s_ref[...] = jnp.zeros_like(s_ref)

        _load_ffn_weights((b0, b1, b2), offs, (wg_s, wu_s, wd_s), sem)
        xf = x_ref[...]
        r = lax.rsqrt(jnp.mean(xf * xf, axis=-1, keepdims=True) + EPS)
        nb = (xf * r * g_ref[...]).astype(BF)
        n_ref[...] = nb
        acc = jnp.zeros((TM, D), F32)
        for c in range(nf):
            sl = slice(c * FC, (c + 1) * FC)
            gb = _nt(nb, wg_s[sl, :]).astype(BF)
            ub = _nt(nb, wu_s[sl, :]).astype(BF)
            gg_ref[:, sl] = gb
            uu_ref[:, sl] = ub
            a = (gb * _sig(gb)) * ub
            a_ref[0, :, sl] = a
            acc = acc + _nn(a, wd_s[sl, :])
        h = xf + 0.5 * acc
        if final is None:
            h_ref[...] = h
        else:
            dh = _final_loss_tile(h, gf_ref[...], t_ref[...], s_ref)
            dh_ref[...] = dh
            dhb_ref[...] = (0.5 * dh).astype(BF)

    row = lambda i: (i, 0)
    vec = pl.BlockSpec((1, D), lambda i: (0, 0))
    tile = pl.BlockSpec((TM, D), row)
    saved_shapes = [jax.ShapeDtypeStruct((T, D), BF), jax.ShapeDtypeStruct((T, F), BF), jax.ShapeDtypeStruct((T, F), BF),
                    jax.ShapeDtypeStruct((1, T, F), BF)]
    saved_specs = [tile, pl.BlockSpec((TM, F), row), pl.BlockSpec((TM, F), row),
                   pl.BlockSpec((1, TM, F), lambda i: (0, i, 0))]
    if final is None:
        extra_args, extra_specs = [], []
        head_shapes, head_specs = [jax.ShapeDtypeStruct((T, D), F32)], [tile]
    else:
        extra_args, extra_specs = list(final), [vec, tile]
        head_shapes = [jax.ShapeDtypeStruct((T, D), F32), jax.ShapeDtypeStruct((T, D), BF), jax.ShapeDtypeStruct((8, D), F32)]
        head_specs = [tile, tile, pl.BlockSpec((8, D), lambda i: (0, 0))]
    return _call(
        body, name=name, grid=(T // TM,), args=[x, g, *wbufs, *extra_args], comm=comm,
        in_specs=[tile, vec, ANY, ANY, ANY] + extra_specs,
        out_shape=head_shapes + saved_shapes, out_specs=head_specs + saved_specs,
        scratch_shapes=[pltpu.VMEM((F, D), BF)] * 3 + [pltpu.SemaphoreType.DMA((3,))])


def _ffn_gate_up(x, g, wbufs, offs, name, comm=None):
    nf = F // FC

    def body(x_ref, g_ref, b0, b1, n_ref, gg_ref, uu_ref, a_ref, wg_s, wu_s, sem):
        _load_ffn_weights((b0, b1), offs, (wg_s, wu_s), sem)
        xf = x_ref[...]
        r = lax.rsqrt(jnp.mean(xf * xf, axis=-1, keepdims=True) + EPS)
        nb = (xf * r * g_ref[...]).astype(BF)
        n_ref[...] = nb
        for c in range(nf):
            sl = slice(c * FC, (c + 1) * FC)
            gb = _nt(nb, wg_s[sl, :]).astype(BF)
            ub = _nt(nb, wu_s[sl, :]).astype(BF)
            gg_ref[:, sl] = gb
            uu_ref[:, sl] = ub
            a_ref[0, :, sl] = (gb * _sig(gb)) * ub

    row = lambda i: (i, 0)
    tile = pl.BlockSpec((TM, D), row)
    return _call(
        body, name=name, grid=(T // TM,), args=[x, g, *wbufs], comm=comm,
        in_specs=[tile, pl.BlockSpec((1, D), lambda i: (0, 0)), ANY, ANY],
        out_shape=[jax.ShapeDtypeStruct((T, D), BF), jax.ShapeDtypeStruct((T, F), BF), jax.ShapeDtypeStruct((T, F), BF),
                   jax.ShapeDtypeStruct((1, T, F), BF)],
        out_specs=[tile, pl.BlockSpec((TM, F), row), pl.BlockSpec((TM, F), row),
                   pl.BlockSpec((1, TM, F), lambda i: (0, i, 0))],
        scratch_shapes=[pltpu.VMEM((F, D), BF)] * 2 + [pltpu.SemaphoreType.DMA((2,))])


def _ffn_down(x, act, wbuf, off, name, comm=None):
    def body(x_ref, a_ref, b0, h_ref, wd_s, sem):
        _load_ffn_weights((b0,), (off,), (wd_s,), sem)
        h_ref[...] = x_ref[...] + 0.5 * _nn(a_ref[0], wd_s[...])

    tile = pl.BlockSpec((TM, D), lambda i: (i, 0))
    return _call(
        body, name=name, grid=(T // TM,), args=[x, act, wbuf], comm=comm,
        in_specs=[tile, pl.BlockSpec((1, TM, F), lambda i: (0, i, 0)), ANY],
        out_shape=[jax.ShapeDtypeStruct((T, D), F32)], out_specs=[tile],
        scratch_shapes=[pltpu.VMEM((F, D), BF), pltpu.SemaphoreType.DMA((1,))])


def _load_in_proj(parts, w_s, sem):
    @pl.when(pl.program_id(0) == 0)
    def _():
        shard = NG * D // NDEV
        rows = shard // len(parts)
        cps = [pltpu.make_async_copy(buf.at[pl.ds(first + k * rows, rows), :],
                                     w_s.at[pl.ds(k * shard + p * rows, rows), :], sem.at[p * NDEV + k])
               for p, (buf, first) in enumerate(parts) for k in range(NDEV)]
        for cp in cps:
            cp.start()
        for cp in cps:
            cp.wait()


def _mix_in(h1, gm, win, comm=None):
    def body(h_ref, g_ref, *rest):
        w_any, (u_ref, z_ref, w_s, sem) = rest[:len(win)], rest[len(win):]
        _load_in_proj([(b, first) for b, (_, first) in zip(w_any, win)], w_s, sem)
        xf = h_ref[...]
        r = lax.rsqrt(jnp.mean(xf * xf, axis=-1, keepdims=True) + EPS)
        ub = (xf * r * g_ref[...]).astype(BF)
        u_ref[...] = ub
        for j in range(NG):
            z_ref[j] = _nt(ub, w_s[j * D:(j + 1) * D, :]).astype(BF)

    row = lambda i: (i, 0)
    return _call(
        body, name="mix_in", grid=(T // TM,), args=[h1, gm] + [b for b, _ in win], comm=comm,
        in_specs=[pl.BlockSpec((TM, D), row), pl.BlockSpec((1, D), lambda i: (0, 0))] + [ANY] * len(win),
        out_shape=[jax.ShapeDtypeStruct((T, D), BF), jax.ShapeDtypeStruct((NG, T, D), BF)],
        out_specs=[pl.BlockSpec((TM, D), row), pl.BlockSpec((NG, TM, D), lambda i: (0, i, 0))],
        scratch_shapes=[pltpu.VMEM((NG * D, D), BF), pltpu.SemaphoreType.DMA((NDEV * len(win),))])


def _shift_up(w, b):
    return w if b == 0 else pltpu.roll(w, w.shape[0] - b, 0)


def _fold8(p):
    red = p[0:8, :]
    for i in range(1, p.shape[0] // 8):
        red = red + p[8 * i:8 * i + 8, :]
    return red


def _dft_constants():
    import numpy as np
    nh = NB // 2
    f, n = np.arange(nh)[:, None], np.arange(NB)[None, :]
    ang = 2.0 * np.pi / NB * f * n
    fc = np.cos(ang)
    fs = np.where(f == 0, (-1.0) ** n, np.sin(ang))
    scale = np.where(f == 0, 1.0, 2.0) / NB
    ic = (scale * np.cos(ang)).T
    isn = np.where(f == 0, (-1.0) ** n / NB, scale * np.sin(ang)).T
    d = (KA - 1 - np.arange(32))[None, :]
    valid = (np.arange(32) < KA)[None, :]
    angk = 2.0 * np.pi / NB * f * d
    kc = np.where(valid, np.cos(angk), 0.0)
    ks = np.where(valid, np.sin(angk), 0.0)
    k2 = np.where(valid, np.where(f == 0, (-1.0) ** d, np.cos(angk)), 0.0)
    rtc = np.where(valid, scale * np.cos(angk), 0.0).T
    rts = np.where(valid, np.where(f == 0, (-1.0) ** d / NB, scale * np.sin(angk)), 0.0).T

    def bf(a):
        return jnp.asarray(a, F32).astype(BF)

    def split(a):
        hi = bf(a)
        return hi, (jnp.asarray(a, F32) - hi.astype(F32)).astype(BF)

    return dict(fc=bf(fc), fs=bf(fs), ic_hi=bf(ic[HB:]), is_hi=bf(isn[HB:]), ic_lo=bf(ic[:HB]), is_lo=bf(isn[:HB]),
                kc=split(kc), ks=split(ks), k2=split(k2), rtc=split(rtc), rts=split(rts))


def _dot3(m_hi, m_lo, x):
    x_hi = x.astype(BF)
    x_lo = (x - x_hi.astype(F32)).astype(BF)
    return _nn(m_hi, x_hi) + _nn(m_hi, x_lo) + _nn(m_lo, x_hi)


def _whole(a):
    return pl.BlockSpec(a.shape, lambda c, t: (0,) * a.ndim)


def _filter_spectrum(cw_ref, tabs, hc, hs, h2):
    w32 = cw_ref[0:32, :]
    for (hi, lo), dst in zip(tabs, (hc, hs, h2)):
        dst[...] = _dot3(hi[...], lo[...], w32)


def _conv_fwd_dft(z, cw, bias, dft, comm=None):
    nt = T // TB
    hb = TB // HB

    def body(z_ref, zh_ref, cw_ref, b_ref, fc_ref, fs_ref, ic_ref, is_ref, kch, kcl, ksh, ksl, k2h, k2l,
             a1_ref, q_ref, aext, ppad, hc, hs, h2):
        first = pl.program_id(1) == 0
        f = lambda ref, j: ref[j].astype(F32)

        @pl.when(first)
        def _():
            _filter_spectrum(cw_ref, ((kch, kcl), (ksh, ksl), (k2h, k2l)), hc, hs, h2)

        aext[0:HB, :] = jnp.where(first, 0.0, f(zh_ref, 0) * _sig(f(zh_ref, 1))).astype(BF)
        aext[HB:, :] = (f(z_ref, 0) * _sig(f(z_ref, 1))).astype(BF)
        ppad[0:8, :] = jnp.where(first, 0.0, f(zh_ref, 3)[HB - 8:HB, :] * f(zh_ref, 4)[HB - 8:HB, :])
        ppad[8:, :] = f(z_ref, 3) * f(z_ref, 4)
        bias_row = b_ref[...]

        for j in range(TB // HB):
            xs = aext[j * HB:j * HB + NB, :]
            xa, xb = _nn(fc_ref[...], xs), _nn(fs_ref[...], xs)
            yc = (hc[...] * xa - hs[...] * xb).astype(BF)
            ys = (h2[...] * xb + hs[...] * xa).astype(BF)
            y = _nn(ic_ref[...], yc) + _nn(is_ref[...], ys)
            a1_ref[j * HB:(j + 1) * HB, :] = (y + bias_row).astype(BF)

        def chunk(r, carry):
            base = pl.multiple_of(r * CHB, CHB)
            pw = ppad[pl.ds(base, CHB + 8), :]
            v = (cw_ref[pl.ds(32, 1), :] * _shift_up(pw, 6)[0:CHB, :]
                 + cw_ref[pl.ds(33, 1), :] * _shift_up(pw, 7)[0:CHB, :]
                 + cw_ref[pl.ds(34, 1), :] * pw[8:8 + CHB, :])
            q_ref[pl.ds(base, CHB), :] = (z_ref[2, pl.ds(base, CHB), :].astype(F32) * v).astype(BF)
            return carry

        lax.fori_loop(0, TB // CHB, chunk, 0)

    blk = pl.BlockSpec((TB, CW), lambda c, t: (t, c))
    tabs = [dft["fc"], dft["fs"], dft["ic_hi"], dft["is_hi"], *dft["kc"], *dft["ks"], *dft["k2"]]
    return _call(
        body, name="conv_fwd", grid=(D // CW, nt), comm=comm, args=[z, z, cw, bias] + tabs,
        in_specs=[pl.BlockSpec((5, TB, CW), lambda c, t: (0, t, c)),
                  pl.BlockSpec((5, HB, CW), lambda c, t: (0, jnp.maximum(t * hb - 1, 0), c)),
                  pl.BlockSpec((40, CW), lambda c, t: (0, c)), pl.BlockSpec((1, CW), lambda c, t: (0, c))]
                 + [_whole(a) for a in tabs],
        out_shape=[jax.ShapeDtypeStruct((T, D), BF), jax.ShapeDtypeStruct((T, D), BF)], out_specs=[blk, blk],
        scratch_shapes=[pltpu.VMEM((TB + HB, CW), BF), pltpu.VMEM((TB + 8, CW), F32)]
                       + [pltpu.VMEM((NB // 2, CW), F32)] * 3)


def _conv_bwd_dft(z, da1, dq, dzg, cw, dft, comm=None):
    nt = T // TB
    hb = TB // HB
    last_h = T // HB - 1

    def body(z_ref, zp_ref, zn_ref, da1_ref, da1n_ref, dq_ref, dqn_ref, dzg_ref, cw_ref,
             fc_ref, fs_ref, ic_ref, is_ref, kch, kcl, ksh, ksl, k2h, k2l, rch, rcl, rsh, rsl,
             dz_ref, dwa_ref, dwb_ref, aext, dyext, ppad, dvpad, hc, hs, h2, rc, rs, nyq, acc_b):
        t = pl.program_id(1)
        first, last = t == 0, t == nt - 1
        f = lambda ref, j: ref[j].astype(F32)

        @pl.when(first)
        def _():
            _filter_spectrum(cw_ref, ((kch, kcl), (ksh, ksl), (k2h, k2l)), hc, hs, h2)
            rc[...] = jnp.zeros_like(rc)
            rs[...] = jnp.zeros_like(rs)
            nyq[...] = jnp.zeros_like(nyq)
            acc_b[...] = jnp.zeros_like(acc_b)

        aext[0:HB, :] = jnp.where(first, 0.0, f(zp_ref, 0) * _sig(f(zp_ref, 1))).astype(BF)
        aext[HB:, :] = (f(z_ref, 0) * _sig(f(z_ref, 1))).astype(BF)
        dyext[0:TB, :] = da1_ref[...]
        dyext[TB:, :] = jnp.where(last, 0.0, da1n_ref[...].astype(F32)).astype(BF)
        ppad[0:8, :] = jnp.where(first, 0.0, f(zp_ref, 3)[HB - 8:HB, :] * f(zp_ref, 4)[HB - 8:HB, :])
        ppad[8:, :] = f(z_ref, 3) * f(z_ref, 4)
        dvpad[0:TB, :] = dq_ref[...].astype(F32) * f(z_ref, 2)
        dvpad[TB:, :] = jnp.where(last, 0.0, dqn_ref[...].astype(F32)[0:8, :] * f(zn_ref, 2)[0:8, :])

        for j in range(TB // HB):
            rows = slice(j * HB, (j + 1) * HB)
            dys = dyext[j * HB:j * HB + NB, :]
            da, db = _nn(fc_ref[...], dys), _nn(fs_ref[...], dys)
            gc = (hc[...] * da + hs[...] * db).astype(BF)
            gs = (h2[...] * db - hs[...] * da).astype(BF)
            da0 = _nn(ic_ref[...], gc) + _nn(is_ref[...], gs)
            z0, z1 = z_ref[0, rows, :].astype(F32), z_ref[1, rows, :].astype(F32)
            s1 = _sig(z1)
            dz_ref[0, rows, :] = (da0 * s1).astype(BF)
            dz_ref[1, rows, :] = (da0 * z0 * (s1 * (1.0 - s1))).astype(BF)
            xs = aext[j * HB:j * HB + NB, :]
            xa, xb = _nn(fc_ref[...], xs), _nn(fs_ref[...], xs)
            dyb = dyext[rows, :]
            pa, pb = _nn(fc_ref[:, HB:NB], dyb), _nn(fs_ref[:, HB:NB], dyb)
            rc[...] += pa * xa + pb * xb
            rs[...] += pb * xa - pa * xb
            nyq[...] += pb[0:8, :] * xb[0:8, :]

        def chunk(r, carry):
            base = pl.multiple_of(r * CHB, CHB)
            rows = pl.ds(base, CHB)
            pw = ppad[pl.ds(base, CHB + 8), :]
            p6 = _shift_up(pw, 6)[0:CHB, :]
            p7 = _shift_up(pw, 7)[0:CHB, :]
            p8 = pw[8:8 + CHB, :]
            wb0, wb1, wb2 = cw_ref[pl.ds(32, 1), :], cw_ref[pl.ds(33, 1), :], cw_ref[pl.ds(34, 1), :]
            v = wb0 * p6 + wb1 * p7 + wb2 * p8
            dz_ref[2, rows, :] = (dq_ref[rows, :].astype(F32) * v).astype(BF)
            dvw = dvpad[pl.ds(base, CHB + 8), :]
            dvc = dvw[0:CHB, :]
            dp = wb2 * dvc + wb1 * _shift_up(dvw, 1)[0:CHB, :] + wb0 * _shift_up(dvw, 2)[0:CHB, :]
            dz_ref[3, rows, :] = (dp * z_ref[4, rows, :].astype(F32)).astype(BF)
            dz_ref[4, rows, :] = (dp * z_ref[3, rows, :].astype(F32)).astype(BF)
            acc_b[0:8, :] += _fold8(dvc * p6)
            acc_b[8:16, :] += _fold8(dvc * p7)
            acc_b[16:24, :] += _fold8(dvc * p8)
            dz_ref[5, rows, :] = dzg_ref[0, rows, :]
            dz_ref[6, rows, :] = dzg_ref[1, rows, :]
            return carry

        lax.fori_loop(0, TB // CHB, chunk, 0)

        @pl.when(last)
        def _():
            row0 = lax.broadcasted_iota(jnp.int32, (NB // 2, CW), 0) == 0
            ny = jnp.broadcast_to(nyq[0:1, :], (NB // 2, CW))
            rcv = jnp.where(row0, rc[...] - ny, rc[...])
            rsv = jnp.where(row0, ny, rs[...])
            dwa_ref[...] = _dot3(rch[...], rcl[...], rcv) + _dot3(rsh[...], rsl[...], rsv)
            for k in range(KB):
                dwb_ref[k:k + 1, :] = jnp.sum(acc_b[8 * k:8 * k + 8, :], axis=0, keepdims=True)
            dwb_ref[KB:8, :] = jnp.zeros((8 - KB, CW), F32)

    blk = lambda c, t: (t, c)
    nxt = lambda c, t: (jnp.minimum((t + 1) * hb, last_h), c)
    tabs = [dft["fc"], dft["fs"], dft["ic_lo"], dft["is_lo"], *dft["kc"], *dft["ks"], *dft["k2"], *dft["rtc"], *dft["rts"]]
    return _call(
        body, name="conv_bwd", grid=(D // CW, nt), comm=comm, args=[z, z, z, da1, da1, dq, dq, dzg, cw] + tabs,
        in_specs=[pl.BlockSpec((5, TB, CW), lambda c, t: (0, t, c)),
                  pl.BlockSpec((5, HB, CW), lambda c, t: (0, jnp.maximum(t * hb - 1, 0), c)),
                  pl.BlockSpec((5, HB, CW), lambda c, t: (0, jnp.minimum((t + 1) * hb, last_h), c)),
                  pl.BlockSpec((TB, CW), blk), pl.BlockSpec((HB, CW), nxt),
                  pl.BlockSpec((TB, CW), blk), pl.BlockSpec((HB, CW), nxt),
                  pl.BlockSpec((2, TB, CW), lambda c, t: (0, t, c)),
                  pl.BlockSpec((40, CW), lambda c, t: (0, c))]
                 + [_whole(a) for a in tabs],
        out_shape=[jax.ShapeDtypeStruct((NG, T, D), BF), jax.ShapeDtypeStruct((32, D), F32),
                   jax.ShapeDtypeStruct((8, D), F32)],
        out_specs=[pl.BlockSpec((NG, TB, CW), lambda c, t: (0, t, c)),
                   pl.BlockSpec((32, CW), lambda c, t: (0, c)), pl.BlockSpec((8, CW), lambda c, t: (0, c))],
        scratch_shapes=[pltpu.VMEM((TB + HB, CW), BF), pltpu.VMEM((TB + HB, CW), BF),
                        pltpu.VMEM((TB + 8, CW), F32), pltpu.VMEM((TB + 8, CW), F32)]
                       + [pltpu.VMEM((NB // 2, CW), F32)] * 5 + [pltpu.VMEM((8, CW), F32), pltpu.VMEM((24, CW), F32)])


def _layernorm_silu(a1, lng, lnb):
    mu = jnp.mean(a1, axis=-1, keepdims=True)
    xc = a1 - mu
    rs = lax.rsqrt(jnp.mean(xc * xc, axis=-1, keepdims=True) + EPS)
    xh = xc * rs
    a2 = xh * lng + lnb
    sg = _sig(a2)
    return xh, rs, a2, sg


def _square_specs(blocks):
    return [pl.BlockSpec((D, D), lambda i, b=b: (b, 0)) for b in blocks]


def _mix_out(a1, q, z, h1, lng, lnb, wsq, comm=None):
    def body(a1_ref, q_ref, ga_ref, gb_ref, h_ref, lng_ref, lnb_ref, wa_ref, wb_ref, wo_ref, h2_ref, ya_ref, yb_ref):
        _, _, a2, sg = _layernorm_silu(a1_ref[...].astype(F32), lng_ref[...], lnb_ref[...])
        ya = _nn((a2 * sg).astype(BF), wa_ref[...])
        yb = _nn(q_ref[...], wb_ref[...])
        ya_ref[...] = ya.astype(BF)
        yb_ref[...] = yb.astype(BF)
        m = _sig(ga_ref[...].astype(F32)) * ya + _sig(gb_ref[...].astype(F32)) * yb
        h2_ref[...] = h_ref[...] + _nn(m.astype(BF), wo_ref[...])

    row = lambda i: (i, 0)
    vec = pl.BlockSpec((1, D), lambda i: (0, 0))
    return _call(
        body, name="mix_out", grid=(T // TM,), args=[a1, q, z, z, h1, lng, lnb, wsq, wsq, wsq], comm=comm,
        in_specs=[pl.BlockSpec((TM, D), row), pl.BlockSpec((TM, D), row),
                  pl.BlockSpec((None, TM, D), lambda i: (5, i, 0)), pl.BlockSpec((None, TM, D), lambda i: (6, i, 0)),
                  pl.BlockSpec((TM, D), row), vec, vec] + _square_specs((0, 1, 2)),
        out_shape=[jax.ShapeDtypeStruct((T, D), F32), jax.ShapeDtypeStruct((T, D), BF), jax.ShapeDtypeStruct((T, D), BF)],
        out_specs=[pl.BlockSpec((TM, D), row)] * 3)


def _rmsnorm_bwd(xf, g, dn):
    r = lax.rsqrt(jnp.mean(xf * xf, axis=-1, keepdims=True) + EPS)
    xr = xf * r
    gdn = dn * g
    dx = r * gdn - xr * (r * jnp.mean(gdn * xr, axis=-1, keepdims=True))
    return dx, jnp.sum(dn * xr, axis=0, keepdims=True)


def _ffn_bwd_hidden(dh, gg, uu, wbuf, off, name, comm=None):
    nf = F // FC

    def body(dh_ref, gg_ref, uu_ref, b0, dgu_ref, wd_s, sem):
        _load_ffn_weights((b0,), (off,), (wd_s,), sem)
        dhb = dh_ref[...]
        for c in range(nf):
            sl = slice(c * FC, (c + 1) * FC)
            da = _nt(dhb, wd_s[sl, :]).astype(BF)
            gb, ub = gg_ref[:, sl], uu_ref[:, sl]
            sg = _sig(gb)
            dgu_ref[0, :, sl] = (da * ub) * (sg * (1.0 + gb * (1.0 - sg)))
            dgu_ref[0, :, F + c * FC:F + (c + 1) * FC] = da * (gb * sg)

    row = lambda i: (i, 0)
    return _call(
        body, name=name, grid=(T // TM,), args=[dh, gg, uu, wbuf], comm=comm,
        in_specs=[pl.BlockSpec((TM, D), row), pl.BlockSpec((TM, F), row), pl.BlockSpec((TM, F), row), ANY],
        out_shape=[jax.ShapeDtypeStruct((1, T, 2 * F), BF)],
        out_specs=[pl.BlockSpec((1, TM, 2 * F), lambda i: (0, i, 0))],
        scratch_shapes=[pltpu.VMEM((F, D), BF), pltpu.SemaphoreType.DMA((1,))])


def _ffn_bwd_input(dgu, dh, x, g, wbufs, offs, name, comm=None, after=None):
    def body(dgu_ref, dh_ref, x_ref, g_ref, b0, b1, dx_ref, s_ref, w_s, sem):
        _load_ffn_weights((b0, b1), offs, (w_s.at[pl.ds(0, F), :], w_s.at[pl.ds(F, F), :]), sem)

        @pl.when(pl.program_id(0) == 0)
        def _():
            s_ref[...] = jnp.zeros_like(s_ref)

        dn = _nn(dgu_ref[0], w_s[...])
        dxn, dg = _rmsnorm_bwd(x_ref[...], g_ref[...], dn)
        dx_ref[...] = dh_ref[...] + dxn
        s_ref[0:1, :] += dg

    row = lambda i: (i, 0)
    return _call(
        body, name=name, grid=(T // TM,), args=[dgu, dh, x, g, *wbufs], comm=comm, after=after,
        in_specs=[pl.BlockSpec((1, TM, 2 * F), lambda i: (0, i, 0)), pl.BlockSpec((TM, D), row),
                  pl.BlockSpec((TM, D), row), pl.BlockSpec((1, D), lambda i: (0, 0)), ANY, ANY],
        out_shape=[jax.ShapeDtypeStruct((T, D), F32), jax.ShapeDtypeStruct((8, D), F32)],
        out_specs=[pl.BlockSpec((TM, D), row), pl.BlockSpec((8, D), lambda i: (0, 0))],
        scratch_shapes=[pltpu.VMEM((2 * F, D), BF), pltpu.SemaphoreType.DMA((2,))])


def _tn_matmul(lhs, rhs, tr, name, comm=None):
    ng, _, cdim = lhs.shape
    nc, nk = cdim // tr, T // TK
    if rhs.ndim == 2:
        r_spec = pl.BlockSpec((TK, D), lambda g, c, k: (k, 0))
    else:
        r_spec = pl.BlockSpec((None, TK, D), lambda g, c, k: (g, k, 0))

    def body(l_ref, r_ref, o_ref, acc):
        k = pl.program_id(2)

        @pl.when(k == 0)
        def _():
            acc[...] = jnp.zeros_like(acc)

        acc[...] += _tn(l_ref[...], r_ref[...])

        @pl.when(k == nk - 1)
        def _():
            o_ref[...] = acc[...].astype(BF)

    return _call(
        body, name=name, grid=(ng, nc, nk), args=[lhs, rhs], comm=comm,
        in_specs=[pl.BlockSpec((None, TK, tr), lambda g, c, k: (g, k, c)), r_spec],
        out_shape=[jax.ShapeDtypeStruct((ng * cdim, D), BF)],
        out_specs=[pl.BlockSpec((tr, D), lambda g, c, k: (g * nc + c, 0))],
        scratch_shapes=[pltpu.VMEM((tr, D), F32)])


def _mix_out_bwd(dh2, ya, yb, z, a1, q, lng, lnb, wsq, comm=None):
    def body(dh_ref, ya_ref, yb_ref, ga_ref, gb_ref, a1_ref, q_ref, lng_ref, lnb_ref, wa_ref, wb_ref, wo_ref,
             dzg_ref, da1_ref, dq_ref, l_ref, r_ref, s_ref):
        @pl.when(pl.program_id(0) == 0)
        def _():
            s_ref[...] = jnp.zeros_like(s_ref)

        dhb = dh_ref[...].astype(BF)
        dm = _nt(dhb, wo_ref[...]).astype(BF)
        ya, yb = ya_ref[...], yb_ref[...]
        sa, sb = _sig(ga_ref[...]), _sig(gb_ref[...])
        l_ref[0] = sa * ya + sb * yb
        l_ref[2] = q_ref[...]
        dzg_ref[0] = (dm * ya) * (sa * (1.0 - sa))
        dzg_ref[1] = (dm * yb) * (sb * (1.0 - sb))
        dya = dm * sa
        dyb = dm * sb
        r_ref[0] = dhb
        r_ref[1] = dya
        r_ref[2] = dyb
        dq_ref[...] = _nt(dyb, wb_ref[...]).astype(BF)
        da3 = _nt(dya, wa_ref[...])
        lng = lng_ref[...]
        xh, rs, a2, sg = _layernorm_silu(a1_ref[...].astype(F32), lng, lnb_ref[...])
        l_ref[1] = (a2 * sg).astype(BF)
        da2 = da3 * (sg * (1.0 + a2 * (1.0 - sg)))
        s_ref[0:1, :] += jnp.sum(da2 * xh, axis=0, keepdims=True)
        s_ref[1:2, :] += jnp.sum(da2, axis=0, keepdims=True)
        dxh = da2 * lng
        da1 = rs * (dxh - jnp.mean(dxh, axis=-1, keepdims=True) - xh * jnp.mean(dxh * xh, axis=-1, keepdims=True))
        da1_ref[...] = da1.astype(BF)
        s_ref[2:3, :] += jnp.sum(da1, axis=0, keepdims=True)

    row = lambda i: (i, 0)
    row3 = lambda i: (0, i, 0)
    vec = pl.BlockSpec((1, D), lambda i: (0, 0))
    return _call(
        body, name="mix_out_bwd", grid=(T // TM,), args=[dh2, ya, yb, z, z, a1, q, lng, lnb, wsq, wsq, wsq], comm=comm,
        in_specs=[pl.BlockSpec((TM, D), row), pl.BlockSpec((TM, D), row), pl.BlockSpec((TM, D), row),
                  pl.BlockSpec((None, TM, D), lambda i: (5, i, 0)), pl.BlockSpec((None, TM, D), lambda i: (6, i, 0)),
                  pl.BlockSpec((TM, D), row), pl.BlockSpec((TM, D), row), vec, vec] + _square_specs((0, 1, 2)),
        out_shape=[jax.ShapeDtypeStruct((2, T, D), BF), jax.ShapeDtypeStruct((T, D), BF),
                   jax.ShapeDtypeStruct((T, D), BF), jax.ShapeDtypeStruct((3, T, D), BF),
                   jax.ShapeDtypeStruct((3, T, D), BF), jax.ShapeDtypeStruct((8, D), F32)],
        out_specs=[pl.BlockSpec((2, TM, D), row3), pl.BlockSpec((TM, D), row), pl.BlockSpec((TM, D), row),
                   pl.BlockSpec((3, TM, D), row3), pl.BlockSpec((3, TM, D), row3), pl.BlockSpec((8, D), lambda i: (0, 0))])


def _mix_in_bwd(dz, dh2, h1, gm, win, comm=None):
    def body(dz_ref, dh_ref, h_ref, g_ref, *rest):
        w_any, (o_ref, ob_ref, s_ref, w_s, sem) = rest[:len(win)], rest[len(win):]
        _load_in_proj([(b, first) for b, (_, first) in zip(w_any, win)], w_s, sem)

        @pl.when(pl.program_id(0) == 0)
        def _():
            s_ref[...] = jnp.zeros_like(s_ref)

        du = _nn(dz_ref[0], w_s[0:D, :])
        for j in range(1, NG):
            du = du + _nn(dz_ref[j], w_s[j * D:(j + 1) * D, :])
        dx, dg = _rmsnorm_bwd(h_ref[...], g_ref[...], du)
        dh1 = dh_ref[...] + dx
        o_ref[...] = dh1
        ob_ref[...] = (0.5 * dh1).astype(BF)
        s_ref[0:1, :] += dg

    row = lambda i: (i, 0)
    return _call(
        body, name="mix_in_bwd", grid=(T // TM,), args=[dz, dh2, h1, gm] + [b for b, _ in win], comm=comm,
        in_specs=[pl.BlockSpec((NG, TM, D), lambda i: (0, i, 0)), pl.BlockSpec((TM, D), row),
                  pl.BlockSpec((TM, D), row), pl.BlockSpec((1, D), lambda i: (0, 0))] + [ANY] * len(win),
        out_shape=[jax.ShapeDtypeStruct((T, D), F32), jax.ShapeDtypeStruct((T, D), BF), jax.ShapeDtypeStruct((8, D), F32)],
        out_specs=[pl.BlockSpec((TM, D), row), pl.BlockSpec((TM, D), row), pl.BlockSpec((8, D), lambda i: (0, 0))],
        scratch_shapes=[pltpu.VMEM((NG * D, D), BF), pltpu.SemaphoreType.DMA((NDEV * len(win),))])


def _row_tile(n, want, mult):
    for t in range(min(want, n), 0, -1):
        if n % t == 0 and t % mult == 0:
            return t
    return n


def _pack_small(s_ffn1, s_in, s_mix, s_ffn2, s_final, dwa, dwb):
    def body(f1, mi, mo, f2, fl, wa_ref, wb_ref, v_ref, k_ref):
        for dst, (ref, row) in enumerate(((f1, 0), (mi, 0), (mo, 0), (mo, 1), (mo, 2), (f2, 0), (fl, 0), (fl, 1))):
            v_ref[dst:dst + 1, :] = ref[row:row + 1, :]
        for k in range(NDEV):
            k_ref[k, 0:32, :] = wa_ref[:, k * LANE:(k + 1) * LANE]
            k_ref[k, 32:40, :] = wb_ref[:, k * LANE:(k + 1) * LANE]

    return pl.pallas_call(
        body, name="pack_small",
        out_shape=(jax.ShapeDtypeStruct((8, D), F32), jax.ShapeDtypeStruct((NDEV, 40, LANE), F32)),
    )(s_ffn1, s_in, s_mix, s_ffn2, s_final, dwa, dwb)


def _adam_update(g, w, m, v):
    m2 = ADAM_B1 * m + (1.0 - ADAM_B1) * g
    v2 = ADAM_B2 * v + (1.0 - ADAM_B2) * (g * g)
    c1 = 1.0 - ADAM_B1 ** ADAM_STEP
    c2 = 1.0 - ADAM_B2 ** ADAM_STEP
    return -ADAM_LR * ((m2 / c1) / (jnp.sqrt(v2 / c2) + ADAM_EPS) + ADAM_WD * w), m2, v2


def _adam_small(vecs, convs, vec_params, tap_params):
    nv, nt = len(vec_params), len(tap_params)

    def body(*refs):
        v_ref, k_ref = refs[:2]
        p_refs = refs[2:2 + 3 * (nv + nt)]
        l_ref = refs[2 + 3 * (nv + nt)]
        o_refs = refs[3 + 3 * (nv + nt):]
        s, c = v_ref[0], k_ref[0]
        for k in range(1, NDEV):
            s = s + v_ref[k]
            c = c + k_ref[k]
        l_ref[...] = jnp.sum(s[7:8, :], axis=-1, keepdims=True)
        for i in range(nv):
            w_ref, m_ref, u_ref = p_refs[3 * i: 3 * i + 3]
            g_ref, d_ref, m2_ref, u2_ref = o_refs[4 * i: 4 * i + 4]
            g = s[i:i + 1, :]
            g_ref[...] = g
            d_ref[...], m2_ref[...], u2_ref[...] = _adam_update(g, w_ref[...], m_ref[...], u_ref[...])
        for i in range(nt):
            w_ref, m_ref, u_ref = p_refs[3 * (nv + i): 3 * (nv + i) + 3]
            g_ref, d_ref, m2_ref, u2_ref = o_refs[4 * (nv + i): 4 * (nv + i) + 4]
            first = tap_params[i][0]
            for k in range(w_ref.shape[0]):
                g = c[first + k:first + k + 1, :]
                g_ref[k] = g
                d_ref[k], m2_ref[k], u2_ref[k] = _adam_update(g, w_ref[k], m_ref[k], u_ref[k])

    params = [a for p in vec_params for a in p] + [a for p in tap_params for a in p[1:]]
    out_shape = [jax.ShapeDtypeStruct((1, 1), F32)]
    for p in list(vec_params) + [p[1:] for p in tap_params]:
        out_shape += [jax.ShapeDtypeStruct(p[0].shape, F32)] * 4
    outs = pl.pallas_call(body, name="adam_small", out_shape=tuple(out_shape))(vecs, convs, *params)
    groups = [tuple(outs[1 + 4 * i: 5 + 4 * i]) for i in range(nv + nt)]
    return outs[0], groups[:nv], groups[nv:]


def _adam_in_proj(parts, w, m, v, after):
    rows = w.shape[1]
    tr = _row_tile(D, 128, LANE)

    def body(*refs):
        p_refs = refs[:len(parts)]
        w_ref, m_ref, v_ref, g_ref, d_ref, m2_ref, v2_ref = refs[len(parts):]
        sums = []
        for p in p_refs:
            s = p[0].astype(F32)
            for k in range(1, p.shape[0]):
                s = s + p[k].astype(F32)
            sums.append(s)
        g = jnp.concatenate(sums, axis=0).T
        g_ref[...] = g
        d_ref[...], m2_ref[...], v2_ref[...] = _adam_update(g, w_ref[...], m_ref[...], v_ref[...])

    spec = pl.BlockSpec((tr, rows), lambda i: (i, 0))
    return _call(body, name="adam_in", grid=(D // tr,), args=list(parts) + [w, m, v], after=after,
                 in_specs=[pl.BlockSpec((p.shape[0], p.shape[1], tr), lambda i: (0, 0, i)) for p in parts] + [spec] * 3,
                 out_shape=[jax.ShapeDtypeStruct((D, rows), F32)] * 4, out_specs=[spec] * 4)


def _adam(gs, ws, ms, vs, name, after):
    n = len(gs)
    rows, cols = ws[0].shape
    tr = _row_tile(rows, 64, 16)

    def body(*refs):
        for i in range(n):
            g_in, w, m, v = refs[4 * i], refs[4 * i + 1][...], refs[4 * i + 2][...], refs[4 * i + 3][...]
            g_ref, d_ref, m_ref, v_ref = refs[4 * n + 4 * i: 4 * n + 4 * i + 4]
            g = g_in[0].astype(F32)
            for k in range(1, g_in.shape[0]):
                g = g + g_in[k].astype(F32)
            g_ref[...] = g
            d_ref[...], m_ref[...], v_ref[...] = _adam_update(g, w, m, v)

    spec = pl.BlockSpec((tr, cols), lambda i: (i, 0))
    args, in_specs = [], []
    for i in range(n):
        slots, first = gs[i]
        args += [slots, ws[i], ms[i], vs[i]]
        in_specs += [pl.BlockSpec((slots.shape[0], tr, cols), lambda i, b=first // tr: (0, b + i, 0))] + [spec] * 3
    outs = _call(body, name=name, grid=(rows // tr,), args=args, in_specs=in_specs, after=after,
                 out_shape=[jax.ShapeDtypeStruct((rows, cols), F32)] * (4 * n), out_specs=[spec] * (4 * n))
    return [tuple(outs[4 * i: 4 * i + 4]) for i in range(n)]


def kernel(x, ffn1_norm, ffn1_w_gate, ffn1_w_up, ffn1_w_down, mix_norm, w_in, a_dw_w, a_dw_b, a_ln_g, a_ln_b, a_w_out, b_conv_w, b_w_out, w_o, ffn2_norm, ffn2_w_gate, ffn2_w_up, ffn2_w_down, final_norm, loss_target, m_ffn1_norm, m_ffn1_w_gate, m_ffn1_w_up, m_ffn1_w_down, m_mix_norm, m_w_in, m_a_dw_w, m_a_dw_b, m_a_ln_g, m_a_ln_b, m_a_w_out, m_b_conv_w, m_b_w_out, m_w_o, m_ffn2_norm, m_ffn2_w_gate, m_ffn2_w_up, m_ffn2_w_down, m_final_norm, v_ffn1_norm, v_ffn1_w_gate, v_ffn1_w_up, v_ffn1_w_down, v_mix_norm, v_w_in, v_a_dw_w, v_a_dw_b, v_a_ln_g, v_a_ln_b, v_a_w_out, v_b_conv_w, v_b_w_out, v_w_o, v_ffn2_norm, v_ffn2_w_gate, v_ffn2_w_up, v_ffn2_w_down, v_final_norm):
    names = ("ffn1_norm", "ffn1_w_gate", "ffn1_w_up", "ffn1_w_down", "mix_norm", "w_in", "a_dw_w", "a_dw_b",
             "a_ln_g", "a_ln_b", "a_w_out", "b_conv_w", "b_w_out", "w_o", "ffn2_norm", "ffn2_w_gate", "ffn2_w_up",
             "ffn2_w_down", "final_norm")
    w = dict(ffn1_norm=ffn1_norm, ffn1_w_gate=ffn1_w_gate, ffn1_w_up=ffn1_w_up, ffn1_w_down=ffn1_w_down,
             mix_norm=mix_norm, w_in=w_in, a_dw_w=a_dw_w, a_dw_b=a_dw_b, a_ln_g=a_ln_g, a_ln_b=a_ln_b,
             a_w_out=a_w_out, b_conv_w=b_conv_w, b_w_out=b_w_out, w_o=w_o, ffn2_norm=ffn2_norm,
             ffn2_w_gate=ffn2_w_gate, ffn2_w_up=ffn2_w_up, ffn2_w_down=ffn2_w_down, final_norm=final_norm)
    m = dict(ffn1_norm=m_ffn1_norm, ffn1_w_gate=m_ffn1_w_gate, ffn1_w_up=m_ffn1_w_up, ffn1_w_down=m_ffn1_w_down,
             mix_norm=m_mix_norm, w_in=m_w_in, a_dw_w=m_a_dw_w, a_dw_b=m_a_dw_b, a_ln_g=m_a_ln_g, a_ln_b=m_a_ln_b,
             a_w_out=m_a_w_out, b_conv_w=m_b_conv_w, b_w_out=m_b_w_out, w_o=m_w_o, ffn2_norm=m_ffn2_norm,
             ffn2_w_gate=m_ffn2_w_gate, ffn2_w_up=m_ffn2_w_up, ffn2_w_down=m_ffn2_w_down, final_norm=m_final_norm)
    v = dict(ffn1_norm=v_ffn1_norm, ffn1_w_gate=v_ffn1_w_gate, ffn1_w_up=v_ffn1_w_up, ffn1_w_down=v_ffn1_w_down,
             mix_norm=v_mix_norm, w_in=v_w_in, a_dw_w=v_a_dw_w, a_dw_b=v_a_dw_b, a_ln_g=v_a_ln_g, a_ln_b=v_a_ln_b,
             a_w_out=v_a_w_out, b_conv_w=v_b_conv_w, b_w_out=v_b_w_out, w_o=v_w_o, ffn2_norm=v_ffn2_norm,
             ffn2_w_gate=v_ffn2_w_gate, ffn2_w_up=v_ffn2_w_up, ffn2_w_down=v_ffn2_w_down, final_norm=v_final_norm)
    flat = _pack_weights(dict(wg1=ffn1_w_gate[0].T, wu1=ffn1_w_up[0].T, wd1=ffn1_w_down[0], wg2=ffn2_w_gate[0].T,
                              wu2=ffn2_w_up[0].T, wd2=ffn2_w_down[0], win=w_in[0], wa=a_w_out[0], wb=b_w_out[0],
                              wo=w_o[0]))
    cw_shard = jnp.concatenate([a_dw_w[0], jnp.zeros((1, LANE), F32), b_conv_w[0], jnp.zeros((5, LANE), F32)], axis=0)

    x2, tgt = x[0], loss_target[0]
    st_a, st_b, st_b2 = ("wg1", "wu1"), ("wd1", "win/0/2"), ("win/1/2",)
    st_c, st_d, st_e = ("wa", "wb", "wo", "wg2"), ("wu2",), ("wd2",)

    buf_a, cw = _run_comm(_join(_ag_comm(st_a, flat), _direct_comm(cw_shard, False)), "ag_ffn1")
    n1, gg1, uu1, act1, buf_b = _ffn_gate_up(x2, ffn1_norm, (buf_a, buf_a), (0, F), "ffn1_gate_up", _ag_comm(st_b, flat))
    h1, buf_b2 = _ffn_down(x2, act1, buf_b, 0, "ffn1_down", _ag_comm(st_b2, flat))
    win = ((buf_b, F), (buf_b2, 0))
    u, z, buf_c = _mix_in(h1, mix_norm, win, _ag_comm(st_c, flat))
    dft = _dft_constants()
    cw = jnp.transpose(cw, (1, 0, 2)).reshape(40, D)
    a1, q, buf_d = _conv_fwd_dft(z, cw, a_dw_b, dft, _ag_comm(st_d, flat))
    h2, ya, yb, buf_e = _mix_out(a1, q, z, h1, a_ln_g, a_ln_b, buf_c, _ag_comm(st_e, flat))
    ffn2_bufs, ffn2_offs = (buf_c, buf_d, buf_e), (3 * D, 0, 0)
    dh3, dhb3, s_final, n2, gg2, uu2, act2 = _ffn_fwd(h2, ffn2_norm, ffn2_bufs, ffn2_offs, "ffn2_fwd",
                                          final=(final_norm.reshape(1, D), tgt))

    tr_f = F // 2 if (F // 2) % LANE == 0 else F
    def pair(stage, src):
        return _rs_pair_comm(stage, src)

    def chip(stage, src, pair_buf, tag):
        return _rs_chip_comm(_pair_add(stage, src, pair_buf, "pair_add_" + tag))

    (dgu2,) = _ffn_bwd_hidden(dhb3, gg2, uu2, buf_e, 0, "ffn2_bwd_h")
    (gu2,) = _tn_matmul(dgu2, n2, tr_f, "dw_gu2")
    s2a, src2a = ("wg2", "wu2"), dict(wg2=(gu2, 0), wu2=(gu2, F))
    (gd2,) = _tn_matmul(act2, dhb3, tr_f, "dw_d2")
    s2b, src2b = ("wd2",), dict(wd2=(gd2, 0))
    dh2, s_ffn2, pair2a, pair2b = _ffn_bwd_input(dgu2, dh3, h2, ffn2_norm, (buf_c, buf_d), (3 * D, 0), "ffn2_bwd_x",
                                                 _join(pair(s2a, src2a), pair(s2b, src2b)))
    dzg, da1, dq, lsq, rsq, s_mix, recv2b = _mix_out_bwd(dh2, ya, yb, z, a1, q, a_ln_g, a_ln_b, buf_c,
                                                          chip(s2b, src2b, pair2b, "2b"))
    (gsq,) = _tn_matmul(lsq, rsq, D, "dw_square")
    ssq, srcsq = ("wa", "wb", "wo"), dict(wa=(gsq, D), wb=(gsq, 2 * D), wo=(gsq, 0))
    dz, dwa, dwb, recv2a, pairsq = _conv_bwd_dft(z, da1, dq, dzg, cw, dft,
                                                 _join(chip(s2a, src2a, pair2a, "2a"), pair(ssq, srcsq)))
    gin, recvsq = _tn_matmul(dz, u, D, "dw_in", chip(ssq, srcsq, pairsq, "sq"))
    sin_a, sin_b, srcin = ("win/0/2",), ("win/1/2",), {"win/0/2": (gin, 0), "win/1/2": (gin, 0)}
    dh1, dhb1, s_in, pairin_a, pairin_b = _mix_in_bwd(dz, dh2, h1, mix_norm, win,
                                                _join(pair(sin_a, srcin), pair(sin_b, srcin)))
    dgu1, recvin_a = _ffn_bwd_hidden(dhb1, gg1, uu1, buf_b, 0, "ffn1_bwd_h",
                                           chip(sin_a, srcin, pairin_a, "in_a"))
    gu1, recvin_b = _tn_matmul(dgu1, n1, tr_f, "dw_gu1", chip(sin_b, srcin, pairin_b, "in_b"))
    s1a, src1a = ("wg1", "wu1"), dict(wg1=(gu1, 0), wu1=(gu1, F))
    gd1, pair1a = _tn_matmul(act1, dhb1, tr_f, "dw_d1", pair(s1a, src1a))
    s1b, src1b = ("wd1",), dict(wd1=(gd1, 0))
    xchg1 = _join(chip(s1a, src1a, pair1a, "1a"), pair(s1b, src1b))
    xchg1_sems, xchg1_bufs, token = _comm_start(xchg1, "xchg_ffn1_start")
    dx, s_ffn1 = _ffn_bwd_input(dgu1, dh1, x2, ffn1_norm, (buf_a, buf_a), (0, F), "ffn1_bwd_x", after=token)
    (_, gd1), (recv1a, pair1b) = _comm_wait(xchg1, "xchg_ffn1_wait", xchg1_sems, xchg1_bufs, s_ffn1)
    src1b = dict(wd1=(gd1, 0))

    vec8, convk = _pack_small(s_ffn1, s_in, s_mix, s_ffn2, s_final, dwa, dwb)
    tail = _join(chip(s1b, src1b, pair1b, "1b"), _join(_direct_comm(vec8, False), _direct_comm(convk, True)))
    tail_sems, tail_bufs, token = _comm_start(tail, "xchg_tail_start")

    fs = F // NDEV
    g = dict(ffn1_w_gate=(recv1a, 0), ffn1_w_up=(recv1a, fs), ffn2_w_gate=(recv2a, 0), ffn2_w_up=(recv2a, fs),
             ffn2_w_down=(recv2b, 0), a_w_out=(recvsq, 0), b_w_out=(recvsq, D // NDEV), w_o=(recvsq, 2 * (D // NDEV)))
    grad, upd = {}, {}

    def run(group, name, after, as2d=lambda a: a[0], back=lambda a, n: a.reshape(w[n].shape)):
        res = _adam([g[n] for n in group], [as2d(w[n]) for n in group], [as2d(m[n]) for n in group],
                    [as2d(v[n]) for n in group], name, after)
        for n, r in zip(group, res):
            grad[n], upd[n] = back(r[0], n), tuple(back(a, n) for a in r[1:])
        return res[0][0]

    done = run(("ffn1_w_gate", "ffn1_w_up", "ffn2_w_gate", "ffn2_w_up"), "adam_gate_up", token,
               as2d=lambda a: a[0].T, back=lambda a, n: a.T[None])
    r_in = _adam_in_proj([recvin_a, recvin_b], w_in[0], m_w_in[0], v_w_in[0], done)
    grad["w_in"], upd["w_in"] = r_in[0][None], tuple(a[None] for a in r_in[1:])
    done = run(("a_w_out", "b_w_out", "w_o"), "adam_square", r_in[0])
    _, (recv1b, vec_all, conv_all) = _comm_wait(tail, "xchg_tail_wait", tail_sems, tail_bufs, done)
    g["ffn1_w_down"] = (recv1b, 0)
    run(("ffn1_w_down", "ffn2_w_down"), "adam_down", done)
    vec_names = ("ffn1_norm", "mix_norm", "a_ln_g", "a_ln_b", "a_dw_b", "ffn2_norm", "final_norm")
    tap_names, tap_rows = ("a_dw_w", "b_conv_w"), (0, 32)
    taps = lambda a: jnp.transpose(a, (1, 0, 2))
    loss, vec_res, tap_res = _adam_small(
        vec_all, conv_all, [tuple(t[n].reshape(1, D) for t in (w, m, v)) for n in vec_names],
        [(r,) + tuple(taps(t[n]) for t in (w, m, v)) for n, r in zip(tap_names, tap_rows)])
    for n, r in zip(vec_names, vec_res):
        grad[n], upd[n] = r[0].reshape(w[n].shape), tuple(a.reshape(w[n].shape) for a in r[1:])
    for n, r in zip(tap_names, tap_res):
        grad[n], upd[n] = taps(r[0]), tuple(taps(a) for a in r[1:])

    return (loss.reshape(()), dx.reshape(x.shape), *[grad[n] for n in names], *[upd[n][0] for n in names],
            *[upd[n][1] for n in names], *[upd[n][2] for n in names])
```

```python
import jax
import jax.numpy as jnp
from jax import lax
from jax.experimental import pallas as pl
from jax.experimental.pallas import tpu as pltpu

T = 4096
D = 1024
F = 2816
NG = 7
NDEV = 8
NCHIP = 4
KA, KB = 31, 3
EPS = 1e-6
ADAM_LR, ADAM_B1, ADAM_B2, ADAM_EPS, ADAM_WD, ADAM_STEP = 0.001, 0.9, 0.999, 1e-08, 0.01, 10

TM = 512
FC = 256
TB = 1024
NB = 256
HB = NB // 2
CW = 256
CHB = 64
LANE = 128
TK = 2048
VMEM_LIMIT = 56 * 1024 * 1024

BF = jnp.bfloat16
F32 = jnp.float32
MESH = pl.DeviceIdType.MESH
ANY = pl.BlockSpec(memory_space=pl.ANY)

ORDER = ("wg1", "wu1", "wd1", "wg2", "wu2", "wd2", "win", "wa", "wb", "wo")


class _Layout:
    def __init__(self):
        fs, dis, ds = F // NDEV, NG * D // NDEV, D // NDEV
        self.rows = dict(wg1=fs, wu1=fs, wd1=fs, wg2=fs, wu2=fs, wd2=fs, win=dis, wa=ds, wb=ds, wo=ds)
        self.fl, off = {}, 0
        for n in ORDER:
            self.fl[n] = off
            off += self.rows[n]
        self.RT = off


class _Stage:
    def __init__(self, names):
        lay = _Layout()
        self.names = names
        self.rows, self.full, self.sub, self.fl = {}, {}, {}, {}
        for n in names:
            base, i, k = (n.split("/") + ["0", "1"])[:3]
            self.full[n] = lay.rows[base]
            self.rows[n] = lay.rows[base] // int(k)
            self.sub[n] = int(i) * self.rows[n]
            self.fl[n] = lay.fl[base] + self.sub[n]
        self.off, self.wc, o, w = {}, {}, 0, 0
        for n in names:
            self.off[n], self.wc[n] = o, w
            o += self.rows[n]
            w += NDEV * self.rows[n]
        self.R, self.W = o, w

    def grad_row(self, n, first, dev_lin):
        return first + dev_lin * self.full[n] + self.sub[n]


def _nt(a, b):
    return lax.dot_general(a, b, (((1,), (1,)), ((), ())), preferred_element_type=F32)


def _nn(a, b):
    return lax.dot_general(a, b, (((1,), (0,)), ((), ())), preferred_element_type=F32)


def _tn(a, b):
    return lax.dot_general(a, b, (((0,), (0,)), ((), ())), preferred_element_type=F32)


def _sig(x):
    return 1.0 / (1.0 + jnp.exp(-x))


def _position():
    return lax.axis_index("x"), lax.axis_index("y"), lax.axis_index("c")


def _peer(pos, j):
    x, y, c = pos
    return (1 - x if j & 4 else x, 1 - y if j & 2 else y, 1 - c if j & 1 else c)


def _lin(pos):
    return 4 * pos[0] + 2 * pos[1] + pos[2]


def _chip(pos):
    return 2 * pos[0] + pos[1]


class _Comm:
    def __init__(self, inputs, out_shapes, scratch, start, finish, middle=None):
        self.inputs, self.out_shapes, self.scratch = inputs, out_shapes, scratch
        self.start, self.finish, self.middle = start, finish, middle


def _call(body, *, name, grid, args, in_specs, out_shape, out_specs, scratch_shapes=(), comm=None,
          num_scalar_prefetch=0, after=None):
    in_specs, out_shape, out_specs, scratch_shapes = list(in_specs), list(out_shape), list(out_specs), list(scratch_shapes)
    if after is not None:
        inner, pos = body, num_scalar_prefetch + len(in_specs)
        body = lambda *refs: inner(*refs[:pos], *refs[pos + 1:])
        args, in_specs = list(args) + [after], in_specs + [ANY]
    n_in, n_out, n_scr = len(in_specs), len(out_shape), len(scratch_shapes)
    sp = num_scalar_prefetch
    if comm is None:
        kernel_fn = lambda *refs: body(*refs)
        c_in = c_out = c_scr = 0
    else:
        c_in, c_out, c_scr = len(comm.inputs), len(comm.out_shapes), len(comm.scratch)

        def kernel_fn(*refs):
            pre, refs = refs[:sp], refs[sp:]
            ins, cins = refs[:n_in], refs[n_in:n_in + c_in]
            o0 = n_in + c_in
            outs, couts = refs[o0:o0 + n_out], refs[o0 + n_out:o0 + n_out + c_out]
            s0 = o0 + n_out + c_out
            scr, cscr = refs[s0:s0 + n_scr], refs[s0 + n_scr:]
            step, steps = pl.program_id(0), grid[0]
            for a in range(1, len(grid)):
                step, steps = step * grid[a] + pl.program_id(a), steps * grid[a]
            first, last = step == 0, step == steps - 1

            @pl.when(first)
            def _():
                comm.start(cins, couts, cscr)

            if comm.middle is not None:
                @pl.when(step == (steps // 2 if steps > 2 else steps - 1))
                def _():
                    comm.middle(cins, couts, cscr)

            body(*pre, *ins, *outs, *scr)

            @pl.when(last)
            def _():
                comm.finish(cins, couts, cscr)

        args = list(args) + list(comm.inputs)
        in_specs += [ANY] * c_in
        out_shape += list(comm.out_shapes)
        out_specs += [ANY] * c_out
        scratch_shapes += list(comm.scratch)
    params = pltpu.CompilerParams(dimension_semantics=("arbitrary",) * len(grid), vmem_limit_bytes=VMEM_LIMIT)
    if sp:
        grid_spec = pltpu.PrefetchScalarGridSpec(num_scalar_prefetch=sp, grid=grid, in_specs=in_specs,
                                                 out_specs=out_specs, scratch_shapes=scratch_shapes)
        return pl.pallas_call(kernel_fn, name=name, grid_spec=grid_spec, out_shape=out_shape,
                              compiler_params=params)(*args)
    return pl.pallas_call(kernel_fn, name=name, grid=grid, in_specs=in_specs, out_shape=out_shape, out_specs=out_specs,
                          scratch_shapes=scratch_shapes, compiler_params=params)(*args)


def _join(a, b):
    na = (len(a.inputs), len(a.out_shapes), len(a.scratch))

    def split(refs):
        return ([r[:n] for r, n in zip(refs, na)], [r[n:] for r, n in zip(refs, na)])

    def start(*refs):
        ra, rb = split(refs)
        a.start(*ra)
        b.start(*rb)

    def finish(*refs):
        ra, rb = split(refs)
        a.finish(*ra)
        b.finish(*rb)

    def middle(*refs):
        for stage, r in zip((a, b), split(refs)):
            if stage.middle is not None:
                stage.middle(*r)

    return _Comm(list(a.inputs) + list(b.inputs), list(a.out_shapes) + list(b.out_shapes),
                 list(a.scratch) + list(b.scratch), start, finish,
                 middle if (a.middle is not None or b.middle is not None) else None)


def _run_comm(comm, name):
    def body(*refs):
        c_in, c_out = len(comm.inputs), len(comm.out_shapes)
        parts = (refs[:c_in], refs[c_in:c_in + c_out], refs[c_in + c_out:])
        comm.start(*parts)
        if comm.middle is not None:
            comm.middle(*parts)
        comm.finish(*parts)

    return pl.pallas_call(
        body, name=name, out_shape=list(comm.out_shapes), in_specs=[ANY] * len(comm.inputs),
        out_specs=[ANY] * len(comm.out_shapes), scratch_shapes=list(comm.scratch))(*comm.inputs)


HBM = pl.BlockSpec(memory_space=pltpu.HBM)
SEM = pl.BlockSpec(memory_space=pltpu.SEMAPHORE)
DATAFLOW = pltpu.SideEffectType.DATAFLOW_SIDE_EFFECTING


def _comm_start(comm, name):
    c_in, c_out = len(comm.inputs), len(comm.out_shapes)
    sems = [s(()) if s is pltpu.SemaphoreType.DMA else s for s in comm.scratch]
    bufs = list(comm.inputs) + [lax.empty(s.shape, s.dtype) for s in comm.out_shapes]

    def body(*refs):
        sem_refs = refs[c_in + c_out:c_in + c_out + len(sems)]
        comm.start(refs[:c_in], refs[c_in:c_in + c_out], sem_refs)
        refs[-1][...] = jnp.zeros_like(refs[-1])

    outs = pl.pallas_call(
        body, name=name,
        out_shape=sems + [pltpu.HBM(b.shape, b.dtype) for b in bufs] + [jax.ShapeDtypeStruct((8, LANE), F32)],
        in_specs=[HBM] * len(bufs),
        out_specs=[SEM] * len(sems) + [HBM] * len(bufs) + [pl.BlockSpec(memory_space=pltpu.VMEM)],
        input_output_aliases={i: len(sems) + i for i in range(len(bufs))},
        compiler_params=pltpu.CompilerParams(has_side_effects=DATAFLOW),
    )(*[pltpu.with_memory_space_constraint(b, pltpu.HBM) for b in bufs])
    return outs[:len(sems)], outs[len(sems):-1], outs[-1]


def _comm_wait(comm, name, sems, bufs, after):
    c_in, c_out = len(comm.inputs), len(comm.out_shapes)

    def body(*refs):
        sem_refs = refs[c_in + c_out:c_in + c_out + len(sems)]
        comm.finish(refs[:c_in], refs[c_in:c_in + c_out], sem_refs)

    outs = pl.pallas_call(
        body, name=name, out_shape=[pltpu.HBM(b.shape, b.dtype) for b in bufs],
        in_specs=[HBM] * len(bufs) + [SEM] * len(sems) + [ANY], out_specs=[HBM] * len(bufs),
        input_output_aliases={i: i for i in range(len(bufs))},
        compiler_params=pltpu.CompilerParams(has_side_effects=DATAFLOW),
    )(*bufs, *sems, after)
    return outs[:c_in], outs[c_in:]


def _ag_comm(names, flat):
    st = _Stage(names)

    def ring(me):
        x, y, c = me
        diagonal = x == y
        up = (jnp.where(diagonal, x, 1 - x), jnp.where(diagonal, 1 - y, y), c)
        down = (jnp.where(diagonal, 1 - x, x), jnp.where(diagonal, y, 1 - y), c)
        low = c == 0
        passed = tuple(jnp.where(low, d, u) for d, u in zip(down, up))
        target = tuple(jnp.where(low, u, d) for d, u in zip(down, up))
        return up, down, (1 - x, 1 - y, c), passed, target

    def parts(refs):
        (flat_ref,), (out_ref,), (send_sems, recv_sems, local_sem) = refs
        me = _position()

        def region(name, dev):
            r = st.rows[name]
            return out_ref.at[pl.ds(st.wc[name] + _lin(dev) * r, r), :]

        def own(name):
            return flat_ref.at[pl.ds(st.fl[name], st.rows[name]), :]

        def copies(k, dev, to, from_flat):
            return [pltpu.make_async_remote_copy(
                src_ref=own(n) if from_flat else region(n, dev), dst_ref=region(n, dev), send_sem=send_sems.at[k],
                recv_sem=recv_sems.at[k], device_id=to, device_id_type=MESH) for n in names]

        def whole(k):
            return pltpu.make_async_remote_copy(
                src_ref=flat_ref.at[pl.ds(0, st.R), :], dst_ref=out_ref.at[pl.ds(0, st.R), :],
                send_sem=send_sems.at[k], recv_sem=recv_sems.at[k], device_id=me, device_id_type=MESH)

        return me, region, own, copies, whole, flat_ref, out_ref, local_sem

    def start(*refs):
        me, region, own, copies, _, _, _, local_sem = parts(refs)
        for n in names:
            pltpu.make_async_copy(own(n), region(n, me), local_sem).start()
        up, down, _, _, _ = ring(me)
        for k, to in ((1, up), (2, down), (0, _peer(me, 1))):
            for cp in copies(k, me, to, True):
                cp.start()

    def middle(*refs):
        me, _, _, copies, whole, _, _, _ = parts(refs)
        up, down, _, passed, target = ring(me)
        sib = _peer(me, 1)
        whole(1).wait_recv()
        whole(2).wait_recv()
        for k, dev, to in ((3, passed, target), (4, down, sib), (5, up, sib)):
            for cp in copies(k, dev, to, False):
                cp.start()

    def finish(*refs):
        me, _, _, copies, whole, flat_ref, out_ref, local_sem = parts(refs)
        _, _, across, _, _ = ring(me)
        whole(3).wait_recv()
        for cp in copies(6, across, _peer(me, 1), False):
            cp.start()
        whole(0).wait_recv()
        for j in range(3):
            whole(4 + j).wait_recv()
        for k in range(7):
            whole(k).wait_send()
        pltpu.make_async_copy(flat_ref.at[pl.ds(0, st.R), :], out_ref.at[pl.ds(0, st.R), :], local_sem).wait()

    return _Comm([flat], [jax.ShapeDtypeStruct((st.W, D), BF)],
                 [pltpu.SemaphoreType.DMA((7,)), pltpu.SemaphoreType.DMA((7,)), pltpu.SemaphoreType.DMA],
                 start, finish, middle)


def _rs_pair_comm(names, src):
    st = _Stage(names)
    arrays = []
    for n in names:
        if not any(src[n][0] is a for a in arrays):
            arrays.append(src[n][0])
    idx = {n: [i for i, a in enumerate(arrays) if a is src[n][0]][0] for n in names}

    def slot_wait(refs):
        recv = refs[1][0]
        send_sem, recv_sem = refs[2]
        return pltpu.make_async_remote_copy(src_ref=recv, dst_ref=recv, send_sem=send_sem, recv_sem=recv_sem,
                                            device_id=_position(), device_id_type=MESH)

    def start(*refs):
        ins, (recv,), (send_sem, recv_sem) = refs
        me = _position()
        sib = _peer(me, 1)
        for q in range(NCHIP):
            dev = (q // 2, q % 2, sib[2])
            for n in names:
                r = st.rows[n]
                pltpu.make_async_remote_copy(
                    src_ref=ins[idx[n]].at[pl.ds(st.grad_row(n, src[n][1], _lin(dev)), r), :],
                    dst_ref=recv.at[q, pl.ds(st.off[n], r), :], send_sem=send_sem, recv_sem=recv_sem,
                    device_id=sib, device_id_type=MESH).start()

    def finish(*refs):
        w = slot_wait(refs)
        w.wait_recv()
        w.wait_send()

    return _Comm(arrays, [jax.ShapeDtypeStruct((NCHIP, st.R, D), BF)],
                 [pltpu.SemaphoreType.DMA, pltpu.SemaphoreType.DMA], start, finish)


def _pair_add(names, src, recv, name):
    st = _Stage(names)
    c_arr = jnp.reshape(lax.axis_index("c"), (1,)).astype(jnp.int32)

    def body(c_ref, *refs):
        r_ref, o_ref = refs[len(names)], refs[len(names) + 1]
        for a_ref, n in zip(refs, names):
            rows = slice(st.off[n], st.off[n] + st.rows[n])
            o_ref[rows, :] = (a_ref[...].astype(F32) + r_ref[rows, :].astype(F32)).astype(BF)

    def shard_spec(n):
        r = st.rows[n]
        base, step = st.grad_row(n, src[n][1], 0) // r, st.full[n] // r
        return pl.BlockSpec((r, D), lambda q, c_ref: (base + step * (2 * q + c_ref[0]), 0))

    slot = pl.BlockSpec((None, st.R, D), lambda q, c_ref: (q, 0, 0))
    return _call(body, name=name, grid=(NCHIP,), args=[c_arr] + [src[n][0] for n in names] + [recv],
                 in_specs=[shard_spec(n) for n in names] + [slot],
                 out_shape=[jax.ShapeDtypeStruct((NCHIP, st.R, D), BF)], out_specs=[slot], num_scalar_prefetch=1)[0]


def _rs_chip_comm(part):
    def copies(refs):
        (p_ref,), (recv,), (send_sems, recv_sems, local_sem) = refs
        me = _position()
        mine = pltpu.make_async_copy(p_ref.at[_chip(me)], recv.at[_chip(me)], local_sem)
        out = []
        for j, bits in enumerate((4, 2, 6)):
            to = _peer(me, bits)
            out.append(pltpu.make_async_remote_copy(
                src_ref=p_ref.at[_chip(to)], dst_ref=recv.at[_chip(me)], send_sem=send_sems.at[j],
                recv_sem=recv_sems.at[j], device_id=to, device_id_type=MESH))
        return mine, out

    def start(*refs):
        mine, out = copies(refs)
        mine.start()
        for cp in out:
            cp.start()

    def finish(*refs):
        mine, out = copies(refs)
        for cp in out:
            cp.wait_recv()
        for cp in out:
            cp.wait_send()
        mine.wait()

    return _Comm([part], [jax.ShapeDtypeStruct(part.shape, BF)],
                 [pltpu.SemaphoreType.DMA((3,)), pltpu.SemaphoreType.DMA((3,)), pltpu.SemaphoreType.DMA],
                 start, finish)


def _direct_comm(x, scatter):
    def copies(refs):
        (x_ref,), (out_ref,), (send_sems, recv_sems, local_sem) = refs
        me = _position()

        def piece(dev):
            return x_ref.at[_lin(dev)] if scatter else x_ref

        mine = pltpu.make_async_copy(piece(me), out_ref.at[_lin(me)], local_sem)
        return mine, [pltpu.make_async_remote_copy(
            src_ref=piece(_peer(me, j)), dst_ref=out_ref.at[_lin(me)], send_sem=send_sems.at[j - 1],
            recv_sem=recv_sems.at[j - 1], device_id=_peer(me, j), device_id_type=MESH) for j in range(1, NDEV)]

    def start(*refs):
        mine, cps = copies(refs)
        mine.start()
        for cp in cps:
            cp.start()

    def finish(*refs):
        mine, cps = copies(refs)
        for cp in cps:
            cp.wait_recv()
        for cp in cps:
            cp.wait_send()
        mine.wait()

    shape = x.shape if scatter else (NDEV,) + x.shape
    return _Comm([x], [jax.ShapeDtypeStruct(shape, x.dtype)],
                 [pltpu.SemaphoreType.DMA((7,)), pltpu.SemaphoreType.DMA((7,)), pltpu.SemaphoreType.DMA],
                 start, finish)


def _pack_weights(shards):
    lay = _Layout()

    def body(*refs):
        o_ref = refs[-1]
        for ref, n in zip(refs, ORDER):
            x = ref[...].T if n == "win" else ref[...]
            o_ref[lay.fl[n]:lay.fl[n] + lay.rows[n], :] = x.astype(BF)

    return pl.pallas_call(
        body, name="pack_weights", out_shape=jax.ShapeDtypeStruct((lay.RT, D), BF),
        compiler_params=pltpu.CompilerParams(vmem_limit_bytes=VMEM_LIMIT))(*[shards[n] for n in ORDER])


def _load_ffn_weights(srcs, offs, scratch, sem):
    @pl.when(pl.program_id(0) == 0)
    def _():
        cps = [pltpu.make_async_copy(s.at[pl.ds(off, dst.shape[0]), :], dst, sem.at[i])
               for i, (s, off, dst) in enumerate(zip(srcs, offs, scratch))]
        for cp in cps:
            cp.start()
        for cp in cps:
            cp.wait()


def _final_loss_tile(xf, g, tgt, s_ref):
    r = lax.rsqrt(jnp.mean(xf * xf, axis=-1, keepdims=True) + EPS)
    xr = xf * r
    e = xr * g - tgt
    s_ref[1:2, :] += jnp.sum(e * e, axis=0, keepdims=True) * (0.5 / D)
    dy = e * (1.0 / D)
    s_ref[0:1, :] += jnp.sum(dy * xr, axis=0, keepdims=True)
    gdy = dy * g
    return r * gdy - xr * (r * jnp.mean(gdy * xr, axis=-1, keepdims=True))


def _ffn_fwd(x, g, wbufs, offs, name, comm=None, final=None):
    nf = F // FC

    def body(x_ref, g_ref, b0, b1, b2, *rest):
        if final is None:
            h_ref, n_ref, gg_ref, uu_ref, a_ref, wg_s, wu_s, wd_s, sem = rest
        else:
            gf_ref, t_ref, dh_ref, dhb_ref, s_ref, n_ref, gg_ref, uu_ref, a_ref, wg_s, wu_s, wd_s, sem = rest

            @pl.when(pl.program_id(0) == 0)
            def _():
                s_ref[...] = jnp.zeros_like(s_ref)

        _load_ffn_weights((b0, b1, b2), offs, (wg_s, wu_s, wd_s), sem)
        xf = x_ref[...]
        r = lax.rsqrt(jnp.mean(xf * xf, axis=-1, keepdims=True) + EPS)
        nb = (xf * r * g_ref[...]).astype(BF)
        n_ref[...] = nb
        acc = jnp.zeros((TM, D), F32)
        for c in range(nf):
            sl = slice(c * FC, (c + 1) * FC)
            gb = _nt(nb, wg_s[sl, :]).astype(BF)
            ub = _nt(nb, wu_s[sl, :]).astype(BF)
            gg_ref[:, sl] = gb
            uu_ref[:, sl] = ub
            a = (gb * _sig(gb)) * ub
            a_ref[0, :, sl] = a
            acc = acc + _nn(a, wd_s[sl, :])
        h = xf + 0.5 * acc
        if final is None:
            h_ref[...] = h
        else:
            dh = _final_loss_tile(h, gf_ref[...], t_ref[...], s_ref)
            dh_ref[...] = dh
            dhb_ref[...] = (0.5 * dh).astype(BF)

    row = lambda i: (i, 0)
    vec = pl.BlockSpec((1, D), lambda i: (0, 0))
    tile = pl.BlockSpec((TM, D), row)
    saved_shapes = [jax.ShapeDtypeStruct((T, D), BF), jax.ShapeDtypeStruct((T, F), BF), jax.ShapeDtypeStruct((T, F), BF),
                    jax.ShapeDtypeStruct((1, T, F), BF)]
    saved_specs = [tile, pl.BlockSpec((TM, F), row), pl.BlockSpec((TM, F), row),
                   pl.BlockSpec((1, TM, F), lambda i: (0, i, 0))]
    if final is None:
        extra_args, extra_specs = [], []
        head_shapes, head_specs = [jax.ShapeDtypeStruct((T, D), F32)], [tile]
    else:
        extra_args, extra_specs = list(final), [vec, tile]
        head_shapes = [jax.ShapeDtypeStruct((T, D), F32), jax.ShapeDtypeStruct((T, D), BF), jax.ShapeDtypeStruct((8, D), F32)]
        head_specs = [tile, tile, pl.BlockSpec((8, D), lambda i: (0, 0))]
    return _call(
        body, name=name, grid=(T // TM,), args=[x, g, *wbufs, *extra_args], comm=comm,
        in_specs=[tile, vec, ANY, ANY, ANY] + extra_specs,
        out_shape=head_shapes + saved_shapes, out_specs=head_specs + saved_specs,
        scratch_shapes=[pltpu.VMEM((F, D), BF)] * 3 + [pltpu.SemaphoreType.DMA((3,))])


def _ffn_gate_up(x, g, wbufs, offs, name, comm=None):
    nf = F // FC

    def body(x_ref, g_ref, b0, b1, n_ref, gg_ref, uu_ref, a_ref, wg_s, wu_s, sem):
        _load_ffn_weights((b0, b1), offs, (wg_s, wu_s), sem)
        xf = x_ref[...]
        r = lax.rsqrt(jnp.mean(xf * xf, axis=-1, keepdims=True) + EPS)
        nb = (xf * r * g_ref[...]).astype(BF)
        n_ref[...] = nb
        for c in range(nf):
            sl = slice(c * FC, (c + 1) * FC)
            gb = _nt(nb, wg_s[sl, :]).astype(BF)
            ub = _nt(nb, wu_s[sl, :]).astype(BF)
            gg_ref[:, sl] = gb
            uu_ref[:, sl] = ub
            a_ref[0, :, sl] = (gb * _sig(gb)) * ub

    row = lambda i: (i, 0)
    tile = pl.BlockSpec((TM, D), row)
    return _call(
        body, name=name, grid=(T // TM,), args=[x, g, *wbufs], comm=comm,
        in_specs=[tile, pl.BlockSpec((1, D), lambda i: (0, 0)), ANY, ANY],
        out_shape=[jax.ShapeDtypeStruct((T, D), BF), jax.ShapeDtypeStruct((T, F), BF), jax.ShapeDtypeStruct((T, F), BF),
                   jax.ShapeDtypeStruct((1, T, F), BF)],
        out_specs=[tile, pl.BlockSpec((TM, F), row), pl.BlockSpec((TM, F), row),
                   pl.BlockSpec((1, TM, F), lambda i: (0, i, 0))],
        scratch_shapes=[pltpu.VMEM((F, D), BF)] * 2 + [pltpu.SemaphoreType.DMA((2,))])


def _ffn_down(x, act, wbuf, off, name, comm=None):
    def body(x_ref, a_ref, b0, h_ref, wd_s, sem):
        _load_ffn_weights((b0,), (off,), (wd_s,), sem)
        h_ref[...] = x_ref[...] + 0.5 * _nn(a_ref[0], wd_s[...])

    tile = pl.BlockSpec((TM, D), lambda i: (i, 0))
    return _call(
        body, name=name, grid=(T // TM,), args=[x, act, wbuf], comm=comm,
        in_specs=[tile, pl.BlockSpec((1, TM, F), lambda i: (0, i, 0)), ANY],
        out_shape=[jax.ShapeDtypeStruct((T, D), F32)], out_specs=[tile],
        scratch_shapes=[pltpu.VMEM((F, D), BF), pltpu.SemaphoreType.DMA((1,))])


def _load_in_proj(parts, w_s, sem):
    @pl.when(pl.program_id(0) == 0)
    def _():
        shard = NG * D // NDEV
        rows = shard // len(parts)
        cps = [pltpu.make_async_copy(buf.at[pl.ds(first + k * rows, rows), :],
                                     w_s.at[pl.ds(k * shard + p * rows, rows), :], sem.at[p * NDEV + k])
               for p, (buf, first) in enumerate(parts) for k in range(NDEV)]
        for cp in cps:
            cp.start()
        for cp in cps:
            cp.wait()


def _mix_in(h1, gm, win, comm=None):
    def body(h_ref, g_ref, *rest):
        w_any, (u_ref, z_ref, w_s, sem) = rest[:len(win)], rest[len(win):]
        _load_in_proj([(b, first) for b, (_, first) in zip(w_any, win)], w_s, sem)
        xf = h_ref[...]
        r = lax.rsqrt(jnp.mean(xf * xf, axis=-1, keepdims=True) + EPS)
        ub = (xf * r * g_ref[...]).astype(BF)
        u_ref[...] = ub
        for j in range(NG):
            z_ref[j] = _nt(ub, w_s[j * D:(j + 1) * D, :]).astype(BF)

    row = lambda i: (i, 0)
    return _call(
        body, name="mix_in", grid=(T // TM,), args=[h1, gm] + [b for b, _ in win], comm=comm,
        in_specs=[pl.BlockSpec((TM, D), row), pl.BlockSpec((1, D), lambda i: (0, 0))] + [ANY] * len(win),
        out_shape=[jax.ShapeDtypeStruct((T, D), BF), jax.ShapeDtypeStruct((NG, T, D), BF)],
        out_specs=[pl.BlockSpec((TM, D), row), pl.BlockSpec((NG, TM, D), lambda i: (0, i, 0))],
        scratch_shapes=[pltpu.VMEM((NG * D, D), BF), pltpu.SemaphoreType.DMA((NDEV * len(win),))])


def _shift_up(w, b):
    return w if b == 0 else pltpu.roll(w, w.shape[0] - b, 0)


def _fold8(p):
    red = p[0:8, :]
    for i in range(1, p.shape[0] // 8):
        red = red + p[8 * i:8 * i + 8, :]
    return red


def _dft_constants():
    import numpy as np
    nh = NB // 2
    f, n = np.arange(nh)[:, None], np.arange(NB)[None, :]
    ang = 2.0 * np.pi / NB * f * n
    fc = np.cos(ang)
    fs = np.where(f == 0, (-1.0) ** n, np.sin(ang))
    scale = np.where(f == 0, 1.0, 2.0) / NB
    ic = (scale * np.cos(ang)).T
    isn = np.where(f == 0, (-1.0) ** n / NB, scale * np.sin(ang)).T
    d = (KA - 1 - np.arange(32))[None, :]
    valid = (np.arange(32) < KA)[None, :]
    angk = 2.0 * np.pi / NB * f * d
    kc = np.where(valid, np.cos(angk), 0.0)
    ks = np.where(valid, np.sin(angk), 0.0)
    k2 = np.where(valid, np.where(f == 0, (-1.0) ** d, np.cos(angk)), 0.0)
    rtc = np.where(valid, scale * np.cos(angk), 0.0).T
    rts = np.where(valid, np.where(f == 0, (-1.0) ** d / NB, scale * np.sin(angk)), 0.0).T

    def bf(a):
        return jnp.asarray(a, F32).astype(BF)

    def split(a):
        hi = bf(a)
        return hi, (jnp.asarray(a, F32) - hi.astype(F32)).astype(BF)

    return dict(fc=bf(fc), fs=bf(fs), ic_hi=bf(ic[HB:]), is_hi=bf(isn[HB:]), ic_lo=bf(ic[:HB]), is_lo=bf(isn[:HB]),
                kc=split(kc), ks=split(ks), k2=split(k2), rtc=split(rtc), rts=split(rts))


def _dot3(m_hi, m_lo, x):
    x_hi = x.astype(BF)
    x_lo = (x - x_hi.astype(F32)).astype(BF)
    return _nn(m_hi, x_hi) + _nn(m_hi, x_lo) + _nn(m_lo, x_hi)


def _whole(a):
    return pl.BlockSpec(a.shape, lambda c, t: (0,) * a.ndim)


def _filter_spectrum(cw_ref, tabs, hc, hs, h2):
    w32 = cw_ref[0:32, :]
    for (hi, lo), dst in zip(tabs, (hc, hs, h2)):
        dst[...] = _dot3(hi[...], lo[...], w32)


def _conv_fwd_dft(z, cw, bias, dft, comm=None):
    nt = T // TB
    hb = TB // HB

    def body(z_ref, zh_ref, cw_ref, b_ref, fc_ref, fs_ref, ic_ref, is_ref, kch, kcl, ksh, ksl, k2h, k2l,
             a1_ref, q_ref, aext, ppad, hc, hs, h2):
        first = pl.program_id(1) == 0
        f = lambda ref, j: ref[j].astype(F32)

        @pl.when(first)
        def _():
            _filter_spectrum(cw_ref, ((kch, kcl), (ksh, ksl), (k2h, k2l)), hc, hs, h2)

        aext[0:HB, :] = jnp.where(first, 0.0, f(zh_ref, 0) * _sig(f(zh_ref, 1))).astype(BF)
        aext[HB:, :] = (f(z_ref, 0) * _sig(f(z_ref, 1))).astype(BF)
        ppad[0:8, :] = jnp.where(first, 0.0, f(zh_ref, 3)[HB - 8:HB, :] * f(zh_ref, 4)[HB - 8:HB, :])
        ppad[8:, :] = f(z_ref, 3) * f(z_ref, 4)
        bias_row = b_ref[...]

        for j in range(TB // HB):
            xs = aext[j * HB:j * HB + NB, :]
            xa, xb = _nn(fc_ref[...], xs), _nn(fs_ref[...], xs)
            yc = (hc[...] * xa - hs[...] * xb).astype(BF)
            ys = (h2[...] * xb + hs[...] * xa).astype(BF)
            y = _nn(ic_ref[...], yc) + _nn(is_ref[...], ys)
            a1_ref[j * HB:(j + 1) * HB, :] = (y + bias_row).astype(BF)

        def chunk(r, carry):
            base = pl.multiple_of(r * CHB, CHB)
            pw = ppad[pl.ds(base, CHB + 8), :]
            v = (cw_ref[pl.ds(32, 1), :] * _shift_up(pw, 6)[0:CHB, :]
                 + cw_ref[pl.ds(33, 1), :] * _shift_up(pw, 7)[0:CHB, :]
                 + cw_ref[pl.ds(34, 1), :] * pw[8:8 + CHB, :])
            q_ref[pl.ds(base, CHB), :] = (z_ref[2, pl.ds(base, CHB), :].astype(F32) * v).astype(BF)
            return carry

        lax.fori_loop(0, TB // CHB, chunk, 0)

    blk = pl.BlockSpec((TB, CW), lambda c, t: (t, c))
    tabs = [dft["fc"], dft["fs"], dft["ic_hi"], dft["is_hi"], *dft["kc"], *dft["ks"], *dft["k2"]]
    return _call(
        body, name="conv_fwd", grid=(D // CW, nt), comm=comm, args=[z, z, cw, bias] + tabs,
        in_specs=[pl.BlockSpec((5, TB, CW), lambda c, t: (0, t, c)),
                  pl.BlockSpec((5, HB, CW), lambda c, t: (0, jnp.maximum(t * hb - 1, 0), c)),
                  pl.BlockSpec((40, CW), lambda c, t: (0, c)), pl.BlockSpec((1, CW), lambda c, t: (0, c))]
                 + [_whole(a) for a in tabs],
        out_shape=[jax.ShapeDtypeStruct((T, D), BF), jax.ShapeDtypeStruct((T, D), BF)], out_specs=[blk, blk],
        scratch_shapes=[pltpu.VMEM((TB + HB, CW), BF), pltpu.VMEM((TB + 8, CW), F32)]
                       + [pltpu.VMEM((NB // 2, CW), F32)] * 3)


def _conv_bwd_dft(z, da1, dq, dzg, cw, dft, comm=None):
    nt = T // TB
    hb = TB // HB
    last_h = T // HB - 1

    def body(z_ref, zp_ref, zn_ref, da1_ref, da1n_ref, dq_ref, dqn_ref, dzg_ref, cw_ref,
             fc_ref, fs_ref, ic_ref, is_ref, kch, kcl, ksh, ksl, k2h, k2l, rch, rcl, rsh, rsl,
             dz_ref, dwa_ref, dwb_ref, aext, dyext, ppad, dvpad, hc, hs, h2, rc, rs, nyq, acc_b):
        t = pl.program_id(1)
        first, last = t == 0, t == nt - 1
        f = lambda ref, j: ref[j].astype(F32)

        @pl.when(first)
        def _():
            _filter_spectrum(cw_ref, ((kch, kcl), (ksh, ksl), (k2h, k2l)), hc, hs, h2)
            rc[...] = jnp.zeros_like(rc)
            rs[...] = jnp.zeros_like(rs)
            nyq[...] = jnp.zeros_like(nyq)
            acc_b[...] = jnp.zeros_like(acc_b)

        aext[0:HB, :] = jnp.where(first, 0.0, f(zp_ref, 0) * _sig(f(zp_ref, 1))).astype(BF)
        aext[HB:, :] = (f(z_ref, 0) * _sig(f(z_ref, 1))).astype(BF)
        dyext[0:TB, :] = da1_ref[...]
        dyext[TB:, :] = jnp.where(last, 0.0, da1n_ref[...].astype(F32)).astype(BF)
        ppad[0:8, :] = jnp.where(first, 0.0, f(zp_ref, 3)[HB - 8:HB, :] * f(zp_ref, 4)[HB - 8:HB, :])
        ppad[8:, :] = f(z_ref, 3) * f(z_ref, 4)
        dvpad[0:TB, :] = dq_ref[...].astype(F32) * f(z_ref, 2)
        dvpad[TB:, :] = jnp.where(last, 0.0, dqn_ref[...].astype(F32)[0:8, :] * f(zn_ref, 2)[0:8, :])

        for j in range(TB // HB):
            rows = slice(j * HB, (j + 1) * HB)
            dys = dyext[j * HB:j * HB + NB, :]
            da, db = _nn(fc_ref[...], dys), _nn(fs_ref[...], dys)
            gc = (hc[...] * da + hs[...] * db).astype(BF)
            gs = (h2[...] * db - hs[...] * da).astype(BF)
            da0 = _nn(ic_ref[...], gc) + _nn(is_ref[...], gs)
            z0, z1 = z_ref[0, rows, :].astype(F32), z_ref[1, rows, :].astype(F32)
            s1 = _sig(z1)
            dz_ref[0, rows, :] = (da0 * s1).astype(BF)
            dz_ref[1, rows, :] = (da0 * z0 * (s1 * (1.0 - s1))).astype(BF)
            xs = aext[j * HB:j * HB + NB, :]
            xa, xb = _nn(fc_ref[...], xs), _nn(fs_ref[...], xs)
            dyb = dyext[rows, :]
            pa, pb = _nn(fc_ref[:, HB:NB], dyb), _nn(fs_ref[:, HB:NB], dyb)
            rc[...] += pa * xa + pb * xb
            rs[...] += pb * xa - pa * xb
            nyq[...] += pb[0:8, :] * xb[0:8, :]

        def chunk(r, carry):
            base = pl.multiple_of(r * CHB, CHB)
            rows = pl.ds(base, CHB)
            pw = ppad[pl.ds(base, CHB + 8), :]
            p6 = _shift_up(pw, 6)[0:CHB, :]
            p7 = _shift_up(pw, 7)[0:CHB, :]
            p8 = pw[8:8 + CHB, :]
            wb0, wb1, wb2 = cw_ref[pl.ds(32, 1), :], cw_ref[pl.ds(33, 1), :], cw_ref[pl.ds(34, 1), :]
            v = wb0 * p6 + wb1 * p7 + wb2 * p8
            dz_ref[2, rows, :] = (dq_ref[rows, :].astype(F32) * v).astype(BF)
            dvw = dvpad[pl.ds(base, CHB + 8), :]
            dvc = dvw[0:CHB, :]
            dp = wb2 * dvc + wb1 * _shift_up(dvw, 1)[0:CHB, :] + wb0 * _shift_up(dvw, 2)[0:CHB, :]
            dz_ref[3, rows, :] = (dp * z_ref[4, rows, :].astype(F32)).astype(BF)
            dz_ref[4, rows, :] = (dp * z_ref[3, rows, :].astype(F32)).astype(BF)
            acc_b[0:8, :] += _fold8(dvc * p6)
            acc_b[8:16, :] += _fold8(dvc * p7)
            acc_b[16:24, :] += _fold8(dvc * p8)
            dz_ref[5, rows, :] = dzg_ref[0, rows, :]
            dz_ref[6, rows, :] = dzg_ref[1, rows, :]
            return carry

        lax.fori_loop(0, TB // CHB, chunk, 0)

        @pl.when(last)
        def _():
            row0 = lax.broadcasted_iota(jnp.int32, (NB // 2, CW), 0) == 0
            ny = jnp.broadcast_to(nyq[0:1, :], (NB // 2, CW))
            rcv = jnp.where(row0, rc[...] - ny, rc[...])
            rsv = jnp.where(row0, ny, rs[...])
            dwa_ref[...] = _dot3(rch[...], rcl[...], rcv) + _dot3(rsh[...], rsl[...], rsv)
            for k in range(KB):
                dwb_ref[k:k + 1, :] = jnp.sum(acc_b[8 * k:8 * k + 8, :], axis=0, keepdims=True)
            dwb_ref[KB:8, :] = jnp.zeros((8 - KB, CW), F32)

    blk = lambda c, t: (t, c)
    nxt = lambda c, t: (jnp.minimum((t + 1) * hb, last_h), c)
    tabs = [dft["fc"], dft["fs"], dft["ic_lo"], dft["is_lo"], *dft["kc"], *dft["ks"], *dft["k2"], *dft["rtc"], *dft["rts"]]
    return _call(
        body, name="conv_bwd", grid=(D // CW, nt), comm=comm, args=[z, z, z, da1, da1, dq, dq, dzg, cw] + tabs,
        in_specs=[pl.BlockSpec((5, TB, CW), lambda c, t: (0, t, c)),
                  pl.BlockSpec((5, HB, CW), lambda c, t: (0, jnp.maximum(t * hb - 1, 0), c)),
                  pl.BlockSpec((5, HB, CW), lambda c, t: (0, jnp.minimum((t + 1) * hb, last_h), c)),
                  pl.BlockSpec((TB, CW), blk), pl.BlockSpec((HB, CW), nxt),
                  pl.BlockSpec((TB, CW), blk), pl.BlockSpec((HB, CW), nxt),
                  pl.BlockSpec((2, TB, CW), lambda c, t: (0, t, c)),
                  pl.BlockSpec((40, CW), lambda c, t: (0, c))]
                 + [_whole(a) for a in tabs],
        out_shape=[jax.ShapeDtypeStruct((NG, T, D), BF), jax.ShapeDtypeStruct((32, D), F32),
                   jax.ShapeDtypeStruct((8, D), F32)],
        out_specs=[pl.BlockSpec((NG, TB, CW), lambda c, t: (0, t, c)),
                   pl.BlockSpec((32, CW), lambda c, t: (0, c)), pl.BlockSpec((8, CW), lambda c, t: (0, c))],
        scratch_shapes=[pltpu.VMEM((TB + HB, CW), BF), pltpu.VMEM((TB + HB, CW), BF),
                        pltpu.VMEM((TB + 8, CW), F32), pltpu.VMEM((TB + 8, CW), F32)]
                       + [pltpu.VMEM((NB // 2, CW), F32)] * 5 + [pltpu.VMEM((8, CW), F32), pltpu.VMEM((24, CW), F32)])


def _layernorm_silu(a1, lng, lnb):
    mu = jnp.mean(a1, axis=-1, keepdims=True)
    xc = a1 - mu
    rs = lax.rsqrt(jnp.mean(xc * xc, axis=-1, keepdims=True) + EPS)
    xh = xc * rs
    a2 = xh * lng + lnb
    sg = _sig(a2)
    return xh, rs, a2, sg


def _square_specs(blocks):
    return [pl.BlockSpec((D, D), lambda i, b=b: (b, 0)) for b in blocks]


def _mix_out(a1, q, z, h1, lng, lnb, wsq, comm=None):
    def body(a1_ref, q_ref, ga_ref, gb_ref, h_ref, lng_ref, lnb_ref, wa_ref, wb_ref, wo_ref, h2_ref, ya_ref, yb_ref):
        _, _, a2, sg = _layernorm_silu(a1_ref[...].astype(F32), lng_ref[...], lnb_ref[...])
        ya = _nn((a2 * sg).astype(BF), wa_ref[...])
        yb = _nn(q_ref[...], wb_ref[...])
        ya_ref[...] = ya.astype(BF)
        yb_ref[...] = yb.astype(BF)
        m = _sig(ga_ref[...].astype(F32)) * ya + _sig(gb_ref[...].astype(F32)) * yb
        h2_ref[...] = h_ref[...] + _nn(m.astype(BF), wo_ref[...])

    row = lambda i: (i, 0)
    vec = pl.BlockSpec((1, D), lambda i: (0, 0))
    return _call(
        body, name="mix_out", grid=(T // TM,), args=[a1, q, z, z, h1, lng, lnb, wsq, wsq, wsq], comm=comm,
        in_specs=[pl.BlockSpec((TM, D), row), pl.BlockSpec((TM, D), row),
                  pl.BlockSpec((None, TM, D), lambda i: (5, i, 0)), pl.BlockSpec((None, TM, D), lambda i: (6, i, 0)),
                  pl.BlockSpec((TM, D), row), vec, vec] + _square_specs((0, 1, 2)),
        out_shape=[jax.ShapeDtypeStruct((T, D), F32), jax.ShapeDtypeStruct((T, D), BF), jax.ShapeDtypeStruct((T, D), BF)],
        out_specs=[pl.BlockSpec((TM, D), row)] * 3)


def _rmsnorm_bwd(xf, g, dn):
    r = lax.rsqrt(jnp.mean(xf * xf, axis=-1, keepdims=True) + EPS)
    xr = xf * r
    gdn = dn * g
    dx = r * gdn - xr * (r * jnp.mean(gdn * xr, axis=-1, keepdims=True))
    return dx, jnp.sum(dn * xr, axis=0, keepdims=True)


def _ffn_bwd_hidden(dh, gg, uu, wbuf, off, name, comm=None):
    nf = F // FC

    def body(dh_ref, gg_ref, uu_ref, b0, dgu_ref, wd_s, sem):
        _load_ffn_weights((b0,), (off,), (wd_s,), sem)
        dhb = dh_ref[...]
        for c in range(nf):
            sl = slice(c * FC, (c + 1) * FC)
            da = _nt(dhb, wd_s[sl, :]).astype(BF)
            gb, ub = gg_ref[:, sl], uu_ref[:, sl]
            sg = _sig(gb)
            dgu_ref[0, :, sl] = (da * ub) * (sg * (1.0 + gb * (1.0 - sg)))
            dgu_ref[0, :, F + c * FC:F + (c + 1) * FC] = da * (gb * sg)

    row = lambda i: (i, 0)
    return _call(
        body, name=name, grid=(T // TM,), args=[dh, gg, uu, wbuf], comm=comm,
        in_specs=[pl.BlockSpec((TM, D), row), pl.BlockSpec((TM, F), row), pl.BlockSpec((TM, F), row), ANY],
        out_shape=[jax.ShapeDtypeStruct((1, T, 2 * F), BF)],
        out_specs=[pl.BlockSpec((1, TM, 2 * F), lambda i: (0, i, 0))],
        scratch_shapes=[pltpu.VMEM((F, D), BF), pltpu.SemaphoreType.DMA((1,))])


def _ffn_bwd_input(dgu, dh, x, g, wbufs, offs, name, comm=None, after=None):
    def body(dgu_ref, dh_ref, x_ref, g_ref, b0, b1, dx_ref, s_ref, w_s, sem):
        _load_ffn_weights((b0, b1), offs, (w_s.at[pl.ds(0, F), :], w_s.at[pl.ds(F, F), :]), sem)

        @pl.when(pl.program_id(0) == 0)
        def _():
            s_ref[...] = jnp.zeros_like(s_ref)

        dn = _nn(dgu_ref[0], w_s[...])
        dxn, dg = _rmsnorm_bwd(x_ref[...], g_ref[...], dn)
        dx_ref[...] = dh_ref[...] + dxn
        s_ref[0:1, :] += dg

    row = lambda i: (i, 0)
    return _call(
        body, name=name, grid=(T // TM,), args=[dgu, dh, x, g, *wbufs], comm=comm, after=after,
        in_specs=[pl.BlockSpec((1, TM, 2 * F), lambda i: (0, i, 0)), pl.BlockSpec((TM, D), row),
                  pl.BlockSpec((TM, D), row), pl.BlockSpec((1, D), lambda i: (0, 0)), ANY, ANY],
        out_shape=[jax.ShapeDtypeStruct((T, D), F32), jax.ShapeDtypeStruct((8, D), F32)],
        out_specs=[pl.BlockSpec((TM, D), row), pl.BlockSpec((8, D), lambda i: (0, 0))],
        scratch_shapes=[pltpu.VMEM((2 * F, D), BF), pltpu.SemaphoreType.DMA((2,))])


def _tn_matmul(lhs, rhs, tr, name, comm=None):
    ng, _, cdim = lhs.shape
    nc, nk = cdim // tr, T // TK
    if rhs.ndim == 2:
        r_spec = pl.BlockSpec((TK, D), lambda g, c, k: (k, 0))
    else:
        r_spec = pl.BlockSpec((None, TK, D), lambda g, c, k: (g, k, 0))

    def body(l_ref, r_ref, o_ref, acc):
        k = pl.program_id(2)

        @pl.when(k == 0)
        def _():
            acc[...] = jnp.zeros_like(acc)

        acc[...] += _tn(l_ref[...], r_ref[...])

        @pl.when(k == nk - 1)
        def _():
            o_ref[...] = acc[...].astype(BF)

    return _call(
        body, name=name, grid=(ng, nc, nk), args=[lhs, rhs], comm=comm,
        in_specs=[pl.BlockSpec((None, TK, tr), lambda g, c, k: (g, k, c)), r_spec],
        out_shape=[jax.ShapeDtypeStruct((ng * cdim, D), BF)],
        out_specs=[pl.BlockSpec((tr, D), lambda g, c, k: (g * nc + c, 0))],
        scratch_shapes=[pltpu.VMEM((tr, D), F32)])


def _mix_out_bwd(dh2, ya, yb, z, a1, q, lng, lnb, wsq, comm=None):
    def body(dh_ref, ya_ref, yb_ref, ga_ref, gb_ref, a1_ref, q_ref, lng_ref, lnb_ref, wa_ref, wb_ref, wo_ref,
             dzg_ref, da1_ref, dq_ref, l_ref, r_ref, s_ref):
        @pl.when(pl.program_id(0) == 0)
        def _():
            s_ref[...] = jnp.zeros_like(s_ref)

        dhb = dh_ref[...].astype(BF)
        dm = _nt(dhb, wo_ref[...]).astype(BF)
        ya, yb = ya_ref[...], yb_ref[...]
        sa, sb = _sig(ga_ref[...]), _sig(gb_ref[...])
        l_ref[0] = sa * ya + sb * yb
        l_ref[2] = q_ref[...]
        dzg_ref[0] = (dm * ya) * (sa * (1.0 - sa))
        dzg_ref[1] = (dm * yb) * (sb * (1.0 - sb))
        dya = dm * sa
        dyb = dm * sb
        r_ref[0] = dhb
        r_ref[1] = dya
        r_ref[2] = dyb
        dq_ref[...] = _nt(dyb, wb_ref[...]).astype(BF)
        da3 = _nt(dya, wa_ref[...])
        lng = lng_ref[...]
        xh, rs, a2, sg = _layernorm_silu(a1_ref[...].astype(F32), lng, lnb_ref[...])
        l_ref[1] = (a2 * sg).astype(BF)
        da2 = da3 * (sg * (1.0 + a2 * (1.0 - sg)))
        s_ref[0:1, :] += jnp.sum(da2 * xh, axis=0, keepdims=True)
        s_ref[1:2, :] += jnp.sum(da2, axis=0, keepdims=True)
        dxh = da2 * lng
        da1 = rs * (dxh - jnp.mean(dxh, axis=-1, keepdims=True) - xh * jnp.mean(dxh * xh, axis=-1, keepdims=True))
        da1_ref[...] = da1.astype(BF)
        s_ref[2:3, :] += jnp.sum(da1, axis=0, keepdims=True)

    row = lambda i: (i, 0)
    row3 = lambda i: (0, i, 0)
    vec = pl.BlockSpec((1, D), lambda i: (0, 0))
    return _call(
        body, name="mix_out_bwd", grid=(T // TM,), args=[dh2, ya, yb, z, z, a1, q, lng, lnb, wsq, wsq, wsq], comm=comm,
        in_specs=[pl.BlockSpec((TM, D), row), pl.BlockSpec((TM, D), row), pl.BlockSpec((TM, D), row),
                  pl.BlockSpec((None, TM, D), lambda i: (5, i, 0)), pl.BlockSpec((None, TM, D), lambda i: (6, i, 0)),
                  pl.BlockSpec((TM, D), row), pl.BlockSpec((TM, D), row), vec, vec] + _square_specs((0, 1, 2)),
        out_shape=[jax.ShapeDtypeStruct((2, T, D), BF), jax.ShapeDtypeStruct((T, D), BF),
                   jax.ShapeDtypeStruct((T, D), BF), jax.ShapeDtypeStruct((3, T, D), BF),
                   jax.ShapeDtypeStruct((3, T, D), BF), jax.ShapeDtypeStruct((8, D), F32)],
        out_specs=[pl.BlockSpec((2, TM, D), row3), pl.BlockSpec((TM, D), row), pl.BlockSpec((TM, D), row),
                   pl.BlockSpec((3, TM, D), row3), pl.BlockSpec((3, TM, D), row3), pl.BlockSpec((8, D), lambda i: (0, 0))])


def _mix_in_bwd(dz, dh2, h1, gm, win, comm=None):
    def body(dz_ref, dh_ref, h_ref, g_ref, *rest):
        w_any, (o_ref, ob_ref, s_ref, w_s, sem) = rest[:len(win)], rest[len(win):]
        _load_in_proj([(b, first) for b, (_, first) in zip(w_any, win)], w_s, sem)

        @pl.when(pl.program_id(0) == 0)
        def _():
            s_ref[...] = jnp.zeros_like(s_ref)

        du = _nn(dz_ref[0], w_s[0:D, :])
        for j in range(1, NG):
            du = du + _nn(dz_ref[j], w_s[j * D:(j + 1) * D, :])
        dx, dg = _rmsnorm_bwd(h_ref[...], g_ref[...], du)
        dh1 = dh_ref[...] + dx
        o_ref[...] = dh1
        ob_ref[...] = (0.5 * dh1).astype(BF)
        s_ref[0:1, :] += dg

    row = lambda i: (i, 0)
    return _call(
        body, name="mix_in_bwd", grid=(T // TM,), args=[dz, dh2, h1, gm] + [b for b, _ in win], comm=comm,
        in_specs=[pl.BlockSpec((NG, TM, D), lambda i: (0, i, 0)), pl.BlockSpec((TM, D), row),
                  pl.BlockSpec((TM, D), row), pl.BlockSpec((1, D), lambda i: (0, 0))] + [ANY] * len(win),
        out_shape=[jax.ShapeDtypeStruct((T, D), F32), jax.ShapeDtypeStruct((T, D), BF), jax.ShapeDtypeStruct((8, D), F32)],
        out_specs=[pl.BlockSpec((TM, D), row), pl.BlockSpec((TM, D), row), pl.BlockSpec((8, D), lambda i: (0, 0))],
        scratch_shapes=[pltpu.VMEM((NG * D, D), BF), pltpu.SemaphoreType.DMA((NDEV * len(win),))])


def _row_tile(n, want, mult):
    for t in range(min(want, n), 0, -1):
        if n % t == 0 and t % mult == 0:
            return t
    return n


def _pack_small(s_ffn1, s_in, s_mix, s_ffn2, s_final, dwa, dwb):
    def body(f1, mi, mo, f2, fl, wa_ref, wb_ref, v_ref, k_ref):
        for dst, (ref, row) in enumerate(((f1, 0), (mi, 0), (mo, 0), (mo, 1), (mo, 2), (f2, 0), (fl, 0), (fl, 1))):
            v_ref[dst:dst + 1, :] = ref[row:row + 1, :]
        for k in range(NDEV):
            k_ref[k, 0:32, :] = wa_ref[:, k * LANE:(k + 1) * LANE]
            k_ref[k, 32:40, :] = wb_ref[:, k * LANE:(k + 1) * LANE]

    return pl.pallas_call(
        body, name="pack_small",
        out_shape=(jax.ShapeDtypeStruct((8, D), F32), jax.ShapeDtypeStruct((NDEV, 40, LANE), F32)),
    )(s_ffn1, s_in, s_mix, s_ffn2, s_final, dwa, dwb)


def _adam_update(g, w, m, v):
    m2 = ADAM_B1 * m + (1.0 - ADAM_B1) * g
    v2 = ADAM_B2 * v + (1.0 - ADAM_B2) * (g * g)
    c1 = 1.0 - ADAM_B1 ** ADAM_STEP
    c2 = 1.0 - ADAM_B2 ** ADAM_STEP
    return -ADAM_LR * ((m2 / c1) / (jnp.sqrt(v2 / c2) + ADAM_EPS) + ADAM_WD * w), m2, v2


def _adam_small(vecs, convs, vec_params, tap_params):
    nv, nt = len(vec_params), len(tap_params)

    def body(*refs):
        v_ref, k_ref = refs[:2]
        p_refs = refs[2:2 + 3 * (nv + nt)]
        l_ref = refs[2 + 3 * (nv + nt)]
        o_refs = refs[3 + 3 * (nv + nt):]
        s, c = v_ref[0], k_ref[0]
        for k in range(1, NDEV):
            s = s + v_ref[k]
            c = c + k_ref[k]
        l_ref[...] = jnp.sum(s[7:8, :], axis=-1, keepdims=True)
        for i in range(nv):
            w_ref, m_ref, u_ref = p_refs[3 * i: 3 * i + 3]
            g_ref, d_ref, m2_ref, u2_ref = o_refs[4 * i: 4 * i + 4]
            g = s[i:i + 1, :]
            g_ref[...] = g
            d_ref[...], m2_ref[...], u2_ref[...] = _adam_update(g, w_ref[...], m_ref[...], u_ref[...])
        for i in range(nt):
            w_ref, m_ref, u_ref = p_refs[3 * (nv + i): 3 * (nv + i) + 3]
            g_ref, d_ref, m2_ref, u2_ref = o_refs[4 * (nv + i): 4 * (nv + i) + 4]
            first = tap_params[i][0]
            for k in range(w_ref.shape[0]):
                g = c[first + k:first + k + 1, :]
                g_ref[k] = g
                d_ref[k], m2_ref[k], u2_ref[k] = _adam_update(g, w_ref[k], m_ref[k], u_ref[k])

    params = [a for p in vec_params for a in p] + [a for p in tap_params for a in p[1:]]
    out_shape = [jax.ShapeDtypeStruct((1, 1), F32)]
    for p in list(vec_params) + [p[1:] for p in tap_params]:
        out_shape += [jax.ShapeDtypeStruct(p[0].shape, F32)] * 4
    outs = pl.pallas_call(body, name="adam_small", out_shape=tuple(out_shape))(vecs, convs, *params)
    groups = [tuple(outs[1 + 4 * i: 5 + 4 * i]) for i in range(nv + nt)]
    return outs[0], groups[:nv], groups[nv:]


def _adam_in_proj(parts, w, m, v, after):
    rows = w.shape[1]
    tr = _row_tile(D, 256, LANE)

    def body(*refs):
        p_refs = refs[:len(parts)]
        w_ref, m_ref, v_ref, g_ref, d_ref, m2_ref, v2_ref = refs[len(parts):]
        sums = []
        for p in p_refs:
            s = p[0].astype(F32)
            for k in range(1, p.shape[0]):
                s = s + p[k].astype(F32)
            sums.append(s)
        g = jnp.concatenate(sums, axis=0).T
        g_ref[...] = g
        d_ref[...], m2_ref[...], v2_ref[...] = _adam_update(g, w_ref[...], m_ref[...], v_ref[...])

    spec = pl.BlockSpec((tr, rows), lambda i: (i, 0))
    return _call(body, name="adam_in", grid=(D // tr,), args=list(parts) + [w, m, v], after=after,
                 in_specs=[pl.BlockSpec((p.shape[0], p.shape[1], tr), lambda i: (0, 0, i)) for p in parts] + [spec] * 3,
                 out_shape=[jax.ShapeDtypeStruct((D, rows), F32)] * 4, out_specs=[spec] * 4)


def _adam(gs, ws, ms, vs, name, after):
    n = len(gs)
    rows, cols = ws[0].shape
    tr = _row_tile(rows, min(256, rows // 2), 16)

    def body(*refs):
        for i in range(n):
            g_in, w, m, v = refs[4 * i], refs[4 * i + 1][...], refs[4 * i + 2][...], refs[4 * i + 3][...]
            g_ref, d_ref, m_ref, v_ref = refs[4 * n + 4 * i: 4 * n + 4 * i + 4]
            g = g_in[0].astype(F32)
            for k in range(1, g_in.shape[0]):
                g = g + g_in[k].astype(F32)
            g_ref[...] = g
            d_ref[...], m_ref[...], v_ref[...] = _adam_update(g, w, m, v)

    spec = pl.BlockSpec((tr, cols), lambda i: (i, 0))
    args, in_specs = [], []
    for i in range(n):
        slots, first = gs[i]
        args += [slots, ws[i], ms[i], vs[i]]
        in_specs += [pl.BlockSpec((slots.shape[0], tr, cols), lambda i, b=first // tr: (0, b + i, 0))] + [spec] * 3
    outs = _call(body, name=name, grid=(rows // tr,), args=args, in_specs=in_specs, after=after,
                 out_shape=[jax.ShapeDtypeStruct((rows, cols), F32)] * (4 * n), out_specs=[spec] * (4 * n))
    return [tuple(outs[4 * i: 4 * i + 4]) for i in range(n)]


def kernel(x, ffn1_norm, ffn1_w_gate, ffn1_w_up, ffn1_w_down, mix_norm, w_in, a_dw_w, a_dw_b, a_ln_g, a_ln_b, a_w_out, b_conv_w, b_w_out, w_o, ffn2_norm, ffn2_w_gate, ffn2_w_up, ffn2_w_down, final_norm, loss_target, m_ffn1_norm, m_ffn1_w_gate, m_ffn1_w_up, m_ffn1_w_down, m_mix_norm, m_w_in, m_a_dw_w, m_a_dw_b, m_a_ln_g, m_a_ln_b, m_a_w_out, m_b_conv_w, m_b_w_out, m_w_o, m_ffn2_norm, m_ffn2_w_gate, m_ffn2_w_up, m_ffn2_w_down, m_final_norm, v_ffn1_norm, v_ffn1_w_gate, v_ffn1_w_up, v_ffn1_w_down, v_mix_norm, v_w_in, v_a_dw_w, v_a_dw_b, v_a_ln_g, v_a_ln_b, v_a_w_out, v_b_conv_w, v_b_w_out, v_w_o, v_ffn2_norm, v_ffn2_w_gate, v_ffn2_w_up, v_ffn2_w_down, v_final_norm):
    names = ("ffn1_norm", "ffn1_w_gate", "ffn1_w_up", "ffn1_w_down", "mix_norm", "w_in", "a_dw_w", "a_dw_b",
             "a_ln_g", "a_ln_b", "a_w_out", "b_conv_w", "b_w_out", "w_o", "ffn2_norm", "ffn2_w_gate", "ffn2_w_up",
             "ffn2_w_down", "final_norm")
    w = dict(ffn1_norm=ffn1_norm, ffn1_w_gate=ffn1_w_gate, ffn1_w_up=ffn1_w_up, ffn1_w_down=ffn1_w_down,
             mix_norm=mix_norm, w_in=w_in, a_dw_w=a_dw_w, a_dw_b=a_dw_b, a_ln_g=a_ln_g, a_ln_b=a_ln_b,
             a_w_out=a_w_out, b_conv_w=b_conv_w, b_w_out=b_w_out, w_o=w_o, ffn2_norm=ffn2_norm,
             ffn2_w_gate=ffn2_w_gate, ffn2_w_up=ffn2_w_up, ffn2_w_down=ffn2_w_down, final_norm=final_norm)
    m = dict(ffn1_norm=m_ffn1_norm, ffn1_w_gate=m_ffn1_w_gate, ffn1_w_up=m_ffn1_w_up, ffn1_w_down=m_ffn1_w_down,
             mix_norm=m_mix_norm, w_in=m_w_in, a_dw_w=m_a_dw_w, a_dw_b=m_a_dw_b, a_ln_g=m_a_ln_g, a_ln_b=m_a_ln_b,
             a_w_out=m_a_w_out, b_conv_w=m_b_conv_w, b_w_out=m_b_w_out, w_o=m_w_o, ffn2_norm=m_ffn2_norm,
             ffn2_w_gate=m_ffn2_w_gate, ffn2_w_up=m_ffn2_w_up, ffn2_w_down=m_ffn2_w_down, final_norm=m_final_norm)
    v = dict(ffn1_norm=v_ffn1_norm, ffn1_w_gate=v_ffn1_w_gate, ffn1_w_up=v_ffn1_w_up, ffn1_w_down=v_ffn1_w_down,
             mix_norm=v_mix_norm, w_in=v_w_in, a_dw_w=v_a_dw_w, a_dw_b=v_a_dw_b, a_ln_g=v_a_ln_g, a_ln_b=v_a_ln_b,
             a_w_out=v_a_w_out, b_conv_w=v_b_conv_w, b_w_out=v_b_w_out, w_o=v_w_o, ffn2_norm=v_ffn2_norm,
             ffn2_w_gate=v_ffn2_w_gate, ffn2_w_up=v_ffn2_w_up, ffn2_w_down=v_ffn2_w_down, final_norm=v_final_norm)
    flat = _pack_weights(dict(wg1=ffn1_w_gate[0].T, wu1=ffn1_w_up[0].T, wd1=ffn1_w_down[0], wg2=ffn2_w_gate[0].T,
                              wu2=ffn2_w_up[0].T, wd2=ffn2_w_down[0], win=w_in[0], wa=a_w_out[0], wb=b_w_out[0],
                              wo=w_o[0]))
    cw_shard = jnp.concatenate([a_dw_w[0], jnp.zeros((1, LANE), F32), b_conv_w[0], jnp.zeros((5, LANE), F32)], axis=0)

    x2, tgt = x[0], loss_target[0]
    st_a, st_b, st_b2 = ("wg1", "wu1"), ("wd1", "win/0/2"), ("win/1/2",)
    st_c, st_d, st_e = ("wa", "wb", "wo", "wg2"), ("wu2",), ("wd2",)

    buf_a, cw = _run_comm(_join(_ag_comm(st_a, flat), _direct_comm(cw_shard, False)), "ag_ffn1")
    n1, gg1, uu1, act1, buf_b = _ffn_gate_up(x2, ffn1_norm, (buf_a, buf_a), (0, F), "ffn1_gate_up", _ag_comm(st_b, flat))
    h1, buf_b2 = _ffn_down(x2, act1, buf_b, 0, "ffn1_down", _ag_comm(st_b2, flat))
    win = ((buf_b, F), (buf_b2, 0))
    u, z, buf_c = _mix_in(h1, mix_norm, win, _ag_comm(st_c, flat))
    dft = _dft_constants()
    cw = jnp.transpose(cw, (1, 0, 2)).reshape(40, D)
    a1, q, buf_d = _conv_fwd_dft(z, cw, a_dw_b, dft, _ag_comm(st_d, flat))
    h2, ya, yb, buf_e = _mix_out(a1, q, z, h1, a_ln_g, a_ln_b, buf_c, _ag_comm(st_e, flat))
    ffn2_bufs, ffn2_offs = (buf_c, buf_d, buf_e), (3 * D, 0, 0)
    dh3, dhb3, s_final, n2, gg2, uu2, act2 = _ffn_fwd(h2, ffn2_norm, ffn2_bufs, ffn2_offs, "ffn2_fwd",
                                          final=(final_norm.reshape(1, D), tgt))

    tr_f = F // 2 if (F // 2) % LANE == 0 else F
    def pair(stage, src):
        return _rs_pair_comm(stage, src)

    def chip(stage, src, pair_buf, tag):
        return _rs_chip_comm(_pair_add(stage, src, pair_buf, "pair_add_" + tag))

    (dgu2,) = _ffn_bwd_hidden(dhb3, gg2, uu2, buf_e, 0, "ffn2_bwd_h")
    (gu2,) = _tn_matmul(dgu2, n2, tr_f, "dw_gu2")
    s2a, src2a = ("wg2", "wu2"), dict(wg2=(gu2, 0), wu2=(gu2, F))
    (gd2,) = _tn_matmul(act2, dhb3, tr_f, "dw_d2")
    s2b, src2b = ("wd2",), dict(wd2=(gd2, 0))
    dh2, s_ffn2, pair2a, pair2b = _ffn_bwd_input(dgu2, dh3, h2, ffn2_norm, (buf_c, buf_d), (3 * D, 0), "ffn2_bwd_x",
                                                 _join(pair(s2a, src2a), pair(s2b, src2b)))
    dzg, da1, dq, lsq, rsq, s_mix, recv2b = _mix_out_bwd(dh2, ya, yb, z, a1, q, a_ln_g, a_ln_b, buf_c,
                                                          chip(s2b, src2b, pair2b, "2b"))
    (gsq,) = _tn_matmul(lsq, rsq, D, "dw_square")
    ssq, srcsq = ("wa", "wb", "wo"), dict(wa=(gsq, D), wb=(gsq, 2 * D), wo=(gsq, 0))
    dz, dwa, dwb, recv2a, pairsq = _conv_bwd_dft(z, da1, dq, dzg, cw, dft,
                                                 _join(chip(s2a, src2a, pair2a, "2a"), pair(ssq, srcsq)))
    gin, recvsq = _tn_matmul(dz, u, D, "dw_in", chip(ssq, srcsq, pairsq, "sq"))
    sin_a, sin_b, srcin = ("win/0/2",), ("win/1/2",), {"win/0/2": (gin, 0), "win/1/2": (gin, 0)}
    dh1, dhb1, s_in, pairin_a, pairin_b = _mix_in_bwd(dz, dh2, h1, mix_norm, win,
                                                _join(pair(sin_a, srcin), pair(sin_b, srcin)))
    dgu1, recvin_a = _ffn_bwd_hidden(dhb1, gg1, uu1, buf_b, 0, "ffn1_bwd_h",
                                           chip(sin_a, srcin, pairin_a, "in_a"))
    gu1, recvin_b = _tn_matmul(dgu1, n1, tr_f, "dw_gu1", chip(sin_b, srcin, pairin_b, "in_b"))
    s1a, src1a = ("wg1", "wu1"), dict(wg1=(gu1, 0), wu1=(gu1, F))
    gd1, pair1a = _tn_matmul(act1, dhb1, tr_f, "dw_d1", pair(s1a, src1a))
    s1b, src1b = ("wd1",), dict(wd1=(gd1, 0))
    xchg1 = _join(chip(s1a, src1a, pair1a, "1a"), pair(s1b, src1b))
    xchg1_sems, xchg1_bufs, token = _comm_start(xchg1, "xchg_ffn1_start")
    dx, s_ffn1 = _ffn_bwd_input(dgu1, dh1, x2, ffn1_norm, (buf_a, buf_a), (0, F), "ffn1_bwd_x", after=token)
    (_, gd1), (recv1a, pair1b) = _comm_wait(xchg1, "xchg_ffn1_wait", xchg1_sems, xchg1_bufs, s_ffn1)
    src1b = dict(wd1=(gd1, 0))

    vec8, convk = _pack_small(s_ffn1, s_in, s_mix, s_ffn2, s_final, dwa, dwb)
    tail = _join(chip(s1b, src1b, pair1b, "1b"), _join(_direct_comm(vec8, False), _direct_comm(convk, True)))
    tail_sems, tail_bufs, token = _comm_start(tail, "xchg_tail_start")

    fs = F // NDEV
    g = dict(ffn1_w_gate=(recv1a, 0), ffn1_w_up=(recv1a, fs), ffn2_w_gate=(recv2a, 0), ffn2_w_up=(recv2a, fs),
             ffn2_w_down=(recv2b, 0), a_w_out=(recvsq, 0), b_w_out=(recvsq, D // NDEV), w_o=(recvsq, 2 * (D // NDEV)))
    grad, upd = {}, {}

    def run(group, name, after, as2d=lambda a: a[0], back=lambda a, n: a.reshape(w[n].shape)):
        res = _adam([g[n] for n in group], [as2d(w[n]) for n in group], [as2d(m[n]) for n in group],
                    [as2d(v[n]) for n in group], name, after)
        for n, r in zip(group, res):
            grad[n], upd[n] = back(r[0], n), tuple(back(a, n) for a in r[1:])
        return res[0][0]

    done = run(("ffn1_w_gate", "ffn1_w_up", "ffn2_w_gate", "ffn2_w_up"), "adam_gate_up", token,
               as2d=lambda a: a[0].T, back=lambda a, n: a.T[None])
    r_in = _adam_in_proj([recvin_a, recvin_b], w_in[0], m_w_in[0], v_w_in[0], done)
    grad["w_in"], upd["w_in"] = r_in[0][None], tuple(a[None] for a in r_in[1:])
    done = run(("a_w_out", "b_w_out", "w_o"), "adam_square", r_in[0])
    _, (recv1b, vec_all, conv_all) = _comm_wait(tail, "xchg_tail_wait", tail_sems, tail_bufs, done)
    g["ffn1_w_down"] = (recv1b, 0)
    run(("ffn1_w_down", "ffn2_w_down"), "adam_down", done)
    vec_names = ("ffn1_norm", "mix_norm", "a_ln_g", "a_ln_b", "a_dw_b", "ffn2_norm", "final_norm")
    tap_names, tap_rows = ("a_dw_w", "b_conv_w"), (0, 32)
    taps = lambda a: jnp.transpose(a, (1, 0, 2))
    loss, vec_res, tap_res = _adam_small(
        vec_all, conv_all, [tuple(t[n].reshape(1, D) for t in (w, m, v)) for n in vec_names],
        [(r,) + tuple(taps(t[n]) for t in (w, m, v)) for n, r in zip(tap_names, tap_rows)])
    for n, r in zip(vec_names, vec_res):
        grad[n], upd[n] = r[0].reshape(w[n].shape), tuple(a.reshape(w[n].shape) for a in r[1:])
    for n, r in zip(tap_names, tap_res):
        grad[n], upd[n] = taps(r[0]), tuple(taps(a) for a in r[1:])

    return (loss.reshape(()), dx.reshape(x.shape), *[grad[n] for n in names], *[upd[n][0] for n in names],
            *[upd[n][1] for n in names], *[upd[n][2] for n in names])
```

```python
import jax
import jax.numpy as jnp
from jax import lax
from jax.experimental import pallas as pl
from jax.experimental.pallas import tpu as pltpu

T = 4096
D = 1024
F = 2816
NG = 7
NDEV = 8
NCHIP = 4
KA, KB = 31, 3
EPS = 1e-6
ADAM_LR, ADAM_B1, ADAM_B2, ADAM_EPS, ADAM_WD, ADAM_STEP = 0.001, 0.9, 0.999, 1e-08, 0.01, 10

TM = 512
FC = 256
TB = 1024
NB = 256
HB = NB // 2
CW = 256
CHB = 64
LANE = 128
TK = 2048
VMEM_LIMIT = 56 * 1024 * 1024

BF = jnp.bfloat16
F32 = jnp.float32
MESH = pl.DeviceIdType.MESH
ANY = pl.BlockSpec(memory_space=pl.ANY)
SIBLING_COLLECTIVE_ID = 0

ORDER = ("wg1", "wu1", "wd1", "wg2", "wu2", "wd2", "win", "wa", "wb", "wo")


class _Layout:
    def __init__(self):
        fs, dis, ds = F // NDEV, NG * D // NDEV, D // NDEV
        self.rows = dict(wg1=fs, wu1=fs, wd1=fs, wg2=fs, wu2=fs, wd2=fs, win=dis, wa=ds, wb=ds, wo=ds)
        self.fl, off = {}, 0
        for n in ORDER:
            self.fl[n] = off
            off += self.rows[n]
        self.RT = off


class _Stage:
    def __init__(self, names):
        lay = _Layout()
        self.names = names
        self.rows, self.full, self.sub, self.fl = {}, {}, {}, {}
        for n in names:
            base, i, k = (n.split("/") + ["0", "1"])[:3]
            self.full[n] = lay.rows[base]
            self.rows[n] = lay.rows[base] // int(k)
            self.sub[n] = int(i) * self.rows[n]
            self.fl[n] = lay.fl[base] + self.sub[n]
        self.off, self.wc, o, w = {}, {}, 0, 0
        for n in names:
            self.off[n], self.wc[n] = o, w
            o += self.rows[n]
            w += NDEV * self.rows[n]
        self.R, self.W = o, w

    def grad_row(self, n, first, dev_lin):
        return first + dev_lin * self.full[n] + self.sub[n]


def _nt(a, b):
    return lax.dot_general(a, b, (((1,), (1,)), ((), ())), preferred_element_type=F32)


def _nn(a, b):
    return lax.dot_general(a, b, (((1,), (0,)), ((), ())), preferred_element_type=F32)


def _tn(a, b):
    return lax.dot_general(a, b, (((0,), (0,)), ((), ())), preferred_element_type=F32)


def _sig(x):
    return 1.0 / (1.0 + jnp.exp(-x))


def _position():
    return lax.axis_index("x"), lax.axis_index("y"), lax.axis_index("c")


def _peer(pos, j):
    x, y, c = pos
    return (1 - x if j & 4 else x, 1 - y if j & 2 else y, 1 - c if j & 1 else c)


def _lin(pos):
    return 4 * pos[0] + 2 * pos[1] + pos[2]


def _chip(pos):
    return 2 * pos[0] + pos[1]


class _Comm:
    def __init__(self, inputs, out_shapes, scratch, start, finish, middle=None, sibling_only=False):
        self.inputs, self.out_shapes, self.scratch = inputs, out_shapes, scratch
        self.start, self.finish, self.middle = start, finish, middle
        self.sibling_only = sibling_only


def _call(body, *, name, grid, args, in_specs, out_shape, out_specs, scratch_shapes=(), comm=None,
          num_scalar_prefetch=0, after=None):
    in_specs, out_shape, out_specs, scratch_shapes = list(in_specs), list(out_shape), list(out_specs), list(scratch_shapes)
    if after is not None:
        inner, pos = body, num_scalar_prefetch + len(in_specs)
        body = lambda *refs: inner(*refs[:pos], *refs[pos + 1:])
        args, in_specs = list(args) + [after], in_specs + [ANY]
    n_in, n_out, n_scr = len(in_specs), len(out_shape), len(scratch_shapes)
    sp = num_scalar_prefetch
    if comm is None:
        kernel_fn = lambda *refs: body(*refs)
        c_in = c_out = c_scr = 0
    else:
        c_in, c_out, c_scr = len(comm.inputs), len(comm.out_shapes), len(comm.scratch)

        def kernel_fn(*refs):
            pre, refs = refs[:sp], refs[sp:]
            ins, cins = refs[:n_in], refs[n_in:n_in + c_in]
            o0 = n_in + c_in
            outs, couts = refs[o0:o0 + n_out], refs[o0 + n_out:o0 + n_out + c_out]
            s0 = o0 + n_out + c_out
            scr, cscr = refs[s0:s0 + n_scr], refs[s0 + n_scr:]
            step, steps = pl.program_id(0), grid[0]
            for a in range(1, len(grid)):
                step, steps = step * grid[a] + pl.program_id(a), steps * grid[a]
            first, last = step == 0, step == steps - 1

            @pl.when(first)
            def _():
                if comm.sibling_only:
                    barrier = pltpu.get_barrier_semaphore()
                    pl.semaphore_signal(barrier, inc=1, device_id=_peer(_position(), 1), device_id_type=MESH)
                    pl.semaphore_wait(barrier, 1)
                comm.start(cins, couts, cscr)

            if comm.middle is not None:
                @pl.when(step == (steps // 2 if steps > 2 else steps - 1))
                def _():
                    comm.middle(cins, couts, cscr)

            body(*pre, *ins, *outs, *scr)

            @pl.when(last)
            def _():
                comm.finish(cins, couts, cscr)

        args = list(args) + list(comm.inputs)
        in_specs += [ANY] * c_in
        out_shape += list(comm.out_shapes)
        out_specs += [ANY] * c_out
        scratch_shapes += list(comm.scratch)
    params = pltpu.CompilerParams(dimension_semantics=("arbitrary",) * len(grid), vmem_limit_bytes=VMEM_LIMIT,
                                  collective_id=SIBLING_COLLECTIVE_ID if comm is not None and comm.sibling_only else None)
    if sp:
        grid_spec = pltpu.PrefetchScalarGridSpec(num_scalar_prefetch=sp, grid=grid, in_specs=in_specs,
                                                 out_specs=out_specs, scratch_shapes=scratch_shapes)
        return pl.pallas_call(kernel_fn, name=name, grid_spec=grid_spec, out_shape=out_shape,
                              compiler_params=params)(*args)
    return pl.pallas_call(kernel_fn, name=name, grid=grid, in_specs=in_specs, out_shape=out_shape, out_specs=out_specs,
                          scratch_shapes=scratch_shapes, compiler_params=params)(*args)


def _join(a, b):
    na = (len(a.inputs), len(a.out_shapes), len(a.scratch))

    def split(refs):
        return ([r[:n] for r, n in zip(refs, na)], [r[n:] for r, n in zip(refs, na)])

    def start(*refs):
        ra, rb = split(refs)
        a.start(*ra)
        b.start(*rb)

    def finish(*refs):
        ra, rb = split(refs)
        a.finish(*ra)
        b.finish(*rb)

    def middle(*refs):
        for stage, r in zip((a, b), split(refs)):
            if stage.middle is not None:
                stage.middle(*r)

    return _Comm(list(a.inputs) + list(b.inputs), list(a.out_shapes) + list(b.out_shapes),
                 list(a.scratch) + list(b.scratch), start, finish,
                 middle if (a.middle is not None or b.middle is not None) else None,
                 sibling_only=a.sibling_only and b.sibling_only)


def _run_comm(comm, name):
    def body(*refs):
        c_in, c_out = len(comm.inputs), len(comm.out_shapes)
        parts = (refs[:c_in], refs[c_in:c_in + c_out], refs[c_in + c_out:])
        comm.start(*parts)
        if comm.middle is not None:
            comm.middle(*parts)
        comm.finish(*parts)

    return pl.pallas_call(
        body, name=name, out_shape=list(comm.out_shapes), in_specs=[ANY] * len(comm.inputs),
        out_specs=[ANY] * len(comm.out_shapes), scratch_shapes=list(comm.scratch))(*comm.inputs)


HBM = pl.BlockSpec(memory_space=pltpu.HBM)
SEM = pl.BlockSpec(memory_space=pltpu.SEMAPHORE)
DATAFLOW = pltpu.SideEffectType.DATAFLOW_SIDE_EFFECTING


def _comm_start(comm, name):
    c_in, c_out = len(comm.inputs), len(comm.out_shapes)
    sems = [s(()) if s is pltpu.SemaphoreType.DMA else s for s in comm.scratch]
    bufs = list(comm.inputs) + [lax.empty(s.shape, s.dtype) for s in comm.out_shapes]

    def body(*refs):
        sem_refs = refs[c_in + c_out:c_in + c_out + len(sems)]
        comm.start(refs[:c_in], refs[c_in:c_in + c_out], sem_refs)
        refs[-1][...] = jnp.zeros_like(refs[-1])

    outs = pl.pallas_call(
        body, name=name,
        out_shape=sems + [pltpu.HBM(b.shape, b.dtype) for b in bufs] + [jax.ShapeDtypeStruct((8, LANE), F32)],
        in_specs=[HBM] * len(bufs),
        out_specs=[SEM] * len(sems) + [HBM] * len(bufs) + [pl.BlockSpec(memory_space=pltpu.VMEM)],
        input_output_aliases={i: len(sems) + i for i in range(len(bufs))},
        compiler_params=pltpu.CompilerParams(has_side_effects=DATAFLOW),
    )(*[pltpu.with_memory_space_constraint(b, pltpu.HBM) for b in bufs])
    return outs[:len(sems)], outs[len(sems):-1], outs[-1]


def _comm_wait(comm, name, sems, bufs, after):
    c_in, c_out = len(comm.inputs), len(comm.out_shapes)

    def body(*refs):
        sem_refs = refs[c_in + c_out:c_in + c_out + len(sems)]
        comm.finish(refs[:c_in], refs[c_in:c_in + c_out], sem_refs)

    outs = pl.pallas_call(
        body, name=name, out_shape=[pltpu.HBM(b.shape, b.dtype) for b in bufs],
        in_specs=[HBM] * len(bufs) + [SEM] * len(sems) + [ANY], out_specs=[HBM] * len(bufs),
        input_output_aliases={i: i for i in range(len(bufs))},
        compiler_params=pltpu.CompilerParams(has_side_effects=DATAFLOW),
    )(*bufs, *sems, after)
    return outs[:c_in], outs[c_in:]


def _ag_comm(names, flat):
    st = _Stage(names)

    def ring(me):
        x, y, c = me
        diagonal = x == y
        up = (jnp.where(diagonal, x, 1 - x), jnp.where(diagonal, 1 - y, y), c)
        down = (jnp.where(diagonal, 1 - x, x), jnp.where(diagonal, y, 1 - y), c)
        low = c == 0
        passed = tuple(jnp.where(low, d, u) for d, u in zip(down, up))
        target = tuple(jnp.where(low, u, d) for d, u in zip(down, up))
        return up, down, (1 - x, 1 - y, c), passed, target

    def parts(refs):
        (flat_ref,), (out_ref,), (send_sems, recv_sems, local_sem) = refs
        me = _position()

        def region(name, dev):
            r = st.rows[name]
            return out_ref.at[pl.ds(st.wc[name] + _lin(dev) * r, r), :]

        def own(name):
            return flat_ref.at[pl.ds(st.fl[name], st.rows[name]), :]

        def copies(k, dev, to, from_flat):
            return [pltpu.make_async_remote_copy(
                src_ref=own(n) if from_flat else region(n, dev), dst_ref=region(n, dev), send_sem=send_sems.at[k],
                recv_sem=recv_sems.at[k], device_id=to, device_id_type=MESH) for n in names]

        def whole(k):
            return pltpu.make_async_remote_copy(
                src_ref=flat_ref.at[pl.ds(0, st.R), :], dst_ref=out_ref.at[pl.ds(0, st.R), :],
                send_sem=send_sems.at[k], recv_sem=recv_sems.at[k], device_id=me, device_id_type=MESH)

        return me, region, own, copies, whole, flat_ref, out_ref, local_sem

    def start(*refs):
        me, region, own, copies, _, _, _, local_sem = parts(refs)
        for n in names:
            pltpu.make_async_copy(own(n), region(n, me), local_sem).start()
        up, down, _, _, _ = ring(me)
        for k, to in ((1, up), (2, down), (0, _peer(me, 1))):
            for cp in copies(k, me, to, True):
                cp.start()

    def middle(*refs):
        me, _, _, copies, whole, _, _, _ = parts(refs)
        up, down, _, passed, target = ring(me)
        sib = _peer(me, 1)
        whole(1).wait_recv()
        whole(2).wait_recv()
        for k, dev, to in ((3, passed, target), (4, down, sib), (5, up, sib)):
            for cp in copies(k, dev, to, False):
                cp.start()

    def finish(*refs):
        me, _, _, copies, whole, flat_ref, out_ref, local_sem = parts(refs)
        _, _, across, _, _ = ring(me)
        whole(3).wait_recv()
        for cp in copies(6, across, _peer(me, 1), False):
            cp.start()
        whole(0).wait_recv()
        for j in range(3):
            whole(4 + j).wait_recv()
        for k in range(7):
            whole(k).wait_send()
        pltpu.make_async_copy(flat_ref.at[pl.ds(0, st.R), :], out_ref.at[pl.ds(0, st.R), :], local_sem).wait()

    return _Comm([flat], [jax.ShapeDtypeStruct((st.W, D), BF)],
                 [pltpu.SemaphoreType.DMA((7,)), pltpu.SemaphoreType.DMA((7,)), pltpu.SemaphoreType.DMA],
                 start, finish, middle)


def _rs_pair_comm(names, src):
    st = _Stage(names)
    arrays = []
    for n in names:
        if not any(src[n][0] is a for a in arrays):
            arrays.append(src[n][0])
    idx = {n: [i for i, a in enumerate(arrays) if a is src[n][0]][0] for n in names}

    def slot_wait(refs):
        recv = refs[1][0]
        send_sem, recv_sem = refs[2]
        return pltpu.make_async_remote_copy(src_ref=recv, dst_ref=recv, send_sem=send_sem, recv_sem=recv_sem,
                                            device_id=_position(), device_id_type=MESH)

    def start(*refs):
        ins, (recv,), (send_sem, recv_sem) = refs
        me = _position()
        sib = _peer(me, 1)
        for q in range(NCHIP):
            dev = (q // 2, q % 2, sib[2])
            for n in names:
                r = st.rows[n]
                pltpu.make_async_remote_copy(
                    src_ref=ins[idx[n]].at[pl.ds(st.grad_row(n, src[n][1], _lin(dev)), r), :],
                    dst_ref=recv.at[q, pl.ds(st.off[n], r), :], send_sem=send_sem, recv_sem=recv_sem,
                    device_id=sib, device_id_type=MESH).start()

    def finish(*refs):
        w = slot_wait(refs)
        w.wait_recv()
        w.wait_send()

    return _Comm(arrays, [jax.ShapeDtypeStruct((NCHIP, st.R, D), BF)],
                 [pltpu.SemaphoreType.DMA, pltpu.SemaphoreType.DMA], start, finish, sibling_only=True)


def _pair_add(names, src, recv, name):
    st = _Stage(names)
    c_arr = jnp.reshape(lax.axis_index("c"), (1,)).astype(jnp.int32)

    def body(c_ref, *refs):
        r_ref, o_ref = refs[len(names)], refs[len(names) + 1]
        for a_ref, n in zip(refs, names):
            rows = slice(st.off[n], st.off[n] + st.rows[n])
            o_ref[rows, :] = (a_ref[...].astype(F32) + r_ref[rows, :].astype(F32)).astype(BF)

    def shard_spec(n):
        r = st.rows[n]
        base, step = st.grad_row(n, src[n][1], 0) // r, st.full[n] // r
        return pl.BlockSpec((r, D), lambda q, c_ref: (base + step * (2 * q + c_ref[0]), 0))

    slot = pl.BlockSpec((None, st.R, D), lambda q, c_ref: (q, 0, 0))
    return _call(body, name=name, grid=(NCHIP,), args=[c_arr] + [src[n][0] for n in names] + [recv],
                 in_specs=[shard_spec(n) for n in names] + [slot],
                 out_shape=[jax.ShapeDtypeStruct((NCHIP, st.R, D), BF)], out_specs=[slot], num_scalar_prefetch=1)[0]


def _rs_chip_comm(part):
    def copies(refs):
        (p_ref,), (recv,), (send_sems, recv_sems, local_sem) = refs
        me = _position()
        mine = pltpu.make_async_copy(p_ref.at[_chip(me)], recv.at[_chip(me)], local_sem)
        out = []
        for j, bits in enumerate((4, 2, 6)):
            to = _peer(me, bits)
            out.append(pltpu.make_async_remote_copy(
                src_ref=p_ref.at[_chip(to)], dst_ref=recv.at[_chip(me)], send_sem=send_sems.at[j],
                recv_sem=recv_sems.at[j], device_id=to, device_id_type=MESH))
        return mine, out

    def start(*refs):
        mine, out = copies(refs)
        mine.start()
        for cp in out:
            cp.start()

    def finish(*refs):
        mine, out = copies(refs)
        for cp in out:
            cp.wait_recv()
        for cp in out:
            cp.wait_send()
        mine.wait()

    return _Comm([part], [jax.ShapeDtypeStruct(part.shape, BF)],
                 [pltpu.SemaphoreType.DMA((3,)), pltpu.SemaphoreType.DMA((3,)), pltpu.SemaphoreType.DMA],
                 start, finish)


def _direct_comm(x, scatter):
    def copies(refs):
        (x_ref,), (out_ref,), (send_sems, recv_sems, local_sem) = refs
        me = _position()

        def piece(dev):
            return x_ref.at[_lin(dev)] if scatter else x_ref

        mine = pltpu.make_async_copy(piece(me), out_ref.at[_lin(me)], local_sem)
        return mine, [pltpu.make_async_remote_copy(
            src_ref=piece(_peer(me, j)), dst_ref=out_ref.at[_lin(me)], send_sem=send_sems.at[j - 1],
            recv_sem=recv_sems.at[j - 1], device_id=_peer(me, j), device_id_type=MESH) for j in range(1, NDEV)]

    def start(*refs):
        mine, cps = copies(refs)
        mine.start()
        for cp in cps:
            cp.start()

    def finish(*refs):
        mine, cps = copies(refs)
        for cp in cps:
            cp.wait_recv()
        for cp in cps:
            cp.wait_send()
        mine.wait()

    shape = x.shape if scatter else (NDEV,) + x.shape
    return _Comm([x], [jax.ShapeDtypeStruct(shape, x.dtype)],
                 [pltpu.SemaphoreType.DMA((7,)), pltpu.SemaphoreType.DMA((7,)), pltpu.SemaphoreType.DMA],
                 start, finish)


def _pack_weights(shards):
    lay = _Layout()

    def body(*refs):
        o_ref = refs[-1]
        for ref, n in zip(refs, ORDER):
            x = ref[...].T if n == "win" else ref[...]
            o_ref[lay.fl[n]:lay.fl[n] + lay.rows[n], :] = x.astype(BF)

    return pl.pallas_call(
        body, name="pack_weights", out_shape=jax.ShapeDtypeStruct((lay.RT, D), BF),
        compiler_params=pltpu.CompilerParams(vmem_limit_bytes=VMEM_LIMIT))(*[shards[n] for n in ORDER])


def _load_ffn_weights(srcs, offs, scratch, sem):
    @pl.when(pl.program_id(0) == 0)
    def _():
        cps = [pltpu.make_async_copy(s.at[pl.ds(off, dst.shape[0]), :], dst, sem.at[i])
               for i, (s, off, dst) in enumerate(zip(srcs, offs, scratch))]
        for cp in cps:
            cp.start()
        for cp in cps:
            cp.wait()


def _final_loss_tile(xf, g, tgt, s_ref):
    r = lax.rsqrt(jnp.mean(xf * xf, axis=-1, keepdims=True) + EPS)
    xr = xf * r
    e = xr * g - tgt
    s_ref[1:2, :] += jnp.sum(e * e, axis=0, keepdims=True) * (0.5 / D)
    dy = e * (1.0 / D)
    s_ref[0:1, :] += jnp.sum(dy * xr, axis=0, keepdims=True)
    gdy = dy * g
    return r * gdy - xr * (r * jnp.mean(gdy * xr, axis=-1, keepdims=True))


def _ffn_fwd(x, g, wbufs, offs, name, comm=None, final=None):
    nf = F // FC

    def body(x_ref, g_ref, b0, b1, b2, *rest):
        if final is None:
            h_ref, n_ref, gg_ref, uu_ref, a_ref, wg_s, wu_s, wd_s, sem = rest
        else:
            gf_ref, t_ref, dh_ref, dhb_ref, s_ref, n_ref, gg_ref, uu_ref, a_ref, wg_s, wu_s, wd_s, sem = rest

            @pl.when(pl.program_id(0) == 0)
            def _():
                s_ref[...] = jnp.zeros_like(s_ref)

        _load_ffn_weights((b0, b1, b2), offs, (wg_s, wu_s, wd_s), sem)
        xf = x_ref[...]
        r = lax.rsqrt(jnp.mean(xf * xf, axis=-1, keepdims=True) + EPS)
        nb = (xf * r * g_ref[...]).astype(BF)
        n_ref[...] = nb
        acc = jnp.zeros((TM, D), F32)
        for c in range(nf):
            sl = slice(c * FC, (c + 1) * FC)
            gb = _nt(nb, wg_s[sl, :]).astype(BF)
            ub = _nt(nb, wu_s[sl, :]).astype(BF)
            gg_ref[:, sl] = gb
            uu_ref[:, sl] = ub
            a = (gb * _sig(gb)) * ub
            a_ref[0, :, sl] = a
            acc = acc + _nn(a, wd_s[sl, :])
        h = xf + 0.5 * acc
        if final is None:
            h_ref[...] = h
        else:
            dh = _final_loss_tile(h, gf_ref[...], t_ref[...], s_ref)
            dh_ref[...] = dh
            dhb_ref[...] = (0.5 * dh).astype(BF)

    row = lambda i: (i, 0)
    vec = pl.BlockSpec((1, D), lambda i: (0, 0))
    tile = pl.BlockSpec((TM, D), row)
    saved_shapes = [jax.ShapeDtypeStruct((T, D), BF), jax.ShapeDtypeStruct((T, F), BF), jax.ShapeDtypeStruct((T, F), BF),
                    jax.ShapeDtypeStruct((1, T, F), BF)]
    saved_specs = [tile, pl.BlockSpec((TM, F), row), pl.BlockSpec((TM, F), row),
                   pl.BlockSpec((1, TM, F), lambda i: (0, i, 0))]
    if final is None:
        extra_args, extra_specs = [], []
        head_shapes, head_specs = [jax.ShapeDtypeStruct((T, D), F32)], [tile]
    else:
        extra_args, extra_specs = list(final), [vec, tile]
        head_shapes = [jax.ShapeDtypeStruct((T, D), F32), jax.ShapeDtypeStruct((T, D), BF), jax.ShapeDtypeStruct((8, D), F32)]
        head_specs = [tile, tile, pl.BlockSpec((8, D), lambda i: (0, 0))]
    return _call(
        body, name=name, grid=(T // TM,), args=[x, g, *wbufs, *extra_args], comm=comm,
        in_specs=[tile, vec, ANY, ANY, ANY] + extra_specs,
        out_shape=head_shapes + saved_shapes, out_specs=head_specs + saved_specs,
        scratch_shapes=[pltpu.VMEM((F, D), BF)] * 3 + [pltpu.SemaphoreType.DMA((3,))])


def _ffn_gate_up(x, g, wbufs, offs, name, comm=None):
    nf = F // FC

    def body(x_ref, g_ref, b0, b1, n_ref, gg_ref, uu_ref, a_ref, wg_s, wu_s, sem):
        _load_ffn_weights((b0, b1), offs, (wg_s, wu_s), sem)
        xf = x_ref[...]
        r = lax.rsqrt(jnp.mean(xf * xf, axis=-1, keepdims=True) + EPS)
        nb = (xf * r * g_ref[...]).astype(BF)
        n_ref[...] = nb
        for c in range(nf):
            sl = slice(c * FC, (c + 1) * FC)
            gb = _nt(nb, wg_s[sl, :]).astype(BF)
            ub = _nt(nb, wu_s[sl, :]).astype(BF)
            gg_ref[:, sl] = gb
            uu_ref[:, sl] = ub
            a_ref[0, :, sl] = (gb * _sig(gb)) * ub

    row = lambda i: (i, 0)
    tile = pl.BlockSpec((TM, D), row)
    return _call(
        body, name=name, grid=(T // TM,), args=[x, g, *wbufs], comm=comm,
        in_specs=[tile, pl.BlockSpec((1, D), lambda i: (0, 0)), ANY, ANY],
        out_shape=[jax.ShapeDtypeStruct((T, D), BF), jax.ShapeDtypeStruct((T, F), BF), jax.ShapeDtypeStruct((T, F), BF),
                   jax.ShapeDtypeStruct((1, T, F), BF)],
        out_specs=[tile, pl.BlockSpec((TM, F), row), pl.BlockSpec((TM, F), row),
                   pl.BlockSpec((1, TM, F), lambda i: (0, i, 0))],
        scratch_shapes=[pltpu.VMEM((F, D), BF)] * 2 + [pltpu.SemaphoreType.DMA((2,))])


def _ffn_down(x, act, wbuf, off, name, comm=None):
    def body(x_ref, a_ref, b0, h_ref, wd_s, sem):
        _load_ffn_weights((b0,), (off,), (wd_s,), sem)
        h_ref[...] = x_ref[...] + 0.5 * _nn(a_ref[0], wd_s[...])

    tile = pl.BlockSpec((TM, D), lambda i: (i, 0))
    return _call(
        body, name=name, grid=(T // TM,), args=[x, act, wbuf], comm=comm,
        in_specs=[tile, pl.BlockSpec((1, TM, F), lambda i: (0, i, 0)), ANY],
        out_shape=[jax.ShapeDtypeStruct((T, D), F32)], out_specs=[tile],
        scratch_shapes=[pltpu.VMEM((F, D), BF), pltpu.SemaphoreType.DMA((1,))])


def _load_in_proj(parts, w_s, sem):
    @pl.when(pl.program_id(0) == 0)
    def _():
        shard = NG * D // NDEV
        rows = shard // len(parts)
        cps = [pltpu.make_async_copy(buf.at[pl.ds(first + k * rows, rows), :],
                                     w_s.at[pl.ds(k * shard + p * rows, rows), :], sem.at[p * NDEV + k])
               for p, (buf, first) in enumerate(parts) for k in range(NDEV)]
        for cp in cps:
            cp.start()
        for cp in cps:
            cp.wait()


def _mix_in(h1, gm, win, comm=None):
    def body(h_ref, g_ref, *rest):
        w_any, (u_ref, z_ref, w_s, sem) = rest[:len(win)], rest[len(win):]
        _load_in_proj([(b, first) for b, (_, first) in zip(w_any, win)], w_s, sem)
        xf = h_ref[...]
        r = lax.rsqrt(jnp.mean(xf * xf, axis=-1, keepdims=True) + EPS)
        ub = (xf * r * g_ref[...]).astype(BF)
        u_ref[...] = ub
        for j in range(NG):
            z_ref[j] = _nt(ub, w_s[j * D:(j + 1) * D, :]).astype(BF)

    row = lambda i: (i, 0)
    return _call(
        body, name="mix_in", grid=(T // TM,), args=[h1, gm] + [b for b, _ in win], comm=comm,
        in_specs=[pl.BlockSpec((TM, D), row), pl.BlockSpec((1, D), lambda i: (0, 0))] + [ANY] * len(win),
        out_shape=[jax.ShapeDtypeStruct((T, D), BF), jax.ShapeDtypeStruct((NG, T, D), BF)],
        out_specs=[pl.BlockSpec((TM, D), row), pl.BlockSpec((NG, TM, D), lambda i: (0, i, 0))],
        scratch_shapes=[pltpu.VMEM((NG * D, D), BF), pltpu.SemaphoreType.DMA((NDEV * len(win),))])


def _shift_up(w, b):
    return w if b == 0 else pltpu.roll(w, w.shape[0] - b, 0)


def _fold8(p):
    red = p[0:8, :]
    for i in range(1, p.shape[0] // 8):
        red = red + p[8 * i:8 * i + 8, :]
    return red


def _dft_constants():
    import numpy as np
    nh = NB // 2
    f, n = np.arange(nh)[:, None], np.arange(NB)[None, :]
    ang = 2.0 * np.pi / NB * f * n
    fc = np.cos(ang)
    fs = np.where(f == 0, (-1.0) ** n, np.sin(ang))
    scale = np.where(f == 0, 1.0, 2.0) / NB
    ic = (scale * np.cos(ang)).T
    isn = np.where(f == 0, (-1.0) ** n / NB, scale * np.sin(ang)).T
    d = (KA - 1 - np.arange(32))[None, :]
    valid = (np.arange(32) < KA)[None, :]
    angk = 2.0 * np.pi / NB * f * d
    kc = np.where(valid, np.cos(angk), 0.0)
    ks = np.where(valid, np.sin(angk), 0.0)
    k2 = np.where(valid, np.where(f == 0, (-1.0) ** d, np.cos(angk)), 0.0)
    rtc = np.where(valid, scale * np.cos(angk), 0.0).T
    rts = np.where(valid, np.where(f == 0, (-1.0) ** d / NB, scale * np.sin(angk)), 0.0).T

    def bf(a):
        return jnp.asarray(a, F32).astype(BF)

    def split(a):
        hi = bf(a)
        return hi, (jnp.asarray(a, F32) - hi.astype(F32)).astype(BF)

    return dict(fc=bf(fc), fs=bf(fs), ic_hi=bf(ic[HB:]), is_hi=bf(isn[HB:]), ic_lo=bf(ic[:HB]), is_lo=bf(isn[:HB]),
                kc=split(kc), ks=split(ks), k2=split(k2), rtc=split(rtc), rts=split(rts))


def _dot3(m_hi, m_lo, x):
    x_hi = x.astype(BF)
    x_lo = (x - x_hi.astype(F32)).astype(BF)
    return _nn(m_hi, x_hi) + _nn(m_hi, x_lo) + _nn(m_lo, x_hi)


def _whole(a):
    return pl.BlockSpec(a.shape, lambda c, t: (0,) * a.ndim)


def _filter_spectrum(cw_ref, tabs, hc, hs, h2):
    w32 = cw_ref[0:32, :]
    for (hi, lo), dst in zip(tabs, (hc, hs, h2)):
        dst[...] = _dot3(hi[...], lo[...], w32)


def _conv_fwd_dft(z, cw, bias, dft, comm=None):
    nt = T // TB
    hb = TB // HB

    def body(z_ref, zh_ref, cw_ref, b_ref, fc_ref, fs_ref, ic_ref, is_ref, kch, kcl, ksh, ksl, k2h, k2l,
             a1_ref, q_ref, aext, ppad, hc, hs, h2):
        first = pl.program_id(1) == 0
        f = lambda ref, j: ref[j].astype(F32)

        @pl.when(first)
        def _():
            _filter_spectrum(cw_ref, ((kch, kcl), (ksh, ksl), (k2h, k2l)), hc, hs, h2)

        aext[0:HB, :] = jnp.where(first, 0.0, f(zh_ref, 0) * _sig(f(zh_ref, 1))).astype(BF)
        aext[HB:, :] = (f(z_ref, 0) * _sig(f(z_ref, 1))).astype(BF)
        ppad[0:8, :] = jnp.where(first, 0.0, f(zh_ref, 3)[HB - 8:HB, :] * f(zh_ref, 4)[HB - 8:HB, :])
        ppad[8:, :] = f(z_ref, 3) * f(z_ref, 4)
        bias_row = b_ref[...]

        for j in range(TB // HB):
            xs = aext[j * HB:j * HB + NB, :]
            xa, xb = _nn(fc_ref[...], xs), _nn(fs_ref[...], xs)
            yc = (hc[...] * xa - hs[...] * xb).astype(BF)
            ys = (h2[...] * xb + hs[...] * xa).astype(BF)
            y = _nn(ic_ref[...], yc) + _nn(is_ref[...], ys)
            a1_ref[j * HB:(j + 1) * HB, :] = (y + bias_row).astype(BF)

        def chunk(r, carry):
            base = pl.multiple_of(r * CHB, CHB)
            pw = ppad[pl.ds(base, CHB + 8), :]
            v = (cw_ref[pl.ds(32, 1), :] * _shift_up(pw, 6)[0:CHB, :]
                 + cw_ref[pl.ds(33, 1), :] * _shift_up(pw, 7)[0:CHB, :]
                 + cw_ref[pl.ds(34, 1), :] * pw[8:8 + CHB, :])
            q_ref[pl.ds(base, CHB), :] = (z_ref[2, pl.ds(base, CHB), :].astype(F32) * v).astype(BF)
            return carry

        lax.fori_loop(0, TB // CHB, chunk, 0)

    blk = pl.BlockSpec((TB, CW), lambda c, t: (t, c))
    tabs = [dft["fc"], dft["fs"], dft["ic_hi"], dft["is_hi"], *dft["kc"], *dft["ks"], *dft["k2"]]
    return _call(
        body, name="conv_fwd", grid=(D // CW, nt), comm=comm, args=[z, z, cw, bias] + tabs,
        in_specs=[pl.BlockSpec((5, TB, CW), lambda c, t: (0, t, c)),
                  pl.BlockSpec((5, HB, CW), lambda c, t: (0, jnp.maximum(t * hb - 1, 0), c)),
                  pl.BlockSpec((40, CW), lambda c, t: (0, c)), pl.BlockSpec((1, CW), lambda c, t: (0, c))]
                 + [_whole(a) for a in tabs],
        out_shape=[jax.ShapeDtypeStruct((T, D), BF), jax.ShapeDtypeStruct((T, D), BF)], out_specs=[blk, blk],
        scratch_shapes=[pltpu.VMEM((TB + HB, CW), BF), pltpu.VMEM((TB + 8, CW), F32)]
                       + [pltpu.VMEM((NB // 2, CW), F32)] * 3)


def _conv_bwd_dft(z, da1, dq, dzg, cw, dft, comm=None):
    nt = T // TB
    hb = TB // HB
    last_h = T // HB - 1

    def body(z_ref, zp_ref, zn_ref, da1_ref, da1n_ref, dq_ref, dqn_ref, dzg_ref, cw_ref,
             fc_ref, fs_ref, ic_ref, is_ref, kch, kcl, ksh, ksl, k2h, k2l, rch, rcl, rsh, rsl,
             dz_ref, dwa_ref, dwb_ref, aext, dyext, ppad, dvpad, hc, hs, h2, rc, rs, nyq, acc_b):
        t = pl.program_id(1)
        first, last = t == 0, t == nt - 1
        f = lambda ref, j: ref[j].astype(F32)

        @pl.when(first)
        def _():
            _filter_spectrum(cw_ref, ((kch, kcl), (ksh, ksl), (k2h, k2l)), hc, hs, h2)
            rc[...] = jnp.zeros_like(rc)
            rs[...] = jnp.zeros_like(rs)
            nyq[...] = jnp.zeros_like(nyq)
            acc_b[...] = jnp.zeros_like(acc_b)

        aext[0:HB, :] = jnp.where(first, 0.0, f(zp_ref, 0) * _sig(f(zp_ref, 1))).astype(BF)
        aext[HB:, :] = (f(z_ref, 0) * _sig(f(z_ref, 1))).astype(BF)
        dyext[0:TB, :] = da1_ref[...]
        dyext[TB:, :] = jnp.where(last, 0.0, da1n_ref[...].astype(F32)).astype(BF)
        ppad[0:8, :] = jnp.where(first, 0.0, f(zp_ref, 3)[HB - 8:HB, :] * f(zp_ref, 4)[HB - 8:HB, :])
        ppad[8:, :] = f(z_ref, 3) * f(z_ref, 4)
        dvpad[0:TB, :] = dq_ref[...].astype(F32) * f(z_ref, 2)
        dvpad[TB:, :] = jnp.where(last, 0.0, dqn_ref[...].astype(F32)[0:8, :] * f(zn_ref, 2)[0:8, :])

        for j in range(TB // HB):
            rows = slice(j * HB, (j + 1) * HB)
            dys = dyext[j * HB:j * HB + NB, :]
            da, db = _nn(fc_ref[...], dys), _nn(fs_ref[...], dys)
            gc = (hc[...] * da + hs[...] * db).astype(BF)
            gs = (h2[...] * db - hs[...] * da).astype(BF)
            da0 = _nn(ic_ref[...], gc) + _nn(is_ref[...], gs)
            z0, z1 = z_ref[0, rows, :].astype(F32), z_ref[1, rows, :].astype(F32)
            s1 = _sig(z1)
            dz_ref[0, rows, :] = (da0 * s1).astype(BF)
            dz_ref[1, rows, :] = (da0 * z0 * (s1 * (1.0 - s1))).astype(BF)
            xs = aext[j * HB:j * HB + NB, :]
            xa, xb = _nn(fc_ref[...], xs), _nn(fs_ref[...], xs)
            dyb = dyext[rows, :]
            pa, pb = _nn(fc_ref[:, HB:NB], dyb), _nn(fs_ref[:, HB:NB], dyb)
            rc[...] += pa * xa + pb * xb
            rs[...] += pb * xa - pa * xb
            nyq[...] += pb[0:8, :] * xb[0:8, :]

        def chunk(r, carry):
            base = pl.multiple_of(r * CHB, CHB)
            rows = pl.ds(base, CHB)
            pw = ppad[pl.ds(base, CHB + 8), :]
            p6 = _shift_up(pw, 6)[0:CHB, :]
            p7 = _shift_up(pw, 7)[0:CHB, :]
            p8 = pw[8:8 + CHB, :]
            wb0, wb1, wb2 = cw_ref[pl.ds(32, 1), :], cw_ref[pl.ds(33, 1), :], cw_ref[pl.ds(34, 1), :]
            v = wb0 * p6 + wb1 * p7 + wb2 * p8
            dz_ref[2, rows, :] = (dq_ref[rows, :].astype(F32) * v).astype(BF)
            dvw = dvpad[pl.ds(base, CHB + 8), :]
            dvc = dvw[0:CHB, :]
            dp = wb2 * dvc + wb1 * _shift_up(dvw, 1)[0:CHB, :] + wb0 * _shift_up(dvw, 2)[0:CHB, :]
            dz_ref[3, rows, :] = (dp * z_ref[4, rows, :].astype(F32)).astype(BF)
            dz_ref[4, rows, :] = (dp * z_ref[3, rows, :].astype(F32)).astype(BF)
            acc_b[0:8, :] += _fold8(dvc * p6)
            acc_b[8:16, :] += _fold8(dvc * p7)
            acc_b[16:24, :] += _fold8(dvc * p8)
            dz_ref[5, rows, :] = dzg_ref[0, rows, :]
            dz_ref[6, rows, :] = dzg_ref[1, rows, :]
            return carry

        lax.fori_loop(0, TB // CHB, chunk, 0)

        @pl.when(last)
        def _():
            row0 = lax.broadcasted_iota(jnp.int32, (NB // 2, CW), 0) == 0
            ny = jnp.broadcast_to(nyq[0:1, :], (NB // 2, CW))
            rcv = jnp.where(row0, rc[...] - ny, rc[...])
            rsv = jnp.where(row0, ny, rs[...])
            dwa_ref[...] = _dot3(rch[...], rcl[...], rcv) + _dot3(rsh[...], rsl[...], rsv)
            for k in range(KB):
                dwb_ref[k:k + 1, :] = jnp.sum(acc_b[8 * k:8 * k + 8, :], axis=0, keepdims=True)
            dwb_ref[KB:8, :] = jnp.zeros((8 - KB, CW), F32)

    blk = lambda c, t: (t, c)
    nxt = lambda c, t: (jnp.minimum((t + 1) * hb, last_h), c)
    tabs = [dft["fc"], dft["fs"], dft["ic_lo"], dft["is_lo"], *dft["kc"], *dft["ks"], *dft["k2"], *dft["rtc"], *dft["rts"]]
    return _call(
        body, name="conv_bwd", grid=(D // CW, nt), comm=comm, args=[z, z, z, da1, da1, dq, dq, dzg, cw] + tabs,
        in_specs=[pl.BlockSpec((5, TB, CW), lambda c, t: (0, t, c)),
                  pl.BlockSpec((5, HB, CW), lambda c, t: (0, jnp.maximum(t * hb - 1, 0), c)),
                  pl.BlockSpec((5, HB, CW), lambda c, t: (0, jnp.minimum((t + 1) * hb, last_h), c)),
                  pl.BlockSpec((TB, CW), blk), pl.BlockSpec((HB, CW), nxt),
                  pl.BlockSpec((TB, CW), blk), pl.BlockSpec((HB, CW), nxt),
                  pl.BlockSpec((2, TB, CW), lambda c, t: (0, t, c)),
                  pl.BlockSpec((40, CW), lambda c, t: (0, c))]
                 + [_whole(a) for a in tabs],
        out_shape=[jax.ShapeDtypeStruct((NG, T, D), BF), jax.ShapeDtypeStruct((32, D), F32),
                   jax.ShapeDtypeStruct((8, D), F32)],
        out_specs=[pl.BlockSpec((NG, TB, CW), lambda c, t: (0, t, c)),
                   pl.BlockSpec((32, CW), lambda c, t: (0, c)), pl.BlockSpec((8, CW), lambda c, t: (0, c))],
        scratch_shapes=[pltpu.VMEM((TB + HB, CW), BF), pltpu.VMEM((TB + HB, CW), BF),
                        pltpu.VMEM((TB + 8, CW), F32), pltpu.VMEM((TB + 8, CW), F32)]
                       + [pltpu.VMEM((NB // 2, CW), F32)] * 5 + [pltpu.VMEM((8, CW), F32), pltpu.VMEM((24, CW), F32)])


def _layernorm_silu(a1, lng, lnb):
    mu = jnp.mean(a1, axis=-1, keepdims=True)
    xc = a1 - mu
    rs = lax.rsqrt(jnp.mean(xc * xc, axis=-1, keepdims=True) + EPS)
    xh = xc * rs
    a2 = xh * lng + lnb
    sg = _sig(a2)
    return xh, rs, a2, sg


def _square_specs(blocks):
    return [pl.BlockSpec((D, D), lambda i, b=b: (b, 0)) for b in blocks]


def _mix_out(a1, q, z, h1, lng, lnb, wsq, comm=None):
    def body(a1_ref, q_ref, ga_ref, gb_ref, h_ref, lng_ref, lnb_ref, wa_ref, wb_ref, wo_ref, h2_ref, ya_ref, yb_ref):
        _, _, a2, sg = _layernorm_silu(a1_ref[...].astype(F32), lng_ref[...], lnb_ref[...])
        ya = _nn((a2 * sg).astype(BF), wa_ref[...])
        yb = _nn(q_ref[...], wb_ref[...])
        ya_ref[...] = ya.astype(BF)
        yb_ref[...] = yb.astype(BF)
        m = _sig(ga_ref[...].astype(F32)) * ya + _sig(gb_ref[...].astype(F32)) * yb
        h2_ref[...] = h_ref[...] + _nn(m.astype(BF), wo_ref[...])

    row = lambda i: (i, 0)
    vec = pl.BlockSpec((1, D), lambda i: (0, 0))
    return _call(
        body, name="mix_out", grid=(T // TM,), args=[a1, q, z, z, h1, lng, lnb, wsq, wsq, wsq], comm=comm,
        in_specs=[pl.BlockSpec((TM, D), row), pl.BlockSpec((TM, D), row),
                  pl.BlockSpec((None, TM, D), lambda i: (5, i, 0)), pl.BlockSpec((None, TM, D), lambda i: (6, i, 0)),
                  pl.BlockSpec((TM, D), row), vec, vec] + _square_specs((0, 1, 2)),
        out_shape=[jax.ShapeDtypeStruct((T, D), F32), jax.ShapeDtypeStruct((T, D), BF), jax.ShapeDtypeStruct((T, D), BF)],
        out_specs=[pl.BlockSpec((TM, D), row)] * 3)


def _rmsnorm_bwd(xf, g, dn):
    r = lax.rsqrt(jnp.mean(xf * xf, axis=-1, keepdims=True) + EPS)
    xr = xf * r
    gdn = dn * g
    dx = r * gdn - xr * (r * jnp.mean(gdn * xr, axis=-1, keepdims=True))
    return dx, jnp.sum(dn * xr, axis=0, keepdims=True)


def _ffn_bwd_hidden(dh, gg, uu, wbuf, off, name, comm=None):
    nf = F // FC

    def body(dh_ref, gg_ref, uu_ref, b0, dgu_ref, wd_s, sem):
        _load_ffn_weights((b0,), (off,), (wd_s,), sem)
        dhb = dh_ref[...]
        for c in range(nf):
            sl = slice(c * FC, (c + 1) * FC)
            da = _nt(dhb, wd_s[sl, :]).astype(BF)
            gb, ub = gg_ref[:, sl], uu_ref[:, sl]
            sg = _sig(gb)
            dgu_ref[0, :, sl] = (da * ub) * (sg * (1.0 + gb * (1.0 - sg)))
            dgu_ref[0, :, F + c * FC:F + (c + 1) * FC] = da * (gb * sg)

    row = lambda i: (i, 0)
    return _call(
        body, name=name, grid=(T // TM,), args=[dh, gg, uu, wbuf], comm=comm,
        in_specs=[pl.BlockSpec((TM, D), row), pl.BlockSpec((TM, F), row), pl.BlockSpec((TM, F), row), ANY],
        out_shape=[jax.ShapeDtypeStruct((1, T, 2 * F), BF)],
        out_specs=[pl.BlockSpec((1, TM, 2 * F), lambda i: (0, i, 0))],
        scratch_shapes=[pltpu.VMEM((F, D), BF), pltpu.SemaphoreType.DMA((1,))])


def _ffn_bwd_input(dgu, dh, x, g, wbufs, offs, name, comm=None, after=None):
    def body(dgu_ref, dh_ref, x_ref, g_ref, b0, b1, dx_ref, s_ref, w_s, sem):
        _load_ffn_weights((b0, b1), offs, (w_s.at[pl.ds(0, F), :], w_s.at[pl.ds(F, F), :]), sem)

        @pl.when(pl.program_id(0) == 0)
        def _():
            s_ref[...] = jnp.zeros_like(s_ref)

        dn = _nn(dgu_ref[0], w_s[...])
        dxn, dg = _rmsnorm_bwd(x_ref[...], g_ref[...], dn)
        dx_ref[...] = dh_ref[...] + dxn
        s_ref[0:1, :] += dg

    row = lambda i: (i, 0)
    return _call(
        body, name=name, grid=(T // TM,), args=[dgu, dh, x, g, *wbufs], comm=comm, after=after,
        in_specs=[pl.BlockSpec((1, TM, 2 * F), lambda i: (0, i, 0)), pl.BlockSpec((TM, D), row),
                  pl.BlockSpec((TM, D), row), pl.BlockSpec((1, D), lambda i: (0, 0)), ANY, ANY],
        out_shape=[jax.ShapeDtypeStruct((T, D), F32), jax.ShapeDtypeStruct((8, D), F32)],
        out_specs=[pl.BlockSpec((TM, D), row), pl.BlockSpec((8, D), lambda i: (0, 0))],
        scratch_shapes=[pltpu.VMEM((2 * F, D), BF), pltpu.SemaphoreType.DMA((2,))])


def _tn_matmul(lhs, rhs, tr, name, comm=None):
    ng, _, cdim = lhs.shape
    nc, nk = cdim // tr, T // TK
    if rhs.ndim == 2:
        r_spec = pl.BlockSpec((TK, D), lambda g, c, k: (k, 0))
    else:
        r_spec = pl.BlockSpec((None, TK, D), lambda g, c, k: (g, k, 0))

    def body(l_ref, r_ref, o_ref, acc):
        k = pl.program_id(2)

        @pl.when(k == 0)
        def _():
            acc[...] = jnp.zeros_like(acc)

        acc[...] += _tn(l_ref[...], r_ref[...])

        @pl.when(k == nk - 1)
        def _():
            o_ref[...] = acc[...].astype(BF)

    return _call(
        body, name=name, grid=(ng, nc, nk), args=[lhs, rhs], comm=comm,
        in_specs=[pl.BlockSpec((None, TK, tr), lambda g, c, k: (g, k, c)), r_spec],
        out_shape=[jax.ShapeDtypeStruct((ng * cdim, D), BF)],
        out_specs=[pl.BlockSpec((tr, D), lambda g, c, k: (g * nc + c, 0))],
        scratch_shapes=[pltpu.VMEM((tr, D), F32)])


def _mix_out_bwd(dh2, ya, yb, z, a1, q, lng, lnb, wsq, comm=None):
    def body(dh_ref, ya_ref, yb_ref, ga_ref, gb_ref, a1_ref, q_ref, lng_ref, lnb_ref, wa_ref, wb_ref, wo_ref,
             dzg_ref, da1_ref, dq_ref, l_ref, r_ref, s_ref):
        @pl.when(pl.program_id(0) == 0)
        def _():
            s_ref[...] = jnp.zeros_like(s_ref)

        dhb = dh_ref[...].astype(BF)
        dm = _nt(dhb, wo_ref[...]).astype(BF)
        ya, yb = ya_ref[...], yb_ref[...]
        sa, sb = _sig(ga_ref[...]), _sig(gb_ref[...])
        l_ref[0] = sa * ya + sb * yb
        l_ref[2] = q_ref[...]
        dzg_ref[0] = (dm * ya) * (sa * (1.0 - sa))
        dzg_ref[1] = (dm * yb) * (sb * (1.0 - sb))
        dya = dm * sa
        dyb = dm * sb
        r_ref[0] = dhb
        r_ref[1] = dya
        r_ref[2] = dyb
        dq_ref[...] = _nt(dyb, wb_ref[...]).astype(BF)
        da3 = _nt(dya, wa_ref[...])
        lng = lng_ref[...]
        xh, rs, a2, sg = _layernorm_silu(a1_ref[...].astype(F32), lng, lnb_ref[...])
        l_ref[1] = (a2 * sg).astype(BF)
        da2 = da3 * (sg * (1.0 + a2 * (1.0 - sg)))
        s_ref[0:1, :] += jnp.sum(da2 * xh, axis=0, keepdims=True)
        s_ref[1:2, :] += jnp.sum(da2, axis=0, keepdims=True)
        dxh = da2 * lng
        da1 = rs * (dxh - jnp.mean(dxh, axis=-1, keepdims=True) - xh * jnp.mean(dxh * xh, axis=-1, keepdims=True))
        da1_ref[...] = da1.astype(BF)
        s_ref[2:3, :] += jnp.sum(da1, axis=0, keepdims=True)

    row = lambda i: (i, 0)
    row3 = lambda i: (0, i, 0)
    vec = pl.BlockSpec((1, D), lambda i: (0, 0))
    return _call(
        body, name="mix_out_bwd", grid=(T // TM,), args=[dh2, ya, yb, z, z, a1, q, lng, lnb, wsq, wsq, wsq], comm=comm,
        in_specs=[pl.BlockSpec((TM, D), row), pl.BlockSpec((TM, D), row), pl.BlockSpec((TM, D), row),
                  pl.BlockSpec((None, TM, D), lambda i: (5, i, 0)), pl.BlockSpec((None, TM, D), lambda i: (6, i, 0)),
                  pl.BlockSpec((TM, D), row), pl.BlockSpec((TM, D), row), vec, vec] + _square_specs((0, 1, 2)),
        out_shape=[jax.ShapeDtypeStruct((2, T, D), BF), jax.ShapeDtypeStruct((T, D), BF),
                   jax.ShapeDtypeStruct((T, D), BF), jax.ShapeDtypeStruct((3, T, D), BF),
                   jax.ShapeDtypeStruct((3, T, D), BF), jax.ShapeDtypeStruct((8, D), F32)],
        out_specs=[pl.BlockSpec((2, TM, D), row3), pl.BlockSpec((TM, D), row), pl.BlockSpec((TM, D), row),
                   pl.BlockSpec((3, TM, D), row3), pl.BlockSpec((3, TM, D), row3), pl.BlockSpec((8, D), lambda i: (0, 0))])


def _mix_in_bwd(dz, dh2, h1, gm, win, comm=None):
    def body(dz_ref, dh_ref, h_ref, g_ref, *rest):
        w_any, (o_ref, ob_ref, s_ref, w_s, sem) = rest[:len(win)], rest[len(win):]
        _load_in_proj([(b, first) for b, (_, first) in zip(w_any, win)], w_s, sem)

        @pl.when(pl.program_id(0) == 0)
        def _():
            s_ref[...] = jnp.zeros_like(s_ref)

        du = _nn(dz_ref[0], w_s[0:D, :])
        for j in range(1, NG):
            du = du + _nn(dz_ref[j], w_s[j * D:(j + 1) * D, :])
        dx, dg = _rmsnorm_bwd(h_ref[...], g_ref[...], du)
        dh1 = dh_ref[...] + dx
        o_ref[...] = dh1
        ob_ref[...] = (0.5 * dh1).astype(BF)
        s_ref[0:1, :] += dg

    row = lambda i: (i, 0)
    return _call(
        body, name="mix_in_bwd", grid=(T // TM,), args=[dz, dh2, h1, gm] + [b for b, _ in win], comm=comm,
        in_specs=[pl.BlockSpec((NG, TM, D), lambda i: (0, i, 0)), pl.BlockSpec((TM, D), row),
                  pl.BlockSpec((TM, D), row), pl.BlockSpec((1, D), lambda i: (0, 0))] + [ANY] * len(win),
        out_shape=[jax.ShapeDtypeStruct((T, D), F32), jax.ShapeDtypeStruct((T, D), BF), jax.ShapeDtypeStruct((8, D), F32)],
        out_specs=[pl.BlockSpec((TM, D), row), pl.BlockSpec((TM, D), row), pl.BlockSpec((8, D), lambda i: (0, 0))],
        scratch_shapes=[pltpu.VMEM((NG * D, D), BF), pltpu.SemaphoreType.DMA((NDEV * len(win),))])


def _row_tile(n, want, mult):
    for t in range(min(want, n), 0, -1):
        if n % t == 0 and t % mult == 0:
            return t
    return n


def _pack_small(s_ffn1, s_in, s_mix, s_ffn2, s_final, dwa, dwb):
    def body(f1, mi, mo, f2, fl, wa_ref, wb_ref, v_ref, k_ref):
        for dst, (ref, row) in enumerate(((f1, 0), (mi, 0), (mo, 0), (mo, 1), (mo, 2), (f2, 0), (fl, 0), (fl, 1))):
            v_ref[dst:dst + 1, :] = ref[row:row + 1, :]
        for k in range(NDEV):
            k_ref[k, 0:32, :] = wa_ref[:, k * LANE:(k + 1) * LANE]
            k_ref[k, 32:40, :] = wb_ref[:, k * LANE:(k + 1) * LANE]

    return pl.pallas_call(
        body, name="pack_small",
        out_shape=(jax.ShapeDtypeStruct((8, D), F32), jax.ShapeDtypeStruct((NDEV, 40, LANE), F32)),
    )(s_ffn1, s_in, s_mix, s_ffn2, s_final, dwa, dwb)


def _adam_update(g, w, m, v):
    m2 = ADAM_B1 * m + (1.0 - ADAM_B1) * g
    v2 = ADAM_B2 * v + (1.0 - ADAM_B2) * (g * g)
    c1 = 1.0 - ADAM_B1 ** ADAM_STEP
    c2 = 1.0 - ADAM_B2 ** ADAM_STEP
    return -ADAM_LR * ((m2 / c1) / (jnp.sqrt(v2 / c2) + ADAM_EPS) + ADAM_WD * w), m2, v2


def _adam_small(vecs, convs, vec_params, tap_params):
    nv, nt = len(vec_params), len(tap_params)

    def body(*refs):
        v_ref, k_ref = refs[:2]
        p_refs = refs[2:2 + 3 * (nv + nt)]
        l_ref = refs[2 + 3 * (nv + nt)]
        o_refs = refs[3 + 3 * (nv + nt):]
        s, c = v_ref[0], k_ref[0]
        for k in range(1, NDEV):
            s = s + v_ref[k]
            c = c + k_ref[k]
        l_ref[...] = jnp.sum(s[7:8, :], axis=-1, keepdims=True)
        for i in range(nv):
            w_ref, m_ref, u_ref = p_refs[3 * i: 3 * i + 3]
            g_ref, d_ref, m2_ref, u2_ref = o_refs[4 * i: 4 * i + 4]
            g = s[i:i + 1, :]
            g_ref[...] = g
            d_ref[...], m2_ref[...], u2_ref[...] = _adam_update(g, w_ref[...], m_ref[...], u_ref[...])
        for i in range(nt):
            w_ref, m_ref, u_ref = p_refs[3 * (nv + i): 3 * (nv + i) + 3]
            g_ref, d_ref, m2_ref, u2_ref = o_refs[4 * (nv + i): 4 * (nv + i) + 4]
            first = tap_params[i][0]
            for k in range(w_ref.shape[0]):
                g = c[first + k:first + k + 1, :]
                g_ref[k] = g
                d_ref[k], m2_ref[k], u2_ref[k] = _adam_update(g, w_ref[k], m_ref[k], u_ref[k])

    params = [a for p in vec_params for a in p] + [a for p in tap_params for a in p[1:]]
    out_shape = [jax.ShapeDtypeStruct((1, 1), F32)]
    for p in list(vec_params) + [p[1:] for p in tap_params]:
        out_shape += [jax.ShapeDtypeStruct(p[0].shape, F32)] * 4
    outs = pl.pallas_call(body, name="adam_small", out_shape=tuple(out_shape))(vecs, convs, *params)
    groups = [tuple(outs[1 + 4 * i: 5 + 4 * i]) for i in range(nv + nt)]
    return outs[0], groups[:nv], groups[nv:]


def _adam_in_proj(parts, w, m, v, after):
    rows = w.shape[1]
    tr = _row_tile(D, 256, LANE)

    def body(*refs):
        p_refs = refs[:len(parts)]
        w_ref, m_ref, v_ref, g_ref, d_ref, m2_ref, v2_ref = refs[len(parts):]
        sums = []
        for p in p_refs:
            s = p[0].astype(F32)
            for k in range(1, p.shape[0]):
                s = s + p[k].astype(F32)
            sums.append(s)
        g = jnp.concatenate(sums, axis=0).T
        g_ref[...] = g
        d_ref[...], m2_ref[...], v2_ref[...] = _adam_update(g, w_ref[...], m_ref[...], v_ref[...])

    spec = pl.BlockSpec((tr, rows), lambda i: (i, 0))
    return _call(body, name="adam_in", grid=(D // tr,), args=list(parts) + [w, m, v], after=after,
                 in_specs=[pl.BlockSpec((p.shape[0], p.shape[1], tr), lambda i: (0, 0, i)) for p in parts] + [spec] * 3,
                 out_shape=[jax.ShapeDtypeStruct((D, rows), F32)] * 4, out_specs=[spec] * 4)


def _adam(gs, ws, ms, vs, name, after):
    n = len(gs)
    rows, cols = ws[0].shape
    tr = _row_tile(rows, min(256, rows // 2), 16)

    def body(*refs):
        for i in range(n):
            g_in, w, m, v = refs[4 * i], refs[4 * i + 1][...], refs[4 * i + 2][...], refs[4 * i + 3][...]
            g_ref, d_ref, m_ref, v_ref = refs[4 * n + 4 * i: 4 * n + 4 * i + 4]
            g = g_in[0].astype(F32)
            for k in range(1, g_in.shape[0]):
                g = g + g_in[k].astype(F32)
            g_ref[...] = g
            d_ref[...], m_ref[...], v_ref[...] = _adam_update(g, w, m, v)

    spec = pl.BlockSpec((tr, cols), lambda i: (i, 0))
    args, in_specs = [], []
    for i in range(n):
        slots, first = gs[i]
        args += [slots, ws[i], ms[i], vs[i]]
        in_specs += [pl.BlockSpec((slots.shape[0], tr, cols), lambda i, b=first // tr: (0, b + i, 0))] + [spec] * 3
    outs = _call(body, name=name, grid=(rows // tr,), args=args, in_specs=in_specs, after=after,
                 out_shape=[jax.ShapeDtypeStruct((rows, cols), F32)] * (4 * n), out_specs=[spec] * (4 * n))
    return [tuple(outs[4 * i: 4 * i + 4]) for i in range(n)]


def kernel(x, ffn1_norm, ffn1_w_gate, ffn1_w_up, ffn1_w_down, mix_norm, w_in, a_dw_w, a_dw_b, a_ln_g, a_ln_b, a_w_out, b_conv_w, b_w_out, w_o, ffn2_norm, ffn2_w_gate, ffn2_w_up, ffn2_w_down, final_norm, loss_target, m_ffn1_norm, m_ffn1_w_gate, m_ffn1_w_up, m_ffn1_w_down, m_mix_norm, m_w_in, m_a_dw_w, m_a_dw_b, m_a_ln_g, m_a_ln_b, m_a_w_out, m_b_conv_w, m_b_w_out, m_w_o, m_ffn2_norm, m_ffn2_w_gate, m_ffn2_w_up, m_ffn2_w_down, m_final_norm, v_ffn1_norm, v_ffn1_w_gate, v_ffn1_w_up, v_ffn1_w_down, v_mix_norm, v_w_in, v_a_dw_w, v_a_dw_b, v_a_ln_g, v_a_ln_b, v_a_w_out, v_b_conv_w, v_b_w_out, v_w_o, v_ffn2_norm, v_ffn2_w_gate, v_ffn2_w_up, v_ffn2_w_down, v_final_norm):
    names = ("ffn1_norm", "ffn1_w_gate", "ffn1_w_up", "ffn1_w_down", "mix_norm", "w_in", "a_dw_w", "a_dw_b",
             "a_ln_g", "a_ln_b", "a_w_out", "b_conv_w", "b_w_out", "w_o", "ffn2_norm", "ffn2_w_gate", "ffn2_w_up",
             "ffn2_w_down", "final_norm")
    w = dict(ffn1_norm=ffn1_norm, ffn1_w_gate=ffn1_w_gate, ffn1_w_up=ffn1_w_up, ffn1_w_down=ffn1_w_down,
             mix_norm=mix_norm, w_in=w_in, a_dw_w=a_dw_w, a_dw_b=a_dw_b, a_ln_g=a_ln_g, a_ln_b=a_ln_b,
             a_w_out=a_w_out, b_conv_w=b_conv_w, b_w_out=b_w_out, w_o=w_o, ffn2_norm=ffn2_norm,
             ffn2_w_gate=ffn2_w_gate, ffn2_w_up=ffn2_w_up, ffn2_w_down=ffn2_w_down, final_norm=final_norm)
    m = dict(ffn1_norm=m_ffn1_norm, ffn1_w_gate=m_ffn1_w_gate, ffn1_w_up=m_ffn1_w_up, ffn1_w_down=m_ffn1_w_down,
             mix_norm=m_mix_norm, w_in=m_w_in, a_dw_w=m_a_dw_w, a_dw_b=m_a_dw_b, a_ln_g=m_a_ln_g, a_ln_b=m_a_ln_b,
             a_w_out=m_a_w_out, b_conv_w=m_b_conv_w, b_w_out=m_b_w_out, w_o=m_w_o, ffn2_norm=m_ffn2_norm,
             ffn2_w_gate=m_ffn2_w_gate, ffn2_w_up=m_ffn2_w_up, ffn2_w_down=m_ffn2_w_down, final_norm=m_final_norm)
    v = dict(ffn1_norm=v_ffn1_norm, ffn1_w_gate=v_ffn1_w_gate, ffn1_w_up=v_ffn1_w_up, ffn1_w_down=v_ffn1_w_down,
             mix_norm=v_mix_norm, w_in=v_w_in, a_dw_w=v_a_dw_w, a_dw_b=v_a_dw_b, a_ln_g=v_a_ln_g, a_ln_b=v_a_ln_b,
             a_w_out=v_a_w_out, b_conv_w=v_b_conv_w, b_w_out=v_b_w_out, w_o=v_w_o, ffn2_norm=v_ffn2_norm,
             ffn2_w_gate=v_ffn2_w_gate, ffn2_w_up=v_ffn2_w_up, ffn2_w_down=v_ffn2_w_down, final_norm=v_final_norm)
    flat = _pack_weights(dict(wg1=ffn1_w_gate[0].T, wu1=ffn1_w_up[0].T, wd1=ffn1_w_down[0], wg2=ffn2_w_gate[0].T,
                              wu2=ffn2_w_up[0].T, wd2=ffn2_w_down[0], win=w_in[0], wa=a_w_out[0], wb=b_w_out[0],
                              wo=w_o[0]))
    cw_shard = jnp.concatenate([a_dw_w[0], jnp.zeros((1, LANE), F32), b_conv_w[0], jnp.zeros((5, LANE), F32)], axis=0)

    x2, tgt = x[0], loss_target[0]
    st_a, st_b, st_b2 = ("wg1", "wu1"), ("wd1", "win/0/2"), ("win/1/2",)
    st_c, st_d, st_e = ("wa", "wb", "wo", "wg2"), ("wu2",), ("wd2",)

    buf_a, cw = _run_comm(_join(_ag_comm(st_a, flat), _direct_comm(cw_shard, False)), "ag_ffn1")
    n1, gg1, uu1, act1, buf_b = _ffn_gate_up(x2, ffn1_norm, (buf_a, buf_a), (0, F), "ffn1_gate_up", _ag_comm(st_b, flat))
    h1, buf_b2 = _ffn_down(x2, act1, buf_b, 0, "ffn1_down", _ag_comm(st_b2, flat))
    win = ((buf_b, F), (buf_b2, 0))
    u, z, buf_c = _mix_in(h1, mix_norm, win, _ag_comm(st_c, flat))
    dft = _dft_constants()
    cw = jnp.transpose(cw, (1, 0, 2)).reshape(40, D)
    a1, q, buf_d = _conv_fwd_dft(z, cw, a_dw_b, dft, _ag_comm(st_d, flat))
    h2, ya, yb, buf_e = _mix_out(a1, q, z, h1, a_ln_g, a_ln_b, buf_c, _ag_comm(st_e, flat))
    ffn2_bufs, ffn2_offs = (buf_c, buf_d, buf_e), (3 * D, 0, 0)
    dh3, dhb3, s_final, n2, gg2, uu2, act2 = _ffn_fwd(h2, ffn2_norm, ffn2_bufs, ffn2_offs, "ffn2_fwd",
                                          final=(final_norm.reshape(1, D), tgt))

    tr_f = F // 2 if (F // 2) % LANE == 0 else F
    def pair(stage, src):
        return _rs_pair_comm(stage, src)

    def chip(stage, src, pair_buf, tag):
        return _rs_chip_comm(_pair_add(stage, src, pair_buf, "pair_add_" + tag))

    (dgu2,) = _ffn_bwd_hidden(dhb3, gg2, uu2, buf_e, 0, "ffn2_bwd_h")
    (gu2,) = _tn_matmul(dgu2, n2, tr_f, "dw_gu2")
    s2a, src2a = ("wg2", "wu2"), dict(wg2=(gu2, 0), wu2=(gu2, F))
    (gd2,) = _tn_matmul(act2, dhb3, tr_f, "dw_d2")
    s2b, src2b = ("wd2",), dict(wd2=(gd2, 0))
    dh2, s_ffn2, pair2a, pair2b = _ffn_bwd_input(dgu2, dh3, h2, ffn2_norm, (buf_c, buf_d), (3 * D, 0), "ffn2_bwd_x",
                                                 _join(pair(s2a, src2a), pair(s2b, src2b)))
    dzg, da1, dq, lsq, rsq, s_mix, recv2b = _mix_out_bwd(dh2, ya, yb, z, a1, q, a_ln_g, a_ln_b, buf_c,
                                                          chip(s2b, src2b, pair2b, "2b"))
    (gsq,) = _tn_matmul(lsq, rsq, D, "dw_square")
    ssq, srcsq = ("wa", "wb", "wo"), dict(wa=(gsq, D), wb=(gsq, 2 * D), wo=(gsq, 0))
    dz, dwa, dwb, recv2a, pairsq = _conv_bwd_dft(z, da1, dq, dzg, cw, dft,
                                                 _join(chip(s2a, src2a, pair2a, "2a"), pair(ssq, srcsq)))
    gin, recvsq = _tn_matmul(dz, u, D, "dw_in", chip(ssq, srcsq, pairsq, "sq"))
    sin_a, sin_b, srcin = ("win/0/2",), ("win/1/2",), {"win/0/2": (gin, 0), "win/1/2": (gin, 0)}
    dh1, dhb1, s_in, pairin_a, pairin_b = _mix_in_bwd(dz, dh2, h1, mix_norm, win,
                                                _join(pair(sin_a, srcin), pair(sin_b, srcin)))
    dgu1, recvin_a = _ffn_bwd_hidden(dhb1, gg1, uu1, buf_b, 0, "ffn1_bwd_h",
                                           chip(sin_a, srcin, pairin_a, "in_a"))
    gu1, recvin_b = _tn_matmul(dgu1, n1, tr_f, "dw_gu1", chip(sin_b, srcin, pairin_b, "in_b"))
    s1a, src1a = ("wg1", "wu1"), dict(wg1=(gu1, 0), wu1=(gu1, F))
    gd1, pair1a = _tn_matmul(act1, dhb1, tr_f, "dw_d1", pair(s1a, src1a))
    s1b, src1b = ("wd1",), dict(wd1=(gd1, 0))
    xchg1 = _join(chip(s1a, src1a, pair1a, "1a"), pair(s1b, src1b))
    xchg1_sems, xchg1_bufs, token = _comm_start(xchg1, "xchg_ffn1_start")
    dx, s_ffn1 = _ffn_bwd_input(dgu1, dh1, x2, ffn1_norm, (buf_a, buf_a), (0, F), "ffn1_bwd_x", after=token)
    (_, gd1), (recv1a, pair1b) = _comm_wait(xchg1, "xchg_ffn1_wait", xchg1_sems, xchg1_bufs, s_ffn1)
    src1b = dict(wd1=(gd1, 0))

    vec8, convk = _pack_small(s_ffn1, s_in, s_mix, s_ffn2, s_final, dwa, dwb)
    tail = _join(chip(s1b, src1b, pair1b, "1b"), _join(_direct_comm(vec8, False), _direct_comm(convk, True)))
    tail_sems, tail_bufs, token = _comm_start(tail, "xchg_tail_start")

    fs = F // NDEV
    g = dict(ffn1_w_gate=(recv1a, 0), ffn1_w_up=(recv1a, fs), ffn2_w_gate=(recv2a, 0), ffn2_w_up=(recv2a, fs),
             ffn2_w_down=(recv2b, 0), a_w_out=(recvsq, 0), b_w_out=(recvsq, D // NDEV), w_o=(recvsq, 2 * (D // NDEV)))
    grad, upd = {}, {}

    def run(group, name, after, as2d=lambda a: a[0], back=lambda a, n: a.reshape(w[n].shape)):
        res = _adam([g[n] for n in group], [as2d(w[n]) for n in group], [as2d(m[n]) for n in group],
                    [as2d(v[n]) for n in group], name, after)
        for n, r in zip(group, res):
            grad[n], upd[n] = back(r[0], n), tuple(back(a, n) for a in r[1:])
        return res[0][0]

    done = run(("ffn1_w_gate", "ffn1_w_up", "ffn2_w_gate", "ffn2_w_up"), "adam_gate_up", token,
               as2d=lambda a: a[0].T, back=lambda a, n: a.T[None])
    r_in = _adam_in_proj([recvin_a, recvin_b], w_in[0], m_w_in[0], v_w_in[0], done)
    grad["w_in"], upd["w_in"] = r_in[0][None], tuple(a[None] for a in r_in[1:])
    done = run(("a_w_out", "b_w_out", "w_o"), "adam_square", r_in[0])
    _, (recv1b, vec_all, conv_all) = _comm_wait(tail, "xchg_tail_wait", tail_sems, tail_bufs, done)
    g["ffn1_w_down"] = (recv1b, 0)
    run(("ffn1_w_down", "ffn2_w_down"), "adam_down", done)
    vec_names = ("ffn1_norm", "mix_norm", "a_ln_g", "a_ln_b", "a_dw_b", "ffn2_norm", "final_norm")
    tap_names, tap_rows = ("a_dw_w", "b_conv_w"), (0, 32)
    taps = lambda a: jnp.transpose(a, (1, 0, 2))
    loss, vec_res, tap_res = _adam_small(
        vec_all, conv_all, [tuple(t[n].reshape(1, D) for t in (w, m, v)) for n in vec_names],
        [(r,) + tuple(taps(t[n]) for t in (w, m, v)) for n, r in zip(tap_names, tap_rows)])
    for n, r in zip(vec_names, vec_res):
        grad[n], upd[n] = r[0].reshape(w[n].shape), tuple(a.reshape(w[n].shape) for a in r[1:])
    for n, r in zip(tap_names, tap_res):
        grad[n], upd[n] = taps(r[0]), tuple(taps(a) for a in r[1:])

    return (loss.reshape(()), dx.reshape(x.shape), *[grad[n] for n in names], *[upd[n][0] for n in names],
            *[upd[n][1] for n in names], *[upd[n][2] for n in names])
```

```python
import jax
import jax.numpy as jnp
from jax import lax
from jax.experimental import pallas as pl
from jax.experimental.pallas import tpu as pltpu

T = 4096
D = 1024
F = 2816
NG = 7
NDEV = 8
NCHIP = 4
KA, KB = 31, 3
EPS = 1e-6
ADAM_LR, ADAM_B1, ADAM_B2, ADAM_EPS, ADAM_WD, ADAM_STEP = 0.001, 0.9, 0.999, 1e-08, 0.01, 10

TM = 512
FC = 256
TB = 1024
NB = 256
HB = NB // 2
CW = 256
CHB = 64
LANE = 128
TK = 2048
VMEM_LIMIT = 56 * 1024 * 1024

BF = jnp.bfloat16
F32 = jnp.float32
MESH = pl.DeviceIdType.MESH
ANY = pl.BlockSpec(memory_space=pl.ANY)
COLLECTIVE_ID = {(1,): 0, (2, 4, 6): 1, (1, 2, 4, 6): 2, (1, 2, 4): 3}

ORDER = ("wg1", "wu1", "wd1", "wg2", "wu2", "wd2", "win", "wa", "wb", "wo")


class _Layout:
    def __init__(self):
        fs, dis, ds = F // NDEV, NG * D // NDEV, D // NDEV
        self.rows = dict(wg1=fs, wu1=fs, wd1=fs, wg2=fs, wu2=fs, wd2=fs, win=dis, wa=ds, wb=ds, wo=ds)
        self.fl, off = {}, 0
        for n in ORDER:
            self.fl[n] = off
            off += self.rows[n]
        self.RT = off


class _Stage:
    def __init__(self, names):
        lay = _Layout()
        self.names = names
        self.rows, self.full, self.sub, self.fl = {}, {}, {}, {}
        for n in names:
            base, i, k = (n.split("/") + ["0", "1"])[:3]
            self.full[n] = lay.rows[base]
            self.rows[n] = lay.rows[base] // int(k)
            self.sub[n] = int(i) * self.rows[n]
            self.fl[n] = lay.fl[base] + self.sub[n]
        self.off, self.wc, o, w = {}, {}, 0, 0
        for n in names:
            self.off[n], self.wc[n] = o, w
            o += self.rows[n]
            w += NDEV * self.rows[n]
        self.R, self.W = o, w

    def grad_row(self, n, first, dev_lin):
        return first + dev_lin * self.full[n] + self.sub[n]


def _nt(a, b):
    return lax.dot_general(a, b, (((1,), (1,)), ((), ())), preferred_element_type=F32)


def _nn(a, b):
    return lax.dot_general(a, b, (((1,), (0,)), ((), ())), preferred_element_type=F32)


def _tn(a, b):
    return lax.dot_general(a, b, (((0,), (0,)), ((), ())), preferred_element_type=F32)


def _sig(x):
    return 1.0 / (1.0 + jnp.exp(-x))


def _position():
    return lax.axis_index("x"), lax.axis_index("y"), lax.axis_index("c")


def _peer(pos, j):
    x, y, c = pos
    return (1 - x if j & 4 else x, 1 - y if j & 2 else y, 1 - c if j & 1 else c)


def _lin(pos):
    return 4 * pos[0] + 2 * pos[1] + pos[2]


def _chip(pos):
    return 2 * pos[0] + pos[1]


class _Comm:
    def __init__(self, inputs, out_shapes, scratch, start, finish, middle=None, peers=None):
        self.inputs, self.out_shapes, self.scratch = inputs, out_shapes, scratch
        self.start, self.finish, self.middle = start, finish, middle
        self.peers = peers


def _call(body, *, name, grid, args, in_specs, out_shape, out_specs, scratch_shapes=(), comm=None,
          num_scalar_prefetch=0, after=None):
    in_specs, out_shape, out_specs, scratch_shapes = list(in_specs), list(out_shape), list(out_specs), list(scratch_shapes)
    if after is not None:
        inner, pos = body, num_scalar_prefetch + len(in_specs)
        body = lambda *refs: inner(*refs[:pos], *refs[pos + 1:])
        args, in_specs = list(args) + [after], in_specs + [ANY]
    n_in, n_out, n_scr = len(in_specs), len(out_shape), len(scratch_shapes)
    sp = num_scalar_prefetch
    if comm is None:
        kernel_fn = lambda *refs: body(*refs)
        c_in = c_out = c_scr = 0
    else:
        c_in, c_out, c_scr = len(comm.inputs), len(comm.out_shapes), len(comm.scratch)

        def kernel_fn(*refs):
            pre, refs = refs[:sp], refs[sp:]
            ins, cins = refs[:n_in], refs[n_in:n_in + c_in]
            o0 = n_in + c_in
            outs, couts = refs[o0:o0 + n_out], refs[o0 + n_out:o0 + n_out + c_out]
            s0 = o0 + n_out + c_out
            scr, cscr = refs[s0:s0 + n_scr], refs[s0 + n_scr:]
            step, steps = pl.program_id(0), grid[0]
            for a in range(1, len(grid)):
                step, steps = step * grid[a] + pl.program_id(a), steps * grid[a]
            first, last = step == 0, step == steps - 1

            @pl.when(first)
            def _():
                if comm.peers is not None:
                    barrier = pltpu.get_barrier_semaphore()
                    for j in comm.peers:
                        pl.semaphore_signal(barrier, inc=1, device_id=_peer(_position(), j), device_id_type=MESH)
                    pl.semaphore_wait(barrier, len(comm.peers))
                comm.start(cins, couts, cscr)

            if comm.middle is not None:
                @pl.when(step == (steps // 2 if steps > 2 else steps - 1))
                def _():
                    comm.middle(cins, couts, cscr)

            body(*pre, *ins, *outs, *scr)

            @pl.when(last)
            def _():
                comm.finish(cins, couts, cscr)

        args = list(args) + list(comm.inputs)
        in_specs += [ANY] * c_in
        out_shape += list(comm.out_shapes)
        out_specs += [ANY] * c_out
        scratch_shapes += list(comm.scratch)
    params = pltpu.CompilerParams(dimension_semantics=("arbitrary",) * len(grid), vmem_limit_bytes=VMEM_LIMIT,
                                  collective_id=COLLECTIVE_ID[comm.peers] if comm is not None and comm.peers else None)
    if sp:
        grid_spec = pltpu.PrefetchScalarGridSpec(num_scalar_prefetch=sp, grid=grid, in_specs=in_specs,
                                                 out_specs=out_specs, scratch_shapes=scratch_shapes)
        return pl.pallas_call(kernel_fn, name=name, grid_spec=grid_spec, out_shape=out_shape,
                              compiler_params=params)(*args)
    return pl.pallas_call(kernel_fn, name=name, grid=grid, in_specs=in_specs, out_shape=out_shape, out_specs=out_specs,
                          scratch_shapes=scratch_shapes, compiler_params=params)(*args)


def _join(a, b):
    na = (len(a.inputs), len(a.out_shapes), len(a.scratch))

    def split(refs):
        return ([r[:n] for r, n in zip(refs, na)], [r[n:] for r, n in zip(refs, na)])

    def start(*refs):
        ra, rb = split(refs)
        a.start(*ra)
        b.start(*rb)

    def finish(*refs):
        ra, rb = split(refs)
        a.finish(*ra)
        b.finish(*rb)

    def middle(*refs):
        for stage, r in zip((a, b), split(refs)):
            if stage.middle is not None:
                stage.middle(*r)

    return _Comm(list(a.inputs) + list(b.inputs), list(a.out_shapes) + list(b.out_shapes),
                 list(a.scratch) + list(b.scratch), start, finish,
                 middle if (a.middle is not None or b.middle is not None) else None,
                 peers=tuple(sorted(set(a.peers) | set(b.peers))) if a.peers and b.peers else None)


def _run_comm(comm, name):
    def body(*refs):
        c_in, c_out = len(comm.inputs), len(comm.out_shapes)
        parts = (refs[:c_in], refs[c_in:c_in + c_out], refs[c_in + c_out:])
        comm.start(*parts)
        if comm.middle is not None:
            comm.middle(*parts)
        comm.finish(*parts)

    return pl.pallas_call(
        body, name=name, out_shape=list(comm.out_shapes), in_specs=[ANY] * len(comm.inputs),
        out_specs=[ANY] * len(comm.out_shapes), scratch_shapes=list(comm.scratch))(*comm.inputs)


HBM = pl.BlockSpec(memory_space=pltpu.HBM)
SEM = pl.BlockSpec(memory_space=pltpu.SEMAPHORE)
DATAFLOW = pltpu.SideEffectType.DATAFLOW_SIDE_EFFECTING


def _comm_start(comm, name):
    c_in, c_out = len(comm.inputs), len(comm.out_shapes)
    sems = [s(()) if s is pltpu.SemaphoreType.DMA else s for s in comm.scratch]
    bufs = list(comm.inputs) + [lax.empty(s.shape, s.dtype) for s in comm.out_shapes]

    def body(*refs):
        sem_refs = refs[c_in + c_out:c_in + c_out + len(sems)]
        comm.start(refs[:c_in], refs[c_in:c_in + c_out], sem_refs)
        refs[-1][...] = jnp.zeros_like(refs[-1])

    outs = pl.pallas_call(
        body, name=name,
        out_shape=sems + [pltpu.HBM(b.shape, b.dtype) for b in bufs] + [jax.ShapeDtypeStruct((8, LANE), F32)],
        in_specs=[HBM] * len(bufs),
        out_specs=[SEM] * len(sems) + [HBM] * len(bufs) + [pl.BlockSpec(memory_space=pltpu.VMEM)],
        input_output_aliases={i: len(sems) + i for i in range(len(bufs))},
        compiler_params=pltpu.CompilerParams(has_side_effects=DATAFLOW),
    )(*[pltpu.with_memory_space_constraint(b, pltpu.HBM) for b in bufs])
    return outs[:len(sems)], outs[len(sems):-1], outs[-1]


def _comm_wait(comm, name, sems, bufs, after):
    c_in, c_out = len(comm.inputs), len(comm.out_shapes)

    def body(*refs):
        sem_refs = refs[c_in + c_out:c_in + c_out + len(sems)]
        comm.finish(refs[:c_in], refs[c_in:c_in + c_out], sem_refs)

    outs = pl.pallas_call(
        body, name=name, out_shape=[pltpu.HBM(b.shape, b.dtype) for b in bufs],
        in_specs=[HBM] * len(bufs) + [SEM] * len(sems) + [ANY], out_specs=[HBM] * len(bufs),
        input_output_aliases={i: i for i in range(len(bufs))},
        compiler_params=pltpu.CompilerParams(has_side_effects=DATAFLOW),
    )(*bufs, *sems, after)
    return outs[:c_in], outs[c_in:]


def _ag_comm(names, flat):
    st = _Stage(names)

    def ring(me):
        x, y, c = me
        diagonal = x == y
        up = (jnp.where(diagonal, x, 1 - x), jnp.where(diagonal, 1 - y, y), c)
        down = (jnp.where(diagonal, 1 - x, x), jnp.where(diagonal, y, 1 - y), c)
        low = c == 0
        passed = tuple(jnp.where(low, d, u) for d, u in zip(down, up))
        target = tuple(jnp.where(low, u, d) for d, u in zip(down, up))
        return up, down, (1 - x, 1 - y, c), passed, target

    def parts(refs):
        (flat_ref,), (out_ref,), (send_sems, recv_sems, local_sem) = refs
        me = _position()

        def region(name, dev):
            r = st.rows[name]
            return out_ref.at[pl.ds(st.wc[name] + _lin(dev) * r, r), :]

        def own(name):
            return flat_ref.at[pl.ds(st.fl[name], st.rows[name]), :]

        def copies(k, dev, to, from_flat):
            return [pltpu.make_async_remote_copy(
                src_ref=own(n) if from_flat else region(n, dev), dst_ref=region(n, dev), send_sem=send_sems.at[k],
                recv_sem=recv_sems.at[k], device_id=to, device_id_type=MESH) for n in names]

        def whole(k):
            return pltpu.make_async_remote_copy(
                src_ref=flat_ref.at[pl.ds(0, st.R), :], dst_ref=out_ref.at[pl.ds(0, st.R), :],
                send_sem=send_sems.at[k], recv_sem=recv_sems.at[k], device_id=me, device_id_type=MESH)

        return me, region, own, copies, whole, flat_ref, out_ref, local_sem

    def start(*refs):
        me, region, own, copies, _, _, _, local_sem = parts(refs)
        for n in names:
            pltpu.make_async_copy(own(n), region(n, me), local_sem).start()
        up, down, _, _, _ = ring(me)
        for k, to in ((1, up), (2, down), (0, _peer(me, 1))):
            for cp in copies(k, me, to, True):
                cp.start()

    def middle(*refs):
        me, _, _, copies, whole, _, _, _ = parts(refs)
        up, down, _, passed, target = ring(me)
        sib = _peer(me, 1)
        whole(1).wait_recv()
        whole(2).wait_recv()
        for k, dev, to in ((3, passed, target), (4, down, sib), (5, up, sib)):
            for cp in copies(k, dev, to, False):
                cp.start()

    def finish(*refs):
        me, _, _, copies, whole, flat_ref, out_ref, local_sem = parts(refs)
        _, _, across, _, _ = ring(me)
        whole(3).wait_recv()
        for cp in copies(6, across, _peer(me, 1), False):
            cp.start()
        whole(0).wait_recv()
        for j in range(3):
            whole(4 + j).wait_recv()
        for k in range(7):
            whole(k).wait_send()
        pltpu.make_async_copy(flat_ref.at[pl.ds(0, st.R), :], out_ref.at[pl.ds(0, st.R), :], local_sem).wait()

    return _Comm([flat], [jax.ShapeDtypeStruct((st.W, D), BF)],
                 [pltpu.SemaphoreType.DMA((7,)), pltpu.SemaphoreType.DMA((7,)), pltpu.SemaphoreType.DMA],
                 start, finish, middle, peers=(1, 2, 4))


def _rs_pair_comm(names, src):
    st = _Stage(names)
    arrays = []
    for n in names:
        if not any(src[n][0] is a for a in arrays):
            arrays.append(src[n][0])
    idx = {n: [i for i, a in enumerate(arrays) if a is src[n][0]][0] for n in names}

    def slot_wait(refs):
        recv = refs[1][0]
        send_sem, recv_sem = refs[2]
        return pltpu.make_async_remote_copy(src_ref=recv, dst_ref=recv, send_sem=send_sem, recv_sem=recv_sem,
                                            device_id=_position(), device_id_type=MESH)

    def start(*refs):
        ins, (recv,), (send_sem, recv_sem) = refs
        me = _position()
        sib = _peer(me, 1)
        for q in range(NCHIP):
            dev = (q // 2, q % 2, sib[2])
            for n in names:
                r = st.rows[n]
                pltpu.make_async_remote_copy(
                    src_ref=ins[idx[n]].at[pl.ds(st.grad_row(n, src[n][1], _lin(dev)), r), :],
                    dst_ref=recv.at[q, pl.ds(st.off[n], r), :], send_sem=send_sem, recv_sem=recv_sem,
                    device_id=sib, device_id_type=MESH).start()

    def finish(*refs):
        w = slot_wait(refs)
        w.wait_recv()
        w.wait_send()

    return _Comm(arrays, [jax.ShapeDtypeStruct((NCHIP, st.R, D), BF)],
                 [pltpu.SemaphoreType.DMA, pltpu.SemaphoreType.DMA], start, finish, peers=(1,))


def _pair_add(names, src, recv, name):
    st = _Stage(names)
    c_arr = jnp.reshape(lax.axis_index("c"), (1,)).astype(jnp.int32)

    def body(c_ref, *refs):
        r_ref, o_ref = refs[len(names)], refs[len(names) + 1]
        for a_ref, n in zip(refs, names):
            rows = slice(st.off[n], st.off[n] + st.rows[n])
            o_ref[rows, :] = (a_ref[...].astype(F32) + r_ref[rows, :].astype(F32)).astype(BF)

    def shard_spec(n):
        r = st.rows[n]
        base, step = st.grad_row(n, src[n][1], 0) // r, st.full[n] // r
        return pl.BlockSpec((r, D), lambda q, c_ref: (base + step * (2 * q + c_ref[0]), 0))

    slot = pl.BlockSpec((None, st.R, D), lambda q, c_ref: (q, 0, 0))
    return _call(body, name=name, grid=(NCHIP,), args=[c_arr] + [src[n][0] for n in names] + [recv],
                 in_specs=[shard_spec(n) for n in names] + [slot],
                 out_shape=[jax.ShapeDtypeStruct((NCHIP, st.R, D), BF)], out_specs=[slot], num_scalar_prefetch=1)[0]


def _rs_chip_comm(part):
    def copies(refs):
        (p_ref,), (recv,), (send_sems, recv_sems, local_sem) = refs
        me = _position()
        mine = pltpu.make_async_copy(p_ref.at[_chip(me)], recv.at[_chip(me)], local_sem)
        out = []
        for j, bits in enumerate((4, 2, 6)):
            to = _peer(me, bits)
            out.append(pltpu.make_async_remote_copy(
                src_ref=p_ref.at[_chip(to)], dst_ref=recv.at[_chip(me)], send_sem=send_sems.at[j],
                recv_sem=recv_sems.at[j], device_id=to, device_id_type=MESH))
        return mine, out

    def start(*refs):
        mine, out = copies(refs)
        mine.start()
        for cp in out:
            cp.start()

    def finish(*refs):
        mine, out = copies(refs)
        for cp in out:
            cp.wait_recv()
        for cp in out:
            cp.wait_send()
        mine.wait()

    return _Comm([part], [jax.ShapeDtypeStruct(part.shape, BF)],
                 [pltpu.SemaphoreType.DMA((3,)), pltpu.SemaphoreType.DMA((3,)), pltpu.SemaphoreType.DMA],
                 start, finish, peers=(2, 4, 6))


def _direct_comm(x, scatter):
    def copies(refs):
        (x_ref,), (out_ref,), (send_sems, recv_sems, local_sem) = refs
        me = _position()

        def piece(dev):
            return x_ref.at[_lin(dev)] if scatter else x_ref

        mine = pltpu.make_async_copy(piece(me), out_ref.at[_lin(me)], local_sem)
        return mine, [pltpu.make_async_remote_copy(
            src_ref=piece(_peer(me, j)), dst_ref=out_ref.at[_lin(me)], send_sem=send_sems.at[j - 1],
            recv_sem=recv_sems.at[j - 1], device_id=_peer(me, j), device_id_type=MESH) for j in range(1, NDEV)]

    def start(*refs):
        mine, cps = copies(refs)
        mine.start()
        for cp in cps:
            cp.start()

    def finish(*refs):
        mine, cps = copies(refs)
        for cp in cps:
            cp.wait_recv()
        for cp in cps:
            cp.wait_send()
        mine.wait()

    shape = x.shape if scatter else (NDEV,) + x.shape
    return _Comm([x], [jax.ShapeDtypeStruct(shape, x.dtype)],
                 [pltpu.SemaphoreType.DMA((7,)), pltpu.SemaphoreType.DMA((7,)), pltpu.SemaphoreType.DMA],
                 start, finish)


def _pack_weights(shards):
    lay = _Layout()

    def body(*refs):
        o_ref = refs[-1]
        for ref, n in zip(refs, ORDER):
            x = ref[...].T if n == "win" else ref[...]
            o_ref[lay.fl[n]:lay.fl[n] + lay.rows[n], :] = x.astype(BF)

    return pl.pallas_call(
        body, name="pack_weights", out_shape=jax.ShapeDtypeStruct((lay.RT, D), BF),
        compiler_params=pltpu.CompilerParams(vmem_limit_bytes=VMEM_LIMIT))(*[shards[n] for n in ORDER])


def _load_ffn_weights(srcs, offs, scratch, sem):
    @pl.when(pl.program_id(0) == 0)
    def _():
        cps = [pltpu.make_async_copy(s.at[pl.ds(off, dst.shape[0]), :], dst, sem.at[i])
               for i, (s, off, dst) in enumerate(zip(srcs, offs, scratch))]
        for cp in cps:
            cp.start()
        for cp in cps:
            cp.wait()


def _final_loss_tile(xf, g, tgt, s_ref):
    r = lax.rsqrt(jnp.mean(xf * xf, axis=-1, keepdims=True) + EPS)
    xr = xf * r
    e = xr * g - tgt
    s_ref[1:2, :] += jnp.sum(e * e, axis=0, keepdims=True) * (0.5 / D)
    dy = e * (1.0 / D)
    s_ref[0:1, :] += jnp.sum(dy * xr, axis=0, keepdims=True)
    gdy = dy * g
    return r * gdy - xr * (r * jnp.mean(gdy * xr, axis=-1, keepdims=True))


def _ffn_fwd(x, g, wbufs, offs, name, comm=None, final=None):
    nf = F // FC

    def body(x_ref, g_ref, b0, b1, b2, *rest):
        if final is None:
            h_ref, n_ref, gg_ref, uu_ref, a_ref, wg_s, wu_s, wd_s, sem = rest
        else:
            gf_ref, t_ref, dh_ref, dhb_ref, s_ref, n_ref, gg_ref, uu_ref, a_ref, wg_s, wu_s, wd_s, sem = rest

            @pl.when(pl.program_id(0) == 0)
            def _():
                s_ref[...] = jnp.zeros_like(s_ref)

        _load_ffn_weights((b0, b1, b2), offs, (wg_s, wu_s, wd_s), sem)
        xf = x_ref[...]
        r = lax.rsqrt(jnp.mean(xf * xf, axis=-1, keepdims=True) + EPS)
        nb = (xf * r * g_ref[...]).astype(BF)
        n_ref[...] = nb
        acc = jnp.zeros((TM, D), F32)
        for c in range(nf):
            sl = slice(c * FC, (c + 1) * FC)
            gb = _nt(nb, wg_s[sl, :]).astype(BF)
            ub = _nt(nb, wu_s[sl, :]).astype(BF)
            gg_ref[:, sl] = gb
            uu_ref[:, sl] = ub
            a = (gb * _sig(gb)) * ub
            a_ref[0, :, sl] = a
            acc = acc + _nn(a, wd_s[sl, :])
        h = xf + 0.5 * acc
        if final is None:
            h_ref[...] = h
        else:
            dh = _final_loss_tile(h, gf_ref[...], t_ref[...], s_ref)
            dh_ref[...] = dh
            dhb_ref[...] = (0.5 * dh).astype(BF)

    row = lambda i: (i, 0)
    vec = pl.BlockSpec((1, D), lambda i: (0, 0))
    tile = pl.BlockSpec((TM, D), row)
    saved_shapes = [jax.ShapeDtypeStruct((T, D), BF), jax.ShapeDtypeStruct((T, F), BF), jax.ShapeDtypeStruct((T, F), BF),
                    jax.ShapeDtypeStruct((1, T, F), BF)]
    saved_specs = [tile, pl.BlockSpec((TM, F), row), pl.BlockSpec((TM, F), row),
                   pl.BlockSpec((1, TM, F), lambda i: (0, i, 0))]
    if final is None:
        extra_args, extra_specs = [], []
        head_shapes, head_specs = [jax.ShapeDtypeStruct((T, D), F32)], [tile]
    else:
        extra_args, extra_specs = list(final), [vec, tile]
        head_shapes = [jax.ShapeDtypeStruct((T, D), F32), jax.ShapeDtypeStruct((T, D), BF), jax.ShapeDtypeStruct((8, D), F32)]
        head_specs = [tile, tile, pl.BlockSpec((8, D), lambda i: (0, 0))]
    return _call(
        body, name=name, grid=(T // TM,), args=[x, g, *wbufs, *extra_args], comm=comm,
        in_specs=[tile, vec, ANY, ANY, ANY] + extra_specs,
        out_shape=head_shapes + saved_shapes, out_specs=head_specs + saved_specs,
        scratch_shapes=[pltpu.VMEM((F, D), BF)] * 3 + [pltpu.SemaphoreType.DMA((3,))])


def _ffn_gate_up(x, g, wbufs, offs, name, comm=None):
    nf = F // FC

    def body(x_ref, g_ref, b0, b1, n_ref, gg_ref, uu_ref, a_ref, wg_s, wu_s, sem):
        _load_ffn_weights((b0, b1), offs, (wg_s, wu_s), sem)
        xf = x_ref[...]
        r = lax.rsqrt(jnp.mean(xf * xf, axis=-1, keepdims=True) + EPS)
        nb = (xf * r * g_ref[...]).astype(BF)
        n_ref[...] = nb
        for c in range(nf):
            sl = slice(c * FC, (c + 1) * FC)
            gb = _nt(nb, wg_s[sl, :]).astype(BF)
            ub = _nt(nb, wu_s[sl, :]).astype(BF)
            gg_ref[:, sl] = gb
            uu_ref[:, sl] = ub
            a_ref[0, :, sl] = (gb * _sig(gb)) * ub

    row = lambda i: (i, 0)
    tile = pl.BlockSpec((TM, D), row)
    return _call(
        body, name=name, grid=(T // TM,), args=[x, g, *wbufs], comm=comm,
        in_specs=[tile, pl.BlockSpec((1, D), lambda i: (0, 0)), ANY, ANY],
        out_shape=[jax.ShapeDtypeStruct((T, D), BF), jax.ShapeDtypeStruct((T, F), BF), jax.ShapeDtypeStruct((T, F), BF),
                   jax.ShapeDtypeStruct((1, T, F), BF)],
        out_specs=[tile, pl.BlockSpec((TM, F), row), pl.BlockSpec((TM, F), row),
                   pl.BlockSpec((1, TM, F), lambda i: (0, i, 0))],
        scratch_shapes=[pltpu.VMEM((F, D), BF)] * 2 + [pltpu.SemaphoreType.DMA((2,))])


def _ffn_down(x, act, wbuf, off, name, comm=None):
    def body(x_ref, a_ref, b0, h_ref, wd_s, sem):
        _load_ffn_weights((b0,), (off,), (wd_s,), sem)
        h_ref[...] = x_ref[...] + 0.5 * _nn(a_ref[0], wd_s[...])

    tile = pl.BlockSpec((TM, D), lambda i: (i, 0))
    return _call(
        body, name=name, grid=(T // TM,), args=[x, act, wbuf], comm=comm,
        in_specs=[tile, pl.BlockSpec((1, TM, F), lambda i: (0, i, 0)), ANY],
        out_shape=[jax.ShapeDtypeStruct((T, D), F32)], out_specs=[tile],
        scratch_shapes=[pltpu.VMEM((F, D), BF), pltpu.SemaphoreType.DMA((1,))])


def _load_in_proj(parts, w_s, sem):
    @pl.when(pl.program_id(0) == 0)
    def _():
        shard = NG * D // NDEV
        rows = shard // len(parts)
        cps = [pltpu.make_async_copy(buf.at[pl.ds(first + k * rows, rows), :],
                                     w_s.at[pl.ds(k * shard + p * rows, rows), :], sem.at[p * NDEV + k])
               for p, (buf, first) in enumerate(parts) for k in range(NDEV)]
        for cp in cps:
            cp.start()
        for cp in cps:
            cp.wait()


def _mix_in(h1, gm, win, comm=None):
    def body(h_ref, g_ref, *rest):
        w_any, (u_ref, z_ref, w_s, sem) = rest[:len(win)], rest[len(win):]
        _load_in_proj([(b, first) for b, (_, first) in zip(w_any, win)], w_s, sem)
        xf = h_ref[...]
        r = lax.rsqrt(jnp.mean(xf * xf, axis=-1, keepdims=True) + EPS)
        ub = (xf * r * g_ref[...]).astype(BF)
        u_ref[...] = ub
        for j in range(NG):
            z_ref[j] = _nt(ub, w_s[j * D:(j + 1) * D, :]).astype(BF)

    row = lambda i: (i, 0)
    return _call(
        body, name="mix_in", grid=(T // TM,), args=[h1, gm] + [b for b, _ in win], comm=comm,
        in_specs=[pl.BlockSpec((TM, D), row), pl.BlockSpec((1, D), lambda i: (0, 0))] + [ANY] * len(win),
        out_shape=[jax.ShapeDtypeStruct((T, D), BF), jax.ShapeDtypeStruct((NG, T, D), BF)],
        out_specs=[pl.BlockSpec((TM, D), row), pl.BlockSpec((NG, TM, D), lambda i: (0, i, 0))],
        scratch_shapes=[pltpu.VMEM((NG * D, D), BF), pltpu.SemaphoreType.DMA((NDEV * len(win),))])


def _shift_up(w, b):
    return w if b == 0 else pltpu.roll(w, w.shape[0] - b, 0)


def _fold8(p):
    red = p[0:8, :]
    for i in range(1, p.shape[0] // 8):
        red = red + p[8 * i:8 * i + 8, :]
    return red


def _dft_constants():
    import numpy as np
    nh = NB // 2
    f, n = np.arange(nh)[:, None], np.arange(NB)[None, :]
    ang = 2.0 * np.pi / NB * f * n
    fc = np.cos(ang)
    fs = np.where(f == 0, (-1.0) ** n, np.sin(ang))
    scale = np.where(f == 0, 1.0, 2.0) / NB
    ic = (scale * np.cos(ang)).T
    isn = np.where(f == 0, (-1.0) ** n / NB, scale * np.sin(ang)).T
    d = (KA - 1 - np.arange(32))[None, :]
    valid = (np.arange(32) < KA)[None, :]
    angk = 2.0 * np.pi / NB * f * d
    kc = np.where(valid, np.cos(angk), 0.0)
    ks = np.where(valid, np.sin(angk), 0.0)
    k2 = np.where(valid, np.where(f == 0, (-1.0) ** d, np.cos(angk)), 0.0)
    rtc = np.where(valid, scale * np.cos(angk), 0.0).T
    rts = np.where(valid, np.where(f == 0, (-1.0) ** d / NB, scale * np.sin(angk)), 0.0).T

    def bf(a):
        return jnp.asarray(a, F32).astype(BF)

    def split(a):
        hi = bf(a)
        return hi, (jnp.asarray(a, F32) - hi.astype(F32)).astype(BF)

    return dict(fc=bf(fc), fs=bf(fs), ic_hi=bf(ic[HB:]), is_hi=bf(isn[HB:]), ic_lo=bf(ic[:HB]), is_lo=bf(isn[:HB]),
                kc=split(kc), ks=split(ks), k2=split(k2), rtc=split(rtc), rts=split(rts))


def _dot3(m_hi, m_lo, x):
    x_hi = x.astype(BF)
    x_lo = (x - x_hi.astype(F32)).astype(BF)
    return _nn(m_hi, x_hi) + _nn(m_hi, x_lo) + _nn(m_lo, x_hi)


def _whole(a):
    return pl.BlockSpec(a.shape, lambda c, t: (0,) * a.ndim)


def _filter_spectrum(cw_ref, tabs, hc, hs, h2):
    w32 = cw_ref[0:32, :]
    for (hi, lo), dst in zip(tabs, (hc, hs, h2)):
        dst[...] = _dot3(hi[...], lo[...], w32)


def _conv_fwd_dft(z, cw, bias, dft, comm=None):
    nt = T // TB
    hb = TB // HB

    def body(z_ref, zh_ref, cw_ref, b_ref, fc_ref, fs_ref, ic_ref, is_ref, kch, kcl, ksh, ksl, k2h, k2l,
             a1_ref, q_ref, aext, ppad, hc, hs, h2):
        first = pl.program_id(1) == 0
        f = lambda ref, j: ref[j].astype(F32)

        @pl.when(first)
        def _():
            _filter_spectrum(cw_ref, ((kch, kcl), (ksh, ksl), (k2h, k2l)), hc, hs, h2)

        aext[0:HB, :] = jnp.where(first, 0.0, f(zh_ref, 0) * _sig(f(zh_ref, 1))).astype(BF)
        aext[HB:, :] = (f(z_ref, 0) * _sig(f(z_ref, 1))).astype(BF)
        ppad[0:8, :] = jnp.where(first, 0.0, f(zh_ref, 3)[HB - 8:HB, :] * f(zh_ref, 4)[HB - 8:HB, :])
        ppad[8:, :] = f(z_ref, 3) * f(z_ref, 4)
        bias_row = b_ref[...]

        for j in range(TB // HB):
            xs = aext[j * HB:j * HB + NB, :]
            xa, xb = _nn(fc_ref[...], xs), _nn(fs_ref[...], xs)
            yc = (hc[...] * xa - hs[...] * xb).astype(BF)
            ys = (h2[...] * xb + hs[...] * xa).astype(BF)
            y = _nn(ic_ref[...], yc) + _nn(is_ref[...], ys)
            a1_ref[j * HB:(j + 1) * HB, :] = (y + bias_row).astype(BF)

        def chunk(r, carry):
            base = pl.multiple_of(r * CHB, CHB)
            pw = ppad[pl.ds(base, CHB + 8), :]
            v = (cw_ref[pl.ds(32, 1), :] * _shift_up(pw, 6)[0:CHB, :]
                 + cw_ref[pl.ds(33, 1), :] * _shift_up(pw, 7)[0:CHB, :]
                 + cw_ref[pl.ds(34, 1), :] * pw[8:8 + CHB, :])
            q_ref[pl.ds(base, CHB), :] = (z_ref[2, pl.ds(base, CHB), :].astype(F32) * v).astype(BF)
            return carry

        lax.fori_loop(0, TB // CHB, chunk, 0)

    blk = pl.BlockSpec((TB, CW), lambda c, t: (t, c))
    tabs = [dft["fc"], dft["fs"], dft["ic_hi"], dft["is_hi"], *dft["kc"], *dft["ks"], *dft["k2"]]
    return _call(
        body, name="conv_fwd", grid=(D // CW, nt), comm=comm, args=[z, z, cw, bias] + tabs,
        in_specs=[pl.BlockSpec((5, TB, CW), lambda c, t: (0, t, c)),
                  pl.BlockSpec((5, HB, CW), lambda c, t: (0, jnp.maximum(t * hb - 1, 0), c)),
                  pl.BlockSpec((40, CW), lambda c, t: (0, c)), pl.BlockSpec((1, CW), lambda c, t: (0, c))]
                 + [_whole(a) for a in tabs],
        out_shape=[jax.ShapeDtypeStruct((T, D), BF), jax.ShapeDtypeStruct((T, D), BF)], out_specs=[blk, blk],
        scratch_shapes=[pltpu.VMEM((TB + HB, CW), BF), pltpu.VMEM((TB + 8, CW), F32)]
                       + [pltpu.VMEM((NB // 2, CW), F32)] * 3)


def _conv_bwd_dft(z, da1, dq, dzg, cw, dft, comm=None):
    nt = T // TB
    hb = TB // HB
    last_h = T // HB - 1

    def body(z_ref, zp_ref, zn_ref, da1_ref, da1n_ref, dq_ref, dqn_ref, dzg_ref, cw_ref,
             fc_ref, fs_ref, ic_ref, is_ref, kch, kcl, ksh, ksl, k2h, k2l, rch, rcl, rsh, rsl,
             dz_ref, dwa_ref, dwb_ref, aext, dyext, ppad, dvpad, hc, hs, h2, rc, rs, nyq, acc_b):
        t = pl.program_id(1)
        first, last = t == 0, t == nt - 1
        f = lambda ref, j: ref[j].astype(F32)

        @pl.when(first)
        def _():
            _filter_spectrum(cw_ref, ((kch, kcl), (ksh, ksl), (k2h, k2l)), hc, hs, h2)
            rc[...] = jnp.zeros_like(rc)
            rs[...] = jnp.zeros_like(rs)
            nyq[...] = jnp.zeros_like(nyq)
            acc_b[...] = jnp.zeros_like(acc_b)

        aext[0:HB, :] = jnp.where(first, 0.0, f(zp_ref, 0) * _sig(f(zp_ref, 1))).astype(BF)
        aext[HB:, :] = (f(z_ref, 0) * _sig(f(z_ref, 1))).astype(BF)
        dyext[0:TB, :] = da1_ref[...]
        dyext[TB:, :] = jnp.where(last, 0.0, da1n_ref[...].astype(F32)).astype(BF)
        ppad[0:8, :] = jnp.where(first, 0.0, f(zp_ref, 3)[HB - 8:HB, :] * f(zp_ref, 4)[HB - 8:HB, :])
        ppad[8:, :] = f(z_ref, 3) * f(z_ref, 4)
        dvpad[0:TB, :] = dq_ref[...].astype(F32) * f(z_ref, 2)
        dvpad[TB:, :] = jnp.where(last, 0.0, dqn_ref[...].astype(F32)[0:8, :] * f(zn_ref, 2)[0:8, :])

        for j in range(TB // HB):
            rows = slice(j * HB, (j + 1) * HB)
            dys = dyext[j * HB:j * HB + NB, :]
            da, db = _nn(fc_ref[...], dys), _nn(fs_ref[...], dys)
            gc = (hc[...] * da + hs[...] * db).astype(BF)
            gs = (h2[...] * db - hs[...] * da).astype(BF)
            da0 = _nn(ic_ref[...], gc) + _nn(is_ref[...], gs)
            z0, z1 = z_ref[0, rows, :].astype(F32), z_ref[1, rows, :].astype(F32)
            s1 = _sig(z1)
            dz_ref[0, rows, :] = (da0 * s1).astype(BF)
            dz_ref[1, rows, :] = (da0 * z0 * (s1 * (1.0 - s1))).astype(BF)
            xs = aext[j * HB:j * HB + NB, :]
            xa, xb = _nn(fc_ref[...], xs), _nn(fs_ref[...], xs)
            dyb = dyext[rows, :]
            pa, pb = _nn(fc_ref[:, HB:NB], dyb), _nn(fs_ref[:, HB:NB], dyb)
            rc[...] += pa * xa + pb * xb
            rs[...] += pb * xa - pa * xb
            nyq[...] += pb[0:8, :] * xb[0:8, :]

        def chunk(r, carry):
            base = pl.multiple_of(r * CHB, CHB)
            rows = pl.ds(base, CHB)
            pw = ppad[pl.ds(base, CHB + 8), :]
            p6 = _shift_up(pw, 6)[0:CHB, :]
            p7 = _shift_up(pw, 7)[0:CHB, :]
            p8 = pw[8:8 + CHB, :]
            wb0, wb1, wb2 = cw_ref[pl.ds(32, 1), :], cw_ref[pl.ds(33, 1), :], cw_ref[pl.ds(34, 1), :]
            v = wb0 * p6 + wb1 * p7 + wb2 * p8
            dz_ref[2, rows, :] = (dq_ref[rows, :].astype(F32) * v).astype(BF)
            dvw = dvpad[pl.ds(base, CHB + 8), :]
            dvc = dvw[0:CHB, :]
            dp = wb2 * dvc + wb1 * _shift_up(dvw, 1)[0:CHB, :] + wb0 * _shift_up(dvw, 2)[0:CHB, :]
            dz_ref[3, rows, :] = (dp * z_ref[4, rows, :].astype(F32)).astype(BF)
            dz_ref[4, rows, :] = (dp * z_ref[3, rows, :].astype(F32)).astype(BF)
            acc_b[0:8, :] += _fold8(dvc * p6)
            acc_b[8:16, :] += _fold8(dvc * p7)
            acc_b[16:24, :] += _fold8(dvc * p8)
            dz_ref[5, rows, :] = dzg_ref[0, rows, :]
            dz_ref[6, rows, :] = dzg_ref[1, rows, :]
            return carry

        lax.fori_loop(0, TB // CHB, chunk, 0)

        @pl.when(last)
        def _():
            row0 = lax.broadcasted_iota(jnp.int32, (NB // 2, CW), 0) == 0
            ny = jnp.broadcast_to(nyq[0:1, :], (NB // 2, CW))
            rcv = jnp.where(row0, rc[...] - ny, rc[...])
            rsv = jnp.where(row0, ny, rs[...])
            dwa_ref[...] = _dot3(rch[...], rcl[...], rcv) + _dot3(rsh[...], rsl[...], rsv)
            for k in range(KB):
                dwb_ref[k:k + 1, :] = jnp.sum(acc_b[8 * k:8 * k + 8, :], axis=0, keepdims=True)
            dwb_ref[KB:8, :] = jnp.zeros((8 - KB, CW), F32)

    blk = lambda c, t: (t, c)
    nxt = lambda c, t: (jnp.minimum((t + 1) * hb, last_h), c)
    tabs = [dft["fc"], dft["fs"], dft["ic_lo"], dft["is_lo"], *dft["kc"], *dft["ks"], *dft["k2"], *dft["rtc"], *dft["rts"]]
    return _call(
        body, name="conv_bwd", grid=(D // CW, nt), comm=comm, args=[z, z, z, da1, da1, dq, dq, dzg, cw] + tabs,
        in_specs=[pl.BlockSpec((5, TB, CW), lambda c, t: (0, t, c)),
                  pl.BlockSpec((5, HB, CW), lambda c, t: (0, jnp.maximum(t * hb - 1, 0), c)),
                  pl.BlockSpec((5, HB, CW), lambda c, t: (0, jnp.minimum((t + 1) * hb, last_h), c)),
                  pl.BlockSpec((TB, CW), blk), pl.BlockSpec((HB, CW), nxt),
                  pl.BlockSpec((TB, CW), blk), pl.BlockSpec((HB, CW), nxt),
                  pl.BlockSpec((2, TB, CW), lambda c, t: (0, t, c)),
                  pl.BlockSpec((40, CW), lambda c, t: (0, c))]
                 + [_whole(a) for a in tabs],
        out_shape=[jax.ShapeDtypeStruct((NG, T, D), BF), jax.ShapeDtypeStruct((32, D), F32),
                   jax.ShapeDtypeStruct((8, D), F32)],
        out_specs=[pl.BlockSpec((NG, TB, CW), lambda c, t: (0, t, c)),
                   pl.BlockSpec((32, CW), lambda c, t: (0, c)), pl.BlockSpec((8, CW), lambda c, t: (0, c))],
        scratch_shapes=[pltpu.VMEM((TB + HB, CW), BF), pltpu.VMEM((TB + HB, CW), BF),
                        pltpu.VMEM((TB + 8, CW), F32), pltpu.VMEM((TB + 8, CW), F32)]
                       + [pltpu.VMEM((NB // 2, CW), F32)] * 5 + [pltpu.VMEM((8, CW), F32), pltpu.VMEM((24, CW), F32)])


def _layernorm_silu(a1, lng, lnb):
    mu = jnp.mean(a1, axis=-1, keepdims=True)
    xc = a1 - mu
    rs = lax.rsqrt(jnp.mean(xc * xc, axis=-1, keepdims=True) + EPS)
    xh = xc * rs
    a2 = xh * lng + lnb
    sg = _sig(a2)
    return xh, rs, a2, sg


def _square_specs(blocks):
    return [pl.BlockSpec((D, D), lambda i, b=b: (b, 0)) for b in blocks]


def _mix_out(a1, q, z, h1, lng, lnb, wsq, comm=None):
    def body(a1_ref, q_ref, ga_ref, gb_ref, h_ref, lng_ref, lnb_ref, wa_ref, wb_ref, wo_ref, h2_ref, ya_ref, yb_ref):
        _, _, a2, sg = _layernorm_silu(a1_ref[...].astype(F32), lng_ref[...], lnb_ref[...])
        ya = _nn((a2 * sg).astype(BF), wa_ref[...])
        yb = _nn(q_ref[...], wb_ref[...])
        ya_ref[...] = ya.astype(BF)
        yb_ref[...] = yb.astype(BF)
        m = _sig(ga_ref[...].astype(F32)) * ya + _sig(gb_ref[...].astype(F32)) * yb
        h2_ref[...] = h_ref[...] + _nn(m.astype(BF), wo_ref[...])

    row = lambda i: (i, 0)
    vec = pl.BlockSpec((1, D), lambda i: (0, 0))
    return _call(
        body, name="mix_out", grid=(T // TM,), args=[a1, q, z, z, h1, lng, lnb, wsq, wsq, wsq], comm=comm,
        in_specs=[pl.BlockSpec((TM, D), row), pl.BlockSpec((TM, D), row),
                  pl.BlockSpec((None, TM, D), lambda i: (5, i, 0)), pl.BlockSpec((None, TM, D), lambda i: (6, i, 0)),
                  pl.BlockSpec((TM, D), row), vec, vec] + _square_specs((0, 1, 2)),
        out_shape=[jax.ShapeDtypeStruct((T, D), F32), jax.ShapeDtypeStruct((T, D), BF), jax.ShapeDtypeStruct((T, D), BF)],
        out_specs=[pl.BlockSpec((TM, D), row)] * 3)


def _rmsnorm_bwd(xf, g, dn):
    r = lax.rsqrt(jnp.mean(xf * xf, axis=-1, keepdims=True) + EPS)
    xr = xf * r
    gdn = dn * g
    dx = r * gdn - xr * (r * jnp.mean(gdn * xr, axis=-1, keepdims=True))
    return dx, jnp.sum(dn * xr, axis=0, keepdims=True)


def _ffn_bwd_hidden(dh, gg, uu, wbuf, off, name, comm=None):
    nf = F // FC

    def body(dh_ref, gg_ref, uu_ref, b0, dgu_ref, wd_s, sem):
        _load_ffn_weights((b0,), (off,), (wd_s,), sem)
        dhb = dh_ref[...]
        for c in range(nf):
            sl = slice(c * FC, (c + 1) * FC)
            da = _nt(dhb, wd_s[sl, :]).astype(BF)
            gb, ub = gg_ref[:, sl], uu_ref[:, sl]
            sg = _sig(gb)
            dgu_ref[0, :, sl] = (da * ub) * (sg * (1.0 + gb * (1.0 - sg)))
            dgu_ref[0, :, F + c * FC:F + (c + 1) * FC] = da * (gb * sg)

    row = lambda i: (i, 0)
    return _call(
        body, name=name, grid=(T // TM,), args=[dh, gg, uu, wbuf], comm=comm,
        in_specs=[pl.BlockSpec((TM, D), row), pl.BlockSpec((TM, F), row), pl.BlockSpec((TM, F), row), ANY],
        out_shape=[jax.ShapeDtypeStruct((1, T, 2 * F), BF)],
        out_specs=[pl.BlockSpec((1, TM, 2 * F), lambda i: (0, i, 0))],
        scratch_shapes=[pltpu.VMEM((F, D), BF), pltpu.SemaphoreType.DMA((1,))])


def _ffn_bwd_input(dgu, dh, x, g, wbufs, offs, name, comm=None, after=None):
    def body(dgu_ref, dh_ref, x_ref, g_ref, b0, b1, dx_ref, s_ref, w_s, sem):
        _load_ffn_weights((b0, b1), offs, (w_s.at[pl.ds(0, F), :], w_s.at[pl.ds(F, F), :]), sem)

        @pl.when(pl.program_id(0) == 0)
        def _():
            s_ref[...] = jnp.zeros_like(s_ref)

        dn = _nn(dgu_ref[0], w_s[...])
        dxn, dg = _rmsnorm_bwd(x_ref[...], g_ref[...], dn)
        dx_ref[...] = dh_ref[...] + dxn
        s_ref[0:1, :] += dg

    row = lambda i: (i, 0)
    return _call(
        body, name=name, grid=(T // TM,), args=[dgu, dh, x, g, *wbufs], comm=comm, after=after,
        in_specs=[pl.BlockSpec((1, TM, 2 * F), lambda i: (0, i, 0)), pl.BlockSpec((TM, D), row),
                  pl.BlockSpec((TM, D), row), pl.BlockSpec((1, D), lambda i: (0, 0)), ANY, ANY],
        out_shape=[jax.ShapeDtypeStruct((T, D), F32), jax.ShapeDtypeStruct((8, D), F32)],
        out_specs=[pl.BlockSpec((TM, D), row), pl.BlockSpec((8, D), lambda i: (0, 0))],
        scratch_shapes=[pltpu.VMEM((2 * F, D), BF), pltpu.SemaphoreType.DMA((2,))])


def _tn_matmul(lhs, rhs, tr, name, comm=None):
    ng, _, cdim = lhs.shape
    nc, nk = cdim // tr, T // TK
    if rhs.ndim == 2:
        r_spec = pl.BlockSpec((TK, D), lambda g, c, k: (k, 0))
    else:
        r_spec = pl.BlockSpec((None, TK, D), lambda g, c, k: (g, k, 0))

    def body(l_ref, r_ref, o_ref, acc):
        k = pl.program_id(2)

        @pl.when(k == 0)
        def _():
            acc[...] = jnp.zeros_like(acc)

        acc[...] += _tn(l_ref[...], r_ref[...])

        @pl.when(k == nk - 1)
        def _():
            o_ref[...] = acc[...].astype(BF)

    return _call(
        body, name=name, grid=(ng, nc, nk), args=[lhs, rhs], comm=comm,
        in_specs=[pl.BlockSpec((None, TK, tr), lambda g, c, k: (g, k, c)), r_spec],
        out_shape=[jax.ShapeDtypeStruct((ng * cdim, D), BF)],
        out_specs=[pl.BlockSpec((tr, D), lambda g, c, k: (g * nc + c, 0))],
        scratch_shapes=[pltpu.VMEM((tr, D), F32)])


def _mix_out_bwd(dh2, ya, yb, z, a1, q, lng, lnb, wsq, comm=None):
    def body(dh_ref, ya_ref, yb_ref, ga_ref, gb_ref, a1_ref, q_ref, lng_ref, lnb_ref, wa_ref, wb_ref, wo_ref,
             dzg_ref, da1_ref, dq_ref, l_ref, r_ref, s_ref):
        @pl.when(pl.program_id(0) == 0)
        def _():
            s_ref[...] = jnp.zeros_like(s_ref)

        dhb = dh_ref[...].astype(BF)
        dm = _nt(dhb, wo_ref[...]).astype(BF)
        ya, yb = ya_ref[...], yb_ref[...]
        sa, sb = _sig(ga_ref[...]), _sig(gb_ref[...])
        l_ref[0] = sa * ya + sb * yb
        l_ref[2] = q_ref[...]
        dzg_ref[0] = (dm * ya) * (sa * (1.0 - sa))
        dzg_ref[1] = (dm * yb) * (sb * (1.0 - sb))
        dya = dm * sa
        dyb = dm * sb
        r_ref[0] = dhb
        r_ref[1] = dya
        r_ref[2] = dyb
        dq_ref[...] = _nt(dyb, wb_ref[...]).astype(BF)
        da3 = _nt(dya, wa_ref[...])
        lng = lng_ref[...]
        xh, rs, a2, sg = _layernorm_silu(a1_ref[...].astype(F32), lng, lnb_ref[...])
        l_ref[1] = (a2 * sg).astype(BF)
        da2 = da3 * (sg * (1.0 + a2 * (1.0 - sg)))
        s_ref[0:1, :] += jnp.sum(da2 * xh, axis=0, keepdims=True)
        s_ref[1:2, :] += jnp.sum(da2, axis=0, keepdims=True)
        dxh = da2 * lng
        da1 = rs * (dxh - jnp.mean(dxh, axis=-1, keepdims=True) - xh * jnp.mean(dxh * xh, axis=-1, keepdims=True))
        da1_ref[...] = da1.astype(BF)
        s_ref[2:3, :] += jnp.sum(da1, axis=0, keepdims=True)

    row = lambda i: (i, 0)
    row3 = lambda i: (0, i, 0)
    vec = pl.BlockSpec((1, D), lambda i: (0, 0))
    return _call(
        body, name="mix_out_bwd", grid=(T // TM,), args=[dh2, ya, yb, z, z, a1, q, lng, lnb, wsq, wsq, wsq], comm=comm,
        in_specs=[pl.BlockSpec((TM, D), row), pl.BlockSpec((TM, D), row), pl.BlockSpec((TM, D), row),
                  pl.BlockSpec((None, TM, D), lambda i: (5, i, 0)), pl.BlockSpec((None, TM, D), lambda i: (6, i, 0)),
                  pl.BlockSpec((TM, D), row), pl.BlockSpec((TM, D), row), vec, vec] + _square_specs((0, 1, 2)),
        out_shape=[jax.ShapeDtypeStruct((2, T, D), BF), jax.ShapeDtypeStruct((T, D), BF),
                   jax.ShapeDtypeStruct((T, D), BF), jax.ShapeDtypeStruct((3, T, D), BF),
                   jax.ShapeDtypeStruct((3, T, D), BF), jax.ShapeDtypeStruct((8, D), F32)],
        out_specs=[pl.BlockSpec((2, TM, D), row3), pl.BlockSpec((TM, D), row), pl.BlockSpec((TM, D), row),
                   pl.BlockSpec((3, TM, D), row3), pl.BlockSpec((3, TM, D), row3), pl.BlockSpec((8, D), lambda i: (0, 0))])


def _mix_in_bwd(dz, dh2, h1, gm, win, comm=None):
    def body(dz_ref, dh_ref, h_ref, g_ref, *rest):
        w_any, (o_ref, ob_ref, s_ref, w_s, sem) = rest[:len(win)], rest[len(win):]
        _load_in_proj([(b, first) for b, (_, first) in zip(w_any, win)], w_s, sem)

        @pl.when(pl.program_id(0) == 0)
        def _():
            s_ref[...] = jnp.zeros_like(s_ref)

        du = _nn(dz_ref[0], w_s[0:D, :])
        for j in range(1, NG):
            du = du + _nn(dz_ref[j], w_s[j * D:(j + 1) * D, :])
        dx, dg = _rmsnorm_bwd(h_ref[...], g_ref[...], du)
        dh1 = dh_ref[...] + dx
        o_ref[...] = dh1
        ob_ref[...] = (0.5 * dh1).astype(BF)
        s_ref[0:1, :] += dg

    row = lambda i: (i, 0)
    return _call(
        body, name="mix_in_bwd", grid=(T // TM,), args=[dz, dh2, h1, gm] + [b for b, _ in win], comm=comm,
        in_specs=[pl.BlockSpec((NG, TM, D), lambda i: (0, i, 0)), pl.BlockSpec((TM, D), row),
                  pl.BlockSpec((TM, D), row), pl.BlockSpec((1, D), lambda i: (0, 0))] + [ANY] * len(win),
        out_shape=[jax.ShapeDtypeStruct((T, D), F32), jax.ShapeDtypeStruct((T, D), BF), jax.ShapeDtypeStruct((8, D), F32)],
        out_specs=[pl.BlockSpec((TM, D), row), pl.BlockSpec((TM, D), row), pl.BlockSpec((8, D), lambda i: (0, 0))],
        scratch_shapes=[pltpu.VMEM((NG * D, D), BF), pltpu.SemaphoreType.DMA((NDEV * len(win),))])


def _row_tile(n, want, mult):
    for t in range(min(want, n), 0, -1):
        if n % t == 0 and t % mult == 0:
            return t
    return n


def _pack_small(s_ffn1, s_in, s_mix, s_ffn2, s_final, dwa, dwb):
    def body(f1, mi, mo, f2, fl, wa_ref, wb_ref, v_ref, k_ref):
        for dst, (ref, row) in enumerate(((f1, 0), (mi, 0), (mo, 0), (mo, 1), (mo, 2), (f2, 0), (fl, 0), (fl, 1))):
            v_ref[dst:dst + 1, :] = ref[row:row + 1, :]
        for k in range(NDEV):
            k_ref[k, 0:32, :] = wa_ref[:, k * LANE:(k + 1) * LANE]
            k_ref[k, 32:40, :] = wb_ref[:, k * LANE:(k + 1) * LANE]

    return pl.pallas_call(
        body, name="pack_small",
        out_shape=(jax.ShapeDtypeStruct((8, D), F32), jax.ShapeDtypeStruct((NDEV, 40, LANE), F32)),
    )(s_ffn1, s_in, s_mix, s_ffn2, s_final, dwa, dwb)


def _adam_update(g, w, m, v):
    m2 = ADAM_B1 * m + (1.0 - ADAM_B1) * g
    v2 = ADAM_B2 * v + (1.0 - ADAM_B2) * (g * g)
    c1 = 1.0 - ADAM_B1 ** ADAM_STEP
    c2 = 1.0 - ADAM_B2 ** ADAM_STEP
    return -ADAM_LR * ((m2 / c1) / (jnp.sqrt(v2 / c2) + ADAM_EPS) + ADAM_WD * w), m2, v2


def _adam_small(vecs, convs, vec_params, tap_params):
    nv, nt = len(vec_params), len(tap_params)

    def body(*refs):
        v_ref, k_ref = refs[:2]
        p_refs = refs[2:2 + 3 * (nv + nt)]
        l_ref = refs[2 + 3 * (nv + nt)]
        o_refs = refs[3 + 3 * (nv + nt):]
        s, c = v_ref[0], k_ref[0]
        for k in range(1, NDEV):
            s = s + v_ref[k]
            c = c + k_ref[k]
        l_ref[...] = jnp.sum(s[7:8, :], axis=-1, keepdims=True)
        for i in range(nv):
            w_ref, m_ref, u_ref = p_refs[3 * i: 3 * i + 3]
            g_ref, d_ref, m2_ref, u2_ref = o_refs[4 * i: 4 * i + 4]
            g = s[i:i + 1, :]
            g_ref[...] = g
            d_ref[...], m2_ref[...], u2_ref[...] = _adam_update(g, w_ref[...], m_ref[...], u_ref[...])
        for i in range(nt):
            w_ref, m_ref, u_ref = p_refs[3 * (nv + i): 3 * (nv + i) + 3]
            g_ref, d_ref, m2_ref, u2_ref = o_refs[4 * (nv + i): 4 * (nv + i) + 4]
            first = tap_params[i][0]
            for k in range(w_ref.shape[0]):
                g = c[first + k:first + k + 1, :]
                g_ref[k] = g
                d_ref[k], m2_ref[k], u2_ref[k] = _adam_update(g, w_ref[k], m_ref[k], u_ref[k])

    params = [a for p in vec_params for a in p] + [a for p in tap_params for a in p[1:]]
    out_shape = [jax.ShapeDtypeStruct((1, 1), F32)]
    for p in list(vec_params) + [p[1:] for p in tap_params]:
        out_shape += [jax.ShapeDtypeStruct(p[0].shape, F32)] * 4
    outs = pl.pallas_call(body, name="adam_small", out_shape=tuple(out_shape))(vecs, convs, *params)
    groups = [tuple(outs[1 + 4 * i: 5 + 4 * i]) for i in range(nv + nt)]
    return outs[0], groups[:nv], groups[nv:]


def _adam_in_proj(parts, w, m, v, after):
    rows = w.shape[1]
    tr = _row_tile(D, 256, LANE)

    def body(*refs):
        p_refs = refs[:len(parts)]
        w_ref, m_ref, v_ref, g_ref, d_ref, m2_ref, v2_ref = refs[len(parts):]
        sums = []
        for p in p_refs:
            s = p[0].astype(F32)
            for k in range(1, p.shape[0]):
                s = s + p[k].astype(F32)
            sums.append(s)
        g = jnp.concatenate(sums, axis=0).T
        g_ref[...] = g
        d_ref[...], m2_ref[...], v2_ref[...] = _adam_update(g, w_ref[...], m_ref[...], v_ref[...])

    spec = pl.BlockSpec((tr, rows), lambda i: (i, 0))
    return _call(body, name="adam_in", grid=(D // tr,), args=list(parts) + [w, m, v], after=after,
                 in_specs=[pl.BlockSpec((p.shape[0], p.shape[1], tr), lambda i: (0, 0, i)) for p in parts] + [spec] * 3,
                 out_shape=[jax.ShapeDtypeStruct((D, rows), F32)] * 4, out_specs=[spec] * 4)


def _adam(gs, ws, ms, vs, name, after):
    n = len(gs)
    rows, cols = ws[0].shape
    tr = _row_tile(rows, min(256, rows // 2), 16)

    def body(*refs):
        for i in range(n):
            g_in, w, m, v = refs[4 * i], refs[4 * i + 1][...], refs[4 * i + 2][...], refs[4 * i + 3][...]
            g_ref, d_ref, m_ref, v_ref = refs[4 * n + 4 * i: 4 * n + 4 * i + 4]
            g = g_in[0].astype(F32)
            for k in range(1, g_in.shape[0]):
                g = g + g_in[k].astype(F32)
            g_ref[...] = g
            d_ref[...], m_ref[...], v_ref[...] = _adam_update(g, w, m, v)

    spec = pl.BlockSpec((tr, cols), lambda i: (i, 0))
    args, in_specs = [], []
    for i in range(n):
        slots, first = gs[i]
        args += [slots, ws[i], ms[i], vs[i]]
        in_specs += [pl.BlockSpec((slots.shape[0], tr, cols), lambda i, b=first // tr: (0, b + i, 0))] + [spec] * 3
    outs = _call(body, name=name, grid=(rows // tr,), args=args, in_specs=in_specs, after=after,
                 out_shape=[jax.ShapeDtypeStruct((rows, cols), F32)] * (4 * n), out_specs=[spec] * (4 * n))
    return [tuple(outs[4 * i: 4 * i + 4]) for i in range(n)]


def kernel(x, ffn1_norm, ffn1_w_gate, ffn1_w_up, ffn1_w_down, mix_norm, w_in, a_dw_w, a_dw_b, a_ln_g, a_ln_b, a_w_out, b_conv_w, b_w_out, w_o, ffn2_norm, ffn2_w_gate, ffn2_w_up, ffn2_w_down, final_norm, loss_target, m_ffn1_norm, m_ffn1_w_gate, m_ffn1_w_up, m_ffn1_w_down, m_mix_norm, m_w_in, m_a_dw_w, m_a_dw_b, m_a_ln_g, m_a_ln_b, m_a_w_out, m_b_conv_w, m_b_w_out, m_w_o, m_ffn2_norm, m_ffn2_w_gate, m_ffn2_w_up, m_ffn2_w_down, m_final_norm, v_ffn1_norm, v_ffn1_w_gate, v_ffn1_w_up, v_ffn1_w_down, v_mix_norm, v_w_in, v_a_dw_w, v_a_dw_b, v_a_ln_g, v_a_ln_b, v_a_w_out, v_b_conv_w, v_b_w_out, v_w_o, v_ffn2_norm, v_ffn2_w_gate, v_ffn2_w_up, v_ffn2_w_down, v_final_norm):
    names = ("ffn1_norm", "ffn1_w_gate", "ffn1_w_up", "ffn1_w_down", "mix_norm", "w_in", "a_dw_w", "a_dw_b",
             "a_ln_g", "a_ln_b", "a_w_out", "b_conv_w", "b_w_out", "w_o", "ffn2_norm", "ffn2_w_gate", "ffn2_w_up",
             "ffn2_w_down", "final_norm")
    w = dict(ffn1_norm=ffn1_norm, ffn1_w_gate=ffn1_w_gate, ffn1_w_up=ffn1_w_up, ffn1_w_down=ffn1_w_down,
             mix_norm=mix_norm, w_in=w_in, a_dw_w=a_dw_w, a_dw_b=a_dw_b, a_ln_g=a_ln_g, a_ln_b=a_ln_b,
             a_w_out=a_w_out, b_conv_w=b_conv_w, b_w_out=b_w_out, w_o=w_o, ffn2_norm=ffn2_norm,
             ffn2_w_gate=ffn2_w_gate, ffn2_w_up=ffn2_w_up, ffn2_w_down=ffn2_w_down, final_norm=final_norm)
    m = dict(ffn1_norm=m_ffn1_norm, ffn1_w_gate=m_ffn1_w_gate, ffn1_w_up=m_ffn1_w_up, ffn1_w_down=m_ffn1_w_down,
             mix_norm=m_mix_norm, w_in=m_w_in, a_dw_w=m_a_dw_w, a_dw_b=m_a_dw_b, a_ln_g=m_a_ln_g, a_ln_b=m_a_ln_b,
             a_w_out=m_a_w_out, b_conv_w=m_b_conv_w, b_w_out=m_b_w_out, w_o=m_w_o, ffn2_norm=m_ffn2_norm,
             ffn2_w_gate=m_ffn2_w_gate, ffn2_w_up=m_ffn2_w_up, ffn2_w_down=m_ffn2_w_down, final_norm=m_final_norm)
    v = dict(ffn1_norm=v_ffn1_norm, ffn1_w_gate=v_ffn1_w_gate, ffn1_w_up=v_ffn1_w_up, ffn1_w_down=v_ffn1_w_down,
             mix_norm=v_mix_norm, w_in=v_w_in, a_dw_w=v_a_dw_w, a_dw_b=v_a_dw_b, a_ln_g=v_a_ln_g, a_ln_b=v_a_ln_b,
             a_w_out=v_a_w_out, b_conv_w=v_b_conv_w, b_w_out=v_b_w_out, w_o=v_w_o, ffn2_norm=v_ffn2_norm,
             ffn2_w_gate=v_ffn2_w_gate, ffn2_w_up=v_ffn2_w_up, ffn2_w_down=v_ffn2_w_down, final_norm=v_final_norm)
    flat = _pack_weights(dict(wg1=ffn1_w_gate[0].T, wu1=ffn1_w_up[0].T, wd1=ffn1_w_down[0], wg2=ffn2_w_gate[0].T,
                              wu2=ffn2_w_up[0].T, wd2=ffn2_w_down[0], win=w_in[0], wa=a_w_out[0], wb=b_w_out[0],
                              wo=w_o[0]))
    cw_shard = jnp.concatenate([a_dw_w[0], jnp.zeros((1, LANE), F32), b_conv_w[0], jnp.zeros((5, LANE), F32)], axis=0)

    x2, tgt = x[0], loss_target[0]
    st_a, st_b, st_b2 = ("wg1", "wu1"), ("wd1", "win/0/2"), ("win/1/2",)
    st_c, st_d, st_e = ("wa", "wb", "wo", "wg2"), ("wu2",), ("wd2",)

    buf_a, cw = _run_comm(_join(_ag_comm(st_a, flat), _direct_comm(cw_shard, False)), "ag_ffn1")
    n1, gg1, uu1, act1, buf_b = _ffn_gate_up(x2, ffn1_norm, (buf_a, buf_a), (0, F), "ffn1_gate_up", _ag_comm(st_b, flat))
    h1, buf_b2 = _ffn_down(x2, act1, buf_b, 0, "ffn1_down", _ag_comm(st_b2, flat))
    win = ((buf_b, F), (buf_b2, 0))
    u, z, buf_c = _mix_in(h1, mix_norm, win, _ag_comm(st_c, flat))
    dft = _dft_constants()
    cw = jnp.transpose(cw, (1, 0, 2)).reshape(40, D)
    a1, q, buf_d = _conv_fwd_dft(z, cw, a_dw_b, dft, _ag_comm(st_d, flat))
    h2, ya, yb, buf_e = _mix_out(a1, q, z, h1, a_ln_g, a_ln_b, buf_c, _ag_comm(st_e, flat))
    ffn2_bufs, ffn2_offs = (buf_c, buf_d, buf_e), (3 * D, 0, 0)
    dh3, dhb3, s_final, n2, gg2, uu2, act2 = _ffn_fwd(h2, ffn2_norm, ffn2_bufs, ffn2_offs, "ffn2_fwd",
                                          final=(final_norm.reshape(1, D), tgt))

    tr_f = F // 2 if (F // 2) % LANE == 0 else F
    def pair(stage, src):
        return _rs_pair_comm(stage, src)

    def chip(stage, src, pair_buf, tag):
        return _rs_chip_comm(_pair_add(stage, src, pair_buf, "pair_add_" + tag))

    (dgu2,) = _ffn_bwd_hidden(dhb3, gg2, uu2, buf_e, 0, "ffn2_bwd_h")
    (gu2,) = _tn_matmul(dgu2, n2, tr_f, "dw_gu2")
    s2a, src2a = ("wg2", "wu2"), dict(wg2=(gu2, 0), wu2=(gu2, F))
    (gd2,) = _tn_matmul(act2, dhb3, tr_f, "dw_d2")
    s2b, src2b = ("wd2",), dict(wd2=(gd2, 0))
    dh2, s_ffn2, pair2a, pair2b = _ffn_bwd_input(dgu2, dh3, h2, ffn2_norm, (buf_c, buf_d), (3 * D, 0), "ffn2_bwd_x",
                                                 _join(pair(s2a, src2a), pair(s2b, src2b)))
    dzg, da1, dq, lsq, rsq, s_mix, recv2b = _mix_out_bwd(dh2, ya, yb, z, a1, q, a_ln_g, a_ln_b, buf_c,
                                                          chip(s2b, src2b, pair2b, "2b"))
    (gsq,) = _tn_matmul(lsq, rsq, D, "dw_square")
    ssq, srcsq = ("wa", "wb", "wo"), dict(wa=(gsq, D), wb=(gsq, 2 * D), wo=(gsq, 0))
    dz, dwa, dwb, recv2a, pairsq = _conv_bwd_dft(z, da1, dq, dzg, cw, dft,
                                                 _join(chip(s2a, src2a, pair2a, "2a"), pair(ssq, srcsq)))
    gin, recvsq = _tn_matmul(dz, u, D, "dw_in", chip(ssq, srcsq, pairsq, "sq"))
    sin_a, sin_b, srcin = ("win/0/2",), ("win/1/2",), {"win/0/2": (gin, 0), "win/1/2": (gin, 0)}
    dh1, dhb1, s_in, pairin_a, pairin_b = _mix_in_bwd(dz, dh2, h1, mix_norm, win,
                                                _join(pair(sin_a, srcin), pair(sin_b, srcin)))
    dgu1, recvin_a = _ffn_bwd_hidden(dhb1, gg1, uu1, buf_b, 0, "ffn1_bwd_h",
                                           chip(sin_a, srcin, pairin_a, "in_a"))
    gu1, recvin_b = _tn_matmul(dgu1, n1, tr_f, "dw_gu1", chip(sin_b, srcin, pairin_b, "in_b"))
    s1a, src1a = ("wg1", "wu1"), dict(wg1=(gu1, 0), wu1=(gu1, F))
    gd1, pair1a = _tn_matmul(act1, dhb1, tr_f, "dw_d1", pair(s1a, src1a))
    s1b, src1b = ("wd1",), dict(wd1=(gd1, 0))
    xchg1 = _join(chip(s1a, src1a, pair1a, "1a"), pair(s1b, src1b))
    xchg1_sems, xchg1_bufs, token = _comm_start(xchg1, "xchg_ffn1_start")
    dx, s_ffn1 = _ffn_bwd_input(dgu1, dh1, x2, ffn1_norm, (buf_a, buf_a), (0, F), "ffn1_bwd_x", after=token)
    (_, gd1), (recv1a, pair1b) = _comm_wait(xchg1, "xchg_ffn1_wait", xchg1_sems, xchg1_bufs, s_ffn1)
    src1b = dict(wd1=(gd1, 0))

    vec8, convk = _pack_small(s_ffn1, s_in, s_mix, s_ffn2, s_final, dwa, dwb)
    tail = _join(chip(s1b, src1b, pair1b, "1b"), _join(_direct_comm(vec8, False), _direct_comm(convk, True)))
    tail_sems, tail_bufs, token = _comm_start(tail, "xchg_tail_start")

    fs = F // NDEV
    g = dict(ffn1_w_gate=(recv1a, 0), ffn1_w_up=(recv1a, fs), ffn2_w_gate=(recv2a, 0), ffn2_w_up=(recv2a, fs),
             ffn2_w_down=(recv2b, 0), a_w_out=(recvsq, 0), b_w_out=(recvsq, D // NDEV), w_o=(recvsq, 2 * (D // NDEV)))
    grad, upd = {}, {}

    def run(group, name, after, as2d=lambda a: a[0], back=lambda a, n: a.reshape(w[n].shape)):
        res = _adam([g[n] for n in group], [as2d(w[n]) for n in group], [as2d(m[n]) for n in group],
                    [as2d(v[n]) for n in group], name, after)
        for n, r in zip(group, res):
            grad[n], upd[n] = back(r[0], n), tuple(back(a, n) for a in r[1:])
        return res[0][0]

    done = run(("ffn1_w_gate", "ffn1_w_up", "ffn2_w_gate", "ffn2_w_up"), "adam_gate_up", token,
               as2d=lambda a: a[0].T, back=lambda a, n: a.T[None])
    r_in = _adam_in_proj([recvin_a, recvin_b], w_in[0], m_w_in[0], v_w_in[0], done)
    grad["w_in"], upd["w_in"] = r_in[0][None], tuple(a[None] for a in r_in[1:])
    done = run(("a_w_out", "b_w_out", "w_o"), "adam_square", r_in[0])
    _, (recv1b, vec_all, conv_all) = _comm_wait(tail, "xchg_tail_wait", tail_sems, tail_bufs, done)
    g["ffn1_w_down"] = (recv1b, 0)
    run(("ffn1_w_down", "ffn2_w_down"), "adam_down", done)
    vec_names = ("ffn1_norm", "mix_norm", "a_ln_g", "a_ln_b", "a_dw_b", "ffn2_norm", "final_norm")
    tap_names, tap_rows = ("a_dw_w", "b_conv_w"), (0, 32)
    taps = lambda a: jnp.transpose(a, (1, 0, 2))
    loss, vec_res, tap_res = _adam_small(
        vec_all, conv_all, [tuple(t[n].reshape(1, D) for t in (w, m, v)) for n in vec_names],
        [(r,) + tuple(taps(t[n]) for t in (w, m, v)) for n, r in zip(tap_names, tap_rows)])
    for n, r in zip(vec_names, vec_res):
        grad[n], upd[n] = r[0].reshape(w[n].shape), tuple(a.reshape(w[n].shape) for a in r[1:])
    for n, r in zip(tap_names, tap_res):
        grad[n], upd[n] = taps(r[0]), tuple(taps(a) for a in r[1:])

    return (loss.reshape(()), dx.reshape(x.shape), *[grad[n] for n in names], *[upd[n][0] for n in names],
            *[upd[n][1] for n in names], *[upd[n][2] for n in names])
```

```python
import jax
import jax.numpy as jnp
from jax import lax
from jax.experimental import pallas as pl
from jax.experimental.pallas import tpu as pltpu

T = 4096
D = 1024
F = 2816
NG = 7
NDEV = 8
NCHIP = 4
KA, KB = 31, 3
EPS = 1e-6
ADAM_LR, ADAM_B1, ADAM_B2, ADAM_EPS, ADAM_WD, ADAM_STEP = 0.001, 0.9, 0.999, 1e-08, 0.01, 10

TM = 512
FC = 256
TB = 1024
NB = 256
HB = NB // 2
CW = 256
CHB = 64
LANE = 128
TK = 2048
VMEM_LIMIT = 56 * 1024 * 1024

BF = jnp.bfloat16
F32 = jnp.float32
MESH = pl.DeviceIdType.MESH
ANY = pl.BlockSpec(memory_space=pl.ANY)
COLLECTIVE_ID = {(1,): 0, (2, 4, 6): 1, (1, 2, 4, 6): 2, (1, 2, 4): 3, (1, 2, 3, 4, 5, 6, 7): 4}
START_COLLECTIVE_ID = {(1, 2, 4, 6): 5, (1, 2, 3, 4, 5, 6, 7): 6}

ORDER = ("wg1", "wu1", "wd1", "wg2", "wu2", "wd2", "win", "wa", "wb", "wo")


class _Layout:
    def __init__(self):
        fs, dis, ds = F // NDEV, NG * D // NDEV, D // NDEV
        self.rows = dict(wg1=fs, wu1=fs, wd1=fs, wg2=fs, wu2=fs, wd2=fs, win=dis, wa=ds, wb=ds, wo=ds)
        self.fl, off = {}, 0
        for n in ORDER:
            self.fl[n] = off
            off += self.rows[n]
        self.RT = off


class _Stage:
    def __init__(self, names):
        lay = _Layout()
        self.names = names
        self.rows, self.full, self.sub, self.fl = {}, {}, {}, {}
        for n in names:
            base, i, k = (n.split("/") + ["0", "1"])[:3]
            self.full[n] = lay.rows[base]
            self.rows[n] = lay.rows[base] // int(k)
            self.sub[n] = int(i) * self.rows[n]
            self.fl[n] = lay.fl[base] + self.sub[n]
        self.off, self.wc, o, w = {}, {}, 0, 0
        for n in names:
            self.off[n], self.wc[n] = o, w
            o += self.rows[n]
            w += NDEV * self.rows[n]
        self.R, self.W = o, w

    def grad_row(self, n, first, dev_lin):
        return first + dev_lin * self.full[n] + self.sub[n]


def _nt(a, b):
    return lax.dot_general(a, b, (((1,), (1,)), ((), ())), preferred_element_type=F32)


def _nn(a, b):
    return lax.dot_general(a, b, (((1,), (0,)), ((), ())), preferred_element_type=F32)


def _tn(a, b):
    return lax.dot_general(a, b, (((0,), (0,)), ((), ())), preferred_element_type=F32)


def _sig(x):
    return 1.0 / (1.0 + jnp.exp(-x))


def _position():
    return lax.axis_index("x"), lax.axis_index("y"), lax.axis_index("c")


def _peer(pos, j):
    x, y, c = pos
    return (1 - x if j & 4 else x, 1 - y if j & 2 else y, 1 - c if j & 1 else c)


def _lin(pos):
    return 4 * pos[0] + 2 * pos[1] + pos[2]


def _chip(pos):
    return 2 * pos[0] + pos[1]


class _Comm:
    def __init__(self, inputs, out_shapes, scratch, start, finish, middle=None, peers=None):
        self.inputs, self.out_shapes, self.scratch = inputs, out_shapes, scratch
        self.start, self.finish, self.middle = start, finish, middle
        self.peers = peers


def _handshake(peers):
    barrier = pltpu.get_barrier_semaphore()
    for j in peers:
        pl.semaphore_signal(barrier, inc=1, device_id=_peer(_position(), j), device_id_type=MESH)
    pl.semaphore_wait(barrier, len(peers))


def _call(body, *, name, grid, args, in_specs, out_shape, out_specs, scratch_shapes=(), comm=None,
          num_scalar_prefetch=0, after=None):
    in_specs, out_shape, out_specs, scratch_shapes = list(in_specs), list(out_shape), list(out_specs), list(scratch_shapes)
    if after is not None:
        inner, pos = body, num_scalar_prefetch + len(in_specs)
        body = lambda *refs: inner(*refs[:pos], *refs[pos + 1:])
        args, in_specs = list(args) + [after], in_specs + [ANY]
    n_in, n_out, n_scr = len(in_specs), len(out_shape), len(scratch_shapes)
    sp = num_scalar_prefetch
    if comm is None:
        kernel_fn = lambda *refs: body(*refs)
        c_in = c_out = c_scr = 0
    else:
        c_in, c_out, c_scr = len(comm.inputs), len(comm.out_shapes), len(comm.scratch)

        def kernel_fn(*refs):
            pre, refs = refs[:sp], refs[sp:]
            ins, cins = refs[:n_in], refs[n_in:n_in + c_in]
            o0 = n_in + c_in
            outs, couts = refs[o0:o0 + n_out], refs[o0 + n_out:o0 + n_out + c_out]
            s0 = o0 + n_out + c_out
            scr, cscr = refs[s0:s0 + n_scr], refs[s0 + n_scr:]
            step, steps = pl.program_id(0), grid[0]
            for a in range(1, len(grid)):
                step, steps = step * grid[a] + pl.program_id(a), steps * grid[a]
            first, last = step == 0, step == steps - 1

            @pl.when(first)
            def _():
                if comm.peers is not None:
                    _handshake(comm.peers)
                comm.start(cins, couts, cscr)

            if comm.middle is not None:
                @pl.when(step == (steps // 2 if steps > 2 else steps - 1))
                def _():
                    comm.middle(cins, couts, cscr)

            body(*pre, *ins, *outs, *scr)

            @pl.when(last)
            def _():
                comm.finish(cins, couts, cscr)

        args = list(args) + list(comm.inputs)
        in_specs += [ANY] * c_in
        out_shape += list(comm.out_shapes)
        out_specs += [ANY] * c_out
        scratch_shapes += list(comm.scratch)
    params = pltpu.CompilerParams(dimension_semantics=("arbitrary",) * len(grid), vmem_limit_bytes=VMEM_LIMIT,
                                  collective_id=COLLECTIVE_ID[comm.peers] if comm is not None and comm.peers else None)
    if sp:
        grid_spec = pltpu.PrefetchScalarGridSpec(num_scalar_prefetch=sp, grid=grid, in_specs=in_specs,
                                                 out_specs=out_specs, scratch_shapes=scratch_shapes)
        return pl.pallas_call(kernel_fn, name=name, grid_spec=grid_spec, out_shape=out_shape,
                              compiler_params=params)(*args)
    return pl.pallas_call(kernel_fn, name=name, grid=grid, in_specs=in_specs, out_shape=out_shape, out_specs=out_specs,
                          scratch_shapes=scratch_shapes, compiler_params=params)(*args)


def _join(a, b):
    na = (len(a.inputs), len(a.out_shapes), len(a.scratch))

    def split(refs):
        return ([r[:n] for r, n in zip(refs, na)], [r[n:] for r, n in zip(refs, na)])

    def start(*refs):
        ra, rb = split(refs)
        a.start(*ra)
        b.start(*rb)

    def finish(*refs):
        ra, rb = split(refs)
        a.finish(*ra)
        b.finish(*rb)

    def middle(*refs):
        for stage, r in zip((a, b), split(refs)):
            if stage.middle is not None:
                stage.middle(*r)

    return _Comm(list(a.inputs) + list(b.inputs), list(a.out_shapes) + list(b.out_shapes),
                 list(a.scratch) + list(b.scratch), start, finish,
                 middle if (a.middle is not None or b.middle is not None) else None,
                 peers=tuple(sorted(set(a.peers) | set(b.peers))) if a.peers and b.peers else None)


def _run_comm(comm, name):
    def body(*refs):
        c_in, c_out = len(comm.inputs), len(comm.out_shapes)
        parts = (refs[:c_in], refs[c_in:c_in + c_out], refs[c_in + c_out:])
        if comm.peers is not None:
            _handshake(comm.peers)
        comm.start(*parts)
        if comm.middle is not None:
            comm.middle(*parts)
        comm.finish(*parts)

    return pl.pallas_call(
        body, name=name, out_shape=list(comm.out_shapes), in_specs=[ANY] * len(comm.inputs),
        out_specs=[ANY] * len(comm.out_shapes), scratch_shapes=list(comm.scratch),
        compiler_params=pltpu.CompilerParams(collective_id=COLLECTIVE_ID[comm.peers] if comm.peers else None),
    )(*comm.inputs)


HBM = pl.BlockSpec(memory_space=pltpu.HBM)
SEM = pl.BlockSpec(memory_space=pltpu.SEMAPHORE)
DATAFLOW = pltpu.SideEffectType.DATAFLOW_SIDE_EFFECTING


def _comm_start(comm, name):
    c_in, c_out = len(comm.inputs), len(comm.out_shapes)
    sems = [s(()) if s is pltpu.SemaphoreType.DMA else s for s in comm.scratch]
    bufs = list(comm.inputs) + [lax.empty(s.shape, s.dtype) for s in comm.out_shapes]

    def body(*refs):
        if comm.peers is not None:
            _handshake(comm.peers)
        sem_refs = refs[c_in + c_out:c_in + c_out + len(sems)]
        comm.start(refs[:c_in], refs[c_in:c_in + c_out], sem_refs)
        refs[-1][...] = jnp.zeros_like(refs[-1])

    outs = pl.pallas_call(
        body, name=name,
        out_shape=sems + [pltpu.HBM(b.shape, b.dtype) for b in bufs] + [jax.ShapeDtypeStruct((8, LANE), F32)],
        in_specs=[HBM] * len(bufs),
        out_specs=[SEM] * len(sems) + [HBM] * len(bufs) + [pl.BlockSpec(memory_space=pltpu.VMEM)],
        input_output_aliases={i: len(sems) + i for i in range(len(bufs))},
        compiler_params=pltpu.CompilerParams(
            has_side_effects=DATAFLOW, collective_id=START_COLLECTIVE_ID[comm.peers] if comm.peers else None),
    )(*[pltpu.with_memory_space_constraint(b, pltpu.HBM) for b in bufs])
    return outs[:len(sems)], outs[len(sems):-1], outs[-1]


def _comm_wait(comm, name, sems, bufs, after):
    c_in, c_out = len(comm.inputs), len(comm.out_shapes)

    def body(*refs):
        sem_refs = refs[c_in + c_out:c_in + c_out + len(sems)]
        comm.finish(refs[:c_in], refs[c_in:c_in + c_out], sem_refs)

    outs = pl.pallas_call(
        body, name=name, out_shape=[pltpu.HBM(b.shape, b.dtype) for b in bufs],
        in_specs=[HBM] * len(bufs) + [SEM] * len(sems) + [ANY], out_specs=[HBM] * len(bufs),
        input_output_aliases={i: i for i in range(len(bufs))},
        compiler_params=pltpu.CompilerParams(has_side_effects=DATAFLOW),
    )(*bufs, *sems, after)
    return outs[:c_in], outs[c_in:]


def _ag_comm(names, flat):
    st = _Stage(names)

    def ring(me):
        x, y, c = me
        diagonal = x == y
        up = (jnp.where(diagonal, x, 1 - x), jnp.where(diagonal, 1 - y, y), c)
        down = (jnp.where(diagonal, 1 - x, x), jnp.where(diagonal, y, 1 - y), c)
        low = c == 0
        passed = tuple(jnp.where(low, d, u) for d, u in zip(down, up))
        target = tuple(jnp.where(low, u, d) for d, u in zip(down, up))
        return up, down, (1 - x, 1 - y, c), passed, target

    def parts(refs):
        (flat_ref,), (out_ref,), (send_sems, recv_sems, local_sem) = refs
        me = _position()

        def region(name, dev):
            r = st.rows[name]
            return out_ref.at[pl.ds(st.wc[name] + _lin(dev) * r, r), :]

        def own(name):
            return flat_ref.at[pl.ds(st.fl[name], st.rows[name]), :]

        def copies(k, dev, to, from_flat):
            return [pltpu.make_async_remote_copy(
                src_ref=own(n) if from_flat else region(n, dev), dst_ref=region(n, dev), send_sem=send_sems.at[k],
                recv_sem=recv_sems.at[k], device_id=to, device_id_type=MESH) for n in names]

        def whole(k):
            return pltpu.make_async_remote_copy(
                src_ref=flat_ref.at[pl.ds(0, st.R), :], dst_ref=out_ref.at[pl.ds(0, st.R), :],
                send_sem=send_sems.at[k], recv_sem=recv_sems.at[k], device_id=me, device_id_type=MESH)

        return me, region, own, copies, whole, flat_ref, out_ref, local_sem

    def start(*refs):
        me, region, own, copies, _, _, _, local_sem = parts(refs)
        for n in names:
            pltpu.make_async_copy(own(n), region(n, me), local_sem).start()
        up, down, _, _, _ = ring(me)
        for k, to in ((1, up), (2, down), (0, _peer(me, 1))):
            for cp in copies(k, me, to, True):
                cp.start()

    def middle(*refs):
        me, _, _, copies, whole, _, _, _ = parts(refs)
        up, down, _, passed, target = ring(me)
        sib = _peer(me, 1)
        whole(1).wait_recv()
        whole(2).wait_recv()
        for k, dev, to in ((3, passed, target), (4, down, sib), (5, up, sib)):
            for cp in copies(k, dev, to, False):
                cp.start()

    def finish(*refs):
        me, _, _, copies, whole, flat_ref, out_ref, local_sem = parts(refs)
        _, _, across, _, _ = ring(me)
        whole(3).wait_recv()
        for cp in copies(6, across, _peer(me, 1), False):
            cp.start()
        whole(0).wait_recv()
        for j in range(3):
            whole(4 + j).wait_recv()
        for k in range(7):
            whole(k).wait_send()
        pltpu.make_async_copy(flat_ref.at[pl.ds(0, st.R), :], out_ref.at[pl.ds(0, st.R), :], local_sem).wait()

    return _Comm([flat], [jax.ShapeDtypeStruct((st.W, D), BF)],
                 [pltpu.SemaphoreType.DMA((7,)), pltpu.SemaphoreType.DMA((7,)), pltpu.SemaphoreType.DMA],
                 start, finish, middle, peers=(1, 2, 4))


def _rs_pair_comm(names, src):
    st = _Stage(names)
    arrays = []
    for n in names:
        if not any(src[n][0] is a for a in arrays):
            arrays.append(src[n][0])
    idx = {n: [i for i, a in enumerate(arrays) if a is src[n][0]][0] for n in names}

    def slot_wait(refs):
        recv = refs[1][0]
        send_sem, recv_sem = refs[2]
        return pltpu.make_async_remote_copy(src_ref=recv, dst_ref=recv, send_sem=send_sem, recv_sem=recv_sem,
                                            device_id=_position(), device_id_type=MESH)

    def start(*refs):
        ins, (recv,), (send_sem, recv_sem) = refs
        me = _position()
        sib = _peer(me, 1)
        for q in range(NCHIP):
            dev = (q // 2, q % 2, sib[2])
            for n in names:
                r = st.rows[n]
                pltpu.make_async_remote_copy(
                    src_ref=ins[idx[n]].at[pl.ds(st.grad_row(n, src[n][1], _lin(dev)), r), :],
                    dst_ref=recv.at[q, pl.ds(st.off[n], r), :], send_sem=send_sem, recv_sem=recv_sem,
                    device_id=sib, device_id_type=MESH).start()

    def finish(*refs):
        w = slot_wait(refs)
        w.wait_recv()
        w.wait_send()

    return _Comm(arrays, [jax.ShapeDtypeStruct((NCHIP, st.R, D), BF)],
                 [pltpu.SemaphoreType.DMA, pltpu.SemaphoreType.DMA], start, finish, peers=(1,))


def _pair_add(names, src, recv, name):
    st = _Stage(names)
    c_arr = jnp.reshape(lax.axis_index("c"), (1,)).astype(jnp.int32)

    def body(c_ref, *refs):
        r_ref, o_ref = refs[len(names)], refs[len(names) + 1]
        for a_ref, n in zip(refs, names):
            rows = slice(st.off[n], st.off[n] + st.rows[n])
            o_ref[rows, :] = (a_ref[...].astype(F32) + r_ref[rows, :].astype(F32)).astype(BF)

    def shard_spec(n):
        r = st.rows[n]
        base, step = st.grad_row(n, src[n][1], 0) // r, st.full[n] // r
        return pl.BlockSpec((r, D), lambda q, c_ref: (base + step * (2 * q + c_ref[0]), 0))

    slot = pl.BlockSpec((None, st.R, D), lambda q, c_ref: (q, 0, 0))
    return _call(body, name=name, grid=(NCHIP,), args=[c_arr] + [src[n][0] for n in names] + [recv],
                 in_specs=[shard_spec(n) for n in names] + [slot],
                 out_shape=[jax.ShapeDtypeStruct((NCHIP, st.R, D), BF)], out_specs=[slot], num_scalar_prefetch=1)[0]


def _rs_chip_comm(part):
    def copies(refs):
        (p_ref,), (recv,), (send_sems, recv_sems, local_sem) = refs
        me = _position()
        mine = pltpu.make_async_copy(p_ref.at[_chip(me)], recv.at[_chip(me)], local_sem)
        out = []
        for j, bits in enumerate((4, 2, 6)):
            to = _peer(me, bits)
            out.append(pltpu.make_async_remote_copy(
                src_ref=p_ref.at[_chip(to)], dst_ref=recv.at[_chip(me)], send_sem=send_sems.at[j],
                recv_sem=recv_sems.at[j], device_id=to, device_id_type=MESH))
        return mine, out

    def start(*refs):
        mine, out = copies(refs)
        mine.start()
        for cp in out:
            cp.start()

    def finish(*refs):
        mine, out = copies(refs)
        for cp in out:
            cp.wait_recv()
        for cp in out:
            cp.wait_send()
        mine.wait()

    return _Comm([part], [jax.ShapeDtypeStruct(part.shape, BF)],
                 [pltpu.SemaphoreType.DMA((3,)), pltpu.SemaphoreType.DMA((3,)), pltpu.SemaphoreType.DMA],
                 start, finish, peers=(2, 4, 6))


def _direct_comm(x, scatter):
    def copies(refs):
        (x_ref,), (out_ref,), (send_sems, recv_sems, local_sem) = refs
        me = _position()

        def piece(dev):
            return x_ref.at[_lin(dev)] if scatter else x_ref

        mine = pltpu.make_async_copy(piece(me), out_ref.at[_lin(me)], local_sem)
        return mine, [pltpu.make_async_remote_copy(
            src_ref=piece(_peer(me, j)), dst_ref=out_ref.at[_lin(me)], send_sem=send_sems.at[j - 1],
            recv_sem=recv_sems.at[j - 1], device_id=_peer(me, j), device_id_type=MESH) for j in range(1, NDEV)]

    def start(*refs):
        mine, cps = copies(refs)
        mine.start()
        for cp in cps:
            cp.start()

    def finish(*refs):
        mine, cps = copies(refs)
        for cp in cps:
            cp.wait_recv()
        for cp in cps:
            cp.wait_send()
        mine.wait()

    shape = x.shape if scatter else (NDEV,) + x.shape
    return _Comm([x], [jax.ShapeDtypeStruct(shape, x.dtype)],
                 [pltpu.SemaphoreType.DMA((7,)), pltpu.SemaphoreType.DMA((7,)), pltpu.SemaphoreType.DMA],
                 start, finish, peers=(1, 2, 3, 4, 5, 6, 7))


def _pack_weights(shards):
    lay = _Layout()

    def body(*refs):
        o_ref = refs[-1]
        for ref, n in zip(refs, ORDER):
            x = ref[...].T if n == "win" else ref[...]
            o_ref[lay.fl[n]:lay.fl[n] + lay.rows[n], :] = x.astype(BF)

    return pl.pallas_call(
        body, name="pack_weights", out_shape=jax.ShapeDtypeStruct((lay.RT, D), BF),
        compiler_params=pltpu.CompilerParams(vmem_limit_bytes=VMEM_LIMIT))(*[shards[n] for n in ORDER])


def _load_ffn_weights(srcs, offs, scratch, sem):
    @pl.when(pl.program_id(0) == 0)
    def _():
        cps = [pltpu.make_async_copy(s.at[pl.ds(off, dst.shape[0]), :], dst, sem.at[i])
               for i, (s, off, dst) in enumerate(zip(srcs, offs, scratch))]
        for cp in cps:
            cp.start()
        for cp in cps:
            cp.wait()


def _final_loss_tile(xf, g, tgt, s_ref):
    r = lax.rsqrt(jnp.mean(xf * xf, axis=-1, keepdims=True) + EPS)
    xr = xf * r
    e = xr * g - tgt
    s_ref[1:2, :] += jnp.sum(e * e, axis=0, keepdims=True) * (0.5 / D)
    dy = e * (1.0 / D)
    s_ref[0:1, :] += jnp.sum(dy * xr, axis=0, keepdims=True)
    gdy = dy * g
    return r * gdy - xr * (r * jnp.mean(gdy * xr, axis=-1, keepdims=True))


def _ffn_fwd(x, g, wbufs, offs, name, comm=None, final=None):
    nf = F // FC

    def body(x_ref, g_ref, b0, b1, b2, *rest):
        if final is None:
            h_ref, n_ref, gg_ref, uu_ref, a_ref, wg_s, wu_s, wd_s, sem = rest
        else:
            gf_ref, t_ref, dh_ref, dhb_ref, s_ref, n_ref, gg_ref, uu_ref, a_ref, wg_s, wu_s, wd_s, sem = rest

            @pl.when(pl.program_id(0) == 0)
            def _():
                s_ref[...] = jnp.zeros_like(s_ref)

        _load_ffn_weights((b0, b1, b2), offs, (wg_s, wu_s, wd_s), sem)
        xf = x_ref[...]
        r = lax.rsqrt(jnp.mean(xf * xf, axis=-1, keepdims=True) + EPS)
        nb = (xf * r * g_ref[...]).astype(BF)
        n_ref[...] = nb
        acc = jnp.zeros((TM, D), F32)
        for c in range(nf):
            sl = slice(c * FC, (c + 1) * FC)
            gb = _nt(nb, wg_s[sl, :]).astype(BF)
            ub = _nt(nb, wu_s[sl, :]).astype(BF)
            gg_ref[:, sl] = gb
            uu_ref[:, sl] = ub
            a = (gb * _sig(gb)) * ub
            a_ref[0, :, sl] = a
            acc = acc + _nn(a, wd_s[sl, :])
        h = xf + 0.5 * acc
        if final is None:
            h_ref[...] = h
        else:
            dh = _final_loss_tile(h, gf_ref[...], t_ref[...], s_ref)
            dh_ref[...] = dh
            dhb_ref[...] = (0.5 * dh).astype(BF)

    row = lambda i: (i, 0)
    vec = pl.BlockSpec((1, D), lambda i: (0, 0))
    tile = pl.BlockSpec((TM, D), row)
    saved_shapes = [jax.ShapeDtypeStruct((T, D), BF), jax.ShapeDtypeStruct((T, F), BF), jax.ShapeDtypeStruct((T, F), BF),
                    jax.ShapeDtypeStruct((1, T, F), BF)]
    saved_specs = [tile, pl.BlockSpec((TM, F), row), pl.BlockSpec((TM, F), row),
                   pl.BlockSpec((1, TM, F), lambda i: (0, i, 0))]
    if final is None:
        extra_args, extra_specs = [], []
        head_shapes, head_specs = [jax.ShapeDtypeStruct((T, D), F32)], [tile]
    else:
        extra_args, extra_specs = list(final), [vec, tile]
        head_shapes = [jax.ShapeDtypeStruct((T, D), F32), jax.ShapeDtypeStruct((T, D), BF), jax.ShapeDtypeStruct((8, D), F32)]
        head_specs = [tile, tile, pl.BlockSpec((8, D), lambda i: (0, 0))]
    return _call(
        body, name=name, grid=(T // TM,), args=[x, g, *wbufs, *extra_args], comm=comm,
        in_specs=[tile, vec, ANY, ANY, ANY] + extra_specs,
        out_shape=head_shapes + saved_shapes, out_specs=head_specs + saved_specs,
        scratch_shapes=[pltpu.VMEM((F, D), BF)] * 3 + [pltpu.SemaphoreType.DMA((3,))])


def _ffn_gate_up(x, g, wbufs, offs, name, comm=None):
    nf = F // FC

    def body(x_ref, g_ref, b0, b1, n_ref, gg_ref, uu_ref, a_ref, wg_s, wu_s, sem):
        _load_ffn_weights((b0, b1), offs, (wg_s, wu_s), sem)
        xf = x_ref[...]
        r = lax.rsqrt(jnp.mean(xf * xf, axis=-1, keepdims=True) + EPS)
        nb = (xf * r * g_ref[...]).astype(BF)
        n_ref[...] = nb
        for c in range(nf):
            sl = slice(c * FC, (c + 1) * FC)
            gb = _nt(nb, wg_s[sl, :]).astype(BF)
            ub = _nt(nb, wu_s[sl, :]).astype(BF)
            gg_ref[:, sl] = gb
            uu_ref[:, sl] = ub
            a_ref[0, :, sl] = (gb * _sig(gb)) * ub

    row = lambda i: (i, 0)
    tile = pl.BlockSpec((TM, D), row)
    return _call(
        body, name=name, grid=(T // TM,), args=[x, g, *wbufs], comm=comm,
        in_specs=[tile, pl.BlockSpec((1, D), lambda i: (0, 0)), ANY, ANY],
        out_shape=[jax.ShapeDtypeStruct((T, D), BF), jax.ShapeDtypeStruct((T, F), BF), jax.ShapeDtypeStruct((T, F), BF),
                   jax.ShapeDtypeStruct((1, T, F), BF)],
        out_specs=[tile, pl.BlockSpec((TM, F), row), pl.BlockSpec((TM, F), row),
                   pl.BlockSpec((1, TM, F), lambda i: (0, i, 0))],
        scratch_shapes=[pltpu.VMEM((F, D), BF)] * 2 + [pltpu.SemaphoreType.DMA((2,))])


def _ffn_down(x, act, wbuf, off, name, comm=None):
    def body(x_ref, a_ref, b0, h_ref, wd_s, sem):
        _load_ffn_weights((b0,), (off,), (wd_s,), sem)
        h_ref[...] = x_ref[...] + 0.5 * _nn(a_ref[0], wd_s[...])

    tile = pl.BlockSpec((TM, D), lambda i: (i, 0))
    return _call(
        body, name=name, grid=(T // TM,), args=[x, act, wbuf], comm=comm,
        in_specs=[tile, pl.BlockSpec((1, TM, F), lambda i: (0, i, 0)), ANY],
        out_shape=[jax.ShapeDtypeStruct((T, D), F32)], out_specs=[tile],
        scratch_shapes=[pltpu.VMEM((F, D), BF), pltpu.SemaphoreType.DMA((1,))])


def _load_in_proj(parts, w_s, sem):
    @pl.when(pl.program_id(0) == 0)
    def _():
        shard = NG * D // NDEV
        rows = shard // len(parts)
        cps = [pltpu.make_async_copy(buf.at[pl.ds(first + k * rows, rows), :],
                                     w_s.at[pl.ds(k * shard + p * rows, rows), :], sem.at[p * NDEV + k])
               for p, (buf, first) in enumerate(parts) for k in range(NDEV)]
        for cp in cps:
            cp.start()
        for cp in cps:
            cp.wait()


def _mix_in(h1, gm, win, comm=None):
    def body(h_ref, g_ref, *rest):
        w_any, (u_ref, z_ref, w_s, sem) = rest[:len(win)], rest[len(win):]
        _load_in_proj([(b, first) for b, (_, first) in zip(w_any, win)], w_s, sem)
        xf = h_ref[...]
        r = lax.rsqrt(jnp.mean(xf * xf, axis=-1, keepdims=True) + EPS)
        ub = (xf * r * g_ref[...]).astype(BF)
        u_ref[...] = ub
        for j in range(NG):
            z_ref[j] = _nt(ub, w_s[j * D:(j + 1) * D, :]).astype(BF)

    row = lambda i: (i, 0)
    return _call(
        body, name="mix_in", grid=(T // TM,), args=[h1, gm] + [b for b, _ in win], comm=comm,
        in_specs=[pl.BlockSpec((TM, D), row), pl.BlockSpec((1, D), lambda i: (0, 0))] + [ANY] * len(win),
        out_shape=[jax.ShapeDtypeStruct((T, D), BF), jax.ShapeDtypeStruct((NG, T, D), BF)],
        out_specs=[pl.BlockSpec((TM, D), row), pl.BlockSpec((NG, TM, D), lambda i: (0, i, 0))],
        scratch_shapes=[pltpu.VMEM((NG * D, D), BF), pltpu.SemaphoreType.DMA((NDEV * len(win),))])


def _shift_up(w, b):
    return w if b == 0 else pltpu.roll(w, w.shape[0] - b, 0)


def _fold8(p):
    red = p[0:8, :]
    for i in range(1, p.shape[0] // 8):
        red = red + p[8 * i:8 * i + 8, :]
    return red


def _dft_constants():
    import numpy as np
    nh = NB // 2
    f, n = np.arange(nh)[:, None], np.arange(NB)[None, :]
    ang = 2.0 * np.pi / NB * f * n
    fc = np.cos(ang)
    fs = np.where(f == 0, (-1.0) ** n, np.sin(ang))
    scale = np.where(f == 0, 1.0, 2.0) / NB
    ic = (scale * np.cos(ang)).T
    isn = np.where(f == 0, (-1.0) ** n / NB, scale * np.sin(ang)).T
    d = (KA - 1 - np.arange(32))[None, :]
    valid = (np.arange(32) < KA)[None, :]
    angk = 2.0 * np.pi / NB * f * d
    kc = np.where(valid, np.cos(angk), 0.0)
    ks = np.where(valid, np.sin(angk), 0.0)
    k2 = np.where(valid, np.where(f == 0, (-1.0) ** d, np.cos(angk)), 0.0)
    rtc = np.where(valid, scale * np.cos(angk), 0.0).T
    rts = np.where(valid, np.where(f == 0, (-1.0) ** d / NB, scale * np.sin(angk)), 0.0).T

    def bf(a):
        return jnp.asarray(a, F32).astype(BF)

    def split(a):
        hi = bf(a)
        return hi, (jnp.asarray(a, F32) - hi.astype(F32)).astype(BF)

    return dict(fc=bf(fc), fs=bf(fs), ic_hi=bf(ic[HB:]), is_hi=bf(isn[HB:]), ic_lo=bf(ic[:HB]), is_lo=bf(isn[:HB]),
                kc=split(kc), ks=split(ks), k2=split(k2), rtc=split(rtc), rts=split(rts))


def _dot3(m_hi, m_lo, x):
    x_hi = x.astype(BF)
    x_lo = (x - x_hi.astype(F32)).astype(BF)
    return _nn(m_hi, x_hi) + _nn(m_hi, x_lo) + _nn(m_lo, x_hi)


def _whole(a):
    return pl.BlockSpec(a.shape, lambda c, t: (0,) * a.ndim)


def _filter_spectrum(cw_ref, tabs, hc, hs, h2):
    w32 = cw_ref[0:32, :]
    for (hi, lo), dst in zip(tabs, (hc, hs, h2)):
        dst[...] = _dot3(hi[...], lo[...], w32)


def _conv_fwd_dft(z, cw, bias, dft, comm=None):
    nt = T // TB
    hb = TB // HB

    def body(z_ref, zh_ref, cw_ref, b_ref, fc_ref, fs_ref, ic_ref, is_ref, kch, kcl, ksh, ksl, k2h, k2l,
             a1_ref, q_ref, aext, ppad, hc, hs, h2):
        first = pl.program_id(1) == 0
        f = lambda ref, j: ref[j].astype(F32)

        @pl.when(first)
        def _():
            _filter_spectrum(cw_ref, ((kch, kcl), (ksh, ksl), (k2h, k2l)), hc, hs, h2)

        aext[0:HB, :] = jnp.where(first, 0.0, f(zh_ref, 0) * _sig(f(zh_ref, 1))).astype(BF)
        aext[HB:, :] = (f(z_ref, 0) * _sig(f(z_ref, 1))).astype(BF)
        ppad[0:8, :] = jnp.where(first, 0.0, f(zh_ref, 3)[HB - 8:HB, :] * f(zh_ref, 4)[HB - 8:HB, :])
        ppad[8:, :] = f(z_ref, 3) * f(z_ref, 4)
        bias_row = b_ref[...]

        for j in range(TB // HB):
            xs = aext[j * HB:j * HB + NB, :]
            xa, xb = _nn(fc_ref[...], xs), _nn(fs_ref[...], xs)
            yc = (hc[...] * xa - hs[...] * xb).astype(BF)
            ys = (h2[...] * xb + hs[...] * xa).astype(BF)
            y = _nn(ic_ref[...], yc) + _nn(is_ref[...], ys)
            a1_ref[j * HB:(j + 1) * HB, :] = (y + bias_row).astype(BF)

        def chunk(r, carry):
            base = pl.multiple_of(r * CHB, CHB)
            pw = ppad[pl.ds(base, CHB + 8), :]
            v = (cw_ref[pl.ds(32, 1), :] * _shift_up(pw, 6)[0:CHB, :]
                 + cw_ref[pl.ds(33, 1), :] * _shift_up(pw, 7)[0:CHB, :]
                 + cw_ref[pl.ds(34, 1), :] * pw[8:8 + CHB, :])
            q_ref[pl.ds(base, CHB), :] = (z_ref[2, pl.ds(base, CHB), :].astype(F32) * v).astype(BF)
            return carry

        lax.fori_loop(0, TB // CHB, chunk, 0)

    blk = pl.BlockSpec((TB, CW), lambda c, t: (t, c))
    tabs = [dft["fc"], dft["fs"], dft["ic_hi"], dft["is_hi"], *dft["kc"], *dft["ks"], *dft["k2"]]
    return _call(
        body, name="conv_fwd", grid=(D // CW, nt), comm=comm, args=[z, z, cw, bias] + tabs,
        in_specs=[pl.BlockSpec((5, TB, CW), lambda c, t: (0, t, c)),
                  pl.BlockSpec((5, HB, CW), lambda c, t: (0, jnp.maximum(t * hb - 1, 0), c)),
                  pl.BlockSpec((40, CW), lambda c, t: (0, c)), pl.BlockSpec((1, CW), lambda c, t: (0, c))]
                 + [_whole(a) for a in tabs],
        out_shape=[jax.ShapeDtypeStruct((T, D), BF), jax.ShapeDtypeStruct((T, D), BF)], out_specs=[blk, blk],
        scratch_shapes=[pltpu.VMEM((TB + HB, CW), BF), pltpu.VMEM((TB + 8, CW), F32)]
                       + [pltpu.VMEM((NB // 2, CW), F32)] * 3)


def _conv_bwd_dft(z, da1, dq, dzg, cw, dft, comm=None):
    nt = T // TB
    hb = TB // HB
    last_h = T // HB - 1

    def body(z_ref, zp_ref, zn_ref, da1_ref, da1n_ref, dq_ref, dqn_ref, dzg_ref, cw_ref,
             fc_ref, fs_ref, ic_ref, is_ref, kch, kcl, ksh, ksl, k2h, k2l, rch, rcl, rsh, rsl,
             dz_ref, dwa_ref, dwb_ref, aext, dyext, ppad, dvpad, hc, hs, h2, rc, rs, nyq, acc_b):
        t = pl.program_id(1)
        first, last = t == 0, t == nt - 1
        f = lambda ref, j: ref[j].astype(F32)

        @pl.when(first)
        def _():
            _filter_spectrum(cw_ref, ((kch, kcl), (ksh, ksl), (k2h, k2l)), hc, hs, h2)
            rc[...] = jnp.zeros_like(rc)
            rs[...] = jnp.zeros_like(rs)
            nyq[...] = jnp.zeros_like(nyq)
            acc_b[...] = jnp.zeros_like(acc_b)

        aext[0:HB, :] = jnp.where(first, 0.0, f(zp_ref, 0) * _sig(f(zp_ref, 1))).astype(BF)
        aext[HB:, :] = (f(z_ref, 0) * _sig(f(z_ref, 1))).astype(BF)
        dyext[0:TB, :] = da1_ref[...]
        dyext[TB:, :] = jnp.where(last, 0.0, da1n_ref[...].astype(F32)).astype(BF)
        ppad[0:8, :] = jnp.where(first, 0.0, f(zp_ref, 3)[HB - 8:HB, :] * f(zp_ref, 4)[HB - 8:HB, :])
        ppad[8:, :] = f(z_ref, 3) * f(z_ref, 4)
        dvpad[0:TB, :] = dq_ref[...].astype(F32) * f(z_ref, 2)
        dvpad[TB:, :] = jnp.where(last, 0.0, dqn_ref[...].astype(F32)[0:8, :] * f(zn_ref, 2)[0:8, :])

        for j in range(TB // HB):
            rows = slice(j * HB, (j + 1) * HB)
            dys = dyext[j * HB:j * HB + NB, :]
            da, db = _nn(fc_ref[...], dys), _nn(fs_ref[...], dys)
            gc = (hc[...] * da + hs[...] * db).astype(BF)
            gs = (h2[...] * db - hs[...] * da).astype(BF)
            da0 = _nn(ic_ref[...], gc) + _nn(is_ref[...], gs)
            z0, z1 = z_ref[0, rows, :].astype(F32), z_ref[1, rows, :].astype(F32)
            s1 = _sig(z1)
            dz_ref[0, rows, :] = (da0 * s1).astype(BF)
            dz_ref[1, rows, :] = (da0 * z0 * (s1 * (1.0 - s1))).astype(BF)
            xs = aext[j * HB:j * HB + NB, :]
            xa, xb = _nn(fc_ref[...], xs), _nn(fs_ref[...], xs)
            dyb = dyext[rows, :]
            pa, pb = _nn(fc_ref[:, HB:NB], dyb), _nn(fs_ref[:, HB:NB], dyb)
            rc[...] += pa * xa + pb * xb
            rs[...] += pb * xa - pa * xb
            nyq[...] += pb[0:8, :] * xb[0:8, :]

        def chunk(r, carry):
            base = pl.multiple_of(r * CHB, CHB)
            rows = pl.ds(base, CHB)
            pw = ppad[pl.ds(base, CHB + 8), :]
            p6 = _shift_up(pw, 6)[0:CHB, :]
            p7 = _shift_up(pw, 7)[0:CHB, :]
            p8 = pw[8:8 + CHB, :]
            wb0, wb1, wb2 = cw_ref[pl.ds(32, 1), :], cw_ref[pl.ds(33, 1), :], cw_ref[pl.ds(34, 1), :]
            v = wb0 * p6 + wb1 * p7 + wb2 * p8
            dz_ref[2, rows, :] = (dq_ref[rows, :].astype(F32) * v).astype(BF)
            dvw = dvpad[pl.ds(base, CHB + 8), :]
            dvc = dvw[0:CHB, :]
            dp = wb2 * dvc + wb1 * _shift_up(dvw, 1)[0:CHB, :] + wb0 * _shift_up(dvw, 2)[0:CHB, :]
            dz_ref[3, rows, :] = (dp * z_ref[4, rows, :].astype(F32)).astype(BF)
            dz_ref[4, rows, :] = (dp * z_ref[3, rows, :].astype(F32)).astype(BF)
            acc_b[0:8, :] += _fold8(dvc * p6)
            acc_b[8:16, :] += _fold8(dvc * p7)
            acc_b[16:24, :] += _fold8(dvc * p8)
            dz_ref[5, rows, :] = dzg_ref[0, rows, :]
            dz_ref[6, rows, :] = dzg_ref[1, rows, :]
            return carry

        lax.fori_loop(0, TB // CHB, chunk, 0)

        @pl.when(last)
        def _():
            row0 = lax.broadcasted_iota(jnp.int32, (NB // 2, CW), 0) == 0
            ny = jnp.broadcast_to(nyq[0:1, :], (NB // 2, CW))
            rcv = jnp.where(row0, rc[...] - ny, rc[...])
            rsv = jnp.where(row0, ny, rs[...])
            dwa_ref[...] = _dot3(rch[...], rcl[...], rcv) + _dot3(rsh[...], rsl[...], rsv)
            for k in range(KB):
                dwb_ref[k:k + 1, :] = jnp.sum(acc_b[8 * k:8 * k + 8, :], axis=0, keepdims=True)
            dwb_ref[KB:8, :] = jnp.zeros((8 - KB, CW), F32)

    blk = lambda c, t: (t, c)
    nxt = lambda c, t: (jnp.minimum((t + 1) * hb, last_h), c)
    tabs = [dft["fc"], dft["fs"], dft["ic_lo"], dft["is_lo"], *dft["kc"], *dft["ks"], *dft["k2"], *dft["rtc"], *dft["rts"]]
    return _call(
        body, name="conv_bwd", grid=(D // CW, nt), comm=comm, args=[z, z, z, da1, da1, dq, dq, dzg, cw] + tabs,
        in_specs=[pl.BlockSpec((5, TB, CW), lambda c, t: (0, t, c)),
                  pl.BlockSpec((5, HB, CW), lambda c, t: (0, jnp.maximum(t * hb - 1, 0), c)),
                  pl.BlockSpec((5, HB, CW), lambda c, t: (0, jnp.minimum((t + 1) * hb, last_h), c)),
                  pl.BlockSpec((TB, CW), blk), pl.BlockSpec((HB, CW), nxt),
                  pl.BlockSpec((TB, CW), blk), pl.BlockSpec((HB, CW), nxt),
                  pl.BlockSpec((2, TB, CW), lambda c, t: (0, t, c)),
                  pl.BlockSpec((40, CW), lambda c, t: (0, c))]
                 + [_whole(a) for a in tabs],
        out_shape=[jax.ShapeDtypeStruct((NG, T, D), BF), jax.ShapeDtypeStruct((32, D), F32),
                   jax.ShapeDtypeStruct((8, D), F32)],
        out_specs=[pl.BlockSpec((NG, TB, CW), lambda c, t: (0, t, c)),
                   pl.BlockSpec((32, CW), lambda c, t: (0, c)), pl.BlockSpec((8, CW), lambda c, t: (0, c))],
        scratch_shapes=[pltpu.VMEM((TB + HB, CW), BF), pltpu.VMEM((TB + HB, CW), BF),
                        pltpu.VMEM((TB + 8, CW), F32), pltpu.VMEM((TB + 8, CW), F32)]
                       + [pltpu.VMEM((NB // 2, CW), F32)] * 5 + [pltpu.VMEM((8, CW), F32), pltpu.VMEM((24, CW), F32)])


def _layernorm_silu(a1, lng, lnb):
    mu = jnp.mean(a1, axis=-1, keepdims=True)
    xc = a1 - mu
    rs = lax.rsqrt(jnp.mean(xc * xc, axis=-1, keepdims=True) + EPS)
    xh = xc * rs
    a2 = xh * lng + lnb
    sg = _sig(a2)
    return xh, rs, a2, sg


def _square_specs(blocks):
    return [pl.BlockSpec((D, D), lambda i, b=b: (b, 0)) for b in blocks]


def _mix_out(a1, q, z, h1, lng, lnb, wsq, comm=None):
    def body(a1_ref, q_ref, ga_ref, gb_ref, h_ref, lng_ref, lnb_ref, wa_ref, wb_ref, wo_ref, h2_ref, ya_ref, yb_ref):
        _, _, a2, sg = _layernorm_silu(a1_ref[...].astype(F32), lng_ref[...], lnb_ref[...])
        ya = _nn((a2 * sg).astype(BF), wa_ref[...])
        yb = _nn(q_ref[...], wb_ref[...])
        ya_ref[...] = ya.astype(BF)
        yb_ref[...] = yb.astype(BF)
        m = _sig(ga_ref[...].astype(F32)) * ya + _sig(gb_ref[...].astype(F32)) * yb
        h2_ref[...] = h_ref[...] + _nn(m.astype(BF), wo_ref[...])

    row = lambda i: (i, 0)
    vec = pl.BlockSpec((1, D), lambda i: (0, 0))
    return _call(
        body, name="mix_out", grid=(T // TM,), args=[a1, q, z, z, h1, lng, lnb, wsq, wsq, wsq], comm=comm,
        in_specs=[pl.BlockSpec((TM, D), row), pl.BlockSpec((TM, D), row),
                  pl.BlockSpec((None, TM, D), lambda i: (5, i, 0)), pl.BlockSpec((None, TM, D), lambda i: (6, i, 0)),
                  pl.BlockSpec((TM, D), row), vec, vec] + _square_specs((0, 1, 2)),
        out_shape=[jax.ShapeDtypeStruct((T, D), F32), jax.ShapeDtypeStruct((T, D), BF), jax.ShapeDtypeStruct((T, D), BF)],
        out_specs=[pl.BlockSpec((TM, D), row)] * 3)


def _rmsnorm_bwd(xf, g, dn):
    r = lax.rsqrt(jnp.mean(xf * xf, axis=-1, keepdims=True) + EPS)
    xr = xf * r
    gdn = dn * g
    dx = r * gdn - xr * (r * jnp.mean(gdn * xr, axis=-1, keepdims=True))
    return dx, jnp.sum(dn * xr, axis=0, keepdims=True)


def _ffn_bwd_hidden(dh, gg, uu, wbuf, off, name, comm=None):
    nf = F // FC

    def body(dh_ref, gg_ref, uu_ref, b0, dgu_ref, wd_s, sem):
        _load_ffn_weights((b0,), (off,), (wd_s,), sem)
        dhb = dh_ref[...]
        for c in range(nf):
            sl = slice(c * FC, (c + 1) * FC)
            da = _nt(dhb, wd_s[sl, :]).astype(BF)
            gb, ub = gg_ref[:, sl], uu_ref[:, sl]
            sg = _sig(gb)
            dgu_ref[0, :, sl] = (da * ub) * (sg * (1.0 + gb * (1.0 - sg)))
            dgu_ref[0, :, F + c * FC:F + (c + 1) * FC] = da * (gb * sg)

    row = lambda i: (i, 0)
    return _call(
        body, name=name, grid=(T // TM,), args=[dh, gg, uu, wbuf], comm=comm,
        in_specs=[pl.BlockSpec((TM, D), row), pl.BlockSpec((TM, F), row), pl.BlockSpec((TM, F), row), ANY],
        out_shape=[jax.ShapeDtypeStruct((1, T, 2 * F), BF)],
        out_specs=[pl.BlockSpec((1, TM, 2 * F), lambda i: (0, i, 0))],
        scratch_shapes=[pltpu.VMEM((F, D), BF), pltpu.SemaphoreType.DMA((1,))])


def _ffn_bwd_input(dgu, dh, x, g, wbufs, offs, name, comm=None, after=None):
    def body(dgu_ref, dh_ref, x_ref, g_ref, b0, b1, dx_ref, s_ref, w_s, sem):
        _load_ffn_weights((b0, b1), offs, (w_s.at[pl.ds(0, F), :], w_s.at[pl.ds(F, F), :]), sem)

        @pl.when(pl.program_id(0) == 0)
        def _():
            s_ref[...] = jnp.zeros_like(s_ref)

        dn = _nn(dgu_ref[0], w_s[...])
        dxn, dg = _rmsnorm_bwd(x_ref[...], g_ref[...], dn)
        dx_ref[...] = dh_ref[...] + dxn
        s_ref[0:1, :] += dg

    row = lambda i: (i, 0)
    return _call(
        body, name=name, grid=(T // TM,), args=[dgu, dh, x, g, *wbufs], comm=comm, after=after,
        in_specs=[pl.BlockSpec((1, TM, 2 * F), lambda i: (0, i, 0)), pl.BlockSpec((TM, D), row),
                  pl.BlockSpec((TM, D), row), pl.BlockSpec((1, D), lambda i: (0, 0)), ANY, ANY],
        out_shape=[jax.ShapeDtypeStruct((T, D), F32), jax.ShapeDtypeStruct((8, D), F32)],
        out_specs=[pl.BlockSpec((TM, D), row), pl.BlockSpec((8, D), lambda i: (0, 0))],
        scratch_shapes=[pltpu.VMEM((2 * F, D), BF), pltpu.SemaphoreType.DMA((2,))])


def _tn_matmul(lhs, rhs, tr, name, comm=None):
    ng, _, cdim = lhs.shape
    nc, nk = cdim // tr, T // TK
    if rhs.ndim == 2:
        r_spec = pl.BlockSpec((TK, D), lambda g, c, k: (k, 0))
    else:
        r_spec = pl.BlockSpec((None, TK, D), lambda g, c, k: (g, k, 0))

    def body(l_ref, r_ref, o_ref, acc):
        k = pl.program_id(2)

        @pl.when(k == 0)
        def _():
            acc[...] = jnp.zeros_like(acc)

        acc[...] += _tn(l_ref[...], r_ref[...])

        @pl.when(k == nk - 1)
        def _():
            o_ref[...] = acc[...].astype(BF)

    return _call(
        body, name=name, grid=(ng, nc, nk), args=[lhs, rhs], comm=comm,
        in_specs=[pl.BlockSpec((None, TK, tr), lambda g, c, k: (g, k, c)), r_spec],
        out_shape=[jax.ShapeDtypeStruct((ng * cdim, D), BF)],
        out_specs=[pl.BlockSpec((tr, D), lambda g, c, k: (g * nc + c, 0))],
        scratch_shapes=[pltpu.VMEM((tr, D), F32)])


def _mix_out_bwd(dh2, ya, yb, z, a1, q, lng, lnb, wsq, comm=None):
    def body(dh_ref, ya_ref, yb_ref, ga_ref, gb_ref, a1_ref, q_ref, lng_ref, lnb_ref, wa_ref, wb_ref, wo_ref,
             dzg_ref, da1_ref, dq_ref, l_ref, r_ref, s_ref):
        @pl.when(pl.program_id(0) == 0)
        def _():
            s_ref[...] = jnp.zeros_like(s_ref)

        dhb = dh_ref[...].astype(BF)
        dm = _nt(dhb, wo_ref[...]).astype(BF)
        ya, yb = ya_ref[...], yb_ref[...]
        sa, sb = _sig(ga_ref[...]), _sig(gb_ref[...])
        l_ref[0] = sa * ya + sb * yb
        l_ref[2] = q_ref[...]
        dzg_ref[0] = (dm * ya) * (sa * (1.0 - sa))
        dzg_ref[1] = (dm * yb) * (sb * (1.0 - sb))
        dya = dm * sa
        dyb = dm * sb
        r_ref[0] = dhb
        r_ref[1] = dya
        r_ref[2] = dyb
        dq_ref[...] = _nt(dyb, wb_ref[...]).astype(BF)
        da3 = _nt(dya, wa_ref[...])
        lng = lng_ref[...]
        xh, rs, a2, sg = _layernorm_silu(a1_ref[...].astype(F32), lng, lnb_ref[...])
        l_ref[1] = (a2 * sg).astype(BF)
        da2 = da3 * (sg * (1.0 + a2 * (1.0 - sg)))
        s_ref[0:1, :] += jnp.sum(da2 * xh, axis=0, keepdims=True)
        s_ref[1:2, :] += jnp.sum(da2, axis=0, keepdims=True)
        dxh = da2 * lng
        da1 = rs * (dxh - jnp.mean(dxh, axis=-1, keepdims=True) - xh * jnp.mean(dxh * xh, axis=-1, keepdims=True))
        da1_ref[...] = da1.astype(BF)
        s_ref[2:3, :] += jnp.sum(da1, axis=0, keepdims=True)

    row = lambda i: (i, 0)
    row3 = lambda i: (0, i, 0)
    vec = pl.BlockSpec((1, D), lambda i: (0, 0))
    return _call(
        body, name="mix_out_bwd", grid=(T // TM,), args=[dh2, ya, yb, z, z, a1, q, lng, lnb, wsq, wsq, wsq], comm=comm,
        in_specs=[pl.BlockSpec((TM, D), row), pl.BlockSpec((TM, D), row), pl.BlockSpec((TM, D), row),
                  pl.BlockSpec((None, TM, D), lambda i: (5, i, 0)), pl.BlockSpec((None, TM, D), lambda i: (6, i, 0)),
                  pl.BlockSpec((TM, D), row), pl.BlockSpec((TM, D), row), vec, vec] + _square_specs((0, 1, 2)),
        out_shape=[jax.ShapeDtypeStruct((2, T, D), BF), jax.ShapeDtypeStruct((T, D), BF),
                   jax.ShapeDtypeStruct((T, D), BF), jax.ShapeDtypeStruct((3, T, D), BF),
                   jax.ShapeDtypeStruct((3, T, D), BF), jax.ShapeDtypeStruct((8, D), F32)],
        out_specs=[pl.BlockSpec((2, TM, D), row3), pl.BlockSpec((TM, D), row), pl.BlockSpec((TM, D), row),
                   pl.BlockSpec((3, TM, D), row3), pl.BlockSpec((3, TM, D), row3), pl.BlockSpec((8, D), lambda i: (0, 0))])


def _mix_in_bwd(dz, dh2, h1, gm, win, comm=None):
    def body(dz_ref, dh_ref, h_ref, g_ref, *rest):
        w_any, (o_ref, ob_ref, s_ref, w_s, sem) = rest[:len(win)], rest[len(win):]
        _load_in_proj([(b, first) for b, (_, first) in zip(w_any, win)], w_s, sem)

        @pl.when(pl.program_id(0) == 0)
        def _():
            s_ref[...] = jnp.zeros_like(s_ref)

        du = _nn(dz_ref[0], w_s[0:D, :])
        for j in range(1, NG):
            du = du + _nn(dz_ref[j], w_s[j * D:(j + 1) * D, :])
        dx, dg = _rmsnorm_bwd(h_ref[...], g_ref[...], du)
        dh1 = dh_ref[...] + dx
        o_ref[...] = dh1
        ob_ref[...] = (0.5 * dh1).astype(BF)
        s_ref[0:1, :] += dg

    row = lambda i: (i, 0)
    return _call(
        body, name="mix_in_bwd", grid=(T // TM,), args=[dz, dh2, h1, gm] + [b for b, _ in win], comm=comm,
        in_specs=[pl.BlockSpec((NG, TM, D), lambda i: (0, i, 0)), pl.BlockSpec((TM, D), row),
                  pl.BlockSpec((TM, D), row), pl.BlockSpec((1, D), lambda i: (0, 0))] + [ANY] * len(win),
        out_shape=[jax.ShapeDtypeStruct((T, D), F32), jax.ShapeDtypeStruct((T, D), BF), jax.ShapeDtypeStruct((8, D), F32)],
        out_specs=[pl.BlockSpec((TM, D), row), pl.BlockSpec((TM, D), row), pl.BlockSpec((8, D), lambda i: (0, 0))],
        scratch_shapes=[pltpu.VMEM((NG * D, D), BF), pltpu.SemaphoreType.DMA((NDEV * len(win),))])


def _row_tile(n, want, mult):
    for t in range(min(want, n), 0, -1):
        if n % t == 0 and t % mult == 0:
            return t
    return n


def _pack_small(s_ffn1, s_in, s_mix, s_ffn2, s_final, dwa, dwb):
    def body(f1, mi, mo, f2, fl, wa_ref, wb_ref, v_ref, k_ref):
        for dst, (ref, row) in enumerate(((f1, 0), (mi, 0), (mo, 0), (mo, 1), (mo, 2), (f2, 0), (fl, 0), (fl, 1))):
            v_ref[dst:dst + 1, :] = ref[row:row + 1, :]
        for k in range(NDEV):
            k_ref[k, 0:32, :] = wa_ref[:, k * LANE:(k + 1) * LANE]
            k_ref[k, 32:40, :] = wb_ref[:, k * LANE:(k + 1) * LANE]

    return pl.pallas_call(
        body, name="pack_small",
        out_shape=(jax.ShapeDtypeStruct((8, D), F32), jax.ShapeDtypeStruct((NDEV, 40, LANE), F32)),
    )(s_ffn1, s_in, s_mix, s_ffn2, s_final, dwa, dwb)


def _adam_update(g, w, m, v):
    m2 = ADAM_B1 * m + (1.0 - ADAM_B1) * g
    v2 = ADAM_B2 * v + (1.0 - ADAM_B2) * (g * g)
    c1 = 1.0 - ADAM_B1 ** ADAM_STEP
    c2 = 1.0 - ADAM_B2 ** ADAM_STEP
    return -ADAM_LR * ((m2 / c1) / (jnp.sqrt(v2 / c2) + ADAM_EPS) + ADAM_WD * w), m2, v2


def _adam_small(vecs, convs, vec_params, tap_params):
    nv, nt = len(vec_params), len(tap_params)

    def body(*refs):
        v_ref, k_ref = refs[:2]
        p_refs = refs[2:2 + 3 * (nv + nt)]
        l_ref = refs[2 + 3 * (nv + nt)]
        o_refs = refs[3 + 3 * (nv + nt):]
        s, c = v_ref[0], k_ref[0]
        for k in range(1, NDEV):
            s = s + v_ref[k]
            c = c + k_ref[k]
        l_ref[...] = jnp.sum(s[7:8, :], axis=-1, keepdims=True)
        for i in range(nv):
            w_ref, m_ref, u_ref = p_refs[3 * i: 3 * i + 3]
            g_ref, d_ref, m2_ref, u2_ref = o_refs[4 * i: 4 * i + 4]
            g = s[i:i + 1, :]
            g_ref[...] = g
            d_ref[...], m2_ref[...], u2_ref[...] = _adam_update(g, w_ref[...], m_ref[...], u_ref[...])
        for i in range(nt):
            w_ref, m_ref, u_ref = p_refs[3 * (nv + i): 3 * (nv + i) + 3]
            g_ref, d_ref, m2_ref, u2_ref = o_refs[4 * (nv + i): 4 * (nv + i) + 4]
            first = tap_params[i][0]
            for k in range(w_ref.shape[0]):
                g = c[first + k:first + k + 1, :]
                g_ref[k] = g
                d_ref[k], m2_ref[k], u2_ref[k] = _adam_update(g, w_ref[k], m_ref[k], u_ref[k])

    params = [a for p in vec_params for a in p] + [a for p in tap_params for a in p[1:]]
    out_shape = [jax.ShapeDtypeStruct((1, 1), F32)]
    for p in list(vec_params) + [p[1:] for p in tap_params]:
        out_shape += [jax.ShapeDtypeStruct(p[0].shape, F32)] * 4
    outs = pl.pallas_call(body, name="adam_small", out_shape=tuple(out_shape))(vecs, convs, *params)
    groups = [tuple(outs[1 + 4 * i: 5 + 4 * i]) for i in range(nv + nt)]
    return outs[0], groups[:nv], groups[nv:]


def _adam_in_proj(parts, w, m, v, after):
    rows = w.shape[1]
    tr = _row_tile(D, 256, LANE)

    def body(*refs):
        p_refs = refs[:len(parts)]
        w_ref, m_ref, v_ref, g_ref, d_ref, m2_ref, v2_ref = refs[len(parts):]
        sums = []
        for p in p_refs:
            s = p[0].astype(F32)
            for k in range(1, p.shape[0]):
                s = s + p[k].astype(F32)
            sums.append(s)
        g = jnp.concatenate(sums, axis=0).T
        g_ref[...] = g
        d_ref[...], m2_ref[...], v2_ref[...] = _adam_update(g, w_ref[...], m_ref[...], v_ref[...])

    spec = pl.BlockSpec((tr, rows), lambda i: (i, 0))
    return _call(body, name="adam_in", grid=(D // tr,), args=list(parts) + [w, m, v], after=after,
                 in_specs=[pl.BlockSpec((p.shape[0], p.shape[1], tr), lambda i: (0, 0, i)) for p in parts] + [spec] * 3,
                 out_shape=[jax.ShapeDtypeStruct((D, rows), F32)] * 4, out_specs=[spec] * 4)


def _adam(gs, ws, ms, vs, name, after):
    n = len(gs)
    rows, cols = ws[0].shape
    tr = _row_tile(rows, min(256, rows // 2), 16)

    def body(*refs):
        for i in range(n):
            g_in, w, m, v = refs[4 * i], refs[4 * i + 1][...], refs[4 * i + 2][...], refs[4 * i + 3][...]
            g_ref, d_ref, m_ref, v_ref = refs[4 * n + 4 * i: 4 * n + 4 * i + 4]
            g = g_in[0].astype(F32)
            for k in range(1, g_in.shape[0]):
                g = g + g_in[k].astype(F32)
            g_ref[...] = g
            d_ref[...], m_ref[...], v_ref[...] = _adam_update(g, w, m, v)

    spec = pl.BlockSpec((tr, cols), lambda i: (i, 0))
    args, in_specs = [], []
    for i in range(n):
        slots, first = gs[i]
        args += [slots, ws[i], ms[i], vs[i]]
        in_specs += [pl.BlockSpec((slots.shape[0], tr, cols), lambda i, b=first // tr: (0, b + i, 0))] + [spec] * 3
    outs = _call(body, name=name, grid=(rows // tr,), args=args, in_specs=in_specs, after=after,
                 out_shape=[jax.ShapeDtypeStruct((rows, cols), F32)] * (4 * n), out_specs=[spec] * (4 * n))
    return [tuple(outs[4 * i: 4 * i + 4]) for i in range(n)]


def kernel(x, ffn1_norm, ffn1_w_gate, ffn1_w_up, ffn1_w_down, mix_norm, w_in, a_dw_w, a_dw_b, a_ln_g, a_ln_b, a_w_out, b_conv_w, b_w_out, w_o, ffn2_norm, ffn2_w_gate, ffn2_w_up, ffn2_w_down, final_norm, loss_target, m_ffn1_norm, m_ffn1_w_gate, m_ffn1_w_up, m_ffn1_w_down, m_mix_norm, m_w_in, m_a_dw_w, m_a_dw_b, m_a_ln_g, m_a_ln_b, m_a_w_out, m_b_conv_w, m_b_w_out, m_w_o, m_ffn2_norm, m_ffn2_w_gate, m_ffn2_w_up, m_ffn2_w_down, m_final_norm, v_ffn1_norm, v_ffn1_w_gate, v_ffn1_w_up, v_ffn1_w_down, v_mix_norm, v_w_in, v_a_dw_w, v_a_dw_b, v_a_ln_g, v_a_ln_b, v_a_w_out, v_b_conv_w, v_b_w_out, v_w_o, v_ffn2_norm, v_ffn2_w_gate, v_ffn2_w_up, v_ffn2_w_down, v_final_norm):
    names = ("ffn1_norm", "ffn1_w_gate", "ffn1_w_up", "ffn1_w_down", "mix_norm", "w_in", "a_dw_w", "a_dw_b",
             "a_ln_g", "a_ln_b", "a_w_out", "b_conv_w", "b_w_out", "w_o", "ffn2_norm", "ffn2_w_gate", "ffn2_w_up",
             "ffn2_w_down", "final_norm")
    w = dict(ffn1_norm=ffn1_norm, ffn1_w_gate=ffn1_w_gate, ffn1_w_up=ffn1_w_up, ffn1_w_down=ffn1_w_down,
             mix_norm=mix_norm, w_in=w_in, a_dw_w=a_dw_w, a_dw_b=a_dw_b, a_ln_g=a_ln_g, a_ln_b=a_ln_b,
             a_w_out=a_w_out, b_conv_w=b_conv_w, b_w_out=b_w_out, w_o=w_o, ffn2_norm=ffn2_norm,
             ffn2_w_gate=ffn2_w_gate, ffn2_w_up=ffn2_w_up, ffn2_w_down=ffn2_w_down, final_norm=final_norm)
    m = dict(ffn1_norm=m_ffn1_norm, ffn1_w_gate=m_ffn1_w_gate, ffn1_w_up=m_ffn1_w_up, ffn1_w_down=m_ffn1_w_down,
             mix_norm=m_mix_norm, w_in=m_w_in, a_dw_w=m_a_dw_w, a_dw_b=m_a_dw_b, a_ln_g=m_a_ln_g, a_ln_b=m_a_ln_b,
             a_w_out=m_a_w_out, b_conv_w=m_b_conv_w, b_w_out=m_b_w_out, w_o=m_w_o, ffn2_norm=m_ffn2_norm,
             ffn2_w_gate=m_ffn2_w_gate, ffn2_w_up=m_ffn2_w_up, ffn2_w_down=m_ffn2_w_down, final_norm=m_final_norm)
    v = dict(ffn1_norm=v_ffn1_norm, ffn1_w_gate=v_ffn1_w_gate, ffn1_w_up=v_ffn1_w_up, ffn1_w_down=v_ffn1_w_down,
             mix_norm=v_mix_norm, w_in=v_w_in, a_dw_w=v_a_dw_w, a_dw_b=v_a_dw_b, a_ln_g=v_a_ln_g, a_ln_b=v_a_ln_b,
             a_w_out=v_a_w_out, b_conv_w=v_b_conv_w, b_w_out=v_b_w_out, w_o=v_w_o, ffn2_norm=v_ffn2_norm,
             ffn2_w_gate=v_ffn2_w_gate, ffn2_w_up=v_ffn2_w_up, ffn2_w_down=v_ffn2_w_down, final_norm=v_final_norm)
    flat = _pack_weights(dict(wg1=ffn1_w_gate[0].T, wu1=ffn1_w_up[0].T, wd1=ffn1_w_down[0], wg2=ffn2_w_gate[0].T,
                              wu2=ffn2_w_up[0].T, wd2=ffn2_w_down[0], win=w_in[0], wa=a_w_out[0], wb=b_w_out[0],
                              wo=w_o[0]))
    cw_shard = jnp.concatenate([a_dw_w[0], jnp.zeros((1, LANE), F32), b_conv_w[0], jnp.zeros((5, LANE), F32)], axis=0)

    x2, tgt = x[0], loss_target[0]
    st_a, st_b, st_b2 = ("wg1", "wu1"), ("wd1", "win/0/2"), ("win/1/2",)
    st_c, st_d, st_e = ("wa", "wb", "wo", "wg2"), ("wu2",), ("wd2",)

    buf_a, cw = _run_comm(_join(_ag_comm(st_a, flat), _direct_comm(cw_shard, False)), "ag_ffn1")
    n1, gg1, uu1, act1, buf_b = _ffn_gate_up(x2, ffn1_norm, (buf_a, buf_a), (0, F), "ffn1_gate_up", _ag_comm(st_b, flat))
    h1, buf_b2 = _ffn_down(x2, act1, buf_b, 0, "ffn1_down", _ag_comm(st_b2, flat))
    win = ((buf_b, F), (buf_b2, 0))
    u, z, buf_c = _mix_in(h1, mix_norm, win, _ag_comm(st_c, flat))
    dft = _dft_constants()
    cw = jnp.transpose(cw, (1, 0, 2)).reshape(40, D)
    a1, q, buf_d = _conv_fwd_dft(z, cw, a_dw_b, dft, _ag_comm(st_d, flat))
    h2, ya, yb, buf_e = _mix_out(a1, q, z, h1, a_ln_g, a_ln_b, buf_c, _ag_comm(st_e, flat))
    ffn2_bufs, ffn2_offs = (buf_c, buf_d, buf_e), (3 * D, 0, 0)
    dh3, dhb3, s_final, n2, gg2, uu2, act2 = _ffn_fwd(h2, ffn2_norm, ffn2_bufs, ffn2_offs, "ffn2_fwd",
                                          final=(final_norm.reshape(1, D), tgt))

    tr_f = F // 2 if (F // 2) % LANE == 0 else F
    def pair(stage, src):
        return _rs_pair_comm(stage, src)

    def chip(stage, src, pair_buf, tag):
        return _rs_chip_comm(_pair_add(stage, src, pair_buf, "pair_add_" + tag))

    (dgu2,) = _ffn_bwd_hidden(dhb3, gg2, uu2, buf_e, 0, "ffn2_bwd_h")
    (gu2,) = _tn_matmul(dgu2, n2, tr_f, "dw_gu2")
    s2a, src2a = ("wg2", "wu2"), dict(wg2=(gu2, 0), wu2=(gu2, F))
    (gd2,) = _tn_matmul(act2, dhb3, tr_f, "dw_d2")
    s2b, src2b = ("wd2",), dict(wd2=(gd2, 0))
    dh2, s_ffn2, pair2a, pair2b = _ffn_bwd_input(dgu2, dh3, h2, ffn2_norm, (buf_c, buf_d), (3 * D, 0), "ffn2_bwd_x",
                                                 _join(pair(s2a, src2a), pair(s2b, src2b)))
    dzg, da1, dq, lsq, rsq, s_mix, recv2b = _mix_out_bwd(dh2, ya, yb, z, a1, q, a_ln_g, a_ln_b, buf_c,
                                                          chip(s2b, src2b, pair2b, "2b"))
    (gsq,) = _tn_matmul(lsq, rsq, D, "dw_square")
    ssq, srcsq = ("wa", "wb", "wo"), dict(wa=(gsq, D), wb=(gsq, 2 * D), wo=(gsq, 0))
    dz, dwa, dwb, recv2a, pairsq = _conv_bwd_dft(z, da1, dq, dzg, cw, dft,
                                                 _join(chip(s2a, src2a, pair2a, "2a"), pair(ssq, srcsq)))
    gin, recvsq = _tn_matmul(dz, u, D, "dw_in", chip(ssq, srcsq, pairsq, "sq"))
    sin_a, sin_b, srcin = ("win/0/2",), ("win/1/2",), {"win/0/2": (gin, 0), "win/1/2": (gin, 0)}
    dh1, dhb1, s_in, pairin_a, pairin_b = _mix_in_bwd(dz, dh2, h1, mix_norm, win,
                                                _join(pair(sin_a, srcin), pair(sin_b, srcin)))
    dgu1, recvin_a = _ffn_bwd_hidden(dhb1, gg1, uu1, buf_b, 0, "ffn1_bwd_h",
                                           chip(sin_a, srcin, pairin_a, "in_a"))
    gu1, recvin_b = _tn_matmul(dgu1, n1, tr_f, "dw_gu1", chip(sin_b, srcin, pairin_b, "in_b"))
    s1a, src1a = ("wg1", "wu1"), dict(wg1=(gu1, 0), wu1=(gu1, F))
    gd1, pair1a = _tn_matmul(act1, dhb1, tr_f, "dw_d1", pair(s1a, src1a))
    s1b, src1b = ("wd1",), dict(wd1=(gd1, 0))
    xchg1 = _join(chip(s1a, src1a, pair1a, "1a"), pair(s1b, src1b))
    xchg1_sems, xchg1_bufs, token = _comm_start(xchg1, "xchg_ffn1_start")
    dx, s_ffn1 = _ffn_bwd_input(dgu1, dh1, x2, ffn1_norm, (buf_a, buf_a), (0, F), "ffn1_bwd_x", after=token)
    (_, gd1), (recv1a, pair1b) = _comm_wait(xchg1, "xchg_ffn1_wait", xchg1_sems, xchg1_bufs, s_ffn1)
    src1b = dict(wd1=(gd1, 0))

    vec8, convk = _pack_small(s_ffn1, s_in, s_mix, s_ffn2, s_final, dwa, dwb)
    tail = _join(chip(s1b, src1b, pair1b, "1b"), _join(_direct_comm(vec8, False), _direct_comm(convk, True)))
    tail_sems, tail_bufs, token = _comm_start(tail, "xchg_tail_start")

    fs = F // NDEV
    g = dict(ffn1_w_gate=(recv1a, 0), ffn1_w_up=(recv1a, fs), ffn2_w_gate=(recv2a, 0), ffn2_w_up=(recv2a, fs),
             ffn2_w_down=(recv2b, 0), a_w_out=(recvsq, 0), b_w_out=(recvsq, D // NDEV), w_o=(recvsq, 2 * (D // NDEV)))
    grad, upd = {}, {}

    def run(group, name, after, as2d=lambda a: a[0], back=lambda a, n: a.reshape(w[n].shape)):
        res = _adam([g[n] for n in group], [as2d(w[n]) for n in group], [as2d(m[n]) for n in group],
                    [as2d(v[n]) for n in group], name, after)
        for n, r in zip(group, res):
            grad[n], upd[n] = back(r[0], n), tuple(back(a, n) for a in r[1:])
        return res[0][0]

    done = run(("ffn1_w_gate", "ffn1_w_up", "ffn2_w_gate", "ffn2_w_up"), "adam_gate_up", token,
               as2d=lambda a: a[0].T, back=lambda a, n: a.T[None])
    r_in = _adam_in_proj([recvin_a, recvin_b], w_in[0], m_w_in[0], v_w_in[0], done)
    grad["w_in"], upd["w_in"] = r_in[0][None], tuple(a[None] for a in r_in[1:])
    done = run(("a_w_out", "b_w_out", "w_o"), "adam_square", r_in[0])
    _, (recv1b, vec_all, conv_all) = _comm_wait(tail, "xchg_tail_wait", tail_sems, tail_bufs, done)
    g["ffn1_w_down"] = (recv1b, 0)
    run(("ffn1_w_down", "ffn2_w_down"), "adam_down", done)
    vec_names = ("ffn1_norm", "mix_norm", "a_ln_g", "a_ln_b", "a_dw_b", "ffn2_norm", "final_norm")
    tap_names, tap_rows = ("a_dw_w", "b_conv_w"), (0, 32)
    taps = lambda a: jnp.transpose(a, (1, 0, 2))
    loss, vec_res, tap_res = _adam_small(
        vec_all, conv_all, [tuple(t[n].reshape(1, D) for t in (w, m, v)) for n in vec_names],
        [(r,) + tuple(taps(t[n]) for t in (w, m, v)) for n, r in zip(tap_names, tap_rows)])
    for n, r in zip(vec_names, vec_res):
        grad[n], upd[n] = r[0].reshape(w[n].shape), tuple(a.reshape(w[n].shape) for a in r[1:])
    for n, r in zip(tap_names, tap_res):
        grad[n], upd[n] = taps(r[0]), tuple(taps(a) for a in r[1:])

    return (loss.reshape(()), dx.reshape(x.shape), *[grad[n] for n in names], *[upd[n][0] for n in names],
            *[upd[n][1] for n in names], *[upd[n][2] for n in names])
```

```python
import jax
import jax.numpy as jnp
from jax import lax
from jax.experimental import pallas as pl
from jax.experimental.pallas import tpu as pltpu

T = 4096
D = 1024
F = 2816
NG = 7
NDEV = 8
NCHIP = 4
KA, KB = 31, 3
EPS = 1e-6
ADAM_LR, ADAM_B1, ADAM_B2, ADAM_EPS, ADAM_WD, ADAM_STEP = 0.001, 0.9, 0.999, 1e-08, 0.01, 10

TM = 512
FC = 256
TB = 1024
NB = 256
HB = NB // 2
CW = 256
CHB = 64
LANE = 128
TK = 2048
VMEM_LIMIT = 56 * 1024 * 1024

BF = jnp.bfloat16
F32 = jnp.float32
MESH = pl.DeviceIdType.MESH
ANY = pl.BlockSpec(memory_space=pl.ANY)
COLLECTIVE_ID = {(1,): 0, (2, 4, 6): 1, (1, 2, 4, 6): 2, (1, 2, 4): 3, (1, 2, 3, 4, 5, 6, 7): 4}
START_COLLECTIVE_ID = {(1, 2, 4, 6): 5, (1, 2, 3, 4, 5, 6, 7): 6}

ORDER = ("wg1", "wu1", "wd1", "wg2", "wu2", "wd2", "win", "wa", "wb", "wo")


class _Layout:
    def __init__(self):
        fs, dis, ds = F // NDEV, NG * D // NDEV, D // NDEV
        self.rows = dict(wg1=fs, wu1=fs, wd1=fs, wg2=fs, wu2=fs, wd2=fs, win=dis, wa=ds, wb=ds, wo=ds)
        self.fl, off = {}, 0
        for n in ORDER:
            self.fl[n] = off
            off += self.rows[n]
        self.RT = off


class _Stage:
    def __init__(self, names):
        lay = _Layout()
        self.names = names
        self.rows, self.full, self.sub, self.fl = {}, {}, {}, {}
        for n in names:
            base, i, k = (n.split("/") + ["0", "1"])[:3]
            self.full[n] = lay.rows[base]
            self.rows[n] = lay.rows[base] // int(k)
            self.sub[n] = int(i) * self.rows[n]
            self.fl[n] = lay.fl[base] + self.sub[n]
        self.off, self.wc, o, w = {}, {}, 0, 0
        for n in names:
            self.off[n], self.wc[n] = o, w
            o += self.rows[n]
            w += NDEV * self.rows[n]
        self.R, self.W = o, w

    def grad_row(self, n, first, dev_lin):
        return first + dev_lin * self.full[n] + self.sub[n]


def _nt(a, b):
    return lax.dot_general(a, b, (((1,), (1,)), ((), ())), preferred_element_type=F32)


def _nn(a, b):
    return lax.dot_general(a, b, (((1,), (0,)), ((), ())), preferred_element_type=F32)


def _tn(a, b):
    return lax.dot_general(a, b, (((0,), (0,)), ((), ())), preferred_element_type=F32)


def _sig(x):
    return 1.0 / (1.0 + jnp.exp(-x))


def _position():
    return lax.axis_index("x"), lax.axis_index("y"), lax.axis_index("c")


def _peer(pos, j):
    x, y, c = pos
    return (1 - x if j & 4 else x, 1 - y if j & 2 else y, 1 - c if j & 1 else c)


def _lin(pos):
    return 4 * pos[0] + 2 * pos[1] + pos[2]


def _chip(pos):
    return 2 * pos[0] + pos[1]


class _Comm:
    def __init__(self, inputs, out_shapes, scratch, start, finish, middle=None, peers=None):
        self.inputs, self.out_shapes, self.scratch = inputs, out_shapes, scratch
        self.start, self.finish, self.middle = start, finish, middle
        self.peers = peers


def _handshake(peers):
    barrier = pltpu.get_barrier_semaphore()
    for j in peers:
        pl.semaphore_signal(barrier, inc=1, device_id=_peer(_position(), j), device_id_type=MESH)
    pl.semaphore_wait(barrier, len(peers))


def _call(body, *, name, grid, args, in_specs, out_shape, out_specs, scratch_shapes=(), comm=None,
          num_scalar_prefetch=0, after=None):
    in_specs, out_shape, out_specs, scratch_shapes = list(in_specs), list(out_shape), list(out_specs), list(scratch_shapes)
    if after is not None:
        inner, pos = body, num_scalar_prefetch + len(in_specs)
        body = lambda *refs: inner(*refs[:pos], *refs[pos + 1:])
        args, in_specs = list(args) + [after], in_specs + [ANY]
    n_in, n_out, n_scr = len(in_specs), len(out_shape), len(scratch_shapes)
    sp = num_scalar_prefetch
    if comm is None:
        kernel_fn = lambda *refs: body(*refs)
        c_in = c_out = c_scr = 0
    else:
        c_in, c_out, c_scr = len(comm.inputs), len(comm.out_shapes), len(comm.scratch)

        def kernel_fn(*refs):
            pre, refs = refs[:sp], refs[sp:]
            ins, cins = refs[:n_in], refs[n_in:n_in + c_in]
            o0 = n_in + c_in
            outs, couts = refs[o0:o0 + n_out], refs[o0 + n_out:o0 + n_out + c_out]
            s0 = o0 + n_out + c_out
            scr, cscr = refs[s0:s0 + n_scr], refs[s0 + n_scr:]
            step, steps = pl.program_id(0), grid[0]
            for a in range(1, len(grid)):
                step, steps = step * grid[a] + pl.program_id(a), steps * grid[a]
            first, last = step == 0, step == steps - 1

            @pl.when(first)
            def _():
                if comm.peers is not None:
                    _handshake(comm.peers)
                comm.start(cins, couts, cscr)

            if comm.middle is not None:
                @pl.when(step == (steps // 2 if steps > 2 else steps - 1))
                def _():
                    comm.middle(cins, couts, cscr)

            body(*pre, *ins, *outs, *scr)

            @pl.when(last)
            def _():
                comm.finish(cins, couts, cscr)

        args = list(args) + list(comm.inputs)
        in_specs += [ANY] * c_in
        out_shape += list(comm.out_shapes)
        out_specs += [ANY] * c_out
        scratch_shapes += list(comm.scratch)
    params = pltpu.CompilerParams(dimension_semantics=("arbitrary",) * len(grid), vmem_limit_bytes=VMEM_LIMIT,
                                  collective_id=COLLECTIVE_ID[comm.peers] if comm is not None and comm.peers else None)
    if sp:
        grid_spec = pltpu.PrefetchScalarGridSpec(num_scalar_prefetch=sp, grid=grid, in_specs=in_specs,
                                                 out_specs=out_specs, scratch_shapes=scratch_shapes)
        return pl.pallas_call(kernel_fn, name=name, grid_spec=grid_spec, out_shape=out_shape,
                              compiler_params=params)(*args)
    return pl.pallas_call(kernel_fn, name=name, grid=grid, in_specs=in_specs, out_shape=out_shape, out_specs=out_specs,
                          scratch_shapes=scratch_shapes, compiler_params=params)(*args)


def _join(a, b):
    na = (len(a.inputs), len(a.out_shapes), len(a.scratch))

    def split(refs):
        return ([r[:n] for r, n in zip(refs, na)], [r[n:] for r, n in zip(refs, na)])

    def start(*refs):
        ra, rb = split(refs)
        a.start(*ra)
        b.start(*rb)

    def finish(*refs):
        ra, rb = split(refs)
        a.finish(*ra)
        b.finish(*rb)

    def middle(*refs):
        for stage, r in zip((a, b), split(refs)):
            if stage.middle is not None:
                stage.middle(*r)

    return _Comm(list(a.inputs) + list(b.inputs), list(a.out_shapes) + list(b.out_shapes),
                 list(a.scratch) + list(b.scratch), start, finish,
                 middle if (a.middle is not None or b.middle is not None) else None,
                 peers=tuple(sorted(set(a.peers) | set(b.peers))) if a.peers and b.peers else None)


def _run_comm(comm, name):
    def body(*refs):
        c_in, c_out = len(comm.inputs), len(comm.out_shapes)
        parts = (refs[:c_in], refs[c_in:c_in + c_out], refs[c_in + c_out:])
        if comm.peers is not None:
            _handshake(comm.peers)
        comm.start(*parts)
        if comm.middle is not None:
            comm.middle(*parts)
        comm.finish(*parts)

    return pl.pallas_call(
        body, name=name, out_shape=list(comm.out_shapes), in_specs=[ANY] * len(comm.inputs),
        out_specs=[ANY] * len(comm.out_shapes), scratch_shapes=list(comm.scratch),
        compiler_params=pltpu.CompilerParams(collective_id=COLLECTIVE_ID[comm.peers] if comm.peers else None),
    )(*comm.inputs)


HBM = pl.BlockSpec(memory_space=pltpu.HBM)
SEM = pl.BlockSpec(memory_space=pltpu.SEMAPHORE)
DATAFLOW = pltpu.SideEffectType.DATAFLOW_SIDE_EFFECTING


def _comm_start(comm, name):
    c_in, c_out = len(comm.inputs), len(comm.out_shapes)
    sems = [s(()) if s is pltpu.SemaphoreType.DMA else s for s in comm.scratch]
    bufs = list(comm.inputs) + [lax.empty(s.shape, s.dtype) for s in comm.out_shapes]

    def body(*refs):
        if comm.peers is not None:
            _handshake(comm.peers)
        sem_refs = refs[c_in + c_out:c_in + c_out + len(sems)]
        comm.start(refs[:c_in], refs[c_in:c_in + c_out], sem_refs)
        refs[-1][...] = jnp.zeros_like(refs[-1])

    outs = pl.pallas_call(
        body, name=name,
        out_shape=sems + [pltpu.HBM(b.shape, b.dtype) for b in bufs] + [jax.ShapeDtypeStruct((8, LANE), F32)],
        in_specs=[HBM] * len(bufs),
        out_specs=[SEM] * len(sems) + [HBM] * len(bufs) + [pl.BlockSpec(memory_space=pltpu.VMEM)],
        input_output_aliases={i: len(sems) + i for i in range(len(bufs))},
        compiler_params=pltpu.CompilerParams(
            has_side_effects=DATAFLOW, collective_id=START_COLLECTIVE_ID[comm.peers] if comm.peers else None),
    )(*[pltpu.with_memory_space_constraint(b, pltpu.HBM) for b in bufs])
    return outs[:len(sems)], outs[len(sems):-1], outs[-1]


def _comm_wait(comm, name, sems, bufs, after):
    c_in, c_out = len(comm.inputs), len(comm.out_shapes)

    def body(*refs):
        sem_refs = refs[c_in + c_out:c_in + c_out + len(sems)]
        comm.finish(refs[:c_in], refs[c_in:c_in + c_out], sem_refs)

    outs = pl.pallas_call(
        body, name=name, out_shape=[pltpu.HBM(b.shape, b.dtype) for b in bufs],
        in_specs=[HBM] * len(bufs) + [SEM] * len(sems) + [ANY], out_specs=[HBM] * len(bufs),
        input_output_aliases={i: i for i in range(len(bufs))},
        compiler_params=pltpu.CompilerParams(has_side_effects=DATAFLOW),
    )(*bufs, *sems, after)
    return outs[:c_in], outs[c_in:]


def _ag_comm(names, flat):
    st = _Stage(names)

    def ring(me):
        x, y, c = me
        diagonal = x == y
        up = (jnp.where(diagonal, x, 1 - x), jnp.where(diagonal, 1 - y, y), c)
        down = (jnp.where(diagonal, 1 - x, x), jnp.where(diagonal, y, 1 - y), c)
        low = c == 0
        passed = tuple(jnp.where(low, d, u) for d, u in zip(down, up))
        target = tuple(jnp.where(low, u, d) for d, u in zip(down, up))
        return up, down, (1 - x, 1 - y, c), passed, target

    def parts(refs):
        (flat_ref,), (out_ref,), (send_sems, recv_sems, local_sem) = refs
        me = _position()

        def region(name, dev):
            r = st.rows[name]
            return out_ref.at[pl.ds(st.wc[name] + _lin(dev) * r, r), :]

        def own(name):
            return flat_ref.at[pl.ds(st.fl[name], st.rows[name]), :]

        def copies(k, dev, to, from_flat):
            return [pltpu.make_async_remote_copy(
                src_ref=own(n) if from_flat else region(n, dev), dst_ref=region(n, dev), send_sem=send_sems.at[k],
                recv_sem=recv_sems.at[k], device_id=to, device_id_type=MESH) for n in names]

        def whole(k):
            return pltpu.make_async_remote_copy(
                src_ref=flat_ref.at[pl.ds(0, st.R), :], dst_ref=out_ref.at[pl.ds(0, st.R), :],
                send_sem=send_sems.at[k], recv_sem=recv_sems.at[k], device_id=me, device_id_type=MESH)

        return me, region, own, copies, whole, flat_ref, out_ref, local_sem

    def start(*refs):
        me, region, own, copies, _, _, _, local_sem = parts(refs)
        for n in names:
            pltpu.make_async_copy(own(n), region(n, me), local_sem).start()
        up, down, _, _, _ = ring(me)
        for k, to in ((1, up), (2, down), (0, _peer(me, 1))):
            for cp in copies(k, me, to, True):
                cp.start()

    def middle(*refs):
        me, _, _, copies, whole, _, _, _ = parts(refs)
        up, down, _, passed, target = ring(me)
        sib = _peer(me, 1)
        whole(1).wait_recv()
        whole(2).wait_recv()
        for k, dev, to in ((3, passed, target), (4, down, sib), (5, up, sib)):
            for cp in copies(k, dev, to, False):
                cp.start()

    def finish(*refs):
        me, _, _, copies, whole, flat_ref, out_ref, local_sem = parts(refs)
        _, _, across, _, _ = ring(me)
        whole(3).wait_recv()
        for cp in copies(6, across, _peer(me, 1), False):
            cp.start()
        whole(0).wait_recv()
        for j in range(3):
            whole(4 + j).wait_recv()
        for k in range(7):
            whole(k).wait_send()
        pltpu.make_async_copy(flat_ref.at[pl.ds(0, st.R), :], out_ref.at[pl.ds(0, st.R), :], local_sem).wait()

    return _Comm([flat], [jax.ShapeDtypeStruct((st.W, D), BF)],
                 [pltpu.SemaphoreType.DMA((7,)), pltpu.SemaphoreType.DMA((7,)), pltpu.SemaphoreType.DMA],
                 start, finish, middle, peers=(1, 2, 4))


def _rs_pair_comm(names, src):
    st = _Stage(names)
    arrays = []
    for n in names:
        if not any(src[n][0] is a for a in arrays):
            arrays.append(src[n][0])
    idx = {n: [i for i, a in enumerate(arrays) if a is src[n][0]][0] for n in names}

    def slot_wait(refs):
        recv = refs[1][0]
        send_sem, recv_sem = refs[2]
        return pltpu.make_async_remote_copy(src_ref=recv, dst_ref=recv, send_sem=send_sem, recv_sem=recv_sem,
                                            device_id=_position(), device_id_type=MESH)

    def start(*refs):
        ins, (recv,), (send_sem, recv_sem) = refs
        me = _position()
        sib = _peer(me, 1)
        for q in range(NCHIP):
            dev = (q // 2, q % 2, sib[2])
            for n in names:
                r = st.rows[n]
                pltpu.make_async_remote_copy(
                    src_ref=ins[idx[n]].at[pl.ds(st.grad_row(n, src[n][1], _lin(dev)), r), :],
                    dst_ref=recv.at[q, pl.ds(st.off[n], r), :], send_sem=send_sem, recv_sem=recv_sem,
                    device_id=sib, device_id_type=MESH).start()

    def finish(*refs):
        w = slot_wait(refs)
        w.wait_recv()
        w.wait_send()

    return _Comm(arrays, [jax.ShapeDtypeStruct((NCHIP, st.R, D), BF)],
                 [pltpu.SemaphoreType.DMA, pltpu.SemaphoreType.DMA], start, finish, peers=(1,))


def _pair_add(names, src, recv, name):
    st = _Stage(names)
    c_arr = jnp.reshape(lax.axis_index("c"), (1,)).astype(jnp.int32)

    def body(c_ref, *refs):
        r_ref, o_ref = refs[len(names)], refs[len(names) + 1]
        for a_ref, n in zip(refs, names):
            rows = slice(st.off[n], st.off[n] + st.rows[n])
            o_ref[rows, :] = (a_ref[...].astype(F32) + r_ref[rows, :].astype(F32)).astype(BF)

    def shard_spec(n):
        r = st.rows[n]
        base, step = st.grad_row(n, src[n][1], 0) // r, st.full[n] // r
        return pl.BlockSpec((r, D), lambda q, c_ref: (base + step * (2 * q + c_ref[0]), 0))

    slot = pl.BlockSpec((None, st.R, D), lambda q, c_ref: (q, 0, 0))
    return _call(body, name=name, grid=(NCHIP,), args=[c_arr] + [src[n][0] for n in names] + [recv],
                 in_specs=[shard_spec(n) for n in names] + [slot],
                 out_shape=[jax.ShapeDtypeStruct((NCHIP, st.R, D), BF)], out_specs=[slot], num_scalar_prefetch=1)[0]


def _rs_chip_comm(part):
    def copies(refs):
        (p_ref,), (recv,), (send_sems, recv_sems, local_sem) = refs
        me = _position()
        mine = pltpu.make_async_copy(p_ref.at[_chip(me)], recv.at[_chip(me)], local_sem)
        out = []
        for j, bits in enumerate((4, 2, 6)):
            to = _peer(me, bits)
            out.append(pltpu.make_async_remote_copy(
                src_ref=p_ref.at[_chip(to)], dst_ref=recv.at[_chip(me)], send_sem=send_sems.at[j],
                recv_sem=recv_sems.at[j], device_id=to, device_id_type=MESH))
        return mine, out

    def start(*refs):
        mine, out = copies(refs)
        mine.start()
        for cp in out:
            cp.start()

    def finish(*refs):
        mine, out = copies(refs)
        for cp in out:
            cp.wait_recv()
        for cp in out:
            cp.wait_send()
        mine.wait()

    return _Comm([part], [jax.ShapeDtypeStruct(part.shape, BF)],
                 [pltpu.SemaphoreType.DMA((3,)), pltpu.SemaphoreType.DMA((3,)), pltpu.SemaphoreType.DMA],
                 start, finish, peers=(2, 4, 6))


def _direct_comm(x, scatter):
    def copies(refs):
        (x_ref,), (out_ref,), (send_sems, recv_sems, local_sem) = refs
        me = _position()

        def piece(dev):
            return x_ref.at[_lin(dev)] if scatter else x_ref

        mine = pltpu.make_async_copy(piece(me), out_ref.at[_lin(me)], local_sem)
        return mine, [pltpu.make_async_remote_copy(
            src_ref=piece(_peer(me, j)), dst_ref=out_ref.at[_lin(me)], send_sem=send_sems.at[j - 1],
            recv_sem=recv_sems.at[j - 1], device_id=_peer(me, j), device_id_type=MESH) for j in range(1, NDEV)]

    def start(*refs):
        mine, cps = copies(refs)
        mine.start()
        for cp in cps:
            cp.start()

    def finish(*refs):
        mine, cps = copies(refs)
        for cp in cps:
            cp.wait_recv()
        for cp in cps:
            cp.wait_send()
        mine.wait()

    shape = x.shape if scatter else (NDEV,) + x.shape
    return _Comm([x], [jax.ShapeDtypeStruct(shape, x.dtype)],
                 [pltpu.SemaphoreType.DMA((7,)), pltpu.SemaphoreType.DMA((7,)), pltpu.SemaphoreType.DMA],
                 start, finish, peers=(1, 2, 3, 4, 5, 6, 7))


def _pack_weights(shards):
    lay = _Layout()

    def body(*refs):
        o_ref = refs[-1]
        for ref, n in zip(refs, ORDER):
            x = ref[...].T if n == "win" else ref[...]
            o_ref[lay.fl[n]:lay.fl[n] + lay.rows[n], :] = x.astype(BF)

    return pl.pallas_call(
        body, name="pack_weights", out_shape=jax.ShapeDtypeStruct((lay.RT, D), BF),
        compiler_params=pltpu.CompilerParams(vmem_limit_bytes=VMEM_LIMIT))(*[shards[n] for n in ORDER])


def _load_ffn_weights(srcs, offs, scratch, sem):
    @pl.when(pl.program_id(0) == 0)
    def _():
        cps = [pltpu.make_async_copy(s.at[pl.ds(off, dst.shape[0]), :], dst, sem.at[i])
               for i, (s, off, dst) in enumerate(zip(srcs, offs, scratch))]
        for cp in cps:
            cp.start()
        for cp in cps:
            cp.wait()


def _final_loss_tile(xf, g, tgt, s_ref):
    r = lax.rsqrt(jnp.mean(xf * xf, axis=-1, keepdims=True) + EPS)
    xr = xf * r
    e = xr * g - tgt
    s_ref[1:2, :] += jnp.sum(e * e, axis=0, keepdims=True) * (0.5 / D)
    dy = e * (1.0 / D)
    s_ref[0:1, :] += jnp.sum(dy * xr, axis=0, keepdims=True)
    gdy = dy * g
    return r * gdy - xr * (r * jnp.mean(gdy * xr, axis=-1, keepdims=True))


def _ffn_fwd(x, g, wbufs, offs, name, comm=None, final=None):
    nf = F // FC

    def body(x_ref, g_ref, b0, b1, b2, *rest):
        if final is None:
            h_ref, n_ref, gg_ref, uu_ref, a_ref, wg_s, wu_s, wd_s, sem = rest
        else:
            gf_ref, t_ref, dh_ref, dhb_ref, s_ref, n_ref, gg_ref, uu_ref, a_ref, wg_s, wu_s, wd_s, sem = rest

            @pl.when(pl.program_id(0) == 0)
            def _():
                s_ref[...] = jnp.zeros_like(s_ref)

        _load_ffn_weights((b0, b1, b2), offs, (wg_s, wu_s, wd_s), sem)
        for k in range(2):
            rs = slice(k * (TM // 2), (k + 1) * (TM // 2))
            xf = x_ref[rs, :]
            r = lax.rsqrt(jnp.mean(xf * xf, axis=-1, keepdims=True) + EPS)
            nb = (xf * r * g_ref[...]).astype(BF)
            n_ref[rs, :] = nb
            acc = jnp.zeros((TM // 2, D), F32)
            for c in range(nf):
                sl = slice(c * FC, (c + 1) * FC)
                gb = _nt(nb, wg_s[sl, :]).astype(BF)
                ub = _nt(nb, wu_s[sl, :]).astype(BF)
                gg_ref[rs, sl] = gb
                uu_ref[rs, sl] = ub
                a = (gb * _sig(gb)) * ub
                a_ref[0, rs, sl] = a
                acc = acc + _nn(a, wd_s[sl, :])
            h = xf + 0.5 * acc
            if final is None:
                h_ref[rs, :] = h
            else:
                dh = _final_loss_tile(h, gf_ref[...], t_ref[rs, :], s_ref)
                dh_ref[rs, :] = dh
                dhb_ref[rs, :] = (0.5 * dh).astype(BF)

    row = lambda i: (i, 0)
    vec = pl.BlockSpec((1, D), lambda i: (0, 0))
    tile = pl.BlockSpec((TM, D), row)
    saved_shapes = [jax.ShapeDtypeStruct((T, D), BF), jax.ShapeDtypeStruct((T, F), BF), jax.ShapeDtypeStruct((T, F), BF),
                    jax.ShapeDtypeStruct((1, T, F), BF)]
    saved_specs = [tile, pl.BlockSpec((TM, F), row), pl.BlockSpec((TM, F), row),
                   pl.BlockSpec((1, TM, F), lambda i: (0, i, 0))]
    if final is None:
        extra_args, extra_specs = [], []
        head_shapes, head_specs = [jax.ShapeDtypeStruct((T, D), F32)], [tile]
    else:
        extra_args, extra_specs = list(final), [vec, tile]
        head_shapes = [jax.ShapeDtypeStruct((T, D), F32), jax.ShapeDtypeStruct((T, D), BF), jax.ShapeDtypeStruct((8, D), F32)]
        head_specs = [tile, tile, pl.BlockSpec((8, D), lambda i: (0, 0))]
    return _call(
        body, name=name, grid=(T // TM,), args=[x, g, *wbufs, *extra_args], comm=comm,
        in_specs=[tile, vec, ANY, ANY, ANY] + extra_specs,
        out_shape=head_shapes + saved_shapes, out_specs=head_specs + saved_specs,
        scratch_shapes=[pltpu.VMEM((F, D), BF)] * 3 + [pltpu.SemaphoreType.DMA((3,))])


def _ffn_gate_up(x, g, wbufs, offs, name, comm=None):
    nf = F // FC

    def body(x_ref, g_ref, b0, b1, n_ref, gg_ref, uu_ref, a_ref, wg_s, wu_s, sem):
        _load_ffn_weights((b0, b1), offs, (wg_s, wu_s), sem)
        xf = x_ref[...]
        r = lax.rsqrt(jnp.mean(xf * xf, axis=-1, keepdims=True) + EPS)
        nb = (xf * r * g_ref[...]).astype(BF)
        n_ref[...] = nb
        for c in range(nf):
            sl = slice(c * FC, (c + 1) * FC)
            gb = _nt(nb, wg_s[sl, :]).astype(BF)
            ub = _nt(nb, wu_s[sl, :]).astype(BF)
            gg_ref[:, sl] = gb
            uu_ref[:, sl] = ub
            a_ref[0, :, sl] = (gb * _sig(gb)) * ub

    row = lambda i: (i, 0)
    tile = pl.BlockSpec((TM, D), row)
    return _call(
        body, name=name, grid=(T // TM,), args=[x, g, *wbufs], comm=comm,
        in_specs=[tile, pl.BlockSpec((1, D), lambda i: (0, 0)), ANY, ANY],
        out_shape=[jax.ShapeDtypeStruct((T, D), BF), jax.ShapeDtypeStruct((T, F), BF), jax.ShapeDtypeStruct((T, F), BF),
                   jax.ShapeDtypeStruct((1, T, F), BF)],
        out_specs=[tile, pl.BlockSpec((TM, F), row), pl.BlockSpec((TM, F), row),
                   pl.BlockSpec((1, TM, F), lambda i: (0, i, 0))],
        scratch_shapes=[pltpu.VMEM((F, D), BF)] * 2 + [pltpu.SemaphoreType.DMA((2,))])


def _ffn_down(x, act, wbuf, off, name, comm=None):
    def body(x_ref, a_ref, b0, h_ref, wd_s, sem):
        _load_ffn_weights((b0,), (off,), (wd_s,), sem)
        h_ref[...] = x_ref[...] + 0.5 * _nn(a_ref[0], wd_s[...])

    tile = pl.BlockSpec((TM, D), lambda i: (i, 0))
    return _call(
        body, name=name, grid=(T // TM,), args=[x, act, wbuf], comm=comm,
        in_specs=[tile, pl.BlockSpec((1, TM, F), lambda i: (0, i, 0)), ANY],
        out_shape=[jax.ShapeDtypeStruct((T, D), F32)], out_specs=[tile],
        scratch_shapes=[pltpu.VMEM((F, D), BF), pltpu.SemaphoreType.DMA((1,))])


def _load_in_proj(parts, w_s, sem):
    @pl.when(pl.program_id(0) == 0)
    def _():
        shard = NG * D // NDEV
        rows = shard // len(parts)
        cps = [pltpu.make_async_copy(buf.at[pl.ds(first + k * rows, rows), :],
                                     w_s.at[pl.ds(k * shard + p * rows, rows), :], sem.at[p * NDEV + k])
               for p, (buf, first) in enumerate(parts) for k in range(NDEV)]
        for cp in cps:
            cp.start()
        for cp in cps:
            cp.wait()


def _mix_in(h1, gm, win, comm=None):
    def body(h_ref, g_ref, *rest):
        w_any, (u_ref, z_ref, w_s, sem) = rest[:len(win)], rest[len(win):]
        _load_in_proj([(b, first) for b, (_, first) in zip(w_any, win)], w_s, sem)
        xf = h_ref[...]
        r = lax.rsqrt(jnp.mean(xf * xf, axis=-1, keepdims=True) + EPS)
        ub = (xf * r * g_ref[...]).astype(BF)
        u_ref[...] = ub
        for j in range(NG):
            z_ref[j] = _nt(ub, w_s[j * D:(j + 1) * D, :]).astype(BF)

    row = lambda i: (i, 0)
    return _call(
        body, name="mix_in", grid=(T // TM,), args=[h1, gm] + [b for b, _ in win], comm=comm,
        in_specs=[pl.BlockSpec((TM, D), row), pl.BlockSpec((1, D), lambda i: (0, 0))] + [ANY] * len(win),
        out_shape=[jax.ShapeDtypeStruct((T, D), BF), jax.ShapeDtypeStruct((NG, T, D), BF)],
        out_specs=[pl.BlockSpec((TM, D), row), pl.BlockSpec((NG, TM, D), lambda i: (0, i, 0))],
        scratch_shapes=[pltpu.VMEM((NG * D, D), BF), pltpu.SemaphoreType.DMA((NDEV * len(win),))])


def _shift_up(w, b):
    return w if b == 0 else pltpu.roll(w, w.shape[0] - b, 0)


def _fold8(p):
    red = p[0:8, :]
    for i in range(1, p.shape[0] // 8):
        red = red + p[8 * i:8 * i + 8, :]
    return red


def _dft_constants():
    import numpy as np
    nh = NB // 2
    f, n = np.arange(nh)[:, None], np.arange(NB)[None, :]
    ang = 2.0 * np.pi / NB * f * n
    fc = np.cos(ang)
    fs = np.where(f == 0, (-1.0) ** n, np.sin(ang))
    scale = np.where(f == 0, 1.0, 2.0) / NB
    ic = (scale * np.cos(ang)).T
    isn = np.where(f == 0, (-1.0) ** n / NB, scale * np.sin(ang)).T
    d = (KA - 1 - np.arange(32))[None, :]
    valid = (np.arange(32) < KA)[None, :]
    angk = 2.0 * np.pi / NB * f * d
    kc = np.where(valid, np.cos(angk), 0.0)
    ks = np.where(valid, np.sin(angk), 0.0)
    k2 = np.where(valid, np.where(f == 0, (-1.0) ** d, np.cos(angk)), 0.0)
    rtc = np.where(valid, scale * np.cos(angk), 0.0).T
    rts = np.where(valid, np.where(f == 0, (-1.0) ** d / NB, scale * np.sin(angk)), 0.0).T

    def bf(a):
        return jnp.asarray(a, F32).astype(BF)

    def split(a):
        hi = bf(a)
        return hi, (jnp.asarray(a, F32) - hi.astype(F32)).astype(BF)

    return dict(fc=bf(fc), fs=bf(fs), ic_hi=bf(ic[HB:]), is_hi=bf(isn[HB:]), ic_lo=bf(ic[:HB]), is_lo=bf(isn[:HB]),
                kc=split(kc), ks=split(ks), k2=split(k2), rtc=split(rtc), rts=split(rts))


def _dot3(m_hi, m_lo, x):
    x_hi = x.astype(BF)
    x_lo = (x - x_hi.astype(F32)).astype(BF)
    return _nn(m_hi, x_hi) + _nn(m_hi, x_lo) + _nn(m_lo, x_hi)


def _whole(a):
    return pl.BlockSpec(a.shape, lambda c, t: (0,) * a.ndim)


def _filter_spectrum(cw_ref, tabs, hc, hs, h2):
    w32 = cw_ref[0:32, :]
    for (hi, lo), dst in zip(tabs, (hc, hs, h2)):
        dst[...] = _dot3(hi[...], lo[...], w32)


def _conv_fwd_dft(z, cw, bias, dft, comm=None):
    nt = T // TB
    hb = TB // HB

    def body(z_ref, zh_ref, cw_ref, b_ref, fc_ref, fs_ref, ic_ref, is_ref, kch, kcl, ksh, ksl, k2h, k2l,
             a1_ref, q_ref, aext, ppad, hc, hs, h2):
        first = pl.program_id(1) == 0
        f = lambda ref, j: ref[j].astype(F32)

        @pl.when(first)
        def _():
            _filter_spectrum(cw_ref, ((kch, kcl), (ksh, ksl), (k2h, k2l)), hc, hs, h2)

        aext[0:HB, :] = jnp.where(first, 0.0, f(zh_ref, 0) * _sig(f(zh_ref, 1))).astype(BF)
        aext[HB:, :] = (f(z_ref, 0) * _sig(f(z_ref, 1))).astype(BF)
        ppad[0:8, :] = jnp.where(first, 0.0, f(zh_ref, 3)[HB - 8:HB, :] * f(zh_ref, 4)[HB - 8:HB, :])
        ppad[8:, :] = f(z_ref, 3) * f(z_ref, 4)
        bias_row = b_ref[...]

        for j in range(TB // HB):
            xs = aext[j * HB:j * HB + NB, :]
            xa, xb = _nn(fc_ref[...], xs), _nn(fs_ref[...], xs)
            yc = (hc[...] * xa - hs[...] * xb).astype(BF)
            ys = (h2[...] * xb + hs[...] * xa).astype(BF)
            y = _nn(ic_ref[...], yc) + _nn(is_ref[...], ys)
            a1_ref[j * HB:(j + 1) * HB, :] = (y + bias_row).astype(BF)

        def chunk(r, carry):
            base = pl.multiple_of(r * CHB, CHB)
            pw = ppad[pl.ds(base, CHB + 8), :]
            v = (cw_ref[pl.ds(32, 1), :] * _shift_up(pw, 6)[0:CHB, :]
                 + cw_ref[pl.ds(33, 1), :] * _shift_up(pw, 7)[0:CHB, :]
                 + cw_ref[pl.ds(34, 1), :] * pw[8:8 + CHB, :])
            q_ref[pl.ds(base, CHB), :] = (z_ref[2, pl.ds(base, CHB), :].astype(F32) * v).astype(BF)
            return carry

        lax.fori_loop(0, TB // CHB, chunk, 0)

    blk = pl.BlockSpec((TB, CW), lambda c, t: (t, c))
    tabs = [dft["fc"], dft["fs"], dft["ic_hi"], dft["is_hi"], *dft["kc"], *dft["ks"], *dft["k2"]]
    return _call(
        body, name="conv_fwd", grid=(D // CW, nt), comm=comm, args=[z, z, cw, bias] + tabs,
        in_specs=[pl.BlockSpec((5, TB, CW), lambda c, t: (0, t, c)),
                  pl.BlockSpec((5, HB, CW), lambda c, t: (0, jnp.maximum(t * hb - 1, 0), c)),
                  pl.BlockSpec((40, CW), lambda c, t: (0, c)), pl.BlockSpec((1, CW), lambda c, t: (0, c))]
                 + [_whole(a) for a in tabs],
        out_shape=[jax.ShapeDtypeStruct((T, D), BF), jax.ShapeDtypeStruct((T, D), BF)], out_specs=[blk, blk],
        scratch_shapes=[pltpu.VMEM((TB + HB, CW), BF), pltpu.VMEM((TB + 8, CW), F32)]
                       + [pltpu.VMEM((NB // 2, CW), F32)] * 3)


def _conv_bwd_dft(z, da1, dq, dzg, cw, dft, comm=None):
    nt = T // TB
    hb = TB // HB
    last_h = T // HB - 1

    def body(z_ref, zp_ref, zn_ref, da1_ref, da1n_ref, dq_ref, dqn_ref, dzg_ref, cw_ref,
             fc_ref, fs_ref, ic_ref, is_ref, kch, kcl, ksh, ksl, k2h, k2l, rch, rcl, rsh, rsl,
             dz_ref, dwa_ref, dwb_ref, aext, dyext, ppad, dvpad, hc, hs, h2, rc, rs, nyq, acc_b):
        t = pl.program_id(1)
        first, last = t == 0, t == nt - 1
        f = lambda ref, j: ref[j].astype(F32)

        @pl.when(first)
        def _():
            _filter_spectrum(cw_ref, ((kch, kcl), (ksh, ksl), (k2h, k2l)), hc, hs, h2)
            rc[...] = jnp.zeros_like(rc)
            rs[...] = jnp.zeros_like(rs)
            nyq[...] = jnp.zeros_like(nyq)
            acc_b[...] = jnp.zeros_like(acc_b)

        aext[0:HB, :] = jnp.where(first, 0.0, f(zp_ref, 0) * _sig(f(zp_ref, 1))).astype(BF)
        aext[HB:, :] = (f(z_ref, 0) * _sig(f(z_ref, 1))).astype(BF)
        dyext[0:TB, :] = da1_ref[...]
        dyext[TB:, :] = jnp.where(last, 0.0, da1n_ref[...].astype(F32)).astype(BF)
        ppad[0:8, :] = jnp.where(first, 0.0, f(zp_ref, 3)[HB - 8:HB, :] * f(zp_ref, 4)[HB - 8:HB, :])
        ppad[8:, :] = f(z_ref, 3) * f(z_ref, 4)
        dvpad[0:TB, :] = dq_ref[...].astype(F32) * f(z_ref, 2)
        dvpad[TB:, :] = jnp.where(last, 0.0, dqn_ref[...].astype(F32)[0:8, :] * f(zn_ref, 2)[0:8, :])

        for j in range(TB // HB):
            rows = slice(j * HB, (j + 1) * HB)
            dys = dyext[j * HB:j * HB + NB, :]
            da, db = _nn(fc_ref[...], dys), _nn(fs_ref[...], dys)
            gc = (hc[...] * da + hs[...] * db).astype(BF)
            gs = (h2[...] * db - hs[...] * da).astype(BF)
            da0 = _nn(ic_ref[...], gc) + _nn(is_ref[...], gs)
            z0, z1 = z_ref[0, rows, :].astype(F32), z_ref[1, rows, :].astype(F32)
            s1 = _sig(z1)
            dz_ref[0, rows, :] = (da0 * s1).astype(BF)
            dz_ref[1, rows, :] = (da0 * z0 * (s1 * (1.0 - s1))).astype(BF)
            xs = aext[j * HB:j * HB + NB, :]
            xa, xb = _nn(fc_ref[...], xs), _nn(fs_ref[...], xs)
            dyb = dyext[rows, :]
            pa, pb = _nn(fc_ref[:, HB:NB], dyb), _nn(fs_ref[:, HB:NB], dyb)
            rc[...] += pa * xa + pb * xb
            rs[...] += pb * xa - pa * xb
            nyq[...] += pb[0:8, :] * xb[0:8, :]

        def chunk(r, carry):
            base = pl.multiple_of(r * CHB, CHB)
            rows = pl.ds(base, CHB)
            pw = ppad[pl.ds(base, CHB + 8), :]
            p6 = _shift_up(pw, 6)[0:CHB, :]
            p7 = _shift_up(pw, 7)[0:CHB, :]
            p8 = pw[8:8 + CHB, :]
            wb0, wb1, wb2 = cw_ref[pl.ds(32, 1), :], cw_ref[pl.ds(33, 1), :], cw_ref[pl.ds(34, 1), :]
            v = wb0 * p6 + wb1 * p7 + wb2 * p8
            dz_ref[2, rows, :] = (dq_ref[rows, :].astype(F32) * v).astype(BF)
            dvw = dvpad[pl.ds(base, CHB + 8), :]
            dvc = dvw[0:CHB, :]
            dp = wb2 * dvc + wb1 * _shift_up(dvw, 1)[0:CHB, :] + wb0 * _shift_up(dvw, 2)[0:CHB, :]
            dz_ref[3, rows, :] = (dp * z_ref[4, rows, :].astype(F32)).astype(BF)
            dz_ref[4, rows, :] = (dp * z_ref[3, rows, :].astype(F32)).astype(BF)
            acc_b[0:8, :] += _fold8(dvc * p6)
            acc_b[8:16, :] += _fold8(dvc * p7)
            acc_b[16:24, :] += _fold8(dvc * p8)
            dz_ref[5, rows, :] = dzg_ref[0, rows, :]
            dz_ref[6, rows, :] = dzg_ref[1, rows, :]
            return carry

        lax.fori_loop(0, TB // CHB, chunk, 0)

        @pl.when(last)
        def _():
            row0 = lax.broadcasted_iota(jnp.int32, (NB // 2, CW), 0) == 0
            ny = jnp.broadcast_to(nyq[0:1, :], (NB // 2, CW))
            rcv = jnp.where(row0, rc[...] - ny, rc[...])
            rsv = jnp.where(row0, ny, rs[...])
            dwa_ref[...] = _dot3(rch[...], rcl[...], rcv) + _dot3(rsh[...], rsl[...], rsv)
            for k in range(KB):
                dwb_ref[k:k + 1, :] = jnp.sum(acc_b[8 * k:8 * k + 8, :], axis=0, keepdims=True)
            dwb_ref[KB:8, :] = jnp.zeros((8 - KB, CW), F32)

    blk = lambda c, t: (t, c)
    nxt = lambda c, t: (jnp.minimum((t + 1) * hb, last_h), c)
    tabs = [dft["fc"], dft["fs"], dft["ic_lo"], dft["is_lo"], *dft["kc"], *dft["ks"], *dft["k2"], *dft["rtc"], *dft["rts"]]
    return _call(
        body, name="conv_bwd", grid=(D // CW, nt), comm=comm, args=[z, z, z, da1, da1, dq, dq, dzg, cw] + tabs,
        in_specs=[pl.BlockSpec((5, TB, CW), lambda c, t: (0, t, c)),
                  pl.BlockSpec((5, HB, CW), lambda c, t: (0, jnp.maximum(t * hb - 1, 0), c)),
                  pl.BlockSpec((5, HB, CW), lambda c, t: (0, jnp.minimum((t + 1) * hb, last_h), c)),
                  pl.BlockSpec((TB, CW), blk), pl.BlockSpec((HB, CW), nxt),
                  pl.BlockSpec((TB, CW), blk), pl.BlockSpec((HB, CW), nxt),
                  pl.BlockSpec((2, TB, CW), lambda c, t: (0, t, c)),
                  pl.BlockSpec((40, CW), lambda c, t: (0, c))]
                 + [_whole(a) for a in tabs],
        out_shape=[jax.ShapeDtypeStruct((NG, T, D), BF), jax.ShapeDtypeStruct((32, D), F32),
                   jax.ShapeDtypeStruct((8, D), F32)],
        out_specs=[pl.BlockSpec((NG, TB, CW), lambda c, t: (0, t, c)),
                   pl.BlockSpec((32, CW), lambda c, t: (0, c)), pl.BlockSpec((8, CW), lambda c, t: (0, c))],
        scratch_shapes=[pltpu.VMEM((TB + HB, CW), BF), pltpu.VMEM((TB + HB, CW), BF),
                        pltpu.VMEM((TB + 8, CW), F32), pltpu.VMEM((TB + 8, CW), F32)]
                       + [pltpu.VMEM((NB // 2, CW), F32)] * 5 + [pltpu.VMEM((8, CW), F32), pltpu.VMEM((24, CW), F32)])


def _layernorm_silu(a1, lng, lnb):
    mu = jnp.mean(a1, axis=-1, keepdims=True)
    xc = a1 - mu
    rs = lax.rsqrt(jnp.mean(xc * xc, axis=-1, keepdims=True) + EPS)
    xh = xc * rs
    a2 = xh * lng + lnb
    sg = _sig(a2)
    return xh, rs, a2, sg


def _square_specs(blocks):
    return [pl.BlockSpec((D, D), lambda i, b=b: (b, 0)) for b in blocks]


def _mix_out(a1, q, z, h1, lng, lnb, wsq, comm=None):
    def body(a1_ref, q_ref, ga_ref, gb_ref, h_ref, lng_ref, lnb_ref, wa_ref, wb_ref, wo_ref, h2_ref, ya_ref, yb_ref):
        _, _, a2, sg = _layernorm_silu(a1_ref[...].astype(F32), lng_ref[...], lnb_ref[...])
        ya = _nn((a2 * sg).astype(BF), wa_ref[...])
        yb = _nn(q_ref[...], wb_ref[...])
        ya_ref[...] = ya.astype(BF)
        yb_ref[...] = yb.astype(BF)
        m = _sig(ga_ref[...].astype(F32)) * ya + _sig(gb_ref[...].astype(F32)) * yb
        h2_ref[...] = h_ref[...] + _nn(m.astype(BF), wo_ref[...])

    row = lambda i: (i, 0)
    vec = pl.BlockSpec((1, D), lambda i: (0, 0))
    return _call(
        body, name="mix_out", grid=(T // TM,), args=[a1, q, z, z, h1, lng, lnb, wsq, wsq, wsq], comm=comm,
        in_specs=[pl.BlockSpec((TM, D), row), pl.BlockSpec((TM, D), row),
                  pl.BlockSpec((None, TM, D), lambda i: (5, i, 0)), pl.BlockSpec((None, TM, D), lambda i: (6, i, 0)),
                  pl.BlockSpec((TM, D), row), vec, vec] + _square_specs((0, 1, 2)),
        out_shape=[jax.ShapeDtypeStruct((T, D), F32), jax.ShapeDtypeStruct((T, D), BF), jax.ShapeDtypeStruct((T, D), BF)],
        out_specs=[pl.BlockSpec((TM, D), row)] * 3)


def _rmsnorm_bwd(xf, g, dn):
    r = lax.rsqrt(jnp.mean(xf * xf, axis=-1, keepdims=True) + EPS)
    xr = xf * r
    gdn = dn * g
    dx = r * gdn - xr * (r * jnp.mean(gdn * xr, axis=-1, keepdims=True))
    return dx, jnp.sum(dn * xr, axis=0, keepdims=True)


def _ffn_bwd_hidden(dh, gg, uu, wbuf, off, name, comm=None):
    nf = F // FC

    def body(dh_ref, gg_ref, uu_ref, b0, dgu_ref, wd_s, sem):
        _load_ffn_weights((b0,), (off,), (wd_s,), sem)
        dhb = dh_ref[...]
        for c in range(nf):
            sl = slice(c * FC, (c + 1) * FC)
            da = _nt(dhb, wd_s[sl, :]).astype(BF)
            gb, ub = gg_ref[:, sl], uu_ref[:, sl]
            sg = _sig(gb)
            dgu_ref[0, :, sl] = (da * ub) * (sg * (1.0 + gb * (1.0 - sg)))
            dgu_ref[0, :, F + c * FC:F + (c + 1) * FC] = da * (gb * sg)

    row = lambda i: (i, 0)
    return _call(
        body, name=name, grid=(T // TM,), args=[dh, gg, uu, wbuf], comm=comm,
        in_specs=[pl.BlockSpec((TM, D), row), pl.BlockSpec((TM, F), row), pl.BlockSpec((TM, F), row), ANY],
        out_shape=[jax.ShapeDtypeStruct((1, T, 2 * F), BF)],
        out_specs=[pl.BlockSpec((1, TM, 2 * F), lambda i: (0, i, 0))],
        scratch_shapes=[pltpu.VMEM((F, D), BF), pltpu.SemaphoreType.DMA((1,))])


def _ffn_bwd_input(dgu, dh, x, g, wbufs, offs, name, comm=None, after=None):
    def body(dgu_ref, dh_ref, x_ref, g_ref, b0, b1, dx_ref, s_ref, w_s, sem):
        _load_ffn_weights((b0, b1), offs, (w_s.at[pl.ds(0, F), :], w_s.at[pl.ds(F, F), :]), sem)

        @pl.when(pl.program_id(0) == 0)
        def _():
            s_ref[...] = jnp.zeros_like(s_ref)

        dn = _nn(dgu_ref[0], w_s[...])
        dxn, dg = _rmsnorm_bwd(x_ref[...], g_ref[...], dn)
        dx_ref[...] = dh_ref[...] + dxn
        s_ref[0:1, :] += dg

    row = lambda i: (i, 0)
    return _call(
        body, name=name, grid=(T // TM,), args=[dgu, dh, x, g, *wbufs], comm=comm, after=after,
        in_specs=[pl.BlockSpec((1, TM, 2 * F), lambda i: (0, i, 0)), pl.BlockSpec((TM, D), row),
                  pl.BlockSpec((TM, D), row), pl.BlockSpec((1, D), lambda i: (0, 0)), ANY, ANY],
        out_shape=[jax.ShapeDtypeStruct((T, D), F32), jax.ShapeDtypeStruct((8, D), F32)],
        out_specs=[pl.BlockSpec((TM, D), row), pl.BlockSpec((8, D), lambda i: (0, 0))],
        scratch_shapes=[pltpu.VMEM((2 * F, D), BF), pltpu.SemaphoreType.DMA((2,))])


def _tn_matmul(lhs, rhs, tr, name, comm=None):
    ng, _, cdim = lhs.shape
    nc, nk = cdim // tr, T // TK
    if rhs.ndim == 2:
        r_spec = pl.BlockSpec((TK, D), lambda g, c, k: (k, 0))
    else:
        r_spec = pl.BlockSpec((None, TK, D), lambda g, c, k: (g, k, 0))

    def body(l_ref, r_ref, o_ref, acc):
        k = pl.program_id(2)

        @pl.when(k == 0)
        def _():
            acc[...] = jnp.zeros_like(acc)

        acc[...] += _tn(l_ref[...], r_ref[...])

        @pl.when(k == nk - 1)
        def _():
            o_ref[...] = acc[...].astype(BF)

    return _call(
        body, name=name, grid=(ng, nc, nk), args=[lhs, rhs], comm=comm,
        in_specs=[pl.BlockSpec((None, TK, tr), lambda g, c, k: (g, k, c)), r_spec],
        out_shape=[jax.ShapeDtypeStruct((ng * cdim, D), BF)],
        out_specs=[pl.BlockSpec((tr, D), lambda g, c, k: (g * nc + c, 0))],
        scratch_shapes=[pltpu.VMEM((tr, D), F32)])


def _mix_out_bwd(dh2, ya, yb, z, a1, q, lng, lnb, wsq, comm=None):
    def body(dh_ref, ya_ref, yb_ref, ga_ref, gb_ref, a1_ref, q_ref, lng_ref, lnb_ref, wa_ref, wb_ref, wo_ref,
             dzg_ref, da1_ref, dq_ref, l_ref, r_ref, s_ref):
        @pl.when(pl.program_id(0) == 0)
        def _():
            s_ref[...] = jnp.zeros_like(s_ref)

        dhb = dh_ref[...].astype(BF)
        dm = _nt(dhb, wo_ref[...]).astype(BF)
        ya, yb = ya_ref[...], yb_ref[...]
        sa, sb = _sig(ga_ref[...]), _sig(gb_ref[...])
        l_ref[0] = sa * ya + sb * yb
        l_ref[2] = q_ref[...]
        dzg_ref[0] = (dm * ya) * (sa * (1.0 - sa))
        dzg_ref[1] = (dm * yb) * (sb * (1.0 - sb))
        dya = dm * sa
        dyb = dm * sb
        r_ref[0] = dhb
        r_ref[1] = dya
        r_ref[2] = dyb
        dq_ref[...] = _nt(dyb, wb_ref[...]).astype(BF)
        da3 = _nt(dya, wa_ref[...])
        lng = lng_ref[...]
        xh, rs, a2, sg = _layernorm_silu(a1_ref[...].astype(F32), lng, lnb_ref[...])
        l_ref[1] = (a2 * sg).astype(BF)
        da2 = da3 * (sg * (1.0 + a2 * (1.0 - sg)))
        s_ref[0:1, :] += jnp.sum(da2 * xh, axis=0, keepdims=True)
        s_ref[1:2, :] += jnp.sum(da2, axis=0, keepdims=True)
        dxh = da2 * lng
        da1 = rs * (dxh - jnp.mean(dxh, axis=-1, keepdims=True) - xh * jnp.mean(dxh * xh, axis=-1, keepdims=True))
        da1_ref[...] = da1.astype(BF)
        s_ref[2:3, :] += jnp.sum(da1, axis=0, keepdims=True)

    row = lambda i: (i, 0)
    row3 = lambda i: (0, i, 0)
    vec = pl.BlockSpec((1, D), lambda i: (0, 0))
    return _call(
        body, name="mix_out_bwd", grid=(T // TM,), args=[dh2, ya, yb, z, z, a1, q, lng, lnb, wsq, wsq, wsq], comm=comm,
        in_specs=[pl.BlockSpec((TM, D), row), pl.BlockSpec((TM, D), row), pl.BlockSpec((TM, D), row),
                  pl.BlockSpec((None, TM, D), lambda i: (5, i, 0)), pl.BlockSpec((None, TM, D), lambda i: (6, i, 0)),
                  pl.BlockSpec((TM, D), row), pl.BlockSpec((TM, D), row), vec, vec] + _square_specs((0, 1, 2)),
        out_shape=[jax.ShapeDtypeStruct((2, T, D), BF), jax.ShapeDtypeStruct((T, D), BF),
                   jax.ShapeDtypeStruct((T, D), BF), jax.ShapeDtypeStruct((3, T, D), BF),
                   jax.ShapeDtypeStruct((3, T, D), BF), jax.ShapeDtypeStruct((8, D), F32)],
        out_specs=[pl.BlockSpec((2, TM, D), row3), pl.BlockSpec((TM, D), row), pl.BlockSpec((TM, D), row),
                   pl.BlockSpec((3, TM, D), row3), pl.BlockSpec((3, TM, D), row3), pl.BlockSpec((8, D), lambda i: (0, 0))])


def _mix_in_bwd(dz, dh2, h1, gm, win, comm=None):
    def body(dz_ref, dh_ref, h_ref, g_ref, *rest):
        w_any, (o_ref, ob_ref, s_ref, w_s, sem) = rest[:len(win)], rest[len(win):]
        _load_in_proj([(b, first) for b, (_, first) in zip(w_any, win)], w_s, sem)

        @pl.when(pl.program_id(0) == 0)
        def _():
            s_ref[...] = jnp.zeros_like(s_ref)

        du = _nn(dz_ref[0], w_s[0:D, :])
        for j in range(1, NG):
            du = du + _nn(dz_ref[j], w_s[j * D:(j + 1) * D, :])
        dx, dg = _rmsnorm_bwd(h_ref[...], g_ref[...], du)
        dh1 = dh_ref[...] + dx
        o_ref[...] = dh1
        ob_ref[...] = (0.5 * dh1).astype(BF)
        s_ref[0:1, :] += dg

    row = lambda i: (i, 0)
    return _call(
        body, name="mix_in_bwd", grid=(T // TM,), args=[dz, dh2, h1, gm] + [b for b, _ in win], comm=comm,
        in_specs=[pl.BlockSpec((NG, TM, D), lambda i: (0, i, 0)), pl.BlockSpec((TM, D), row),
                  pl.BlockSpec((TM, D), row), pl.BlockSpec((1, D), lambda i: (0, 0))] + [ANY] * len(win),
        out_shape=[jax.ShapeDtypeStruct((T, D), F32), jax.ShapeDtypeStruct((T, D), BF), jax.ShapeDtypeStruct((8, D), F32)],
        out_specs=[pl.BlockSpec((TM, D), row), pl.BlockSpec((TM, D), row), pl.BlockSpec((8, D), lambda i: (0, 0))],
        scratch_shapes=[pltpu.VMEM((NG * D, D), BF), pltpu.SemaphoreType.DMA((NDEV * len(win),))])


def _row_tile(n, want, mult):
    for t in range(min(want, n), 0, -1):
        if n % t == 0 and t % mult == 0:
            return t
    return n


def _pack_small(s_ffn1, s_in, s_mix, s_ffn2, s_final, dwa, dwb):
    def body(f1, mi, mo, f2, fl, wa_ref, wb_ref, v_ref, k_ref):
        for dst, (ref, row) in enumerate(((f1, 0), (mi, 0), (mo, 0), (mo, 1), (mo, 2), (f2, 0), (fl, 0), (fl, 1))):
            v_ref[dst:dst + 1, :] = ref[row:row + 1, :]
        for k in range(NDEV):
            k_ref[k, 0:32, :] = wa_ref[:, k * LANE:(k + 1) * LANE]
            k_ref[k, 32:40, :] = wb_ref[:, k * LANE:(k + 1) * LANE]

    return pl.pallas_call(
        body, name="pack_small",
        out_shape=(jax.ShapeDtypeStruct((8, D), F32), jax.ShapeDtypeStruct((NDEV, 40, LANE), F32)),
    )(s_ffn1, s_in, s_mix, s_ffn2, s_final, dwa, dwb)


def _adam_update(g, w, m, v):
    m2 = ADAM_B1 * m + (1.0 - ADAM_B1) * g
    v2 = ADAM_B2 * v + (1.0 - ADAM_B2) * (g * g)
    c1 = 1.0 - ADAM_B1 ** ADAM_STEP
    c2 = 1.0 - ADAM_B2 ** ADAM_STEP
    return -ADAM_LR * ((m2 / c1) / (jnp.sqrt(v2 / c2) + ADAM_EPS) + ADAM_WD * w), m2, v2


def _adam_small(vecs, convs, vec_params, tap_params):
    nv, nt = len(vec_params), len(tap_params)

    def body(*refs):
        v_ref, k_ref = refs[:2]
        p_refs = refs[2:2 + 3 * (nv + nt)]
        l_ref = refs[2 + 3 * (nv + nt)]
        o_refs = refs[3 + 3 * (nv + nt):]
        s, c = v_ref[0], k_ref[0]
        for k in range(1, NDEV):
            s = s + v_ref[k]
            c = c + k_ref[k]
        l_ref[...] = jnp.sum(s[7:8, :], axis=-1, keepdims=True)
        for i in range(nv):
            w_ref, m_ref, u_ref = p_refs[3 * i: 3 * i + 3]
            g_ref, d_ref, m2_ref, u2_ref = o_refs[4 * i: 4 * i + 4]
            g = s[i:i + 1, :]
            g_ref[...] = g
            d_ref[...], m2_ref[...], u2_ref[...] = _adam_update(g, w_ref[...], m_ref[...], u_ref[...])
        for i in range(nt):
            w_ref, m_ref, u_ref = p_refs[3 * (nv + i): 3 * (nv + i) + 3]
            g_ref, d_ref, m2_ref, u2_ref = o_refs[4 * (nv + i): 4 * (nv + i) + 4]
            first = tap_params[i][0]
            for k in range(w_ref.shape[0]):
                g = c[first + k:first + k + 1, :]
                g_ref[k] = g
                d_ref[k], m2_ref[k], u2_ref[k] = _adam_update(g, w_ref[k], m_ref[k], u_ref[k])

    params = [a for p in vec_params for a in p] + [a for p in tap_params for a in p[1:]]
    out_shape = [jax.ShapeDtypeStruct((1, 1), F32)]
    for p in list(vec_params) + [p[1:] for p in tap_params]:
        out_shape += [jax.ShapeDtypeStruct(p[0].shape, F32)] * 4
    outs = pl.pallas_call(body, name="adam_small", out_shape=tuple(out_shape))(vecs, convs, *params)
    groups = [tuple(outs[1 + 4 * i: 5 + 4 * i]) for i in range(nv + nt)]
    return outs[0], groups[:nv], groups[nv:]


def _adam_in_proj(parts, w, m, v, after):
    rows = w.shape[1]
    tr = _row_tile(D, 256, LANE)

    def body(*refs):
        p_refs = refs[:len(parts)]
        w_ref, m_ref, v_ref, g_ref, d_ref, m2_ref, v2_ref = refs[len(parts):]
        sums = []
        for p in p_refs:
            s = p[0].astype(F32)
            for k in range(1, p.shape[0]):
                s = s + p[k].astype(F32)
            sums.append(s)
        g = jnp.concatenate(sums, axis=0).T
        g_ref[...] = g
        d_ref[...], m2_ref[...], v2_ref[...] = _adam_update(g, w_ref[...], m_ref[...], v_ref[...])

    spec = pl.BlockSpec((tr, rows), lambda i: (i, 0))
    return _call(body, name="adam_in", grid=(D // tr,), args=list(parts) + [w, m, v], after=after,
                 in_specs=[pl.BlockSpec((p.shape[0], p.shape[1], tr), lambda i: (0, 0, i)) for p in parts] + [spec] * 3,
                 out_shape=[jax.ShapeDtypeStruct((D, rows), F32)] * 4, out_specs=[spec] * 4)


def _adam(gs, ws, ms, vs, name, after):
    n = len(gs)
    rows, cols = ws[0].shape
    tr = _row_tile(rows, min(256, rows // 2), 16)

    def body(*refs):
        for i in range(n):
            g_in, w, m, v = refs[4 * i], refs[4 * i + 1][...], refs[4 * i + 2][...], refs[4 * i + 3][...]
            g_ref, d_ref, m_ref, v_ref = refs[4 * n + 4 * i: 4 * n + 4 * i + 4]
            g = g_in[0].astype(F32)
            for k in range(1, g_in.shape[0]):
                g = g + g_in[k].astype(F32)
            g_ref[...] = g
            d_ref[...], m_ref[...], v_ref[...] = _adam_update(g, w, m, v)

    spec = pl.BlockSpec((tr, cols), lambda i: (i, 0))
    args, in_specs = [], []
    for i in range(n):
        slots, first = gs[i]
        args += [slots, ws[i], ms[i], vs[i]]
        in_specs += [pl.BlockSpec((slots.shape[0], tr, cols), lambda i, b=first // tr: (0, b + i, 0))] + [spec] * 3
    outs = _call(body, name=name, grid=(rows // tr,), args=args, in_specs=in_specs, after=after,
                 out_shape=[jax.ShapeDtypeStruct((rows, cols), F32)] * (4 * n), out_specs=[spec] * (4 * n))
    return [tuple(outs[4 * i: 4 * i + 4]) for i in range(n)]


def kernel(x, ffn1_norm, ffn1_w_gate, ffn1_w_up, ffn1_w_down, mix_norm, w_in, a_dw_w, a_dw_b, a_ln_g, a_ln_b, a_w_out, b_conv_w, b_w_out, w_o, ffn2_norm, ffn2_w_gate, ffn2_w_up, ffn2_w_down, final_norm, loss_target, m_ffn1_norm, m_ffn1_w_gate, m_ffn1_w_up, m_ffn1_w_down, m_mix_norm, m_w_in, m_a_dw_w, m_a_dw_b, m_a_ln_g, m_a_ln_b, m_a_w_out, m_b_conv_w, m_b_w_out, m_w_o, m_ffn2_norm, m_ffn2_w_gate, m_ffn2_w_up, m_ffn2_w_down, m_final_norm, v_ffn1_norm, v_ffn1_w_gate, v_ffn1_w_up, v_ffn1_w_down, v_mix_norm, v_w_in, v_a_dw_w, v_a_dw_b, v_a_ln_g, v_a_ln_b, v_a_w_out, v_b_conv_w, v_b_w_out, v_w_o, v_ffn2_norm, v_ffn2_w_gate, v_ffn2_w_up, v_ffn2_w_down, v_final_norm):
    names = ("ffn1_norm", "ffn1_w_gate", "ffn1_w_up", "ffn1_w_down", "mix_norm", "w_in", "a_dw_w", "a_dw_b",
             "a_ln_g", "a_ln_b", "a_w_out", "b_conv_w", "b_w_out", "w_o", "ffn2_norm", "ffn2_w_gate", "ffn2_w_up",
             "ffn2_w_down", "final_norm")
    w = dict(ffn1_norm=ffn1_norm, ffn1_w_gate=ffn1_w_gate, ffn1_w_up=ffn1_w_up, ffn1_w_down=ffn1_w_down,
             mix_norm=mix_norm, w_in=w_in, a_dw_w=a_dw_w, a_dw_b=a_dw_b, a_ln_g=a_ln_g, a_ln_b=a_ln_b,
             a_w_out=a_w_out, b_conv_w=b_conv_w, b_w_out=b_w_out, w_o=w_o, ffn2_norm=ffn2_norm,
             ffn2_w_gate=ffn2_w_gate, ffn2_w_up=ffn2_w_up, ffn2_w_down=ffn2_w_down, final_norm=final_norm)
    m = dict(ffn1_norm=m_ffn1_norm, ffn1_w_gate=m_ffn1_w_gate, ffn1_w_up=m_ffn1_w_up, ffn1_w_down=m_ffn1_w_down,
             mix_norm=m_mix_norm, w_in=m_w_in, a_dw_w=m_a_dw_w, a_dw_b=m_a_dw_b, a_ln_g=m_a_ln_g, a_ln_b=m_a_ln_b,
             a_w_out=m_a_w_out, b_conv_w=m_b_conv_w, b_w_out=m_b_w_out, w_o=m_w_o, ffn2_norm=m_ffn2_norm,
             ffn2_w_gate=m_ffn2_w_gate, ffn2_w_up=m_ffn2_w_up, ffn2_w_down=m_ffn2_w_down, final_norm=m_final_norm)
    v = dict(ffn1_norm=v_ffn1_norm, ffn1_w_gate=v_ffn1_w_gate, ffn1_w_up=v_ffn1_w_up, ffn1_w_down=v_ffn1_w_down,
             mix_norm=v_mix_norm, w_in=v_w_in, a_dw_w=v_a_dw_w, a_dw_b=v_a_dw_b, a_ln_g=v_a_ln_g, a_ln_b=v_a_ln_b,
             a_w_out=v_a_w_out, b_conv_w=v_b_conv_w, b_w_out=v_b_w_out, w_o=v_w_o, ffn2_norm=v_ffn2_norm,
             ffn2_w_gate=v_ffn2_w_gate, ffn2_w_up=v_ffn2_w_up, ffn2_w_down=v_ffn2_w_down, final_norm=v_final_norm)
    flat = _pack_weights(dict(wg1=ffn1_w_gate[0].T, wu1=ffn1_w_up[0].T, wd1=ffn1_w_down[0], wg2=ffn2_w_gate[0].T,
                              wu2=ffn2_w_up[0].T, wd2=ffn2_w_down[0], win=w_in[0], wa=a_w_out[0], wb=b_w_out[0],
                              wo=w_o[0]))
    cw_shard = jnp.concatenate([a_dw_w[0], jnp.zeros((1, LANE), F32), b_conv_w[0], jnp.zeros((5, LANE), F32)], axis=0)

    x2, tgt = x[0], loss_target[0]
    st_a, st_b, st_b2 = ("wg1", "wu1"), ("wd1", "win/0/2"), ("win/1/2",)
    st_c, st_d, st_e = ("wa", "wb", "wo", "wg2"), ("wu2",), ("wd2",)

    buf_a, cw = _run_comm(_join(_ag_comm(st_a, flat), _direct_comm(cw_shard, False)), "ag_ffn1")
    n1, gg1, uu1, act1, buf_b = _ffn_gate_up(x2, ffn1_norm, (buf_a, buf_a), (0, F), "ffn1_gate_up", _ag_comm(st_b, flat))
    h1, buf_b2 = _ffn_down(x2, act1, buf_b, 0, "ffn1_down", _ag_comm(st_b2, flat))
    win = ((buf_b, F), (buf_b2, 0))
    u, z, buf_c = _mix_in(h1, mix_norm, win, _ag_comm(st_c, flat))
    dft = _dft_constants()
    cw = jnp.transpose(cw, (1, 0, 2)).reshape(40, D)
    a1, q, buf_d = _conv_fwd_dft(z, cw, a_dw_b, dft, _ag_comm(st_d, flat))
    h2, ya, yb, buf_e = _mix_out(a1, q, z, h1, a_ln_g, a_ln_b, buf_c, _ag_comm(st_e, flat))
    ffn2_bufs, ffn2_offs = (buf_c, buf_d, buf_e), (3 * D, 0, 0)
    dh3, dhb3, s_final, n2, gg2, uu2, act2 = _ffn_fwd(h2, ffn2_norm, ffn2_bufs, ffn2_offs, "ffn2_fwd",
                                          final=(final_norm.reshape(1, D), tgt))

    tr_f = F // 2 if (F // 2) % LANE == 0 else F
    def pair(stage, src):
        return _rs_pair_comm(stage, src)

    def chip(stage, src, pair_buf, tag):
        return _rs_chip_comm(_pair_add(stage, src, pair_buf, "pair_add_" + tag))

    (dgu2,) = _ffn_bwd_hidden(dhb3, gg2, uu2, buf_e, 0, "ffn2_bwd_h")
    (gu2,) = _tn_matmul(dgu2, n2, tr_f, "dw_gu2")
    s2a, src2a = ("wg2", "wu2"), dict(wg2=(gu2, 0), wu2=(gu2, F))
    (gd2,) = _tn_matmul(act2, dhb3, tr_f, "dw_d2")
    s2b, src2b = ("wd2",), dict(wd2=(gd2, 0))
    dh2, s_ffn2, pair2a, pair2b = _ffn_bwd_input(dgu2, dh3, h2, ffn2_norm, (buf_c, buf_d), (3 * D, 0), "ffn2_bwd_x",
                                                 _join(pair(s2a, src2a), pair(s2b, src2b)))
    dzg, da1, dq, lsq, rsq, s_mix, recv2b = _mix_out_bwd(dh2, ya, yb, z, a1, q, a_ln_g, a_ln_b, buf_c,
                                                          chip(s2b, src2b, pair2b, "2b"))
    (gsq,) = _tn_matmul(lsq, rsq, D, "dw_square")
    ssq, srcsq = ("wa", "wb", "wo"), dict(wa=(gsq, D), wb=(gsq, 2 * D), wo=(gsq, 0))
    dz, dwa, dwb, recv2a, pairsq = _conv_bwd_dft(z, da1, dq, dzg, cw, dft,
                                                 _join(chip(s2a, src2a, pair2a, "2a"), pair(ssq, srcsq)))
    gin, recvsq = _tn_matmul(dz, u, D, "dw_in", chip(ssq, srcsq, pairsq, "sq"))
    sin_a, sin_b, srcin = ("win/0/2",), ("win/1/2",), {"win/0/2": (gin, 0), "win/1/2": (gin, 0)}
    dh1, dhb1, s_in, pairin_a, pairin_b = _mix_in_bwd(dz, dh2, h1, mix_norm, win,
                                                _join(pair(sin_a, srcin), pair(sin_b, srcin)))
    dgu1, recvin_a = _ffn_bwd_hidden(dhb1, gg1, uu1, buf_b, 0, "ffn1_bwd_h",
                                           chip(sin_a, srcin, pairin_a, "in_a"))
    gu1, recvin_b = _tn_matmul(dgu1, n1, tr_f, "dw_gu1", chip(sin_b, srcin, pairin_b, "in_b"))
    s1a, src1a = ("wg1", "wu1"), dict(wg1=(gu1, 0), wu1=(gu1, F))
    gd1, pair1a = _tn_matmul(act1, dhb1, tr_f, "dw_d1", pair(s1a, src1a))
    s1b, src1b = ("wd1",), dict(wd1=(gd1, 0))
    xchg1 = _join(chip(s1a, src1a, pair1a, "1a"), pair(s1b, src1b))
    xchg1_sems, xchg1_bufs, token = _comm_start(xchg1, "xchg_ffn1_start")
    dx, s_ffn1 = _ffn_bwd_input(dgu1, dh1, x2, ffn1_norm, (buf_a, buf_a), (0, F), "ffn1_bwd_x", after=token)
    (_, gd1), (recv1a, pair1b) = _comm_wait(xchg1, "xchg_ffn1_wait", xchg1_sems, xchg1_bufs, s_ffn1)
    src1b = dict(wd1=(gd1, 0))

    vec8, convk = _pack_small(s_ffn1, s_in, s_mix, s_ffn2, s_final, dwa, dwb)
    tail = _join(chip(s1b, src1b, pair1b, "1b"), _join(_direct_comm(vec8, False), _direct_comm(convk, True)))
    tail_sems, tail_bufs, token = _comm_start(tail, "xchg_tail_start")

    fs = F // NDEV
    g = dict(ffn1_w_gate=(recv1a, 0), ffn1_w_up=(recv1a, fs), ffn2_w_gate=(recv2a, 0), ffn2_w_up=(recv2a, fs),
             ffn2_w_down=(recv2b, 0), a_w_out=(recvsq, 0), b_w_out=(recvsq, D // NDEV), w_o=(recvsq, 2 * (D // NDEV)))
    grad, upd = {}, {}

    def run(group, name, after, as2d=lambda a: a[0], back=lambda a, n: a.reshape(w[n].shape)):
        res = _adam([g[n] for n in group], [as2d(w[n]) for n in group], [as2d(m[n]) for n in group],
                    [as2d(v[n]) for n in group], name, after)
        for n, r in zip(group, res):
            grad[n], upd[n] = back(r[0], n), tuple(back(a, n) for a in r[1:])
        return res[0][0]

    done = run(("ffn1_w_gate", "ffn1_w_up", "ffn2_w_gate", "ffn2_w_up"), "adam_gate_up", token,
               as2d=lambda a: a[0].T, back=lambda a, n: a.T[None])
    r_in = _adam_in_proj([recvin_a, recvin_b], w_in[0], m_w_in[0], v_w_in[0], done)
    grad["w_in"], upd["w_in"] = r_in[0][None], tuple(a[None] for a in r_in[1:])
    done = run(("a_w_out", "b_w_out", "w_o"), "adam_square", r_in[0])
    _, (recv1b, vec_all, conv_all) = _comm_wait(tail, "xchg_tail_wait", tail_sems, tail_bufs, done)
    g["ffn1_w_down"] = (recv1b, 0)
    run(("ffn1_w_down", "ffn2_w_down"), "adam_down", done)
    vec_names = ("ffn1_norm", "mix_norm", "a_ln_g", "a_ln_b", "a_dw_b", "ffn2_norm", "final_norm")
    tap_names, tap_rows = ("a_dw_w", "b_conv_w"), (0, 32)
    taps = lambda a: jnp.transpose(a, (1, 0, 2))
    loss, vec_res, tap_res = _adam_small(
        vec_all, conv_all, [tuple(t[n].reshape(1, D) for t in (w, m, v)) for n in vec_names],
        [(r,) + tuple(taps(t[n]) for t in (w, m, v)) for n, r in zip(tap_names, tap_rows)])
    for n, r in zip(vec_names, vec_res):
        grad[n], upd[n] = r[0].reshape(w[n].shape), tuple(a.reshape(w[n].shape) for a in r[1:])
    for n, r in zip(tap_names, tap_res):
        grad[n], upd[n] = taps(r[0]), tuple(taps(a) for a in r[1:])

    return (loss.reshape(()), dx.reshape(x.shape), *[grad[n] for n in names], *[upd[n][0] for n in names],
            *[upd[n][1] for n in names], *[upd[n][2] for n in names])
```

```python
import jax
import jax.numpy as jnp
from jax import lax
from jax.experimental import pallas as pl
from jax.experimental.pallas import tpu as pltpu

T = 4096
D = 1024
F = 2816
NG = 7
NDEV = 8
NCHIP = 4
KA, KB = 31, 3
EPS = 1e-6
ADAM_LR, ADAM_B1, ADAM_B2, ADAM_EPS, ADAM_WD, ADAM_STEP = 0.001, 0.9, 0.999, 1e-08, 0.01, 10

TM = 512
FC = 256
TB = 1024
NB = 256
HB = NB // 2
CW = 256
CHB = 64
LANE = 128
TK = 2048
VMEM_LIMIT = 56 * 1024 * 1024

BF = jnp.bfloat16
F32 = jnp.float32
MESH = pl.DeviceIdType.MESH
ANY = pl.BlockSpec(memory_space=pl.ANY)
COLLECTIVE_ID = {(1,): 0, (2, 4, 6): 1, (1, 2, 4, 6): 2, (1, 2, 4): 3, (1, 2, 3, 4, 5, 6, 7): 4}
START_COLLECTIVE_ID = {(1, 2, 4, 6): 5, (1, 2, 3, 4, 5, 6, 7): 6}

ORDER = ("wg1", "wu1", "wd1", "wg2", "wu2", "wd2", "win", "wa", "wb", "wo")


class _Layout:
    def __init__(self):
        fs, dis, ds = F // NDEV, NG * D // NDEV, D // NDEV
        self.rows = dict(wg1=fs, wu1=fs, wd1=fs, wg2=fs, wu2=fs, wd2=fs, win=dis, wa=ds, wb=ds, wo=ds)
        self.fl, off = {}, 0
        for n in ORDER:
            self.fl[n] = off
            off += self.rows[n]
        self.RT = off


class _Stage:
    def __init__(self, names):
        lay = _Layout()
        self.names = names
        self.rows, self.full, self.sub, self.fl = {}, {}, {}, {}
        for n in names:
            base, i, k = (n.split("/") + ["0", "1"])[:3]
            self.full[n] = lay.rows[base]
            self.rows[n] = lay.rows[base] // int(k)
            self.sub[n] = int(i) * self.rows[n]
            self.fl[n] = lay.fl[base] + self.sub[n]
        self.off, self.wc, o, w = {}, {}, 0, 0
        for n in names:
            self.off[n], self.wc[n] = o, w
            o += self.rows[n]
            w += NDEV * self.rows[n]
        self.R, self.W = o, w

    def grad_row(self, n, first, dev_lin):
        return first + dev_lin * self.full[n] + self.sub[n]


def _nt(a, b):
    return lax.dot_general(a, b, (((1,), (1,)), ((), ())), preferred_element_type=F32)


def _nn(a, b):
    return lax.dot_general(a, b, (((1,), (0,)), ((), ())), preferred_element_type=F32)


def _tn(a, b):
    return lax.dot_general(a, b, (((0,), (0,)), ((), ())), preferred_element_type=F32)


def _sig(x):
    return 1.0 / (1.0 + jnp.exp(-x))


def _position():
    return lax.axis_index("x"), lax.axis_index("y"), lax.axis_index("c")


def _peer(pos, j):
    x, y, c = pos
    return (1 - x if j & 4 else x, 1 - y if j & 2 else y, 1 - c if j & 1 else c)


def _lin(pos):
    return 4 * pos[0] + 2 * pos[1] + pos[2]


def _chip(pos):
    return 2 * pos[0] + pos[1]


class _Comm:
    def __init__(self, inputs, out_shapes, scratch, start, finish, middle=None, peers=None):
        self.inputs, self.out_shapes, self.scratch = inputs, out_shapes, scratch
        self.start, self.finish, self.middle = start, finish, middle
        self.peers = peers


def _handshake(peers):
    barrier = pltpu.get_barrier_semaphore()
    for j in peers:
        pl.semaphore_signal(barrier, inc=1, device_id=_peer(_position(), j), device_id_type=MESH)
    pl.semaphore_wait(barrier, len(peers))


def _call(body, *, name, grid, args, in_specs, out_shape, out_specs, scratch_shapes=(), comm=None,
          num_scalar_prefetch=0, after=None):
    in_specs, out_shape, out_specs, scratch_shapes = list(in_specs), list(out_shape), list(out_specs), list(scratch_shapes)
    if after is not None:
        inner, pos = body, num_scalar_prefetch + len(in_specs)
        body = lambda *refs: inner(*refs[:pos], *refs[pos + 1:])
        args, in_specs = list(args) + [after], in_specs + [ANY]
    n_in, n_out, n_scr = len(in_specs), len(out_shape), len(scratch_shapes)
    sp = num_scalar_prefetch
    if comm is None:
        kernel_fn = lambda *refs: body(*refs)
        c_in = c_out = c_scr = 0
    else:
        c_in, c_out, c_scr = len(comm.inputs), len(comm.out_shapes), len(comm.scratch)

        def kernel_fn(*refs):
            pre, refs = refs[:sp], refs[sp:]
            ins, cins = refs[:n_in], refs[n_in:n_in + c_in]
            o0 = n_in + c_in
            outs, couts = refs[o0:o0 + n_out], refs[o0 + n_out:o0 + n_out + c_out]
            s0 = o0 + n_out + c_out
            scr, cscr = refs[s0:s0 + n_scr], refs[s0 + n_scr:]
            step, steps = pl.program_id(0), grid[0]
            for a in range(1, len(grid)):
                step, steps = step * grid[a] + pl.program_id(a), steps * grid[a]
            first, last = step == 0, step == steps - 1

            @pl.when(first)
            def _():
                if comm.peers is not None:
                    _handshake(comm.peers)
                comm.start(cins, couts, cscr)

            if comm.middle is not None:
                @pl.when(step == (steps // 2 if steps > 2 else steps - 1))
                def _():
                    comm.middle(cins, couts, cscr)

            body(*pre, *ins, *outs, *scr)

            @pl.when(last)
            def _():
                comm.finish(cins, couts, cscr)

        args = list(args) + list(comm.inputs)
        in_specs += [ANY] * c_in
        out_shape += list(comm.out_shapes)
        out_specs += [ANY] * c_out
        scratch_shapes += list(comm.scratch)
    params = pltpu.CompilerParams(dimension_semantics=("arbitrary",) * len(grid), vmem_limit_bytes=VMEM_LIMIT,
                                  collective_id=COLLECTIVE_ID[comm.peers] if comm is not None and comm.peers else None)
    if sp:
        grid_spec = pltpu.PrefetchScalarGridSpec(num_scalar_prefetch=sp, grid=grid, in_specs=in_specs,
                                                 out_specs=out_specs, scratch_shapes=scratch_shapes)
        return pl.pallas_call(kernel_fn, name=name, grid_spec=grid_spec, out_shape=out_shape,
                              compiler_params=params)(*args)
    return pl.pallas_call(kernel_fn, name=name, grid=grid, in_specs=in_specs, out_shape=out_shape, out_specs=out_specs,
                          scratch_shapes=scratch_shapes, compiler_params=params)(*args)


def _join(a, b):
    na = (len(a.inputs), len(a.out_shapes), len(a.scratch))

    def split(refs):
        return ([r[:n] for r, n in zip(refs, na)], [r[n:] for r, n in zip(refs, na)])

    def start(*refs):
        ra, rb = split(refs)
        a.start(*ra)
        b.start(*rb)

    def finish(*refs):
        ra, rb = split(refs)
        a.finish(*ra)
        b.finish(*rb)

    def middle(*refs):
        for stage, r in zip((a, b), split(refs)):
            if stage.middle is not None:
                stage.middle(*r)

    return _Comm(list(a.inputs) + list(b.inputs), list(a.out_shapes) + list(b.out_shapes),
                 list(a.scratch) + list(b.scratch), start, finish,
                 middle if (a.middle is not None or b.middle is not None) else None,
                 peers=tuple(sorted(set(a.peers) | set(b.peers))) if a.peers and b.peers else None)


def _run_comm(comm, name):
    def body(*refs):
        c_in, c_out = len(comm.inputs), len(comm.out_shapes)
        parts = (refs[:c_in], refs[c_in:c_in + c_out], refs[c_in + c_out:])
        if comm.peers is not None:
            _handshake(comm.peers)
        comm.start(*parts)
        if comm.middle is not None:
            comm.middle(*parts)
        comm.finish(*parts)

    return pl.pallas_call(
        body, name=name, out_shape=list(comm.out_shapes), in_specs=[ANY] * len(comm.inputs),
        out_specs=[ANY] * len(comm.out_shapes), scratch_shapes=list(comm.scratch),
        compiler_params=pltpu.CompilerParams(collective_id=COLLECTIVE_ID[comm.peers] if comm.peers else None),
    )(*comm.inputs)


HBM = pl.BlockSpec(memory_space=pltpu.HBM)
SEM = pl.BlockSpec(memory_space=pltpu.SEMAPHORE)
DATAFLOW = pltpu.SideEffectType.DATAFLOW_SIDE_EFFECTING


def _comm_start(comm, name):
    c_in, c_out = len(comm.inputs), len(comm.out_shapes)
    sems = [s(()) if s is pltpu.SemaphoreType.DMA else s for s in comm.scratch]
    bufs = list(comm.inputs) + [lax.empty(s.shape, s.dtype) for s in comm.out_shapes]

    def body(*refs):
        if comm.peers is not None:
            _handshake(comm.peers)
        sem_refs = refs[c_in + c_out:c_in + c_out + len(sems)]
        comm.start(refs[:c_in], refs[c_in:c_in + c_out], sem_refs)
        refs[-1][...] = jnp.zeros_like(refs[-1])

    outs = pl.pallas_call(
        body, name=name,
        out_shape=sems + [pltpu.HBM(b.shape, b.dtype) for b in bufs] + [jax.ShapeDtypeStruct((8, LANE), F32)],
        in_specs=[HBM] * len(bufs),
        out_specs=[SEM] * len(sems) + [HBM] * len(bufs) + [pl.BlockSpec(memory_space=pltpu.VMEM)],
        input_output_aliases={i: len(sems) + i for i in range(len(bufs))},
        compiler_params=pltpu.CompilerParams(
            has_side_effects=DATAFLOW, collective_id=START_COLLECTIVE_ID[comm.peers] if comm.peers else None),
    )(*[pltpu.with_memory_space_constraint(b, pltpu.HBM) for b in bufs])
    return outs[:len(sems)], outs[len(sems):-1], outs[-1]


def _comm_wait(comm, name, sems, bufs, after):
    c_in, c_out = len(comm.inputs), len(comm.out_shapes)

    def body(*refs):
        sem_refs = refs[c_in + c_out:c_in + c_out + len(sems)]
        comm.finish(refs[:c_in], refs[c_in:c_in + c_out], sem_refs)

    outs = pl.pallas_call(
        body, name=name, out_shape=[pltpu.HBM(b.shape, b.dtype) for b in bufs],
        in_specs=[HBM] * len(bufs) + [SEM] * len(sems) + [ANY], out_specs=[HBM] * len(bufs),
        input_output_aliases={i: i for i in range(len(bufs))},
        compiler_params=pltpu.CompilerParams(has_side_effects=DATAFLOW),
    )(*bufs, *sems, after)
    return outs[:c_in], outs[c_in:]


def _ag_comm(names, flat):
    st = _Stage(names)

    def ring(me):
        x, y, c = me
        diagonal = x == y
        up = (jnp.where(diagonal, x, 1 - x), jnp.where(diagonal, 1 - y, y), c)
        down = (jnp.where(diagonal, 1 - x, x), jnp.where(diagonal, y, 1 - y), c)
        low = c == 0
        passed = tuple(jnp.where(low, d, u) for d, u in zip(down, up))
        target = tuple(jnp.where(low, u, d) for d, u in zip(down, up))
        return up, down, (1 - x, 1 - y, c), passed, target

    def parts(refs):
        (flat_ref,), (out_ref,), (send_sems, recv_sems, local_sem) = refs
        me = _position()

        def region(name, dev):
            r = st.rows[name]
            return out_ref.at[pl.ds(st.wc[name] + _lin(dev) * r, r), :]

        def own(name):
            return flat_ref.at[pl.ds(st.fl[name], st.rows[name]), :]

        def copies(k, dev, to, from_flat):
            return [pltpu.make_async_remote_copy(
                src_ref=own(n) if from_flat else region(n, dev), dst_ref=region(n, dev), send_sem=send_sems.at[k],
                recv_sem=recv_sems.at[k], device_id=to, device_id_type=MESH) for n in names]

        def whole(k):
            return pltpu.make_async_remote_copy(
                src_ref=flat_ref.at[pl.ds(0, st.R), :], dst_ref=out_ref.at[pl.ds(0, st.R), :],
                send_sem=send_sems.at[k], recv_sem=recv_sems.at[k], device_id=me, device_id_type=MESH)

        return me, region, own, copies, whole, flat_ref, out_ref, local_sem

    def start(*refs):
        me, region, own, copies, _, _, _, local_sem = parts(refs)
        for n in names:
            pltpu.make_async_copy(own(n), region(n, me), local_sem).start()
        up, down, _, _, _ = ring(me)
        for k, to in ((1, up), (2, down), (0, _peer(me, 1))):
            for cp in copies(k, me, to, True):
                cp.start()

    def middle(*refs):
        me, _, _, copies, whole, _, _, _ = parts(refs)
        up, down, _, passed, target = ring(me)
        sib = _peer(me, 1)
        whole(1).wait_recv()
        whole(2).wait_recv()
        for k, dev, to in ((3, passed, target), (4, down, sib), (5, up, sib)):
            for cp in copies(k, dev, to, False):
                cp.start()

    def finish(*refs):
        me, _, _, copies, whole, flat_ref, out_ref, local_sem = parts(refs)
        _, _, across, _, _ = ring(me)
        whole(3).wait_recv()
        for cp in copies(6, across, _peer(me, 1), False):
            cp.start()
        whole(0).wait_recv()
        for j in range(3):
            whole(4 + j).wait_recv()
        for k in range(7):
            whole(k).wait_send()
        pltpu.make_async_copy(flat_ref.at[pl.ds(0, st.R), :], out_ref.at[pl.ds(0, st.R), :], local_sem).wait()

    return _Comm([flat], [jax.ShapeDtypeStruct((st.W, D), BF)],
                 [pltpu.SemaphoreType.DMA((7,)), pltpu.SemaphoreType.DMA((7,)), pltpu.SemaphoreType.DMA],
                 start, finish, middle, peers=(1, 2, 4))


def _rs_pair_comm(names, src):
    st = _Stage(names)
    arrays = []
    for n in names:
        if not any(src[n][0] is a for a in arrays):
            arrays.append(src[n][0])
    idx = {n: [i for i, a in enumerate(arrays) if a is src[n][0]][0] for n in names}

    def slot_wait(refs):
        recv = refs[1][0]
        send_sem, recv_sem = refs[2]
        return pltpu.make_async_remote_copy(src_ref=recv, dst_ref=recv, send_sem=send_sem, recv_sem=recv_sem,
                                            device_id=_position(), device_id_type=MESH)

    def start(*refs):
        ins, (recv,), (send_sem, recv_sem) = refs
        me = _position()
        sib = _peer(me, 1)
        for q in range(NCHIP):
            dev = (q // 2, q % 2, sib[2])
            for n in names:
                r = st.rows[n]
                pltpu.make_async_remote_copy(
                    src_ref=ins[idx[n]].at[pl.ds(st.grad_row(n, src[n][1], _lin(dev)), r), :],
                    dst_ref=recv.at[q, pl.ds(st.off[n], r), :], send_sem=send_sem, recv_sem=recv_sem,
                    device_id=sib, device_id_type=MESH).start()

    def finish(*refs):
        w = slot_wait(refs)
        w.wait_recv()
        w.wait_send()

    return _Comm(arrays, [jax.ShapeDtypeStruct((NCHIP, st.R, D), BF)],
                 [pltpu.SemaphoreType.DMA, pltpu.SemaphoreType.DMA], start, finish, peers=(1,))


def _pair_add(names, src, recv, name):
    st = _Stage(names)
    c_arr = jnp.reshape(lax.axis_index("c"), (1,)).astype(jnp.int32)

    def body(c_ref, *refs):
        r_ref, o_ref = refs[len(names)], refs[len(names) + 1]
        for a_ref, n in zip(refs, names):
            rows = slice(st.off[n], st.off[n] + st.rows[n])
            o_ref[rows, :] = (a_ref[...].astype(F32) + r_ref[rows, :].astype(F32)).astype(BF)

    def shard_spec(n):
        r = st.rows[n]
        base, step = st.grad_row(n, src[n][1], 0) // r, st.full[n] // r
        return pl.BlockSpec((r, D), lambda q, c_ref: (base + step * (2 * q + c_ref[0]), 0))

    slot = pl.BlockSpec((None, st.R, D), lambda q, c_ref: (q, 0, 0))
    return _call(body, name=name, grid=(NCHIP,), args=[c_arr] + [src[n][0] for n in names] + [recv],
                 in_specs=[shard_spec(n) for n in names] + [slot],
                 out_shape=[jax.ShapeDtypeStruct((NCHIP, st.R, D), BF)], out_specs=[slot], num_scalar_prefetch=1)[0]


def _rs_chip_comm(part):
    def copies(refs):
        (p_ref,), (recv,), (send_sems, recv_sems, local_sem) = refs
        me = _position()
        mine = pltpu.make_async_copy(p_ref.at[_chip(me)], recv.at[_chip(me)], local_sem)
        out = []
        for j, bits in enumerate((4, 2, 6)):
            to = _peer(me, bits)
            out.append(pltpu.make_async_remote_copy(
                src_ref=p_ref.at[_chip(to)], dst_ref=recv.at[_chip(me)], send_sem=send_sems.at[j],
                recv_sem=recv_sems.at[j], device_id=to, device_id_type=MESH))
        return mine, out

    def start(*refs):
        mine, out = copies(refs)
        mine.start()
        for cp in out:
            cp.start()

    def finish(*refs):
        mine, out = copies(refs)
        for cp in out:
            cp.wait_recv()
        for cp in out:
            cp.wait_send()
        mine.wait()

    return _Comm([part], [jax.ShapeDtypeStruct(part.shape, BF)],
                 [pltpu.SemaphoreType.DMA((3,)), pltpu.SemaphoreType.DMA((3,)), pltpu.SemaphoreType.DMA],
                 start, finish, peers=(2, 4, 6))


def _direct_comm(x, scatter):
    def copies(refs):
        (x_ref,), (out_ref,), (send_sems, recv_sems, local_sem) = refs
        me = _position()

        def piece(dev):
            return x_ref.at[_lin(dev)] if scatter else x_ref

        mine = pltpu.make_async_copy(piece(me), out_ref.at[_lin(me)], local_sem)
        return mine, [pltpu.make_async_remote_copy(
            src_ref=piece(_peer(me, j)), dst_ref=out_ref.at[_lin(me)], send_sem=send_sems.at[j - 1],
            recv_sem=recv_sems.at[j - 1], device_id=_peer(me, j), device_id_type=MESH) for j in range(1, NDEV)]

    def start(*refs):
        mine, cps = copies(refs)
        mine.start()
        for cp in cps:
            cp.start()

    def finish(*refs):
        mine, cps = copies(refs)
        for cp in cps:
            cp.wait_recv()
        for cp in cps:
            cp.wait_send()
        mine.wait()

    shape = x.shape if scatter else (NDEV,) + x.shape
    return _Comm([x], [jax.ShapeDtypeStruct(shape, x.dtype)],
                 [pltpu.SemaphoreType.DMA((7,)), pltpu.SemaphoreType.DMA((7,)), pltpu.SemaphoreType.DMA],
                 start, finish, peers=(1, 2, 3, 4, 5, 6, 7))


def _pack_weights(shards):
    lay = _Layout()

    def body(*refs):
        o_ref = refs[-1]
        for ref, n in zip(refs, ORDER):
            x = ref[...].T if n == "win" else ref[...]
            o_ref[lay.fl[n]:lay.fl[n] + lay.rows[n], :] = x.astype(BF)

    return pl.pallas_call(
        body, name="pack_weights", out_shape=jax.ShapeDtypeStruct((lay.RT, D), BF),
        compiler_params=pltpu.CompilerParams(vmem_limit_bytes=VMEM_LIMIT))(*[shards[n] for n in ORDER])


def _load_ffn_weights(srcs, offs, scratch, sem):
    @pl.when(pl.program_id(0) == 0)
    def _():
        cps = [pltpu.make_async_copy(s.at[pl.ds(off, dst.shape[0]), :], dst, sem.at[i])
               for i, (s, off, dst) in enumerate(zip(srcs, offs, scratch))]
        for cp in cps:
            cp.start()
        for cp in cps:
            cp.wait()


def _final_loss_tile(xf, g, tgt, s_ref):
    r = lax.rsqrt(jnp.mean(xf * xf, axis=-1, keepdims=True) + EPS)
    xr = xf * r
    e = xr * g - tgt
    s_ref[1:2, :] += jnp.sum(e * e, axis=0, keepdims=True) * (0.5 / D)
    dy = e * (1.0 / D)
    s_ref[0:1, :] += jnp.sum(dy * xr, axis=0, keepdims=True)
    gdy = dy * g
    return r * gdy - xr * (r * jnp.mean(gdy * xr, axis=-1, keepdims=True))


def _ffn_fwd(x, g, wbufs, offs, name, comm=None, final=None):
    nf = F // FC

    def body(x_ref, g_ref, b0, b1, b2, *rest):
        if final is None:
            h_ref, n_ref, gg_ref, uu_ref, a_ref, wg_s, wu_s, wd_s, sem = rest
        else:
            gf_ref, t_ref, dh_ref, dhb_ref, s_ref, n_ref, gg_ref, uu_ref, a_ref, wg_s, wu_s, wd_s, sem = rest

            @pl.when(pl.program_id(0) == 0)
            def _():
                s_ref[...] = jnp.zeros_like(s_ref)

        _load_ffn_weights((b0, b1, b2), offs, (wg_s, wu_s, wd_s), sem)
        xf = x_ref[...]
        r = lax.rsqrt(jnp.mean(xf * xf, axis=-1, keepdims=True) + EPS)
        nb = (xf * r * g_ref[...]).astype(BF)
        n_ref[...] = nb
        acc = jnp.zeros((TM, D), F32)
        for c in range(nf):
            sl = slice(c * FC, (c + 1) * FC)
            gb = _nt(nb, wg_s[sl, :]).astype(BF)
            ub = _nt(nb, wu_s[sl, :]).astype(BF)
            gg_ref[:, sl] = gb
            uu_ref[:, sl] = ub
            a = (gb * _sig(gb)) * ub
            a_ref[0, :, sl] = a
            acc = acc + _nn(a, wd_s[sl, :])
        h = xf + 0.5 * acc
        if final is None:
            h_ref[...] = h
        else:
            dh = _final_loss_tile(h, gf_ref[...], t_ref[...], s_ref)
            dh_ref[...] = dh
            dhb_ref[...] = (0.5 * dh).astype(BF)

    row = lambda i: (i, 0)
    vec = pl.BlockSpec((1, D), lambda i: (0, 0))
    tile = pl.BlockSpec((TM, D), row)
    saved_shapes = [jax.ShapeDtypeStruct((T, D), BF), jax.ShapeDtypeStruct((T, F), BF), jax.ShapeDtypeStruct((T, F), BF),
                    jax.ShapeDtypeStruct((1, T, F), BF)]
    saved_specs = [tile, pl.BlockSpec((TM, F), row), pl.BlockSpec((TM, F), row),
                   pl.BlockSpec((1, TM, F), lambda i: (0, i, 0))]
    if final is None:
        extra_args, extra_specs = [], []
        head_shapes, head_specs = [jax.ShapeDtypeStruct((T, D), F32)], [tile]
    else:
        extra_args, extra_specs = list(final), [vec, tile]
        head_shapes = [jax.ShapeDtypeStruct((T, D), F32), jax.ShapeDtypeStruct((T, D), BF), jax.ShapeDtypeStruct((8, D), F32)]
        head_specs = [tile, tile, pl.BlockSpec((8, D), lambda i: (0, 0))]
    return _call(
        body, name=name, grid=(T // TM,), args=[x, g, *wbufs, *extra_args], comm=comm,
        in_specs=[tile, vec, ANY, ANY, ANY] + extra_specs,
        out_shape=head_shapes + saved_shapes, out_specs=head_specs + saved_specs,
        scratch_shapes=[pltpu.VMEM((F, D), BF)] * 3 + [pltpu.SemaphoreType.DMA((3,))])


def _ffn_gate_up(x, g, wbufs, offs, name, comm=None):
    nf = F // FC

    def body(x_ref, g_ref, b0, b1, n_ref, gg_ref, uu_ref, a_ref, wg_s, wu_s, sem):
        _load_ffn_weights((b0, b1), offs, (wg_s, wu_s), sem)
        xf = x_ref[...]
        r = lax.rsqrt(jnp.mean(xf * xf, axis=-1, keepdims=True) + EPS)
        nb = (xf * r * g_ref[...]).astype(BF)
        n_ref[...] = nb
        for c in range(nf):
            sl = slice(c * FC, (c + 1) * FC)
            gb = _nt(nb, wg_s[sl, :]).astype(BF)
            ub = _nt(nb, wu_s[sl, :]).astype(BF)
            gg_ref[:, sl] = gb
            uu_ref[:, sl] = ub
            a_ref[0, :, sl] = (gb * _sig(gb)) * ub

    row = lambda i: (i, 0)
    tile = pl.BlockSpec((TM, D), row)
    return _call(
        body, name=name, grid=(T // TM,), args=[x, g, *wbufs], comm=comm,
        in_specs=[tile, pl.BlockSpec((1, D), lambda i: (0, 0)), ANY, ANY],
        out_shape=[jax.ShapeDtypeStruct((T, D), BF), jax.ShapeDtypeStruct((T, F), BF), jax.ShapeDtypeStruct((T, F), BF),
                   jax.ShapeDtypeStruct((1, T, F), BF)],
        out_specs=[tile, pl.BlockSpec((TM, F), row), pl.BlockSpec((TM, F), row),
                   pl.BlockSpec((1, TM, F), lambda i: (0, i, 0))],
        scratch_shapes=[pltpu.VMEM((F, D), BF)] * 2 + [pltpu.SemaphoreType.DMA((2,))])


def _ffn_down(x, act, wbuf, off, name, comm=None):
    def body(x_ref, a_ref, b0, h_ref, wd_s, sem):
        _load_ffn_weights((b0,), (off,), (wd_s,), sem)
        h_ref[...] = x_ref[...] + 0.5 * _nn(a_ref[0], wd_s[...])

    tile = pl.BlockSpec((TM, D), lambda i: (i, 0))
    return _call(
        body, name=name, grid=(T // TM,), args=[x, act, wbuf], comm=comm,
        in_specs=[tile, pl.BlockSpec((1, TM, F), lambda i: (0, i, 0)), ANY],
        out_shape=[jax.ShapeDtypeStruct((T, D), F32)], out_specs=[tile],
        scratch_shapes=[pltpu.VMEM((F, D), BF), pltpu.SemaphoreType.DMA((1,))])


def _load_in_proj(parts, w_s, sem):
    @pl.when(pl.program_id(0) == 0)
    def _():
        shard = NG * D // NDEV
        rows = shard // len(parts)
        cps = [pltpu.make_async_copy(buf.at[pl.ds(first + k * rows, rows), :],
                                     w_s.at[pl.ds(k * shard + p * rows, rows), :], sem.at[p * NDEV + k])
               for p, (buf, first) in enumerate(parts) for k in range(NDEV)]
        for cp in cps:
            cp.start()
        for cp in cps:
            cp.wait()


def _mix_in(h1, gm, win, comm=None):
    def body(h_ref, g_ref, *rest):
        w_any, (u_ref, z_ref, w_s, sem) = rest[:len(win)], rest[len(win):]
        _load_in_proj([(b, first) for b, (_, first) in zip(w_any, win)], w_s, sem)
        xf = h_ref[...]
        r = lax.rsqrt(jnp.mean(xf * xf, axis=-1, keepdims=True) + EPS)
        ub = (xf * r * g_ref[...]).astype(BF)
        u_ref[...] = ub
        for j in range(NG):
            z_ref[j] = _nt(ub, w_s[j * D:(j + 1) * D, :]).astype(BF)

    row = lambda i: (i, 0)
    return _call(
        body, name="mix_in", grid=(T // TM,), args=[h1, gm] + [b for b, _ in win], comm=comm,
        in_specs=[pl.BlockSpec((TM, D), row), pl.BlockSpec((1, D), lambda i: (0, 0))] + [ANY] * len(win),
        out_shape=[jax.ShapeDtypeStruct((T, D), BF), jax.ShapeDtypeStruct((NG, T, D), BF)],
        out_specs=[pl.BlockSpec((TM, D), row), pl.BlockSpec((NG, TM, D), lambda i: (0, i, 0))],
        scratch_shapes=[pltpu.VMEM((NG * D, D), BF), pltpu.SemaphoreType.DMA((NDEV * len(win),))])


def _shift_up(w, b):
    return w if b == 0 else pltpu.roll(w, w.shape[0] - b, 0)


def _fold8(p):
    red = p[0:8, :]
    for i in range(1, p.shape[0] // 8):
        red = red + p[8 * i:8 * i + 8, :]
    return red


def _dft_constants():
    import numpy as np
    nh = NB // 2
    f, n = np.arange(nh)[:, None], np.arange(NB)[None, :]
    ang = 2.0 * np.pi / NB * f * n
    fc = np.cos(ang)
    fs = np.where(f == 0, (-1.0) ** n, np.sin(ang))
    scale = np.where(f == 0, 1.0, 2.0) / NB
    ic = (scale * np.cos(ang)).T
    isn = np.where(f == 0, (-1.0) ** n / NB, scale * np.sin(ang)).T
    d = (KA - 1 - np.arange(32))[None, :]
    valid = (np.arange(32) < KA)[None, :]
    angk = 2.0 * np.pi / NB * f * d
    kc = np.where(valid, np.cos(angk), 0.0)
    ks = np.where(valid, np.sin(angk), 0.0)
    k2 = np.where(valid, np.where(f == 0, (-1.0) ** d, np.cos(angk)), 0.0)
    rtc = np.where(valid, scale * np.cos(angk), 0.0).T
    rts = np.where(valid, np.where(f == 0, (-1.0) ** d / NB, scale * np.sin(angk)), 0.0).T

    def bf(a):
        return jnp.asarray(a, F32).astype(BF)

    def split(a):
        hi = bf(a)
        return hi, (jnp.asarray(a, F32) - hi.astype(F32)).astype(BF)

    return dict(fc=bf(fc), fs=bf(fs), ic_hi=bf(ic[HB:]), is_hi=bf(isn[HB:]), ic_lo=bf(ic[:HB]), is_lo=bf(isn[:HB]),
                kc=split(kc), ks=split(ks), k2=split(k2), rtc=split(rtc), rts=split(rts))


def _dot3(m_hi, m_lo, x):
    x_hi = x.astype(BF)
    x_lo = (x - x_hi.astype(F32)).astype(BF)
    return _nn(m_hi, x_hi) + _nn(m_hi, x_lo) + _nn(m_lo, x_hi)


def _whole(a):
    return pl.BlockSpec(a.shape, lambda c, t: (0,) * a.ndim)


def _filter_spectrum(cw_ref, tabs, hc, hs, h2):
    w32 = cw_ref[0:32, :]
    for (hi, lo), dst in zip(tabs, (hc, hs, h2)):
        dst[...] = _dot3(hi[...], lo[...], w32)


def _conv_fwd_dft(z, cw, bias, dft, comm=None):
    nt = T // TB
    hb = TB // HB

    def body(z_ref, zh_ref, cw_ref, b_ref, fc_ref, fs_ref, ic_ref, is_ref, kch, kcl, ksh, ksl, k2h, k2l,
             a1_ref, q_ref, aext, ppad, hc, hs, h2):
        first = pl.program_id(1) == 0
        f = lambda ref, j: ref[j].astype(F32)

        @pl.when(first)
        def _():
            _filter_spectrum(cw_ref, ((kch, kcl), (ksh, ksl), (k2h, k2l)), hc, hs, h2)

        aext[0:HB, :] = jnp.where(first, 0.0, f(zh_ref, 0) * _sig(f(zh_ref, 1))).astype(BF)
        aext[HB:, :] = (f(z_ref, 0) * _sig(f(z_ref, 1))).astype(BF)
        ppad[0:8, :] = jnp.where(first, 0.0, f(zh_ref, 3)[HB - 8:HB, :] * f(zh_ref, 4)[HB - 8:HB, :])
        ppad[8:, :] = f(z_ref, 3) * f(z_ref, 4)
        bias_row = b_ref[...]

        for j in range(TB // HB):
            xs = aext[j * HB:j * HB + NB, :]
            xa, xb = _nn(fc_ref[...], xs), _nn(fs_ref[...], xs)
            yc = (hc[...] * xa - hs[...] * xb).astype(BF)
            ys = (h2[...] * xb + hs[...] * xa).astype(BF)
            y = _nn(ic_ref[...], yc) + _nn(is_ref[...], ys)
            a1_ref[j * HB:(j + 1) * HB, :] = (y + bias_row).astype(BF)

        def chunk(r, carry):
            base = pl.multiple_of(r * CHB, CHB)
            pw = ppad[pl.ds(base, CHB + 8), :]
            v = (cw_ref[pl.ds(32, 1), :] * _shift_up(pw, 6)[0:CHB, :]
                 + cw_ref[pl.ds(33, 1), :] * _shift_up(pw, 7)[0:CHB, :]
                 + cw_ref[pl.ds(34, 1), :] * pw[8:8 + CHB, :])
            q_ref[pl.ds(base, CHB), :] = (z_ref[2, pl.ds(base, CHB), :].astype(F32) * v).astype(BF)
            return carry

        lax.fori_loop(0, TB // CHB, chunk, 0)

    blk = pl.BlockSpec((TB, CW), lambda c, t: (t, c))
    tabs = [dft["fc"], dft["fs"], dft["ic_hi"], dft["is_hi"], *dft["kc"], *dft["ks"], *dft["k2"]]
    return _call(
        body, name="conv_fwd", grid=(D // CW, nt), comm=comm, args=[z, z, cw, bias] + tabs,
        in_specs=[pl.BlockSpec((5, TB, CW), lambda c, t: (0, t, c)),
                  pl.BlockSpec((5, HB, CW), lambda c, t: (0, jnp.maximum(t * hb - 1, 0), c)),
                  pl.BlockSpec((40, CW), lambda c, t: (0, c)), pl.BlockSpec((1, CW), lambda c, t: (0, c))]
                 + [_whole(a) for a in tabs],
        out_shape=[jax.ShapeDtypeStruct((T, D), BF), jax.ShapeDtypeStruct((T, D), BF)], out_specs=[blk, blk],
        scratch_shapes=[pltpu.VMEM((TB + HB, CW), BF), pltpu.VMEM((TB + 8, CW), F32)]
                       + [pltpu.VMEM((NB // 2, CW), F32)] * 3)


def _conv_bwd_dft(z, da1, dq, dzg, cw, dft, comm=None):
    nt = T // TB
    hb = TB // HB
    last_h = T // HB - 1

    def body(z_ref, zp_ref, zn_ref, da1_ref, da1n_ref, dq_ref, dqn_ref, dzg_ref, cw_ref,
             fc_ref, fs_ref, ic_ref, is_ref, kch, kcl, ksh, ksl, k2h, k2l, rch, rcl, rsh, rsl,
             dz_ref, dwa_ref, dwb_ref, aext, dyext, ppad, dvpad, hc, hs, h2, rc, rs, nyq, acc_b):
        t = pl.program_id(1)
        first, last = t == 0, t == nt - 1
        f = lambda ref, j: ref[j].astype(F32)

        @pl.when(first)
        def _():
            _filter_spectrum(cw_ref, ((kch, kcl), (ksh, ksl), (k2h, k2l)), hc, hs, h2)
            rc[...] = jnp.zeros_like(rc)
            rs[...] = jnp.zeros_like(rs)
            nyq[...] = jnp.zeros_like(nyq)
            acc_b[...] = jnp.zeros_like(acc_b)

        aext[0:HB, :] = jnp.where(first, 0.0, f(zp_ref, 0) * _sig(f(zp_ref, 1))).astype(BF)
        aext[HB:, :] = (f(z_ref, 0) * _sig(f(z_ref, 1))).astype(BF)
        dyext[0:TB, :] = da1_ref[...]
        dyext[TB:, :] = jnp.where(last, 0.0, da1n_ref[...].astype(F32)).astype(BF)
        ppad[0:8, :] = jnp.where(first, 0.0, f(zp_ref, 3)[HB - 8:HB, :] * f(zp_ref, 4)[HB - 8:HB, :])
        ppad[8:, :] = f(z_ref, 3) * f(z_ref, 4)
        dvpad[0:TB, :] = dq_ref[...].astype(F32) * f(z_ref, 2)
        dvpad[TB:, :] = jnp.where(last, 0.0, dqn_ref[...].astype(F32)[0:8, :] * f(zn_ref, 2)[0:8, :])

        for j in range(TB // HB):
            rows = slice(j * HB, (j + 1) * HB)
            dys = dyext[j * HB:j * HB + NB, :]
            da, db = _nn(fc_ref[...], dys), _nn(fs_ref[...], dys)
            gc = (hc[...] * da + hs[...] * db).astype(BF)
            gs = (h2[...] * db - hs[...] * da).astype(BF)
            da0 = _nn(ic_ref[...], gc) + _nn(is_ref[...], gs)
            z0, z1 = z_ref[0, rows, :].astype(F32), z_ref[1, rows, :].astype(F32)
            s1 = _sig(z1)
            dz_ref[0, rows, :] = (da0 * s1).astype(BF)
            dz_ref[1, rows, :] = (da0 * z0 * (s1 * (1.0 - s1))).astype(BF)
            xs = aext[j * HB:j * HB + NB, :]
            xa, xb = _nn(fc_ref[...], xs), _nn(fs_ref[...], xs)
            dyb = dyext[rows, :]
            pa, pb = _nn(fc_ref[:, HB:NB], dyb), _nn(fs_ref[:, HB:NB], dyb)
            rc[...] += pa * xa + pb * xb
            rs[...] += pb * xa - pa * xb
            nyq[...] += pb[0:8, :] * xb[0:8, :]

        def chunk(r, carry):
            base = pl.multiple_of(r * CHB, CHB)
            rows = pl.ds(base, CHB)
            pw = ppad[pl.ds(base, CHB + 8), :]
            p6 = _shift_up(pw, 6)[0:CHB, :]
            p7 = _shift_up(pw, 7)[0:CHB, :]
            p8 = pw[8:8 + CHB, :]
            wb0, wb1, wb2 = cw_ref[pl.ds(32, 1), :], cw_ref[pl.ds(33, 1), :], cw_ref[pl.ds(34, 1), :]
            v = wb0 * p6 + wb1 * p7 + wb2 * p8
            dz_ref[2, rows, :] = (dq_ref[rows, :].astype(F32) * v).astype(BF)
            dvw = dvpad[pl.ds(base, CHB + 8), :]
            dvc = dvw[0:CHB, :]
            dp = wb2 * dvc + wb1 * _shift_up(dvw, 1)[0:CHB, :] + wb0 * _shift_up(dvw, 2)[0:CHB, :]
            dz_ref[3, rows, :] = (dp * z_ref[4, rows, :].astype(F32)).astype(BF)
            dz_ref[4, rows, :] = (dp * z_ref[3, rows, :].astype(F32)).astype(BF)
            acc_b[0:8, :] += _fold8(dvc * p6)
            acc_b[8:16, :] += _fold8(dvc * p7)
            acc_b[16:24, :] += _fold8(dvc * p8)
            dz_ref[5, rows, :] = dzg_ref[0, rows, :]
            dz_ref[6, rows, :] = dzg_ref[1, rows, :]
            return carry

        lax.fori_loop(0, TB // CHB, chunk, 0)

        @pl.when(last)
        def _():
            row0 = lax.broadcasted_iota(jnp.int32, (NB // 2, CW), 0) == 0
            ny = jnp.broadcast_to(nyq[0:1, :], (NB // 2, CW))
            rcv = jnp.where(row0, rc[...] - ny, rc[...])
            rsv = jnp.where(row0, ny, rs[...])
            dwa_ref[...] = _dot3(rch[...], rcl[...], rcv) + _dot3(rsh[...], rsl[...], rsv)
            for k in range(KB):
                dwb_ref[k:k + 1, :] = jnp.sum(acc_b[8 * k:8 * k + 8, :], axis=0, keepdims=True)
            dwb_ref[KB:8, :] = jnp.zeros((8 - KB, CW), F32)

    blk = lambda c, t: (t, c)
    nxt = lambda c, t: (jnp.minimum((t + 1) * hb, last_h), c)
    tabs = [dft["fc"], dft["fs"], dft["ic_lo"], dft["is_lo"], *dft["kc"], *dft["ks"], *dft["k2"], *dft["rtc"], *dft["rts"]]
    return _call(
        body, name="conv_bwd", grid=(D // CW, nt), comm=comm, args=[z, z, z, da1, da1, dq, dq, dzg, cw] + tabs,
        in_specs=[pl.BlockSpec((5, TB, CW), lambda c, t: (0, t, c)),
                  pl.BlockSpec((5, HB, CW), lambda c, t: (0, jnp.maximum(t * hb - 1, 0), c)),
                  pl.BlockSpec((5, HB, CW), lambda c, t: (0, jnp.minimum((t + 1) * hb, last_h), c)),
                  pl.BlockSpec((TB, CW), blk), pl.BlockSpec((HB, CW), nxt),
                  pl.BlockSpec((TB, CW), blk), pl.BlockSpec((HB, CW), nxt),
                  pl.BlockSpec((2, TB, CW), lambda c, t: (0, t, c)),
                  pl.BlockSpec((40, CW), lambda c, t: (0, c))]
                 + [_whole(a) for a in tabs],
        out_shape=[jax.ShapeDtypeStruct((NG, T, D), BF), jax.ShapeDtypeStruct((32, D), F32),
                   jax.ShapeDtypeStruct((8, D), F32)],
        out_specs=[pl.BlockSpec((NG, TB, CW), lambda c, t: (0, t, c)),
                   pl.BlockSpec((32, CW), lambda c, t: (0, c)), pl.BlockSpec((8, CW), lambda c, t: (0, c))],
        scratch_shapes=[pltpu.VMEM((TB + HB, CW), BF), pltpu.VMEM((TB + HB, CW), BF),
                        pltpu.VMEM((TB + 8, CW), F32), pltpu.VMEM((TB + 8, CW), F32)]
                       + [pltpu.VMEM((NB // 2, CW), F32)] * 5 + [pltpu.VMEM((8, CW), F32), pltpu.VMEM((24, CW), F32)])


def _layernorm_silu(a1, lng, lnb):
    mu = jnp.mean(a1, axis=-1, keepdims=True)
    xc = a1 - mu
    rs = lax.rsqrt(jnp.mean(xc * xc, axis=-1, keepdims=True) + EPS)
    xh = xc * rs
    a2 = xh * lng + lnb
    sg = _sig(a2)
    return xh, rs, a2, sg


def _square_specs(blocks):
    return [pl.BlockSpec((D, D), lambda i, b=b: (b, 0)) for b in blocks]


def _mix_out(a1, q, z, h1, lng, lnb, wsq, comm=None):
    def body(a1_ref, q_ref, ga_ref, gb_ref, h_ref, lng_ref, lnb_ref, wa_ref, wb_ref, wo_ref, h2_ref, ya_ref, yb_ref):
        _, _, a2, sg = _layernorm_silu(a1_ref[...].astype(F32), lng_ref[...], lnb_ref[...])
        ya = _nn((a2 * sg).astype(BF), wa_ref[...])
        yb = _nn(q_ref[...], wb_ref[...])
        ya_ref[...] = ya.astype(BF)
        yb_ref[...] = yb.astype(BF)
        m = _sig(ga_ref[...].astype(F32)) * ya + _sig(gb_ref[...].astype(F32)) * yb
        h2_ref[...] = h_ref[...] + _nn(m.astype(BF), wo_ref[...])

    row = lambda i: (i, 0)
    vec = pl.BlockSpec((1, D), lambda i: (0, 0))
    return _call(
        body, name="mix_out", grid=(T // TM,), args=[a1, q, z, z, h1, lng, lnb, wsq, wsq, wsq], comm=comm,
        in_specs=[pl.BlockSpec((TM, D), row), pl.BlockSpec((TM, D), row),
                  pl.BlockSpec((None, TM, D), lambda i: (5, i, 0)), pl.BlockSpec((None, TM, D), lambda i: (6, i, 0)),
                  pl.BlockSpec((TM, D), row), vec, vec] + _square_specs((0, 1, 2)),
        out_shape=[jax.ShapeDtypeStruct((T, D), F32), jax.ShapeDtypeStruct((T, D), BF), jax.ShapeDtypeStruct((T, D), BF)],
        out_specs=[pl.BlockSpec((TM, D), row)] * 3)


def _rmsnorm_bwd(xf, g, dn):
    r = lax.rsqrt(jnp.mean(xf * xf, axis=-1, keepdims=True) + EPS)
    xr = xf * r
    gdn = dn * g
    dx = r * gdn - xr * (r * jnp.mean(gdn * xr, axis=-1, keepdims=True))
    return dx, jnp.sum(dn * xr, axis=0, keepdims=True)


def _ffn_bwd_hidden(dh, gg, uu, wbuf, off, name, comm=None):
    nf, nt = F // FC, T // TM

    def body(dh_ref, gg_hbm, uu_hbm, b0, dgu_ref, wd_s, sem, ring_g, ring_u, ring_sem):
        i = pl.program_id(0)

        def fetch(t):
            return [pltpu.make_async_copy(src.at[pl.ds(t * TM, TM), :], ring.at[t % 3], ring_sem.at[k, t % 3])
                    for k, (src, ring) in enumerate(((gg_hbm, ring_g), (uu_hbm, ring_u)))]

        @pl.when(i == 0)
        def _():
            for t in range(min(2, nt)):
                for cp in fetch(t):
                    cp.start()

        @pl.when(i + 2 < nt)
        def _():
            for cp in fetch(i + 2):
                cp.start()

        _load_ffn_weights((b0,), (off,), (wd_s,), sem)
        for cp in fetch(i):
            cp.wait()
        gg_ref, uu_ref = ring_g.at[i % 3], ring_u.at[i % 3]
        dhb = dh_ref[...]
        for c in range(nf):
            sl = slice(c * FC, (c + 1) * FC)
            da = _nt(dhb, wd_s[sl, :]).astype(BF)
            gb, ub = gg_ref[:, sl], uu_ref[:, sl]
            sg = _sig(gb)
            dgu_ref[0, :, sl] = (da * ub) * (sg * (1.0 + gb * (1.0 - sg)))
            dgu_ref[0, :, F + c * FC:F + (c + 1) * FC] = da * (gb * sg)

    row = lambda i: (i, 0)
    return _call(
        body, name=name, grid=(nt,), args=[dh, gg, uu, wbuf], comm=comm,
        in_specs=[pl.BlockSpec((TM, D), row), ANY, ANY, ANY],
        out_shape=[jax.ShapeDtypeStruct((1, T, 2 * F), BF)],
        out_specs=[pl.BlockSpec((1, TM, 2 * F), lambda i: (0, i, 0))],
        scratch_shapes=[pltpu.VMEM((F, D), BF), pltpu.SemaphoreType.DMA((1,)), pltpu.VMEM((3, TM, F), BF),
                        pltpu.VMEM((3, TM, F), BF), pltpu.SemaphoreType.DMA((2, 3))])


def _ffn_bwd_input(dgu, dh, x, g, wbufs, offs, name, comm=None, after=None):
    def body(dgu_ref, dh_ref, x_ref, g_ref, b0, b1, dx_ref, s_ref, w_s, sem):
        _load_ffn_weights((b0, b1), offs, (w_s.at[pl.ds(0, F), :], w_s.at[pl.ds(F, F), :]), sem)

        @pl.when(pl.program_id(0) == 0)
        def _():
            s_ref[...] = jnp.zeros_like(s_ref)

        dn = _nn(dgu_ref[0], w_s[...])
        dxn, dg = _rmsnorm_bwd(x_ref[...], g_ref[...], dn)
        dx_ref[...] = dh_ref[...] + dxn
        s_ref[0:1, :] += dg

    row = lambda i: (i, 0)
    return _call(
        body, name=name, grid=(T // TM,), args=[dgu, dh, x, g, *wbufs], comm=comm, after=after,
        in_specs=[pl.BlockSpec((1, TM, 2 * F), lambda i: (0, i, 0)), pl.BlockSpec((TM, D), row),
                  pl.BlockSpec((TM, D), row), pl.BlockSpec((1, D), lambda i: (0, 0)), ANY, ANY],
        out_shape=[jax.ShapeDtypeStruct((T, D), F32), jax.ShapeDtypeStruct((8, D), F32)],
        out_specs=[pl.BlockSpec((TM, D), row), pl.BlockSpec((8, D), lambda i: (0, 0))],
        scratch_shapes=[pltpu.VMEM((2 * F, D), BF), pltpu.SemaphoreType.DMA((2,))])


def _tn_matmul(lhs, rhs, tr, name, comm=None):
    ng, _, cdim = lhs.shape
    nc, nk = cdim // tr, T // TK
    if rhs.ndim == 2:
        r_spec = pl.BlockSpec((TK, D), lambda g, c, k: (k, 0))
    else:
        r_spec = pl.BlockSpec((None, TK, D), lambda g, c, k: (g, k, 0))

    def body(l_ref, r_ref, o_ref, acc):
        k = pl.program_id(2)

        @pl.when(k == 0)
        def _():
            acc[...] = jnp.zeros_like(acc)

        acc[...] += _tn(l_ref[...], r_ref[...])

        @pl.when(k == nk - 1)
        def _():
            o_ref[...] = acc[...].astype(BF)

    return _call(
        body, name=name, grid=(ng, nc, nk), args=[lhs, rhs], comm=comm,
        in_specs=[pl.BlockSpec((None, TK, tr), lambda g, c, k: (g, k, c)), r_spec],
        out_shape=[jax.ShapeDtypeStruct((ng * cdim, D), BF)],
        out_specs=[pl.BlockSpec((tr, D), lambda g, c, k: (g * nc + c, 0))],
        scratch_shapes=[pltpu.VMEM((tr, D), F32)])


def _mix_out_bwd(dh2, ya, yb, z, a1, q, lng, lnb, wsq, comm=None):
    def body(dh_ref, ya_ref, yb_ref, ga_ref, gb_ref, a1_ref, q_ref, lng_ref, lnb_ref, wa_ref, wb_ref, wo_ref,
             dzg_ref, da1_ref, dq_ref, l_ref, r_ref, s_ref):
        @pl.when(pl.program_id(0) == 0)
        def _():
            s_ref[...] = jnp.zeros_like(s_ref)

        dhb = dh_ref[...].astype(BF)
        dm = _nt(dhb, wo_ref[...]).astype(BF)
        ya, yb = ya_ref[...], yb_ref[...]
        sa, sb = _sig(ga_ref[...]), _sig(gb_ref[...])
        l_ref[0] = sa * ya + sb * yb
        l_ref[2] = q_ref[...]
        dzg_ref[0] = (dm * ya) * (sa * (1.0 - sa))
        dzg_ref[1] = (dm * yb) * (sb * (1.0 - sb))
        dya = dm * sa
        dyb = dm * sb
        r_ref[0] = dhb
        r_ref[1] = dya
        r_ref[2] = dyb
        dq_ref[...] = _nt(dyb, wb_ref[...]).astype(BF)
        da3 = _nt(dya, wa_ref[...])
        lng = lng_ref[...]
        xh, rs, a2, sg = _layernorm_silu(a1_ref[...].astype(F32), lng, lnb_ref[...])
        l_ref[1] = (a2 * sg).astype(BF)
        da2 = da3 * (sg * (1.0 + a2 * (1.0 - sg)))
        s_ref[0:1, :] += jnp.sum(da2 * xh, axis=0, keepdims=True)
        s_ref[1:2, :] += jnp.sum(da2, axis=0, keepdims=True)
        dxh = da2 * lng
        da1 = rs * (dxh - jnp.mean(dxh, axis=-1, keepdims=True) - xh * jnp.mean(dxh * xh, axis=-1, keepdims=True))
        da1_ref[...] = da1.astype(BF)
        s_ref[2:3, :] += jnp.sum(da1, axis=0, keepdims=True)

    row = lambda i: (i, 0)
    row3 = lambda i: (0, i, 0)
    vec = pl.BlockSpec((1, D), lambda i: (0, 0))
    return _call(
        body, name="mix_out_bwd", grid=(T // TM,), args=[dh2, ya, yb, z, z, a1, q, lng, lnb, wsq, wsq, wsq], comm=comm,
        in_specs=[pl.BlockSpec((TM, D), row), pl.BlockSpec((TM, D), row), pl.BlockSpec((TM, D), row),
                  pl.BlockSpec((None, TM, D), lambda i: (5, i, 0)), pl.BlockSpec((None, TM, D), lambda i: (6, i, 0)),
                  pl.BlockSpec((TM, D), row), pl.BlockSpec((TM, D), row), vec, vec] + _square_specs((0, 1, 2)),
        out_shape=[jax.ShapeDtypeStruct((2, T, D), BF), jax.ShapeDtypeStruct((T, D), BF),
                   jax.ShapeDtypeStruct((T, D), BF), jax.ShapeDtypeStruct((3, T, D), BF),
                   jax.ShapeDtypeStruct((3, T, D), BF), jax.ShapeDtypeStruct((8, D), F32)],
        out_specs=[pl.BlockSpec((2, TM, D), row3), pl.BlockSpec((TM, D), row), pl.BlockSpec((TM, D), row),
                   pl.BlockSpec((3, TM, D), row3), pl.BlockSpec((3, TM, D), row3), pl.BlockSpec((8, D), lambda i: (0, 0))])


def _mix_in_bwd(dz, dh2, h1, gm, win, comm=None):
    def body(dz_ref, dh_ref, h_ref, g_ref, *rest):
        w_any, (o_ref, ob_ref, s_ref, w_s, sem) = rest[:len(win)], rest[len(win):]
        _load_in_proj([(b, first) for b, (_, first) in zip(w_any, win)], w_s, sem)

        @pl.when(pl.program_id(0) == 0)
        def _():
            s_ref[...] = jnp.zeros_like(s_ref)

        du = _nn(dz_ref[0], w_s[0:D, :])
        for j in range(1, NG):
            du = du + _nn(dz_ref[j], w_s[j * D:(j + 1) * D, :])
        dx, dg = _rmsnorm_bwd(h_ref[...], g_ref[...], du)
        dh1 = dh_ref[...] + dx
        o_ref[...] = dh1
        ob_ref[...] = (0.5 * dh1).astype(BF)
        s_ref[0:1, :] += dg

    row = lambda i: (i, 0)
    return _call(
        body, name="mix_in_bwd", grid=(T // TM,), args=[dz, dh2, h1, gm] + [b for b, _ in win], comm=comm,
        in_specs=[pl.BlockSpec((NG, TM, D), lambda i: (0, i, 0)), pl.BlockSpec((TM, D), row),
                  pl.BlockSpec((TM, D), row), pl.BlockSpec((1, D), lambda i: (0, 0))] + [ANY] * len(win),
        out_shape=[jax.ShapeDtypeStruct((T, D), F32), jax.ShapeDtypeStruct((T, D), BF), jax.ShapeDtypeStruct((8, D), F32)],
        out_specs=[pl.BlockSpec((TM, D), row), pl.BlockSpec((TM, D), row), pl.BlockSpec((8, D), lambda i: (0, 0))],
        scratch_shapes=[pltpu.VMEM((NG * D, D), BF), pltpu.SemaphoreType.DMA((NDEV * len(win),))])


def _row_tile(n, want, mult):
    for t in range(min(want, n), 0, -1):
        if n % t == 0 and t % mult == 0:
            return t
    return n


def _pack_small(s_ffn1, s_in, s_mix, s_ffn2, s_final, dwa, dwb):
    def body(f1, mi, mo, f2, fl, wa_ref, wb_ref, v_ref, k_ref):
        for dst, (ref, row) in enumerate(((f1, 0), (mi, 0), (mo, 0), (mo, 1), (mo, 2), (f2, 0), (fl, 0), (fl, 1))):
            v_ref[dst:dst + 1, :] = ref[row:row + 1, :]
        for k in range(NDEV):
            k_ref[k, 0:32, :] = wa_ref[:, k * LANE:(k + 1) * LANE]
            k_ref[k, 32:40, :] = wb_ref[:, k * LANE:(k + 1) * LANE]

    return pl.pallas_call(
        body, name="pack_small",
        out_shape=(jax.ShapeDtypeStruct((8, D), F32), jax.ShapeDtypeStruct((NDEV, 40, LANE), F32)),
    )(s_ffn1, s_in, s_mix, s_ffn2, s_final, dwa, dwb)


def _adam_update(g, w, m, v):
    m2 = ADAM_B1 * m + (1.0 - ADAM_B1) * g
    v2 = ADAM_B2 * v + (1.0 - ADAM_B2) * (g * g)
    c1 = 1.0 - ADAM_B1 ** ADAM_STEP
    c2 = 1.0 - ADAM_B2 ** ADAM_STEP
    return -ADAM_LR * ((m2 / c1) / (jnp.sqrt(v2 / c2) + ADAM_EPS) + ADAM_WD * w), m2, v2


def _adam_small(vecs, convs, vec_params, tap_params):
    nv, nt = len(vec_params), len(tap_params)

    def body(*refs):
        v_ref, k_ref = refs[:2]
        p_refs = refs[2:2 + 3 * (nv + nt)]
        l_ref = refs[2 + 3 * (nv + nt)]
        o_refs = refs[3 + 3 * (nv + nt):]
        s, c = v_ref[0], k_ref[0]
        for k in range(1, NDEV):
            s = s + v_ref[k]
            c = c + k_ref[k]
        l_ref[...] = jnp.sum(s[7:8, :], axis=-1, keepdims=True)
        for i in range(nv):
            w_ref, m_ref, u_ref = p_refs[3 * i: 3 * i + 3]
            g_ref, d_ref, m2_ref, u2_ref = o_refs[4 * i: 4 * i + 4]
            g = s[i:i + 1, :]
            g_ref[...] = g
            d_ref[...], m2_ref[...], u2_ref[...] = _adam_update(g, w_ref[...], m_ref[...], u_ref[...])
        for i in range(nt):
            w_ref, m_ref, u_ref = p_refs[3 * (nv + i): 3 * (nv + i) + 3]
            g_ref, d_ref, m2_ref, u2_ref = o_refs[4 * (nv + i): 4 * (nv + i) + 4]
            first = tap_params[i][0]
            for k in range(w_ref.shape[0]):
                g = c[first + k:first + k + 1, :]
                g_ref[k] = g
                d_ref[k], m2_ref[k], u2_ref[k] = _adam_update(g, w_ref[k], m_ref[k], u_ref[k])

    params = [a for p in vec_params for a in p] + [a for p in tap_params for a in p[1:]]
    out_shape = [jax.ShapeDtypeStruct((1, 1), F32)]
    for p in list(vec_params) + [p[1:] for p in tap_params]:
        out_shape += [jax.ShapeDtypeStruct(p[0].shape, F32)] * 4
    outs = pl.pallas_call(body, name="adam_small", out_shape=tuple(out_shape))(vecs, convs, *params)
    groups = [tuple(outs[1 + 4 * i: 5 + 4 * i]) for i in range(nv + nt)]
    return outs[0], groups[:nv], groups[nv:]


def _adam_in_proj(parts, w, m, v, after):
    rows = w.shape[1]
    tr = _row_tile(D, 256, LANE)

    def body(*refs):
        p_refs = refs[:len(parts)]
        w_ref, m_ref, v_ref, g_ref, d_ref, m2_ref, v2_ref = refs[len(parts):]
        sums = []
        for p in p_refs:
            s = p[0].astype(F32)
            for k in range(1, p.shape[0]):
                s = s + p[k].astype(F32)
            sums.append(s)
        g = jnp.concatenate(sums, axis=0).T
        g_ref[...] = g
        d_ref[...], m2_ref[...], v2_ref[...] = _adam_update(g, w_ref[...], m_ref[...], v_ref[...])

    spec = pl.BlockSpec((tr, rows), lambda i: (i, 0))
    return _call(body, name="adam_in", grid=(D // tr,), args=list(parts) + [w, m, v], after=after,
                 in_specs=[pl.BlockSpec((p.shape[0], p.shape[1], tr), lambda i: (0, 0, i)) for p in parts] + [spec] * 3,
                 out_shape=[jax.ShapeDtypeStruct((D, rows), F32)] * 4, out_specs=[spec] * 4)


def _adam(gs, ws, ms, vs, name, after):
    n = len(gs)
    rows, cols = ws[0].shape
    tr = _row_tile(rows, min(256, rows // 2), 16)

    def body(*refs):
        for i in range(n):
            g_in, w, m, v = refs[4 * i], refs[4 * i + 1][...], refs[4 * i + 2][...], refs[4 * i + 3][...]
            g_ref, d_ref, m_ref, v_ref = refs[4 * n + 4 * i: 4 * n + 4 * i + 4]
            g = g_in[0].astype(F32)
            for k in range(1, g_in.shape[0]):
                g = g + g_in[k].astype(F32)
            g_ref[...] = g
            d_ref[...], m_ref[...], v_ref[...] = _adam_update(g, w, m, v)

    spec = pl.BlockSpec((tr, cols), lambda i: (i, 0))
    args, in_specs = [], []
    for i in range(n):
        slots, first = gs[i]
        args += [slots, ws[i], ms[i], vs[i]]
        in_specs += [pl.BlockSpec((slots.shape[0], tr, cols), lambda i, b=first // tr: (0, b + i, 0))] + [spec] * 3
    outs = _call(body, name=name, grid=(rows // tr,), args=args, in_specs=in_specs, after=after,
                 out_shape=[jax.ShapeDtypeStruct((rows, cols), F32)] * (4 * n), out_specs=[spec] * (4 * n))
    return [tuple(outs[4 * i: 4 * i + 4]) for i in range(n)]


def kernel(x, ffn1_norm, ffn1_w_gate, ffn1_w_up, ffn1_w_down, mix_norm, w_in, a_dw_w, a_dw_b, a_ln_g, a_ln_b, a_w_out, b_conv_w, b_w_out, w_o, ffn2_norm, ffn2_w_gate, ffn2_w_up, ffn2_w_down, final_norm, loss_target, m_ffn1_norm, m_ffn1_w_gate, m_ffn1_w_up, m_ffn1_w_down, m_mix_norm, m_w_in, m_a_dw_w, m_a_dw_b, m_a_ln_g, m_a_ln_b, m_a_w_out, m_b_conv_w, m_b_w_out, m_w_o, m_ffn2_norm, m_ffn2_w_gate, m_ffn2_w_up, m_ffn2_w_down, m_final_norm, v_ffn1_norm, v_ffn1_w_gate, v_ffn1_w_up, v_ffn1_w_down, v_mix_norm, v_w_in, v_a_dw_w, v_a_dw_b, v_a_ln_g, v_a_ln_b, v_a_w_out, v_b_conv_w, v_b_w_out, v_w_o, v_ffn2_norm, v_ffn2_w_gate, v_ffn2_w_up, v_ffn2_w_down, v_final_norm):
    names = ("ffn1_norm", "ffn1_w_gate", "ffn1_w_up", "ffn1_w_down", "mix_norm", "w_in", "a_dw_w", "a_dw_b",
             "a_ln_g", "a_ln_b", "a_w_out", "b_conv_w", "b_w_out", "w_o", "ffn2_norm", "ffn2_w_gate", "ffn2_w_up",
             "ffn2_w_down", "final_norm")
    w = dict(ffn1_norm=ffn1_norm, ffn1_w_gate=ffn1_w_gate, ffn1_w_up=ffn1_w_up, ffn1_w_down=ffn1_w_down,
             mix_norm=mix_norm, w_in=w_in, a_dw_w=a_dw_w, a_dw_b=a_dw_b, a_ln_g=a_ln_g, a_ln_b=a_ln_b,
             a_w_out=a_w_out, b_conv_w=b_conv_w, b_w_out=b_w_out, w_o=w_o, ffn2_norm=ffn2_norm,
             ffn2_w_gate=ffn2_w_gate, ffn2_w_up=ffn2_w_up, ffn2_w_down=ffn2_w_down, final_norm=final_norm)
    m = dict(ffn1_norm=m_ffn1_norm, ffn1_w_gate=m_ffn1_w_gate, ffn1_w_up=m_ffn1_w_up, ffn1_w_down=m_ffn1_w_down,
             mix_norm=m_mix_norm, w_in=m_w_in, a_dw_w=m_a_dw_w, a_dw_b=m_a_dw_b, a_ln_g=m_a_ln_g, a_ln_b=m_a_ln_b,
             a_w_out=m_a_w_out, b_conv_w=m_b_conv_w, b_w_out=m_b_w_out, w_o=m_w_o, ffn2_norm=m_ffn2_norm,
             ffn2_w_gate=m_ffn2_w_gate, ffn2_w_up=m_ffn2_w_up, ffn2_w_down=m_ffn2_w_down, final_norm=m_final_norm)
    v = dict(ffn1_norm=v_ffn1_norm, ffn1_w_gate=v_ffn1_w_gate, ffn1_w_up=v_ffn1_w_up, ffn1_w_down=v_ffn1_w_down,
             mix_norm=v_mix_norm, w_in=v_w_in, a_dw_w=v_a_dw_w, a_dw_b=v_a_dw_b, a_ln_g=v_a_ln_g, a_ln_b=v_a_ln_b,
             a_w_out=v_a_w_out, b_conv_w=v_b_conv_w, b_w_out=v_b_w_out, w_o=v_w_o, ffn2_norm=v_ffn2_norm,
             ffn2_w_gate=v_ffn2_w_gate, ffn2_w_up=v_ffn2_w_up, ffn2_w_down=v_ffn2_w_down, final_norm=v_final_norm)
    flat = _pack_weights(dict(wg1=ffn1_w_gate[0].T, wu1=ffn1_w_up[0].T, wd1=ffn1_w_down[0], wg2=ffn2_w_gate[0].T,
                              wu2=ffn2_w_up[0].T, wd2=ffn2_w_down[0], win=w_in[0], wa=a_w_out[0], wb=b_w_out[0],
                              wo=w_o[0]))
    cw_shard = jnp.concatenate([a_dw_w[0], jnp.zeros((1, LANE), F32), b_conv_w[0], jnp.zeros((5, LANE), F32)], axis=0)

    x2, tgt = x[0], loss_target[0]
    st_a, st_b, st_b2 = ("wg1", "wu1"), ("wd1", "win/0/2"), ("win/1/2",)
    st_c, st_d, st_e = ("wa", "wb", "wo", "wg2"), ("wu2",), ("wd2",)

    buf_a, cw = _run_comm(_join(_ag_comm(st_a, flat), _direct_comm(cw_shard, False)), "ag_ffn1")
    n1, gg1, uu1, act1, buf_b = _ffn_gate_up(x2, ffn1_norm, (buf_a, buf_a), (0, F), "ffn1_gate_up", _ag_comm(st_b, flat))
    h1, buf_b2 = _ffn_down(x2, act1, buf_b, 0, "ffn1_down", _ag_comm(st_b2, flat))
    win = ((buf_b, F), (buf_b2, 0))
    u, z, buf_c = _mix_in(h1, mix_norm, win, _ag_comm(st_c, flat))
    dft = _dft_constants()
    cw = jnp.transpose(cw, (1, 0, 2)).reshape(40, D)
    a1, q, buf_d = _conv_fwd_dft(z, cw, a_dw_b, dft, _ag_comm(st_d, flat))
    h2, ya, yb, buf_e = _mix_out(a1, q, z, h1, a_ln_g, a_ln_b, buf_c, _ag_comm(st_e, flat))
    ffn2_bufs, ffn2_offs = (buf_c, buf_d, buf_e), (3 * D, 0, 0)
    dh3, dhb3, s_final, n2, gg2, uu2, act2 = _ffn_fwd(h2, ffn2_norm, ffn2_bufs, ffn2_offs, "ffn2_fwd",
                                          final=(final_norm.reshape(1, D), tgt))

    tr_f = F // 2 if (F // 2) % LANE == 0 else F
    def pair(stage, src):
        return _rs_pair_comm(stage, src)

    def chip(stage, src, pair_buf, tag):
        return _rs_chip_comm(_pair_add(stage, src, pair_buf, "pair_add_" + tag))

    (dgu2,) = _ffn_bwd_hidden(dhb3, gg2, uu2, buf_e, 0, "ffn2_bwd_h")
    (gu2,) = _tn_matmul(dgu2, n2, tr_f, "dw_gu2")
    s2a, src2a = ("wg2", "wu2"), dict(wg2=(gu2, 0), wu2=(gu2, F))
    (gd2,) = _tn_matmul(act2, dhb3, tr_f, "dw_d2")
    s2b, src2b = ("wd2",), dict(wd2=(gd2, 0))
    dh2, s_ffn2, pair2a, pair2b = _ffn_bwd_input(dgu2, dh3, h2, ffn2_norm, (buf_c, buf_d), (3 * D, 0), "ffn2_bwd_x",
                                                 _join(pair(s2a, src2a), pair(s2b, src2b)))
    dzg, da1, dq, lsq, rsq, s_mix, recv2b = _mix_out_bwd(dh2, ya, yb, z, a1, q, a_ln_g, a_ln_b, buf_c,
                                                          chip(s2b, src2b, pair2b, "2b"))
    (gsq,) = _tn_matmul(lsq, rsq, D, "dw_square")
    ssq, srcsq = ("wa", "wb", "wo"), dict(wa=(gsq, D), wb=(gsq, 2 * D), wo=(gsq, 0))
    dz, dwa, dwb, recv2a, pairsq = _conv_bwd_dft(z, da1, dq, dzg, cw, dft,
                                                 _join(chip(s2a, src2a, pair2a, "2a"), pair(ssq, srcsq)))
    gin, recvsq = _tn_matmul(dz, u, D, "dw_in", chip(ssq, srcsq, pairsq, "sq"))
    sin_a, sin_b, srcin = ("win/0/2",), ("win/1/2",), {"win/0/2": (gin, 0), "win/1/2": (gin, 0)}
    dh1, dhb1, s_in, pairin_a, pairin_b = _mix_in_bwd(dz, dh2, h1, mix_norm, win,
                                                _join(pair(sin_a, srcin), pair(sin_b, srcin)))
    dgu1, recvin_a = _ffn_bwd_hidden(dhb1, gg1, uu1, buf_b, 0, "ffn1_bwd_h",
                                           chip(sin_a, srcin, pairin_a, "in_a"))
    gu1, recvin_b = _tn_matmul(dgu1, n1, tr_f, "dw_gu1", chip(sin_b, srcin, pairin_b, "in_b"))
    s1a, src1a = ("wg1", "wu1"), dict(wg1=(gu1, 0), wu1=(gu1, F))
    gd1, pair1a = _tn_matmul(act1, dhb1, tr_f, "dw_d1", pair(s1a, src1a))
    s1b, src1b = ("wd1",), dict(wd1=(gd1, 0))
    xchg1 = _join(chip(s1a, src1a, pair1a, "1a"), pair(s1b, src1b))
    xchg1_sems, xchg1_bufs, token = _comm_start(xchg1, "xchg_ffn1_start")
    dx, s_ffn1 = _ffn_bwd_input(dgu1, dh1, x2, ffn1_norm, (buf_a, buf_a), (0, F), "ffn1_bwd_x", after=token)
    (_, gd1), (recv1a, pair1b) = _comm_wait(xchg1, "xchg_ffn1_wait", xchg1_sems, xchg1_bufs, s_ffn1)
    src1b = dict(wd1=(gd1, 0))

    vec8, convk = _pack_small(s_ffn1, s_in, s_mix, s_ffn2, s_final, dwa, dwb)
    tail = _join(chip(s1b, src1b, pair1b, "1b"), _join(_direct_comm(vec8, False), _direct_comm(convk, True)))
    tail_sems, tail_bufs, token = _comm_start(tail, "xchg_tail_start")

    fs = F // NDEV
    g = dict(ffn1_w_gate=(recv1a, 0), ffn1_w_up=(recv1a, fs), ffn2_w_gate=(recv2a, 0), ffn2_w_up=(recv2a, fs),
             ffn2_w_down=(recv2b, 0), a_w_out=(recvsq, 0), b_w_out=(recvsq, D // NDEV), w_o=(recvsq, 2 * (D // NDEV)))
    grad, upd = {}, {}

    def run(group, name, after, as2d=lambda a: a[0], back=lambda a, n: a.reshape(w[n].shape)):
        res = _adam([g[n] for n in group], [as2d(w[n]) for n in group], [as2d(m[n]) for n in group],
                    [as2d(v[n]) for n in group], name, after)
        for n, r in zip(group, res):
            grad[n], upd[n] = back(r[0], n), tuple(back(a, n) for a in r[1:])
        return res[0][0]

    done = run(("ffn1_w_gate", "ffn1_w_up", "ffn2_w_gate", "ffn2_w_up"), "adam_gate_up", token,
               as2d=lambda a: a[0].T, back=lambda a, n: a.T[None])
    r_in = _adam_in_proj([recvin_a, recvin_b], w_in[0], m_w_in[0], v_w_in[0], done)
    grad["w_in"], upd["w_in"] = r_in[0][None], tuple(a[None] for a in r_in[1:])
    done = run(("a_w_out", "b_w_out", "w_o"), "adam_square", r_in[0])
    _, (recv1b, vec_all, conv_all) = _comm_wait(tail, "xchg_tail_wait", tail_sems, tail_bufs, done)
    g["ffn1_w_down"] = (recv1b, 0)
    run(("ffn1_w_down", "ffn2_w_down"), "adam_down", done)
    vec_names = ("ffn1_norm", "mix_norm", "a_ln_g", "a_ln_b", "a_dw_b", "ffn2_norm", "final_norm")
    tap_names, tap_rows = ("a_dw_w", "b_conv_w"), (0, 32)
    taps = lambda a: jnp.transpose(a, (1, 0, 2))
    loss, vec_res, tap_res = _adam_small(
        vec_all, conv_all, [tuple(t[n].reshape(1, D) for t in (w, m, v)) for n in vec_names],
        [(r,) + tuple(taps(t[n]) for t in (w, m, v)) for n, r in zip(tap_names, tap_rows)])
    for n, r in zip(vec_names, vec_res):
        grad[n], upd[n] = r[0].reshape(w[n].shape), tuple(a.reshape(w[n].shape) for a in r[1:])
    for n, r in zip(tap_names, tap_res):
        grad[n], upd[n] = taps(r[0]), tuple(taps(a) for a in r[1:])

    return (loss.reshape(()), dx.reshape(x.shape), *[grad[n] for n in names], *[upd[n][0] for n in names],
            *[upd[n][1] for n in names], *[upd[n][2] for n in names])
```

```python
import jax
import jax.numpy as jnp
from jax import lax
from jax.experimental import pallas as pl
from jax.experimental.pallas import tpu as pltpu

T = 4096
D = 1024
F = 2816
NG = 7
NDEV = 8
NCHIP = 4
KA, KB = 31, 3
EPS = 1e-6
ADAM_LR, ADAM_B1, ADAM_B2, ADAM_EPS, ADAM_WD, ADAM_STEP = 0.001, 0.9, 0.999, 1e-08, 0.01, 10

TM = 512
FC = 256
TB = 1024
NB = 256
HB = NB // 2
CW = 256
CHB = 64
LANE = 128
TK = 2048
VMEM_LIMIT = 56 * 1024 * 1024

BF = jnp.bfloat16
F32 = jnp.float32
MESH = pl.DeviceIdType.MESH
ANY = pl.BlockSpec(memory_space=pl.ANY)
COLLECTIVE_ID = {(1,): 0, (2, 4, 6): 1, (1, 2, 4, 6): 2, (1, 2, 4): 3, (1, 2, 3, 4, 5, 6, 7): 4}
START_COLLECTIVE_ID = {(1, 2, 4, 6): 5, (1, 2, 3, 4, 5, 6, 7): 6}

ORDER = ("wg1", "wu1", "wd1", "wg2", "wu2", "wd2", "win", "wa", "wb", "wo")


class _Layout:
    def __init__(self):
        fs, dis, ds = F // NDEV, NG * D // NDEV, D // NDEV
        self.rows = dict(wg1=fs, wu1=fs, wd1=fs, wg2=fs, wu2=fs, wd2=fs, win=dis, wa=ds, wb=ds, wo=ds)
        self.fl, off = {}, 0
        for n in ORDER:
            self.fl[n] = off
            off += self.rows[n]
        self.RT = off


class _Stage:
    def __init__(self, names):
        lay = _Layout()
        self.names = names
        self.rows, self.full, self.sub, self.fl = {}, {}, {}, {}
        for n in names:
            base, i, k = (n.split("/") + ["0", "1"])[:3]
            self.full[n] = lay.rows[base]
            self.rows[n] = lay.rows[base] // int(k)
            self.sub[n] = int(i) * self.rows[n]
            self.fl[n] = lay.fl[base] + self.sub[n]
        self.off, self.wc, o, w = {}, {}, 0, 0
        for n in names:
            self.off[n], self.wc[n] = o, w
            o += self.rows[n]
            w += NDEV * self.rows[n]
        self.R, self.W = o, w

    def grad_row(self, n, first, dev_lin):
        return first + dev_lin * self.full[n] + self.sub[n]


def _nt(a, b):
    return lax.dot_general(a, b, (((1,), (1,)), ((), ())), preferred_element_type=F32)


def _nn(a, b):
    return lax.dot_general(a, b, (((1,), (0,)), ((), ())), preferred_element_type=F32)


def _tn(a, b):
    return lax.dot_general(a, b, (((0,), (0,)), ((), ())), preferred_element_type=F32)


def _sig(x):
    return 1.0 / (1.0 + jnp.exp(-x))


def _position():
    return lax.axis_index("x"), lax.axis_index("y"), lax.axis_index("c")


def _peer(pos, j):
    x, y, c = pos
    return (1 - x if j & 4 else x, 1 - y if j & 2 else y, 1 - c if j & 1 else c)


def _lin(pos):
    return 4 * pos[0] + 2 * pos[1] + pos[2]


def _chip(pos):
    return 2 * pos[0] + pos[1]


class _Comm:
    def __init__(self, inputs, out_shapes, scratch, start, finish, middle=None, peers=None):
        self.inputs, self.out_shapes, self.scratch = inputs, out_shapes, scratch
        self.start, self.finish, self.middle = start, finish, middle
        self.peers = peers


def _handshake(peers):
    barrier = pltpu.get_barrier_semaphore()
    for j in peers:
        pl.semaphore_signal(barrier, inc=1, device_id=_peer(_position(), j), device_id_type=MESH)
    pl.semaphore_wait(barrier, len(peers))


def _call(body, *, name, grid, args, in_specs, out_shape, out_specs, scratch_shapes=(), comm=None,
          num_scalar_prefetch=0, after=None):
    in_specs, out_shape, out_specs, scratch_shapes = list(in_specs), list(out_shape), list(out_specs), list(scratch_shapes)
    if after is not None:
        inner, pos = body, num_scalar_prefetch + len(in_specs)
        body = lambda *refs: inner(*refs[:pos], *refs[pos + 1:])
        args, in_specs = list(args) + [after], in_specs + [ANY]
    n_in, n_out, n_scr = len(in_specs), len(out_shape), len(scratch_shapes)
    sp = num_scalar_prefetch
    if comm is None:
        kernel_fn = lambda *refs: body(*refs)
        c_in = c_out = c_scr = 0
    else:
        c_in, c_out, c_scr = len(comm.inputs), len(comm.out_shapes), len(comm.scratch)

        def kernel_fn(*refs):
            pre, refs = refs[:sp], refs[sp:]
            ins, cins = refs[:n_in], refs[n_in:n_in + c_in]
            o0 = n_in + c_in
            outs, couts = refs[o0:o0 + n_out], refs[o0 + n_out:o0 + n_out + c_out]
            s0 = o0 + n_out + c_out
            scr, cscr = refs[s0:s0 + n_scr], refs[s0 + n_scr:]
            step, steps = pl.program_id(0), grid[0]
            for a in range(1, len(grid)):
                step, steps = step * grid[a] + pl.program_id(a), steps * grid[a]
            first, last = step == 0, step == steps - 1

            @pl.when(first)
            def _():
                if comm.peers is not None:
                    _handshake(comm.peers)
                comm.start(cins, couts, cscr)

            if comm.middle is not None:
                @pl.when(step == (steps // 2 if steps > 2 else steps - 1))
                def _():
                    comm.middle(cins, couts, cscr)

            body(*pre, *ins, *outs, *scr)

            @pl.when(last)
            def _():
                comm.finish(cins, couts, cscr)

        args = list(args) + list(comm.inputs)
        in_specs += [ANY] * c_in
        out_shape += list(comm.out_shapes)
        out_specs += [ANY] * c_out
        scratch_shapes += list(comm.scratch)
    params = pltpu.CompilerParams(dimension_semantics=("arbitrary",) * len(grid), vmem_limit_bytes=VMEM_LIMIT,
                                  collective_id=COLLECTIVE_ID[comm.peers] if comm is not None and comm.peers else None)
    if sp:
        grid_spec = pltpu.PrefetchScalarGridSpec(num_scalar_prefetch=sp, grid=grid, in_specs=in_specs,
                                                 out_specs=out_specs, scratch_shapes=scratch_shapes)
        return pl.pallas_call(kernel_fn, name=name, grid_spec=grid_spec, out_shape=out_shape,
                              compiler_params=params)(*args)
    return pl.pallas_call(kernel_fn, name=name, grid=grid, in_specs=in_specs, out_shape=out_shape, out_specs=out_specs,
                          scratch_shapes=scratch_shapes, compiler_params=params)(*args)


def _join(a, b):
    na = (len(a.inputs), len(a.out_shapes), len(a.scratch))

    def split(refs):
        return ([r[:n] for r, n in zip(refs, na)], [r[n:] for r, n in zip(refs, na)])

    def start(*refs):
        ra, rb = split(refs)
        a.start(*ra)
        b.start(*rb)

    def finish(*refs):
        ra, rb = split(refs)
        a.finish(*ra)
        b.finish(*rb)

    def middle(*refs):
        for stage, r in zip((a, b), split(refs)):
            if stage.middle is not None:
                stage.middle(*r)

    return _Comm(list(a.inputs) + list(b.inputs), list(a.out_shapes) + list(b.out_shapes),
                 list(a.scratch) + list(b.scratch), start, finish,
                 middle if (a.middle is not None or b.middle is not None) else None,
                 peers=tuple(sorted(set(a.peers) | set(b.peers))) if a.peers and b.peers else None)


def _run_comm(comm, name):
    def body(*refs):
        c_in, c_out = len(comm.inputs), len(comm.out_shapes)
        parts = (refs[:c_in], refs[c_in:c_in + c_out], refs[c_in + c_out:])
        if comm.peers is not None:
            _handshake(comm.peers)
        comm.start(*parts)
        if comm.middle is not None:
            comm.middle(*parts)
        comm.finish(*parts)

    return pl.pallas_call(
        body, name=name, out_shape=list(comm.out_shapes), in_specs=[ANY] * len(comm.inputs),
        out_specs=[ANY] * len(comm.out_shapes), scratch_shapes=list(comm.scratch),
        compiler_params=pltpu.CompilerParams(collective_id=COLLECTIVE_ID[comm.peers] if comm.peers else None),
    )(*comm.inputs)


HBM = pl.BlockSpec(memory_space=pltpu.HBM)
SEM = pl.BlockSpec(memory_space=pltpu.SEMAPHORE)
DATAFLOW = pltpu.SideEffectType.DATAFLOW_SIDE_EFFECTING


def _comm_start(comm, name):
    c_in, c_out = len(comm.inputs), len(comm.out_shapes)
    sems = [s(()) if s is pltpu.SemaphoreType.DMA else s for s in comm.scratch]
    bufs = list(comm.inputs) + [lax.empty(s.shape, s.dtype) for s in comm.out_shapes]

    def body(*refs):
        if comm.peers is not None:
            _handshake(comm.peers)
        sem_refs = refs[c_in + c_out:c_in + c_out + len(sems)]
        comm.start(refs[:c_in], refs[c_in:c_in + c_out], sem_refs)
        refs[-1][...] = jnp.zeros_like(refs[-1])

    outs = pl.pallas_call(
        body, name=name,
        out_shape=sems + [pltpu.HBM(b.shape, b.dtype) for b in bufs] + [jax.ShapeDtypeStruct((8, LANE), F32)],
        in_specs=[HBM] * len(bufs),
        out_specs=[SEM] * len(sems) + [HBM] * len(bufs) + [pl.BlockSpec(memory_space=pltpu.VMEM)],
        input_output_aliases={i: len(sems) + i for i in range(len(bufs))},
        compiler_params=pltpu.CompilerParams(
            has_side_effects=DATAFLOW, collective_id=START_COLLECTIVE_ID[comm.peers] if comm.peers else None),
    )(*[pltpu.with_memory_space_constraint(b, pltpu.HBM) for b in bufs])
    return outs[:len(sems)], outs[len(sems):-1], outs[-1]


def _comm_wait(comm, name, sems, bufs, after):
    c_in, c_out = len(comm.inputs), len(comm.out_shapes)

    def body(*refs):
        sem_refs = refs[c_in + c_out:c_in + c_out + len(sems)]
        comm.finish(refs[:c_in], refs[c_in:c_in + c_out], sem_refs)

    outs = pl.pallas_call(
        body, name=name, out_shape=[pltpu.HBM(b.shape, b.dtype) for b in bufs],
        in_specs=[HBM] * len(bufs) + [SEM] * len(sems) + [ANY], out_specs=[HBM] * len(bufs),
        input_output_aliases={i: i for i in range(len(bufs))},
        compiler_params=pltpu.CompilerParams(has_side_effects=DATAFLOW),
    )(*bufs, *sems, after)
    return outs[:c_in], outs[c_in:]


def _ag_comm(names, flat):
    st = _Stage(names)

    def ring(me):
        x, y, c = me
        diagonal = x == y
        up = (jnp.where(diagonal, x, 1 - x), jnp.where(diagonal, 1 - y, y), c)
        down = (jnp.where(diagonal, 1 - x, x), jnp.where(diagonal, y, 1 - y), c)
        low = c == 0
        passed = tuple(jnp.where(low, d, u) for d, u in zip(down, up))
        target = tuple(jnp.where(low, u, d) for d, u in zip(down, up))
        return up, down, (1 - x, 1 - y, c), passed, target

    def parts(refs):
        (flat_ref,), (out_ref,), (send_sems, recv_sems, local_sem) = refs
        me = _position()

        def region(name, dev):
            r = st.rows[name]
            return out_ref.at[pl.ds(st.wc[name] + _lin(dev) * r, r), :]

        def own(name):
            return flat_ref.at[pl.ds(st.fl[name], st.rows[name]), :]

        def copies(k, dev, to, from_flat):
            return [pltpu.make_async_remote_copy(
                src_ref=own(n) if from_flat else region(n, dev), dst_ref=region(n, dev), send_sem=send_sems.at[k],
                recv_sem=recv_sems.at[k], device_id=to, device_id_type=MESH) for n in names]

        def whole(k):
            return pltpu.make_async_remote_copy(
                src_ref=flat_ref.at[pl.ds(0, st.R), :], dst_ref=out_ref.at[pl.ds(0, st.R), :],
                send_sem=send_sems.at[k], recv_sem=recv_sems.at[k], device_id=me, device_id_type=MESH)

        return me, region, own, copies, whole, flat_ref, out_ref, local_sem

    def start(*refs):
        me, region, own, copies, _, _, _, local_sem = parts(refs)
        for n in names:
            pltpu.make_async_copy(own(n), region(n, me), local_sem).start()
        up, down, _, _, _ = ring(me)
        for k, to in ((1, up), (2, down), (0, _peer(me, 1))):
            for cp in copies(k, me, to, True):
                cp.start()

    def middle(*refs):
        me, _, _, copies, whole, _, _, _ = parts(refs)
        up, down, _, passed, target = ring(me)
        sib = _peer(me, 1)
        whole(1).wait_recv()
        whole(2).wait_recv()
        for k, dev, to in ((3, passed, target), (4, down, sib), (5, up, sib)):
            for cp in copies(k, dev, to, False):
                cp.start()

    def finish(*refs):
        me, _, _, copies, whole, flat_ref, out_ref, local_sem = parts(refs)
        _, _, across, _, _ = ring(me)
        whole(3).wait_recv()
        for cp in copies(6, across, _peer(me, 1), False):
            cp.start()
        whole(0).wait_recv()
        for j in range(3):
            whole(4 + j).wait_recv()
        for k in range(7):
            whole(k).wait_send()
        pltpu.make_async_copy(flat_ref.at[pl.ds(0, st.R), :], out_ref.at[pl.ds(0, st.R), :], local_sem).wait()

    return _Comm([flat], [jax.ShapeDtypeStruct((st.W, D), BF)],
                 [pltpu.SemaphoreType.DMA((7,)), pltpu.SemaphoreType.DMA((7,)), pltpu.SemaphoreType.DMA],
                 start, finish, middle, peers=(1, 2, 4))


def _rs_pair_comm(names, src):
    st = _Stage(names)
    arrays = []
    for n in names:
        if not any(src[n][0] is a for a in arrays):
            arrays.append(src[n][0])
    idx = {n: [i for i, a in enumerate(arrays) if a is src[n][0]][0] for n in names}

    def slot_wait(refs):
        recv = refs[1][0]
        send_sem, recv_sem = refs[2]
        return pltpu.make_async_remote_copy(src_ref=recv, dst_ref=recv, send_sem=send_sem, recv_sem=recv_sem,
                                            device_id=_position(), device_id_type=MESH)

    def start(*refs):
        ins, (recv,), (send_sem, recv_sem) = refs
        me = _position()
        sib = _peer(me, 1)
        for q in range(NCHIP):
            dev = (q // 2, q % 2, sib[2])
            for n in names:
                r = st.rows[n]
                pltpu.make_async_remote_copy(
                    src_ref=ins[idx[n]].at[pl.ds(st.grad_row(n, src[n][1], _lin(dev)), r), :],
                    dst_ref=recv.at[q, pl.ds(st.off[n], r), :], send_sem=send_sem, recv_sem=recv_sem,
                    device_id=sib, device_id_type=MESH).start()

    def finish(*refs):
        w = slot_wait(refs)
        w.wait_recv()
        w.wait_send()

    return _Comm(arrays, [jax.ShapeDtypeStruct((NCHIP, st.R, D), BF)],
                 [pltpu.SemaphoreType.DMA, pltpu.SemaphoreType.DMA], start, finish, peers=(1,))


def _pair_add(names, src, recv, name):
    st = _Stage(names)
    c_arr = jnp.reshape(lax.axis_index("c"), (1,)).astype(jnp.int32)

    def body(c_ref, *refs):
        r_ref, o_ref = refs[len(names)], refs[len(names) + 1]
        for a_ref, n in zip(refs, names):
            rows = slice(st.off[n], st.off[n] + st.rows[n])
            o_ref[rows, :] = (a_ref[...].astype(F32) + r_ref[rows, :].astype(F32)).astype(BF)

    def shard_spec(n):
        r = st.rows[n]
        base, step = st.grad_row(n, src[n][1], 0) // r, st.full[n] // r
        return pl.BlockSpec((r, D), lambda q, c_ref: (base + step * (2 * q + c_ref[0]), 0))

    slot = pl.BlockSpec((None, st.R, D), lambda q, c_ref: (q, 0, 0))
    return _call(body, name=name, grid=(NCHIP,), args=[c_arr] + [src[n][0] for n in names] + [recv],
                 in_specs=[shard_spec(n) for n in names] + [slot],
                 out_shape=[jax.ShapeDtypeStruct((NCHIP, st.R, D), BF)], out_specs=[slot], num_scalar_prefetch=1)[0]


def _rs_chip_comm(part):
    def copies(refs):
        (p_ref,), (recv,), (send_sems, recv_sems, local_sem) = refs
        me = _position()
        mine = pltpu.make_async_copy(p_ref.at[_chip(me)], recv.at[_chip(me)], local_sem)
        out = []
        for j, bits in enumerate((4, 2, 6)):
            to = _peer(me, bits)
            out.append(pltpu.make_async_remote_copy(
                src_ref=p_ref.at[_chip(to)], dst_ref=recv.at[_chip(me)], send_sem=send_sems.at[j],
                recv_sem=recv_sems.at[j], device_id=to, device_id_type=MESH))
        return mine, out

    def start(*refs):
        mine, out = copies(refs)
        mine.start()
        for cp in out:
            cp.start()

    def finish(*refs):
        mine, out = copies(refs)
        for cp in out:
            cp.wait_recv()
        for cp in out:
            cp.wait_send()
        mine.wait()

    return _Comm([part], [jax.ShapeDtypeStruct(part.shape, BF)],
                 [pltpu.SemaphoreType.DMA((3,)), pltpu.SemaphoreType.DMA((3,)), pltpu.SemaphoreType.DMA],
                 start, finish, peers=(2, 4, 6))


def _direct_comm(x, scatter):
    def copies(refs):
        (x_ref,), (out_ref,), (send_sems, recv_sems, local_sem) = refs
        me = _position()

        def piece(dev):
            return x_ref.at[_lin(dev)] if scatter else x_ref

        mine = pltpu.make_async_copy(piece(me), out_ref.at[_lin(me)], local_sem)
        return mine, [pltpu.make_async_remote_copy(
            src_ref=piece(_peer(me, j)), dst_ref=out_ref.at[_lin(me)], send_sem=send_sems.at[j - 1],
            recv_sem=recv_sems.at[j - 1], device_id=_peer(me, j), device_id_type=MESH) for j in range(1, NDEV)]

    def start(*refs):
        mine, cps = copies(refs)
        mine.start()
        for cp in cps:
            cp.start()

    def finish(*refs):
        mine, cps = copies(refs)
        for cp in cps:
            cp.wait_recv()
        for cp in cps:
            cp.wait_send()
        mine.wait()

    shape = x.shape if scatter else (NDEV,) + x.shape
    return _Comm([x], [jax.ShapeDtypeStruct(shape, x.dtype)],
                 [pltpu.SemaphoreType.DMA((7,)), pltpu.SemaphoreType.DMA((7,)), pltpu.SemaphoreType.DMA],
                 start, finish, peers=(1, 2, 3, 4, 5, 6, 7))


def _pack_weights(shards):
    lay = _Layout()

    def body(*refs):
        o_ref = refs[-1]
        for ref, n in zip(refs, ORDER):
            x = ref[...].T if n == "win" else ref[...]
            o_ref[lay.fl[n]:lay.fl[n] + lay.rows[n], :] = x.astype(BF)

    return pl.pallas_call(
        body, name="pack_weights", out_shape=jax.ShapeDtypeStruct((lay.RT, D), BF),
        compiler_params=pltpu.CompilerParams(vmem_limit_bytes=VMEM_LIMIT))(*[shards[n] for n in ORDER])


def _load_ffn_weights(srcs, offs, scratch, sem):
    @pl.when(pl.program_id(0) == 0)
    def _():
        cps = [pltpu.make_async_copy(s.at[pl.ds(off, dst.shape[0]), :], dst, sem.at[i])
               for i, (s, off, dst) in enumerate(zip(srcs, offs, scratch))]
        for cp in cps:
            cp.start()
        for cp in cps:
            cp.wait()


def _final_loss_tile(xf, g, tgt, s_ref):
    r = lax.rsqrt(jnp.mean(xf * xf, axis=-1, keepdims=True) + EPS)
    xr = xf * r
    e = xr * g - tgt
    s_ref[1:2, :] += jnp.sum(e * e, axis=0, keepdims=True) * (0.5 / D)
    dy = e * (1.0 / D)
    s_ref[0:1, :] += jnp.sum(dy * xr, axis=0, keepdims=True)
    gdy = dy * g
    return r * gdy - xr * (r * jnp.mean(gdy * xr, axis=-1, keepdims=True))


def _ffn_fwd(x, g, wbufs, offs, name, comm=None, final=None):
    nf = F // FC

    def body(x_ref, g_ref, b0, b1, b2, *rest):
        if final is None:
            h_ref, n_ref, gg_ref, uu_ref, a_ref, wg_s, wu_s, wd_s, sem = rest
        else:
            gf_ref, t_ref, dh_ref, dhb_ref, s_ref, n_ref, gg_ref, uu_ref, a_ref, wg_s, wu_s, wd_s, sem = rest

            @pl.when(pl.program_id(0) == 0)
            def _():
                s_ref[...] = jnp.zeros_like(s_ref)

        _load_ffn_weights((b0, b1, b2), offs, (wg_s, wu_s, wd_s), sem)
        xf = x_ref[...]
        r = lax.rsqrt(jnp.mean(xf * xf, axis=-1, keepdims=True) + EPS)
        nb = (xf * r * g_ref[...]).astype(BF)
        n_ref[...] = nb
        acc = jnp.zeros((TM, D), F32)
        for c in range(nf):
            sl = slice(c * FC, (c + 1) * FC)
            gb = _nt(nb, wg_s[sl, :]).astype(BF)
            ub = _nt(nb, wu_s[sl, :]).astype(BF)
            gg_ref[:, sl] = gb
            uu_ref[:, sl] = ub
            a = (gb * _sig(gb)) * ub
            a_ref[0, :, sl] = a
            acc = acc + _nn(a, wd_s[sl, :])
        h = xf + 0.5 * acc
        if final is None:
            h_ref[...] = h
        else:
            dh = _final_loss_tile(h, gf_ref[...], t_ref[...], s_ref)
            dh_ref[...] = dh
            dhb_ref[...] = (0.5 * dh).astype(BF)

    row = lambda i: (i, 0)
    vec = pl.BlockSpec((1, D), lambda i: (0, 0))
    tile = pl.BlockSpec((TM, D), row)
    saved_shapes = [jax.ShapeDtypeStruct((T, D), BF), jax.ShapeDtypeStruct((T, F), BF), jax.ShapeDtypeStruct((T, F), BF),
                    jax.ShapeDtypeStruct((1, T, F), BF)]
    saved_specs = [tile, pl.BlockSpec((TM, F), row), pl.BlockSpec((TM, F), row),
                   pl.BlockSpec((1, TM, F), lambda i: (0, i, 0))]
    if final is None:
        extra_args, extra_specs = [], []
        head_shapes, head_specs = [jax.ShapeDtypeStruct((T, D), F32)], [tile]
    else:
        extra_args, extra_specs = list(final), [vec, tile]
        head_shapes = [jax.ShapeDtypeStruct((T, D), F32), jax.ShapeDtypeStruct((T, D), BF), jax.ShapeDtypeStruct((8, D), F32)]
        head_specs = [tile, tile, pl.BlockSpec((8, D), lambda i: (0, 0))]
    return _call(
        body, name=name, grid=(T // TM,), args=[x, g, *wbufs, *extra_args], comm=comm,
        in_specs=[tile, vec, ANY, ANY, ANY] + extra_specs,
        out_shape=head_shapes + saved_shapes, out_specs=head_specs + saved_specs,
        scratch_shapes=[pltpu.VMEM((F, D), BF)] * 3 + [pltpu.SemaphoreType.DMA((3,))])


def _ffn_gate_up(x, g, wbufs, offs, name, comm=None):
    nf = F // FC

    def body(x_ref, g_ref, b0, b1, n_ref, gg_ref, uu_ref, a_ref, wg_s, wu_s, sem):
        _load_ffn_weights((b0, b1), offs, (wg_s, wu_s), sem)
        xf = x_ref[...]
        r = lax.rsqrt(jnp.mean(xf * xf, axis=-1, keepdims=True) + EPS)
        nb = (xf * r * g_ref[...]).astype(BF)
        n_ref[...] = nb
        for c in range(nf):
            sl = slice(c * FC, (c + 1) * FC)
            gb = _nt(nb, wg_s[sl, :]).astype(BF)
            ub = _nt(nb, wu_s[sl, :]).astype(BF)
            gg_ref[:, sl] = gb
            uu_ref[:, sl] = ub
            a_ref[0, :, sl] = (gb * _sig(gb)) * ub

    row = lambda i: (i, 0)
    tile = pl.BlockSpec((TM, D), row)
    return _call(
        body, name=name, grid=(T // TM,), args=[x, g, *wbufs], comm=comm,
        in_specs=[tile, pl.BlockSpec((1, D), lambda i: (0, 0)), ANY, ANY],
        out_shape=[jax.ShapeDtypeStruct((T, D), BF), jax.ShapeDtypeStruct((T, F), BF), jax.ShapeDtypeStruct((T, F), BF),
                   jax.ShapeDtypeStruct((1, T, F), BF)],
        out_specs=[tile, pl.BlockSpec((TM, F), row), pl.BlockSpec((TM, F), row),
                   pl.BlockSpec((1, TM, F), lambda i: (0, i, 0))],
        scratch_shapes=[pltpu.VMEM((F, D), BF)] * 2 + [pltpu.SemaphoreType.DMA((2,))])


def _ffn_down(x, act, wbuf, off, name, comm=None):
    def body(x_ref, a_ref, b0, h_ref, wd_s, sem):
        _load_ffn_weights((b0,), (off,), (wd_s,), sem)
        h_ref[...] = x_ref[...] + 0.5 * _nn(a_ref[0], wd_s[...])

    tile = pl.BlockSpec((TM, D), lambda i: (i, 0))
    return _call(
        body, name=name, grid=(T // TM,), args=[x, act, wbuf], comm=comm,
        in_specs=[tile, pl.BlockSpec((1, TM, F), lambda i: (0, i, 0)), ANY],
        out_shape=[jax.ShapeDtypeStruct((T, D), F32)], out_specs=[tile],
        scratch_shapes=[pltpu.VMEM((F, D), BF), pltpu.SemaphoreType.DMA((1,))])


def _load_in_proj(parts, w_s, sem):
    @pl.when(pl.program_id(0) == 0)
    def _():
        shard = NG * D // NDEV
        rows = shard // len(parts)
        cps = [pltpu.make_async_copy(buf.at[pl.ds(first + k * rows, rows), :],
                                     w_s.at[pl.ds(k * shard + p * rows, rows), :], sem.at[p * NDEV + k])
               for p, (buf, first) in enumerate(parts) for k in range(NDEV)]
        for cp in cps:
            cp.start()
        for cp in cps:
            cp.wait()


def _mix_in(h1, gm, win, comm=None):
    def body(h_ref, g_ref, *rest):
        w_any, (u_ref, z_ref, w_s, sem) = rest[:len(win)], rest[len(win):]
        _load_in_proj([(b, first) for b, (_, first) in zip(w_any, win)], w_s, sem)
        xf = h_ref[...]
        r = lax.rsqrt(jnp.mean(xf * xf, axis=-1, keepdims=True) + EPS)
        ub = (xf * r * g_ref[...]).astype(BF)
        u_ref[...] = ub
        for j in range(NG):
            z_ref[j] = _nt(ub, w_s[j * D:(j + 1) * D, :]).astype(BF)

    row = lambda i: (i, 0)
    return _call(
        body, name="mix_in", grid=(T // TM,), args=[h1, gm] + [b for b, _ in win], comm=comm,
        in_specs=[pl.BlockSpec((TM, D), row), pl.BlockSpec((1, D), lambda i: (0, 0))] + [ANY] * len(win),
        out_shape=[jax.ShapeDtypeStruct((T, D), BF), jax.ShapeDtypeStruct((NG, T, D), BF)],
        out_specs=[pl.BlockSpec((TM, D), row), pl.BlockSpec((NG, TM, D), lambda i: (0, i, 0))],
        scratch_shapes=[pltpu.VMEM((NG * D, D), BF), pltpu.SemaphoreType.DMA((NDEV * len(win),))])


def _shift_up(w, b):
    return w if b == 0 else pltpu.roll(w, w.shape[0] - b, 0)


def _fold8(p):
    red = p[0:8, :]
    for i in range(1, p.shape[0] // 8):
        red = red + p[8 * i:8 * i + 8, :]
    return red


def _dft_constants():
    import numpy as np
    nh = NB // 2
    f, n = np.arange(nh)[:, None], np.arange(NB)[None, :]
    ang = 2.0 * np.pi / NB * f * n
    fc = np.cos(ang)
    fs = np.where(f == 0, (-1.0) ** n, np.sin(ang))
    scale = np.where(f == 0, 1.0, 2.0) / NB
    ic = (scale * np.cos(ang)).T
    isn = np.where(f == 0, (-1.0) ** n / NB, scale * np.sin(ang)).T
    d = (KA - 1 - np.arange(32))[None, :]
    valid = (np.arange(32) < KA)[None, :]
    angk = 2.0 * np.pi / NB * f * d
    kc = np.where(valid, np.cos(angk), 0.0)
    ks = np.where(valid, np.sin(angk), 0.0)
    k2 = np.where(valid, np.where(f == 0, (-1.0) ** d, np.cos(angk)), 0.0)
    rtc = np.where(valid, scale * np.cos(angk), 0.0).T
    rts = np.where(valid, np.where(f == 0, (-1.0) ** d / NB, scale * np.sin(angk)), 0.0).T

    def bf(a):
        return jnp.asarray(a, F32).astype(BF)

    def split(a):
        hi = bf(a)
        return hi, (jnp.asarray(a, F32) - hi.astype(F32)).astype(BF)

    return dict(fc=bf(fc), fs=bf(fs), ic_hi=bf(ic[HB:]), is_hi=bf(isn[HB:]), ic_lo=bf(ic[:HB]), is_lo=bf(isn[:HB]),
                kc=split(kc), ks=split(ks), k2=split(k2), rtc=split(rtc), rts=split(rts))


def _dot3(m_hi, m_lo, x):
    x_hi = x.astype(BF)
    x_lo = (x - x_hi.astype(F32)).astype(BF)
    return _nn(m_hi, x_hi) + _nn(m_hi, x_lo) + _nn(m_lo, x_hi)


def _whole(a):
    return pl.BlockSpec(a.shape, lambda c, t: (0,) * a.ndim)


def _filter_spectrum(cw_ref, tabs, hc, hs, h2):
    w32 = cw_ref[0:32, :]
    for (hi, lo), dst in zip(tabs, (hc, hs, h2)):
        dst[...] = _dot3(hi[...], lo[...], w32)


def _conv_fwd_dft(z, cw, bias, dft, comm=None):
    nt = T // TB
    hb = TB // HB

    def body(z_ref, zh_ref, cw_ref, b_ref, fc_ref, fs_ref, ic_ref, is_ref, kch, kcl, ksh, ksl, k2h, k2l,
             a1_ref, q_ref, aext, ppad, hc, hs, h2):
        first = pl.program_id(1) == 0
        f = lambda ref, j: ref[j].astype(F32)

        @pl.when(first)
        def _():
            _filter_spectrum(cw_ref, ((kch, kcl), (ksh, ksl), (k2h, k2l)), hc, hs, h2)

        aext[0:HB, :] = jnp.where(first, 0.0, f(zh_ref, 0) * _sig(f(zh_ref, 1))).astype(BF)
        aext[HB:, :] = (f(z_ref, 0) * _sig(f(z_ref, 1))).astype(BF)
        ppad[0:8, :] = jnp.where(first, 0.0, f(zh_ref, 3)[HB - 8:HB, :] * f(zh_ref, 4)[HB - 8:HB, :])
        ppad[8:, :] = f(z_ref, 3) * f(z_ref, 4)
        bias_row = b_ref[...]

        for j in range(TB // HB):
            xs = aext[j * HB:j * HB + NB, :]
            xa, xb = _nn(fc_ref[...], xs), _nn(fs_ref[...], xs)
            yc = (hc[...] * xa - hs[...] * xb).astype(BF)
            ys = (h2[...] * xb + hs[...] * xa).astype(BF)
            y = _nn(ic_ref[...], yc) + _nn(is_ref[...], ys)
            a1_ref[j * HB:(j + 1) * HB, :] = (y + bias_row).astype(BF)

        def chunk(r, carry):
            base = pl.multiple_of(r * CHB, CHB)
            pw = ppad[pl.ds(base, CHB + 8), :]
            v = (cw_ref[pl.ds(32, 1), :] * _shift_up(pw, 6)[0:CHB, :]
                 + cw_ref[pl.ds(33, 1), :] * _shift_up(pw, 7)[0:CHB, :]
                 + cw_ref[pl.ds(34, 1), :] * pw[8:8 + CHB, :])
            q_ref[pl.ds(base, CHB), :] = (z_ref[2, pl.ds(base, CHB), :].astype(F32) * v).astype(BF)
            return carry

        lax.fori_loop(0, TB // CHB, chunk, 0)

    blk = pl.BlockSpec((TB, CW), lambda c, t: (t, c))
    tabs = [dft["fc"], dft["fs"], dft["ic_hi"], dft["is_hi"], *dft["kc"], *dft["ks"], *dft["k2"]]
    return _call(
        body, name="conv_fwd", grid=(D // CW, nt), comm=comm, args=[z, z, cw, bias] + tabs,
        in_specs=[pl.BlockSpec((5, TB, CW), lambda c, t: (0, t, c)),
                  pl.BlockSpec((5, HB, CW), lambda c, t: (0, jnp.maximum(t * hb - 1, 0), c)),
                  pl.BlockSpec((40, CW), lambda c, t: (0, c)), pl.BlockSpec((1, CW), lambda c, t: (0, c))]
                 + [_whole(a) for a in tabs],
        out_shape=[jax.ShapeDtypeStruct((T, D), BF), jax.ShapeDtypeStruct((T, D), BF)], out_specs=[blk, blk],
        scratch_shapes=[pltpu.VMEM((TB + HB, CW), BF), pltpu.VMEM((TB + 8, CW), F32)]
                       + [pltpu.VMEM((NB // 2, CW), F32)] * 3)


def _conv_bwd_dft(z, da1, dq, dzg, cw, dft, comm=None):
    nt = T // TB
    hb = TB // HB
    last_h = T // HB - 1

    def body(z_ref, zp_ref, zn_ref, da1_ref, da1n_ref, dq_ref, dqn_ref, dzg_ref, cw_ref,
             fc_ref, fs_ref, ic_ref, is_ref, kch, kcl, ksh, ksl, k2h, k2l, rch, rcl, rsh, rsl,
             dz_ref, dwa_ref, dwb_ref, aext, dyext, ppad, dvpad, hc, hs, h2, rc, rs, nyq, acc_b):
        t = pl.program_id(1)
        first, last = t == 0, t == nt - 1
        f = lambda ref, j: ref[j].astype(F32)

        @pl.when(first)
        def _():
            _filter_spectrum(cw_ref, ((kch, kcl), (ksh, ksl), (k2h, k2l)), hc, hs, h2)
            rc[...] = jnp.zeros_like(rc)
            rs[...] = jnp.zeros_like(rs)
            nyq[...] = jnp.zeros_like(nyq)
            acc_b[...] = jnp.zeros_like(acc_b)

        aext[0:HB, :] = jnp.where(first, 0.0, f(zp_ref, 0) * _sig(f(zp_ref, 1))).astype(BF)
        aext[HB:, :] = (f(z_ref, 0) * _sig(f(z_ref, 1))).astype(BF)
        dyext[0:TB, :] = da1_ref[...]
        dyext[TB:, :] = jnp.where(last, 0.0, da1n_ref[...].astype(F32)).astype(BF)
        ppad[0:8, :] = jnp.where(first, 0.0, f(zp_ref, 3)[HB - 8:HB, :] * f(zp_ref, 4)[HB - 8:HB, :])
        ppad[8:, :] = f(z_ref, 3) * f(z_ref, 4)
        dvpad[0:TB, :] = dq_ref[...].astype(F32) * f(z_ref, 2)
        dvpad[TB:, :] = jnp.where(last, 0.0, dqn_ref[...].astype(F32)[0:8, :] * f(zn_ref, 2)[0:8, :])

        for j in range(TB // HB):
            rows = slice(j * HB, (j + 1) * HB)
            dys = dyext[j * HB:j * HB + NB, :]
            da, db = _nn(fc_ref[...], dys), _nn(fs_ref[...], dys)
            gc = (hc[...] * da + hs[...] * db).astype(BF)
            gs = (h2[...] * db - hs[...] * da).astype(BF)
            da0 = _nn(ic_ref[...], gc) + _nn(is_ref[...], gs)
            z0, z1 = z_ref[0, rows, :].astype(F32), z_ref[1, rows, :].astype(F32)
            s1 = _sig(z1)
            dz_ref[0, rows, :] = (da0 * s1).astype(BF)
            dz_ref[1, rows, :] = (da0 * z0 * (s1 * (1.0 - s1))).astype(BF)
            xs = aext[j * HB:j * HB + NB, :]
            xa, xb = _nn(fc_ref[...], xs), _nn(fs_ref[...], xs)
            dyb = dyext[rows, :]
            pa, pb = _nn(fc_ref[:, HB:NB], dyb), _nn(fs_ref[:, HB:NB], dyb)
            rc[...] += pa * xa + pb * xb
            rs[...] += pb * xa - pa * xb
            nyq[...] += pb[0:8, :] * xb[0:8, :]

        def chunk(r, carry):
            base = pl.multiple_of(r * CHB, CHB)
            rows = pl.ds(base, CHB)
            pw = ppad[pl.ds(base, CHB + 8), :]
            p6 = _shift_up(pw, 6)[0:CHB, :]
            p7 = _shift_up(pw, 7)[0:CHB, :]
            p8 = pw[8:8 + CHB, :]
            wb0, wb1, wb2 = cw_ref[pl.ds(32, 1), :], cw_ref[pl.ds(33, 1), :], cw_ref[pl.ds(34, 1), :]
            v = wb0 * p6 + wb1 * p7 + wb2 * p8
            dz_ref[2, rows, :] = (dq_ref[rows, :].astype(F32) * v).astype(BF)
            dvw = dvpad[pl.ds(base, CHB + 8), :]
            dvc = dvw[0:CHB, :]
            dp = wb2 * dvc + wb1 * _shift_up(dvw, 1)[0:CHB, :] + wb0 * _shift_up(dvw, 2)[0:CHB, :]
            dz_ref[3, rows, :] = (dp * z_ref[4, rows, :].astype(F32)).astype(BF)
            dz_ref[4, rows, :] = (dp * z_ref[3, rows, :].astype(F32)).astype(BF)
            acc_b[0:8, :] += _fold8(dvc * p6)
            acc_b[8:16, :] += _fold8(dvc * p7)
            acc_b[16:24, :] += _fold8(dvc * p8)
            dz_ref[5, rows, :] = dzg_ref[0, rows, :]
            dz_ref[6, rows, :] = dzg_ref[1, rows, :]
            return carry

        lax.fori_loop(0, TB // CHB, chunk, 0)

        @pl.when(last)
        def _():
            row0 = lax.broadcasted_iota(jnp.int32, (NB // 2, CW), 0) == 0
            ny = jnp.broadcast_to(nyq[0:1, :], (NB // 2, CW))
            rcv = jnp.where(row0, rc[...] - ny, rc[...])
            rsv = jnp.where(row0, ny, rs[...])
            dwa_ref[...] = _dot3(rch[...], rcl[...], rcv) + _dot3(rsh[...], rsl[...], rsv)
            for k in range(KB):
                dwb_ref[k:k + 1, :] = jnp.sum(acc_b[8 * k:8 * k + 8, :], axis=0, keepdims=True)
            dwb_ref[KB:8, :] = jnp.zeros((8 - KB, CW), F32)

    blk = lambda c, t: (t, c)
    nxt = lambda c, t: (jnp.minimum((t + 1) * hb, last_h), c)
    tabs = [dft["fc"], dft["fs"], dft["ic_lo"], dft["is_lo"], *dft["kc"], *dft["ks"], *dft["k2"], *dft["rtc"], *dft["rts"]]
    return _call(
        body, name="conv_bwd", grid=(D // CW, nt), comm=comm, args=[z, z, z, da1, da1, dq, dq, dzg, cw] + tabs,
        in_specs=[pl.BlockSpec((5, TB, CW), lambda c, t: (0, t, c)),
                  pl.BlockSpec((5, HB, CW), lambda c, t: (0, jnp.maximum(t * hb - 1, 0), c)),
                  pl.BlockSpec((5, HB, CW), lambda c, t: (0, jnp.minimum((t + 1) * hb, last_h), c)),
                  pl.BlockSpec((TB, CW), blk), pl.BlockSpec((HB, CW), nxt),
                  pl.BlockSpec((TB, CW), blk), pl.BlockSpec((HB, CW), nxt),
                  pl.BlockSpec((2, TB, CW), lambda c, t: (0, t, c)),
                  pl.BlockSpec((40, CW), lambda c, t: (0, c))]
                 + [_whole(a) for a in tabs],
        out_shape=[jax.ShapeDtypeStruct((NG, T, D), BF), jax.ShapeDtypeStruct((32, D), F32),
                   jax.ShapeDtypeStruct((8, D), F32)],
        out_specs=[pl.BlockSpec((NG, TB, CW), lambda c, t: (0, t, c)),
                   pl.BlockSpec((32, CW), lambda c, t: (0, c)), pl.BlockSpec((8, CW), lambda c, t: (0, c))],
        scratch_shapes=[pltpu.VMEM((TB + HB, CW), BF), pltpu.VMEM((TB + HB, CW), BF),
                        pltpu.VMEM((TB + 8, CW), F32), pltpu.VMEM((TB + 8, CW), F32)]
                       + [pltpu.VMEM((NB // 2, CW), F32)] * 5 + [pltpu.VMEM((8, CW), F32), pltpu.VMEM((24, CW), F32)])


def _layernorm_silu(a1, lng, lnb):
    mu = jnp.mean(a1, axis=-1, keepdims=True)
    xc = a1 - mu
    rs = lax.rsqrt(jnp.mean(xc * xc, axis=-1, keepdims=True) + EPS)
    xh = xc * rs
    a2 = xh * lng + lnb
    sg = _sig(a2)
    return xh, rs, a2, sg


def _square_specs(blocks):
    return [pl.BlockSpec((D, D), lambda i, b=b: (b, 0)) for b in blocks]


def _mix_out(a1, q, z, h1, lng, lnb, wsq, comm=None):
    def body(a1_ref, q_ref, ga_ref, gb_ref, h_ref, lng_ref, lnb_ref, wa_ref, wb_ref, wo_ref, h2_ref, ya_ref, yb_ref):
        _, _, a2, sg = _layernorm_silu(a1_ref[...].astype(F32), lng_ref[...], lnb_ref[...])
        ya = _nn((a2 * sg).astype(BF), wa_ref[...])
        yb = _nn(q_ref[...], wb_ref[...])
        ya_ref[...] = ya.astype(BF)
        yb_ref[...] = yb.astype(BF)
        m = _sig(ga_ref[...].astype(F32)) * ya + _sig(gb_ref[...].astype(F32)) * yb
        h2_ref[...] = h_ref[...] + _nn(m.astype(BF), wo_ref[...])

    row = lambda i: (i, 0)
    vec = pl.BlockSpec((1, D), lambda i: (0, 0))
    return _call(
        body, name="mix_out", grid=(T // TM,), args=[a1, q, z, z, h1, lng, lnb, wsq, wsq, wsq], comm=comm,
        in_specs=[pl.BlockSpec((TM, D), row), pl.BlockSpec((TM, D), row),
                  pl.BlockSpec((None, TM, D), lambda i: (5, i, 0)), pl.BlockSpec((None, TM, D), lambda i: (6, i, 0)),
                  pl.BlockSpec((TM, D), row), vec, vec] + _square_specs((0, 1, 2)),
        out_shape=[jax.ShapeDtypeStruct((T, D), F32), jax.ShapeDtypeStruct((T, D), BF), jax.ShapeDtypeStruct((T, D), BF)],
        out_specs=[pl.BlockSpec((TM, D), row)] * 3)


def _rmsnorm_bwd(xf, g, dn):
    r = lax.rsqrt(jnp.mean(xf * xf, axis=-1, keepdims=True) + EPS)
    xr = xf * r
    gdn = dn * g
    dx = r * gdn - xr * (r * jnp.mean(gdn * xr, axis=-1, keepdims=True))
    return dx, jnp.sum(dn * xr, axis=0, keepdims=True)


def _ffn_bwd_hidden(dh, gg, uu, wbuf, off, name, comm=None):
    nf, nt = F // FC, T // TM

    def body(dh_ref, gg_hbm, uu_hbm, b0, dgu_ref, wd_s, sem, ring_g, ring_u, ring_sem):
        i = pl.program_id(0)

        def fetch(t):
            return [pltpu.make_async_copy(src.at[pl.ds(t * TM, TM), :], ring.at[t % 3], ring_sem.at[k, t % 3])
                    for k, (src, ring) in enumerate(((gg_hbm, ring_g), (uu_hbm, ring_u)))]

        @pl.when(i == 0)
        def _():
            for t in range(min(2, nt)):
                for cp in fetch(t):
                    cp.start()

        @pl.when(i + 2 < nt)
        def _():
            for cp in fetch(i + 2):
                cp.start()

        _load_ffn_weights((b0,), (off,), (wd_s,), sem)
        for cp in fetch(i):
            cp.wait()
        gg_ref, uu_ref = ring_g.at[i % 3], ring_u.at[i % 3]
        dhb = dh_ref[...]
        for c in range(nf):
            sl = slice(c * FC, (c + 1) * FC)
            da = _nt(dhb, wd_s[sl, :]).astype(BF)
            gb, ub = gg_ref[:, sl], uu_ref[:, sl]
            sg = _sig(gb)
            dgu_ref[0, :, sl] = (da * ub) * (sg * (1.0 + gb * (1.0 - sg)))
            dgu_ref[0, :, F + c * FC:F + (c + 1) * FC] = da * (gb * sg)

    row = lambda i: (i, 0)
    return _call(
        body, name=name, grid=(nt,), args=[dh, gg, uu, wbuf], comm=comm,
        in_specs=[pl.BlockSpec((TM, D), row), ANY, ANY, ANY],
        out_shape=[jax.ShapeDtypeStruct((1, T, 2 * F), BF)],
        out_specs=[pl.BlockSpec((1, TM, 2 * F), lambda i: (0, i, 0))],
        scratch_shapes=[pltpu.VMEM((F, D), BF), pltpu.SemaphoreType.DMA((1,)), pltpu.VMEM((3, TM, F), BF),
                        pltpu.VMEM((3, TM, F), BF), pltpu.SemaphoreType.DMA((2, 3))])


def _ffn_bwd_input(dgu, dh, x, g, wbufs, offs, name, comm=None, after=None):
    nt = T // TM

    def body(dgu_hbm, dh_ref, x_ref, g_ref, b0, b1, dx_ref, s_ref, w_s, sem, ring, ring_sem):
        i = pl.program_id(0)

        def fetch(t):
            return pltpu.make_async_copy(dgu_hbm.at[0, pl.ds(t * TM, TM), :], ring.at[t % 3], ring_sem.at[t % 3])

        @pl.when(i == 0)
        def _():
            for t in range(min(2, nt)):
                fetch(t).start()

        @pl.when(i + 2 < nt)
        def _():
            fetch(i + 2).start()

        _load_ffn_weights((b0, b1), offs, (w_s.at[pl.ds(0, F), :], w_s.at[pl.ds(F, F), :]), sem)

        @pl.when(i == 0)
        def _():
            s_ref[...] = jnp.zeros_like(s_ref)

        fetch(i).wait()
        dn = _nn(ring[i % 3], w_s[...])
        dxn, dg = _rmsnorm_bwd(x_ref[...], g_ref[...], dn)
        dx_ref[...] = dh_ref[...] + dxn
        s_ref[0:1, :] += dg

    row = lambda i: (i, 0)
    return _call(
        body, name=name, grid=(T // TM,), args=[dgu, dh, x, g, *wbufs], comm=comm, after=after,
        in_specs=[ANY, pl.BlockSpec((TM, D), row),
                  pl.BlockSpec((TM, D), row), pl.BlockSpec((1, D), lambda i: (0, 0)), ANY, ANY],
        out_shape=[jax.ShapeDtypeStruct((T, D), F32), jax.ShapeDtypeStruct((8, D), F32)],
        out_specs=[pl.BlockSpec((TM, D), row), pl.BlockSpec((8, D), lambda i: (0, 0))],
        scratch_shapes=[pltpu.VMEM((2 * F, D), BF), pltpu.SemaphoreType.DMA((2,)), pltpu.VMEM((3, TM, 2 * F), BF),
                        pltpu.SemaphoreType.DMA((3,))])


def _tn_matmul(lhs, rhs, tr, name, comm=None):
    ng, _, cdim = lhs.shape
    nc, nk = cdim // tr, T // TK
    if rhs.ndim == 2:
        r_spec = pl.BlockSpec((TK, D), lambda g, c, k: (k, 0))
    else:
        r_spec = pl.BlockSpec((None, TK, D), lambda g, c, k: (g, k, 0))

    def body(l_ref, r_ref, o_ref, acc):
        k = pl.program_id(2)

        @pl.when(k == 0)
        def _():
            acc[...] = jnp.zeros_like(acc)

        acc[...] += _tn(l_ref[...], r_ref[...])

        @pl.when(k == nk - 1)
        def _():
            o_ref[...] = acc[...].astype(BF)

    return _call(
        body, name=name, grid=(ng, nc, nk), args=[lhs, rhs], comm=comm,
        in_specs=[pl.BlockSpec((None, TK, tr), lambda g, c, k: (g, k, c)), r_spec],
        out_shape=[jax.ShapeDtypeStruct((ng * cdim, D), BF)],
        out_specs=[pl.BlockSpec((tr, D), lambda g, c, k: (g * nc + c, 0))],
        scratch_shapes=[pltpu.VMEM((tr, D), F32)])


def _mix_out_bwd(dh2, ya, yb, z, a1, q, lng, lnb, wsq, comm=None):
    def body(dh_ref, ya_ref, yb_ref, ga_ref, gb_ref, a1_ref, q_ref, lng_ref, lnb_ref, wa_ref, wb_ref, wo_ref,
             dzg_ref, da1_ref, dq_ref, l_ref, r_ref, s_ref):
        @pl.when(pl.program_id(0) == 0)
        def _():
            s_ref[...] = jnp.zeros_like(s_ref)

        dhb = dh_ref[...].astype(BF)
        dm = _nt(dhb, wo_ref[...]).astype(BF)
        ya, yb = ya_ref[...], yb_ref[...]
        sa, sb = _sig(ga_ref[...]), _sig(gb_ref[...])
        l_ref[0] = sa * ya + sb * yb
        l_ref[2] = q_ref[...]
        dzg_ref[0] = (dm * ya) * (sa * (1.0 - sa))
        dzg_ref[1] = (dm * yb) * (sb * (1.0 - sb))
        dya = dm * sa
        dyb = dm * sb
        r_ref[0] = dhb
        r_ref[1] = dya
        r_ref[2] = dyb
        dq_ref[...] = _nt(dyb, wb_ref[...]).astype(BF)
        da3 = _nt(dya, wa_ref[...])
        lng = lng_ref[...]
        xh, rs, a2, sg = _layernorm_silu(a1_ref[...].astype(F32), lng, lnb_ref[...])
        l_ref[1] = (a2 * sg).astype(BF)
        da2 = da3 * (sg * (1.0 + a2 * (1.0 - sg)))
        s_ref[0:1, :] += jnp.sum(da2 * xh, axis=0, keepdims=True)
        s_ref[1:2, :] += jnp.sum(da2, axis=0, keepdims=True)
        dxh = da2 * lng
        da1 = rs * (dxh - jnp.mean(dxh, axis=-1, keepdims=True) - xh * jnp.mean(dxh * xh, axis=-1, keepdims=True))
        da1_ref[...] = da1.astype(BF)
        s_ref[2:3, :] += jnp.sum(da1, axis=0, keepdims=True)

    row = lambda i: (i, 0)
    row3 = lambda i: (0, i, 0)
    vec = pl.BlockSpec((1, D), lambda i: (0, 0))
    return _call(
        body, name="mix_out_bwd", grid=(T // TM,), args=[dh2, ya, yb, z, z, a1, q, lng, lnb, wsq, wsq, wsq], comm=comm,
        in_specs=[pl.BlockSpec((TM, D), row), pl.BlockSpec((TM, D), row), pl.BlockSpec((TM, D), row),
                  pl.BlockSpec((None, TM, D), lambda i: (5, i, 0)), pl.BlockSpec((None, TM, D), lambda i: (6, i, 0)),
                  pl.BlockSpec((TM, D), row), pl.BlockSpec((TM, D), row), vec, vec] + _square_specs((0, 1, 2)),
        out_shape=[jax.ShapeDtypeStruct((2, T, D), BF), jax.ShapeDtypeStruct((T, D), BF),
                   jax.ShapeDtypeStruct((T, D), BF), jax.ShapeDtypeStruct((3, T, D), BF),
                   jax.ShapeDtypeStruct((3, T, D), BF), jax.ShapeDtypeStruct((8, D), F32)],
        out_specs=[pl.BlockSpec((2, TM, D), row3), pl.BlockSpec((TM, D), row), pl.BlockSpec((TM, D), row),
                   pl.BlockSpec((3, TM, D), row3), pl.BlockSpec((3, TM, D), row3), pl.BlockSpec((8, D), lambda i: (0, 0))])


def _mix_in_bwd(dz, dh2, h1, gm, win, comm=None):
    def body(dz_ref, dh_ref, h_ref, g_ref, *rest):
        w_any, (o_ref, ob_ref, s_ref, w_s, sem) = rest[:len(win)], rest[len(win):]
        _load_in_proj([(b, first) for b, (_, first) in zip(w_any, win)], w_s, sem)

        @pl.when(pl.program_id(0) == 0)
        def _():
            s_ref[...] = jnp.zeros_like(s_ref)

        du = _nn(dz_ref[0], w_s[0:D, :])
        for j in range(1, NG):
            du = du + _nn(dz_ref[j], w_s[j * D:(j + 1) * D, :])
        dx, dg = _rmsnorm_bwd(h_ref[...], g_ref[...], du)
        dh1 = dh_ref[...] + dx
        o_ref[...] = dh1
        ob_ref[...] = (0.5 * dh1).astype(BF)
        s_ref[0:1, :] += dg

    row = lambda i: (i, 0)
    return _call(
        body, name="mix_in_bwd", grid=(T // TM,), args=[dz, dh2, h1, gm] + [b for b, _ in win], comm=comm,
        in_specs=[pl.BlockSpec((NG, TM, D), lambda i: (0, i, 0)), pl.BlockSpec((TM, D), row),
                  pl.BlockSpec((TM, D), row), pl.BlockSpec((1, D), lambda i: (0, 0))] + [ANY] * len(win),
        out_shape=[jax.ShapeDtypeStruct((T, D), F32), jax.ShapeDtypeStruct((T, D), BF), jax.ShapeDtypeStruct((8, D), F32)],
        out_specs=[pl.BlockSpec((TM, D), row), pl.BlockSpec((TM, D), row), pl.BlockSpec((8, D), lambda i: (0, 0))],
        scratch_shapes=[pltpu.VMEM((NG * D, D), BF), pltpu.SemaphoreType.DMA((NDEV * len(win),))])


def _row_tile(n, want, mult):
    for t in range(min(want, n), 0, -1):
        if n % t == 0 and t % mult == 0:
            return t
    return n


def _pack_small(s_ffn1, s_in, s_mix, s_ffn2, s_final, dwa, dwb):
    def body(f1, mi, mo, f2, fl, wa_ref, wb_ref, v_ref, k_ref):
        for dst, (ref, row) in enumerate(((f1, 0), (mi, 0), (mo, 0), (mo, 1), (mo, 2), (f2, 0), (fl, 0), (fl, 1))):
            v_ref[dst:dst + 1, :] = ref[row:row + 1, :]
        for k in range(NDEV):
            k_ref[k, 0:32, :] = wa_ref[:, k * LANE:(k + 1) * LANE]
            k_ref[k, 32:40, :] = wb_ref[:, k * LANE:(k + 1) * LANE]

    return pl.pallas_call(
        body, name="pack_small",
        out_shape=(jax.ShapeDtypeStruct((8, D), F32), jax.ShapeDtypeStruct((NDEV, 40, LANE), F32)),
    )(s_ffn1, s_in, s_mix, s_ffn2, s_final, dwa, dwb)


def _adam_update(g, w, m, v):
    m2 = ADAM_B1 * m + (1.0 - ADAM_B1) * g
    v2 = ADAM_B2 * v + (1.0 - ADAM_B2) * (g * g)
    c1 = 1.0 - ADAM_B1 ** ADAM_STEP
    c2 = 1.0 - ADAM_B2 ** ADAM_STEP
    return -ADAM_LR * ((m2 / c1) / (jnp.sqrt(v2 / c2) + ADAM_EPS) + ADAM_WD * w), m2, v2


def _adam_small(vecs, convs, vec_params, tap_params):
    nv, nt = len(vec_params), len(tap_params)

    def body(*refs):
        v_ref, k_ref = refs[:2]
        p_refs = refs[2:2 + 3 * (nv + nt)]
        l_ref = refs[2 + 3 * (nv + nt)]
        o_refs = refs[3 + 3 * (nv + nt):]
        s, c = v_ref[0], k_ref[0]
        for k in range(1, NDEV):
            s = s + v_ref[k]
            c = c + k_ref[k]
        l_ref[...] = jnp.sum(s[7:8, :], axis=-1, keepdims=True)
        for i in range(nv):
            w_ref, m_ref, u_ref = p_refs[3 * i: 3 * i + 3]
            g_ref, d_ref, m2_ref, u2_ref = o_refs[4 * i: 4 * i + 4]
            g = s[i:i + 1, :]
            g_ref[...] = g
            d_ref[...], m2_ref[...], u2_ref[...] = _adam_update(g, w_ref[...], m_ref[...], u_ref[...])
        for i in range(nt):
            w_ref, m_ref, u_ref = p_refs[3 * (nv + i): 3 * (nv + i) + 3]
            g_ref, d_ref, m2_ref, u2_ref = o_refs[4 * (nv + i): 4 * (nv + i) + 4]
            first = tap_params[i][0]
            for k in range(w_ref.shape[0]):
                g = c[first + k:first + k + 1, :]
                g_ref[k] = g
                d_ref[k], m2_ref[k], u2_ref[k] = _adam_update(g, w_ref[k], m_ref[k], u_ref[k])

    params = [a for p in vec_params for a in p] + [a for p in tap_params for a in p[1:]]
    out_shape = [jax.ShapeDtypeStruct((1, 1), F32)]
    for p in list(vec_params) + [p[1:] for p in tap_params]:
        out_shape += [jax.ShapeDtypeStruct(p[0].shape, F32)] * 4
    outs = pl.pallas_call(body, name="adam_small", out_shape=tuple(out_shape))(vecs, convs, *params)
    groups = [tuple(outs[1 + 4 * i: 5 + 4 * i]) for i in range(nv + nt)]
    return outs[0], groups[:nv], groups[nv:]


def _adam_in_proj(parts, w, m, v, after):
    rows = w.shape[1]
    tr = _row_tile(D, 256, LANE)

    def body(*refs):
        p_refs = refs[:len(parts)]
        w_ref, m_ref, v_ref, g_ref, d_ref, m2_ref, v2_ref = refs[len(parts):]
        sums = []
        for p in p_refs:
            s = p[0].astype(F32)
            for k in range(1, p.shape[0]):
                s = s + p[k].astype(F32)
            sums.append(s)
        g = jnp.concatenate(sums, axis=0).T
        g_ref[...] = g
        d_ref[...], m2_ref[...], v2_ref[...] = _adam_update(g, w_ref[...], m_ref[...], v_ref[...])

    spec = pl.BlockSpec((tr, rows), lambda i: (i, 0))
    return _call(body, name="adam_in", grid=(D // tr,), args=list(parts) + [w, m, v], after=after,
                 in_specs=[pl.BlockSpec((p.shape[0], p.shape[1], tr), lambda i: (0, 0, i)) for p in parts] + [spec] * 3,
                 out_shape=[jax.ShapeDtypeStruct((D, rows), F32)] * 4, out_specs=[spec] * 4)


def _adam(gs, ws, ms, vs, name, after):
    n = len(gs)
    rows, cols = ws[0].shape
    tr = _row_tile(rows, min(256, rows // 2), 16)

    def body(*refs):
        for i in range(n):
            g_in, w, m, v = refs[4 * i], refs[4 * i + 1][...], refs[4 * i + 2][...], refs[4 * i + 3][...]
            g_ref, d_ref, m_ref, v_ref = refs[4 * n + 4 * i: 4 * n + 4 * i + 4]
            g = g_in[0].astype(F32)
            for k in range(1, g_in.shape[0]):
                g = g + g_in[k].astype(F32)
            g_ref[...] = g
            d_ref[...], m_ref[...], v_ref[...] = _adam_update(g, w, m, v)

    spec = pl.BlockSpec((tr, cols), lambda i: (i, 0))
    args, in_specs = [], []
    for i in range(n):
        slots, first = gs[i]
        args += [slots, ws[i], ms[i], vs[i]]
        in_specs += [pl.BlockSpec((slots.shape[0], tr, cols), lambda i, b=first // tr: (0, b + i, 0))] + [spec] * 3
    outs = _call(body, name=name, grid=(rows // tr,), args=args, in_specs=in_specs, after=after,
                 out_shape=[jax.ShapeDtypeStruct((rows, cols), F32)] * (4 * n), out_specs=[spec] * (4 * n))
    return [tuple(outs[4 * i: 4 * i + 4]) for i in range(n)]


def kernel(x, ffn1_norm, ffn1_w_gate, ffn1_w_up, ffn1_w_down, mix_norm, w_in, a_dw_w, a_dw_b, a_ln_g, a_ln_b, a_w_out, b_conv_w, b_w_out, w_o, ffn2_norm, ffn2_w_gate, ffn2_w_up, ffn2_w_down, final_norm, loss_target, m_ffn1_norm, m_ffn1_w_gate, m_ffn1_w_up, m_ffn1_w_down, m_mix_norm, m_w_in, m_a_dw_w, m_a_dw_b, m_a_ln_g, m_a_ln_b, m_a_w_out, m_b_conv_w, m_b_w_out, m_w_o, m_ffn2_norm, m_ffn2_w_gate, m_ffn2_w_up, m_ffn2_w_down, m_final_norm, v_ffn1_norm, v_ffn1_w_gate, v_ffn1_w_up, v_ffn1_w_down, v_mix_norm, v_w_in, v_a_dw_w, v_a_dw_b, v_a_ln_g, v_a_ln_b, v_a_w_out, v_b_conv_w, v_b_w_out, v_w_o, v_ffn2_norm, v_ffn2_w_gate, v_ffn2_w_up, v_ffn2_w_down, v_final_norm):
    names = ("ffn1_norm", "ffn1_w_gate", "ffn1_w_up", "ffn1_w_down", "mix_norm", "w_in", "a_dw_w", "a_dw_b",
             "a_ln_g", "a_ln_b", "a_w_out", "b_conv_w", "b_w_out", "w_o", "ffn2_norm", "ffn2_w_gate", "ffn2_w_up",
             "ffn2_w_down", "final_norm")
    w = dict(ffn1_norm=ffn1_norm, ffn1_w_gate=ffn1_w_gate, ffn1_w_up=ffn1_w_up, ffn1_w_down=ffn1_w_down,
             mix_norm=mix_norm, w_in=w_in, a_dw_w=a_dw_w, a_dw_b=a_dw_b, a_ln_g=a_ln_g, a_ln_b=a_ln_b,
             a_w_out=a_w_out, b_conv_w=b_conv_w, b_w_out=b_w_out, w_o=w_o, ffn2_norm=ffn2_norm,
             ffn2_w_gate=ffn2_w_gate, ffn2_w_up=ffn2_w_up, ffn2_w_down=ffn2_w_down, final_norm=final_norm)
    m = dict(ffn1_norm=m_ffn1_norm, ffn1_w_gate=m_ffn1_w_gate, ffn1_w_up=m_ffn1_w_up, ffn1_w_down=m_ffn1_w_down,
             mix_norm=m_mix_norm, w_in=m_w_in, a_dw_w=m_a_dw_w, a_dw_b=m_a_dw_b, a_ln_g=m_a_ln_g, a_ln_b=m_a_ln_b,
             a_w_out=m_a_w_out, b_conv_w=m_b_conv_w, b_w_out=m_b_w_out, w_o=m_w_o, ffn2_norm=m_ffn2_norm,
             ffn2_w_gate=m_ffn2_w_gate, ffn2_w_up=m_ffn2_w_up, ffn2_w_down=m_ffn2_w_down, final_norm=m_final_norm)
    v = dict(ffn1_norm=v_ffn1_norm, ffn1_w_gate=v_ffn1_w_gate, ffn1_w_up=v_ffn1_w_up, ffn1_w_down=v_ffn1_w_down,
             mix_norm=v_mix_norm, w_in=v_w_in, a_dw_w=v_a_dw_w, a_dw_b=v_a_dw_b, a_ln_g=v_a_ln_g, a_ln_b=v_a_ln_b,
             a_w_out=v_a_w_out, b_conv_w=v_b_conv_w, b_w_out=v_b_w_out, w_o=v_w_o, ffn2_norm=v_ffn2_norm,
             ffn2_w_gate=v_ffn2_w_gate, ffn2_w_up=v_ffn2_w_up, ffn2_w_down=v_ffn2_w_down, final_norm=v_final_norm)
    flat = _pack_weights(dict(wg1=ffn1_w_gate[0].T, wu1=ffn1_w_up[0].T, wd1=ffn1_w_down[0], wg2=ffn2_w_gate[0].T,
                              wu2=ffn2_w_up[0].T, wd2=ffn2_w_down[0], win=w_in[0], wa=a_w_out[0], wb=b_w_out[0],
                              wo=w_o[0]))
    cw_shard = jnp.concatenate([a_dw_w[0], jnp.zeros((1, LANE), F32), b_conv_w[0], jnp.zeros((5, LANE), F32)], axis=0)

    x2, tgt = x[0], loss_target[0]
    st_a, st_b, st_b2 = ("wg1", "wu1"), ("wd1", "win/0/2"), ("win/1/2",)
    st_c, st_d, st_e = ("wa", "wb", "wo", "wg2"), ("wu2",), ("wd2",)

    buf_a, cw = _run_comm(_join(_ag_comm(st_a, flat), _direct_comm(cw_shard, False)), "ag_ffn1")
    n1, gg1, uu1, act1, buf_b = _ffn_gate_up(x2, ffn1_norm, (buf_a, buf_a), (0, F), "ffn1_gate_up", _ag_comm(st_b, flat))
    h1, buf_b2 = _ffn_down(x2, act1, buf_b, 0, "ffn1_down", _ag_comm(st_b2, flat))
    win = ((buf_b, F), (buf_b2, 0))
    u, z, buf_c = _mix_in(h1, mix_norm, win, _ag_comm(st_c, flat))
    dft = _dft_constants()
    cw = jnp.transpose(cw, (1, 0, 2)).reshape(40, D)
    a1, q, buf_d = _conv_fwd_dft(z, cw, a_dw_b, dft, _ag_comm(st_d, flat))
    h2, ya, yb, buf_e = _mix_out(a1, q, z, h1, a_ln_g, a_ln_b, buf_c, _ag_comm(st_e, flat))
    ffn2_bufs, ffn2_offs = (buf_c, buf_d, buf_e), (3 * D, 0, 0)
    dh3, dhb3, s_final, n2, gg2, uu2, act2 = _ffn_fwd(h2, ffn2_norm, ffn2_bufs, ffn2_offs, "ffn2_fwd",
                                          final=(final_norm.reshape(1, D), tgt))

    tr_f = F // 2 if (F // 2) % LANE == 0 else F
    def pair(stage, src):
        return _rs_pair_comm(stage, src)

    def chip(stage, src, pair_buf, tag):
        return _rs_chip_comm(_pair_add(stage, src, pair_buf, "pair_add_" + tag))

    (dgu2,) = _ffn_bwd_hidden(dhb3, gg2, uu2, buf_e, 0, "ffn2_bwd_h")
    (gu2,) = _tn_matmul(dgu2, n2, tr_f, "dw_gu2")
    s2a, src2a = ("wg2", "wu2"), dict(wg2=(gu2, 0), wu2=(gu2, F))
    (gd2,) = _tn_matmul(act2, dhb3, tr_f, "dw_d2")
    s2b, src2b = ("wd2",), dict(wd2=(gd2, 0))
    dh2, s_ffn2, pair2a, pair2b = _ffn_bwd_input(dgu2, dh3, h2, ffn2_norm, (buf_c, buf_d), (3 * D, 0), "ffn2_bwd_x",
                                                 _join(pair(s2a, src2a), pair(s2b, src2b)))
    dzg, da1, dq, lsq, rsq, s_mix, recv2b = _mix_out_bwd(dh2, ya, yb, z, a1, q, a_ln_g, a_ln_b, buf_c,
                                                          chip(s2b, src2b, pair2b, "2b"))
    (gsq,) = _tn_matmul(lsq, rsq, D, "dw_square")
    ssq, srcsq = ("wa", "wb", "wo"), dict(wa=(gsq, D), wb=(gsq, 2 * D), wo=(gsq, 0))
    dz, dwa, dwb, recv2a, pairsq = _conv_bwd_dft(z, da1, dq, dzg, cw, dft,
                                                 _join(chip(s2a, src2a, pair2a, "2a"), pair(ssq, srcsq)))
    gin, recvsq = _tn_matmul(dz, u, D, "dw_in", chip(ssq, srcsq, pairsq, "sq"))
    sin_a, sin_b, srcin = ("win/0/2",), ("win/1/2",), {"win/0/2": (gin, 0), "win/1/2": (gin, 0)}
    dh1, dhb1, s_in, pairin_a, pairin_b = _mix_in_bwd(dz, dh2, h1, mix_norm, win,
                                                _join(pair(sin_a, srcin), pair(sin_b, srcin)))
    dgu1, recvin_a = _ffn_bwd_hidden(dhb1, gg1, uu1, buf_b, 0, "ffn1_bwd_h",
                                           chip(sin_a, srcin, pairin_a, "in_a"))
    gu1, recvin_b = _tn_matmul(dgu1, n1, tr_f, "dw_gu1", chip(sin_b, srcin, pairin_b, "in_b"))
    s1a, src1a = ("wg1", "wu1"), dict(wg1=(gu1, 0), wu1=(gu1, F))
    gd1, pair1a = _tn_matmul(act1, dhb1, tr_f, "dw_d1", pair(s1a, src1a))
    s1b, src1b = ("wd1",), dict(wd1=(gd1, 0))
    xchg1 = _join(chip(s1a, src1a, pair1a, "1a"), pair(s1b, src1b))
    xchg1_sems, xchg1_bufs, token = _comm_start(xchg1, "xchg_ffn1_start")
    dx, s_ffn1 = _ffn_bwd_input(dgu1, dh1, x2, ffn1_norm, (buf_a, buf_a), (0, F), "ffn1_bwd_x", after=token)
    (_, gd1), (recv1a, pair1b) = _comm_wait(xchg1, "xchg_ffn1_wait", xchg1_sems, xchg1_bufs, s_ffn1)
    src1b = dict(wd1=(gd1, 0))

    vec8, convk = _pack_small(s_ffn1, s_in, s_mix, s_ffn2, s_final, dwa, dwb)
    tail = _join(chip(s1b, src1b, pair1b, "1b"), _join(_direct_comm(vec8, False), _direct_comm(convk, True)))
    tail_sems, tail_bufs, token = _comm_start(tail, "xchg_tail_start")

    fs = F // NDEV
    g = dict(ffn1_w_gate=(recv1a, 0), ffn1_w_up=(recv1a, fs), ffn2_w_gate=(recv2a, 0), ffn2_w_up=(recv2a, fs),
             ffn2_w_down=(recv2b, 0), a_w_out=(recvsq, 0), b_w_out=(recvsq, D // NDEV), w_o=(recvsq, 2 * (D // NDEV)))
    grad, upd = {}, {}

    def run(group, name, after, as2d=lambda a: a[0], back=lambda a, n: a.reshape(w[n].shape)):
        res = _adam([g[n] for n in group], [as2d(w[n]) for n in group], [as2d(m[n]) for n in group],
                    [as2d(v[n]) for n in group], name, after)
        for n, r in zip(group, res):
            grad[n], upd[n] = back(r[0], n), tuple(back(a, n) for a in r[1:])
        return res[0][0]

    done = run(("ffn1_w_gate", "ffn1_w_up", "ffn2_w_gate", "ffn2_w_up"), "adam_gate_up", token,
               as2d=lambda a: a[0].T, back=lambda a, n: a.T[None])
    r_in = _adam_in_proj([recvin_a, recvin_b], w_in[0], m_w_in[0], v_w_in[0], done)
    grad["w_in"], upd["w_in"] = r_in[0][None], tuple(a[None] for a in r_in[1:])
    done = run(("a_w_out", "b_w_out", "w_o"), "adam_square", r_in[0])
    _, (recv1b, vec_all, conv_all) = _comm_wait(tail, "xchg_tail_wait", tail_sems, tail_bufs, done)
    g["ffn1_w_down"] = (recv1b, 0)
    run(("ffn1_w_down", "ffn2_w_down"), "adam_down", done)
    vec_names = ("ffn1_norm", "mix_norm", "a_ln_g", "a_ln_b", "a_dw_b", "ffn2_norm", "final_norm")
    tap_names, tap_rows = ("a_dw_w", "b_conv_w"), (0, 32)
    taps = lambda a: jnp.transpose(a, (1, 0, 2))
    loss, vec_res, tap_res = _adam_small(
        vec_all, conv_all, [tuple(t[n].reshape(1, D) for t in (w, m, v)) for n in vec_names],
        [(r,) + tuple(taps(t[n]) for t in (w, m, v)) for n, r in zip(tap_names, tap_rows)])
    for n, r in zip(vec_names, vec_res):
        grad[n], upd[n] = r[0].reshape(w[n].shape), tuple(a.reshape(w[n].shape) for a in r[1:])
    for n, r in zip(tap_names, tap_res):
        grad[n], upd[n] = taps(r[0]), tuple(taps(a) for a in r[1:])

    return (loss.reshape(()), dx.reshape(x.shape), *[grad[n] for n in names], *[upd[n][0] for n in names],
            *[upd[n][1] for n in names], *[upd[n][2] for n in names])
```
